```python
import math
import jax, jax.numpy as jnp
from jax import lax
import numpy as np

D_MODEL = 1024
BATCH = 8
SEQ = 4096
DEPTH = 1

HEAD_DIM = 64
BLOCK = 128
WINDOW = 128
A_Q_HEADS = 8
A_KV_HEADS = 2
A_GROUP = A_Q_HEADS // A_KV_HEADS
B_HEADS = 8
N_BRANCHES = 2
ROPE_THETA = 10000.0
RMS_EPS = 1e-6
_ffn_raw = (8 * D_MODEL + 2) // 3
D_FF = ((_ffn_raw + 255) // 256) * 256

A_Q_W = A_Q_HEADS * HEAD_DIM
A_KV_W = A_KV_HEADS * HEAD_DIM
B_W = B_HEADS * HEAD_DIM
GATE_W = N_BRANCHES * D_MODEL
IN_SPLITS = [A_Q_W, A_KV_W, A_KV_W, B_W, B_W, B_W, B_HEADS, GATE_W]
IN_OFFSETS = list(np.cumsum(IN_SPLITS)[:-1])
IN_W = int(sum(IN_SPLITS))
N_ADA = 6

kernel_name = "hybrid_swa_sink_fox_gated_block"


def rms_norm(x, g):
    xf = x.astype(jnp.float32)
    y = xf * lax.rsqrt(jnp.mean(xf * xf, axis=-1, keepdims=True) + RMS_EPS)
    return (y * g.astype(jnp.float32)).astype(x.dtype)


def rope(x, positions):
    dh = x.shape[-1]
    inv_freq = 1.0 / (ROPE_THETA ** (jnp.arange(0, dh, 2, dtype=jnp.float32) / dh))
    ang = positions.astype(jnp.float32)[..., None] * inv_freq
    cos = jnp.cos(ang)[:, :, None, :]
    sin = jnp.sin(ang)[:, :, None, :]
    xf = x.astype(jnp.float32)
    x1, x2 = xf[..., : dh // 2], xf[..., dh // 2:]
    out = jnp.concatenate([x1 * cos - x2 * sin, x2 * cos + x1 * sin], axis=-1)
    return out.astype(x.dtype)


def sliding_window_gqa_sinks(q, k, v, sinks):
    B, S, _, dh = q.shape
    nb = S // BLOCK
    scale = 1.0 / math.sqrt(dh)
    qb = q.reshape(B, nb, BLOCK, A_KV_HEADS, A_GROUP, dh)
    pad = ((0, 0), (BLOCK, 0), (0, 0), (0, 0))
    kp = jnp.pad(k, pad).reshape(B, nb + 1, BLOCK, A_KV_HEADS, dh)
    vp = jnp.pad(v, pad).reshape(B, nb + 1, BLOCK, A_KV_HEADS, dh)
    kb = jnp.concatenate([kp[:, :-1], kp[:, 1:]], axis=2)
    vb = jnp.concatenate([vp[:, :-1], vp[:, 1:]], axis=2)
    s = jnp.einsum('bnqhgd,bnkhd->bnhgqk', qb, kb).astype(jnp.float32) * scale
    q_loc = jnp.arange(BLOCK) + BLOCK
    k_loc = jnp.arange(2 * BLOCK)
    rel = q_loc[:, None] - k_loc[None, :]
    band = (rel >= 0) & (rel < WINDOW)
    k_abs = jnp.arange(nb)[:, None] * BLOCK + k_loc[None, :] - BLOCK
    valid = band[None, :, :] & (k_abs >= 0)[:, None, :]
    s = jnp.where(valid[None, :, None, None], s, -jnp.inf)
    sink = sinks.astype(jnp.float32).reshape(1, 1, A_KV_HEADS, A_GROUP, 1, 1)
    m = jnp.maximum(jnp.max(s, axis=-1, keepdims=True), sink)
    p = jnp.exp(s - m)
    denom = jnp.sum(p, axis=-1, keepdims=True) + jnp.exp(sink - m)
    p = p / denom
    o = jnp.einsum('bnhgqk,bnkhd->bnqhgd', p, vb.astype(jnp.float32))
    return o.reshape(B, S, A_Q_HEADS * dh)


def forgetting_attention(q, k, v, log_f):
    B, S, H, dh = q.shape
    nb = S // BLOCK
    scale = 1.0 / math.sqrt(dh)
    cum = lax.cumsum(log_f, axis=1)
    cum_k = jnp.transpose(cum, (0, 2, 1))
    k_pos = jnp.arange(S)
    vf = v.astype(jnp.float32)

    def one_block(i):
        start = i * BLOCK
        qi = lax.dynamic_slice_in_dim(q, start, BLOCK, axis=1)
        ci = lax.dynamic_slice_in_dim(cum, start, BLOCK, axis=1)
        s = jnp.einsum('bqhd,bkhd->bhqk', qi, k).astype(jnp.float32) * scale
        s = s + jnp.transpose(ci, (0, 2, 1))[..., None] - cum_k[:, :, None, :]
        q_pos = start + jnp.arange(BLOCK)
        s = jnp.where((k_pos[None, :] <= q_pos[:, None])[None, None], s, -jnp.inf)
        p = jax.nn.softmax(s, axis=-1)
        return jnp.einsum('bhqk,bkhd->bqhd', p, vf)

    o = lax.map(one_block, jnp.arange(nb))
    return jnp.transpose(o, (1, 0, 2, 3, 4)).reshape(B, S, H * dh)


def _fwd_setup_inputs(seed: int = 0) -> dict:
    key = jax.random.key(seed)
    ks = jax.random.split(key, 20)
    f32 = jnp.float32
    nrm = lambda k, shape, s: jax.random.normal(k, shape, f32) * s
    x = nrm(ks[0], (BATCH, SEQ, D_MODEL), 1.0)
    c = nrm(ks[1], (BATCH, D_MODEL), 1.0)
    positions = jnp.broadcast_to(jnp.arange(SEQ, dtype=jnp.int32), (BATCH, SEQ))
    w_ada = nrm(ks[2], (DEPTH, D_MODEL, N_ADA * D_MODEL), 0.5 * D_MODEL ** -0.5)
    b_ada = nrm(ks[3], (DEPTH, N_ADA * D_MODEL), 0.01)
    g_pre_mix = 1.0 + nrm(ks[4], (DEPTH, D_MODEL), 0.05)
    g_post_mix = 1.0 + nrm(ks[5], (DEPTH, D_MODEL), 0.05)
    w_in = nrm(ks[6], (DEPTH, D_MODEL, IN_W), D_MODEL ** -0.5)
    b_f = 1.0 + nrm(ks[7], (DEPTH, B_HEADS), 0.1)
    sinks = nrm(ks[8], (DEPTH, A_Q_HEADS), 0.5)
    w_branch_a = nrm(ks[9], (DEPTH, A_Q_W, D_MODEL), A_Q_W ** -0.5)
    w_branch_b = nrm(ks[10], (DEPTH, B_W, D_MODEL), B_W ** -0.5)
    w_out = nrm(ks[11], (DEPTH, D_MODEL, D_MODEL), D_MODEL ** -0.5)
    g_pre_ffn = 1.0 + nrm(ks[12], (DEPTH, D_MODEL), 0.05)
    g_post_ffn = 1.0 + nrm(ks[13], (DEPTH, D_MODEL), 0.05)
    w_ffn_in = nrm(ks[14], (DEPTH, D_MODEL, 2 * D_FF), D_MODEL ** -0.5)
    w_ffn_out = nrm(ks[15], (DEPTH, D_FF, D_MODEL), D_FF ** -0.5)
    return {"x": x, "c": c, "positions": positions, "w_ada": w_ada, "b_ada": b_ada,
            "g_pre_mix": g_pre_mix, "g_post_mix": g_post_mix, "w_in": w_in, "b_f": b_f,
            "sinks": sinks, "w_branch_a": w_branch_a, "w_branch_b": w_branch_b,
            "w_out": w_out, "g_pre_ffn": g_pre_ffn, "g_post_ffn": g_post_ffn,
            "w_ffn_in": w_ffn_in, "w_ffn_out": w_ffn_out}


def _fwd_reference(x, c, positions, w_ada, b_ada, g_pre_mix, g_post_mix, w_in, b_f, sinks,
              w_branch_a, w_branch_b, w_out, g_pre_ffn, g_post_ffn, w_ffn_in, w_ffn_out):
    B, S, D = x.shape
    for l in range(DEPTH):
        ada = (c @ w_ada[l] + b_ada[l]).reshape(B, N_ADA, D)[:, :, None, :]
        shift_m, scale_m, gate_m = ada[:, 0], ada[:, 1], ada[:, 2]
        shift_f, scale_f, gate_f = ada[:, 3], ada[:, 4], ada[:, 5]

        h = rms_norm(x, g_pre_mix[l]) * (1.0 + scale_m) + shift_m
        proj = h @ w_in[l]
        qa, ka, va, qb, kb, vb, f_logit, gate_logit = jnp.split(proj, IN_OFFSETS, axis=-1)
        qa = rope(qa.reshape(B, S, A_Q_HEADS, HEAD_DIM), positions)
        ka = rope(ka.reshape(B, S, A_KV_HEADS, HEAD_DIM), positions)
        va = va.reshape(B, S, A_KV_HEADS, HEAD_DIM)
        o_a = sliding_window_gqa_sinks(qa, ka, va, sinks[l]).astype(x.dtype)

        log_f = jax.nn.log_sigmoid((f_logit + b_f[l]).astype(jnp.float32))
        o_b = forgetting_attention(qb.reshape(B, S, B_HEADS, HEAD_DIM),
                                   kb.reshape(B, S, B_HEADS, HEAD_DIM),
                                   vb.reshape(B, S, B_HEADS, HEAD_DIM), log_f).astype(x.dtype)

        gates = jax.nn.sigmoid(gate_logit).reshape(B, S, N_BRANCHES, D)
        merged = gates[:, :, 0] * (o_a @ w_branch_a[l]) + gates[:, :, 1] * (o_b @ w_branch_b[l])
        y = merged @ w_out[l]
        x = x + gate_m * rms_norm(y, g_post_mix[l])

        h = rms_norm(x, g_pre_ffn[l]) * (1.0 + scale_f) + shift_f
        gu = h @ w_ffn_in[l]
        g_part, u_part = gu[..., :D_FF], gu[..., D_FF:]
        y = (jax.nn.silu(g_part) * u_part) @ w_ffn_out[l]
        x = x + gate_f * rms_norm(y, g_post_ffn[l])
    return x


import jax as _jax
import jax.numpy as _jnp

TWIN_FORMAT = 'train_step'
FWD_PARAMS = ['x', 'c', 'positions', 'w_ada', 'b_ada', 'g_pre_mix', 'g_post_mix', 'w_in', 'b_f', 'sinks', 'w_branch_a', 'w_branch_b', 'w_out', 'g_pre_ffn', 'g_post_ffn', 'w_ffn_in', 'w_ffn_out']
TWIN_WEIGHTS = ['w_ada', 'b_ada', 'g_pre_mix', 'g_post_mix', 'w_in', 'b_f', 'sinks', 'w_branch_a', 'w_branch_b', 'w_out', 'g_pre_ffn', 'g_post_ffn', 'w_ffn_in', 'w_ffn_out']
TWIN_DIFF_INPUT = 'x'
TWIN_INPUTS = ['x', 'c', 'positions', 'w_ada', 'b_ada', 'g_pre_mix', 'g_post_mix', 'w_in', 'b_f', 'sinks', 'w_branch_a', 'w_branch_b', 'w_out', 'g_pre_ffn', 'g_post_ffn', 'w_ffn_in', 'w_ffn_out', 'loss_target', 'm_w_ada', 'm_b_ada', 'm_g_pre_mix', 'm_g_post_mix', 'm_w_in', 'm_b_f', 'm_sinks', 'm_w_branch_a', 'm_w_branch_b', 'm_w_out', 'm_g_pre_ffn', 'm_g_post_ffn', 'm_w_ffn_in', 'm_w_ffn_out', 'v_w_ada', 'v_b_ada', 'v_g_pre_mix', 'v_g_post_mix', 'v_w_in', 'v_b_f', 'v_sinks', 'v_w_branch_a', 'v_w_branch_b', 'v_w_out', 'v_g_pre_ffn', 'v_g_post_ffn', 'v_w_ffn_in', 'v_w_ffn_out']
TWIN_OUTPUTS = ['loss', 'grad_x', 'grad_w_ada', 'grad_b_ada', 'grad_g_pre_mix', 'grad_g_post_mix', 'grad_w_in', 'grad_b_f', 'grad_sinks', 'grad_w_branch_a', 'grad_w_branch_b', 'grad_w_out', 'grad_g_pre_ffn', 'grad_g_post_ffn', 'grad_w_ffn_in', 'grad_w_ffn_out', 'delta_w_ada', 'delta_b_ada', 'delta_g_pre_mix', 'delta_g_post_mix', 'delta_w_in', 'delta_b_f', 'delta_sinks', 'delta_w_branch_a', 'delta_w_branch_b', 'delta_w_out', 'delta_g_pre_ffn', 'delta_g_post_ffn', 'delta_w_ffn_in', 'delta_w_ffn_out', 'new_m_w_ada', 'new_m_b_ada', 'new_m_g_pre_mix', 'new_m_g_post_mix', 'new_m_w_in', 'new_m_b_f', 'new_m_sinks', 'new_m_w_branch_a', 'new_m_w_branch_b', 'new_m_w_out', 'new_m_g_pre_ffn', 'new_m_g_post_ffn', 'new_m_w_ffn_in', 'new_m_w_ffn_out', 'new_v_w_ada', 'new_v_b_ada', 'new_v_g_pre_mix', 'new_v_g_post_mix', 'new_v_w_in', 'new_v_b_f', 'new_v_sinks', 'new_v_w_branch_a', 'new_v_w_branch_b', 'new_v_w_out', 'new_v_g_pre_ffn', 'new_v_g_post_ffn', 'new_v_w_ffn_in', 'new_v_w_ffn_out']
TWIN_LEAF_KINDS = {'loss': 'loss', 'grad_x': 'grad_x', 'grad_w_ada': 'grad_w', 'grad_b_ada': 'grad_w', 'grad_g_pre_mix': 'grad_w', 'grad_g_post_mix': 'grad_w', 'grad_w_in': 'grad_w', 'grad_b_f': 'grad_w', 'grad_sinks': 'grad_w', 'grad_w_branch_a': 'grad_w', 'grad_w_branch_b': 'grad_w', 'grad_w_out': 'grad_w', 'grad_g_pre_ffn': 'grad_w', 'grad_g_post_ffn': 'grad_w', 'grad_w_ffn_in': 'grad_w', 'grad_w_ffn_out': 'grad_w', 'delta_w_ada': 'delta_w', 'delta_b_ada': 'delta_w', 'delta_g_pre_mix': 'delta_w', 'delta_g_post_mix': 'delta_w', 'delta_w_in': 'delta_w', 'delta_b_f': 'delta_w', 'delta_sinks': 'delta_w', 'delta_w_branch_a': 'delta_w', 'delta_w_branch_b': 'delta_w', 'delta_w_out': 'delta_w', 'delta_g_pre_ffn': 'delta_w', 'delta_g_post_ffn': 'delta_w', 'delta_w_ffn_in': 'delta_w', 'delta_w_ffn_out': 'delta_w', 'new_m_w_ada': 'new_m', 'new_m_b_ada': 'new_m', 'new_m_g_pre_mix': 'new_m', 'new_m_g_post_mix': 'new_m', 'new_m_w_in': 'new_m', 'new_m_b_f': 'new_m', 'new_m_sinks': 'new_m', 'new_m_w_branch_a': 'new_m', 'new_m_w_branch_b': 'new_m', 'new_m_w_out': 'new_m', 'new_m_g_pre_ffn': 'new_m', 'new_m_g_post_ffn': 'new_m', 'new_m_w_ffn_in': 'new_m', 'new_m_w_ffn_out': 'new_m', 'new_v_w_ada': 'new_v', 'new_v_b_ada': 'new_v', 'new_v_g_pre_mix': 'new_v', 'new_v_g_post_mix': 'new_v', 'new_v_w_in': 'new_v', 'new_v_b_f': 'new_v', 'new_v_sinks': 'new_v', 'new_v_w_branch_a': 'new_v', 'new_v_w_branch_b': 'new_v', 'new_v_w_out': 'new_v', 'new_v_g_pre_ffn': 'new_v', 'new_v_g_post_ffn': 'new_v', 'new_v_w_ffn_in': 'new_v', 'new_v_w_ffn_out': 'new_v'}


def _forward(args):
    return _fwd_reference(*[args[k] for k in FWD_PARAMS])


def _output_shape():
    out = _jax.eval_shape(lambda: _forward(_fwd_setup_inputs(0)))
    return out.shape, out.dtype

N_MICROBATCH = 1
ADAM_LR = 0.001
ADAM_B1 = 0.9
ADAM_B2 = 0.999
ADAM_EPS = 1e-08
ADAM_WD = 0.01
ADAM_STEP = 10
PER_EXAMPLE_BATCH_AXIS = {'x': 0, 'c': 0, 'positions': 0, 'loss_target': 0}
SHARED_INPUTS = []
_WEIGHT_DTYPES = {'w_ada': _jnp.float32, 'b_ada': _jnp.float32, 'g_pre_mix': _jnp.float32, 'g_post_mix': _jnp.float32, 'w_in': _jnp.float32, 'b_f': _jnp.float32, 'sinks': _jnp.float32, 'w_branch_a': _jnp.float32, 'w_branch_b': _jnp.float32, 'w_out': _jnp.float32, 'g_pre_ffn': _jnp.float32, 'g_post_ffn': _jnp.float32, 'w_ffn_in': _jnp.float32, 'w_ffn_out': _jnp.float32}
MOMENT_SCALE = {'w_ada': 5.574645e+00, 'b_ada': 5.362292e+00, 'g_pre_mix': 2.622385e-01, 'g_post_mix': 1.050470e+01, 'w_in': 1.168240e+00, 'b_f': 1.784030e+00, 'sinks': 6.748555e-02, 'w_branch_a': 1.759797e+00, 'w_branch_b': 1.757161e+00, 'w_out': 2.419594e+00, 'g_pre_ffn': 4.193129e-01, 'g_post_ffn': 9.811408e+00, 'w_ffn_in': 3.526658e-01, 'w_ffn_out': 6.961895e-01}


def _to_microbatches(a, axis):
    t = _jnp.moveaxis(a, axis, 0)
    t = t.reshape((N_MICROBATCH, t.shape[0] // N_MICROBATCH) + t.shape[1:])
    return _jnp.moveaxis(t, 1, axis + 1)


def setup_inputs(seed: int = 0) -> dict:
    inp = _fwd_setup_inputs(seed)
    key = _jax.random.fold_in(_jax.random.key(seed), 7919)
    shape, _ = _output_shape()
    out = dict(inp)
    out["loss_target"] = _jax.random.normal(_jax.random.fold_in(key, 0), shape, _jnp.float32)
    for i, name in enumerate(TWIN_WEIGHTS):
        w = inp[name].astype(_jnp.float32)
        if MOMENT_SCALE is None:
            s = _jnp.sqrt(_jnp.mean(_jnp.square(w)) + 1e-30)
        else:
            s = MOMENT_SCALE[name]
        km, kv = _jax.random.split(_jax.random.fold_in(key, i + 1))
        out[name] = w
        out["m_" + name] = s * _jax.random.normal(km, w.shape, _jnp.float32)
        out["v_" + name] = (s * s) * _jax.random.uniform(kv, w.shape, _jnp.float32, 0.5, 1.5)
    if N_MICROBATCH > 1:
        for name, axis in PER_EXAMPLE_BATCH_AXIS.items():
            out[name] = _to_microbatches(out[name], axis)
    return {'x': out['x'], 'c': out['c'], 'positions': out['positions'], 'w_ada': out['w_ada'], 'b_ada': out['b_ada'], 'g_pre_mix': out['g_pre_mix'], 'g_post_mix': out['g_post_mix'], 'w_in': out['w_in'], 'b_f': out['b_f'], 'sinks': out['sinks'], 'w_branch_a': out['w_branch_a'], 'w_branch_b': out['w_branch_b'], 'w_out': out['w_out'], 'g_pre_ffn': out['g_pre_ffn'], 'g_post_ffn': out['g_post_ffn'], 'w_ffn_in': out['w_ffn_in'], 'w_ffn_out': out['w_ffn_out'], 'loss_target': out['loss_target'], 'm_w_ada': out['m_w_ada'], 'm_b_ada': out['m_b_ada'], 'm_g_pre_mix': out['m_g_pre_mix'], 'm_g_post_mix': out['m_g_post_mix'], 'm_w_in': out['m_w_in'], 'm_b_f': out['m_b_f'], 'm_sinks': out['m_sinks'], 'm_w_branch_a': out['m_w_branch_a'], 'm_w_branch_b': out['m_w_branch_b'], 'm_w_out': out['m_w_out'], 'm_g_pre_ffn': out['m_g_pre_ffn'], 'm_g_post_ffn': out['m_g_post_ffn'], 'm_w_ffn_in': out['m_w_ffn_in'], 'm_w_ffn_out': out['m_w_ffn_out'], 'v_w_ada': out['v_w_ada'], 'v_b_ada': out['v_b_ada'], 'v_g_pre_mix': out['v_g_pre_mix'], 'v_g_post_mix': out['v_g_post_mix'], 'v_w_in': out['v_w_in'], 'v_b_f': out['v_b_f'], 'v_sinks': out['v_sinks'], 'v_w_branch_a': out['v_w_branch_a'], 'v_w_branch_b': out['v_w_branch_b'], 'v_w_out': out['v_w_out'], 'v_g_pre_ffn': out['v_g_pre_ffn'], 'v_g_post_ffn': out['v_g_post_ffn'], 'v_w_ffn_in': out['v_w_ffn_in'], 'v_w_ffn_out': out['v_w_ffn_out']}


def _loss(weights, diff, rest, loss_target):
    with _jax.named_scope("forward"):
        args = {**rest, TWIN_DIFF_INPUT: diff, **{k: w.astype(_WEIGHT_DTYPES[k]) for k, w in weights.items()}}
        y = _forward(args)
    with _jax.named_scope("loss_head"):
        err = _jnp.square(y.astype(_jnp.float32) - loss_target)
        return 0.5 * _jnp.sum(_jnp.mean(err, axis=-1)) if err.ndim else 0.5 * err


def _adamw(w, g, m, v):
    m = ADAM_B1 * m + (1.0 - ADAM_B1) * g
    v = ADAM_B2 * v + (1.0 - ADAM_B2) * _jnp.square(g)
    m_hat = m / (1.0 - ADAM_B1 ** ADAM_STEP)
    v_hat = v / (1.0 - ADAM_B2 ** ADAM_STEP)
    delta = -ADAM_LR * (m_hat / (_jnp.sqrt(v_hat) + ADAM_EPS) + ADAM_WD * w)
    return delta, m, v


def reference(x, c, positions, w_ada, b_ada, g_pre_mix, g_post_mix, w_in, b_f, sinks, w_branch_a, w_branch_b, w_out, g_pre_ffn, g_post_ffn, w_ffn_in, w_ffn_out, loss_target, m_w_ada, m_b_ada, m_g_pre_mix, m_g_post_mix, m_w_in, m_b_f, m_sinks, m_w_branch_a, m_w_branch_b, m_w_out, m_g_pre_ffn, m_g_post_ffn, m_w_ffn_in, m_w_ffn_out, v_w_ada, v_b_ada, v_g_pre_mix, v_g_post_mix, v_w_in, v_b_f, v_sinks, v_w_branch_a, v_w_branch_b, v_w_out, v_g_pre_ffn, v_g_post_ffn, v_w_ffn_in, v_w_ffn_out):
    given = dict(x=x, c=c, positions=positions, w_ada=w_ada, b_ada=b_ada, g_pre_mix=g_pre_mix, g_post_mix=g_post_mix, w_in=w_in, b_f=b_f, sinks=sinks, w_branch_a=w_branch_a, w_branch_b=w_branch_b, w_out=w_out, g_pre_ffn=g_pre_ffn, g_post_ffn=g_post_ffn, w_ffn_in=w_ffn_in, w_ffn_out=w_ffn_out, loss_target=loss_target, m_w_ada=m_w_ada, m_b_ada=m_b_ada, m_g_pre_mix=m_g_pre_mix, m_g_post_mix=m_g_post_mix, m_w_in=m_w_in, m_b_f=m_b_f, m_sinks=m_sinks, m_w_branch_a=m_w_branch_a, m_w_branch_b=m_w_branch_b, m_w_out=m_w_out, m_g_pre_ffn=m_g_pre_ffn, m_g_post_ffn=m_g_post_ffn, m_w_ffn_in=m_w_ffn_in, m_w_ffn_out=m_w_ffn_out, v_w_ada=v_w_ada, v_b_ada=v_b_ada, v_g_pre_mix=v_g_pre_mix, v_g_post_mix=v_g_post_mix, v_w_in=v_w_in, v_b_f=v_b_f, v_sinks=v_sinks, v_w_branch_a=v_w_branch_a, v_w_branch_b=v_w_branch_b, v_w_out=v_w_out, v_g_pre_ffn=v_g_pre_ffn, v_g_post_ffn=v_g_post_ffn, v_w_ffn_in=v_w_ffn_in, v_w_ffn_out=v_w_ffn_out)
    weights = {n: given[n] for n in TWIN_WEIGHTS}
    shared = {n: given[n] for n in SHARED_INPUTS}
    per_example = {n: given[n] for n in ['x', 'c', 'positions']}
    grad_fn = _jax.value_and_grad(_loss, argnums=(0, 1))

    def one_microbatch(ex, loss_target):
        ex = dict(ex)
        diff = ex.pop(TWIN_DIFF_INPUT)
        return grad_fn(weights, diff, {**shared, **ex}, loss_target)

    if N_MICROBATCH == 1:
        loss, (grad_w, grad_x) = one_microbatch(per_example, given["loss_target"])
    else:
        def body(carry, xs):
            loss_sum, grad_sum = carry
            l_k, (gw_k, gx_k) = one_microbatch(xs[0], xs[1])
            with _jax.named_scope("update"):
                return (loss_sum + l_k, _jax.tree.map(_jnp.add, grad_sum, gw_k)), gx_k

        init = (_jnp.zeros((), _jnp.float32), _jax.tree.map(_jnp.zeros_like, weights))
        (loss, grad_w), grad_x = _jax.lax.scan(body, init, (per_example, given["loss_target"]))
    with _jax.named_scope("update"):
        delta_w, new_m, new_v = {}, {}, {}
        for n in TWIN_WEIGHTS:
            delta_w[n], new_m[n], new_v[n] = _adamw(weights[n], grad_w[n], given["m_" + n], given["v_" + n])
    return (loss, grad_x, *[grad_w[n] for n in TWIN_WEIGHTS], *[delta_w[n] for n in TWIN_WEIGHTS],
            *[new_m[n] for n in TWIN_WEIGHTS], *[new_v[n] for n in TWIN_WEIGHTS])
```

```python
import functools
import math

import jax
import jax.numpy as jnp
from jax import lax
from jax.experimental import pallas as pl
from jax.experimental.pallas import tpu as pltpu

f32 = jnp.float32
bf16 = jnp.bfloat16

D_MODEL = 1024
HEAD_DIM = 64
N_HEADS = 8
N_PAIRS = 4
QKV_W = 2304
GATE_W = 2048
F_OFF = 2304
IN_W = 4360
WINDOW = 128
ROPE_THETA = 10000.0
RMS_EPS = 1e-6
D_FF = 2816
N_DEV = 8
ADAM_LR, ADAM_B1, ADAM_B2, ADAM_EPS, ADAM_WD, ADAM_STEP = 0.001, 0.9, 0.999, 1e-08, 0.01, 10
NEG = -1e30
LANES = 128
VMEM_LIMIT = 48 * 1024 * 1024
MESH = pl.DeviceIdType.MESH

_NT = (((1,), (1,)), ((), ()))
_TN = (((0,), (0,)), ((), ()))


def _params(n_grid=0):
    sem = ("arbitrary",) * n_grid if n_grid else None
    return pltpu.CompilerParams(dimension_semantics=sem, vmem_limit_bytes=VMEM_LIMIT)


def _row_tile(s, want):
    t = min(s, want)
    assert s % t == 0, (s, t)
    return t


def _col_tile(n):
    for t in (512, 768, 640, 256, 384, 128):
        if n % t == 0:
            return t
    raise ValueError(n)


def _matmul(a, b, mode, out_dtype, name):
    if mode == "nn":
        (m, k), n = a.shape, b.shape[1]
    elif mode == "nt":
        (m, k), n = a.shape, b.shape[0]
    else:
        (k, m), n = a.shape, b.shape[1]
    tm = _row_tile(m, 256 if mode == "tn" else 512)
    tn = _col_tile(n)
    if mode == "nn":
        a_spec, b_spec, dims = pl.BlockSpec((tm, k), lambda i, j: (i, 0)), pl.BlockSpec((k, tn), lambda i, j: (0, j)), None
    elif mode == "nt":
        a_spec, b_spec, dims = pl.BlockSpec((tm, k), lambda i, j: (i, 0)), pl.BlockSpec((tn, k), lambda i, j: (j, 0)), _NT
    else:
        a_spec, b_spec, dims = pl.BlockSpec((k, tm), lambda i, j: (0, i)), pl.BlockSpec((k, tn), lambda i, j: (0, j)), _TN

    def body(a_ref, b_ref, o_ref):
        av, bv = a_ref[...].astype(bf16), b_ref[...].astype(bf16)
        if dims is None:
            r = jnp.dot(av, bv, preferred_element_type=f32)
        else:
            r = lax.dot_general(av, bv, dims, preferred_element_type=f32)
        o_ref[...] = r.astype(out_dtype)

    return pl.pallas_call(
        body, name=name, grid=(m // tm, n // tn), in_specs=[a_spec, b_spec],
        out_specs=pl.BlockSpec((tm, tn), lambda i, j: (i, j)),
        out_shape=jax.ShapeDtypeStruct((m, n), out_dtype), compiler_params=_params(2),
    )(a, b)


def _rstd(v):
    return lax.rsqrt(jnp.mean(v * v, axis=-1, keepdims=True) + RMS_EPS)


def _row_spec(tm, d):
    return pl.BlockSpec((tm, d), lambda i: (i, 0))


def _vec_spec(d, rows=1):
    return pl.BlockSpec((rows, d), lambda i: (0, 0))


def _prenorm(x, g, scale, shift, name):
    s, d = x.shape
    tm = _row_tile(s, 512)

    def body(x_ref, g_ref, sc_ref, sh_ref, h_ref):
        xv = x_ref[...]
        h = (xv * _rstd(xv) * g_ref[...]) * (1.0 + sc_ref[...]) + sh_ref[...]
        h_ref[...] = h.astype(bf16)

    return pl.pallas_call(
        body, name=name, grid=(s // tm,), in_specs=[_row_spec(tm, d)] + [_vec_spec(d)] * 3,
        out_specs=_row_spec(tm, d), out_shape=jax.ShapeDtypeStruct((s, d), bf16), compiler_params=_params(1),
    )(x, g, scale, shift)


def _postnorm_res(x, y, g, gate, name):
    s, d = x.shape
    tm = _row_tile(s, 512)

    def body(x_ref, y_ref, g_ref, gate_ref, o_ref):
        yv = y_ref[...]
        o_ref[...] = x_ref[...] + gate_ref[...] * (yv * _rstd(yv) * g_ref[...])

    return pl.pallas_call(
        body, name=name, grid=(s // tm,), in_specs=[_row_spec(tm, d)] * 2 + [_vec_spec(d)] * 2,
        out_specs=_row_spec(tm, d), out_shape=jax.ShapeDtypeStruct((s, d), f32), compiler_params=_params(1),
    )(x, y, g, gate)


def _loss_head(out, target, name):
    s, d = out.shape
    tm = _row_tile(s, 512)

    def body(o_ref, t_ref, loss_ref, d_ref):
        @pl.when(pl.program_id(0) == 0)
        def _():
            loss_ref[...] = jnp.zeros_like(loss_ref)
        err = o_ref[...] - t_ref[...]
        d_ref[...] = err / d
        loss_ref[...] += 0.5 * jnp.sum(jnp.mean(err * err, axis=-1, keepdims=True), axis=0, keepdims=True)

    return pl.pallas_call(
        body, name=name, grid=(s // tm,), in_specs=[_row_spec(tm, d)] * 2,
        out_specs=[_vec_spec(LANES), _row_spec(tm, d)],
        out_shape=[jax.ShapeDtypeStruct((1, LANES), f32), jax.ShapeDtypeStruct((s, d), f32)], compiler_params=_params(1),
    )(out, target)


def _rms_bwd(u, v, r):
    return r * u - v * (r * r * r) * jnp.mean(u * v, axis=-1, keepdims=True)


def _postnorm_bwd(dres, y, g, gate, name):
    s, d = y.shape
    tm = _row_tile(s, 512)

    def body(dr_ref, y_ref, g_ref, gate_ref, dy_ref, vec_ref):
        @pl.when(pl.program_id(0) == 0)
        def _():
            vec_ref[...] = jnp.zeros_like(vec_ref)
        dr, yv = dr_ref[...], y_ref[...]
        r = _rstd(yv)
        yn = yv * r
        dn = dr * gate_ref[...]
        vec_ref[0:1, :] += jnp.sum(dr * (yn * g_ref[...]), axis=0, keepdims=True)
        vec_ref[1:2, :] += jnp.sum(dn * yn, axis=0, keepdims=True)
        dy_ref[...] = _rms_bwd(dn * g_ref[...], yv, r).astype(bf16)

    return pl.pallas_call(
        body, name=name, grid=(s // tm,), in_specs=[_row_spec(tm, d)] * 2 + [_vec_spec(d)] * 2,
        out_specs=[_row_spec(tm, d), _vec_spec(d, 8)],
        out_shape=[jax.ShapeDtypeStruct((s, d), bf16), jax.ShapeDtypeStruct((8, d), f32)], compiler_params=_params(1),
    )(dres, y, g, gate)


def _prenorm_bwd(dh, x, g, scale, dres, name):
    s, d = x.shape
    tm = _row_tile(s, 512)

    def body(dh_ref, x_ref, g_ref, sc_ref, dr_ref, dx_ref, vec_ref):
        @pl.when(pl.program_id(0) == 0)
        def _():
            vec_ref[...] = jnp.zeros_like(vec_ref)
        dhv, xv = dh_ref[...], x_ref[...]
        r = _rstd(xv)
        xn = xv * r
        dn = dhv * (1.0 + sc_ref[...])
        vec_ref[0:1, :] += jnp.sum(dhv, axis=0, keepdims=True)
        vec_ref[1:2, :] += jnp.sum(dhv * (xn * g_ref[...]), axis=0, keepdims=True)
        vec_ref[2:3, :] += jnp.sum(dn * xn, axis=0, keepdims=True)
        dx_ref[...] = dr_ref[...] + _rms_bwd(dn * g_ref[...], xv, r)

    return pl.pallas_call(
        body, name=name, grid=(s // tm,),
        in_specs=[_row_spec(tm, d)] * 2 + [_vec_spec(d)] * 2 + [_row_spec(tm, d)],
        out_specs=[_row_spec(tm, d), _vec_spec(d, 8)],
        out_shape=[jax.ShapeDtypeStruct((s, d), f32), jax.ShapeDtypeStruct((8, d), f32)], compiler_params=_params(1),
    )(dh, x, g, scale, dres)


def _lane():
    return lax.broadcasted_iota(jnp.int32, (1, LANES), 1)


def _rope_tables(pos_col, inv_freq, name):
    s = pos_col.shape[0]

    def body(p_ref, f_ref, cos_ref, sin_ref):
        ang = p_ref[...].astype(f32) * f_ref[...]
        first_half = (_lane() % HEAD_DIM) < HEAD_DIM // 2
        cos_ref[...] = jnp.cos(ang)
        sn = jnp.sin(ang)
        sin_ref[...] = jnp.where(first_half, -sn, sn)

    return pl.pallas_call(
        body, name=name, out_shape=[jax.ShapeDtypeStruct((s, LANES), f32)] * 2, compiler_params=_params(),
    )(pos_col, inv_freq)


def _swap_halves(v):
    first_half = (_lane() % HEAD_DIM) < HEAD_DIM // 2
    return jnp.where(first_half, pltpu.roll(v, LANES - HEAD_DIM // 2, axis=1), pltpu.roll(v, HEAD_DIM // 2, axis=1))


def _qkv_prep(qkv, cos, sin_s, name):
    s = qkv.shape[0]
    tm = _row_tile(s, 256)
    scale = 1.0 / math.sqrt(HEAD_DIM)

    def body(p_ref, c_ref, s_ref, qa_ref, ka_ref, va_ref, qb_ref, kb_ref, vb_ref):
        cs, sn = c_ref[...], s_ref[...]
        low = _lane() < HEAD_DIM

        def blk(j):
            return p_ref[:, j * LANES:(j + 1) * LANES]

        def rope(v):
            return v * cs + _swap_halves(v) * sn

        def expand(v):
            other = pltpu.roll(v, HEAD_DIM, axis=1)
            return jnp.where(low, v, other), jnp.where(low, other, v)

        for j in range(N_PAIRS):
            qa_ref[:, j * LANES:(j + 1) * LANES] = (rope(blk(j)) * scale).astype(bf16)
            qb_ref[:, j * LANES:(j + 1) * LANES] = (blk(6 + j) * scale).astype(bf16)
            kb_ref[:, j * LANES:(j + 1) * LANES] = blk(10 + j).astype(bf16)
            vb_ref[:, j * LANES:(j + 1) * LANES] = blk(14 + j).astype(bf16)
        k0, k1 = expand(rope(blk(4)))
        v0, v1 = expand(blk(5))
        for j in range(N_PAIRS):
            ka_ref[:, j * LANES:(j + 1) * LANES] = (k0 if j < 2 else k1).astype(bf16)
            va_ref[:, j * LANES:(j + 1) * LANES] = (v0 if j < 2 else v1).astype(bf16)

    hw = N_PAIRS * LANES
    return pl.pallas_call(
        body, name=name, grid=(s // tm,),
        in_specs=[_row_spec(tm, QKV_W), _row_spec(tm, LANES), _row_spec(tm, LANES)],
        out_specs=[_row_spec(tm, hw)] * 6, out_shape=[jax.ShapeDtypeStruct((s, hw), bf16)] * 6, compiler_params=_params(1),
    )(qkv, cos, sin_s)


def _qkv_prep_bwd(dqa, dka, dva, dqb, dkb, dvb, cos, sin_s, name):
    s = dqa.shape[0]
    tm = _row_tile(s, 256)
    scale = 1.0 / math.sqrt(HEAD_DIM)
    hw = N_PAIRS * LANES

    def body(dqa_ref, dka_ref, dva_ref, dqb_ref, dkb_ref, dvb_ref, c_ref, s_ref, o_ref):
        cs, sn = c_ref[...], s_ref[...]
        low = _lane() < HEAD_DIM

        def blk(ref, j):
            return ref[:, j * LANES:(j + 1) * LANES]

        def unrope(v):
            return v * cs + _swap_halves(v * sn)

        def fold(ref):
            a, b = blk(ref, 0) + blk(ref, 1), blk(ref, 2) + blk(ref, 3)
            kv0 = a + pltpu.roll(a, HEAD_DIM, axis=1)
            kv1 = b + pltpu.roll(b, HEAD_DIM, axis=1)
            return jnp.where(low, kv0, kv1)

        for j in range(N_PAIRS):
            o_ref[:, j * LANES:(j + 1) * LANES] = (unrope(blk(dqa_ref, j)) * scale).astype(bf16)
            o_ref[:, (6 + j) * LANES:(7 + j) * LANES] = (blk(dqb_ref, j) * scale).astype(bf16)
            o_ref[:, (10 + j) * LANES:(11 + j) * LANES] = blk(dkb_ref, j).astype(bf16)
            o_ref[:, (14 + j) * LANES:(15 + j) * LANES] = blk(dvb_ref, j).astype(bf16)
        o_ref[:, 4 * LANES:5 * LANES] = unrope(fold(dka_ref)).astype(bf16)
        o_ref[:, 5 * LANES:6 * LANES] = fold(dva_ref).astype(bf16)

    return pl.pallas_call(
        body, name=name, grid=(s // tm,),
        in_specs=[_row_spec(tm, hw)] * 6 + [_row_spec(tm, LANES)] * 2,
        out_specs=_row_spec(tm, QKV_W), out_shape=jax.ShapeDtypeStruct((s, QKV_W), bf16), compiler_params=_params(1),
    )(dqa, dka, dva, dqb, dkb, dvb, cos, sin_s)


def _cumsum_rows(v, reverse=False):
    n = v.shape[0]
    row = lax.broadcasted_iota(jnp.int32, v.shape, 0)
    sh = 1
    while sh < n:
        if reverse:
            v = v + jnp.where(row < n - sh, pltpu.roll(v, n - sh, axis=0), 0.0)
        else:
            v = v + jnp.where(row >= sh, pltpu.roll(v, sh, axis=0), 0.0)
        sh *= 2
    return v


def _log_sigmoid(z):
    return jnp.minimum(z, 0.0) - jnp.log1p(jnp.exp(-jnp.abs(z)))


def _forget_prep(fl, bf_row, name):
    s = fl.shape[0]

    def body(f_ref, b_ref, cb_ref, cr_ref):
        cum = _cumsum_rows(_log_sigmoid(f_ref[...] + b_ref[...]))
        low = _lane() < HEAD_DIM
        for j in range(N_PAIRS):
            cb_ref[:, j * LANES:(j + 1) * LANES] = jnp.where(low, cum[:, 2 * j:2 * j + 1], cum[:, 2 * j + 1:2 * j + 2])
        cr_ref[...] = cum.T[0:N_HEADS, :]

    return pl.pallas_call(
        body, name=name,
        out_shape=[jax.ShapeDtypeStruct((s, N_PAIRS * LANES), f32), jax.ShapeDtypeStruct((N_HEADS, s), f32)],
        compiler_params=_params(),
    )(fl, bf_row)


def _forget_prep_bwd(dcr, rs, fl, bf_row, name):
    s = fl.shape[0]

    def body(d_ref, rs_ref, f_ref, b_ref, df_ref, db_ref):
        eye = (lax.broadcasted_iota(jnp.int32, (N_HEADS, LANES), 0) == lax.broadcasted_iota(jnp.int32, (N_HEADS, LANES), 1)).astype(f32)
        dcum = lax.dot_general(d_ref[...], eye, _TN, precision=lax.Precision.HIGHEST, preferred_element_type=f32)
        for h in range(N_HEADS):
            dcum = dcum + jnp.where(_lane() == h, jnp.sum(rs_ref[:, h * LANES:(h + 1) * LANES], axis=1, keepdims=True), 0.0)
        dlf = _cumsum_rows(dcum, reverse=True)
        z = f_ref[...] + b_ref[...]
        df = jnp.where(_lane() < N_HEADS, dlf * jax.nn.sigmoid(-z), 0.0)
        df_ref[...] = df.astype(bf16)
        db_ref[...] = jnp.zeros_like(db_ref)
        db_ref[0:1, :] = jnp.sum(df, axis=0, keepdims=True)

    return pl.pallas_call(
        body, name=name,
        out_shape=[jax.ShapeDtypeStruct((s, LANES), bf16), jax.ShapeDtypeStruct((8, LANES), f32)], compiler_params=_params(),
    )(dcr, rs, fl, bf_row)


def _attn_fwd(q, k, v, name, *, cum=None, sink_b=None, window=None, tq=256, tk=256):
    s = q.shape[0]
    tq, tk = _row_tile(s, tq), _row_tile(s, tk)
    fox, has_sink = cum is not None, sink_b is not None

    def body(*refs):
        q_ref, k_ref, v_ref = refs[:3]
        rest = list(refs[3:])
        cb_ref, cr_ref = (rest.pop(0), rest.pop(0)) if fox else (None, None)
        sink_ref = rest.pop(0) if has_sink else None
        o_ref, lse_ref = rest
        q0 = pl.program_id(1) * tq
        low = _lane() < HEAD_DIM
        q2 = q_ref[...]
        qpos = q0 + lax.broadcasted_iota(jnp.int32, (tq, 1), 0)
        lo = jnp.maximum(q0 - (window - 1), 0) // tk if window else 0
        hi = (q0 + tq - 1) // tk
        res = []
        for i in range(2):
            sel = low if i == 0 else jnp.logical_not(low)
            qm = jnp.where(sel, q2, jnp.zeros_like(q2))
            cq = cb_ref[:, HEAD_DIM * i:HEAD_DIM * i + 1] if fox else None
            if has_sink:
                m0 = jnp.broadcast_to(sink_ref[:, HEAD_DIM * i:HEAD_DIM * i + 1], (tq, 1))
                l0 = jnp.ones((tq, 1), f32)
            else:
                m0, l0 = jnp.full((tq, 1), NEG, f32), jnp.zeros((tq, 1), f32)

            def step(kb, carry, qm=qm, cq=cq, i=i):
                m, l, acc = carry
                k0 = pl.multiple_of(kb * tk, tk)
                kblk, vblk = k_ref[pl.ds(k0, tk), :], v_ref[pl.ds(k0, tk), :]
                sc = lax.dot_general(qm, kblk, _NT, preferred_element_type=f32)
                if fox:
                    sc = sc + cq - cr_ref[i:i + 1, pl.ds(k0, tk)]
                kpos = k0 + lax.broadcasted_iota(jnp.int32, (1, tk), 1)
                valid = kpos <= qpos
                if window:
                    valid = jnp.logical_and(valid, qpos - kpos < window)
                sc = jnp.where(valid, sc, NEG)
                m_new = jnp.maximum(m, jnp.max(sc, axis=1, keepdims=True))
                p = jnp.where(valid, jnp.exp(sc - m_new), 0.0)
                alpha = jnp.exp(m - m_new)
                l = alpha * l + jnp.sum(p, axis=1, keepdims=True)
                acc = alpha * acc + jnp.dot(p.astype(bf16), vblk, preferred_element_type=f32)
                return m_new, l, acc

            m, l, acc = lax.fori_loop(lo, hi + 1, step, (m0, l0, jnp.zeros((tq, LANES), f32)))
            res.append((acc / l, m + jnp.log(l)))
        o_ref[...] = jnp.where(low, res[0][0], res[1][0]).astype(bf16)
        lse_ref[...] = jnp.where(low, res[0][1], res[1][1])

    q_spec = pl.BlockSpec((tq, LANES), lambda j, i: (i, j))
    kv_spec = pl.BlockSpec((s, LANES), lambda j, i: (0, j))
    in_specs, args = [q_spec, kv_spec, kv_spec], [q, k, v]
    if fox:
        in_specs += [q_spec, pl.BlockSpec((None, 2, s), lambda j, i: (j, 0, 0))]
        args += [cum[0], cum[1]]
    if has_sink:
        in_specs += [pl.BlockSpec((1, LANES), lambda j, i: (0, j))]
        args += [sink_b]
    hw = N_PAIRS * LANES
    return pl.pallas_call(
        body, name=name, grid=(N_PAIRS, s // tq), in_specs=in_specs, out_specs=[q_spec, q_spec],
        out_shape=[jax.ShapeDtypeStruct((s, hw), bf16), jax.ShapeDtypeStruct((s, hw), f32)], compiler_params=_params(2),
    )(*args)


def _attn_delta(do, o, name, *, lse=None, sink_b=None):
    s, hw = do.shape
    tm = _row_tile(s, 512)
    has_sink = sink_b is not None

    def body(*refs):
        do_ref, o_ref = refs[:2]
        if has_sink:
            lse_ref, sink_ref, dl_ref, ds_ref = refs[2:]

            @pl.when(pl.program_id(0) == 0)
            def _():
                ds_ref[...] = jnp.zeros_like(ds_ref)
        else:
            dl_ref, = refs[2:]
        low = _lane() < HEAD_DIM
        for j in range(N_PAIRS):
            cols = slice(j * LANES, (j + 1) * LANES)
            prod = do_ref[:, cols] * o_ref[:, cols].astype(f32)
            s0 = jnp.sum(jnp.where(low, prod, 0.0), axis=1, keepdims=True)
            s1 = jnp.sum(jnp.where(low, 0.0, prod), axis=1, keepdims=True)
            dl = jnp.where(low, s0, s1)
            dl_ref[:, cols] = dl
            if has_sink:
                p_sink = jnp.exp(sink_ref[:, cols] - lse_ref[:, cols])
                ds_ref[:, cols] += -jnp.sum(p_sink * dl, axis=0, keepdims=True)

    in_specs, args = [_row_spec(tm, hw)] * 2, [do, o]
    out_specs, out_shape = [_row_spec(tm, hw)], [jax.ShapeDtypeStruct((s, hw), f32)]
    if has_sink:
        in_specs += [_row_spec(tm, hw), _vec_spec(hw)]
        args += [lse, sink_b]
        out_specs += [_vec_spec(hw)]
        out_shape += [jax.ShapeDtypeStruct((1, hw), f32)]
    return pl.pallas_call(
        body, name=name, grid=(s // tm,), in_specs=in_specs, out_specs=out_specs, out_shape=out_shape,
        compiler_params=_params(1),
    )(*args)


def _attn_bwd(q, k, v, do, lse, delta, name, *, cum=None, window=None, tq=256, tk=256):
    s = q.shape[0]
    tq, tk = _row_tile(s, tq), _row_tile(s, tk)
    nq = s // tq
    fox = cum is not None

    def body(*refs):
        k_ref, v_ref, q_ref, do_ref, lse_ref, dl_ref = refs[:6]
        rest = list(refs[6:])
        cb_ref, cr_ref = (rest.pop(0), rest.pop(0)) if fox else (None, None)
        dq_ref, dk_ref, dv_ref = rest[:3]
        dc_ref, rs_ref = (rest[3], rest[4]) if fox else (None, None)
        kb = pl.program_id(1)
        k0 = kb * tk

        @pl.when(kb == 0)
        def _():
            dq_ref[...] = jnp.zeros_like(dq_ref)
            if fox:
                rs_ref[...] = jnp.zeros_like(rs_ref)

        low = _lane() < HEAD_DIM
        kblk, vblk = k_ref[...], v_ref[...]
        kpos = k0 + lax.broadcasted_iota(jnp.int32, (1, tk), 1)
        qlo = k0 // tq
        qhi = jnp.minimum((k0 + tk + window - 2) // tq, nq - 1) if window else nq - 1
        dk_acc = jnp.zeros((tk, LANES), f32)
        dv_acc = jnp.zeros((tk, LANES), f32)
        for i in range(2):
            sel = low if i == 0 else jnp.logical_not(low)
            ck = cr_ref[i:i + 1, :] if fox else None

            def step(qb, carry, sel=sel, ck=ck, i=i):
                dk_a, dv_a, cs = carry
                q0 = pl.multiple_of(qb * tq, tq)
                rows = pl.ds(q0, tq)
                q2, do2 = q_ref[rows, :], do_ref[rows, :].astype(bf16)
                qm = jnp.where(sel, q2, jnp.zeros_like(q2))
                dom = jnp.where(sel, do2, jnp.zeros_like(do2))
                lse_c = lse_ref[rows, HEAD_DIM * i:HEAD_DIM * i + 1]
                dl_c = dl_ref[rows, HEAD_DIM * i:HEAD_DIM * i + 1]
                sc = lax.dot_general(qm, kblk, _NT, preferred_element_type=f32)
                if fox:
                    sc = sc + cb_ref[rows, HEAD_DIM * i:HEAD_DIM * i + 1] - ck
                qpos = q0 + lax.broadcasted_iota(jnp.int32, (tq, 1), 0)
                valid = kpos <= qpos
                if window:
                    valid = jnp.logical_and(valid, qpos - kpos < window)
                p = jnp.where(valid, jnp.exp(jnp.where(valid, sc, NEG) - lse_c), 0.0)
                dp = lax.dot_general(dom, vblk, _NT, preferred_element_type=f32)
                ds = p * (dp - dl_c)
                pb, dsb = p.astype(bf16), ds.astype(bf16)
                dv_a = dv_a + lax.dot_general(pb, dom, _TN, preferred_element_type=f32)
                dk_a = dk_a + lax.dot_general(dsb, qm, _TN, preferred_element_type=f32)
                dq_blk = jnp.dot(dsb, kblk, preferred_element_type=f32)
                dq_ref[rows, :] += jnp.where(sel, dq_blk, 0.0)
                if fox:
                    rs_ref[rows, i * LANES:(i + 1) * LANES] += sum(ds[:, g * LANES:(g + 1) * LANES] for g in range(tk // LANES))
                return dk_a, dv_a, cs + jnp.sum(ds, axis=0, keepdims=True)

            dk_acc, dv_acc, cs = lax.fori_loop(qlo, qhi + 1, step, (dk_acc, dv_acc, jnp.zeros((1, tk), f32)))
            if fox:
                dc_ref[i:i + 1, :] = -cs
        dk_ref[...] = dk_acc
        dv_ref[...] = dv_acc

    kv_spec = pl.BlockSpec((tk, LANES), lambda j, b: (b, j))
    seq_spec = pl.BlockSpec((s, LANES), lambda j, b: (0, j))
    hw = N_PAIRS * LANES
    in_specs, args = [kv_spec, kv_spec] + [seq_spec] * 4, [k, v, q, do, lse, delta]
    out_specs = [seq_spec, kv_spec, kv_spec]
    out_shape = [jax.ShapeDtypeStruct((s, hw), f32)] * 3
    if fox:
        in_specs += [seq_spec, pl.BlockSpec((None, 2, tk), lambda j, b: (j, 0, b))]
        args += [cum[0], cum[1]]
        out_specs += [pl.BlockSpec((None, 2, tk), lambda j, b: (j, 0, b)), pl.BlockSpec((s, 2 * LANES), lambda j, b: (0, j))]
        out_shape += [jax.ShapeDtypeStruct((N_PAIRS, 2, s), f32), jax.ShapeDtypeStruct((s, N_HEADS * LANES), f32)]
    return pl.pallas_call(
        body, name=name, grid=(N_PAIRS, s // tk), in_specs=in_specs, out_specs=out_specs, out_shape=out_shape,
        compiler_params=_params(2),
    )(*args)


def _merge(ba, bb, gl, name):
    s, d = ba.shape
    tm = _row_tile(s, 512)

    def body(a_ref, b_ref, g_ref, o_ref):
        o_ref[...] = (jax.nn.sigmoid(g_ref[:, :d]) * a_ref[...] + jax.nn.sigmoid(g_ref[:, d:]) * b_ref[...]).astype(bf16)

    return pl.pallas_call(
        body, name=name, grid=(s // tm,), in_specs=[_row_spec(tm, d)] * 2 + [_row_spec(tm, 2 * d)],
        out_specs=_row_spec(tm, d), out_shape=jax.ShapeDtypeStruct((s, d), bf16), compiler_params=_params(1),
    )(ba, bb, gl)


def _merge_bwd(dm, ba, bb, gl, name):
    s, d = ba.shape
    tm = _row_tile(s, 512)

    def body(dm_ref, a_ref, b_ref, g_ref, da_ref, db_ref, dg_ref):
        dmv = dm_ref[...]
        g0, g1 = jax.nn.sigmoid(g_ref[:, :d]), jax.nn.sigmoid(g_ref[:, d:])
        da_ref[...] = (dmv * g0).astype(bf16)
        db_ref[...] = (dmv * g1).astype(bf16)
        dg_ref[:, :d] = (dmv * a_ref[...] * (g0 * (1.0 - g0))).astype(bf16)
        dg_ref[:, d:] = (dmv * b_ref[...] * (g1 * (1.0 - g1))).astype(bf16)

    return pl.pallas_call(
        body, name=name, grid=(s // tm,), in_specs=[_row_spec(tm, d)] * 3 + [_row_spec(tm, 2 * d)],
        out_specs=[_row_spec(tm, d)] * 2 + [_row_spec(tm, 2 * d)],
        out_shape=[jax.ShapeDtypeStruct((s, d), bf16)] * 2 + [jax.ShapeDtypeStruct((s, 2 * d), bf16)],
        compiler_params=_params(1),
    )(dm, ba, bb, gl)


def _swiglu(gu, name):
    s, w = gu.shape
    h = w // 2
    tm = _row_tile(s, 256)

    def body(g_ref, o_ref):
        g = g_ref[:, :h]
        o_ref[...] = (g * jax.nn.sigmoid(g) * g_ref[:, h:]).astype(bf16)

    return pl.pallas_call(
        body, name=name, grid=(s // tm,), in_specs=[_row_spec(tm, w)], out_specs=_row_spec(tm, h),
        out_shape=jax.ShapeDtypeStruct((s, h), bf16), compiler_params=_params(1),
    )(gu)


def _swiglu_bwd(dact, gu, name):
    s, w = gu.shape
    h = w // 2
    tm = _row_tile(s, 256)

    def body(d_ref, g_ref, o_ref):
        dv, g, u = d_ref[...], g_ref[:, :h], g_ref[:, h:]
        sg = jax.nn.sigmoid(g)
        o_ref[:, :h] = (dv * u * (sg * (1.0 + g * (1.0 - sg)))).astype(bf16)
        o_ref[:, h:] = (dv * (g * sg)).astype(bf16)

    return pl.pallas_call(
        body, name=name, grid=(s // tm,), in_specs=[_row_spec(tm, h), _row_spec(tm, w)], out_specs=_row_spec(tm, w),
        out_shape=jax.ShapeDtypeStruct((s, w), bf16), compiler_params=_params(1),
    )(dact, gu)


def _ada_fwd(c_all, w, b, name):
    def body(c_ref, w_ref, b_ref, o_ref):
        o_ref[...] = jnp.dot(c_ref[...].astype(bf16), w_ref[...].astype(bf16), preferred_element_type=f32) + b_ref[...]

    return pl.pallas_call(
        body, name=name, out_shape=jax.ShapeDtypeStruct((c_all.shape[0], w.shape[1]), f32), compiler_params=_params(),
    )(c_all, w, b)


def _ada_wgrad(c_all, d_all, name):
    n, d = c_all.shape
    w = d_all.shape[1]

    def body(c_ref, d_ref, o_ref):
        eye = (lax.broadcasted_iota(jnp.int32, (n, n), 0) == lax.broadcasted_iota(jnp.int32, (n, n), 1)).astype(f32)
        ct = lax.dot_general(c_ref[...], eye, _TN, precision=lax.Precision.HIGHEST, preferred_element_type=f32)
        g = ct[:, 0:1] * d_ref[0:1, :]
        for bi in range(1, n):
            g = g + ct[:, bi:bi + 1] * d_ref[bi:bi + 1, :]
        o_ref[0] = g

    return pl.pallas_call(
        body, name=name, out_shape=jax.ShapeDtypeStruct((1, d, w), f32), compiler_params=_params(),
    )(c_all, d_all)


def _adamw(parts, w, m, v, name):
    r, c = w.shape
    n_parts = parts.shape[0]
    tr = next(t for t in range(min(r, 256), 0, -1) if r % t == 0 and (t % 16 == 0 or t == r))

    def body(p_ref, w_ref, m_ref, v_ref, g_ref, d_ref, nm_ref, nv_ref):
        g = p_ref[0].astype(f32)
        for i in range(1, n_parts):
            g = g + p_ref[i].astype(f32)
        mm = ADAM_B1 * m_ref[...] + (1.0 - ADAM_B1) * g
        vv = ADAM_B2 * v_ref[...] + (1.0 - ADAM_B2) * (g * g)
        m_hat = mm / (1.0 - ADAM_B1 ** ADAM_STEP)
        v_hat = vv / (1.0 - ADAM_B2 ** ADAM_STEP)
        g_ref[...] = g
        d_ref[...] = -ADAM_LR * (m_hat / (jnp.sqrt(v_hat) + ADAM_EPS) + ADAM_WD * w_ref[...])
        nm_ref[...] = mm
        nv_ref[...] = vv

    spec = pl.BlockSpec((tr, c), lambda i: (i, 0))
    return pl.pallas_call(
        body, name=name, grid=(r // tr,), in_specs=[pl.BlockSpec((n_parts, tr, c), lambda i: (0, i, 0))] + [spec] * 3,
        out_specs=[spec] * 4, out_shape=[jax.ShapeDtypeStruct((r, c), f32)] * 4, compiler_params=_params(1),
    )(parts, w, m, v)


def _me():
    return lax.axis_index("x"), lax.axis_index("y"), lax.axis_index("c")


def _all_gather(arrays, name, vmem=False):
    n = len(arrays)
    space = pltpu.VMEM if vmem else pl.ANY

    def body(*refs):
        ins, outs = refs[:n], refs[n:2 * n]
        send_sems, recv_sems, local_sems = refs[2 * n:]
        x, y, c = _me()
        me, sibling = (x, y, c), (x, y, 1 - c)
        chips = [(1 - x, y), (x, 1 - y), (1 - x, 1 - y)]

        def rows(a, dev):
            return outs[a].at[4 * dev[0] + 2 * dev[1] + dev[2]]

        def copy(a, k, block, to, src=None):
            return pltpu.make_async_remote_copy(
                src_ref=rows(a, block) if src is None else src, dst_ref=rows(a, block),
                send_sem=send_sems.at[a, k], recv_sem=recv_sems.at[a, k], device_id=to, device_id_type=MESH)

        mine = [pltpu.make_async_copy(ins[a], rows(a, me), local_sems.at[a]) for a in range(n)]
        for cp in mine:
            cp.start()
        first = []
        for a in range(n):
            first.append(copy(a, 0, me, sibling, src=ins[a]))
            first += [copy(a, 1 + j, me, (*chip, c), src=ins[a]) for j, chip in enumerate(chips)]
        for cp in first:
            cp.start()
        passed = []
        for j, chip in enumerate(chips):
            for a in range(n):
                copy(a, 1 + j, (*chip, c), me).wait_recv()
                fwd = copy(a, 4 + j, (*chip, c), sibling)
                fwd.start()
                passed.append(fwd)
        for a in range(n):
            copy(a, 0, sibling, me).wait_recv()
            for j, chip in enumerate(chips):
                copy(a, 4 + j, (*chip, 1 - c), me).wait_recv()
        for cp in first + passed:
            cp.wait_send()
        for cp in mine:
            cp.wait()

    outs = pl.pallas_call(
        body, name=name,
        in_specs=[pl.BlockSpec(memory_space=space)] * n, out_specs=[pl.BlockSpec(memory_space=space)] * n,
        out_shape=[jax.ShapeDtypeStruct((N_DEV,) + a.shape, a.dtype) for a in arrays],
        scratch_shapes=[pltpu.SemaphoreType.DMA((n, 7)), pltpu.SemaphoreType.DMA((n, 7)), pltpu.SemaphoreType.DMA((n,))],
        compiler_params=pltpu.CompilerParams(vmem_limit_bytes=VMEM_LIMIT),
    )(*arrays)
    return list(outs)


def _all_to_all(arrays, name):
    n = len(arrays)
    flips = [(0, 0, 1), (1, 0, 0), (0, 1, 0), (1, 1, 0), (1, 0, 1), (0, 1, 1), (1, 1, 1)]

    def body(*refs):
        ins, outs = refs[:n], refs[n:2 * n]
        send_sems, recv_sems, local_sems = refs[2 * n:]
        x, y, c = _me()
        me_row = 4 * x + 2 * y + c
        peers = [(x ^ fx, y ^ fy, c ^ fc) for fx, fy, fc in flips]

        def row(dev):
            return 4 * dev[0] + 2 * dev[1] + dev[2]

        def copy(a, k):
            return pltpu.make_async_remote_copy(
                src_ref=ins[a].at[row(peers[k])], dst_ref=outs[a].at[me_row],
                send_sem=send_sems.at[a, k], recv_sem=recv_sems.at[a, k], device_id=peers[k], device_id_type=MESH)

        def landed(a, k):
            return pltpu.make_async_remote_copy(
                src_ref=ins[a].at[me_row], dst_ref=outs[a].at[row(peers[k])],
                send_sem=send_sems.at[a, k], recv_sem=recv_sems.at[a, k], device_id=peers[k], device_id_type=MESH)

        mine = [pltpu.make_async_copy(ins[a].at[me_row], outs[a].at[me_row], local_sems.at[a]) for a in range(n)]
        sends = [copy(a, k) for k in range(7) for a in range(n)]
        for cp in mine + sends:
            cp.start()
        for k in range(7):
            for a in range(n):
                landed(a, k).wait_recv()
        for cp in sends:
            cp.wait_send()
        for cp in mine:
            cp.wait()

    outs = pl.pallas_call(
        body, name=name,
        in_specs=[pl.BlockSpec(memory_space=pl.ANY)] * n, out_specs=[pl.BlockSpec(memory_space=pl.ANY)] * n,
        out_shape=[jax.ShapeDtypeStruct(a.shape, a.dtype) for a in arrays],
        scratch_shapes=[pltpu.SemaphoreType.DMA((n, 7)), pltpu.SemaphoreType.DMA((n, 7)), pltpu.SemaphoreType.DMA((n,))],
        compiler_params=pltpu.CompilerParams(vmem_limit_bytes=VMEM_LIMIT),
    )(*arrays)
    return list(outs)


def _cols_from_shards(g):
    return jnp.transpose(g, (1, 0, 2)).reshape(g.shape[1], -1)


def _shards_from_cols(a):
    return jnp.transpose(a.reshape(a.shape[0], N_DEV, -1), (1, 0, 2))


def _head_lanes(v):
    return jnp.repeat(v.astype(f32), HEAD_DIM, axis=1)


def _local_step(x, positions, ada, g_pre_mix, g_post_mix, b_f, sinks, g_pre_ffn, g_post_ffn, target,
                w_in, w_branch_a, w_branch_b, w_out, w_ffn_in, w_ffn_out):
    s, d = x.shape
    row = lambda v: v.reshape(1, -1)
    shift_m, scale_m, gate_m, shift_f, scale_f, gate_f = (ada[i:i + 1] for i in range(6))
    w_gate, w_qkv = w_in[:, F_OFF + N_HEADS:], w_in[:, :QKV_W]
    w_f = jnp.pad(w_in[:, F_OFF:F_OFF + N_HEADS], ((0, 0), (0, LANES - N_HEADS)))
    w_in_p = jnp.concatenate([w_gate, w_qkv, w_f], axis=1)
    bf_row = jnp.pad(row(b_f), ((0, 0), (0, LANES - N_HEADS)))
    sink_b = _head_lanes(row(sinks))
    inv_freq = 1.0 / (ROPE_THETA ** (jnp.arange(0, HEAD_DIM, 2, dtype=f32) / HEAD_DIM))
    cos, sin_s = _rope_tables(positions.reshape(s, 1), jnp.tile(inv_freq, 4).reshape(1, LANES), "rope_tables")

    h1 = _prenorm(x, row(g_pre_mix), scale_m, shift_m, "prenorm_mix")
    gl = _matmul(h1, w_gate, "nn", f32, "proj_gate")
    qkv = _matmul(h1, w_qkv, "nn", f32, "proj_qkv")
    fl = _matmul(h1, w_f, "nn", f32, "proj_forget")
    qa, ka, va, qb, kb, vb = _qkv_prep(qkv, cos, sin_s, "qkv_prep")
    cum_b, cum_r = _forget_prep(fl, bf_row, "forget_prep")
    cum = (cum_b, cum_r.reshape(N_PAIRS, 2, s))
    o_a, lse_a = _attn_fwd(qa, ka, va, "swa_fwd", sink_b=sink_b, window=WINDOW, tk=128)
    o_b, lse_b = _attn_fwd(qb, kb, vb, "fox_fwd", cum=cum)
    ba = _matmul(o_a, w_branch_a, "nn", f32, "branch_a")
    bb = _matmul(o_b, w_branch_b, "nn", f32, "branch_b")
    merged = _merge(ba, bb, gl, "merge")
    y1 = _matmul(merged, w_out, "nn", f32, "out_proj")
    x2 = _postnorm_res(x, y1, row(g_post_mix), gate_m, "postnorm_mix")

    h2 = _prenorm(x2, row(g_pre_ffn), scale_f, shift_f, "prenorm_ffn")
    gu = _matmul(h2, w_ffn_in, "nn", f32, "ffn_in")
    act = _swiglu(gu, "swiglu")
    y2 = _matmul(act, w_ffn_out, "nn", f32, "ffn_out")
    out = _postnorm_res(x2, y2, row(g_post_ffn), gate_f, "postnorm_ffn")
    loss_row, d_out = _loss_head(out, target, "loss_head")

    d_y2, vec_pf = _postnorm_bwd(d_out, y2, row(g_post_ffn), gate_f, "postnorm_ffn_bwd")
    g_w_ffn_out = _matmul(act, d_y2, "tn", bf16, "ffn_out_wgrad")
    d_act = _matmul(d_y2, w_ffn_out, "nt", f32, "ffn_out_dgrad")
    dgu = _swiglu_bwd(d_act, gu, "swiglu_bwd")
    g_w_ffn_in = _matmul(h2, dgu, "tn", bf16, "ffn_in_wgrad")
    d_h2 = _matmul(dgu, w_ffn_in, "nt", f32, "ffn_in_dgrad")
    d_x2, vec_nf = _prenorm_bwd(d_h2, x2, row(g_pre_ffn), scale_f, d_out, "prenorm_ffn_bwd")

    d_y1, vec_pm = _postnorm_bwd(d_x2, y1, row(g_post_mix), gate_m, "postnorm_mix_bwd")
    g_w_out = _matmul(merged, d_y1, "tn", bf16, "out_proj_wgrad")
    d_merged = _matmul(d_y1, w_out, "nt", f32, "out_proj_dgrad")
    d_ba, d_bb, dgl = _merge_bwd(d_merged, ba, bb, gl, "merge_bwd")
    g_w_branch_a = _matmul(o_a, d_ba, "tn", bf16, "branch_a_wgrad")
    g_w_branch_b = _matmul(o_b, d_bb, "tn", bf16, "branch_b_wgrad")
    d_oa = _matmul(d_ba, w_branch_a, "nt", f32, "branch_a_dgrad")
    d_ob = _matmul(d_bb, w_branch_b, "nt", f32, "branch_b_dgrad")
    delta_a, d_sink_b = _attn_delta(d_oa, o_a, "swa_delta", lse=lse_a, sink_b=sink_b)
    delta_b, = _attn_delta(d_ob, o_b, "fox_delta")
    dqa, dka, dva = _attn_bwd(qa, ka, va, d_oa, lse_a, delta_a, "swa_bwd", window=WINDOW, tk=128)
    dqb, dkb, dvb, dcr, rs = _attn_bwd(qb, kb, vb, d_ob, lse_b, delta_b, "fox_bwd", cum=cum)
    dqkv = _qkv_prep_bwd(dqa, dka, dva, dqb, dkb, dvb, cos, sin_s, "qkv_prep_bwd")
    dfl, vec_bf = _forget_prep_bwd(dcr.reshape(N_HEADS, s), rs, fl, bf_row, "forget_prep_bwd")
    dproj = jnp.concatenate([dgl, dqkv, dfl], axis=1)
    g_w_in_p = _matmul(h1, dproj, "tn", bf16, "in_proj_wgrad")
    d_h1 = _matmul(dproj, w_in_p, "nt", f32, "in_proj_dgrad")
    grad_x, vec_nm = _prenorm_bwd(d_h1, x, row(g_pre_mix), scale_m, d_x2, "prenorm_mix_bwd")

    g_w_in = jnp.concatenate([g_w_in_p[:, GATE_W:GATE_W + QKV_W], g_w_in_p[:, GATE_W + QKV_W:GATE_W + QKV_W + N_HEADS],
                              g_w_in_p[:, :GATE_W]], axis=1)
    d_ada = jnp.concatenate([vec_nm[0], vec_nm[1], vec_pm[0], vec_nf[0], vec_nf[1], vec_pf[0]])
    small = dict(b_ada=d_ada, g_pre_mix=vec_nm[2], g_post_mix=vec_pm[1], g_pre_ffn=vec_nf[2], g_post_ffn=vec_pf[1],
                 b_f=vec_bf[0, :N_HEADS], sinks=d_sink_b[0, ::HEAD_DIM], loss=loss_row[0, :1])
    big = dict(w_in=g_w_in, w_branch_a=g_w_branch_a, w_branch_b=g_w_branch_b, w_out=g_w_out,
               w_ffn_in=g_w_ffn_in, w_ffn_out=g_w_ffn_out)
    return grad_x, big, small


_SMALL = (("b_ada", 6144), ("g_pre_mix", 1024), ("g_post_mix", 1024), ("g_pre_ffn", 1024), ("g_post_ffn", 1024),
          ("b_f", 128), ("sinks", 128), ("loss", 128))
_SMALL_ROWS = 88


def _pack_small(vals):
    parts = [jnp.pad(vals[k].reshape(-1).astype(f32), (0, n - vals[k].size)) for k, n in _SMALL]
    flat = jnp.concatenate(parts)
    return jnp.pad(flat, (0, _SMALL_ROWS * LANES - flat.size)).reshape(_SMALL_ROWS, LANES)


def _unpack_small(slab, shapes):
    flat, out, off = slab.reshape(-1), {}, 0
    for k, n in _SMALL:
        size = math.prod(shapes[k])
        out[k] = flat[off:off + size].reshape(shapes[k])
        off += n
    return out


def kernel(x, c, positions, w_ada, b_ada, g_pre_mix, g_post_mix, w_in, b_f, sinks, w_branch_a, w_branch_b, w_out, g_pre_ffn, g_post_ffn, w_ffn_in, w_ffn_out, loss_target, m_w_ada, m_b_ada, m_g_pre_mix, m_g_post_mix, m_w_in, m_b_f, m_sinks, m_w_branch_a, m_w_branch_b, m_w_out, m_g_pre_ffn, m_g_post_ffn, m_w_ffn_in, m_w_ffn_out, v_w_ada, v_b_ada, v_g_pre_mix, v_g_post_mix, v_w_in, v_b_f, v_sinks, v_w_branch_a, v_w_branch_b, v_w_out, v_g_pre_ffn, v_g_post_ffn, v_w_ffn_in, v_w_ffn_out):
    xi, yi, ci = _me()
    me = 4 * xi + 2 * yi + ci
    d = D_MODEL
    ada_w = w_ada.shape[2]

    c_all, = _all_gather([c], "gather_c", vmem=True)
    c_all = c_all.reshape(N_DEV, d)
    b_mine = lax.dynamic_slice(b_ada, (0, me * ada_w), (1, ada_w))
    ada_cols = _ada_fwd(c_all, w_ada[0], b_mine, "ada_fwd")
    ada_all, = _all_gather([ada_cols], "gather_ada", vmem=True)
    ada = lax.dynamic_index_in_dim(ada_all, me, axis=1, keepdims=False).reshape(6, d)

    shards = [w_in[0], w_branch_a[0], w_branch_b[0], w_out[0], w_ffn_in[0], w_ffn_out[0]]
    g_in, g_ba, g_bb, g_out, g_fi, g_fo = _all_gather([w.astype(bf16) for w in shards], "gather_weights")
    grad_x, big, small = _local_step(
        x[0], positions[0], ada, g_pre_mix[0], g_post_mix[0], b_f[0], sinks[0], g_pre_ffn[0], g_post_ffn[0], loss_target[0],
        _cols_from_shards(g_in), _cols_from_shards(g_ba), _cols_from_shards(g_bb), g_out.reshape(d, d),
        _cols_from_shards(g_fi), g_fo.reshape(D_FF, d))

    slab_all, = _all_gather([_pack_small(small)], "gather_small", vmem=True)
    small_w = dict(b_ada=b_ada, g_pre_mix=g_pre_mix, g_post_mix=g_post_mix, g_pre_ffn=g_pre_ffn, g_post_ffn=g_post_ffn,
                   b_f=b_f, sinks=sinks, loss=jnp.zeros((1,), f32))
    small_m = dict(b_ada=m_b_ada, g_pre_mix=m_g_pre_mix, g_post_mix=m_g_post_mix, g_pre_ffn=m_g_pre_ffn,
                   g_post_ffn=m_g_post_ffn, b_f=m_b_f, sinks=m_sinks, loss=jnp.zeros((1,), f32))
    small_v = dict(b_ada=v_b_ada, g_pre_mix=v_g_pre_mix, g_post_mix=v_g_post_mix, g_pre_ffn=v_g_pre_ffn,
                   g_post_ffn=v_g_post_ffn, b_f=v_b_f, sinks=v_sinks, loss=jnp.ones((1,), f32))
    shapes = {k: small_w[k].shape for k, _ in _SMALL}
    s_out = _adamw(slab_all, _pack_small(small_w), _pack_small(small_m), _pack_small(small_v), "adamw_small")
    s_grad, s_delta, s_m, s_v = (_unpack_small(o, shapes) for o in s_out)

    d_ada_all = lax.dynamic_slice(slab_all[:, :6144 // LANES, :].reshape(N_DEV, 6144), (0, me * ada_w), (N_DEV, ada_w))
    ada_parts = _ada_wgrad(c_all, d_ada_all, "ada_wgrad")

    sends = [_shards_from_cols(big["w_in"]), _shards_from_cols(big["w_branch_a"]), _shards_from_cols(big["w_branch_b"]),
             big["w_out"].reshape(N_DEV, d // N_DEV, d), _shards_from_cols(big["w_ffn_in"]),
             big["w_ffn_out"].reshape(N_DEV, D_FF // N_DEV, d)]
    recv = _all_to_all(sends, "scatter_grads")
    names = ["w_in", "w_branch_a", "w_branch_b", "w_out", "w_ffn_in", "w_ffn_out"]
    ws = dict(w_in=(w_in, m_w_in, v_w_in), w_branch_a=(w_branch_a, m_w_branch_a, v_w_branch_a),
              w_branch_b=(w_branch_b, m_w_branch_b, v_w_branch_b), w_out=(w_out, m_w_out, v_w_out),
              w_ffn_in=(w_ffn_in, m_w_ffn_in, v_w_ffn_in), w_ffn_out=(w_ffn_out, m_w_ffn_out, v_w_ffn_out),
              w_ada=(w_ada, m_w_ada, v_w_ada))
    res = {"w_ada": _adamw(ada_parts, w_ada[0], m_w_ada[0], v_w_ada[0], "adamw_w_ada")}
    for nm, parts in zip(names, recv):
        w, m, v = ws[nm]
        res[nm] = _adamw(parts, w[0], m[0], v[0], "adamw_" + nm)

    order = ["w_ada", "b_ada", "g_pre_mix", "g_post_mix", "w_in", "b_f", "sinks", "w_branch_a", "w_branch_b", "w_out",
             "g_pre_ffn", "g_post_ffn", "w_ffn_in", "w_ffn_out"]
    outs = [s_grad["loss"].reshape(()), grad_x[None]]
    for which, small_o in enumerate((s_grad, s_delta, s_m, s_v)):
        for nm in order:
            outs.append(res[nm][which][None] if nm in res else small_o[nm])
    return tuple(outs)
```

```python
import functools
import math

import jax
import jax.numpy as jnp
from jax import lax
from jax.experimental import pallas as pl
from jax.experimental.pallas import tpu as pltpu

f32 = jnp.float32
bf16 = jnp.bfloat16

D_MODEL = 1024
HEAD_DIM = 64
N_HEADS = 8
N_PAIRS = 4
QKV_W = 2304
GATE_W = 2048
F_OFF = 2304
IN_W = 4360
WINDOW = 128
ROPE_THETA = 10000.0
RMS_EPS = 1e-6
D_FF = 2816
N_DEV = 8
ADAM_LR, ADAM_B1, ADAM_B2, ADAM_EPS, ADAM_WD, ADAM_STEP = 0.001, 0.9, 0.999, 1e-08, 0.01, 10
NEG = -1e30
LANES = 128
VMEM_LIMIT = 48 * 1024 * 1024
MESH = pl.DeviceIdType.MESH

_NT = (((1,), (1,)), ((), ()))
_TN = (((0,), (0,)), ((), ()))


def _params(n_grid=0):
    sem = ("arbitrary",) * n_grid if n_grid else None
    return pltpu.CompilerParams(dimension_semantics=sem, vmem_limit_bytes=VMEM_LIMIT)


def _row_tile(s, want):
    t = min(s, want)
    assert s % t == 0, (s, t)
    return t


def _col_tile(n):
    for t in (512, 768, 640, 256, 384, 128):
        if n % t == 0:
            return t
    raise ValueError(n)


def _matmul(a, b, mode, out_dtype, name):
    if mode == "nn":
        (m, k), n = a.shape, b.shape[1]
    elif mode == "nt":
        (m, k), n = a.shape, b.shape[0]
    else:
        (k, m), n = a.shape, b.shape[1]
    tm = _row_tile(m, {"nn": 1024, "nt": 512, "tn": 256}[mode])
    tn = _col_tile(n)
    if mode == "nn":
        a_spec, b_spec, dims = pl.BlockSpec((tm, k), lambda i, j: (i, 0)), pl.BlockSpec((k, tn), lambda i, j: (0, j)), None
    elif mode == "nt":
        a_spec, b_spec, dims = pl.BlockSpec((tm, k), lambda i, j: (i, 0)), pl.BlockSpec((tn, k), lambda i, j: (j, 0)), _NT
    else:
        a_spec, b_spec, dims = pl.BlockSpec((k, tm), lambda i, j: (0, i)), pl.BlockSpec((k, tn), lambda i, j: (0, j)), _TN

    def body(a_ref, b_ref, o_ref):
        av, bv = a_ref[...].astype(bf16), b_ref[...].astype(bf16)
        if dims is None:
            r = jnp.dot(av, bv, preferred_element_type=f32)
        else:
            r = lax.dot_general(av, bv, dims, preferred_element_type=f32)
        o_ref[...] = r.astype(out_dtype)

    return pl.pallas_call(
        body, name=name, grid=(m // tm, n // tn), in_specs=[a_spec, b_spec],
        out_specs=pl.BlockSpec((tm, tn), lambda i, j: (i, j)),
        out_shape=jax.ShapeDtypeStruct((m, n), out_dtype), compiler_params=_params(2),
    )(a, b)


def _rstd(v):
    return lax.rsqrt(jnp.mean(v * v, axis=-1, keepdims=True) + RMS_EPS)


def _row_spec(tm, d):
    return pl.BlockSpec((tm, d), lambda i: (i, 0))


def _vec_spec(d, rows=1):
    return pl.BlockSpec((rows, d), lambda i: (0, 0))


def _prenorm(x, g, scale, shift, name):
    s, d = x.shape
    tm = _row_tile(s, 512)

    def body(x_ref, g_ref, sc_ref, sh_ref, h_ref):
        xv = x_ref[...]
        h = (xv * _rstd(xv) * g_ref[...]) * (1.0 + sc_ref[...]) + sh_ref[...]
        h_ref[...] = h.astype(bf16)

    return pl.pallas_call(
        body, name=name, grid=(s // tm,), in_specs=[_row_spec(tm, d)] + [_vec_spec(d)] * 3,
        out_specs=_row_spec(tm, d), out_shape=jax.ShapeDtypeStruct((s, d), bf16), compiler_params=_params(1),
    )(x, g, scale, shift)


def _postnorm_res(x, y, g, gate, name):
    s, d = x.shape
    tm = _row_tile(s, 512)

    def body(x_ref, y_ref, g_ref, gate_ref, o_ref):
        yv = y_ref[...]
        o_ref[...] = x_ref[...] + gate_ref[...] * (yv * _rstd(yv) * g_ref[...])

    return pl.pallas_call(
        body, name=name, grid=(s // tm,), in_specs=[_row_spec(tm, d)] * 2 + [_vec_spec(d)] * 2,
        out_specs=_row_spec(tm, d), out_shape=jax.ShapeDtypeStruct((s, d), f32), compiler_params=_params(1),
    )(x, y, g, gate)


def _loss_head(out, target, name):
    s, d = out.shape
    tm = _row_tile(s, 512)

    def body(o_ref, t_ref, loss_ref, d_ref):
        @pl.when(pl.program_id(0) == 0)
        def _():
            loss_ref[...] = jnp.zeros_like(loss_ref)
        err = o_ref[...] - t_ref[...]
        d_ref[...] = err / d
        loss_ref[...] += 0.5 * jnp.sum(jnp.mean(err * err, axis=-1, keepdims=True), axis=0, keepdims=True)

    return pl.pallas_call(
        body, name=name, grid=(s // tm,), in_specs=[_row_spec(tm, d)] * 2,
        out_specs=[_vec_spec(LANES), _row_spec(tm, d)],
        out_shape=[jax.ShapeDtypeStruct((1, LANES), f32), jax.ShapeDtypeStruct((s, d), f32)], compiler_params=_params(1),
    )(out, target)


def _rms_bwd(u, v, r):
    return r * u - v * (r * r * r) * jnp.mean(u * v, axis=-1, keepdims=True)


def _postnorm_bwd(dres, y, g, gate, name):
    s, d = y.shape
    tm = _row_tile(s, 512)

    def body(dr_ref, y_ref, g_ref, gate_ref, dy_ref, vec_ref):
        @pl.when(pl.program_id(0) == 0)
        def _():
            vec_ref[...] = jnp.zeros_like(vec_ref)
        dr, yv = dr_ref[...], y_ref[...]
        r = _rstd(yv)
        yn = yv * r
        dn = dr * gate_ref[...]
        vec_ref[0:1, :] += jnp.sum(dr * (yn * g_ref[...]), axis=0, keepdims=True)
        vec_ref[1:2, :] += jnp.sum(dn * yn, axis=0, keepdims=True)
        dy_ref[...] = _rms_bwd(dn * g_ref[...], yv, r).astype(bf16)

    return pl.pallas_call(
        body, name=name, grid=(s // tm,), in_specs=[_row_spec(tm, d)] * 2 + [_vec_spec(d)] * 2,
        out_specs=[_row_spec(tm, d), _vec_spec(d, 8)],
        out_shape=[jax.ShapeDtypeStruct((s, d), bf16), jax.ShapeDtypeStruct((8, d), f32)], compiler_params=_params(1),
    )(dres, y, g, gate)


def _prenorm_bwd(dh, x, g, scale, dres, name):
    s, d = x.shape
    tm = _row_tile(s, 512)

    def body(dh_ref, x_ref, g_ref, sc_ref, dr_ref, dx_ref, vec_ref):
        @pl.when(pl.program_id(0) == 0)
        def _():
            vec_ref[...] = jnp.zeros_like(vec_ref)
        dhv, xv = dh_ref[...], x_ref[...]
        r = _rstd(xv)
        xn = xv * r
        dn = dhv * (1.0 + sc_ref[...])
        vec_ref[0:1, :] += jnp.sum(dhv, axis=0, keepdims=True)
        vec_ref[1:2, :] += jnp.sum(dhv * (xn * g_ref[...]), axis=0, keepdims=True)
        vec_ref[2:3, :] += jnp.sum(dn * xn, axis=0, keepdims=True)
        dx_ref[...] = dr_ref[...] + _rms_bwd(dn * g_ref[...], xv, r)

    return pl.pallas_call(
        body, name=name, grid=(s // tm,),
        in_specs=[_row_spec(tm, d)] * 2 + [_vec_spec(d)] * 2 + [_row_spec(tm, d)],
        out_specs=[_row_spec(tm, d), _vec_spec(d, 8)],
        out_shape=[jax.ShapeDtypeStruct((s, d), f32), jax.ShapeDtypeStruct((8, d), f32)], compiler_params=_params(1),
    )(dh, x, g, scale, dres)


def _lane():
    return lax.broadcasted_iota(jnp.int32, (1, LANES), 1)


def _rope_tables(pos_col, inv_freq, name):
    s = pos_col.shape[0]

    def body(p_ref, f_ref, cos_ref, sin_ref):
        ang = p_ref[...].astype(f32) * f_ref[...]
        first_half = (_lane() % HEAD_DIM) < HEAD_DIM // 2
        cos_ref[...] = jnp.cos(ang)
        sn = jnp.sin(ang)
        sin_ref[...] = jnp.where(first_half, -sn, sn)

    return pl.pallas_call(
        body, name=name, out_shape=[jax.ShapeDtypeStruct((s, LANES), f32)] * 2, compiler_params=_params(),
    )(pos_col, inv_freq)


def _swap_halves(v):
    first_half = (_lane() % HEAD_DIM) < HEAD_DIM // 2
    return jnp.where(first_half, pltpu.roll(v, LANES - HEAD_DIM // 2, axis=1), pltpu.roll(v, HEAD_DIM // 2, axis=1))


def _qkv_prep(qkv, cos, sin_s, name):
    s = qkv.shape[0]
    tm = _row_tile(s, 256)
    scale = 1.0 / math.sqrt(HEAD_DIM)

    def body(p_ref, c_ref, s_ref, qa_ref, ka_ref, va_ref, qb_ref, kb_ref, vb_ref):
        cs, sn = c_ref[...], s_ref[...]
        low = _lane() < HEAD_DIM

        def blk(j):
            return p_ref[:, j * LANES:(j + 1) * LANES]

        def rope(v):
            return v * cs + _swap_halves(v) * sn

        def expand(v):
            other = pltpu.roll(v, HEAD_DIM, axis=1)
            return jnp.where(low, v, other), jnp.where(low, other, v)

        for j in range(N_PAIRS):
            qa_ref[:, j * LANES:(j + 1) * LANES] = (rope(blk(j)) * scale).astype(bf16)
            qb_ref[:, j * LANES:(j + 1) * LANES] = (blk(6 + j) * scale).astype(bf16)
            kb_ref[:, j * LANES:(j + 1) * LANES] = blk(10 + j).astype(bf16)
            vb_ref[:, j * LANES:(j + 1) * LANES] = blk(14 + j).astype(bf16)
        k0, k1 = expand(rope(blk(4)))
        v0, v1 = expand(blk(5))
        for j in range(N_PAIRS):
            ka_ref[:, j * LANES:(j + 1) * LANES] = (k0 if j < 2 else k1).astype(bf16)
            va_ref[:, j * LANES:(j + 1) * LANES] = (v0 if j < 2 else v1).astype(bf16)

    hw = N_PAIRS * LANES
    return pl.pallas_call(
        body, name=name, grid=(s // tm,),
        in_specs=[_row_spec(tm, QKV_W), _row_spec(tm, LANES), _row_spec(tm, LANES)],
        out_specs=[_row_spec(tm, hw)] * 6, out_shape=[jax.ShapeDtypeStruct((s, hw), bf16)] * 6, compiler_params=_params(1),
    )(qkv, cos, sin_s)


def _qkv_prep_bwd(dqa_t, dka, dva, dqb_t, dkb, dvb, cos, sin_s, name):
    s = dka.shape[0]
    tm = _row_tile(s, 256)
    scale = 1.0 / math.sqrt(HEAD_DIM)
    hw = N_PAIRS * LANES
    t_spec = pl.BlockSpec((hw, tm), lambda i: (0, i))

    def body(dqa_ref, dka_ref, dva_ref, dqb_ref, dkb_ref, dvb_ref, c_ref, s_ref, o_ref):
        cs, sn = c_ref[...], s_ref[...]
        low = _lane() < HEAD_DIM

        def blk(ref, j):
            return ref[:, j * LANES:(j + 1) * LANES]

        def blk_t(ref, j):
            return ref[j * LANES:(j + 1) * LANES, :].T

        def unrope(v):
            return v * cs + _swap_halves(v * sn)

        def fold(ref):
            a, b = blk(ref, 0) + blk(ref, 1), blk(ref, 2) + blk(ref, 3)
            kv0 = a + pltpu.roll(a, HEAD_DIM, axis=1)
            kv1 = b + pltpu.roll(b, HEAD_DIM, axis=1)
            return jnp.where(low, kv0, kv1)

        for j in range(N_PAIRS):
            o_ref[:, j * LANES:(j + 1) * LANES] = (unrope(blk_t(dqa_ref, j)) * scale).astype(bf16)
            o_ref[:, (6 + j) * LANES:(7 + j) * LANES] = (blk_t(dqb_ref, j) * scale).astype(bf16)
            o_ref[:, (10 + j) * LANES:(11 + j) * LANES] = blk(dkb_ref, j).astype(bf16)
            o_ref[:, (14 + j) * LANES:(15 + j) * LANES] = blk(dvb_ref, j).astype(bf16)
        o_ref[:, 4 * LANES:5 * LANES] = unrope(fold(dka_ref)).astype(bf16)
        o_ref[:, 5 * LANES:6 * LANES] = fold(dva_ref).astype(bf16)

    return pl.pallas_call(
        body, name=name, grid=(s // tm,),
        in_specs=[t_spec, _row_spec(tm, hw), _row_spec(tm, hw), t_spec, _row_spec(tm, hw), _row_spec(tm, hw)] + [_row_spec(tm, LANES)] * 2,
        out_specs=_row_spec(tm, QKV_W), out_shape=jax.ShapeDtypeStruct((s, QKV_W), bf16), compiler_params=_params(1),
    )(dqa_t, dka, dva, dqb_t, dkb, dvb, cos, sin_s)


def _cumsum_rows(v, reverse=False):
    n = v.shape[0]
    row = lax.broadcasted_iota(jnp.int32, v.shape, 0)
    sh = 1
    while sh < n:
        if reverse:
            v = v + jnp.where(row < n - sh, pltpu.roll(v, n - sh, axis=0), 0.0)
        else:
            v = v + jnp.where(row >= sh, pltpu.roll(v, sh, axis=0), 0.0)
        sh *= 2
    return v


def _log_sigmoid(z):
    return jnp.minimum(z, 0.0) - jnp.log1p(jnp.exp(-jnp.abs(z)))


def _forget_prep(fl, bf_row, name):
    s = fl.shape[0]

    def body(f_ref, b_ref, cb_ref):
        cum = _cumsum_rows(_log_sigmoid(f_ref[...] + b_ref[...]))
        for h in range(N_HEADS):
            cb_ref[:, h * LANES:(h + 1) * LANES] = jnp.broadcast_to(cum[:, h:h + 1], (s, LANES))

    return pl.pallas_call(
        body, name=name, out_shape=jax.ShapeDtypeStruct((s, N_HEADS * LANES), f32), compiler_params=_params(),
    )(fl, bf_row)


def _forget_prep_bwd(rs, dcs, fl, bf_row, name):
    s = fl.shape[0]

    def body(r_ref, c_ref, f_ref, b_ref, df_ref, db_ref):
        eye = (lax.broadcasted_iota(jnp.int32, (N_HEADS, LANES), 0) == lax.broadcasted_iota(jnp.int32, (N_HEADS, LANES), 1)).astype(f32)
        dcum = lax.dot_general(r_ref[...], eye, _TN, precision=lax.Precision.HIGHEST, preferred_element_type=f32)
        for h in range(N_HEADS):
            dcum = dcum - jnp.where(_lane() == h, jnp.sum(c_ref[:, h * LANES:(h + 1) * LANES], axis=1, keepdims=True), 0.0)
        dlf = _cumsum_rows(dcum, reverse=True)
        z = f_ref[...] + b_ref[...]
        df = jnp.where(_lane() < N_HEADS, dlf * jax.nn.sigmoid(-z), 0.0)
        df_ref[...] = df.astype(bf16)
        db_ref[...] = jnp.zeros_like(db_ref)
        db_ref[0:1, :] = jnp.sum(df, axis=0, keepdims=True)

    return pl.pallas_call(
        body, name=name,
        out_shape=[jax.ShapeDtypeStruct((s, LANES), bf16), jax.ShapeDtypeStruct((8, LANES), f32)], compiler_params=_params(),
    )(rs, dcs, fl, bf_row)


def _tile_mask(t, off, window):
    d = lax.broadcasted_iota(jnp.int32, (t, t), 1) - lax.broadcasted_iota(jnp.int32, (t, t), 0) + off
    valid = d >= 0
    return jnp.logical_and(valid, d < window) if window else valid


def _wide(v, t):
    return jnp.concatenate([v] * (t // LANES), axis=1)


def _attn_fwd(q, k, v, name, *, cum_b=None, sink_rows=None, window=None, t=256):
    s = q.shape[0]
    t = _row_tile(s, t)
    fox, has_sink = cum_b is not None, sink_rows is not None
    assert not window or window <= t

    def body(*refs):
        q_ref, k_ref, v_ref = refs[:3]
        rest = list(refs[3:])
        cb_ref = rest.pop(0) if fox else None
        sink_ref = rest.pop(0) if has_sink else None
        o_ref, lse_ref = rest
        i = pl.program_id(1)
        low = _lane() < HEAD_DIM
        q2 = q_ref[...]
        zero = jnp.zeros_like(q2)
        qms = (jnp.where(low, q2, zero), jnp.where(low, zero, q2))

        def tile(kb, carry, masked):
            k0 = pl.multiple_of(kb * t, t)
            kblk, vblk = k_ref[pl.ds(k0, t), :], v_ref[pl.ds(k0, t), :]
            valid = _tile_mask(t, (i - kb) * t, window) if masked else None
            out = []
            for h in range(2):
                m, l, acc = carry[h]
                sc = lax.dot_general(kblk, qms[h], _NT, preferred_element_type=f32)
                if fox:
                    sc = sc - _wide(cb_ref[pl.ds(k0, t), h * LANES:(h + 1) * LANES], t)
                if masked:
                    sc = jnp.where(valid, sc, NEG)
                m_new = jnp.maximum(m, jnp.max(sc, axis=0, keepdims=True))
                p = jnp.exp(sc - m_new)
                alpha = jnp.exp(m - m_new)
                l = alpha * l + jnp.sum(p, axis=0, keepdims=True)
                acc = alpha * acc + lax.dot_general(vblk, p.astype(bf16), _TN, preferred_element_type=f32)
                out.append((m_new, l, acc))
            return tuple(out)

        init = []
        for h in range(2):
            if has_sink:
                init.append((_wide(sink_ref[h:h + 1, :], t), jnp.ones((1, t), f32), jnp.zeros((LANES, t), f32)))
            else:
                init.append((jnp.full((1, t), NEG, f32), jnp.zeros((1, t), f32), jnp.zeros((LANES, t), f32)))
        carry = tuple(init)
        if window:
            carry = lax.fori_loop(jnp.maximum(i - 1, 0), i + 1, functools.partial(tile, masked=True), carry)
        else:
            carry = lax.fori_loop(0, i, functools.partial(tile, masked=False), carry)
            carry = tile(i, carry, True)
        (m0, l0, a0), (m1, l1, a1) = carry
        top = lax.broadcasted_iota(jnp.int32, (LANES, 1), 0) < HEAD_DIM
        o_t = jnp.where(top, a0 * (1.0 / l0), a1 * (1.0 / l1))
        o_ref[...] = o_t.T.astype(bf16)
        lse_ref[0:1, :] = m0 + jnp.log(l0)
        lse_ref[1:2, :] = m1 + jnp.log(l1)

    q_spec = pl.BlockSpec((t, LANES), lambda j, i: (i, j))
    kv_spec = pl.BlockSpec((s, LANES), lambda j, i: (0, j))
    in_specs, args = [q_spec, kv_spec, kv_spec], [q, k, v]
    if fox:
        in_specs += [pl.BlockSpec((s, 2 * LANES), lambda j, i: (0, j))]
        args += [cum_b]
    if has_sink:
        in_specs += [pl.BlockSpec((None, 2, LANES), lambda j, i: (j, 0, 0))]
        args += [sink_rows.reshape(N_PAIRS, 2, LANES)]
    return pl.pallas_call(
        body, name=name, grid=(N_PAIRS, s // t), in_specs=in_specs,
        out_specs=[q_spec, pl.BlockSpec((None, 2, t), lambda j, i: (j, 0, i))],
        out_shape=[jax.ShapeDtypeStruct((s, N_PAIRS * LANES), bf16), jax.ShapeDtypeStruct((N_PAIRS, 2, s), f32)],
        compiler_params=_params(2),
    )(*args)


def _attn_delta(do, o, name, *, lse=None, sink_rows=None):
    s, hw = do.shape
    tm = _row_tile(s, 512)
    has_sink = sink_rows is not None

    def body(*refs):
        do_ref, o_ref = refs[:2]
        if has_sink:
            lse_ref, sink_ref, dl_ref, ds_ref = refs[2:]

            @pl.when(pl.program_id(0) == 0)
            def _():
                ds_ref[...] = jnp.zeros_like(ds_ref)
        else:
            dl_ref, = refs[2:]
        for j in range(N_PAIRS):
            cols = slice(j * LANES, (j + 1) * LANES)
            prod_t = (do_ref[:, cols].astype(f32) * o_ref[:, cols].astype(f32)).T
            for h in range(2):
                dl = jnp.sum(prod_t[h * HEAD_DIM:(h + 1) * HEAD_DIM, :], axis=0, keepdims=True)
                dl_ref[j, h:h + 1, :] = dl
                if has_sink:
                    r = 2 * j + h
                    p_sink = jnp.exp(sink_ref[r:r + 1, 0:1] - lse_ref[j, h:h + 1, :])
                    ds_ref[r:r + 1, :] += -jnp.sum(p_sink * dl, axis=1, keepdims=True)

    rows_spec = pl.BlockSpec((N_PAIRS, 2, tm), lambda i: (0, 0, i))
    in_specs, args = [_row_spec(tm, hw)] * 2, [do, o]
    out_specs, out_shape = [rows_spec], [jax.ShapeDtypeStruct((N_PAIRS, 2, s), f32)]
    if has_sink:
        in_specs += [rows_spec, _vec_spec(LANES, N_HEADS)]
        args += [lse, sink_rows]
        out_specs += [_vec_spec(LANES, N_HEADS)]
        out_shape += [jax.ShapeDtypeStruct((N_HEADS, LANES), f32)]
    return pl.pallas_call(
        body, name=name, grid=(s // tm,), in_specs=in_specs, out_specs=out_specs, out_shape=out_shape,
        compiler_params=_params(1),
    )(*args)


def _attn_bwd(q, k, v, do, lse, delta, name, *, cum_b=None, window=None, t=256):
    s = q.shape[0]
    t = _row_tile(s, t)
    nblk = s // t
    fox = cum_b is not None
    assert not window or window <= t

    def body(*refs):
        k_ref, v_ref, q_ref, do_ref, lse_ref, dl_ref = refs[:6]
        rest = list(refs[6:])
        cb_ref = rest.pop(0) if fox else None
        dq_ref, dk_ref, dv_ref = rest[:3]
        dcs_ref, rs_ref = (rest[3], rest[4]) if fox else (None, None)
        b = pl.program_id(1)
        k0 = pl.multiple_of(b * t, t)

        @pl.when(b == 0)
        def _():
            dq_ref[...] = jnp.zeros_like(dq_ref)
            if fox:
                rs_ref[...] = jnp.zeros_like(rs_ref)

        dk_ref[...] = jnp.zeros_like(dk_ref)
        dv_ref[...] = jnp.zeros_like(dv_ref)
        if fox:
            dcs_ref[...] = jnp.zeros_like(dcs_ref)
        low = _lane() < HEAD_DIM
        top = lax.broadcasted_iota(jnp.int32, (LANES, 1), 0) < HEAD_DIM
        kblk, vblk = k_ref[...], v_ref[...]
        k_t = kblk.astype(f32).T.astype(bf16)
        cks = [_wide(cb_ref[pl.ds(k0, t), h * LANES:(h + 1) * LANES], t) for h in range(2)] if fox else None

        def tile(qb, carry, masked):
            q0 = pl.multiple_of(qb * t, t)
            cols = pl.ds(q0, t)
            q2, do2 = q_ref[cols, :], do_ref[cols, :]
            zero = jnp.zeros_like(q2)
            valid = _tile_mask(t, (qb - b) * t, window) if masked else None
            dq_parts = []
            for h in range(2):
                qm = jnp.where(low, q2, zero) if h == 0 else jnp.where(low, zero, q2)
                dom = jnp.where(low, do2, zero) if h == 0 else jnp.where(low, zero, do2)
                sc = lax.dot_general(kblk, qm, _NT, preferred_element_type=f32)
                if fox:
                    sc = sc - cks[h]
                if masked:
                    sc = jnp.where(valid, sc, NEG)
                p = jnp.exp(sc - lse_ref[h:h + 1, cols])
                dp = lax.dot_general(vblk, dom, _NT, preferred_element_type=f32)
                ds = p * (dp - dl_ref[h:h + 1, cols])
                pb, dsb = p.astype(bf16), ds.astype(bf16)
                dv_ref[...] += jnp.dot(pb, dom, preferred_element_type=f32)
                dk_ref[...] += jnp.dot(dsb, qm, preferred_element_type=f32)
                dq_parts.append(jnp.dot(k_t, dsb, preferred_element_type=f32))
                if fox:
                    dcs_ref[:, h * LANES:(h + 1) * LANES] += sum(ds[:, g * LANES:(g + 1) * LANES] for g in range(t // LANES))
                    rs_ref[h:h + 1, cols] += jnp.sum(ds, axis=0, keepdims=True)
            dq_ref[:, cols] += jnp.where(top, dq_parts[0], dq_parts[1])
            return carry

        if window:
            lax.fori_loop(b, jnp.minimum(b + 1, nblk - 1) + 1, functools.partial(tile, masked=True), 0)
        else:
            tile(b, 0, True)
            lax.fori_loop(b + 1, nblk, functools.partial(tile, masked=False), 0)

    kv_spec = pl.BlockSpec((t, LANES), lambda j, b: (b, j))
    seq_spec = pl.BlockSpec((s, LANES), lambda j, b: (0, j))
    rows_spec = pl.BlockSpec((None, 2, s), lambda j, b: (j, 0, 0))
    hw = N_PAIRS * LANES
    in_specs, args = [kv_spec, kv_spec, seq_spec, seq_spec, rows_spec, rows_spec], [k, v, q, do, lse, delta]
    out_specs = [pl.BlockSpec((LANES, s), lambda j, b: (j, 0)), kv_spec, kv_spec]
    out_shape = [jax.ShapeDtypeStruct((hw, s), f32), jax.ShapeDtypeStruct((s, hw), f32), jax.ShapeDtypeStruct((s, hw), f32)]
    if fox:
        in_specs += [pl.BlockSpec((s, 2 * LANES), lambda j, b: (0, j))]
        args += [cum_b]
        out_specs += [pl.BlockSpec((t, 2 * LANES), lambda j, b: (b, j)), rows_spec]
        out_shape += [jax.ShapeDtypeStruct((s, N_HEADS * LANES), f32), jax.ShapeDtypeStruct((N_PAIRS, 2, s), f32)]
    return pl.pallas_call(
        body, name=name, grid=(N_PAIRS, nblk), in_specs=in_specs, out_specs=out_specs, out_shape=out_shape,
        compiler_params=_params(2),
    )(*args)


def _merge(ba, bb, gl, name):
    s, d = ba.shape
    tm = _row_tile(s, 512)

    def body(a_ref, b_ref, g_ref, o_ref):
        o_ref[...] = (jax.nn.sigmoid(g_ref[:, :d]) * a_ref[...] + jax.nn.sigmoid(g_ref[:, d:]) * b_ref[...]).astype(bf16)

    return pl.pallas_call(
        body, name=name, grid=(s // tm,), in_specs=[_row_spec(tm, d)] * 2 + [_row_spec(tm, 2 * d)],
        out_specs=_row_spec(tm, d), out_shape=jax.ShapeDtypeStruct((s, d), bf16), compiler_params=_params(1),
    )(ba, bb, gl)


def _merge_bwd(dm, ba, bb, gl, name):
    s, d = ba.shape
    tm = _row_tile(s, 512)

    def body(dm_ref, a_ref, b_ref, g_ref, da_ref, db_ref, dg_ref):
        dmv = dm_ref[...]
        g0, g1 = jax.nn.sigmoid(g_ref[:, :d]), jax.nn.sigmoid(g_ref[:, d:])
        da_ref[...] = (dmv * g0).astype(bf16)
        db_ref[...] = (dmv * g1).astype(bf16)
        dg_ref[:, :d] = (dmv * a_ref[...] * (g0 * (1.0 - g0))).astype(bf16)
        dg_ref[:, d:] = (dmv * b_ref[...] * (g1 * (1.0 - g1))).astype(bf16)

    return pl.pallas_call(
        body, name=name, grid=(s // tm,), in_specs=[_row_spec(tm, d)] * 3 + [_row_spec(tm, 2 * d)],
        out_specs=[_row_spec(tm, d)] * 2 + [_row_spec(tm, 2 * d)],
        out_shape=[jax.ShapeDtypeStruct((s, d), bf16)] * 2 + [jax.ShapeDtypeStruct((s, 2 * d), bf16)],
        compiler_params=_params(1),
    )(dm, ba, bb, gl)


def _swiglu(gu, name):
    s, w = gu.shape
    h = w // 2
    tm = _row_tile(s, 256)

    def body(g_ref, o_ref):
        g = g_ref[:, :h]
        o_ref[...] = (g * jax.nn.sigmoid(g) * g_ref[:, h:]).astype(bf16)

    return pl.pallas_call(
        body, name=name, grid=(s // tm,), in_specs=[_row_spec(tm, w)], out_specs=_row_spec(tm, h),
        out_shape=jax.ShapeDtypeStruct((s, h), bf16), compiler_params=_params(1),
    )(gu)


def _swiglu_bwd(dact, gu, name):
    s, w = gu.shape
    h = w // 2
    tm = _row_tile(s, 256)

    def body(d_ref, g_ref, o_ref):
        dv, g, u = d_ref[...], g_ref[:, :h], g_ref[:, h:]
        sg = jax.nn.sigmoid(g)
        o_ref[:, :h] = (dv * u * (sg * (1.0 + g * (1.0 - sg)))).astype(bf16)
        o_ref[:, h:] = (dv * (g * sg)).astype(bf16)

    return pl.pallas_call(
        body, name=name, grid=(s // tm,), in_specs=[_row_spec(tm, h), _row_spec(tm, w)], out_specs=_row_spec(tm, w),
        out_shape=jax.ShapeDtypeStruct((s, w), bf16), compiler_params=_params(1),
    )(dact, gu)


def _ada_fwd(c_all, w, b, name):
    def body(c_ref, w_ref, b_ref, o_ref):
        o_ref[...] = jnp.dot(c_ref[...].astype(bf16), w_ref[...].astype(bf16), preferred_element_type=f32) + b_ref[...]

    return pl.pallas_call(
        body, name=name, out_shape=jax.ShapeDtypeStruct((c_all.shape[0], w.shape[1]), f32), compiler_params=_params(),
    )(c_all, w, b)


def _ada_wgrad(c_all, d_all, name):
    n, d = c_all.shape
    w = d_all.shape[1]

    def body(c_ref, d_ref, o_ref):
        eye = (lax.broadcasted_iota(jnp.int32, (n, n), 0) == lax.broadcasted_iota(jnp.int32, (n, n), 1)).astype(f32)
        ct = lax.dot_general(c_ref[...], eye, _TN, precision=lax.Precision.HIGHEST, preferred_element_type=f32)
        g = ct[:, 0:1] * d_ref[0:1, :]
        for bi in range(1, n):
            g = g + ct[:, bi:bi + 1] * d_ref[bi:bi + 1, :]
        o_ref[0] = g

    return pl.pallas_call(
        body, name=name, out_shape=jax.ShapeDtypeStruct((1, d, w), f32), compiler_params=_params(),
    )(c_all, d_all)


def _adamw(parts, w, m, v, name):
    r, c = w.shape
    n_parts = parts.shape[0]
    tr = next(t for t in range(min(r, 256), 0, -1) if r % t == 0 and (t % 16 == 0 or t == r))

    def body(p_ref, w_ref, m_ref, v_ref, g_ref, d_ref, nm_ref, nv_ref):
        g = p_ref[0].astype(f32)
        for i in range(1, n_parts):
            g = g + p_ref[i].astype(f32)
        mm = ADAM_B1 * m_ref[...] + (1.0 - ADAM_B1) * g
        vv = ADAM_B2 * v_ref[...] + (1.0 - ADAM_B2) * (g * g)
        m_hat = mm / (1.0 - ADAM_B1 ** ADAM_STEP)
        v_hat = vv / (1.0 - ADAM_B2 ** ADAM_STEP)
        g_ref[...] = g
        d_ref[...] = -ADAM_LR * (m_hat / (jnp.sqrt(v_hat) + ADAM_EPS) + ADAM_WD * w_ref[...])
        nm_ref[...] = mm
        nv_ref[...] = vv

    spec = pl.BlockSpec((tr, c), lambda i: (i, 0))
    return pl.pallas_call(
        body, name=name, grid=(r // tr,), in_specs=[pl.BlockSpec((n_parts, tr, c), lambda i: (0, i, 0))] + [spec] * 3,
        out_specs=[spec] * 4, out_shape=[jax.ShapeDtypeStruct((r, c), f32)] * 4, compiler_params=_params(1),
    )(parts, w, m, v)


def _me():
    return lax.axis_index("x"), lax.axis_index("y"), lax.axis_index("c")


def _all_gather(arrays, name, vmem=False):
    n = len(arrays)
    space = pltpu.VMEM if vmem else pl.ANY

    def body(*refs):
        ins, outs = refs[:n], refs[n:2 * n]
        send_sems, recv_sems, local_sems = refs[2 * n:]
        x, y, c = _me()
        me, sibling = (x, y, c), (x, y, 1 - c)
        chips = [(1 - x, y), (x, 1 - y), (1 - x, 1 - y)]

        def rows(a, dev):
            return outs[a].at[4 * dev[0] + 2 * dev[1] + dev[2]]

        def copy(a, k, block, to, src=None):
            return pltpu.make_async_remote_copy(
                src_ref=rows(a, block) if src is None else src, dst_ref=rows(a, block),
                send_sem=send_sems.at[a, k], recv_sem=recv_sems.at[a, k], device_id=to, device_id_type=MESH)

        mine = [pltpu.make_async_copy(ins[a], rows(a, me), local_sems.at[a]) for a in range(n)]
        for cp in mine:
            cp.start()
        first = []
        for a in range(n):
            first.append(copy(a, 0, me, sibling, src=ins[a]))
            first += [copy(a, 1 + j, me, (*chip, c), src=ins[a]) for j, chip in enumerate(chips)]
        for cp in first:
            cp.start()
        passed = []
        for j, chip in enumerate(chips):
            for a in range(n):
                copy(a, 1 + j, (*chip, c), me).wait_recv()
                fwd = copy(a, 4 + j, (*chip, c), sibling)
                fwd.start()
                passed.append(fwd)
        for a in range(n):
            copy(a, 0, sibling, me).wait_recv()
            for j, chip in enumerate(chips):
                copy(a, 4 + j, (*chip, 1 - c), me).wait_recv()
        for cp in first + passed:
            cp.wait_send()
        for cp in mine:
            cp.wait()

    outs = pl.pallas_call(
        body, name=name,
        in_specs=[pl.BlockSpec(memory_space=space)] * n, out_specs=[pl.BlockSpec(memory_space=space)] * n,
        out_shape=[jax.ShapeDtypeStruct((N_DEV,) + a.shape, a.dtype) for a in arrays],
        scratch_shapes=[pltpu.SemaphoreType.DMA((n, 7)), pltpu.SemaphoreType.DMA((n, 7)), pltpu.SemaphoreType.DMA((n,))],
        compiler_params=pltpu.CompilerParams(vmem_limit_bytes=VMEM_LIMIT),
    )(*arrays)
    return list(outs)


def _all_to_all(arrays, name):
    n = len(arrays)
    flips = [(0, 0, 1), (1, 0, 0), (0, 1, 0), (1, 1, 0), (1, 0, 1), (0, 1, 1), (1, 1, 1)]

    def body(*refs):
        ins, outs = refs[:n], refs[n:2 * n]
        send_sems, recv_sems, local_sems = refs[2 * n:]
        x, y, c = _me()
        me_row = 4 * x + 2 * y + c
        peers = [(x ^ fx, y ^ fy, c ^ fc) for fx, fy, fc in flips]

        def row(dev):
            return 4 * dev[0] + 2 * dev[1] + dev[2]

        def copy(a, k):
            return pltpu.make_async_remote_copy(
                src_ref=ins[a].at[row(peers[k])], dst_ref=outs[a].at[me_row],
                send_sem=send_sems.at[a, k], recv_sem=recv_sems.at[a, k], device_id=peers[k], device_id_type=MESH)

        def landed(a, k):
            return pltpu.make_async_remote_copy(
                src_ref=ins[a].at[me_row], dst_ref=outs[a].at[row(peers[k])],
                send_sem=send_sems.at[a, k], recv_sem=recv_sems.at[a, k], device_id=peers[k], device_id_type=MESH)

        mine = [pltpu.make_async_copy(ins[a].at[me_row], outs[a].at[me_row], local_sems.at[a]) for a in range(n)]
        sends = [copy(a, k) for k in range(7) for a in range(n)]
        for cp in mine + sends:
            cp.start()
        for k in range(7):
            for a in range(n):
                landed(a, k).wait_recv()
        for cp in sends:
            cp.wait_send()
        for cp in mine:
            cp.wait()

    outs = pl.pallas_call(
        body, name=name,
        in_specs=[pl.BlockSpec(memory_space=pl.ANY)] * n, out_specs=[pl.BlockSpec(memory_space=pl.ANY)] * n,
        out_shape=[jax.ShapeDtypeStruct(a.shape, a.dtype) for a in arrays],
        scratch_shapes=[pltpu.SemaphoreType.DMA((n, 7)), pltpu.SemaphoreType.DMA((n, 7)), pltpu.SemaphoreType.DMA((n,))],
        compiler_params=pltpu.CompilerParams(vmem_limit_bytes=VMEM_LIMIT),
    )(*arrays)
    return list(outs)


def _cols_from_shards(g):
    return jnp.transpose(g, (1, 0, 2)).reshape(g.shape[1], -1)


def _shards_from_cols(a):
    return jnp.transpose(a.reshape(a.shape[0], N_DEV, -1), (1, 0, 2))


def _local_step(x, positions, ada, g_pre_mix, g_post_mix, b_f, sinks, g_pre_ffn, g_post_ffn, target,
                w_in, w_branch_a, w_branch_b, w_out, w_ffn_in, w_ffn_out):
    s, d = x.shape
    row = lambda v: v.reshape(1, -1)
    shift_m, scale_m, gate_m, shift_f, scale_f, gate_f = (ada[i:i + 1] for i in range(6))
    w_gate, w_qkv = w_in[:, F_OFF + N_HEADS:], w_in[:, :QKV_W]
    w_f = jnp.pad(w_in[:, F_OFF:F_OFF + N_HEADS], ((0, 0), (0, LANES - N_HEADS)))
    w_in_p = jnp.concatenate([w_gate, w_qkv, w_f], axis=1)
    bf_row = jnp.pad(row(b_f), ((0, 0), (0, LANES - N_HEADS)))
    sink_rows = jnp.broadcast_to(sinks.reshape(N_HEADS, 1).astype(f32), (N_HEADS, LANES))
    inv_freq = 1.0 / (ROPE_THETA ** (jnp.arange(0, HEAD_DIM, 2, dtype=f32) / HEAD_DIM))
    cos, sin_s = _rope_tables(positions.reshape(s, 1), jnp.tile(inv_freq, 4).reshape(1, LANES), "rope_tables")

    h1 = _prenorm(x, row(g_pre_mix), scale_m, shift_m, "prenorm_mix")
    gl = _matmul(h1, w_gate, "nn", f32, "proj_gate")
    qkv = _matmul(h1, w_qkv, "nn", f32, "proj_qkv")
    fl = _matmul(h1, w_f, "nn", f32, "proj_forget")
    qa, ka, va, qb, kb, vb = _qkv_prep(qkv, cos, sin_s, "qkv_prep")
    cum_b = _forget_prep(fl, bf_row, "forget_prep")
    o_a, lse_a = _attn_fwd(qa, ka, va, "swa_fwd", sink_rows=sink_rows, window=WINDOW)
    o_b, lse_b = _attn_fwd(qb, kb, vb, "fox_fwd", cum_b=cum_b)
    ba = _matmul(o_a, w_branch_a, "nn", f32, "branch_a")
    bb = _matmul(o_b, w_branch_b, "nn", f32, "branch_b")
    merged = _merge(ba, bb, gl, "merge")
    y1 = _matmul(merged, w_out, "nn", f32, "out_proj")
    x2 = _postnorm_res(x, y1, row(g_post_mix), gate_m, "postnorm_mix")

    h2 = _prenorm(x2, row(g_pre_ffn), scale_f, shift_f, "prenorm_ffn")
    gu = _matmul(h2, w_ffn_in, "nn", f32, "ffn_in")
    act = _swiglu(gu, "swiglu")
    y2 = _matmul(act, w_ffn_out, "nn", f32, "ffn_out")
    out = _postnorm_res(x2, y2, row(g_post_ffn), gate_f, "postnorm_ffn")
    loss_row, d_out = _loss_head(out, target, "loss_head")

    d_y2, vec_pf = _postnorm_bwd(d_out, y2, row(g_post_ffn), gate_f, "postnorm_ffn_bwd")
    g_w_ffn_out = _matmul(act, d_y2, "tn", bf16, "ffn_out_wgrad")
    d_act = _matmul(d_y2, w_ffn_out, "nt", f32, "ffn_out_dgrad")
    dgu = _swiglu_bwd(d_act, gu, "swiglu_bwd")
    g_w_ffn_in = _matmul(h2, dgu, "tn", bf16, "ffn_in_wgrad")
    d_h2 = _matmul(dgu, w_ffn_in, "nt", f32, "ffn_in_dgrad")
    d_x2, vec_nf = _prenorm_bwd(d_h2, x2, row(g_pre_ffn), scale_f, d_out, "prenorm_ffn_bwd")

    d_y1, vec_pm = _postnorm_bwd(d_x2, y1, row(g_post_mix), gate_m, "postnorm_mix_bwd")
    g_w_out = _matmul(merged, d_y1, "tn", bf16, "out_proj_wgrad")
    d_merged = _matmul(d_y1, w_out, "nt", f32, "out_proj_dgrad")
    d_ba, d_bb, dgl = _merge_bwd(d_merged, ba, bb, gl, "merge_bwd")
    g_w_branch_a = _matmul(o_a, d_ba, "tn", bf16, "branch_a_wgrad")
    g_w_branch_b = _matmul(o_b, d_bb, "tn", bf16, "branch_b_wgrad")
    d_oa = _matmul(d_ba, w_branch_a, "nt", bf16, "branch_a_dgrad")
    d_ob = _matmul(d_bb, w_branch_b, "nt", bf16, "branch_b_dgrad")
    delta_a, d_sink = _attn_delta(d_oa, o_a, "swa_delta", lse=lse_a, sink_rows=sink_rows)
    delta_b, = _attn_delta(d_ob, o_b, "fox_delta")
    dqa_t, dka, dva = _attn_bwd(qa, ka, va, d_oa, lse_a, delta_a, "swa_bwd", window=WINDOW)
    dqb_t, dkb, dvb, dcs, rs = _attn_bwd(qb, kb, vb, d_ob, lse_b, delta_b, "fox_bwd", cum_b=cum_b)
    dqkv = _qkv_prep_bwd(dqa_t, dka, dva, dqb_t, dkb, dvb, cos, sin_s, "qkv_prep_bwd")
    dfl, vec_bf = _forget_prep_bwd(rs.reshape(N_HEADS, s), dcs, fl, bf_row, "forget_prep_bwd")
    dproj = jnp.concatenate([dgl, dqkv, dfl], axis=1)
    g_w_in_p = _matmul(h1, dproj, "tn", bf16, "in_proj_wgrad")
    d_h1 = _matmul(dproj, w_in_p, "nt", f32, "in_proj_dgrad")
    grad_x, vec_nm = _prenorm_bwd(d_h1, x, row(g_pre_mix), scale_m, d_x2, "prenorm_mix_bwd")

    g_w_in = jnp.concatenate([g_w_in_p[:, GATE_W:GATE_W + QKV_W], g_w_in_p[:, GATE_W + QKV_W:GATE_W + QKV_W + N_HEADS],
                              g_w_in_p[:, :GATE_W]], axis=1)
    d_ada = jnp.concatenate([vec_nm[0], vec_nm[1], vec_pm[0], vec_nf[0], vec_nf[1], vec_pf[0]])
    small = dict(b_ada=d_ada, g_pre_mix=vec_nm[2], g_post_mix=vec_pm[1], g_pre_ffn=vec_nf[2], g_post_ffn=vec_pf[1],
                 b_f=vec_bf[0, :N_HEADS], sinks=d_sink[:, 0], loss=loss_row[0, :1])
    big = dict(w_in=g_w_in, w_branch_a=g_w_branch_a, w_branch_b=g_w_branch_b, w_out=g_w_out,
               w_ffn_in=g_w_ffn_in, w_ffn_out=g_w_ffn_out)
    return grad_x, big, small


_SMALL = (("b_ada", 6144), ("g_pre_mix", 1024), ("g_post_mix", 1024), ("g_pre_ffn", 1024), ("g_post_ffn", 1024),
          ("b_f", 128), ("sinks", 128), ("loss", 128))
_SMALL_ROWS = 88


def _pack_small(vals):
    parts = [jnp.pad(vals[k].reshape(-1).astype(f32), (0, n - vals[k].size)) for k, n in _SMALL]
    flat = jnp.concatenate(parts)
    return jnp.pad(flat, (0, _SMALL_ROWS * LANES - flat.size)).reshape(_SMALL_ROWS, LANES)


def _unpack_small(slab, shapes):
    flat, out, off = slab.reshape(-1), {}, 0
    for k, n in _SMALL:
        size = math.prod(shapes[k])
        out[k] = flat[off:off + size].reshape(shapes[k])
        off += n
    return out


def kernel(x, c, positions, w_ada, b_ada, g_pre_mix, g_post_mix, w_in, b_f, sinks, w_branch_a, w_branch_b, w_out, g_pre_ffn, g_post_ffn, w_ffn_in, w_ffn_out, loss_target, m_w_ada, m_b_ada, m_g_pre_mix, m_g_post_mix, m_w_in, m_b_f, m_sinks, m_w_branch_a, m_w_branch_b, m_w_out, m_g_pre_ffn, m_g_post_ffn, m_w_ffn_in, m_w_ffn_out, v_w_ada, v_b_ada, v_g_pre_mix, v_g_post_mix, v_w_in, v_b_f, v_sinks, v_w_branch_a, v_w_branch_b, v_w_out, v_g_pre_ffn, v_g_post_ffn, v_w_ffn_in, v_w_ffn_out):
    xi, yi, ci = _me()
    me = 4 * xi + 2 * yi + ci
    d = D_MODEL
    ada_w = w_ada.shape[2]

    c_all, = _all_gather([c], "gather_c", vmem=True)
    c_all = c_all.reshape(N_DEV, d)
    b_mine = lax.dynamic_slice(b_ada, (0, me * ada_w), (1, ada_w))
    ada_cols = _ada_fwd(c_all, w_ada[0], b_mine, "ada_fwd")
    ada_all, = _all_gather([ada_cols], "gather_ada", vmem=True)
    ada = lax.dynamic_index_in_dim(ada_all, me, axis=1, keepdims=False).reshape(6, d)

    shards = [w_in[0], w_branch_a[0], w_branch_b[0], w_out[0], w_ffn_in[0], w_ffn_out[0]]
    g_in, g_ba, g_bb, g_out, g_fi, g_fo = _all_gather([w.astype(bf16) for w in shards], "gather_weights")
    grad_x, big, small = _local_step(
        x[0], positions[0], ada, g_pre_mix[0], g_post_mix[0], b_f[0], sinks[0], g_pre_ffn[0], g_post_ffn[0], loss_target[0],
        _cols_from_shards(g_in), _cols_from_shards(g_ba), _cols_from_shards(g_bb), g_out.reshape(d, d),
        _cols_from_shards(g_fi), g_fo.reshape(D_FF, d))

    slab_all, = _all_gather([_pack_small(small)], "gather_small", vmem=True)
    small_w = dict(b_ada=b_ada, g_pre_mix=g_pre_mix, g_post_mix=g_post_mix, g_pre_ffn=g_pre_ffn, g_post_ffn=g_post_ffn,
                   b_f=b_f, sinks=sinks, loss=jnp.zeros((1,), f32))
    small_m = dict(b_ada=m_b_ada, g_pre_mix=m_g_pre_mix, g_post_mix=m_g_post_mix, g_pre_ffn=m_g_pre_ffn,
                   g_post_ffn=m_g_post_ffn, b_f=m_b_f, sinks=m_sinks, loss=jnp.zeros((1,), f32))
    small_v = dict(b_ada=v_b_ada, g_pre_mix=v_g_pre_mix, g_post_mix=v_g_post_mix, g_pre_ffn=v_g_pre_ffn,
                   g_post_ffn=v_g_post_ffn, b_f=v_b_f, sinks=v_sinks, loss=jnp.ones((1,), f32))
    shapes = {k: small_w[k].shape for k, _ in _SMALL}
    s_out = _adamw(slab_all, _pack_small(small_w), _pack_small(small_m), _pack_small(small_v), "adamw_small")
    s_grad, s_delta, s_m, s_v = (_unpack_small(o, shapes) for o in s_out)

    d_ada_all = lax.dynamic_slice(slab_all[:, :6144 // LANES, :].reshape(N_DEV, 6144), (0, me * ada_w), (N_DEV, ada_w))
    ada_parts = _ada_wgrad(c_all, d_ada_all, "ada_wgrad")

    sends = [_shards_from_cols(big["w_in"]), _shards_from_cols(big["w_branch_a"]), _shards_from_cols(big["w_branch_b"]),
             big["w_out"].reshape(N_DEV, d // N_DEV, d), _shards_from_cols(big["w_ffn_in"]),
             big["w_ffn_out"].reshape(N_DEV, D_FF // N_DEV, d)]
    recv = _all_to_all(sends, "scatter_grads")
    names = ["w_in", "w_branch_a", "w_branch_b", "w_out", "w_ffn_in", "w_ffn_out"]
    ws = dict(w_in=(w_in, m_w_in, v_w_in), w_branch_a=(w_branch_a, m_w_branch_a, v_w_branch_a),
              w_branch_b=(w_branch_b, m_w_branch_b, v_w_branch_b), w_out=(w_out, m_w_out, v_w_out),
              w_ffn_in=(w_ffn_in, m_w_ffn_in, v_w_ffn_in), w_ffn_out=(w_ffn_out, m_w_ffn_out, v_w_ffn_out),
              w_ada=(w_ada, m_w_ada, v_w_ada))
    res = {"w_ada": _adamw(ada_parts, w_ada[0], m_w_ada[0], v_w_ada[0], "adamw_w_ada")}
    for nm, parts in zip(names, recv):
        w, m, v = ws[nm]
        res[nm] = _adamw(parts, w[0], m[0], v[0], "adamw_" + nm)

    order = ["w_ada", "b_ada", "g_pre_mix", "g_post_mix", "w_in", "b_f", "sinks", "w_branch_a", "w_branch_b", "w_out",
             "g_pre_ffn", "g_post_ffn", "w_ffn_in", "w_ffn_out"]
    outs = [s_grad["loss"].reshape(()), grad_x[None]]
    for which, small_o in enumerate((s_grad, s_delta, s_m, s_v)):
        for nm in order:
            outs.append(res[nm][which][None] if nm in res else small_o[nm])
    return tuple(outs)
```

```python
import functools
import math

import jax
import jax.numpy as jnp
from jax import lax
from jax.experimental import pallas as pl
from jax.experimental.pallas import tpu as pltpu

f32 = jnp.float32
bf16 = jnp.bfloat16

D_MODEL = 1024
HEAD_DIM = 64
N_HEADS = 8
N_PAIRS = 4
QKV_W = 2304
GATE_W = 2048
F_OFF = 2304
IN_W = 4360
WINDOW = 128
ROPE_THETA = 10000.0
RMS_EPS = 1e-6
D_FF = 2816
N_DEV = 8
ADAM_LR, ADAM_B1, ADAM_B2, ADAM_EPS, ADAM_WD, ADAM_STEP = 0.001, 0.9, 0.999, 1e-08, 0.01, 10
NEG = -1e30
LANES = 128
VMEM_LIMIT = 48 * 1024 * 1024
MESH = pl.DeviceIdType.MESH

_NT = (((1,), (1,)), ((), ()))
_TN = (((0,), (0,)), ((), ()))


def _params(n_grid=0):
    sem = ("arbitrary",) * n_grid if n_grid else None
    return pltpu.CompilerParams(dimension_semantics=sem, vmem_limit_bytes=VMEM_LIMIT)


def _row_tile(s, want):
    t = min(s, want)
    assert s % t == 0, (s, t)
    return t


def _col_tile(n):
    for t in (512, 768, 640, 256, 384, 128):
        if n % t == 0:
            return t
    raise ValueError(n)


def _matmul(a, b, mode, out_dtype, name):
    if mode == "nn":
        (m, k), n = a.shape, b.shape[1]
    elif mode == "nt":
        (m, k), n = a.shape, b.shape[0]
    else:
        (k, m), n = a.shape, b.shape[1]
    tm = _row_tile(m, {"nn": 1024, "nt": 512, "tn": 256}[mode])
    tn = _col_tile(n)
    if mode == "nn":
        a_spec, b_spec, dims = pl.BlockSpec((tm, k), lambda i, j: (i, 0)), pl.BlockSpec((k, tn), lambda i, j: (0, j)), None
    elif mode == "nt":
        a_spec, b_spec, dims = pl.BlockSpec((tm, k), lambda i, j: (i, 0)), pl.BlockSpec((tn, k), lambda i, j: (j, 0)), _NT
    else:
        a_spec, b_spec, dims = pl.BlockSpec((k, tm), lambda i, j: (0, i)), pl.BlockSpec((k, tn), lambda i, j: (0, j)), _TN

    def body(a_ref, b_ref, o_ref):
        av, bv = a_ref[...].astype(bf16), b_ref[...].astype(bf16)
        if dims is None:
            r = jnp.dot(av, bv, preferred_element_type=f32)
        else:
            r = lax.dot_general(av, bv, dims, preferred_element_type=f32)
        o_ref[...] = r.astype(out_dtype)

    return pl.pallas_call(
        body, name=name, grid=(m // tm, n // tn), in_specs=[a_spec, b_spec],
        out_specs=pl.BlockSpec((tm, tn), lambda i, j: (i, j)),
        out_shape=jax.ShapeDtypeStruct((m, n), out_dtype), compiler_params=_params(2),
    )(a, b)


def _rstd(v):
    return lax.rsqrt(jnp.mean(v * v, axis=-1, keepdims=True) + RMS_EPS)


def _row_spec(tm, d):
    return pl.BlockSpec((tm, d), lambda i: (i, 0))


def _vec_spec(d, rows=1):
    return pl.BlockSpec((rows, d), lambda i: (0, 0))


def _prenorm(x, g, scale, shift, name):
    s, d = x.shape
    tm = _row_tile(s, 512)

    def body(x_ref, g_ref, sc_ref, sh_ref, h_ref):
        xv = x_ref[...]
        h = (xv * _rstd(xv) * g_ref[...]) * (1.0 + sc_ref[...]) + sh_ref[...]
        h_ref[...] = h.astype(bf16)

    return pl.pallas_call(
        body, name=name, grid=(s // tm,), in_specs=[_row_spec(tm, d)] + [_vec_spec(d)] * 3,
        out_specs=_row_spec(tm, d), out_shape=jax.ShapeDtypeStruct((s, d), bf16), compiler_params=_params(1),
    )(x, g, scale, shift)


def _postnorm_res(x, y, g, gate, name):
    s, d = x.shape
    tm = _row_tile(s, 512)

    def body(x_ref, y_ref, g_ref, gate_ref, o_ref):
        yv = y_ref[...]
        o_ref[...] = x_ref[...] + gate_ref[...] * (yv * _rstd(yv) * g_ref[...])

    return pl.pallas_call(
        body, name=name, grid=(s // tm,), in_specs=[_row_spec(tm, d)] * 2 + [_vec_spec(d)] * 2,
        out_specs=_row_spec(tm, d), out_shape=jax.ShapeDtypeStruct((s, d), f32), compiler_params=_params(1),
    )(x, y, g, gate)


def _loss_head(out, target, name):
    s, d = out.shape
    tm = _row_tile(s, 512)

    def body(o_ref, t_ref, loss_ref, d_ref):
        @pl.when(pl.program_id(0) == 0)
        def _():
            loss_ref[...] = jnp.zeros_like(loss_ref)
        err = o_ref[...] - t_ref[...]
        d_ref[...] = err / d
        loss_ref[...] += 0.5 * jnp.sum(jnp.mean(err * err, axis=-1, keepdims=True), axis=0, keepdims=True)

    return pl.pallas_call(
        body, name=name, grid=(s // tm,), in_specs=[_row_spec(tm, d)] * 2,
        out_specs=[_vec_spec(LANES), _row_spec(tm, d)],
        out_shape=[jax.ShapeDtypeStruct((1, LANES), f32), jax.ShapeDtypeStruct((s, d), f32)], compiler_params=_params(1),
    )(out, target)


def _rms_bwd(u, v, r):
    return r * u - v * (r * r * r) * jnp.mean(u * v, axis=-1, keepdims=True)


def _postnorm_bwd(dres, y, g, gate, name):
    s, d = y.shape
    tm = _row_tile(s, 512)

    def body(dr_ref, y_ref, g_ref, gate_ref, dy_ref, vec_ref):
        @pl.when(pl.program_id(0) == 0)
        def _():
            vec_ref[...] = jnp.zeros_like(vec_ref)
        dr, yv = dr_ref[...], y_ref[...]
        r = _rstd(yv)
        yn = yv * r
        dn = dr * gate_ref[...]
        vec_ref[0:1, :] += jnp.sum(dr * (yn * g_ref[...]), axis=0, keepdims=True)
        vec_ref[1:2, :] += jnp.sum(dn * yn, axis=0, keepdims=True)
        dy_ref[...] = _rms_bwd(dn * g_ref[...], yv, r).astype(bf16)

    return pl.pallas_call(
        body, name=name, grid=(s // tm,), in_specs=[_row_spec(tm, d)] * 2 + [_vec_spec(d)] * 2,
        out_specs=[_row_spec(tm, d), _vec_spec(d, 8)],
        out_shape=[jax.ShapeDtypeStruct((s, d), bf16), jax.ShapeDtypeStruct((8, d), f32)], compiler_params=_params(1),
    )(dres, y, g, gate)


def _prenorm_bwd(dh, x, g, scale, dres, name):
    s, d = x.shape
    tm = _row_tile(s, 512)

    def body(dh_ref, x_ref, g_ref, sc_ref, dr_ref, dx_ref, vec_ref):
        @pl.when(pl.program_id(0) == 0)
        def _():
            vec_ref[...] = jnp.zeros_like(vec_ref)
        dhv, xv = dh_ref[...], x_ref[...]
        r = _rstd(xv)
        xn = xv * r
        dn = dhv * (1.0 + sc_ref[...])
        vec_ref[0:1, :] += jnp.sum(dhv, axis=0, keepdims=True)
        vec_ref[1:2, :] += jnp.sum(dhv * (xn * g_ref[...]), axis=0, keepdims=True)
        vec_ref[2:3, :] += jnp.sum(dn * xn, axis=0, keepdims=True)
        dx_ref[...] = dr_ref[...] + _rms_bwd(dn * g_ref[...], xv, r)

    return pl.pallas_call(
        body, name=name, grid=(s // tm,),
        in_specs=[_row_spec(tm, d)] * 2 + [_vec_spec(d)] * 2 + [_row_spec(tm, d)],
        out_specs=[_row_spec(tm, d), _vec_spec(d, 8)],
        out_shape=[jax.ShapeDtypeStruct((s, d), f32), jax.ShapeDtypeStruct((8, d), f32)], compiler_params=_params(1),
    )(dh, x, g, scale, dres)


def _lane():
    return lax.broadcasted_iota(jnp.int32, (1, LANES), 1)


def _rope_tables(pos_col, inv_freq, name):
    s = pos_col.shape[0]

    def body(p_ref, f_ref, cos_ref, sin_ref):
        ang = p_ref[...].astype(f32) * f_ref[...]
        first_half = (_lane() % HEAD_DIM) < HEAD_DIM // 2
        cos_ref[...] = jnp.cos(ang)
        sn = jnp.sin(ang)
        sin_ref[...] = jnp.where(first_half, -sn, sn)

    return pl.pallas_call(
        body, name=name, out_shape=[jax.ShapeDtypeStruct((s, LANES), f32)] * 2, compiler_params=_params(),
    )(pos_col, inv_freq)


def _swap_halves(v):
    first_half = (_lane() % HEAD_DIM) < HEAD_DIM // 2
    return jnp.where(first_half, pltpu.roll(v, LANES - HEAD_DIM // 2, axis=1), pltpu.roll(v, HEAD_DIM // 2, axis=1))


def _qkv_prep(qkv, cos, sin_s, name):
    s = qkv.shape[0]
    tm = _row_tile(s, 256)
    scale = 1.0 / math.sqrt(HEAD_DIM)

    def body(p_ref, c_ref, s_ref, qa_ref, ka_ref, va_ref, qb_ref, kb_ref, vb_ref):
        cs, sn = c_ref[...], s_ref[...]
        low = _lane() < HEAD_DIM

        def blk(j):
            return p_ref[:, j * LANES:(j + 1) * LANES]

        def rope(v):
            return v * cs + _swap_halves(v) * sn

        def expand(v):
            other = pltpu.roll(v, HEAD_DIM, axis=1)
            return jnp.where(low, v, other), jnp.where(low, other, v)

        for j in range(N_PAIRS):
            qa_ref[:, j * LANES:(j + 1) * LANES] = (rope(blk(j)) * scale).astype(bf16)
            qb_ref[:, j * LANES:(j + 1) * LANES] = (blk(6 + j) * scale).astype(bf16)
            kb_ref[:, j * LANES:(j + 1) * LANES] = blk(10 + j).astype(bf16)
            vb_ref[:, j * LANES:(j + 1) * LANES] = blk(14 + j).astype(bf16)
        k0, k1 = expand(rope(blk(4)))
        v0, v1 = expand(blk(5))
        for j in range(N_PAIRS):
            ka_ref[:, j * LANES:(j + 1) * LANES] = (k0 if j < 2 else k1).astype(bf16)
            va_ref[:, j * LANES:(j + 1) * LANES] = (v0 if j < 2 else v1).astype(bf16)

    hw = N_PAIRS * LANES
    return pl.pallas_call(
        body, name=name, grid=(s // tm,),
        in_specs=[_row_spec(tm, QKV_W), _row_spec(tm, LANES), _row_spec(tm, LANES)],
        out_specs=[_row_spec(tm, hw)] * 6, out_shape=[jax.ShapeDtypeStruct((s, hw), bf16)] * 6, compiler_params=_params(1),
    )(qkv, cos, sin_s)


def _qkv_prep_bwd(dqa_t, dka, dva, dqb_t, dkb, dvb, cos, sin_s, name):
    s = dka.shape[0]
    tm = _row_tile(s, 256)
    scale = 1.0 / math.sqrt(HEAD_DIM)
    hw = N_PAIRS * LANES
    t_spec = pl.BlockSpec((hw, tm), lambda i: (0, i))

    def body(dqa_ref, dka_ref, dva_ref, dqb_ref, dkb_ref, dvb_ref, c_ref, s_ref, o_ref):
        cs, sn = c_ref[...], s_ref[...]
        low = _lane() < HEAD_DIM

        def blk(ref, j):
            return ref[:, j * LANES:(j + 1) * LANES]

        def blk_t(ref, j):
            return ref[j * LANES:(j + 1) * LANES, :].T

        def unrope(v):
            return v * cs + _swap_halves(v * sn)

        def fold(ref):
            a, b = blk(ref, 0) + blk(ref, 1), blk(ref, 2) + blk(ref, 3)
            kv0 = a + pltpu.roll(a, HEAD_DIM, axis=1)
            kv1 = b + pltpu.roll(b, HEAD_DIM, axis=1)
            return jnp.where(low, kv0, kv1)

        for j in range(N_PAIRS):
            o_ref[:, j * LANES:(j + 1) * LANES] = (unrope(blk_t(dqa_ref, j)) * scale).astype(bf16)
            o_ref[:, (6 + j) * LANES:(7 + j) * LANES] = (blk_t(dqb_ref, j) * scale).astype(bf16)
            o_ref[:, (10 + j) * LANES:(11 + j) * LANES] = blk(dkb_ref, j).astype(bf16)
            o_ref[:, (14 + j) * LANES:(15 + j) * LANES] = blk(dvb_ref, j).astype(bf16)
        o_ref[:, 4 * LANES:5 * LANES] = unrope(fold(dka_ref)).astype(bf16)
        o_ref[:, 5 * LANES:6 * LANES] = fold(dva_ref).astype(bf16)

    return pl.pallas_call(
        body, name=name, grid=(s // tm,),
        in_specs=[t_spec, _row_spec(tm, hw), _row_spec(tm, hw), t_spec, _row_spec(tm, hw), _row_spec(tm, hw)] + [_row_spec(tm, LANES)] * 2,
        out_specs=_row_spec(tm, QKV_W), out_shape=jax.ShapeDtypeStruct((s, QKV_W), bf16), compiler_params=_params(1),
    )(dqa_t, dka, dva, dqb_t, dkb, dvb, cos, sin_s)


def _cumsum_rows(v, reverse=False):
    n = v.shape[0]
    row = lax.broadcasted_iota(jnp.int32, v.shape, 0)
    sh = 1
    while sh < n:
        if reverse:
            v = v + jnp.where(row < n - sh, pltpu.roll(v, n - sh, axis=0), 0.0)
        else:
            v = v + jnp.where(row >= sh, pltpu.roll(v, sh, axis=0), 0.0)
        sh *= 2
    return v


def _log_sigmoid(z):
    return jnp.minimum(z, 0.0) - jnp.log1p(jnp.exp(-jnp.abs(z)))


def _forget_prep(fl, bf_row, name):
    s = fl.shape[0]

    def body(f_ref, b_ref, cb_ref):
        cum = _cumsum_rows(_log_sigmoid(f_ref[...] + b_ref[...]))
        for h in range(N_HEADS):
            cb_ref[:, h * LANES:(h + 1) * LANES] = jnp.broadcast_to(cum[:, h:h + 1], (s, LANES))

    return pl.pallas_call(
        body, name=name, out_shape=jax.ShapeDtypeStruct((s, N_HEADS * LANES), f32), compiler_params=_params(),
    )(fl, bf_row)


def _forget_prep_bwd(rs, dcs, fl, bf_row, name):
    s = fl.shape[0]

    def body(r_ref, c_ref, f_ref, b_ref, df_ref, db_ref):
        eye = (lax.broadcasted_iota(jnp.int32, (N_HEADS, LANES), 0) == lax.broadcasted_iota(jnp.int32, (N_HEADS, LANES), 1)).astype(f32)
        dcum = lax.dot_general(r_ref[...], eye, _TN, precision=lax.Precision.HIGHEST, preferred_element_type=f32)
        for h in range(N_HEADS):
            dcum = dcum - jnp.where(_lane() == h, jnp.sum(c_ref[:, h * LANES:(h + 1) * LANES], axis=1, keepdims=True), 0.0)
        dlf = _cumsum_rows(dcum, reverse=True)
        z = f_ref[...] + b_ref[...]
        df = jnp.where(_lane() < N_HEADS, dlf * jax.nn.sigmoid(-z), 0.0)
        df_ref[...] = df.astype(bf16)
        db_ref[...] = jnp.zeros_like(db_ref)
        db_ref[0:1, :] = jnp.sum(df, axis=0, keepdims=True)

    return pl.pallas_call(
        body, name=name,
        out_shape=[jax.ShapeDtypeStruct((s, LANES), bf16), jax.ShapeDtypeStruct((8, LANES), f32)], compiler_params=_params(),
    )(rs, dcs, fl, bf_row)


def _tile_mask(t, off, window):
    d = lax.broadcasted_iota(jnp.int32, (t, t), 1) - lax.broadcasted_iota(jnp.int32, (t, t), 0) + off
    valid = d >= 0
    return jnp.logical_and(valid, d < window) if window else valid


def _wide(v, t):
    return jnp.concatenate([v] * (t // LANES), axis=1)


def _attn_fwd(q, k, v, name, *, cum_b=None, sink_rows=None, window=None, t=256):
    s = q.shape[0]
    t = _row_tile(s, t)
    fox, has_sink = cum_b is not None, sink_rows is not None
    assert not window or window <= t

    def body(*refs):
        q_ref, k_ref, v_ref = refs[:3]
        rest = list(refs[3:])
        cb_ref = rest.pop(0) if fox else None
        sink_ref = rest.pop(0) if has_sink else None
        o_ref, lse_ref = rest
        i = pl.program_id(1)
        low = _lane() < HEAD_DIM
        q2 = q_ref[...]
        zero = jnp.zeros_like(q2)
        qms = (jnp.where(low, q2, zero), jnp.where(low, zero, q2))

        def tile(kb, carry, masked):
            k0 = pl.multiple_of(kb * t, t)
            kblk, vblk = k_ref[pl.ds(k0, t), :], v_ref[pl.ds(k0, t), :]
            valid = _tile_mask(t, (i - kb) * t, window) if masked else None
            out = []
            for h in range(2):
                m, l, acc = carry[h]
                sc = lax.dot_general(kblk, qms[h], _NT, preferred_element_type=f32)
                if fox:
                    sc = sc - _wide(cb_ref[pl.ds(k0, t), h * LANES:(h + 1) * LANES], t)
                if masked:
                    sc = jnp.where(valid, sc, NEG)
                m_new = jnp.maximum(m, jnp.max(sc, axis=0, keepdims=True))
                p = jnp.exp(sc - m_new)
                alpha = jnp.exp(m - m_new)
                l = alpha * l + jnp.sum(p, axis=0, keepdims=True)
                acc = alpha * acc + lax.dot_general(vblk, p.astype(bf16), _TN, preferred_element_type=f32)
                out.append((m_new, l, acc))
            return tuple(out)

        init = []
        for h in range(2):
            if has_sink:
                init.append((_wide(sink_ref[h:h + 1, :], t), jnp.ones((1, t), f32), jnp.zeros((LANES, t), f32)))
            else:
                init.append((jnp.full((1, t), NEG, f32), jnp.zeros((1, t), f32), jnp.zeros((LANES, t), f32)))
        carry = tuple(init)
        if window:
            carry = lax.fori_loop(jnp.maximum(i - 1, 0), i + 1, functools.partial(tile, masked=True), carry)
        else:
            carry = lax.fori_loop(0, i, functools.partial(tile, masked=False), carry)
            carry = tile(i, carry, True)
        (m0, l0, a0), (m1, l1, a1) = carry
        top = lax.broadcasted_iota(jnp.int32, (LANES, 1), 0) < HEAD_DIM
        o_t = jnp.where(top, a0 * (1.0 / l0), a1 * (1.0 / l1))
        o_ref[...] = o_t.T.astype(bf16)
        lse_ref[0:1, :] = m0 + jnp.log(l0)
        lse_ref[1:2, :] = m1 + jnp.log(l1)

    q_spec = pl.BlockSpec((t, LANES), lambda j, i: (i, j))
    kv_spec = pl.BlockSpec((s, LANES), lambda j, i: (0, j))
    in_specs, args = [q_spec, kv_spec, kv_spec], [q, k, v]
    if fox:
        in_specs += [pl.BlockSpec((s, 2 * LANES), lambda j, i: (0, j))]
        args += [cum_b]
    if has_sink:
        in_specs += [pl.BlockSpec((None, 2, LANES), lambda j, i: (j, 0, 0))]
        args += [sink_rows.reshape(N_PAIRS, 2, LANES)]
    return pl.pallas_call(
        body, name=name, grid=(N_PAIRS, s // t), in_specs=in_specs,
        out_specs=[q_spec, pl.BlockSpec((None, 2, t), lambda j, i: (j, 0, i))],
        out_shape=[jax.ShapeDtypeStruct((s, N_PAIRS * LANES), bf16), jax.ShapeDtypeStruct((N_PAIRS, 2, s), f32)],
        compiler_params=_params(2),
    )(*args)


def _attn_delta(do, o, name, *, lse=None, sink_rows=None):
    s, hw = do.shape
    tm = _row_tile(s, 512)
    has_sink = sink_rows is not None

    def body(*refs):
        do_ref, o_ref = refs[:2]
        if has_sink:
            lse_ref, sink_ref, dl_ref, ds_ref = refs[2:]

            @pl.when(pl.program_id(0) == 0)
            def _():
                ds_ref[...] = jnp.zeros_like(ds_ref)
        else:
            dl_ref, = refs[2:]
        for j in range(N_PAIRS):
            cols = slice(j * LANES, (j + 1) * LANES)
            prod_t = (do_ref[:, cols].astype(f32) * o_ref[:, cols].astype(f32)).T
            for h in range(2):
                dl = jnp.sum(prod_t[h * HEAD_DIM:(h + 1) * HEAD_DIM, :], axis=0, keepdims=True)
                dl_ref[j, h:h + 1, :] = dl
                if has_sink:
                    r = 2 * j + h
                    p_sink = jnp.exp(sink_ref[r:r + 1, 0:1] - lse_ref[j, h:h + 1, :])
                    ds_ref[r:r + 1, :] += -jnp.sum(p_sink * dl, axis=1, keepdims=True)

    rows_spec = pl.BlockSpec((N_PAIRS, 2, tm), lambda i: (0, 0, i))
    in_specs, args = [_row_spec(tm, hw)] * 2, [do, o]
    out_specs, out_shape = [rows_spec], [jax.ShapeDtypeStruct((N_PAIRS, 2, s), f32)]
    if has_sink:
        in_specs += [rows_spec, _vec_spec(LANES, N_HEADS)]
        args += [lse, sink_rows]
        out_specs += [_vec_spec(LANES, N_HEADS)]
        out_shape += [jax.ShapeDtypeStruct((N_HEADS, LANES), f32)]
    return pl.pallas_call(
        body, name=name, grid=(s // tm,), in_specs=in_specs, out_specs=out_specs, out_shape=out_shape,
        compiler_params=_params(1),
    )(*args)


def _attn_bwd(q, k, v, do, lse, delta, name, *, cum_b=None, window=None, t=256):
    s = q.shape[0]
    t = _row_tile(s, t)
    nblk = s // t
    fox = cum_b is not None
    assert not window or window <= t

    def body(*refs):
        k_ref, v_ref, q_ref, do_ref, lse_ref, dl_ref = refs[:6]
        rest = list(refs[6:])
        cb_ref = rest.pop(0) if fox else None
        dq_ref, dk_ref, dv_ref = rest[:3]
        dcs_ref, rs_ref = (rest[3], rest[4]) if fox else (None, None)
        b = pl.program_id(1)
        k0 = pl.multiple_of(b * t, t)

        @pl.when(b == 0)
        def _():
            dq_ref[...] = jnp.zeros_like(dq_ref)
            if fox:
                rs_ref[...] = jnp.zeros_like(rs_ref)

        dk_ref[...] = jnp.zeros_like(dk_ref)
        dv_ref[...] = jnp.zeros_like(dv_ref)
        if fox:
            dcs_ref[...] = jnp.zeros_like(dcs_ref)
        low = _lane() < HEAD_DIM
        top = lax.broadcasted_iota(jnp.int32, (LANES, 1), 0) < HEAD_DIM
        kblk, vblk = k_ref[...], v_ref[...]
        k_t = kblk.astype(f32).T.astype(bf16)
        cks = [_wide(cb_ref[pl.ds(k0, t), h * LANES:(h + 1) * LANES], t) for h in range(2)] if fox else None

        def tile(qb, carry, masked):
            q0 = pl.multiple_of(qb * t, t)
            cols = pl.ds(q0, t)
            q2, do2 = q_ref[cols, :], do_ref[cols, :]
            zero = jnp.zeros_like(q2)
            valid = _tile_mask(t, (qb - b) * t, window) if masked else None
            dq_parts = []
            for h in range(2):
                qm = jnp.where(low, q2, zero) if h == 0 else jnp.where(low, zero, q2)
                dom = jnp.where(low, do2, zero) if h == 0 else jnp.where(low, zero, do2)
                sc = lax.dot_general(kblk, qm, _NT, preferred_element_type=f32)
                if fox:
                    sc = sc - cks[h]
                if masked:
                    sc = jnp.where(valid, sc, NEG)
                p = jnp.exp(sc - lse_ref[h:h + 1, cols])
                dp = lax.dot_general(vblk, dom, _NT, preferred_element_type=f32)
                ds = p * (dp - dl_ref[h:h + 1, cols])
                pb, dsb = p.astype(bf16), ds.astype(bf16)
                dv_ref[...] += jnp.dot(pb, dom, preferred_element_type=f32)
                dk_ref[...] += jnp.dot(dsb, qm, preferred_element_type=f32)
                dq_parts.append(jnp.dot(k_t, dsb, preferred_element_type=f32))
                if fox:
                    dcs_ref[:, h * LANES:(h + 1) * LANES] += sum(ds[:, g * LANES:(g + 1) * LANES] for g in range(t // LANES))
                    rs_ref[h:h + 1, cols] += jnp.sum(ds, axis=0, keepdims=True)
            dq_ref[:, cols] += jnp.where(top, dq_parts[0], dq_parts[1])
            return carry

        if window:
            lax.fori_loop(b, jnp.minimum(b + 1, nblk - 1) + 1, functools.partial(tile, masked=True), 0)
        else:
            tile(b, 0, True)
            lax.fori_loop(b + 1, nblk, functools.partial(tile, masked=False), 0)

    kv_spec = pl.BlockSpec((t, LANES), lambda j, b: (b, j))
    seq_spec = pl.BlockSpec((s, LANES), lambda j, b: (0, j))
    rows_spec = pl.BlockSpec((None, 2, s), lambda j, b: (j, 0, 0))
    hw = N_PAIRS * LANES
    in_specs, args = [kv_spec, kv_spec, seq_spec, seq_spec, rows_spec, rows_spec], [k, v, q, do, lse, delta]
    out_specs = [pl.BlockSpec((LANES, s), lambda j, b: (j, 0)), kv_spec, kv_spec]
    out_shape = [jax.ShapeDtypeStruct((hw, s), f32), jax.ShapeDtypeStruct((s, hw), f32), jax.ShapeDtypeStruct((s, hw), f32)]
    if fox:
        in_specs += [pl.BlockSpec((s, 2 * LANES), lambda j, b: (0, j))]
        args += [cum_b]
        out_specs += [pl.BlockSpec((t, 2 * LANES), lambda j, b: (b, j)), rows_spec]
        out_shape += [jax.ShapeDtypeStruct((s, N_HEADS * LANES), f32), jax.ShapeDtypeStruct((N_PAIRS, 2, s), f32)]
    return pl.pallas_call(
        body, name=name, grid=(N_PAIRS, nblk), in_specs=in_specs, out_specs=out_specs, out_shape=out_shape,
        compiler_params=_params(2),
    )(*args)


def _merge(ba, bb, gl, name):
    s, d = ba.shape
    tm = _row_tile(s, 512)

    def body(a_ref, b_ref, g_ref, o_ref):
        o_ref[...] = (jax.nn.sigmoid(g_ref[:, :d]) * a_ref[...] + jax.nn.sigmoid(g_ref[:, d:]) * b_ref[...]).astype(bf16)

    return pl.pallas_call(
        body, name=name, grid=(s // tm,), in_specs=[_row_spec(tm, d)] * 2 + [_row_spec(tm, 2 * d)],
        out_specs=_row_spec(tm, d), out_shape=jax.ShapeDtypeStruct((s, d), bf16), compiler_params=_params(1),
    )(ba, bb, gl)


def _merge_bwd(dm, ba, bb, gl, name):
    s, d = ba.shape
    tm = _row_tile(s, 512)

    def body(dm_ref, a_ref, b_ref, g_ref, da_ref, db_ref, dg_ref):
        dmv = dm_ref[...]
        g0, g1 = jax.nn.sigmoid(g_ref[:, :d]), jax.nn.sigmoid(g_ref[:, d:])
        da_ref[...] = (dmv * g0).astype(bf16)
        db_ref[...] = (dmv * g1).astype(bf16)
        dg_ref[:, :d] = (dmv * a_ref[...] * (g0 * (1.0 - g0))).astype(bf16)
        dg_ref[:, d:] = (dmv * b_ref[...] * (g1 * (1.0 - g1))).astype(bf16)

    return pl.pallas_call(
        body, name=name, grid=(s // tm,), in_specs=[_row_spec(tm, d)] * 3 + [_row_spec(tm, 2 * d)],
        out_specs=[_row_spec(tm, d)] * 2 + [_row_spec(tm, 2 * d)],
        out_shape=[jax.ShapeDtypeStruct((s, d), bf16)] * 2 + [jax.ShapeDtypeStruct((s, 2 * d), bf16)],
        compiler_params=_params(1),
    )(dm, ba, bb, gl)


def _swiglu(gu, name):
    s, w = gu.shape
    h = w // 2
    tm = _row_tile(s, 256)

    def body(g_ref, o_ref):
        g = g_ref[:, :h]
        o_ref[...] = (g * jax.nn.sigmoid(g) * g_ref[:, h:]).astype(bf16)

    return pl.pallas_call(
        body, name=name, grid=(s // tm,), in_specs=[_row_spec(tm, w)], out_specs=_row_spec(tm, h),
        out_shape=jax.ShapeDtypeStruct((s, h), bf16), compiler_params=_params(1),
    )(gu)


def _swiglu_bwd(dact, gu, name):
    s, w = gu.shape
    h = w // 2
    tm = _row_tile(s, 256)

    def body(d_ref, g_ref, o_ref):
        dv, g, u = d_ref[...], g_ref[:, :h], g_ref[:, h:]
        sg = jax.nn.sigmoid(g)
        o_ref[:, :h] = (dv * u * (sg * (1.0 + g * (1.0 - sg)))).astype(bf16)
        o_ref[:, h:] = (dv * (g * sg)).astype(bf16)

    return pl.pallas_call(
        body, name=name, grid=(s // tm,), in_specs=[_row_spec(tm, h), _row_spec(tm, w)], out_specs=_row_spec(tm, w),
        out_shape=jax.ShapeDtypeStruct((s, w), bf16), compiler_params=_params(1),
    )(dact, gu)


def _ada_fwd(c_all, w, b, name):
    def body(c_ref, w_ref, b_ref, o_ref):
        o_ref[...] = jnp.dot(c_ref[...].astype(bf16), w_ref[...].astype(bf16), preferred_element_type=f32) + b_ref[...]

    return pl.pallas_call(
        body, name=name, out_shape=jax.ShapeDtypeStruct((c_all.shape[0], w.shape[1]), f32), compiler_params=_params(),
    )(c_all, w, b)


def _ada_wgrad(c_all, d_all, name):
    n, d = c_all.shape
    w = d_all.shape[1]

    def body(c_ref, d_ref, o_ref):
        eye = (lax.broadcasted_iota(jnp.int32, (n, n), 0) == lax.broadcasted_iota(jnp.int32, (n, n), 1)).astype(f32)
        ct = lax.dot_general(c_ref[...], eye, _TN, precision=lax.Precision.HIGHEST, preferred_element_type=f32)
        g = ct[:, 0:1] * d_ref[0:1, :]
        for bi in range(1, n):
            g = g + ct[:, bi:bi + 1] * d_ref[bi:bi + 1, :]
        o_ref[0] = g

    return pl.pallas_call(
        body, name=name, out_shape=jax.ShapeDtypeStruct((1, d, w), f32), compiler_params=_params(),
    )(c_all, d_all)


def _adamw(parts, w, m, v, name):
    r, c = w.shape
    n_parts = parts.shape[0]
    tr = next(t for t in range(min(r, 256), 0, -1) if r % t == 0 and (t % 16 == 0 or t == r))

    def body(p_ref, w_ref, m_ref, v_ref, g_ref, d_ref, nm_ref, nv_ref):
        g = p_ref[0].astype(f32)
        for i in range(1, n_parts):
            g = g + p_ref[i].astype(f32)
        mm = ADAM_B1 * m_ref[...] + (1.0 - ADAM_B1) * g
        vv = ADAM_B2 * v_ref[...] + (1.0 - ADAM_B2) * (g * g)
        m_hat = mm / (1.0 - ADAM_B1 ** ADAM_STEP)
        v_hat = vv / (1.0 - ADAM_B2 ** ADAM_STEP)
        g_ref[...] = g
        d_ref[...] = -ADAM_LR * (m_hat / (jnp.sqrt(v_hat) + ADAM_EPS) + ADAM_WD * w_ref[...])
        nm_ref[...] = mm
        nv_ref[...] = vv

    spec = pl.BlockSpec((tr, c), lambda i: (i, 0))
    return pl.pallas_call(
        body, name=name, grid=(r // tr,), in_specs=[pl.BlockSpec((n_parts, tr, c), lambda i: (0, i, 0))] + [spec] * 3,
        out_specs=[spec] * 4, out_shape=[jax.ShapeDtypeStruct((r, c), f32)] * 4, compiler_params=_params(1),
    )(parts, w, m, v)


def _me():
    return lax.axis_index("x"), lax.axis_index("y"), lax.axis_index("c")


def _all_gather(arrays, name, vmem=False):
    n = len(arrays)
    space = pltpu.VMEM if vmem else pl.ANY

    def body(*refs):
        ins, outs = refs[:n], refs[n:2 * n]
        send_sems, recv_sems, local_sems = refs[2 * n:]
        x, y, c = _me()
        me, sibling = (x, y, c), (x, y, 1 - c)
        chips = [(1 - x, y), (x, 1 - y), (1 - x, 1 - y)]

        def rows(a, dev):
            return outs[a].at[4 * dev[0] + 2 * dev[1] + dev[2]]

        def copy(a, k, block, to, src=None):
            return pltpu.make_async_remote_copy(
                src_ref=rows(a, block) if src is None else src, dst_ref=rows(a, block),
                send_sem=send_sems.at[a, k], recv_sem=recv_sems.at[a, k], device_id=to, device_id_type=MESH)

        mine = [pltpu.make_async_copy(ins[a], rows(a, me), local_sems.at[a]) for a in range(n)]
        for cp in mine:
            cp.start()
        first = []
        for a in range(n):
            first.append(copy(a, 0, me, sibling, src=ins[a]))
            first += [copy(a, 1 + j, me, (*chip, c), src=ins[a]) for j, chip in enumerate(chips)]
        for cp in first:
            cp.start()
        passed = []
        for j, chip in enumerate(chips):
            for a in range(n):
                copy(a, 1 + j, (*chip, c), me).wait_recv()
                fwd = copy(a, 4 + j, (*chip, c), sibling)
                fwd.start()
                passed.append(fwd)
        for a in range(n):
            copy(a, 0, sibling, me).wait_recv()
            for j, chip in enumerate(chips):
                copy(a, 4 + j, (*chip, 1 - c), me).wait_recv()
        for cp in first + passed:
            cp.wait_send()
        for cp in mine:
            cp.wait()

    outs = pl.pallas_call(
        body, name=name,
        in_specs=[pl.BlockSpec(memory_space=space)] * n, out_specs=[pl.BlockSpec(memory_space=space)] * n,
        out_shape=[jax.ShapeDtypeStruct((N_DEV,) + a.shape, a.dtype) for a in arrays],
        scratch_shapes=[pltpu.SemaphoreType.DMA((n, 7)), pltpu.SemaphoreType.DMA((n, 7)), pltpu.SemaphoreType.DMA((n,))],
        compiler_params=pltpu.CompilerParams(vmem_limit_bytes=VMEM_LIMIT),
    )(*arrays)
    return list(outs)


_FLIPS = ((0, 0, 1), (1, 0, 0), (0, 1, 0), (1, 1, 0), (1, 0, 1), (0, 1, 1), (1, 1, 1))
_HBM = pl.BlockSpec(memory_space=pltpu.HBM)
_SEM = pl.BlockSpec(memory_space=pltpu.SEMAPHORE)


def _exchange_copies(scatter, srcs, lands, send_sems, recv_sems):
    x, y, c = _me()
    me_row = 4 * x + 2 * y + c
    out = []
    for k, (fx, fy, fc) in enumerate(_FLIPS):
        peer = (x ^ fx, y ^ fy, c ^ fc)
        peer_row = 4 * peer[0] + 2 * peer[1] + peer[2]
        for a in range(len(srcs)):
            out.append(pltpu.make_async_remote_copy(
                src_ref=srcs[a].at[peer_row] if scatter else srcs[a], dst_ref=lands[a].at[me_row],
                send_sem=send_sems.at[7 * a + k], recv_sem=recv_sems.at[7 * a + k], device_id=peer, device_id_type=MESH))
    return out


def _exchange_start(arrays, scatter, name):
    n = len(arrays)
    lands = [lax.empty(a.shape if scatter else (N_DEV,) + a.shape, a.dtype) for a in arrays]

    def body(*refs):
        srcs, zones = refs[:n], refs[n:2 * n]
        send_sems, recv_sems = refs[2 * n], refs[2 * n + 1]
        token = refs[-1]
        for cp in _exchange_copies(scatter, srcs, zones, send_sems, recv_sems):
            cp.start()
        token[...] = jnp.zeros_like(token)

    thru = [pltpu.HBM(a.shape, a.dtype) for a in list(arrays) + lands]
    outs = pl.pallas_call(
        body, name=name,
        out_shape=(pltpu.SemaphoreType.DMA((7 * n,)), pltpu.SemaphoreType.DMA((7 * n,)), *thru, jax.ShapeDtypeStruct((8, LANES), f32)),
        in_specs=[_HBM] * (2 * n), out_specs=(_SEM, _SEM, *[_HBM] * (2 * n), pl.BlockSpec(memory_space=pltpu.VMEM)),
        input_output_aliases={i: 2 + i for i in range(2 * n)},
        compiler_params=pltpu.CompilerParams(has_side_effects=pltpu.SideEffectType.DATAFLOW_SIDE_EFFECTING),
    )(*[pltpu.with_memory_space_constraint(a, pltpu.HBM) for a in list(arrays) + lands])
    return dict(n=n, scatter=scatter, sems=outs[:2], srcs=outs[2:2 + n], lands=outs[2 + n:2 + 2 * n], token=outs[-1][0, 0])


def _exchange_wait(handle, after, name):
    n, scatter = handle["n"], handle["scatter"]

    def body(*refs):
        srcs, zones = refs[:n], refs[n:2 * n]
        send_sems, recv_sems = refs[2 * n], refs[2 * n + 1]
        for cp in _exchange_copies(scatter, srcs, zones, send_sems, recv_sems):
            cp.wait_send()
            cp.wait_recv()

    thru = [pltpu.HBM(a.shape, a.dtype) for a in list(handle["srcs"]) + list(handle["lands"])]
    outs = pl.pallas_call(
        body, name=name, out_shape=tuple(thru),
        in_specs=[_HBM] * (2 * n) + [_SEM, _SEM, pl.BlockSpec(memory_space=pl.ANY)], out_specs=tuple([_HBM] * (2 * n)),
        input_output_aliases={i: i for i in range(2 * n)},
        compiler_params=pltpu.CompilerParams(has_side_effects=pltpu.SideEffectType.DATAFLOW_SIDE_EFFECTING),
    )(*handle["srcs"], *handle["lands"], *handle["sems"], after)
    return list(outs[n:])


def _cols_from_shards(g):
    return jnp.transpose(g, (1, 0, 2)).reshape(g.shape[1], -1)


def _shards_from_cols(a):
    return jnp.transpose(a.reshape(a.shape[0], N_DEV, -1), (1, 0, 2))


def _local_step(x, positions, ada, g_pre_mix, g_post_mix, b_f, sinks, g_pre_ffn, g_post_ffn, target,
                w_in, late_weights, on_grads):
    s, d = x.shape
    row = lambda v: v.reshape(1, -1)
    shift_m, scale_m, gate_m, shift_f, scale_f, gate_f = (ada[i:i + 1] for i in range(6))
    w_gate, w_qkv = w_in[:, F_OFF + N_HEADS:], w_in[:, :QKV_W]
    w_f = jnp.pad(w_in[:, F_OFF:F_OFF + N_HEADS], ((0, 0), (0, LANES - N_HEADS)))
    w_in_p = jnp.concatenate([w_gate, w_qkv, w_f], axis=1)
    bf_row = jnp.pad(row(b_f), ((0, 0), (0, LANES - N_HEADS)))
    sink_rows = jnp.broadcast_to(sinks.reshape(N_HEADS, 1).astype(f32), (N_HEADS, LANES))
    inv_freq = 1.0 / (ROPE_THETA ** (jnp.arange(0, HEAD_DIM, 2, dtype=f32) / HEAD_DIM))
    cos, sin_s = _rope_tables(positions.reshape(s, 1), jnp.tile(inv_freq, 4).reshape(1, LANES), "rope_tables")

    h1 = _prenorm(x, row(g_pre_mix), scale_m, shift_m, "prenorm_mix")
    gl = _matmul(h1, w_gate, "nn", f32, "proj_gate")
    qkv = _matmul(h1, w_qkv, "nn", f32, "proj_qkv")
    fl = _matmul(h1, w_f, "nn", f32, "proj_forget")
    qa, ka, va, qb, kb, vb = _qkv_prep(qkv, cos, sin_s, "qkv_prep")
    cum_b = _forget_prep(fl, bf_row, "forget_prep")
    o_a, lse_a = _attn_fwd(qa, ka, va, "swa_fwd", sink_rows=sink_rows, window=WINDOW)
    o_b, lse_b = _attn_fwd(qb, kb, vb, "fox_fwd", cum_b=cum_b)
    w_branch_a, w_branch_b, w_out, w_ffn_in, w_ffn_out = late_weights(o_b)
    ba = _matmul(o_a, w_branch_a, "nn", f32, "branch_a")
    bb = _matmul(o_b, w_branch_b, "nn", f32, "branch_b")
    merged = _merge(ba, bb, gl, "merge")
    y1 = _matmul(merged, w_out, "nn", f32, "out_proj")
    x2 = _postnorm_res(x, y1, row(g_post_mix), gate_m, "postnorm_mix")

    h2 = _prenorm(x2, row(g_pre_ffn), scale_f, shift_f, "prenorm_ffn")
    gu = _matmul(h2, w_ffn_in, "nn", f32, "ffn_in")
    act = _swiglu(gu, "swiglu")
    y2 = _matmul(act, w_ffn_out, "nn", f32, "ffn_out")
    out = _postnorm_res(x2, y2, row(g_post_ffn), gate_f, "postnorm_ffn")
    loss_row, d_out = _loss_head(out, target, "loss_head")

    d_y2, vec_pf = _postnorm_bwd(d_out, y2, row(g_post_ffn), gate_f, "postnorm_ffn_bwd")
    g_w_ffn_out = _matmul(act, d_y2, "tn", bf16, "ffn_out_wgrad")
    d_act = _matmul(d_y2, w_ffn_out, "nt", f32, "ffn_out_dgrad")
    dgu = _swiglu_bwd(d_act, gu, "swiglu_bwd")
    g_w_ffn_in = _matmul(h2, dgu, "tn", bf16, "ffn_in_wgrad")
    sent = on_grads(dict(w_ffn_in=g_w_ffn_in, w_ffn_out=g_w_ffn_out))
    d_h2 = _matmul(dgu, w_ffn_in, "nt", f32, "ffn_in_dgrad")
    d_x2, vec_nf = _prenorm_bwd(d_h2, x2, row(g_pre_ffn), scale_f + sent, d_out, "prenorm_ffn_bwd")

    d_y1, vec_pm = _postnorm_bwd(d_x2, y1, row(g_post_mix), gate_m, "postnorm_mix_bwd")
    g_w_out = _matmul(merged, d_y1, "tn", bf16, "out_proj_wgrad")
    d_merged = _matmul(d_y1, w_out, "nt", f32, "out_proj_dgrad")
    d_ba, d_bb, dgl = _merge_bwd(d_merged, ba, bb, gl, "merge_bwd")
    g_w_branch_a = _matmul(o_a, d_ba, "tn", bf16, "branch_a_wgrad")
    g_w_branch_b = _matmul(o_b, d_bb, "tn", bf16, "branch_b_wgrad")
    sent = on_grads(dict(w_out=g_w_out, w_branch_a=g_w_branch_a, w_branch_b=g_w_branch_b))
    d_oa = _matmul(d_ba, w_branch_a, "nt", bf16, "branch_a_dgrad")
    d_ob = _matmul(d_bb, w_branch_b, "nt", bf16, "branch_b_dgrad")
    delta_a, d_sink = _attn_delta(d_oa, o_a, "swa_delta", lse=lse_a, sink_rows=sink_rows + sent)
    delta_b, = _attn_delta(d_ob, o_b, "fox_delta")
    dqa_t, dka, dva = _attn_bwd(qa, ka, va, d_oa, lse_a, delta_a, "swa_bwd", window=WINDOW)
    dqb_t, dkb, dvb, dcs, rs = _attn_bwd(qb, kb, vb, d_ob, lse_b, delta_b, "fox_bwd", cum_b=cum_b)
    dqkv = _qkv_prep_bwd(dqa_t, dka, dva, dqb_t, dkb, dvb, cos, sin_s, "qkv_prep_bwd")
    dfl, vec_bf = _forget_prep_bwd(rs.reshape(N_HEADS, s), dcs, fl, bf_row, "forget_prep_bwd")
    dproj = jnp.concatenate([dgl, dqkv, dfl], axis=1)
    g_w_in_p = _matmul(h1, dproj, "tn", bf16, "in_proj_wgrad")
    g_w_in = jnp.concatenate([g_w_in_p[:, GATE_W:GATE_W + QKV_W], g_w_in_p[:, GATE_W + QKV_W:GATE_W + QKV_W + N_HEADS],
                              g_w_in_p[:, :GATE_W]], axis=1)
    sent = on_grads(dict(w_in=g_w_in))
    d_h1 = _matmul(dproj, w_in_p, "nt", f32, "in_proj_dgrad")
    grad_x, vec_nm = _prenorm_bwd(d_h1, x, row(g_pre_mix), scale_m + sent, d_x2, "prenorm_mix_bwd")

    d_ada = jnp.concatenate([vec_nm[0], vec_nm[1], vec_pm[0], vec_nf[0], vec_nf[1], vec_pf[0]])
    small = dict(b_ada=d_ada, g_pre_mix=vec_nm[2], g_post_mix=vec_pm[1], g_pre_ffn=vec_nf[2], g_post_ffn=vec_pf[1],
                 b_f=vec_bf[0, :N_HEADS], sinks=d_sink[:, 0], loss=loss_row[0, :1])
    return grad_x, small


_SMALL = (("b_ada", 6144), ("g_pre_mix", 1024), ("g_post_mix", 1024), ("g_pre_ffn", 1024), ("g_post_ffn", 1024),
          ("b_f", 128), ("sinks", 128), ("loss", 128))
_SMALL_ROWS = 88


def _pack_small(vals):
    parts = [jnp.pad(vals[k].reshape(-1).astype(f32), (0, n - vals[k].size)) for k, n in _SMALL]
    flat = jnp.concatenate(parts)
    return jnp.pad(flat, (0, _SMALL_ROWS * LANES - flat.size)).reshape(_SMALL_ROWS, LANES)


def _unpack_small(slab, shapes):
    flat, out, off = slab.reshape(-1), {}, 0
    for k, n in _SMALL:
        size = math.prod(shapes[k])
        out[k] = flat[off:off + size].reshape(shapes[k])
        off += n
    return out


def kernel(x, c, positions, w_ada, b_ada, g_pre_mix, g_post_mix, w_in, b_f, sinks, w_branch_a, w_branch_b, w_out, g_pre_ffn, g_post_ffn, w_ffn_in, w_ffn_out, loss_target, m_w_ada, m_b_ada, m_g_pre_mix, m_g_post_mix, m_w_in, m_b_f, m_sinks, m_w_branch_a, m_w_branch_b, m_w_out, m_g_pre_ffn, m_g_post_ffn, m_w_ffn_in, m_w_ffn_out, v_w_ada, v_b_ada, v_g_pre_mix, v_g_post_mix, v_w_in, v_b_f, v_sinks, v_w_branch_a, v_w_branch_b, v_w_out, v_g_pre_ffn, v_g_post_ffn, v_w_ffn_in, v_w_ffn_out):
    xi, yi, ci = _me()
    me = 4 * xi + 2 * yi + ci
    d = D_MODEL
    ada_w = w_ada.shape[2]

    c_all, = _all_gather([c], "gather_c", vmem=True)
    c_all = c_all.reshape(N_DEV, d)
    b_mine = lax.dynamic_slice(b_ada, (0, me * ada_w), (1, ada_w))
    ada_cols = _ada_fwd(c_all, w_ada[0], b_mine, "ada_fwd")
    ada_all, = _all_gather([ada_cols], "gather_ada", vmem=True)
    ada = lax.dynamic_index_in_dim(ada_all, me, axis=1, keepdims=False).reshape(6, d)

    g_in, = _all_gather([w_in[0].astype(bf16)], "gather_w_in")
    late = [w.astype(bf16) for w in (w_branch_a[0], w_branch_b[0], w_out[0], w_ffn_in[0], w_ffn_out[0])]
    late_h = _exchange_start(late, False, "gather_late_start")

    def mine_into(zone, block):
        return lax.dynamic_update_index_in_dim(zone, block, me, 0)

    def late_weights(after):
        zones = _exchange_wait(late_h, after, "gather_late_wait")
        g_ba, g_bb, g_out, g_fi, g_fo = (mine_into(z, w) for z, w in zip(zones, late))
        return (_cols_from_shards(g_ba), _cols_from_shards(g_bb), g_out.reshape(d, d), _cols_from_shards(g_fi),
                g_fo.reshape(D_FF, d))

    row_sharded = ("w_out", "w_ffn_out")
    in_flight = []

    def on_grads(group):
        sends = [g.reshape(N_DEV, g.shape[0] // N_DEV, g.shape[1]) if nm in row_sharded else _shards_from_cols(g)
                 for nm, g in group.items()]
        handle = _exchange_start(sends, True, "scatter_start_%d" % len(in_flight))
        in_flight.append((list(group), sends, handle))
        return handle["token"]

    grad_x, small = _local_step(
        x[0], positions[0], ada + late_h["token"], g_pre_mix[0], g_post_mix[0], b_f[0], sinks[0], g_pre_ffn[0],
        g_post_ffn[0], loss_target[0], _cols_from_shards(g_in), late_weights, on_grads)

    slab_all, = _all_gather([_pack_small(small)], "gather_small", vmem=True)
    small_w = dict(b_ada=b_ada, g_pre_mix=g_pre_mix, g_post_mix=g_post_mix, g_pre_ffn=g_pre_ffn, g_post_ffn=g_post_ffn,
                   b_f=b_f, sinks=sinks, loss=jnp.zeros((1,), f32))
    small_m = dict(b_ada=m_b_ada, g_pre_mix=m_g_pre_mix, g_post_mix=m_g_post_mix, g_pre_ffn=m_g_pre_ffn,
                   g_post_ffn=m_g_post_ffn, b_f=m_b_f, sinks=m_sinks, loss=jnp.zeros((1,), f32))
    small_v = dict(b_ada=v_b_ada, g_pre_mix=v_g_pre_mix, g_post_mix=v_g_post_mix, g_pre_ffn=v_g_pre_ffn,
                   g_post_ffn=v_g_post_ffn, b_f=v_b_f, sinks=v_sinks, loss=jnp.ones((1,), f32))
    shapes = {k: small_w[k].shape for k, _ in _SMALL}
    s_out = _adamw(slab_all, _pack_small(small_w), _pack_small(small_m), _pack_small(small_v), "adamw_small")
    s_grad, s_delta, s_m, s_v = (_unpack_small(o, shapes) for o in s_out)

    d_ada_all = lax.dynamic_slice(slab_all[:, :6144 // LANES, :].reshape(N_DEV, 6144), (0, me * ada_w), (N_DEV, ada_w))
    ada_parts = _ada_wgrad(c_all, d_ada_all, "ada_wgrad")

    ws = dict(w_in=(w_in, m_w_in, v_w_in), w_branch_a=(w_branch_a, m_w_branch_a, v_w_branch_a),
              w_branch_b=(w_branch_b, m_w_branch_b, v_w_branch_b), w_out=(w_out, m_w_out, v_w_out),
              w_ffn_in=(w_ffn_in, m_w_ffn_in, v_w_ffn_in), w_ffn_out=(w_ffn_out, m_w_ffn_out, v_w_ffn_out))
    res = {"w_ada": _adamw(ada_parts, w_ada[0], m_w_ada[0], v_w_ada[0], "adamw_w_ada")}
    after = res["w_ada"][0]
    for gi, (names, sends, handle) in enumerate(in_flight):
        zones = _exchange_wait(handle, after, "scatter_wait_%d" % gi)
        for nm, zone, sent in zip(names, zones, sends):
            w, m, v = ws[nm]
            parts = mine_into(zone, lax.dynamic_index_in_dim(sent, me, 0, keepdims=False))
            res[nm] = _adamw(parts, w[0], m[0], v[0], "adamw_" + nm)
            after = res[nm][0]

    order = ["w_ada", "b_ada", "g_pre_mix", "g_post_mix", "w_in", "b_f", "sinks", "w_branch_a", "w_branch_b", "w_out",
             "g_pre_ffn", "g_post_ffn", "w_ffn_in", "w_ffn_out"]
    outs = [s_grad["loss"].reshape(()), grad_x[None]]
    for which, small_o in enumerate((s_grad, s_delta, s_m, s_v)):
        for nm in order:
            outs.append(res[nm][which][None] if nm in res else small_o[nm])
    return tuple(outs)
```

```python
import functools
import math

import jax
import jax.numpy as jnp
from jax import lax
from jax.experimental import pallas as pl
from jax.experimental.pallas import tpu as pltpu

f32 = jnp.float32
bf16 = jnp.bfloat16

D_MODEL = 1024
HEAD_DIM = 64
N_HEADS = 8
N_PAIRS = 4
QKV_W = 2304
GATE_W = 2048
F_OFF = 2304
IN_W = 4360
WINDOW = 128
ROPE_THETA = 10000.0
RMS_EPS = 1e-6
D_FF = 2816
N_DEV = 8
ADAM_LR, ADAM_B1, ADAM_B2, ADAM_EPS, ADAM_WD, ADAM_STEP = 0.001, 0.9, 0.999, 1e-08, 0.01, 10
NEG = -1e30
LANES = 128
VMEM_LIMIT = 48 * 1024 * 1024
MESH = pl.DeviceIdType.MESH

_NT = (((1,), (1,)), ((), ()))
_TN = (((0,), (0,)), ((), ()))


def _params(n_grid=0):
    sem = ("arbitrary",) * n_grid if n_grid else None
    return pltpu.CompilerParams(dimension_semantics=sem, vmem_limit_bytes=VMEM_LIMIT)


def _row_tile(s, want):
    t = min(s, want)
    assert s % t == 0, (s, t)
    return t


def _col_tile(n):
    for t in (512, 768, 640, 256, 384, 128):
        if n % t == 0:
            return t
    raise ValueError(n)


def _matmul(a, b, mode, out_dtype, name, after=None):
    if mode == "nn":
        (m, k), n = a.shape, b.shape[1]
    elif mode == "nt":
        (m, k), n = a.shape, b.shape[0]
    else:
        (k, m), n = a.shape, b.shape[1]
    tm = _row_tile(m, {"nn": 1024, "nt": 512, "tn": 256}[mode])
    tn = _col_tile(n)
    if mode == "nn":
        a_spec, b_spec, dims = pl.BlockSpec((tm, k), lambda i, j: (i, 0)), pl.BlockSpec((k, tn), lambda i, j: (0, j)), None
    elif mode == "nt":
        a_spec, b_spec, dims = pl.BlockSpec((tm, k), lambda i, j: (i, 0)), pl.BlockSpec((tn, k), lambda i, j: (j, 0)), _NT
    else:
        a_spec, b_spec, dims = pl.BlockSpec((k, tm), lambda i, j: (0, i)), pl.BlockSpec((k, tn), lambda i, j: (0, j)), _TN

    def body(a_ref, b_ref, *rest):
        o_ref = rest[-1]
        av, bv = a_ref[...].astype(bf16), b_ref[...].astype(bf16)
        if dims is None:
            r = jnp.dot(av, bv, preferred_element_type=f32)
        else:
            r = lax.dot_general(av, bv, dims, preferred_element_type=f32)
        o_ref[...] = r.astype(out_dtype)

    extra = [] if after is None else [after]
    return pl.pallas_call(
        body, name=name, grid=(m // tm, n // tn), in_specs=[a_spec, b_spec] + [pl.BlockSpec(memory_space=pl.ANY)] * len(extra),
        out_specs=pl.BlockSpec((tm, tn), lambda i, j: (i, j)),
        out_shape=jax.ShapeDtypeStruct((m, n), out_dtype), compiler_params=_params(2),
    )(a, b, *extra)


def _rstd(v):
    return lax.rsqrt(jnp.mean(v * v, axis=-1, keepdims=True) + RMS_EPS)


def _row_spec(tm, d):
    return pl.BlockSpec((tm, d), lambda i: (i, 0))


def _vec_spec(d, rows=1):
    return pl.BlockSpec((rows, d), lambda i: (0, 0))


def _prenorm(x, g, scale, shift, name):
    s, d = x.shape
    tm = _row_tile(s, 512)

    def body(x_ref, g_ref, sc_ref, sh_ref, h_ref):
        xv = x_ref[...]
        h = (xv * _rstd(xv) * g_ref[...]) * (1.0 + sc_ref[...]) + sh_ref[...]
        h_ref[...] = h.astype(bf16)

    return pl.pallas_call(
        body, name=name, grid=(s // tm,), in_specs=[_row_spec(tm, d)] + [_vec_spec(d)] * 3,
        out_specs=_row_spec(tm, d), out_shape=jax.ShapeDtypeStruct((s, d), bf16), compiler_params=_params(1),
    )(x, g, scale, shift)


def _postnorm_res(x, y, g, gate, name):
    s, d = x.shape
    tm = _row_tile(s, 512)

    def body(x_ref, y_ref, g_ref, gate_ref, o_ref):
        yv = y_ref[...]
        o_ref[...] = x_ref[...] + gate_ref[...] * (yv * _rstd(yv) * g_ref[...])

    return pl.pallas_call(
        body, name=name, grid=(s // tm,), in_specs=[_row_spec(tm, d)] * 2 + [_vec_spec(d)] * 2,
        out_specs=_row_spec(tm, d), out_shape=jax.ShapeDtypeStruct((s, d), f32), compiler_params=_params(1),
    )(x, y, g, gate)


def _loss_head(out, target, name):
    s, d = out.shape
    tm = _row_tile(s, 512)

    def body(o_ref, t_ref, loss_ref, d_ref):
        @pl.when(pl.program_id(0) == 0)
        def _():
            loss_ref[...] = jnp.zeros_like(loss_ref)
        err = o_ref[...] - t_ref[...]
        d_ref[...] = err / d
        loss_ref[...] += 0.5 * jnp.sum(jnp.mean(err * err, axis=-1, keepdims=True), axis=0, keepdims=True)

    return pl.pallas_call(
        body, name=name, grid=(s // tm,), in_specs=[_row_spec(tm, d)] * 2,
        out_specs=[_vec_spec(LANES), _row_spec(tm, d)],
        out_shape=[jax.ShapeDtypeStruct((1, LANES), f32), jax.ShapeDtypeStruct((s, d), f32)], compiler_params=_params(1),
    )(out, target)


def _rms_bwd(u, v, r):
    return r * u - v * (r * r * r) * jnp.mean(u * v, axis=-1, keepdims=True)


def _postnorm_bwd(dres, y, g, gate, name):
    s, d = y.shape
    tm = _row_tile(s, 512)

    def body(dr_ref, y_ref, g_ref, gate_ref, dy_ref, vec_ref):
        @pl.when(pl.program_id(0) == 0)
        def _():
            vec_ref[...] = jnp.zeros_like(vec_ref)
        dr, yv = dr_ref[...], y_ref[...]
        r = _rstd(yv)
        yn = yv * r
        dn = dr * gate_ref[...]
        vec_ref[0:1, :] += jnp.sum(dr * (yn * g_ref[...]), axis=0, keepdims=True)
        vec_ref[1:2, :] += jnp.sum(dn * yn, axis=0, keepdims=True)
        dy_ref[...] = _rms_bwd(dn * g_ref[...], yv, r).astype(bf16)

    return pl.pallas_call(
        body, name=name, grid=(s // tm,), in_specs=[_row_spec(tm, d)] * 2 + [_vec_spec(d)] * 2,
        out_specs=[_row_spec(tm, d), _vec_spec(d, 8)],
        out_shape=[jax.ShapeDtypeStruct((s, d), bf16), jax.ShapeDtypeStruct((8, d), f32)], compiler_params=_params(1),
    )(dres, y, g, gate)


def _prenorm_bwd(dh, x, g, scale, dres, name):
    s, d = x.shape
    tm = _row_tile(s, 512)

    def body(dh_ref, x_ref, g_ref, sc_ref, dr_ref, dx_ref, vec_ref):
        @pl.when(pl.program_id(0) == 0)
        def _():
            vec_ref[...] = jnp.zeros_like(vec_ref)
        dhv, xv = dh_ref[...], x_ref[...]
        r = _rstd(xv)
        xn = xv * r
        dn = dhv * (1.0 + sc_ref[...])
        vec_ref[0:1, :] += jnp.sum(dhv, axis=0, keepdims=True)
        vec_ref[1:2, :] += jnp.sum(dhv * (xn * g_ref[...]), axis=0, keepdims=True)
        vec_ref[2:3, :] += jnp.sum(dn * xn, axis=0, keepdims=True)
        dx_ref[...] = dr_ref[...] + _rms_bwd(dn * g_ref[...], xv, r)

    return pl.pallas_call(
        body, name=name, grid=(s // tm,),
        in_specs=[_row_spec(tm, d)] * 2 + [_vec_spec(d)] * 2 + [_row_spec(tm, d)],
        out_specs=[_row_spec(tm, d), _vec_spec(d, 8)],
        out_shape=[jax.ShapeDtypeStruct((s, d), f32), jax.ShapeDtypeStruct((8, d), f32)], compiler_params=_params(1),
    )(dh, x, g, scale, dres)


def _lane():
    return lax.broadcasted_iota(jnp.int32, (1, LANES), 1)


def _rope_tables(pos_col, inv_freq, name):
    s = pos_col.shape[0]

    def body(p_ref, f_ref, cos_ref, sin_ref):
        ang = p_ref[...].astype(f32) * f_ref[...]
        first_half = (_lane() % HEAD_DIM) < HEAD_DIM // 2
        cos_ref[...] = jnp.cos(ang)
        sn = jnp.sin(ang)
        sin_ref[...] = jnp.where(first_half, -sn, sn)

    return pl.pallas_call(
        body, name=name, out_shape=[jax.ShapeDtypeStruct((s, LANES), f32)] * 2, compiler_params=_params(),
    )(pos_col, inv_freq)


def _swap_halves(v):
    first_half = (_lane() % HEAD_DIM) < HEAD_DIM // 2
    return jnp.where(first_half, pltpu.roll(v, LANES - HEAD_DIM // 2, axis=1), pltpu.roll(v, HEAD_DIM // 2, axis=1))


def _qkv_prep(qkv, cos, sin_s, name):
    s = qkv.shape[0]
    tm = _row_tile(s, 256)
    scale = 1.0 / math.sqrt(HEAD_DIM)

    def body(p_ref, c_ref, s_ref, qa_ref, ka_ref, va_ref, qb_ref, kb_ref, vb_ref):
        cs, sn = c_ref[...], s_ref[...]
        low = _lane() < HEAD_DIM

        def blk(j):
            return p_ref[:, j * LANES:(j + 1) * LANES]

        def rope(v):
            return v * cs + _swap_halves(v) * sn

        def expand(v):
            other = pltpu.roll(v, HEAD_DIM, axis=1)
            return jnp.where(low, v, other), jnp.where(low, other, v)

        for j in range(N_PAIRS):
            qa_ref[:, j * LANES:(j + 1) * LANES] = (rope(blk(j)) * scale).astype(bf16)
            qb_ref[:, j * LANES:(j + 1) * LANES] = (blk(6 + j) * scale).astype(bf16)
            kb_ref[:, j * LANES:(j + 1) * LANES] = blk(10 + j).astype(bf16)
            vb_ref[:, j * LANES:(j + 1) * LANES] = blk(14 + j).astype(bf16)
        k0, k1 = expand(rope(blk(4)))
        v0, v1 = expand(blk(5))
        for j in range(N_PAIRS):
            ka_ref[:, j * LANES:(j + 1) * LANES] = (k0 if j < 2 else k1).astype(bf16)
            va_ref[:, j * LANES:(j + 1) * LANES] = (v0 if j < 2 else v1).astype(bf16)

    hw = N_PAIRS * LANES
    return pl.pallas_call(
        body, name=name, grid=(s // tm,),
        in_specs=[_row_spec(tm, QKV_W), _row_spec(tm, LANES), _row_spec(tm, LANES)],
        out_specs=[_row_spec(tm, hw)] * 6, out_shape=[jax.ShapeDtypeStruct((s, hw), bf16)] * 6, compiler_params=_params(1),
    )(qkv, cos, sin_s)


def _qkv_prep_bwd(dqa_t, dka, dva, dqb_t, dkb, dvb, cos, sin_s, name):
    s = dka.shape[0]
    tm = _row_tile(s, 256)
    scale = 1.0 / math.sqrt(HEAD_DIM)
    hw = N_PAIRS * LANES
    t_spec = pl.BlockSpec((hw, tm), lambda i: (0, i))

    def body(dqa_ref, dka_ref, dva_ref, dqb_ref, dkb_ref, dvb_ref, c_ref, s_ref, o_ref):
        cs, sn = c_ref[...], s_ref[...]
        low = _lane() < HEAD_DIM

        def blk(ref, j):
            return ref[:, j * LANES:(j + 1) * LANES]

        def blk_t(ref, j):
            return ref[j * LANES:(j + 1) * LANES, :].T

        def unrope(v):
            return v * cs + _swap_halves(v * sn)

        def fold(ref):
            a, b = blk(ref, 0) + blk(ref, 1), blk(ref, 2) + blk(ref, 3)
            kv0 = a + pltpu.roll(a, HEAD_DIM, axis=1)
            kv1 = b + pltpu.roll(b, HEAD_DIM, axis=1)
            return jnp.where(low, kv0, kv1)

        for j in range(N_PAIRS):
            o_ref[:, j * LANES:(j + 1) * LANES] = (unrope(blk_t(dqa_ref, j)) * scale).astype(bf16)
            o_ref[:, (6 + j) * LANES:(7 + j) * LANES] = (blk_t(dqb_ref, j) * scale).astype(bf16)
            o_ref[:, (10 + j) * LANES:(11 + j) * LANES] = blk(dkb_ref, j).astype(bf16)
            o_ref[:, (14 + j) * LANES:(15 + j) * LANES] = blk(dvb_ref, j).astype(bf16)
        o_ref[:, 4 * LANES:5 * LANES] = unrope(fold(dka_ref)).astype(bf16)
        o_ref[:, 5 * LANES:6 * LANES] = fold(dva_ref).astype(bf16)

    return pl.pallas_call(
        body, name=name, grid=(s // tm,),
        in_specs=[t_spec, _row_spec(tm, hw), _row_spec(tm, hw), t_spec, _row_spec(tm, hw), _row_spec(tm, hw)] + [_row_spec(tm, LANES)] * 2,
        out_specs=_row_spec(tm, QKV_W), out_shape=jax.ShapeDtypeStruct((s, QKV_W), bf16), compiler_params=_params(1),
    )(dqa_t, dka, dva, dqb_t, dkb, dvb, cos, sin_s)


def _cumsum_rows(v, reverse=False):
    n = v.shape[0]
    row = lax.broadcasted_iota(jnp.int32, v.shape, 0)
    sh = 1
    while sh < n:
        if reverse:
            v = v + jnp.where(row < n - sh, pltpu.roll(v, n - sh, axis=0), 0.0)
        else:
            v = v + jnp.where(row >= sh, pltpu.roll(v, sh, axis=0), 0.0)
        sh *= 2
    return v


def _log_sigmoid(z):
    return jnp.minimum(z, 0.0) - jnp.log1p(jnp.exp(-jnp.abs(z)))


def _forget_prep(fl, bf_row, name):
    s = fl.shape[0]

    def body(f_ref, b_ref, cb_ref):
        cum = _cumsum_rows(_log_sigmoid(f_ref[...] + b_ref[...]))
        for h in range(N_HEADS):
            cb_ref[:, h * LANES:(h + 1) * LANES] = jnp.broadcast_to(cum[:, h:h + 1], (s, LANES))

    return pl.pallas_call(
        body, name=name, out_shape=jax.ShapeDtypeStruct((s, N_HEADS * LANES), f32), compiler_params=_params(),
    )(fl, bf_row)


def _forget_prep_bwd(rs, dcs, fl, bf_row, name):
    s = fl.shape[0]

    def body(r_ref, c_ref, f_ref, b_ref, df_ref, db_ref):
        eye = (lax.broadcasted_iota(jnp.int32, (N_HEADS, LANES), 0) == lax.broadcasted_iota(jnp.int32, (N_HEADS, LANES), 1)).astype(f32)
        dcum = lax.dot_general(r_ref[...], eye, _TN, precision=lax.Precision.HIGHEST, preferred_element_type=f32)
        for h in range(N_HEADS):
            dcum = dcum - jnp.where(_lane() == h, jnp.sum(c_ref[:, h * LANES:(h + 1) * LANES], axis=1, keepdims=True), 0.0)
        dlf = _cumsum_rows(dcum, reverse=True)
        z = f_ref[...] + b_ref[...]
        df = jnp.where(_lane() < N_HEADS, dlf * jax.nn.sigmoid(-z), 0.0)
        df_ref[...] = df.astype(bf16)
        db_ref[...] = jnp.zeros_like(db_ref)
        db_ref[0:1, :] = jnp.sum(df, axis=0, keepdims=True)

    return pl.pallas_call(
        body, name=name,
        out_shape=[jax.ShapeDtypeStruct((s, LANES), bf16), jax.ShapeDtypeStruct((8, LANES), f32)], compiler_params=_params(),
    )(rs, dcs, fl, bf_row)


def _tile_mask(t, off, window):
    d = lax.broadcasted_iota(jnp.int32, (t, t), 1) - lax.broadcasted_iota(jnp.int32, (t, t), 0) + off
    valid = d >= 0
    return jnp.logical_and(valid, d < window) if window else valid


def _wide(v, t):
    return jnp.concatenate([v] * (t // LANES), axis=1)


def _attn_fwd(q, k, v, name, *, cum_b=None, sink_rows=None, window=None, t=256):
    s = q.shape[0]
    t = _row_tile(s, t)
    fox, has_sink = cum_b is not None, sink_rows is not None
    assert not window or window <= t

    def body(*refs):
        q_ref, k_ref, v_ref = refs[:3]
        rest = list(refs[3:])
        cb_ref = rest.pop(0) if fox else None
        sink_ref = rest.pop(0) if has_sink else None
        o_ref, lse_ref = rest
        i = pl.program_id(1)
        low = _lane() < HEAD_DIM
        q2 = q_ref[...]
        zero = jnp.zeros_like(q2)
        qms = (jnp.where(low, q2, zero), jnp.where(low, zero, q2))

        def tile(kb, carry, masked):
            k0 = pl.multiple_of(kb * t, t)
            kblk, vblk = k_ref[pl.ds(k0, t), :], v_ref[pl.ds(k0, t), :]
            valid = _tile_mask(t, (i - kb) * t, window) if masked else None
            out = []
            for h in range(2):
                m, l, acc = carry[h]
                sc = lax.dot_general(kblk, qms[h], _NT, preferred_element_type=f32)
                if fox:
                    sc = sc - _wide(cb_ref[pl.ds(k0, t), h * LANES:(h + 1) * LANES], t)
                if masked:
                    sc = jnp.where(valid, sc, NEG)
                m_new = jnp.maximum(m, jnp.max(sc, axis=0, keepdims=True))
                p = jnp.exp(sc - m_new)
                alpha = jnp.exp(m - m_new)
                l = alpha * l + jnp.sum(p, axis=0, keepdims=True)
                acc = alpha * acc + lax.dot_general(vblk, p.astype(bf16), _TN, preferred_element_type=f32)
                out.append((m_new, l, acc))
            return tuple(out)

        init = []
        for h in range(2):
            if has_sink:
                init.append((_wide(sink_ref[h:h + 1, :], t), jnp.ones((1, t), f32), jnp.zeros((LANES, t), f32)))
            else:
                init.append((jnp.full((1, t), NEG, f32), jnp.zeros((1, t), f32), jnp.zeros((LANES, t), f32)))
        carry = tuple(init)
        if window:
            carry = lax.fori_loop(jnp.maximum(i - 1, 0), i + 1, functools.partial(tile, masked=True), carry)
        else:
            carry = lax.fori_loop(0, i, functools.partial(tile, masked=False), carry)
            carry = tile(i, carry, True)
        (m0, l0, a0), (m1, l1, a1) = carry
        top = lax.broadcasted_iota(jnp.int32, (LANES, 1), 0) < HEAD_DIM
        o_t = jnp.where(top, a0 * (1.0 / l0), a1 * (1.0 / l1))
        o_ref[...] = o_t.T.astype(bf16)
        lse_ref[0:1, :] = m0 + jnp.log(l0)
        lse_ref[1:2, :] = m1 + jnp.log(l1)

    q_spec = pl.BlockSpec((t, LANES), lambda j, i: (i, j))
    kv_spec = pl.BlockSpec((s, LANES), lambda j, i: (0, j))
    in_specs, args = [q_spec, kv_spec, kv_spec], [q, k, v]
    if fox:
        in_specs += [pl.BlockSpec((s, 2 * LANES), lambda j, i: (0, j))]
        args += [cum_b]
    if has_sink:
        in_specs += [pl.BlockSpec((None, 2, LANES), lambda j, i: (j, 0, 0))]
        args += [sink_rows.reshape(N_PAIRS, 2, LANES)]
    return pl.pallas_call(
        body, name=name, grid=(N_PAIRS, s // t), in_specs=in_specs,
        out_specs=[q_spec, pl.BlockSpec((None, 2, t), lambda j, i: (j, 0, i))],
        out_shape=[jax.ShapeDtypeStruct((s, N_PAIRS * LANES), bf16), jax.ShapeDtypeStruct((N_PAIRS, 2, s), f32)],
        compiler_params=_params(2),
    )(*args)


def _attn_delta(do, o, name, *, lse=None, sink_rows=None):
    s, hw = do.shape
    tm = _row_tile(s, 512)
    has_sink = sink_rows is not None

    def body(*refs):
        do_ref, o_ref = refs[:2]
        if has_sink:
            lse_ref, sink_ref, dl_ref, ds_ref = refs[2:]

            @pl.when(pl.program_id(0) == 0)
            def _():
                ds_ref[...] = jnp.zeros_like(ds_ref)
        else:
            dl_ref, = refs[2:]
        for j in range(N_PAIRS):
            cols = slice(j * LANES, (j + 1) * LANES)
            prod_t = (do_ref[:, cols].astype(f32) * o_ref[:, cols].astype(f32)).T
            for h in range(2):
                dl = jnp.sum(prod_t[h * HEAD_DIM:(h + 1) * HEAD_DIM, :], axis=0, keepdims=True)
                dl_ref[j, h:h + 1, :] = dl
                if has_sink:
                    r = 2 * j + h
                    p_sink = jnp.exp(sink_ref[r:r + 1, 0:1] - lse_ref[j, h:h + 1, :])
                    ds_ref[r:r + 1, :] += -jnp.sum(p_sink * dl, axis=1, keepdims=True)

    rows_spec = pl.BlockSpec((N_PAIRS, 2, tm), lambda i: (0, 0, i))
    in_specs, args = [_row_spec(tm, hw)] * 2, [do, o]
    out_specs, out_shape = [rows_spec], [jax.ShapeDtypeStruct((N_PAIRS, 2, s), f32)]
    if has_sink:
        in_specs += [rows_spec, _vec_spec(LANES, N_HEADS)]
        args += [lse, sink_rows]
        out_specs += [_vec_spec(LANES, N_HEADS)]
        out_shape += [jax.ShapeDtypeStruct((N_HEADS, LANES), f32)]
    return pl.pallas_call(
        body, name=name, grid=(s // tm,), in_specs=in_specs, out_specs=out_specs, out_shape=out_shape,
        compiler_params=_params(1),
    )(*args)


def _attn_bwd(q, k, v, do, lse, delta, name, *, cum_b=None, window=None, t=256):
    s = q.shape[0]
    t = _row_tile(s, t)
    nblk = s // t
    fox = cum_b is not None
    assert not window or window <= t

    def body(*refs):
        k_ref, v_ref, q_ref, do_ref, lse_ref, dl_ref = refs[:6]
        rest = list(refs[6:])
        cb_ref = rest.pop(0) if fox else None
        dq_ref, dk_ref, dv_ref = rest[:3]
        dcs_ref, rs_ref = (rest[3], rest[4]) if fox else (None, None)
        b = pl.program_id(1)
        k0 = pl.multiple_of(b * t, t)

        @pl.when(b == 0)
        def _():
            dq_ref[...] = jnp.zeros_like(dq_ref)
            if fox:
                rs_ref[...] = jnp.zeros_like(rs_ref)

        dk_ref[...] = jnp.zeros_like(dk_ref)
        dv_ref[...] = jnp.zeros_like(dv_ref)
        if fox:
            dcs_ref[...] = jnp.zeros_like(dcs_ref)
        low = _lane() < HEAD_DIM
        top = lax.broadcasted_iota(jnp.int32, (LANES, 1), 0) < HEAD_DIM
        kblk, vblk = k_ref[...], v_ref[...]
        k_t = kblk.astype(f32).T.astype(bf16)
        cks = [_wide(cb_ref[pl.ds(k0, t), h * LANES:(h + 1) * LANES], t) for h in range(2)] if fox else None

        def tile(qb, carry, masked):
            q0 = pl.multiple_of(qb * t, t)
            cols = pl.ds(q0, t)
            q2, do2 = q_ref[cols, :], do_ref[cols, :]
            zero = jnp.zeros_like(q2)
            valid = _tile_mask(t, (qb - b) * t, window) if masked else None
            dq_parts = []
            for h in range(2):
                qm = jnp.where(low, q2, zero) if h == 0 else jnp.where(low, zero, q2)
                dom = jnp.where(low, do2, zero) if h == 0 else jnp.where(low, zero, do2)
                sc = lax.dot_general(kblk, qm, _NT, preferred_element_type=f32)
                if fox:
                    sc = sc - cks[h]
                if masked:
                    sc = jnp.where(valid, sc, NEG)
                p = jnp.exp(sc - lse_ref[h:h + 1, cols])
                dp = lax.dot_general(vblk, dom, _NT, preferred_element_type=f32)
                ds = p * (dp - dl_ref[h:h + 1, cols])
                pb, dsb = p.astype(bf16), ds.astype(bf16)
                dv_ref[...] += jnp.dot(pb, dom, preferred_element_type=f32)
                dk_ref[...] += jnp.dot(dsb, qm, preferred_element_type=f32)
                dq_parts.append(jnp.dot(k_t, dsb, preferred_element_type=f32))
                if fox:
                    dcs_ref[:, h * LANES:(h + 1) * LANES] += sum(ds[:, g * LANES:(g + 1) * LANES] for g in range(t // LANES))
                    rs_ref[h:h + 1, cols] += jnp.sum(ds, axis=0, keepdims=True)
            dq_ref[:, cols] += jnp.where(top, dq_parts[0], dq_parts[1])
            return carry

        if window:
            lax.fori_loop(b, jnp.minimum(b + 1, nblk - 1) + 1, functools.partial(tile, masked=True), 0)
        else:
            tile(b, 0, True)
            lax.fori_loop(b + 1, nblk, functools.partial(tile, masked=False), 0)

    kv_spec = pl.BlockSpec((t, LANES), lambda j, b: (b, j))
    seq_spec = pl.BlockSpec((s, LANES), lambda j, b: (0, j))
    rows_spec = pl.BlockSpec((None, 2, s), lambda j, b: (j, 0, 0))
    hw = N_PAIRS * LANES
    in_specs, args = [kv_spec, kv_spec, seq_spec, seq_spec, rows_spec, rows_spec], [k, v, q, do, lse, delta]
    out_specs = [pl.BlockSpec((LANES, s), lambda j, b: (j, 0)), kv_spec, kv_spec]
    out_shape = [jax.ShapeDtypeStruct((hw, s), f32), jax.ShapeDtypeStruct((s, hw), f32), jax.ShapeDtypeStruct((s, hw), f32)]
    if fox:
        in_specs += [pl.BlockSpec((s, 2 * LANES), lambda j, b: (0, j))]
        args += [cum_b]
        out_specs += [pl.BlockSpec((t, 2 * LANES), lambda j, b: (b, j)), rows_spec]
        out_shape += [jax.ShapeDtypeStruct((s, N_HEADS * LANES), f32), jax.ShapeDtypeStruct((N_PAIRS, 2, s), f32)]
    return pl.pallas_call(
        body, name=name, grid=(N_PAIRS, nblk), in_specs=in_specs, out_specs=out_specs, out_shape=out_shape,
        compiler_params=_params(2),
    )(*args)


def _merge(ba, bb, gl, name):
    s, d = ba.shape
    tm = _row_tile(s, 512)

    def body(a_ref, b_ref, g_ref, o_ref):
        o_ref[...] = (jax.nn.sigmoid(g_ref[:, :d]) * a_ref[...] + jax.nn.sigmoid(g_ref[:, d:]) * b_ref[...]).astype(bf16)

    return pl.pallas_call(
        body, name=name, grid=(s // tm,), in_specs=[_row_spec(tm, d)] * 2 + [_row_spec(tm, 2 * d)],
        out_specs=_row_spec(tm, d), out_shape=jax.ShapeDtypeStruct((s, d), bf16), compiler_params=_params(1),
    )(ba, bb, gl)


def _merge_bwd(dm, ba, bb, gl, name):
    s, d = ba.shape
    tm = _row_tile(s, 512)

    def body(dm_ref, a_ref, b_ref, g_ref, da_ref, db_ref, dg_ref):
        dmv = dm_ref[...]
        g0, g1 = jax.nn.sigmoid(g_ref[:, :d]), jax.nn.sigmoid(g_ref[:, d:])
        da_ref[...] = (dmv * g0).astype(bf16)
        db_ref[...] = (dmv * g1).astype(bf16)
        dg_ref[:, :d] = (dmv * a_ref[...] * (g0 * (1.0 - g0))).astype(bf16)
        dg_ref[:, d:] = (dmv * b_ref[...] * (g1 * (1.0 - g1))).astype(bf16)

    return pl.pallas_call(
        body, name=name, grid=(s // tm,), in_specs=[_row_spec(tm, d)] * 3 + [_row_spec(tm, 2 * d)],
        out_specs=[_row_spec(tm, d)] * 2 + [_row_spec(tm, 2 * d)],
        out_shape=[jax.ShapeDtypeStruct((s, d), bf16)] * 2 + [jax.ShapeDtypeStruct((s, 2 * d), bf16)],
        compiler_params=_params(1),
    )(dm, ba, bb, gl)


def _swiglu(gu, name):
    s, w = gu.shape
    h = w // 2
    tm = _row_tile(s, 256)

    def body(g_ref, o_ref):
        g = g_ref[:, :h]
        o_ref[...] = (g * jax.nn.sigmoid(g) * g_ref[:, h:]).astype(bf16)

    return pl.pallas_call(
        body, name=name, grid=(s // tm,), in_specs=[_row_spec(tm, w)], out_specs=_row_spec(tm, h),
        out_shape=jax.ShapeDtypeStruct((s, h), bf16), compiler_params=_params(1),
    )(gu)


def _swiglu_bwd(dact, gu, name):
    s, w = gu.shape
    h = w // 2
    tm = _row_tile(s, 256)

    def body(d_ref, g_ref, o_ref):
        dv, g, u = d_ref[...], g_ref[:, :h], g_ref[:, h:]
        sg = jax.nn.sigmoid(g)
        o_ref[:, :h] = (dv * u * (sg * (1.0 + g * (1.0 - sg)))).astype(bf16)
        o_ref[:, h:] = (dv * (g * sg)).astype(bf16)

    return pl.pallas_call(
        body, name=name, grid=(s // tm,), in_specs=[_row_spec(tm, h), _row_spec(tm, w)], out_specs=_row_spec(tm, w),
        out_shape=jax.ShapeDtypeStruct((s, w), bf16), compiler_params=_params(1),
    )(dact, gu)


def _ada_fwd(c_all, w, b, name):
    def body(c_ref, w_ref, b_ref, o_ref):
        o_ref[...] = jnp.dot(c_ref[...].astype(bf16), w_ref[...].astype(bf16), preferred_element_type=f32) + b_ref[...]

    return pl.pallas_call(
        body, name=name, out_shape=jax.ShapeDtypeStruct((c_all.shape[0], w.shape[1]), f32), compiler_params=_params(),
    )(c_all, w, b)


def _ada_wgrad(c_all, d_all, name):
    n, d = c_all.shape
    w = d_all.shape[1]

    def body(c_ref, d_ref, o_ref):
        eye = (lax.broadcasted_iota(jnp.int32, (n, n), 0) == lax.broadcasted_iota(jnp.int32, (n, n), 1)).astype(f32)
        ct = lax.dot_general(c_ref[...], eye, _TN, precision=lax.Precision.HIGHEST, preferred_element_type=f32)
        g = ct[:, 0:1] * d_ref[0:1, :]
        for bi in range(1, n):
            g = g + ct[:, bi:bi + 1] * d_ref[bi:bi + 1, :]
        o_ref[0] = g

    return pl.pallas_call(
        body, name=name, out_shape=jax.ShapeDtypeStruct((1, d, w), f32), compiler_params=_params(),
    )(c_all, d_all)


def _adamw(parts, w, m, v, name):
    r, c = w.shape
    n_parts = parts.shape[0]
    tr = next(t for t in range(min(r, 256), 0, -1) if r % t == 0 and (t % 16 == 0 or t == r))

    def body(p_ref, w_ref, m_ref, v_ref, g_ref, d_ref, nm_ref, nv_ref):
        g = p_ref[0].astype(f32)
        for i in range(1, n_parts):
            g = g + p_ref[i].astype(f32)
        mm = ADAM_B1 * m_ref[...] + (1.0 - ADAM_B1) * g
        vv = ADAM_B2 * v_ref[...] + (1.0 - ADAM_B2) * (g * g)
        m_hat = mm / (1.0 - ADAM_B1 ** ADAM_STEP)
        v_hat = vv / (1.0 - ADAM_B2 ** ADAM_STEP)
        g_ref[...] = g
        d_ref[...] = -ADAM_LR * (m_hat / (jnp.sqrt(v_hat) + ADAM_EPS) + ADAM_WD * w_ref[...])
        nm_ref[...] = mm
        nv_ref[...] = vv

    spec = pl.BlockSpec((tr, c), lambda i: (i, 0))
    return pl.pallas_call(
        body, name=name, grid=(r // tr,), in_specs=[pl.BlockSpec((n_parts, tr, c), lambda i: (0, i, 0))] + [spec] * 3,
        out_specs=[spec] * 4, out_shape=[jax.ShapeDtypeStruct((r, c), f32)] * 4, compiler_params=_params(1),
    )(parts, w, m, v)


def _me():
    return lax.axis_index("x"), lax.axis_index("y"), lax.axis_index("c")


def _all_gather(arrays, name, vmem=False):
    n = len(arrays)
    space = pltpu.VMEM if vmem else pl.ANY

    def body(*refs):
        ins, outs = refs[:n], refs[n:2 * n]
        send_sems, recv_sems, local_sems = refs[2 * n:]
        x, y, c = _me()
        me, sibling = (x, y, c), (x, y, 1 - c)
        chips = [(1 - x, y), (x, 1 - y), (1 - x, 1 - y)]

        def rows(a, dev):
            return outs[a].at[4 * dev[0] + 2 * dev[1] + dev[2]]

        def copy(a, k, block, to, src=None):
            return pltpu.make_async_remote_copy(
                src_ref=rows(a, block) if src is None else src, dst_ref=rows(a, block),
                send_sem=send_sems.at[a, k], recv_sem=recv_sems.at[a, k], device_id=to, device_id_type=MESH)

        mine = [pltpu.make_async_copy(ins[a], rows(a, me), local_sems.at[a]) for a in range(n)]
        for cp in mine:
            cp.start()
        first = []
        for a in range(n):
            first.append(copy(a, 0, me, sibling, src=ins[a]))
            first += [copy(a, 1 + j, me, (*chip, c), src=ins[a]) for j, chip in enumerate(chips)]
        for cp in first:
            cp.start()
        passed = []
        for j, chip in enumerate(chips):
            for a in range(n):
                copy(a, 1 + j, (*chip, c), me).wait_recv()
                fwd = copy(a, 4 + j, (*chip, c), sibling)
                fwd.start()
                passed.append(fwd)
        for a in range(n):
            copy(a, 0, sibling, me).wait_recv()
            for j, chip in enumerate(chips):
                copy(a, 4 + j, (*chip, 1 - c), me).wait_recv()
        for cp in first + passed:
            cp.wait_send()
        for cp in mine:
            cp.wait()

    outs = pl.pallas_call(
        body, name=name,
        in_specs=[pl.BlockSpec(memory_space=space)] * n, out_specs=[pl.BlockSpec(memory_space=space)] * n,
        out_shape=[jax.ShapeDtypeStruct((N_DEV,) + a.shape, a.dtype) for a in arrays],
        scratch_shapes=[pltpu.SemaphoreType.DMA((n, 7)), pltpu.SemaphoreType.DMA((n, 7)), pltpu.SemaphoreType.DMA((n,))],
        compiler_params=pltpu.CompilerParams(vmem_limit_bytes=VMEM_LIMIT),
    )(*arrays)
    return list(outs)


_FLIPS = ((0, 0, 1), (1, 0, 0), (0, 1, 0), (1, 1, 0), (1, 0, 1), (0, 1, 1), (1, 1, 1))
_HBM = pl.BlockSpec(memory_space=pltpu.HBM)
_SEM = pl.BlockSpec(memory_space=pltpu.SEMAPHORE)


def _exchange_copies(scatter, srcs, lands, send_sems, recv_sems):
    x, y, c = _me()
    me_row = 4 * x + 2 * y + c
    out = []
    for k, (fx, fy, fc) in enumerate(_FLIPS):
        peer = (x ^ fx, y ^ fy, c ^ fc)
        peer_row = 4 * peer[0] + 2 * peer[1] + peer[2]
        for a in range(len(srcs)):
            out.append(pltpu.make_async_remote_copy(
                src_ref=srcs[a].at[peer_row] if scatter else srcs[a], dst_ref=lands[a].at[me_row],
                send_sem=send_sems.at[7 * a + k], recv_sem=recv_sems.at[7 * a + k], device_id=peer, device_id_type=MESH))
    return out


def _exchange_start(arrays, scatter, name):
    n = len(arrays)
    lands = [lax.empty(a.shape if scatter else (N_DEV,) + a.shape, a.dtype) for a in arrays]

    def body(*refs):
        srcs, zones = refs[:n], refs[n:2 * n]
        send_sems, recv_sems = refs[2 * n], refs[2 * n + 1]
        token = refs[-1]
        for cp in _exchange_copies(scatter, srcs, zones, send_sems, recv_sems):
            cp.start()
        token[...] = jnp.zeros_like(token)

    thru = [pltpu.HBM(a.shape, a.dtype) for a in list(arrays) + lands]
    outs = pl.pallas_call(
        body, name=name,
        out_shape=(pltpu.SemaphoreType.DMA((7 * n,)), pltpu.SemaphoreType.DMA((7 * n,)), *thru, jax.ShapeDtypeStruct((8, LANES), f32)),
        in_specs=[_HBM] * (2 * n), out_specs=(_SEM, _SEM, *[_HBM] * (2 * n), pl.BlockSpec(memory_space=pltpu.VMEM)),
        input_output_aliases={i: 2 + i for i in range(2 * n)},
        compiler_params=pltpu.CompilerParams(has_side_effects=pltpu.SideEffectType.DATAFLOW_SIDE_EFFECTING),
    )(*[pltpu.with_memory_space_constraint(a, pltpu.HBM) for a in list(arrays) + lands])
    return dict(n=n, scatter=scatter, sems=outs[:2], srcs=outs[2:2 + n], lands=outs[2 + n:2 + 2 * n], token=outs[-1])


def _exchange_wait(handle, after, name):
    n, scatter = handle["n"], handle["scatter"]

    def body(*refs):
        srcs, zones = refs[:n], refs[n:2 * n]
        send_sems, recv_sems = refs[2 * n], refs[2 * n + 1]
        for cp in _exchange_copies(scatter, srcs, zones, send_sems, recv_sems):
            cp.wait_send()
            cp.wait_recv()

    thru = [pltpu.HBM(a.shape, a.dtype) for a in list(handle["srcs"]) + list(handle["lands"])]
    outs = pl.pallas_call(
        body, name=name, out_shape=tuple(thru),
        in_specs=[_HBM] * (2 * n) + [_SEM, _SEM, pl.BlockSpec(memory_space=pl.ANY)], out_specs=tuple([_HBM] * (2 * n)),
        input_output_aliases={i: i for i in range(2 * n)},
        compiler_params=pltpu.CompilerParams(has_side_effects=pltpu.SideEffectType.DATAFLOW_SIDE_EFFECTING),
    )(*handle["srcs"], *handle["lands"], *handle["sems"], after)
    return list(outs[n:])


def _cols_from_shards(g):
    return jnp.transpose(g, (1, 0, 2)).reshape(g.shape[1], -1)


def _shards_from_cols(a):
    return jnp.transpose(a.reshape(a.shape[0], N_DEV, -1), (1, 0, 2))


def _local_step(x, positions, ada, g_pre_mix, g_post_mix, b_f, sinks, g_pre_ffn, g_post_ffn, target,
                w_in, late_weights, on_grads):
    s, d = x.shape
    row = lambda v: v.reshape(1, -1)
    shift_m, scale_m, gate_m, shift_f, scale_f, gate_f = (ada[i:i + 1] for i in range(6))
    w_gate, w_qkv = w_in[:, F_OFF + N_HEADS:], w_in[:, :QKV_W]
    w_f = jnp.pad(w_in[:, F_OFF:F_OFF + N_HEADS], ((0, 0), (0, LANES - N_HEADS)))
    w_in_p = jnp.concatenate([w_gate, w_qkv, w_f], axis=1)
    bf_row = jnp.pad(row(b_f), ((0, 0), (0, LANES - N_HEADS)))
    sink_rows = jnp.broadcast_to(sinks.reshape(N_HEADS, 1).astype(f32), (N_HEADS, LANES))
    inv_freq = 1.0 / (ROPE_THETA ** (jnp.arange(0, HEAD_DIM, 2, dtype=f32) / HEAD_DIM))
    cos, sin_s = _rope_tables(positions.reshape(s, 1), jnp.tile(inv_freq, 4).reshape(1, LANES), "rope_tables")

    h1 = _prenorm(x, row(g_pre_mix), scale_m, shift_m, "prenorm_mix")
    gl = _matmul(h1, w_gate, "nn", f32, "proj_gate")
    qkv = _matmul(h1, w_qkv, "nn", f32, "proj_qkv")
    fl = _matmul(h1, w_f, "nn", f32, "proj_forget")
    qa, ka, va, qb, kb, vb = _qkv_prep(qkv, cos, sin_s, "qkv_prep")
    cum_b = _forget_prep(fl, bf_row, "forget_prep")
    o_a, lse_a = _attn_fwd(qa, ka, va, "swa_fwd", sink_rows=sink_rows, window=WINDOW)
    o_b, lse_b = _attn_fwd(qb, kb, vb, "fox_fwd", cum_b=cum_b, t=512)
    w_branch_a, w_branch_b, w_out, w_ffn_in, w_ffn_out = late_weights(o_b)
    ba = _matmul(o_a, w_branch_a, "nn", f32, "branch_a")
    bb = _matmul(o_b, w_branch_b, "nn", f32, "branch_b")
    merged = _merge(ba, bb, gl, "merge")
    y1 = _matmul(merged, w_out, "nn", f32, "out_proj")
    x2 = _postnorm_res(x, y1, row(g_post_mix), gate_m, "postnorm_mix")

    h2 = _prenorm(x2, row(g_pre_ffn), scale_f, shift_f, "prenorm_ffn")
    gu = _matmul(h2, w_ffn_in, "nn", f32, "ffn_in")
    act = _swiglu(gu, "swiglu")
    y2 = _matmul(act, w_ffn_out, "nn", f32, "ffn_out")
    out = _postnorm_res(x2, y2, row(g_post_ffn), gate_f, "postnorm_ffn")
    loss_row, d_out = _loss_head(out, target, "loss_head")

    d_y2, vec_pf = _postnorm_bwd(d_out, y2, row(g_post_ffn), gate_f, "postnorm_ffn_bwd")
    g_w_ffn_out = _matmul(act, d_y2, "tn", bf16, "ffn_out_wgrad")
    d_act = _matmul(d_y2, w_ffn_out, "nt", f32, "ffn_out_dgrad")
    dgu = _swiglu_bwd(d_act, gu, "swiglu_bwd")
    g_w_ffn_in = _matmul(h2, dgu, "tn", bf16, "ffn_in_wgrad")
    sent = on_grads(dict(w_ffn_in=g_w_ffn_in, w_ffn_out=g_w_ffn_out))
    d_h2 = _matmul(dgu, w_ffn_in, "nt", f32, "ffn_in_dgrad", after=sent)
    d_x2, vec_nf = _prenorm_bwd(d_h2, x2, row(g_pre_ffn), scale_f, d_out, "prenorm_ffn_bwd")

    d_y1, vec_pm = _postnorm_bwd(d_x2, y1, row(g_post_mix), gate_m, "postnorm_mix_bwd")
    g_w_out = _matmul(merged, d_y1, "tn", bf16, "out_proj_wgrad")
    d_merged = _matmul(d_y1, w_out, "nt", f32, "out_proj_dgrad")
    d_ba, d_bb, dgl = _merge_bwd(d_merged, ba, bb, gl, "merge_bwd")
    g_w_branch_a = _matmul(o_a, d_ba, "tn", bf16, "branch_a_wgrad")
    g_w_branch_b = _matmul(o_b, d_bb, "tn", bf16, "branch_b_wgrad")
    sent = on_grads(dict(w_out=g_w_out, w_branch_a=g_w_branch_a, w_branch_b=g_w_branch_b))
    d_oa = _matmul(d_ba, w_branch_a, "nt", bf16, "branch_a_dgrad", after=sent)
    d_ob = _matmul(d_bb, w_branch_b, "nt", bf16, "branch_b_dgrad", after=sent)
    delta_a, d_sink = _attn_delta(d_oa, o_a, "swa_delta", lse=lse_a, sink_rows=sink_rows)
    delta_b, = _attn_delta(d_ob, o_b, "fox_delta")
    dqa_t, dka, dva = _attn_bwd(qa, ka, va, d_oa, lse_a, delta_a, "swa_bwd", window=WINDOW)
    dqb_t, dkb, dvb, dcs, rs = _attn_bwd(qb, kb, vb, d_ob, lse_b, delta_b, "fox_bwd", cum_b=cum_b, t=512)
    dqkv = _qkv_prep_bwd(dqa_t, dka, dva, dqb_t, dkb, dvb, cos, sin_s, "qkv_prep_bwd")
    dfl, vec_bf = _forget_prep_bwd(rs.reshape(N_HEADS, s), dcs, fl, bf_row, "forget_prep_bwd")
    dproj = jnp.concatenate([dgl, dqkv, dfl], axis=1)
    g_w_in_p = _matmul(h1, dproj, "tn", bf16, "in_proj_wgrad")
    g_w_in = jnp.concatenate([g_w_in_p[:, GATE_W:GATE_W + QKV_W], g_w_in_p[:, GATE_W + QKV_W:GATE_W + QKV_W + N_HEADS],
                              g_w_in_p[:, :GATE_W]], axis=1)
    sent = on_grads(dict(w_in=g_w_in))
    d_h1 = _matmul(dproj, w_in_p, "nt", f32, "in_proj_dgrad", after=sent)
    grad_x, vec_nm = _prenorm_bwd(d_h1, x, row(g_pre_mix), scale_m, d_x2, "prenorm_mix_bwd")

    d_ada = jnp.concatenate([vec_nm[0], vec_nm[1], vec_pm[0], vec_nf[0], vec_nf[1], vec_pf[0]])
    small = dict(b_ada=d_ada, g_pre_mix=vec_nm[2], g_post_mix=vec_pm[1], g_pre_ffn=vec_nf[2], g_post_ffn=vec_pf[1],
                 b_f=vec_bf[0, :N_HEADS], sinks=d_sink[:, 0], loss=loss_row[0, :1])
    return grad_x, small


_SMALL = (("b_ada", 6144), ("g_pre_mix", 1024), ("g_post_mix", 1024), ("g_pre_ffn", 1024), ("g_post_ffn", 1024),
          ("b_f", 128), ("sinks", 128), ("loss", 128))
_SMALL_ROWS = 88


def _pack_small(vals):
    parts = [jnp.pad(vals[k].reshape(-1).astype(f32), (0, n - vals[k].size)) for k, n in _SMALL]
    flat = jnp.concatenate(parts)
    return jnp.pad(flat, (0, _SMALL_ROWS * LANES - flat.size)).reshape(_SMALL_ROWS, LANES)


def _unpack_small(slab, shapes):
    flat, out, off = slab.reshape(-1), {}, 0
    for k, n in _SMALL:
        size = math.prod(shapes[k])
        out[k] = flat[off:off + size].reshape(shapes[k])
        off += n
    return out


def kernel(x, c, positions, w_ada, b_ada, g_pre_mix, g_post_mix, w_in, b_f, sinks, w_branch_a, w_branch_b, w_out, g_pre_ffn, g_post_ffn, w_ffn_in, w_ffn_out, loss_target, m_w_ada, m_b_ada, m_g_pre_mix, m_g_post_mix, m_w_in, m_b_f, m_sinks, m_w_branch_a, m_w_branch_b, m_w_out, m_g_pre_ffn, m_g_post_ffn, m_w_ffn_in, m_w_ffn_out, v_w_ada, v_b_ada, v_g_pre_mix, v_g_post_mix, v_w_in, v_b_f, v_sinks, v_w_branch_a, v_w_branch_b, v_w_out, v_g_pre_ffn, v_g_post_ffn, v_w_ffn_in, v_w_ffn_out):
    xi, yi, ci = _me()
    me = 4 * xi + 2 * yi + ci
    d = D_MODEL
    ada_w = w_ada.shape[2]

    c_all, = _all_gather([c], "gather_c", vmem=True)
    c_all = c_all.reshape(N_DEV, d)
    b_mine = lax.dynamic_slice(b_ada, (0, me * ada_w), (1, ada_w))
    ada_cols = _ada_fwd(c_all, w_ada[0], b_mine, "ada_fwd")
    ada_all, = _all_gather([ada_cols], "gather_ada", vmem=True)
    ada = lax.dynamic_index_in_dim(ada_all, me, axis=1, keepdims=False).reshape(6, d)

    g_in, = _all_gather([w_in[0].astype(bf16)], "gather_w_in")
    late = [w.astype(bf16) for w in (w_branch_a[0], w_branch_b[0], w_out[0], w_ffn_in[0], w_ffn_out[0])]
    late_h = _exchange_start(late, False, "gather_late_start")

    def mine_into(zone, block):
        return lax.dynamic_update_index_in_dim(zone, block, me, 0)

    def late_weights(after):
        zones = _exchange_wait(late_h, after, "gather_late_wait")
        g_ba, g_bb, g_out, g_fi, g_fo = (mine_into(z, w) for z, w in zip(zones, late))
        return (_cols_from_shards(g_ba), _cols_from_shards(g_bb), g_out.reshape(d, d), _cols_from_shards(g_fi),
                g_fo.reshape(D_FF, d))

    row_sharded = ("w_out", "w_ffn_out")
    in_flight = []

    def on_grads(group):
        sends = [g.reshape(N_DEV, g.shape[0] // N_DEV, g.shape[1]) if nm in row_sharded else _shards_from_cols(g)
                 for nm, g in group.items()]
        handle = _exchange_start(sends, True, "scatter_start_%d" % len(in_flight))
        in_flight.append((list(group), sends, handle))
        return handle["token"]

    grad_x, small = _local_step(
        x[0], positions[0], ada + late_h["token"][0, 0], g_pre_mix[0], g_post_mix[0], b_f[0], sinks[0], g_pre_ffn[0],
        g_post_ffn[0], loss_target[0], _cols_from_shards(g_in), late_weights, on_grads)

    slab_all, = _all_gather([_pack_small(small)], "gather_small", vmem=True)
    small_w = dict(b_ada=b_ada, g_pre_mix=g_pre_mix, g_post_mix=g_post_mix, g_pre_ffn=g_pre_ffn, g_post_ffn=g_post_ffn,
                   b_f=b_f, sinks=sinks, loss=jnp.zeros((1,), f32))
    small_m = dict(b_ada=m_b_ada, g_pre_mix=m_g_pre_mix, g_post_mix=m_g_post_mix, g_pre_ffn=m_g_pre_ffn,
                   g_post_ffn=m_g_post_ffn, b_f=m_b_f, sinks=m_sinks, loss=jnp.zeros((1,), f32))
    small_v = dict(b_ada=v_b_ada, g_pre_mix=v_g_pre_mix, g_post_mix=v_g_post_mix, g_pre_ffn=v_g_pre_ffn,
                   g_post_ffn=v_g_post_ffn, b_f=v_b_f, sinks=v_sinks, loss=jnp.ones((1,), f32))
    shapes = {k: small_w[k].shape for k, _ in _SMALL}
    s_out = _adamw(slab_all, _pack_small(small_w), _pack_small(small_m), _pack_small(small_v), "adamw_small")
    s_grad, s_delta, s_m, s_v = (_unpack_small(o, shapes) for o in s_out)

    d_ada_all = lax.dynamic_slice(slab_all[:, :6144 // LANES, :].reshape(N_DEV, 6144), (0, me * ada_w), (N_DEV, ada_w))
    ada_parts = _ada_wgrad(c_all, d_ada_all, "ada_wgrad")

    ws = dict(w_in=(w_in, m_w_in, v_w_in), w_branch_a=(w_branch_a, m_w_branch_a, v_w_branch_a),
              w_branch_b=(w_branch_b, m_w_branch_b, v_w_branch_b), w_out=(w_out, m_w_out, v_w_out),
              w_ffn_in=(w_ffn_in, m_w_ffn_in, v_w_ffn_in), w_ffn_out=(w_ffn_out, m_w_ffn_out, v_w_ffn_out))
    res = {"w_ada": _adamw(ada_parts, w_ada[0], m_w_ada[0], v_w_ada[0], "adamw_w_ada")}
    after = res["w_ada"][0]
    for gi, (names, sends, handle) in enumerate(in_flight):
        zones = _exchange_wait(handle, after, "scatter_wait_%d" % gi)
        for nm, zone, sent in zip(names, zones, sends):
            w, m, v = ws[nm]
            parts = mine_into(zone, lax.dynamic_index_in_dim(sent, me, 0, keepdims=False))
            res[nm] = _adamw(parts, w[0], m[0], v[0], "adamw_" + nm)
            after = res[nm][0]

    order = ["w_ada", "b_ada", "g_pre_mix", "g_post_mix", "w_in", "b_f", "sinks", "w_branch_a", "w_branch_b", "w_out",
             "g_pre_ffn", "g_post_ffn", "w_ffn_in", "w_ffn_out"]
    outs = [s_grad["loss"].reshape(()), grad_x[None]]
    for which, small_o in enumerate((s_grad, s_delta, s_m, s_v)):
        for nm in order:
            outs.append(res[nm][which][None] if nm in res else small_o[nm])
    return tuple(outs)
```

```python
import functools
import math

import jax
import jax.numpy as jnp
from jax import lax
from jax.experimental import pallas as pl
from jax.experimental.pallas import tpu as pltpu

f32 = jnp.float32
bf16 = jnp.bfloat16

D_MODEL = 1024
HEAD_DIM = 64
N_HEADS = 8
N_PAIRS = 4
QKV_W = 2304
GATE_W = 2048
F_OFF = 2304
IN_W = 4360
WINDOW = 128
ROPE_THETA = 10000.0
RMS_EPS = 1e-6
D_FF = 2816
N_DEV = 8
ADAM_LR, ADAM_B1, ADAM_B2, ADAM_EPS, ADAM_WD, ADAM_STEP = 0.001, 0.9, 0.999, 1e-08, 0.01, 10
NEG = -1e30
LANES = 128
VMEM_LIMIT = 48 * 1024 * 1024
MESH = pl.DeviceIdType.MESH

_NT = (((1,), (1,)), ((), ()))
_TN = (((0,), (0,)), ((), ()))


def _params(n_grid=0):
    sem = ("arbitrary",) * n_grid if n_grid else None
    return pltpu.CompilerParams(dimension_semantics=sem, vmem_limit_bytes=VMEM_LIMIT)


def _row_tile(s, want):
    t = min(s, want)
    assert s % t == 0, (s, t)
    return t


def _col_tile(n):
    for t in (512, 768, 640, 256, 384, 128):
        if n % t == 0:
            return t
    raise ValueError(n)


def _matmul(a, b, mode, out_dtype, name, after=None):
    if mode == "nn":
        (m, k), n = a.shape, b.shape[1]
    elif mode == "nt":
        (m, k), n = a.shape, b.shape[0]
    else:
        (k, m), n = a.shape, b.shape[1]
    tm = _row_tile(m, {"nn": 1024, "nt": 512, "tn": 256}[mode])
    tn = _col_tile(n)
    if mode == "nn":
        a_spec, b_spec, dims = pl.BlockSpec((tm, k), lambda i, j: (i, 0)), pl.BlockSpec((k, tn), lambda i, j: (0, j)), None
    elif mode == "nt":
        a_spec, b_spec, dims = pl.BlockSpec((tm, k), lambda i, j: (i, 0)), pl.BlockSpec((tn, k), lambda i, j: (j, 0)), _NT
    else:
        a_spec, b_spec, dims = pl.BlockSpec((k, tm), lambda i, j: (0, i)), pl.BlockSpec((k, tn), lambda i, j: (0, j)), _TN

    def body(a_ref, b_ref, *rest):
        o_ref = rest[-1]
        av, bv = a_ref[...].astype(bf16), b_ref[...].astype(bf16)
        if dims is None:
            r = jnp.dot(av, bv, preferred_element_type=f32)
        else:
            r = lax.dot_general(av, bv, dims, preferred_element_type=f32)
        o_ref[...] = r.astype(out_dtype)

    extra = [] if after is None else [after]
    return pl.pallas_call(
        body, name=name, grid=(m // tm, n // tn), in_specs=[a_spec, b_spec] + [pl.BlockSpec(memory_space=pl.ANY)] * len(extra),
        out_specs=pl.BlockSpec((tm, tn), lambda i, j: (i, j)),
        out_shape=jax.ShapeDtypeStruct((m, n), out_dtype), compiler_params=_params(2),
    )(a, b, *extra)


def _rstd(v):
    return lax.rsqrt(jnp.mean(v * v, axis=-1, keepdims=True) + RMS_EPS)


def _row_spec(tm, d):
    return pl.BlockSpec((tm, d), lambda i: (i, 0))


def _vec_spec(d, rows=1):
    return pl.BlockSpec((rows, d), lambda i: (0, 0))


def _prenorm(x, g, scale, shift, name):
    s, d = x.shape
    tm = _row_tile(s, 512)

    def body(x_ref, g_ref, sc_ref, sh_ref, h_ref):
        xv = x_ref[...]
        h = (xv * _rstd(xv) * g_ref[...]) * (1.0 + sc_ref[...]) + sh_ref[...]
        h_ref[...] = h.astype(bf16)

    return pl.pallas_call(
        body, name=name, grid=(s // tm,), in_specs=[_row_spec(tm, d)] + [_vec_spec(d)] * 3,
        out_specs=_row_spec(tm, d), out_shape=jax.ShapeDtypeStruct((s, d), bf16), compiler_params=_params(1),
    )(x, g, scale, shift)


def _postnorm_res(x, y, g, gate, name):
    s, d = x.shape
    tm = _row_tile(s, 512)

    def body(x_ref, y_ref, g_ref, gate_ref, o_ref):
        yv = y_ref[...]
        o_ref[...] = x_ref[...] + gate_ref[...] * (yv * _rstd(yv) * g_ref[...])

    return pl.pallas_call(
        body, name=name, grid=(s // tm,), in_specs=[_row_spec(tm, d)] * 2 + [_vec_spec(d)] * 2,
        out_specs=_row_spec(tm, d), out_shape=jax.ShapeDtypeStruct((s, d), f32), compiler_params=_params(1),
    )(x, y, g, gate)


def _loss_head(out, target, name):
    s, d = out.shape
    tm = _row_tile(s, 512)

    def body(o_ref, t_ref, loss_ref, d_ref):
        @pl.when(pl.program_id(0) == 0)
        def _():
            loss_ref[...] = jnp.zeros_like(loss_ref)
        err = o_ref[...] - t_ref[...]
        d_ref[...] = err / d
        loss_ref[...] += 0.5 * jnp.sum(jnp.mean(err * err, axis=-1, keepdims=True), axis=0, keepdims=True)

    return pl.pallas_call(
        body, name=name, grid=(s // tm,), in_specs=[_row_spec(tm, d)] * 2,
        out_specs=[_vec_spec(LANES), _row_spec(tm, d)],
        out_shape=[jax.ShapeDtypeStruct((1, LANES), f32), jax.ShapeDtypeStruct((s, d), f32)], compiler_params=_params(1),
    )(out, target)


def _rms_bwd(u, v, r):
    return r * u - v * (r * r * r) * jnp.mean(u * v, axis=-1, keepdims=True)


def _postnorm_bwd(dres, y, g, gate, name):
    s, d = y.shape
    tm = _row_tile(s, 512)

    def body(dr_ref, y_ref, g_ref, gate_ref, dy_ref, vec_ref):
        @pl.when(pl.program_id(0) == 0)
        def _():
            vec_ref[...] = jnp.zeros_like(vec_ref)
        dr, yv = dr_ref[...], y_ref[...]
        r = _rstd(yv)
        yn = yv * r
        dn = dr * gate_ref[...]
        vec_ref[0:1, :] += jnp.sum(dr * (yn * g_ref[...]), axis=0, keepdims=True)
        vec_ref[1:2, :] += jnp.sum(dn * yn, axis=0, keepdims=True)
        dy_ref[...] = _rms_bwd(dn * g_ref[...], yv, r).astype(bf16)

    return pl.pallas_call(
        body, name=name, grid=(s // tm,), in_specs=[_row_spec(tm, d)] * 2 + [_vec_spec(d)] * 2,
        out_specs=[_row_spec(tm, d), _vec_spec(d, 8)],
        out_shape=[jax.ShapeDtypeStruct((s, d), bf16), jax.ShapeDtypeStruct((8, d), f32)], compiler_params=_params(1),
    )(dres, y, g, gate)


def _prenorm_bwd(dh, x, g, scale, dres, name):
    s, d = x.shape
    tm = _row_tile(s, 512)

    def body(dh_ref, x_ref, g_ref, sc_ref, dr_ref, dx_ref, vec_ref):
        @pl.when(pl.program_id(0) == 0)
        def _():
            vec_ref[...] = jnp.zeros_like(vec_ref)
        dhv, xv = dh_ref[...], x_ref[...]
        r = _rstd(xv)
        xn = xv * r
        dn = dhv * (1.0 + sc_ref[...])
        vec_ref[0:1, :] += jnp.sum(dhv, axis=0, keepdims=True)
        vec_ref[1:2, :] += jnp.sum(dhv * (xn * g_ref[...]), axis=0, keepdims=True)
        vec_ref[2:3, :] += jnp.sum(dn * xn, axis=0, keepdims=True)
        dx_ref[...] = dr_ref[...] + _rms_bwd(dn * g_ref[...], xv, r)

    return pl.pallas_call(
        body, name=name, grid=(s // tm,),
        in_specs=[_row_spec(tm, d)] * 2 + [_vec_spec(d)] * 2 + [_row_spec(tm, d)],
        out_specs=[_row_spec(tm, d), _vec_spec(d, 8)],
        out_shape=[jax.ShapeDtypeStruct((s, d), f32), jax.ShapeDtypeStruct((8, d), f32)], compiler_params=_params(1),
    )(dh, x, g, scale, dres)


def _lane():
    return lax.broadcasted_iota(jnp.int32, (1, LANES), 1)


def _rope_tables(pos_col, inv_freq, name):
    s = pos_col.shape[0]

    def body(p_ref, f_ref, cos_ref, sin_ref):
        ang = p_ref[...].astype(f32) * f_ref[...]
        first_half = (_lane() % HEAD_DIM) < HEAD_DIM // 2
        cos_ref[...] = jnp.cos(ang)
        sn = jnp.sin(ang)
        sin_ref[...] = jnp.where(first_half, -sn, sn)

    return pl.pallas_call(
        body, name=name, out_shape=[jax.ShapeDtypeStruct((s, LANES), f32)] * 2, compiler_params=_params(),
    )(pos_col, inv_freq)


def _swap_halves(v):
    first_half = (_lane() % HEAD_DIM) < HEAD_DIM // 2
    return jnp.where(first_half, pltpu.roll(v, LANES - HEAD_DIM // 2, axis=1), pltpu.roll(v, HEAD_DIM // 2, axis=1))


def _qkv_prep(qkv, cos, sin_s, name):
    s = qkv.shape[0]
    tm = _row_tile(s, 256)
    scale = 1.0 / math.sqrt(HEAD_DIM)

    def body(p_ref, c_ref, s_ref, qa_ref, ka_ref, va_ref, qb_ref, kb_ref, vb_ref):
        cs, sn = c_ref[...], s_ref[...]
        low = _lane() < HEAD_DIM

        def blk(j):
            return p_ref[:, j * LANES:(j + 1) * LANES]

        def rope(v):
            return v * cs + _swap_halves(v) * sn

        def expand(v):
            other = pltpu.roll(v, HEAD_DIM, axis=1)
            return jnp.where(low, v, other), jnp.where(low, other, v)

        for j in range(N_PAIRS):
            qa_ref[:, j * LANES:(j + 1) * LANES] = (rope(blk(j)) * scale).astype(bf16)
            qb_ref[:, j * LANES:(j + 1) * LANES] = (blk(6 + j) * scale).astype(bf16)
            kb_ref[:, j * LANES:(j + 1) * LANES] = blk(10 + j).astype(bf16)
            vb_ref[:, j * LANES:(j + 1) * LANES] = blk(14 + j).astype(bf16)
        k0, k1 = expand(rope(blk(4)))
        v0, v1 = expand(blk(5))
        for j in range(N_PAIRS):
            ka_ref[:, j * LANES:(j + 1) * LANES] = (k0 if j < 2 else k1).astype(bf16)
            va_ref[:, j * LANES:(j + 1) * LANES] = (v0 if j < 2 else v1).astype(bf16)

    hw = N_PAIRS * LANES
    return pl.pallas_call(
        body, name=name, grid=(s // tm,),
        in_specs=[_row_spec(tm, QKV_W), _row_spec(tm, LANES), _row_spec(tm, LANES)],
        out_specs=[_row_spec(tm, hw)] * 6, out_shape=[jax.ShapeDtypeStruct((s, hw), bf16)] * 6, compiler_params=_params(1),
    )(qkv, cos, sin_s)


def _qkv_prep_bwd(dqa_t, dka, dva, dqb_t, dkb, dvb, cos, sin_s, name):
    s = dka.shape[0]
    tm = _row_tile(s, 256)
    scale = 1.0 / math.sqrt(HEAD_DIM)
    hw = N_PAIRS * LANES
    t_spec = pl.BlockSpec((hw, tm), lambda i: (0, i))

    def body(dqa_ref, dka_ref, dva_ref, dqb_ref, dkb_ref, dvb_ref, c_ref, s_ref, o_ref):
        cs, sn = c_ref[...], s_ref[...]
        low = _lane() < HEAD_DIM

        def blk(ref, j):
            return ref[:, j * LANES:(j + 1) * LANES]

        def blk_t(ref, j):
            return ref[j * LANES:(j + 1) * LANES, :].T

        def unrope(v):
            return v * cs + _swap_halves(v * sn)

        def fold(ref):
            a, b = blk(ref, 0) + blk(ref, 1), blk(ref, 2) + blk(ref, 3)
            kv0 = a + pltpu.roll(a, HEAD_DIM, axis=1)
            kv1 = b + pltpu.roll(b, HEAD_DIM, axis=1)
            return jnp.where(low, kv0, kv1)

        for j in range(N_PAIRS):
            o_ref[:, j * LANES:(j + 1) * LANES] = (unrope(blk_t(dqa_ref, j)) * scale).astype(bf16)
            o_ref[:, (6 + j) * LANES:(7 + j) * LANES] = (blk_t(dqb_ref, j) * scale).astype(bf16)
            o_ref[:, (10 + j) * LANES:(11 + j) * LANES] = blk(dkb_ref, j).astype(bf16)
            o_ref[:, (14 + j) * LANES:(15 + j) * LANES] = blk(dvb_ref, j).astype(bf16)
        o_ref[:, 4 * LANES:5 * LANES] = unrope(fold(dka_ref)).astype(bf16)
        o_ref[:, 5 * LANES:6 * LANES] = fold(dva_ref).astype(bf16)

    return pl.pallas_call(
        body, name=name, grid=(s // tm,),
        in_specs=[t_spec, _row_spec(tm, hw), _row_spec(tm, hw), t_spec, _row_spec(tm, hw), _row_spec(tm, hw)] + [_row_spec(tm, LANES)] * 2,
        out_specs=_row_spec(tm, QKV_W), out_shape=jax.ShapeDtypeStruct((s, QKV_W), bf16), compiler_params=_params(1),
    )(dqa_t, dka, dva, dqb_t, dkb, dvb, cos, sin_s)


def _cumsum_rows(v, reverse=False):
    n = v.shape[0]
    row = lax.broadcasted_iota(jnp.int32, v.shape, 0)
    sh = 1
    while sh < n:
        if reverse:
            v = v + jnp.where(row < n - sh, pltpu.roll(v, n - sh, axis=0), 0.0)
        else:
            v = v + jnp.where(row >= sh, pltpu.roll(v, sh, axis=0), 0.0)
        sh *= 2
    return v


def _log_sigmoid(z):
    return jnp.minimum(z, 0.0) - jnp.log1p(jnp.exp(-jnp.abs(z)))


def _forget_prep(fl, bf_row, name):
    s = fl.shape[0]

    def body(f_ref, b_ref, cb_ref):
        cum = _cumsum_rows(_log_sigmoid(f_ref[...] + b_ref[...]))
        for h in range(N_HEADS):
            cb_ref[:, h * LANES:(h + 1) * LANES] = jnp.broadcast_to(cum[:, h:h + 1], (s, LANES))

    return pl.pallas_call(
        body, name=name, out_shape=jax.ShapeDtypeStruct((s, N_HEADS * LANES), f32), compiler_params=_params(),
    )(fl, bf_row)


def _forget_prep_bwd(rs, dcs, fl, bf_row, name):
    s = fl.shape[0]

    def body(r_ref, c_ref, f_ref, b_ref, df_ref, db_ref):
        eye = (lax.broadcasted_iota(jnp.int32, (N_HEADS, LANES), 0) == lax.broadcasted_iota(jnp.int32, (N_HEADS, LANES), 1)).astype(f32)
        dcum = lax.dot_general(r_ref[...], eye, _TN, precision=lax.Precision.HIGHEST, preferred_element_type=f32)
        for h in range(N_HEADS):
            dcum = dcum - jnp.where(_lane() == h, jnp.sum(c_ref[:, h * LANES:(h + 1) * LANES], axis=1, keepdims=True), 0.0)
        dlf = _cumsum_rows(dcum, reverse=True)
        z = f_ref[...] + b_ref[...]
        df = jnp.where(_lane() < N_HEADS, dlf * jax.nn.sigmoid(-z), 0.0)
        df_ref[...] = df.astype(bf16)
        db_ref[...] = jnp.zeros_like(db_ref)
        db_ref[0:1, :] = jnp.sum(df, axis=0, keepdims=True)

    return pl.pallas_call(
        body, name=name,
        out_shape=[jax.ShapeDtypeStruct((s, LANES), bf16), jax.ShapeDtypeStruct((8, LANES), f32)], compiler_params=_params(),
    )(rs, dcs, fl, bf_row)


def _tile_mask(n_keys, n_queries, off, window):
    shape = (n_keys, n_queries)
    d = lax.broadcasted_iota(jnp.int32, shape, 1) - lax.broadcasted_iota(jnp.int32, shape, 0) + off
    valid = d >= 0
    return jnp.logical_and(valid, d < window) if window else valid


def _wide(v, t):
    return jnp.concatenate([v] * (t // LANES), axis=1)


def _attn_fwd(q, k, v, name, *, cum_b=None, sink_rows=None, window=None, t=256):
    s = q.shape[0]
    t = _row_tile(s, t)
    fox, has_sink = cum_b is not None, sink_rows is not None
    assert not window or (window % LANES == 0 and t + window <= s)

    def body(*refs):
        q_ref, k_ref, v_ref = refs[:3]
        rest = list(refs[3:])
        cb_ref = rest.pop(0) if fox else None
        sink_ref = rest.pop(0) if has_sink else None
        o_ref, lse_ref = rest
        i = pl.program_id(1)
        low = _lane() < HEAD_DIM
        q2 = q_ref[...]
        zero = jnp.zeros_like(q2)
        qms = (jnp.where(low, q2, zero), jnp.where(low, zero, q2))

        def tile(k0, n_keys, off, carry, masked):
            kblk, vblk = k_ref[pl.ds(k0, n_keys), :], v_ref[pl.ds(k0, n_keys), :]
            valid = _tile_mask(n_keys, t, off, window) if masked else None
            out = []
            for h in range(2):
                m, l, acc = carry[h]
                sc = lax.dot_general(kblk, qms[h], _NT, preferred_element_type=f32)
                if fox:
                    sc = sc - _wide(cb_ref[pl.ds(k0, n_keys), h * LANES:(h + 1) * LANES], t)
                if masked:
                    sc = jnp.where(valid, sc, NEG)
                m_new = jnp.maximum(m, jnp.max(sc, axis=0, keepdims=True))
                p = jnp.exp(sc - m_new)
                alpha = jnp.exp(m - m_new)
                l = alpha * l + jnp.sum(p, axis=0, keepdims=True)
                acc = alpha * acc + lax.dot_general(vblk, p.astype(bf16), _TN, preferred_element_type=f32)
                out.append((m_new, l, acc))
            return tuple(out)

        init = []
        for h in range(2):
            if has_sink:
                init.append((_wide(sink_ref[h:h + 1, :], t), jnp.ones((1, t), f32), jnp.zeros((LANES, t), f32)))
            else:
                init.append((jnp.full((1, t), NEG, f32), jnp.zeros((1, t), f32), jnp.zeros((LANES, t), f32)))
        carry = tuple(init)
        if window:
            k0 = pl.multiple_of(jnp.maximum(i * t - window, 0), LANES)
            carry = tile(k0, t + window, i * t - k0, carry, True)
        else:
            carry = lax.fori_loop(0, i, lambda kb, c: tile(pl.multiple_of(kb * t, t), t, 0, c, False), carry)
            carry = tile(pl.multiple_of(i * t, t), t, 0, carry, True)
        (m0, l0, a0), (m1, l1, a1) = carry
        top = lax.broadcasted_iota(jnp.int32, (LANES, 1), 0) < HEAD_DIM
        o_t = jnp.where(top, a0 * (1.0 / l0), a1 * (1.0 / l1))
        o_ref[...] = o_t.T.astype(bf16)
        lse_ref[0:1, :] = m0 + jnp.log(l0)
        lse_ref[1:2, :] = m1 + jnp.log(l1)

    q_spec = pl.BlockSpec((t, LANES), lambda j, i: (i, j))
    kv_spec = pl.BlockSpec((s, LANES), lambda j, i: (0, j))
    in_specs, args = [q_spec, kv_spec, kv_spec], [q, k, v]
    if fox:
        in_specs += [pl.BlockSpec((s, 2 * LANES), lambda j, i: (0, j))]
        args += [cum_b]
    if has_sink:
        in_specs += [pl.BlockSpec((None, 2, LANES), lambda j, i: (j, 0, 0))]
        args += [sink_rows.reshape(N_PAIRS, 2, LANES)]
    return pl.pallas_call(
        body, name=name, grid=(N_PAIRS, s // t), in_specs=in_specs,
        out_specs=[q_spec, pl.BlockSpec((None, 2, t), lambda j, i: (j, 0, i))],
        out_shape=[jax.ShapeDtypeStruct((s, N_PAIRS * LANES), bf16), jax.ShapeDtypeStruct((N_PAIRS, 2, s), f32)],
        compiler_params=_params(2),
    )(*args)


def _attn_delta(do, o, name, *, lse=None, sink_rows=None):
    s, hw = do.shape
    tm = _row_tile(s, 512)
    has_sink = sink_rows is not None

    def body(*refs):
        do_ref, o_ref = refs[:2]
        if has_sink:
            lse_ref, sink_ref, dl_ref, ds_ref = refs[2:]

            @pl.when(pl.program_id(0) == 0)
            def _():
                ds_ref[...] = jnp.zeros_like(ds_ref)
        else:
            dl_ref, = refs[2:]
        for j in range(N_PAIRS):
            cols = slice(j * LANES, (j + 1) * LANES)
            prod_t = (do_ref[:, cols].astype(f32) * o_ref[:, cols].astype(f32)).T
            for h in range(2):
                dl = jnp.sum(prod_t[h * HEAD_DIM:(h + 1) * HEAD_DIM, :], axis=0, keepdims=True)
                dl_ref[j, h:h + 1, :] = dl
                if has_sink:
                    r = 2 * j + h
                    p_sink = jnp.exp(sink_ref[r:r + 1, 0:1] - lse_ref[j, h:h + 1, :])
                    ds_ref[r:r + 1, :] += -jnp.sum(p_sink * dl, axis=1, keepdims=True)

    rows_spec = pl.BlockSpec((N_PAIRS, 2, tm), lambda i: (0, 0, i))
    in_specs, args = [_row_spec(tm, hw)] * 2, [do, o]
    out_specs, out_shape = [rows_spec], [jax.ShapeDtypeStruct((N_PAIRS, 2, s), f32)]
    if has_sink:
        in_specs += [rows_spec, _vec_spec(LANES, N_HEADS)]
        args += [lse, sink_rows]
        out_specs += [_vec_spec(LANES, N_HEADS)]
        out_shape += [jax.ShapeDtypeStruct((N_HEADS, LANES), f32)]
    return pl.pallas_call(
        body, name=name, grid=(s // tm,), in_specs=in_specs, out_specs=out_specs, out_shape=out_shape,
        compiler_params=_params(1),
    )(*args)


def _attn_bwd(q, k, v, do, lse, delta, name, *, cum_b=None, window=None, t=256):
    s = q.shape[0]
    t = _row_tile(s, t)
    nblk = s // t
    fox = cum_b is not None
    assert not window or (window % LANES == 0 and t + window <= s)

    def body(*refs):
        k_ref, v_ref, q_ref, do_ref, lse_ref, dl_ref = refs[:6]
        rest = list(refs[6:])
        cb_ref = rest.pop(0) if fox else None
        dq_ref, dk_ref, dv_ref = rest[:3]
        dcs_ref, rs_ref = (rest[3], rest[4]) if fox else (None, None)
        b = pl.program_id(1)
        k0 = pl.multiple_of(b * t, t)

        @pl.when(b == 0)
        def _():
            dq_ref[...] = jnp.zeros_like(dq_ref)
            if fox:
                rs_ref[...] = jnp.zeros_like(rs_ref)

        dk_ref[...] = jnp.zeros_like(dk_ref)
        dv_ref[...] = jnp.zeros_like(dv_ref)
        if fox:
            dcs_ref[...] = jnp.zeros_like(dcs_ref)
        low = _lane() < HEAD_DIM
        top = lax.broadcasted_iota(jnp.int32, (LANES, 1), 0) < HEAD_DIM
        kblk, vblk = k_ref[...], v_ref[...]
        k_t = kblk.astype(f32).T.astype(bf16)
        cks = [_wide(cb_ref[pl.ds(k0, t), h * LANES:(h + 1) * LANES], t) for h in range(2)] if fox else None

        def tile(q0, n_queries, off, masked):
            cols = pl.ds(q0, n_queries)
            q2, do2 = q_ref[cols, :], do_ref[cols, :]
            zero = jnp.zeros_like(q2)
            valid = _tile_mask(t, n_queries, off, window) if masked else None
            dq_parts = []
            for h in range(2):
                qm = jnp.where(low, q2, zero) if h == 0 else jnp.where(low, zero, q2)
                dom = jnp.where(low, do2, zero) if h == 0 else jnp.where(low, zero, do2)
                sc = lax.dot_general(kblk, qm, _NT, preferred_element_type=f32)
                if fox:
                    sc = sc - cks[h]
                if masked:
                    sc = jnp.where(valid, sc, NEG)
                p = jnp.exp(sc - lse_ref[h:h + 1, cols])
                dp = lax.dot_general(vblk, dom, _NT, preferred_element_type=f32)
                ds = p * (dp - dl_ref[h:h + 1, cols])
                pb, dsb = p.astype(bf16), ds.astype(bf16)
                dv_ref[...] += jnp.dot(pb, dom, preferred_element_type=f32)
                dk_ref[...] += jnp.dot(dsb, qm, preferred_element_type=f32)
                dq_parts.append(jnp.dot(k_t, dsb, preferred_element_type=f32))
                if fox:
                    dcs_ref[:, h * LANES:(h + 1) * LANES] += sum(ds[:, g * LANES:(g + 1) * LANES] for g in range(t // LANES))
                    rs_ref[h:h + 1, cols] += jnp.sum(ds, axis=0, keepdims=True)
            dq_ref[:, cols] += jnp.where(top, dq_parts[0], dq_parts[1])

        def later_block(qb, carry):
            tile(pl.multiple_of(qb * t, t), t, 0, False)
            return carry

        if window:
            q0 = pl.multiple_of(jnp.minimum(b * t, s - (t + window)), LANES)
            tile(q0, t + window, q0 - b * t, True)
        else:
            tile(k0, t, 0, True)
            lax.fori_loop(b + 1, nblk, later_block, 0)

    kv_spec = pl.BlockSpec((t, LANES), lambda j, b: (b, j))
    seq_spec = pl.BlockSpec((s, LANES), lambda j, b: (0, j))
    rows_spec = pl.BlockSpec((None, 2, s), lambda j, b: (j, 0, 0))
    hw = N_PAIRS * LANES
    in_specs, args = [kv_spec, kv_spec, seq_spec, seq_spec, rows_spec, rows_spec], [k, v, q, do, lse, delta]
    out_specs = [pl.BlockSpec((LANES, s), lambda j, b: (j, 0)), kv_spec, kv_spec]
    out_shape = [jax.ShapeDtypeStruct((hw, s), f32), jax.ShapeDtypeStruct((s, hw), f32), jax.ShapeDtypeStruct((s, hw), f32)]
    if fox:
        in_specs += [pl.BlockSpec((s, 2 * LANES), lambda j, b: (0, j))]
        args += [cum_b]
        out_specs += [pl.BlockSpec((t, 2 * LANES), lambda j, b: (b, j)), rows_spec]
        out_shape += [jax.ShapeDtypeStruct((s, N_HEADS * LANES), f32), jax.ShapeDtypeStruct((N_PAIRS, 2, s), f32)]
    return pl.pallas_call(
        body, name=name, grid=(N_PAIRS, nblk), in_specs=in_specs, out_specs=out_specs, out_shape=out_shape,
        compiler_params=_params(2),
    )(*args)


def _merge(ba, bb, gl, name):
    s, d = ba.shape
    tm = _row_tile(s, 512)

    def body(a_ref, b_ref, g_ref, o_ref):
        o_ref[...] = (jax.nn.sigmoid(g_ref[:, :d]) * a_ref[...] + jax.nn.sigmoid(g_ref[:, d:]) * b_ref[...]).astype(bf16)

    return pl.pallas_call(
        body, name=name, grid=(s // tm,), in_specs=[_row_spec(tm, d)] * 2 + [_row_spec(tm, 2 * d)],
        out_specs=_row_spec(tm, d), out_shape=jax.ShapeDtypeStruct((s, d), bf16), compiler_params=_params(1),
    )(ba, bb, gl)


def _merge_bwd(dm, ba, bb, gl, name):
    s, d = ba.shape
    tm = _row_tile(s, 512)

    def body(dm_ref, a_ref, b_ref, g_ref, da_ref, db_ref, dg_ref):
        dmv = dm_ref[...]
        g0, g1 = jax.nn.sigmoid(g_ref[:, :d]), jax.nn.sigmoid(g_ref[:, d:])
        da_ref[...] = (dmv * g0).astype(bf16)
        db_ref[...] = (dmv * g1).astype(bf16)
        dg_ref[:, :d] = (dmv * a_ref[...] * (g0 * (1.0 - g0))).astype(bf16)
        dg_ref[:, d:] = (dmv * b_ref[...] * (g1 * (1.0 - g1))).astype(bf16)

    return pl.pallas_call(
        body, name=name, grid=(s // tm,), in_specs=[_row_spec(tm, d)] * 3 + [_row_spec(tm, 2 * d)],
        out_specs=[_row_spec(tm, d)] * 2 + [_row_spec(tm, 2 * d)],
        out_shape=[jax.ShapeDtypeStruct((s, d), bf16)] * 2 + [jax.ShapeDtypeStruct((s, 2 * d), bf16)],
        compiler_params=_params(1),
    )(dm, ba, bb, gl)


def _swiglu(gu, name):
    s, w = gu.shape
    h = w // 2
    tm = _row_tile(s, 256)

    def body(g_ref, o_ref):
        g = g_ref[:, :h]
        o_ref[...] = (g * jax.nn.sigmoid(g) * g_ref[:, h:]).astype(bf16)

    return pl.pallas_call(
        body, name=name, grid=(s // tm,), in_specs=[_row_spec(tm, w)], out_specs=_row_spec(tm, h),
        out_shape=jax.ShapeDtypeStruct((s, h), bf16), compiler_params=_params(1),
    )(gu)


def _swiglu_bwd(dact, gu, name):
    s, w = gu.shape
    h = w // 2
    tm = _row_tile(s, 256)

    def body(d_ref, g_ref, o_ref):
        dv, g, u = d_ref[...], g_ref[:, :h], g_ref[:, h:]
        sg = jax.nn.sigmoid(g)
        o_ref[:, :h] = (dv * u * (sg * (1.0 + g * (1.0 - sg)))).astype(bf16)
        o_ref[:, h:] = (dv * (g * sg)).astype(bf16)

    return pl.pallas_call(
        body, name=name, grid=(s // tm,), in_specs=[_row_spec(tm, h), _row_spec(tm, w)], out_specs=_row_spec(tm, w),
        out_shape=jax.ShapeDtypeStruct((s, w), bf16), compiler_params=_params(1),
    )(dact, gu)


def _ada_fwd(c_all, w, b, name):
    def body(c_ref, w_ref, b_ref, o_ref):
        o_ref[...] = jnp.dot(c_ref[...].astype(bf16), w_ref[...].astype(bf16), preferred_element_type=f32) + b_ref[...]

    return pl.pallas_call(
        body, name=name, out_shape=jax.ShapeDtypeStruct((c_all.shape[0], w.shape[1]), f32), compiler_params=_params(),
    )(c_all, w, b)


def _ada_wgrad(c_all, d_all, name):
    n, d = c_all.shape
    w = d_all.shape[1]

    def body(c_ref, d_ref, o_ref):
        eye = (lax.broadcasted_iota(jnp.int32, (n, n), 0) == lax.broadcasted_iota(jnp.int32, (n, n), 1)).astype(f32)
        ct = lax.dot_general(c_ref[...], eye, _TN, precision=lax.Precision.HIGHEST, preferred_element_type=f32)
        g = ct[:, 0:1] * d_ref[0:1, :]
        for bi in range(1, n):
            g = g + ct[:, bi:bi + 1] * d_ref[bi:bi + 1, :]
        o_ref[0] = g

    return pl.pallas_call(
        body, name=name, out_shape=jax.ShapeDtypeStruct((1, d, w), f32), compiler_params=_params(),
    )(c_all, d_all)


def _adamw(parts, w, m, v, name):
    r, c = w.shape
    n_parts = parts.shape[0]
    tr = next(t for t in range(min(r, 256), 0, -1) if r % t == 0 and (t % 16 == 0 or t == r))

    def body(p_ref, w_ref, m_ref, v_ref, g_ref, d_ref, nm_ref, nv_ref):
        g = p_ref[0].astype(f32)
        for i in range(1, n_parts):
            g = g + p_ref[i].astype(f32)
        mm = ADAM_B1 * m_ref[...] + (1.0 - ADAM_B1) * g
        vv = ADAM_B2 * v_ref[...] + (1.0 - ADAM_B2) * (g * g)
        m_hat = mm / (1.0 - ADAM_B1 ** ADAM_STEP)
        v_hat = vv / (1.0 - ADAM_B2 ** ADAM_STEP)
        g_ref[...] = g
        d_ref[...] = -ADAM_LR * (m_hat / (jnp.sqrt(v_hat) + ADAM_EPS) + ADAM_WD * w_ref[...])
        nm_ref[...] = mm
        nv_ref[...] = vv

    spec = pl.BlockSpec((tr, c), lambda i: (i, 0))
    return pl.pallas_call(
        body, name=name, grid=(r // tr,), in_specs=[pl.BlockSpec((n_parts, tr, c), lambda i: (0, i, 0))] + [spec] * 3,
        out_specs=[spec] * 4, out_shape=[jax.ShapeDtypeStruct((r, c), f32)] * 4, compiler_params=_params(1),
    )(parts, w, m, v)


def _me():
    return lax.axis_index("x"), lax.axis_index("y"), lax.axis_index("c")


def _all_gather(arrays, name, vmem=False):
    n = len(arrays)
    space = pltpu.VMEM if vmem else pl.ANY

    def body(*refs):
        ins, outs = refs[:n], refs[n:2 * n]
        send_sems, recv_sems, local_sems = refs[2 * n:]
        x, y, c = _me()
        me, sibling = (x, y, c), (x, y, 1 - c)
        chips = [(1 - x, y), (x, 1 - y), (1 - x, 1 - y)]

        def rows(a, dev):
            return outs[a].at[4 * dev[0] + 2 * dev[1] + dev[2]]

        def copy(a, k, block, to, src=None):
            return pltpu.make_async_remote_copy(
                src_ref=rows(a, block) if src is None else src, dst_ref=rows(a, block),
                send_sem=send_sems.at[a, k], recv_sem=recv_sems.at[a, k], device_id=to, device_id_type=MESH)

        mine = [pltpu.make_async_copy(ins[a], rows(a, me), local_sems.at[a]) for a in range(n)]
        for cp in mine:
            cp.start()
        first = []
        for a in range(n):
            first.append(copy(a, 0, me, sibling, src=ins[a]))
            first += [copy(a, 1 + j, me, (*chip, c), src=ins[a]) for j, chip in enumerate(chips)]
        for cp in first:
            cp.start()
        passed = []
        for j, chip in enumerate(chips):
            for a in range(n):
                copy(a, 1 + j, (*chip, c), me).wait_recv()
                fwd = copy(a, 4 + j, (*chip, c), sibling)
                fwd.start()
                passed.append(fwd)
        for a in range(n):
            copy(a, 0, sibling, me).wait_recv()
            for j, chip in enumerate(chips):
                copy(a, 4 + j, (*chip, 1 - c), me).wait_recv()
        for cp in first + passed:
            cp.wait_send()
        for cp in mine:
            cp.wait()

    outs = pl.pallas_call(
        body, name=name,
        in_specs=[pl.BlockSpec(memory_space=space)] * n, out_specs=[pl.BlockSpec(memory_space=space)] * n,
        out_shape=[jax.ShapeDtypeStruct((N_DEV,) + a.shape, a.dtype) for a in arrays],
        scratch_shapes=[pltpu.SemaphoreType.DMA((n, 7)), pltpu.SemaphoreType.DMA((n, 7)), pltpu.SemaphoreType.DMA((n,))],
        compiler_params=pltpu.CompilerParams(vmem_limit_bytes=VMEM_LIMIT),
    )(*arrays)
    return list(outs)


_FLIPS = ((0, 0, 1), (1, 0, 0), (0, 1, 0), (1, 1, 0), (1, 0, 1), (0, 1, 1), (1, 1, 1))
_HBM = pl.BlockSpec(memory_space=pltpu.HBM)
_SEM = pl.BlockSpec(memory_space=pltpu.SEMAPHORE)


def _exchange_copies(scatter, srcs, lands, send_sems, recv_sems):
    x, y, c = _me()
    me_row = 4 * x + 2 * y + c
    out = []
    for k, (fx, fy, fc) in enumerate(_FLIPS):
        peer = (x ^ fx, y ^ fy, c ^ fc)
        peer_row = 4 * peer[0] + 2 * peer[1] + peer[2]
        for a in range(len(srcs)):
            out.append(pltpu.make_async_remote_copy(
                src_ref=srcs[a].at[peer_row] if scatter else srcs[a], dst_ref=lands[a].at[me_row],
                send_sem=send_sems.at[7 * a + k], recv_sem=recv_sems.at[7 * a + k], device_id=peer, device_id_type=MESH))
    return out


def _exchange_start(arrays, scatter, name):
    n = len(arrays)
    lands = [lax.empty(a.shape if scatter else (N_DEV,) + a.shape, a.dtype) for a in arrays]

    def body(*refs):
        srcs, zones = refs[:n], refs[n:2 * n]
        send_sems, recv_sems = refs[2 * n], refs[2 * n + 1]
        token = refs[-1]
        for cp in _exchange_copies(scatter, srcs, zones, send_sems, recv_sems):
            cp.start()
        token[...] = jnp.zeros_like(token)

    thru = [pltpu.HBM(a.shape, a.dtype) for a in list(arrays) + lands]
    outs = pl.pallas_call(
        body, name=name,
        out_shape=(pltpu.SemaphoreType.DMA((7 * n,)), pltpu.SemaphoreType.DMA((7 * n,)), *thru, jax.ShapeDtypeStruct((8, LANES), f32)),
        in_specs=[_HBM] * (2 * n), out_specs=(_SEM, _SEM, *[_HBM] * (2 * n), pl.BlockSpec(memory_space=pltpu.VMEM)),
        input_output_aliases={i: 2 + i for i in range(2 * n)},
        compiler_params=pltpu.CompilerParams(has_side_effects=pltpu.SideEffectType.DATAFLOW_SIDE_EFFECTING),
    )(*[pltpu.with_memory_space_constraint(a, pltpu.HBM) for a in list(arrays) + lands])
    return dict(n=n, scatter=scatter, sems=outs[:2], srcs=outs[2:2 + n], lands=outs[2 + n:2 + 2 * n], token=outs[-1])


def _exchange_wait(handle, after, name):
    n, scatter = handle["n"], handle["scatter"]

    def body(*refs):
        srcs, zones = refs[:n], refs[n:2 * n]
        send_sems, recv_sems = refs[2 * n], refs[2 * n + 1]
        for cp in _exchange_copies(scatter, srcs, zones, send_sems, recv_sems):
            cp.wait_send()
            cp.wait_recv()

    thru = [pltpu.HBM(a.shape, a.dtype) for a in list(handle["srcs"]) + list(handle["lands"])]
    outs = pl.pallas_call(
        body, name=name, out_shape=tuple(thru),
        in_specs=[_HBM] * (2 * n) + [_SEM, _SEM, pl.BlockSpec(memory_space=pl.ANY)], out_specs=tuple([_HBM] * (2 * n)),
        input_output_aliases={i: i for i in range(2 * n)},
        compiler_params=pltpu.CompilerParams(has_side_effects=pltpu.SideEffectType.DATAFLOW_SIDE_EFFECTING),
    )(*handle["srcs"], *handle["lands"], *handle["sems"], after)
    return list(outs[n:])


def _cols_from_shards(g):
    return jnp.transpose(g, (1, 0, 2)).reshape(g.shape[1], -1)


def _shards_from_cols(a):
    return jnp.transpose(a.reshape(a.shape[0], N_DEV, -1), (1, 0, 2))


def _local_step(x, positions, ada, g_pre_mix, g_post_mix, b_f, sinks, g_pre_ffn, g_post_ffn, target,
                w_in, late_weights, on_grads):
    s, d = x.shape
    row = lambda v: v.reshape(1, -1)
    shift_m, scale_m, gate_m, shift_f, scale_f, gate_f = (ada[i:i + 1] for i in range(6))
    w_gate, w_qkv = w_in[:, F_OFF + N_HEADS:], w_in[:, :QKV_W]
    w_f = jnp.pad(w_in[:, F_OFF:F_OFF + N_HEADS], ((0, 0), (0, LANES - N_HEADS)))
    w_in_p = jnp.concatenate([w_gate, w_qkv, w_f], axis=1)
    bf_row = jnp.pad(row(b_f), ((0, 0), (0, LANES - N_HEADS)))
    sink_rows = jnp.broadcast_to(sinks.reshape(N_HEADS, 1).astype(f32), (N_HEADS, LANES))
    inv_freq = 1.0 / (ROPE_THETA ** (jnp.arange(0, HEAD_DIM, 2, dtype=f32) / HEAD_DIM))
    cos, sin_s = _rope_tables(positions.reshape(s, 1), jnp.tile(inv_freq, 4).reshape(1, LANES), "rope_tables")

    h1 = _prenorm(x, row(g_pre_mix), scale_m, shift_m, "prenorm_mix")
    gl = _matmul(h1, w_gate, "nn", f32, "proj_gate")
    qkv = _matmul(h1, w_qkv, "nn", f32, "proj_qkv")
    fl = _matmul(h1, w_f, "nn", f32, "proj_forget")
    qa, ka, va, qb, kb, vb = _qkv_prep(qkv, cos, sin_s, "qkv_prep")
    cum_b = _forget_prep(fl, bf_row, "forget_prep")
    o_a, lse_a = _attn_fwd(qa, ka, va, "swa_fwd", sink_rows=sink_rows, window=WINDOW, t=512)
    o_b, lse_b = _attn_fwd(qb, kb, vb, "fox_fwd", cum_b=cum_b, t=512)
    w_branch_a, w_branch_b, w_out, w_ffn_in, w_ffn_out = late_weights(o_b)
    ba = _matmul(o_a, w_branch_a, "nn", f32, "branch_a")
    bb = _matmul(o_b, w_branch_b, "nn", f32, "branch_b")
    merged = _merge(ba, bb, gl, "merge")
    y1 = _matmul(merged, w_out, "nn", f32, "out_proj")
    x2 = _postnorm_res(x, y1, row(g_post_mix), gate_m, "postnorm_mix")

    h2 = _prenorm(x2, row(g_pre_ffn), scale_f, shift_f, "prenorm_ffn")
    gu = _matmul(h2, w_ffn_in, "nn", f32, "ffn_in")
    act = _swiglu(gu, "swiglu")
    y2 = _matmul(act, w_ffn_out, "nn", f32, "ffn_out")
    out = _postnorm_res(x2, y2, row(g_post_ffn), gate_f, "postnorm_ffn")
    loss_row, d_out = _loss_head(out, target, "loss_head")

    d_y2, vec_pf = _postnorm_bwd(d_out, y2, row(g_post_ffn), gate_f, "postnorm_ffn_bwd")
    g_w_ffn_out = _matmul(act, d_y2, "tn", bf16, "ffn_out_wgrad")
    d_act = _matmul(d_y2, w_ffn_out, "nt", f32, "ffn_out_dgrad")
    dgu = _swiglu_bwd(d_act, gu, "swiglu_bwd")
    g_w_ffn_in = _matmul(h2, dgu, "tn", bf16, "ffn_in_wgrad")
    sent = on_grads(dict(w_ffn_in=g_w_ffn_in, w_ffn_out=g_w_ffn_out))
    d_h2 = _matmul(dgu, w_ffn_in, "nt", f32, "ffn_in_dgrad", after=sent)
    d_x2, vec_nf = _prenorm_bwd(d_h2, x2, row(g_pre_ffn), scale_f, d_out, "prenorm_ffn_bwd")

    d_y1, vec_pm = _postnorm_bwd(d_x2, y1, row(g_post_mix), gate_m, "postnorm_mix_bwd")
    g_w_out = _matmul(merged, d_y1, "tn", bf16, "out_proj_wgrad")
    d_merged = _matmul(d_y1, w_out, "nt", f32, "out_proj_dgrad")
    d_ba, d_bb, dgl = _merge_bwd(d_merged, ba, bb, gl, "merge_bwd")
    g_w_branch_a = _matmul(o_a, d_ba, "tn", bf16, "branch_a_wgrad")
    g_w_branch_b = _matmul(o_b, d_bb, "tn", bf16, "branch_b_wgrad")
    sent = on_grads(dict(w_out=g_w_out, w_branch_a=g_w_branch_a, w_branch_b=g_w_branch_b))
    d_oa = _matmul(d_ba, w_branch_a, "nt", bf16, "branch_a_dgrad", after=sent)
    d_ob = _matmul(d_bb, w_branch_b, "nt", bf16, "branch_b_dgrad", after=sent)
    delta_a, d_sink = _attn_delta(d_oa, o_a, "swa_delta", lse=lse_a, sink_rows=sink_rows)
    delta_b, = _attn_delta(d_ob, o_b, "fox_delta")
    dqa_t, dka, dva = _attn_bwd(qa, ka, va, d_oa, lse_a, delta_a, "swa_bwd", window=WINDOW, t=512)
    dqb_t, dkb, dvb, dcs, rs = _attn_bwd(qb, kb, vb, d_ob, lse_b, delta_b, "fox_bwd", cum_b=cum_b, t=512)
    dqkv = _qkv_prep_bwd(dqa_t, dka, dva, dqb_t, dkb, dvb, cos, sin_s, "qkv_prep_bwd")
    dfl, vec_bf = _forget_prep_bwd(rs.reshape(N_HEADS, s), dcs, fl, bf_row, "forget_prep_bwd")
    dproj = jnp.concatenate([dgl, dqkv, dfl], axis=1)
    g_w_in_p = _matmul(h1, dproj, "tn", bf16, "in_proj_wgrad")
    g_w_in = jnp.concatenate([g_w_in_p[:, GATE_W:GATE_W + QKV_W], g_w_in_p[:, GATE_W + QKV_W:GATE_W + QKV_W + N_HEADS],
                              g_w_in_p[:, :GATE_W]], axis=1)
    sent = on_grads(dict(w_in=g_w_in))
    d_h1 = _matmul(dproj, w_in_p, "nt", f32, "in_proj_dgrad", after=sent)
    grad_x, vec_nm = _prenorm_bwd(d_h1, x, row(g_pre_mix), scale_m, d_x2, "prenorm_mix_bwd")

    d_ada = jnp.concatenate([vec_nm[0], vec_nm[1], vec_pm[0], vec_nf[0], vec_nf[1], vec_pf[0]])
    small = dict(b_ada=d_ada, g_pre_mix=vec_nm[2], g_post_mix=vec_pm[1], g_pre_ffn=vec_nf[2], g_post_ffn=vec_pf[1],
                 b_f=vec_bf[0, :N_HEADS], sinks=d_sink[:, 0], loss=loss_row[0, :1])
    return grad_x, small


_SMALL = (("b_ada", 6144), ("g_pre_mix", 1024), ("g_post_mix", 1024), ("g_pre_ffn", 1024), ("g_post_ffn", 1024),
          ("b_f", 128), ("sinks", 128), ("loss", 128))
_SMALL_ROWS = 88


def _pack_small(vals):
    parts = [jnp.pad(vals[k].reshape(-1).astype(f32), (0, n - vals[k].size)) for k, n in _SMALL]
    flat = jnp.concatenate(parts)
    return jnp.pad(flat, (0, _SMALL_ROWS * LANES - flat.size)).reshape(_SMALL_ROWS, LANES)


def _unpack_small(slab, shapes):
    flat, out, off = slab.reshape(-1), {}, 0
    for k, n in _SMALL:
        size = math.prod(shapes[k])
        out[k] = flat[off:off + size].reshape(shapes[k])
        off += n
    return out


def kernel(x, c, positions, w_ada, b_ada, g_pre_mix, g_post_mix, w_in, b_f, sinks, w_branch_a, w_branch_b, w_out, g_pre_ffn, g_post_ffn, w_ffn_in, w_ffn_out, loss_target, m_w_ada, m_b_ada, m_g_pre_mix, m_g_post_mix, m_w_in, m_b_f, m_sinks, m_w_branch_a, m_w_branch_b, m_w_out, m_g_pre_ffn, m_g_post_ffn, m_w_ffn_in, m_w_ffn_out, v_w_ada, v_b_ada, v_g_pre_mix, v_g_post_mix, v_w_in, v_b_f, v_sinks, v_w_branch_a, v_w_branch_b, v_w_out, v_g_pre_ffn, v_g_post_ffn, v_w_ffn_in, v_w_ffn_out):
    xi, yi, ci = _me()
    me = 4 * xi + 2 * yi + ci
    d = D_MODEL
    ada_w = w_ada.shape[2]

    c_all, = _all_gather([c], "gather_c", vmem=True)
    c_all = c_all.reshape(N_DEV, d)
    b_mine = lax.dynamic_slice(b_ada, (0, me * ada_w), (1, ada_w))
    ada_cols = _ada_fwd(c_all, w_ada[0], b_mine, "ada_fwd")
    ada_all, = _all_gather([ada_cols], "gather_ada", vmem=True)
    ada = lax.dynamic_index_in_dim(ada_all, me, axis=1, keepdims=False).reshape(6, d)

    g_in, = _all_gather([w_in[0].astype(bf16)], "gather_w_in")
    late = [w.astype(bf16) for w in (w_branch_a[0], w_branch_b[0], w_out[0], w_ffn_in[0], w_ffn_out[0])]
    late_h = _exchange_start(late, False, "gather_late_start")

    def mine_into(zone, block):
        return lax.dynamic_update_index_in_dim(zone, block, me, 0)

    def late_weights(after):
        zones = _exchange_wait(late_h, after, "gather_late_wait")
        g_ba, g_bb, g_out, g_fi, g_fo = (mine_into(z, w) for z, w in zip(zones, late))
        return (_cols_from_shards(g_ba), _cols_from_shards(g_bb), g_out.reshape(d, d), _cols_from_shards(g_fi),
                g_fo.reshape(D_FF, d))

    row_sharded = ("w_out", "w_ffn_out")
    in_flight = []

    def on_grads(group):
        sends = [g.reshape(N_DEV, g.shape[0] // N_DEV, g.shape[1]) if nm in row_sharded else _shards_from_cols(g)
                 for nm, g in group.items()]
        handle = _exchange_start(sends, True, "scatter_start_%d" % len(in_flight))
        in_flight.append((list(group), sends, handle))
        return handle["token"]

    grad_x, small = _local_step(
        x[0], positions[0], ada + late_h["token"][0, 0], g_pre_mix[0], g_post_mix[0], b_f[0], sinks[0], g_pre_ffn[0],
        g_post_ffn[0], loss_target[0], _cols_from_shards(g_in), late_weights, on_grads)

    slab_all, = _all_gather([_pack_small(small)], "gather_small", vmem=True)
    small_w = dict(b_ada=b_ada, g_pre_mix=g_pre_mix, g_post_mix=g_post_mix, g_pre_ffn=g_pre_ffn, g_post_ffn=g_post_ffn,
                   b_f=b_f, sinks=sinks, loss=jnp.zeros((1,), f32))
    small_m = dict(b_ada=m_b_ada, g_pre_mix=m_g_pre_mix, g_post_mix=m_g_post_mix, g_pre_ffn=m_g_pre_ffn,
                   g_post_ffn=m_g_post_ffn, b_f=m_b_f, sinks=m_sinks, loss=jnp.zeros((1,), f32))
    small_v = dict(b_ada=v_b_ada, g_pre_mix=v_g_pre_mix, g_post_mix=v_g_post_mix, g_pre_ffn=v_g_pre_ffn,
                   g_post_ffn=v_g_post_ffn, b_f=v_b_f, sinks=v_sinks, loss=jnp.ones((1,), f32))
    shapes = {k: small_w[k].shape for k, _ in _SMALL}
    s_out = _adamw(slab_all, _pack_small(small_w), _pack_small(small_m), _pack_small(small_v), "adamw_small")
    s_grad, s_delta, s_m, s_v = (_unpack_small(o, shapes) for o in s_out)

    d_ada_all = lax.dynamic_slice(slab_all[:, :6144 // LANES, :].reshape(N_DEV, 6144), (0, me * ada_w), (N_DEV, ada_w))
    ada_parts = _ada_wgrad(c_all, d_ada_all, "ada_wgrad")

    ws = dict(w_in=(w_in, m_w_in, v_w_in), w_branch_a=(w_branch_a, m_w_branch_a, v_w_branch_a),
              w_branch_b=(w_branch_b, m_w_branch_b, v_w_branch_b), w_out=(w_out, m_w_out, v_w_out),
              w_ffn_in=(w_ffn_in, m_w_ffn_in, v_w_ffn_in), w_ffn_out=(w_ffn_out, m_w_ffn_out, v_w_ffn_out))
    res = {"w_ada": _adamw(ada_parts, w_ada[0], m_w_ada[0], v_w_ada[0], "adamw_w_ada")}
    after = res["w_ada"][0]
    for gi, (names, sends, handle) in enumerate(in_flight):
        zones = _exchange_wait(handle, after, "scatter_wait_%d" % gi)
        for nm, zone, sent in zip(names, zones, sends):
            w, m, v = ws[nm]
            parts = mine_into(zone, lax.dynamic_index_in_dim(sent, me, 0, keepdims=False))
            res[nm] = _adamw(parts, w[0], m[0], v[0], "adamw_" + nm)
            after = res[nm][0]

    order = ["w_ada", "b_ada", "g_pre_mix", "g_post_mix", "w_in", "b_f", "sinks", "w_branch_a", "w_branch_b", "w_out",
             "g_pre_ffn", "g_post_ffn", "w_ffn_in", "w_ffn_out"]
    outs = [s_grad["loss"].reshape(()), grad_x[None]]
    for which, small_o in enumerate((s_grad, s_delta, s_m, s_v)):
        for nm in order:
            outs.append(res[nm][which][None] if nm in res else small_o[nm])
    return tuple(outs)
```

```python
import functools
import math

import jax
import jax.numpy as jnp
from jax import lax
from jax.experimental import pallas as pl
from jax.experimental.pallas import tpu as pltpu

f32 = jnp.float32
bf16 = jnp.bfloat16

D_MODEL = 1024
HEAD_DIM = 64
N_HEADS = 8
N_PAIRS = 4
QKV_W = 2304
GATE_W = 2048
F_OFF = 2304
IN_W = 4360
WINDOW = 128
ROPE_THETA = 10000.0
RMS_EPS = 1e-6
D_FF = 2816
N_DEV = 8
ADAM_LR, ADAM_B1, ADAM_B2, ADAM_EPS, ADAM_WD, ADAM_STEP = 0.001, 0.9, 0.999, 1e-08, 0.01, 10
NEG = -1e30
LANES = 128
VMEM_LIMIT = 48 * 1024 * 1024
MESH = pl.DeviceIdType.MESH

_NT = (((1,), (1,)), ((), ()))
_TN = (((0,), (0,)), ((), ()))


def _params(n_grid=0):
    sem = ("arbitrary",) * n_grid if n_grid else None
    return pltpu.CompilerParams(dimension_semantics=sem, vmem_limit_bytes=VMEM_LIMIT)


def _row_tile(s, want):
    t = min(s, want)
    assert s % t == 0, (s, t)
    return t


MATMUL_VMEM_BUDGET = 40 * 1024 * 1024


def _matmul_tiles(m, n, k, a_item, b_item, o_item):
    def tiles(d):
        return [t for t in range(LANES, min(d, 2048) + 1, LANES) if d % t == 0] or [d]

    best = None
    for tm in tiles(m):
        for tn in tiles(n):
            vmem = 2 * (tm * k * a_item + tn * k * b_item + tm * tn * o_item) + tm * tn * 4
            if vmem > MATMUL_VMEM_BUDGET:
                continue
            traffic = m * k * a_item + n * k * b_item * (1 if tn == n else m // tm) + m * n * o_item
            steps = (m // tm) * (n // tn)
            key = (traffic, 0, steps) if steps >= 4 else (traffic, 1, -steps)
            if best is None or key < best[0]:
                best = (key, tm, tn)
    assert best is not None, (m, n, k)
    return best[1], best[2]


def _matmul(a, b, mode, out_dtype, name, after=None):
    if mode == "nn":
        (m, k), n = a.shape, b.shape[1]
    elif mode == "nt":
        (m, k), n = a.shape, b.shape[0]
    else:
        (k, m), n = a.shape, b.shape[1]
    tm, tn = _matmul_tiles(m, n, k, a.dtype.itemsize, b.dtype.itemsize, jnp.dtype(out_dtype).itemsize)
    if mode == "nn":
        a_spec, b_spec, dims = pl.BlockSpec((tm, k), lambda i, j: (i, 0)), pl.BlockSpec((k, tn), lambda i, j: (0, j)), None
    elif mode == "nt":
        a_spec, b_spec, dims = pl.BlockSpec((tm, k), lambda i, j: (i, 0)), pl.BlockSpec((tn, k), lambda i, j: (j, 0)), _NT
    else:
        a_spec, b_spec, dims = pl.BlockSpec((k, tm), lambda i, j: (0, i)), pl.BlockSpec((k, tn), lambda i, j: (0, j)), _TN

    def body(a_ref, b_ref, *rest):
        o_ref = rest[-1]
        av, bv = a_ref[...].astype(bf16), b_ref[...].astype(bf16)
        if dims is None:
            r = jnp.dot(av, bv, preferred_element_type=f32)
        else:
            r = lax.dot_general(av, bv, dims, preferred_element_type=f32)
        o_ref[...] = r.astype(out_dtype)

    extra = [] if after is None else [after]
    return pl.pallas_call(
        body, name=name, grid=(m // tm, n // tn), in_specs=[a_spec, b_spec] + [pl.BlockSpec(memory_space=pl.ANY)] * len(extra),
        out_specs=pl.BlockSpec((tm, tn), lambda i, j: (i, j)),
        out_shape=jax.ShapeDtypeStruct((m, n), out_dtype), compiler_params=_params(2),
    )(a, b, *extra)


def _rstd(v):
    return lax.rsqrt(jnp.mean(v * v, axis=-1, keepdims=True) + RMS_EPS)


def _row_spec(tm, d):
    return pl.BlockSpec((tm, d), lambda i: (i, 0))


def _vec_spec(d, rows=1):
    return pl.BlockSpec((rows, d), lambda i: (0, 0))


def _prenorm(x, g, scale, shift, name):
    s, d = x.shape
    tm = _row_tile(s, 512)

    def body(x_ref, g_ref, sc_ref, sh_ref, h_ref):
        xv = x_ref[...]
        h = (xv * _rstd(xv) * g_ref[...]) * (1.0 + sc_ref[...]) + sh_ref[...]
        h_ref[...] = h.astype(bf16)

    return pl.pallas_call(
        body, name=name, grid=(s // tm,), in_specs=[_row_spec(tm, d)] + [_vec_spec(d)] * 3,
        out_specs=_row_spec(tm, d), out_shape=jax.ShapeDtypeStruct((s, d), bf16), compiler_params=_params(1),
    )(x, g, scale, shift)


def _postnorm_res(x, y, g, gate, name):
    s, d = x.shape
    tm = _row_tile(s, 512)

    def body(x_ref, y_ref, g_ref, gate_ref, o_ref):
        yv = y_ref[...]
        o_ref[...] = x_ref[...] + gate_ref[...] * (yv * _rstd(yv) * g_ref[...])

    return pl.pallas_call(
        body, name=name, grid=(s // tm,), in_specs=[_row_spec(tm, d)] * 2 + [_vec_spec(d)] * 2,
        out_specs=_row_spec(tm, d), out_shape=jax.ShapeDtypeStruct((s, d), f32), compiler_params=_params(1),
    )(x, y, g, gate)


def _loss_head(out, target, name):
    s, d = out.shape
    tm = _row_tile(s, 512)

    def body(o_ref, t_ref, loss_ref, d_ref):
        @pl.when(pl.program_id(0) == 0)
        def _():
            loss_ref[...] = jnp.zeros_like(loss_ref)
        err = o_ref[...] - t_ref[...]
        d_ref[...] = err / d
        loss_ref[...] += 0.5 * jnp.sum(jnp.mean(err * err, axis=-1, keepdims=True), axis=0, keepdims=True)

    return pl.pallas_call(
        body, name=name, grid=(s // tm,), in_specs=[_row_spec(tm, d)] * 2,
        out_specs=[_vec_spec(LANES), _row_spec(tm, d)],
        out_shape=[jax.ShapeDtypeStruct((1, LANES), f32), jax.ShapeDtypeStruct((s, d), f32)], compiler_params=_params(1),
    )(out, target)


def _rms_bwd(u, v, r):
    return r * u - v * (r * r * r) * jnp.mean(u * v, axis=-1, keepdims=True)


def _postnorm_bwd(dres, y, g, gate, name):
    s, d = y.shape
    tm = _row_tile(s, 512)

    def body(dr_ref, y_ref, g_ref, gate_ref, dy_ref, vec_ref):
        @pl.when(pl.program_id(0) == 0)
        def _():
            vec_ref[...] = jnp.zeros_like(vec_ref)
        dr, yv = dr_ref[...], y_ref[...]
        r = _rstd(yv)
        yn = yv * r
        dn = dr * gate_ref[...]
        vec_ref[0:1, :] += jnp.sum(dr * (yn * g_ref[...]), axis=0, keepdims=True)
        vec_ref[1:2, :] += jnp.sum(dn * yn, axis=0, keepdims=True)
        dy_ref[...] = _rms_bwd(dn * g_ref[...], yv, r).astype(bf16)

    return pl.pallas_call(
        body, name=name, grid=(s // tm,), in_specs=[_row_spec(tm, d)] * 2 + [_vec_spec(d)] * 2,
        out_specs=[_row_spec(tm, d), _vec_spec(d, 8)],
        out_shape=[jax.ShapeDtypeStruct((s, d), bf16), jax.ShapeDtypeStruct((8, d), f32)], compiler_params=_params(1),
    )(dres, y, g, gate)


def _prenorm_bwd(dh, x, g, scale, dres, name):
    s, d = x.shape
    tm = _row_tile(s, 512)

    def body(dh_ref, x_ref, g_ref, sc_ref, dr_ref, dx_ref, vec_ref):
        @pl.when(pl.program_id(0) == 0)
        def _():
            vec_ref[...] = jnp.zeros_like(vec_ref)
        dhv, xv = dh_ref[...], x_ref[...]
        r = _rstd(xv)
        xn = xv * r
        dn = dhv * (1.0 + sc_ref[...])
        vec_ref[0:1, :] += jnp.sum(dhv, axis=0, keepdims=True)
        vec_ref[1:2, :] += jnp.sum(dhv * (xn * g_ref[...]), axis=0, keepdims=True)
        vec_ref[2:3, :] += jnp.sum(dn * xn, axis=0, keepdims=True)
        dx_ref[...] = dr_ref[...] + _rms_bwd(dn * g_ref[...], xv, r)

    return pl.pallas_call(
        body, name=name, grid=(s // tm,),
        in_specs=[_row_spec(tm, d)] * 2 + [_vec_spec(d)] * 2 + [_row_spec(tm, d)],
        out_specs=[_row_spec(tm, d), _vec_spec(d, 8)],
        out_shape=[jax.ShapeDtypeStruct((s, d), f32), jax.ShapeDtypeStruct((8, d), f32)], compiler_params=_params(1),
    )(dh, x, g, scale, dres)


def _lane():
    return lax.broadcasted_iota(jnp.int32, (1, LANES), 1)


def _rope_tables(pos_col, inv_freq, name):
    s = pos_col.shape[0]

    def body(p_ref, f_ref, cos_ref, sin_ref):
        ang = p_ref[...].astype(f32) * f_ref[...]
        first_half = (_lane() % HEAD_DIM) < HEAD_DIM // 2
        cos_ref[...] = jnp.cos(ang)
        sn = jnp.sin(ang)
        sin_ref[...] = jnp.where(first_half, -sn, sn)

    return pl.pallas_call(
        body, name=name, out_shape=[jax.ShapeDtypeStruct((s, LANES), f32)] * 2, compiler_params=_params(),
    )(pos_col, inv_freq)


def _swap_halves(v):
    first_half = (_lane() % HEAD_DIM) < HEAD_DIM // 2
    return jnp.where(first_half, pltpu.roll(v, LANES - HEAD_DIM // 2, axis=1), pltpu.roll(v, HEAD_DIM // 2, axis=1))


def _qkv_prep(qkv, cos, sin_s, name):
    s = qkv.shape[0]
    tm = _row_tile(s, 256)
    scale = 1.0 / math.sqrt(HEAD_DIM)

    def body(p_ref, c_ref, s_ref, qa_ref, ka_ref, va_ref, qb_ref, kb_ref, vb_ref):
        cs, sn = c_ref[...], s_ref[...]
        low = _lane() < HEAD_DIM

        def blk(j):
            return p_ref[:, j * LANES:(j + 1) * LANES]

        def rope(v):
            return v * cs + _swap_halves(v) * sn

        def expand(v):
            other = pltpu.roll(v, HEAD_DIM, axis=1)
            return jnp.where(low, v, other), jnp.where(low, other, v)

        for j in range(N_PAIRS):
            qa_ref[:, j * LANES:(j + 1) * LANES] = (rope(blk(j)) * scale).astype(bf16)
            qb_ref[:, j * LANES:(j + 1) * LANES] = (blk(6 + j) * scale).astype(bf16)
            kb_ref[:, j * LANES:(j + 1) * LANES] = blk(10 + j).astype(bf16)
            vb_ref[:, j * LANES:(j + 1) * LANES] = blk(14 + j).astype(bf16)
        k0, k1 = expand(rope(blk(4)))
        v0, v1 = expand(blk(5))
        for j in range(N_PAIRS):
            ka_ref[:, j * LANES:(j + 1) * LANES] = (k0 if j < 2 else k1).astype(bf16)
            va_ref[:, j * LANES:(j + 1) * LANES] = (v0 if j < 2 else v1).astype(bf16)

    hw = N_PAIRS * LANES
    return pl.pallas_call(
        body, name=name, grid=(s // tm,),
        in_specs=[_row_spec(tm, QKV_W), _row_spec(tm, LANES), _row_spec(tm, LANES)],
        out_specs=[_row_spec(tm, hw)] * 6, out_shape=[jax.ShapeDtypeStruct((s, hw), bf16)] * 6, compiler_params=_params(1),
    )(qkv, cos, sin_s)


def _qkv_prep_bwd(dqa_t, dka, dva, dqb_t, dkb, dvb, cos, sin_s, name):
    s = dka.shape[0]
    tm = _row_tile(s, 256)
    scale = 1.0 / math.sqrt(HEAD_DIM)
    hw = N_PAIRS * LANES
    t_spec = pl.BlockSpec((hw, tm), lambda i: (0, i))

    def body(dqa_ref, dka_ref, dva_ref, dqb_ref, dkb_ref, dvb_ref, c_ref, s_ref, o_ref):
        cs, sn = c_ref[...], s_ref[...]
        low = _lane() < HEAD_DIM

        def blk(ref, j):
            return ref[:, j * LANES:(j + 1) * LANES]

        def blk_t(ref, j):
            return ref[j * LANES:(j + 1) * LANES, :].T

        def unrope(v):
            return v * cs + _swap_halves(v * sn)

        def fold(ref):
            a, b = blk(ref, 0) + blk(ref, 1), blk(ref, 2) + blk(ref, 3)
            kv0 = a + pltpu.roll(a, HEAD_DIM, axis=1)
            kv1 = b + pltpu.roll(b, HEAD_DIM, axis=1)
            return jnp.where(low, kv0, kv1)

        for j in range(N_PAIRS):
            o_ref[:, j * LANES:(j + 1) * LANES] = (unrope(blk_t(dqa_ref, j)) * scale).astype(bf16)
            o_ref[:, (6 + j) * LANES:(7 + j) * LANES] = (blk_t(dqb_ref, j) * scale).astype(bf16)
            o_ref[:, (10 + j) * LANES:(11 + j) * LANES] = blk(dkb_ref, j).astype(bf16)
            o_ref[:, (14 + j) * LANES:(15 + j) * LANES] = blk(dvb_ref, j).astype(bf16)
        o_ref[:, 4 * LANES:5 * LANES] = unrope(fold(dka_ref)).astype(bf16)
        o_ref[:, 5 * LANES:6 * LANES] = fold(dva_ref).astype(bf16)

    return pl.pallas_call(
        body, name=name, grid=(s // tm,),
        in_specs=[t_spec, _row_spec(tm, hw), _row_spec(tm, hw), t_spec, _row_spec(tm, hw), _row_spec(tm, hw)] + [_row_spec(tm, LANES)] * 2,
        out_specs=_row_spec(tm, QKV_W), out_shape=jax.ShapeDtypeStruct((s, QKV_W), bf16), compiler_params=_params(1),
    )(dqa_t, dka, dva, dqb_t, dkb, dvb, cos, sin_s)


def _cumsum_rows(v, reverse=False):
    n = v.shape[0]
    row = lax.broadcasted_iota(jnp.int32, v.shape, 0)
    sh = 1
    while sh < n:
        if reverse:
            v = v + jnp.where(row < n - sh, pltpu.roll(v, n - sh, axis=0), 0.0)
        else:
            v = v + jnp.where(row >= sh, pltpu.roll(v, sh, axis=0), 0.0)
        sh *= 2
    return v


def _log_sigmoid(z):
    return jnp.minimum(z, 0.0) - jnp.log1p(jnp.exp(-jnp.abs(z)))


def _forget_prep(fl, bf_row, name):
    s = fl.shape[0]

    def body(f_ref, b_ref, cb_ref):
        cum = _cumsum_rows(_log_sigmoid(f_ref[...] + b_ref[...]))
        for h in range(N_HEADS):
            cb_ref[:, h * LANES:(h + 1) * LANES] = jnp.broadcast_to(cum[:, h:h + 1], (s, LANES))

    return pl.pallas_call(
        body, name=name, out_shape=jax.ShapeDtypeStruct((s, N_HEADS * LANES), f32), compiler_params=_params(),
    )(fl, bf_row)


def _forget_prep_bwd(rs, dcs, fl, bf_row, name):
    s = fl.shape[0]

    def body(r_ref, c_ref, f_ref, b_ref, df_ref, db_ref):
        eye = (lax.broadcasted_iota(jnp.int32, (N_HEADS, LANES), 0) == lax.broadcasted_iota(jnp.int32, (N_HEADS, LANES), 1)).astype(f32)
        dcum = lax.dot_general(r_ref[...], eye, _TN, precision=lax.Precision.HIGHEST, preferred_element_type=f32)
        for h in range(N_HEADS):
            dcum = dcum - jnp.where(_lane() == h, jnp.sum(c_ref[:, h * LANES:(h + 1) * LANES], axis=1, keepdims=True), 0.0)
        dlf = _cumsum_rows(dcum, reverse=True)
        z = f_ref[...] + b_ref[...]
        df = jnp.where(_lane() < N_HEADS, dlf * jax.nn.sigmoid(-z), 0.0)
        df_ref[...] = df.astype(bf16)
        db_ref[...] = jnp.zeros_like(db_ref)
        db_ref[0:1, :] = jnp.sum(df, axis=0, keepdims=True)

    return pl.pallas_call(
        body, name=name,
        out_shape=[jax.ShapeDtypeStruct((s, LANES), bf16), jax.ShapeDtypeStruct((8, LANES), f32)], compiler_params=_params(),
    )(rs, dcs, fl, bf_row)


def _tile_mask(n_keys, n_queries, off, window):
    shape = (n_keys, n_queries)
    d = lax.broadcasted_iota(jnp.int32, shape, 1) - lax.broadcasted_iota(jnp.int32, shape, 0) + off
    valid = d >= 0
    return jnp.logical_and(valid, d < window) if window else valid


def _wide(v, t):
    return jnp.concatenate([v] * (t // LANES), axis=1)


def _attn_fwd(q, k, v, name, *, cum_b=None, sink_rows=None, window=None, t=256):
    s = q.shape[0]
    t = _row_tile(s, t)
    fox, has_sink = cum_b is not None, sink_rows is not None
    assert not window or (window % LANES == 0 and t + window <= s)

    def body(*refs):
        q_ref, k_ref, v_ref = refs[:3]
        rest = list(refs[3:])
        cb_ref = rest.pop(0) if fox else None
        sink_ref = rest.pop(0) if has_sink else None
        o_ref, lse_ref = rest
        i = pl.program_id(1)
        low = _lane() < HEAD_DIM
        q2 = q_ref[...]
        zero = jnp.zeros_like(q2)
        qms = (jnp.where(low, q2, zero), jnp.where(low, zero, q2))

        def tile(k0, n_keys, off, carry, masked):
            kblk, vblk = k_ref[pl.ds(k0, n_keys), :], v_ref[pl.ds(k0, n_keys), :]
            valid = _tile_mask(n_keys, t, off, window) if masked else None
            out = []
            for h in range(2):
                m, l, acc = carry[h]
                sc = lax.dot_general(kblk, qms[h], _NT, preferred_element_type=f32)
                if fox:
                    sc = sc - _wide(cb_ref[pl.ds(k0, n_keys), h * LANES:(h + 1) * LANES], t)
                if masked:
                    sc = jnp.where(valid, sc, NEG)
                m_new = jnp.maximum(m, jnp.max(sc, axis=0, keepdims=True))
                p = jnp.exp(sc - m_new)
                alpha = jnp.exp(m - m_new)
                l = alpha * l + jnp.sum(p, axis=0, keepdims=True)
                acc = alpha * acc + lax.dot_general(vblk, p.astype(bf16), _TN, preferred_element_type=f32)
                out.append((m_new, l, acc))
            return tuple(out)

        init = []
        for h in range(2):
            if has_sink:
                init.append((_wide(sink_ref[h:h + 1, :], t), jnp.ones((1, t), f32), jnp.zeros((LANES, t), f32)))
            else:
                init.append((jnp.full((1, t), NEG, f32), jnp.zeros((1, t), f32), jnp.zeros((LANES, t), f32)))
        carry = tuple(init)
        if window:
            k0 = pl.multiple_of(jnp.maximum(i * t - window, 0), LANES)
            carry = tile(k0, t + window, i * t - k0, carry, True)
        else:
            carry = lax.fori_loop(0, i, lambda kb, c: tile(pl.multiple_of(kb * t, t), t, 0, c, False), carry)
            carry = tile(pl.multiple_of(i * t, t), t, 0, carry, True)
        (m0, l0, a0), (m1, l1, a1) = carry
        top = lax.broadcasted_iota(jnp.int32, (LANES, 1), 0) < HEAD_DIM
        o_t = jnp.where(top, a0 * (1.0 / l0), a1 * (1.0 / l1))
        o_ref[...] = o_t.T.astype(bf16)
        lse_ref[0:1, :] = m0 + jnp.log(l0)
        lse_ref[1:2, :] = m1 + jnp.log(l1)

    q_spec = pl.BlockSpec((t, LANES), lambda j, i: (i, j))
    kv_spec = pl.BlockSpec((s, LANES), lambda j, i: (0, j))
    in_specs, args = [q_spec, kv_spec, kv_spec], [q, k, v]
    if fox:
        in_specs += [pl.BlockSpec((s, 2 * LANES), lambda j, i: (0, j))]
        args += [cum_b]
    if has_sink:
        in_specs += [pl.BlockSpec((None, 2, LANES), lambda j, i: (j, 0, 0))]
        args += [sink_rows.reshape(N_PAIRS, 2, LANES)]
    return pl.pallas_call(
        body, name=name, grid=(N_PAIRS, s // t), in_specs=in_specs,
        out_specs=[q_spec, pl.BlockSpec((None, 2, t), lambda j, i: (j, 0, i))],
        out_shape=[jax.ShapeDtypeStruct((s, N_PAIRS * LANES), bf16), jax.ShapeDtypeStruct((N_PAIRS, 2, s), f32)],
        compiler_params=_params(2),
    )(*args)


def _attn_delta(do, o, name, *, lse=None, sink_rows=None):
    s, hw = do.shape
    tm = _row_tile(s, 512)
    has_sink = sink_rows is not None

    def body(*refs):
        do_ref, o_ref = refs[:2]
        if has_sink:
            lse_ref, sink_ref, dl_ref, ds_ref = refs[2:]

            @pl.when(pl.program_id(0) == 0)
            def _():
                ds_ref[...] = jnp.zeros_like(ds_ref)
        else:
            dl_ref, = refs[2:]
        for j in range(N_PAIRS):
            cols = slice(j * LANES, (j + 1) * LANES)
            prod_t = (do_ref[:, cols].astype(f32) * o_ref[:, cols].astype(f32)).T
            for h in range(2):
                dl = jnp.sum(prod_t[h * HEAD_DIM:(h + 1) * HEAD_DIM, :], axis=0, keepdims=True)
                dl_ref[j, h:h + 1, :] = dl
                if has_sink:
                    r = 2 * j + h
                    p_sink = jnp.exp(sink_ref[r:r + 1, 0:1] - lse_ref[j, h:h + 1, :])
                    ds_ref[r:r + 1, :] += -jnp.sum(p_sink * dl, axis=1, keepdims=True)

    rows_spec = pl.BlockSpec((N_PAIRS, 2, tm), lambda i: (0, 0, i))
    in_specs, args = [_row_spec(tm, hw)] * 2, [do, o]
    out_specs, out_shape = [rows_spec], [jax.ShapeDtypeStruct((N_PAIRS, 2, s), f32)]
    if has_sink:
        in_specs += [rows_spec, _vec_spec(LANES, N_HEADS)]
        args += [lse, sink_rows]
        out_specs += [_vec_spec(LANES, N_HEADS)]
        out_shape += [jax.ShapeDtypeStruct((N_HEADS, LANES), f32)]
    return pl.pallas_call(
        body, name=name, grid=(s // tm,), in_specs=in_specs, out_specs=out_specs, out_shape=out_shape,
        compiler_params=_params(1),
    )(*args)


def _attn_bwd(q, k, v, do, lse, delta, name, *, cum_b=None, window=None, t=256):
    s = q.shape[0]
    t = _row_tile(s, t)
    nblk = s // t
    fox = cum_b is not None
    assert not window or (window % LANES == 0 and t + window <= s)

    def body(*refs):
        k_ref, v_ref, q_ref, do_ref, lse_ref, dl_ref = refs[:6]
        rest = list(refs[6:])
        cb_ref = rest.pop(0) if fox else None
        dq_ref, dk_ref, dv_ref = rest[:3]
        dcs_ref, rs_ref = (rest[3], rest[4]) if fox else (None, None)
        b = pl.program_id(1)
        k0 = pl.multiple_of(b * t, t)

        @pl.when(b == 0)
        def _():
            dq_ref[...] = jnp.zeros_like(dq_ref)
            if fox:
                rs_ref[...] = jnp.zeros_like(rs_ref)

        dk_ref[...] = jnp.zeros_like(dk_ref)
        dv_ref[...] = jnp.zeros_like(dv_ref)
        if fox:
            dcs_ref[...] = jnp.zeros_like(dcs_ref)
        low = _lane() < HEAD_DIM
        top = lax.broadcasted_iota(jnp.int32, (LANES, 1), 0) < HEAD_DIM
        kblk, vblk = k_ref[...], v_ref[...]
        k_t = kblk.astype(f32).T.astype(bf16)
        cks = [_wide(cb_ref[pl.ds(k0, t), h * LANES:(h + 1) * LANES], t) for h in range(2)] if fox else None

        def tile(q0, n_queries, off, masked):
            cols = pl.ds(q0, n_queries)
            q2, do2 = q_ref[cols, :], do_ref[cols, :]
            zero = jnp.zeros_like(q2)
            valid = _tile_mask(t, n_queries, off, window) if masked else None
            dq_parts = []
            for h in range(2):
                qm = jnp.where(low, q2, zero) if h == 0 else jnp.where(low, zero, q2)
                dom = jnp.where(low, do2, zero) if h == 0 else jnp.where(low, zero, do2)
                sc = lax.dot_general(kblk, qm, _NT, preferred_element_type=f32)
                if fox:
                    sc = sc - cks[h]
                if masked:
                    sc = jnp.where(valid, sc, NEG)
                p = jnp.exp(sc - lse_ref[h:h + 1, cols])
                dp = lax.dot_general(vblk, dom, _NT, preferred_element_type=f32)
                ds = p * (dp - dl_ref[h:h + 1, cols])
                pb, dsb = p.astype(bf16), ds.astype(bf16)
                dv_ref[...] += jnp.dot(pb, dom, preferred_element_type=f32)
                dk_ref[...] += jnp.dot(dsb, qm, preferred_element_type=f32)
                dq_parts.append(jnp.dot(k_t, dsb, preferred_element_type=f32))
                if fox:
                    dcs_ref[:, h * LANES:(h + 1) * LANES] += sum(ds[:, g * LANES:(g + 1) * LANES] for g in range(t // LANES))
                    rs_ref[h:h + 1, cols] += jnp.sum(ds, axis=0, keepdims=True)
            dq_ref[:, cols] += jnp.where(top, dq_parts[0], dq_parts[1])

        def later_block(qb, carry):
            tile(pl.multiple_of(qb * t, t), t, 0, False)
            return carry

        if window:
            q0 = pl.multiple_of(jnp.minimum(b * t, s - (t + window)), LANES)
            tile(q0, t + window, q0 - b * t, True)
        else:
            tile(k0, t, 0, True)
            lax.fori_loop(b + 1, nblk, later_block, 0)

    kv_spec = pl.BlockSpec((t, LANES), lambda j, b: (b, j))
    seq_spec = pl.BlockSpec((s, LANES), lambda j, b: (0, j))
    rows_spec = pl.BlockSpec((None, 2, s), lambda j, b: (j, 0, 0))
    hw = N_PAIRS * LANES
    in_specs, args = [kv_spec, kv_spec, seq_spec, seq_spec, rows_spec, rows_spec], [k, v, q, do, lse, delta]
    out_specs = [pl.BlockSpec((LANES, s), lambda j, b: (j, 0)), kv_spec, kv_spec]
    out_shape = [jax.ShapeDtypeStruct((hw, s), f32), jax.ShapeDtypeStruct((s, hw), f32), jax.ShapeDtypeStruct((s, hw), f32)]
    if fox:
        in_specs += [pl.BlockSpec((s, 2 * LANES), lambda j, b: (0, j))]
        args += [cum_b]
        out_specs += [pl.BlockSpec((t, 2 * LANES), lambda j, b: (b, j)), rows_spec]
        out_shape += [jax.ShapeDtypeStruct((s, N_HEADS * LANES), f32), jax.ShapeDtypeStruct((N_PAIRS, 2, s), f32)]
    return pl.pallas_call(
        body, name=name, grid=(N_PAIRS, nblk), in_specs=in_specs, out_specs=out_specs, out_shape=out_shape,
        compiler_params=_params(2),
    )(*args)


def _merge(ba, bb, gl, name):
    s, d = ba.shape
    tm = _row_tile(s, 512)

    def body(a_ref, b_ref, g_ref, o_ref):
        g0, g1 = jax.nn.sigmoid(g_ref[:, :d].astype(f32)), jax.nn.sigmoid(g_ref[:, d:].astype(f32))
        o_ref[...] = (g0 * a_ref[...].astype(f32) + g1 * b_ref[...].astype(f32)).astype(bf16)

    return pl.pallas_call(
        body, name=name, grid=(s // tm,), in_specs=[_row_spec(tm, d)] * 2 + [_row_spec(tm, 2 * d)],
        out_specs=_row_spec(tm, d), out_shape=jax.ShapeDtypeStruct((s, d), bf16), compiler_params=_params(1),
    )(ba, bb, gl)


def _merge_bwd(dm, ba, bb, gl, name):
    s, d = ba.shape
    tm = _row_tile(s, 512)

    def body(dm_ref, a_ref, b_ref, g_ref, da_ref, db_ref, dg_ref):
        dmv = dm_ref[...].astype(f32)
        g0, g1 = jax.nn.sigmoid(g_ref[:, :d].astype(f32)), jax.nn.sigmoid(g_ref[:, d:].astype(f32))
        da_ref[...] = (dmv * g0).astype(bf16)
        db_ref[...] = (dmv * g1).astype(bf16)
        dg_ref[:, :d] = (dmv * a_ref[...].astype(f32) * (g0 * (1.0 - g0))).astype(bf16)
        dg_ref[:, d:] = (dmv * b_ref[...].astype(f32) * (g1 * (1.0 - g1))).astype(bf16)

    return pl.pallas_call(
        body, name=name, grid=(s // tm,), in_specs=[_row_spec(tm, d)] * 3 + [_row_spec(tm, 2 * d)],
        out_specs=[_row_spec(tm, d)] * 2 + [_row_spec(tm, 2 * d)],
        out_shape=[jax.ShapeDtypeStruct((s, d), bf16)] * 2 + [jax.ShapeDtypeStruct((s, 2 * d), bf16)],
        compiler_params=_params(1),
    )(dm, ba, bb, gl)


def _swiglu(gu, name):
    s, w = gu.shape
    h = w // 2
    tm = _row_tile(s, 256)

    def body(g_ref, o_ref):
        g = g_ref[:, :h].astype(f32)
        o_ref[...] = (g * jax.nn.sigmoid(g) * g_ref[:, h:].astype(f32)).astype(bf16)

    return pl.pallas_call(
        body, name=name, grid=(s // tm,), in_specs=[_row_spec(tm, w)], out_specs=_row_spec(tm, h),
        out_shape=jax.ShapeDtypeStruct((s, h), bf16), compiler_params=_params(1),
    )(gu)


def _swiglu_bwd(dact, gu, name):
    s, w = gu.shape
    h = w // 2
    tm = _row_tile(s, 256)

    def body(d_ref, g_ref, o_ref):
        dv, g, u = d_ref[...].astype(f32), g_ref[:, :h].astype(f32), g_ref[:, h:].astype(f32)
        sg = jax.nn.sigmoid(g)
        o_ref[:, :h] = (dv * u * (sg * (1.0 + g * (1.0 - sg)))).astype(bf16)
        o_ref[:, h:] = (dv * (g * sg)).astype(bf16)

    return pl.pallas_call(
        body, name=name, grid=(s // tm,), in_specs=[_row_spec(tm, h), _row_spec(tm, w)], out_specs=_row_spec(tm, w),
        out_shape=jax.ShapeDtypeStruct((s, w), bf16), compiler_params=_params(1),
    )(dact, gu)


def _ada_fwd(c_all, w, b, name):
    def body(c_ref, w_ref, b_ref, o_ref):
        o_ref[...] = jnp.dot(c_ref[...].astype(bf16), w_ref[...].astype(bf16), preferred_element_type=f32) + b_ref[...]

    return pl.pallas_call(
        body, name=name, out_shape=jax.ShapeDtypeStruct((c_all.shape[0], w.shape[1]), f32), compiler_params=_params(),
    )(c_all, w, b)


def _ada_wgrad(c_all, d_all, name):
    n, d = c_all.shape
    w = d_all.shape[1]

    def body(c_ref, d_ref, o_ref):
        eye = (lax.broadcasted_iota(jnp.int32, (n, n), 0) == lax.broadcasted_iota(jnp.int32, (n, n), 1)).astype(f32)
        ct = lax.dot_general(c_ref[...], eye, _TN, precision=lax.Precision.HIGHEST, preferred_element_type=f32)
        g = ct[:, 0:1] * d_ref[0:1, :]
        for bi in range(1, n):
            g = g + ct[:, bi:bi + 1] * d_ref[bi:bi + 1, :]
        o_ref[0] = g

    return pl.pallas_call(
        body, name=name, out_shape=jax.ShapeDtypeStruct((1, d, w), f32), compiler_params=_params(),
    )(c_all, d_all)


def _adamw(parts, w, m, v, name):
    r, c = w.shape
    n_parts = parts.shape[0]
    tr = next(t for t in range(min(r, 256), 0, -1) if r % t == 0 and (t % 16 == 0 or t == r))

    def body(p_ref, w_ref, m_ref, v_ref, g_ref, d_ref, nm_ref, nv_ref):
        g = p_ref[0].astype(f32)
        for i in range(1, n_parts):
            g = g + p_ref[i].astype(f32)
        mm = ADAM_B1 * m_ref[...] + (1.0 - ADAM_B1) * g
        vv = ADAM_B2 * v_ref[...] + (1.0 - ADAM_B2) * (g * g)
        m_hat = mm / (1.0 - ADAM_B1 ** ADAM_STEP)
        v_hat = vv / (1.0 - ADAM_B2 ** ADAM_STEP)
        g_ref[...] = g
        d_ref[...] = -ADAM_LR * (m_hat / (jnp.sqrt(v_hat) + ADAM_EPS) + ADAM_WD * w_ref[...])
        nm_ref[...] = mm
        nv_ref[...] = vv

    spec = pl.BlockSpec((tr, c), lambda i: (i, 0))
    return pl.pallas_call(
        body, name=name, grid=(r // tr,), in_specs=[pl.BlockSpec((n_parts, tr, c), lambda i: (0, i, 0))] + [spec] * 3,
        out_specs=[spec] * 4, out_shape=[jax.ShapeDtypeStruct((r, c), f32)] * 4, compiler_params=_params(1),
    )(parts, w, m, v)


def _me():
    return lax.axis_index("x"), lax.axis_index("y"), lax.axis_index("c")


def _all_gather(arrays, name, vmem=False):
    n = len(arrays)
    space = pltpu.VMEM if vmem else pl.ANY

    def body(*refs):
        ins, outs = refs[:n], refs[n:2 * n]
        send_sems, recv_sems, local_sems = refs[2 * n:]
        x, y, c = _me()
        me, sibling = (x, y, c), (x, y, 1 - c)
        chips = [(1 - x, y), (x, 1 - y), (1 - x, 1 - y)]

        def rows(a, dev):
            return outs[a].at[4 * dev[0] + 2 * dev[1] + dev[2]]

        def copy(a, k, block, to, src=None):
            return pltpu.make_async_remote_copy(
                src_ref=rows(a, block) if src is None else src, dst_ref=rows(a, block),
                send_sem=send_sems.at[a, k], recv_sem=recv_sems.at[a, k], device_id=to, device_id_type=MESH)

        mine = [pltpu.make_async_copy(ins[a], rows(a, me), local_sems.at[a]) for a in range(n)]
        for cp in mine:
            cp.start()
        first = []
        for a in range(n):
            first.append(copy(a, 0, me, sibling, src=ins[a]))
            first += [copy(a, 1 + j, me, (*chip, c), src=ins[a]) for j, chip in enumerate(chips)]
        for cp in first:
            cp.start()
        passed = []
        for j, chip in enumerate(chips):
            for a in range(n):
                copy(a, 1 + j, (*chip, c), me).wait_recv()
                fwd = copy(a, 4 + j, (*chip, c), sibling)
                fwd.start()
                passed.append(fwd)
        for a in range(n):
            copy(a, 0, sibling, me).wait_recv()
            for j, chip in enumerate(chips):
                copy(a, 4 + j, (*chip, 1 - c), me).wait_recv()
        for cp in first + passed:
            cp.wait_send()
        for cp in mine:
            cp.wait()

    outs = pl.pallas_call(
        body, name=name,
        in_specs=[pl.BlockSpec(memory_space=space)] * n, out_specs=[pl.BlockSpec(memory_space=space)] * n,
        out_shape=[jax.ShapeDtypeStruct((N_DEV,) + a.shape, a.dtype) for a in arrays],
        scratch_shapes=[pltpu.SemaphoreType.DMA((n, 7)), pltpu.SemaphoreType.DMA((n, 7)), pltpu.SemaphoreType.DMA((n,))],
        compiler_params=pltpu.CompilerParams(vmem_limit_bytes=VMEM_LIMIT),
    )(*arrays)
    return list(outs)


_FLIPS = ((0, 0, 1), (1, 0, 0), (0, 1, 0), (1, 1, 0), (1, 0, 1), (0, 1, 1), (1, 1, 1))
_HBM = pl.BlockSpec(memory_space=pltpu.HBM)
_SEM = pl.BlockSpec(memory_space=pltpu.SEMAPHORE)


def _exchange_copies(scatter, srcs, lands, send_sems, recv_sems):
    x, y, c = _me()
    me_row = 4 * x + 2 * y + c
    out = []
    for k, (fx, fy, fc) in enumerate(_FLIPS):
        peer = (x ^ fx, y ^ fy, c ^ fc)
        peer_row = 4 * peer[0] + 2 * peer[1] + peer[2]
        for a in range(len(srcs)):
            out.append(pltpu.make_async_remote_copy(
                src_ref=srcs[a].at[peer_row] if scatter else srcs[a], dst_ref=lands[a].at[me_row],
                send_sem=send_sems.at[7 * a + k], recv_sem=recv_sems.at[7 * a + k], device_id=peer, device_id_type=MESH))
    return out


def _exchange_start(arrays, scatter, name):
    n = len(arrays)
    lands = [lax.empty(a.shape if scatter else (N_DEV,) + a.shape, a.dtype) for a in arrays]

    def body(*refs):
        srcs, zones = refs[:n], refs[n:2 * n]
        send_sems, recv_sems = refs[2 * n], refs[2 * n + 1]
        token = refs[-1]
        for cp in _exchange_copies(scatter, srcs, zones, send_sems, recv_sems):
            cp.start()
        token[...] = jnp.zeros_like(token)

    thru = [pltpu.HBM(a.shape, a.dtype) for a in list(arrays) + lands]
    outs = pl.pallas_call(
        body, name=name,
        out_shape=(pltpu.SemaphoreType.DMA((7 * n,)), pltpu.SemaphoreType.DMA((7 * n,)), *thru, jax.ShapeDtypeStruct((8, LANES), f32)),
        in_specs=[_HBM] * (2 * n), out_specs=(_SEM, _SEM, *[_HBM] * (2 * n), pl.BlockSpec(memory_space=pltpu.VMEM)),
        input_output_aliases={i: 2 + i for i in range(2 * n)},
        compiler_params=pltpu.CompilerParams(has_side_effects=pltpu.SideEffectType.DATAFLOW_SIDE_EFFECTING),
    )(*[pltpu.with_memory_space_constraint(a, pltpu.HBM) for a in list(arrays) + lands])
    return dict(n=n, scatter=scatter, sems=outs[:2], srcs=outs[2:2 + n], lands=outs[2 + n:2 + 2 * n], token=outs[-1])


def _exchange_wait(handle, after, name):
    n, scatter = handle["n"], handle["scatter"]

    def body(*refs):
        srcs, zones = refs[:n], refs[n:2 * n]
        send_sems, recv_sems = refs[2 * n], refs[2 * n + 1]
        for cp in _exchange_copies(scatter, srcs, zones, send_sems, recv_sems):
            cp.wait_send()
            cp.wait_recv()

    thru = [pltpu.HBM(a.shape, a.dtype) for a in list(handle["srcs"]) + list(handle["lands"])]
    outs = pl.pallas_call(
        body, name=name, out_shape=tuple(thru),
        in_specs=[_HBM] * (2 * n) + [_SEM, _SEM, pl.BlockSpec(memory_space=pl.ANY)], out_specs=tuple([_HBM] * (2 * n)),
        input_output_aliases={i: i for i in range(2 * n)},
        compiler_params=pltpu.CompilerParams(has_side_effects=pltpu.SideEffectType.DATAFLOW_SIDE_EFFECTING),
    )(*handle["srcs"], *handle["lands"], *handle["sems"], after)
    return list(outs[n:])


def _cols_from_shards(g):
    return jnp.transpose(g, (1, 0, 2)).reshape(g.shape[1], -1)


def _shards_from_cols(a):
    return jnp.transpose(a.reshape(a.shape[0], N_DEV, -1), (1, 0, 2))


def _local_step(x, positions, ada, g_pre_mix, g_post_mix, b_f, sinks, g_pre_ffn, g_post_ffn, target,
                w_in, late_weights, on_grads):
    s, d = x.shape
    row = lambda v: v.reshape(1, -1)
    shift_m, scale_m, gate_m, shift_f, scale_f, gate_f = (ada[i:i + 1] for i in range(6))
    w_gate, w_qkv = w_in[:, F_OFF + N_HEADS:], w_in[:, :QKV_W]
    w_f = jnp.pad(w_in[:, F_OFF:F_OFF + N_HEADS], ((0, 0), (0, LANES - N_HEADS)))
    w_in_p = jnp.concatenate([w_gate, w_qkv, w_f], axis=1)
    bf_row = jnp.pad(row(b_f), ((0, 0), (0, LANES - N_HEADS)))
    sink_rows = jnp.broadcast_to(sinks.reshape(N_HEADS, 1).astype(f32), (N_HEADS, LANES))
    inv_freq = 1.0 / (ROPE_THETA ** (jnp.arange(0, HEAD_DIM, 2, dtype=f32) / HEAD_DIM))
    cos, sin_s = _rope_tables(positions.reshape(s, 1), jnp.tile(inv_freq, 4).reshape(1, LANES), "rope_tables")

    h1 = _prenorm(x, row(g_pre_mix), scale_m, shift_m, "prenorm_mix")
    gl = _matmul(h1, w_gate, "nn", bf16, "proj_gate")
    qkv = _matmul(h1, w_qkv, "nn", f32, "proj_qkv")
    fl = _matmul(h1, w_f, "nn", f32, "proj_forget")
    qa, ka, va, qb, kb, vb = _qkv_prep(qkv, cos, sin_s, "qkv_prep")
    cum_b = _forget_prep(fl, bf_row, "forget_prep")
    o_a, lse_a = _attn_fwd(qa, ka, va, "swa_fwd", sink_rows=sink_rows, window=WINDOW, t=512)
    o_b, lse_b = _attn_fwd(qb, kb, vb, "fox_fwd", cum_b=cum_b, t=512)
    w_branch_a, w_branch_b, w_out, w_ffn_in, w_ffn_out = late_weights(o_b)
    ba = _matmul(o_a, w_branch_a, "nn", bf16, "branch_a")
    bb = _matmul(o_b, w_branch_b, "nn", bf16, "branch_b")
    merged = _merge(ba, bb, gl, "merge")
    y1 = _matmul(merged, w_out, "nn", f32, "out_proj")
    x2 = _postnorm_res(x, y1, row(g_post_mix), gate_m, "postnorm_mix")

    h2 = _prenorm(x2, row(g_pre_ffn), scale_f, shift_f, "prenorm_ffn")
    gu = _matmul(h2, w_ffn_in, "nn", bf16, "ffn_in")
    act = _swiglu(gu, "swiglu")
    y2 = _matmul(act, w_ffn_out, "nn", f32, "ffn_out")
    out = _postnorm_res(x2, y2, row(g_post_ffn), gate_f, "postnorm_ffn")
    loss_row, d_out = _loss_head(out, target, "loss_head")

    d_y2, vec_pf = _postnorm_bwd(d_out, y2, row(g_post_ffn), gate_f, "postnorm_ffn_bwd")
    g_w_ffn_out = _matmul(act, d_y2, "tn", bf16, "ffn_out_wgrad")
    d_act = _matmul(d_y2, w_ffn_out, "nt", bf16, "ffn_out_dgrad")
    dgu = _swiglu_bwd(d_act, gu, "swiglu_bwd")
    g_w_ffn_in = _matmul(h2, dgu, "tn", bf16, "ffn_in_wgrad")
    sent = on_grads(dict(w_ffn_in=g_w_ffn_in, w_ffn_out=g_w_ffn_out))
    d_h2 = _matmul(dgu, w_ffn_in, "nt", f32, "ffn_in_dgrad", after=sent)
    d_x2, vec_nf = _prenorm_bwd(d_h2, x2, row(g_pre_ffn), scale_f, d_out, "prenorm_ffn_bwd")

    d_y1, vec_pm = _postnorm_bwd(d_x2, y1, row(g_post_mix), gate_m, "postnorm_mix_bwd")
    g_w_out = _matmul(merged, d_y1, "tn", bf16, "out_proj_wgrad")
    d_merged = _matmul(d_y1, w_out, "nt", bf16, "out_proj_dgrad")
    d_ba, d_bb, dgl = _merge_bwd(d_merged, ba, bb, gl, "merge_bwd")
    g_w_branch_a = _matmul(o_a, d_ba, "tn", bf16, "branch_a_wgrad")
    g_w_branch_b = _matmul(o_b, d_bb, "tn", bf16, "branch_b_wgrad")
    sent = on_grads(dict(w_out=g_w_out, w_branch_a=g_w_branch_a, w_branch_b=g_w_branch_b))
    d_oa = _matmul(d_ba, w_branch_a, "nt", bf16, "branch_a_dgrad", after=sent)
    d_ob = _matmul(d_bb, w_branch_b, "nt", bf16, "branch_b_dgrad", after=sent)
    delta_a, d_sink = _attn_delta(d_oa, o_a, "swa_delta", lse=lse_a, sink_rows=sink_rows)
    delta_b, = _attn_delta(d_ob, o_b, "fox_delta")
    dqa_t, dka, dva = _attn_bwd(qa, ka, va, d_oa, lse_a, delta_a, "swa_bwd", window=WINDOW, t=512)
    dqb_t, dkb, dvb, dcs, rs = _attn_bwd(qb, kb, vb, d_ob, lse_b, delta_b, "fox_bwd", cum_b=cum_b, t=512)
    dqkv = _qkv_prep_bwd(dqa_t, dka, dva, dqb_t, dkb, dvb, cos, sin_s, "qkv_prep_bwd")
    dfl, vec_bf = _forget_prep_bwd(rs.reshape(N_HEADS, s), dcs, fl, bf_row, "forget_prep_bwd")
    dproj = jnp.concatenate([dgl, dqkv, dfl], axis=1)
    g_w_in_p = _matmul(h1, dproj, "tn", bf16, "in_proj_wgrad")
    g_w_in = jnp.concatenate([g_w_in_p[:, GATE_W:GATE_W + QKV_W], g_w_in_p[:, GATE_W + QKV_W:GATE_W + QKV_W + N_HEADS],
                              g_w_in_p[:, :GATE_W]], axis=1)
    sent = on_grads(dict(w_in=g_w_in))
    d_h1 = _matmul(dproj, w_in_p, "nt", f32, "in_proj_dgrad", after=sent)
    grad_x, vec_nm = _prenorm_bwd(d_h1, x, row(g_pre_mix), scale_m, d_x2, "prenorm_mix_bwd")

    d_ada = jnp.concatenate([vec_nm[0], vec_nm[1], vec_pm[0], vec_nf[0], vec_nf[1], vec_pf[0]])
    small = dict(b_ada=d_ada, g_pre_mix=vec_nm[2], g_post_mix=vec_pm[1], g_pre_ffn=vec_nf[2], g_post_ffn=vec_pf[1],
                 b_f=vec_bf[0, :N_HEADS], sinks=d_sink[:, 0], loss=loss_row[0, :1])
    return grad_x, small


_SMALL = (("b_ada", 6144), ("g_pre_mix", 1024), ("g_post_mix", 1024), ("g_pre_ffn", 1024), ("g_post_ffn", 1024),
          ("b_f", 128), ("sinks", 128), ("loss", 128))
_SMALL_ROWS = 88


def _pack_small(vals):
    parts = [jnp.pad(vals[k].reshape(-1).astype(f32), (0, n - vals[k].size)) for k, n in _SMALL]
    flat = jnp.concatenate(parts)
    return jnp.pad(flat, (0, _SMALL_ROWS * LANES - flat.size)).reshape(_SMALL_ROWS, LANES)


def _unpack_small(slab, shapes):
    flat, out, off = slab.reshape(-1), {}, 0
    for k, n in _SMALL:
        size = math.prod(shapes[k])
        out[k] = flat[off:off + size].reshape(shapes[k])
        off += n
    return out


def kernel(x, c, positions, w_ada, b_ada, g_pre_mix, g_post_mix, w_in, b_f, sinks, w_branch_a, w_branch_b, w_out, g_pre_ffn, g_post_ffn, w_ffn_in, w_ffn_out, loss_target, m_w_ada, m_b_ada, m_g_pre_mix, m_g_post_mix, m_w_in, m_b_f, m_sinks, m_w_branch_a, m_w_branch_b, m_w_out, m_g_pre_ffn, m_g_post_ffn, m_w_ffn_in, m_w_ffn_out, v_w_ada, v_b_ada, v_g_pre_mix, v_g_post_mix, v_w_in, v_b_f, v_sinks, v_w_branch_a, v_w_branch_b, v_w_out, v_g_pre_ffn, v_g_post_ffn, v_w_ffn_in, v_w_ffn_out):
    xi, yi, ci = _me()
    me = 4 * xi + 2 * yi + ci
    d = D_MODEL
    ada_w = w_ada.shape[2]

    c_all, = _all_gather([c], "gather_c", vmem=True)
    c_all = c_all.reshape(N_DEV, d)
    b_mine = lax.dynamic_slice(b_ada, (0, me * ada_w), (1, ada_w))
    ada_cols = _ada_fwd(c_all, w_ada[0], b_mine, "ada_fwd")
    ada_all, = _all_gather([ada_cols], "gather_ada", vmem=True)
    ada = lax.dynamic_index_in_dim(ada_all, me, axis=1, keepdims=False).reshape(6, d)

    g_in, = _all_gather([w_in[0].astype(bf16)], "gather_w_in")
    late = [w.astype(bf16) for w in (w_branch_a[0], w_branch_b[0], w_out[0], w_ffn_in[0], w_ffn_out[0])]
    late_h = _exchange_start(late, False, "gather_late_start")

    def mine_into(zone, block):
        return lax.dynamic_update_index_in_dim(zone, block, me, 0)

    def late_weights(after):
        zones = _exchange_wait(late_h, after, "gather_late_wait")
        g_ba, g_bb, g_out, g_fi, g_fo = (mine_into(z, w) for z, w in zip(zones, late))
        return (_cols_from_shards(g_ba), _cols_from_shards(g_bb), g_out.reshape(d, d), _cols_from_shards(g_fi),
                g_fo.reshape(D_FF, d))

    row_sharded = ("w_out", "w_ffn_out")
    in_flight = []

    def on_grads(group):
        sends = [g.reshape(N_DEV, g.shape[0] // N_DEV, g.shape[1]) if nm in row_sharded else _shards_from_cols(g)
                 for nm, g in group.items()]
        handle = _exchange_start(sends, True, "scatter_start_%d" % len(in_flight))
        in_flight.append((list(group), sends, handle))
        return handle["token"]

    grad_x, small = _local_step(
        x[0], positions[0], ada + late_h["token"][0, 0], g_pre_mix[0], g_post_mix[0], b_f[0], sinks[0], g_pre_ffn[0],
        g_post_ffn[0], loss_target[0], _cols_from_shards(g_in), late_weights, on_grads)

    slab_all, = _all_gather([_pack_small(small)], "gather_small", vmem=True)
    small_w = dict(b_ada=b_ada, g_pre_mix=g_pre_mix, g_post_mix=g_post_mix, g_pre_ffn=g_pre_ffn, g_post_ffn=g_post_ffn,
                   b_f=b_f, sinks=sinks, loss=jnp.zeros((1,), f32))
    small_m = dict(b_ada=m_b_ada, g_pre_mix=m_g_pre_mix, g_post_mix=m_g_post_mix, g_pre_ffn=m_g_pre_ffn,
                   g_post_ffn=m_g_post_ffn, b_f=m_b_f, sinks=m_sinks, loss=jnp.zeros((1,), f32))
    small_v = dict(b_ada=v_b_ada, g_pre_mix=v_g_pre_mix, g_post_mix=v_g_post_mix, g_pre_ffn=v_g_pre_ffn,
                   g_post_ffn=v_g_post_ffn, b_f=v_b_f, sinks=v_sinks, loss=jnp.ones((1,), f32))
    shapes = {k: small_w[k].shape for k, _ in _SMALL}
    s_out = _adamw(slab_all, _pack_small(small_w), _pack_small(small_m), _pack_small(small_v), "adamw_small")
    s_grad, s_delta, s_m, s_v = (_unpack_small(o, shapes) for o in s_out)

    d_ada_all = lax.dynamic_slice(slab_all[:, :6144 // LANES, :].reshape(N_DEV, 6144), (0, me * ada_w), (N_DEV, ada_w))
    ada_parts = _ada_wgrad(c_all, d_ada_all, "ada_wgrad")

    ws = dict(w_in=(w_in, m_w_in, v_w_in), w_branch_a=(w_branch_a, m_w_branch_a, v_w_branch_a),
              w_branch_b=(w_branch_b, m_w_branch_b, v_w_branch_b), w_out=(w_out, m_w_out, v_w_out),
              w_ffn_in=(w_ffn_in, m_w_ffn_in, v_w_ffn_in), w_ffn_out=(w_ffn_out, m_w_ffn_out, v_w_ffn_out))
    res = {"w_ada": _adamw(ada_parts, w_ada[0], m_w_ada[0], v_w_ada[0], "adamw_w_ada")}
    after = res["w_ada"][0]
    for gi, (names, sends, handle) in enumerate(in_flight):
        zones = _exchange_wait(handle, after, "scatter_wait_%d" % gi)
        for nm, zone, sent in zip(names, zones, sends):
            w, m, v = ws[nm]
            parts = mine_into(zone, lax.dynamic_index_in_dim(sent, me, 0, keepdims=False))
            res[nm] = _adamw(parts, w[0], m[0], v[0], "adamw_" + nm)
            after = res[nm][0]

    order = ["w_ada", "b_ada", "g_pre_mix", "g_post_mix", "w_in", "b_f", "sinks", "w_branch_a", "w_branch_b", "w_out",
             "g_pre_ffn", "g_post_ffn", "w_ffn_in", "w_ffn_out"]
    outs = [s_grad["loss"].reshape(()), grad_x[None]]
    for which, small_o in enumerate((s_grad, s_delta, s_m, s_v)):
        for nm in order:
            outs.append(res[nm][which][None] if nm in res else small_o[nm])
    return tuple(outs)
```

```python
import functools
import math

import jax
import jax.numpy as jnp
from jax import lax
from jax.experimental import pallas as pl
from jax.experimental.pallas import tpu as pltpu

f32 = jnp.float32
bf16 = jnp.bfloat16

D_MODEL = 1024
HEAD_DIM = 64
N_HEADS = 8
N_PAIRS = 4
QKV_W = 2304
GATE_W = 2048
F_OFF = 2304
IN_W = 4360
WINDOW = 128
ROPE_THETA = 10000.0
RMS_EPS = 1e-6
D_FF = 2816
N_DEV = 8
ADAM_LR, ADAM_B1, ADAM_B2, ADAM_EPS, ADAM_WD, ADAM_STEP = 0.001, 0.9, 0.999, 1e-08, 0.01, 10
NEG = -1e30
LANES = 128
VMEM_LIMIT = 48 * 1024 * 1024
MESH = pl.DeviceIdType.MESH

_NT = (((1,), (1,)), ((), ()))
_TN = (((0,), (0,)), ((), ()))


def _params(n_grid=0):
    sem = ("arbitrary",) * n_grid if n_grid else None
    return pltpu.CompilerParams(dimension_semantics=sem, vmem_limit_bytes=VMEM_LIMIT)


def _row_tile(s, want):
    t = min(s, want)
    assert s % t == 0, (s, t)
    return t


MATMUL_VMEM_BUDGET = 40 * 1024 * 1024


def _matmul_tiles(m, n, k, a_item, b_item, o_item):
    def tiles(d):
        return [t for t in range(LANES, min(d, 2048) + 1, LANES) if d % t == 0] or [d]

    best = None
    for tm in tiles(m):
        for tn in tiles(n):
            vmem = 2 * (tm * k * a_item + tn * k * b_item + tm * tn * o_item) + tm * tn * 4
            if vmem > MATMUL_VMEM_BUDGET:
                continue
            traffic = m * k * a_item + n * k * b_item * (1 if tn == n else m // tm) + m * n * o_item
            steps = (m // tm) * (n // tn)
            key = (traffic, 0, steps) if steps >= 4 else (traffic, 1, -steps)
            if best is None or key < best[0]:
                best = (key, tm, tn)
    assert best is not None, (m, n, k)
    return best[1], best[2]


def _matmul(a, b, mode, out_dtype, name, after=None):
    if mode == "nn":
        (m, k), n = a.shape, b.shape[1]
    elif mode == "nt":
        (m, k), n = a.shape, b.shape[0]
    else:
        (k, m), n = a.shape, b.shape[1]
    tm, tn = _matmul_tiles(m, n, k, a.dtype.itemsize, b.dtype.itemsize, jnp.dtype(out_dtype).itemsize)
    if mode == "nn":
        a_spec, b_spec, dims = pl.BlockSpec((tm, k), lambda i, j: (i, 0)), pl.BlockSpec((k, tn), lambda i, j: (0, j)), None
    elif mode == "nt":
        a_spec, b_spec, dims = pl.BlockSpec((tm, k), lambda i, j: (i, 0)), pl.BlockSpec((tn, k), lambda i, j: (j, 0)), _NT
    else:
        a_spec, b_spec, dims = pl.BlockSpec((k, tm), lambda i, j: (0, i)), pl.BlockSpec((k, tn), lambda i, j: (0, j)), _TN

    def body(a_ref, b_ref, *rest):
        o_ref = rest[-1]
        av, bv = a_ref[...].astype(bf16), b_ref[...].astype(bf16)
        if dims is None:
            r = jnp.dot(av, bv, preferred_element_type=f32)
        else:
            r = lax.dot_general(av, bv, dims, preferred_element_type=f32)
        o_ref[...] = r.astype(out_dtype)

    extra = [] if after is None else [after]
    return pl.pallas_call(
        body, name=name, grid=(m // tm, n // tn), in_specs=[a_spec, b_spec] + [pl.BlockSpec(memory_space=pl.ANY)] * len(extra),
        out_specs=pl.BlockSpec((tm, tn), lambda i, j: (i, j)),
        out_shape=jax.ShapeDtypeStruct((m, n), out_dtype), compiler_params=_params(2),
    )(a, b, *extra)


def _rstd(v):
    return lax.rsqrt(jnp.mean(v * v, axis=-1, keepdims=True) + RMS_EPS)


def _row_spec(tm, d):
    return pl.BlockSpec((tm, d), lambda i: (i, 0))


def _vec_spec(d, rows=1):
    return pl.BlockSpec((rows, d), lambda i: (0, 0))


def _prenorm(x, g, scale, shift, name):
    s, d = x.shape
    tm = _row_tile(s, 512)

    def body(x_ref, g_ref, sc_ref, sh_ref, h_ref):
        xv = x_ref[...]
        h = (xv * _rstd(xv) * g_ref[...]) * (1.0 + sc_ref[...]) + sh_ref[...]
        h_ref[...] = h.astype(bf16)

    return pl.pallas_call(
        body, name=name, grid=(s // tm,), in_specs=[_row_spec(tm, d)] + [_vec_spec(d)] * 3,
        out_specs=_row_spec(tm, d), out_shape=jax.ShapeDtypeStruct((s, d), bf16), compiler_params=_params(1),
    )(x, g, scale, shift)


def _postnorm_res(x, y, g, gate, name):
    s, d = x.shape
    tm = _row_tile(s, 512)

    def body(x_ref, y_ref, g_ref, gate_ref, o_ref):
        yv = y_ref[...]
        o_ref[...] = x_ref[...] + gate_ref[...] * (yv * _rstd(yv) * g_ref[...])

    return pl.pallas_call(
        body, name=name, grid=(s // tm,), in_specs=[_row_spec(tm, d)] * 2 + [_vec_spec(d)] * 2,
        out_specs=_row_spec(tm, d), out_shape=jax.ShapeDtypeStruct((s, d), f32), compiler_params=_params(1),
    )(x, y, g, gate)


def _rms_bwd(u, v, r):
    return r * u - v * (r * r * r) * jnp.mean(u * v, axis=-1, keepdims=True)


def _loss_tail(x, y, g, gate, target, name):
    s, d = x.shape
    tm = _row_tile(s, 512)

    def body(x_ref, y_ref, g_ref, gate_ref, t_ref, loss_ref, do_ref, dy_ref, vec_ref):
        @pl.when(pl.program_id(0) == 0)
        def _():
            loss_ref[...] = jnp.zeros_like(loss_ref)
            vec_ref[...] = jnp.zeros_like(vec_ref)
        yv = y_ref[...]
        r = _rstd(yv)
        yn = yv * r
        err = x_ref[...] + gate_ref[...] * (yn * g_ref[...]) - t_ref[...]
        loss_ref[...] += 0.5 * jnp.sum(jnp.mean(err * err, axis=-1, keepdims=True), axis=0, keepdims=True)
        dr = err / d
        do_ref[...] = dr
        dn = dr * gate_ref[...]
        vec_ref[0:1, :] += jnp.sum(dr * (yn * g_ref[...]), axis=0, keepdims=True)
        vec_ref[1:2, :] += jnp.sum(dn * yn, axis=0, keepdims=True)
        dy_ref[...] = _rms_bwd(dn * g_ref[...], yv, r).astype(bf16)

    return pl.pallas_call(
        body, name=name, grid=(s // tm,), in_specs=[_row_spec(tm, d)] * 2 + [_vec_spec(d)] * 2 + [_row_spec(tm, d)],
        out_specs=[_vec_spec(LANES), _row_spec(tm, d), _row_spec(tm, d), _vec_spec(d, 8)],
        out_shape=[jax.ShapeDtypeStruct((1, LANES), f32), jax.ShapeDtypeStruct((s, d), f32),
                   jax.ShapeDtypeStruct((s, d), bf16), jax.ShapeDtypeStruct((8, d), f32)],
        compiler_params=_params(1),
    )(x, y, g, gate, target)


def _postnorm_bwd(dres, y, g, gate, name):
    s, d = y.shape
    tm = _row_tile(s, 512)

    def body(dr_ref, y_ref, g_ref, gate_ref, dy_ref, vec_ref):
        @pl.when(pl.program_id(0) == 0)
        def _():
            vec_ref[...] = jnp.zeros_like(vec_ref)
        dr, yv = dr_ref[...], y_ref[...]
        r = _rstd(yv)
        yn = yv * r
        dn = dr * gate_ref[...]
        vec_ref[0:1, :] += jnp.sum(dr * (yn * g_ref[...]), axis=0, keepdims=True)
        vec_ref[1:2, :] += jnp.sum(dn * yn, axis=0, keepdims=True)
        dy_ref[...] = _rms_bwd(dn * g_ref[...], yv, r).astype(bf16)

    return pl.pallas_call(
        body, name=name, grid=(s // tm,), in_specs=[_row_spec(tm, d)] * 2 + [_vec_spec(d)] * 2,
        out_specs=[_row_spec(tm, d), _vec_spec(d, 8)],
        out_shape=[jax.ShapeDtypeStruct((s, d), bf16), jax.ShapeDtypeStruct((8, d), f32)], compiler_params=_params(1),
    )(dres, y, g, gate)


def _prenorm_bwd(dh, x, g, scale, dres, name):
    s, d = x.shape
    tm = _row_tile(s, 512)

    def body(dh_ref, x_ref, g_ref, sc_ref, dr_ref, dx_ref, vec_ref):
        @pl.when(pl.program_id(0) == 0)
        def _():
            vec_ref[...] = jnp.zeros_like(vec_ref)
        dhv, xv = dh_ref[...], x_ref[...]
        r = _rstd(xv)
        xn = xv * r
        dn = dhv * (1.0 + sc_ref[...])
        vec_ref[0:1, :] += jnp.sum(dhv, axis=0, keepdims=True)
        vec_ref[1:2, :] += jnp.sum(dhv * (xn * g_ref[...]), axis=0, keepdims=True)
        vec_ref[2:3, :] += jnp.sum(dn * xn, axis=0, keepdims=True)
        dx_ref[...] = dr_ref[...] + _rms_bwd(dn * g_ref[...], xv, r)

    return pl.pallas_call(
        body, name=name, grid=(s // tm,),
        in_specs=[_row_spec(tm, d)] * 2 + [_vec_spec(d)] * 2 + [_row_spec(tm, d)],
        out_specs=[_row_spec(tm, d), _vec_spec(d, 8)],
        out_shape=[jax.ShapeDtypeStruct((s, d), f32), jax.ShapeDtypeStruct((8, d), f32)], compiler_params=_params(1),
    )(dh, x, g, scale, dres)


def _lane():
    return lax.broadcasted_iota(jnp.int32, (1, LANES), 1)


def _rope_tables(pos_col, inv_freq, name):
    s = pos_col.shape[0]

    def body(p_ref, f_ref, cos_ref, sin_ref):
        ang = p_ref[...].astype(f32) * f_ref[...]
        first_half = (_lane() % HEAD_DIM) < HEAD_DIM // 2
        cos_ref[...] = jnp.cos(ang)
        sn = jnp.sin(ang)
        sin_ref[...] = jnp.where(first_half, -sn, sn)

    return pl.pallas_call(
        body, name=name, out_shape=[jax.ShapeDtypeStruct((s, LANES), f32)] * 2, compiler_params=_params(),
    )(pos_col, inv_freq)


def _swap_halves(v):
    first_half = (_lane() % HEAD_DIM) < HEAD_DIM // 2
    return jnp.where(first_half, pltpu.roll(v, LANES - HEAD_DIM // 2, axis=1), pltpu.roll(v, HEAD_DIM // 2, axis=1))


def _qkv_prep(qkv, cos, sin_s, name):
    s = qkv.shape[0]
    tm = _row_tile(s, 256)
    scale = 1.0 / math.sqrt(HEAD_DIM)

    def body(p_ref, c_ref, s_ref, qa_ref, ka_ref, va_ref, qb_ref, kb_ref, vb_ref):
        cs, sn = c_ref[...], s_ref[...]
        low = _lane() < HEAD_DIM

        def blk(j):
            return p_ref[:, j * LANES:(j + 1) * LANES]

        def rope(v):
            return v * cs + _swap_halves(v) * sn

        def expand(v):
            other = pltpu.roll(v, HEAD_DIM, axis=1)
            return jnp.where(low, v, other), jnp.where(low, other, v)

        for j in range(N_PAIRS):
            qa_ref[:, j * LANES:(j + 1) * LANES] = (rope(blk(j)) * scale).astype(bf16)
            qb_ref[:, j * LANES:(j + 1) * LANES] = (blk(6 + j) * scale).astype(bf16)
            kb_ref[:, j * LANES:(j + 1) * LANES] = blk(10 + j).astype(bf16)
            vb_ref[:, j * LANES:(j + 1) * LANES] = blk(14 + j).astype(bf16)
        k0, k1 = expand(rope(blk(4)))
        v0, v1 = expand(blk(5))
        for j in range(N_PAIRS):
            ka_ref[:, j * LANES:(j + 1) * LANES] = (k0 if j < 2 else k1).astype(bf16)
            va_ref[:, j * LANES:(j + 1) * LANES] = (v0 if j < 2 else v1).astype(bf16)

    hw = N_PAIRS * LANES
    return pl.pallas_call(
        body, name=name, grid=(s // tm,),
        in_specs=[_row_spec(tm, QKV_W), _row_spec(tm, LANES), _row_spec(tm, LANES)],
        out_specs=[_row_spec(tm, hw)] * 6, out_shape=[jax.ShapeDtypeStruct((s, hw), bf16)] * 6, compiler_params=_params(1),
    )(qkv, cos, sin_s)


def _qkv_prep_bwd(dqa_t, dka, dva, dqb_t, dkb, dvb, cos, sin_s, name):
    s = dka.shape[0]
    tm = _row_tile(s, 256)
    scale = 1.0 / math.sqrt(HEAD_DIM)
    hw = N_PAIRS * LANES
    t_spec = pl.BlockSpec((hw, tm), lambda i: (0, i))

    def body(dqa_ref, dka_ref, dva_ref, dqb_ref, dkb_ref, dvb_ref, c_ref, s_ref, o_ref):
        cs, sn = c_ref[...], s_ref[...]
        low = _lane() < HEAD_DIM

        def blk(ref, j):
            return ref[:, j * LANES:(j + 1) * LANES]

        def blk_t(ref, j):
            return ref[j * LANES:(j + 1) * LANES, :].T

        def unrope(v):
            return v * cs + _swap_halves(v * sn)

        def fold(ref):
            a, b = blk(ref, 0) + blk(ref, 1), blk(ref, 2) + blk(ref, 3)
            kv0 = a + pltpu.roll(a, HEAD_DIM, axis=1)
            kv1 = b + pltpu.roll(b, HEAD_DIM, axis=1)
            return jnp.where(low, kv0, kv1)

        for j in range(N_PAIRS):
            o_ref[:, j * LANES:(j + 1) * LANES] = (unrope(blk_t(dqa_ref, j)) * scale).astype(bf16)
            o_ref[:, (6 + j) * LANES:(7 + j) * LANES] = (blk_t(dqb_ref, j) * scale).astype(bf16)
            o_ref[:, (10 + j) * LANES:(11 + j) * LANES] = blk(dkb_ref, j).astype(bf16)
            o_ref[:, (14 + j) * LANES:(15 + j) * LANES] = blk(dvb_ref, j).astype(bf16)
        o_ref[:, 4 * LANES:5 * LANES] = unrope(fold(dka_ref)).astype(bf16)
        o_ref[:, 5 * LANES:6 * LANES] = fold(dva_ref).astype(bf16)

    return pl.pallas_call(
        body, name=name, grid=(s // tm,),
        in_specs=[t_spec, _row_spec(tm, hw), _row_spec(tm, hw), t_spec, _row_spec(tm, hw), _row_spec(tm, hw)] + [_row_spec(tm, LANES)] * 2,
        out_specs=_row_spec(tm, QKV_W), out_shape=jax.ShapeDtypeStruct((s, QKV_W), bf16), compiler_params=_params(1),
    )(dqa_t, dka, dva, dqb_t, dkb, dvb, cos, sin_s)


def _cumsum_rows(v, reverse=False):
    n = v.shape[0]
    row = lax.broadcasted_iota(jnp.int32, v.shape, 0)
    sh = 1
    while sh < n:
        if reverse:
            v = v + jnp.where(row < n - sh, pltpu.roll(v, n - sh, axis=0), 0.0)
        else:
            v = v + jnp.where(row >= sh, pltpu.roll(v, sh, axis=0), 0.0)
        sh *= 2
    return v


def _log_sigmoid(z):
    return jnp.minimum(z, 0.0) - jnp.log1p(jnp.exp(-jnp.abs(z)))


def _forget_prep(fl, bf_row, name):
    s = fl.shape[0]

    def body(f_ref, b_ref, cb_ref):
        cum = _cumsum_rows(_log_sigmoid(f_ref[...] + b_ref[...]))
        for h in range(N_HEADS):
            cb_ref[:, h * LANES:(h + 1) * LANES] = jnp.broadcast_to(cum[:, h:h + 1], (s, LANES))

    return pl.pallas_call(
        body, name=name, out_shape=jax.ShapeDtypeStruct((s, N_HEADS * LANES), f32), compiler_params=_params(),
    )(fl, bf_row)


def _forget_prep_bwd(rs, dcs, fl, bf_row, name):
    s = fl.shape[0]

    def body(r_ref, c_ref, f_ref, b_ref, df_ref, db_ref):
        eye = (lax.broadcasted_iota(jnp.int32, (N_HEADS, LANES), 0) == lax.broadcasted_iota(jnp.int32, (N_HEADS, LANES), 1)).astype(f32)
        dcum = lax.dot_general(r_ref[...], eye, _TN, precision=lax.Precision.HIGHEST, preferred_element_type=f32)
        for h in range(N_HEADS):
            dcum = dcum - jnp.where(_lane() == h, jnp.sum(c_ref[:, h * LANES:(h + 1) * LANES], axis=1, keepdims=True), 0.0)
        dlf = _cumsum_rows(dcum, reverse=True)
        z = f_ref[...] + b_ref[...]
        df = jnp.where(_lane() < N_HEADS, dlf * jax.nn.sigmoid(-z), 0.0)
        df_ref[...] = df.astype(bf16)
        db_ref[...] = jnp.zeros_like(db_ref)
        db_ref[0:1, :] = jnp.sum(df, axis=0, keepdims=True)

    return pl.pallas_call(
        body, name=name,
        out_shape=[jax.ShapeDtypeStruct((s, LANES), bf16), jax.ShapeDtypeStruct((8, LANES), f32)], compiler_params=_params(),
    )(rs, dcs, fl, bf_row)


def _tile_mask(n_keys, n_queries, off, window):
    shape = (n_keys, n_queries)
    d = lax.broadcasted_iota(jnp.int32, shape, 1) - lax.broadcasted_iota(jnp.int32, shape, 0) + off
    valid = d >= 0
    return jnp.logical_and(valid, d < window) if window else valid


def _wide(v, t):
    return jnp.concatenate([v] * (t // LANES), axis=1)


def _attn_fwd(q, k, v, name, *, cum_b=None, sink_rows=None, window=None, t=256):
    s = q.shape[0]
    t = _row_tile(s, t)
    fox, has_sink = cum_b is not None, sink_rows is not None
    assert not window or (window % LANES == 0 and LANES + window <= s)

    def body(*refs):
        q_ref, k_ref, v_ref = refs[:3]
        rest = list(refs[3:])
        cb_ref = rest.pop(0) if fox else None
        sink_ref = rest.pop(0) if has_sink else None
        o_ref, lse_ref = rest
        i = pl.program_id(1)
        low = _lane() < HEAD_DIM
        top = lax.broadcasted_iota(jnp.int32, (LANES, 1), 0) < HEAD_DIM
        q2 = q_ref[...]
        zero = jnp.zeros_like(q2)
        qms = (jnp.where(low, q2, zero), jnp.where(low, zero, q2))

        def tile(k0, n_keys, off, carry, masked, queries=slice(0, t)):
            nq = queries.stop - queries.start
            kblk, vblk = k_ref[pl.ds(k0, n_keys), :], v_ref[pl.ds(k0, n_keys), :]
            valid = _tile_mask(n_keys, nq, off, window) if masked else None
            out = []
            for h in range(2):
                m, l, acc = carry[h]
                sc = lax.dot_general(kblk, qms[h][queries], _NT, preferred_element_type=f32)
                if fox:
                    sc = sc - _wide(cb_ref[pl.ds(k0, n_keys), h * LANES:(h + 1) * LANES], nq)
                if masked:
                    sc = jnp.where(valid, sc, NEG)
                m_new = jnp.maximum(m, jnp.max(sc, axis=0, keepdims=True))
                p = jnp.exp(sc - m_new)
                alpha = jnp.exp(m - m_new)
                l = alpha * l + jnp.sum(p, axis=0, keepdims=True)
                acc = alpha * acc + lax.dot_general(vblk, p.astype(bf16), _TN, preferred_element_type=f32)
                out.append((m_new, l, acc))
            return tuple(out)

        def start(nq):
            if has_sink:
                return tuple((_wide(sink_ref[h:h + 1, :], nq), jnp.ones((1, nq), f32), jnp.zeros((LANES, nq), f32))
                             for h in range(2))
            return tuple((jnp.full((1, nq), NEG, f32), jnp.zeros((1, nq), f32), jnp.zeros((LANES, nq), f32)) for h in range(2))

        def finish(carry, queries):
            (m0, l0, a0), (m1, l1, a1) = carry
            o_t = jnp.where(top, a0 * (1.0 / l0), a1 * (1.0 / l1))
            o_ref[queries, :] = o_t.T.astype(bf16)
            lse_ref[0:1, queries] = m0 + jnp.log(l0)
            lse_ref[1:2, queries] = m1 + jnp.log(l1)

        if window:
            for c in range(t // LANES):
                queries = slice(c * LANES, (c + 1) * LANES)
                q0 = i * t + c * LANES
                k0 = pl.multiple_of(jnp.maximum(q0 - window, 0), LANES)
                finish(tile(k0, LANES + window, q0 - k0, start(LANES), True, queries), queries)
        else:
            carry = lax.fori_loop(0, i, lambda kb, c: tile(pl.multiple_of(kb * t, t), t, 0, c, False), start(t))
            finish(tile(pl.multiple_of(i * t, t), t, 0, carry, True), slice(0, t))

    q_spec = pl.BlockSpec((t, LANES), lambda j, i: (i, j))
    kv_spec = pl.BlockSpec((s, LANES), lambda j, i: (0, j))
    in_specs, args = [q_spec, kv_spec, kv_spec], [q, k, v]
    if fox:
        in_specs += [pl.BlockSpec((s, 2 * LANES), lambda j, i: (0, j))]
        args += [cum_b]
    if has_sink:
        in_specs += [pl.BlockSpec((None, 2, LANES), lambda j, i: (j, 0, 0))]
        args += [sink_rows.reshape(N_PAIRS, 2, LANES)]
    return pl.pallas_call(
        body, name=name, grid=(N_PAIRS, s // t), in_specs=in_specs,
        out_specs=[q_spec, pl.BlockSpec((None, 2, t), lambda j, i: (j, 0, i))],
        out_shape=[jax.ShapeDtypeStruct((s, N_PAIRS * LANES), bf16), jax.ShapeDtypeStruct((N_PAIRS, 2, s), f32)],
        compiler_params=_params(2),
    )(*args)


def _attn_delta(do, o, name, *, lse=None, sink_rows=None):
    s, hw = do.shape
    tm = _row_tile(s, 512)
    has_sink = sink_rows is not None

    def body(*refs):
        do_ref, o_ref = refs[:2]
        if has_sink:
            lse_ref, sink_ref, dl_ref, ds_ref = refs[2:]

            @pl.when(pl.program_id(0) == 0)
            def _():
                ds_ref[...] = jnp.zeros_like(ds_ref)
        else:
            dl_ref, = refs[2:]
        for j in range(N_PAIRS):
            cols = slice(j * LANES, (j + 1) * LANES)
            prod_t = (do_ref[:, cols].astype(f32) * o_ref[:, cols].astype(f32)).T
            for h in range(2):
                dl = jnp.sum(prod_t[h * HEAD_DIM:(h + 1) * HEAD_DIM, :], axis=0, keepdims=True)
                dl_ref[j, h:h + 1, :] = dl
                if has_sink:
                    r = 2 * j + h
                    p_sink = jnp.exp(sink_ref[r:r + 1, 0:1] - lse_ref[j, h:h + 1, :])
                    ds_ref[r:r + 1, :] += -jnp.sum(p_sink * dl, axis=1, keepdims=True)

    rows_spec = pl.BlockSpec((N_PAIRS, 2, tm), lambda i: (0, 0, i))
    in_specs, args = [_row_spec(tm, hw)] * 2, [do, o]
    out_specs, out_shape = [rows_spec], [jax.ShapeDtypeStruct((N_PAIRS, 2, s), f32)]
    if has_sink:
        in_specs += [rows_spec, _vec_spec(LANES, N_HEADS)]
        args += [lse, sink_rows]
        out_specs += [_vec_spec(LANES, N_HEADS)]
        out_shape += [jax.ShapeDtypeStruct((N_HEADS, LANES), f32)]
    return pl.pallas_call(
        body, name=name, grid=(s // tm,), in_specs=in_specs, out_specs=out_specs, out_shape=out_shape,
        compiler_params=_params(1),
    )(*args)


def _attn_bwd(q, k, v, do, lse, delta, name, *, cum_b=None, window=None, t=256):
    s = q.shape[0]
    t = _row_tile(s, t)
    nblk = s // t
    fox = cum_b is not None
    assert not window or (window % LANES == 0 and LANES + window <= s)

    def body(*refs):
        k_ref, v_ref, q_ref, do_ref, lse_ref, dl_ref = refs[:6]
        rest = list(refs[6:])
        cb_ref = rest.pop(0) if fox else None
        dq_ref, dk_ref, dv_ref = rest[:3]
        dcs_ref, rs_ref = (rest[3], rest[4]) if fox else (None, None)
        b = pl.program_id(1)
        k0 = pl.multiple_of(b * t, t)

        @pl.when(b == 0)
        def _():
            dq_ref[...] = jnp.zeros_like(dq_ref)
            if fox:
                rs_ref[...] = jnp.zeros_like(rs_ref)

        dk_ref[...] = jnp.zeros_like(dk_ref)
        dv_ref[...] = jnp.zeros_like(dv_ref)
        if fox:
            dcs_ref[...] = jnp.zeros_like(dcs_ref)
        low = _lane() < HEAD_DIM
        top = lax.broadcasted_iota(jnp.int32, (LANES, 1), 0) < HEAD_DIM
        kblk, vblk = k_ref[...], v_ref[...]
        k_t = kblk.astype(f32).T.astype(bf16)
        cks = [_wide(cb_ref[pl.ds(k0, t), h * LANES:(h + 1) * LANES], t) for h in range(2)] if fox else None

        def tile(q0, n_queries, off, masked, keys=slice(0, t)):
            cols = pl.ds(q0, n_queries)
            q2, do2 = q_ref[cols, :], do_ref[cols, :]
            zero = jnp.zeros_like(q2)
            valid = _tile_mask(keys.stop - keys.start, n_queries, off, window) if masked else None
            dq_parts = []
            for h in range(2):
                qm = jnp.where(low, q2, zero) if h == 0 else jnp.where(low, zero, q2)
                dom = jnp.where(low, do2, zero) if h == 0 else jnp.where(low, zero, do2)
                sc = lax.dot_general(kblk[keys], qm, _NT, preferred_element_type=f32)
                if fox:
                    sc = sc - cks[h]
                if masked:
                    sc = jnp.where(valid, sc, NEG)
                p = jnp.exp(sc - lse_ref[h:h + 1, cols])
                dp = lax.dot_general(vblk[keys], dom, _NT, preferred_element_type=f32)
                ds = p * (dp - dl_ref[h:h + 1, cols])
                pb, dsb = p.astype(bf16), ds.astype(bf16)
                dv_ref[keys, :] += jnp.dot(pb, dom, preferred_element_type=f32)
                dk_ref[keys, :] += jnp.dot(dsb, qm, preferred_element_type=f32)
                dq_parts.append(jnp.dot(k_t[:, keys], dsb, preferred_element_type=f32))
                if fox:
                    dcs_ref[:, h * LANES:(h + 1) * LANES] += sum(ds[:, g * LANES:(g + 1) * LANES] for g in range(t // LANES))
                    rs_ref[h:h + 1, cols] += jnp.sum(ds, axis=0, keepdims=True)
            dq_ref[:, cols] += jnp.where(top, dq_parts[0], dq_parts[1])

        def later_block(qb, carry):
            tile(pl.multiple_of(qb * t, t), t, 0, False)
            return carry

        if window:
            for c in range(t // LANES):
                first = b * t + c * LANES
                q0 = pl.multiple_of(jnp.minimum(first, s - (LANES + window)), LANES)
                tile(q0, LANES + window, q0 - first, True, slice(c * LANES, (c + 1) * LANES))
        else:
            tile(k0, t, 0, True)
            lax.fori_loop(b + 1, nblk, later_block, 0)

    kv_spec = pl.BlockSpec((t, LANES), lambda j, b: (b, j))
    seq_spec = pl.BlockSpec((s, LANES), lambda j, b: (0, j))
    rows_spec = pl.BlockSpec((None, 2, s), lambda j, b: (j, 0, 0))
    hw = N_PAIRS * LANES
    in_specs, args = [kv_spec, kv_spec, seq_spec, seq_spec, rows_spec, rows_spec], [k, v, q, do, lse, delta]
    out_specs = [pl.BlockSpec((LANES, s), lambda j, b: (j, 0)), kv_spec, kv_spec]
    out_shape = [jax.ShapeDtypeStruct((hw, s), f32), jax.ShapeDtypeStruct((s, hw), f32), jax.ShapeDtypeStruct((s, hw), f32)]
    if fox:
        in_specs += [pl.BlockSpec((s, 2 * LANES), lambda j, b: (0, j))]
        args += [cum_b]
        out_specs += [pl.BlockSpec((t, 2 * LANES), lambda j, b: (b, j)), rows_spec]
        out_shape += [jax.ShapeDtypeStruct((s, N_HEADS * LANES), f32), jax.ShapeDtypeStruct((N_PAIRS, 2, s), f32)]
    return pl.pallas_call(
        body, name=name, grid=(N_PAIRS, nblk), in_specs=in_specs, out_specs=out_specs, out_shape=out_shape,
        compiler_params=_params(2),
    )(*args)


def _merge(ba, bb, gl, name):
    s, d = ba.shape
    tm = _row_tile(s, 512)

    def body(a_ref, b_ref, g_ref, o_ref):
        g0, g1 = jax.nn.sigmoid(g_ref[:, :d].astype(f32)), jax.nn.sigmoid(g_ref[:, d:].astype(f32))
        o_ref[...] = (g0 * a_ref[...].astype(f32) + g1 * b_ref[...].astype(f32)).astype(bf16)

    return pl.pallas_call(
        body, name=name, grid=(s // tm,), in_specs=[_row_spec(tm, d)] * 2 + [_row_spec(tm, 2 * d)],
        out_specs=_row_spec(tm, d), out_shape=jax.ShapeDtypeStruct((s, d), bf16), compiler_params=_params(1),
    )(ba, bb, gl)


def _merge_bwd(dm, ba, bb, gl, name):
    s, d = ba.shape
    tm = _row_tile(s, 512)

    def body(dm_ref, a_ref, b_ref, g_ref, da_ref, db_ref, dg_ref):
        dmv = dm_ref[...].astype(f32)
        g0, g1 = jax.nn.sigmoid(g_ref[:, :d].astype(f32)), jax.nn.sigmoid(g_ref[:, d:].astype(f32))
        da_ref[...] = (dmv * g0).astype(bf16)
        db_ref[...] = (dmv * g1).astype(bf16)
        dg_ref[:, :d] = (dmv * a_ref[...].astype(f32) * (g0 * (1.0 - g0))).astype(bf16)
        dg_ref[:, d:] = (dmv * b_ref[...].astype(f32) * (g1 * (1.0 - g1))).astype(bf16)

    return pl.pallas_call(
        body, name=name, grid=(s // tm,), in_specs=[_row_spec(tm, d)] * 3 + [_row_spec(tm, 2 * d)],
        out_specs=[_row_spec(tm, d)] * 2 + [_row_spec(tm, 2 * d)],
        out_shape=[jax.ShapeDtypeStruct((s, d), bf16)] * 2 + [jax.ShapeDtypeStruct((s, 2 * d), bf16)],
        compiler_params=_params(1),
    )(dm, ba, bb, gl)


GLU_TILE = 256


def _glu_interleave(w):
    r, f2 = w.shape
    return jnp.transpose(w.reshape(r, 2, f2 // (2 * GLU_TILE), GLU_TILE), (0, 2, 1, 3)).reshape(r, f2)


def _glu_deinterleave(w):
    r, f2 = w.shape
    return jnp.transpose(w.reshape(r, f2 // (2 * GLU_TILE), 2, GLU_TILE), (0, 2, 1, 3)).reshape(r, f2)


def _ffn_in_swiglu(h, w_il, name):
    s, d = h.shape
    f2 = w_il.shape[1]
    tm = _row_tile(s, 2048)
    tg = GLU_TILE

    def body(h_ref, w_ref, gu_ref, act_ref):
        r = jnp.dot(h_ref[...], w_ref[...], preferred_element_type=f32)
        g, u = r[:, :tg], r[:, tg:]
        gu_ref[...] = r.astype(bf16)
        act_ref[...] = (g * jax.nn.sigmoid(g) * u).astype(bf16)

    return pl.pallas_call(
        body, name=name, grid=(s // tm, f2 // (2 * tg)),
        in_specs=[pl.BlockSpec((tm, d), lambda i, j: (i, 0)), pl.BlockSpec((d, 2 * tg), lambda i, j: (0, j))],
        out_specs=[pl.BlockSpec((tm, 2 * tg), lambda i, j: (i, j)), pl.BlockSpec((tm, tg), lambda i, j: (i, j))],
        out_shape=[jax.ShapeDtypeStruct((s, f2), bf16), jax.ShapeDtypeStruct((s, f2 // 2), bf16)], compiler_params=_params(2),
    )(h, w_il)


def _ffn_out_dgrad_swiglu(dy, w_out, gu, name):
    s, d = dy.shape
    f2 = gu.shape[1]
    tm = _row_tile(s, 2048)
    tg = GLU_TILE

    def body(dy_ref, w_ref, gu_ref, o_ref):
        dv = lax.dot_general(dy_ref[...], w_ref[...], _NT, preferred_element_type=f32)
        g, u = gu_ref[:, :tg].astype(f32), gu_ref[:, tg:].astype(f32)
        sg = jax.nn.sigmoid(g)
        o_ref[:, :tg] = (dv * u * (sg * (1.0 + g * (1.0 - sg)))).astype(bf16)
        o_ref[:, tg:] = (dv * (g * sg)).astype(bf16)

    return pl.pallas_call(
        body, name=name, grid=(s // tm, f2 // (2 * tg)),
        in_specs=[pl.BlockSpec((tm, d), lambda i, j: (i, 0)), pl.BlockSpec((tg, d), lambda i, j: (j, 0)),
                  pl.BlockSpec((tm, 2 * tg), lambda i, j: (i, j))],
        out_specs=pl.BlockSpec((tm, 2 * tg), lambda i, j: (i, j)),
        out_shape=jax.ShapeDtypeStruct((s, f2), bf16), compiler_params=_params(2),
    )(dy, w_out, gu)


def _ada_fwd(c_all, w, b, name):
    def body(c_ref, w_ref, b_ref, o_ref):
        o_ref[...] = jnp.dot(c_ref[...].astype(bf16), w_ref[...].astype(bf16), preferred_element_type=f32) + b_ref[...]

    return pl.pallas_call(
        body, name=name, out_shape=jax.ShapeDtypeStruct((c_all.shape[0], w.shape[1]), f32), compiler_params=_params(),
    )(c_all, w, b)


def _ada_wgrad(c_all, d_all, name):
    n, d = c_all.shape
    w = d_all.shape[1]

    def body(c_ref, d_ref, o_ref):
        eye = (lax.broadcasted_iota(jnp.int32, (n, n), 0) == lax.broadcasted_iota(jnp.int32, (n, n), 1)).astype(f32)
        ct = lax.dot_general(c_ref[...], eye, _TN, precision=lax.Precision.HIGHEST, preferred_element_type=f32)
        g = ct[:, 0:1] * d_ref[0:1, :]
        for bi in range(1, n):
            g = g + ct[:, bi:bi + 1] * d_ref[bi:bi + 1, :]
        o_ref[0] = g

    return pl.pallas_call(
        body, name=name, out_shape=jax.ShapeDtypeStruct((1, d, w), f32), compiler_params=_params(),
    )(c_all, d_all)


def _adamw(parts, w, m, v, name):
    r, c = w.shape
    n_parts = parts.shape[0]
    tr = next(t for t in range(min(r, 256), 0, -1) if r % t == 0 and (t % 16 == 0 or t == r))

    def body(p_ref, w_ref, m_ref, v_ref, g_ref, d_ref, nm_ref, nv_ref):
        g = p_ref[0].astype(f32)
        for i in range(1, n_parts):
            g = g + p_ref[i].astype(f32)
        mm = ADAM_B1 * m_ref[...] + (1.0 - ADAM_B1) * g
        vv = ADAM_B2 * v_ref[...] + (1.0 - ADAM_B2) * (g * g)
        m_hat = mm / (1.0 - ADAM_B1 ** ADAM_STEP)
        v_hat = vv / (1.0 - ADAM_B2 ** ADAM_STEP)
        g_ref[...] = g
        d_ref[...] = -ADAM_LR * (m_hat / (jnp.sqrt(v_hat) + ADAM_EPS) + ADAM_WD * w_ref[...])
        nm_ref[...] = mm
        nv_ref[...] = vv

    spec = pl.BlockSpec((tr, c), lambda i: (i, 0))
    return pl.pallas_call(
        body, name=name, grid=(r // tr,), in_specs=[pl.BlockSpec((n_parts, tr, c), lambda i: (0, i, 0))] + [spec] * 3,
        out_specs=[spec] * 4, out_shape=[jax.ShapeDtypeStruct((r, c), f32)] * 4, compiler_params=_params(1),
    )(parts, w, m, v)


def _me():
    return lax.axis_index("x"), lax.axis_index("y"), lax.axis_index("c")


def _all_gather(arrays, name, vmem=False):
    n = len(arrays)
    space = pltpu.VMEM if vmem else pl.ANY

    def body(*refs):
        ins, outs = refs[:n], refs[n:2 * n]
        send_sems, recv_sems, local_sems = refs[2 * n:]
        x, y, c = _me()
        me, sibling = (x, y, c), (x, y, 1 - c)
        chips = [(1 - x, y), (x, 1 - y), (1 - x, 1 - y)]

        def rows(a, dev):
            return outs[a].at[4 * dev[0] + 2 * dev[1] + dev[2]]

        def copy(a, k, block, to, src=None):
            return pltpu.make_async_remote_copy(
                src_ref=rows(a, block) if src is None else src, dst_ref=rows(a, block),
                send_sem=send_sems.at[a, k], recv_sem=recv_sems.at[a, k], device_id=to, device_id_type=MESH)

        mine = [pltpu.make_async_copy(ins[a], rows(a, me), local_sems.at[a]) for a in range(n)]
        for cp in mine:
            cp.start()
        first = []
        for a in range(n):
            first.append(copy(a, 0, me, sibling, src=ins[a]))
            first += [copy(a, 1 + j, me, (*chip, c), src=ins[a]) for j, chip in enumerate(chips)]
        for cp in first:
            cp.start()
        passed = []
        for j, chip in enumerate(chips):
            for a in range(n):
                copy(a, 1 + j, (*chip, c), me).wait_recv()
                fwd = copy(a, 4 + j, (*chip, c), sibling)
                fwd.start()
                passed.append(fwd)
        for a in range(n):
            copy(a, 0, sibling, me).wait_recv()
            for j, chip in enumerate(chips):
                copy(a, 4 + j, (*chip, 1 - c), me).wait_recv()
        for cp in first + passed:
            cp.wait_send()
        for cp in mine:
            cp.wait()

    outs = pl.pallas_call(
        body, name=name,
        in_specs=[pl.BlockSpec(memory_space=space)] * n, out_specs=[pl.BlockSpec(memory_space=space)] * n,
        out_shape=[jax.ShapeDtypeStruct((N_DEV,) + a.shape, a.dtype) for a in arrays],
        scratch_shapes=[pltpu.SemaphoreType.DMA((n, 7)), pltpu.SemaphoreType.DMA((n, 7)), pltpu.SemaphoreType.DMA((n,))],
        compiler_params=pltpu.CompilerParams(vmem_limit_bytes=VMEM_LIMIT),
    )(*arrays)
    return list(outs)


_FLIPS = ((0, 0, 1), (1, 0, 0), (0, 1, 0), (1, 1, 0), (1, 0, 1), (0, 1, 1), (1, 1, 1))
_HBM = pl.BlockSpec(memory_space=pltpu.HBM)
_SEM = pl.BlockSpec(memory_space=pltpu.SEMAPHORE)


def _exchange_copies(scatter, srcs, lands, send_sems, recv_sems):
    x, y, c = _me()
    me_row = 4 * x + 2 * y + c
    out = []
    for k, (fx, fy, fc) in enumerate(_FLIPS):
        peer = (x ^ fx, y ^ fy, c ^ fc)
        peer_row = 4 * peer[0] + 2 * peer[1] + peer[2]
        for a in range(len(srcs)):
            out.append(pltpu.make_async_remote_copy(
                src_ref=srcs[a].at[peer_row] if scatter else srcs[a], dst_ref=lands[a].at[me_row],
                send_sem=send_sems.at[7 * a + k], recv_sem=recv_sems.at[7 * a + k], device_id=peer, device_id_type=MESH))
    return out


def _exchange_start(arrays, scatter, name):
    n = len(arrays)
    lands = [lax.empty(a.shape if scatter else (N_DEV,) + a.shape, a.dtype) for a in arrays]

    def body(*refs):
        srcs, zones = refs[:n], refs[n:2 * n]
        send_sems, recv_sems = refs[2 * n], refs[2 * n + 1]
        token = refs[-1]
        for cp in _exchange_copies(scatter, srcs, zones, send_sems, recv_sems):
            cp.start()
        token[...] = jnp.zeros_like(token)

    thru = [pltpu.HBM(a.shape, a.dtype) for a in list(arrays) + lands]
    outs = pl.pallas_call(
        body, name=name,
        out_shape=(pltpu.SemaphoreType.DMA((7 * n,)), pltpu.SemaphoreType.DMA((7 * n,)), *thru, jax.ShapeDtypeStruct((8, LANES), f32)),
        in_specs=[_HBM] * (2 * n), out_specs=(_SEM, _SEM, *[_HBM] * (2 * n), pl.BlockSpec(memory_space=pltpu.VMEM)),
        input_output_aliases={i: 2 + i for i in range(2 * n)},
        compiler_params=pltpu.CompilerParams(has_side_effects=pltpu.SideEffectType.DATAFLOW_SIDE_EFFECTING),
    )(*[pltpu.with_memory_space_constraint(a, pltpu.HBM) for a in list(arrays) + lands])
    return dict(n=n, scatter=scatter, sems=outs[:2], srcs=outs[2:2 + n], lands=outs[2 + n:2 + 2 * n], token=outs[-1])


def _exchange_wait(handle, after, name):
    n, scatter = handle["n"], handle["scatter"]

    def body(*refs):
        srcs, zones = refs[:n], refs[n:2 * n]
        send_sems, recv_sems = refs[2 * n], refs[2 * n + 1]
        for cp in _exchange_copies(scatter, srcs, zones, send_sems, recv_sems):
            cp.wait_send()
            cp.wait_recv()

    thru = [pltpu.HBM(a.shape, a.dtype) for a in list(handle["srcs"]) + list(handle["lands"])]
    outs = pl.pallas_call(
        body, name=name, out_shape=tuple(thru),
        in_specs=[_HBM] * (2 * n) + [_SEM, _SEM, pl.BlockSpec(memory_space=pl.ANY)], out_specs=tuple([_HBM] * (2 * n)),
        input_output_aliases={i: i for i in range(2 * n)},
        compiler_params=pltpu.CompilerParams(has_side_effects=pltpu.SideEffectType.DATAFLOW_SIDE_EFFECTING),
    )(*handle["srcs"], *handle["lands"], *handle["sems"], after)
    return list(outs[n:])


def _cols_from_shards(g):
    return jnp.transpose(g, (1, 0, 2)).reshape(g.shape[1], -1)


def _shards_from_cols(a):
    return jnp.transpose(a.reshape(a.shape[0], N_DEV, -1), (1, 0, 2))


def _local_step(x, positions, ada, g_pre_mix, g_post_mix, b_f, sinks, g_pre_ffn, g_post_ffn, target,
                w_in, late_weights, on_grads):
    s, d = x.shape
    row = lambda v: v.reshape(1, -1)
    shift_m, scale_m, gate_m, shift_f, scale_f, gate_f = (ada[i:i + 1] for i in range(6))
    w_gate, w_qkv = w_in[:, F_OFF + N_HEADS:], w_in[:, :QKV_W]
    w_f = jnp.pad(w_in[:, F_OFF:F_OFF + N_HEADS], ((0, 0), (0, LANES - N_HEADS)))
    w_in_p = jnp.concatenate([w_gate, w_qkv, w_f], axis=1)
    bf_row = jnp.pad(row(b_f), ((0, 0), (0, LANES - N_HEADS)))
    sink_rows = jnp.broadcast_to(sinks.reshape(N_HEADS, 1).astype(f32), (N_HEADS, LANES))
    inv_freq = 1.0 / (ROPE_THETA ** (jnp.arange(0, HEAD_DIM, 2, dtype=f32) / HEAD_DIM))
    cos, sin_s = _rope_tables(positions.reshape(s, 1), jnp.tile(inv_freq, 4).reshape(1, LANES), "rope_tables")

    h1 = _prenorm(x, row(g_pre_mix), scale_m, shift_m, "prenorm_mix")
    gl = _matmul(h1, w_gate, "nn", bf16, "proj_gate")
    qkv = _matmul(h1, w_qkv, "nn", f32, "proj_qkv")
    fl = _matmul(h1, w_f, "nn", f32, "proj_forget")
    qa, ka, va, qb, kb, vb = _qkv_prep(qkv, cos, sin_s, "qkv_prep")
    cum_b = _forget_prep(fl, bf_row, "forget_prep")
    o_a, lse_a = _attn_fwd(qa, ka, va, "swa_fwd", sink_rows=sink_rows, window=WINDOW, t=512)
    o_b, lse_b = _attn_fwd(qb, kb, vb, "fox_fwd", cum_b=cum_b, t=512)
    w_branch_a, w_branch_b, w_out, w_ffn_in, w_ffn_out = late_weights(o_b)
    ba = _matmul(o_a, w_branch_a, "nn", bf16, "branch_a")
    bb = _matmul(o_b, w_branch_b, "nn", bf16, "branch_b")
    merged = _merge(ba, bb, gl, "merge")
    y1 = _matmul(merged, w_out, "nn", f32, "out_proj")
    x2 = _postnorm_res(x, y1, row(g_post_mix), gate_m, "postnorm_mix")

    h2 = _prenorm(x2, row(g_pre_ffn), scale_f, shift_f, "prenorm_ffn")
    w_ffn_il = _glu_interleave(w_ffn_in)
    gu, act = _ffn_in_swiglu(h2, w_ffn_il, "ffn_in_swiglu")
    y2 = _matmul(act, w_ffn_out, "nn", f32, "ffn_out")
    loss_row, d_out, d_y2, vec_pf = _loss_tail(x2, y2, row(g_post_ffn), gate_f, target, "loss_tail")

    g_w_ffn_out = _matmul(act, d_y2, "tn", bf16, "ffn_out_wgrad")
    dgu = _ffn_out_dgrad_swiglu(d_y2, w_ffn_out, gu, "ffn_out_dgrad_swiglu")
    g_w_ffn_in = _glu_deinterleave(_matmul(h2, dgu, "tn", bf16, "ffn_in_wgrad"))
    sent = on_grads(dict(w_ffn_in=g_w_ffn_in, w_ffn_out=g_w_ffn_out))
    d_h2 = _matmul(dgu, w_ffn_il, "nt", f32, "ffn_in_dgrad", after=sent)
    d_x2, vec_nf = _prenorm_bwd(d_h2, x2, row(g_pre_ffn), scale_f, d_out, "prenorm_ffn_bwd")

    d_y1, vec_pm = _postnorm_bwd(d_x2, y1, row(g_post_mix), gate_m, "postnorm_mix_bwd")
    g_w_out = _matmul(merged, d_y1, "tn", bf16, "out_proj_wgrad")
    d_merged = _matmul(d_y1, w_out, "nt", bf16, "out_proj_dgrad")
    d_ba, d_bb, dgl = _merge_bwd(d_merged, ba, bb, gl, "merge_bwd")
    g_w_branch_a = _matmul(o_a, d_ba, "tn", bf16, "branch_a_wgrad")
    g_w_branch_b = _matmul(o_b, d_bb, "tn", bf16, "branch_b_wgrad")
    sent = on_grads(dict(w_out=g_w_out, w_branch_a=g_w_branch_a, w_branch_b=g_w_branch_b))
    d_oa = _matmul(d_ba, w_branch_a, "nt", bf16, "branch_a_dgrad", after=sent)
    d_ob = _matmul(d_bb, w_branch_b, "nt", bf16, "branch_b_dgrad", after=sent)
    delta_a, d_sink = _attn_delta(d_oa, o_a, "swa_delta", lse=lse_a, sink_rows=sink_rows)
    delta_b, = _attn_delta(d_ob, o_b, "fox_delta")
    dqa_t, dka, dva = _attn_bwd(qa, ka, va, d_oa, lse_a, delta_a, "swa_bwd", window=WINDOW, t=512)
    dqb_t, dkb, dvb, dcs, rs = _attn_bwd(qb, kb, vb, d_ob, lse_b, delta_b, "fox_bwd", cum_b=cum_b, t=512)
    dqkv = _qkv_prep_bwd(dqa_t, dka, dva, dqb_t, dkb, dvb, cos, sin_s, "qkv_prep_bwd")
    dfl, vec_bf = _forget_prep_bwd(rs.reshape(N_HEADS, s), dcs, fl, bf_row, "forget_prep_bwd")
    dproj = jnp.concatenate([dgl, dqkv, dfl], axis=1)
    g_w_in_p = _matmul(h1, dproj, "tn", bf16, "in_proj_wgrad")
    g_w_in = jnp.concatenate([g_w_in_p[:, GATE_W:GATE_W + QKV_W], g_w_in_p[:, GATE_W + QKV_W:GATE_W + QKV_W + N_HEADS],
                              g_w_in_p[:, :GATE_W]], axis=1)
    sent = on_grads(dict(w_in=g_w_in))
    d_h1 = _matmul(dproj, w_in_p, "nt", f32, "in_proj_dgrad", after=sent)
    grad_x, vec_nm = _prenorm_bwd(d_h1, x, row(g_pre_mix), scale_m, d_x2, "prenorm_mix_bwd")

    d_ada = jnp.concatenate([vec_nm[0], vec_nm[1], vec_pm[0], vec_nf[0], vec_nf[1], vec_pf[0]])
    small = dict(b_ada=d_ada, g_pre_mix=vec_nm[2], g_post_mix=vec_pm[1], g_pre_ffn=vec_nf[2], g_post_ffn=vec_pf[1],
                 b_f=vec_bf[0, :N_HEADS], sinks=d_sink[:, 0], loss=loss_row[0, :1])
    return grad_x, small


_SMALL = (("b_ada", 6144), ("g_pre_mix", 1024), ("g_post_mix", 1024), ("g_pre_ffn", 1024), ("g_post_ffn", 1024),
          ("b_f", 128), ("sinks", 128), ("loss", 128))
_SMALL_ROWS = 88


def _pack_small(vals):
    parts = [jnp.pad(vals[k].reshape(-1).astype(f32), (0, n - vals[k].size)) for k, n in _SMALL]
    flat = jnp.concatenate(parts)
    return jnp.pad(flat, (0, _SMALL_ROWS * LANES - flat.size)).reshape(_SMALL_ROWS, LANES)


def _unpack_small(slab, shapes):
    flat, out, off = slab.reshape(-1), {}, 0
    for k, n in _SMALL:
        size = math.prod(shapes[k])
        out[k] = flat[off:off + size].reshape(shapes[k])
        off += n
    return out


def kernel(x, c, positions, w_ada, b_ada, g_pre_mix, g_post_mix, w_in, b_f, sinks, w_branch_a, w_branch_b, w_out, g_pre_ffn, g_post_ffn, w_ffn_in, w_ffn_out, loss_target, m_w_ada, m_b_ada, m_g_pre_mix, m_g_post_mix, m_w_in, m_b_f, m_sinks, m_w_branch_a, m_w_branch_b, m_w_out, m_g_pre_ffn, m_g_post_ffn, m_w_ffn_in, m_w_ffn_out, v_w_ada, v_b_ada, v_g_pre_mix, v_g_post_mix, v_w_in, v_b_f, v_sinks, v_w_branch_a, v_w_branch_b, v_w_out, v_g_pre_ffn, v_g_post_ffn, v_w_ffn_in, v_w_ffn_out):
    xi, yi, ci = _me()
    me = 4 * xi + 2 * yi + ci
    d = D_MODEL
    ada_w = w_ada.shape[2]

    c_all, = _all_gather([c], "gather_c", vmem=True)
    c_all = c_all.reshape(N_DEV, d)
    b_mine = lax.dynamic_slice(b_ada, (0, me * ada_w), (1, ada_w))
    ada_cols = _ada_fwd(c_all, w_ada[0], b_mine, "ada_fwd")
    ada_all, = _all_gather([ada_cols], "gather_ada", vmem=True)
    ada = lax.dynamic_index_in_dim(ada_all, me, axis=1, keepdims=False).reshape(6, d)

    g_in, = _all_gather([w_in[0].astype(bf16)], "gather_w_in")
    late = [w.astype(bf16) for w in (w_branch_a[0], w_branch_b[0], w_out[0], w_ffn_in[0], w_ffn_out[0])]
    late_h = _exchange_start(late, False, "gather_late_start")

    def mine_into(zone, block):
        return lax.dynamic_update_index_in_dim(zone, block, me, 0)

    def late_weights(after):
        zones = _exchange_wait(late_h, after, "gather_late_wait")
        g_ba, g_bb, g_out, g_fi, g_fo = (mine_into(z, w) for z, w in zip(zones, late))
        return (_cols_from_shards(g_ba), _cols_from_shards(g_bb), g_out.reshape(d, d), _cols_from_shards(g_fi),
                g_fo.reshape(D_FF, d))

    row_sharded = ("w_out", "w_ffn_out")
    in_flight = []

    def on_grads(group):
        sends = [g.reshape(N_DEV, g.shape[0] // N_DEV, g.shape[1]) if nm in row_sharded else _shards_from_cols(g)
                 for nm, g in group.items()]
        handle = _exchange_start(sends, True, "scatter_start_%d" % len(in_flight))
        in_flight.append((list(group), sends, handle))
        return handle["token"]

    grad_x, small = _local_step(
        x[0], positions[0], ada + late_h["token"][0, 0], g_pre_mix[0], g_post_mix[0], b_f[0], sinks[0], g_pre_ffn[0],
        g_post_ffn[0], loss_target[0], _cols_from_shards(g_in), late_weights, on_grads)

    slab_all, = _all_gather([_pack_small(small)], "gather_small", vmem=True)
    small_w = dict(b_ada=b_ada, g_pre_mix=g_pre_mix, g_post_mix=g_post_mix, g_pre_ffn=g_pre_ffn, g_post_ffn=g_post_ffn,
                   b_f=b_f, sinks=sinks, loss=jnp.zeros((1,), f32))
    small_m = dict(b_ada=m_b_ada, g_pre_mix=m_g_pre_mix, g_post_mix=m_g_post_mix, g_pre_ffn=m_g_pre_ffn,
                   g_post_ffn=m_g_post_ffn, b_f=m_b_f, sinks=m_sinks, loss=jnp.zeros((1,), f32))
    small_v = dict(b_ada=v_b_ada, g_pre_mix=v_g_pre_mix, g_post_mix=v_g_post_mix, g_pre_ffn=v_g_pre_ffn,
                   g_post_ffn=v_g_post_ffn, b_f=v_b_f, sinks=v_sinks, loss=jnp.ones((1,), f32))
    shapes = {k: small_w[k].shape for k, _ in _SMALL}
    s_out = _adamw(slab_all, _pack_small(small_w), _pack_small(small_m), _pack_small(small_v), "adamw_small")
    s_grad, s_delta, s_m, s_v = (_unpack_small(o, shapes) for o in s_out)

    d_ada_all = lax.dynamic_slice(slab_all[:, :6144 // LANES, :].reshape(N_DEV, 6144), (0, me * ada_w), (N_DEV, ada_w))
    ada_parts = _ada_wgrad(c_all, d_ada_all, "ada_wgrad")

    ws = dict(w_in=(w_in, m_w_in, v_w_in), w_branch_a=(w_branch_a, m_w_branch_a, v_w_branch_a),
              w_branch_b=(w_branch_b, m_w_branch_b, v_w_branch_b), w_out=(w_out, m_w_out, v_w_out),
              w_ffn_in=(w_ffn_in, m_w_ffn_in, v_w_ffn_in), w_ffn_out=(w_ffn_out, m_w_ffn_out, v_w_ffn_out))
    res = {"w_ada": _adamw(ada_parts, w_ada[0], m_w_ada[0], v_w_ada[0], "adamw_w_ada")}
    after = res["w_ada"][0]
    for gi, (names, sends, handle) in enumerate(in_flight):
        zones = _exchange_wait(handle, after, "scatter_wait_%d" % gi)
        for nm, zone, sent in zip(names, zones, sends):
            w, m, v = ws[nm]
            parts = mine_into(zone, lax.dynamic_index_in_dim(sent, me, 0, keepdims=False))
            res[nm] = _adamw(parts, w[0], m[0], v[0], "adamw_" + nm)
            after = res[nm][0]

    order = ["w_ada", "b_ada", "g_pre_mix", "g_post_mix", "w_in", "b_f", "sinks", "w_branch_a", "w_branch_b", "w_out",
             "g_pre_ffn", "g_post_ffn", "w_ffn_in", "w_ffn_out"]
    outs = [s_grad["loss"].reshape(()), grad_x[None]]
    for which, small_o in enumerate((s_grad, s_delta, s_m, s_v)):
        for nm in order:
            outs.append(res[nm][which][None] if nm in res else small_o[nm])
    return tuple(outs)
```

```python
import functools
import math

import jax
import jax.numpy as jnp
from jax import lax
from jax.experimental import pallas as pl
from jax.experimental.pallas import tpu as pltpu

f32 = jnp.float32
bf16 = jnp.bfloat16

D_MODEL = 1024
HEAD_DIM = 64
N_HEADS = 8
N_PAIRS = 4
QKV_W = 2304
GATE_W = 2048
F_OFF = 2304
IN_W = 4360
WINDOW = 128
ROPE_THETA = 10000.0
RMS_EPS = 1e-6
D_FF = 2816
N_DEV = 8
ADAM_LR, ADAM_B1, ADAM_B2, ADAM_EPS, ADAM_WD, ADAM_STEP = 0.001, 0.9, 0.999, 1e-08, 0.01, 10
NEG = -1e30
LANES = 128
VMEM_LIMIT = 48 * 1024 * 1024
MESH = pl.DeviceIdType.MESH

_NT = (((1,), (1,)), ((), ()))
_TN = (((0,), (0,)), ((), ()))


def _params(n_grid=0):
    sem = ("arbitrary",) * n_grid if n_grid else None
    return pltpu.CompilerParams(dimension_semantics=sem, vmem_limit_bytes=VMEM_LIMIT)


def _row_tile(s, want):
    t = min(s, want)
    assert s % t == 0, (s, t)
    return t


MATMUL_VMEM_BUDGET = 40 * 1024 * 1024


def _matmul_tiles(m, n, k, a_item, b_item, o_item):
    def tiles(d):
        return [t for t in range(LANES, min(d, 2048) + 1, LANES) if d % t == 0] or [d]

    best = None
    for tm in tiles(m):
        for tn in tiles(n):
            vmem = 2 * (tm * k * a_item + tn * k * b_item + tm * tn * o_item) + tm * tn * 4
            if vmem > MATMUL_VMEM_BUDGET:
                continue
            traffic = m * k * a_item + n * k * b_item * (1 if tn == n else m // tm) + m * n * o_item
            steps = (m // tm) * (n // tn)
            key = (traffic, 0, steps) if steps >= 4 else (traffic, 1, -steps)
            if best is None or key < best[0]:
                best = (key, tm, tn)
    assert best is not None, (m, n, k)
    return best[1], best[2]


def _matmul(a, b, mode, out_dtype, name, after=None):
    if mode == "nn":
        (m, k), n = a.shape, b.shape[1]
    elif mode == "nt":
        (m, k), n = a.shape, b.shape[0]
    else:
        (k, m), n = a.shape, b.shape[1]
    tm, tn = _matmul_tiles(m, n, k, a.dtype.itemsize, b.dtype.itemsize, jnp.dtype(out_dtype).itemsize)
    if mode == "nn":
        a_spec, b_spec, dims = pl.BlockSpec((tm, k), lambda i, j: (i, 0)), pl.BlockSpec((k, tn), lambda i, j: (0, j)), None
    elif mode == "nt":
        a_spec, b_spec, dims = pl.BlockSpec((tm, k), lambda i, j: (i, 0)), pl.BlockSpec((tn, k), lambda i, j: (j, 0)), _NT
    else:
        a_spec, b_spec, dims = pl.BlockSpec((k, tm), lambda i, j: (0, i)), pl.BlockSpec((k, tn), lambda i, j: (0, j)), _TN

    def body(a_ref, b_ref, *rest):
        o_ref = rest[-1]
        av, bv = a_ref[...].astype(bf16), b_ref[...].astype(bf16)
        if dims is None:
            r = jnp.dot(av, bv, preferred_element_type=f32)
        else:
            r = lax.dot_general(av, bv, dims, preferred_element_type=f32)
        o_ref[...] = r.astype(out_dtype)

    extra = [] if after is None else [after]
    return pl.pallas_call(
        body, name=name, grid=(m // tm, n // tn), in_specs=[a_spec, b_spec] + [pl.BlockSpec(memory_space=pl.ANY)] * len(extra),
        out_specs=pl.BlockSpec((tm, tn), lambda i, j: (i, j)),
        out_shape=jax.ShapeDtypeStruct((m, n), out_dtype), compiler_params=_params(2),
    )(a, b, *extra)


def _rstd(v):
    return lax.rsqrt(jnp.mean(v * v, axis=-1, keepdims=True) + RMS_EPS)


def _row_spec(tm, d):
    return pl.BlockSpec((tm, d), lambda i: (i, 0))


def _vec_spec(d, rows=1):
    return pl.BlockSpec((rows, d), lambda i: (0, 0))


def _prenorm(x, g, scale, shift, name):
    s, d = x.shape
    tm = _row_tile(s, 512)

    def body(x_ref, g_ref, sc_ref, sh_ref, h_ref):
        xv = x_ref[...]
        h = (xv * _rstd(xv) * g_ref[...]) * (1.0 + sc_ref[...]) + sh_ref[...]
        h_ref[...] = h.astype(bf16)

    return pl.pallas_call(
        body, name=name, grid=(s // tm,), in_specs=[_row_spec(tm, d)] + [_vec_spec(d)] * 3,
        out_specs=_row_spec(tm, d), out_shape=jax.ShapeDtypeStruct((s, d), bf16), compiler_params=_params(1),
    )(x, g, scale, shift)


def _postnorm_res(x, y, g, gate, name):
    s, d = x.shape
    tm = _row_tile(s, 512)

    def body(x_ref, y_ref, g_ref, gate_ref, o_ref):
        yv = y_ref[...]
        o_ref[...] = x_ref[...] + gate_ref[...] * (yv * _rstd(yv) * g_ref[...])

    return pl.pallas_call(
        body, name=name, grid=(s // tm,), in_specs=[_row_spec(tm, d)] * 2 + [_vec_spec(d)] * 2,
        out_specs=_row_spec(tm, d), out_shape=jax.ShapeDtypeStruct((s, d), f32), compiler_params=_params(1),
    )(x, y, g, gate)


def _rms_bwd(u, v, r):
    return r * u - v * (r * r * r) * jnp.mean(u * v, axis=-1, keepdims=True)


def _loss_tail(x, y, g, gate, target, name):
    s, d = x.shape
    tm = _row_tile(s, 512)

    def body(x_ref, y_ref, g_ref, gate_ref, t_ref, loss_ref, do_ref, dy_ref, vec_ref):
        @pl.when(pl.program_id(0) == 0)
        def _():
            loss_ref[...] = jnp.zeros_like(loss_ref)
            vec_ref[...] = jnp.zeros_like(vec_ref)
        yv = y_ref[...]
        r = _rstd(yv)
        yn = yv * r
        err = x_ref[...] + gate_ref[...] * (yn * g_ref[...]) - t_ref[...]
        loss_ref[...] += 0.5 * jnp.sum(jnp.mean(err * err, axis=-1, keepdims=True), axis=0, keepdims=True)
        dr = err / d
        do_ref[...] = dr
        dn = dr * gate_ref[...]
        vec_ref[0:1, :] += jnp.sum(dr * (yn * g_ref[...]), axis=0, keepdims=True)
        vec_ref[1:2, :] += jnp.sum(dn * yn, axis=0, keepdims=True)
        dy_ref[...] = _rms_bwd(dn * g_ref[...], yv, r).astype(bf16)

    return pl.pallas_call(
        body, name=name, grid=(s // tm,), in_specs=[_row_spec(tm, d)] * 2 + [_vec_spec(d)] * 2 + [_row_spec(tm, d)],
        out_specs=[_vec_spec(LANES), _row_spec(tm, d), _row_spec(tm, d), _vec_spec(d, 8)],
        out_shape=[jax.ShapeDtypeStruct((1, LANES), f32), jax.ShapeDtypeStruct((s, d), f32),
                   jax.ShapeDtypeStruct((s, d), bf16), jax.ShapeDtypeStruct((8, d), f32)],
        compiler_params=_params(1),
    )(x, y, g, gate, target)


def _postnorm_bwd(dres, y, g, gate, name):
    s, d = y.shape
    tm = _row_tile(s, 512)

    def body(dr_ref, y_ref, g_ref, gate_ref, dy_ref, vec_ref):
        @pl.when(pl.program_id(0) == 0)
        def _():
            vec_ref[...] = jnp.zeros_like(vec_ref)
        dr, yv = dr_ref[...], y_ref[...]
        r = _rstd(yv)
        yn = yv * r
        dn = dr * gate_ref[...]
        vec_ref[0:1, :] += jnp.sum(dr * (yn * g_ref[...]), axis=0, keepdims=True)
        vec_ref[1:2, :] += jnp.sum(dn * yn, axis=0, keepdims=True)
        dy_ref[...] = _rms_bwd(dn * g_ref[...], yv, r).astype(bf16)

    return pl.pallas_call(
        body, name=name, grid=(s // tm,), in_specs=[_row_spec(tm, d)] * 2 + [_vec_spec(d)] * 2,
        out_specs=[_row_spec(tm, d), _vec_spec(d, 8)],
        out_shape=[jax.ShapeDtypeStruct((s, d), bf16), jax.ShapeDtypeStruct((8, d), f32)], compiler_params=_params(1),
    )(dres, y, g, gate)


def _prenorm_bwd(dh, x, g, scale, dres, name):
    s, d = x.shape
    tm = _row_tile(s, 512)

    def body(dh_ref, x_ref, g_ref, sc_ref, dr_ref, dx_ref, vec_ref):
        @pl.when(pl.program_id(0) == 0)
        def _():
            vec_ref[...] = jnp.zeros_like(vec_ref)
        dhv, xv = dh_ref[...], x_ref[...]
        r = _rstd(xv)
        xn = xv * r
        dn = dhv * (1.0 + sc_ref[...])
        vec_ref[0:1, :] += jnp.sum(dhv, axis=0, keepdims=True)
        vec_ref[1:2, :] += jnp.sum(dhv * (xn * g_ref[...]), axis=0, keepdims=True)
        vec_ref[2:3, :] += jnp.sum(dn * xn, axis=0, keepdims=True)
        dx_ref[...] = dr_ref[...] + _rms_bwd(dn * g_ref[...], xv, r)

    return pl.pallas_call(
        body, name=name, grid=(s // tm,),
        in_specs=[_row_spec(tm, d)] * 2 + [_vec_spec(d)] * 2 + [_row_spec(tm, d)],
        out_specs=[_row_spec(tm, d), _vec_spec(d, 8)],
        out_shape=[jax.ShapeDtypeStruct((s, d), f32), jax.ShapeDtypeStruct((8, d), f32)], compiler_params=_params(1),
    )(dh, x, g, scale, dres)


def _lane():
    return lax.broadcasted_iota(jnp.int32, (1, LANES), 1)


def _rope_tables(pos_col, inv_freq, name):
    s = pos_col.shape[0]

    def body(p_ref, f_ref, cos_ref, sin_ref):
        ang = p_ref[...].astype(f32) * f_ref[...]
        first_half = (_lane() % HEAD_DIM) < HEAD_DIM // 2
        cos_ref[...] = jnp.cos(ang)
        sn = jnp.sin(ang)
        sin_ref[...] = jnp.where(first_half, -sn, sn)

    return pl.pallas_call(
        body, name=name, out_shape=[jax.ShapeDtypeStruct((s, LANES), f32)] * 2, compiler_params=_params(),
    )(pos_col, inv_freq)


def _swap_halves(v):
    first_half = (_lane() % HEAD_DIM) < HEAD_DIM // 2
    return jnp.where(first_half, pltpu.roll(v, LANES - HEAD_DIM // 2, axis=1), pltpu.roll(v, HEAD_DIM // 2, axis=1))


def _qkv_prep(qkv, cos, sin_s, name):
    s = qkv.shape[0]
    tm = _row_tile(s, 256)
    scale = 1.0 / math.sqrt(HEAD_DIM)

    def body(p_ref, c_ref, s_ref, qa_ref, ka_ref, va_ref, qb_ref, kb_ref, vb_ref):
        cs, sn = c_ref[...], s_ref[...]
        low = _lane() < HEAD_DIM

        def blk(j):
            return p_ref[:, j * LANES:(j + 1) * LANES]

        def rope(v):
            return v * cs + _swap_halves(v) * sn

        def expand(v):
            other = pltpu.roll(v, HEAD_DIM, axis=1)
            return jnp.where(low, v, other), jnp.where(low, other, v)

        for j in range(N_PAIRS):
            qa_ref[:, j * LANES:(j + 1) * LANES] = (rope(blk(j)) * scale).astype(bf16)
            qb_ref[:, j * LANES:(j + 1) * LANES] = (blk(6 + j) * scale).astype(bf16)
            kb_ref[:, j * LANES:(j + 1) * LANES] = blk(10 + j).astype(bf16)
            vb_ref[:, j * LANES:(j + 1) * LANES] = blk(14 + j).astype(bf16)
        k0, k1 = expand(rope(blk(4)))
        v0, v1 = expand(blk(5))
        for j in range(N_PAIRS):
            ka_ref[:, j * LANES:(j + 1) * LANES] = (k0 if j < 2 else k1).astype(bf16)
            va_ref[:, j * LANES:(j + 1) * LANES] = (v0 if j < 2 else v1).astype(bf16)

    hw = N_PAIRS * LANES
    return pl.pallas_call(
        body, name=name, grid=(s // tm,),
        in_specs=[_row_spec(tm, QKV_W), _row_spec(tm, LANES), _row_spec(tm, LANES)],
        out_specs=[_row_spec(tm, hw)] * 6, out_shape=[jax.ShapeDtypeStruct((s, hw), bf16)] * 6, compiler_params=_params(1),
    )(qkv, cos, sin_s)


def _qkv_prep_bwd(dqa_t, dka, dva, dqb_t, dkb, dvb, cos, sin_s, name):
    s = dka.shape[0]
    tm = _row_tile(s, 256)
    scale = 1.0 / math.sqrt(HEAD_DIM)
    hw = N_PAIRS * LANES
    t_spec = pl.BlockSpec((hw, tm), lambda i: (0, i))

    def body(dqa_ref, dka_ref, dva_ref, dqb_ref, dkb_ref, dvb_ref, c_ref, s_ref, o_ref):
        cs, sn = c_ref[...], s_ref[...]
        low = _lane() < HEAD_DIM

        def blk(ref, j):
            return ref[:, j * LANES:(j + 1) * LANES]

        def blk_t(ref, j):
            return ref[j * LANES:(j + 1) * LANES, :].T

        def unrope(v):
            return v * cs + _swap_halves(v * sn)

        def fold(ref):
            a, b = blk(ref, 0) + blk(ref, 1), blk(ref, 2) + blk(ref, 3)
            kv0 = a + pltpu.roll(a, HEAD_DIM, axis=1)
            kv1 = b + pltpu.roll(b, HEAD_DIM, axis=1)
            return jnp.where(low, kv0, kv1)

        for j in range(N_PAIRS):
            o_ref[:, j * LANES:(j + 1) * LANES] = (unrope(blk_t(dqa_ref, j)) * scale).astype(bf16)
            o_ref[:, (6 + j) * LANES:(7 + j) * LANES] = (blk_t(dqb_ref, j) * scale).astype(bf16)
            o_ref[:, (10 + j) * LANES:(11 + j) * LANES] = blk(dkb_ref, j).astype(bf16)
            o_ref[:, (14 + j) * LANES:(15 + j) * LANES] = blk(dvb_ref, j).astype(bf16)
        o_ref[:, 4 * LANES:5 * LANES] = unrope(fold(dka_ref)).astype(bf16)
        o_ref[:, 5 * LANES:6 * LANES] = fold(dva_ref).astype(bf16)

    return pl.pallas_call(
        body, name=name, grid=(s // tm,),
        in_specs=[t_spec, _row_spec(tm, hw), _row_spec(tm, hw), t_spec, _row_spec(tm, hw), _row_spec(tm, hw)] + [_row_spec(tm, LANES)] * 2,
        out_specs=_row_spec(tm, QKV_W), out_shape=jax.ShapeDtypeStruct((s, QKV_W), bf16), compiler_params=_params(1),
    )(dqa_t, dka, dva, dqb_t, dkb, dvb, cos, sin_s)


def _cumsum_rows(v, reverse=False):
    n = v.shape[0]
    row = lax.broadcasted_iota(jnp.int32, v.shape, 0)
    sh = 1
    while sh < n:
        if reverse:
            v = v + jnp.where(row < n - sh, pltpu.roll(v, n - sh, axis=0), 0.0)
        else:
            v = v + jnp.where(row >= sh, pltpu.roll(v, sh, axis=0), 0.0)
        sh *= 2
    return v


def _log_sigmoid(z):
    return jnp.minimum(z, 0.0) - jnp.log1p(jnp.exp(-jnp.abs(z)))


def _forget_prep(fl, bf_row, name):
    s = fl.shape[0]

    def body(f_ref, b_ref, cb_ref):
        cum = _cumsum_rows(_log_sigmoid(f_ref[...] + b_ref[...]))
        for h in range(N_HEADS):
            cb_ref[:, h * LANES:(h + 1) * LANES] = jnp.broadcast_to(cum[:, h:h + 1], (s, LANES))

    return pl.pallas_call(
        body, name=name, out_shape=jax.ShapeDtypeStruct((s, N_HEADS * LANES), f32), compiler_params=_params(),
    )(fl, bf_row)


def _forget_prep_bwd(rs, dcs, fl, bf_row, name):
    s = fl.shape[0]

    def body(r_ref, c_ref, f_ref, b_ref, df_ref, db_ref):
        eye = (lax.broadcasted_iota(jnp.int32, (N_HEADS, LANES), 0) == lax.broadcasted_iota(jnp.int32, (N_HEADS, LANES), 1)).astype(f32)
        dcum = lax.dot_general(r_ref[...], eye, _TN, precision=lax.Precision.HIGHEST, preferred_element_type=f32)
        for h in range(N_HEADS):
            dcum = dcum - jnp.where(_lane() == h, jnp.sum(c_ref[:, h * LANES:(h + 1) * LANES], axis=1, keepdims=True), 0.0)
        dlf = _cumsum_rows(dcum, reverse=True)
        z = f_ref[...] + b_ref[...]
        df = jnp.where(_lane() < N_HEADS, dlf * jax.nn.sigmoid(-z), 0.0)
        df_ref[...] = df.astype(bf16)
        db_ref[...] = jnp.zeros_like(db_ref)
        db_ref[0:1, :] = jnp.sum(df, axis=0, keepdims=True)

    return pl.pallas_call(
        body, name=name,
        out_shape=[jax.ShapeDtypeStruct((s, LANES), bf16), jax.ShapeDtypeStruct((8, LANES), f32)], compiler_params=_params(),
    )(rs, dcs, fl, bf_row)


def _tile_mask(n_keys, n_queries, off, window):
    shape = (n_keys, n_queries)
    d = lax.broadcasted_iota(jnp.int32, shape, 1) - lax.broadcasted_iota(jnp.int32, shape, 0) + off
    valid = d >= 0
    return jnp.logical_and(valid, d < window) if window else valid


def _wide(v, t):
    return jnp.concatenate([v] * (t // LANES), axis=1)


def _attn_fwd(q, k, v, name, *, cum_b=None, sink_rows=None, window=None, t=256):
    s = q.shape[0]
    t = _row_tile(s, t)
    fox, has_sink = cum_b is not None, sink_rows is not None
    assert not window or (window % LANES == 0 and LANES + window <= s)

    def body(*refs):
        q_ref, k_ref, v_ref = refs[:3]
        rest = list(refs[3:])
        cb_ref = rest.pop(0) if fox else None
        sink_ref = rest.pop(0) if has_sink else None
        o_ref, lse_ref = rest
        i = pl.program_id(1)
        low = _lane() < HEAD_DIM
        top = lax.broadcasted_iota(jnp.int32, (LANES, 1), 0) < HEAD_DIM
        q2 = q_ref[...]
        zero = jnp.zeros_like(q2)
        qms = (jnp.where(low, q2, zero), jnp.where(low, zero, q2))

        def tile(k0, n_keys, off, carry, masked, queries=slice(0, t)):
            nq = queries.stop - queries.start
            kblk, vblk = k_ref[pl.ds(k0, n_keys), :], v_ref[pl.ds(k0, n_keys), :]
            valid = _tile_mask(n_keys, nq, off, window) if masked else None
            out = []
            for h in range(2):
                m, l, acc = carry[h]
                sc = lax.dot_general(kblk, qms[h][queries], _NT, preferred_element_type=f32)
                if fox:
                    sc = sc - _wide(cb_ref[pl.ds(k0, n_keys), h * LANES:(h + 1) * LANES], nq)
                if masked:
                    sc = jnp.where(valid, sc, NEG)
                m_new = jnp.maximum(m, jnp.max(sc, axis=0, keepdims=True))
                p = jnp.exp(sc - m_new)
                alpha = jnp.exp(m - m_new)
                l = alpha * l + jnp.sum(p, axis=0, keepdims=True)
                acc = alpha * acc + lax.dot_general(vblk, p.astype(bf16), _TN, preferred_element_type=f32)
                out.append((m_new, l, acc))
            return tuple(out)

        def start(nq):
            if has_sink:
                return tuple((_wide(sink_ref[h:h + 1, :], nq), jnp.ones((1, nq), f32), jnp.zeros((LANES, nq), f32))
                             for h in range(2))
            return tuple((jnp.full((1, nq), NEG, f32), jnp.zeros((1, nq), f32), jnp.zeros((LANES, nq), f32)) for h in range(2))

        def finish(carry, queries):
            (m0, l0, a0), (m1, l1, a1) = carry
            o_t = jnp.where(top, a0 * (1.0 / l0), a1 * (1.0 / l1))
            o_ref[queries, :] = o_t.T.astype(bf16)
            lse_ref[0:1, queries] = m0 + jnp.log(l0)
            lse_ref[1:2, queries] = m1 + jnp.log(l1)

        if window:
            for c in range(t // LANES):
                queries = slice(c * LANES, (c + 1) * LANES)
                q0 = i * t + c * LANES
                k0 = pl.multiple_of(jnp.maximum(q0 - window, 0), LANES)
                finish(tile(k0, LANES + window, q0 - k0, start(LANES), True, queries), queries)
        else:
            carry = lax.fori_loop(0, i, lambda kb, c: tile(pl.multiple_of(kb * t, t), t, 0, c, False), start(t))
            finish(tile(pl.multiple_of(i * t, t), t, 0, carry, True), slice(0, t))

    q_spec = pl.BlockSpec((t, LANES), lambda j, i: (i, j))
    kv_spec = pl.BlockSpec((s, LANES), lambda j, i: (0, j))
    in_specs, args = [q_spec, kv_spec, kv_spec], [q, k, v]
    if fox:
        in_specs += [pl.BlockSpec((s, 2 * LANES), lambda j, i: (0, j))]
        args += [cum_b]
    if has_sink:
        in_specs += [pl.BlockSpec((None, 2, LANES), lambda j, i: (j, 0, 0))]
        args += [sink_rows.reshape(N_PAIRS, 2, LANES)]
    return pl.pallas_call(
        body, name=name, grid=(N_PAIRS, s // t), in_specs=in_specs,
        out_specs=[q_spec, pl.BlockSpec((None, 2, t), lambda j, i: (j, 0, i))],
        out_shape=[jax.ShapeDtypeStruct((s, N_PAIRS * LANES), bf16), jax.ShapeDtypeStruct((N_PAIRS, 2, s), f32)],
        compiler_params=_params(2),
    )(*args)


def _attn_delta(do, o, name, *, lse=None, sink_rows=None):
    s, hw = do.shape
    tm = _row_tile(s, 512)
    has_sink = sink_rows is not None

    def body(*refs):
        do_ref, o_ref = refs[:2]
        if has_sink:
            lse_ref, sink_ref, dl_ref, ds_ref = refs[2:]

            @pl.when(pl.program_id(0) == 0)
            def _():
                ds_ref[...] = jnp.zeros_like(ds_ref)
        else:
            dl_ref, = refs[2:]
        for j in range(N_PAIRS):
            cols = slice(j * LANES, (j + 1) * LANES)
            prod_t = (do_ref[:, cols].astype(f32) * o_ref[:, cols].astype(f32)).T
            for h in range(2):
                dl = jnp.sum(prod_t[h * HEAD_DIM:(h + 1) * HEAD_DIM, :], axis=0, keepdims=True)
                dl_ref[j, h:h + 1, :] = dl
                if has_sink:
                    r = 2 * j + h
                    p_sink = jnp.exp(sink_ref[r:r + 1, 0:1] - lse_ref[j, h:h + 1, :])
                    ds_ref[r:r + 1, :] += -jnp.sum(p_sink * dl, axis=1, keepdims=True)

    rows_spec = pl.BlockSpec((N_PAIRS, 2, tm), lambda i: (0, 0, i))
    in_specs, args = [_row_spec(tm, hw)] * 2, [do, o]
    out_specs, out_shape = [rows_spec], [jax.ShapeDtypeStruct((N_PAIRS, 2, s), f32)]
    if has_sink:
        in_specs += [rows_spec, _vec_spec(LANES, N_HEADS)]
        args += [lse, sink_rows]
        out_specs += [_vec_spec(LANES, N_HEADS)]
        out_shape += [jax.ShapeDtypeStruct((N_HEADS, LANES), f32)]
    return pl.pallas_call(
        body, name=name, grid=(s // tm,), in_specs=in_specs, out_specs=out_specs, out_shape=out_shape,
        compiler_params=_params(1),
    )(*args)


def _attn_bwd(q, k, v, do, lse, delta, name, *, cum_b=None, window=None, t=256):
    s = q.shape[0]
    t = _row_tile(s, t)
    nblk = s // t
    fox = cum_b is not None
    assert not window or (window % LANES == 0 and LANES + window <= s)

    def body(*refs):
        k_ref, v_ref, q_ref, do_ref, lse_ref, dl_ref = refs[:6]
        rest = list(refs[6:])
        cb_ref = rest.pop(0) if fox else None
        dq_ref, dk_ref, dv_ref = rest[:3]
        dcs_ref, rs_ref = (rest[3], rest[4]) if fox else (None, None)
        b = pl.program_id(1)
        k0 = pl.multiple_of(b * t, t)

        @pl.when(b == 0)
        def _():
            dq_ref[...] = jnp.zeros_like(dq_ref)
            if fox:
                rs_ref[...] = jnp.zeros_like(rs_ref)

        dk_ref[...] = jnp.zeros_like(dk_ref)
        dv_ref[...] = jnp.zeros_like(dv_ref)
        if fox:
            dcs_ref[...] = jnp.zeros_like(dcs_ref)
        low = _lane() < HEAD_DIM
        top = lax.broadcasted_iota(jnp.int32, (LANES, 1), 0) < HEAD_DIM
        kblk, vblk = k_ref[...], v_ref[...]
        k_t = kblk.astype(f32).T.astype(bf16)
        cks = [_wide(cb_ref[pl.ds(k0, t), h * LANES:(h + 1) * LANES], t) for h in range(2)] if fox else None

        def tile(q0, n_queries, off, masked, keys=slice(0, t)):
            cols = pl.ds(q0, n_queries)
            q2, do2 = q_ref[cols, :], do_ref[cols, :]
            zero = jnp.zeros_like(q2)
            valid = _tile_mask(keys.stop - keys.start, n_queries, off, window) if masked else None
            dq_parts = []
            for h in range(2):
                qm = jnp.where(low, q2, zero) if h == 0 else jnp.where(low, zero, q2)
                dom = jnp.where(low, do2, zero) if h == 0 else jnp.where(low, zero, do2)
                sc = lax.dot_general(kblk[keys], qm, _NT, preferred_element_type=f32)
                if fox:
                    sc = sc - cks[h]
                if masked:
                    sc = jnp.where(valid, sc, NEG)
                p = jnp.exp(sc - lse_ref[h:h + 1, cols])
                dp = lax.dot_general(vblk[keys], dom, _NT, preferred_element_type=f32)
                ds = p * (dp - dl_ref[h:h + 1, cols])
                pb, dsb = p.astype(bf16), ds.astype(bf16)
                dv_ref[keys, :] += jnp.dot(pb, dom, preferred_element_type=f32)
                dk_ref[keys, :] += jnp.dot(dsb, qm, preferred_element_type=f32)
                dq_parts.append(jnp.dot(k_t[:, keys], dsb, preferred_element_type=f32))
                if fox:
                    dcs_ref[:, h * LANES:(h + 1) * LANES] += sum(ds[:, g * LANES:(g + 1) * LANES] for g in range(t // LANES))
                    rs_ref[h:h + 1, cols] += jnp.sum(ds, axis=0, keepdims=True)
            dq_ref[:, cols] += jnp.where(top, dq_parts[0], dq_parts[1])

        def later_block(qb, carry):
            tile(pl.multiple_of(qb * t, t), t, 0, False)
            return carry

        if window:
            for c in range(t // LANES):
                first = b * t + c * LANES
                q0 = pl.multiple_of(jnp.minimum(first, s - (LANES + window)), LANES)
                tile(q0, LANES + window, q0 - first, True, slice(c * LANES, (c + 1) * LANES))
        else:
            tile(k0, t, 0, True)
            lax.fori_loop(b + 1, nblk, later_block, 0)

    kv_spec = pl.BlockSpec((t, LANES), lambda j, b: (b, j))
    seq_spec = pl.BlockSpec((s, LANES), lambda j, b: (0, j))
    rows_spec = pl.BlockSpec((None, 2, s), lambda j, b: (j, 0, 0))
    hw = N_PAIRS * LANES
    in_specs, args = [kv_spec, kv_spec, seq_spec, seq_spec, rows_spec, rows_spec], [k, v, q, do, lse, delta]
    out_specs = [pl.BlockSpec((LANES, s), lambda j, b: (j, 0)), kv_spec, kv_spec]
    out_shape = [jax.ShapeDtypeStruct((hw, s), f32), jax.ShapeDtypeStruct((s, hw), f32), jax.ShapeDtypeStruct((s, hw), f32)]
    if fox:
        in_specs += [pl.BlockSpec((s, 2 * LANES), lambda j, b: (0, j))]
        args += [cum_b]
        out_specs += [pl.BlockSpec((t, 2 * LANES), lambda j, b: (b, j)), rows_spec]
        out_shape += [jax.ShapeDtypeStruct((s, N_HEADS * LANES), f32), jax.ShapeDtypeStruct((N_PAIRS, 2, s), f32)]
    return pl.pallas_call(
        body, name=name, grid=(N_PAIRS, nblk), in_specs=in_specs, out_specs=out_specs, out_shape=out_shape,
        compiler_params=_params(2),
    )(*args)


def _merge(ba, bb, gl, name):
    s, d = ba.shape
    tm = _row_tile(s, 512)

    def body(a_ref, b_ref, g_ref, o_ref):
        g0, g1 = jax.nn.sigmoid(g_ref[:, :d].astype(f32)), jax.nn.sigmoid(g_ref[:, d:].astype(f32))
        o_ref[...] = (g0 * a_ref[...].astype(f32) + g1 * b_ref[...].astype(f32)).astype(bf16)

    return pl.pallas_call(
        body, name=name, grid=(s // tm,), in_specs=[_row_spec(tm, d)] * 2 + [_row_spec(tm, 2 * d)],
        out_specs=_row_spec(tm, d), out_shape=jax.ShapeDtypeStruct((s, d), bf16), compiler_params=_params(1),
    )(ba, bb, gl)


def _merge_bwd(dm, ba, bb, gl, name):
    s, d = ba.shape
    tm = _row_tile(s, 512)

    def body(dm_ref, a_ref, b_ref, g_ref, da_ref, db_ref, dg_ref):
        dmv = dm_ref[...].astype(f32)
        g0, g1 = jax.nn.sigmoid(g_ref[:, :d].astype(f32)), jax.nn.sigmoid(g_ref[:, d:].astype(f32))
        da_ref[...] = (dmv * g0).astype(bf16)
        db_ref[...] = (dmv * g1).astype(bf16)
        dg_ref[:, :d] = (dmv * a_ref[...].astype(f32) * (g0 * (1.0 - g0))).astype(bf16)
        dg_ref[:, d:] = (dmv * b_ref[...].astype(f32) * (g1 * (1.0 - g1))).astype(bf16)

    return pl.pallas_call(
        body, name=name, grid=(s // tm,), in_specs=[_row_spec(tm, d)] * 3 + [_row_spec(tm, 2 * d)],
        out_specs=[_row_spec(tm, d)] * 2 + [_row_spec(tm, 2 * d)],
        out_shape=[jax.ShapeDtypeStruct((s, d), bf16)] * 2 + [jax.ShapeDtypeStruct((s, 2 * d), bf16)],
        compiler_params=_params(1),
    )(dm, ba, bb, gl)


GLU_TILE = 256


def _ffn_in_swiglu(h, w, name):
    s, d = h.shape
    f = w.shape[1] // 2
    tm = _row_tile(s, 2048)
    tg = GLU_TILE
    nb = f // tg

    def body(h_ref, wg_ref, wu_ref, g_ref, u_ref, act_ref):
        hv = h_ref[...]
        g = jnp.dot(hv, wg_ref[...], preferred_element_type=f32)
        u = jnp.dot(hv, wu_ref[...], preferred_element_type=f32)
        g_ref[...] = g.astype(bf16)
        u_ref[...] = u.astype(bf16)
        act_ref[...] = (g * jax.nn.sigmoid(g) * u).astype(bf16)

    col = pl.BlockSpec((tm, tg), lambda i, j: (i, j))
    return pl.pallas_call(
        body, name=name, grid=(s // tm, nb),
        in_specs=[pl.BlockSpec((tm, d), lambda i, j: (i, 0)), pl.BlockSpec((d, tg), lambda i, j: (0, j)),
                  pl.BlockSpec((d, tg), lambda i, j: (0, j + nb))],
        out_specs=[col] * 3, out_shape=[jax.ShapeDtypeStruct((s, f), bf16)] * 3, compiler_params=_params(2),
    )(h, w, w)


def _ffn_out_dgrad_swiglu(dy, w_out, g, u, name):
    s, d = dy.shape
    f = g.shape[1]
    tm = _row_tile(s, 2048)
    tg = GLU_TILE

    def body(dy_ref, w_ref, g_ref, u_ref, dg_ref, du_ref):
        dv = lax.dot_general(dy_ref[...], w_ref[...], _NT, preferred_element_type=f32)
        gv, uv = g_ref[...].astype(f32), u_ref[...].astype(f32)
        sg = jax.nn.sigmoid(gv)
        dg_ref[...] = (dv * uv * (sg * (1.0 + gv * (1.0 - sg)))).astype(bf16)
        du_ref[...] = (dv * (gv * sg)).astype(bf16)

    col = pl.BlockSpec((tm, tg), lambda i, j: (i, j))
    return pl.pallas_call(
        body, name=name, grid=(s // tm, f // tg),
        in_specs=[pl.BlockSpec((tm, d), lambda i, j: (i, 0)), pl.BlockSpec((tg, d), lambda i, j: (j, 0)), col, col],
        out_specs=[col] * 2, out_shape=[jax.ShapeDtypeStruct((s, f), bf16)] * 2, compiler_params=_params(2),
    )(dy, w_out, g, u)


def _ffn_in_dgrad(dg, du, w, name, after=None):
    s, f = dg.shape
    d = w.shape[0]
    tm, tn = _matmul_tiles(s, d, 2 * f, dg.dtype.itemsize, w.dtype.itemsize, 4)

    def body(dg_ref, du_ref, wg_ref, wu_ref, *rest):
        rest[-1][...] = (lax.dot_general(dg_ref[...], wg_ref[...], _NT, preferred_element_type=f32)
                         + lax.dot_general(du_ref[...], wu_ref[...], _NT, preferred_element_type=f32))

    extra = [] if after is None else [after]
    rows = pl.BlockSpec((tm, f), lambda i, j: (i, 0))
    return pl.pallas_call(
        body, name=name, grid=(s // tm, d // tn),
        in_specs=[rows, rows, pl.BlockSpec((tn, f), lambda i, j: (j, 0)), pl.BlockSpec((tn, f), lambda i, j: (j, 1))]
        + [pl.BlockSpec(memory_space=pl.ANY)] * len(extra),
        out_specs=pl.BlockSpec((tm, tn), lambda i, j: (i, j)),
        out_shape=jax.ShapeDtypeStruct((s, d), f32), compiler_params=_params(2),
    )(dg, du, w, w, *extra)


def _ada_fwd(c_all, w, b, name):
    def body(c_ref, w_ref, b_ref, o_ref):
        o_ref[...] = jnp.dot(c_ref[...].astype(bf16), w_ref[...].astype(bf16), preferred_element_type=f32) + b_ref[...]

    return pl.pallas_call(
        body, name=name, out_shape=jax.ShapeDtypeStruct((c_all.shape[0], w.shape[1]), f32), compiler_params=_params(),
    )(c_all, w, b)


def _ada_wgrad(c_all, d_all, name):
    n, d = c_all.shape
    w = d_all.shape[1]

    def body(c_ref, d_ref, o_ref):
        eye = (lax.broadcasted_iota(jnp.int32, (n, n), 0) == lax.broadcasted_iota(jnp.int32, (n, n), 1)).astype(f32)
        ct = lax.dot_general(c_ref[...], eye, _TN, precision=lax.Precision.HIGHEST, preferred_element_type=f32)
        g = ct[:, 0:1] * d_ref[0:1, :]
        for bi in range(1, n):
            g = g + ct[:, bi:bi + 1] * d_ref[bi:bi + 1, :]
        o_ref[0] = g

    return pl.pallas_call(
        body, name=name, out_shape=jax.ShapeDtypeStruct((1, d, w), f32), compiler_params=_params(),
    )(c_all, d_all)


def _adamw(parts, w, m, v, name):
    r, c = w.shape
    n_parts = parts.shape[0]
    tr = next(t for t in range(min(r, 256), 0, -1) if r % t == 0 and (t % 16 == 0 or t == r))

    def body(p_ref, w_ref, m_ref, v_ref, g_ref, d_ref, nm_ref, nv_ref):
        g = p_ref[0].astype(f32)
        for i in range(1, n_parts):
            g = g + p_ref[i].astype(f32)
        mm = ADAM_B1 * m_ref[...] + (1.0 - ADAM_B1) * g
        vv = ADAM_B2 * v_ref[...] + (1.0 - ADAM_B2) * (g * g)
        m_hat = mm / (1.0 - ADAM_B1 ** ADAM_STEP)
        v_hat = vv / (1.0 - ADAM_B2 ** ADAM_STEP)
        g_ref[...] = g
        d_ref[...] = -ADAM_LR * (m_hat / (jnp.sqrt(v_hat) + ADAM_EPS) + ADAM_WD * w_ref[...])
        nm_ref[...] = mm
        nv_ref[...] = vv

    spec = pl.BlockSpec((tr, c), lambda i: (i, 0))
    return pl.pallas_call(
        body, name=name, grid=(r // tr,), in_specs=[pl.BlockSpec((n_parts, tr, c), lambda i: (0, i, 0))] + [spec] * 3,
        out_specs=[spec] * 4, out_shape=[jax.ShapeDtypeStruct((r, c), f32)] * 4, compiler_params=_params(1),
    )(parts, w, m, v)


def _me():
    return lax.axis_index("x"), lax.axis_index("y"), lax.axis_index("c")


def _all_gather(arrays, name, vmem=False):
    n = len(arrays)
    space = pltpu.VMEM if vmem else pl.ANY

    def body(*refs):
        ins, outs = refs[:n], refs[n:2 * n]
        send_sems, recv_sems, local_sems = refs[2 * n:]
        x, y, c = _me()
        me, sibling = (x, y, c), (x, y, 1 - c)
        chips = [(1 - x, y), (x, 1 - y), (1 - x, 1 - y)]

        def rows(a, dev):
            return outs[a].at[4 * dev[0] + 2 * dev[1] + dev[2]]

        def copy(a, k, block, to, src=None):
            return pltpu.make_async_remote_copy(
                src_ref=rows(a, block) if src is None else src, dst_ref=rows(a, block),
                send_sem=send_sems.at[a, k], recv_sem=recv_sems.at[a, k], device_id=to, device_id_type=MESH)

        mine = [pltpu.make_async_copy(ins[a], rows(a, me), local_sems.at[a]) for a in range(n)]
        for cp in mine:
            cp.start()
        first = []
        for a in range(n):
            first.append(copy(a, 0, me, sibling, src=ins[a]))
            first += [copy(a, 1 + j, me, (*chip, c), src=ins[a]) for j, chip in enumerate(chips)]
        for cp in first:
            cp.start()
        passed = []
        for j, chip in enumerate(chips):
            for a in range(n):
                copy(a, 1 + j, (*chip, c), me).wait_recv()
                fwd = copy(a, 4 + j, (*chip, c), sibling)
                fwd.start()
                passed.append(fwd)
        for a in range(n):
            copy(a, 0, sibling, me).wait_recv()
            for j, chip in enumerate(chips):
                copy(a, 4 + j, (*chip, 1 - c), me).wait_recv()
        for cp in first + passed:
            cp.wait_send()
        for cp in mine:
            cp.wait()

    outs = pl.pallas_call(
        body, name=name,
        in_specs=[pl.BlockSpec(memory_space=space)] * n, out_specs=[pl.BlockSpec(memory_space=space)] * n,
        out_shape=[jax.ShapeDtypeStruct((N_DEV,) + a.shape, a.dtype) for a in arrays],
        scratch_shapes=[pltpu.SemaphoreType.DMA((n, 7)), pltpu.SemaphoreType.DMA((n, 7)), pltpu.SemaphoreType.DMA((n,))],
        compiler_params=pltpu.CompilerParams(vmem_limit_bytes=VMEM_LIMIT),
    )(*arrays)
    return list(outs)


_FLIPS = ((0, 0, 1), (1, 0, 0), (0, 1, 0), (1, 1, 0), (1, 0, 1), (0, 1, 1), (1, 1, 1))
_HBM = pl.BlockSpec(memory_space=pltpu.HBM)
_SEM = pl.BlockSpec(memory_space=pltpu.SEMAPHORE)


def _exchange_copies(scatter, srcs, lands, send_sems, recv_sems):
    x, y, c = _me()
    me_row = 4 * x + 2 * y + c
    out = []
    for k, (fx, fy, fc) in enumerate(_FLIPS):
        peer = (x ^ fx, y ^ fy, c ^ fc)
        peer_row = 4 * peer[0] + 2 * peer[1] + peer[2]
        for a in range(len(srcs)):
            out.append(pltpu.make_async_remote_copy(
                src_ref=srcs[a].at[peer_row] if scatter else srcs[a], dst_ref=lands[a].at[me_row],
                send_sem=send_sems.at[7 * a + k], recv_sem=recv_sems.at[7 * a + k], device_id=peer, device_id_type=MESH))
    return out


def _exchange_start(arrays, scatter, name):
    n = len(arrays)
    lands = [lax.empty(a.shape if scatter else (N_DEV,) + a.shape, a.dtype) for a in arrays]

    def body(*refs):
        srcs, zones = refs[:n], refs[n:2 * n]
        send_sems, recv_sems = refs[2 * n], refs[2 * n + 1]
        token = refs[-1]
        for cp in _exchange_copies(scatter, srcs, zones, send_sems, recv_sems):
            cp.start()
        token[...] = jnp.zeros_like(token)

    thru = [pltpu.HBM(a.shape, a.dtype) for a in list(arrays) + lands]
    outs = pl.pallas_call(
        body, name=name,
        out_shape=(pltpu.SemaphoreType.DMA((7 * n,)), pltpu.SemaphoreType.DMA((7 * n,)), *thru, jax.ShapeDtypeStruct((8, LANES), f32)),
        in_specs=[_HBM] * (2 * n), out_specs=(_SEM, _SEM, *[_HBM] * (2 * n), pl.BlockSpec(memory_space=pltpu.VMEM)),
        input_output_aliases={i: 2 + i for i in range(2 * n)},
        compiler_params=pltpu.CompilerParams(has_side_effects=pltpu.SideEffectType.DATAFLOW_SIDE_EFFECTING),
    )(*[pltpu.with_memory_space_constraint(a, pltpu.HBM) for a in list(arrays) + lands])
    return dict(n=n, scatter=scatter, sems=outs[:2], srcs=outs[2:2 + n], lands=outs[2 + n:2 + 2 * n], token=outs[-1])


def _exchange_wait(handle, after, name):
    n, scatter = handle["n"], handle["scatter"]

    def body(*refs):
        srcs, zones = refs[:n], refs[n:2 * n]
        send_sems, recv_sems = refs[2 * n], refs[2 * n + 1]
        for cp in _exchange_copies(scatter, srcs, zones, send_sems, recv_sems):
            cp.wait_send()
            cp.wait_recv()

    thru = [pltpu.HBM(a.shape, a.dtype) for a in list(handle["srcs"]) + list(handle["lands"])]
    outs = pl.pallas_call(
        body, name=name, out_shape=tuple(thru),
        in_specs=[_HBM] * (2 * n) + [_SEM, _SEM, pl.BlockSpec(memory_space=pl.ANY)], out_specs=tuple([_HBM] * (2 * n)),
        input_output_aliases={i: i for i in range(2 * n)},
        compiler_params=pltpu.CompilerParams(has_side_effects=pltpu.SideEffectType.DATAFLOW_SIDE_EFFECTING),
    )(*handle["srcs"], *handle["lands"], *handle["sems"], after)
    return list(outs[n:])


def _cols_from_shards(g):
    return jnp.transpose(g, (1, 0, 2)).reshape(g.shape[1], -1)


def _shards_from_cols(a):
    return jnp.transpose(a.reshape(a.shape[0], N_DEV, -1), (1, 0, 2))


def _local_step(x, positions, ada, g_pre_mix, g_post_mix, b_f, sinks, g_pre_ffn, g_post_ffn, target,
                w_in, late_weights, on_grads):
    s, d = x.shape
    row = lambda v: v.reshape(1, -1)
    shift_m, scale_m, gate_m, shift_f, scale_f, gate_f = (ada[i:i + 1] for i in range(6))
    w_gate, w_qkv = w_in[:, F_OFF + N_HEADS:], w_in[:, :QKV_W]
    w_f = jnp.pad(w_in[:, F_OFF:F_OFF + N_HEADS], ((0, 0), (0, LANES - N_HEADS)))
    w_in_p = jnp.concatenate([w_gate, w_qkv, w_f], axis=1)
    bf_row = jnp.pad(row(b_f), ((0, 0), (0, LANES - N_HEADS)))
    sink_rows = jnp.broadcast_to(sinks.reshape(N_HEADS, 1).astype(f32), (N_HEADS, LANES))
    inv_freq = 1.0 / (ROPE_THETA ** (jnp.arange(0, HEAD_DIM, 2, dtype=f32) / HEAD_DIM))
    cos, sin_s = _rope_tables(positions.reshape(s, 1), jnp.tile(inv_freq, 4).reshape(1, LANES), "rope_tables")

    h1 = _prenorm(x, row(g_pre_mix), scale_m, shift_m, "prenorm_mix")
    gl = _matmul(h1, w_gate, "nn", bf16, "proj_gate")
    qkv = _matmul(h1, w_qkv, "nn", f32, "proj_qkv")
    fl = _matmul(h1, w_f, "nn", f32, "proj_forget")
    qa, ka, va, qb, kb, vb = _qkv_prep(qkv, cos, sin_s, "qkv_prep")
    cum_b = _forget_prep(fl, bf_row, "forget_prep")
    o_a, lse_a = _attn_fwd(qa, ka, va, "swa_fwd", sink_rows=sink_rows, window=WINDOW, t=512)
    o_b, lse_b = _attn_fwd(qb, kb, vb, "fox_fwd", cum_b=cum_b, t=512)
    w_branch_a, w_branch_b, w_out, w_ffn_in, w_ffn_out = late_weights(o_b)
    ba = _matmul(o_a, w_branch_a, "nn", bf16, "branch_a")
    bb = _matmul(o_b, w_branch_b, "nn", bf16, "branch_b")
    merged = _merge(ba, bb, gl, "merge")
    y1 = _matmul(merged, w_out, "nn", f32, "out_proj")
    x2 = _postnorm_res(x, y1, row(g_post_mix), gate_m, "postnorm_mix")

    h2 = _prenorm(x2, row(g_pre_ffn), scale_f, shift_f, "prenorm_ffn")
    g_ff, u_ff, act = _ffn_in_swiglu(h2, w_ffn_in, "ffn_in_swiglu")
    y2 = _matmul(act, w_ffn_out, "nn", f32, "ffn_out")
    loss_row, d_out, d_y2, vec_pf = _loss_tail(x2, y2, row(g_post_ffn), gate_f, target, "loss_tail")

    g_w_ffn_out = _matmul(act, d_y2, "tn", bf16, "ffn_out_wgrad")
    dg_ff, du_ff = _ffn_out_dgrad_swiglu(d_y2, w_ffn_out, g_ff, u_ff, "ffn_out_dgrad_swiglu")
    g_w_ffn_in = jnp.concatenate([_matmul(h2, dg_ff, "tn", bf16, "ffn_gate_wgrad"),
                                  _matmul(h2, du_ff, "tn", bf16, "ffn_up_wgrad")], axis=1)
    sent = on_grads(dict(w_ffn_in=g_w_ffn_in, w_ffn_out=g_w_ffn_out))
    d_h2 = _ffn_in_dgrad(dg_ff, du_ff, w_ffn_in, "ffn_in_dgrad", after=sent)
    d_x2, vec_nf = _prenorm_bwd(d_h2, x2, row(g_pre_ffn), scale_f, d_out, "prenorm_ffn_bwd")

    d_y1, vec_pm = _postnorm_bwd(d_x2, y1, row(g_post_mix), gate_m, "postnorm_mix_bwd")
    g_w_out = _matmul(merged, d_y1, "tn", bf16, "out_proj_wgrad")
    d_merged = _matmul(d_y1, w_out, "nt", bf16, "out_proj_dgrad")
    d_ba, d_bb, dgl = _merge_bwd(d_merged, ba, bb, gl, "merge_bwd")
    g_w_branch_a = _matmul(o_a, d_ba, "tn", bf16, "branch_a_wgrad")
    g_w_branch_b = _matmul(o_b, d_bb, "tn", bf16, "branch_b_wgrad")
    sent = on_grads(dict(w_out=g_w_out, w_branch_a=g_w_branch_a, w_branch_b=g_w_branch_b))
    d_oa = _matmul(d_ba, w_branch_a, "nt", bf16, "branch_a_dgrad", after=sent)
    d_ob = _matmul(d_bb, w_branch_b, "nt", bf16, "branch_b_dgrad", after=sent)
    delta_a, d_sink = _attn_delta(d_oa, o_a, "swa_delta", lse=lse_a, sink_rows=sink_rows)
    delta_b, = _attn_delta(d_ob, o_b, "fox_delta")
    dqa_t, dka, dva = _attn_bwd(qa, ka, va, d_oa, lse_a, delta_a, "swa_bwd", window=WINDOW, t=512)
    dqb_t, dkb, dvb, dcs, rs = _attn_bwd(qb, kb, vb, d_ob, lse_b, delta_b, "fox_bwd", cum_b=cum_b, t=512)
    dqkv = _qkv_prep_bwd(dqa_t, dka, dva, dqb_t, dkb, dvb, cos, sin_s, "qkv_prep_bwd")
    dfl, vec_bf = _forget_prep_bwd(rs.reshape(N_HEADS, s), dcs, fl, bf_row, "forget_prep_bwd")
    dproj = jnp.concatenate([dgl, dqkv, dfl], axis=1)
    g_w_in_p = _matmul(h1, dproj, "tn", bf16, "in_proj_wgrad")
    g_w_in = jnp.concatenate([g_w_in_p[:, GATE_W:GATE_W + QKV_W], g_w_in_p[:, GATE_W + QKV_W:GATE_W + QKV_W + N_HEADS],
                              g_w_in_p[:, :GATE_W]], axis=1)
    sent = on_grads(dict(w_in=g_w_in))
    d_h1 = _matmul(dproj, w_in_p, "nt", f32, "in_proj_dgrad", after=sent)
    grad_x, vec_nm = _prenorm_bwd(d_h1, x, row(g_pre_mix), scale_m, d_x2, "prenorm_mix_bwd")

    d_ada = jnp.concatenate([vec_nm[0], vec_nm[1], vec_pm[0], vec_nf[0], vec_nf[1], vec_pf[0]])
    small = dict(b_ada=d_ada, g_pre_mix=vec_nm[2], g_post_mix=vec_pm[1], g_pre_ffn=vec_nf[2], g_post_ffn=vec_pf[1],
                 b_f=vec_bf[0, :N_HEADS], sinks=d_sink[:, 0], loss=loss_row[0, :1])
    return grad_x, small


_SMALL = (("b_ada", 6144), ("g_pre_mix", 1024), ("g_post_mix", 1024), ("g_pre_ffn", 1024), ("g_post_ffn", 1024),
          ("b_f", 128), ("sinks", 128), ("loss", 128))
_SMALL_ROWS = 88


def _pack_small(vals):
    parts = [jnp.pad(vals[k].reshape(-1).astype(f32), (0, n - vals[k].size)) for k, n in _SMALL]
    flat = jnp.concatenate(parts)
    return jnp.pad(flat, (0, _SMALL_ROWS * LANES - flat.size)).reshape(_SMALL_ROWS, LANES)


def _unpack_small(slab, shapes):
    flat, out, off = slab.reshape(-1), {}, 0
    for k, n in _SMALL:
        size = math.prod(shapes[k])
        out[k] = flat[off:off + size].reshape(shapes[k])
        off += n
    return out


def kernel(x, c, positions, w_ada, b_ada, g_pre_mix, g_post_mix, w_in, b_f, sinks, w_branch_a, w_branch_b, w_out, g_pre_ffn, g_post_ffn, w_ffn_in, w_ffn_out, loss_target, m_w_ada, m_b_ada, m_g_pre_mix, m_g_post_mix, m_w_in, m_b_f, m_sinks, m_w_branch_a, m_w_branch_b, m_w_out, m_g_pre_ffn, m_g_post_ffn, m_w_ffn_in, m_w_ffn_out, v_w_ada, v_b_ada, v_g_pre_mix, v_g_post_mix, v_w_in, v_b_f, v_sinks, v_w_branch_a, v_w_branch_b, v_w_out, v_g_pre_ffn, v_g_post_ffn, v_w_ffn_in, v_w_ffn_out):
    xi, yi, ci = _me()
    me = 4 * xi + 2 * yi + ci
    d = D_MODEL
    ada_w = w_ada.shape[2]

    c_all, = _all_gather([c], "gather_c", vmem=True)
    c_all = c_all.reshape(N_DEV, d)
    b_mine = lax.dynamic_slice(b_ada, (0, me * ada_w), (1, ada_w))
    ada_cols = _ada_fwd(c_all, w_ada[0], b_mine, "ada_fwd")
    ada_all, = _all_gather([ada_cols], "gather_ada", vmem=True)
    ada = lax.dynamic_index_in_dim(ada_all, me, axis=1, keepdims=False).reshape(6, d)

    g_in, = _all_gather([w_in[0].astype(bf16)], "gather_w_in")
    late = [w.astype(bf16) for w in (w_branch_a[0], w_branch_b[0], w_out[0], w_ffn_in[0], w_ffn_out[0])]
    late_h = _exchange_start(late, False, "gather_late_start")

    def mine_into(zone, block):
        return lax.dynamic_update_index_in_dim(zone, block, me, 0)

    def late_weights(after):
        zones = _exchange_wait(late_h, after, "gather_late_wait")
        g_ba, g_bb, g_out, g_fi, g_fo = (mine_into(z, w) for z, w in zip(zones, late))
        return (_cols_from_shards(g_ba), _cols_from_shards(g_bb), g_out.reshape(d, d), _cols_from_shards(g_fi),
                g_fo.reshape(D_FF, d))

    row_sharded = ("w_out", "w_ffn_out")
    in_flight = []

    def on_grads(group):
        sends = [g.reshape(N_DEV, g.shape[0] // N_DEV, g.shape[1]) if nm in row_sharded else _shards_from_cols(g)
                 for nm, g in group.items()]
        handle = _exchange_start(sends, True, "scatter_start_%d" % len(in_flight))
        in_flight.append((list(group), sends, handle))
        return handle["token"]

    grad_x, small = _local_step(
        x[0], positions[0], ada + late_h["token"][0, 0], g_pre_mix[0], g_post_mix[0], b_f[0], sinks[0], g_pre_ffn[0],
        g_post_ffn[0], loss_target[0], _cols_from_shards(g_in), late_weights, on_grads)

    slab_all, = _all_gather([_pack_small(small)], "gather_small", vmem=True)
    small_w = dict(b_ada=b_ada, g_pre_mix=g_pre_mix, g_post_mix=g_post_mix, g_pre_ffn=g_pre_ffn, g_post_ffn=g_post_ffn,
                   b_f=b_f, sinks=sinks, loss=jnp.zeros((1,), f32))
    small_m = dict(b_ada=m_b_ada, g_pre_mix=m_g_pre_mix, g_post_mix=m_g_post_mix, g_pre_ffn=m_g_pre_ffn,
                   g_post_ffn=m_g_post_ffn, b_f=m_b_f, sinks=m_sinks, loss=jnp.zeros((1,), f32))
    small_v = dict(b_ada=v_b_ada, g_pre_mix=v_g_pre_mix, g_post_mix=v_g_post_mix, g_pre_ffn=v_g_pre_ffn,
                   g_post_ffn=v_g_post_ffn, b_f=v_b_f, sinks=v_sinks, loss=jnp.ones((1,), f32))
    shapes = {k: small_w[k].shape for k, _ in _SMALL}
    s_out = _adamw(slab_all, _pack_small(small_w), _pack_small(small_m), _pack_small(small_v), "adamw_small")
    s_grad, s_delta, s_m, s_v = (_unpack_small(o, shapes) for o in s_out)

    d_ada_all = lax.dynamic_slice(slab_all[:, :6144 // LANES, :].reshape(N_DEV, 6144), (0, me * ada_w), (N_DEV, ada_w))
    ada_parts = _ada_wgrad(c_all, d_ada_all, "ada_wgrad")

    ws = dict(w_in=(w_in, m_w_in, v_w_in), w_branch_a=(w_branch_a, m_w_branch_a, v_w_branch_a),
              w_branch_b=(w_branch_b, m_w_branch_b, v_w_branch_b), w_out=(w_out, m_w_out, v_w_out),
              w_ffn_in=(w_ffn_in, m_w_ffn_in, v_w_ffn_in), w_ffn_out=(w_ffn_out, m_w_ffn_out, v_w_ffn_out))
    res = {"w_ada": _adamw(ada_parts, w_ada[0], m_w_ada[0], v_w_ada[0], "adamw_w_ada")}
    after = res["w_ada"][0]
    for gi, (names, sends, handle) in enumerate(in_flight):
        zones = _exchange_wait(handle, after, "scatter_wait_%d" % gi)
        for nm, zone, sent in zip(names, zones, sends):
            w, m, v = ws[nm]
            parts = mine_into(zone, lax.dynamic_index_in_dim(sent, me, 0, keepdims=False))
            res[nm] = _adamw(parts, w[0], m[0], v[0], "adamw_" + nm)
            after = res[nm][0]

    order = ["w_ada", "b_ada", "g_pre_mix", "g_post_mix", "w_in", "b_f", "sinks", "w_branch_a", "w_branch_b", "w_out",
             "g_pre_ffn", "g_post_ffn", "w_ffn_in", "w_ffn_out"]
    outs = [s_grad["loss"].reshape(()), grad_x[None]]
    for which, small_o in enumerate((s_grad, s_delta, s_m, s_v)):
        for nm in order:
            outs.append(res[nm][which][None] if nm in res else small_o[nm])
    return tuple(outs)
```

```python
import functools
import math

import jax
import jax.numpy as jnp
from jax import lax
from jax.experimental import pallas as pl
from jax.experimental.pallas import tpu as pltpu

f32 = jnp.float32
bf16 = jnp.bfloat16

D_MODEL = 1024
HEAD_DIM = 64
N_HEADS = 8
N_PAIRS = 4
QKV_W = 2304
GATE_W = 2048
F_OFF = 2304
IN_W = 4360
WINDOW = 128
ROPE_THETA = 10000.0
RMS_EPS = 1e-6
D_FF = 2816
N_DEV = 8
ADAM_LR, ADAM_B1, ADAM_B2, ADAM_EPS, ADAM_WD, ADAM_STEP = 0.001, 0.9, 0.999, 1e-08, 0.01, 10
NEG = -1e30
LANES = 128
VMEM_LIMIT = 48 * 1024 * 1024
MESH = pl.DeviceIdType.MESH

_NT = (((1,), (1,)), ((), ()))
_TN = (((0,), (0,)), ((), ()))


def _params(n_grid=0):
    sem = ("arbitrary",) * n_grid if n_grid else None
    return pltpu.CompilerParams(dimension_semantics=sem, vmem_limit_bytes=VMEM_LIMIT)


def _row_tile(s, want):
    t = min(s, want)
    assert s % t == 0, (s, t)
    return t


MATMUL_VMEM_BUDGET = 40 * 1024 * 1024


def _matmul_tiles(m, n, k, a_item, b_item, o_item):
    def tiles(d):
        return [t for t in range(LANES, min(d, 2048) + 1, LANES) if d % t == 0] or [d]

    best = None
    for tm in tiles(m):
        for tn in tiles(n):
            vmem = 2 * (tm * k * a_item + tn * k * b_item + tm * tn * o_item) + tm * tn * 4
            if vmem > MATMUL_VMEM_BUDGET:
                continue
            traffic = m * k * a_item + n * k * b_item * (1 if tn == n else m // tm) + m * n * o_item
            steps = (m // tm) * (n // tn)
            key = (traffic, 0, steps) if steps >= 4 else (traffic, 1, -steps)
            if best is None or key < best[0]:
                best = (key, tm, tn)
    assert best is not None, (m, n, k)
    return best[1], best[2]


def _matmul(a, b, mode, out_dtype, name, after=None):
    if mode == "nn":
        (m, k), n = a.shape, b.shape[1]
    elif mode == "nt":
        (m, k), n = a.shape, b.shape[0]
    else:
        (k, m), n = a.shape, b.shape[1]
    tm, tn = _matmul_tiles(m, n, k, a.dtype.itemsize, b.dtype.itemsize, jnp.dtype(out_dtype).itemsize)
    if mode == "nn":
        a_spec, b_spec, dims = pl.BlockSpec((tm, k), lambda i, j: (i, 0)), pl.BlockSpec((k, tn), lambda i, j: (0, j)), None
    elif mode == "nt":
        a_spec, b_spec, dims = pl.BlockSpec((tm, k), lambda i, j: (i, 0)), pl.BlockSpec((tn, k), lambda i, j: (j, 0)), _NT
    else:
        a_spec, b_spec, dims = pl.BlockSpec((k, tm), lambda i, j: (0, i)), pl.BlockSpec((k, tn), lambda i, j: (0, j)), _TN

    def body(a_ref, b_ref, *rest):
        o_ref = rest[-1]
        av, bv = a_ref[...].astype(bf16), b_ref[...].astype(bf16)
        if dims is None:
            r = jnp.dot(av, bv, preferred_element_type=f32)
        else:
            r = lax.dot_general(av, bv, dims, preferred_element_type=f32)
        o_ref[...] = r.astype(out_dtype)

    extra = [] if after is None else [after]
    return pl.pallas_call(
        body, name=name, grid=(m // tm, n // tn), in_specs=[a_spec, b_spec] + [pl.BlockSpec(memory_space=pl.ANY)] * len(extra),
        out_specs=pl.BlockSpec((tm, tn), lambda i, j: (i, j)),
        out_shape=jax.ShapeDtypeStruct((m, n), out_dtype), compiler_params=_params(2),
    )(a, b, *extra)


def _rstd(v):
    return lax.rsqrt(jnp.mean(v * v, axis=-1, keepdims=True) + RMS_EPS)


def _row_spec(tm, d):
    return pl.BlockSpec((tm, d), lambda i: (i, 0))


def _vec_spec(d, rows=1):
    return pl.BlockSpec((rows, d), lambda i: (0, 0))


def _prenorm(x, g, scale, shift, name):
    s, d = x.shape
    tm = _row_tile(s, 512)

    def body(x_ref, g_ref, sc_ref, sh_ref, h_ref):
        xv = x_ref[...]
        h = (xv * _rstd(xv) * g_ref[...]) * (1.0 + sc_ref[...]) + sh_ref[...]
        h_ref[...] = h.astype(bf16)

    return pl.pallas_call(
        body, name=name, grid=(s // tm,), in_specs=[_row_spec(tm, d)] + [_vec_spec(d)] * 3,
        out_specs=_row_spec(tm, d), out_shape=jax.ShapeDtypeStruct((s, d), bf16), compiler_params=_params(1),
    )(x, g, scale, shift)


def _postnorm_res(x, y, g, gate, name):
    s, d = x.shape
    tm = _row_tile(s, 512)

    def body(x_ref, y_ref, g_ref, gate_ref, o_ref):
        yv = y_ref[...]
        o_ref[...] = x_ref[...] + gate_ref[...] * (yv * _rstd(yv) * g_ref[...])

    return pl.pallas_call(
        body, name=name, grid=(s // tm,), in_specs=[_row_spec(tm, d)] * 2 + [_vec_spec(d)] * 2,
        out_specs=_row_spec(tm, d), out_shape=jax.ShapeDtypeStruct((s, d), f32), compiler_params=_params(1),
    )(x, y, g, gate)


def _rms_bwd(u, v, r):
    return r * u - v * (r * r * r) * jnp.mean(u * v, axis=-1, keepdims=True)


def _loss_tail(x, y, g, gate, target, name):
    s, d = x.shape
    tm = _row_tile(s, 512)

    def body(x_ref, y_ref, g_ref, gate_ref, t_ref, loss_ref, do_ref, dy_ref, vec_ref):
        @pl.when(pl.program_id(0) == 0)
        def _():
            loss_ref[...] = jnp.zeros_like(loss_ref)
            vec_ref[...] = jnp.zeros_like(vec_ref)
        yv = y_ref[...]
        r = _rstd(yv)
        yn = yv * r
        err = x_ref[...] + gate_ref[...] * (yn * g_ref[...]) - t_ref[...]
        loss_ref[...] += 0.5 * jnp.sum(jnp.mean(err * err, axis=-1, keepdims=True), axis=0, keepdims=True)
        dr = err / d
        do_ref[...] = dr
        dn = dr * gate_ref[...]
        vec_ref[0:1, :] += jnp.sum(dr * (yn * g_ref[...]), axis=0, keepdims=True)
        vec_ref[1:2, :] += jnp.sum(dn * yn, axis=0, keepdims=True)
        dy_ref[...] = _rms_bwd(dn * g_ref[...], yv, r).astype(bf16)

    return pl.pallas_call(
        body, name=name, grid=(s // tm,), in_specs=[_row_spec(tm, d)] * 2 + [_vec_spec(d)] * 2 + [_row_spec(tm, d)],
        out_specs=[_vec_spec(LANES), _row_spec(tm, d), _row_spec(tm, d), _vec_spec(d, 8)],
        out_shape=[jax.ShapeDtypeStruct((1, LANES), f32), jax.ShapeDtypeStruct((s, d), f32),
                   jax.ShapeDtypeStruct((s, d), bf16), jax.ShapeDtypeStruct((8, d), f32)],
        compiler_params=_params(1),
    )(x, y, g, gate, target)


def _postnorm_bwd(dres, y, g, gate, name):
    s, d = y.shape
    tm = _row_tile(s, 512)

    def body(dr_ref, y_ref, g_ref, gate_ref, dy_ref, vec_ref):
        @pl.when(pl.program_id(0) == 0)
        def _():
            vec_ref[...] = jnp.zeros_like(vec_ref)
        dr, yv = dr_ref[...], y_ref[...]
        r = _rstd(yv)
        yn = yv * r
        dn = dr * gate_ref[...]
        vec_ref[0:1, :] += jnp.sum(dr * (yn * g_ref[...]), axis=0, keepdims=True)
        vec_ref[1:2, :] += jnp.sum(dn * yn, axis=0, keepdims=True)
        dy_ref[...] = _rms_bwd(dn * g_ref[...], yv, r).astype(bf16)

    return pl.pallas_call(
        body, name=name, grid=(s // tm,), in_specs=[_row_spec(tm, d)] * 2 + [_vec_spec(d)] * 2,
        out_specs=[_row_spec(tm, d), _vec_spec(d, 8)],
        out_shape=[jax.ShapeDtypeStruct((s, d), bf16), jax.ShapeDtypeStruct((8, d), f32)], compiler_params=_params(1),
    )(dres, y, g, gate)


def _prenorm_bwd(dh, x, g, scale, dres, name):
    s, d = x.shape
    tm = _row_tile(s, 512)

    def body(dh_ref, x_ref, g_ref, sc_ref, dr_ref, dx_ref, vec_ref):
        @pl.when(pl.program_id(0) == 0)
        def _():
            vec_ref[...] = jnp.zeros_like(vec_ref)
        dhv, xv = dh_ref[...], x_ref[...]
        r = _rstd(xv)
        xn = xv * r
        dn = dhv * (1.0 + sc_ref[...])
        vec_ref[0:1, :] += jnp.sum(dhv, axis=0, keepdims=True)
        vec_ref[1:2, :] += jnp.sum(dhv * (xn * g_ref[...]), axis=0, keepdims=True)
        vec_ref[2:3, :] += jnp.sum(dn * xn, axis=0, keepdims=True)
        dx_ref[...] = dr_ref[...] + _rms_bwd(dn * g_ref[...], xv, r)

    return pl.pallas_call(
        body, name=name, grid=(s // tm,),
        in_specs=[_row_spec(tm, d)] * 2 + [_vec_spec(d)] * 2 + [_row_spec(tm, d)],
        out_specs=[_row_spec(tm, d), _vec_spec(d, 8)],
        out_shape=[jax.ShapeDtypeStruct((s, d), f32), jax.ShapeDtypeStruct((8, d), f32)], compiler_params=_params(1),
    )(dh, x, g, scale, dres)


def _lane():
    return lax.broadcasted_iota(jnp.int32, (1, LANES), 1)


def _rope_tables(pos_col, inv_freq, name):
    s = pos_col.shape[0]

    def body(p_ref, f_ref, cos_ref, sin_ref):
        ang = p_ref[...].astype(f32) * f_ref[...]
        first_half = (_lane() % HEAD_DIM) < HEAD_DIM // 2
        cos_ref[...] = jnp.cos(ang)
        sn = jnp.sin(ang)
        sin_ref[...] = jnp.where(first_half, -sn, sn)

    return pl.pallas_call(
        body, name=name, out_shape=[jax.ShapeDtypeStruct((s, LANES), f32)] * 2, compiler_params=_params(),
    )(pos_col, inv_freq)


def _swap_halves(v):
    first_half = (_lane() % HEAD_DIM) < HEAD_DIM // 2
    return jnp.where(first_half, pltpu.roll(v, LANES - HEAD_DIM // 2, axis=1), pltpu.roll(v, HEAD_DIM // 2, axis=1))


def _qkv_prep(qkv, cos, sin_s, name):
    s = qkv.shape[0]
    tm = _row_tile(s, 256)
    scale = 1.0 / math.sqrt(HEAD_DIM)

    def body(p_ref, c_ref, s_ref, qa_ref, ka_ref, va_ref, qb_ref, kb_ref, vb_ref):
        cs, sn = c_ref[...], s_ref[...]
        low = _lane() < HEAD_DIM

        def blk(j):
            return p_ref[:, j * LANES:(j + 1) * LANES]

        def rope(v):
            return v * cs + _swap_halves(v) * sn

        def expand(v):
            other = pltpu.roll(v, HEAD_DIM, axis=1)
            return jnp.where(low, v, other), jnp.where(low, other, v)

        for j in range(N_PAIRS):
            qa_ref[:, j * LANES:(j + 1) * LANES] = (rope(blk(j)) * scale).astype(bf16)
            qb_ref[:, j * LANES:(j + 1) * LANES] = (blk(6 + j) * scale).astype(bf16)
            kb_ref[:, j * LANES:(j + 1) * LANES] = blk(10 + j).astype(bf16)
            vb_ref[:, j * LANES:(j + 1) * LANES] = blk(14 + j).astype(bf16)
        k0, k1 = expand(rope(blk(4)))
        v0, v1 = expand(blk(5))
        for j in range(N_PAIRS):
            ka_ref[:, j * LANES:(j + 1) * LANES] = (k0 if j < 2 else k1).astype(bf16)
            va_ref[:, j * LANES:(j + 1) * LANES] = (v0 if j < 2 else v1).astype(bf16)

    hw = N_PAIRS * LANES
    return pl.pallas_call(
        body, name=name, grid=(s // tm,),
        in_specs=[_row_spec(tm, QKV_W), _row_spec(tm, LANES), _row_spec(tm, LANES)],
        out_specs=[_row_spec(tm, hw)] * 6, out_shape=[jax.ShapeDtypeStruct((s, hw), bf16)] * 6, compiler_params=_params(1),
    )(qkv, cos, sin_s)


def _qkv_prep_bwd(dqa_t, dka, dva, dqb_t, dkb, dvb, cos, sin_s, name):
    s = dka.shape[0]
    tm = _row_tile(s, 256)
    scale = 1.0 / math.sqrt(HEAD_DIM)
    hw = N_PAIRS * LANES
    t_spec = pl.BlockSpec((hw, tm), lambda i: (0, i))

    def body(dqa_ref, dka_ref, dva_ref, dqb_ref, dkb_ref, dvb_ref, c_ref, s_ref, o_ref):
        cs, sn = c_ref[...], s_ref[...]
        low = _lane() < HEAD_DIM

        def blk(ref, j):
            return ref[:, j * LANES:(j + 1) * LANES]

        def blk_t(ref, j):
            return ref[j * LANES:(j + 1) * LANES, :].T

        def unrope(v):
            return v * cs + _swap_halves(v * sn)

        def fold(ref):
            a, b = blk(ref, 0) + blk(ref, 1), blk(ref, 2) + blk(ref, 3)
            kv0 = a + pltpu.roll(a, HEAD_DIM, axis=1)
            kv1 = b + pltpu.roll(b, HEAD_DIM, axis=1)
            return jnp.where(low, kv0, kv1)

        for j in range(N_PAIRS):
            o_ref[:, j * LANES:(j + 1) * LANES] = (unrope(blk_t(dqa_ref, j)) * scale).astype(bf16)
            o_ref[:, (6 + j) * LANES:(7 + j) * LANES] = (blk_t(dqb_ref, j) * scale).astype(bf16)
            o_ref[:, (10 + j) * LANES:(11 + j) * LANES] = blk(dkb_ref, j).astype(bf16)
            o_ref[:, (14 + j) * LANES:(15 + j) * LANES] = blk(dvb_ref, j).astype(bf16)
        o_ref[:, 4 * LANES:5 * LANES] = unrope(fold(dka_ref)).astype(bf16)
        o_ref[:, 5 * LANES:6 * LANES] = fold(dva_ref).astype(bf16)

    return pl.pallas_call(
        body, name=name, grid=(s // tm,),
        in_specs=[t_spec, _row_spec(tm, hw), _row_spec(tm, hw), t_spec, _row_spec(tm, hw), _row_spec(tm, hw)] + [_row_spec(tm, LANES)] * 2,
        out_specs=_row_spec(tm, QKV_W), out_shape=jax.ShapeDtypeStruct((s, QKV_W), bf16), compiler_params=_params(1),
    )(dqa_t, dka, dva, dqb_t, dkb, dvb, cos, sin_s)


def _cumsum_rows(v, reverse=False):
    n = v.shape[0]
    row = lax.broadcasted_iota(jnp.int32, v.shape, 0)
    sh = 1
    while sh < n:
        if reverse:
            v = v + jnp.where(row < n - sh, pltpu.roll(v, n - sh, axis=0), 0.0)
        else:
            v = v + jnp.where(row >= sh, pltpu.roll(v, sh, axis=0), 0.0)
        sh *= 2
    return v


def _log_sigmoid(z):
    return jnp.minimum(z, 0.0) - jnp.log1p(jnp.exp(-jnp.abs(z)))


def _forget_prep(fl, bf_row, name):
    s = fl.shape[0]

    def body(f_ref, b_ref, cb_ref):
        cum = _cumsum_rows(_log_sigmoid(f_ref[...] + b_ref[...]))
        for h in range(N_HEADS):
            cb_ref[:, h * LANES:(h + 1) * LANES] = jnp.broadcast_to(cum[:, h:h + 1], (s, LANES))

    return pl.pallas_call(
        body, name=name, out_shape=jax.ShapeDtypeStruct((s, N_HEADS * LANES), f32), compiler_params=_params(),
    )(fl, bf_row)


def _forget_prep_bwd(rs, dcs, fl, bf_row, name):
    s = fl.shape[0]

    def body(r_ref, c_ref, f_ref, b_ref, df_ref, db_ref):
        eye = (lax.broadcasted_iota(jnp.int32, (N_HEADS, LANES), 0) == lax.broadcasted_iota(jnp.int32, (N_HEADS, LANES), 1)).astype(f32)
        dcum = lax.dot_general(r_ref[...], eye, _TN, precision=lax.Precision.HIGHEST, preferred_element_type=f32)
        for h in range(N_HEADS):
            dcum = dcum - jnp.where(_lane() == h, jnp.sum(c_ref[:, h * LANES:(h + 1) * LANES], axis=1, keepdims=True), 0.0)
        dlf = _cumsum_rows(dcum, reverse=True)
        z = f_ref[...] + b_ref[...]
        df = jnp.where(_lane() < N_HEADS, dlf * jax.nn.sigmoid(-z), 0.0)
        df_ref[...] = df.astype(bf16)
        db_ref[...] = jnp.zeros_like(db_ref)
        db_ref[0:1, :] = jnp.sum(df, axis=0, keepdims=True)

    return pl.pallas_call(
        body, name=name,
        out_shape=[jax.ShapeDtypeStruct((s, LANES), bf16), jax.ShapeDtypeStruct((8, LANES), f32)], compiler_params=_params(),
    )(rs, dcs, fl, bf_row)


def _tile_mask(n_keys, n_queries, off, window):
    shape = (n_keys, n_queries)
    d = lax.broadcasted_iota(jnp.int32, shape, 1) - lax.broadcasted_iota(jnp.int32, shape, 0) + off
    valid = d >= 0
    return jnp.logical_and(valid, d < window) if window else valid


def _wide(v, t):
    return jnp.concatenate([v] * (t // LANES), axis=1)


def _attn_fwd(q, k, v, name, *, cum_b=None, sink_rows=None, window=None, t=256):
    s = q.shape[0]
    t = _row_tile(s, t)
    fox, has_sink = cum_b is not None, sink_rows is not None
    assert not window or (window % LANES == 0 and LANES + window <= s)

    def body(*refs):
        q_ref, k_ref, v_ref = refs[:3]
        rest = list(refs[3:])
        cb_ref = rest.pop(0) if fox else None
        sink_ref = rest.pop(0) if has_sink else None
        o_ref, lse_ref = rest
        i = pl.program_id(1)
        low = _lane() < HEAD_DIM
        top = lax.broadcasted_iota(jnp.int32, (LANES, 1), 0) < HEAD_DIM
        q2 = q_ref[...]
        zero = jnp.zeros_like(q2)
        qms = (jnp.where(low, q2, zero), jnp.where(low, zero, q2))

        def tile(k0, n_keys, off, carry, masked, queries=slice(0, t)):
            nq = queries.stop - queries.start
            kblk, vblk = k_ref[pl.ds(k0, n_keys), :], v_ref[pl.ds(k0, n_keys), :]
            valid = _tile_mask(n_keys, nq, off, window) if masked else None
            out = []
            for h in range(2):
                m, l, acc = carry[h]
                sc = lax.dot_general(kblk, qms[h][queries], _NT, preferred_element_type=f32)
                if fox:
                    sc = sc - _wide(cb_ref[pl.ds(k0, n_keys), h * LANES:(h + 1) * LANES], nq)
                if masked:
                    sc = jnp.where(valid, sc, NEG)
                m_new = jnp.maximum(m, jnp.max(sc, axis=0, keepdims=True))
                p = jnp.exp(sc - m_new)
                alpha = jnp.exp(m - m_new)
                l = alpha * l + jnp.sum(p, axis=0, keepdims=True)
                acc = alpha * acc + lax.dot_general(vblk, p.astype(bf16), _TN, preferred_element_type=f32)
                out.append((m_new, l, acc))
            return tuple(out)

        def start(nq):
            if has_sink:
                return tuple((_wide(sink_ref[h:h + 1, :], nq), jnp.ones((1, nq), f32), jnp.zeros((LANES, nq), f32))
                             for h in range(2))
            return tuple((jnp.full((1, nq), NEG, f32), jnp.zeros((1, nq), f32), jnp.zeros((LANES, nq), f32)) for h in range(2))

        def finish(carry, queries):
            (m0, l0, a0), (m1, l1, a1) = carry
            o_t = jnp.where(top, a0 * (1.0 / l0), a1 * (1.0 / l1))
            o_ref[queries, :] = o_t.T.astype(bf16)
            lse_ref[0:1, queries] = m0 + jnp.log(l0)
            lse_ref[1:2, queries] = m1 + jnp.log(l1)

        if window:
            for c in range(t // LANES):
                queries = slice(c * LANES, (c + 1) * LANES)
                q0 = i * t + c * LANES
                k0 = pl.multiple_of(jnp.maximum(q0 - window, 0), LANES)
                finish(tile(k0, LANES + window, q0 - k0, start(LANES), True, queries), queries)
        else:
            carry = lax.fori_loop(0, i, lambda kb, c: tile(pl.multiple_of(kb * t, t), t, 0, c, False), start(t))
            finish(tile(pl.multiple_of(i * t, t), t, 0, carry, True), slice(0, t))

    q_spec = pl.BlockSpec((t, LANES), lambda j, i: (i, j))
    kv_spec = pl.BlockSpec((s, LANES), lambda j, i: (0, j))
    in_specs, args = [q_spec, kv_spec, kv_spec], [q, k, v]
    if fox:
        in_specs += [pl.BlockSpec((s, 2 * LANES), lambda j, i: (0, j))]
        args += [cum_b]
    if has_sink:
        in_specs += [pl.BlockSpec((None, 2, LANES), lambda j, i: (j, 0, 0))]
        args += [sink_rows.reshape(N_PAIRS, 2, LANES)]
    return pl.pallas_call(
        body, name=name, grid=(N_PAIRS, s // t), in_specs=in_specs,
        out_specs=[q_spec, pl.BlockSpec((None, 2, t), lambda j, i: (j, 0, i))],
        out_shape=[jax.ShapeDtypeStruct((s, N_PAIRS * LANES), bf16), jax.ShapeDtypeStruct((N_PAIRS, 2, s), f32)],
        compiler_params=_params(2),
    )(*args)


def _attn_delta(do, o, name, *, lse=None, sink_rows=None):
    s, hw = do.shape
    tm = _row_tile(s, 512)
    has_sink = sink_rows is not None

    def body(*refs):
        do_ref, o_ref = refs[:2]
        if has_sink:
            lse_ref, sink_ref, dl_ref, ds_ref = refs[2:]

            @pl.when(pl.program_id(0) == 0)
            def _():
                ds_ref[...] = jnp.zeros_like(ds_ref)
        else:
            dl_ref, = refs[2:]
        for j in range(N_PAIRS):
            cols = slice(j * LANES, (j + 1) * LANES)
            prod_t = (do_ref[:, cols].astype(f32) * o_ref[:, cols].astype(f32)).T
            for h in range(2):
                dl = jnp.sum(prod_t[h * HEAD_DIM:(h + 1) * HEAD_DIM, :], axis=0, keepdims=True)
                dl_ref[j, h:h + 1, :] = dl
                if has_sink:
                    r = 2 * j + h
                    p_sink = jnp.exp(sink_ref[r:r + 1, 0:1] - lse_ref[j, h:h + 1, :])
                    ds_ref[r:r + 1, :] += -jnp.sum(p_sink * dl, axis=1, keepdims=True)

    rows_spec = pl.BlockSpec((N_PAIRS, 2, tm), lambda i: (0, 0, i))
    in_specs, args = [_row_spec(tm, hw)] * 2, [do, o]
    out_specs, out_shape = [rows_spec], [jax.ShapeDtypeStruct((N_PAIRS, 2, s), f32)]
    if has_sink:
        in_specs += [rows_spec, _vec_spec(LANES, N_HEADS)]
        args += [lse, sink_rows]
        out_specs += [_vec_spec(LANES, N_HEADS)]
        out_shape += [jax.ShapeDtypeStruct((N_HEADS, LANES), f32)]
    return pl.pallas_call(
        body, name=name, grid=(s // tm,), in_specs=in_specs, out_specs=out_specs, out_shape=out_shape,
        compiler_params=_params(1),
    )(*args)


def _attn_bwd(q, k, v, do, lse, delta, name, *, cum_b=None, window=None, t=256):
    s = q.shape[0]
    t = _row_tile(s, t)
    nblk = s // t
    fox = cum_b is not None
    assert not window or (window % LANES == 0 and LANES + window <= s)

    def body(*refs):
        k_ref, v_ref, q_ref, do_ref, lse_ref, dl_ref = refs[:6]
        rest = list(refs[6:])
        cb_ref = rest.pop(0) if fox else None
        dq_ref, dk_ref, dv_ref = rest[:3]
        dcs_ref, rs_ref = (rest[3], rest[4]) if fox else (None, None)
        b = pl.program_id(1)
        k0 = pl.multiple_of(b * t, t)

        @pl.when(b == 0)
        def _():
            dq_ref[...] = jnp.zeros_like(dq_ref)
            if fox:
                rs_ref[...] = jnp.zeros_like(rs_ref)

        dk_ref[...] = jnp.zeros_like(dk_ref)
        dv_ref[...] = jnp.zeros_like(dv_ref)
        if fox:
            dcs_ref[...] = jnp.zeros_like(dcs_ref)
        low = _lane() < HEAD_DIM
        top = lax.broadcasted_iota(jnp.int32, (LANES, 1), 0) < HEAD_DIM
        kblk, vblk = k_ref[...], v_ref[...]
        k_t = kblk.astype(f32).T.astype(bf16)
        cks = [_wide(cb_ref[pl.ds(k0, t), h * LANES:(h + 1) * LANES], t) for h in range(2)] if fox else None

        def tile(q0, n_queries, off, masked, keys=slice(0, t)):
            cols = pl.ds(q0, n_queries)
            q2, do2 = q_ref[cols, :], do_ref[cols, :]
            zero = jnp.zeros_like(q2)
            valid = _tile_mask(keys.stop - keys.start, n_queries, off, window) if masked else None
            dq_parts = []
            for h in range(2):
                qm = jnp.where(low, q2, zero) if h == 0 else jnp.where(low, zero, q2)
                dom = jnp.where(low, do2, zero) if h == 0 else jnp.where(low, zero, do2)
                sc = lax.dot_general(kblk[keys], qm, _NT, preferred_element_type=f32)
                if fox:
                    sc = sc - cks[h]
                if masked:
                    sc = jnp.where(valid, sc, NEG)
                p = jnp.exp(sc - lse_ref[h:h + 1, cols])
                dp = lax.dot_general(vblk[keys], dom, _NT, preferred_element_type=f32)
                ds = p * (dp - dl_ref[h:h + 1, cols])
                pb, dsb = p.astype(bf16), ds.astype(bf16)
                dv_ref[keys, :] += jnp.dot(pb, dom, preferred_element_type=f32)
                dk_ref[keys, :] += jnp.dot(dsb, qm, preferred_element_type=f32)
                dq_parts.append(jnp.dot(k_t[:, keys], dsb, preferred_element_type=f32))
                if fox:
                    dcs_ref[:, h * LANES:(h + 1) * LANES] += sum(ds[:, g * LANES:(g + 1) * LANES] for g in range(t // LANES))
                    rs_ref[h:h + 1, cols] += jnp.sum(ds, axis=0, keepdims=True)
            dq_ref[:, cols] += jnp.where(top, dq_parts[0], dq_parts[1])

        def later_block(qb, carry):
            tile(pl.multiple_of(qb * t, t), t, 0, False)
            return carry

        if window:
            for c in range(t // LANES):
                first = b * t + c * LANES
                q0 = pl.multiple_of(jnp.minimum(first, s - (LANES + window)), LANES)
                tile(q0, LANES + window, q0 - first, True, slice(c * LANES, (c + 1) * LANES))
        else:
            tile(k0, t, 0, True)
            lax.fori_loop(b + 1, nblk, later_block, 0)

    kv_spec = pl.BlockSpec((t, LANES), lambda j, b: (b, j))
    seq_spec = pl.BlockSpec((s, LANES), lambda j, b: (0, j))
    rows_spec = pl.BlockSpec((None, 2, s), lambda j, b: (j, 0, 0))
    hw = N_PAIRS * LANES
    in_specs, args = [kv_spec, kv_spec, seq_spec, seq_spec, rows_spec, rows_spec], [k, v, q, do, lse, delta]
    out_specs = [pl.BlockSpec((LANES, s), lambda j, b: (j, 0)), kv_spec, kv_spec]
    out_shape = [jax.ShapeDtypeStruct((hw, s), f32), jax.ShapeDtypeStruct((s, hw), f32), jax.ShapeDtypeStruct((s, hw), f32)]
    if fox:
        in_specs += [pl.BlockSpec((s, 2 * LANES), lambda j, b: (0, j))]
        args += [cum_b]
        out_specs += [pl.BlockSpec((t, 2 * LANES), lambda j, b: (b, j)), rows_spec]
        out_shape += [jax.ShapeDtypeStruct((s, N_HEADS * LANES), f32), jax.ShapeDtypeStruct((N_PAIRS, 2, s), f32)]
    return pl.pallas_call(
        body, name=name, grid=(N_PAIRS, nblk), in_specs=in_specs, out_specs=out_specs, out_shape=out_shape,
        compiler_params=_params(2),
    )(*args)


def _merge(ba, bb, gl, name):
    s, d = ba.shape
    tm = _row_tile(s, 512)

    def body(a_ref, b_ref, g_ref, o_ref):
        g0, g1 = jax.nn.sigmoid(g_ref[:, :d].astype(f32)), jax.nn.sigmoid(g_ref[:, d:].astype(f32))
        o_ref[...] = (g0 * a_ref[...].astype(f32) + g1 * b_ref[...].astype(f32)).astype(bf16)

    return pl.pallas_call(
        body, name=name, grid=(s // tm,), in_specs=[_row_spec(tm, d)] * 2 + [_row_spec(tm, 2 * d)],
        out_specs=_row_spec(tm, d), out_shape=jax.ShapeDtypeStruct((s, d), bf16), compiler_params=_params(1),
    )(ba, bb, gl)


def _merge_bwd(dm, ba, bb, gl, name):
    s, d = ba.shape
    tm = _row_tile(s, 512)

    def body(dm_ref, a_ref, b_ref, g_ref, da_ref, db_ref, dg_ref):
        dmv = dm_ref[...].astype(f32)
        g0, g1 = jax.nn.sigmoid(g_ref[:, :d].astype(f32)), jax.nn.sigmoid(g_ref[:, d:].astype(f32))
        da_ref[...] = (dmv * g0).astype(bf16)
        db_ref[...] = (dmv * g1).astype(bf16)
        dg_ref[:, :d] = (dmv * a_ref[...].astype(f32) * (g0 * (1.0 - g0))).astype(bf16)
        dg_ref[:, d:] = (dmv * b_ref[...].astype(f32) * (g1 * (1.0 - g1))).astype(bf16)

    return pl.pallas_call(
        body, name=name, grid=(s // tm,), in_specs=[_row_spec(tm, d)] * 3 + [_row_spec(tm, 2 * d)],
        out_specs=[_row_spec(tm, d)] * 2 + [_row_spec(tm, 2 * d)],
        out_shape=[jax.ShapeDtypeStruct((s, d), bf16)] * 2 + [jax.ShapeDtypeStruct((s, 2 * d), bf16)],
        compiler_params=_params(1),
    )(dm, ba, bb, gl)


GLU_TILE = 256


def _ffn_in_swiglu(h, w, name):
    s, d = h.shape
    f = w.shape[1] // 2
    tm = _row_tile(s, 2048)
    tg = GLU_TILE
    nb = f // tg

    def body(h_ref, wg_ref, wu_ref, g_ref, u_ref, act_ref):
        hv = h_ref[...]
        g = jnp.dot(hv, wg_ref[...], preferred_element_type=f32)
        u = jnp.dot(hv, wu_ref[...], preferred_element_type=f32)
        g_ref[...] = g.astype(bf16)
        u_ref[...] = u.astype(bf16)
        act_ref[...] = (g * jax.nn.sigmoid(g) * u).astype(bf16)

    col = pl.BlockSpec((tm, tg), lambda i, j: (i, j))
    return pl.pallas_call(
        body, name=name, grid=(s // tm, nb),
        in_specs=[pl.BlockSpec((tm, d), lambda i, j: (i, 0)), pl.BlockSpec((d, tg), lambda i, j: (0, j)),
                  pl.BlockSpec((d, tg), lambda i, j: (0, j + nb))],
        out_specs=[col] * 3, out_shape=[jax.ShapeDtypeStruct((s, f), bf16)] * 3, compiler_params=_params(2),
    )(h, w, w)


def _ffn_out_dgrad_swiglu(dy, w_out, g, u, name):
    s, d = dy.shape
    f = g.shape[1]
    tm = _row_tile(s, 2048)
    tg = GLU_TILE

    def body(dy_ref, w_ref, g_ref, u_ref, dg_ref, du_ref):
        dv = lax.dot_general(dy_ref[...], w_ref[...], _NT, preferred_element_type=f32)
        gv, uv = g_ref[...].astype(f32), u_ref[...].astype(f32)
        sg = jax.nn.sigmoid(gv)
        dg_ref[...] = (dv * uv * (sg * (1.0 + gv * (1.0 - sg)))).astype(bf16)
        du_ref[...] = (dv * (gv * sg)).astype(bf16)

    col = pl.BlockSpec((tm, tg), lambda i, j: (i, j))
    return pl.pallas_call(
        body, name=name, grid=(s // tm, f // tg),
        in_specs=[pl.BlockSpec((tm, d), lambda i, j: (i, 0)), pl.BlockSpec((tg, d), lambda i, j: (j, 0)), col, col],
        out_specs=[col] * 2, out_shape=[jax.ShapeDtypeStruct((s, f), bf16)] * 2, compiler_params=_params(2),
    )(dy, w_out, g, u)


def _ffn_in_dgrad(dg, du, w, name, after=None):
    s, f = dg.shape
    d = w.shape[0]
    tm, tn = _matmul_tiles(s, d, 2 * f, dg.dtype.itemsize, w.dtype.itemsize, 4)

    def body(dg_ref, du_ref, wg_ref, wu_ref, *rest):
        rest[-1][...] = (lax.dot_general(dg_ref[...], wg_ref[...], _NT, preferred_element_type=f32)
                         + lax.dot_general(du_ref[...], wu_ref[...], _NT, preferred_element_type=f32))

    extra = [] if after is None else [after]
    rows = pl.BlockSpec((tm, f), lambda i, j: (i, 0))
    return pl.pallas_call(
        body, name=name, grid=(s // tm, d // tn),
        in_specs=[rows, rows, pl.BlockSpec((tn, f), lambda i, j: (j, 0)), pl.BlockSpec((tn, f), lambda i, j: (j, 1))]
        + [pl.BlockSpec(memory_space=pl.ANY)] * len(extra),
        out_specs=pl.BlockSpec((tm, tn), lambda i, j: (i, j)),
        out_shape=jax.ShapeDtypeStruct((s, d), f32), compiler_params=_params(2),
    )(dg, du, w, w, *extra)


def _ada_fwd(c_all, w, b, name):
    def body(c_ref, w_ref, b_ref, o_ref):
        o_ref[...] = jnp.dot(c_ref[...].astype(bf16), w_ref[...].astype(bf16), preferred_element_type=f32) + b_ref[...]

    return pl.pallas_call(
        body, name=name, out_shape=jax.ShapeDtypeStruct((c_all.shape[0], w.shape[1]), f32), compiler_params=_params(),
    )(c_all, w, b)


def _ada_wgrad(c_all, d_all, name):
    n, d = c_all.shape
    w = d_all.shape[1]

    def body(c_ref, d_ref, o_ref):
        eye = (lax.broadcasted_iota(jnp.int32, (n, n), 0) == lax.broadcasted_iota(jnp.int32, (n, n), 1)).astype(f32)
        ct = lax.dot_general(c_ref[...], eye, _TN, precision=lax.Precision.HIGHEST, preferred_element_type=f32)
        g = ct[:, 0:1] * d_ref[0:1, :]
        for bi in range(1, n):
            g = g + ct[:, bi:bi + 1] * d_ref[bi:bi + 1, :]
        o_ref[0] = g

    return pl.pallas_call(
        body, name=name, out_shape=jax.ShapeDtypeStruct((1, d, w), f32), compiler_params=_params(),
    )(c_all, d_all)


def _adamw(parts, w, m, v, name, mine=None):
    r, c = w.shape
    n_parts = parts.shape[0]
    tr = next(t for t in range(min(r, 256), 0, -1) if r % t == 0 and (t % 16 == 0 or t == r))

    def body(p_ref, *rest):
        own_ref = rest[0] if mine is not None else None
        w_ref, m_ref, v_ref, g_ref, d_ref, nm_ref, nv_ref = rest[-7:]
        if mine is not None:
            x, y, cc = _me()
            me = 4 * x + 2 * y + cc

        def part(i):
            if mine is None:
                return p_ref[i].astype(f32)
            return jnp.where(me == i, own_ref[i], p_ref[i]).astype(f32)

        g = part(0)
        for i in range(1, n_parts):
            g = g + part(i)
        mm = ADAM_B1 * m_ref[...] + (1.0 - ADAM_B1) * g
        vv = ADAM_B2 * v_ref[...] + (1.0 - ADAM_B2) * (g * g)
        m_hat = mm / (1.0 - ADAM_B1 ** ADAM_STEP)
        v_hat = vv / (1.0 - ADAM_B2 ** ADAM_STEP)
        g_ref[...] = g
        d_ref[...] = -ADAM_LR * (m_hat / (jnp.sqrt(v_hat) + ADAM_EPS) + ADAM_WD * w_ref[...])
        nm_ref[...] = mm
        nv_ref[...] = vv

    spec = pl.BlockSpec((tr, c), lambda i: (i, 0))
    stack = [parts] if mine is None else [parts, mine]
    return pl.pallas_call(
        body, name=name, grid=(r // tr,),
        in_specs=[pl.BlockSpec((n_parts, tr, c), lambda i: (0, i, 0))] * len(stack) + [spec] * 3,
        out_specs=[spec] * 4, out_shape=[jax.ShapeDtypeStruct((r, c), f32)] * 4, compiler_params=_params(1),
    )(*stack, w, m, v)


def _me():
    return lax.axis_index("x"), lax.axis_index("y"), lax.axis_index("c")


def _all_gather(arrays, name, vmem=False, after=None):
    n = len(arrays)
    space = pltpu.VMEM if vmem else pl.ANY
    extra = [] if after is None else [after]

    def body(*refs):
        ins = refs[:n]
        outs = refs[n + len(extra):2 * n + len(extra)]
        send_sems, recv_sems, local_sems = refs[2 * n + len(extra):]
        x, y, c = _me()
        me, sibling = (x, y, c), (x, y, 1 - c)
        chips = [(1 - x, y), (x, 1 - y), (1 - x, 1 - y)]

        def rows(a, dev):
            return outs[a].at[4 * dev[0] + 2 * dev[1] + dev[2]]

        def copy(a, k, block, to, src=None):
            return pltpu.make_async_remote_copy(
                src_ref=rows(a, block) if src is None else src, dst_ref=rows(a, block),
                send_sem=send_sems.at[a, k], recv_sem=recv_sems.at[a, k], device_id=to, device_id_type=MESH)

        mine = [pltpu.make_async_copy(ins[a], rows(a, me), local_sems.at[a]) for a in range(n)]
        for cp in mine:
            cp.start()
        first = []
        for a in range(n):
            first.append(copy(a, 0, me, sibling, src=ins[a]))
            first += [copy(a, 1 + j, me, (*chip, c), src=ins[a]) for j, chip in enumerate(chips)]
        for cp in first:
            cp.start()
        passed = []
        for j, chip in enumerate(chips):
            for a in range(n):
                copy(a, 1 + j, (*chip, c), me).wait_recv()
                fwd = copy(a, 4 + j, (*chip, c), sibling)
                fwd.start()
                passed.append(fwd)
        for a in range(n):
            copy(a, 0, sibling, me).wait_recv()
            for j, chip in enumerate(chips):
                copy(a, 4 + j, (*chip, 1 - c), me).wait_recv()
        for cp in first + passed:
            cp.wait_send()
        for cp in mine:
            cp.wait()

    outs = pl.pallas_call(
        body, name=name,
        in_specs=[pl.BlockSpec(memory_space=space)] * n + [pl.BlockSpec(memory_space=pl.ANY)] * len(extra),
        out_specs=[pl.BlockSpec(memory_space=space)] * n,
        out_shape=[jax.ShapeDtypeStruct((N_DEV,) + a.shape, a.dtype) for a in arrays],
        scratch_shapes=[pltpu.SemaphoreType.DMA((n, 7)), pltpu.SemaphoreType.DMA((n, 7)), pltpu.SemaphoreType.DMA((n,))],
        compiler_params=pltpu.CompilerParams(vmem_limit_bytes=VMEM_LIMIT),
    )(*arrays, *extra)
    return list(outs)


_FLIPS = ((0, 0, 1), (1, 0, 0), (0, 1, 0), (1, 1, 0), (1, 0, 1), (0, 1, 1), (1, 1, 1))
_HBM = pl.BlockSpec(memory_space=pltpu.HBM)
_SEM = pl.BlockSpec(memory_space=pltpu.SEMAPHORE)


def _exchange_copies(scatter, srcs, lands, send_sems, recv_sems):
    x, y, c = _me()
    me_row = 4 * x + 2 * y + c
    out = []
    for k, (fx, fy, fc) in enumerate(_FLIPS):
        peer = (x ^ fx, y ^ fy, c ^ fc)
        peer_row = 4 * peer[0] + 2 * peer[1] + peer[2]
        for a in range(len(srcs)):
            out.append(pltpu.make_async_remote_copy(
                src_ref=srcs[a].at[peer_row] if scatter else srcs[a], dst_ref=lands[a].at[me_row],
                send_sem=send_sems.at[7 * a + k], recv_sem=recv_sems.at[7 * a + k], device_id=peer, device_id_type=MESH))
    return out


def _exchange_start(arrays, scatter, name):
    n = len(arrays)
    lands = [lax.empty(a.shape if scatter else (N_DEV,) + a.shape, a.dtype) for a in arrays]

    def body(*refs):
        srcs, zones = refs[:n], refs[n:2 * n]
        send_sems, recv_sems = refs[2 * n], refs[2 * n + 1]
        token = refs[-1]
        for cp in _exchange_copies(scatter, srcs, zones, send_sems, recv_sems):
            cp.start()
        token[...] = jnp.zeros_like(token)

    thru = [pltpu.HBM(a.shape, a.dtype) for a in list(arrays) + lands]
    outs = pl.pallas_call(
        body, name=name,
        out_shape=(pltpu.SemaphoreType.DMA((7 * n,)), pltpu.SemaphoreType.DMA((7 * n,)), *thru, jax.ShapeDtypeStruct((8, LANES), f32)),
        in_specs=[_HBM] * (2 * n), out_specs=(_SEM, _SEM, *[_HBM] * (2 * n), pl.BlockSpec(memory_space=pltpu.VMEM)),
        input_output_aliases={i: 2 + i for i in range(2 * n)},
        compiler_params=pltpu.CompilerParams(has_side_effects=pltpu.SideEffectType.DATAFLOW_SIDE_EFFECTING),
    )(*[pltpu.with_memory_space_constraint(a, pltpu.HBM) for a in list(arrays) + lands])
    return dict(n=n, scatter=scatter, sems=outs[:2], srcs=outs[2:2 + n], lands=outs[2 + n:2 + 2 * n], token=outs[-1])


def _exchange_wait(handle, after, name):
    n, scatter = handle["n"], handle["scatter"]

    def body(*refs):
        srcs, zones = refs[:n], refs[n:2 * n]
        send_sems, recv_sems = refs[2 * n], refs[2 * n + 1]
        for cp in _exchange_copies(scatter, srcs, zones, send_sems, recv_sems):
            cp.wait_send()
            cp.wait_recv()

    thru = [pltpu.HBM(a.shape, a.dtype) for a in list(handle["srcs"]) + list(handle["lands"])]
    outs = pl.pallas_call(
        body, name=name, out_shape=tuple(thru),
        in_specs=[_HBM] * (2 * n) + [_SEM, _SEM, pl.BlockSpec(memory_space=pl.ANY)], out_specs=tuple([_HBM] * (2 * n)),
        input_output_aliases={i: i for i in range(2 * n)},
        compiler_params=pltpu.CompilerParams(has_side_effects=pltpu.SideEffectType.DATAFLOW_SIDE_EFFECTING),
    )(*handle["srcs"], *handle["lands"], *handle["sems"], after)
    return list(outs[n:])


def _cols_from_shards(g):
    return jnp.transpose(g, (1, 0, 2)).reshape(g.shape[1], -1)


def _shards_from_cols(a):
    return jnp.transpose(a.reshape(a.shape[0], N_DEV, -1), (1, 0, 2))


def _local_step(x, positions, ada, g_pre_mix, g_post_mix, b_f, sinks, g_pre_ffn, g_post_ffn, target,
                w_in, late_weights, on_grads):
    s, d = x.shape
    row = lambda v: v.reshape(1, -1)
    shift_m, scale_m, gate_m, shift_f, scale_f, gate_f = (ada[i:i + 1] for i in range(6))
    w_gate, w_qkv = w_in[:, F_OFF + N_HEADS:], w_in[:, :QKV_W]
    w_f = jnp.pad(w_in[:, F_OFF:F_OFF + N_HEADS], ((0, 0), (0, LANES - N_HEADS)))
    w_in_p = jnp.concatenate([w_gate, w_qkv, w_f], axis=1)
    bf_row = jnp.pad(row(b_f), ((0, 0), (0, LANES - N_HEADS)))
    sink_rows = jnp.broadcast_to(sinks.reshape(N_HEADS, 1).astype(f32), (N_HEADS, LANES))
    inv_freq = 1.0 / (ROPE_THETA ** (jnp.arange(0, HEAD_DIM, 2, dtype=f32) / HEAD_DIM))
    cos, sin_s = _rope_tables(positions.reshape(s, 1), jnp.tile(inv_freq, 4).reshape(1, LANES), "rope_tables")

    h1 = _prenorm(x, row(g_pre_mix), scale_m, shift_m, "prenorm_mix")
    gl = _matmul(h1, w_gate, "nn", bf16, "proj_gate")
    qkv = _matmul(h1, w_qkv, "nn", f32, "proj_qkv")
    fl = _matmul(h1, w_f, "nn", f32, "proj_forget")
    qa, ka, va, qb, kb, vb = _qkv_prep(qkv, cos, sin_s, "qkv_prep")
    cum_b = _forget_prep(fl, bf_row, "forget_prep")
    o_a, lse_a = _attn_fwd(qa, ka, va, "swa_fwd", sink_rows=sink_rows, window=WINDOW, t=512)
    o_b, lse_b = _attn_fwd(qb, kb, vb, "fox_fwd", cum_b=cum_b, t=512)
    w_branch_a, w_branch_b, w_out, w_ffn_in, w_ffn_out = late_weights(o_b)
    ba = _matmul(o_a, w_branch_a, "nn", bf16, "branch_a")
    bb = _matmul(o_b, w_branch_b, "nn", bf16, "branch_b")
    merged = _merge(ba, bb, gl, "merge")
    y1 = _matmul(merged, w_out, "nn", f32, "out_proj")
    x2 = _postnorm_res(x, y1, row(g_post_mix), gate_m, "postnorm_mix")

    h2 = _prenorm(x2, row(g_pre_ffn), scale_f, shift_f, "prenorm_ffn")
    g_ff, u_ff, act = _ffn_in_swiglu(h2, w_ffn_in, "ffn_in_swiglu")
    y2 = _matmul(act, w_ffn_out, "nn", f32, "ffn_out")
    loss_row, d_out, d_y2, vec_pf = _loss_tail(x2, y2, row(g_post_ffn), gate_f, target, "loss_tail")

    g_w_ffn_out = _matmul(act, d_y2, "tn", bf16, "ffn_out_wgrad")
    dg_ff, du_ff = _ffn_out_dgrad_swiglu(d_y2, w_ffn_out, g_ff, u_ff, "ffn_out_dgrad_swiglu")
    g_w_ffn_in = jnp.concatenate([_matmul(h2, dg_ff, "tn", bf16, "ffn_gate_wgrad"),
                                  _matmul(h2, du_ff, "tn", bf16, "ffn_up_wgrad")], axis=1)
    sent = on_grads(dict(w_ffn_in=g_w_ffn_in, w_ffn_out=g_w_ffn_out))
    d_h2 = _ffn_in_dgrad(dg_ff, du_ff, w_ffn_in, "ffn_in_dgrad", after=sent)
    d_x2, vec_nf = _prenorm_bwd(d_h2, x2, row(g_pre_ffn), scale_f, d_out, "prenorm_ffn_bwd")

    d_y1, vec_pm = _postnorm_bwd(d_x2, y1, row(g_post_mix), gate_m, "postnorm_mix_bwd")
    g_w_out = _matmul(merged, d_y1, "tn", bf16, "out_proj_wgrad")
    d_merged = _matmul(d_y1, w_out, "nt", bf16, "out_proj_dgrad")
    d_ba, d_bb, dgl = _merge_bwd(d_merged, ba, bb, gl, "merge_bwd")
    g_w_branch_a = _matmul(o_a, d_ba, "tn", bf16, "branch_a_wgrad")
    g_w_branch_b = _matmul(o_b, d_bb, "tn", bf16, "branch_b_wgrad")
    sent = on_grads(dict(w_out=g_w_out, w_branch_a=g_w_branch_a, w_branch_b=g_w_branch_b))
    d_oa = _matmul(d_ba, w_branch_a, "nt", bf16, "branch_a_dgrad", after=sent)
    d_ob = _matmul(d_bb, w_branch_b, "nt", bf16, "branch_b_dgrad", after=sent)
    delta_a, d_sink = _attn_delta(d_oa, o_a, "swa_delta", lse=lse_a, sink_rows=sink_rows)
    delta_b, = _attn_delta(d_ob, o_b, "fox_delta")
    dqa_t, dka, dva = _attn_bwd(qa, ka, va, d_oa, lse_a, delta_a, "swa_bwd", window=WINDOW, t=512)
    dqb_t, dkb, dvb, dcs, rs = _attn_bwd(qb, kb, vb, d_ob, lse_b, delta_b, "fox_bwd", cum_b=cum_b, t=512)
    dqkv = _qkv_prep_bwd(dqa_t, dka, dva, dqb_t, dkb, dvb, cos, sin_s, "qkv_prep_bwd")
    dfl, vec_bf = _forget_prep_bwd(rs.reshape(N_HEADS, s), dcs, fl, bf_row, "forget_prep_bwd")
    dproj = jnp.concatenate([dgl, dqkv, dfl], axis=1)
    g_w_in_p = _matmul(h1, dproj, "tn", bf16, "in_proj_wgrad")
    g_w_in = jnp.concatenate([g_w_in_p[:, GATE_W:GATE_W + QKV_W], g_w_in_p[:, GATE_W + QKV_W:GATE_W + QKV_W + N_HEADS],
                              g_w_in_p[:, :GATE_W]], axis=1)
    sent = on_grads(dict(w_in=g_w_in))
    d_h1 = _matmul(dproj, w_in_p, "nt", f32, "in_proj_dgrad", after=sent)
    grad_x, vec_nm = _prenorm_bwd(d_h1, x, row(g_pre_mix), scale_m, d_x2, "prenorm_mix_bwd")

    d_ada = jnp.concatenate([vec_nm[0], vec_nm[1], vec_pm[0], vec_nf[0], vec_nf[1], vec_pf[0]])
    small = dict(b_ada=d_ada, g_pre_mix=vec_nm[2], g_post_mix=vec_pm[1], g_pre_ffn=vec_nf[2], g_post_ffn=vec_pf[1],
                 b_f=vec_bf[0, :N_HEADS], sinks=d_sink[:, 0], loss=loss_row[0, :1])
    return grad_x, small


_SMALL = (("b_ada", 6144), ("g_pre_mix", 1024), ("g_post_mix", 1024), ("g_pre_ffn", 1024), ("g_post_ffn", 1024),
          ("b_f", 128), ("sinks", 128), ("loss", 128))
_SMALL_ROWS = 88


def _pack_small(vals):
    parts = [jnp.pad(vals[k].reshape(-1).astype(f32), (0, n - vals[k].size)) for k, n in _SMALL]
    flat = jnp.concatenate(parts)
    return jnp.pad(flat, (0, _SMALL_ROWS * LANES - flat.size)).reshape(_SMALL_ROWS, LANES)


def _unpack_small(slab, shapes):
    flat, out, off = slab.reshape(-1), {}, 0
    for k, n in _SMALL:
        size = math.prod(shapes[k])
        out[k] = flat[off:off + size].reshape(shapes[k])
        off += n
    return out


def kernel(x, c, positions, w_ada, b_ada, g_pre_mix, g_post_mix, w_in, b_f, sinks, w_branch_a, w_branch_b, w_out, g_pre_ffn, g_post_ffn, w_ffn_in, w_ffn_out, loss_target, m_w_ada, m_b_ada, m_g_pre_mix, m_g_post_mix, m_w_in, m_b_f, m_sinks, m_w_branch_a, m_w_branch_b, m_w_out, m_g_pre_ffn, m_g_post_ffn, m_w_ffn_in, m_w_ffn_out, v_w_ada, v_b_ada, v_g_pre_mix, v_g_post_mix, v_w_in, v_b_f, v_sinks, v_w_branch_a, v_w_branch_b, v_w_out, v_g_pre_ffn, v_g_post_ffn, v_w_ffn_in, v_w_ffn_out):
    xi, yi, ci = _me()
    me = 4 * xi + 2 * yi + ci
    d = D_MODEL
    ada_w = w_ada.shape[2]

    c_all, = _all_gather([c], "gather_c", vmem=True)
    c_all = c_all.reshape(N_DEV, d)
    b_mine = lax.dynamic_slice(b_ada, (0, me * ada_w), (1, ada_w))
    ada_cols = _ada_fwd(c_all, w_ada[0], b_mine, "ada_fwd")
    ada_all, = _all_gather([ada_cols], "gather_ada", vmem=True)
    ada = lax.dynamic_index_in_dim(ada_all, me, axis=1, keepdims=False).reshape(6, d)

    g_in, = _all_gather([w_in[0].astype(bf16)], "gather_w_in")
    late = [w.astype(bf16) for w in (w_branch_a[0], w_branch_b[0], w_out[0], w_ffn_in[0], w_ffn_out[0])]
    late_h = _exchange_start(late, False, "gather_late_start")

    def mine_into(zone, block):
        return lax.dynamic_update_index_in_dim(zone, block, me, 0)

    def late_weights(after):
        zones = _exchange_wait(late_h, after, "gather_late_wait")
        g_ba, g_bb, g_out, g_fi, g_fo = (mine_into(z, w) for z, w in zip(zones, late))
        return (_cols_from_shards(g_ba), _cols_from_shards(g_bb), g_out.reshape(d, d), _cols_from_shards(g_fi),
                g_fo.reshape(D_FF, d))

    row_sharded = ("w_out", "w_ffn_out")
    in_flight = []

    def on_grads(group):
        sends = [g.reshape(N_DEV, g.shape[0] // N_DEV, g.shape[1]) if nm in row_sharded else _shards_from_cols(g)
                 for nm, g in group.items()]
        handle = _exchange_start(sends, True, "scatter_start_%d" % len(in_flight))
        in_flight.append((list(group), sends, handle))
        return handle["token"]

    grad_x, small = _local_step(
        x[0], positions[0], ada + late_h["token"][0, 0], g_pre_mix[0], g_post_mix[0], b_f[0], sinks[0], g_pre_ffn[0],
        g_post_ffn[0], loss_target[0], _cols_from_shards(g_in), late_weights, on_grads)

    ws = dict(w_in=(w_in, m_w_in, v_w_in), w_branch_a=(w_branch_a, m_w_branch_a, v_w_branch_a),
              w_branch_b=(w_branch_b, m_w_branch_b, v_w_branch_b), w_out=(w_out, m_w_out, v_w_out),
              w_ffn_in=(w_ffn_in, m_w_ffn_in, v_w_ffn_in), w_ffn_out=(w_ffn_out, m_w_ffn_out, v_w_ffn_out))
    res = {}

    def finish_group(gi, after):
        names, sends, handle = in_flight[gi]
        zones = _exchange_wait(handle, after, "scatter_wait_%d" % gi)
        for nm, zone, sent in zip(names, zones, sends):
            w, m, v = ws[nm]
            res[nm] = _adamw(zone, w[0], m[0], v[0], "adamw_" + nm, mine=sent)
            after = res[nm][0]
        return after

    done = finish_group(1, finish_group(0, grad_x))

    slab_all, = _all_gather([_pack_small(small)], "gather_small", vmem=True, after=done)
    small_w = dict(b_ada=b_ada, g_pre_mix=g_pre_mix, g_post_mix=g_post_mix, g_pre_ffn=g_pre_ffn, g_post_ffn=g_post_ffn,
                   b_f=b_f, sinks=sinks, loss=jnp.zeros((1,), f32))
    small_m = dict(b_ada=m_b_ada, g_pre_mix=m_g_pre_mix, g_post_mix=m_g_post_mix, g_pre_ffn=m_g_pre_ffn,
                   g_post_ffn=m_g_post_ffn, b_f=m_b_f, sinks=m_sinks, loss=jnp.zeros((1,), f32))
    small_v = dict(b_ada=v_b_ada, g_pre_mix=v_g_pre_mix, g_post_mix=v_g_post_mix, g_pre_ffn=v_g_pre_ffn,
                   g_post_ffn=v_g_post_ffn, b_f=v_b_f, sinks=v_sinks, loss=jnp.ones((1,), f32))
    shapes = {k: small_w[k].shape for k, _ in _SMALL}
    s_out = _adamw(slab_all, _pack_small(small_w), _pack_small(small_m), _pack_small(small_v), "adamw_small")
    s_grad, s_delta, s_m, s_v = (_unpack_small(o, shapes) for o in s_out)

    d_ada_all = lax.dynamic_slice(slab_all[:, :6144 // LANES, :].reshape(N_DEV, 6144), (0, me * ada_w), (N_DEV, ada_w))
    ada_parts = _ada_wgrad(c_all, d_ada_all, "ada_wgrad")

    res["w_ada"] = _adamw(ada_parts, w_ada[0], m_w_ada[0], v_w_ada[0], "adamw_w_ada")
    finish_group(2, res["w_ada"][0])

    order = ["w_ada", "b_ada", "g_pre_mix", "g_post_mix", "w_in", "b_f", "sinks", "w_branch_a", "w_branch_b", "w_out",
             "g_pre_ffn", "g_post_ffn", "w_ffn_in", "w_ffn_out"]
    outs = [s_grad["loss"].reshape(()), grad_x[None]]
    for which, small_o in enumerate((s_grad, s_delta, s_m, s_v)):
        for nm in order:
            outs.append(res[nm][which][None] if nm in res else small_o[nm])
    return tuple(outs)
```

```python
import functools
import math

import jax
import jax.numpy as jnp
from jax import lax
from jax.experimental import pallas as pl
from jax.experimental.pallas import tpu as pltpu

f32 = jnp.float32
bf16 = jnp.bfloat16

D_MODEL = 1024
HEAD_DIM = 64
N_HEADS = 8
N_PAIRS = 4
QKV_W = 2304
GATE_W = 2048
F_OFF = 2304
IN_W = 4360
WINDOW = 128
ROPE_THETA = 10000.0
RMS_EPS = 1e-6
D_FF = 2816
N_DEV = 8
ADAM_LR, ADAM_B1, ADAM_B2, ADAM_EPS, ADAM_WD, ADAM_STEP = 0.001, 0.9, 0.999, 1e-08, 0.01, 10
NEG = -1e30
LANES = 128
VMEM_LIMIT = 48 * 1024 * 1024
MESH = pl.DeviceIdType.MESH

_NT = (((1,), (1,)), ((), ()))
_TN = (((0,), (0,)), ((), ()))


def _params(n_grid=0):
    sem = ("arbitrary",) * n_grid if n_grid else None
    return pltpu.CompilerParams(dimension_semantics=sem, vmem_limit_bytes=VMEM_LIMIT)


def _row_tile(s, want):
    t = min(s, want)
    assert s % t == 0, (s, t)
    return t


MATMUL_VMEM_BUDGET = 40 * 1024 * 1024


def _matmul_tiles(m, n, k, a_item, b_item, o_item):
    def tiles(d):
        return [t for t in range(LANES, min(d, 2048) + 1, LANES) if d % t == 0] or [d]

    best = None
    for tm in tiles(m):
        for tn in tiles(n):
            vmem = 2 * (tm * k * a_item + tn * k * b_item + tm * tn * o_item) + tm * tn * 4
            if vmem > MATMUL_VMEM_BUDGET:
                continue
            traffic = m * k * a_item + n * k * b_item * (1 if tn == n else m // tm) + m * n * o_item
            steps = (m // tm) * (n // tn)
            key = (traffic, 0, steps) if steps >= 4 else (traffic, 1, -steps)
            if best is None or key < best[0]:
                best = (key, tm, tn)
    assert best is not None, (m, n, k)
    return best[1], best[2]


def _matmul(a, b, mode, out_dtype, name, after=None):
    if mode == "nn":
        (m, k), n = a.shape, b.shape[1]
    elif mode == "nt":
        (m, k), n = a.shape, b.shape[0]
    else:
        (k, m), n = a.shape, b.shape[1]
    tm, tn = _matmul_tiles(m, n, k, a.dtype.itemsize, b.dtype.itemsize, jnp.dtype(out_dtype).itemsize)
    if mode == "nn":
        a_spec, b_spec, dims = pl.BlockSpec((tm, k), lambda i, j: (i, 0)), pl.BlockSpec((k, tn), lambda i, j: (0, j)), None
    elif mode == "nt":
        a_spec, b_spec, dims = pl.BlockSpec((tm, k), lambda i, j: (i, 0)), pl.BlockSpec((tn, k), lambda i, j: (j, 0)), _NT
    else:
        a_spec, b_spec, dims = pl.BlockSpec((k, tm), lambda i, j: (0, i)), pl.BlockSpec((k, tn), lambda i, j: (0, j)), _TN

    def body(a_ref, b_ref, *rest):
        o_ref = rest[-1]
        av, bv = a_ref[...].astype(bf16), b_ref[...].astype(bf16)
        if dims is None:
            r = jnp.dot(av, bv, preferred_element_type=f32)
        else:
            r = lax.dot_general(av, bv, dims, preferred_element_type=f32)
        o_ref[...] = r.astype(out_dtype)

    extra = [] if after is None else [after]
    return pl.pallas_call(
        body, name=name, grid=(m // tm, n // tn), in_specs=[a_spec, b_spec] + [pl.BlockSpec(memory_space=pl.ANY)] * len(extra),
        out_specs=pl.BlockSpec((tm, tn), lambda i, j: (i, j)),
        out_shape=jax.ShapeDtypeStruct((m, n), out_dtype), compiler_params=_params(2),
    )(a, b, *extra)


def _rstd(v):
    return lax.rsqrt(jnp.mean(v * v, axis=-1, keepdims=True) + RMS_EPS)


def _row_spec(tm, d):
    return pl.BlockSpec((tm, d), lambda i: (i, 0))


def _vec_spec(d, rows=1):
    return pl.BlockSpec((rows, d), lambda i: (0, 0))


def _prenorm(x, g, scale, shift, name):
    s, d = x.shape
    tm = _row_tile(s, 512)

    def body(x_ref, g_ref, sc_ref, sh_ref, h_ref):
        xv = x_ref[...]
        h = (xv * _rstd(xv) * g_ref[...]) * (1.0 + sc_ref[...]) + sh_ref[...]
        h_ref[...] = h.astype(bf16)

    return pl.pallas_call(
        body, name=name, grid=(s // tm,), in_specs=[_row_spec(tm, d)] + [_vec_spec(d)] * 3,
        out_specs=_row_spec(tm, d), out_shape=jax.ShapeDtypeStruct((s, d), bf16), compiler_params=_params(1),
    )(x, g, scale, shift)


def _postnorm_res(x, y, g, gate, name):
    s, d = x.shape
    tm = _row_tile(s, 512)

    def body(x_ref, y_ref, g_ref, gate_ref, o_ref):
        yv = y_ref[...]
        o_ref[...] = x_ref[...] + gate_ref[...] * (yv * _rstd(yv) * g_ref[...])

    return pl.pallas_call(
        body, name=name, grid=(s // tm,), in_specs=[_row_spec(tm, d)] * 2 + [_vec_spec(d)] * 2,
        out_specs=_row_spec(tm, d), out_shape=jax.ShapeDtypeStruct((s, d), f32), compiler_params=_params(1),
    )(x, y, g, gate)


def _rms_bwd(u, v, r):
    return r * u - v * (r * r * r) * jnp.mean(u * v, axis=-1, keepdims=True)


def _loss_tail(x, y, g, gate, target, name):
    s, d = x.shape
    tm = _row_tile(s, 512)

    def body(x_ref, y_ref, g_ref, gate_ref, t_ref, loss_ref, do_ref, dy_ref, vec_ref):
        @pl.when(pl.program_id(0) == 0)
        def _():
            loss_ref[...] = jnp.zeros_like(loss_ref)
            vec_ref[...] = jnp.zeros_like(vec_ref)
        yv = y_ref[...]
        r = _rstd(yv)
        yn = yv * r
        err = x_ref[...] + gate_ref[...] * (yn * g_ref[...]) - t_ref[...]
        loss_ref[...] += 0.5 * jnp.sum(jnp.mean(err * err, axis=-1, keepdims=True), axis=0, keepdims=True)
        dr = err / d
        do_ref[...] = dr
        dn = dr * gate_ref[...]
        vec_ref[0:1, :] += jnp.sum(dr * (yn * g_ref[...]), axis=0, keepdims=True)
        vec_ref[1:2, :] += jnp.sum(dn * yn, axis=0, keepdims=True)
        dy_ref[...] = _rms_bwd(dn * g_ref[...], yv, r).astype(bf16)

    return pl.pallas_call(
        body, name=name, grid=(s // tm,), in_specs=[_row_spec(tm, d)] * 2 + [_vec_spec(d)] * 2 + [_row_spec(tm, d)],
        out_specs=[_vec_spec(LANES), _row_spec(tm, d), _row_spec(tm, d), _vec_spec(d, 8)],
        out_shape=[jax.ShapeDtypeStruct((1, LANES), f32), jax.ShapeDtypeStruct((s, d), f32),
                   jax.ShapeDtypeStruct((s, d), bf16), jax.ShapeDtypeStruct((8, d), f32)],
        compiler_params=_params(1),
    )(x, y, g, gate, target)


def _postnorm_bwd(dres, y, g, gate, name):
    s, d = y.shape
    tm = _row_tile(s, 512)

    def body(dr_ref, y_ref, g_ref, gate_ref, dy_ref, vec_ref):
        @pl.when(pl.program_id(0) == 0)
        def _():
            vec_ref[...] = jnp.zeros_like(vec_ref)
        dr, yv = dr_ref[...], y_ref[...]
        r = _rstd(yv)
        yn = yv * r
        dn = dr * gate_ref[...]
        vec_ref[0:1, :] += jnp.sum(dr * (yn * g_ref[...]), axis=0, keepdims=True)
        vec_ref[1:2, :] += jnp.sum(dn * yn, axis=0, keepdims=True)
        dy_ref[...] = _rms_bwd(dn * g_ref[...], yv, r).astype(bf16)

    return pl.pallas_call(
        body, name=name, grid=(s // tm,), in_specs=[_row_spec(tm, d)] * 2 + [_vec_spec(d)] * 2,
        out_specs=[_row_spec(tm, d), _vec_spec(d, 8)],
        out_shape=[jax.ShapeDtypeStruct((s, d), bf16), jax.ShapeDtypeStruct((8, d), f32)], compiler_params=_params(1),
    )(dres, y, g, gate)


def _prenorm_bwd(dh, x, g, scale, dres, name):
    s, d = x.shape
    tm = _row_tile(s, 512)

    def body(dh_ref, x_ref, g_ref, sc_ref, dr_ref, dx_ref, vec_ref):
        @pl.when(pl.program_id(0) == 0)
        def _():
            vec_ref[...] = jnp.zeros_like(vec_ref)
        dhv, xv = dh_ref[...], x_ref[...]
        r = _rstd(xv)
        xn = xv * r
        dn = dhv * (1.0 + sc_ref[...])
        vec_ref[0:1, :] += jnp.sum(dhv, axis=0, keepdims=True)
        vec_ref[1:2, :] += jnp.sum(dhv * (xn * g_ref[...]), axis=0, keepdims=True)
        vec_ref[2:3, :] += jnp.sum(dn * xn, axis=0, keepdims=True)
        dx_ref[...] = dr_ref[...] + _rms_bwd(dn * g_ref[...], xv, r)

    return pl.pallas_call(
        body, name=name, grid=(s // tm,),
        in_specs=[_row_spec(tm, d)] * 2 + [_vec_spec(d)] * 2 + [_row_spec(tm, d)],
        out_specs=[_row_spec(tm, d), _vec_spec(d, 8)],
        out_shape=[jax.ShapeDtypeStruct((s, d), f32), jax.ShapeDtypeStruct((8, d), f32)], compiler_params=_params(1),
    )(dh, x, g, scale, dres)


def _lane():
    return lax.broadcasted_iota(jnp.int32, (1, LANES), 1)


def _rope_tables(pos_col, inv_freq, name):
    s = pos_col.shape[0]

    def body(p_ref, f_ref, cos_ref, sin_ref):
        ang = p_ref[...].astype(f32) * f_ref[...]
        first_half = (_lane() % HEAD_DIM) < HEAD_DIM // 2
        cos_ref[...] = jnp.cos(ang)
        sn = jnp.sin(ang)
        sin_ref[...] = jnp.where(first_half, -sn, sn)

    return pl.pallas_call(
        body, name=name, out_shape=[jax.ShapeDtypeStruct((s, LANES), f32)] * 2, compiler_params=_params(),
    )(pos_col, inv_freq)


def _swap_halves(v):
    first_half = (_lane() % HEAD_DIM) < HEAD_DIM // 2
    return jnp.where(first_half, pltpu.roll(v, LANES - HEAD_DIM // 2, axis=1), pltpu.roll(v, HEAD_DIM // 2, axis=1))


def _qkv_prep(qkv, cos, sin_s, name):
    s = qkv.shape[0]
    tm = _row_tile(s, 256)
    scale = 1.0 / math.sqrt(HEAD_DIM)

    def body(p_ref, c_ref, s_ref, qa_ref, ka_ref, va_ref, qb_ref, kb_ref, vb_ref):
        cs, sn = c_ref[...], s_ref[...]
        low = _lane() < HEAD_DIM

        def blk(j):
            return p_ref[:, j * LANES:(j + 1) * LANES]

        def rope(v):
            return v * cs + _swap_halves(v) * sn

        def expand(v):
            other = pltpu.roll(v, HEAD_DIM, axis=1)
            return jnp.where(low, v, other), jnp.where(low, other, v)

        for j in range(N_PAIRS):
            qa_ref[:, j * LANES:(j + 1) * LANES] = (rope(blk(j)) * scale).astype(bf16)
            qb_ref[:, j * LANES:(j + 1) * LANES] = (blk(6 + j) * scale).astype(bf16)
            kb_ref[:, j * LANES:(j + 1) * LANES] = blk(10 + j).astype(bf16)
            vb_ref[:, j * LANES:(j + 1) * LANES] = blk(14 + j).astype(bf16)
        k0, k1 = expand(rope(blk(4)))
        v0, v1 = expand(blk(5))
        for j in range(N_PAIRS):
            ka_ref[:, j * LANES:(j + 1) * LANES] = (k0 if j < 2 else k1).astype(bf16)
            va_ref[:, j * LANES:(j + 1) * LANES] = (v0 if j < 2 else v1).astype(bf16)

    hw = N_PAIRS * LANES
    return pl.pallas_call(
        body, name=name, grid=(s // tm,),
        in_specs=[_row_spec(tm, QKV_W), _row_spec(tm, LANES), _row_spec(tm, LANES)],
        out_specs=[_row_spec(tm, hw)] * 6, out_shape=[jax.ShapeDtypeStruct((s, hw), bf16)] * 6, compiler_params=_params(1),
    )(qkv, cos, sin_s)


def _qkv_prep_bwd(dqa_t, dka, dva, dqb_t, dkb, dvb, cos, sin_s, name):
    s = dka.shape[0]
    tm = _row_tile(s, 256)
    scale = 1.0 / math.sqrt(HEAD_DIM)
    hw = N_PAIRS * LANES
    t_spec = pl.BlockSpec((hw, tm), lambda i: (0, i))

    def body(dqa_ref, dka_ref, dva_ref, dqb_ref, dkb_ref, dvb_ref, c_ref, s_ref, o_ref):
        cs, sn = c_ref[...], s_ref[...]
        low = _lane() < HEAD_DIM

        def blk(ref, j):
            return ref[:, j * LANES:(j + 1) * LANES]

        def blk_t(ref, j):
            return ref[j * LANES:(j + 1) * LANES, :].T

        def unrope(v):
            return v * cs + _swap_halves(v * sn)

        def fold(ref):
            a, b = blk(ref, 0) + blk(ref, 1), blk(ref, 2) + blk(ref, 3)
            kv0 = a + pltpu.roll(a, HEAD_DIM, axis=1)
            kv1 = b + pltpu.roll(b, HEAD_DIM, axis=1)
            return jnp.where(low, kv0, kv1)

        for j in range(N_PAIRS):
            o_ref[:, j * LANES:(j + 1) * LANES] = (unrope(blk_t(dqa_ref, j)) * scale).astype(bf16)
            o_ref[:, (6 + j) * LANES:(7 + j) * LANES] = (blk_t(dqb_ref, j) * scale).astype(bf16)
            o_ref[:, (10 + j) * LANES:(11 + j) * LANES] = blk(dkb_ref, j).astype(bf16)
            o_ref[:, (14 + j) * LANES:(15 + j) * LANES] = blk(dvb_ref, j).astype(bf16)
        o_ref[:, 4 * LANES:5 * LANES] = unrope(fold(dka_ref)).astype(bf16)
        o_ref[:, 5 * LANES:6 * LANES] = fold(dva_ref).astype(bf16)

    return pl.pallas_call(
        body, name=name, grid=(s // tm,),
        in_specs=[t_spec, _row_spec(tm, hw), _row_spec(tm, hw), t_spec, _row_spec(tm, hw), _row_spec(tm, hw)] + [_row_spec(tm, LANES)] * 2,
        out_specs=_row_spec(tm, QKV_W), out_shape=jax.ShapeDtypeStruct((s, QKV_W), bf16), compiler_params=_params(1),
    )(dqa_t, dka, dva, dqb_t, dkb, dvb, cos, sin_s)


def _cumsum_rows(v, reverse=False):
    n = v.shape[0]
    row = lax.broadcasted_iota(jnp.int32, v.shape, 0)
    sh = 1
    while sh < n:
        if reverse:
            v = v + jnp.where(row < n - sh, pltpu.roll(v, n - sh, axis=0), 0.0)
        else:
            v = v + jnp.where(row >= sh, pltpu.roll(v, sh, axis=0), 0.0)
        sh *= 2
    return v


def _log_sigmoid(z):
    return jnp.minimum(z, 0.0) - jnp.log1p(jnp.exp(-jnp.abs(z)))


def _forget_prep(fl, bf_row, name):
    s = fl.shape[0]

    def body(f_ref, b_ref, cb_ref):
        cum = _cumsum_rows(_log_sigmoid(f_ref[...] + b_ref[...]))
        for h in range(N_HEADS):
            cb_ref[:, h * LANES:(h + 1) * LANES] = jnp.broadcast_to(cum[:, h:h + 1], (s, LANES))

    return pl.pallas_call(
        body, name=name, out_shape=jax.ShapeDtypeStruct((s, N_HEADS * LANES), f32), compiler_params=_params(),
    )(fl, bf_row)


def _forget_prep_bwd(rs, dcs, fl, bf_row, name):
    s = fl.shape[0]

    def body(r_ref, c_ref, f_ref, b_ref, df_ref, db_ref):
        eye = (lax.broadcasted_iota(jnp.int32, (N_HEADS, LANES), 0) == lax.broadcasted_iota(jnp.int32, (N_HEADS, LANES), 1)).astype(f32)
        dcum = lax.dot_general(r_ref[...], eye, _TN, precision=lax.Precision.HIGHEST, preferred_element_type=f32)
        for h in range(N_HEADS):
            dcum = dcum - jnp.where(_lane() == h, jnp.sum(c_ref[:, h * LANES:(h + 1) * LANES], axis=1, keepdims=True), 0.0)
        dlf = _cumsum_rows(dcum, reverse=True)
        z = f_ref[...] + b_ref[...]
        df = jnp.where(_lane() < N_HEADS, dlf * jax.nn.sigmoid(-z), 0.0)
        df_ref[...] = df.astype(bf16)
        db_ref[...] = jnp.zeros_like(db_ref)
        db_ref[0:1, :] = jnp.sum(df, axis=0, keepdims=True)

    return pl.pallas_call(
        body, name=name,
        out_shape=[jax.ShapeDtypeStruct((s, LANES), bf16), jax.ShapeDtypeStruct((8, LANES), f32)], compiler_params=_params(),
    )(rs, dcs, fl, bf_row)


def _tile_mask(n_keys, n_queries, off, window):
    shape = (n_keys, n_queries)
    d = lax.broadcasted_iota(jnp.int32, shape, 1) - lax.broadcasted_iota(jnp.int32, shape, 0) + off
    valid = d >= 0
    return jnp.logical_and(valid, d < window) if window else valid


def _wide(v, t):
    return jnp.concatenate([v] * (t // LANES), axis=1)


def _attn_fwd(q, k, v, name, *, cum_b=None, sink_rows=None, window=None, t=256):
    s = q.shape[0]
    t = _row_tile(s, t)
    fox, has_sink = cum_b is not None, sink_rows is not None
    assert not window or (window % LANES == 0 and LANES + window <= s)

    def body(*refs):
        q_ref, k_ref, v_ref = refs[:3]
        rest = list(refs[3:])
        cb_ref = rest.pop(0) if fox else None
        sink_ref = rest.pop(0) if has_sink else None
        o_ref, lse_ref = rest
        i = pl.program_id(1)
        low = _lane() < HEAD_DIM
        top = lax.broadcasted_iota(jnp.int32, (LANES, 1), 0) < HEAD_DIM
        q2 = q_ref[...]
        zero = jnp.zeros_like(q2)
        qms = (jnp.where(low, q2, zero), jnp.where(low, zero, q2))

        def tile(k0, n_keys, off, carry, masked, queries=slice(0, t)):
            nq = queries.stop - queries.start
            kblk, vblk = k_ref[pl.ds(k0, n_keys), :], v_ref[pl.ds(k0, n_keys), :]
            valid = _tile_mask(n_keys, nq, off, window) if masked else None
            out = []
            for h in range(2):
                m, l, acc = carry[h]
                sc = lax.dot_general(kblk, qms[h][queries], _NT, preferred_element_type=f32)
                if fox:
                    sc = sc - _wide(cb_ref[pl.ds(k0, n_keys), h * LANES:(h + 1) * LANES], nq)
                if masked:
                    sc = jnp.where(valid, sc, NEG)
                m_new = jnp.maximum(m, jnp.max(sc, axis=0, keepdims=True))
                p = jnp.exp(sc - m_new)
                alpha = jnp.exp(m - m_new)
                l = alpha * l + jnp.sum(p, axis=0, keepdims=True)
                acc = alpha * acc + lax.dot_general(vblk, p.astype(bf16), _TN, preferred_element_type=f32)
                out.append((m_new, l, acc))
            return tuple(out)

        def start(nq):
            if has_sink:
                return tuple((_wide(sink_ref[h:h + 1, :], nq), jnp.ones((1, nq), f32), jnp.zeros((LANES, nq), f32))
                             for h in range(2))
            return tuple((jnp.full((1, nq), NEG, f32), jnp.zeros((1, nq), f32), jnp.zeros((LANES, nq), f32)) for h in range(2))

        def finish(carry, queries):
            (m0, l0, a0), (m1, l1, a1) = carry
            o_t = jnp.where(top, a0 * (1.0 / l0), a1 * (1.0 / l1))
            o_ref[queries, :] = o_t.T.astype(bf16)
            lse_ref[0:1, queries] = m0 + jnp.log(l0)
            lse_ref[1:2, queries] = m1 + jnp.log(l1)

        if window:
            for c in range(t // LANES):
                queries = slice(c * LANES, (c + 1) * LANES)
                q0 = i * t + c * LANES
                k0 = pl.multiple_of(jnp.maximum(q0 - window, 0), LANES)
                finish(tile(k0, LANES + window, q0 - k0, start(LANES), True, queries), queries)
        else:
            carry = lax.fori_loop(0, i, lambda kb, c: tile(pl.multiple_of(kb * t, t), t, 0, c, False), start(t))
            finish(tile(pl.multiple_of(i * t, t), t, 0, carry, True), slice(0, t))

    q_spec = pl.BlockSpec((t, LANES), lambda j, i: (i, j))
    kv_spec = pl.BlockSpec((s, LANES), lambda j, i: (0, j))
    in_specs, args = [q_spec, kv_spec, kv_spec], [q, k, v]
    if fox:
        in_specs += [pl.BlockSpec((s, 2 * LANES), lambda j, i: (0, j))]
        args += [cum_b]
    if has_sink:
        in_specs += [pl.BlockSpec((None, 2, LANES), lambda j, i: (j, 0, 0))]
        args += [sink_rows.reshape(N_PAIRS, 2, LANES)]
    return pl.pallas_call(
        body, name=name, grid=(N_PAIRS, s // t), in_specs=in_specs,
        out_specs=[q_spec, pl.BlockSpec((None, 2, t), lambda j, i: (j, 0, i))],
        out_shape=[jax.ShapeDtypeStruct((s, N_PAIRS * LANES), bf16), jax.ShapeDtypeStruct((N_PAIRS, 2, s), f32)],
        compiler_params=_params(2),
    )(*args)


def _attn_delta(do, o, name, *, lse=None, sink_rows=None):
    s, hw = do.shape
    tm = _row_tile(s, 512)
    has_sink = sink_rows is not None

    def body(*refs):
        do_ref, o_ref = refs[:2]
        if has_sink:
            lse_ref, sink_ref, dl_ref, ds_ref = refs[2:]

            @pl.when(pl.program_id(0) == 0)
            def _():
                ds_ref[...] = jnp.zeros_like(ds_ref)
        else:
            dl_ref, = refs[2:]
        for j in range(N_PAIRS):
            cols = slice(j * LANES, (j + 1) * LANES)
            prod_t = (do_ref[:, cols].astype(f32) * o_ref[:, cols].astype(f32)).T
            for h in range(2):
                dl = jnp.sum(prod_t[h * HEAD_DIM:(h + 1) * HEAD_DIM, :], axis=0, keepdims=True)
                dl_ref[j, h:h + 1, :] = dl
                if has_sink:
                    r = 2 * j + h
                    p_sink = jnp.exp(sink_ref[r:r + 1, 0:1] - lse_ref[j, h:h + 1, :])
                    ds_ref[r:r + 1, :] += -jnp.sum(p_sink * dl, axis=1, keepdims=True)

    rows_spec = pl.BlockSpec((N_PAIRS, 2, tm), lambda i: (0, 0, i))
    in_specs, args = [_row_spec(tm, hw)] * 2, [do, o]
    out_specs, out_shape = [rows_spec], [jax.ShapeDtypeStruct((N_PAIRS, 2, s), f32)]
    if has_sink:
        in_specs += [rows_spec, _vec_spec(LANES, N_HEADS)]
        args += [lse, sink_rows]
        out_specs += [_vec_spec(LANES, N_HEADS)]
        out_shape += [jax.ShapeDtypeStruct((N_HEADS, LANES), f32)]
    return pl.pallas_call(
        body, name=name, grid=(s // tm,), in_specs=in_specs, out_specs=out_specs, out_shape=out_shape,
        compiler_params=_params(1),
    )(*args)


def _attn_bwd(q, k, v, do, lse, delta, name, *, cum_b=None, window=None, t=256):
    s = q.shape[0]
    t = _row_tile(s, t)
    nblk = s // t
    fox = cum_b is not None
    assert not window or (window % LANES == 0 and LANES + window <= s)

    def body(*refs):
        k_ref, v_ref, q_ref, do_ref, lse_ref, dl_ref = refs[:6]
        rest = list(refs[6:])
        cb_ref = rest.pop(0) if fox else None
        dq_ref, dk_ref, dv_ref = rest[:3]
        dcs_ref, rs_ref = (rest[3], rest[4]) if fox else (None, None)
        b = pl.program_id(1)
        k0 = pl.multiple_of(b * t, t)

        @pl.when(b == 0)
        def _():
            dq_ref[...] = jnp.zeros_like(dq_ref)
            if fox:
                rs_ref[...] = jnp.zeros_like(rs_ref)

        dk_ref[...] = jnp.zeros_like(dk_ref)
        dv_ref[...] = jnp.zeros_like(dv_ref)
        if fox:
            dcs_ref[...] = jnp.zeros_like(dcs_ref)
        low = _lane() < HEAD_DIM
        top = lax.broadcasted_iota(jnp.int32, (LANES, 1), 0) < HEAD_DIM
        kblk, vblk = k_ref[...], v_ref[...]
        k_t = kblk.astype(f32).T.astype(bf16)
        cks = [_wide(cb_ref[pl.ds(k0, t), h * LANES:(h + 1) * LANES], t) for h in range(2)] if fox else None

        def tile(q0, n_queries, off, masked, keys=slice(0, t)):
            cols = pl.ds(q0, n_queries)
            q2, do2 = q_ref[cols, :], do_ref[cols, :]
            zero = jnp.zeros_like(q2)
            valid = _tile_mask(keys.stop - keys.start, n_queries, off, window) if masked else None
            dq_parts = []
            for h in range(2):
                qm = jnp.where(low, q2, zero) if h == 0 else jnp.where(low, zero, q2)
                dom = jnp.where(low, do2, zero) if h == 0 else jnp.where(low, zero, do2)
                sc = lax.dot_general(kblk[keys], qm, _NT, preferred_element_type=f32)
                if fox:
                    sc = sc - cks[h]
                if masked:
                    sc = jnp.where(valid, sc, NEG)
                p = jnp.exp(sc - lse_ref[h:h + 1, cols])
                dp = lax.dot_general(vblk[keys], dom, _NT, preferred_element_type=f32)
                ds = p * (dp - dl_ref[h:h + 1, cols])
                pb, dsb = p.astype(bf16), ds.astype(bf16)
                dv_ref[keys, :] += jnp.dot(pb, dom, preferred_element_type=f32)
                dk_ref[keys, :] += jnp.dot(dsb, qm, preferred_element_type=f32)
                dq_parts.append(jnp.dot(k_t[:, keys], dsb, preferred_element_type=f32))
                if fox:
                    dcs_ref[:, h * LANES:(h + 1) * LANES] += sum(ds[:, g * LANES:(g + 1) * LANES] for g in range(t // LANES))
                    rs_ref[h:h + 1, cols] += jnp.sum(ds, axis=0, keepdims=True)
            dq_ref[:, cols] += jnp.where(top, dq_parts[0], dq_parts[1])

        def later_block(qb, carry):
            tile(pl.multiple_of(qb * t, t), t, 0, False)
            return carry

        if window:
            for c in range(t // LANES):
                first = b * t + c * LANES
                q0 = pl.multiple_of(jnp.minimum(first, s - (LANES + window)), LANES)
                tile(q0, LANES + window, q0 - first, True, slice(c * LANES, (c + 1) * LANES))
        else:
            tile(k0, t, 0, True)
            lax.fori_loop(b + 1, nblk, later_block, 0)

    kv_spec = pl.BlockSpec((t, LANES), lambda j, b: (b, j))
    seq_spec = pl.BlockSpec((s, LANES), lambda j, b: (0, j))
    rows_spec = pl.BlockSpec((None, 2, s), lambda j, b: (j, 0, 0))
    hw = N_PAIRS * LANES
    in_specs, args = [kv_spec, kv_spec, seq_spec, seq_spec, rows_spec, rows_spec], [k, v, q, do, lse, delta]
    out_specs = [pl.BlockSpec((LANES, s), lambda j, b: (j, 0)), kv_spec, kv_spec]
    out_shape = [jax.ShapeDtypeStruct((hw, s), f32), jax.ShapeDtypeStruct((s, hw), f32), jax.ShapeDtypeStruct((s, hw), f32)]
    if fox:
        in_specs += [pl.BlockSpec((s, 2 * LANES), lambda j, b: (0, j))]
        args += [cum_b]
        out_specs += [pl.BlockSpec((t, 2 * LANES), lambda j, b: (b, j)), rows_spec]
        out_shape += [jax.ShapeDtypeStruct((s, N_HEADS * LANES), f32), jax.ShapeDtypeStruct((N_PAIRS, 2, s), f32)]
    return pl.pallas_call(
        body, name=name, grid=(N_PAIRS, nblk), in_specs=in_specs, out_specs=out_specs, out_shape=out_shape,
        compiler_params=_params(2),
    )(*args)


def _merge(ba, bb, gl, name):
    s, d = ba.shape
    tm = _row_tile(s, 512)

    def body(a_ref, b_ref, g_ref, o_ref):
        g0, g1 = jax.nn.sigmoid(g_ref[:, :d].astype(f32)), jax.nn.sigmoid(g_ref[:, d:].astype(f32))
        o_ref[...] = (g0 * a_ref[...].astype(f32) + g1 * b_ref[...].astype(f32)).astype(bf16)

    return pl.pallas_call(
        body, name=name, grid=(s // tm,), in_specs=[_row_spec(tm, d)] * 2 + [_row_spec(tm, 2 * d)],
        out_specs=_row_spec(tm, d), out_shape=jax.ShapeDtypeStruct((s, d), bf16), compiler_params=_params(1),
    )(ba, bb, gl)


def _merge_bwd(dm, ba, bb, gl, name):
    s, d = ba.shape
    tm = _row_tile(s, 512)

    def body(dm_ref, a_ref, b_ref, g_ref, da_ref, db_ref, dg_ref):
        dmv = dm_ref[...].astype(f32)
        g0, g1 = jax.nn.sigmoid(g_ref[:, :d].astype(f32)), jax.nn.sigmoid(g_ref[:, d:].astype(f32))
        da_ref[...] = (dmv * g0).astype(bf16)
        db_ref[...] = (dmv * g1).astype(bf16)
        dg_ref[:, :d] = (dmv * a_ref[...].astype(f32) * (g0 * (1.0 - g0))).astype(bf16)
        dg_ref[:, d:] = (dmv * b_ref[...].astype(f32) * (g1 * (1.0 - g1))).astype(bf16)

    return pl.pallas_call(
        body, name=name, grid=(s // tm,), in_specs=[_row_spec(tm, d)] * 3 + [_row_spec(tm, 2 * d)],
        out_specs=[_row_spec(tm, d)] * 2 + [_row_spec(tm, 2 * d)],
        out_shape=[jax.ShapeDtypeStruct((s, d), bf16)] * 2 + [jax.ShapeDtypeStruct((s, 2 * d), bf16)],
        compiler_params=_params(1),
    )(dm, ba, bb, gl)


GLU_TILE = 256


def _ffn_in_swiglu(h, w_t, name):
    s, d = h.shape
    f = w_t.shape[0] // 2
    tm = _row_tile(s, 2048)
    tg = GLU_TILE
    nb = f // tg

    def body(h_ref, wg_ref, wu_ref, g_ref, u_ref, act_ref):
        hv = h_ref[...]
        g = lax.dot_general(hv, wg_ref[...], _NT, preferred_element_type=f32)
        u = lax.dot_general(hv, wu_ref[...], _NT, preferred_element_type=f32)
        g_ref[...] = g.astype(bf16)
        u_ref[...] = u.astype(bf16)
        act_ref[...] = (g * jax.nn.sigmoid(g) * u).astype(bf16)

    col = pl.BlockSpec((tm, tg), lambda i, j: (i, j))
    return pl.pallas_call(
        body, name=name, grid=(s // tm, nb),
        in_specs=[pl.BlockSpec((tm, d), lambda i, j: (i, 0)), pl.BlockSpec((tg, d), lambda i, j: (j, 0)),
                  pl.BlockSpec((tg, d), lambda i, j: (j + nb, 0))],
        out_specs=[col] * 3, out_shape=[jax.ShapeDtypeStruct((s, f), bf16)] * 3, compiler_params=_params(2),
    )(h, w_t, w_t)


def _ffn_out_dgrad_swiglu(dy, w_out, g, u, name):
    s, d = dy.shape
    f = g.shape[1]
    tm = _row_tile(s, 2048)
    tg = GLU_TILE

    def body(dy_ref, w_ref, g_ref, u_ref, dg_ref, du_ref):
        dv = lax.dot_general(dy_ref[...], w_ref[...], _NT, preferred_element_type=f32)
        gv, uv = g_ref[...].astype(f32), u_ref[...].astype(f32)
        sg = jax.nn.sigmoid(gv)
        dg_ref[...] = (dv * uv * (sg * (1.0 + gv * (1.0 - sg)))).astype(bf16)
        du_ref[...] = (dv * (gv * sg)).astype(bf16)

    col = pl.BlockSpec((tm, tg), lambda i, j: (i, j))
    return pl.pallas_call(
        body, name=name, grid=(s // tm, f // tg),
        in_specs=[pl.BlockSpec((tm, d), lambda i, j: (i, 0)), pl.BlockSpec((tg, d), lambda i, j: (j, 0)), col, col],
        out_specs=[col] * 2, out_shape=[jax.ShapeDtypeStruct((s, f), bf16)] * 2, compiler_params=_params(2),
    )(dy, w_out, g, u)


def _ffn_in_dgrad(dg, du, w_t, name, after=None):
    s, f = dg.shape
    d = w_t.shape[1]
    tm, tn = _matmul_tiles(s, d, 2 * f, dg.dtype.itemsize, w_t.dtype.itemsize, 4)

    def body(dg_ref, du_ref, wg_ref, wu_ref, *rest):
        rest[-1][...] = (jnp.dot(dg_ref[...], wg_ref[...], preferred_element_type=f32)
                         + jnp.dot(du_ref[...], wu_ref[...], preferred_element_type=f32))

    extra = [] if after is None else [after]
    rows = pl.BlockSpec((tm, f), lambda i, j: (i, 0))
    return pl.pallas_call(
        body, name=name, grid=(s // tm, d // tn),
        in_specs=[rows, rows, pl.BlockSpec((f, tn), lambda i, j: (0, j)), pl.BlockSpec((f, tn), lambda i, j: (1, j))]
        + [pl.BlockSpec(memory_space=pl.ANY)] * len(extra),
        out_specs=pl.BlockSpec((tm, tn), lambda i, j: (i, j)),
        out_shape=jax.ShapeDtypeStruct((s, d), f32), compiler_params=_params(2),
    )(dg, du, w_t, w_t, *extra)


def _ada_fwd(c_all, w, b, name):
    def body(c_ref, w_ref, b_ref, o_ref):
        o_ref[...] = jnp.dot(c_ref[...].astype(bf16), w_ref[...].astype(bf16), preferred_element_type=f32) + b_ref[...]

    return pl.pallas_call(
        body, name=name, out_shape=jax.ShapeDtypeStruct((c_all.shape[0], w.shape[1]), f32), compiler_params=_params(),
    )(c_all, w, b)


def _ada_wgrad(c_all, d_all, name):
    n, d = c_all.shape
    w = d_all.shape[1]

    def body(c_ref, d_ref, o_ref):
        eye = (lax.broadcasted_iota(jnp.int32, (n, n), 0) == lax.broadcasted_iota(jnp.int32, (n, n), 1)).astype(f32)
        ct = lax.dot_general(c_ref[...], eye, _TN, precision=lax.Precision.HIGHEST, preferred_element_type=f32)
        g = ct[:, 0:1] * d_ref[0:1, :]
        for bi in range(1, n):
            g = g + ct[:, bi:bi + 1] * d_ref[bi:bi + 1, :]
        o_ref[0] = g

    return pl.pallas_call(
        body, name=name, out_shape=jax.ShapeDtypeStruct((1, d, w), f32), compiler_params=_params(),
    )(c_all, d_all)


def _adamw(parts, w, m, v, name, mine=None):
    r, c = w.shape
    n_parts = parts.shape[0]
    row_tiles = [t for t in range(min(r, 256), 0, -1) if r % t == 0 and (t % 16 == 0 or t == r)]
    if row_tiles:
        tr, tc = row_tiles[0], c
    else:
        tr, tc = r, next(t for t in (256, LANES) if c % t == 0)

    def body(p_ref, *rest):
        own_ref = rest[0] if mine is not None else None
        w_ref, m_ref, v_ref, g_ref, d_ref, nm_ref, nv_ref = rest[-7:]
        if mine is not None:
            x, y, cc = _me()
            me = 4 * x + 2 * y + cc

        def part(i):
            if mine is None:
                return p_ref[i].astype(f32)
            return jnp.where(me == i, own_ref[i], p_ref[i]).astype(f32)

        g = part(0)
        for i in range(1, n_parts):
            g = g + part(i)
        mm = ADAM_B1 * m_ref[...] + (1.0 - ADAM_B1) * g
        vv = ADAM_B2 * v_ref[...] + (1.0 - ADAM_B2) * (g * g)
        m_hat = mm / (1.0 - ADAM_B1 ** ADAM_STEP)
        v_hat = vv / (1.0 - ADAM_B2 ** ADAM_STEP)
        g_ref[...] = g
        d_ref[...] = -ADAM_LR * (m_hat / (jnp.sqrt(v_hat) + ADAM_EPS) + ADAM_WD * w_ref[...])
        nm_ref[...] = mm
        nv_ref[...] = vv

    spec = pl.BlockSpec((tr, tc), lambda i, j: (i, j))
    stack = [parts] if mine is None else [parts, mine]
    return pl.pallas_call(
        body, name=name, grid=(r // tr, c // tc),
        in_specs=[pl.BlockSpec((n_parts, tr, tc), lambda i, j: (0, i, j))] * len(stack) + [spec] * 3,
        out_specs=[spec] * 4, out_shape=[jax.ShapeDtypeStruct((r, c), f32)] * 4, compiler_params=_params(2),
    )(*stack, w, m, v)


def _me():
    return lax.axis_index("x"), lax.axis_index("y"), lax.axis_index("c")


def _all_gather(arrays, name, vmem=False, after=None):
    n = len(arrays)
    space = pltpu.VMEM if vmem else pl.ANY
    extra = [] if after is None else [after]

    def body(*refs):
        ins = refs[:n]
        outs = refs[n + len(extra):2 * n + len(extra)]
        send_sems, recv_sems, local_sems = refs[2 * n + len(extra):]
        x, y, c = _me()
        me, sibling = (x, y, c), (x, y, 1 - c)
        chips = [(1 - x, y), (x, 1 - y), (1 - x, 1 - y)]

        def rows(a, dev):
            return outs[a].at[4 * dev[0] + 2 * dev[1] + dev[2]]

        def copy(a, k, block, to, src=None):
            return pltpu.make_async_remote_copy(
                src_ref=rows(a, block) if src is None else src, dst_ref=rows(a, block),
                send_sem=send_sems.at[a, k], recv_sem=recv_sems.at[a, k], device_id=to, device_id_type=MESH)

        mine = [pltpu.make_async_copy(ins[a], rows(a, me), local_sems.at[a]) for a in range(n)]
        for cp in mine:
            cp.start()
        first = []
        for a in range(n):
            first.append(copy(a, 0, me, sibling, src=ins[a]))
            first += [copy(a, 1 + j, me, (*chip, c), src=ins[a]) for j, chip in enumerate(chips)]
        for cp in first:
            cp.start()
        passed = []
        for j, chip in enumerate(chips):
            for a in range(n):
                copy(a, 1 + j, (*chip, c), me).wait_recv()
                fwd = copy(a, 4 + j, (*chip, c), sibling)
                fwd.start()
                passed.append(fwd)
        for a in range(n):
            copy(a, 0, sibling, me).wait_recv()
            for j, chip in enumerate(chips):
                copy(a, 4 + j, (*chip, 1 - c), me).wait_recv()
        for cp in first + passed:
            cp.wait_send()
        for cp in mine:
            cp.wait()

    outs = pl.pallas_call(
        body, name=name,
        in_specs=[pl.BlockSpec(memory_space=space)] * n + [pl.BlockSpec(memory_space=pl.ANY)] * len(extra),
        out_specs=[pl.BlockSpec(memory_space=space)] * n,
        out_shape=[jax.ShapeDtypeStruct((N_DEV,) + a.shape, a.dtype) for a in arrays],
        scratch_shapes=[pltpu.SemaphoreType.DMA((n, 7)), pltpu.SemaphoreType.DMA((n, 7)), pltpu.SemaphoreType.DMA((n,))],
        compiler_params=pltpu.CompilerParams(vmem_limit_bytes=VMEM_LIMIT),
    )(*arrays, *extra)
    return list(outs)


_FLIPS = ((0, 0, 1), (1, 0, 0), (0, 1, 0), (1, 1, 0), (1, 0, 1), (0, 1, 1), (1, 1, 1))
_HBM = pl.BlockSpec(memory_space=pltpu.HBM)
_SEM = pl.BlockSpec(memory_space=pltpu.SEMAPHORE)


def _exchange_copies(scatter, srcs, lands, send_sems, recv_sems):
    x, y, c = _me()
    me_row = 4 * x + 2 * y + c
    out = []
    for k, (fx, fy, fc) in enumerate(_FLIPS):
        peer = (x ^ fx, y ^ fy, c ^ fc)
        peer_row = 4 * peer[0] + 2 * peer[1] + peer[2]
        for a in range(len(srcs)):
            out.append(pltpu.make_async_remote_copy(
                src_ref=srcs[a].at[peer_row] if scatter else srcs[a], dst_ref=lands[a].at[me_row],
                send_sem=send_sems.at[7 * a + k], recv_sem=recv_sems.at[7 * a + k], device_id=peer, device_id_type=MESH))
    return out


def _exchange_start(arrays, scatter, name):
    n = len(arrays)
    lands = [lax.empty(a.shape if scatter else (N_DEV,) + a.shape, a.dtype) for a in arrays]

    def body(*refs):
        srcs, zones = refs[:n], refs[n:2 * n]
        send_sems, recv_sems = refs[2 * n], refs[2 * n + 1]
        token = refs[-1]
        for cp in _exchange_copies(scatter, srcs, zones, send_sems, recv_sems):
            cp.start()
        token[...] = jnp.zeros_like(token)

    thru = [pltpu.HBM(a.shape, a.dtype) for a in list(arrays) + lands]
    outs = pl.pallas_call(
        body, name=name,
        out_shape=(pltpu.SemaphoreType.DMA((7 * n,)), pltpu.SemaphoreType.DMA((7 * n,)), *thru, jax.ShapeDtypeStruct((8, LANES), f32)),
        in_specs=[_HBM] * (2 * n), out_specs=(_SEM, _SEM, *[_HBM] * (2 * n), pl.BlockSpec(memory_space=pltpu.VMEM)),
        input_output_aliases={i: 2 + i for i in range(2 * n)},
        compiler_params=pltpu.CompilerParams(has_side_effects=pltpu.SideEffectType.DATAFLOW_SIDE_EFFECTING),
    )(*[pltpu.with_memory_space_constraint(a, pltpu.HBM) for a in list(arrays) + lands])
    return dict(n=n, scatter=scatter, sems=outs[:2], srcs=outs[2:2 + n], lands=outs[2 + n:2 + 2 * n], token=outs[-1])


def _exchange_wait(handle, after, name):
    n, scatter = handle["n"], handle["scatter"]

    def body(*refs):
        srcs, zones = refs[:n], refs[n:2 * n]
        send_sems, recv_sems = refs[2 * n], refs[2 * n + 1]
        for cp in _exchange_copies(scatter, srcs, zones, send_sems, recv_sems):
            cp.wait_send()
            cp.wait_recv()

    thru = [pltpu.HBM(a.shape, a.dtype) for a in list(handle["srcs"]) + list(handle["lands"])]
    outs = pl.pallas_call(
        body, name=name, out_shape=tuple(thru),
        in_specs=[_HBM] * (2 * n) + [_SEM, _SEM, pl.BlockSpec(memory_space=pl.ANY)], out_specs=tuple([_HBM] * (2 * n)),
        input_output_aliases={i: i for i in range(2 * n)},
        compiler_params=pltpu.CompilerParams(has_side_effects=pltpu.SideEffectType.DATAFLOW_SIDE_EFFECTING),
    )(*handle["srcs"], *handle["lands"], *handle["sems"], after)
    return list(outs[n:])


def _cols_from_shards(g):
    return jnp.transpose(g, (1, 0, 2)).reshape(g.shape[1], -1)


def _shards_from_cols(a):
    return jnp.transpose(a.reshape(a.shape[0], N_DEV, -1), (1, 0, 2))


def _local_step(x, positions, ada, g_pre_mix, g_post_mix, b_f, sinks, g_pre_ffn, g_post_ffn, target,
                w_in_t, late_weights, on_grads):
    s, d = x.shape
    row = lambda v: v.reshape(1, -1)
    shift_m, scale_m, gate_m, shift_f, scale_f, gate_f = (ada[i:i + 1] for i in range(6))
    w_gate_t, w_qkv_t = w_in_t[F_OFF + N_HEADS:], w_in_t[:QKV_W]
    w_f_t = jnp.pad(w_in_t[F_OFF:F_OFF + N_HEADS], ((0, LANES - N_HEADS), (0, 0)))
    w_in_p_t = jnp.concatenate([w_gate_t, w_qkv_t, w_f_t], axis=0)
    bf_row = jnp.pad(row(b_f), ((0, 0), (0, LANES - N_HEADS)))
    sink_rows = jnp.broadcast_to(sinks.reshape(N_HEADS, 1).astype(f32), (N_HEADS, LANES))
    inv_freq = 1.0 / (ROPE_THETA ** (jnp.arange(0, HEAD_DIM, 2, dtype=f32) / HEAD_DIM))
    cos, sin_s = _rope_tables(positions.reshape(s, 1), jnp.tile(inv_freq, 4).reshape(1, LANES), "rope_tables")

    h1 = _prenorm(x, row(g_pre_mix), scale_m, shift_m, "prenorm_mix")
    gl = _matmul(h1, w_gate_t, "nt", bf16, "proj_gate")
    qkv = _matmul(h1, w_qkv_t, "nt", f32, "proj_qkv")
    fl = _matmul(h1, w_f_t, "nt", f32, "proj_forget")
    qa, ka, va, qb, kb, vb = _qkv_prep(qkv, cos, sin_s, "qkv_prep")
    cum_b = _forget_prep(fl, bf_row, "forget_prep")
    o_a, lse_a = _attn_fwd(qa, ka, va, "swa_fwd", sink_rows=sink_rows, window=WINDOW, t=512)
    o_b, lse_b = _attn_fwd(qb, kb, vb, "fox_fwd", cum_b=cum_b, t=512)
    w_branch_a, w_branch_b, w_out, w_ffn_in_t, w_ffn_out = late_weights(o_b)
    ba = _matmul(o_a, w_branch_a, "nn", bf16, "branch_a")
    bb = _matmul(o_b, w_branch_b, "nn", bf16, "branch_b")
    merged = _merge(ba, bb, gl, "merge")
    y1 = _matmul(merged, w_out, "nn", f32, "out_proj")
    x2 = _postnorm_res(x, y1, row(g_post_mix), gate_m, "postnorm_mix")

    h2 = _prenorm(x2, row(g_pre_ffn), scale_f, shift_f, "prenorm_ffn")
    g_ff, u_ff, act = _ffn_in_swiglu(h2, w_ffn_in_t, "ffn_in_swiglu")
    y2 = _matmul(act, w_ffn_out, "nn", f32, "ffn_out")
    loss_row, d_out, d_y2, vec_pf = _loss_tail(x2, y2, row(g_post_ffn), gate_f, target, "loss_tail")

    g_w_ffn_out = _matmul(act, d_y2, "tn", bf16, "ffn_out_wgrad")
    dg_ff, du_ff = _ffn_out_dgrad_swiglu(d_y2, w_ffn_out, g_ff, u_ff, "ffn_out_dgrad_swiglu")
    g_w_ffn_in_t = jnp.concatenate([_matmul(dg_ff, h2, "tn", bf16, "ffn_gate_wgrad"),
                                    _matmul(du_ff, h2, "tn", bf16, "ffn_up_wgrad")], axis=0)
    sent = on_grads(dict(w_ffn_in=g_w_ffn_in_t, w_ffn_out=g_w_ffn_out))
    d_h2 = _ffn_in_dgrad(dg_ff, du_ff, w_ffn_in_t, "ffn_in_dgrad", after=sent)
    d_x2, vec_nf = _prenorm_bwd(d_h2, x2, row(g_pre_ffn), scale_f, d_out, "prenorm_ffn_bwd")

    d_y1, vec_pm = _postnorm_bwd(d_x2, y1, row(g_post_mix), gate_m, "postnorm_mix_bwd")
    g_w_out = _matmul(merged, d_y1, "tn", bf16, "out_proj_wgrad")
    d_merged = _matmul(d_y1, w_out, "nt", bf16, "out_proj_dgrad")
    d_ba, d_bb, dgl = _merge_bwd(d_merged, ba, bb, gl, "merge_bwd")
    g_w_branch_a = _matmul(o_a, d_ba, "tn", bf16, "branch_a_wgrad")
    g_w_branch_b = _matmul(o_b, d_bb, "tn", bf16, "branch_b_wgrad")
    sent = on_grads(dict(w_out=g_w_out, w_branch_a=g_w_branch_a, w_branch_b=g_w_branch_b))
    d_oa = _matmul(d_ba, w_branch_a, "nt", bf16, "branch_a_dgrad", after=sent)
    d_ob = _matmul(d_bb, w_branch_b, "nt", bf16, "branch_b_dgrad", after=sent)
    delta_a, d_sink = _attn_delta(d_oa, o_a, "swa_delta", lse=lse_a, sink_rows=sink_rows)
    delta_b, = _attn_delta(d_ob, o_b, "fox_delta")
    dqa_t, dka, dva = _attn_bwd(qa, ka, va, d_oa, lse_a, delta_a, "swa_bwd", window=WINDOW, t=512)
    dqb_t, dkb, dvb, dcs, rs = _attn_bwd(qb, kb, vb, d_ob, lse_b, delta_b, "fox_bwd", cum_b=cum_b, t=512)
    dqkv = _qkv_prep_bwd(dqa_t, dka, dva, dqb_t, dkb, dvb, cos, sin_s, "qkv_prep_bwd")
    dfl, vec_bf = _forget_prep_bwd(rs.reshape(N_HEADS, s), dcs, fl, bf_row, "forget_prep_bwd")
    dproj = jnp.concatenate([dgl, dqkv, dfl], axis=1)
    g_w_in_p_t = _matmul(dproj, h1, "tn", bf16, "in_proj_wgrad")
    g_w_in_t = jnp.concatenate([g_w_in_p_t[GATE_W:GATE_W + QKV_W], g_w_in_p_t[GATE_W + QKV_W:GATE_W + QKV_W + N_HEADS],
                                g_w_in_p_t[:GATE_W]], axis=0)
    sent = on_grads(dict(w_in=g_w_in_t))
    d_h1 = _matmul(dproj, w_in_p_t, "nn", f32, "in_proj_dgrad", after=sent)
    grad_x, vec_nm = _prenorm_bwd(d_h1, x, row(g_pre_mix), scale_m, d_x2, "prenorm_mix_bwd")

    d_ada = jnp.concatenate([vec_nm[0], vec_nm[1], vec_pm[0], vec_nf[0], vec_nf[1], vec_pf[0]])
    small = dict(b_ada=d_ada, g_pre_mix=vec_nm[2], g_post_mix=vec_pm[1], g_pre_ffn=vec_nf[2], g_post_ffn=vec_pf[1],
                 b_f=vec_bf[0, :N_HEADS], sinks=d_sink[:, 0], loss=loss_row[0, :1])
    return grad_x, small


_SMALL = (("b_ada", 6144), ("g_pre_mix", 1024), ("g_post_mix", 1024), ("g_pre_ffn", 1024), ("g_post_ffn", 1024),
          ("b_f", 128), ("sinks", 128), ("loss", 128))
_SMALL_ROWS = 88


def _pack_small(vals):
    parts = [jnp.pad(vals[k].reshape(-1).astype(f32), (0, n - vals[k].size)) for k, n in _SMALL]
    flat = jnp.concatenate(parts)
    return jnp.pad(flat, (0, _SMALL_ROWS * LANES - flat.size)).reshape(_SMALL_ROWS, LANES)


def _unpack_small(slab, shapes):
    flat, out, off = slab.reshape(-1), {}, 0
    for k, n in _SMALL:
        size = math.prod(shapes[k])
        out[k] = flat[off:off + size].reshape(shapes[k])
        off += n
    return out


def kernel(x, c, positions, w_ada, b_ada, g_pre_mix, g_post_mix, w_in, b_f, sinks, w_branch_a, w_branch_b, w_out, g_pre_ffn, g_post_ffn, w_ffn_in, w_ffn_out, loss_target, m_w_ada, m_b_ada, m_g_pre_mix, m_g_post_mix, m_w_in, m_b_f, m_sinks, m_w_branch_a, m_w_branch_b, m_w_out, m_g_pre_ffn, m_g_post_ffn, m_w_ffn_in, m_w_ffn_out, v_w_ada, v_b_ada, v_g_pre_mix, v_g_post_mix, v_w_in, v_b_f, v_sinks, v_w_branch_a, v_w_branch_b, v_w_out, v_g_pre_ffn, v_g_post_ffn, v_w_ffn_in, v_w_ffn_out):
    xi, yi, ci = _me()
    me = 4 * xi + 2 * yi + ci
    d = D_MODEL
    ada_w = w_ada.shape[2]

    c_all, = _all_gather([c], "gather_c", vmem=True)
    c_all = c_all.reshape(N_DEV, d)
    b_mine = lax.dynamic_slice(b_ada, (0, me * ada_w), (1, ada_w))
    ada_cols = _ada_fwd(c_all, w_ada[0], b_mine, "ada_fwd")
    ada_all, = _all_gather([ada_cols], "gather_ada", vmem=True)
    ada = lax.dynamic_index_in_dim(ada_all, me, axis=1, keepdims=False).reshape(6, d)

    transposed = ("w_in", "w_ffn_in")
    tr = lambda a: jnp.transpose(a[0])

    g_in, = _all_gather([tr(w_in).astype(bf16)], "gather_w_in")
    late = [w.astype(bf16) for w in (w_branch_a[0], w_branch_b[0], w_out[0], tr(w_ffn_in), w_ffn_out[0])]
    late_h = _exchange_start(late, False, "gather_late_start")

    def mine_into(zone, block):
        return lax.dynamic_update_index_in_dim(zone, block, me, 0)

    def rows_from_shards(g):
        return g.reshape(g.shape[0] * g.shape[1], g.shape[2])

    def late_weights(after):
        zones = _exchange_wait(late_h, after, "gather_late_wait")
        g_ba, g_bb, g_out, g_fi, g_fo = (mine_into(z, w) for z, w in zip(zones, late))
        return (_cols_from_shards(g_ba), _cols_from_shards(g_bb), rows_from_shards(g_out), rows_from_shards(g_fi),
                rows_from_shards(g_fo))

    row_sharded = ("w_out", "w_ffn_out") + transposed
    in_flight = []

    def on_grads(group):
        sends = [g.reshape(N_DEV, g.shape[0] // N_DEV, g.shape[1]) if nm in row_sharded else _shards_from_cols(g)
                 for nm, g in group.items()]
        handle = _exchange_start(sends, True, "scatter_start_%d" % len(in_flight))
        in_flight.append((list(group), sends, handle))
        return handle["token"]

    grad_x, small = _local_step(
        x[0], positions[0], ada + late_h["token"][0, 0], g_pre_mix[0], g_post_mix[0], b_f[0], sinks[0], g_pre_ffn[0],
        g_post_ffn[0], loss_target[0], rows_from_shards(g_in), late_weights, on_grads)

    ws = dict(w_in=(w_in, m_w_in, v_w_in), w_branch_a=(w_branch_a, m_w_branch_a, v_w_branch_a),
              w_branch_b=(w_branch_b, m_w_branch_b, v_w_branch_b), w_out=(w_out, m_w_out, v_w_out),
              w_ffn_in=(w_ffn_in, m_w_ffn_in, v_w_ffn_in), w_ffn_out=(w_ffn_out, m_w_ffn_out, v_w_ffn_out))
    res = {}

    def finish_group(gi, after):
        names, sends, handle = in_flight[gi]
        zones = _exchange_wait(handle, after, "scatter_wait_%d" % gi)
        for nm, zone, sent in zip(names, zones, sends):
            w, m, v = (tr(a) if nm in transposed else a[0] for a in ws[nm])
            out = _adamw(zone, w, m, v, "adamw_" + nm, mine=sent)
            after = out[0]
            res[nm] = [jnp.transpose(o) for o in out] if nm in transposed else out
        return after

    done = finish_group(1, finish_group(0, grad_x))

    slab_all, = _all_gather([_pack_small(small)], "gather_small", vmem=True, after=done)
    small_w = dict(b_ada=b_ada, g_pre_mix=g_pre_mix, g_post_mix=g_post_mix, g_pre_ffn=g_pre_ffn, g_post_ffn=g_post_ffn,
                   b_f=b_f, sinks=sinks, loss=jnp.zeros((1,), f32))
    small_m = dict(b_ada=m_b_ada, g_pre_mix=m_g_pre_mix, g_post_mix=m_g_post_mix, g_pre_ffn=m_g_pre_ffn,
                   g_post_ffn=m_g_post_ffn, b_f=m_b_f, sinks=m_sinks, loss=jnp.zeros((1,), f32))
    small_v = dict(b_ada=v_b_ada, g_pre_mix=v_g_pre_mix, g_post_mix=v_g_post_mix, g_pre_ffn=v_g_pre_ffn,
                   g_post_ffn=v_g_post_ffn, b_f=v_b_f, sinks=v_sinks, loss=jnp.ones((1,), f32))
    shapes = {k: small_w[k].shape for k, _ in _SMALL}
    s_out = _adamw(slab_all, _pack_small(small_w), _pack_small(small_m), _pack_small(small_v), "adamw_small")
    s_grad, s_delta, s_m, s_v = (_unpack_small(o, shapes) for o in s_out)

    d_ada_all = lax.dynamic_slice(slab_all[:, :6144 // LANES, :].reshape(N_DEV, 6144), (0, me * ada_w), (N_DEV, ada_w))
    ada_parts = _ada_wgrad(c_all, d_ada_all, "ada_wgrad")

    res["w_ada"] = _adamw(ada_parts, w_ada[0], m_w_ada[0], v_w_ada[0], "adamw_w_ada")
    finish_group(2, res["w_ada"][0])

    order = ["w_ada", "b_ada", "g_pre_mix", "g_post_mix", "w_in", "b_f", "sinks", "w_branch_a", "w_branch_b", "w_out",
             "g_pre_ffn", "g_post_ffn", "w_ffn_in", "w_ffn_out"]
    outs = [s_grad["loss"].reshape(()), grad_x[None]]
    for which, small_o in enumerate((s_grad, s_delta, s_m, s_v)):
        for nm in order:
            outs.append(res[nm][which][None] if nm in res else small_o[nm])
    return tuple(outs)
```

```python
import functools
import math

import jax
import jax.numpy as jnp
from jax import lax
from jax.experimental import pallas as pl
from jax.experimental.pallas import tpu as pltpu

f32 = jnp.float32
bf16 = jnp.bfloat16

D_MODEL = 1024
HEAD_DIM = 64
N_HEADS = 8
N_PAIRS = 4
QKV_W = 2304
GATE_W = 2048
F_OFF = 2304
IN_W = 4360
WINDOW = 128
ROPE_THETA = 10000.0
RMS_EPS = 1e-6
D_FF = 2816
N_DEV = 8
ADAM_LR, ADAM_B1, ADAM_B2, ADAM_EPS, ADAM_WD, ADAM_STEP = 0.001, 0.9, 0.999, 1e-08, 0.01, 10
NEG = -1e30
LANES = 128
VMEM_LIMIT = 48 * 1024 * 1024
MESH = pl.DeviceIdType.MESH

_NT = (((1,), (1,)), ((), ()))
_TN = (((0,), (0,)), ((), ()))


def _params(n_grid=0):
    sem = ("arbitrary",) * n_grid if n_grid else None
    return pltpu.CompilerParams(dimension_semantics=sem, vmem_limit_bytes=VMEM_LIMIT)


def _row_tile(s, want):
    t = min(s, want)
    assert s % t == 0, (s, t)
    return t


MATMUL_VMEM_BUDGET = 40 * 1024 * 1024


def _matmul_tiles(m, n, k, a_item, b_item, o_item):
    def tiles(d):
        return [t for t in range(LANES, min(d, 2048) + 1, LANES) if d % t == 0] or [d]

    best = None
    for tm in tiles(m):
        for tn in tiles(n):
            vmem = 2 * (tm * k * a_item + tn * k * b_item + tm * tn * o_item) + tm * tn * 4
            if vmem > MATMUL_VMEM_BUDGET:
                continue
            traffic = m * k * a_item + n * k * b_item * (1 if tn == n else m // tm) + m * n * o_item
            steps = (m // tm) * (n // tn)
            key = (traffic, 0, steps) if steps >= 4 else (traffic, 1, -steps)
            if best is None or key < best[0]:
                best = (key, tm, tn)
    assert best is not None, (m, n, k)
    return best[1], best[2]


def _matmul(a, b, mode, out_dtype, name, after=None):
    if mode == "nn":
        (m, k), n = a.shape, b.shape[1]
    elif mode == "nt":
        (m, k), n = a.shape, b.shape[0]
    else:
        (k, m), n = a.shape, b.shape[1]
    tm, tn = _matmul_tiles(m, n, k, a.dtype.itemsize, b.dtype.itemsize, jnp.dtype(out_dtype).itemsize)
    if mode == "nn":
        a_spec, b_spec, dims = pl.BlockSpec((tm, k), lambda i, j: (i, 0)), pl.BlockSpec((k, tn), lambda i, j: (0, j)), None
    elif mode == "nt":
        a_spec, b_spec, dims = pl.BlockSpec((tm, k), lambda i, j: (i, 0)), pl.BlockSpec((tn, k), lambda i, j: (j, 0)), _NT
    else:
        a_spec, b_spec, dims = pl.BlockSpec((k, tm), lambda i, j: (0, i)), pl.BlockSpec((k, tn), lambda i, j: (0, j)), _TN

    def body(a_ref, b_ref, *rest):
        o_ref = rest[-1]
        av, bv = a_ref[...].astype(bf16), b_ref[...].astype(bf16)
        if dims is None:
            r = jnp.dot(av, bv, preferred_element_type=f32)
        else:
            r = lax.dot_general(av, bv, dims, preferred_element_type=f32)
        o_ref[...] = r.astype(out_dtype)

    extra = [] if after is None else [after]
    return pl.pallas_call(
        body, name=name, grid=(m // tm, n // tn), in_specs=[a_spec, b_spec] + [pl.BlockSpec(memory_space=pl.ANY)] * len(extra),
        out_specs=pl.BlockSpec((tm, tn), lambda i, j: (i, j)),
        out_shape=jax.ShapeDtypeStruct((m, n), out_dtype), compiler_params=_params(2),
    )(a, b, *extra)


def _rstd(v):
    return lax.rsqrt(jnp.mean(v * v, axis=-1, keepdims=True) + RMS_EPS)


def _row_spec(tm, d):
    return pl.BlockSpec((tm, d), lambda i: (i, 0))


def _vec_spec(d, rows=1):
    return pl.BlockSpec((rows, d), lambda i: (0, 0))


def _prenorm(x, g, scale, shift, name):
    s, d = x.shape
    tm = _row_tile(s, 512)

    def body(x_ref, g_ref, sc_ref, sh_ref, h_ref):
        xv = x_ref[...]
        h = (xv * _rstd(xv) * g_ref[...]) * (1.0 + sc_ref[...]) + sh_ref[...]
        h_ref[...] = h.astype(bf16)

    return pl.pallas_call(
        body, name=name, grid=(s // tm,), in_specs=[_row_spec(tm, d)] + [_vec_spec(d)] * 3,
        out_specs=_row_spec(tm, d), out_shape=jax.ShapeDtypeStruct((s, d), bf16), compiler_params=_params(1),
    )(x, g, scale, shift)


def _postnorm_res(x, y, g, gate, name):
    s, d = x.shape
    tm = _row_tile(s, 512)

    def body(x_ref, y_ref, g_ref, gate_ref, o_ref):
        yv = y_ref[...]
        o_ref[...] = x_ref[...] + gate_ref[...] * (yv * _rstd(yv) * g_ref[...])

    return pl.pallas_call(
        body, name=name, grid=(s // tm,), in_specs=[_row_spec(tm, d)] * 2 + [_vec_spec(d)] * 2,
        out_specs=_row_spec(tm, d), out_shape=jax.ShapeDtypeStruct((s, d), f32), compiler_params=_params(1),
    )(x, y, g, gate)


def _rms_bwd(u, v, r):
    return r * u - v * (r * r * r) * jnp.mean(u * v, axis=-1, keepdims=True)


def _loss_tail(x, y, g, gate, target, name):
    s, d = x.shape
    tm = _row_tile(s, 512)

    def body(x_ref, y_ref, g_ref, gate_ref, t_ref, loss_ref, do_ref, dy_ref, vec_ref):
        @pl.when(pl.program_id(0) == 0)
        def _():
            loss_ref[...] = jnp.zeros_like(loss_ref)
            vec_ref[...] = jnp.zeros_like(vec_ref)
        yv = y_ref[...]
        r = _rstd(yv)
        yn = yv * r
        err = x_ref[...] + gate_ref[...] * (yn * g_ref[...]) - t_ref[...]
        loss_ref[...] += 0.5 * jnp.sum(jnp.mean(err * err, axis=-1, keepdims=True), axis=0, keepdims=True)
        dr = err / d
        do_ref[...] = dr
        dn = dr * gate_ref[...]
        vec_ref[0:1, :] += jnp.sum(dr * (yn * g_ref[...]), axis=0, keepdims=True)
        vec_ref[1:2, :] += jnp.sum(dn * yn, axis=0, keepdims=True)
        dy_ref[...] = _rms_bwd(dn * g_ref[...], yv, r).astype(bf16)

    return pl.pallas_call(
        body, name=name, grid=(s // tm,), in_specs=[_row_spec(tm, d)] * 2 + [_vec_spec(d)] * 2 + [_row_spec(tm, d)],
        out_specs=[_vec_spec(LANES), _row_spec(tm, d), _row_spec(tm, d), _vec_spec(d, 8)],
        out_shape=[jax.ShapeDtypeStruct((1, LANES), f32), jax.ShapeDtypeStruct((s, d), f32),
                   jax.ShapeDtypeStruct((s, d), bf16), jax.ShapeDtypeStruct((8, d), f32)],
        compiler_params=_params(1),
    )(x, y, g, gate, target)


def _postnorm_bwd(dres, y, g, gate, name):
    s, d = y.shape
    tm = _row_tile(s, 512)

    def body(dr_ref, y_ref, g_ref, gate_ref, dy_ref, vec_ref):
        @pl.when(pl.program_id(0) == 0)
        def _():
            vec_ref[...] = jnp.zeros_like(vec_ref)
        dr, yv = dr_ref[...], y_ref[...]
        r = _rstd(yv)
        yn = yv * r
        dn = dr * gate_ref[...]
        vec_ref[0:1, :] += jnp.sum(dr * (yn * g_ref[...]), axis=0, keepdims=True)
        vec_ref[1:2, :] += jnp.sum(dn * yn, axis=0, keepdims=True)
        dy_ref[...] = _rms_bwd(dn * g_ref[...], yv, r).astype(bf16)

    return pl.pallas_call(
        body, name=name, grid=(s // tm,), in_specs=[_row_spec(tm, d)] * 2 + [_vec_spec(d)] * 2,
        out_specs=[_row_spec(tm, d), _vec_spec(d, 8)],
        out_shape=[jax.ShapeDtypeStruct((s, d), bf16), jax.ShapeDtypeStruct((8, d), f32)], compiler_params=_params(1),
    )(dres, y, g, gate)


def _prenorm_bwd(dh, x, g, scale, dres, name):
    s, d = x.shape
    tm = _row_tile(s, 512)

    def body(dh_ref, x_ref, g_ref, sc_ref, dr_ref, dx_ref, vec_ref):
        @pl.when(pl.program_id(0) == 0)
        def _():
            vec_ref[...] = jnp.zeros_like(vec_ref)
        dhv, xv = dh_ref[...], x_ref[...]
        r = _rstd(xv)
        xn = xv * r
        dn = dhv * (1.0 + sc_ref[...])
        vec_ref[0:1, :] += jnp.sum(dhv, axis=0, keepdims=True)
        vec_ref[1:2, :] += jnp.sum(dhv * (xn * g_ref[...]), axis=0, keepdims=True)
        vec_ref[2:3, :] += jnp.sum(dn * xn, axis=0, keepdims=True)
        dx_ref[...] = dr_ref[...] + _rms_bwd(dn * g_ref[...], xv, r)

    return pl.pallas_call(
        body, name=name, grid=(s // tm,),
        in_specs=[_row_spec(tm, d)] * 2 + [_vec_spec(d)] * 2 + [_row_spec(tm, d)],
        out_specs=[_row_spec(tm, d), _vec_spec(d, 8)],
        out_shape=[jax.ShapeDtypeStruct((s, d), f32), jax.ShapeDtypeStruct((8, d), f32)], compiler_params=_params(1),
    )(dh, x, g, scale, dres)


def _lane():
    return lax.broadcasted_iota(jnp.int32, (1, LANES), 1)


def _rope_tables(pos_col, inv_freq, name):
    s = pos_col.shape[0]

    def body(p_ref, f_ref, cos_ref, sin_ref):
        ang = p_ref[...].astype(f32) * f_ref[...]
        first_half = (_lane() % HEAD_DIM) < HEAD_DIM // 2
        cos_ref[...] = jnp.cos(ang)
        sn = jnp.sin(ang)
        sin_ref[...] = jnp.where(first_half, -sn, sn)

    return pl.pallas_call(
        body, name=name, out_shape=[jax.ShapeDtypeStruct((s, LANES), f32)] * 2, compiler_params=_params(),
    )(pos_col, inv_freq)


def _swap_halves(v):
    first_half = (_lane() % HEAD_DIM) < HEAD_DIM // 2
    return jnp.where(first_half, pltpu.roll(v, LANES - HEAD_DIM // 2, axis=1), pltpu.roll(v, HEAD_DIM // 2, axis=1))


def _qkv_prep(qkv, cos, sin_s, name):
    s = qkv.shape[0]
    tm = _row_tile(s, 256)
    scale = 1.0 / math.sqrt(HEAD_DIM)

    def body(p_ref, c_ref, s_ref, qa_ref, ka_ref, va_ref, qb_ref, kb_ref, vb_ref):
        cs, sn = c_ref[...], s_ref[...]
        low = _lane() < HEAD_DIM

        def blk(j):
            return p_ref[:, j * LANES:(j + 1) * LANES]

        def rope(v):
            return v * cs + _swap_halves(v) * sn

        def expand(v):
            other = pltpu.roll(v, HEAD_DIM, axis=1)
            return jnp.where(low, v, other), jnp.where(low, other, v)

        for j in range(N_PAIRS):
            qa_ref[:, j * LANES:(j + 1) * LANES] = (rope(blk(j)) * scale).astype(bf16)
            qb_ref[:, j * LANES:(j + 1) * LANES] = (blk(6 + j) * scale).astype(bf16)
            kb_ref[:, j * LANES:(j + 1) * LANES] = blk(10 + j).astype(bf16)
            vb_ref[:, j * LANES:(j + 1) * LANES] = blk(14 + j).astype(bf16)
        k0, k1 = expand(rope(blk(4)))
        v0, v1 = expand(blk(5))
        for j in range(N_PAIRS):
            ka_ref[:, j * LANES:(j + 1) * LANES] = (k0 if j < 2 else k1).astype(bf16)
            va_ref[:, j * LANES:(j + 1) * LANES] = (v0 if j < 2 else v1).astype(bf16)

    hw = N_PAIRS * LANES
    return pl.pallas_call(
        body, name=name, grid=(s // tm,),
        in_specs=[_row_spec(tm, QKV_W), _row_spec(tm, LANES), _row_spec(tm, LANES)],
        out_specs=[_row_spec(tm, hw)] * 6, out_shape=[jax.ShapeDtypeStruct((s, hw), bf16)] * 6, compiler_params=_params(1),
    )(qkv, cos, sin_s)


def _qkv_prep_bwd(dqa_t, dka, dva, dqb_t, dkb, dvb, cos, sin_s, name):
    s = dka.shape[0]
    tm = _row_tile(s, 256)
    scale = 1.0 / math.sqrt(HEAD_DIM)
    hw = N_PAIRS * LANES
    t_spec = pl.BlockSpec((hw, tm), lambda i: (0, i))

    def body(dqa_ref, dka_ref, dva_ref, dqb_ref, dkb_ref, dvb_ref, c_ref, s_ref, o_ref):
        cs, sn = c_ref[...], s_ref[...]
        low = _lane() < HEAD_DIM

        def blk(ref, j):
            return ref[:, j * LANES:(j + 1) * LANES]

        def blk_t(ref, j):
            return ref[j * LANES:(j + 1) * LANES, :].T

        def unrope(v):
            return v * cs + _swap_halves(v * sn)

        def fold(ref):
            a, b = blk(ref, 0) + blk(ref, 1), blk(ref, 2) + blk(ref, 3)
            kv0 = a + pltpu.roll(a, HEAD_DIM, axis=1)
            kv1 = b + pltpu.roll(b, HEAD_DIM, axis=1)
            return jnp.where(low, kv0, kv1)

        for j in range(N_PAIRS):
            o_ref[:, j * LANES:(j + 1) * LANES] = (unrope(blk_t(dqa_ref, j)) * scale).astype(bf16)
            o_ref[:, (6 + j) * LANES:(7 + j) * LANES] = (blk_t(dqb_ref, j) * scale).astype(bf16)
            o_ref[:, (10 + j) * LANES:(11 + j) * LANES] = blk(dkb_ref, j).astype(bf16)
            o_ref[:, (14 + j) * LANES:(15 + j) * LANES] = blk(dvb_ref, j).astype(bf16)
        o_ref[:, 4 * LANES:5 * LANES] = unrope(fold(dka_ref)).astype(bf16)
        o_ref[:, 5 * LANES:6 * LANES] = fold(dva_ref).astype(bf16)

    return pl.pallas_call(
        body, name=name, grid=(s // tm,),
        in_specs=[t_spec, _row_spec(tm, hw), _row_spec(tm, hw), t_spec, _row_spec(tm, hw), _row_spec(tm, hw)] + [_row_spec(tm, LANES)] * 2,
        out_specs=_row_spec(tm, QKV_W), out_shape=jax.ShapeDtypeStruct((s, QKV_W), bf16), compiler_params=_params(1),
    )(dqa_t, dka, dva, dqb_t, dkb, dvb, cos, sin_s)


def _cumsum_rows(v, reverse=False):
    n = v.shape[0]
    row = lax.broadcasted_iota(jnp.int32, v.shape, 0)
    sh = 1
    while sh < n:
        if reverse:
            v = v + jnp.where(row < n - sh, pltpu.roll(v, n - sh, axis=0), 0.0)
        else:
            v = v + jnp.where(row >= sh, pltpu.roll(v, sh, axis=0), 0.0)
        sh *= 2
    return v


def _log_sigmoid(z):
    return jnp.minimum(z, 0.0) - jnp.log1p(jnp.exp(-jnp.abs(z)))


def _forget_prep(fl, bf_row, name):
    s = fl.shape[0]

    def body(f_ref, b_ref, cb_ref):
        cum = _cumsum_rows(_log_sigmoid(f_ref[...] + b_ref[...]))
        for h in range(N_HEADS):
            cb_ref[:, h * LANES:(h + 1) * LANES] = jnp.broadcast_to(cum[:, h:h + 1], (s, LANES))

    return pl.pallas_call(
        body, name=name, out_shape=jax.ShapeDtypeStruct((s, N_HEADS * LANES), f32), compiler_params=_params(),
    )(fl, bf_row)


def _forget_prep_bwd(rs, dcs, fl, bf_row, name):
    s = fl.shape[0]

    def body(r_ref, c_ref, f_ref, b_ref, df_ref, db_ref):
        eye = (lax.broadcasted_iota(jnp.int32, (N_HEADS, LANES), 0) == lax.broadcasted_iota(jnp.int32, (N_HEADS, LANES), 1)).astype(f32)
        dcum = lax.dot_general(r_ref[...], eye, _TN, precision=lax.Precision.HIGHEST, preferred_element_type=f32)
        for h in range(N_HEADS):
            dcum = dcum - jnp.where(_lane() == h, jnp.sum(c_ref[:, h * LANES:(h + 1) * LANES], axis=1, keepdims=True), 0.0)
        dlf = _cumsum_rows(dcum, reverse=True)
        z = f_ref[...] + b_ref[...]
        df = jnp.where(_lane() < N_HEADS, dlf * jax.nn.sigmoid(-z), 0.0)
        df_ref[...] = df.astype(bf16)
        db_ref[...] = jnp.zeros_like(db_ref)
        db_ref[0:1, :] = jnp.sum(df, axis=0, keepdims=True)

    return pl.pallas_call(
        body, name=name,
        out_shape=[jax.ShapeDtypeStruct((s, LANES), bf16), jax.ShapeDtypeStruct((8, LANES), f32)], compiler_params=_params(),
    )(rs, dcs, fl, bf_row)


def _tile_mask(n_keys, n_queries, off, window):
    shape = (n_keys, n_queries)
    d = lax.broadcasted_iota(jnp.int32, shape, 1) - lax.broadcasted_iota(jnp.int32, shape, 0) + off
    valid = d >= 0
    return jnp.logical_and(valid, d < window) if window else valid


def _wide(v, t):
    return jnp.concatenate([v] * (t // LANES), axis=1)


def _attn_fwd(q, k, v, name, *, cum_b=None, sink_rows=None, window=None, t=256):
    s = q.shape[0]
    t = _row_tile(s, t)
    fox, has_sink = cum_b is not None, sink_rows is not None
    assert not window or (window % LANES == 0 and LANES + window <= s)

    def body(*refs):
        q_ref, k_ref, v_ref = refs[:3]
        rest = list(refs[3:])
        cb_ref = rest.pop(0) if fox else None
        sink_ref = rest.pop(0) if has_sink else None
        o_ref, lse_ref = rest
        i = pl.program_id(1)
        low = _lane() < HEAD_DIM
        top = lax.broadcasted_iota(jnp.int32, (LANES, 1), 0) < HEAD_DIM
        q2 = q_ref[...]
        zero = jnp.zeros_like(q2)
        qms = (jnp.where(low, q2, zero), jnp.where(low, zero, q2))

        def tile(k0, n_keys, off, carry, masked, queries=slice(0, t)):
            nq = queries.stop - queries.start
            kblk, vblk = k_ref[pl.ds(k0, n_keys), :], v_ref[pl.ds(k0, n_keys), :]
            valid = _tile_mask(n_keys, nq, off, window) if masked else None
            out = []
            for h in range(2):
                m, l, acc = carry[h]
                sc = lax.dot_general(kblk, qms[h][queries], _NT, preferred_element_type=f32)
                if fox:
                    sc = sc - _wide(cb_ref[pl.ds(k0, n_keys), h * LANES:(h + 1) * LANES], nq)
                if masked:
                    sc = jnp.where(valid, sc, NEG)
                m_new = jnp.maximum(m, jnp.max(sc, axis=0, keepdims=True))
                p = jnp.exp(sc - m_new)
                alpha = jnp.exp(m - m_new)
                l = alpha * l + jnp.sum(p, axis=0, keepdims=True)
                acc = alpha * acc + lax.dot_general(vblk, p.astype(bf16), _TN, preferred_element_type=f32)
                out.append((m_new, l, acc))
            return tuple(out)

        def start(nq):
            if has_sink:
                return tuple((_wide(sink_ref[h:h + 1, :], nq), jnp.ones((1, nq), f32), jnp.zeros((LANES, nq), f32))
                             for h in range(2))
            return tuple((jnp.full((1, nq), NEG, f32), jnp.zeros((1, nq), f32), jnp.zeros((LANES, nq), f32)) for h in range(2))

        def finish(carry, queries):
            (m0, l0, a0), (m1, l1, a1) = carry
            o_t = jnp.where(top, a0 * (1.0 / l0), a1 * (1.0 / l1))
            o_ref[queries, :] = o_t.T.astype(bf16)
            lse_ref[0:1, queries] = m0 + jnp.log(l0)
            lse_ref[1:2, queries] = m1 + jnp.log(l1)

        if window:
            for c in range(t // LANES):
                queries = slice(c * LANES, (c + 1) * LANES)
                q0 = i * t + c * LANES
                k0 = pl.multiple_of(jnp.maximum(q0 - window, 0), LANES)
                finish(tile(k0, LANES + window, q0 - k0, start(LANES), True, queries), queries)
        else:
            carry = lax.fori_loop(0, i, lambda kb, c: tile(pl.multiple_of(kb * t, t), t, 0, c, False), start(t))
            finish(tile(pl.multiple_of(i * t, t), t, 0, carry, True), slice(0, t))

    q_spec = pl.BlockSpec((t, LANES), lambda j, i: (i, j))
    kv_spec = pl.BlockSpec((s, LANES), lambda j, i: (0, j))
    in_specs, args = [q_spec, kv_spec, kv_spec], [q, k, v]
    if fox:
        in_specs += [pl.BlockSpec((s, 2 * LANES), lambda j, i: (0, j))]
        args += [cum_b]
    if has_sink:
        in_specs += [pl.BlockSpec((None, 2, LANES), lambda j, i: (j, 0, 0))]
        args += [sink_rows.reshape(N_PAIRS, 2, LANES)]
    return pl.pallas_call(
        body, name=name, grid=(N_PAIRS, s // t), in_specs=in_specs,
        out_specs=[q_spec, pl.BlockSpec((None, 2, t), lambda j, i: (j, 0, i))],
        out_shape=[jax.ShapeDtypeStruct((s, N_PAIRS * LANES), bf16), jax.ShapeDtypeStruct((N_PAIRS, 2, s), f32)],
        compiler_params=_params(2),
    )(*args)


def _attn_delta(do, o, name, *, lse=None, sink_rows=None):
    s, hw = do.shape
    tm = _row_tile(s, 512)
    has_sink = sink_rows is not None

    def body(*refs):
        do_ref, o_ref = refs[:2]
        if has_sink:
            lse_ref, sink_ref, dl_ref, ds_ref = refs[2:]

            @pl.when(pl.program_id(0) == 0)
            def _():
                ds_ref[...] = jnp.zeros_like(ds_ref)
        else:
            dl_ref, = refs[2:]
        for j in range(N_PAIRS):
            cols = slice(j * LANES, (j + 1) * LANES)
            prod_t = (do_ref[:, cols].astype(f32) * o_ref[:, cols].astype(f32)).T
            for h in range(2):
                dl = jnp.sum(prod_t[h * HEAD_DIM:(h + 1) * HEAD_DIM, :], axis=0, keepdims=True)
                dl_ref[j, h:h + 1, :] = dl
                if has_sink:
                    r = 2 * j + h
                    p_sink = jnp.exp(sink_ref[r:r + 1, 0:1] - lse_ref[j, h:h + 1, :])
                    ds_ref[r:r + 1, :] += -jnp.sum(p_sink * dl, axis=1, keepdims=True)

    rows_spec = pl.BlockSpec((N_PAIRS, 2, tm), lambda i: (0, 0, i))
    in_specs, args = [_row_spec(tm, hw)] * 2, [do, o]
    out_specs, out_shape = [rows_spec], [jax.ShapeDtypeStruct((N_PAIRS, 2, s), f32)]
    if has_sink:
        in_specs += [rows_spec, _vec_spec(LANES, N_HEADS)]
        args += [lse, sink_rows]
        out_specs += [_vec_spec(LANES, N_HEADS)]
        out_shape += [jax.ShapeDtypeStruct((N_HEADS, LANES), f32)]
    return pl.pallas_call(
        body, name=name, grid=(s // tm,), in_specs=in_specs, out_specs=out_specs, out_shape=out_shape,
        compiler_params=_params(1),
    )(*args)


def _attn_bwd(q, k, v, do, lse, delta, name, *, cum_b=None, window=None, t=256):
    s = q.shape[0]
    t = _row_tile(s, t)
    nblk = s // t
    fox = cum_b is not None
    assert not window or (window % LANES == 0 and LANES + window <= s)

    def body(*refs):
        k_ref, v_ref, q_ref, do_ref, lse_ref, dl_ref = refs[:6]
        rest = list(refs[6:])
        cb_ref = rest.pop(0) if fox else None
        dq_ref, dk_ref, dv_ref = rest[:3]
        dcs_ref, rs_ref = (rest[3], rest[4]) if fox else (None, None)
        b = pl.program_id(1)
        k0 = pl.multiple_of(b * t, t)

        @pl.when(b == 0)
        def _():
            dq_ref[...] = jnp.zeros_like(dq_ref)
            if fox:
                rs_ref[...] = jnp.zeros_like(rs_ref)

        dk_ref[...] = jnp.zeros_like(dk_ref)
        dv_ref[...] = jnp.zeros_like(dv_ref)
        if fox:
            dcs_ref[...] = jnp.zeros_like(dcs_ref)
        low = _lane() < HEAD_DIM
        top = lax.broadcasted_iota(jnp.int32, (LANES, 1), 0) < HEAD_DIM
        kblk, vblk = k_ref[...], v_ref[...]
        k_t = kblk.astype(f32).T.astype(bf16)
        cks = [_wide(cb_ref[pl.ds(k0, t), h * LANES:(h + 1) * LANES], t) for h in range(2)] if fox else None

        def tile(q0, n_queries, off, masked, keys=slice(0, t)):
            cols = pl.ds(q0, n_queries)
            q2, do2 = q_ref[cols, :], do_ref[cols, :]
            zero = jnp.zeros_like(q2)
            valid = _tile_mask(keys.stop - keys.start, n_queries, off, window) if masked else None
            dq_parts = []
            for h in range(2):
                qm = jnp.where(low, q2, zero) if h == 0 else jnp.where(low, zero, q2)
                dom = jnp.where(low, do2, zero) if h == 0 else jnp.where(low, zero, do2)
                sc = lax.dot_general(kblk[keys], qm, _NT, preferred_element_type=f32)
                if fox:
                    sc = sc - cks[h]
                if masked:
                    sc = jnp.where(valid, sc, NEG)
                p = jnp.exp(sc - lse_ref[h:h + 1, cols])
                dp = lax.dot_general(vblk[keys], dom, _NT, preferred_element_type=f32)
                ds = p * (dp - dl_ref[h:h + 1, cols])
                pb, dsb = p.astype(bf16), ds.astype(bf16)
                dv_ref[keys, :] += jnp.dot(pb, dom, preferred_element_type=f32)
                dk_ref[keys, :] += jnp.dot(dsb, qm, preferred_element_type=f32)
                dq_parts.append(jnp.dot(k_t[:, keys], dsb, preferred_element_type=f32))
                if fox:
                    dcs_ref[:, h * LANES:(h + 1) * LANES] += sum(ds[:, g * LANES:(g + 1) * LANES] for g in range(t // LANES))
                    rs_ref[h:h + 1, cols] += jnp.sum(ds, axis=0, keepdims=True)
            dq_ref[:, cols] += jnp.where(top, dq_parts[0], dq_parts[1])

        def later_block(qb, carry):
            tile(pl.multiple_of(qb * t, t), t, 0, False)
            return carry

        if window:
            for c in range(t // LANES):
                first = b * t + c * LANES
                q0 = pl.multiple_of(jnp.minimum(first, s - (LANES + window)), LANES)
                tile(q0, LANES + window, q0 - first, True, slice(c * LANES, (c + 1) * LANES))
        else:
            tile(k0, t, 0, True)
            lax.fori_loop(b + 1, nblk, later_block, 0)

    kv_spec = pl.BlockSpec((t, LANES), lambda j, b: (b, j))
    seq_spec = pl.BlockSpec((s, LANES), lambda j, b: (0, j))
    rows_spec = pl.BlockSpec((None, 2, s), lambda j, b: (j, 0, 0))
    hw = N_PAIRS * LANES
    in_specs, args = [kv_spec, kv_spec, seq_spec, seq_spec, rows_spec, rows_spec], [k, v, q, do, lse, delta]
    out_specs = [pl.BlockSpec((LANES, s), lambda j, b: (j, 0)), kv_spec, kv_spec]
    out_shape = [jax.ShapeDtypeStruct((hw, s), f32), jax.ShapeDtypeStruct((s, hw), f32), jax.ShapeDtypeStruct((s, hw), f32)]
    if fox:
        in_specs += [pl.BlockSpec((s, 2 * LANES), lambda j, b: (0, j))]
        args += [cum_b]
        out_specs += [pl.BlockSpec((t, 2 * LANES), lambda j, b: (b, j)), rows_spec]
        out_shape += [jax.ShapeDtypeStruct((s, N_HEADS * LANES), f32), jax.ShapeDtypeStruct((N_PAIRS, 2, s), f32)]
    return pl.pallas_call(
        body, name=name, grid=(N_PAIRS, nblk), in_specs=in_specs, out_specs=out_specs, out_shape=out_shape,
        compiler_params=_params(2),
    )(*args)


def _merge(ba, bb, gl, name):
    s, d = ba.shape
    tm = _row_tile(s, 512)

    def body(a_ref, b_ref, g_ref, o_ref):
        g0, g1 = jax.nn.sigmoid(g_ref[:, :d].astype(f32)), jax.nn.sigmoid(g_ref[:, d:].astype(f32))
        o_ref[...] = (g0 * a_ref[...].astype(f32) + g1 * b_ref[...].astype(f32)).astype(bf16)

    return pl.pallas_call(
        body, name=name, grid=(s // tm,), in_specs=[_row_spec(tm, d)] * 2 + [_row_spec(tm, 2 * d)],
        out_specs=_row_spec(tm, d), out_shape=jax.ShapeDtypeStruct((s, d), bf16), compiler_params=_params(1),
    )(ba, bb, gl)


def _merge_bwd(dm, ba, bb, gl, name):
    s, d = ba.shape
    tm = _row_tile(s, 512)

    def body(dm_ref, a_ref, b_ref, g_ref, da_ref, db_ref, dg_ref):
        dmv = dm_ref[...].astype(f32)
        g0, g1 = jax.nn.sigmoid(g_ref[:, :d].astype(f32)), jax.nn.sigmoid(g_ref[:, d:].astype(f32))
        da_ref[...] = (dmv * g0).astype(bf16)
        db_ref[...] = (dmv * g1).astype(bf16)
        dg_ref[:, :d] = (dmv * a_ref[...].astype(f32) * (g0 * (1.0 - g0))).astype(bf16)
        dg_ref[:, d:] = (dmv * b_ref[...].astype(f32) * (g1 * (1.0 - g1))).astype(bf16)

    return pl.pallas_call(
        body, name=name, grid=(s // tm,), in_specs=[_row_spec(tm, d)] * 3 + [_row_spec(tm, 2 * d)],
        out_specs=[_row_spec(tm, d)] * 2 + [_row_spec(tm, 2 * d)],
        out_shape=[jax.ShapeDtypeStruct((s, d), bf16)] * 2 + [jax.ShapeDtypeStruct((s, 2 * d), bf16)],
        compiler_params=_params(1),
    )(dm, ba, bb, gl)


GLU_TILE = 256


def _ffn_in_swiglu(h, w_t, name):
    s, d = h.shape
    f = w_t.shape[0] // 2
    tm = _row_tile(s, 2048)
    tg = GLU_TILE
    nb = f // tg

    def body(h_ref, wg_ref, wu_ref, g_ref, u_ref, act_ref):
        hv = h_ref[...]
        g = lax.dot_general(hv, wg_ref[...], _NT, preferred_element_type=f32)
        u = lax.dot_general(hv, wu_ref[...], _NT, preferred_element_type=f32)
        g_ref[...] = g.astype(bf16)
        u_ref[...] = u.astype(bf16)
        act_ref[...] = (g * jax.nn.sigmoid(g) * u).astype(bf16)

    col = pl.BlockSpec((tm, tg), lambda i, j: (i, j))
    return pl.pallas_call(
        body, name=name, grid=(s // tm, nb),
        in_specs=[pl.BlockSpec((tm, d), lambda i, j: (i, 0)), pl.BlockSpec((tg, d), lambda i, j: (j, 0)),
                  pl.BlockSpec((tg, d), lambda i, j: (j + nb, 0))],
        out_specs=[col] * 3, out_shape=[jax.ShapeDtypeStruct((s, f), bf16)] * 3, compiler_params=_params(2),
    )(h, w_t, w_t)


def _ffn_out_dgrad_swiglu(dy, w_out, g, u, name):
    s, d = dy.shape
    f = g.shape[1]
    tm = _row_tile(s, 2048)
    tg = GLU_TILE

    def body(dy_ref, w_ref, g_ref, u_ref, dg_ref, du_ref):
        dv = lax.dot_general(dy_ref[...], w_ref[...], _NT, preferred_element_type=f32)
        gv, uv = g_ref[...].astype(f32), u_ref[...].astype(f32)
        sg = jax.nn.sigmoid(gv)
        dg_ref[...] = (dv * uv * (sg * (1.0 + gv * (1.0 - sg)))).astype(bf16)
        du_ref[...] = (dv * (gv * sg)).astype(bf16)

    col = pl.BlockSpec((tm, tg), lambda i, j: (i, j))
    return pl.pallas_call(
        body, name=name, grid=(s // tm, f // tg),
        in_specs=[pl.BlockSpec((tm, d), lambda i, j: (i, 0)), pl.BlockSpec((tg, d), lambda i, j: (j, 0)), col, col],
        out_specs=[col] * 2, out_shape=[jax.ShapeDtypeStruct((s, f), bf16)] * 2, compiler_params=_params(2),
    )(dy, w_out, g, u)


def _ffn_in_dgrad(dg, du, w_t, name, after=None):
    s, f = dg.shape
    d = w_t.shape[1]
    tm, tn = _matmul_tiles(s, d, 2 * f, dg.dtype.itemsize, w_t.dtype.itemsize, 4)

    def body(dg_ref, du_ref, wg_ref, wu_ref, *rest):
        rest[-1][...] = (jnp.dot(dg_ref[...], wg_ref[...], preferred_element_type=f32)
                         + jnp.dot(du_ref[...], wu_ref[...], preferred_element_type=f32))

    extra = [] if after is None else [after]
    rows = pl.BlockSpec((tm, f), lambda i, j: (i, 0))
    return pl.pallas_call(
        body, name=name, grid=(s // tm, d // tn),
        in_specs=[rows, rows, pl.BlockSpec((f, tn), lambda i, j: (0, j)), pl.BlockSpec((f, tn), lambda i, j: (1, j))]
        + [pl.BlockSpec(memory_space=pl.ANY)] * len(extra),
        out_specs=pl.BlockSpec((tm, tn), lambda i, j: (i, j)),
        out_shape=jax.ShapeDtypeStruct((s, d), f32), compiler_params=_params(2),
    )(dg, du, w_t, w_t, *extra)


def _ada_fwd(c_all, w, b, name):
    def body(c_ref, w_ref, b_ref, o_ref):
        o_ref[...] = jnp.dot(c_ref[...].astype(bf16), w_ref[...].astype(bf16), preferred_element_type=f32) + b_ref[...]

    return pl.pallas_call(
        body, name=name, out_shape=jax.ShapeDtypeStruct((c_all.shape[0], w.shape[1]), f32), compiler_params=_params(),
    )(c_all, w, b)


def _ada_wgrad(c_all, d_all, name):
    n, d = c_all.shape
    w = d_all.shape[1]

    def body(c_ref, d_ref, o_ref):
        eye = (lax.broadcasted_iota(jnp.int32, (n, n), 0) == lax.broadcasted_iota(jnp.int32, (n, n), 1)).astype(f32)
        ct = lax.dot_general(c_ref[...], eye, _TN, precision=lax.Precision.HIGHEST, preferred_element_type=f32)
        g = ct[:, 0:1] * d_ref[0:1, :]
        for bi in range(1, n):
            g = g + ct[:, bi:bi + 1] * d_ref[bi:bi + 1, :]
        o_ref[0] = g

    return pl.pallas_call(
        body, name=name, out_shape=jax.ShapeDtypeStruct((1, d, w), f32), compiler_params=_params(),
    )(c_all, d_all)


def _adamw(parts, w, m, v, name, mine=None):
    r, c = w.shape
    n_parts = parts.shape[0]
    row_tiles = [t for t in range(min(r, 256), 0, -1) if r % t == 0 and (t % 16 == 0 or t == r)]
    if row_tiles:
        tr, tc = row_tiles[0], c
    else:
        tr, tc = r, next(t for t in (256, LANES) if c % t == 0)

    def body(p_ref, *rest):
        own_ref = rest[0] if mine is not None else None
        w_ref, m_ref, v_ref, g_ref, d_ref, nm_ref, nv_ref = rest[-7:]
        if mine is not None:
            x, y, cc = _me()
            me = 4 * x + 2 * y + cc

        def part(i):
            if mine is None:
                return p_ref[i].astype(f32)
            return jnp.where(me == i, own_ref[i], p_ref[i]).astype(f32)

        g = part(0)
        for i in range(1, n_parts):
            g = g + part(i)
        mm = ADAM_B1 * m_ref[...] + (1.0 - ADAM_B1) * g
        vv = ADAM_B2 * v_ref[...] + (1.0 - ADAM_B2) * (g * g)
        m_hat = mm / (1.0 - ADAM_B1 ** ADAM_STEP)
        v_hat = vv / (1.0 - ADAM_B2 ** ADAM_STEP)
        g_ref[...] = g
        d_ref[...] = -ADAM_LR * (m_hat / (jnp.sqrt(v_hat) + ADAM_EPS) + ADAM_WD * w_ref[...])
        nm_ref[...] = mm
        nv_ref[...] = vv

    spec = pl.BlockSpec((tr, tc), lambda i, j: (i, j))
    stack = [parts] if mine is None else [parts, mine]
    return pl.pallas_call(
        body, name=name, grid=(r // tr, c // tc),
        in_specs=[pl.BlockSpec((n_parts, tr, tc), lambda i, j: (0, i, j))] * len(stack) + [spec] * 3,
        out_specs=[spec] * 4, out_shape=[jax.ShapeDtypeStruct((r, c), f32)] * 4, compiler_params=_params(2),
    )(*stack, w, m, v)


def _me():
    return lax.axis_index("x"), lax.axis_index("y"), lax.axis_index("c")


def _all_gather(arrays, name, vmem=False, after=None):
    n = len(arrays)
    space = pltpu.VMEM if vmem else pl.ANY
    extra = [] if after is None else [after]

    def body(*refs):
        ins = refs[:n]
        outs = refs[n + len(extra):2 * n + len(extra)]
        send_sems, recv_sems, local_sems = refs[2 * n + len(extra):]
        x, y, c = _me()
        me, sibling = (x, y, c), (x, y, 1 - c)
        chips = [(1 - x, y), (x, 1 - y), (1 - x, 1 - y)]

        def rows(a, dev):
            return outs[a].at[4 * dev[0] + 2 * dev[1] + dev[2]]

        def copy(a, k, block, to, src=None):
            return pltpu.make_async_remote_copy(
                src_ref=rows(a, block) if src is None else src, dst_ref=rows(a, block),
                send_sem=send_sems.at[a, k], recv_sem=recv_sems.at[a, k], device_id=to, device_id_type=MESH)

        mine = [pltpu.make_async_copy(ins[a], rows(a, me), local_sems.at[a]) for a in range(n)]
        for cp in mine:
            cp.start()
        first = []
        for a in range(n):
            first.append(copy(a, 0, me, sibling, src=ins[a]))
            first += [copy(a, 1 + j, me, (*chip, c), src=ins[a]) for j, chip in enumerate(chips)]
        for cp in first:
            cp.start()
        passed = []
        for j, chip in enumerate(chips):
            for a in range(n):
                copy(a, 1 + j, (*chip, c), me).wait_recv()
                fwd = copy(a, 4 + j, (*chip, c), sibling)
                fwd.start()
                passed.append(fwd)
        for a in range(n):
            copy(a, 0, sibling, me).wait_recv()
            for j, chip in enumerate(chips):
                copy(a, 4 + j, (*chip, 1 - c), me).wait_recv()
        for cp in first + passed:
            cp.wait_send()
        for cp in mine:
            cp.wait()

    outs = pl.pallas_call(
        body, name=name,
        in_specs=[pl.BlockSpec(memory_space=space)] * n + [pl.BlockSpec(memory_space=pl.ANY)] * len(extra),
        out_specs=[pl.BlockSpec(memory_space=space)] * n,
        out_shape=[jax.ShapeDtypeStruct((N_DEV,) + a.shape, a.dtype) for a in arrays],
        scratch_shapes=[pltpu.SemaphoreType.DMA((n, 7)), pltpu.SemaphoreType.DMA((n, 7)), pltpu.SemaphoreType.DMA((n,))],
        compiler_params=pltpu.CompilerParams(vmem_limit_bytes=VMEM_LIMIT),
    )(*arrays, *extra)
    return list(outs)


_FLIPS = ((0, 0, 1), (1, 0, 0), (0, 1, 0), (1, 1, 0), (1, 0, 1), (0, 1, 1), (1, 1, 1))
_HBM = pl.BlockSpec(memory_space=pltpu.HBM)
_SEM = pl.BlockSpec(memory_space=pltpu.SEMAPHORE)


def _exchange_copies(scatter, srcs, lands, send_sems, recv_sems):
    x, y, c = _me()
    me_row = 4 * x + 2 * y + c
    out = []
    for k, (fx, fy, fc) in enumerate(_FLIPS):
        peer = (x ^ fx, y ^ fy, c ^ fc)
        peer_row = 4 * peer[0] + 2 * peer[1] + peer[2]
        for a in range(len(srcs)):
            out.append(pltpu.make_async_remote_copy(
                src_ref=srcs[a].at[peer_row] if scatter else srcs[a], dst_ref=lands[a].at[me_row],
                send_sem=send_sems.at[7 * a + k], recv_sem=recv_sems.at[7 * a + k], device_id=peer, device_id_type=MESH))
    return out


def _exchange_start(arrays, scatter, name):
    n = len(arrays)
    lands = [lax.empty(a.shape if scatter else (N_DEV,) + a.shape, a.dtype) for a in arrays]

    def body(*refs):
        srcs, zones = refs[:n], refs[n:2 * n]
        send_sems, recv_sems = refs[2 * n], refs[2 * n + 1]
        token = refs[-1]
        for cp in _exchange_copies(scatter, srcs, zones, send_sems, recv_sems):
            cp.start()
        token[...] = jnp.zeros_like(token)

    thru = [pltpu.HBM(a.shape, a.dtype) for a in list(arrays) + lands]
    outs = pl.pallas_call(
        body, name=name,
        out_shape=(pltpu.SemaphoreType.DMA((7 * n,)), pltpu.SemaphoreType.DMA((7 * n,)), *thru, jax.ShapeDtypeStruct((8, LANES), f32)),
        in_specs=[_HBM] * (2 * n), out_specs=(_SEM, _SEM, *[_HBM] * (2 * n), pl.BlockSpec(memory_space=pltpu.VMEM)),
        input_output_aliases={i: 2 + i for i in range(2 * n)},
        compiler_params=pltpu.CompilerParams(has_side_effects=pltpu.SideEffectType.DATAFLOW_SIDE_EFFECTING),
    )(*[pltpu.with_memory_space_constraint(a, pltpu.HBM) for a in list(arrays) + lands])
    return dict(n=n, scatter=scatter, sems=outs[:2], srcs=outs[2:2 + n], lands=outs[2 + n:2 + 2 * n], token=outs[-1])


def _exchange_wait(handle, after, name):
    n, scatter = handle["n"], handle["scatter"]

    def body(*refs):
        srcs, zones = refs[:n], refs[n:2 * n]
        send_sems, recv_sems = refs[2 * n], refs[2 * n + 1]
        for cp in _exchange_copies(scatter, srcs, zones, send_sems, recv_sems):
            cp.wait_send()
            cp.wait_recv()

    thru = [pltpu.HBM(a.shape, a.dtype) for a in list(handle["srcs"]) + list(handle["lands"])]
    outs = pl.pallas_call(
        body, name=name, out_shape=tuple(thru),
        in_specs=[_HBM] * (2 * n) + [_SEM, _SEM, pl.BlockSpec(memory_space=pl.ANY)], out_specs=tuple([_HBM] * (2 * n)),
        input_output_aliases={i: i for i in range(2 * n)},
        compiler_params=pltpu.CompilerParams(has_side_effects=pltpu.SideEffectType.DATAFLOW_SIDE_EFFECTING),
    )(*handle["srcs"], *handle["lands"], *handle["sems"], after)
    return list(outs[n:])


def _cols_from_shards(g):
    return jnp.transpose(g, (1, 0, 2)).reshape(g.shape[1], -1)


def _shards_from_cols(a):
    return jnp.transpose(a.reshape(a.shape[0], N_DEV, -1), (1, 0, 2))


def _local_step(x, positions, ada, g_pre_mix, g_post_mix, b_f, sinks, g_pre_ffn, g_post_ffn, target,
                w_in_t, late_weights, on_grads):
    s, d = x.shape
    row = lambda v: v.reshape(1, -1)
    shift_m, scale_m, gate_m, shift_f, scale_f, gate_f = (ada[i:i + 1] for i in range(6))
    w_gate_t, w_qkv_t = w_in_t[F_OFF + N_HEADS:], w_in_t[:QKV_W]
    w_f_t = jnp.pad(w_in_t[F_OFF:F_OFF + N_HEADS], ((0, LANES - N_HEADS), (0, 0)))
    w_in_p_t = jnp.concatenate([w_gate_t, w_qkv_t, w_f_t], axis=0)
    bf_row = jnp.pad(row(b_f), ((0, 0), (0, LANES - N_HEADS)))
    sink_rows = jnp.broadcast_to(sinks.reshape(N_HEADS, 1).astype(f32), (N_HEADS, LANES))
    inv_freq = 1.0 / (ROPE_THETA ** (jnp.arange(0, HEAD_DIM, 2, dtype=f32) / HEAD_DIM))
    cos, sin_s = _rope_tables(positions.reshape(s, 1), jnp.tile(inv_freq, 4).reshape(1, LANES), "rope_tables")

    h1 = _prenorm(x, row(g_pre_mix), scale_m, shift_m, "prenorm_mix")
    gl = _matmul(h1, w_gate_t, "nt", bf16, "proj_gate")
    qkv = _matmul(h1, w_qkv_t, "nt", f32, "proj_qkv")
    fl = _matmul(h1, w_f_t, "nt", f32, "proj_forget")
    qa, ka, va, qb, kb, vb = _qkv_prep(qkv, cos, sin_s, "qkv_prep")
    cum_b = _forget_prep(fl, bf_row, "forget_prep")
    o_a, lse_a = _attn_fwd(qa, ka, va, "swa_fwd", sink_rows=sink_rows, window=WINDOW, t=512)
    o_b, lse_b = _attn_fwd(qb, kb, vb, "fox_fwd", cum_b=cum_b, t=512)
    w_branch_a, w_branch_b, w_out, w_ffn_in_t, w_ffn_out = late_weights(o_b)
    ba = _matmul(o_a, w_branch_a, "nn", bf16, "branch_a")
    bb = _matmul(o_b, w_branch_b, "nn", bf16, "branch_b")
    merged = _merge(ba, bb, gl, "merge")
    y1 = _matmul(merged, w_out, "nn", f32, "out_proj")
    x2 = _postnorm_res(x, y1, row(g_post_mix), gate_m, "postnorm_mix")

    h2 = _prenorm(x2, row(g_pre_ffn), scale_f, shift_f, "prenorm_ffn")
    g_ff, u_ff, act = _ffn_in_swiglu(h2, w_ffn_in_t, "ffn_in_swiglu")
    y2 = _matmul(act, w_ffn_out, "nn", f32, "ffn_out")
    loss_row, d_out, d_y2, vec_pf = _loss_tail(x2, y2, row(g_post_ffn), gate_f, target, "loss_tail")

    g_w_ffn_out = _matmul(act, d_y2, "tn", bf16, "ffn_out_wgrad")
    dg_ff, du_ff = _ffn_out_dgrad_swiglu(d_y2, w_ffn_out, g_ff, u_ff, "ffn_out_dgrad_swiglu")
    g_w_ffn_in_t = jnp.concatenate([_matmul(dg_ff, h2, "tn", bf16, "ffn_gate_wgrad"),
                                    _matmul(du_ff, h2, "tn", bf16, "ffn_up_wgrad")], axis=0)
    sent = on_grads(dict(w_ffn_in=g_w_ffn_in_t, w_ffn_out=g_w_ffn_out))
    d_h2 = _ffn_in_dgrad(dg_ff, du_ff, w_ffn_in_t, "ffn_in_dgrad", after=sent)
    d_x2, vec_nf = _prenorm_bwd(d_h2, x2, row(g_pre_ffn), scale_f, d_out, "prenorm_ffn_bwd")

    d_y1, vec_pm = _postnorm_bwd(d_x2, y1, row(g_post_mix), gate_m, "postnorm_mix_bwd")
    g_w_out = _matmul(merged, d_y1, "tn", bf16, "out_proj_wgrad")
    d_merged = _matmul(d_y1, w_out, "nt", bf16, "out_proj_dgrad")
    d_ba, d_bb, dgl = _merge_bwd(d_merged, ba, bb, gl, "merge_bwd")
    g_w_branch_a = _matmul(o_a, d_ba, "tn", bf16, "branch_a_wgrad")
    g_w_branch_b = _matmul(o_b, d_bb, "tn", bf16, "branch_b_wgrad")
    sent = on_grads(dict(w_out=g_w_out, w_branch_a=g_w_branch_a, w_branch_b=g_w_branch_b))
    d_oa = _matmul(d_ba, w_branch_a, "nt", bf16, "branch_a_dgrad", after=sent)
    d_ob = _matmul(d_bb, w_branch_b, "nt", bf16, "branch_b_dgrad", after=sent)
    delta_a, d_sink = _attn_delta(d_oa, o_a, "swa_delta", lse=lse_a, sink_rows=sink_rows)
    delta_b, = _attn_delta(d_ob, o_b, "fox_delta")
    dqa_t, dka, dva = _attn_bwd(qa, ka, va, d_oa, lse_a, delta_a, "swa_bwd", window=WINDOW, t=512)
    dqb_t, dkb, dvb, dcs, rs = _attn_bwd(qb, kb, vb, d_ob, lse_b, delta_b, "fox_bwd", cum_b=cum_b, t=512)
    dqkv = _qkv_prep_bwd(dqa_t, dka, dva, dqb_t, dkb, dvb, cos, sin_s, "qkv_prep_bwd")
    dfl, vec_bf = _forget_prep_bwd(rs.reshape(N_HEADS, s), dcs, fl, bf_row, "forget_prep_bwd")
    dproj = jnp.concatenate([dgl, dqkv, dfl], axis=1)
    g_w_in_p_t = _matmul(dproj, h1, "tn", bf16, "in_proj_wgrad")
    g_w_in_t = jnp.concatenate([g_w_in_p_t[GATE_W:GATE_W + QKV_W], g_w_in_p_t[GATE_W + QKV_W:GATE_W + QKV_W + N_HEADS],
                                g_w_in_p_t[:GATE_W]], axis=0)
    sent = on_grads(dict(w_in=g_w_in_t))
    d_h1 = _matmul(dproj, w_in_p_t, "nn", f32, "in_proj_dgrad", after=sent)
    grad_x, vec_nm = _prenorm_bwd(d_h1, x, row(g_pre_mix), scale_m, d_x2, "prenorm_mix_bwd")

    d_ada = jnp.concatenate([vec_nm[0], vec_nm[1], vec_pm[0], vec_nf[0], vec_nf[1], vec_pf[0]])
    small = dict(b_ada=d_ada, g_pre_mix=vec_nm[2], g_post_mix=vec_pm[1], g_pre_ffn=vec_nf[2], g_post_ffn=vec_pf[1],
                 b_f=vec_bf[0, :N_HEADS], sinks=d_sink[:, 0], loss=loss_row[0, :1])
    return grad_x, small


_SMALL = (("b_ada", 6144), ("g_pre_mix", 1024), ("g_post_mix", 1024), ("g_pre_ffn", 1024), ("g_post_ffn", 1024),
          ("b_f", 128), ("sinks", 128), ("loss", 128))
_SMALL_ROWS = 88


def _pack_small(vals):
    parts = [jnp.pad(vals[k].reshape(-1).astype(f32), (0, n - vals[k].size)) for k, n in _SMALL]
    flat = jnp.concatenate(parts)
    return jnp.pad(flat, (0, _SMALL_ROWS * LANES - flat.size)).reshape(_SMALL_ROWS, LANES)


def _unpack_small(slab, shapes):
    flat, out, off = slab.reshape(-1), {}, 0
    for k, n in _SMALL:
        size = math.prod(shapes[k])
        out[k] = flat[off:off + size].reshape(shapes[k])
        off += n
    return out


def kernel(x, c, positions, w_ada, b_ada, g_pre_mix, g_post_mix, w_in, b_f, sinks, w_branch_a, w_branch_b, w_out, g_pre_ffn, g_post_ffn, w_ffn_in, w_ffn_out, loss_target, m_w_ada, m_b_ada, m_g_pre_mix, m_g_post_mix, m_w_in, m_b_f, m_sinks, m_w_branch_a, m_w_branch_b, m_w_out, m_g_pre_ffn, m_g_post_ffn, m_w_ffn_in, m_w_ffn_out, v_w_ada, v_b_ada, v_g_pre_mix, v_g_post_mix, v_w_in, v_b_f, v_sinks, v_w_branch_a, v_w_branch_b, v_w_out, v_g_pre_ffn, v_g_post_ffn, v_w_ffn_in, v_w_ffn_out):
    xi, yi, ci = _me()
    me = 4 * xi + 2 * yi + ci
    d = D_MODEL
    ada_w = w_ada.shape[2]

    c_all, = _all_gather([c], "gather_c", vmem=True)
    c_all = c_all.reshape(N_DEV, d)
    b_mine = lax.dynamic_slice(b_ada, (0, me * ada_w), (1, ada_w))
    ada_cols = _ada_fwd(c_all, w_ada[0], b_mine, "ada_fwd")
    ada_all, = _all_gather([ada_cols], "gather_ada", vmem=True)
    ada = lax.dynamic_index_in_dim(ada_all, me, axis=1, keepdims=False).reshape(6, d)

    transposed = ("w_in", "w_ffn_in")
    tr = lambda a: jnp.transpose(a[0])
    in_rows = w_in.shape[2]

    def pad_rows(a):
        extra = -a.shape[-2] % 16
        return jnp.pad(a, [(0, 0)] * (a.ndim - 2) + [(0, extra), (0, 0)]) if extra else a

    g_in, = _all_gather([pad_rows(tr(w_in).astype(bf16))], "gather_w_in")
    late = [w.astype(bf16) for w in (w_branch_a[0], w_branch_b[0], w_out[0], tr(w_ffn_in), w_ffn_out[0])]
    late_h = _exchange_start(late, False, "gather_late_start")

    def mine_into(zone, block):
        return lax.dynamic_update_index_in_dim(zone, block, me, 0)

    def rows_from_shards(g):
        return g.reshape(g.shape[0] * g.shape[1], g.shape[2])

    def late_weights(after):
        zones = _exchange_wait(late_h, after, "gather_late_wait")
        g_ba, g_bb, g_out, g_fi, g_fo = (mine_into(z, w) for z, w in zip(zones, late))
        return (_cols_from_shards(g_ba), _cols_from_shards(g_bb), rows_from_shards(g_out), rows_from_shards(g_fi),
                rows_from_shards(g_fo))

    row_sharded = ("w_out", "w_ffn_out") + transposed
    in_flight = []

    def on_grads(group):
        sends = [pad_rows(g.reshape(N_DEV, g.shape[0] // N_DEV, g.shape[1])) if nm in row_sharded else _shards_from_cols(g)
                 for nm, g in group.items()]
        handle = _exchange_start(sends, True, "scatter_start_%d" % len(in_flight))
        in_flight.append((list(group), sends, handle))
        return handle["token"]

    grad_x, small = _local_step(
        x[0], positions[0], ada + late_h["token"][0, 0], g_pre_mix[0], g_post_mix[0], b_f[0], sinks[0], g_pre_ffn[0],
        g_post_ffn[0], loss_target[0], rows_from_shards(g_in[:, :in_rows]), late_weights, on_grads)

    ws = dict(w_in=(w_in, m_w_in, v_w_in), w_branch_a=(w_branch_a, m_w_branch_a, v_w_branch_a),
              w_branch_b=(w_branch_b, m_w_branch_b, v_w_branch_b), w_out=(w_out, m_w_out, v_w_out),
              w_ffn_in=(w_ffn_in, m_w_ffn_in, v_w_ffn_in), w_ffn_out=(w_ffn_out, m_w_ffn_out, v_w_ffn_out))
    res = {}

    def finish_group(gi, after):
        names, sends, handle = in_flight[gi]
        zones = _exchange_wait(handle, after, "scatter_wait_%d" % gi)
        for nm, zone, sent in zip(names, zones, sends):
            w, m, v = (pad_rows(tr(a)) if nm in transposed else a[0] for a in ws[nm])
            out = _adamw(zone, w, m, v, "adamw_" + nm, mine=sent)
            after = out[0]
            res[nm] = [jnp.transpose(o[:ws[nm][0].shape[2]]) for o in out] if nm in transposed else out
        return after

    done = finish_group(1, finish_group(0, grad_x))

    slab_all, = _all_gather([_pack_small(small)], "gather_small", vmem=True, after=done)
    small_w = dict(b_ada=b_ada, g_pre_mix=g_pre_mix, g_post_mix=g_post_mix, g_pre_ffn=g_pre_ffn, g_post_ffn=g_post_ffn,
                   b_f=b_f, sinks=sinks, loss=jnp.zeros((1,), f32))
    small_m = dict(b_ada=m_b_ada, g_pre_mix=m_g_pre_mix, g_post_mix=m_g_post_mix, g_pre_ffn=m_g_pre_ffn,
                   g_post_ffn=m_g_post_ffn, b_f=m_b_f, sinks=m_sinks, loss=jnp.zeros((1,), f32))
    small_v = dict(b_ada=v_b_ada, g_pre_mix=v_g_pre_mix, g_post_mix=v_g_post_mix, g_pre_ffn=v_g_pre_ffn,
                   g_post_ffn=v_g_post_ffn, b_f=v_b_f, sinks=v_sinks, loss=jnp.ones((1,), f32))
    shapes = {k: small_w[k].shape for k, _ in _SMALL}
    s_out = _adamw(slab_all, _pack_small(small_w), _pack_small(small_m), _pack_small(small_v), "adamw_small")
    s_grad, s_delta, s_m, s_v = (_unpack_small(o, shapes) for o in s_out)

    d_ada_all = lax.dynamic_slice(slab_all[:, :6144 // LANES, :].reshape(N_DEV, 6144), (0, me * ada_w), (N_DEV, ada_w))
    ada_parts = _ada_wgrad(c_all, d_ada_all, "ada_wgrad")

    res["w_ada"] = _adamw(ada_parts, w_ada[0], m_w_ada[0], v_w_ada[0], "adamw_w_ada")
    finish_group(2, res["w_ada"][0])

    order = ["w_ada", "b_ada", "g_pre_mix", "g_post_mix", "w_in", "b_f", "sinks", "w_branch_a", "w_branch_b", "w_out",
             "g_pre_ffn", "g_post_ffn", "w_ffn_in", "w_ffn_out"]
    outs = [s_grad["loss"].reshape(()), grad_x[None]]
    for which, small_o in enumerate((s_grad, s_delta, s_m, s_v)):
        for nm in order:
            outs.append(res[nm][which][None] if nm in res else small_o[nm])
    return tuple(outs)
```

```python
import functools
import math

import jax
import jax.numpy as jnp
from jax import lax
from jax.experimental import pallas as pl
from jax.experimental.pallas import tpu as pltpu

f32 = jnp.float32
bf16 = jnp.bfloat16

D_MODEL = 1024
HEAD_DIM = 64
N_HEADS = 8
N_PAIRS = 4
QKV_W = 2304
GATE_W = 2048
F_OFF = 2304
IN_W = 4360
WINDOW = 128
ROPE_THETA = 10000.0
RMS_EPS = 1e-6
D_FF = 2816
N_DEV = 8
ADAM_LR, ADAM_B1, ADAM_B2, ADAM_EPS, ADAM_WD, ADAM_STEP = 0.001, 0.9, 0.999, 1e-08, 0.01, 10
NEG = -1e30
LANES = 128
VMEM_LIMIT = 48 * 1024 * 1024
MESH = pl.DeviceIdType.MESH

_NT = (((1,), (1,)), ((), ()))
_TN = (((0,), (0,)), ((), ()))


def _params(n_grid=0):
    sem = ("arbitrary",) * n_grid if n_grid else None
    return pltpu.CompilerParams(dimension_semantics=sem, vmem_limit_bytes=VMEM_LIMIT)


def _row_tile(s, want):
    t = min(s, want)
    assert s % t == 0, (s, t)
    return t


MATMUL_VMEM_BUDGET = 40 * 1024 * 1024


def _matmul_tiles(m, n, k, a_item, b_item, o_item):
    def tiles(d):
        return [t for t in range(LANES, min(d, 2048) + 1, LANES) if d % t == 0] or [d]

    best = None
    for tm in tiles(m):
        for tn in tiles(n):
            vmem = 2 * (tm * k * a_item + tn * k * b_item + tm * tn * o_item) + tm * tn * 4
            if vmem > MATMUL_VMEM_BUDGET:
                continue
            traffic = m * k * a_item + n * k * b_item * (1 if tn == n else m // tm) + m * n * o_item
            steps = (m // tm) * (n // tn)
            key = (traffic, 0, steps) if steps >= 4 else (traffic, 1, -steps)
            if best is None or key < best[0]:
                best = (key, tm, tn)
    assert best is not None, (m, n, k)
    return best[1], best[2]


def _matmul(a, b, mode, out_dtype, name, after=None):
    if mode == "nn":
        (m, k), n = a.shape, b.shape[1]
    elif mode == "nt":
        (m, k), n = a.shape, b.shape[0]
    else:
        (k, m), n = a.shape, b.shape[1]
    tm, tn = _matmul_tiles(m, n, k, a.dtype.itemsize, b.dtype.itemsize, jnp.dtype(out_dtype).itemsize)
    if mode == "nn":
        a_spec, b_spec, dims = pl.BlockSpec((tm, k), lambda i, j: (i, 0)), pl.BlockSpec((k, tn), lambda i, j: (0, j)), None
    elif mode == "nt":
        a_spec, b_spec, dims = pl.BlockSpec((tm, k), lambda i, j: (i, 0)), pl.BlockSpec((tn, k), lambda i, j: (j, 0)), _NT
    else:
        a_spec, b_spec, dims = pl.BlockSpec((k, tm), lambda i, j: (0, i)), pl.BlockSpec((k, tn), lambda i, j: (0, j)), _TN

    def body(a_ref, b_ref, *rest):
        o_ref = rest[-1]
        av, bv = a_ref[...].astype(bf16), b_ref[...].astype(bf16)
        if dims is None:
            r = jnp.dot(av, bv, preferred_element_type=f32)
        else:
            r = lax.dot_general(av, bv, dims, preferred_element_type=f32)
        o_ref[...] = r.astype(out_dtype)

    extra = [] if after is None else [after]
    return pl.pallas_call(
        body, name=name, grid=(m // tm, n // tn), in_specs=[a_spec, b_spec] + [pl.BlockSpec(memory_space=pl.ANY)] * len(extra),
        out_specs=pl.BlockSpec((tm, tn), lambda i, j: (i, j)),
        out_shape=jax.ShapeDtypeStruct((m, n), out_dtype), compiler_params=_params(2),
    )(a, b, *extra)


def _rstd(v):
    return lax.rsqrt(jnp.mean(v * v, axis=-1, keepdims=True) + RMS_EPS)


def _row_spec(tm, d):
    return pl.BlockSpec((tm, d), lambda i: (i, 0))


def _vec_spec(d, rows=1):
    return pl.BlockSpec((rows, d), lambda i: (0, 0))


def _prenorm(x, g, scale, shift, name):
    s, d = x.shape
    tm = _row_tile(s, 512)

    def body(x_ref, g_ref, sc_ref, sh_ref, h_ref):
        xv = x_ref[...]
        h = (xv * _rstd(xv) * g_ref[...]) * (1.0 + sc_ref[...]) + sh_ref[...]
        h_ref[...] = h.astype(bf16)

    return pl.pallas_call(
        body, name=name, grid=(s // tm,), in_specs=[_row_spec(tm, d)] + [_vec_spec(d)] * 3,
        out_specs=_row_spec(tm, d), out_shape=jax.ShapeDtypeStruct((s, d), bf16), compiler_params=_params(1),
    )(x, g, scale, shift)


def _postnorm_res(x, y, g, gate, name):
    s, d = x.shape
    tm = _row_tile(s, 512)

    def body(x_ref, y_ref, g_ref, gate_ref, o_ref):
        yv = y_ref[...]
        o_ref[...] = x_ref[...] + gate_ref[...] * (yv * _rstd(yv) * g_ref[...])

    return pl.pallas_call(
        body, name=name, grid=(s // tm,), in_specs=[_row_spec(tm, d)] * 2 + [_vec_spec(d)] * 2,
        out_specs=_row_spec(tm, d), out_shape=jax.ShapeDtypeStruct((s, d), f32), compiler_params=_params(1),
    )(x, y, g, gate)


def _rms_bwd(u, v, r):
    return r * u - v * (r * r * r) * jnp.mean(u * v, axis=-1, keepdims=True)


def _loss_tail(x, y, g, gate, target, name):
    s, d = x.shape
    tm = _row_tile(s, 512)

    def body(x_ref, y_ref, g_ref, gate_ref, t_ref, loss_ref, do_ref, dy_ref, vec_ref):
        @pl.when(pl.program_id(0) == 0)
        def _():
            loss_ref[...] = jnp.zeros_like(loss_ref)
            vec_ref[...] = jnp.zeros_like(vec_ref)
        yv = y_ref[...]
        r = _rstd(yv)
        yn = yv * r
        err = x_ref[...] + gate_ref[...] * (yn * g_ref[...]) - t_ref[...]
        loss_ref[...] += 0.5 * jnp.sum(jnp.mean(err * err, axis=-1, keepdims=True), axis=0, keepdims=True)
        dr = err / d
        do_ref[...] = dr
        dn = dr * gate_ref[...]
        vec_ref[0:1, :] += jnp.sum(dr * (yn * g_ref[...]), axis=0, keepdims=True)
        vec_ref[1:2, :] += jnp.sum(dn * yn, axis=0, keepdims=True)
        dy_ref[...] = _rms_bwd(dn * g_ref[...], yv, r).astype(bf16)

    return pl.pallas_call(
        body, name=name, grid=(s // tm,), in_specs=[_row_spec(tm, d)] * 2 + [_vec_spec(d)] * 2 + [_row_spec(tm, d)],
        out_specs=[_vec_spec(LANES), _row_spec(tm, d), _row_spec(tm, d), _vec_spec(d, 8)],
        out_shape=[jax.ShapeDtypeStruct((1, LANES), f32), jax.ShapeDtypeStruct((s, d), f32),
                   jax.ShapeDtypeStruct((s, d), bf16), jax.ShapeDtypeStruct((8, d), f32)],
        compiler_params=_params(1),
    )(x, y, g, gate, target)


def _postnorm_bwd(dres, y, g, gate, name):
    s, d = y.shape
    tm = _row_tile(s, 512)

    def body(dr_ref, y_ref, g_ref, gate_ref, dy_ref, vec_ref):
        @pl.when(pl.program_id(0) == 0)
        def _():
            vec_ref[...] = jnp.zeros_like(vec_ref)
        dr, yv = dr_ref[...], y_ref[...]
        r = _rstd(yv)
        yn = yv * r
        dn = dr * gate_ref[...]
        vec_ref[0:1, :] += jnp.sum(dr * (yn * g_ref[...]), axis=0, keepdims=True)
        vec_ref[1:2, :] += jnp.sum(dn * yn, axis=0, keepdims=True)
        dy_ref[...] = _rms_bwd(dn * g_ref[...], yv, r).astype(bf16)

    return pl.pallas_call(
        body, name=name, grid=(s // tm,), in_specs=[_row_spec(tm, d)] * 2 + [_vec_spec(d)] * 2,
        out_specs=[_row_spec(tm, d), _vec_spec(d, 8)],
        out_shape=[jax.ShapeDtypeStruct((s, d), bf16), jax.ShapeDtypeStruct((8, d), f32)], compiler_params=_params(1),
    )(dres, y, g, gate)


def _prenorm_bwd(dh, x, g, scale, dres, name):
    s, d = x.shape
    tm = _row_tile(s, 512)

    def body(dh_ref, x_ref, g_ref, sc_ref, dr_ref, dx_ref, vec_ref):
        @pl.when(pl.program_id(0) == 0)
        def _():
            vec_ref[...] = jnp.zeros_like(vec_ref)
        dhv, xv = dh_ref[...], x_ref[...]
        r = _rstd(xv)
        xn = xv * r
        dn = dhv * (1.0 + sc_ref[...])
        vec_ref[0:1, :] += jnp.sum(dhv, axis=0, keepdims=True)
        vec_ref[1:2, :] += jnp.sum(dhv * (xn * g_ref[...]), axis=0, keepdims=True)
        vec_ref[2:3, :] += jnp.sum(dn * xn, axis=0, keepdims=True)
        dx_ref[...] = dr_ref[...] + _rms_bwd(dn * g_ref[...], xv, r)

    return pl.pallas_call(
        body, name=name, grid=(s // tm,),
        in_specs=[_row_spec(tm, d)] * 2 + [_vec_spec(d)] * 2 + [_row_spec(tm, d)],
        out_specs=[_row_spec(tm, d), _vec_spec(d, 8)],
        out_shape=[jax.ShapeDtypeStruct((s, d), f32), jax.ShapeDtypeStruct((8, d), f32)], compiler_params=_params(1),
    )(dh, x, g, scale, dres)


def _lane():
    return lax.broadcasted_iota(jnp.int32, (1, LANES), 1)


def _rope_tables(pos_col, inv_freq, name):
    s = pos_col.shape[0]

    def body(p_ref, f_ref, cos_ref, sin_ref):
        ang = p_ref[...].astype(f32) * f_ref[...]
        first_half = (_lane() % HEAD_DIM) < HEAD_DIM // 2
        cos_ref[...] = jnp.cos(ang)
        sn = jnp.sin(ang)
        sin_ref[...] = jnp.where(first_half, -sn, sn)

    return pl.pallas_call(
        body, name=name, out_shape=[jax.ShapeDtypeStruct((s, LANES), f32)] * 2, compiler_params=_params(),
    )(pos_col, inv_freq)


def _swap_halves(v):
    first_half = (_lane() % HEAD_DIM) < HEAD_DIM // 2
    return jnp.where(first_half, pltpu.roll(v, LANES - HEAD_DIM // 2, axis=1), pltpu.roll(v, HEAD_DIM // 2, axis=1))


def _qkv_prep(qkv, cos, sin_s, name):
    s = qkv.shape[0]
    tm = _row_tile(s, 256)
    scale = 1.0 / math.sqrt(HEAD_DIM)

    def body(p_ref, c_ref, s_ref, qa_ref, ka_ref, va_ref, qb_ref, kb_ref, vb_ref):
        cs, sn = c_ref[...], s_ref[...]
        low = _lane() < HEAD_DIM

        def blk(j):
            return p_ref[:, j * LANES:(j + 1) * LANES]

        def rope(v):
            return v * cs + _swap_halves(v) * sn

        def expand(v):
            other = pltpu.roll(v, HEAD_DIM, axis=1)
            return jnp.where(low, v, other), jnp.where(low, other, v)

        for j in range(N_PAIRS):
            qa_ref[:, j * LANES:(j + 1) * LANES] = (rope(blk(j)) * scale).astype(bf16)
            qb_ref[:, j * LANES:(j + 1) * LANES] = (blk(6 + j) * scale).astype(bf16)
            kb_ref[:, j * LANES:(j + 1) * LANES] = blk(10 + j).astype(bf16)
            vb_ref[:, j * LANES:(j + 1) * LANES] = blk(14 + j).astype(bf16)
        k0, k1 = expand(rope(blk(4)))
        v0, v1 = expand(blk(5))
        for j in range(N_PAIRS):
            ka_ref[:, j * LANES:(j + 1) * LANES] = (k0 if j < 2 else k1).astype(bf16)
            va_ref[:, j * LANES:(j + 1) * LANES] = (v0 if j < 2 else v1).astype(bf16)

    hw = N_PAIRS * LANES
    return pl.pallas_call(
        body, name=name, grid=(s // tm,),
        in_specs=[_row_spec(tm, QKV_W), _row_spec(tm, LANES), _row_spec(tm, LANES)],
        out_specs=[_row_spec(tm, hw)] * 6, out_shape=[jax.ShapeDtypeStruct((s, hw), bf16)] * 6, compiler_params=_params(1),
    )(qkv, cos, sin_s)


def _qkv_prep_bwd(dqa_t, dka, dva, dqb_t, dkb, dvb, cos, sin_s, name):
    s = dka.shape[0]
    tm = _row_tile(s, 256)
    scale = 1.0 / math.sqrt(HEAD_DIM)
    hw = N_PAIRS * LANES
    t_spec = pl.BlockSpec((hw, tm), lambda i: (0, i))

    def body(dqa_ref, dka_ref, dva_ref, dqb_ref, dkb_ref, dvb_ref, c_ref, s_ref, o_ref):
        cs, sn = c_ref[...], s_ref[...]
        low = _lane() < HEAD_DIM

        def blk(ref, j):
            return ref[:, j * LANES:(j + 1) * LANES]

        def blk_t(ref, j):
            return ref[j * LANES:(j + 1) * LANES, :].T

        def unrope(v):
            return v * cs + _swap_halves(v * sn)

        def fold(ref):
            a, b = blk(ref, 0) + blk(ref, 1), blk(ref, 2) + blk(ref, 3)
            kv0 = a + pltpu.roll(a, HEAD_DIM, axis=1)
            kv1 = b + pltpu.roll(b, HEAD_DIM, axis=1)
            return jnp.where(low, kv0, kv1)

        for j in range(N_PAIRS):
            o_ref[:, j * LANES:(j + 1) * LANES] = (unrope(blk_t(dqa_ref, j)) * scale).astype(bf16)
            o_ref[:, (6 + j) * LANES:(7 + j) * LANES] = (blk_t(dqb_ref, j) * scale).astype(bf16)
            o_ref[:, (10 + j) * LANES:(11 + j) * LANES] = blk(dkb_ref, j).astype(bf16)
            o_ref[:, (14 + j) * LANES:(15 + j) * LANES] = blk(dvb_ref, j).astype(bf16)
        o_ref[:, 4 * LANES:5 * LANES] = unrope(fold(dka_ref)).astype(bf16)
        o_ref[:, 5 * LANES:6 * LANES] = fold(dva_ref).astype(bf16)

    return pl.pallas_call(
        body, name=name, grid=(s // tm,),
        in_specs=[t_spec, _row_spec(tm, hw), _row_spec(tm, hw), t_spec, _row_spec(tm, hw), _row_spec(tm, hw)] + [_row_spec(tm, LANES)] * 2,
        out_specs=_row_spec(tm, QKV_W), out_shape=jax.ShapeDtypeStruct((s, QKV_W), bf16), compiler_params=_params(1),
    )(dqa_t, dka, dva, dqb_t, dkb, dvb, cos, sin_s)


def _cumsum_rows(v, reverse=False):
    n = v.shape[0]
    row = lax.broadcasted_iota(jnp.int32, v.shape, 0)
    sh = 1
    while sh < n:
        if reverse:
            v = v + jnp.where(row < n - sh, pltpu.roll(v, n - sh, axis=0), 0.0)
        else:
            v = v + jnp.where(row >= sh, pltpu.roll(v, sh, axis=0), 0.0)
        sh *= 2
    return v


def _log_sigmoid(z):
    return jnp.minimum(z, 0.0) - jnp.log1p(jnp.exp(-jnp.abs(z)))


def _forget_prep(fl, bf_row, name):
    s = fl.shape[0]

    def body(f_ref, b_ref, cb_ref):
        cum = _cumsum_rows(_log_sigmoid(f_ref[...] + b_ref[...]))
        for h in range(N_HEADS):
            cb_ref[:, h * LANES:(h + 1) * LANES] = jnp.broadcast_to(cum[:, h:h + 1], (s, LANES))

    return pl.pallas_call(
        body, name=name, out_shape=jax.ShapeDtypeStruct((s, N_HEADS * LANES), f32), compiler_params=_params(),
    )(fl, bf_row)


def _forget_prep_bwd(rs, dcs, fl, bf_row, name):
    s = fl.shape[0]

    def body(r_ref, c_ref, f_ref, b_ref, df_ref, db_ref):
        eye = (lax.broadcasted_iota(jnp.int32, (N_HEADS, LANES), 0) == lax.broadcasted_iota(jnp.int32, (N_HEADS, LANES), 1)).astype(f32)
        dcum = lax.dot_general(r_ref[...], eye, _TN, precision=lax.Precision.HIGHEST, preferred_element_type=f32)
        for h in range(N_HEADS):
            dcum = dcum - jnp.where(_lane() == h, jnp.sum(c_ref[:, h * LANES:(h + 1) * LANES], axis=1, keepdims=True), 0.0)
        dlf = _cumsum_rows(dcum, reverse=True)
        z = f_ref[...] + b_ref[...]
        df = jnp.where(_lane() < N_HEADS, dlf * jax.nn.sigmoid(-z), 0.0)
        df_ref[...] = df.astype(bf16)
        db_ref[...] = jnp.zeros_like(db_ref)
        db_ref[0:1, :] = jnp.sum(df, axis=0, keepdims=True)

    return pl.pallas_call(
        body, name=name,
        out_shape=[jax.ShapeDtypeStruct((s, LANES), bf16), jax.ShapeDtypeStruct((8, LANES), f32)], compiler_params=_params(),
    )(rs, dcs, fl, bf_row)


def _tile_mask(n_keys, n_queries, off, window):
    shape = (n_keys, n_queries)
    d = lax.broadcasted_iota(jnp.int32, shape, 1) - lax.broadcasted_iota(jnp.int32, shape, 0) + off
    valid = d >= 0
    return jnp.logical_and(valid, d < window) if window else valid


def _wide(v, t):
    return jnp.concatenate([v] * (t // LANES), axis=1)


def _attn_fwd(q, k, v, name, *, cum_b=None, sink_rows=None, window=None, t=256):
    s = q.shape[0]
    t = _row_tile(s, t)
    fox, has_sink = cum_b is not None, sink_rows is not None
    assert not window or (window % LANES == 0 and LANES + window <= s)

    def body(*refs):
        q_ref, k_ref, v_ref = refs[:3]
        rest = list(refs[3:])
        cb_ref = rest.pop(0) if fox else None
        sink_ref = rest.pop(0) if has_sink else None
        o_ref, lse_ref = rest
        i = pl.program_id(1)
        low = _lane() < HEAD_DIM
        top = lax.broadcasted_iota(jnp.int32, (LANES, 1), 0) < HEAD_DIM
        q2 = q_ref[...]
        zero = jnp.zeros_like(q2)
        qms = (jnp.where(low, q2, zero), jnp.where(low, zero, q2))

        def tile(k0, n_keys, off, carry, masked, queries=slice(0, t)):
            nq = queries.stop - queries.start
            kblk, vblk = k_ref[pl.ds(k0, n_keys), :], v_ref[pl.ds(k0, n_keys), :]
            valid = _tile_mask(n_keys, nq, off, window) if masked else None
            out = []
            for h in range(2):
                m, l, acc = carry[h]
                sc = lax.dot_general(kblk, qms[h][queries], _NT, preferred_element_type=f32)
                if fox:
                    sc = sc - _wide(cb_ref[pl.ds(k0, n_keys), h * LANES:(h + 1) * LANES], nq)
                if masked:
                    sc = jnp.where(valid, sc, NEG)
                m_new = jnp.maximum(m, jnp.max(sc, axis=0, keepdims=True))
                p = jnp.exp(sc - m_new)
                alpha = jnp.exp(m - m_new)
                l = alpha * l + jnp.sum(p, axis=0, keepdims=True)
                acc = alpha * acc + lax.dot_general(vblk, p.astype(bf16), _TN, preferred_element_type=f32)
                out.append((m_new, l, acc))
            return tuple(out)

        def start(nq):
            if has_sink:
                return tuple((_wide(sink_ref[h:h + 1, :], nq), jnp.ones((1, nq), f32), jnp.zeros((LANES, nq), f32))
                             for h in range(2))
            return tuple((jnp.full((1, nq), NEG, f32), jnp.zeros((1, nq), f32), jnp.zeros((LANES, nq), f32)) for h in range(2))

        def finish(carry, queries):
            (m0, l0, a0), (m1, l1, a1) = carry
            o_t = jnp.where(top, a0 * (1.0 / l0), a1 * (1.0 / l1))
            o_ref[queries, :] = o_t.T.astype(bf16)
            lse_ref[0:1, queries] = m0 + jnp.log(l0)
            lse_ref[1:2, queries] = m1 + jnp.log(l1)

        if window:
            for c in range(t // LANES):
                queries = slice(c * LANES, (c + 1) * LANES)
                q0 = i * t + c * LANES
                k0 = pl.multiple_of(jnp.maximum(q0 - window, 0), LANES)
                finish(tile(k0, LANES + window, q0 - k0, start(LANES), True, queries), queries)
        else:
            carry = lax.fori_loop(0, i, lambda kb, c: tile(pl.multiple_of(kb * t, t), t, 0, c, False), start(t))
            finish(tile(pl.multiple_of(i * t, t), t, 0, carry, True), slice(0, t))

    q_spec = pl.BlockSpec((t, LANES), lambda j, i: (i, j))
    kv_spec = pl.BlockSpec((s, LANES), lambda j, i: (0, j))
    in_specs, args = [q_spec, kv_spec, kv_spec], [q, k, v]
    if fox:
        in_specs += [pl.BlockSpec((s, 2 * LANES), lambda j, i: (0, j))]
        args += [cum_b]
    if has_sink:
        in_specs += [pl.BlockSpec((None, 2, LANES), lambda j, i: (j, 0, 0))]
        args += [sink_rows.reshape(N_PAIRS, 2, LANES)]
    return pl.pallas_call(
        body, name=name, grid=(N_PAIRS, s // t), in_specs=in_specs,
        out_specs=[q_spec, pl.BlockSpec((None, 2, t), lambda j, i: (j, 0, i))],
        out_shape=[jax.ShapeDtypeStruct((s, N_PAIRS * LANES), bf16), jax.ShapeDtypeStruct((N_PAIRS, 2, s), f32)],
        compiler_params=_params(2),
    )(*args)


def _attn_delta(do, o, name, *, lse=None, sink_rows=None):
    s, hw = do.shape
    tm = _row_tile(s, 512)
    has_sink = sink_rows is not None

    def body(*refs):
        do_ref, o_ref = refs[:2]
        if has_sink:
            lse_ref, sink_ref, dl_ref, ds_ref = refs[2:]

            @pl.when(pl.program_id(0) == 0)
            def _():
                ds_ref[...] = jnp.zeros_like(ds_ref)
        else:
            dl_ref, = refs[2:]
        for j in range(N_PAIRS):
            cols = slice(j * LANES, (j + 1) * LANES)
            prod_t = (do_ref[:, cols].astype(f32) * o_ref[:, cols].astype(f32)).T
            for h in range(2):
                dl = jnp.sum(prod_t[h * HEAD_DIM:(h + 1) * HEAD_DIM, :], axis=0, keepdims=True)
                dl_ref[j, h:h + 1, :] = dl
                if has_sink:
                    r = 2 * j + h
                    p_sink = jnp.exp(sink_ref[r:r + 1, 0:1] - lse_ref[j, h:h + 1, :])
                    ds_ref[r:r + 1, :] += -jnp.sum(p_sink * dl, axis=1, keepdims=True)

    rows_spec = pl.BlockSpec((N_PAIRS, 2, tm), lambda i: (0, 0, i))
    in_specs, args = [_row_spec(tm, hw)] * 2, [do, o]
    out_specs, out_shape = [rows_spec], [jax.ShapeDtypeStruct((N_PAIRS, 2, s), f32)]
    if has_sink:
        in_specs += [rows_spec, _vec_spec(LANES, N_HEADS)]
        args += [lse, sink_rows]
        out_specs += [_vec_spec(LANES, N_HEADS)]
        out_shape += [jax.ShapeDtypeStruct((N_HEADS, LANES), f32)]
    return pl.pallas_call(
        body, name=name, grid=(s // tm,), in_specs=in_specs, out_specs=out_specs, out_shape=out_shape,
        compiler_params=_params(1),
    )(*args)


def _attn_bwd(q, k, v, do, lse, delta, name, *, cum_b=None, window=None, t=256):
    s = q.shape[0]
    t = _row_tile(s, t)
    nblk = s // t
    fox = cum_b is not None
    assert not window or (window % LANES == 0 and LANES + window <= s)

    def body(*refs):
        k_ref, v_ref, q_ref, do_ref, lse_ref, dl_ref = refs[:6]
        rest = list(refs[6:])
        cb_ref = rest.pop(0) if fox else None
        dq_ref, dk_ref, dv_ref = rest[:3]
        dcs_ref, rs_ref = (rest[3], rest[4]) if fox else (None, None)
        b = pl.program_id(1)
        k0 = pl.multiple_of(b * t, t)

        @pl.when(b == 0)
        def _():
            dq_ref[...] = jnp.zeros_like(dq_ref)
            if fox:
                rs_ref[...] = jnp.zeros_like(rs_ref)

        dk_ref[...] = jnp.zeros_like(dk_ref)
        dv_ref[...] = jnp.zeros_like(dv_ref)
        if fox:
            dcs_ref[...] = jnp.zeros_like(dcs_ref)
        low = _lane() < HEAD_DIM
        top = lax.broadcasted_iota(jnp.int32, (LANES, 1), 0) < HEAD_DIM
        kblk, vblk = k_ref[...], v_ref[...]
        k_t = kblk.astype(f32).T.astype(bf16)
        cks = [_wide(cb_ref[pl.ds(k0, t), h * LANES:(h + 1) * LANES], t) for h in range(2)] if fox else None

        def tile(q0, n_queries, off, masked, keys=slice(0, t)):
            cols = pl.ds(q0, n_queries)
            q2, do2 = q_ref[cols, :], do_ref[cols, :]
            zero = jnp.zeros_like(q2)
            valid = _tile_mask(keys.stop - keys.start, n_queries, off, window) if masked else None
            dq_parts = []
            for h in range(2):
                qm = jnp.where(low, q2, zero) if h == 0 else jnp.where(low, zero, q2)
                dom = jnp.where(low, do2, zero) if h == 0 else jnp.where(low, zero, do2)
                sc = lax.dot_general(kblk[keys], qm, _NT, preferred_element_type=f32)
                if fox:
                    sc = sc - cks[h]
                if masked:
                    sc = jnp.where(valid, sc, NEG)
                p = jnp.exp(sc - lse_ref[h:h + 1, cols])
                dp = lax.dot_general(vblk[keys], dom, _NT, preferred_element_type=f32)
                ds = p * (dp - dl_ref[h:h + 1, cols])
                pb, dsb = p.astype(bf16), ds.astype(bf16)
                dv_ref[keys, :] += jnp.dot(pb, dom, preferred_element_type=f32)
                dk_ref[keys, :] += jnp.dot(dsb, qm, preferred_element_type=f32)
                dq_parts.append(jnp.dot(k_t[:, keys], dsb, preferred_element_type=f32))
                if fox:
                    dcs_ref[:, h * LANES:(h + 1) * LANES] += sum(ds[:, g * LANES:(g + 1) * LANES] for g in range(t // LANES))
                    rs_ref[h:h + 1, cols] += jnp.sum(ds, axis=0, keepdims=True)
            dq_ref[:, cols] += jnp.where(top, dq_parts[0], dq_parts[1])

        def later_block(qb, carry):
            tile(pl.multiple_of(qb * t, t), t, 0, False)
            return carry

        if window:
            for c in range(t // LANES):
                first = b * t + c * LANES
                q0 = pl.multiple_of(jnp.minimum(first, s - (LANES + window)), LANES)
                tile(q0, LANES + window, q0 - first, True, slice(c * LANES, (c + 1) * LANES))
        else:
            tile(k0, t, 0, True)
            lax.fori_loop(b + 1, nblk, later_block, 0)

    kv_spec = pl.BlockSpec((t, LANES), lambda j, b: (b, j))
    seq_spec = pl.BlockSpec((s, LANES), lambda j, b: (0, j))
    rows_spec = pl.BlockSpec((None, 2, s), lambda j, b: (j, 0, 0))
    hw = N_PAIRS * LANES
    in_specs, args = [kv_spec, kv_spec, seq_spec, seq_spec, rows_spec, rows_spec], [k, v, q, do, lse, delta]
    out_specs = [pl.BlockSpec((LANES, s), lambda j, b: (j, 0)), kv_spec, kv_spec]
    out_shape = [jax.ShapeDtypeStruct((hw, s), f32), jax.ShapeDtypeStruct((s, hw), f32), jax.ShapeDtypeStruct((s, hw), f32)]
    if fox:
        in_specs += [pl.BlockSpec((s, 2 * LANES), lambda j, b: (0, j))]
        args += [cum_b]
        out_specs += [pl.BlockSpec((t, 2 * LANES), lambda j, b: (b, j)), rows_spec]
        out_shape += [jax.ShapeDtypeStruct((s, N_HEADS * LANES), f32), jax.ShapeDtypeStruct((N_PAIRS, 2, s), f32)]
    return pl.pallas_call(
        body, name=name, grid=(N_PAIRS, nblk), in_specs=in_specs, out_specs=out_specs, out_shape=out_shape,
        compiler_params=_params(2),
    )(*args)


def _merge(ba, bb, gl, name):
    s, d = ba.shape
    tm = _row_tile(s, 512)

    def body(a_ref, b_ref, g_ref, o_ref):
        g0, g1 = jax.nn.sigmoid(g_ref[:, :d].astype(f32)), jax.nn.sigmoid(g_ref[:, d:].astype(f32))
        o_ref[...] = (g0 * a_ref[...].astype(f32) + g1 * b_ref[...].astype(f32)).astype(bf16)

    return pl.pallas_call(
        body, name=name, grid=(s // tm,), in_specs=[_row_spec(tm, d)] * 2 + [_row_spec(tm, 2 * d)],
        out_specs=_row_spec(tm, d), out_shape=jax.ShapeDtypeStruct((s, d), bf16), compiler_params=_params(1),
    )(ba, bb, gl)


def _merge_bwd(dm, ba, bb, gl, name):
    s, d = ba.shape
    tm = _row_tile(s, 512)

    def body(dm_ref, a_ref, b_ref, g_ref, da_ref, db_ref, dg_ref):
        dmv = dm_ref[...].astype(f32)
        g0, g1 = jax.nn.sigmoid(g_ref[:, :d].astype(f32)), jax.nn.sigmoid(g_ref[:, d:].astype(f32))
        da_ref[...] = (dmv * g0).astype(bf16)
        db_ref[...] = (dmv * g1).astype(bf16)
        dg_ref[:, :d] = (dmv * a_ref[...].astype(f32) * (g0 * (1.0 - g0))).astype(bf16)
        dg_ref[:, d:] = (dmv * b_ref[...].astype(f32) * (g1 * (1.0 - g1))).astype(bf16)

    return pl.pallas_call(
        body, name=name, grid=(s // tm,), in_specs=[_row_spec(tm, d)] * 3 + [_row_spec(tm, 2 * d)],
        out_specs=[_row_spec(tm, d)] * 2 + [_row_spec(tm, 2 * d)],
        out_shape=[jax.ShapeDtypeStruct((s, d), bf16)] * 2 + [jax.ShapeDtypeStruct((s, 2 * d), bf16)],
        compiler_params=_params(1),
    )(dm, ba, bb, gl)


GLU_TILE = 256


def _ffn_in_swiglu(h, w_t, name):
    s, d = h.shape
    f = w_t.shape[0] // 2
    tm = _row_tile(s, 2048)
    tg = GLU_TILE
    nb = f // tg

    def body(h_ref, wg_ref, wu_ref, g_ref, u_ref, act_ref):
        hv = h_ref[...]
        g = lax.dot_general(hv, wg_ref[...], _NT, preferred_element_type=f32)
        u = lax.dot_general(hv, wu_ref[...], _NT, preferred_element_type=f32)
        g_ref[...] = g.astype(bf16)
        u_ref[...] = u.astype(bf16)
        act_ref[...] = (g * jax.nn.sigmoid(g) * u).astype(bf16)

    col = pl.BlockSpec((tm, tg), lambda i, j: (i, j))
    return pl.pallas_call(
        body, name=name, grid=(s // tm, nb),
        in_specs=[pl.BlockSpec((tm, d), lambda i, j: (i, 0)), pl.BlockSpec((tg, d), lambda i, j: (j, 0)),
                  pl.BlockSpec((tg, d), lambda i, j: (j + nb, 0))],
        out_specs=[col] * 3, out_shape=[jax.ShapeDtypeStruct((s, f), bf16)] * 3, compiler_params=_params(2),
    )(h, w_t, w_t)


def _ffn_out_dgrad_swiglu(dy, w_out, g, u, name):
    s, d = dy.shape
    f = g.shape[1]
    tm = _row_tile(s, 2048)
    tg = GLU_TILE

    def body(dy_ref, w_ref, g_ref, u_ref, dg_ref, du_ref):
        dv = lax.dot_general(dy_ref[...], w_ref[...], _NT, preferred_element_type=f32)
        gv, uv = g_ref[...].astype(f32), u_ref[...].astype(f32)
        sg = jax.nn.sigmoid(gv)
        dg_ref[...] = (dv * uv * (sg * (1.0 + gv * (1.0 - sg)))).astype(bf16)
        du_ref[...] = (dv * (gv * sg)).astype(bf16)

    col = pl.BlockSpec((tm, tg), lambda i, j: (i, j))
    return pl.pallas_call(
        body, name=name, grid=(s // tm, f // tg),
        in_specs=[pl.BlockSpec((tm, d), lambda i, j: (i, 0)), pl.BlockSpec((tg, d), lambda i, j: (j, 0)), col, col],
        out_specs=[col] * 2, out_shape=[jax.ShapeDtypeStruct((s, f), bf16)] * 2, compiler_params=_params(2),
    )(dy, w_out, g, u)


def _ffn_in_dgrad(dg, du, w_t, name, after=None):
    s, f = dg.shape
    d = w_t.shape[1]
    tm, tn = _matmul_tiles(s, d, 2 * f, dg.dtype.itemsize, w_t.dtype.itemsize, 4)

    def body(dg_ref, du_ref, wg_ref, wu_ref, *rest):
        rest[-1][...] = (jnp.dot(dg_ref[...], wg_ref[...], preferred_element_type=f32)
                         + jnp.dot(du_ref[...], wu_ref[...], preferred_element_type=f32))

    extra = [] if after is None else [after]
    rows = pl.BlockSpec((tm, f), lambda i, j: (i, 0))
    return pl.pallas_call(
        body, name=name, grid=(s // tm, d // tn),
        in_specs=[rows, rows, pl.BlockSpec((f, tn), lambda i, j: (0, j)), pl.BlockSpec((f, tn), lambda i, j: (1, j))]
        + [pl.BlockSpec(memory_space=pl.ANY)] * len(extra),
        out_specs=pl.BlockSpec((tm, tn), lambda i, j: (i, j)),
        out_shape=jax.ShapeDtypeStruct((s, d), f32), compiler_params=_params(2),
    )(dg, du, w_t, w_t, *extra)


def _ada_fwd(c_all, w, b, name):
    def body(c_ref, w_ref, b_ref, o_ref):
        o_ref[...] = jnp.dot(c_ref[...].astype(bf16), w_ref[...].astype(bf16), preferred_element_type=f32) + b_ref[...]

    return pl.pallas_call(
        body, name=name, out_shape=jax.ShapeDtypeStruct((c_all.shape[0], w.shape[1]), f32), compiler_params=_params(),
    )(c_all, w, b)


def _ada_wgrad(c_all, d_all, name):
    n, d = c_all.shape
    w = d_all.shape[1]

    def body(c_ref, d_ref, o_ref):
        eye = (lax.broadcasted_iota(jnp.int32, (n, n), 0) == lax.broadcasted_iota(jnp.int32, (n, n), 1)).astype(f32)
        ct = lax.dot_general(c_ref[...], eye, _TN, precision=lax.Precision.HIGHEST, preferred_element_type=f32)
        g = ct[:, 0:1] * d_ref[0:1, :]
        for bi in range(1, n):
            g = g + ct[:, bi:bi + 1] * d_ref[bi:bi + 1, :]
        o_ref[0] = g

    return pl.pallas_call(
        body, name=name, out_shape=jax.ShapeDtypeStruct((1, d, w), f32), compiler_params=_params(),
    )(c_all, d_all)


def _adamw(parts, w, m, v, name, mine=None):
    r, c = w.shape
    n_parts = parts.shape[0]
    row_tiles = [t for t in range(min(r, 256), 0, -1) if r % t == 0 and (t % 16 == 0 or t == r)]
    if row_tiles:
        tr, tc = row_tiles[0], c
    else:
        tr, tc = r, next(t for t in (256, LANES) if c % t == 0)

    def body(p_ref, *rest):
        own_ref = rest[0] if mine is not None else None
        w_ref, m_ref, v_ref, g_ref, d_ref, nm_ref, nv_ref = rest[-7:]
        if mine is not None:
            x, y, cc = _me()
            me = 4 * x + 2 * y + cc

        def part(i):
            if mine is None:
                return p_ref[i].astype(f32)
            return jnp.where(me == i, own_ref[i], p_ref[i]).astype(f32)

        g = part(0)
        for i in range(1, n_parts):
            g = g + part(i)
        mm = ADAM_B1 * m_ref[...] + (1.0 - ADAM_B1) * g
        vv = ADAM_B2 * v_ref[...] + (1.0 - ADAM_B2) * (g * g)
        m_hat = mm / (1.0 - ADAM_B1 ** ADAM_STEP)
        v_hat = vv / (1.0 - ADAM_B2 ** ADAM_STEP)
        g_ref[...] = g
        d_ref[...] = -ADAM_LR * (m_hat / (jnp.sqrt(v_hat) + ADAM_EPS) + ADAM_WD * w_ref[...])
        nm_ref[...] = mm
        nv_ref[...] = vv

    spec = pl.BlockSpec((tr, tc), lambda i, j: (i, j))
    stack = [parts] if mine is None else [parts, mine]
    return pl.pallas_call(
        body, name=name, grid=(r // tr, c // tc),
        in_specs=[pl.BlockSpec((n_parts, tr, tc), lambda i, j: (0, i, j))] * len(stack) + [spec] * 3,
        out_specs=[spec] * 4, out_shape=[jax.ShapeDtypeStruct((r, c), f32)] * 4, compiler_params=_params(2),
    )(*stack, w, m, v)


def _me():
    return lax.axis_index("x"), lax.axis_index("y"), lax.axis_index("c")


def _all_gather(arrays, name, vmem=False, after=None):
    n = len(arrays)
    space = pltpu.VMEM if vmem else pl.ANY
    extra = [] if after is None else [after]

    def body(*refs):
        ins = refs[:n]
        outs = refs[n + len(extra):2 * n + len(extra)]
        send_sems, recv_sems, local_sems = refs[2 * n + len(extra):]
        x, y, c = _me()
        me, sibling = (x, y, c), (x, y, 1 - c)
        chips = [(1 - x, y), (x, 1 - y), (1 - x, 1 - y)]

        def rows(a, dev):
            return outs[a].at[4 * dev[0] + 2 * dev[1] + dev[2]]

        def copy(a, k, block, to, src=None):
            return pltpu.make_async_remote_copy(
                src_ref=rows(a, block) if src is None else src, dst_ref=rows(a, block),
                send_sem=send_sems.at[a, k], recv_sem=recv_sems.at[a, k], device_id=to, device_id_type=MESH)

        mine = [pltpu.make_async_copy(ins[a], rows(a, me), local_sems.at[a]) for a in range(n)]
        for cp in mine:
            cp.start()
        first = []
        for a in range(n):
            first.append(copy(a, 0, me, sibling, src=ins[a]))
            first += [copy(a, 1 + j, me, (*chip, c), src=ins[a]) for j, chip in enumerate(chips)]
        for cp in first:
            cp.start()
        passed = []
        for j, chip in enumerate(chips):
            for a in range(n):
                copy(a, 1 + j, (*chip, c), me).wait_recv()
                fwd = copy(a, 4 + j, (*chip, c), sibling)
                fwd.start()
                passed.append(fwd)
        for a in range(n):
            copy(a, 0, sibling, me).wait_recv()
            for j, chip in enumerate(chips):
                copy(a, 4 + j, (*chip, 1 - c), me).wait_recv()
        for cp in first + passed:
            cp.wait_send()
        for cp in mine:
            cp.wait()

    outs = pl.pallas_call(
        body, name=name,
        in_specs=[pl.BlockSpec(memory_space=space)] * n + [pl.BlockSpec(memory_space=pl.ANY)] * len(extra),
        out_specs=[pl.BlockSpec(memory_space=space)] * n,
        out_shape=[jax.ShapeDtypeStruct((N_DEV,) + a.shape, a.dtype) for a in arrays],
        scratch_shapes=[pltpu.SemaphoreType.DMA((n, 7)), pltpu.SemaphoreType.DMA((n, 7)), pltpu.SemaphoreType.DMA((n,))],
        compiler_params=pltpu.CompilerParams(vmem_limit_bytes=VMEM_LIMIT),
    )(*arrays, *extra)
    return list(outs)


_FLIPS = ((0, 0, 1), (1, 0, 0), (0, 1, 0), (1, 1, 0), (1, 0, 1), (0, 1, 1), (1, 1, 1))
_HBM = pl.BlockSpec(memory_space=pltpu.HBM)
_SEM = pl.BlockSpec(memory_space=pltpu.SEMAPHORE)


def _exchange_copies(scatter, srcs, lands, send_sems, recv_sems):
    x, y, c = _me()
    me_row = 4 * x + 2 * y + c
    out = []
    for k, (fx, fy, fc) in enumerate(_FLIPS):
        peer = (x ^ fx, y ^ fy, c ^ fc)
        peer_row = 4 * peer[0] + 2 * peer[1] + peer[2]
        for a in range(len(srcs)):
            out.append(pltpu.make_async_remote_copy(
                src_ref=srcs[a].at[peer_row] if scatter else srcs[a], dst_ref=lands[a].at[me_row],
                send_sem=send_sems.at[7 * a + k], recv_sem=recv_sems.at[7 * a + k], device_id=peer, device_id_type=MESH))
    return out


def _exchange_start(arrays, scatter, name, after=None):
    n = len(arrays)
    lands = [lax.empty(a.shape if scatter else (N_DEV,) + a.shape, a.dtype) for a in arrays]
    extra = [] if after is None else [after]

    def body(*refs):
        srcs, zones = refs[:n], refs[n:2 * n]
        send_sems, recv_sems = refs[2 * n + len(extra)], refs[2 * n + len(extra) + 1]
        token = refs[-1]
        for cp in _exchange_copies(scatter, srcs, zones, send_sems, recv_sems):
            cp.start()
        token[...] = jnp.zeros_like(token)

    thru = [pltpu.HBM(a.shape, a.dtype) for a in list(arrays) + lands]
    outs = pl.pallas_call(
        body, name=name,
        out_shape=(pltpu.SemaphoreType.DMA((7 * n,)), pltpu.SemaphoreType.DMA((7 * n,)), *thru, jax.ShapeDtypeStruct((8, LANES), f32)),
        in_specs=[_HBM] * (2 * n) + [pl.BlockSpec(memory_space=pl.ANY)] * len(extra),
        out_specs=(_SEM, _SEM, *[_HBM] * (2 * n), pl.BlockSpec(memory_space=pltpu.VMEM)),
        input_output_aliases={i: 2 + i for i in range(2 * n)},
        compiler_params=pltpu.CompilerParams(has_side_effects=pltpu.SideEffectType.DATAFLOW_SIDE_EFFECTING),
    )(*[pltpu.with_memory_space_constraint(a, pltpu.HBM) for a in list(arrays) + lands], *extra)
    return dict(n=n, scatter=scatter, sems=outs[:2], srcs=outs[2:2 + n], lands=outs[2 + n:2 + 2 * n], token=outs[-1])


def _exchange_wait(handle, after, name):
    n, scatter = handle["n"], handle["scatter"]

    def body(*refs):
        srcs, zones = refs[:n], refs[n:2 * n]
        send_sems, recv_sems = refs[2 * n], refs[2 * n + 1]
        for cp in _exchange_copies(scatter, srcs, zones, send_sems, recv_sems):
            cp.wait_send()
            cp.wait_recv()

    thru = [pltpu.HBM(a.shape, a.dtype) for a in list(handle["srcs"]) + list(handle["lands"])]
    outs = pl.pallas_call(
        body, name=name, out_shape=tuple(thru),
        in_specs=[_HBM] * (2 * n) + [_SEM, _SEM, pl.BlockSpec(memory_space=pl.ANY)], out_specs=tuple([_HBM] * (2 * n)),
        input_output_aliases={i: i for i in range(2 * n)},
        compiler_params=pltpu.CompilerParams(has_side_effects=pltpu.SideEffectType.DATAFLOW_SIDE_EFFECTING),
    )(*handle["srcs"], *handle["lands"], *handle["sems"], after)
    return list(outs[n:])


def _cols_from_shards(g):
    return jnp.transpose(g, (1, 0, 2)).reshape(g.shape[1], -1)


def _shards_from_cols(a):
    return jnp.transpose(a.reshape(a.shape[0], N_DEV, -1), (1, 0, 2))


def _local_step(x, positions, ada, g_pre_mix, g_post_mix, b_f, sinks, g_pre_ffn, g_post_ffn, target,
                w_in_t, late_weights, on_grads):
    s, d = x.shape
    row = lambda v: v.reshape(1, -1)
    shift_m, scale_m, gate_m, shift_f, scale_f, gate_f = (ada[i:i + 1] for i in range(6))
    w_gate_t, w_qkv_t = w_in_t[F_OFF + N_HEADS:], w_in_t[:QKV_W]
    w_f_t = jnp.pad(w_in_t[F_OFF:F_OFF + N_HEADS], ((0, LANES - N_HEADS), (0, 0)))
    w_in_p_t = jnp.concatenate([w_gate_t, w_qkv_t, w_f_t], axis=0)
    bf_row = jnp.pad(row(b_f), ((0, 0), (0, LANES - N_HEADS)))
    sink_rows = jnp.broadcast_to(sinks.reshape(N_HEADS, 1).astype(f32), (N_HEADS, LANES))
    inv_freq = 1.0 / (ROPE_THETA ** (jnp.arange(0, HEAD_DIM, 2, dtype=f32) / HEAD_DIM))
    cos, sin_s = _rope_tables(positions.reshape(s, 1), jnp.tile(inv_freq, 4).reshape(1, LANES), "rope_tables")

    h1 = _prenorm(x, row(g_pre_mix), scale_m, shift_m, "prenorm_mix")
    gl = _matmul(h1, w_gate_t, "nt", bf16, "proj_gate")
    qkv = _matmul(h1, w_qkv_t, "nt", f32, "proj_qkv")
    fl = _matmul(h1, w_f_t, "nt", f32, "proj_forget")
    qa, ka, va, qb, kb, vb = _qkv_prep(qkv, cos, sin_s, "qkv_prep")
    cum_b = _forget_prep(fl, bf_row, "forget_prep")
    o_a, lse_a = _attn_fwd(qa, ka, va, "swa_fwd", sink_rows=sink_rows, window=WINDOW, t=512)
    o_b, lse_b = _attn_fwd(qb, kb, vb, "fox_fwd", cum_b=cum_b, t=512)
    w_branch_a, w_branch_b, w_out, w_ffn_in_t, w_ffn_out = late_weights(o_b)
    ba = _matmul(o_a, w_branch_a, "nn", bf16, "branch_a")
    bb = _matmul(o_b, w_branch_b, "nn", bf16, "branch_b")
    merged = _merge(ba, bb, gl, "merge")
    y1 = _matmul(merged, w_out, "nn", f32, "out_proj")
    x2 = _postnorm_res(x, y1, row(g_post_mix), gate_m, "postnorm_mix")

    h2 = _prenorm(x2, row(g_pre_ffn), scale_f, shift_f, "prenorm_ffn")
    g_ff, u_ff, act = _ffn_in_swiglu(h2, w_ffn_in_t, "ffn_in_swiglu")
    y2 = _matmul(act, w_ffn_out, "nn", f32, "ffn_out")
    loss_row, d_out, d_y2, vec_pf = _loss_tail(x2, y2, row(g_post_ffn), gate_f, target, "loss_tail")

    g_w_ffn_out = _matmul(act, d_y2, "tn", bf16, "ffn_out_wgrad")
    dg_ff, du_ff = _ffn_out_dgrad_swiglu(d_y2, w_ffn_out, g_ff, u_ff, "ffn_out_dgrad_swiglu")
    g_w_ffn_in_t = jnp.concatenate([_matmul(dg_ff, h2, "tn", bf16, "ffn_gate_wgrad"),
                                    _matmul(du_ff, h2, "tn", bf16, "ffn_up_wgrad")], axis=0)
    sent = on_grads(dict(w_ffn_in=g_w_ffn_in_t, w_ffn_out=g_w_ffn_out))
    d_h2 = _ffn_in_dgrad(dg_ff, du_ff, w_ffn_in_t, "ffn_in_dgrad", after=sent)
    d_x2, vec_nf = _prenorm_bwd(d_h2, x2, row(g_pre_ffn), scale_f, d_out, "prenorm_ffn_bwd")

    d_y1, vec_pm = _postnorm_bwd(d_x2, y1, row(g_post_mix), gate_m, "postnorm_mix_bwd")
    g_w_out = _matmul(merged, d_y1, "tn", bf16, "out_proj_wgrad")
    d_merged = _matmul(d_y1, w_out, "nt", bf16, "out_proj_dgrad")
    d_ba, d_bb, dgl = _merge_bwd(d_merged, ba, bb, gl, "merge_bwd")
    g_w_branch_a = _matmul(o_a, d_ba, "tn", bf16, "branch_a_wgrad")
    g_w_branch_b = _matmul(o_b, d_bb, "tn", bf16, "branch_b_wgrad")
    sent = on_grads(dict(w_out=g_w_out, w_branch_a=g_w_branch_a, w_branch_b=g_w_branch_b))
    d_oa = _matmul(d_ba, w_branch_a, "nt", bf16, "branch_a_dgrad", after=sent)
    d_ob = _matmul(d_bb, w_branch_b, "nt", bf16, "branch_b_dgrad", after=sent)
    delta_a, d_sink = _attn_delta(d_oa, o_a, "swa_delta", lse=lse_a, sink_rows=sink_rows)
    delta_b, = _attn_delta(d_ob, o_b, "fox_delta")
    dqa_t, dka, dva = _attn_bwd(qa, ka, va, d_oa, lse_a, delta_a, "swa_bwd", window=WINDOW, t=512)
    dqb_t, dkb, dvb, dcs, rs = _attn_bwd(qb, kb, vb, d_ob, lse_b, delta_b, "fox_bwd", cum_b=cum_b, t=512)
    dqkv = _qkv_prep_bwd(dqa_t, dka, dva, dqb_t, dkb, dvb, cos, sin_s, "qkv_prep_bwd")
    dfl, vec_bf = _forget_prep_bwd(rs.reshape(N_HEADS, s), dcs, fl, bf_row, "forget_prep_bwd")
    dproj = jnp.concatenate([dgl, dqkv, dfl], axis=1)
    g_w_in_p_t = _matmul(dproj, h1, "tn", bf16, "in_proj_wgrad")
    g_w_in_t = jnp.concatenate([g_w_in_p_t[GATE_W:GATE_W + QKV_W], g_w_in_p_t[GATE_W + QKV_W:GATE_W + QKV_W + N_HEADS],
                                g_w_in_p_t[:GATE_W]], axis=0)
    sent = on_grads(dict(w_in=g_w_in_t))
    d_h1 = _matmul(dproj, w_in_p_t, "nn", f32, "in_proj_dgrad", after=sent)
    grad_x, vec_nm = _prenorm_bwd(d_h1, x, row(g_pre_mix), scale_m, d_x2, "prenorm_mix_bwd")

    d_ada = jnp.concatenate([vec_nm[0], vec_nm[1], vec_pm[0], vec_nf[0], vec_nf[1], vec_pf[0]])
    small = dict(b_ada=d_ada, g_pre_mix=vec_nm[2], g_post_mix=vec_pm[1], g_pre_ffn=vec_nf[2], g_post_ffn=vec_pf[1],
                 b_f=vec_bf[0, :N_HEADS], sinks=d_sink[:, 0], loss=loss_row[0, :1])
    return grad_x, small


_SMALL = (("b_ada", 6144), ("g_pre_mix", 1024), ("g_post_mix", 1024), ("g_pre_ffn", 1024), ("g_post_ffn", 1024),
          ("b_f", 128), ("sinks", 128), ("loss", 128))
_SMALL_ROWS = 88


def _pack_small(vals):
    parts = [jnp.pad(vals[k].reshape(-1).astype(f32), (0, n - vals[k].size)) for k, n in _SMALL]
    flat = jnp.concatenate(parts)
    return jnp.pad(flat, (0, _SMALL_ROWS * LANES - flat.size)).reshape(_SMALL_ROWS, LANES)


def _unpack_small(slab, shapes):
    flat, out, off = slab.reshape(-1), {}, 0
    for k, n in _SMALL:
        size = math.prod(shapes[k])
        out[k] = flat[off:off + size].reshape(shapes[k])
        off += n
    return out


def kernel(x, c, positions, w_ada, b_ada, g_pre_mix, g_post_mix, w_in, b_f, sinks, w_branch_a, w_branch_b, w_out, g_pre_ffn, g_post_ffn, w_ffn_in, w_ffn_out, loss_target, m_w_ada, m_b_ada, m_g_pre_mix, m_g_post_mix, m_w_in, m_b_f, m_sinks, m_w_branch_a, m_w_branch_b, m_w_out, m_g_pre_ffn, m_g_post_ffn, m_w_ffn_in, m_w_ffn_out, v_w_ada, v_b_ada, v_g_pre_mix, v_g_post_mix, v_w_in, v_b_f, v_sinks, v_w_branch_a, v_w_branch_b, v_w_out, v_g_pre_ffn, v_g_post_ffn, v_w_ffn_in, v_w_ffn_out):
    xi, yi, ci = _me()
    me = 4 * xi + 2 * yi + ci
    d = D_MODEL
    ada_w = w_ada.shape[2]

    c_all, = _all_gather([c], "gather_c", vmem=True)
    c_all = c_all.reshape(N_DEV, d)
    b_mine = lax.dynamic_slice(b_ada, (0, me * ada_w), (1, ada_w))
    ada_cols = _ada_fwd(c_all, w_ada[0], b_mine, "ada_fwd")
    ada_all, = _all_gather([ada_cols], "gather_ada", vmem=True)
    ada = lax.dynamic_index_in_dim(ada_all, me, axis=1, keepdims=False).reshape(6, d)

    transposed = ("w_in", "w_ffn_in")
    tr = lambda a: jnp.transpose(a[0])

    g_in, = _all_gather([tr(w_in).astype(bf16)], "gather_w_in")
    late = [w.astype(bf16) for w in (w_branch_a[0], w_branch_b[0], w_out[0], tr(w_ffn_in), w_ffn_out[0])]
    late_h = _exchange_start(late, False, "gather_late_start", after=g_in)

    def mine_into(zone, block):
        return lax.dynamic_update_index_in_dim(zone, block, me, 0)

    def rows_from_shards(g):
        return g.reshape(g.shape[0] * g.shape[1], g.shape[2])

    def late_weights(after):
        zones = _exchange_wait(late_h, after, "gather_late_wait")
        g_ba, g_bb, g_out, g_fi, g_fo = (mine_into(z, w) for z, w in zip(zones, late))
        return (_cols_from_shards(g_ba), _cols_from_shards(g_bb), rows_from_shards(g_out), rows_from_shards(g_fi),
                rows_from_shards(g_fo))

    row_sharded = ("w_out", "w_ffn_out") + transposed
    in_flight = []

    def on_grads(group):
        sends = [g.reshape(N_DEV, g.shape[0] // N_DEV, g.shape[1]) if nm in row_sharded else _shards_from_cols(g)
                 for nm, g in group.items()]
        handle = _exchange_start(sends, True, "scatter_start_%d" % len(in_flight))
        in_flight.append((list(group), sends, handle))
        return handle["token"]

    grad_x, small = _local_step(
        x[0], positions[0], ada + late_h["token"][0, 0], g_pre_mix[0], g_post_mix[0], b_f[0], sinks[0], g_pre_ffn[0],
        g_post_ffn[0], loss_target[0], rows_from_shards(g_in), late_weights, on_grads)

    ws = dict(w_in=(w_in, m_w_in, v_w_in), w_branch_a=(w_branch_a, m_w_branch_a, v_w_branch_a),
              w_branch_b=(w_branch_b, m_w_branch_b, v_w_branch_b), w_out=(w_out, m_w_out, v_w_out),
              w_ffn_in=(w_ffn_in, m_w_ffn_in, v_w_ffn_in), w_ffn_out=(w_ffn_out, m_w_ffn_out, v_w_ffn_out))
    res = {}

    def finish_group(gi, after):
        names, sends, handle = in_flight[gi]
        zones = _exchange_wait(handle, after, "scatter_wait_%d" % gi)
        for nm, zone, sent in zip(names, zones, sends):
            w, m, v = (tr(a) if nm in transposed else a[0] for a in ws[nm])
            out = _adamw(zone, w, m, v, "adamw_" + nm, mine=sent)
            after = out[0]
            res[nm] = [jnp.transpose(o) for o in out] if nm in transposed else out
        return after

    done = finish_group(1, finish_group(0, grad_x))

    slab_all, = _all_gather([_pack_small(small)], "gather_small", vmem=True, after=done)
    small_w = dict(b_ada=b_ada, g_pre_mix=g_pre_mix, g_post_mix=g_post_mix, g_pre_ffn=g_pre_ffn, g_post_ffn=g_post_ffn,
                   b_f=b_f, sinks=sinks, loss=jnp.zeros((1,), f32))
    small_m = dict(b_ada=m_b_ada, g_pre_mix=m_g_pre_mix, g_post_mix=m_g_post_mix, g_pre_ffn=m_g_pre_ffn,
                   g_post_ffn=m_g_post_ffn, b_f=m_b_f, sinks=m_sinks, loss=jnp.zeros((1,), f32))
    small_v = dict(b_ada=v_b_ada, g_pre_mix=v_g_pre_mix, g_post_mix=v_g_post_mix, g_pre_ffn=v_g_pre_ffn,
                   g_post_ffn=v_g_post_ffn, b_f=v_b_f, sinks=v_sinks, loss=jnp.ones((1,), f32))
    shapes = {k: small_w[k].shape for k, _ in _SMALL}
    s_out = _adamw(slab_all, _pack_small(small_w), _pack_small(small_m), _pack_small(small_v), "adamw_small")
    s_grad, s_delta, s_m, s_v = (_unpack_small(o, shapes) for o in s_out)

    d_ada_all = lax.dynamic_slice(slab_all[:, :6144 // LANES, :].reshape(N_DEV, 6144), (0, me * ada_w), (N_DEV, ada_w))
    ada_parts = _ada_wgrad(c_all, d_ada_all, "ada_wgrad")

    res["w_ada"] = _adamw(ada_parts, w_ada[0], m_w_ada[0], v_w_ada[0], "adamw_w_ada")
    finish_group(2, res["w_ada"][0])

    order = ["w_ada", "b_ada", "g_pre_mix", "g_post_mix", "w_in", "b_f", "sinks", "w_branch_a", "w_branch_b", "w_out",
             "g_pre_ffn", "g_post_ffn", "w_ffn_in", "w_ffn_out"]
    outs = [s_grad["loss"].reshape(()), grad_x[None]]
    for which, small_o in enumerate((s_grad, s_delta, s_m, s_v)):
        for nm in order:
            outs.append(res[nm][which][None] if nm in res else small_o[nm])
    return tuple(outs)
```

```python
import functools
import math

import jax
import jax.numpy as jnp
from jax import lax
from jax.experimental import pallas as pl
from jax.experimental.pallas import tpu as pltpu

f32 = jnp.float32
bf16 = jnp.bfloat16

D_MODEL = 1024
HEAD_DIM = 64
N_HEADS = 8
N_PAIRS = 4
QKV_W = 2304
GATE_W = 2048
F_OFF = 2304
IN_W = 4360
WINDOW = 128
ROPE_THETA = 10000.0
RMS_EPS = 1e-6
D_FF = 2816
N_DEV = 8
ADAM_LR, ADAM_B1, ADAM_B2, ADAM_EPS, ADAM_WD, ADAM_STEP = 0.001, 0.9, 0.999, 1e-08, 0.01, 10
NEG = -1e30
LANES = 128
VMEM_LIMIT = 48 * 1024 * 1024
MESH = pl.DeviceIdType.MESH

_NT = (((1,), (1,)), ((), ()))
_TN = (((0,), (0,)), ((), ()))


def _params(n_grid=0):
    sem = ("arbitrary",) * n_grid if n_grid else None
    return pltpu.CompilerParams(dimension_semantics=sem, vmem_limit_bytes=VMEM_LIMIT)


def _row_tile(s, want):
    t = min(s, want)
    assert s % t == 0, (s, t)
    return t


MATMUL_VMEM_BUDGET = 40 * 1024 * 1024


def _matmul_tiles(m, n, k, a_item, b_item, o_item):
    def tiles(d):
        return [t for t in range(LANES, min(d, 2048) + 1, LANES) if d % t == 0] or [d]

    best = None
    for tm in tiles(m):
        for tn in tiles(n):
            vmem = 2 * (tm * k * a_item + tn * k * b_item + tm * tn * o_item) + tm * tn * 4
            if vmem > MATMUL_VMEM_BUDGET:
                continue
            traffic = m * k * a_item + n * k * b_item * (1 if tn == n else m // tm) + m * n * o_item
            steps = (m // tm) * (n // tn)
            key = (traffic, 0, steps) if steps >= 4 else (traffic, 1, -steps)
            if best is None or key < best[0]:
                best = (key, tm, tn)
    assert best is not None, (m, n, k)
    return best[1], best[2]


def _matmul(a, b, mode, out_dtype, name, after=None):
    if mode == "nn":
        (m, k), n = a.shape, b.shape[1]
    elif mode == "nt":
        (m, k), n = a.shape, b.shape[0]
    else:
        (k, m), n = a.shape, b.shape[1]
    tm, tn = _matmul_tiles(m, n, k, a.dtype.itemsize, b.dtype.itemsize, jnp.dtype(out_dtype).itemsize)
    if mode == "nn":
        a_spec, b_spec, dims = pl.BlockSpec((tm, k), lambda i, j: (i, 0)), pl.BlockSpec((k, tn), lambda i, j: (0, j)), None
    elif mode == "nt":
        a_spec, b_spec, dims = pl.BlockSpec((tm, k), lambda i, j: (i, 0)), pl.BlockSpec((tn, k), lambda i, j: (j, 0)), _NT
    else:
        a_spec, b_spec, dims = pl.BlockSpec((k, tm), lambda i, j: (0, i)), pl.BlockSpec((k, tn), lambda i, j: (0, j)), _TN

    def body(a_ref, b_ref, *rest):
        o_ref = rest[-1]
        av, bv = a_ref[...].astype(bf16), b_ref[...].astype(bf16)
        if dims is None:
            r = jnp.dot(av, bv, preferred_element_type=f32)
        else:
            r = lax.dot_general(av, bv, dims, preferred_element_type=f32)
        o_ref[...] = r.astype(out_dtype)

    extra = [] if after is None else [after]
    return pl.pallas_call(
        body, name=name, grid=(m // tm, n // tn), in_specs=[a_spec, b_spec] + [pl.BlockSpec(memory_space=pl.ANY)] * len(extra),
        out_specs=pl.BlockSpec((tm, tn), lambda i, j: (i, j)),
        out_shape=jax.ShapeDtypeStruct((m, n), out_dtype), compiler_params=_params(2),
    )(a, b, *extra)


def _rstd(v):
    return lax.rsqrt(jnp.mean(v * v, axis=-1, keepdims=True) + RMS_EPS)


def _row_spec(tm, d):
    return pl.BlockSpec((tm, d), lambda i: (i, 0))


def _vec_spec(d, rows=1):
    return pl.BlockSpec((rows, d), lambda i: (0, 0))


def _prenorm(x, g, scale, shift, name):
    s, d = x.shape
    tm = _row_tile(s, 512)

    def body(x_ref, g_ref, sc_ref, sh_ref, h_ref):
        xv = x_ref[...]
        h = (xv * _rstd(xv) * g_ref[...]) * (1.0 + sc_ref[...]) + sh_ref[...]
        h_ref[...] = h.astype(bf16)

    return pl.pallas_call(
        body, name=name, grid=(s // tm,), in_specs=[_row_spec(tm, d)] + [_vec_spec(d)] * 3,
        out_specs=_row_spec(tm, d), out_shape=jax.ShapeDtypeStruct((s, d), bf16), compiler_params=_params(1),
    )(x, g, scale, shift)


def _postnorm_res(x, y, g, gate, name):
    s, d = x.shape
    tm = _row_tile(s, 512)

    def body(x_ref, y_ref, g_ref, gate_ref, o_ref):
        yv = y_ref[...]
        o_ref[...] = x_ref[...] + gate_ref[...] * (yv * _rstd(yv) * g_ref[...])

    return pl.pallas_call(
        body, name=name, grid=(s // tm,), in_specs=[_row_spec(tm, d)] * 2 + [_vec_spec(d)] * 2,
        out_specs=_row_spec(tm, d), out_shape=jax.ShapeDtypeStruct((s, d), f32), compiler_params=_params(1),
    )(x, y, g, gate)


def _rms_bwd(u, v, r):
    return r * u - v * (r * r * r) * jnp.mean(u * v, axis=-1, keepdims=True)


def _loss_tail(x, y, g, gate, target, name):
    s, d = x.shape
    tm = _row_tile(s, 512)

    def body(x_ref, y_ref, g_ref, gate_ref, t_ref, loss_ref, do_ref, dy_ref, vec_ref):
        @pl.when(pl.program_id(0) == 0)
        def _():
            loss_ref[...] = jnp.zeros_like(loss_ref)
            vec_ref[...] = jnp.zeros_like(vec_ref)
        yv = y_ref[...]
        r = _rstd(yv)
        yn = yv * r
        err = x_ref[...] + gate_ref[...] * (yn * g_ref[...]) - t_ref[...]
        loss_ref[...] += 0.5 * jnp.sum(jnp.mean(err * err, axis=-1, keepdims=True), axis=0, keepdims=True)
        dr = err / d
        do_ref[...] = dr
        dn = dr * gate_ref[...]
        vec_ref[0:1, :] += jnp.sum(dr * (yn * g_ref[...]), axis=0, keepdims=True)
        vec_ref[1:2, :] += jnp.sum(dn * yn, axis=0, keepdims=True)
        dy_ref[...] = _rms_bwd(dn * g_ref[...], yv, r).astype(bf16)

    return pl.pallas_call(
        body, name=name, grid=(s // tm,), in_specs=[_row_spec(tm, d)] * 2 + [_vec_spec(d)] * 2 + [_row_spec(tm, d)],
        out_specs=[_vec_spec(LANES), _row_spec(tm, d), _row_spec(tm, d), _vec_spec(d, 8)],
        out_shape=[jax.ShapeDtypeStruct((1, LANES), f32), jax.ShapeDtypeStruct((s, d), f32),
                   jax.ShapeDtypeStruct((s, d), bf16), jax.ShapeDtypeStruct((8, d), f32)],
        compiler_params=_params(1),
    )(x, y, g, gate, target)


def _postnorm_bwd(dres, y, g, gate, name):
    s, d = y.shape
    tm = _row_tile(s, 512)

    def body(dr_ref, y_ref, g_ref, gate_ref, dy_ref, vec_ref):
        @pl.when(pl.program_id(0) == 0)
        def _():
            vec_ref[...] = jnp.zeros_like(vec_ref)
        dr, yv = dr_ref[...], y_ref[...]
        r = _rstd(yv)
        yn = yv * r
        dn = dr * gate_ref[...]
        vec_ref[0:1, :] += jnp.sum(dr * (yn * g_ref[...]), axis=0, keepdims=True)
        vec_ref[1:2, :] += jnp.sum(dn * yn, axis=0, keepdims=True)
        dy_ref[...] = _rms_bwd(dn * g_ref[...], yv, r).astype(bf16)

    return pl.pallas_call(
        body, name=name, grid=(s // tm,), in_specs=[_row_spec(tm, d)] * 2 + [_vec_spec(d)] * 2,
        out_specs=[_row_spec(tm, d), _vec_spec(d, 8)],
        out_shape=[jax.ShapeDtypeStruct((s, d), bf16), jax.ShapeDtypeStruct((8, d), f32)], compiler_params=_params(1),
    )(dres, y, g, gate)


def _prenorm_bwd(dh, x, g, scale, dres, name):
    s, d = x.shape
    tm = _row_tile(s, 512)

    def body(dh_ref, x_ref, g_ref, sc_ref, dr_ref, dx_ref, vec_ref):
        @pl.when(pl.program_id(0) == 0)
        def _():
            vec_ref[...] = jnp.zeros_like(vec_ref)
        dhv, xv = dh_ref[...], x_ref[...]
        r = _rstd(xv)
        xn = xv * r
        dn = dhv * (1.0 + sc_ref[...])
        vec_ref[0:1, :] += jnp.sum(dhv, axis=0, keepdims=True)
        vec_ref[1:2, :] += jnp.sum(dhv * (xn * g_ref[...]), axis=0, keepdims=True)
        vec_ref[2:3, :] += jnp.sum(dn * xn, axis=0, keepdims=True)
        dx_ref[...] = dr_ref[...] + _rms_bwd(dn * g_ref[...], xv, r)

    return pl.pallas_call(
        body, name=name, grid=(s // tm,),
        in_specs=[_row_spec(tm, d)] * 2 + [_vec_spec(d)] * 2 + [_row_spec(tm, d)],
        out_specs=[_row_spec(tm, d), _vec_spec(d, 8)],
        out_shape=[jax.ShapeDtypeStruct((s, d), f32), jax.ShapeDtypeStruct((8, d), f32)], compiler_params=_params(1),
    )(dh, x, g, scale, dres)


def _lane():
    return lax.broadcasted_iota(jnp.int32, (1, LANES), 1)


def _rope_tables(pos_col, inv_freq, name):
    s = pos_col.shape[0]

    def body(p_ref, f_ref, cos_ref, sin_ref):
        ang = p_ref[...].astype(f32) * f_ref[...]
        first_half = (_lane() % HEAD_DIM) < HEAD_DIM // 2
        cos_ref[...] = jnp.cos(ang)
        sn = jnp.sin(ang)
        sin_ref[...] = jnp.where(first_half, -sn, sn)

    return pl.pallas_call(
        body, name=name, out_shape=[jax.ShapeDtypeStruct((s, LANES), f32)] * 2, compiler_params=_params(),
    )(pos_col, inv_freq)


def _swap_halves(v):
    first_half = (_lane() % HEAD_DIM) < HEAD_DIM // 2
    return jnp.where(first_half, pltpu.roll(v, LANES - HEAD_DIM // 2, axis=1), pltpu.roll(v, HEAD_DIM // 2, axis=1))


def _qkv_prep(qkv, cos, sin_s, name):
    s = qkv.shape[0]
    tm = _row_tile(s, 256)
    scale = 1.0 / math.sqrt(HEAD_DIM)

    def body(p_ref, c_ref, s_ref, qa_ref, ka_ref, va_ref, qb_ref, kb_ref, vb_ref):
        cs, sn = c_ref[...], s_ref[...]
        low = _lane() < HEAD_DIM

        def blk(j):
            return p_ref[:, j * LANES:(j + 1) * LANES]

        def rope(v):
            return v * cs + _swap_halves(v) * sn

        def expand(v):
            other = pltpu.roll(v, HEAD_DIM, axis=1)
            return jnp.where(low, v, other), jnp.where(low, other, v)

        for j in range(N_PAIRS):
            qa_ref[:, j * LANES:(j + 1) * LANES] = (rope(blk(j)) * scale).astype(bf16)
            qb_ref[:, j * LANES:(j + 1) * LANES] = (blk(6 + j) * scale).astype(bf16)
            kb_ref[:, j * LANES:(j + 1) * LANES] = blk(10 + j).astype(bf16)
            vb_ref[:, j * LANES:(j + 1) * LANES] = blk(14 + j).astype(bf16)
        k0, k1 = expand(rope(blk(4)))
        v0, v1 = expand(blk(5))
        for j in range(N_PAIRS):
            ka_ref[:, j * LANES:(j + 1) * LANES] = (k0 if j < 2 else k1).astype(bf16)
            va_ref[:, j * LANES:(j + 1) * LANES] = (v0 if j < 2 else v1).astype(bf16)

    hw = N_PAIRS * LANES
    return pl.pallas_call(
        body, name=name, grid=(s // tm,),
        in_specs=[_row_spec(tm, QKV_W), _row_spec(tm, LANES), _row_spec(tm, LANES)],
        out_specs=[_row_spec(tm, hw)] * 6, out_shape=[jax.ShapeDtypeStruct((s, hw), bf16)] * 6, compiler_params=_params(1),
    )(qkv, cos, sin_s)


def _qkv_prep_bwd(dqa_t, dka, dva, dqb_t, dkb, dvb, cos, sin_s, name):
    s = dka.shape[0]
    tm = _row_tile(s, 256)
    scale = 1.0 / math.sqrt(HEAD_DIM)
    hw = N_PAIRS * LANES
    t_spec = pl.BlockSpec((hw, tm), lambda i: (0, i))

    def body(dqa_ref, dka_ref, dva_ref, dqb_ref, dkb_ref, dvb_ref, c_ref, s_ref, o_ref):
        cs, sn = c_ref[...], s_ref[...]
        low = _lane() < HEAD_DIM

        def blk(ref, j):
            return ref[:, j * LANES:(j + 1) * LANES]

        def blk_t(ref, j):
            return ref[j * LANES:(j + 1) * LANES, :].T

        def unrope(v):
            return v * cs + _swap_halves(v * sn)

        def fold(ref):
            a, b = blk(ref, 0) + blk(ref, 1), blk(ref, 2) + blk(ref, 3)
            kv0 = a + pltpu.roll(a, HEAD_DIM, axis=1)
            kv1 = b + pltpu.roll(b, HEAD_DIM, axis=1)
            return jnp.where(low, kv0, kv1)

        for j in range(N_PAIRS):
            o_ref[:, j * LANES:(j + 1) * LANES] = (unrope(blk_t(dqa_ref, j)) * scale).astype(bf16)
            o_ref[:, (6 + j) * LANES:(7 + j) * LANES] = (blk_t(dqb_ref, j) * scale).astype(bf16)
            o_ref[:, (10 + j) * LANES:(11 + j) * LANES] = blk(dkb_ref, j).astype(bf16)
            o_ref[:, (14 + j) * LANES:(15 + j) * LANES] = blk(dvb_ref, j).astype(bf16)
        o_ref[:, 4 * LANES:5 * LANES] = unrope(fold(dka_ref)).astype(bf16)
        o_ref[:, 5 * LANES:6 * LANES] = fold(dva_ref).astype(bf16)

    return pl.pallas_call(
        body, name=name, grid=(s // tm,),
        in_specs=[t_spec, _row_spec(tm, hw), _row_spec(tm, hw), t_spec, _row_spec(tm, hw), _row_spec(tm, hw)] + [_row_spec(tm, LANES)] * 2,
        out_specs=_row_spec(tm, QKV_W), out_shape=jax.ShapeDtypeStruct((s, QKV_W), bf16), compiler_params=_params(1),
    )(dqa_t, dka, dva, dqb_t, dkb, dvb, cos, sin_s)


def _cumsum_rows(v, reverse=False):
    n = v.shape[0]
    row = lax.broadcasted_iota(jnp.int32, v.shape, 0)
    sh = 1
    while sh < n:
        if reverse:
            v = v + jnp.where(row < n - sh, pltpu.roll(v, n - sh, axis=0), 0.0)
        else:
            v = v + jnp.where(row >= sh, pltpu.roll(v, sh, axis=0), 0.0)
        sh *= 2
    return v


def _log_sigmoid(z):
    return jnp.minimum(z, 0.0) - jnp.log1p(jnp.exp(-jnp.abs(z)))


def _forget_prep(fl, bf_row, name):
    s = fl.shape[0]

    def body(f_ref, b_ref, cb_ref):
        cum = _cumsum_rows(_log_sigmoid(f_ref[...] + b_ref[...]))
        for h in range(N_HEADS):
            cb_ref[:, h * LANES:(h + 1) * LANES] = jnp.broadcast_to(cum[:, h:h + 1], (s, LANES))

    return pl.pallas_call(
        body, name=name, out_shape=jax.ShapeDtypeStruct((s, N_HEADS * LANES), f32), compiler_params=_params(),
    )(fl, bf_row)


def _forget_prep_bwd(rs, dcs, fl, bf_row, name):
    s = fl.shape[0]

    def body(r_ref, c_ref, f_ref, b_ref, df_ref, db_ref):
        eye = (lax.broadcasted_iota(jnp.int32, (N_HEADS, LANES), 0) == lax.broadcasted_iota(jnp.int32, (N_HEADS, LANES), 1)).astype(f32)
        dcum = lax.dot_general(r_ref[...], eye, _TN, precision=lax.Precision.HIGHEST, preferred_element_type=f32)
        for h in range(N_HEADS):
            dcum = dcum - jnp.where(_lane() == h, jnp.sum(c_ref[:, h * LANES:(h + 1) * LANES], axis=1, keepdims=True), 0.0)
        dlf = _cumsum_rows(dcum, reverse=True)
        z = f_ref[...] + b_ref[...]
        df = jnp.where(_lane() < N_HEADS, dlf * jax.nn.sigmoid(-z), 0.0)
        df_ref[...] = df.astype(bf16)
        db_ref[...] = jnp.zeros_like(db_ref)
        db_ref[0:1, :] = jnp.sum(df, axis=0, keepdims=True)

    return pl.pallas_call(
        body, name=name,
        out_shape=[jax.ShapeDtypeStruct((s, LANES), bf16), jax.ShapeDtypeStruct((8, LANES), f32)], compiler_params=_params(),
    )(rs, dcs, fl, bf_row)


def _tile_mask(n_keys, n_queries, off, window):
    shape = (n_keys, n_queries)
    d = lax.broadcasted_iota(jnp.int32, shape, 1) - lax.broadcasted_iota(jnp.int32, shape, 0) + off
    valid = d >= 0
    return jnp.logical_and(valid, d < window) if window else valid


def _wide(v, t):
    return jnp.concatenate([v] * (t // LANES), axis=1)


def _attn_fwd(q, k, v, name, *, cum_b=None, sink_rows=None, window=None, t=256):
    s = q.shape[0]
    t = _row_tile(s, t)
    fox, has_sink = cum_b is not None, sink_rows is not None
    assert not window or (window % LANES == 0 and LANES + window <= s)

    def body(*refs):
        q_ref, k_ref, v_ref = refs[:3]
        rest = list(refs[3:])
        cb_ref = rest.pop(0) if fox else None
        sink_ref = rest.pop(0) if has_sink else None
        o_ref, lse_ref = rest
        i = pl.program_id(1)
        low = _lane() < HEAD_DIM
        top = lax.broadcasted_iota(jnp.int32, (LANES, 1), 0) < HEAD_DIM
        q2 = q_ref[...]
        zero = jnp.zeros_like(q2)
        qms = (jnp.where(low, q2, zero), jnp.where(low, zero, q2))

        def tile(k0, n_keys, off, carry, masked, queries=slice(0, t)):
            nq = queries.stop - queries.start
            kblk, vblk = k_ref[pl.ds(k0, n_keys), :], v_ref[pl.ds(k0, n_keys), :]
            valid = _tile_mask(n_keys, nq, off, window) if masked else None
            out = []
            for h in range(2):
                m, l, acc = carry[h]
                sc = lax.dot_general(kblk, qms[h][queries], _NT, preferred_element_type=f32)
                if fox:
                    sc = sc - _wide(cb_ref[pl.ds(k0, n_keys), h * LANES:(h + 1) * LANES], nq)
                if masked:
                    sc = jnp.where(valid, sc, NEG)
                m_new = jnp.maximum(m, jnp.max(sc, axis=0, keepdims=True))
                p = jnp.exp(sc - m_new)
                alpha = jnp.exp(m - m_new)
                l = alpha * l + jnp.sum(p, axis=0, keepdims=True)
                acc = alpha * acc + lax.dot_general(vblk, p.astype(bf16), _TN, preferred_element_type=f32)
                out.append((m_new, l, acc))
            return tuple(out)

        def start(nq):
            if has_sink:
                return tuple((_wide(sink_ref[h:h + 1, :], nq), jnp.ones((1, nq), f32), jnp.zeros((LANES, nq), f32))
                             for h in range(2))
            return tuple((jnp.full((1, nq), NEG, f32), jnp.zeros((1, nq), f32), jnp.zeros((LANES, nq), f32)) for h in range(2))

        def finish(carry, queries):
            (m0, l0, a0), (m1, l1, a1) = carry
            o_t = jnp.where(top, a0 * (1.0 / l0), a1 * (1.0 / l1))
            o_ref[queries, :] = o_t.T.astype(bf16)
            lse_ref[0:1, queries] = m0 + jnp.log(l0)
            lse_ref[1:2, queries] = m1 + jnp.log(l1)

        if window:
            for c in range(t // LANES):
                queries = slice(c * LANES, (c + 1) * LANES)
                q0 = i * t + c * LANES
                k0 = pl.multiple_of(jnp.maximum(q0 - window, 0), LANES)
                finish(tile(k0, LANES + window, q0 - k0, start(LANES), True, queries), queries)
        else:
            carry = lax.fori_loop(0, i, lambda kb, c: tile(pl.multiple_of(kb * t, t), t, 0, c, False), start(t))
            finish(tile(pl.multiple_of(i * t, t), t, 0, carry, True), slice(0, t))

    q_spec = pl.BlockSpec((t, LANES), lambda j, i: (i, j))
    kv_spec = pl.BlockSpec((s, LANES), lambda j, i: (0, j))
    in_specs, args = [q_spec, kv_spec, kv_spec], [q, k, v]
    if fox:
        in_specs += [pl.BlockSpec((s, 2 * LANES), lambda j, i: (0, j))]
        args += [cum_b]
    if has_sink:
        in_specs += [pl.BlockSpec((None, 2, LANES), lambda j, i: (j, 0, 0))]
        args += [sink_rows.reshape(N_PAIRS, 2, LANES)]
    return pl.pallas_call(
        body, name=name, grid=(N_PAIRS, s // t), in_specs=in_specs,
        out_specs=[q_spec, pl.BlockSpec((None, 2, t), lambda j, i: (j, 0, i))],
        out_shape=[jax.ShapeDtypeStruct((s, N_PAIRS * LANES), bf16), jax.ShapeDtypeStruct((N_PAIRS, 2, s), f32)],
        compiler_params=_params(2),
    )(*args)


def _attn_delta(do, o, name, *, lse=None, sink_rows=None):
    s, hw = do.shape
    tm = _row_tile(s, 512)
    has_sink = sink_rows is not None

    def body(*refs):
        do_ref, o_ref = refs[:2]
        if has_sink:
            lse_ref, sink_ref, dl_ref, ds_ref = refs[2:]

            @pl.when(pl.program_id(0) == 0)
            def _():
                ds_ref[...] = jnp.zeros_like(ds_ref)
        else:
            dl_ref, = refs[2:]
        for j in range(N_PAIRS):
            cols = slice(j * LANES, (j + 1) * LANES)
            prod_t = (do_ref[:, cols].astype(f32) * o_ref[:, cols].astype(f32)).T
            for h in range(2):
                dl = jnp.sum(prod_t[h * HEAD_DIM:(h + 1) * HEAD_DIM, :], axis=0, keepdims=True)
                dl_ref[j, h:h + 1, :] = dl
                if has_sink:
                    r = 2 * j + h
                    p_sink = jnp.exp(sink_ref[r:r + 1, 0:1] - lse_ref[j, h:h + 1, :])
                    ds_ref[r:r + 1, :] += -jnp.sum(p_sink * dl, axis=1, keepdims=True)

    rows_spec = pl.BlockSpec((N_PAIRS, 2, tm), lambda i: (0, 0, i))
    in_specs, args = [_row_spec(tm, hw)] * 2, [do, o]
    out_specs, out_shape = [rows_spec], [jax.ShapeDtypeStruct((N_PAIRS, 2, s), f32)]
    if has_sink:
        in_specs += [rows_spec, _vec_spec(LANES, N_HEADS)]
        args += [lse, sink_rows]
        out_specs += [_vec_spec(LANES, N_HEADS)]
        out_shape += [jax.ShapeDtypeStruct((N_HEADS, LANES), f32)]
    return pl.pallas_call(
        body, name=name, grid=(s // tm,), in_specs=in_specs, out_specs=out_specs, out_shape=out_shape,
        compiler_params=_params(1),
    )(*args)


def _attn_bwd(q, k, v, do, lse, delta, name, *, cum_b=None, window=None, t=256):
    s = q.shape[0]
    t = _row_tile(s, t)
    nblk = s // t
    fox = cum_b is not None
    assert not window or (window % LANES == 0 and LANES + window <= s)

    def body(*refs):
        k_ref, v_ref, q_ref, do_ref, lse_ref, dl_ref = refs[:6]
        rest = list(refs[6:])
        cb_ref = rest.pop(0) if fox else None
        dq_ref, dk_ref, dv_ref = rest[:3]
        dcs_ref, rs_ref = (rest[3], rest[4]) if fox else (None, None)
        b = pl.program_id(1)
        k0 = pl.multiple_of(b * t, t)

        @pl.when(b == 0)
        def _():
            dq_ref[...] = jnp.zeros_like(dq_ref)
            if fox:
                rs_ref[...] = jnp.zeros_like(rs_ref)

        dk_ref[...] = jnp.zeros_like(dk_ref)
        dv_ref[...] = jnp.zeros_like(dv_ref)
        if fox:
            dcs_ref[...] = jnp.zeros_like(dcs_ref)
        low = _lane() < HEAD_DIM
        top = lax.broadcasted_iota(jnp.int32, (LANES, 1), 0) < HEAD_DIM
        kblk, vblk = k_ref[...], v_ref[...]
        k_t = kblk.astype(f32).T.astype(bf16)
        cks = [_wide(cb_ref[pl.ds(k0, t), h * LANES:(h + 1) * LANES], t) for h in range(2)] if fox else None

        def tile(q0, n_queries, off, masked, keys=slice(0, t)):
            cols = pl.ds(q0, n_queries)
            q2, do2 = q_ref[cols, :], do_ref[cols, :]
            zero = jnp.zeros_like(q2)
            valid = _tile_mask(keys.stop - keys.start, n_queries, off, window) if masked else None
            dq_parts = []
            for h in range(2):
                qm = jnp.where(low, q2, zero) if h == 0 else jnp.where(low, zero, q2)
                dom = jnp.where(low, do2, zero) if h == 0 else jnp.where(low, zero, do2)
                sc = lax.dot_general(kblk[keys], qm, _NT, preferred_element_type=f32)
                if fox:
                    sc = sc - cks[h]
                if masked:
                    sc = jnp.where(valid, sc, NEG)
                p = jnp.exp(sc - lse_ref[h:h + 1, cols])
                dp = lax.dot_general(vblk[keys], dom, _NT, preferred_element_type=f32)
                ds = p * (dp - dl_ref[h:h + 1, cols])
                pb, dsb = p.astype(bf16), ds.astype(bf16)
                dv_ref[keys, :] += jnp.dot(pb, dom, preferred_element_type=f32)
                dk_ref[keys, :] += jnp.dot(dsb, qm, preferred_element_type=f32)
                dq_parts.append(jnp.dot(k_t[:, keys], dsb, preferred_element_type=f32))
                if fox:
                    dcs_ref[:, h * LANES:(h + 1) * LANES] += sum(ds[:, g * LANES:(g + 1) * LANES] for g in range(t // LANES))
                    rs_ref[h:h + 1, cols] += jnp.sum(ds, axis=0, keepdims=True)
            dq_ref[:, cols] += jnp.where(top, dq_parts[0], dq_parts[1])

        def later_block(qb, carry):
            tile(pl.multiple_of(qb * t, t), t, 0, False)
            return carry

        if window:
            for c in range(t // LANES):
                first = b * t + c * LANES
                q0 = pl.multiple_of(jnp.minimum(first, s - (LANES + window)), LANES)
                tile(q0, LANES + window, q0 - first, True, slice(c * LANES, (c + 1) * LANES))
        else:
            tile(k0, t, 0, True)
            lax.fori_loop(b + 1, nblk, later_block, 0)

    kv_spec = pl.BlockSpec((t, LANES), lambda j, b: (b, j))
    seq_spec = pl.BlockSpec((s, LANES), lambda j, b: (0, j))
    rows_spec = pl.BlockSpec((None, 2, s), lambda j, b: (j, 0, 0))
    hw = N_PAIRS * LANES
    in_specs, args = [kv_spec, kv_spec, seq_spec, seq_spec, rows_spec, rows_spec], [k, v, q, do, lse, delta]
    out_specs = [pl.BlockSpec((LANES, s), lambda j, b: (j, 0)), kv_spec, kv_spec]
    out_shape = [jax.ShapeDtypeStruct((hw, s), f32), jax.ShapeDtypeStruct((s, hw), f32), jax.ShapeDtypeStruct((s, hw), f32)]
    if fox:
        in_specs += [pl.BlockSpec((s, 2 * LANES), lambda j, b: (0, j))]
        args += [cum_b]
        out_specs += [pl.BlockSpec((t, 2 * LANES), lambda j, b: (b, j)), rows_spec]
        out_shape += [jax.ShapeDtypeStruct((s, N_HEADS * LANES), f32), jax.ShapeDtypeStruct((N_PAIRS, 2, s), f32)]
    return pl.pallas_call(
        body, name=name, grid=(N_PAIRS, nblk), in_specs=in_specs, out_specs=out_specs, out_shape=out_shape,
        compiler_params=_params(2),
    )(*args)


def _merge(ba, bb, gl, name):
    s, d = ba.shape
    tm = _row_tile(s, 512)

    def body(a_ref, b_ref, g_ref, o_ref):
        g0, g1 = jax.nn.sigmoid(g_ref[:, :d].astype(f32)), jax.nn.sigmoid(g_ref[:, d:].astype(f32))
        o_ref[...] = (g0 * a_ref[...].astype(f32) + g1 * b_ref[...].astype(f32)).astype(bf16)

    return pl.pallas_call(
        body, name=name, grid=(s // tm,), in_specs=[_row_spec(tm, d)] * 2 + [_row_spec(tm, 2 * d)],
        out_specs=_row_spec(tm, d), out_shape=jax.ShapeDtypeStruct((s, d), bf16), compiler_params=_params(1),
    )(ba, bb, gl)


def _merge_bwd(dm, ba, bb, gl, name):
    s, d = ba.shape
    tm = _row_tile(s, 512)

    def body(dm_ref, a_ref, b_ref, g_ref, da_ref, db_ref, dg_ref):
        dmv = dm_ref[...].astype(f32)
        g0, g1 = jax.nn.sigmoid(g_ref[:, :d].astype(f32)), jax.nn.sigmoid(g_ref[:, d:].astype(f32))
        da_ref[...] = (dmv * g0).astype(bf16)
        db_ref[...] = (dmv * g1).astype(bf16)
        dg_ref[:, :d] = (dmv * a_ref[...].astype(f32) * (g0 * (1.0 - g0))).astype(bf16)
        dg_ref[:, d:] = (dmv * b_ref[...].astype(f32) * (g1 * (1.0 - g1))).astype(bf16)

    return pl.pallas_call(
        body, name=name, grid=(s // tm,), in_specs=[_row_spec(tm, d)] * 3 + [_row_spec(tm, 2 * d)],
        out_specs=[_row_spec(tm, d)] * 2 + [_row_spec(tm, 2 * d)],
        out_shape=[jax.ShapeDtypeStruct((s, d), bf16)] * 2 + [jax.ShapeDtypeStruct((s, 2 * d), bf16)],
        compiler_params=_params(1),
    )(dm, ba, bb, gl)


GLU_TILE = 256


def _ffn_in_swiglu(h, w_t, name):
    s, d = h.shape
    f = w_t.shape[0] // 2
    tm = _row_tile(s, 2048)
    tg = GLU_TILE
    nb = f // tg

    def body(h_ref, wg_ref, wu_ref, g_ref, u_ref, act_ref):
        hv = h_ref[...]
        g = lax.dot_general(hv, wg_ref[...], _NT, preferred_element_type=f32)
        u = lax.dot_general(hv, wu_ref[...], _NT, preferred_element_type=f32)
        g_ref[...] = g.astype(bf16)
        u_ref[...] = u.astype(bf16)
        act_ref[...] = (g * jax.nn.sigmoid(g) * u).astype(bf16)

    col = pl.BlockSpec((tm, tg), lambda i, j: (i, j))
    return pl.pallas_call(
        body, name=name, grid=(s // tm, nb),
        in_specs=[pl.BlockSpec((tm, d), lambda i, j: (i, 0)), pl.BlockSpec((tg, d), lambda i, j: (j, 0)),
                  pl.BlockSpec((tg, d), lambda i, j: (j + nb, 0))],
        out_specs=[col] * 3, out_shape=[jax.ShapeDtypeStruct((s, f), bf16)] * 3, compiler_params=_params(2),
    )(h, w_t, w_t)


def _ffn_out_dgrad_swiglu(dy, w_out, g, u, name):
    s, d = dy.shape
    f = g.shape[1]
    tm = _row_tile(s, 2048)
    tg = GLU_TILE

    def body(dy_ref, w_ref, g_ref, u_ref, dg_ref, du_ref):
        dv = lax.dot_general(dy_ref[...], w_ref[...], _NT, preferred_element_type=f32)
        gv, uv = g_ref[...].astype(f32), u_ref[...].astype(f32)
        sg = jax.nn.sigmoid(gv)
        dg_ref[...] = (dv * uv * (sg * (1.0 + gv * (1.0 - sg)))).astype(bf16)
        du_ref[...] = (dv * (gv * sg)).astype(bf16)

    col = pl.BlockSpec((tm, tg), lambda i, j: (i, j))
    return pl.pallas_call(
        body, name=name, grid=(s // tm, f // tg),
        in_specs=[pl.BlockSpec((tm, d), lambda i, j: (i, 0)), pl.BlockSpec((tg, d), lambda i, j: (j, 0)), col, col],
        out_specs=[col] * 2, out_shape=[jax.ShapeDtypeStruct((s, f), bf16)] * 2, compiler_params=_params(2),
    )(dy, w_out, g, u)


def _sum_matmul(terms, name, after=None):
    s = terms[0][0].shape[0]
    d = terms[0][1].shape[1]
    k = sum(a.shape[1] for a, _, _ in terms)
    tm, tn = _matmul_tiles(s, d, k, terms[0][0].dtype.itemsize, terms[0][1].dtype.itemsize, 4)
    n = len(terms)

    def body(*refs):
        acc = jnp.dot(refs[0][...], refs[n][...], preferred_element_type=f32)
        for i in range(1, n):
            acc = acc + jnp.dot(refs[i][...], refs[n + i][...], preferred_element_type=f32)
        refs[-1][...] = acc

    extra = [] if after is None else [after]
    a_specs = [pl.BlockSpec((tm, a.shape[1]), lambda i, j: (i, 0)) for a, _, _ in terms]
    b_specs = [pl.BlockSpec((a.shape[1], tn), lambda i, j, r=r: (r, j)) for a, _, r in terms]
    return pl.pallas_call(
        body, name=name, grid=(s // tm, d // tn),
        in_specs=a_specs + b_specs + [pl.BlockSpec(memory_space=pl.ANY)] * len(extra),
        out_specs=pl.BlockSpec((tm, tn), lambda i, j: (i, j)),
        out_shape=jax.ShapeDtypeStruct((s, d), f32), compiler_params=_params(2),
    )(*[a for a, _, _ in terms], *[b for _, b, _ in terms], *extra)


def _wgrad_stack(parts, h, name):
    s, m = parts[0].shape
    d = h.shape[1]
    tm = 256
    nb = m // tm
    n = len(parts)

    def body(*refs):
        i = pl.program_id(0)
        for p in range(n):
            @pl.when(i // nb == p)
            def _(p=p):
                refs[n + 1][...] = lax.dot_general(refs[p][...], refs[n][...], _TN, preferred_element_type=f32).astype(bf16)

    a_specs = [pl.BlockSpec((s, tm), lambda i, p=p: (0, jnp.clip(i - p * nb, 0, nb - 1))) for p in range(n)]
    return pl.pallas_call(
        body, name=name, grid=(n * nb,), in_specs=a_specs + [pl.BlockSpec((s, d), lambda i: (0, 0))],
        out_specs=pl.BlockSpec((tm, d), lambda i: (i, 0)),
        out_shape=jax.ShapeDtypeStruct((n * m, d), bf16), compiler_params=_params(1),
    )(*parts, h)


def _ada_fwd(c_all, w, b, name):
    def body(c_ref, w_ref, b_ref, o_ref):
        o_ref[...] = jnp.dot(c_ref[...].astype(bf16), w_ref[...].astype(bf16), preferred_element_type=f32) + b_ref[...]

    return pl.pallas_call(
        body, name=name, out_shape=jax.ShapeDtypeStruct((c_all.shape[0], w.shape[1]), f32), compiler_params=_params(),
    )(c_all, w, b)


def _ada_wgrad(c_all, d_all, name):
    n, d = c_all.shape
    w = d_all.shape[1]

    def body(c_ref, d_ref, o_ref):
        eye = (lax.broadcasted_iota(jnp.int32, (n, n), 0) == lax.broadcasted_iota(jnp.int32, (n, n), 1)).astype(f32)
        ct = lax.dot_general(c_ref[...], eye, _TN, precision=lax.Precision.HIGHEST, preferred_element_type=f32)
        g = ct[:, 0:1] * d_ref[0:1, :]
        for bi in range(1, n):
            g = g + ct[:, bi:bi + 1] * d_ref[bi:bi + 1, :]
        o_ref[0] = g

    return pl.pallas_call(
        body, name=name, out_shape=jax.ShapeDtypeStruct((1, d, w), f32), compiler_params=_params(),
    )(c_all, d_all)


def _adamw(parts, w, m, v, name, mine=None):
    r, c = w.shape
    n_parts = parts.shape[0]
    row_tiles = [t for t in range(min(r, 256), 0, -1) if r % t == 0 and (t % 16 == 0 or t == r)]
    if row_tiles:
        tr, tc = row_tiles[0], c
    else:
        tr, tc = r, next(t for t in (256, LANES) if c % t == 0)

    def body(p_ref, *rest):
        own_ref = rest[0] if mine is not None else None
        w_ref, m_ref, v_ref, g_ref, d_ref, nm_ref, nv_ref = rest[-7:]
        if mine is not None:
            x, y, cc = _me()
            me = 4 * x + 2 * y + cc

        def part(i):
            if mine is None:
                return p_ref[i].astype(f32)
            return jnp.where(me == i, own_ref[i], p_ref[i]).astype(f32)

        g = part(0)
        for i in range(1, n_parts):
            g = g + part(i)
        mm = ADAM_B1 * m_ref[...] + (1.0 - ADAM_B1) * g
        vv = ADAM_B2 * v_ref[...] + (1.0 - ADAM_B2) * (g * g)
        m_hat = mm / (1.0 - ADAM_B1 ** ADAM_STEP)
        v_hat = vv / (1.0 - ADAM_B2 ** ADAM_STEP)
        g_ref[...] = g
        d_ref[...] = -ADAM_LR * (m_hat / (jnp.sqrt(v_hat) + ADAM_EPS) + ADAM_WD * w_ref[...])
        nm_ref[...] = mm
        nv_ref[...] = vv

    spec = pl.BlockSpec((tr, tc), lambda i, j: (i, j))
    stack = [parts] if mine is None else [parts, mine]
    return pl.pallas_call(
        body, name=name, grid=(r // tr, c // tc),
        in_specs=[pl.BlockSpec((n_parts, tr, tc), lambda i, j: (0, i, j))] * len(stack) + [spec] * 3,
        out_specs=[spec] * 4, out_shape=[jax.ShapeDtypeStruct((r, c), f32)] * 4, compiler_params=_params(2),
    )(*stack, w, m, v)


def _me():
    return lax.axis_index("x"), lax.axis_index("y"), lax.axis_index("c")


def _all_gather(arrays, name, vmem=False, after=None):
    n = len(arrays)
    space = pltpu.VMEM if vmem else pl.ANY
    extra = [] if after is None else [after]

    def body(*refs):
        ins = refs[:n]
        outs = refs[n + len(extra):2 * n + len(extra)]
        send_sems, recv_sems, local_sems = refs[2 * n + len(extra):]
        x, y, c = _me()
        me, sibling = (x, y, c), (x, y, 1 - c)
        chips = [(1 - x, y), (x, 1 - y), (1 - x, 1 - y)]

        def rows(a, dev):
            return outs[a].at[4 * dev[0] + 2 * dev[1] + dev[2]]

        def copy(a, k, block, to, src=None):
            return pltpu.make_async_remote_copy(
                src_ref=rows(a, block) if src is None else src, dst_ref=rows(a, block),
                send_sem=send_sems.at[a, k], recv_sem=recv_sems.at[a, k], device_id=to, device_id_type=MESH)

        mine = [pltpu.make_async_copy(ins[a], rows(a, me), local_sems.at[a]) for a in range(n)]
        for cp in mine:
            cp.start()
        first = []
        for a in range(n):
            first.append(copy(a, 0, me, sibling, src=ins[a]))
            first += [copy(a, 1 + j, me, (*chip, c), src=ins[a]) for j, chip in enumerate(chips)]
        for cp in first:
            cp.start()
        passed = []
        for j, chip in enumerate(chips):
            for a in range(n):
                copy(a, 1 + j, (*chip, c), me).wait_recv()
                fwd = copy(a, 4 + j, (*chip, c), sibling)
                fwd.start()
                passed.append(fwd)
        for a in range(n):
            copy(a, 0, sibling, me).wait_recv()
            for j, chip in enumerate(chips):
                copy(a, 4 + j, (*chip, 1 - c), me).wait_recv()
        for cp in first + passed:
            cp.wait_send()
        for cp in mine:
            cp.wait()

    outs = pl.pallas_call(
        body, name=name,
        in_specs=[pl.BlockSpec(memory_space=space)] * n + [pl.BlockSpec(memory_space=pl.ANY)] * len(extra),
        out_specs=[pl.BlockSpec(memory_space=space)] * n,
        out_shape=[jax.ShapeDtypeStruct((N_DEV,) + a.shape, a.dtype) for a in arrays],
        scratch_shapes=[pltpu.SemaphoreType.DMA((n, 7)), pltpu.SemaphoreType.DMA((n, 7)), pltpu.SemaphoreType.DMA((n,))],
        compiler_params=pltpu.CompilerParams(vmem_limit_bytes=VMEM_LIMIT),
    )(*arrays, *extra)
    return list(outs)


_FLIPS = ((0, 0, 1), (1, 0, 0), (0, 1, 0), (1, 1, 0), (1, 0, 1), (0, 1, 1), (1, 1, 1))
_HBM = pl.BlockSpec(memory_space=pltpu.HBM)
_SEM = pl.BlockSpec(memory_space=pltpu.SEMAPHORE)


def _exchange_copies(scatter, srcs, lands, send_sems, recv_sems):
    x, y, c = _me()
    me_row = 4 * x + 2 * y + c
    out = []
    for k, (fx, fy, fc) in enumerate(_FLIPS):
        peer = (x ^ fx, y ^ fy, c ^ fc)
        peer_row = 4 * peer[0] + 2 * peer[1] + peer[2]
        for a in range(len(srcs)):
            out.append(pltpu.make_async_remote_copy(
                src_ref=srcs[a].at[peer_row] if scatter else srcs[a], dst_ref=lands[a].at[me_row],
                send_sem=send_sems.at[7 * a + k], recv_sem=recv_sems.at[7 * a + k], device_id=peer, device_id_type=MESH))
    return out


def _exchange_start(arrays, scatter, name, after=None):
    n = len(arrays)
    lands = [lax.empty(a.shape if scatter else (N_DEV,) + a.shape, a.dtype) for a in arrays]
    extra = [] if after is None else [after]

    def body(*refs):
        srcs, zones = refs[:n], refs[n:2 * n]
        send_sems, recv_sems = refs[2 * n + len(extra)], refs[2 * n + len(extra) + 1]
        token = refs[-1]
        for cp in _exchange_copies(scatter, srcs, zones, send_sems, recv_sems):
            cp.start()
        token[...] = jnp.zeros_like(token)

    thru = [pltpu.HBM(a.shape, a.dtype) for a in list(arrays) + lands]
    outs = pl.pallas_call(
        body, name=name,
        out_shape=(pltpu.SemaphoreType.DMA((7 * n,)), pltpu.SemaphoreType.DMA((7 * n,)), *thru, jax.ShapeDtypeStruct((8, LANES), f32)),
        in_specs=[_HBM] * (2 * n) + [pl.BlockSpec(memory_space=pl.ANY)] * len(extra),
        out_specs=(_SEM, _SEM, *[_HBM] * (2 * n), pl.BlockSpec(memory_space=pltpu.VMEM)),
        input_output_aliases={i: 2 + i for i in range(2 * n)},
        compiler_params=pltpu.CompilerParams(has_side_effects=pltpu.SideEffectType.DATAFLOW_SIDE_EFFECTING),
    )(*[pltpu.with_memory_space_constraint(a, pltpu.HBM) for a in list(arrays) + lands], *extra)
    return dict(n=n, scatter=scatter, sems=outs[:2], srcs=outs[2:2 + n], lands=outs[2 + n:2 + 2 * n], token=outs[-1])


def _exchange_wait(handle, after, name):
    n, scatter = handle["n"], handle["scatter"]

    def body(*refs):
        srcs, zones = refs[:n], refs[n:2 * n]
        send_sems, recv_sems = refs[2 * n], refs[2 * n + 1]
        for cp in _exchange_copies(scatter, srcs, zones, send_sems, recv_sems):
            cp.wait_send()
            cp.wait_recv()

    thru = [pltpu.HBM(a.shape, a.dtype) for a in list(handle["srcs"]) + list(handle["lands"])]
    outs = pl.pallas_call(
        body, name=name, out_shape=tuple(thru),
        in_specs=[_HBM] * (2 * n) + [_SEM, _SEM, pl.BlockSpec(memory_space=pl.ANY)], out_specs=tuple([_HBM] * (2 * n)),
        input_output_aliases={i: i for i in range(2 * n)},
        compiler_params=pltpu.CompilerParams(has_side_effects=pltpu.SideEffectType.DATAFLOW_SIDE_EFFECTING),
    )(*handle["srcs"], *handle["lands"], *handle["sems"], after)
    return list(outs[n:])


def _cols_from_shards(g):
    return jnp.transpose(g, (1, 0, 2)).reshape(g.shape[1], -1)


def _shards_from_cols(a):
    return jnp.transpose(a.reshape(a.shape[0], N_DEV, -1), (1, 0, 2))


def _local_step(x, positions, ada, g_pre_mix, g_post_mix, b_f, sinks, g_pre_ffn, g_post_ffn, target,
                w_in_t, late_weights, on_grads):
    s, d = x.shape
    row = lambda v: v.reshape(1, -1)
    shift_m, scale_m, gate_m, shift_f, scale_f, gate_f = (ada[i:i + 1] for i in range(6))
    w_gate_t, w_qkv_t = w_in_t[F_OFF + N_HEADS:], w_in_t[:QKV_W]
    w_f_t = jnp.pad(w_in_t[F_OFF:F_OFF + N_HEADS], ((0, LANES - N_HEADS), (0, 0)))
    bf_row = jnp.pad(row(b_f), ((0, 0), (0, LANES - N_HEADS)))
    sink_rows = jnp.broadcast_to(sinks.reshape(N_HEADS, 1).astype(f32), (N_HEADS, LANES))
    inv_freq = 1.0 / (ROPE_THETA ** (jnp.arange(0, HEAD_DIM, 2, dtype=f32) / HEAD_DIM))
    cos, sin_s = _rope_tables(positions.reshape(s, 1), jnp.tile(inv_freq, 4).reshape(1, LANES), "rope_tables")

    h1 = _prenorm(x, row(g_pre_mix), scale_m, shift_m, "prenorm_mix")
    gl = _matmul(h1, w_gate_t, "nt", bf16, "proj_gate")
    qkv = _matmul(h1, w_qkv_t, "nt", f32, "proj_qkv")
    fl = _matmul(h1, w_f_t, "nt", f32, "proj_forget")
    qa, ka, va, qb, kb, vb = _qkv_prep(qkv, cos, sin_s, "qkv_prep")
    cum_b = _forget_prep(fl, bf_row, "forget_prep")
    o_a, lse_a = _attn_fwd(qa, ka, va, "swa_fwd", sink_rows=sink_rows, window=WINDOW, t=512)
    o_b, lse_b = _attn_fwd(qb, kb, vb, "fox_fwd", cum_b=cum_b, t=512)
    w_branch_a, w_branch_b, w_out, w_ffn_in_t, w_ffn_out = late_weights(o_b)
    ba = _matmul(o_a, w_branch_a, "nn", bf16, "branch_a")
    bb = _matmul(o_b, w_branch_b, "nn", bf16, "branch_b")
    merged = _merge(ba, bb, gl, "merge")
    y1 = _matmul(merged, w_out, "nn", f32, "out_proj")
    x2 = _postnorm_res(x, y1, row(g_post_mix), gate_m, "postnorm_mix")

    h2 = _prenorm(x2, row(g_pre_ffn), scale_f, shift_f, "prenorm_ffn")
    g_ff, u_ff, act = _ffn_in_swiglu(h2, w_ffn_in_t, "ffn_in_swiglu")
    y2 = _matmul(act, w_ffn_out, "nn", f32, "ffn_out")
    loss_row, d_out, d_y2, vec_pf = _loss_tail(x2, y2, row(g_post_ffn), gate_f, target, "loss_tail")

    g_w_ffn_out = _matmul(act, d_y2, "tn", bf16, "ffn_out_wgrad")
    dg_ff, du_ff = _ffn_out_dgrad_swiglu(d_y2, w_ffn_out, g_ff, u_ff, "ffn_out_dgrad_swiglu")
    g_w_ffn_in_t = _wgrad_stack([dg_ff, du_ff], h2, "ffn_in_wgrad")
    sent = on_grads(dict(w_ffn_in=g_w_ffn_in_t, w_ffn_out=g_w_ffn_out))
    d_h2 = _sum_matmul([(dg_ff, w_ffn_in_t, 0), (du_ff, w_ffn_in_t, 1)], "ffn_in_dgrad", after=sent)
    d_x2, vec_nf = _prenorm_bwd(d_h2, x2, row(g_pre_ffn), scale_f, d_out, "prenorm_ffn_bwd")

    d_y1, vec_pm = _postnorm_bwd(d_x2, y1, row(g_post_mix), gate_m, "postnorm_mix_bwd")
    g_w_out = _matmul(merged, d_y1, "tn", bf16, "out_proj_wgrad")
    d_merged = _matmul(d_y1, w_out, "nt", bf16, "out_proj_dgrad")
    d_ba, d_bb, dgl = _merge_bwd(d_merged, ba, bb, gl, "merge_bwd")
    g_w_branch_a = _matmul(o_a, d_ba, "tn", bf16, "branch_a_wgrad")
    g_w_branch_b = _matmul(o_b, d_bb, "tn", bf16, "branch_b_wgrad")
    sent = on_grads(dict(w_out=g_w_out, w_branch_a=g_w_branch_a, w_branch_b=g_w_branch_b))
    d_oa = _matmul(d_ba, w_branch_a, "nt", bf16, "branch_a_dgrad", after=sent)
    d_ob = _matmul(d_bb, w_branch_b, "nt", bf16, "branch_b_dgrad", after=sent)
    delta_a, d_sink = _attn_delta(d_oa, o_a, "swa_delta", lse=lse_a, sink_rows=sink_rows)
    delta_b, = _attn_delta(d_ob, o_b, "fox_delta")
    dqa_t, dka, dva = _attn_bwd(qa, ka, va, d_oa, lse_a, delta_a, "swa_bwd", window=WINDOW, t=512)
    dqb_t, dkb, dvb, dcs, rs = _attn_bwd(qb, kb, vb, d_ob, lse_b, delta_b, "fox_bwd", cum_b=cum_b, t=512)
    dqkv = _qkv_prep_bwd(dqa_t, dka, dva, dqb_t, dkb, dvb, cos, sin_s, "qkv_prep_bwd")
    dfl, vec_bf = _forget_prep_bwd(rs.reshape(N_HEADS, s), dcs, fl, bf_row, "forget_prep_bwd")
    g_w_in_t = jnp.concatenate([_matmul(dqkv, h1, "tn", bf16, "qkv_wgrad"), _matmul(dfl, h1, "tn", bf16, "forget_wgrad")[:N_HEADS],
                                _matmul(dgl, h1, "tn", bf16, "gate_wgrad")], axis=0)
    sent = on_grads(dict(w_in=g_w_in_t))
    d_h1 = _sum_matmul([(dgl, w_gate_t, 0), (dqkv, w_qkv_t, 0), (dfl, w_f_t, 0)], "in_proj_dgrad", after=sent)
    grad_x, vec_nm = _prenorm_bwd(d_h1, x, row(g_pre_mix), scale_m, d_x2, "prenorm_mix_bwd")

    d_ada = jnp.concatenate([vec_nm[0], vec_nm[1], vec_pm[0], vec_nf[0], vec_nf[1], vec_pf[0]])
    small = dict(b_ada=d_ada, g_pre_mix=vec_nm[2], g_post_mix=vec_pm[1], g_pre_ffn=vec_nf[2], g_post_ffn=vec_pf[1],
                 b_f=vec_bf[0, :N_HEADS], sinks=d_sink[:, 0], loss=loss_row[0, :1])
    return grad_x, small


_SMALL = (("b_ada", 6144), ("g_pre_mix", 1024), ("g_post_mix", 1024), ("g_pre_ffn", 1024), ("g_post_ffn", 1024),
          ("b_f", 128), ("sinks", 128), ("loss", 128))
_SMALL_ROWS = 88


def _pack_small(vals):
    parts = [jnp.pad(vals[k].reshape(-1).astype(f32), (0, n - vals[k].size)) for k, n in _SMALL]
    flat = jnp.concatenate(parts)
    return jnp.pad(flat, (0, _SMALL_ROWS * LANES - flat.size)).reshape(_SMALL_ROWS, LANES)


def _unpack_small(slab, shapes):
    flat, out, off = slab.reshape(-1), {}, 0
    for k, n in _SMALL:
        size = math.prod(shapes[k])
        out[k] = flat[off:off + size].reshape(shapes[k])
        off += n
    return out


def kernel(x, c, positions, w_ada, b_ada, g_pre_mix, g_post_mix, w_in, b_f, sinks, w_branch_a, w_branch_b, w_out, g_pre_ffn, g_post_ffn, w_ffn_in, w_ffn_out, loss_target, m_w_ada, m_b_ada, m_g_pre_mix, m_g_post_mix, m_w_in, m_b_f, m_sinks, m_w_branch_a, m_w_branch_b, m_w_out, m_g_pre_ffn, m_g_post_ffn, m_w_ffn_in, m_w_ffn_out, v_w_ada, v_b_ada, v_g_pre_mix, v_g_post_mix, v_w_in, v_b_f, v_sinks, v_w_branch_a, v_w_branch_b, v_w_out, v_g_pre_ffn, v_g_post_ffn, v_w_ffn_in, v_w_ffn_out):
    xi, yi, ci = _me()
    me = 4 * xi + 2 * yi + ci
    d = D_MODEL
    ada_w = w_ada.shape[2]

    c_all, = _all_gather([c], "gather_c", vmem=True)
    c_all = c_all.reshape(N_DEV, d)
    b_mine = lax.dynamic_slice(b_ada, (0, me * ada_w), (1, ada_w))
    ada_cols = _ada_fwd(c_all, w_ada[0], b_mine, "ada_fwd")
    ada_all, = _all_gather([ada_cols], "gather_ada", vmem=True)
    ada = lax.dynamic_index_in_dim(ada_all, me, axis=1, keepdims=False).reshape(6, d)

    transposed = ("w_in", "w_ffn_in")
    tr = lambda a: jnp.transpose(a[0])

    g_in, = _all_gather([tr(w_in).astype(bf16)], "gather_w_in")
    late = [w.astype(bf16) for w in (w_branch_a[0], w_branch_b[0], w_out[0], tr(w_ffn_in), w_ffn_out[0])]
    late_h = _exchange_start(late, False, "gather_late_start", after=g_in)

    def mine_into(zone, block):
        return lax.dynamic_update_index_in_dim(zone, block, me, 0)

    def rows_from_shards(g):
        return g.reshape(g.shape[0] * g.shape[1], g.shape[2])

    def late_weights(after):
        zones = _exchange_wait(late_h, after, "gather_late_wait")
        g_ba, g_bb, g_out, g_fi, g_fo = (mine_into(z, w) for z, w in zip(zones, late))
        return (_cols_from_shards(g_ba), _cols_from_shards(g_bb), rows_from_shards(g_out), rows_from_shards(g_fi),
                rows_from_shards(g_fo))

    row_sharded = ("w_out", "w_ffn_out") + transposed
    in_flight = []

    def on_grads(group):
        sends = [g.reshape(N_DEV, g.shape[0] // N_DEV, g.shape[1]) if nm in row_sharded else _shards_from_cols(g)
                 for nm, g in group.items()]
        handle = _exchange_start(sends, True, "scatter_start_%d" % len(in_flight))
        in_flight.append((list(group), sends, handle))
        return handle["token"]

    grad_x, small = _local_step(
        x[0], positions[0], ada + late_h["token"][0, 0], g_pre_mix[0], g_post_mix[0], b_f[0], sinks[0], g_pre_ffn[0],
        g_post_ffn[0], loss_target[0], rows_from_shards(g_in), late_weights, on_grads)

    ws = dict(w_in=(w_in, m_w_in, v_w_in), w_branch_a=(w_branch_a, m_w_branch_a, v_w_branch_a),
              w_branch_b=(w_branch_b, m_w_branch_b, v_w_branch_b), w_out=(w_out, m_w_out, v_w_out),
              w_ffn_in=(w_ffn_in, m_w_ffn_in, v_w_ffn_in), w_ffn_out=(w_ffn_out, m_w_ffn_out, v_w_ffn_out))
    res = {}

    def finish_group(gi, after):
        names, sends, handle = in_flight[gi]
        zones = _exchange_wait(handle, after, "scatter_wait_%d" % gi)
        for nm, zone, sent in zip(names, zones, sends):
            w, m, v = (tr(a) if nm in transposed else a[0] for a in ws[nm])
            out = _adamw(zone, w, m, v, "adamw_" + nm, mine=sent)
            after = out[0]
            res[nm] = [jnp.transpose(o) for o in out] if nm in transposed else out
        return after

    done = finish_group(1, finish_group(0, grad_x))

    slab_all, = _all_gather([_pack_small(small)], "gather_small", vmem=True, after=done)
    small_w = dict(b_ada=b_ada, g_pre_mix=g_pre_mix, g_post_mix=g_post_mix, g_pre_ffn=g_pre_ffn, g_post_ffn=g_post_ffn,
                   b_f=b_f, sinks=sinks, loss=jnp.zeros((1,), f32))
    small_m = dict(b_ada=m_b_ada, g_pre_mix=m_g_pre_mix, g_post_mix=m_g_post_mix, g_pre_ffn=m_g_pre_ffn,
                   g_post_ffn=m_g_post_ffn, b_f=m_b_f, sinks=m_sinks, loss=jnp.zeros((1,), f32))
    small_v = dict(b_ada=v_b_ada, g_pre_mix=v_g_pre_mix, g_post_mix=v_g_post_mix, g_pre_ffn=v_g_pre_ffn,
                   g_post_ffn=v_g_post_ffn, b_f=v_b_f, sinks=v_sinks, loss=jnp.ones((1,), f32))
    shapes = {k: small_w[k].shape for k, _ in _SMALL}
    s_out = _adamw(slab_all, _pack_small(small_w), _pack_small(small_m), _pack_small(small_v), "adamw_small")
    s_grad, s_delta, s_m, s_v = (_unpack_small(o, shapes) for o in s_out)

    d_ada_all = lax.dynamic_slice(slab_all[:, :6144 // LANES, :].reshape(N_DEV, 6144), (0, me * ada_w), (N_DEV, ada_w))
    ada_parts = _ada_wgrad(c_all, d_ada_all, "ada_wgrad")

    res["w_ada"] = _adamw(ada_parts, w_ada[0], m_w_ada[0], v_w_ada[0], "adamw_w_ada")
    finish_group(2, res["w_ada"][0])

    order = ["w_ada", "b_ada", "g_pre_mix", "g_post_mix", "w_in", "b_f", "sinks", "w_branch_a", "w_branch_b", "w_out",
             "g_pre_ffn", "g_post_ffn", "w_ffn_in", "w_ffn_out"]
    outs = [s_grad["loss"].reshape(()), grad_x[None]]
    for which, small_o in enumerate((s_grad, s_delta, s_m, s_v)):
        for nm in order:
            outs.append(res[nm][which][None] if nm in res else small_o[nm])
    return tuple(outs)
```

```python
import functools
import math

import jax
import jax.numpy as jnp
from jax import lax
from jax.experimental import pallas as pl
from jax.experimental.pallas import tpu as pltpu

f32 = jnp.float32
bf16 = jnp.bfloat16

D_MODEL = 1024
HEAD_DIM = 64
N_HEADS = 8
N_PAIRS = 4
QKV_W = 2304
GATE_W = 2048
F_OFF = 2304
IN_W = 4360
WINDOW = 128
ROPE_THETA = 10000.0
RMS_EPS = 1e-6
D_FF = 2816
N_DEV = 8
ADAM_LR, ADAM_B1, ADAM_B2, ADAM_EPS, ADAM_WD, ADAM_STEP = 0.001, 0.9, 0.999, 1e-08, 0.01, 10
NEG = -1e30
LANES = 128
VMEM_LIMIT = 48 * 1024 * 1024
MESH = pl.DeviceIdType.MESH

_NT = (((1,), (1,)), ((), ()))
_TN = (((0,), (0,)), ((), ()))


def _params(n_grid=0):
    sem = ("arbitrary",) * n_grid if n_grid else None
    return pltpu.CompilerParams(dimension_semantics=sem, vmem_limit_bytes=VMEM_LIMIT)


def _row_tile(s, want):
    t = min(s, want)
    assert s % t == 0, (s, t)
    return t


MATMUL_VMEM_BUDGET = 40 * 1024 * 1024


def _matmul_tiles(m, n, k, a_item, b_item, o_item):
    def tiles(d):
        return [t for t in range(LANES, min(d, 2048) + 1, LANES) if d % t == 0] or [d]

    best = None
    for tm in tiles(m):
        for tn in tiles(n):
            vmem = 2 * (tm * k * a_item + tn * k * b_item + tm * tn * o_item) + tm * tn * 4
            if vmem > MATMUL_VMEM_BUDGET:
                continue
            traffic = m * k * a_item + n * k * b_item * (1 if tn == n else m // tm) + m * n * o_item
            steps = (m // tm) * (n // tn)
            key = (traffic, 0, steps) if steps >= 4 else (traffic, 1, -steps)
            if best is None or key < best[0]:
                best = (key, tm, tn)
    assert best is not None, (m, n, k)
    return best[1], best[2]


def _matmul(a, b, mode, out_dtype, name, after=None):
    if mode == "nn":
        (m, k), n = a.shape, b.shape[1]
    elif mode == "nt":
        (m, k), n = a.shape, b.shape[0]
    else:
        (k, m), n = a.shape, b.shape[1]
    tm, tn = _matmul_tiles(m, n, k, a.dtype.itemsize, b.dtype.itemsize, jnp.dtype(out_dtype).itemsize)
    if mode == "nn":
        a_spec, b_spec, dims = pl.BlockSpec((tm, k), lambda i, j: (i, 0)), pl.BlockSpec((k, tn), lambda i, j: (0, j)), None
    elif mode == "nt":
        a_spec, b_spec, dims = pl.BlockSpec((tm, k), lambda i, j: (i, 0)), pl.BlockSpec((tn, k), lambda i, j: (j, 0)), _NT
    else:
        a_spec, b_spec, dims = pl.BlockSpec((k, tm), lambda i, j: (0, i)), pl.BlockSpec((k, tn), lambda i, j: (0, j)), _TN

    def body(a_ref, b_ref, *rest):
        o_ref = rest[-1]
        av, bv = a_ref[...].astype(bf16), b_ref[...].astype(bf16)
        if dims is None:
            r = jnp.dot(av, bv, preferred_element_type=f32)
        else:
            r = lax.dot_general(av, bv, dims, preferred_element_type=f32)
        o_ref[...] = r.astype(out_dtype)

    extra = [] if after is None else [after]
    return pl.pallas_call(
        body, name=name, grid=(m // tm, n // tn), in_specs=[a_spec, b_spec] + [pl.BlockSpec(memory_space=pl.ANY)] * len(extra),
        out_specs=pl.BlockSpec((tm, tn), lambda i, j: (i, j)),
        out_shape=jax.ShapeDtypeStruct((m, n), out_dtype), compiler_params=_params(2),
    )(a, b, *extra)


def _rstd(v):
    return lax.rsqrt(jnp.mean(v * v, axis=-1, keepdims=True) + RMS_EPS)


def _row_spec(tm, d):
    return pl.BlockSpec((tm, d), lambda i: (i, 0))


def _vec_spec(d, rows=1):
    return pl.BlockSpec((rows, d), lambda i: (0, 0))


def _prenorm(x, g, scale, shift, name):
    s, d = x.shape
    tm = _row_tile(s, 512)

    def body(x_ref, g_ref, sc_ref, sh_ref, h_ref):
        xv = x_ref[...]
        h = (xv * _rstd(xv) * g_ref[...]) * (1.0 + sc_ref[...]) + sh_ref[...]
        h_ref[...] = h.astype(bf16)

    return pl.pallas_call(
        body, name=name, grid=(s // tm,), in_specs=[_row_spec(tm, d)] + [_vec_spec(d)] * 3,
        out_specs=_row_spec(tm, d), out_shape=jax.ShapeDtypeStruct((s, d), bf16), compiler_params=_params(1),
    )(x, g, scale, shift)


def _postnorm_res(x, y, g, gate, name):
    s, d = x.shape
    tm = _row_tile(s, 512)

    def body(x_ref, y_ref, g_ref, gate_ref, o_ref):
        yv = y_ref[...]
        o_ref[...] = x_ref[...] + gate_ref[...] * (yv * _rstd(yv) * g_ref[...])

    return pl.pallas_call(
        body, name=name, grid=(s // tm,), in_specs=[_row_spec(tm, d)] * 2 + [_vec_spec(d)] * 2,
        out_specs=_row_spec(tm, d), out_shape=jax.ShapeDtypeStruct((s, d), f32), compiler_params=_params(1),
    )(x, y, g, gate)


def _rms_bwd(u, v, r):
    return r * u - v * (r * r * r) * jnp.mean(u * v, axis=-1, keepdims=True)


def _loss_tail(x, y, g, gate, target, name):
    s, d = x.shape
    tm = _row_tile(s, 512)

    def body(x_ref, y_ref, g_ref, gate_ref, t_ref, loss_ref, do_ref, dy_ref, vec_ref):
        @pl.when(pl.program_id(0) == 0)
        def _():
            loss_ref[...] = jnp.zeros_like(loss_ref)
            vec_ref[...] = jnp.zeros_like(vec_ref)
        yv = y_ref[...]
        r = _rstd(yv)
        yn = yv * r
        err = x_ref[...] + gate_ref[...] * (yn * g_ref[...]) - t_ref[...]
        loss_ref[...] += 0.5 * jnp.sum(jnp.mean(err * err, axis=-1, keepdims=True), axis=0, keepdims=True)
        dr = err / d
        do_ref[...] = dr
        dn = dr * gate_ref[...]
        vec_ref[0:1, :] += jnp.sum(dr * (yn * g_ref[...]), axis=0, keepdims=True)
        vec_ref[1:2, :] += jnp.sum(dn * yn, axis=0, keepdims=True)
        dy_ref[...] = _rms_bwd(dn * g_ref[...], yv, r).astype(bf16)

    return pl.pallas_call(
        body, name=name, grid=(s // tm,), in_specs=[_row_spec(tm, d)] * 2 + [_vec_spec(d)] * 2 + [_row_spec(tm, d)],
        out_specs=[_vec_spec(LANES), _row_spec(tm, d), _row_spec(tm, d), _vec_spec(d, 8)],
        out_shape=[jax.ShapeDtypeStruct((1, LANES), f32), jax.ShapeDtypeStruct((s, d), f32),
                   jax.ShapeDtypeStruct((s, d), bf16), jax.ShapeDtypeStruct((8, d), f32)],
        compiler_params=_params(1),
    )(x, y, g, gate, target)


def _postnorm_bwd(dres, y, g, gate, name):
    s, d = y.shape
    tm = _row_tile(s, 512)

    def body(dr_ref, y_ref, g_ref, gate_ref, dy_ref, vec_ref):
        @pl.when(pl.program_id(0) == 0)
        def _():
            vec_ref[...] = jnp.zeros_like(vec_ref)
        dr, yv = dr_ref[...], y_ref[...]
        r = _rstd(yv)
        yn = yv * r
        dn = dr * gate_ref[...]
        vec_ref[0:1, :] += jnp.sum(dr * (yn * g_ref[...]), axis=0, keepdims=True)
        vec_ref[1:2, :] += jnp.sum(dn * yn, axis=0, keepdims=True)
        dy_ref[...] = _rms_bwd(dn * g_ref[...], yv, r).astype(bf16)

    return pl.pallas_call(
        body, name=name, grid=(s // tm,), in_specs=[_row_spec(tm, d)] * 2 + [_vec_spec(d)] * 2,
        out_specs=[_row_spec(tm, d), _vec_spec(d, 8)],
        out_shape=[jax.ShapeDtypeStruct((s, d), bf16), jax.ShapeDtypeStruct((8, d), f32)], compiler_params=_params(1),
    )(dres, y, g, gate)


def _prenorm_bwd(dh, x, g, scale, dres, name):
    s, d = x.shape
    tm = _row_tile(s, 512)

    def body(dh_ref, x_ref, g_ref, sc_ref, dr_ref, dx_ref, vec_ref):
        @pl.when(pl.program_id(0) == 0)
        def _():
            vec_ref[...] = jnp.zeros_like(vec_ref)
        dhv, xv = dh_ref[...], x_ref[...]
        r = _rstd(xv)
        xn = xv * r
        dn = dhv * (1.0 + sc_ref[...])
        vec_ref[0:1, :] += jnp.sum(dhv, axis=0, keepdims=True)
        vec_ref[1:2, :] += jnp.sum(dhv * (xn * g_ref[...]), axis=0, keepdims=True)
        vec_ref[2:3, :] += jnp.sum(dn * xn, axis=0, keepdims=True)
        dx_ref[...] = dr_ref[...] + _rms_bwd(dn * g_ref[...], xv, r)

    return pl.pallas_call(
        body, name=name, grid=(s // tm,),
        in_specs=[_row_spec(tm, d)] * 2 + [_vec_spec(d)] * 2 + [_row_spec(tm, d)],
        out_specs=[_row_spec(tm, d), _vec_spec(d, 8)],
        out_shape=[jax.ShapeDtypeStruct((s, d), f32), jax.ShapeDtypeStruct((8, d), f32)], compiler_params=_params(1),
    )(dh, x, g, scale, dres)


def _lane():
    return lax.broadcasted_iota(jnp.int32, (1, LANES), 1)


def _rope_tables(pos_col, inv_freq, name):
    s = pos_col.shape[0]

    def body(p_ref, f_ref, cos_ref, sin_ref):
        ang = p_ref[...].astype(f32) * f_ref[...]
        first_half = (_lane() % HEAD_DIM) < HEAD_DIM // 2
        cos_ref[...] = jnp.cos(ang)
        sn = jnp.sin(ang)
        sin_ref[...] = jnp.where(first_half, -sn, sn)

    return pl.pallas_call(
        body, name=name, out_shape=[jax.ShapeDtypeStruct((s, LANES), f32)] * 2, compiler_params=_params(),
    )(pos_col, inv_freq)


def _swap_halves(v):
    first_half = (_lane() % HEAD_DIM) < HEAD_DIM // 2
    return jnp.where(first_half, pltpu.roll(v, LANES - HEAD_DIM // 2, axis=1), pltpu.roll(v, HEAD_DIM // 2, axis=1))


def _qkv_prep(qkv, cos, sin_s, name):
    s = qkv.shape[0]
    tm = _row_tile(s, 256)
    scale = 1.0 / math.sqrt(HEAD_DIM)

    def body(p_ref, c_ref, s_ref, qa_ref, ka_ref, va_ref, qb_ref, kb_ref, vb_ref):
        cs, sn = c_ref[...], s_ref[...]
        low = _lane() < HEAD_DIM

        def blk(j):
            return p_ref[:, j * LANES:(j + 1) * LANES]

        def rope(v):
            return v * cs + _swap_halves(v) * sn

        def expand(v):
            other = pltpu.roll(v, HEAD_DIM, axis=1)
            return jnp.where(low, v, other), jnp.where(low, other, v)

        for j in range(N_PAIRS):
            qa_ref[:, j * LANES:(j + 1) * LANES] = (rope(blk(j)) * scale).astype(bf16)
            qb_ref[:, j * LANES:(j + 1) * LANES] = (blk(6 + j) * scale).astype(bf16)
            kb_ref[:, j * LANES:(j + 1) * LANES] = blk(10 + j).astype(bf16)
            vb_ref[:, j * LANES:(j + 1) * LANES] = blk(14 + j).astype(bf16)
        k0, k1 = expand(rope(blk(4)))
        v0, v1 = expand(blk(5))
        for j in range(N_PAIRS):
            ka_ref[:, j * LANES:(j + 1) * LANES] = (k0 if j < 2 else k1).astype(bf16)
            va_ref[:, j * LANES:(j + 1) * LANES] = (v0 if j < 2 else v1).astype(bf16)

    hw = N_PAIRS * LANES
    return pl.pallas_call(
        body, name=name, grid=(s // tm,),
        in_specs=[_row_spec(tm, QKV_W), _row_spec(tm, LANES), _row_spec(tm, LANES)],
        out_specs=[_row_spec(tm, hw)] * 6, out_shape=[jax.ShapeDtypeStruct((s, hw), bf16)] * 6, compiler_params=_params(1),
    )(qkv, cos, sin_s)


def _qkv_prep_bwd(dqa_t, dka, dva, dqb_t, dkb, dvb, cos, sin_s, name):
    s = dka.shape[0]
    tm = _row_tile(s, 256)
    scale = 1.0 / math.sqrt(HEAD_DIM)
    hw = N_PAIRS * LANES
    t_spec = pl.BlockSpec((hw, tm), lambda i: (0, i))

    def body(dqa_ref, dka_ref, dva_ref, dqb_ref, dkb_ref, dvb_ref, c_ref, s_ref, o_ref):
        cs, sn = c_ref[...], s_ref[...]
        low = _lane() < HEAD_DIM

        def blk(ref, j):
            return ref[:, j * LANES:(j + 1) * LANES]

        def blk_t(ref, j):
            return ref[j * LANES:(j + 1) * LANES, :].T

        def unrope(v):
            return v * cs + _swap_halves(v * sn)

        def fold(ref):
            a, b = blk(ref, 0) + blk(ref, 1), blk(ref, 2) + blk(ref, 3)
            kv0 = a + pltpu.roll(a, HEAD_DIM, axis=1)
            kv1 = b + pltpu.roll(b, HEAD_DIM, axis=1)
            return jnp.where(low, kv0, kv1)

        for j in range(N_PAIRS):
            o_ref[:, j * LANES:(j + 1) * LANES] = (unrope(blk_t(dqa_ref, j)) * scale).astype(bf16)
            o_ref[:, (6 + j) * LANES:(7 + j) * LANES] = (blk_t(dqb_ref, j) * scale).astype(bf16)
            o_ref[:, (10 + j) * LANES:(11 + j) * LANES] = blk(dkb_ref, j).astype(bf16)
            o_ref[:, (14 + j) * LANES:(15 + j) * LANES] = blk(dvb_ref, j).astype(bf16)
        o_ref[:, 4 * LANES:5 * LANES] = unrope(fold(dka_ref)).astype(bf16)
        o_ref[:, 5 * LANES:6 * LANES] = fold(dva_ref).astype(bf16)

    return pl.pallas_call(
        body, name=name, grid=(s // tm,),
        in_specs=[t_spec, _row_spec(tm, hw), _row_spec(tm, hw), t_spec, _row_spec(tm, hw), _row_spec(tm, hw)] + [_row_spec(tm, LANES)] * 2,
        out_specs=_row_spec(tm, QKV_W), out_shape=jax.ShapeDtypeStruct((s, QKV_W), bf16), compiler_params=_params(1),
    )(dqa_t, dka, dva, dqb_t, dkb, dvb, cos, sin_s)


def _cumsum_rows(v, reverse=False):
    n = v.shape[0]
    row = lax.broadcasted_iota(jnp.int32, v.shape, 0)
    sh = 1
    while sh < n:
        if reverse:
            v = v + jnp.where(row < n - sh, pltpu.roll(v, n - sh, axis=0), 0.0)
        else:
            v = v + jnp.where(row >= sh, pltpu.roll(v, sh, axis=0), 0.0)
        sh *= 2
    return v


def _log_sigmoid(z):
    return jnp.minimum(z, 0.0) - jnp.log1p(jnp.exp(-jnp.abs(z)))


def _forget_prep(fl, bf_row, name):
    s = fl.shape[0]

    def body(f_ref, b_ref, cb_ref):
        cum = _cumsum_rows(_log_sigmoid(f_ref[...] + b_ref[...]))
        for h in range(N_HEADS):
            cb_ref[:, h * LANES:(h + 1) * LANES] = jnp.broadcast_to(cum[:, h:h + 1], (s, LANES))

    return pl.pallas_call(
        body, name=name, out_shape=jax.ShapeDtypeStruct((s, N_HEADS * LANES), f32), compiler_params=_params(),
    )(fl, bf_row)


def _forget_prep_bwd(rs, dcs, fl, bf_row, name):
    s = fl.shape[0]

    def body(r_ref, c_ref, f_ref, b_ref, df_ref, db_ref):
        eye = (lax.broadcasted_iota(jnp.int32, (N_HEADS, LANES), 0) == lax.broadcasted_iota(jnp.int32, (N_HEADS, LANES), 1)).astype(f32)
        dcum = lax.dot_general(r_ref[...], eye, _TN, precision=lax.Precision.HIGHEST, preferred_element_type=f32)
        for h in range(N_HEADS):
            dcum = dcum - jnp.where(_lane() == h, jnp.sum(c_ref[:, h * LANES:(h + 1) * LANES], axis=1, keepdims=True), 0.0)
        dlf = _cumsum_rows(dcum, reverse=True)
        z = f_ref[...] + b_ref[...]
        df = jnp.where(_lane() < N_HEADS, dlf * jax.nn.sigmoid(-z), 0.0)
        df_ref[...] = df.astype(bf16)
        db_ref[...] = jnp.zeros_like(db_ref)
        db_ref[0:1, :] = jnp.sum(df, axis=0, keepdims=True)

    return pl.pallas_call(
        body, name=name,
        out_shape=[jax.ShapeDtypeStruct((s, LANES), bf16), jax.ShapeDtypeStruct((8, LANES), f32)], compiler_params=_params(),
    )(rs, dcs, fl, bf_row)


def _tile_mask(n_keys, n_queries, off, window):
    shape = (n_keys, n_queries)
    d = lax.broadcasted_iota(jnp.int32, shape, 1) - lax.broadcasted_iota(jnp.int32, shape, 0) + off
    valid = d >= 0
    return jnp.logical_and(valid, d < window) if window else valid


def _wide(v, t):
    return jnp.concatenate([v] * (t // LANES), axis=1)


def _attn_fwd(q, k, v, name, *, cum_b=None, sink_rows=None, window=None, t=256):
    s = q.shape[0]
    t = _row_tile(s, t)
    fox, has_sink = cum_b is not None, sink_rows is not None
    assert not window or (window % LANES == 0 and LANES + window <= s)

    def body(*refs):
        q_ref, k_ref, v_ref = refs[:3]
        rest = list(refs[3:])
        cb_ref = rest.pop(0) if fox else None
        sink_ref = rest.pop(0) if has_sink else None
        o_ref, lse_ref = rest
        i = pl.program_id(1)
        low = _lane() < HEAD_DIM
        top = lax.broadcasted_iota(jnp.int32, (LANES, 1), 0) < HEAD_DIM
        q2 = q_ref[...]
        zero = jnp.zeros_like(q2)
        qms = (jnp.where(low, q2, zero), jnp.where(low, zero, q2))

        def tile(k0, n_keys, off, carry, masked, queries=slice(0, t)):
            nq = queries.stop - queries.start
            kblk, vblk = k_ref[pl.ds(k0, n_keys), :], v_ref[pl.ds(k0, n_keys), :]
            valid = _tile_mask(n_keys, nq, off, window) if masked else None
            scs = [lax.dot_general(kblk, qms[h][queries], _NT, preferred_element_type=f32) for h in range(2)]
            stats = []
            for h in range(2):
                m, l, _ = carry[h]
                sc = scs[h]
                if fox:
                    sc = sc - _wide(cb_ref[pl.ds(k0, n_keys), h * LANES:(h + 1) * LANES], nq)
                if masked:
                    sc = jnp.where(valid, sc, NEG)
                m_new = jnp.maximum(m, jnp.max(sc, axis=0, keepdims=True))
                p = jnp.exp(sc - m_new)
                alpha = jnp.exp(m - m_new)
                stats.append((m_new, alpha * l + jnp.sum(p, axis=0, keepdims=True), alpha, p.astype(bf16)))
            return tuple((m_new, l, alpha * carry[h][2] + lax.dot_general(vblk, p, _TN, preferred_element_type=f32))
                         for h, (m_new, l, alpha, p) in enumerate(stats))

        def start(nq):
            if has_sink:
                return tuple((_wide(sink_ref[h:h + 1, :], nq), jnp.ones((1, nq), f32), jnp.zeros((LANES, nq), f32))
                             for h in range(2))
            return tuple((jnp.full((1, nq), NEG, f32), jnp.zeros((1, nq), f32), jnp.zeros((LANES, nq), f32)) for h in range(2))

        def finish(carry, queries):
            (m0, l0, a0), (m1, l1, a1) = carry
            o_t = jnp.where(top, a0 * (1.0 / l0), a1 * (1.0 / l1))
            o_ref[queries, :] = o_t.T.astype(bf16)
            lse_ref[0:1, queries] = m0 + jnp.log(l0)
            lse_ref[1:2, queries] = m1 + jnp.log(l1)

        if window:
            for c in range(t // LANES):
                queries = slice(c * LANES, (c + 1) * LANES)
                q0 = i * t + c * LANES
                k0 = pl.multiple_of(jnp.maximum(q0 - window, 0), LANES)
                finish(tile(k0, LANES + window, q0 - k0, start(LANES), True, queries), queries)
        else:
            carry = lax.fori_loop(0, i, lambda kb, c: tile(pl.multiple_of(kb * t, t), t, 0, c, False), start(t))
            finish(tile(pl.multiple_of(i * t, t), t, 0, carry, True), slice(0, t))

    q_spec = pl.BlockSpec((t, LANES), lambda j, i: (i, j))
    kv_spec = pl.BlockSpec((s, LANES), lambda j, i: (0, j))
    in_specs, args = [q_spec, kv_spec, kv_spec], [q, k, v]
    if fox:
        in_specs += [pl.BlockSpec((s, 2 * LANES), lambda j, i: (0, j))]
        args += [cum_b]
    if has_sink:
        in_specs += [pl.BlockSpec((None, 2, LANES), lambda j, i: (j, 0, 0))]
        args += [sink_rows.reshape(N_PAIRS, 2, LANES)]
    return pl.pallas_call(
        body, name=name, grid=(N_PAIRS, s // t), in_specs=in_specs,
        out_specs=[q_spec, pl.BlockSpec((None, 2, t), lambda j, i: (j, 0, i))],
        out_shape=[jax.ShapeDtypeStruct((s, N_PAIRS * LANES), bf16), jax.ShapeDtypeStruct((N_PAIRS, 2, s), f32)],
        compiler_params=_params(2),
    )(*args)


def _attn_delta(do, o, name, *, lse=None, sink_rows=None):
    s, hw = do.shape
    tm = _row_tile(s, 512)
    has_sink = sink_rows is not None

    def body(*refs):
        do_ref, o_ref = refs[:2]
        if has_sink:
            lse_ref, sink_ref, dl_ref, ds_ref = refs[2:]

            @pl.when(pl.program_id(0) == 0)
            def _():
                ds_ref[...] = jnp.zeros_like(ds_ref)
        else:
            dl_ref, = refs[2:]
        for j in range(N_PAIRS):
            cols = slice(j * LANES, (j + 1) * LANES)
            prod_t = (do_ref[:, cols].astype(f32) * o_ref[:, cols].astype(f32)).T
            for h in range(2):
                dl = jnp.sum(prod_t[h * HEAD_DIM:(h + 1) * HEAD_DIM, :], axis=0, keepdims=True)
                dl_ref[j, h:h + 1, :] = dl
                if has_sink:
                    r = 2 * j + h
                    p_sink = jnp.exp(sink_ref[r:r + 1, 0:1] - lse_ref[j, h:h + 1, :])
                    ds_ref[r:r + 1, :] += -jnp.sum(p_sink * dl, axis=1, keepdims=True)

    rows_spec = pl.BlockSpec((N_PAIRS, 2, tm), lambda i: (0, 0, i))
    in_specs, args = [_row_spec(tm, hw)] * 2, [do, o]
    out_specs, out_shape = [rows_spec], [jax.ShapeDtypeStruct((N_PAIRS, 2, s), f32)]
    if has_sink:
        in_specs += [rows_spec, _vec_spec(LANES, N_HEADS)]
        args += [lse, sink_rows]
        out_specs += [_vec_spec(LANES, N_HEADS)]
        out_shape += [jax.ShapeDtypeStruct((N_HEADS, LANES), f32)]
    return pl.pallas_call(
        body, name=name, grid=(s // tm,), in_specs=in_specs, out_specs=out_specs, out_shape=out_shape,
        compiler_params=_params(1),
    )(*args)


def _attn_bwd(q, k, v, do, lse, delta, name, *, cum_b=None, window=None, t=256):
    s = q.shape[0]
    t = _row_tile(s, t)
    nblk = s // t
    fox = cum_b is not None
    assert not window or (window % LANES == 0 and LANES + window <= s)

    def body(*refs):
        k_ref, v_ref, q_ref, do_ref, lse_ref, dl_ref = refs[:6]
        rest = list(refs[6:])
        cb_ref = rest.pop(0) if fox else None
        dq_ref, dk_ref, dv_ref = rest[:3]
        dcs_ref, rs_ref = (rest[3], rest[4]) if fox else (None, None)
        b = pl.program_id(1)
        k0 = pl.multiple_of(b * t, t)

        @pl.when(b == 0)
        def _():
            dq_ref[...] = jnp.zeros_like(dq_ref)
            if fox:
                rs_ref[...] = jnp.zeros_like(rs_ref)

        dk_ref[...] = jnp.zeros_like(dk_ref)
        dv_ref[...] = jnp.zeros_like(dv_ref)
        if fox:
            dcs_ref[...] = jnp.zeros_like(dcs_ref)
        low = _lane() < HEAD_DIM
        top = lax.broadcasted_iota(jnp.int32, (LANES, 1), 0) < HEAD_DIM
        kblk, vblk = k_ref[...], v_ref[...]
        k_t = kblk.astype(f32).T.astype(bf16)
        cks = [_wide(cb_ref[pl.ds(k0, t), h * LANES:(h + 1) * LANES], t) for h in range(2)] if fox else None

        def tile(q0, n_queries, off, masked, keys=slice(0, t)):
            cols = pl.ds(q0, n_queries)
            q2, do2 = q_ref[cols, :], do_ref[cols, :]
            zero = jnp.zeros_like(q2)
            valid = _tile_mask(keys.stop - keys.start, n_queries, off, window) if masked else None
            dq_parts = []
            for h in range(2):
                qm = jnp.where(low, q2, zero) if h == 0 else jnp.where(low, zero, q2)
                dom = jnp.where(low, do2, zero) if h == 0 else jnp.where(low, zero, do2)
                sc = lax.dot_general(kblk[keys], qm, _NT, preferred_element_type=f32)
                if fox:
                    sc = sc - cks[h]
                if masked:
                    sc = jnp.where(valid, sc, NEG)
                p = jnp.exp(sc - lse_ref[h:h + 1, cols])
                dp = lax.dot_general(vblk[keys], dom, _NT, preferred_element_type=f32)
                ds = p * (dp - dl_ref[h:h + 1, cols])
                pb, dsb = p.astype(bf16), ds.astype(bf16)
                dv_ref[keys, :] += jnp.dot(pb, dom, preferred_element_type=f32)
                dk_ref[keys, :] += jnp.dot(dsb, qm, preferred_element_type=f32)
                dq_parts.append(jnp.dot(k_t[:, keys], dsb, preferred_element_type=f32))
                if fox:
                    dcs_ref[:, h * LANES:(h + 1) * LANES] += sum(ds[:, g * LANES:(g + 1) * LANES] for g in range(t // LANES))
                    rs_ref[h:h + 1, cols] += jnp.sum(ds, axis=0, keepdims=True)
            dq_ref[:, cols] += jnp.where(top, dq_parts[0], dq_parts[1])

        def later_block(qb, carry):
            tile(pl.multiple_of(qb * t, t), t, 0, False)
            return carry

        if window:
            for c in range(t // LANES):
                first = b * t + c * LANES
                q0 = pl.multiple_of(jnp.minimum(first, s - (LANES + window)), LANES)
                tile(q0, LANES + window, q0 - first, True, slice(c * LANES, (c + 1) * LANES))
        else:
            tile(k0, t, 0, True)
            lax.fori_loop(b + 1, nblk, later_block, 0)

    kv_spec = pl.BlockSpec((t, LANES), lambda j, b: (b, j))
    seq_spec = pl.BlockSpec((s, LANES), lambda j, b: (0, j))
    rows_spec = pl.BlockSpec((None, 2, s), lambda j, b: (j, 0, 0))
    hw = N_PAIRS * LANES
    in_specs, args = [kv_spec, kv_spec, seq_spec, seq_spec, rows_spec, rows_spec], [k, v, q, do, lse, delta]
    out_specs = [pl.BlockSpec((LANES, s), lambda j, b: (j, 0)), kv_spec, kv_spec]
    out_shape = [jax.ShapeDtypeStruct((hw, s), f32), jax.ShapeDtypeStruct((s, hw), f32), jax.ShapeDtypeStruct((s, hw), f32)]
    if fox:
        in_specs += [pl.BlockSpec((s, 2 * LANES), lambda j, b: (0, j))]
        args += [cum_b]
        out_specs += [pl.BlockSpec((t, 2 * LANES), lambda j, b: (b, j)), rows_spec]
        out_shape += [jax.ShapeDtypeStruct((s, N_HEADS * LANES), f32), jax.ShapeDtypeStruct((N_PAIRS, 2, s), f32)]
    return pl.pallas_call(
        body, name=name, grid=(N_PAIRS, nblk), in_specs=in_specs, out_specs=out_specs, out_shape=out_shape,
        compiler_params=_params(2),
    )(*args)


def _merge(ba, bb, gl, name):
    s, d = ba.shape
    tm = _row_tile(s, 512)

    def body(a_ref, b_ref, g_ref, o_ref):
        g0, g1 = jax.nn.sigmoid(g_ref[:, :d].astype(f32)), jax.nn.sigmoid(g_ref[:, d:].astype(f32))
        o_ref[...] = (g0 * a_ref[...].astype(f32) + g1 * b_ref[...].astype(f32)).astype(bf16)

    return pl.pallas_call(
        body, name=name, grid=(s // tm,), in_specs=[_row_spec(tm, d)] * 2 + [_row_spec(tm, 2 * d)],
        out_specs=_row_spec(tm, d), out_shape=jax.ShapeDtypeStruct((s, d), bf16), compiler_params=_params(1),
    )(ba, bb, gl)


def _merge_bwd(dm, ba, bb, gl, name):
    s, d = ba.shape
    tm = _row_tile(s, 512)

    def body(dm_ref, a_ref, b_ref, g_ref, da_ref, db_ref, dg_ref):
        dmv = dm_ref[...].astype(f32)
        g0, g1 = jax.nn.sigmoid(g_ref[:, :d].astype(f32)), jax.nn.sigmoid(g_ref[:, d:].astype(f32))
        da_ref[...] = (dmv * g0).astype(bf16)
        db_ref[...] = (dmv * g1).astype(bf16)
        dg_ref[:, :d] = (dmv * a_ref[...].astype(f32) * (g0 * (1.0 - g0))).astype(bf16)
        dg_ref[:, d:] = (dmv * b_ref[...].astype(f32) * (g1 * (1.0 - g1))).astype(bf16)

    return pl.pallas_call(
        body, name=name, grid=(s // tm,), in_specs=[_row_spec(tm, d)] * 3 + [_row_spec(tm, 2 * d)],
        out_specs=[_row_spec(tm, d)] * 2 + [_row_spec(tm, 2 * d)],
        out_shape=[jax.ShapeDtypeStruct((s, d), bf16)] * 2 + [jax.ShapeDtypeStruct((s, 2 * d), bf16)],
        compiler_params=_params(1),
    )(dm, ba, bb, gl)


GLU_TILE = 256


def _ffn_in_swiglu(h, w_t, name):
    s, d = h.shape
    f = w_t.shape[0] // 2
    tm = _row_tile(s, 2048)
    tg = GLU_TILE
    nb = f // tg

    def body(h_ref, wg_ref, wu_ref, g_ref, u_ref, act_ref):
        hv = h_ref[...]
        g = lax.dot_general(hv, wg_ref[...], _NT, preferred_element_type=f32)
        u = lax.dot_general(hv, wu_ref[...], _NT, preferred_element_type=f32)
        g_ref[...] = g.astype(bf16)
        u_ref[...] = u.astype(bf16)
        act_ref[...] = (g * jax.nn.sigmoid(g) * u).astype(bf16)

    col = pl.BlockSpec((tm, tg), lambda i, j: (i, j))
    return pl.pallas_call(
        body, name=name, grid=(s // tm, nb),
        in_specs=[pl.BlockSpec((tm, d), lambda i, j: (i, 0)), pl.BlockSpec((tg, d), lambda i, j: (j, 0)),
                  pl.BlockSpec((tg, d), lambda i, j: (j + nb, 0))],
        out_specs=[col] * 3, out_shape=[jax.ShapeDtypeStruct((s, f), bf16)] * 3, compiler_params=_params(2),
    )(h, w_t, w_t)


def _ffn_out_dgrad_swiglu(dy, w_out, g, u, name):
    s, d = dy.shape
    f = g.shape[1]
    tm = _row_tile(s, 2048)
    tg = GLU_TILE

    def body(dy_ref, w_ref, g_ref, u_ref, dg_ref, du_ref):
        dv = lax.dot_general(dy_ref[...], w_ref[...], _NT, preferred_element_type=f32)
        gv, uv = g_ref[...].astype(f32), u_ref[...].astype(f32)
        sg = jax.nn.sigmoid(gv)
        dg_ref[...] = (dv * uv * (sg * (1.0 + gv * (1.0 - sg)))).astype(bf16)
        du_ref[...] = (dv * (gv * sg)).astype(bf16)

    col = pl.BlockSpec((tm, tg), lambda i, j: (i, j))
    return pl.pallas_call(
        body, name=name, grid=(s // tm, f // tg),
        in_specs=[pl.BlockSpec((tm, d), lambda i, j: (i, 0)), pl.BlockSpec((tg, d), lambda i, j: (j, 0)), col, col],
        out_specs=[col] * 2, out_shape=[jax.ShapeDtypeStruct((s, f), bf16)] * 2, compiler_params=_params(2),
    )(dy, w_out, g, u)


def _sum_matmul(terms, name, after=None):
    s = terms[0][0].shape[0]
    d = terms[0][1].shape[1]
    k = sum(a.shape[1] for a, _, _ in terms)
    tm, tn = _matmul_tiles(s, d, k, terms[0][0].dtype.itemsize, terms[0][1].dtype.itemsize, 4)
    n = len(terms)

    def body(*refs):
        acc = jnp.dot(refs[0][...], refs[n][...], preferred_element_type=f32)
        for i in range(1, n):
            acc = acc + jnp.dot(refs[i][...], refs[n + i][...], preferred_element_type=f32)
        refs[-1][...] = acc

    extra = [] if after is None else [after]
    a_specs = [pl.BlockSpec((tm, a.shape[1]), lambda i, j: (i, 0)) for a, _, _ in terms]
    b_specs = [pl.BlockSpec((a.shape[1], tn), lambda i, j, r=r: (r, j)) for a, _, r in terms]
    return pl.pallas_call(
        body, name=name, grid=(s // tm, d // tn),
        in_specs=a_specs + b_specs + [pl.BlockSpec(memory_space=pl.ANY)] * len(extra),
        out_specs=pl.BlockSpec((tm, tn), lambda i, j: (i, j)),
        out_shape=jax.ShapeDtypeStruct((s, d), f32), compiler_params=_params(2),
    )(*[a for a, _, _ in terms], *[b for _, b, _ in terms], *extra)


def _wgrad_stack(parts, h, name):
    s, m = parts[0].shape
    d = h.shape[1]
    tm = 256
    nb = m // tm
    n = len(parts)

    def body(*refs):
        i = pl.program_id(0)
        for p in range(n):
            @pl.when(i // nb == p)
            def _(p=p):
                refs[n + 1][...] = lax.dot_general(refs[p][...], refs[n][...], _TN, preferred_element_type=f32).astype(bf16)

    a_specs = [pl.BlockSpec((s, tm), lambda i, p=p: (0, jnp.clip(i - p * nb, 0, nb - 1))) for p in range(n)]
    return pl.pallas_call(
        body, name=name, grid=(n * nb,), in_specs=a_specs + [pl.BlockSpec((s, d), lambda i: (0, 0))],
        out_specs=pl.BlockSpec((tm, d), lambda i: (i, 0)),
        out_shape=jax.ShapeDtypeStruct((n * m, d), bf16), compiler_params=_params(1),
    )(*parts, h)


def _ada_fwd(c_all, w, b, name):
    def body(c_ref, w_ref, b_ref, o_ref):
        o_ref[...] = jnp.dot(c_ref[...].astype(bf16), w_ref[...].astype(bf16), preferred_element_type=f32) + b_ref[...]

    return pl.pallas_call(
        body, name=name, out_shape=jax.ShapeDtypeStruct((c_all.shape[0], w.shape[1]), f32), compiler_params=_params(),
    )(c_all, w, b)


def _ada_wgrad(c_all, d_all, name):
    n, d = c_all.shape
    w = d_all.shape[1]

    def body(c_ref, d_ref, o_ref):
        eye = (lax.broadcasted_iota(jnp.int32, (n, n), 0) == lax.broadcasted_iota(jnp.int32, (n, n), 1)).astype(f32)
        ct = lax.dot_general(c_ref[...], eye, _TN, precision=lax.Precision.HIGHEST, preferred_element_type=f32)
        g = ct[:, 0:1] * d_ref[0:1, :]
        for bi in range(1, n):
            g = g + ct[:, bi:bi + 1] * d_ref[bi:bi + 1, :]
        o_ref[0] = g

    return pl.pallas_call(
        body, name=name, out_shape=jax.ShapeDtypeStruct((1, d, w), f32), compiler_params=_params(),
    )(c_all, d_all)


def _adamw(parts, w, m, v, name, mine=None):
    r, c = w.shape
    n_parts = parts.shape[0]
    row_tiles = [t for t in range(min(r, 256), 0, -1) if r % t == 0 and (t % 16 == 0 or t == r)]
    if row_tiles:
        tr, tc = row_tiles[0], c
    else:
        tr, tc = r, next(t for t in (256, LANES) if c % t == 0)

    def body(p_ref, *rest):
        own_ref = rest[0] if mine is not None else None
        w_ref, m_ref, v_ref, g_ref, d_ref, nm_ref, nv_ref = rest[-7:]
        if mine is not None:
            x, y, cc = _me()
            me = 4 * x + 2 * y + cc

        def part(i):
            if mine is None:
                return p_ref[i].astype(f32)
            return jnp.where(me == i, own_ref[i], p_ref[i]).astype(f32)

        g = part(0)
        for i in range(1, n_parts):
            g = g + part(i)
        mm = ADAM_B1 * m_ref[...] + (1.0 - ADAM_B1) * g
        vv = ADAM_B2 * v_ref[...] + (1.0 - ADAM_B2) * (g * g)
        m_hat = mm / (1.0 - ADAM_B1 ** ADAM_STEP)
        v_hat = vv / (1.0 - ADAM_B2 ** ADAM_STEP)
        g_ref[...] = g
        d_ref[...] = -ADAM_LR * (m_hat / (jnp.sqrt(v_hat) + ADAM_EPS) + ADAM_WD * w_ref[...])
        nm_ref[...] = mm
        nv_ref[...] = vv

    spec = pl.BlockSpec((tr, tc), lambda i, j: (i, j))
    stack = [parts] if mine is None else [parts, mine]
    return pl.pallas_call(
        body, name=name, grid=(r // tr, c // tc),
        in_specs=[pl.BlockSpec((n_parts, tr, tc), lambda i, j: (0, i, j))] * len(stack) + [spec] * 3,
        out_specs=[spec] * 4, out_shape=[jax.ShapeDtypeStruct((r, c), f32)] * 4, compiler_params=_params(2),
    )(*stack, w, m, v)


def _me():
    return lax.axis_index("x"), lax.axis_index("y"), lax.axis_index("c")


def _all_gather(arrays, name, vmem=False, after=None):
    n = len(arrays)
    space = pltpu.VMEM if vmem else pl.ANY
    extra = [] if after is None else [after]

    def body(*refs):
        ins = refs[:n]
        outs = refs[n + len(extra):2 * n + len(extra)]
        send_sems, recv_sems, local_sems = refs[2 * n + len(extra):]
        x, y, c = _me()
        me, sibling = (x, y, c), (x, y, 1 - c)
        chips = [(1 - x, y), (x, 1 - y), (1 - x, 1 - y)]

        def rows(a, dev):
            return outs[a].at[4 * dev[0] + 2 * dev[1] + dev[2]]

        def copy(a, k, block, to, src=None):
            return pltpu.make_async_remote_copy(
                src_ref=rows(a, block) if src is None else src, dst_ref=rows(a, block),
                send_sem=send_sems.at[a, k], recv_sem=recv_sems.at[a, k], device_id=to, device_id_type=MESH)

        mine = [pltpu.make_async_copy(ins[a], rows(a, me), local_sems.at[a]) for a in range(n)]
        for cp in mine:
            cp.start()
        first = []
        for a in range(n):
            first.append(copy(a, 0, me, sibling, src=ins[a]))
            first += [copy(a, 1 + j, me, (*chip, c), src=ins[a]) for j, chip in enumerate(chips)]
        for cp in first:
            cp.start()
        passed = []
        for j, chip in enumerate(chips):
            for a in range(n):
                copy(a, 1 + j, (*chip, c), me).wait_recv()
                fwd = copy(a, 4 + j, (*chip, c), sibling)
                fwd.start()
                passed.append(fwd)
        for a in range(n):
            copy(a, 0, sibling, me).wait_recv()
            for j, chip in enumerate(chips):
                copy(a, 4 + j, (*chip, 1 - c), me).wait_recv()
        for cp in first + passed:
            cp.wait_send()
        for cp in mine:
            cp.wait()

    outs = pl.pallas_call(
        body, name=name,
        in_specs=[pl.BlockSpec(memory_space=space)] * n + [pl.BlockSpec(memory_space=pl.ANY)] * len(extra),
        out_specs=[pl.BlockSpec(memory_space=space)] * n,
        out_shape=[jax.ShapeDtypeStruct((N_DEV,) + a.shape, a.dtype) for a in arrays],
        scratch_shapes=[pltpu.SemaphoreType.DMA((n, 7)), pltpu.SemaphoreType.DMA((n, 7)), pltpu.SemaphoreType.DMA((n,))],
        compiler_params=pltpu.CompilerParams(vmem_limit_bytes=VMEM_LIMIT),
    )(*arrays, *extra)
    return list(outs)


_FLIPS = ((0, 0, 1), (1, 0, 0), (0, 1, 0), (1, 1, 0), (1, 0, 1), (0, 1, 1), (1, 1, 1))
_HBM = pl.BlockSpec(memory_space=pltpu.HBM)
_SEM = pl.BlockSpec(memory_space=pltpu.SEMAPHORE)


def _exchange_copies(scatter, srcs, lands, send_sems, recv_sems):
    x, y, c = _me()
    me_row = 4 * x + 2 * y + c
    out = []
    for k, (fx, fy, fc) in enumerate(_FLIPS):
        peer = (x ^ fx, y ^ fy, c ^ fc)
        peer_row = 4 * peer[0] + 2 * peer[1] + peer[2]
        for a in range(len(srcs)):
            out.append(pltpu.make_async_remote_copy(
                src_ref=srcs[a].at[peer_row] if scatter else srcs[a], dst_ref=lands[a].at[me_row],
                send_sem=send_sems.at[7 * a + k], recv_sem=recv_sems.at[7 * a + k], device_id=peer, device_id_type=MESH))
    return out


def _exchange_start(arrays, scatter, name, after=None):
    n = len(arrays)
    lands = [lax.empty(a.shape if scatter else (N_DEV,) + a.shape, a.dtype) for a in arrays]
    extra = [] if after is None else [after]

    def body(*refs):
        srcs, zones = refs[:n], refs[n:2 * n]
        send_sems, recv_sems = refs[2 * n + len(extra)], refs[2 * n + len(extra) + 1]
        token = refs[-1]
        for cp in _exchange_copies(scatter, srcs, zones, send_sems, recv_sems):
            cp.start()
        token[...] = jnp.zeros_like(token)

    thru = [pltpu.HBM(a.shape, a.dtype) for a in list(arrays) + lands]
    outs = pl.pallas_call(
        body, name=name,
        out_shape=(pltpu.SemaphoreType.DMA((7 * n,)), pltpu.SemaphoreType.DMA((7 * n,)), *thru, jax.ShapeDtypeStruct((8, LANES), f32)),
        in_specs=[_HBM] * (2 * n) + [pl.BlockSpec(memory_space=pl.ANY)] * len(extra),
        out_specs=(_SEM, _SEM, *[_HBM] * (2 * n), pl.BlockSpec(memory_space=pltpu.VMEM)),
        input_output_aliases={i: 2 + i for i in range(2 * n)},
        compiler_params=pltpu.CompilerParams(has_side_effects=pltpu.SideEffectType.DATAFLOW_SIDE_EFFECTING),
    )(*[pltpu.with_memory_space_constraint(a, pltpu.HBM) for a in list(arrays) + lands], *extra)
    return dict(n=n, scatter=scatter, sems=outs[:2], srcs=outs[2:2 + n], lands=outs[2 + n:2 + 2 * n], token=outs[-1])


def _exchange_wait(handle, after, name):
    n, scatter = handle["n"], handle["scatter"]

    def body(*refs):
        srcs, zones = refs[:n], refs[n:2 * n]
        send_sems, recv_sems = refs[2 * n], refs[2 * n + 1]
        for cp in _exchange_copies(scatter, srcs, zones, send_sems, recv_sems):
            cp.wait_send()
            cp.wait_recv()

    thru = [pltpu.HBM(a.shape, a.dtype) for a in list(handle["srcs"]) + list(handle["lands"])]
    outs = pl.pallas_call(
        body, name=name, out_shape=tuple(thru),
        in_specs=[_HBM] * (2 * n) + [_SEM, _SEM, pl.BlockSpec(memory_space=pl.ANY)], out_specs=tuple([_HBM] * (2 * n)),
        input_output_aliases={i: i for i in range(2 * n)},
        compiler_params=pltpu.CompilerParams(has_side_effects=pltpu.SideEffectType.DATAFLOW_SIDE_EFFECTING),
    )(*handle["srcs"], *handle["lands"], *handle["sems"], after)
    return list(outs[n:])


def _cols_from_shards(g):
    return jnp.transpose(g, (1, 0, 2)).reshape(g.shape[1], -1)


def _shards_from_cols(a):
    return jnp.transpose(a.reshape(a.shape[0], N_DEV, -1), (1, 0, 2))


def _local_step(x, positions, ada, g_pre_mix, g_post_mix, b_f, sinks, g_pre_ffn, g_post_ffn, target,
                w_in_t, late_weights, on_grads):
    s, d = x.shape
    row = lambda v: v.reshape(1, -1)
    shift_m, scale_m, gate_m, shift_f, scale_f, gate_f = (ada[i:i + 1] for i in range(6))
    w_gate_t, w_qkv_t = w_in_t[F_OFF + N_HEADS:], w_in_t[:QKV_W]
    w_f_t = jnp.pad(w_in_t[F_OFF:F_OFF + N_HEADS], ((0, LANES - N_HEADS), (0, 0)))
    bf_row = jnp.pad(row(b_f), ((0, 0), (0, LANES - N_HEADS)))
    sink_rows = jnp.broadcast_to(sinks.reshape(N_HEADS, 1).astype(f32), (N_HEADS, LANES))
    inv_freq = 1.0 / (ROPE_THETA ** (jnp.arange(0, HEAD_DIM, 2, dtype=f32) / HEAD_DIM))
    cos, sin_s = _rope_tables(positions.reshape(s, 1), jnp.tile(inv_freq, 4).reshape(1, LANES), "rope_tables")

    h1 = _prenorm(x, row(g_pre_mix), scale_m, shift_m, "prenorm_mix")
    gl = _matmul(h1, w_gate_t, "nt", bf16, "proj_gate")
    qkv = _matmul(h1, w_qkv_t, "nt", f32, "proj_qkv")
    fl = _matmul(h1, w_f_t, "nt", f32, "proj_forget")
    qa, ka, va, qb, kb, vb = _qkv_prep(qkv, cos, sin_s, "qkv_prep")
    cum_b = _forget_prep(fl, bf_row, "forget_prep")
    o_a, lse_a = _attn_fwd(qa, ka, va, "swa_fwd", sink_rows=sink_rows, window=WINDOW, t=512)
    o_b, lse_b = _attn_fwd(qb, kb, vb, "fox_fwd", cum_b=cum_b, t=512)
    w_branch_a, w_branch_b, w_out, w_ffn_in_t, w_ffn_out = late_weights(o_b)
    ba = _matmul(o_a, w_branch_a, "nn", bf16, "branch_a")
    bb = _matmul(o_b, w_branch_b, "nn", bf16, "branch_b")
    merged = _merge(ba, bb, gl, "merge")
    y1 = _matmul(merged, w_out, "nn", f32, "out_proj")
    x2 = _postnorm_res(x, y1, row(g_post_mix), gate_m, "postnorm_mix")

    h2 = _prenorm(x2, row(g_pre_ffn), scale_f, shift_f, "prenorm_ffn")
    g_ff, u_ff, act = _ffn_in_swiglu(h2, w_ffn_in_t, "ffn_in_swiglu")
    y2 = _matmul(act, w_ffn_out, "nn", f32, "ffn_out")
    loss_row, d_out, d_y2, vec_pf = _loss_tail(x2, y2, row(g_post_ffn), gate_f, target, "loss_tail")

    g_w_ffn_out = _matmul(act, d_y2, "tn", bf16, "ffn_out_wgrad")
    dg_ff, du_ff = _ffn_out_dgrad_swiglu(d_y2, w_ffn_out, g_ff, u_ff, "ffn_out_dgrad_swiglu")
    g_w_ffn_in_t = _wgrad_stack([dg_ff, du_ff], h2, "ffn_in_wgrad")
    sent = on_grads(dict(w_ffn_in=g_w_ffn_in_t, w_ffn_out=g_w_ffn_out))
    d_h2 = _sum_matmul([(dg_ff, w_ffn_in_t, 0), (du_ff, w_ffn_in_t, 1)], "ffn_in_dgrad", after=sent)
    d_x2, vec_nf = _prenorm_bwd(d_h2, x2, row(g_pre_ffn), scale_f, d_out, "prenorm_ffn_bwd")

    d_y1, vec_pm = _postnorm_bwd(d_x2, y1, row(g_post_mix), gate_m, "postnorm_mix_bwd")
    g_w_out = _matmul(merged, d_y1, "tn", bf16, "out_proj_wgrad")
    d_merged = _matmul(d_y1, w_out, "nt", bf16, "out_proj_dgrad")
    d_ba, d_bb, dgl = _merge_bwd(d_merged, ba, bb, gl, "merge_bwd")
    g_w_branch_a = _matmul(o_a, d_ba, "tn", bf16, "branch_a_wgrad")
    g_w_branch_b = _matmul(o_b, d_bb, "tn", bf16, "branch_b_wgrad")
    sent = on_grads(dict(w_out=g_w_out, w_branch_a=g_w_branch_a, w_branch_b=g_w_branch_b))
    d_oa = _matmul(d_ba, w_branch_a, "nt", bf16, "branch_a_dgrad", after=sent)
    d_ob = _matmul(d_bb, w_branch_b, "nt", bf16, "branch_b_dgrad", after=sent)
    delta_a, d_sink = _attn_delta(d_oa, o_a, "swa_delta", lse=lse_a, sink_rows=sink_rows)
    delta_b, = _attn_delta(d_ob, o_b, "fox_delta")
    dqa_t, dka, dva = _attn_bwd(qa, ka, va, d_oa, lse_a, delta_a, "swa_bwd", window=WINDOW, t=512)
    dqb_t, dkb, dvb, dcs, rs = _attn_bwd(qb, kb, vb, d_ob, lse_b, delta_b, "fox_bwd", cum_b=cum_b, t=512)
    dqkv = _qkv_prep_bwd(dqa_t, dka, dva, dqb_t, dkb, dvb, cos, sin_s, "qkv_prep_bwd")
    dfl, vec_bf = _forget_prep_bwd(rs.reshape(N_HEADS, s), dcs, fl, bf_row, "forget_prep_bwd")
    g_w_in_t = jnp.concatenate([_matmul(dqkv, h1, "tn", bf16, "qkv_wgrad"), _matmul(dfl, h1, "tn", bf16, "forget_wgrad")[:N_HEADS],
                                _matmul(dgl, h1, "tn", bf16, "gate_wgrad")], axis=0)
    sent = on_grads(dict(w_in=g_w_in_t))
    d_h1 = _sum_matmul([(dgl, w_gate_t, 0), (dqkv, w_qkv_t, 0), (dfl, w_f_t, 0)], "in_proj_dgrad", after=sent)
    grad_x, vec_nm = _prenorm_bwd(d_h1, x, row(g_pre_mix), scale_m, d_x2, "prenorm_mix_bwd")

    d_ada = jnp.concatenate([vec_nm[0], vec_nm[1], vec_pm[0], vec_nf[0], vec_nf[1], vec_pf[0]])
    small = dict(b_ada=d_ada, g_pre_mix=vec_nm[2], g_post_mix=vec_pm[1], g_pre_ffn=vec_nf[2], g_post_ffn=vec_pf[1],
                 b_f=vec_bf[0, :N_HEADS], sinks=d_sink[:, 0], loss=loss_row[0, :1])
    return grad_x, small


_SMALL = (("b_ada", 6144), ("g_pre_mix", 1024), ("g_post_mix", 1024), ("g_pre_ffn", 1024), ("g_post_ffn", 1024),
          ("b_f", 128), ("sinks", 128), ("loss", 128))
_SMALL_ROWS = 88


def _pack_small(vals):
    parts = [jnp.pad(vals[k].reshape(-1).astype(f32), (0, n - vals[k].size)) for k, n in _SMALL]
    flat = jnp.concatenate(parts)
    return jnp.pad(flat, (0, _SMALL_ROWS * LANES - flat.size)).reshape(_SMALL_ROWS, LANES)


def _unpack_small(slab, shapes):
    flat, out, off = slab.reshape(-1), {}, 0
    for k, n in _SMALL:
        size = math.prod(shapes[k])
        out[k] = flat[off:off + size].reshape(shapes[k])
        off += n
    return out


def kernel(x, c, positions, w_ada, b_ada, g_pre_mix, g_post_mix, w_in, b_f, sinks, w_branch_a, w_branch_b, w_out, g_pre_ffn, g_post_ffn, w_ffn_in, w_ffn_out, loss_target, m_w_ada, m_b_ada, m_g_pre_mix, m_g_post_mix, m_w_in, m_b_f, m_sinks, m_w_branch_a, m_w_branch_b, m_w_out, m_g_pre_ffn, m_g_post_ffn, m_w_ffn_in, m_w_ffn_out, v_w_ada, v_b_ada, v_g_pre_mix, v_g_post_mix, v_w_in, v_b_f, v_sinks, v_w_branch_a, v_w_branch_b, v_w_out, v_g_pre_ffn, v_g_post_ffn, v_w_ffn_in, v_w_ffn_out):
    xi, yi, ci = _me()
    me = 4 * xi + 2 * yi + ci
    d = D_MODEL
    ada_w = w_ada.shape[2]

    c_all, = _all_gather([c], "gather_c", vmem=True)
    c_all = c_all.reshape(N_DEV, d)
    b_mine = lax.dynamic_slice(b_ada, (0, me * ada_w), (1, ada_w))
    ada_cols = _ada_fwd(c_all, w_ada[0], b_mine, "ada_fwd")
    ada_all, = _all_gather([ada_cols], "gather_ada", vmem=True)
    ada = lax.dynamic_index_in_dim(ada_all, me, axis=1, keepdims=False).reshape(6, d)

    transposed = ("w_in", "w_ffn_in")
    tr = lambda a: jnp.transpose(a[0])

    g_in, = _all_gather([tr(w_in).astype(bf16)], "gather_w_in")
    late = [w.astype(bf16) for w in (w_branch_a[0], w_branch_b[0], w_out[0], tr(w_ffn_in), w_ffn_out[0])]
    late_h = _exchange_start(late, False, "gather_late_start", after=g_in)

    def mine_into(zone, block):
        return lax.dynamic_update_index_in_dim(zone, block, me, 0)

    def rows_from_shards(g):
        return g.reshape(g.shape[0] * g.shape[1], g.shape[2])

    def late_weights(after):
        zones = _exchange_wait(late_h, after, "gather_late_wait")
        g_ba, g_bb, g_out, g_fi, g_fo = (mine_into(z, w) for z, w in zip(zones, late))
        return (_cols_from_shards(g_ba), _cols_from_shards(g_bb), rows_from_shards(g_out), rows_from_shards(g_fi),
                rows_from_shards(g_fo))

    row_sharded = ("w_out", "w_ffn_out") + transposed
    in_flight = []

    def on_grads(group):
        sends = [g.reshape(N_DEV, g.shape[0] // N_DEV, g.shape[1]) if nm in row_sharded else _shards_from_cols(g)
                 for nm, g in group.items()]
        handle = _exchange_start(sends, True, "scatter_start_%d" % len(in_flight))
        in_flight.append((list(group), sends, handle))
        return handle["token"]

    grad_x, small = _local_step(
        x[0], positions[0], ada + late_h["token"][0, 0], g_pre_mix[0], g_post_mix[0], b_f[0], sinks[0], g_pre_ffn[0],
        g_post_ffn[0], loss_target[0], rows_from_shards(g_in), late_weights, on_grads)

    ws = dict(w_in=(w_in, m_w_in, v_w_in), w_branch_a=(w_branch_a, m_w_branch_a, v_w_branch_a),
              w_branch_b=(w_branch_b, m_w_branch_b, v_w_branch_b), w_out=(w_out, m_w_out, v_w_out),
              w_ffn_in=(w_ffn_in, m_w_ffn_in, v_w_ffn_in), w_ffn_out=(w_ffn_out, m_w_ffn_out, v_w_ffn_out))
    res = {}

    def finish_group(gi, after):
        names, sends, handle = in_flight[gi]
        zones = _exchange_wait(handle, after, "scatter_wait_%d" % gi)
        for nm, zone, sent in zip(names, zones, sends):
            w, m, v = (tr(a) if nm in transposed else a[0] for a in ws[nm])
            out = _adamw(zone, w, m, v, "adamw_" + nm, mine=sent)
            after = out[0]
            res[nm] = [jnp.transpose(o) for o in out] if nm in transposed else out
        return after

    done = finish_group(1, finish_group(0, grad_x))

    slab_all, = _all_gather([_pack_small(small)], "gather_small", vmem=True, after=done)
    small_w = dict(b_ada=b_ada, g_pre_mix=g_pre_mix, g_post_mix=g_post_mix, g_pre_ffn=g_pre_ffn, g_post_ffn=g_post_ffn,
                   b_f=b_f, sinks=sinks, loss=jnp.zeros((1,), f32))
    small_m = dict(b_ada=m_b_ada, g_pre_mix=m_g_pre_mix, g_post_mix=m_g_post_mix, g_pre_ffn=m_g_pre_ffn,
                   g_post_ffn=m_g_post_ffn, b_f=m_b_f, sinks=m_sinks, loss=jnp.zeros((1,), f32))
    small_v = dict(b_ada=v_b_ada, g_pre_mix=v_g_pre_mix, g_post_mix=v_g_post_mix, g_pre_ffn=v_g_pre_ffn,
                   g_post_ffn=v_g_post_ffn, b_f=v_b_f, sinks=v_sinks, loss=jnp.ones((1,), f32))
    shapes = {k: small_w[k].shape for k, _ in _SMALL}
    s_out = _adamw(slab_all, _pack_small(small_w), _pack_small(small_m), _pack_small(small_v), "adamw_small")
    s_grad, s_delta, s_m, s_v = (_unpack_small(o, shapes) for o in s_out)

    d_ada_all = lax.dynamic_slice(slab_all[:, :6144 // LANES, :].reshape(N_DEV, 6144), (0, me * ada_w), (N_DEV, ada_w))
    ada_parts = _ada_wgrad(c_all, d_ada_all, "ada_wgrad")

    res["w_ada"] = _adamw(ada_parts, w_ada[0], m_w_ada[0], v_w_ada[0], "adamw_w_ada")
    finish_group(2, res["w_ada"][0])

    order = ["w_ada", "b_ada", "g_pre_mix", "g_post_mix", "w_in", "b_f", "sinks", "w_branch_a", "w_branch_b", "w_out",
             "g_pre_ffn", "g_post_ffn", "w_ffn_in", "w_ffn_out"]
    outs = [s_grad["loss"].reshape(()), grad_x[None]]
    for which, small_o in enumerate((s_grad, s_delta, s_m, s_v)):
        for nm in order:
            outs.append(res[nm][which][None] if nm in res else small_o[nm])
    return tuple(outs)
```

```python
import functools
import math

import jax
import jax.numpy as jnp
from jax import lax
from jax.experimental import pallas as pl
from jax.experimental.pallas import tpu as pltpu

f32 = jnp.float32
bf16 = jnp.bfloat16

D_MODEL = 1024
HEAD_DIM = 64
N_HEADS = 8
N_PAIRS = 4
QKV_W = 2304
GATE_W = 2048
F_OFF = 2304
IN_W = 4360
WINDOW = 128
ROPE_THETA = 10000.0
RMS_EPS = 1e-6
D_FF = 2816
N_DEV = 8
ADAM_LR, ADAM_B1, ADAM_B2, ADAM_EPS, ADAM_WD, ADAM_STEP = 0.001, 0.9, 0.999, 1e-08, 0.01, 10
NEG = -1e30
LANES = 128
VMEM_LIMIT = 48 * 1024 * 1024
MESH = pl.DeviceIdType.MESH

_NT = (((1,), (1,)), ((), ()))
_TN = (((0,), (0,)), ((), ()))


def _params(n_grid=0):
    sem = ("arbitrary",) * n_grid if n_grid else None
    return pltpu.CompilerParams(dimension_semantics=sem, vmem_limit_bytes=VMEM_LIMIT)


def _row_tile(s, want):
    t = min(s, want)
    assert s % t == 0, (s, t)
    return t


MATMUL_VMEM_BUDGET = 40 * 1024 * 1024


def _matmul_tiles(m, n, k, a_item, b_item, o_item):
    def tiles(d):
        return [t for t in range(LANES, min(d, 2048) + 1, LANES) if d % t == 0] or [d]

    best = None
    for tm in tiles(m):
        for tn in tiles(n):
            vmem = 2 * (tm * k * a_item + tn * k * b_item + tm * tn * o_item) + tm * tn * 4
            if vmem > MATMUL_VMEM_BUDGET:
                continue
            traffic = m * k * a_item + n * k * b_item * (1 if tn == n else m // tm) + m * n * o_item
            steps = (m // tm) * (n // tn)
            key = (traffic, 0, steps) if steps >= 4 else (traffic, 1, -steps)
            if best is None or key < best[0]:
                best = (key, tm, tn)
    assert best is not None, (m, n, k)
    return best[1], best[2]


def _matmul(a, b, mode, out_dtype, name, after=None):
    if mode == "nn":
        (m, k), n = a.shape, b.shape[1]
    elif mode == "nt":
        (m, k), n = a.shape, b.shape[0]
    else:
        (k, m), n = a.shape, b.shape[1]
    tm, tn = _matmul_tiles(m, n, k, a.dtype.itemsize, b.dtype.itemsize, jnp.dtype(out_dtype).itemsize)
    if mode == "nn":
        a_spec, b_spec, dims = pl.BlockSpec((tm, k), lambda i, j: (i, 0)), pl.BlockSpec((k, tn), lambda i, j: (0, j)), None
    elif mode == "nt":
        a_spec, b_spec, dims = pl.BlockSpec((tm, k), lambda i, j: (i, 0)), pl.BlockSpec((tn, k), lambda i, j: (j, 0)), _NT
    else:
        a_spec, b_spec, dims = pl.BlockSpec((k, tm), lambda i, j: (0, i)), pl.BlockSpec((k, tn), lambda i, j: (0, j)), _TN

    def body(a_ref, b_ref, *rest):
        o_ref = rest[-1]
        av, bv = a_ref[...].astype(bf16), b_ref[...].astype(bf16)
        if dims is None:
            r = jnp.dot(av, bv, preferred_element_type=f32)
        else:
            r = lax.dot_general(av, bv, dims, preferred_element_type=f32)
        o_ref[...] = r.astype(out_dtype)

    extra = [] if after is None else [after]
    return pl.pallas_call(
        body, name=name, grid=(m // tm, n // tn), in_specs=[a_spec, b_spec] + [pl.BlockSpec(memory_space=pl.ANY)] * len(extra),
        out_specs=pl.BlockSpec((tm, tn), lambda i, j: (i, j)),
        out_shape=jax.ShapeDtypeStruct((m, n), out_dtype), compiler_params=_params(2),
    )(a, b, *extra)


def _rstd(v):
    return lax.rsqrt(jnp.mean(v * v, axis=-1, keepdims=True) + RMS_EPS)


def _row_spec(tm, d):
    return pl.BlockSpec((tm, d), lambda i: (i, 0))


def _vec_spec(d, rows=1):
    return pl.BlockSpec((rows, d), lambda i: (0, 0))


def _prenorm(x, g, scale, shift, name):
    s, d = x.shape
    tm = _row_tile(s, 512)

    def body(x_ref, g_ref, sc_ref, sh_ref, h_ref):
        xv = x_ref[...]
        h = (xv * _rstd(xv) * g_ref[...]) * (1.0 + sc_ref[...]) + sh_ref[...]
        h_ref[...] = h.astype(bf16)

    return pl.pallas_call(
        body, name=name, grid=(s // tm,), in_specs=[_row_spec(tm, d)] + [_vec_spec(d)] * 3,
        out_specs=_row_spec(tm, d), out_shape=jax.ShapeDtypeStruct((s, d), bf16), compiler_params=_params(1),
    )(x, g, scale, shift)


def _postnorm_prenorm(x, y, g_post, gate, g_pre, scale, shift, name):
    s, d = x.shape
    tm = _row_tile(s, 512)

    def body(x_ref, y_ref, gp_ref, gate_ref, g_ref, sc_ref, sh_ref, x2_ref, h_ref):
        yv = y_ref[...]
        x2 = x_ref[...] + gate_ref[...] * (yv * _rstd(yv) * gp_ref[...])
        x2_ref[...] = x2
        h_ref[...] = ((x2 * _rstd(x2) * g_ref[...]) * (1.0 + sc_ref[...]) + sh_ref[...]).astype(bf16)

    return pl.pallas_call(
        body, name=name, grid=(s // tm,), in_specs=[_row_spec(tm, d)] * 2 + [_vec_spec(d)] * 5,
        out_specs=[_row_spec(tm, d)] * 2,
        out_shape=[jax.ShapeDtypeStruct((s, d), f32), jax.ShapeDtypeStruct((s, d), bf16)], compiler_params=_params(1),
    )(x, y, g_post, gate, g_pre, scale, shift)


def _rms_bwd(u, v, r):
    return r * u - v * (r * r * r) * jnp.mean(u * v, axis=-1, keepdims=True)


def _loss_tail(x, y, g, gate, target, name):
    s, d = x.shape
    tm = _row_tile(s, 512)

    def body(x_ref, y_ref, g_ref, gate_ref, t_ref, loss_ref, do_ref, dy_ref, vec_ref):
        @pl.when(pl.program_id(0) == 0)
        def _():
            loss_ref[...] = jnp.zeros_like(loss_ref)
            vec_ref[...] = jnp.zeros_like(vec_ref)
        yv = y_ref[...]
        r = _rstd(yv)
        yn = yv * r
        err = x_ref[...] + gate_ref[...] * (yn * g_ref[...]) - t_ref[...]
        loss_ref[...] += 0.5 * jnp.sum(jnp.mean(err * err, axis=-1, keepdims=True), axis=0, keepdims=True)
        dr = err / d
        do_ref[...] = dr
        dn = dr * gate_ref[...]
        vec_ref[0:1, :] += jnp.sum(dr * (yn * g_ref[...]), axis=0, keepdims=True)
        vec_ref[1:2, :] += jnp.sum(dn * yn, axis=0, keepdims=True)
        dy_ref[...] = _rms_bwd(dn * g_ref[...], yv, r).astype(bf16)

    return pl.pallas_call(
        body, name=name, grid=(s // tm,), in_specs=[_row_spec(tm, d)] * 2 + [_vec_spec(d)] * 2 + [_row_spec(tm, d)],
        out_specs=[_vec_spec(LANES), _row_spec(tm, d), _row_spec(tm, d), _vec_spec(d, 8)],
        out_shape=[jax.ShapeDtypeStruct((1, LANES), f32), jax.ShapeDtypeStruct((s, d), f32),
                   jax.ShapeDtypeStruct((s, d), bf16), jax.ShapeDtypeStruct((8, d), f32)],
        compiler_params=_params(1),
    )(x, y, g, gate, target)


def _prenorm_bwd(dh, x, g, scale, dres, name, below=None):
    s, d = x.shape
    tm = _row_tile(s, 512)

    def body(dh_ref, x_ref, g_ref, sc_ref, dr_ref, *rest):
        dx_ref, vec_ref = rest[-4:-2] if below else rest[-2:]

        @pl.when(pl.program_id(0) == 0)
        def _():
            vec_ref[...] = jnp.zeros_like(vec_ref)
            if below:
                rest[-1][...] = jnp.zeros_like(rest[-1])
        dhv, xv = dh_ref[...], x_ref[...]
        r = _rstd(xv)
        xn = xv * r
        dn = dhv * (1.0 + sc_ref[...])
        vec_ref[0:1, :] += jnp.sum(dhv, axis=0, keepdims=True)
        vec_ref[1:2, :] += jnp.sum(dhv * (xn * g_ref[...]), axis=0, keepdims=True)
        vec_ref[2:3, :] += jnp.sum(dn * xn, axis=0, keepdims=True)
        dx = dr_ref[...] + _rms_bwd(dn * g_ref[...], xv, r)
        dx_ref[...] = dx
        if below:
            y_ref, gp_ref, gate_ref, _, _, dy_ref, vec2_ref = rest
            yv = y_ref[...]
            ry = _rstd(yv)
            yn = yv * ry
            dny = dx * gate_ref[...]
            vec2_ref[0:1, :] += jnp.sum(dx * (yn * gp_ref[...]), axis=0, keepdims=True)
            vec2_ref[1:2, :] += jnp.sum(dny * yn, axis=0, keepdims=True)
            dy_ref[...] = _rms_bwd(dny * gp_ref[...], yv, ry).astype(bf16)

    in_specs = [_row_spec(tm, d)] * 2 + [_vec_spec(d)] * 2 + [_row_spec(tm, d)]
    out_specs = [_row_spec(tm, d), _vec_spec(d, 8)]
    out_shape = [jax.ShapeDtypeStruct((s, d), f32), jax.ShapeDtypeStruct((8, d), f32)]
    args = [dh, x, g, scale, dres]
    if below:
        in_specs += [_row_spec(tm, d)] + [_vec_spec(d)] * 2
        out_specs += [_row_spec(tm, d), _vec_spec(d, 8)]
        out_shape += [jax.ShapeDtypeStruct((s, d), bf16), jax.ShapeDtypeStruct((8, d), f32)]
        args += list(below)
    return pl.pallas_call(
        body, name=name, grid=(s // tm,), in_specs=in_specs, out_specs=out_specs, out_shape=out_shape,
        compiler_params=_params(1),
    )(*args)


def _lane():
    return lax.broadcasted_iota(jnp.int32, (1, LANES), 1)


def _rope_tables(pos_col, inv_freq, name):
    s = pos_col.shape[0]

    def body(p_ref, f_ref, cos_ref, sin_ref):
        ang = p_ref[...].astype(f32) * f_ref[...]
        first_half = (_lane() % HEAD_DIM) < HEAD_DIM // 2
        cos_ref[...] = jnp.cos(ang)
        sn = jnp.sin(ang)
        sin_ref[...] = jnp.where(first_half, -sn, sn)

    return pl.pallas_call(
        body, name=name, out_shape=[jax.ShapeDtypeStruct((s, LANES), f32)] * 2, compiler_params=_params(),
    )(pos_col, inv_freq)


def _swap_halves(v):
    first_half = (_lane() % HEAD_DIM) < HEAD_DIM // 2
    return jnp.where(first_half, pltpu.roll(v, LANES - HEAD_DIM // 2, axis=1), pltpu.roll(v, HEAD_DIM // 2, axis=1))


def _qkv_prep(qkv, cos, sin_s, name):
    s = qkv.shape[0]
    tm = _row_tile(s, 256)
    scale = 1.0 / math.sqrt(HEAD_DIM)

    def body(p_ref, c_ref, s_ref, qa_ref, ka_ref, va_ref, qb_ref, kb_ref, vb_ref):
        cs, sn = c_ref[...], s_ref[...]
        low = _lane() < HEAD_DIM

        def blk(j):
            return p_ref[:, j * LANES:(j + 1) * LANES]

        def rope(v):
            return v * cs + _swap_halves(v) * sn

        def expand(v):
            other = pltpu.roll(v, HEAD_DIM, axis=1)
            return jnp.where(low, v, other), jnp.where(low, other, v)

        for j in range(N_PAIRS):
            qa_ref[:, j * LANES:(j + 1) * LANES] = (rope(blk(j)) * scale).astype(bf16)
            qb_ref[:, j * LANES:(j + 1) * LANES] = (blk(6 + j) * scale).astype(bf16)
            kb_ref[:, j * LANES:(j + 1) * LANES] = blk(10 + j).astype(bf16)
            vb_ref[:, j * LANES:(j + 1) * LANES] = blk(14 + j).astype(bf16)
        k0, k1 = expand(rope(blk(4)))
        v0, v1 = expand(blk(5))
        for j in range(N_PAIRS):
            ka_ref[:, j * LANES:(j + 1) * LANES] = (k0 if j < 2 else k1).astype(bf16)
            va_ref[:, j * LANES:(j + 1) * LANES] = (v0 if j < 2 else v1).astype(bf16)

    hw = N_PAIRS * LANES
    return pl.pallas_call(
        body, name=name, grid=(s // tm,),
        in_specs=[_row_spec(tm, QKV_W), _row_spec(tm, LANES), _row_spec(tm, LANES)],
        out_specs=[_row_spec(tm, hw)] * 6, out_shape=[jax.ShapeDtypeStruct((s, hw), bf16)] * 6, compiler_params=_params(1),
    )(qkv, cos, sin_s)


def _qkv_prep_bwd(dqa_t, dka, dva, dqb_t, dkb, dvb, cos, sin_s, name):
    s = dka.shape[0]
    tm = _row_tile(s, 256)
    scale = 1.0 / math.sqrt(HEAD_DIM)
    hw = N_PAIRS * LANES
    t_spec = pl.BlockSpec((hw, tm), lambda i: (0, i))

    def body(dqa_ref, dka_ref, dva_ref, dqb_ref, dkb_ref, dvb_ref, c_ref, s_ref, o_ref):
        cs, sn = c_ref[...], s_ref[...]
        low = _lane() < HEAD_DIM

        def blk(ref, j):
            return ref[:, j * LANES:(j + 1) * LANES]

        def blk_t(ref, j):
            return ref[j * LANES:(j + 1) * LANES, :].T

        def unrope(v):
            return v * cs + _swap_halves(v * sn)

        def fold(ref):
            a, b = blk(ref, 0) + blk(ref, 1), blk(ref, 2) + blk(ref, 3)
            kv0 = a + pltpu.roll(a, HEAD_DIM, axis=1)
            kv1 = b + pltpu.roll(b, HEAD_DIM, axis=1)
            return jnp.where(low, kv0, kv1)

        for j in range(N_PAIRS):
            o_ref[:, j * LANES:(j + 1) * LANES] = (unrope(blk_t(dqa_ref, j)) * scale).astype(bf16)
            o_ref[:, (6 + j) * LANES:(7 + j) * LANES] = (blk_t(dqb_ref, j) * scale).astype(bf16)
            o_ref[:, (10 + j) * LANES:(11 + j) * LANES] = blk(dkb_ref, j).astype(bf16)
            o_ref[:, (14 + j) * LANES:(15 + j) * LANES] = blk(dvb_ref, j).astype(bf16)
        o_ref[:, 4 * LANES:5 * LANES] = unrope(fold(dka_ref)).astype(bf16)
        o_ref[:, 5 * LANES:6 * LANES] = fold(dva_ref).astype(bf16)

    return pl.pallas_call(
        body, name=name, grid=(s // tm,),
        in_specs=[t_spec, _row_spec(tm, hw), _row_spec(tm, hw), t_spec, _row_spec(tm, hw), _row_spec(tm, hw)] + [_row_spec(tm, LANES)] * 2,
        out_specs=_row_spec(tm, QKV_W), out_shape=jax.ShapeDtypeStruct((s, QKV_W), bf16), compiler_params=_params(1),
    )(dqa_t, dka, dva, dqb_t, dkb, dvb, cos, sin_s)


def _cumsum_rows(v, reverse=False):
    n = v.shape[0]
    row = lax.broadcasted_iota(jnp.int32, v.shape, 0)
    sh = 1
    while sh < n:
        if reverse:
            v = v + jnp.where(row < n - sh, pltpu.roll(v, n - sh, axis=0), 0.0)
        else:
            v = v + jnp.where(row >= sh, pltpu.roll(v, sh, axis=0), 0.0)
        sh *= 2
    return v


def _log_sigmoid(z):
    return jnp.minimum(z, 0.0) - jnp.log1p(jnp.exp(-jnp.abs(z)))


def _forget_prep(fl, bf_row, name):
    s = fl.shape[0]

    def body(f_ref, b_ref, cb_ref):
        cum = _cumsum_rows(_log_sigmoid(f_ref[...] + b_ref[...]))
        for h in range(N_HEADS):
            cb_ref[:, h * LANES:(h + 1) * LANES] = jnp.broadcast_to(cum[:, h:h + 1], (s, LANES))

    return pl.pallas_call(
        body, name=name, out_shape=jax.ShapeDtypeStruct((s, N_HEADS * LANES), f32), compiler_params=_params(),
    )(fl, bf_row)


def _forget_prep_bwd(rs, dcs, fl, bf_row, name):
    s = fl.shape[0]

    def body(r_ref, c_ref, f_ref, b_ref, df_ref, db_ref):
        eye = (lax.broadcasted_iota(jnp.int32, (N_HEADS, LANES), 0) == lax.broadcasted_iota(jnp.int32, (N_HEADS, LANES), 1)).astype(f32)
        dcum = lax.dot_general(r_ref[...], eye, _TN, precision=lax.Precision.HIGHEST, preferred_element_type=f32)
        for h in range(N_HEADS):
            dcum = dcum - jnp.where(_lane() == h, jnp.sum(c_ref[:, h * LANES:(h + 1) * LANES], axis=1, keepdims=True), 0.0)
        dlf = _cumsum_rows(dcum, reverse=True)
        z = f_ref[...] + b_ref[...]
        df = jnp.where(_lane() < N_HEADS, dlf * jax.nn.sigmoid(-z), 0.0)
        df_ref[...] = df.astype(bf16)
        db_ref[...] = jnp.zeros_like(db_ref)
        db_ref[0:1, :] = jnp.sum(df, axis=0, keepdims=True)

    return pl.pallas_call(
        body, name=name,
        out_shape=[jax.ShapeDtypeStruct((s, LANES), bf16), jax.ShapeDtypeStruct((8, LANES), f32)], compiler_params=_params(),
    )(rs, dcs, fl, bf_row)


def _tile_mask(n_keys, n_queries, off, window):
    shape = (n_keys, n_queries)
    d = lax.broadcasted_iota(jnp.int32, shape, 1) - lax.broadcasted_iota(jnp.int32, shape, 0) + off
    valid = d >= 0
    return jnp.logical_and(valid, d < window) if window else valid


def _wide(v, t):
    return jnp.concatenate([v] * (t // LANES), axis=1)


def _attn_fwd(q, k, v, name, *, cum_b=None, sink_rows=None, window=None, t=256):
    s = q.shape[0]
    t = _row_tile(s, t)
    fox, has_sink = cum_b is not None, sink_rows is not None
    assert not window or (window % LANES == 0 and LANES + window <= s)

    def body(*refs):
        q_ref, k_ref, v_ref = refs[:3]
        rest = list(refs[3:])
        cb_ref = rest.pop(0) if fox else None
        sink_ref = rest.pop(0) if has_sink else None
        o_ref, lse_ref = rest
        i = pl.program_id(1)
        low = _lane() < HEAD_DIM
        top = lax.broadcasted_iota(jnp.int32, (LANES, 1), 0) < HEAD_DIM
        q2 = q_ref[...]
        zero = jnp.zeros_like(q2)
        qms = (jnp.where(low, q2, zero), jnp.where(low, zero, q2))

        def tile(k0, n_keys, off, carry, masked, queries=slice(0, t)):
            nq = queries.stop - queries.start
            kblk, vblk = k_ref[pl.ds(k0, n_keys), :], v_ref[pl.ds(k0, n_keys), :]
            valid = _tile_mask(n_keys, nq, off, window) if masked else None
            def scores(h):
                return lax.dot_general(kblk, qms[h][queries], _NT, preferred_element_type=f32)

            def softmax(h, sc):
                m, l, _ = carry[h]
                if fox:
                    sc = sc - _wide(cb_ref[pl.ds(k0, n_keys), h * LANES:(h + 1) * LANES], nq)
                if masked:
                    sc = jnp.where(valid, sc, NEG)
                m_new = jnp.maximum(m, jnp.max(sc, axis=0, keepdims=True))
                p = jnp.exp(sc - m_new)
                alpha = jnp.exp(m - m_new)
                return m_new, alpha * l + jnp.sum(p, axis=0, keepdims=True), alpha, p.astype(bf16)

            def update(h, m_new, l, alpha, p):
                return m_new, l, alpha * carry[h][2] + lax.dot_general(vblk, p, _TN, preferred_element_type=f32)

            if window:
                return tuple(update(h, *softmax(h, scores(h))) for h in range(2))
            scs = [scores(h) for h in range(2)]
            stats = [softmax(h, scs[h]) for h in range(2)]
            return tuple(update(h, *stats[h]) for h in range(2))

        def start(nq):
            if has_sink:
                return tuple((_wide(sink_ref[h:h + 1, :], nq), jnp.ones((1, nq), f32), jnp.zeros((LANES, nq), f32))
                             for h in range(2))
            return tuple((jnp.full((1, nq), NEG, f32), jnp.zeros((1, nq), f32), jnp.zeros((LANES, nq), f32)) for h in range(2))

        def finish(carry, queries):
            (m0, l0, a0), (m1, l1, a1) = carry
            o_t = jnp.where(top, a0 * (1.0 / l0), a1 * (1.0 / l1))
            o_ref[queries, :] = o_t.T.astype(bf16)
            lse_ref[0:1, queries] = m0 + jnp.log(l0)
            lse_ref[1:2, queries] = m1 + jnp.log(l1)

        if window:
            for c in range(t // LANES):
                queries = slice(c * LANES, (c + 1) * LANES)
                q0 = i * t + c * LANES
                k0 = pl.multiple_of(jnp.maximum(q0 - window, 0), LANES)
                finish(tile(k0, LANES + window, q0 - k0, start(LANES), True, queries), queries)
        else:
            carry = lax.fori_loop(0, i, lambda kb, c: tile(pl.multiple_of(kb * t, t), t, 0, c, False), start(t))
            finish(tile(pl.multiple_of(i * t, t), t, 0, carry, True), slice(0, t))

    q_spec = pl.BlockSpec((t, LANES), lambda j, i: (i, j))
    kv_spec = pl.BlockSpec((s, LANES), lambda j, i: (0, j))
    in_specs, args = [q_spec, kv_spec, kv_spec], [q, k, v]
    if fox:
        in_specs += [pl.BlockSpec((s, 2 * LANES), lambda j, i: (0, j))]
        args += [cum_b]
    if has_sink:
        in_specs += [pl.BlockSpec((None, 2, LANES), lambda j, i: (j, 0, 0))]
        args += [sink_rows.reshape(N_PAIRS, 2, LANES)]
    return pl.pallas_call(
        body, name=name, grid=(N_PAIRS, s // t), in_specs=in_specs,
        out_specs=[q_spec, pl.BlockSpec((None, 2, t), lambda j, i: (j, 0, i))],
        out_shape=[jax.ShapeDtypeStruct((s, N_PAIRS * LANES), bf16), jax.ShapeDtypeStruct((N_PAIRS, 2, s), f32)],
        compiler_params=_params(2),
    )(*args)


def _attn_delta(do, o, name, *, lse=None, sink_rows=None):
    s, hw = do.shape
    tm = _row_tile(s, 512)
    has_sink = sink_rows is not None

    def body(*refs):
        do_ref, o_ref = refs[:2]
        if has_sink:
            lse_ref, sink_ref, dl_ref, ds_ref = refs[2:]

            @pl.when(pl.program_id(0) == 0)
            def _():
                ds_ref[...] = jnp.zeros_like(ds_ref)
        else:
            dl_ref, = refs[2:]
        for j in range(N_PAIRS):
            cols = slice(j * LANES, (j + 1) * LANES)
            prod_t = (do_ref[:, cols].astype(f32) * o_ref[:, cols].astype(f32)).T
            for h in range(2):
                dl = jnp.sum(prod_t[h * HEAD_DIM:(h + 1) * HEAD_DIM, :], axis=0, keepdims=True)
                dl_ref[j, h:h + 1, :] = dl
                if has_sink:
                    r = 2 * j + h
                    p_sink = jnp.exp(sink_ref[r:r + 1, 0:1] - lse_ref[j, h:h + 1, :])
                    ds_ref[r:r + 1, :] += -jnp.sum(p_sink * dl, axis=1, keepdims=True)

    rows_spec = pl.BlockSpec((N_PAIRS, 2, tm), lambda i: (0, 0, i))
    in_specs, args = [_row_spec(tm, hw)] * 2, [do, o]
    out_specs, out_shape = [rows_spec], [jax.ShapeDtypeStruct((N_PAIRS, 2, s), f32)]
    if has_sink:
        in_specs += [rows_spec, _vec_spec(LANES, N_HEADS)]
        args += [lse, sink_rows]
        out_specs += [_vec_spec(LANES, N_HEADS)]
        out_shape += [jax.ShapeDtypeStruct((N_HEADS, LANES), f32)]
    return pl.pallas_call(
        body, name=name, grid=(s // tm,), in_specs=in_specs, out_specs=out_specs, out_shape=out_shape,
        compiler_params=_params(1),
    )(*args)


def _attn_bwd(q, k, v, do, lse, delta, name, *, cum_b=None, window=None, t=256):
    s = q.shape[0]
    t = _row_tile(s, t)
    nblk = s // t
    fox = cum_b is not None
    assert not window or (window % LANES == 0 and LANES + window <= s)

    def body(*refs):
        k_ref, v_ref, q_ref, do_ref, lse_ref, dl_ref = refs[:6]
        rest = list(refs[6:])
        cb_ref = rest.pop(0) if fox else None
        dq_ref, dk_ref, dv_ref = rest[:3]
        dcs_ref, rs_ref = (rest[3], rest[4]) if fox else (None, None)
        b = pl.program_id(1)
        k0 = pl.multiple_of(b * t, t)

        @pl.when(b == 0)
        def _():
            dq_ref[...] = jnp.zeros_like(dq_ref)
            if fox:
                rs_ref[...] = jnp.zeros_like(rs_ref)

        dk_ref[...] = jnp.zeros_like(dk_ref)
        dv_ref[...] = jnp.zeros_like(dv_ref)
        if fox:
            dcs_ref[...] = jnp.zeros_like(dcs_ref)
        low = _lane() < HEAD_DIM
        top = lax.broadcasted_iota(jnp.int32, (LANES, 1), 0) < HEAD_DIM
        kblk, vblk = k_ref[...], v_ref[...]
        k_t = kblk.astype(f32).T.astype(bf16)
        cks = [_wide(cb_ref[pl.ds(k0, t), h * LANES:(h + 1) * LANES], t) for h in range(2)] if fox else None

        def tile(q0, n_queries, off, masked, keys=slice(0, t)):
            cols = pl.ds(q0, n_queries)
            q2, do2 = q_ref[cols, :], do_ref[cols, :]
            zero = jnp.zeros_like(q2)
            valid = _tile_mask(keys.stop - keys.start, n_queries, off, window) if masked else None
            dq_parts = []
            for h in range(2):
                qm = jnp.where(low, q2, zero) if h == 0 else jnp.where(low, zero, q2)
                dom = jnp.where(low, do2, zero) if h == 0 else jnp.where(low, zero, do2)
                sc = lax.dot_general(kblk[keys], qm, _NT, preferred_element_type=f32)
                if fox:
                    sc = sc - cks[h]
                if masked:
                    sc = jnp.where(valid, sc, NEG)
                p = jnp.exp(sc - lse_ref[h:h + 1, cols])
                dp = lax.dot_general(vblk[keys], dom, _NT, preferred_element_type=f32)
                ds = p * (dp - dl_ref[h:h + 1, cols])
                pb, dsb = p.astype(bf16), ds.astype(bf16)
                dv_ref[keys, :] += jnp.dot(pb, dom, preferred_element_type=f32)
                dk_ref[keys, :] += jnp.dot(dsb, qm, preferred_element_type=f32)
                dq_parts.append(jnp.dot(k_t[:, keys], dsb, preferred_element_type=f32))
                if fox:
                    dcs_ref[:, h * LANES:(h + 1) * LANES] += sum(ds[:, g * LANES:(g + 1) * LANES] for g in range(t // LANES))
                    rs_ref[h:h + 1, cols] += jnp.sum(ds, axis=0, keepdims=True)
            dq_ref[:, cols] += jnp.where(top, dq_parts[0], dq_parts[1])

        def later_block(qb, carry):
            tile(pl.multiple_of(qb * t, t), t, 0, False)
            return carry

        if window:
            for c in range(t // LANES):
                first = b * t + c * LANES
                q0 = pl.multiple_of(jnp.minimum(first, s - (LANES + window)), LANES)
                tile(q0, LANES + window, q0 - first, True, slice(c * LANES, (c + 1) * LANES))
        else:
            tile(k0, t, 0, True)
            lax.fori_loop(b + 1, nblk, later_block, 0)

    kv_spec = pl.BlockSpec((t, LANES), lambda j, b: (b, j))
    seq_spec = pl.BlockSpec((s, LANES), lambda j, b: (0, j))
    rows_spec = pl.BlockSpec((None, 2, s), lambda j, b: (j, 0, 0))
    hw = N_PAIRS * LANES
    in_specs, args = [kv_spec, kv_spec, seq_spec, seq_spec, rows_spec, rows_spec], [k, v, q, do, lse, delta]
    out_specs = [pl.BlockSpec((LANES, s), lambda j, b: (j, 0)), kv_spec, kv_spec]
    out_shape = [jax.ShapeDtypeStruct((hw, s), f32), jax.ShapeDtypeStruct((s, hw), f32), jax.ShapeDtypeStruct((s, hw), f32)]
    if fox:
        in_specs += [pl.BlockSpec((s, 2 * LANES), lambda j, b: (0, j))]
        args += [cum_b]
        out_specs += [pl.BlockSpec((t, 2 * LANES), lambda j, b: (b, j)), rows_spec]
        out_shape += [jax.ShapeDtypeStruct((s, N_HEADS * LANES), f32), jax.ShapeDtypeStruct((N_PAIRS, 2, s), f32)]
    return pl.pallas_call(
        body, name=name, grid=(N_PAIRS, nblk), in_specs=in_specs, out_specs=out_specs, out_shape=out_shape,
        compiler_params=_params(2),
    )(*args)


def _merge(ba, bb, gl, name):
    s, d = ba.shape
    tm = _row_tile(s, 512)

    def body(a_ref, b_ref, g_ref, o_ref):
        g0, g1 = jax.nn.sigmoid(g_ref[:, :d].astype(f32)), jax.nn.sigmoid(g_ref[:, d:].astype(f32))
        o_ref[...] = (g0 * a_ref[...].astype(f32) + g1 * b_ref[...].astype(f32)).astype(bf16)

    return pl.pallas_call(
        body, name=name, grid=(s // tm,), in_specs=[_row_spec(tm, d)] * 2 + [_row_spec(tm, 2 * d)],
        out_specs=_row_spec(tm, d), out_shape=jax.ShapeDtypeStruct((s, d), bf16), compiler_params=_params(1),
    )(ba, bb, gl)


def _merge_bwd(dm, ba, bb, gl, name):
    s, d = ba.shape
    tm = _row_tile(s, 512)

    def body(dm_ref, a_ref, b_ref, g_ref, da_ref, db_ref, dg_ref):
        dmv = dm_ref[...].astype(f32)
        g0, g1 = jax.nn.sigmoid(g_ref[:, :d].astype(f32)), jax.nn.sigmoid(g_ref[:, d:].astype(f32))
        da_ref[...] = (dmv * g0).astype(bf16)
        db_ref[...] = (dmv * g1).astype(bf16)
        dg_ref[:, :d] = (dmv * a_ref[...].astype(f32) * (g0 * (1.0 - g0))).astype(bf16)
        dg_ref[:, d:] = (dmv * b_ref[...].astype(f32) * (g1 * (1.0 - g1))).astype(bf16)

    return pl.pallas_call(
        body, name=name, grid=(s // tm,), in_specs=[_row_spec(tm, d)] * 3 + [_row_spec(tm, 2 * d)],
        out_specs=[_row_spec(tm, d)] * 2 + [_row_spec(tm, 2 * d)],
        out_shape=[jax.ShapeDtypeStruct((s, d), bf16)] * 2 + [jax.ShapeDtypeStruct((s, 2 * d), bf16)],
        compiler_params=_params(1),
    )(dm, ba, bb, gl)


GLU_TILE = 256


def _ffn_in_swiglu(h, w_t, name):
    s, d = h.shape
    f = w_t.shape[0] // 2
    tm = _row_tile(s, 2048)
    tg = GLU_TILE
    nb = f // tg

    def body(h_ref, wg_ref, wu_ref, g_ref, u_ref, act_ref):
        hv = h_ref[...]
        g = lax.dot_general(hv, wg_ref[...], _NT, preferred_element_type=f32)
        u = lax.dot_general(hv, wu_ref[...], _NT, preferred_element_type=f32)
        g_ref[...] = g.astype(bf16)
        u_ref[...] = u.astype(bf16)
        act_ref[...] = (g * jax.nn.sigmoid(g) * u).astype(bf16)

    col = pl.BlockSpec((tm, tg), lambda i, j: (i, j))
    return pl.pallas_call(
        body, name=name, grid=(s // tm, nb),
        in_specs=[pl.BlockSpec((tm, d), lambda i, j: (i, 0)), pl.BlockSpec((tg, d), lambda i, j: (j, 0)),
                  pl.BlockSpec((tg, d), lambda i, j: (j + nb, 0))],
        out_specs=[col] * 3, out_shape=[jax.ShapeDtypeStruct((s, f), bf16)] * 3, compiler_params=_params(2),
    )(h, w_t, w_t)


def _ffn_out_dgrad_swiglu(dy, w_out, g, u, name):
    s, d = dy.shape
    f = g.shape[1]
    tm = _row_tile(s, 2048)
    tg = GLU_TILE

    def body(dy_ref, w_ref, g_ref, u_ref, dg_ref, du_ref):
        dv = lax.dot_general(dy_ref[...], w_ref[...], _NT, preferred_element_type=f32)
        gv, uv = g_ref[...].astype(f32), u_ref[...].astype(f32)
        sg = jax.nn.sigmoid(gv)
        dg_ref[...] = (dv * uv * (sg * (1.0 + gv * (1.0 - sg)))).astype(bf16)
        du_ref[...] = (dv * (gv * sg)).astype(bf16)

    col = pl.BlockSpec((tm, tg), lambda i, j: (i, j))
    return pl.pallas_call(
        body, name=name, grid=(s // tm, f // tg),
        in_specs=[pl.BlockSpec((tm, d), lambda i, j: (i, 0)), pl.BlockSpec((tg, d), lambda i, j: (j, 0)), col, col],
        out_specs=[col] * 2, out_shape=[jax.ShapeDtypeStruct((s, f), bf16)] * 2, compiler_params=_params(2),
    )(dy, w_out, g, u)


def _sum_matmul(terms, name, after=None):
    s = terms[0][0].shape[0]
    d = terms[0][1].shape[1]
    k = sum(a.shape[1] for a, _, _ in terms)
    tm, tn = _matmul_tiles(s, d, k, terms[0][0].dtype.itemsize, terms[0][1].dtype.itemsize, 4)
    n = len(terms)

    def body(*refs):
        acc = jnp.dot(refs[0][...], refs[n][...], preferred_element_type=f32)
        for i in range(1, n):
            acc = acc + jnp.dot(refs[i][...], refs[n + i][...], preferred_element_type=f32)
        refs[-1][...] = acc

    extra = [] if after is None else [after]
    a_specs = [pl.BlockSpec((tm, a.shape[1]), lambda i, j: (i, 0)) for a, _, _ in terms]
    b_specs = [pl.BlockSpec((a.shape[1], tn), lambda i, j, r=r: (r, j)) for a, _, r in terms]
    return pl.pallas_call(
        body, name=name, grid=(s // tm, d // tn),
        in_specs=a_specs + b_specs + [pl.BlockSpec(memory_space=pl.ANY)] * len(extra),
        out_specs=pl.BlockSpec((tm, tn), lambda i, j: (i, j)),
        out_shape=jax.ShapeDtypeStruct((s, d), f32), compiler_params=_params(2),
    )(*[a for a, _, _ in terms], *[b for _, b, _ in terms], *extra)


def _wgrad_stack(parts, h, name):
    s, m = parts[0].shape
    d = h.shape[1]
    tm = 256
    nb = m // tm
    n = len(parts)

    def body(*refs):
        i = pl.program_id(0)
        for p in range(n):
            @pl.when(i // nb == p)
            def _(p=p):
                refs[n + 1][...] = lax.dot_general(refs[p][...], refs[n][...], _TN, preferred_element_type=f32).astype(bf16)

    a_specs = [pl.BlockSpec((s, tm), lambda i, p=p: (0, jnp.clip(i - p * nb, 0, nb - 1))) for p in range(n)]
    return pl.pallas_call(
        body, name=name, grid=(n * nb,), in_specs=a_specs + [pl.BlockSpec((s, d), lambda i: (0, 0))],
        out_specs=pl.BlockSpec((tm, d), lambda i: (i, 0)),
        out_shape=jax.ShapeDtypeStruct((n * m, d), bf16), compiler_params=_params(1),
    )(*parts, h)


def _ada_fwd(c_all, w, b, name):
    def body(c_ref, w_ref, b_ref, o_ref):
        o_ref[...] = jnp.dot(c_ref[...].astype(bf16), w_ref[...].astype(bf16), preferred_element_type=f32) + b_ref[...]

    return pl.pallas_call(
        body, name=name, out_shape=jax.ShapeDtypeStruct((c_all.shape[0], w.shape[1]), f32), compiler_params=_params(),
    )(c_all, w, b)


def _ada_wgrad(c_all, d_all, name):
    n, d = c_all.shape
    w = d_all.shape[1]

    def body(c_ref, d_ref, o_ref):
        eye = (lax.broadcasted_iota(jnp.int32, (n, n), 0) == lax.broadcasted_iota(jnp.int32, (n, n), 1)).astype(f32)
        ct = lax.dot_general(c_ref[...], eye, _TN, precision=lax.Precision.HIGHEST, preferred_element_type=f32)
        g = ct[:, 0:1] * d_ref[0:1, :]
        for bi in range(1, n):
            g = g + ct[:, bi:bi + 1] * d_ref[bi:bi + 1, :]
        o_ref[0] = g

    return pl.pallas_call(
        body, name=name, out_shape=jax.ShapeDtypeStruct((1, d, w), f32), compiler_params=_params(),
    )(c_all, d_all)


def _adamw(parts, w, m, v, name, mine=None):
    r, c = w.shape
    n_parts = parts.shape[0]
    row_tiles = [t for t in range(min(r, 256), 0, -1) if r % t == 0 and (t % 16 == 0 or t == r)]
    if row_tiles:
        tr, tc = row_tiles[0], c
    else:
        tr, tc = r, next(t for t in (256, LANES) if c % t == 0)

    def body(p_ref, *rest):
        own_ref = rest[0] if mine is not None else None
        w_ref, m_ref, v_ref, g_ref, d_ref, nm_ref, nv_ref = rest[-7:]
        if mine is not None:
            x, y, cc = _me()
            me = 4 * x + 2 * y + cc

        def part(i):
            if mine is None:
                return p_ref[i].astype(f32)
            return jnp.where(me == i, own_ref[i], p_ref[i]).astype(f32)

        g = part(0)
        for i in range(1, n_parts):
            g = g + part(i)
        mm = ADAM_B1 * m_ref[...] + (1.0 - ADAM_B1) * g
        vv = ADAM_B2 * v_ref[...] + (1.0 - ADAM_B2) * (g * g)
        m_hat = mm / (1.0 - ADAM_B1 ** ADAM_STEP)
        v_hat = vv / (1.0 - ADAM_B2 ** ADAM_STEP)
        g_ref[...] = g
        d_ref[...] = -ADAM_LR * (m_hat / (jnp.sqrt(v_hat) + ADAM_EPS) + ADAM_WD * w_ref[...])
        nm_ref[...] = mm
        nv_ref[...] = vv

    spec = pl.BlockSpec((tr, tc), lambda i, j: (i, j))
    stack = [parts] if mine is None else [parts, mine]
    return pl.pallas_call(
        body, name=name, grid=(r // tr, c // tc),
        in_specs=[pl.BlockSpec((n_parts, tr, tc), lambda i, j: (0, i, j))] * len(stack) + [spec] * 3,
        out_specs=[spec] * 4, out_shape=[jax.ShapeDtypeStruct((r, c), f32)] * 4, compiler_params=_params(2),
    )(*stack, w, m, v)


def _me():
    return lax.axis_index("x"), lax.axis_index("y"), lax.axis_index("c")


def _all_gather(arrays, name, vmem=False, after=None):
    n = len(arrays)
    space = pltpu.VMEM if vmem else pl.ANY
    extra = [] if after is None else [after]

    def body(*refs):
        ins = refs[:n]
        outs = refs[n + len(extra):2 * n + len(extra)]
        send_sems, recv_sems, local_sems = refs[2 * n + len(extra):]
        x, y, c = _me()
        me, sibling = (x, y, c), (x, y, 1 - c)
        chips = [(1 - x, y), (x, 1 - y), (1 - x, 1 - y)]

        def rows(a, dev):
            return outs[a].at[4 * dev[0] + 2 * dev[1] + dev[2]]

        def copy(a, k, block, to, src=None):
            return pltpu.make_async_remote_copy(
                src_ref=rows(a, block) if src is None else src, dst_ref=rows(a, block),
                send_sem=send_sems.at[a, k], recv_sem=recv_sems.at[a, k], device_id=to, device_id_type=MESH)

        mine = [pltpu.make_async_copy(ins[a], rows(a, me), local_sems.at[a]) for a in range(n)]
        for cp in mine:
            cp.start()
        first = []
        for a in range(n):
            first.append(copy(a, 0, me, sibling, src=ins[a]))
            first += [copy(a, 1 + j, me, (*chip, c), src=ins[a]) for j, chip in enumerate(chips)]
        for cp in first:
            cp.start()
        passed = []
        for j, chip in enumerate(chips):
            for a in range(n):
                copy(a, 1 + j, (*chip, c), me).wait_recv()
                fwd = copy(a, 4 + j, (*chip, c), sibling)
                fwd.start()
                passed.append(fwd)
        for a in range(n):
            copy(a, 0, sibling, me).wait_recv()
            for j, chip in enumerate(chips):
                copy(a, 4 + j, (*chip, 1 - c), me).wait_recv()
        for cp in first + passed:
            cp.wait_send()
        for cp in mine:
            cp.wait()

    outs = pl.pallas_call(
        body, name=name,
        in_specs=[pl.BlockSpec(memory_space=space)] * n + [pl.BlockSpec(memory_space=pl.ANY)] * len(extra),
        out_specs=[pl.BlockSpec(memory_space=space)] * n,
        out_shape=[jax.ShapeDtypeStruct((N_DEV,) + a.shape, a.dtype) for a in arrays],
        scratch_shapes=[pltpu.SemaphoreType.DMA((n, 7)), pltpu.SemaphoreType.DMA((n, 7)), pltpu.SemaphoreType.DMA((n,))],
        compiler_params=pltpu.CompilerParams(vmem_limit_bytes=VMEM_LIMIT),
    )(*arrays, *extra)
    return list(outs)


_FLIPS = ((0, 0, 1), (1, 0, 0), (0, 1, 0), (1, 1, 0), (1, 0, 1), (0, 1, 1), (1, 1, 1))
_HBM = pl.BlockSpec(memory_space=pltpu.HBM)
_SEM = pl.BlockSpec(memory_space=pltpu.SEMAPHORE)


def _exchange_copies(scatter, srcs, lands, send_sems, recv_sems):
    x, y, c = _me()
    me_row = 4 * x + 2 * y + c
    out = []
    for k, (fx, fy, fc) in enumerate(_FLIPS):
        peer = (x ^ fx, y ^ fy, c ^ fc)
        peer_row = 4 * peer[0] + 2 * peer[1] + peer[2]
        for a in range(len(srcs)):
            out.append(pltpu.make_async_remote_copy(
                src_ref=srcs[a].at[peer_row] if scatter else srcs[a], dst_ref=lands[a].at[me_row],
                send_sem=send_sems.at[7 * a + k], recv_sem=recv_sems.at[7 * a + k], device_id=peer, device_id_type=MESH))
    return out


def _exchange_start(arrays, scatter, name, after=None):
    n = len(arrays)
    lands = [lax.empty(a.shape if scatter else (N_DEV,) + a.shape, a.dtype) for a in arrays]
    extra = [] if after is None else [after]

    def body(*refs):
        srcs, zones = refs[:n], refs[n:2 * n]
        send_sems, recv_sems = refs[2 * n + len(extra)], refs[2 * n + len(extra) + 1]
        token = refs[-1]
        for cp in _exchange_copies(scatter, srcs, zones, send_sems, recv_sems):
            cp.start()
        token[...] = jnp.zeros_like(token)

    thru = [pltpu.HBM(a.shape, a.dtype) for a in list(arrays) + lands]
    outs = pl.pallas_call(
        body, name=name,
        out_shape=(pltpu.SemaphoreType.DMA((7 * n,)), pltpu.SemaphoreType.DMA((7 * n,)), *thru, jax.ShapeDtypeStruct((8, LANES), f32)),
        in_specs=[_HBM] * (2 * n) + [pl.BlockSpec(memory_space=pl.ANY)] * len(extra),
        out_specs=(_SEM, _SEM, *[_HBM] * (2 * n), pl.BlockSpec(memory_space=pltpu.VMEM)),
        input_output_aliases={i: 2 + i for i in range(2 * n)},
        compiler_params=pltpu.CompilerParams(has_side_effects=pltpu.SideEffectType.DATAFLOW_SIDE_EFFECTING),
    )(*[pltpu.with_memory_space_constraint(a, pltpu.HBM) for a in list(arrays) + lands], *extra)
    return dict(n=n, scatter=scatter, sems=outs[:2], srcs=outs[2:2 + n], lands=outs[2 + n:2 + 2 * n], token=outs[-1])


def _exchange_wait(handle, after, name):
    n, scatter = handle["n"], handle["scatter"]

    def body(*refs):
        srcs, zones = refs[:n], refs[n:2 * n]
        send_sems, recv_sems = refs[2 * n], refs[2 * n + 1]
        for cp in _exchange_copies(scatter, srcs, zones, send_sems, recv_sems):
            cp.wait_send()
            cp.wait_recv()

    thru = [pltpu.HBM(a.shape, a.dtype) for a in list(handle["srcs"]) + list(handle["lands"])]
    outs = pl.pallas_call(
        body, name=name, out_shape=tuple(thru),
        in_specs=[_HBM] * (2 * n) + [_SEM, _SEM, pl.BlockSpec(memory_space=pl.ANY)], out_specs=tuple([_HBM] * (2 * n)),
        input_output_aliases={i: i for i in range(2 * n)},
        compiler_params=pltpu.CompilerParams(has_side_effects=pltpu.SideEffectType.DATAFLOW_SIDE_EFFECTING),
    )(*handle["srcs"], *handle["lands"], *handle["sems"], after)
    return list(outs[n:])


def _cols_from_shards(g):
    return jnp.transpose(g, (1, 0, 2)).reshape(g.shape[1], -1)


def _shards_from_cols(a):
    return jnp.transpose(a.reshape(a.shape[0], N_DEV, -1), (1, 0, 2))


def _local_step(x, positions, ada, g_pre_mix, g_post_mix, b_f, sinks, g_pre_ffn, g_post_ffn, target,
                w_in_t, late_weights, on_grads):
    s, d = x.shape
    row = lambda v: v.reshape(1, -1)
    shift_m, scale_m, gate_m, shift_f, scale_f, gate_f = (ada[i:i + 1] for i in range(6))
    w_gate_t, w_qkv_t = w_in_t[F_OFF + N_HEADS:], w_in_t[:QKV_W]
    w_f_t = jnp.pad(w_in_t[F_OFF:F_OFF + N_HEADS], ((0, LANES - N_HEADS), (0, 0)))
    bf_row = jnp.pad(row(b_f), ((0, 0), (0, LANES - N_HEADS)))
    sink_rows = jnp.broadcast_to(sinks.reshape(N_HEADS, 1).astype(f32), (N_HEADS, LANES))
    inv_freq = 1.0 / (ROPE_THETA ** (jnp.arange(0, HEAD_DIM, 2, dtype=f32) / HEAD_DIM))
    cos, sin_s = _rope_tables(positions.reshape(s, 1), jnp.tile(inv_freq, 4).reshape(1, LANES), "rope_tables")

    h1 = _prenorm(x, row(g_pre_mix), scale_m, shift_m, "prenorm_mix")
    gl = _matmul(h1, w_gate_t, "nt", bf16, "proj_gate")
    qkv = _matmul(h1, w_qkv_t, "nt", f32, "proj_qkv")
    fl = _matmul(h1, w_f_t, "nt", f32, "proj_forget")
    qa, ka, va, qb, kb, vb = _qkv_prep(qkv, cos, sin_s, "qkv_prep")
    cum_b = _forget_prep(fl, bf_row, "forget_prep")
    o_a, lse_a = _attn_fwd(qa, ka, va, "swa_fwd", sink_rows=sink_rows, window=WINDOW, t=512)
    o_b, lse_b = _attn_fwd(qb, kb, vb, "fox_fwd", cum_b=cum_b, t=512)
    w_branch_a, w_branch_b, w_out, w_ffn_in_t, w_ffn_out = late_weights(o_b)
    ba = _matmul(o_a, w_branch_a, "nn", bf16, "branch_a")
    bb = _matmul(o_b, w_branch_b, "nn", bf16, "branch_b")
    merged = _merge(ba, bb, gl, "merge")
    y1 = _matmul(merged, w_out, "nn", f32, "out_proj")

    x2, h2 = _postnorm_prenorm(x, y1, row(g_post_mix), gate_m, row(g_pre_ffn), scale_f, shift_f, "postnorm_mix_prenorm_ffn")
    g_ff, u_ff, act = _ffn_in_swiglu(h2, w_ffn_in_t, "ffn_in_swiglu")
    y2 = _matmul(act, w_ffn_out, "nn", f32, "ffn_out")
    loss_row, d_out, d_y2, vec_pf = _loss_tail(x2, y2, row(g_post_ffn), gate_f, target, "loss_tail")

    g_w_ffn_out = _matmul(act, d_y2, "tn", bf16, "ffn_out_wgrad")
    dg_ff, du_ff = _ffn_out_dgrad_swiglu(d_y2, w_ffn_out, g_ff, u_ff, "ffn_out_dgrad_swiglu")
    g_w_ffn_in_t = _wgrad_stack([dg_ff, du_ff], h2, "ffn_in_wgrad")
    sent = on_grads(dict(w_ffn_in=g_w_ffn_in_t, w_ffn_out=g_w_ffn_out))
    d_h2 = _sum_matmul([(dg_ff, w_ffn_in_t, 0), (du_ff, w_ffn_in_t, 1)], "ffn_in_dgrad", after=sent)
    d_x2, vec_nf, d_y1, vec_pm = _prenorm_bwd(d_h2, x2, row(g_pre_ffn), scale_f, d_out, "prenorm_ffn_postnorm_mix_bwd",
                                              below=(y1, row(g_post_mix), gate_m))

    g_w_out = _matmul(merged, d_y1, "tn", bf16, "out_proj_wgrad")
    d_merged = _matmul(d_y1, w_out, "nt", bf16, "out_proj_dgrad")
    d_ba, d_bb, dgl = _merge_bwd(d_merged, ba, bb, gl, "merge_bwd")
    g_w_branch_a = _matmul(o_a, d_ba, "tn", bf16, "branch_a_wgrad")
    g_w_branch_b = _matmul(o_b, d_bb, "tn", bf16, "branch_b_wgrad")
    sent = on_grads(dict(w_out=g_w_out, w_branch_a=g_w_branch_a, w_branch_b=g_w_branch_b))
    d_oa = _matmul(d_ba, w_branch_a, "nt", bf16, "branch_a_dgrad", after=sent)
    d_ob = _matmul(d_bb, w_branch_b, "nt", bf16, "branch_b_dgrad", after=sent)
    delta_a, d_sink = _attn_delta(d_oa, o_a, "swa_delta", lse=lse_a, sink_rows=sink_rows)
    delta_b, = _attn_delta(d_ob, o_b, "fox_delta")
    dqa_t, dka, dva = _attn_bwd(qa, ka, va, d_oa, lse_a, delta_a, "swa_bwd", window=WINDOW, t=512)
    dqb_t, dkb, dvb, dcs, rs = _attn_bwd(qb, kb, vb, d_ob, lse_b, delta_b, "fox_bwd", cum_b=cum_b, t=512)
    dqkv = _qkv_prep_bwd(dqa_t, dka, dva, dqb_t, dkb, dvb, cos, sin_s, "qkv_prep_bwd")
    dfl, vec_bf = _forget_prep_bwd(rs.reshape(N_HEADS, s), dcs, fl, bf_row, "forget_prep_bwd")
    g_w_in_t = jnp.concatenate([_matmul(dqkv, h1, "tn", bf16, "qkv_wgrad"), _matmul(dfl, h1, "tn", bf16, "forget_wgrad")[:N_HEADS],
                                _matmul(dgl, h1, "tn", bf16, "gate_wgrad")], axis=0)
    sent = on_grads(dict(w_in=g_w_in_t))
    d_h1 = _sum_matmul([(dgl, w_gate_t, 0), (dqkv, w_qkv_t, 0), (dfl, w_f_t, 0)], "in_proj_dgrad", after=sent)
    grad_x, vec_nm = _prenorm_bwd(d_h1, x, row(g_pre_mix), scale_m, d_x2, "prenorm_mix_bwd")

    d_ada = jnp.concatenate([vec_nm[0], vec_nm[1], vec_pm[0], vec_nf[0], vec_nf[1], vec_pf[0]])
    small = dict(b_ada=d_ada, g_pre_mix=vec_nm[2], g_post_mix=vec_pm[1], g_pre_ffn=vec_nf[2], g_post_ffn=vec_pf[1],
                 b_f=vec_bf[0, :N_HEADS], sinks=d_sink[:, 0], loss=loss_row[0, :1])
    return grad_x, small


_SMALL = (("b_ada", 6144), ("g_pre_mix", 1024), ("g_post_mix", 1024), ("g_pre_ffn", 1024), ("g_post_ffn", 1024),
          ("b_f", 128), ("sinks", 128), ("loss", 128))
_SMALL_ROWS = 88


def _pack_small(vals):
    parts = [jnp.pad(vals[k].reshape(-1).astype(f32), (0, n - vals[k].size)) for k, n in _SMALL]
    flat = jnp.concatenate(parts)
    return jnp.pad(flat, (0, _SMALL_ROWS * LANES - flat.size)).reshape(_SMALL_ROWS, LANES)


def _unpack_small(slab, shapes):
    flat, out, off = slab.reshape(-1), {}, 0
    for k, n in _SMALL:
        size = math.prod(shapes[k])
        out[k] = flat[off:off + size].reshape(shapes[k])
        off += n
    return out


def kernel(x, c, positions, w_ada, b_ada, g_pre_mix, g_post_mix, w_in, b_f, sinks, w_branch_a, w_branch_b, w_out, g_pre_ffn, g_post_ffn, w_ffn_in, w_ffn_out, loss_target, m_w_ada, m_b_ada, m_g_pre_mix, m_g_post_mix, m_w_in, m_b_f, m_sinks, m_w_branch_a, m_w_branch_b, m_w_out, m_g_pre_ffn, m_g_post_ffn, m_w_ffn_in, m_w_ffn_out, v_w_ada, v_b_ada, v_g_pre_mix, v_g_post_mix, v_w_in, v_b_f, v_sinks, v_w_branch_a, v_w_branch_b, v_w_out, v_g_pre_ffn, v_g_post_ffn, v_w_ffn_in, v_w_ffn_out):
    xi, yi, ci = _me()
    me = 4 * xi + 2 * yi + ci
    d = D_MODEL
    ada_w = w_ada.shape[2]

    c_all, = _all_gather([c], "gather_c", vmem=True)
    c_all = c_all.reshape(N_DEV, d)
    b_mine = lax.dynamic_slice(b_ada, (0, me * ada_w), (1, ada_w))
    ada_cols = _ada_fwd(c_all, w_ada[0], b_mine, "ada_fwd")
    ada_all, = _all_gather([ada_cols], "gather_ada", vmem=True)
    ada = lax.dynamic_index_in_dim(ada_all, me, axis=1, keepdims=False).reshape(6, d)

    transposed = ("w_in", "w_ffn_in")
    tr = lambda a: jnp.transpose(a[0])

    g_in, = _all_gather([tr(w_in).astype(bf16)], "gather_w_in")
    late = [w.astype(bf16) for w in (w_branch_a[0], w_branch_b[0], w_out[0], tr(w_ffn_in), w_ffn_out[0])]
    late_h = _exchange_start(late, False, "gather_late_start", after=g_in)

    def mine_into(zone, block):
        return lax.dynamic_update_index_in_dim(zone, block, me, 0)

    def rows_from_shards(g):
        return g.reshape(g.shape[0] * g.shape[1], g.shape[2])

    def late_weights(after):
        zones = _exchange_wait(late_h, after, "gather_late_wait")
        g_ba, g_bb, g_out, g_fi, g_fo = (mine_into(z, w) for z, w in zip(zones, late))
        return (_cols_from_shards(g_ba), _cols_from_shards(g_bb), rows_from_shards(g_out), rows_from_shards(g_fi),
                rows_from_shards(g_fo))

    row_sharded = ("w_out", "w_ffn_out") + transposed
    in_flight = []

    def on_grads(group):
        sends = [g.reshape(N_DEV, g.shape[0] // N_DEV, g.shape[1]) if nm in row_sharded else _shards_from_cols(g)
                 for nm, g in group.items()]
        handle = _exchange_start(sends, True, "scatter_start_%d" % len(in_flight))
        in_flight.append((list(group), sends, handle))
        return handle["token"]

    grad_x, small = _local_step(
        x[0], positions[0], ada + late_h["token"][0, 0], g_pre_mix[0], g_post_mix[0], b_f[0], sinks[0], g_pre_ffn[0],
        g_post_ffn[0], loss_target[0], rows_from_shards(g_in), late_weights, on_grads)

    ws = dict(w_in=(w_in, m_w_in, v_w_in), w_branch_a=(w_branch_a, m_w_branch_a, v_w_branch_a),
              w_branch_b=(w_branch_b, m_w_branch_b, v_w_branch_b), w_out=(w_out, m_w_out, v_w_out),
              w_ffn_in=(w_ffn_in, m_w_ffn_in, v_w_ffn_in), w_ffn_out=(w_ffn_out, m_w_ffn_out, v_w_ffn_out))
    res = {}

    def finish_group(gi, after):
        names, sends, handle = in_flight[gi]
        zones = _exchange_wait(handle, after, "scatter_wait_%d" % gi)
        for nm, zone, sent in zip(names, zones, sends):
            w, m, v = (tr(a) if nm in transposed else a[0] for a in ws[nm])
            out = _adamw(zone, w, m, v, "adamw_" + nm, mine=sent)
            after = out[0]
            res[nm] = [jnp.transpose(o) for o in out] if nm in transposed else out
        return after

    done = finish_group(1, finish_group(0, grad_x))

    slab_all, = _all_gather([_pack_small(small)], "gather_small", vmem=True, after=done)
    small_w = dict(b_ada=b_ada, g_pre_mix=g_pre_mix, g_post_mix=g_post_mix, g_pre_ffn=g_pre_ffn, g_post_ffn=g_post_ffn,
                   b_f=b_f, sinks=sinks, loss=jnp.zeros((1,), f32))
    small_m = dict(b_ada=m_b_ada, g_pre_mix=m_g_pre_mix, g_post_mix=m_g_post_mix, g_pre_ffn=m_g_pre_ffn,
                   g_post_ffn=m_g_post_ffn, b_f=m_b_f, sinks=m_sinks, loss=jnp.zeros((1,), f32))
    small_v = dict(b_ada=v_b_ada, g_pre_mix=v_g_pre_mix, g_post_mix=v_g_post_mix, g_pre_ffn=v_g_pre_ffn,
                   g_post_ffn=v_g_post_ffn, b_f=v_b_f, sinks=v_sinks, loss=jnp.ones((1,), f32))
    shapes = {k: small_w[k].shape for k, _ in _SMALL}
    s_out = _adamw(slab_all, _pack_small(small_w), _pack_small(small_m), _pack_small(small_v), "adamw_small")
    s_grad, s_delta, s_m, s_v = (_unpack_small(o, shapes) for o in s_out)

    d_ada_all = lax.dynamic_slice(slab_all[:, :6144 // LANES, :].reshape(N_DEV, 6144), (0, me * ada_w), (N_DEV, ada_w))
    ada_parts = _ada_wgrad(c_all, d_ada_all, "ada_wgrad")

    res["w_ada"] = _adamw(ada_parts, w_ada[0], m_w_ada[0], v_w_ada[0], "adamw_w_ada")
    finish_group(2, res["w_ada"][0])

    order = ["w_ada", "b_ada", "g_pre_mix", "g_post_mix", "w_in", "b_f", "sinks", "w_branch_a", "w_branch_b", "w_out",
             "g_pre_ffn", "g_post_ffn", "w_ffn_in", "w_ffn_out"]
    outs = [s_grad["loss"].reshape(()), grad_x[None]]
    for which, small_o in enumerate((s_grad, s_delta, s_m, s_v)):
        for nm in order:
            outs.append(res[nm][which][None] if nm in res else small_o[nm])
    return tuple(outs)
```

```python
import functools
import math

import jax
import jax.numpy as jnp
from jax import lax
from jax.experimental import pallas as pl
from jax.experimental.pallas import tpu as pltpu

f32 = jnp.float32
bf16 = jnp.bfloat16

D_MODEL = 1024
HEAD_DIM = 64
N_HEADS = 8
N_PAIRS = 4
QKV_W = 2304
GATE_W = 2048
F_OFF = 2304
IN_W = 4360
WINDOW = 128
ROPE_THETA = 10000.0
RMS_EPS = 1e-6
D_FF = 2816
N_DEV = 8
ADAM_LR, ADAM_B1, ADAM_B2, ADAM_EPS, ADAM_WD, ADAM_STEP = 0.001, 0.9, 0.999, 1e-08, 0.01, 10
NEG = -1e30
LANES = 128
VMEM_LIMIT = 48 * 1024 * 1024
MESH = pl.DeviceIdType.MESH

_NT = (((1,), (1,)), ((), ()))
_TN = (((0,), (0,)), ((), ()))


def _params(n_grid=0):
    sem = ("arbitrary",) * n_grid if n_grid else None
    return pltpu.CompilerParams(dimension_semantics=sem, vmem_limit_bytes=VMEM_LIMIT)


def _row_tile(s, want):
    t = min(s, want)
    assert s % t == 0, (s, t)
    return t


MATMUL_VMEM_BUDGET = 40 * 1024 * 1024


def _matmul_tiles(m, n, k, a_item, b_item, o_item):
    def tiles(d):
        return [t for t in range(LANES, min(d, 2048) + 1, LANES) if d % t == 0] or [d]

    best = None
    for tm in tiles(m):
        for tn in tiles(n):
            vmem = 2 * (tm * k * a_item + tn * k * b_item + tm * tn * o_item) + tm * tn * 4
            if vmem > MATMUL_VMEM_BUDGET:
                continue
            traffic = m * k * a_item + n * k * b_item * (1 if tn == n else m // tm) + m * n * o_item
            steps = (m // tm) * (n // tn)
            key = (traffic, 0, steps) if steps >= 4 else (traffic, 1, -steps)
            if best is None or key < best[0]:
                best = (key, tm, tn)
    assert best is not None, (m, n, k)
    return best[1], best[2]


def _matmul(a, b, mode, out_dtype, name, after=None):
    if mode == "nn":
        (m, k), n = a.shape, b.shape[1]
    elif mode == "nt":
        (m, k), n = a.shape, b.shape[0]
    else:
        (k, m), n = a.shape, b.shape[1]
    tm, tn = _matmul_tiles(m, n, k, a.dtype.itemsize, b.dtype.itemsize, jnp.dtype(out_dtype).itemsize)
    if mode == "nn":
        a_spec, b_spec, dims = pl.BlockSpec((tm, k), lambda i, j: (i, 0)), pl.BlockSpec((k, tn), lambda i, j: (0, j)), None
    elif mode == "nt":
        a_spec, b_spec, dims = pl.BlockSpec((tm, k), lambda i, j: (i, 0)), pl.BlockSpec((tn, k), lambda i, j: (j, 0)), _NT
    else:
        a_spec, b_spec, dims = pl.BlockSpec((k, tm), lambda i, j: (0, i)), pl.BlockSpec((k, tn), lambda i, j: (0, j)), _TN

    def body(a_ref, b_ref, *rest):
        o_ref = rest[-1]
        av, bv = a_ref[...].astype(bf16), b_ref[...].astype(bf16)
        if dims is None:
            r = jnp.dot(av, bv, preferred_element_type=f32)
        else:
            r = lax.dot_general(av, bv, dims, preferred_element_type=f32)
        o_ref[...] = r.astype(out_dtype)

    extra = [] if after is None else [after]
    return pl.pallas_call(
        body, name=name, grid=(m // tm, n // tn), in_specs=[a_spec, b_spec] + [pl.BlockSpec(memory_space=pl.ANY)] * len(extra),
        out_specs=pl.BlockSpec((tm, tn), lambda i, j: (i, j)),
        out_shape=jax.ShapeDtypeStruct((m, n), out_dtype), compiler_params=_params(2),
    )(a, b, *extra)


def _rstd(v):
    return lax.rsqrt(jnp.mean(v * v, axis=-1, keepdims=True) + RMS_EPS)


def _row_spec(tm, d):
    return pl.BlockSpec((tm, d), lambda i: (i, 0))


def _vec_spec(d, rows=1):
    return pl.BlockSpec((rows, d), lambda i: (0, 0))


def _prenorm(x, g, scale, shift, name):
    s, d = x.shape
    tm = _row_tile(s, 512)

    def body(x_ref, g_ref, sc_ref, sh_ref, h_ref):
        xv = x_ref[...]
        h = (xv * _rstd(xv) * g_ref[...]) * (1.0 + sc_ref[...]) + sh_ref[...]
        h_ref[...] = h.astype(bf16)

    return pl.pallas_call(
        body, name=name, grid=(s // tm,), in_specs=[_row_spec(tm, d)] + [_vec_spec(d)] * 3,
        out_specs=_row_spec(tm, d), out_shape=jax.ShapeDtypeStruct((s, d), bf16), compiler_params=_params(1),
    )(x, g, scale, shift)


def _postnorm_prenorm(x, y, g_post, gate, g_pre, scale, shift, name):
    s, d = x.shape
    tm = _row_tile(s, 512)

    def body(x_ref, y_ref, gp_ref, gate_ref, g_ref, sc_ref, sh_ref, x2_ref, h_ref):
        yv = y_ref[...]
        x2 = x_ref[...] + gate_ref[...] * (yv * _rstd(yv) * gp_ref[...])
        x2_ref[...] = x2
        h_ref[...] = ((x2 * _rstd(x2) * g_ref[...]) * (1.0 + sc_ref[...]) + sh_ref[...]).astype(bf16)

    return pl.pallas_call(
        body, name=name, grid=(s // tm,), in_specs=[_row_spec(tm, d)] * 2 + [_vec_spec(d)] * 5,
        out_specs=[_row_spec(tm, d)] * 2,
        out_shape=[jax.ShapeDtypeStruct((s, d), f32), jax.ShapeDtypeStruct((s, d), bf16)], compiler_params=_params(1),
    )(x, y, g_post, gate, g_pre, scale, shift)


def _rms_bwd(u, v, r):
    return r * u - v * (r * r * r) * jnp.mean(u * v, axis=-1, keepdims=True)


def _loss_tail(x, y, g, gate, target, name):
    s, d = x.shape
    tm = _row_tile(s, 512)

    def body(x_ref, y_ref, g_ref, gate_ref, t_ref, loss_ref, do_ref, dy_ref, vec_ref):
        @pl.when(pl.program_id(0) == 0)
        def _():
            loss_ref[...] = jnp.zeros_like(loss_ref)
            vec_ref[...] = jnp.zeros_like(vec_ref)
        yv = y_ref[...]
        r = _rstd(yv)
        yn = yv * r
        err = x_ref[...] + gate_ref[...] * (yn * g_ref[...]) - t_ref[...]
        loss_ref[...] += 0.5 * jnp.sum(jnp.mean(err * err, axis=-1, keepdims=True), axis=0, keepdims=True)
        dr = err / d
        do_ref[...] = dr
        dn = dr * gate_ref[...]
        vec_ref[0:1, :] += jnp.sum(dr * (yn * g_ref[...]), axis=0, keepdims=True)
        vec_ref[1:2, :] += jnp.sum(dn * yn, axis=0, keepdims=True)
        dy_ref[...] = _rms_bwd(dn * g_ref[...], yv, r).astype(bf16)

    return pl.pallas_call(
        body, name=name, grid=(s // tm,), in_specs=[_row_spec(tm, d)] * 2 + [_vec_spec(d)] * 2 + [_row_spec(tm, d)],
        out_specs=[_vec_spec(LANES), _row_spec(tm, d), _row_spec(tm, d), _vec_spec(d, 8)],
        out_shape=[jax.ShapeDtypeStruct((1, LANES), f32), jax.ShapeDtypeStruct((s, d), f32),
                   jax.ShapeDtypeStruct((s, d), bf16), jax.ShapeDtypeStruct((8, d), f32)],
        compiler_params=_params(1),
    )(x, y, g, gate, target)


def _prenorm_bwd(dh, x, g, scale, dres, name, below=None):
    s, d = x.shape
    tm = _row_tile(s, 512)

    def body(dh_ref, x_ref, g_ref, sc_ref, dr_ref, *rest):
        dx_ref, vec_ref = rest[-4:-2] if below else rest[-2:]

        @pl.when(pl.program_id(0) == 0)
        def _():
            vec_ref[...] = jnp.zeros_like(vec_ref)
            if below:
                rest[-1][...] = jnp.zeros_like(rest[-1])
        dhv, xv = dh_ref[...], x_ref[...]
        r = _rstd(xv)
        xn = xv * r
        dn = dhv * (1.0 + sc_ref[...])
        vec_ref[0:1, :] += jnp.sum(dhv, axis=0, keepdims=True)
        vec_ref[1:2, :] += jnp.sum(dhv * (xn * g_ref[...]), axis=0, keepdims=True)
        vec_ref[2:3, :] += jnp.sum(dn * xn, axis=0, keepdims=True)
        dx = dr_ref[...] + _rms_bwd(dn * g_ref[...], xv, r)
        dx_ref[...] = dx
        if below:
            y_ref, gp_ref, gate_ref, _, _, dy_ref, vec2_ref = rest
            yv = y_ref[...]
            ry = _rstd(yv)
            yn = yv * ry
            dny = dx * gate_ref[...]
            vec2_ref[0:1, :] += jnp.sum(dx * (yn * gp_ref[...]), axis=0, keepdims=True)
            vec2_ref[1:2, :] += jnp.sum(dny * yn, axis=0, keepdims=True)
            dy_ref[...] = _rms_bwd(dny * gp_ref[...], yv, ry).astype(bf16)

    in_specs = [_row_spec(tm, d)] * 2 + [_vec_spec(d)] * 2 + [_row_spec(tm, d)]
    out_specs = [_row_spec(tm, d), _vec_spec(d, 8)]
    out_shape = [jax.ShapeDtypeStruct((s, d), f32), jax.ShapeDtypeStruct((8, d), f32)]
    args = [dh, x, g, scale, dres]
    if below:
        in_specs += [_row_spec(tm, d)] + [_vec_spec(d)] * 2
        out_specs += [_row_spec(tm, d), _vec_spec(d, 8)]
        out_shape += [jax.ShapeDtypeStruct((s, d), bf16), jax.ShapeDtypeStruct((8, d), f32)]
        args += list(below)
    return pl.pallas_call(
        body, name=name, grid=(s // tm,), in_specs=in_specs, out_specs=out_specs, out_shape=out_shape,
        compiler_params=_params(1),
    )(*args)


def _lane():
    return lax.broadcasted_iota(jnp.int32, (1, LANES), 1)


def _rope_tables(pos_col, inv_freq, name):
    s = pos_col.shape[0]

    def body(p_ref, f_ref, cos_ref, sin_ref):
        ang = p_ref[...].astype(f32) * f_ref[...]
        first_half = (_lane() % HEAD_DIM) < HEAD_DIM // 2
        cos_ref[...] = jnp.cos(ang)
        sn = jnp.sin(ang)
        sin_ref[...] = jnp.where(first_half, -sn, sn)

    return pl.pallas_call(
        body, name=name, out_shape=[jax.ShapeDtypeStruct((s, LANES), f32)] * 2, compiler_params=_params(),
    )(pos_col, inv_freq)


def _swap_halves(v):
    first_half = (_lane() % HEAD_DIM) < HEAD_DIM // 2
    return jnp.where(first_half, pltpu.roll(v, LANES - HEAD_DIM // 2, axis=1), pltpu.roll(v, HEAD_DIM // 2, axis=1))


def _qkv_prep(qkv, cos, sin_s, name):
    s = qkv.shape[0]
    tm = _row_tile(s, 256)
    scale = 1.0 / math.sqrt(HEAD_DIM)

    def body(p_ref, c_ref, s_ref, qa_ref, ka_ref, va_ref, qb_ref, kb_ref, vb_ref):
        cs, sn = c_ref[...], s_ref[...]
        low = _lane() < HEAD_DIM

        def blk(j):
            return p_ref[:, j * LANES:(j + 1) * LANES]

        def rope(v):
            return v * cs + _swap_halves(v) * sn

        def expand(v):
            other = pltpu.roll(v, HEAD_DIM, axis=1)
            return jnp.where(low, v, other), jnp.where(low, other, v)

        for j in range(N_PAIRS):
            qa_ref[:, j * LANES:(j + 1) * LANES] = (rope(blk(j)) * scale).astype(bf16)
            qb_ref[:, j * LANES:(j + 1) * LANES] = (blk(6 + j) * scale).astype(bf16)
            kb_ref[:, j * LANES:(j + 1) * LANES] = blk(10 + j).astype(bf16)
            vb_ref[:, j * LANES:(j + 1) * LANES] = blk(14 + j).astype(bf16)
        k0, k1 = expand(rope(blk(4)))
        v0, v1 = expand(blk(5))
        for j in range(N_PAIRS):
            ka_ref[:, j * LANES:(j + 1) * LANES] = (k0 if j < 2 else k1).astype(bf16)
            va_ref[:, j * LANES:(j + 1) * LANES] = (v0 if j < 2 else v1).astype(bf16)

    hw = N_PAIRS * LANES
    return pl.pallas_call(
        body, name=name, grid=(s // tm,),
        in_specs=[_row_spec(tm, QKV_W), _row_spec(tm, LANES), _row_spec(tm, LANES)],
        out_specs=[_row_spec(tm, hw)] * 6, out_shape=[jax.ShapeDtypeStruct((s, hw), bf16)] * 6, compiler_params=_params(1),
    )(qkv, cos, sin_s)


def _qkv_prep_bwd(dqa_t, dka, dva, dqb_t, dkb, dvb, cos, sin_s, name):
    s = dka.shape[0]
    tm = _row_tile(s, 256)
    scale = 1.0 / math.sqrt(HEAD_DIM)
    hw = N_PAIRS * LANES
    t_spec = pl.BlockSpec((hw, tm), lambda i: (0, i))

    def body(dqa_ref, dka_ref, dva_ref, dqb_ref, dkb_ref, dvb_ref, c_ref, s_ref, o_ref):
        cs, sn = c_ref[...], s_ref[...]
        low = _lane() < HEAD_DIM

        def blk(ref, j):
            return ref[:, j * LANES:(j + 1) * LANES]

        def blk_t(ref, j):
            return ref[j * LANES:(j + 1) * LANES, :].T

        def unrope(v):
            return v * cs + _swap_halves(v * sn)

        def fold(ref):
            a, b = blk(ref, 0) + blk(ref, 1), blk(ref, 2) + blk(ref, 3)
            kv0 = a + pltpu.roll(a, HEAD_DIM, axis=1)
            kv1 = b + pltpu.roll(b, HEAD_DIM, axis=1)
            return jnp.where(low, kv0, kv1)

        for j in range(N_PAIRS):
            o_ref[:, j * LANES:(j + 1) * LANES] = (unrope(blk_t(dqa_ref, j)) * scale).astype(bf16)
            o_ref[:, (6 + j) * LANES:(7 + j) * LANES] = (blk_t(dqb_ref, j) * scale).astype(bf16)
            o_ref[:, (10 + j) * LANES:(11 + j) * LANES] = blk(dkb_ref, j).astype(bf16)
            o_ref[:, (14 + j) * LANES:(15 + j) * LANES] = blk(dvb_ref, j).astype(bf16)
        o_ref[:, 4 * LANES:5 * LANES] = unrope(fold(dka_ref)).astype(bf16)
        o_ref[:, 5 * LANES:6 * LANES] = fold(dva_ref).astype(bf16)

    return pl.pallas_call(
        body, name=name, grid=(s // tm,),
        in_specs=[t_spec, _row_spec(tm, hw), _row_spec(tm, hw), t_spec, _row_spec(tm, hw), _row_spec(tm, hw)] + [_row_spec(tm, LANES)] * 2,
        out_specs=_row_spec(tm, QKV_W), out_shape=jax.ShapeDtypeStruct((s, QKV_W), bf16), compiler_params=_params(1),
    )(dqa_t, dka, dva, dqb_t, dkb, dvb, cos, sin_s)


def _cumsum_rows(v, reverse=False):
    n = v.shape[0]
    row = lax.broadcasted_iota(jnp.int32, v.shape, 0)
    sh = 1
    while sh < n:
        if reverse:
            v = v + jnp.where(row < n - sh, pltpu.roll(v, n - sh, axis=0), 0.0)
        else:
            v = v + jnp.where(row >= sh, pltpu.roll(v, sh, axis=0), 0.0)
        sh *= 2
    return v


def _log_sigmoid(z):
    return jnp.minimum(z, 0.0) - jnp.log1p(jnp.exp(-jnp.abs(z)))


def _forget_prep(fl, bf_row, name):
    s = fl.shape[0]

    def body(f_ref, b_ref, cb_ref):
        cum = _cumsum_rows(_log_sigmoid(f_ref[...] + b_ref[...]))
        for h in range(N_HEADS):
            cb_ref[:, h * LANES:(h + 1) * LANES] = jnp.broadcast_to(cum[:, h:h + 1], (s, LANES))

    return pl.pallas_call(
        body, name=name, out_shape=jax.ShapeDtypeStruct((s, N_HEADS * LANES), f32), compiler_params=_params(),
    )(fl, bf_row)


def _forget_prep_bwd(rs, dcs, fl, bf_row, name):
    s = fl.shape[0]

    def body(r_ref, c_ref, f_ref, b_ref, df_ref, db_ref):
        eye = (lax.broadcasted_iota(jnp.int32, (N_HEADS, LANES), 0) == lax.broadcasted_iota(jnp.int32, (N_HEADS, LANES), 1)).astype(f32)
        dcum = lax.dot_general(r_ref[...], eye, _TN, precision=lax.Precision.HIGHEST, preferred_element_type=f32)
        for h in range(N_HEADS):
            dcum = dcum - jnp.where(_lane() == h, jnp.sum(c_ref[:, h * LANES:(h + 1) * LANES], axis=1, keepdims=True), 0.0)
        dlf = _cumsum_rows(dcum, reverse=True)
        z = f_ref[...] + b_ref[...]
        df = jnp.where(_lane() < N_HEADS, dlf * jax.nn.sigmoid(-z), 0.0)
        df_ref[...] = df.astype(bf16)
        db_ref[...] = jnp.zeros_like(db_ref)
        db_ref[0:1, :] = jnp.sum(df, axis=0, keepdims=True)

    return pl.pallas_call(
        body, name=name,
        out_shape=[jax.ShapeDtypeStruct((s, LANES), bf16), jax.ShapeDtypeStruct((8, LANES), f32)], compiler_params=_params(),
    )(rs, dcs, fl, bf_row)


def _tile_mask(n_keys, n_queries, off, window):
    shape = (n_keys, n_queries)
    d = lax.broadcasted_iota(jnp.int32, shape, 1) - lax.broadcasted_iota(jnp.int32, shape, 0) + off
    valid = d >= 0
    return jnp.logical_and(valid, d < window) if window else valid


def _wide(v, t):
    return jnp.concatenate([v] * (t // LANES), axis=1)


def _attn_fwd(q, k, v, name, *, cum_b=None, sink_rows=None, window=None, t=256):
    s = q.shape[0]
    t = _row_tile(s, t)
    fox, has_sink = cum_b is not None, sink_rows is not None
    assert not window or (window % LANES == 0 and LANES + window <= s)

    def body(*refs):
        q_ref, k_ref, v_ref = refs[:3]
        rest = list(refs[3:])
        cb_ref = rest.pop(0) if fox else None
        sink_ref = rest.pop(0) if has_sink else None
        o_ref, lse_ref = rest
        i = pl.program_id(1)
        low = _lane() < HEAD_DIM
        top = lax.broadcasted_iota(jnp.int32, (LANES, 1), 0) < HEAD_DIM
        q2 = q_ref[...]
        zero = jnp.zeros_like(q2)
        qms = (jnp.where(low, q2, zero), jnp.where(low, zero, q2))

        def tile(k0, n_keys, off, carry, masked, queries=slice(0, t)):
            nq = queries.stop - queries.start
            kblk, vblk = k_ref[pl.ds(k0, n_keys), :], v_ref[pl.ds(k0, n_keys), :]
            valid = _tile_mask(n_keys, nq, off, window) if masked else None
            def scores(h):
                return lax.dot_general(kblk, qms[h][queries], _NT, preferred_element_type=f32)

            def softmax(h, sc):
                m, l, _ = carry[h]
                if fox:
                    sc = sc - _wide(cb_ref[pl.ds(k0, n_keys), h * LANES:(h + 1) * LANES], nq)
                if masked:
                    sc = jnp.where(valid, sc, NEG)
                m_new = jnp.maximum(m, jnp.max(sc, axis=0, keepdims=True))
                p = jnp.exp(sc - m_new)
                alpha = jnp.exp(m - m_new)
                return m_new, alpha * l + jnp.sum(p, axis=0, keepdims=True), alpha, p.astype(bf16)

            def update(h, m_new, l, alpha, p):
                return m_new, l, alpha * carry[h][2] + lax.dot_general(vblk, p, _TN, preferred_element_type=f32)

            if window:
                return tuple(update(h, *softmax(h, scores(h))) for h in range(2))
            scs = [scores(h) for h in range(2)]
            stats = [softmax(h, scs[h]) for h in range(2)]
            return tuple(update(h, *stats[h]) for h in range(2))

        def start(nq):
            if has_sink:
                return tuple((_wide(sink_ref[h:h + 1, :], nq), jnp.ones((1, nq), f32), jnp.zeros((LANES, nq), f32))
                             for h in range(2))
            return tuple((jnp.full((1, nq), NEG, f32), jnp.zeros((1, nq), f32), jnp.zeros((LANES, nq), f32)) for h in range(2))

        def finish(carry, queries):
            (m0, l0, a0), (m1, l1, a1) = carry
            o_t = jnp.where(top, a0 * (1.0 / l0), a1 * (1.0 / l1))
            o_ref[queries, :] = o_t.T.astype(bf16)
            lse_ref[0:1, queries] = m0 + jnp.log(l0)
            lse_ref[1:2, queries] = m1 + jnp.log(l1)

        if window:
            for c in range(t // LANES):
                queries = slice(c * LANES, (c + 1) * LANES)
                q0 = i * t + c * LANES
                k0 = pl.multiple_of(jnp.maximum(q0 - window, 0), LANES)
                finish(tile(k0, LANES + window, q0 - k0, start(LANES), True, queries), queries)
        else:
            carry = lax.fori_loop(0, i, lambda kb, c: tile(pl.multiple_of(kb * t, t), t, 0, c, False), start(t))
            finish(tile(pl.multiple_of(i * t, t), t, 0, carry, True), slice(0, t))

    q_spec = pl.BlockSpec((t, LANES), lambda j, i: (i, j))
    kv_spec = pl.BlockSpec((s, LANES), lambda j, i: (0, j))
    in_specs, args = [q_spec, kv_spec, kv_spec], [q, k, v]
    if fox:
        in_specs += [pl.BlockSpec((s, 2 * LANES), lambda j, i: (0, j))]
        args += [cum_b]
    if has_sink:
        in_specs += [pl.BlockSpec((None, 2, LANES), lambda j, i: (j, 0, 0))]
        args += [sink_rows.reshape(N_PAIRS, 2, LANES)]
    return pl.pallas_call(
        body, name=name, grid=(N_PAIRS, s // t), in_specs=in_specs,
        out_specs=[q_spec, pl.BlockSpec((None, 2, t), lambda j, i: (j, 0, i))],
        out_shape=[jax.ShapeDtypeStruct((s, N_PAIRS * LANES), bf16), jax.ShapeDtypeStruct((N_PAIRS, 2, s), f32)],
        compiler_params=_params(2),
    )(*args)


def _attn_delta(do, o, name, *, lse=None, sink_rows=None):
    s, hw = do.shape
    tm = _row_tile(s, 512)
    has_sink = sink_rows is not None

    def body(*refs):
        do_ref, o_ref = refs[:2]
        if has_sink:
            lse_ref, sink_ref, dl_ref, ds_ref = refs[2:]

            @pl.when(pl.program_id(0) == 0)
            def _():
                ds_ref[...] = jnp.zeros_like(ds_ref)
        else:
            dl_ref, = refs[2:]
        for j in range(N_PAIRS):
            cols = slice(j * LANES, (j + 1) * LANES)
            prod_t = (do_ref[:, cols].astype(f32) * o_ref[:, cols].astype(f32)).T
            for h in range(2):
                dl = jnp.sum(prod_t[h * HEAD_DIM:(h + 1) * HEAD_DIM, :], axis=0, keepdims=True)
                dl_ref[j, h:h + 1, :] = dl
                if has_sink:
                    r = 2 * j + h
                    p_sink = jnp.exp(sink_ref[r:r + 1, 0:1] - lse_ref[j, h:h + 1, :])
                    ds_ref[r:r + 1, :] += -jnp.sum(p_sink * dl, axis=1, keepdims=True)

    rows_spec = pl.BlockSpec((N_PAIRS, 2, tm), lambda i: (0, 0, i))
    in_specs, args = [_row_spec(tm, hw)] * 2, [do, o]
    out_specs, out_shape = [rows_spec], [jax.ShapeDtypeStruct((N_PAIRS, 2, s), f32)]
    if has_sink:
        in_specs += [rows_spec, _vec_spec(LANES, N_HEADS)]
        args += [lse, sink_rows]
        out_specs += [_vec_spec(LANES, N_HEADS)]
        out_shape += [jax.ShapeDtypeStruct((N_HEADS, LANES), f32)]
    return pl.pallas_call(
        body, name=name, grid=(s // tm,), in_specs=in_specs, out_specs=out_specs, out_shape=out_shape,
        compiler_params=_params(1),
    )(*args)


def _attn_bwd(q, k, v, do, lse, delta, name, *, cum_b=None, window=None, t=256):
    s = q.shape[0]
    t = _row_tile(s, t)
    nblk = s // t
    fox = cum_b is not None
    assert not window or (window % LANES == 0 and LANES + window <= s)

    def body(*refs):
        k_ref, v_ref, q_ref, do_ref, lse_ref, dl_ref = refs[:6]
        rest = list(refs[6:])
        cb_ref = rest.pop(0) if fox else None
        dq_ref, dk_ref, dv_ref = rest[:3]
        dcs_ref, rs_ref = (rest[3], rest[4]) if fox else (None, None)
        b = pl.program_id(1)
        k0 = pl.multiple_of(b * t, t)

        @pl.when(b == 0)
        def _():
            dq_ref[...] = jnp.zeros_like(dq_ref)
            if fox:
                rs_ref[...] = jnp.zeros_like(rs_ref)

        dk_ref[...] = jnp.zeros_like(dk_ref)
        dv_ref[...] = jnp.zeros_like(dv_ref)
        if fox:
            dcs_ref[...] = jnp.zeros_like(dcs_ref)
        low = _lane() < HEAD_DIM
        top = lax.broadcasted_iota(jnp.int32, (LANES, 1), 0) < HEAD_DIM
        kblk, vblk = k_ref[...], v_ref[...]
        k_t = kblk.astype(f32).T.astype(bf16)
        cks = [_wide(cb_ref[pl.ds(k0, t), h * LANES:(h + 1) * LANES], t) for h in range(2)] if fox else None

        def tile(q0, n_queries, off, masked, keys=slice(0, t)):
            cols = pl.ds(q0, n_queries)
            q2, do2 = q_ref[cols, :], do_ref[cols, :]
            zero = jnp.zeros_like(q2)
            valid = _tile_mask(keys.stop - keys.start, n_queries, off, window) if masked else None
            dq_parts = []
            for h in range(2):
                qm = jnp.where(low, q2, zero) if h == 0 else jnp.where(low, zero, q2)
                dom = jnp.where(low, do2, zero) if h == 0 else jnp.where(low, zero, do2)
                sc = lax.dot_general(kblk[keys], qm, _NT, preferred_element_type=f32)
                if fox:
                    sc = sc - cks[h]
                if masked:
                    sc = jnp.where(valid, sc, NEG)
                p = jnp.exp(sc - lse_ref[h:h + 1, cols])
                dp = lax.dot_general(vblk[keys], dom, _NT, preferred_element_type=f32)
                ds = p * (dp - dl_ref[h:h + 1, cols])
                pb, dsb = p.astype(bf16), ds.astype(bf16)
                dv_ref[keys, :] += jnp.dot(pb, dom, preferred_element_type=f32)
                dk_ref[keys, :] += jnp.dot(dsb, qm, preferred_element_type=f32)
                dq_parts.append(jnp.dot(k_t[:, keys], dsb, preferred_element_type=f32))
                if fox:
                    dcs_ref[:, h * LANES:(h + 1) * LANES] += sum(ds[:, g * LANES:(g + 1) * LANES] for g in range(t // LANES))
                    rs_ref[h:h + 1, cols] += jnp.sum(ds, axis=0, keepdims=True)
            dq_ref[:, cols] += jnp.where(top, dq_parts[0], dq_parts[1])

        def later_block(qb, carry):
            tile(pl.multiple_of(qb * t, t), t, 0, False)
            return carry

        if window:
            for c in range(t // LANES):
                first = b * t + c * LANES
                q0 = pl.multiple_of(jnp.minimum(first, s - (LANES + window)), LANES)
                tile(q0, LANES + window, q0 - first, True, slice(c * LANES, (c + 1) * LANES))
        else:
            tile(k0, t, 0, True)
            lax.fori_loop(b + 1, nblk, later_block, 0)

    kv_spec = pl.BlockSpec((t, LANES), lambda j, b: (b, j))
    seq_spec = pl.BlockSpec((s, LANES), lambda j, b: (0, j))
    rows_spec = pl.BlockSpec((None, 2, s), lambda j, b: (j, 0, 0))
    hw = N_PAIRS * LANES
    in_specs, args = [kv_spec, kv_spec, seq_spec, seq_spec, rows_spec, rows_spec], [k, v, q, do, lse, delta]
    out_specs = [pl.BlockSpec((LANES, s), lambda j, b: (j, 0)), kv_spec, kv_spec]
    out_shape = [jax.ShapeDtypeStruct((hw, s), f32), jax.ShapeDtypeStruct((s, hw), f32), jax.ShapeDtypeStruct((s, hw), f32)]
    if fox:
        in_specs += [pl.BlockSpec((s, 2 * LANES), lambda j, b: (0, j))]
        args += [cum_b]
        out_specs += [pl.BlockSpec((t, 2 * LANES), lambda j, b: (b, j)), rows_spec]
        out_shape += [jax.ShapeDtypeStruct((s, N_HEADS * LANES), f32), jax.ShapeDtypeStruct((N_PAIRS, 2, s), f32)]
    return pl.pallas_call(
        body, name=name, grid=(N_PAIRS, nblk), in_specs=in_specs, out_specs=out_specs, out_shape=out_shape,
        compiler_params=_params(2),
    )(*args)


def _merge(ba, bb, gl, name):
    s, d = ba.shape
    tm = _row_tile(s, 512)

    def body(a_ref, b_ref, g_ref, o_ref):
        g0, g1 = jax.nn.sigmoid(g_ref[:, :d].astype(f32)), jax.nn.sigmoid(g_ref[:, d:].astype(f32))
        o_ref[...] = (g0 * a_ref[...].astype(f32) + g1 * b_ref[...].astype(f32)).astype(bf16)

    return pl.pallas_call(
        body, name=name, grid=(s // tm,), in_specs=[_row_spec(tm, d)] * 2 + [_row_spec(tm, 2 * d)],
        out_specs=_row_spec(tm, d), out_shape=jax.ShapeDtypeStruct((s, d), bf16), compiler_params=_params(1),
    )(ba, bb, gl)


def _merge_bwd(dm, ba, bb, gl, name):
    s, d = ba.shape
    tm = _row_tile(s, 512)

    def body(dm_ref, a_ref, b_ref, g_ref, da_ref, db_ref, dg_ref):
        dmv = dm_ref[...].astype(f32)
        g0, g1 = jax.nn.sigmoid(g_ref[:, :d].astype(f32)), jax.nn.sigmoid(g_ref[:, d:].astype(f32))
        da_ref[...] = (dmv * g0).astype(bf16)
        db_ref[...] = (dmv * g1).astype(bf16)
        dg_ref[:, :d] = (dmv * a_ref[...].astype(f32) * (g0 * (1.0 - g0))).astype(bf16)
        dg_ref[:, d:] = (dmv * b_ref[...].astype(f32) * (g1 * (1.0 - g1))).astype(bf16)

    return pl.pallas_call(
        body, name=name, grid=(s // tm,), in_specs=[_row_spec(tm, d)] * 3 + [_row_spec(tm, 2 * d)],
        out_specs=[_row_spec(tm, d)] * 2 + [_row_spec(tm, 2 * d)],
        out_shape=[jax.ShapeDtypeStruct((s, d), bf16)] * 2 + [jax.ShapeDtypeStruct((s, 2 * d), bf16)],
        compiler_params=_params(1),
    )(dm, ba, bb, gl)


GLU_TILE = 256


def _ffn_in_swiglu(h, w_t, name):
    s, d = h.shape
    f = w_t.shape[0] // 2
    tm = _row_tile(s, 2048)
    tg = GLU_TILE
    nb = f // tg

    def body(h_ref, wg_ref, wu_ref, g_ref, u_ref, act_ref):
        hv = h_ref[...]
        g = lax.dot_general(hv, wg_ref[...], _NT, preferred_element_type=f32)
        u = lax.dot_general(hv, wu_ref[...], _NT, preferred_element_type=f32)
        g_ref[...] = g.astype(bf16)
        u_ref[...] = u.astype(bf16)
        act_ref[...] = (g * jax.nn.sigmoid(g) * u).astype(bf16)

    col = pl.BlockSpec((tm, tg), lambda i, j: (i, j))
    return pl.pallas_call(
        body, name=name, grid=(s // tm, nb),
        in_specs=[pl.BlockSpec((tm, d), lambda i, j: (i, 0)), pl.BlockSpec((tg, d), lambda i, j: (j, 0)),
                  pl.BlockSpec((tg, d), lambda i, j: (j + nb, 0))],
        out_specs=[col] * 3, out_shape=[jax.ShapeDtypeStruct((s, f), bf16)] * 3, compiler_params=_params(2),
    )(h, w_t, w_t)


def _ffn_out_dgrad_swiglu(dy, w_out, g, u, name):
    s, d = dy.shape
    f = g.shape[1]
    tm = _row_tile(s, 2048)
    tg = GLU_TILE

    def body(dy_ref, w_ref, g_ref, u_ref, dg_ref, du_ref):
        dv = lax.dot_general(dy_ref[...], w_ref[...], _NT, preferred_element_type=f32)
        gv, uv = g_ref[...].astype(f32), u_ref[...].astype(f32)
        sg = jax.nn.sigmoid(gv)
        dg_ref[...] = (dv * uv * (sg * (1.0 + gv * (1.0 - sg)))).astype(bf16)
        du_ref[...] = (dv * (gv * sg)).astype(bf16)

    col = pl.BlockSpec((tm, tg), lambda i, j: (i, j))
    return pl.pallas_call(
        body, name=name, grid=(s // tm, f // tg),
        in_specs=[pl.BlockSpec((tm, d), lambda i, j: (i, 0)), pl.BlockSpec((tg, d), lambda i, j: (j, 0)), col, col],
        out_specs=[col] * 2, out_shape=[jax.ShapeDtypeStruct((s, f), bf16)] * 2, compiler_params=_params(2),
    )(dy, w_out, g, u)


def _sum_matmul(terms, name, after=None):
    s = terms[0][0].shape[0]
    d = terms[0][1].shape[1]
    k = sum(a.shape[1] for a, _, _ in terms)
    tm, tn = _matmul_tiles(s, d, k, terms[0][0].dtype.itemsize, terms[0][1].dtype.itemsize, 4)
    n = len(terms)

    def body(*refs):
        acc = jnp.dot(refs[0][...], refs[n][...], preferred_element_type=f32)
        for i in range(1, n):
            acc = acc + jnp.dot(refs[i][...], refs[n + i][...], preferred_element_type=f32)
        refs[-1][...] = acc

    extra = [] if after is None else [after]
    a_specs = [pl.BlockSpec((tm, a.shape[1]), lambda i, j: (i, 0)) for a, _, _ in terms]
    b_specs = [pl.BlockSpec((a.shape[1], tn), lambda i, j, r=r: (r, j)) for a, _, r in terms]
    return pl.pallas_call(
        body, name=name, grid=(s // tm, d // tn),
        in_specs=a_specs + b_specs + [pl.BlockSpec(memory_space=pl.ANY)] * len(extra),
        out_specs=pl.BlockSpec((tm, tn), lambda i, j: (i, j)),
        out_shape=jax.ShapeDtypeStruct((s, d), f32), compiler_params=_params(2),
    )(*[a for a, _, _ in terms], *[b for _, b, _ in terms], *extra)


def _wgrad_stack(parts, h, name):
    s, m = parts[0].shape
    d = h.shape[1]
    tm = 256
    nb = m // tm
    n = len(parts)

    def body(*refs):
        i = pl.program_id(0)
        for p in range(n):
            @pl.when(i // nb == p)
            def _(p=p):
                refs[n + 1][...] = lax.dot_general(refs[p][...], refs[n][...], _TN, preferred_element_type=f32).astype(bf16)

    a_specs = [pl.BlockSpec((s, tm), lambda i, p=p: (0, jnp.clip(i - p * nb, 0, nb - 1))) for p in range(n)]
    return pl.pallas_call(
        body, name=name, grid=(n * nb,), in_specs=a_specs + [pl.BlockSpec((s, d), lambda i: (0, 0))],
        out_specs=pl.BlockSpec((tm, d), lambda i: (i, 0)),
        out_shape=jax.ShapeDtypeStruct((n * m, d), bf16), compiler_params=_params(1),
    )(*parts, h)


def _ada_fwd(c_all, w, b, name):
    def body(c_ref, w_ref, b_ref, o_ref):
        o_ref[...] = jnp.dot(c_ref[...].astype(bf16), w_ref[...].astype(bf16), preferred_element_type=f32) + b_ref[...]

    return pl.pallas_call(
        body, name=name, out_shape=jax.ShapeDtypeStruct((c_all.shape[0], w.shape[1]), f32), compiler_params=_params(),
    )(c_all, w, b)


def _ada_wgrad(c_all, d_all, name):
    n, d = c_all.shape
    w = d_all.shape[1]

    def body(c_ref, d_ref, o_ref):
        eye = (lax.broadcasted_iota(jnp.int32, (n, n), 0) == lax.broadcasted_iota(jnp.int32, (n, n), 1)).astype(f32)
        ct = lax.dot_general(c_ref[...], eye, _TN, precision=lax.Precision.HIGHEST, preferred_element_type=f32)
        g = ct[:, 0:1] * d_ref[0:1, :]
        for bi in range(1, n):
            g = g + ct[:, bi:bi + 1] * d_ref[bi:bi + 1, :]
        o_ref[0] = g

    return pl.pallas_call(
        body, name=name, out_shape=jax.ShapeDtypeStruct((1, d, w), f32), compiler_params=_params(),
    )(c_all, d_all)


def _adamw(parts, w, m, v, name, mine=None):
    r, c = w.shape
    n_parts = parts.shape[0]
    row_tiles = [t for t in range(min(r, 256), 0, -1) if r % t == 0 and (t % 16 == 0 or t == r)]
    if row_tiles:
        tr, tc = row_tiles[0], c
    else:
        tr, tc = r, next(t for t in (256, LANES) if c % t == 0)

    def body(p_ref, *rest):
        own_ref = rest[0] if mine is not None else None
        w_ref, m_ref, v_ref, g_ref, d_ref, nm_ref, nv_ref = rest[-7:]
        if mine is not None:
            x, y, cc = _me()
            me = 4 * x + 2 * y + cc

        def part(i):
            if mine is None:
                return p_ref[i].astype(f32)
            return jnp.where(me == i, own_ref[i], p_ref[i]).astype(f32)

        g = part(0)
        for i in range(1, n_parts):
            g = g + part(i)
        mm = ADAM_B1 * m_ref[...] + (1.0 - ADAM_B1) * g
        vv = ADAM_B2 * v_ref[...] + (1.0 - ADAM_B2) * (g * g)
        m_hat = mm / (1.0 - ADAM_B1 ** ADAM_STEP)
        v_hat = vv / (1.0 - ADAM_B2 ** ADAM_STEP)
        g_ref[...] = g
        d_ref[...] = -ADAM_LR * (m_hat / (jnp.sqrt(v_hat) + ADAM_EPS) + ADAM_WD * w_ref[...])
        nm_ref[...] = mm
        nv_ref[...] = vv

    spec = pl.BlockSpec((tr, tc), lambda i, j: (i, j))
    stack = [parts] if mine is None else [parts, mine]
    return pl.pallas_call(
        body, name=name, grid=(r // tr, c // tc),
        in_specs=[pl.BlockSpec((n_parts, tr, tc), lambda i, j: (0, i, j))] * len(stack) + [spec] * 3,
        out_specs=[spec] * 4, out_shape=[jax.ShapeDtypeStruct((r, c), f32)] * 4, compiler_params=_params(2),
    )(*stack, w, m, v)


def _me():
    return lax.axis_index("x"), lax.axis_index("y"), lax.axis_index("c")


def _all_gather(arrays, name, vmem=False, after=None):
    n = len(arrays)
    space = pltpu.VMEM if vmem else pl.ANY
    extra = [] if after is None else [after]

    def body(*refs):
        ins = refs[:n]
        outs = refs[n + len(extra):2 * n + len(extra)]
        send_sems, recv_sems, local_sems = refs[2 * n + len(extra):]
        x, y, c = _me()
        me, sibling = (x, y, c), (x, y, 1 - c)
        chips = [(1 - x, y), (x, 1 - y), (1 - x, 1 - y)]

        def rows(a, dev):
            return outs[a].at[4 * dev[0] + 2 * dev[1] + dev[2]]

        def copy(a, k, block, to, src=None):
            return pltpu.make_async_remote_copy(
                src_ref=rows(a, block) if src is None else src, dst_ref=rows(a, block),
                send_sem=send_sems.at[a, k], recv_sem=recv_sems.at[a, k], device_id=to, device_id_type=MESH)

        mine = [pltpu.make_async_copy(ins[a], rows(a, me), local_sems.at[a]) for a in range(n)]
        for cp in mine:
            cp.start()
        first = []
        for a in range(n):
            first.append(copy(a, 0, me, sibling, src=ins[a]))
            first += [copy(a, 1 + j, me, (*chip, c), src=ins[a]) for j, chip in enumerate(chips)]
        for cp in first:
            cp.start()
        passed = []
        for j, chip in enumerate(chips):
            for a in range(n):
                copy(a, 1 + j, (*chip, c), me).wait_recv()
                fwd = copy(a, 4 + j, (*chip, c), sibling)
                fwd.start()
                passed.append(fwd)
        for a in range(n):
            copy(a, 0, sibling, me).wait_recv()
            for j, chip in enumerate(chips):
                copy(a, 4 + j, (*chip, 1 - c), me).wait_recv()
        for cp in first + passed:
            cp.wait_send()
        for cp in mine:
            cp.wait()

    outs = pl.pallas_call(
        body, name=name,
        in_specs=[pl.BlockSpec(memory_space=space)] * n + [pl.BlockSpec(memory_space=pl.ANY)] * len(extra),
        out_specs=[pl.BlockSpec(memory_space=space)] * n,
        out_shape=[jax.ShapeDtypeStruct((N_DEV,) + a.shape, a.dtype) for a in arrays],
        scratch_shapes=[pltpu.SemaphoreType.DMA((n, 7)), pltpu.SemaphoreType.DMA((n, 7)), pltpu.SemaphoreType.DMA((n,))],
        compiler_params=pltpu.CompilerParams(vmem_limit_bytes=VMEM_LIMIT),
    )(*arrays, *extra)
    return list(outs)


_FLIPS = ((0, 0, 1), (1, 0, 0), (0, 1, 0), (1, 1, 0), (1, 0, 1), (0, 1, 1), (1, 1, 1))
_HBM = pl.BlockSpec(memory_space=pltpu.HBM)
_SEM = pl.BlockSpec(memory_space=pltpu.SEMAPHORE)


def _exchange_copies(scatter, srcs, lands, send_sems, recv_sems):
    x, y, c = _me()
    me_row = 4 * x + 2 * y + c
    out = []
    for k, (fx, fy, fc) in enumerate(_FLIPS):
        peer = (x ^ fx, y ^ fy, c ^ fc)
        peer_row = 4 * peer[0] + 2 * peer[1] + peer[2]
        for a in range(len(srcs)):
            out.append(pltpu.make_async_remote_copy(
                src_ref=srcs[a].at[peer_row] if scatter else srcs[a], dst_ref=lands[a].at[me_row],
                send_sem=send_sems.at[7 * a + k], recv_sem=recv_sems.at[7 * a + k], device_id=peer, device_id_type=MESH))
    return out


def _exchange_start(arrays, scatter, name, after=None):
    n = len(arrays)
    lands = [lax.empty(a.shape if scatter else (N_DEV,) + a.shape, a.dtype) for a in arrays]
    extra = [] if after is None else [after]

    def body(*refs):
        srcs, zones = refs[:n], refs[n:2 * n]
        send_sems, recv_sems = refs[2 * n + len(extra)], refs[2 * n + len(extra) + 1]
        token = refs[-1]
        for cp in _exchange_copies(scatter, srcs, zones, send_sems, recv_sems):
            cp.start()
        token[...] = jnp.zeros_like(token)

    thru = [pltpu.HBM(a.shape, a.dtype) for a in list(arrays) + lands]
    outs = pl.pallas_call(
        body, name=name,
        out_shape=(pltpu.SemaphoreType.DMA((7 * n,)), pltpu.SemaphoreType.DMA((7 * n,)), *thru, jax.ShapeDtypeStruct((8, LANES), f32)),
        in_specs=[_HBM] * (2 * n) + [pl.BlockSpec(memory_space=pl.ANY)] * len(extra),
        out_specs=(_SEM, _SEM, *[_HBM] * (2 * n), pl.BlockSpec(memory_space=pltpu.VMEM)),
        input_output_aliases={i: 2 + i for i in range(2 * n)},
        compiler_params=pltpu.CompilerParams(has_side_effects=pltpu.SideEffectType.DATAFLOW_SIDE_EFFECTING),
    )(*[pltpu.with_memory_space_constraint(a, pltpu.HBM) for a in list(arrays) + lands], *extra)
    return dict(n=n, scatter=scatter, sems=outs[:2], srcs=outs[2:2 + n], lands=outs[2 + n:2 + 2 * n], token=outs[-1])


def _exchange_wait(handle, after, name):
    n, scatter = handle["n"], handle["scatter"]

    def body(*refs):
        srcs, zones = refs[:n], refs[n:2 * n]
        send_sems, recv_sems = refs[2 * n], refs[2 * n + 1]
        for cp in _exchange_copies(scatter, srcs, zones, send_sems, recv_sems):
            cp.wait_send()
            cp.wait_recv()

    thru = [pltpu.HBM(a.shape, a.dtype) for a in list(handle["srcs"]) + list(handle["lands"])]
    outs = pl.pallas_call(
        body, name=name, out_shape=tuple(thru),
        in_specs=[_HBM] * (2 * n) + [_SEM, _SEM, pl.BlockSpec(memory_space=pl.ANY)], out_specs=tuple([_HBM] * (2 * n)),
        input_output_aliases={i: i for i in range(2 * n)},
        compiler_params=pltpu.CompilerParams(has_side_effects=pltpu.SideEffectType.DATAFLOW_SIDE_EFFECTING),
    )(*handle["srcs"], *handle["lands"], *handle["sems"], after)
    return list(outs[n:])


def _cols_from_shards(g):
    return jnp.transpose(g, (1, 0, 2)).reshape(g.shape[1], -1)


def _shards_from_cols(a):
    return jnp.transpose(a.reshape(a.shape[0], N_DEV, -1), (1, 0, 2))


def _local_step(x, positions, ada, g_pre_mix, g_post_mix, b_f, sinks, g_pre_ffn, g_post_ffn, target,
                w_in_t, late_weights, on_grads):
    s, d = x.shape
    row = lambda v: v.reshape(1, -1)
    shift_m, scale_m, gate_m, shift_f, scale_f, gate_f = (ada[i:i + 1] for i in range(6))
    w_gate_t, w_qkv_t = w_in_t[F_OFF + N_HEADS:], w_in_t[:QKV_W]
    w_f_t = jnp.pad(w_in_t[F_OFF:F_OFF + N_HEADS], ((0, LANES - N_HEADS), (0, 0)))
    bf_row = jnp.pad(row(b_f), ((0, 0), (0, LANES - N_HEADS)))
    sink_rows = jnp.broadcast_to(sinks.reshape(N_HEADS, 1).astype(f32), (N_HEADS, LANES))
    inv_freq = 1.0 / (ROPE_THETA ** (jnp.arange(0, HEAD_DIM, 2, dtype=f32) / HEAD_DIM))
    cos, sin_s = _rope_tables(positions.reshape(s, 1), jnp.tile(inv_freq, 4).reshape(1, LANES), "rope_tables")

    h1 = _prenorm(x, row(g_pre_mix), scale_m, shift_m, "prenorm_mix")
    gl = _matmul(h1, w_gate_t, "nt", bf16, "proj_gate")
    qkv = _matmul(h1, w_qkv_t, "nt", f32, "proj_qkv")
    fl = _matmul(h1, w_f_t, "nt", f32, "proj_forget")
    qa, ka, va, qb, kb, vb = _qkv_prep(qkv, cos, sin_s, "qkv_prep")
    cum_b = _forget_prep(fl, bf_row, "forget_prep")
    o_a, lse_a = _attn_fwd(qa, ka, va, "swa_fwd", sink_rows=sink_rows, window=WINDOW, t=512)
    o_b, lse_b = _attn_fwd(qb, kb, vb, "fox_fwd", cum_b=cum_b, t=1024)
    w_branch_a, w_branch_b, w_out, w_ffn_in_t, w_ffn_out = late_weights(o_b)
    ba = _matmul(o_a, w_branch_a, "nn", bf16, "branch_a")
    bb = _matmul(o_b, w_branch_b, "nn", bf16, "branch_b")
    merged = _merge(ba, bb, gl, "merge")
    y1 = _matmul(merged, w_out, "nn", f32, "out_proj")

    x2, h2 = _postnorm_prenorm(x, y1, row(g_post_mix), gate_m, row(g_pre_ffn), scale_f, shift_f, "postnorm_mix_prenorm_ffn")
    g_ff, u_ff, act = _ffn_in_swiglu(h2, w_ffn_in_t, "ffn_in_swiglu")
    y2 = _matmul(act, w_ffn_out, "nn", f32, "ffn_out")
    loss_row, d_out, d_y2, vec_pf = _loss_tail(x2, y2, row(g_post_ffn), gate_f, target, "loss_tail")

    g_w_ffn_out = _matmul(act, d_y2, "tn", bf16, "ffn_out_wgrad")
    dg_ff, du_ff = _ffn_out_dgrad_swiglu(d_y2, w_ffn_out, g_ff, u_ff, "ffn_out_dgrad_swiglu")
    g_w_ffn_in_t = _wgrad_stack([dg_ff, du_ff], h2, "ffn_in_wgrad")
    sent = on_grads(dict(w_ffn_in=g_w_ffn_in_t, w_ffn_out=g_w_ffn_out))
    d_h2 = _sum_matmul([(dg_ff, w_ffn_in_t, 0), (du_ff, w_ffn_in_t, 1)], "ffn_in_dgrad", after=sent)
    d_x2, vec_nf, d_y1, vec_pm = _prenorm_bwd(d_h2, x2, row(g_pre_ffn), scale_f, d_out, "prenorm_ffn_postnorm_mix_bwd",
                                              below=(y1, row(g_post_mix), gate_m))

    g_w_out = _matmul(merged, d_y1, "tn", bf16, "out_proj_wgrad")
    d_merged = _matmul(d_y1, w_out, "nt", bf16, "out_proj_dgrad")
    d_ba, d_bb, dgl = _merge_bwd(d_merged, ba, bb, gl, "merge_bwd")
    g_w_branch_a = _matmul(o_a, d_ba, "tn", bf16, "branch_a_wgrad")
    g_w_branch_b = _matmul(o_b, d_bb, "tn", bf16, "branch_b_wgrad")
    sent = on_grads(dict(w_out=g_w_out, w_branch_a=g_w_branch_a, w_branch_b=g_w_branch_b))
    d_oa = _matmul(d_ba, w_branch_a, "nt", bf16, "branch_a_dgrad", after=sent)
    d_ob = _matmul(d_bb, w_branch_b, "nt", bf16, "branch_b_dgrad", after=sent)
    delta_a, d_sink = _attn_delta(d_oa, o_a, "swa_delta", lse=lse_a, sink_rows=sink_rows)
    delta_b, = _attn_delta(d_ob, o_b, "fox_delta")
    dqa_t, dka, dva = _attn_bwd(qa, ka, va, d_oa, lse_a, delta_a, "swa_bwd", window=WINDOW, t=512)
    dqb_t, dkb, dvb, dcs, rs = _attn_bwd(qb, kb, vb, d_ob, lse_b, delta_b, "fox_bwd", cum_b=cum_b, t=512)
    dqkv = _qkv_prep_bwd(dqa_t, dka, dva, dqb_t, dkb, dvb, cos, sin_s, "qkv_prep_bwd")
    dfl, vec_bf = _forget_prep_bwd(rs.reshape(N_HEADS, s), dcs, fl, bf_row, "forget_prep_bwd")
    g_w_in_t = jnp.concatenate([_matmul(dqkv, h1, "tn", bf16, "qkv_wgrad"), _matmul(dfl, h1, "tn", bf16, "forget_wgrad")[:N_HEADS],
                                _matmul(dgl, h1, "tn", bf16, "gate_wgrad")], axis=0)
    sent = on_grads(dict(w_in=g_w_in_t))
    d_h1 = _sum_matmul([(dgl, w_gate_t, 0), (dqkv, w_qkv_t, 0), (dfl, w_f_t, 0)], "in_proj_dgrad", after=sent)
    grad_x, vec_nm = _prenorm_bwd(d_h1, x, row(g_pre_mix), scale_m, d_x2, "prenorm_mix_bwd")

    d_ada = jnp.concatenate([vec_nm[0], vec_nm[1], vec_pm[0], vec_nf[0], vec_nf[1], vec_pf[0]])
    small = dict(b_ada=d_ada, g_pre_mix=vec_nm[2], g_post_mix=vec_pm[1], g_pre_ffn=vec_nf[2], g_post_ffn=vec_pf[1],
                 b_f=vec_bf[0, :N_HEADS], sinks=d_sink[:, 0], loss=loss_row[0, :1])
    return grad_x, small


_SMALL = (("b_ada", 6144), ("g_pre_mix", 1024), ("g_post_mix", 1024), ("g_pre_ffn", 1024), ("g_post_ffn", 1024),
          ("b_f", 128), ("sinks", 128), ("loss", 128))
_SMALL_ROWS = 88


def _pack_small(vals):
    parts = [jnp.pad(vals[k].reshape(-1).astype(f32), (0, n - vals[k].size)) for k, n in _SMALL]
    flat = jnp.concatenate(parts)
    return jnp.pad(flat, (0, _SMALL_ROWS * LANES - flat.size)).reshape(_SMALL_ROWS, LANES)


def _unpack_small(slab, shapes):
    flat, out, off = slab.reshape(-1), {}, 0
    for k, n in _SMALL:
        size = math.prod(shapes[k])
        out[k] = flat[off:off + size].reshape(shapes[k])
        off += n
    return out


def kernel(x, c, positions, w_ada, b_ada, g_pre_mix, g_post_mix, w_in, b_f, sinks, w_branch_a, w_branch_b, w_out, g_pre_ffn, g_post_ffn, w_ffn_in, w_ffn_out, loss_target, m_w_ada, m_b_ada, m_g_pre_mix, m_g_post_mix, m_w_in, m_b_f, m_sinks, m_w_branch_a, m_w_branch_b, m_w_out, m_g_pre_ffn, m_g_post_ffn, m_w_ffn_in, m_w_ffn_out, v_w_ada, v_b_ada, v_g_pre_mix, v_g_post_mix, v_w_in, v_b_f, v_sinks, v_w_branch_a, v_w_branch_b, v_w_out, v_g_pre_ffn, v_g_post_ffn, v_w_ffn_in, v_w_ffn_out):
    xi, yi, ci = _me()
    me = 4 * xi + 2 * yi + ci
    d = D_MODEL
    ada_w = w_ada.shape[2]

    c_all, = _all_gather([c], "gather_c", vmem=True)
    c_all = c_all.reshape(N_DEV, d)
    b_mine = lax.dynamic_slice(b_ada, (0, me * ada_w), (1, ada_w))
    ada_cols = _ada_fwd(c_all, w_ada[0], b_mine, "ada_fwd")

    transposed = ("w_in", "w_ffn_in")
    tr = lambda a: jnp.transpose(a[0])

    ada_all, g_in = _all_gather([ada_cols, tr(w_in).astype(bf16)], "gather_ada_w_in")
    ada = lax.dynamic_index_in_dim(ada_all, me, axis=1, keepdims=False).reshape(6, d)
    late = [w.astype(bf16) for w in (w_branch_a[0], w_branch_b[0], w_out[0], tr(w_ffn_in), w_ffn_out[0])]
    late_h = _exchange_start(late, False, "gather_late_start", after=g_in)

    def mine_into(zone, block):
        return lax.dynamic_update_index_in_dim(zone, block, me, 0)

    def rows_from_shards(g):
        return g.reshape(g.shape[0] * g.shape[1], g.shape[2])

    def late_weights(after):
        zones = _exchange_wait(late_h, after, "gather_late_wait")
        g_ba, g_bb, g_out, g_fi, g_fo = (mine_into(z, w) for z, w in zip(zones, late))
        return (_cols_from_shards(g_ba), _cols_from_shards(g_bb), rows_from_shards(g_out), rows_from_shards(g_fi),
                rows_from_shards(g_fo))

    row_sharded = ("w_out", "w_ffn_out") + transposed
    in_flight = []

    def on_grads(group):
        sends = [g.reshape(N_DEV, g.shape[0] // N_DEV, g.shape[1]) if nm in row_sharded else _shards_from_cols(g)
                 for nm, g in group.items()]
        handle = _exchange_start(sends, True, "scatter_start_%d" % len(in_flight))
        in_flight.append((list(group), sends, handle))
        return handle["token"]

    grad_x, small = _local_step(
        x[0], positions[0], ada + late_h["token"][0, 0], g_pre_mix[0], g_post_mix[0], b_f[0], sinks[0], g_pre_ffn[0],
        g_post_ffn[0], loss_target[0], rows_from_shards(g_in), late_weights, on_grads)

    ws = dict(w_in=(w_in, m_w_in, v_w_in), w_branch_a=(w_branch_a, m_w_branch_a, v_w_branch_a),
              w_branch_b=(w_branch_b, m_w_branch_b, v_w_branch_b), w_out=(w_out, m_w_out, v_w_out),
              w_ffn_in=(w_ffn_in, m_w_ffn_in, v_w_ffn_in), w_ffn_out=(w_ffn_out, m_w_ffn_out, v_w_ffn_out))
    res = {}

    def finish_group(gi, after):
        names, sends, handle = in_flight[gi]
        zones = _exchange_wait(handle, after, "scatter_wait_%d" % gi)
        for nm, zone, sent in zip(names, zones, sends):
            w, m, v = (tr(a) if nm in transposed else a[0] for a in ws[nm])
            out = _adamw(zone, w, m, v, "adamw_" + nm, mine=sent)
            after = out[0]
            res[nm] = [jnp.transpose(o) for o in out] if nm in transposed else out
        return after

    done = finish_group(1, finish_group(0, grad_x))

    slab_all, = _all_gather([_pack_small(small)], "gather_small", vmem=True, after=done)
    small_w = dict(b_ada=b_ada, g_pre_mix=g_pre_mix, g_post_mix=g_post_mix, g_pre_ffn=g_pre_ffn, g_post_ffn=g_post_ffn,
                   b_f=b_f, sinks=sinks, loss=jnp.zeros((1,), f32))
    small_m = dict(b_ada=m_b_ada, g_pre_mix=m_g_pre_mix, g_post_mix=m_g_post_mix, g_pre_ffn=m_g_pre_ffn,
                   g_post_ffn=m_g_post_ffn, b_f=m_b_f, sinks=m_sinks, loss=jnp.zeros((1,), f32))
    small_v = dict(b_ada=v_b_ada, g_pre_mix=v_g_pre_mix, g_post_mix=v_g_post_mix, g_pre_ffn=v_g_pre_ffn,
                   g_post_ffn=v_g_post_ffn, b_f=v_b_f, sinks=v_sinks, loss=jnp.ones((1,), f32))
    shapes = {k: small_w[k].shape for k, _ in _SMALL}
    s_out = _adamw(slab_all, _pack_small(small_w), _pack_small(small_m), _pack_small(small_v), "adamw_small")
    s_grad, s_delta, s_m, s_v = (_unpack_small(o, shapes) for o in s_out)

    d_ada_all = lax.dynamic_slice(slab_all[:, :6144 // LANES, :].reshape(N_DEV, 6144), (0, me * ada_w), (N_DEV, ada_w))
    ada_parts = _ada_wgrad(c_all, d_ada_all, "ada_wgrad")

    res["w_ada"] = _adamw(ada_parts, w_ada[0], m_w_ada[0], v_w_ada[0], "adamw_w_ada")
    finish_group(2, res["w_ada"][0])

    order = ["w_ada", "b_ada", "g_pre_mix", "g_post_mix", "w_in", "b_f", "sinks", "w_branch_a", "w_branch_b", "w_out",
             "g_pre_ffn", "g_post_ffn", "w_ffn_in", "w_ffn_out"]
    outs = [s_grad["loss"].reshape(()), grad_x[None]]
    for which, small_o in enumerate((s_grad, s_delta, s_m, s_v)):
        for nm in order:
            outs.append(res[nm][which][None] if nm in res else small_o[nm])
    return tuple(outs)
```

```python
import functools
import math

import jax
import jax.numpy as jnp
from jax import lax
from jax.experimental import pallas as pl
from jax.experimental.pallas import tpu as pltpu

f32 = jnp.float32
bf16 = jnp.bfloat16

D_MODEL = 1024
HEAD_DIM = 64
N_HEADS = 8
N_PAIRS = 4
QKV_W = 2304
GATE_W = 2048
F_OFF = 2304
IN_W = 4360
WINDOW = 128
ROPE_THETA = 10000.0
RMS_EPS = 1e-6
D_FF = 2816
N_DEV = 8
ADAM_LR, ADAM_B1, ADAM_B2, ADAM_EPS, ADAM_WD, ADAM_STEP = 0.001, 0.9, 0.999, 1e-08, 0.01, 10
NEG = -1e30
LANES = 128
VMEM_LIMIT = 48 * 1024 * 1024
MESH = pl.DeviceIdType.MESH

_NT = (((1,), (1,)), ((), ()))
_TN = (((0,), (0,)), ((), ()))


def _params(n_grid=0):
    sem = ("arbitrary",) * n_grid if n_grid else None
    return pltpu.CompilerParams(dimension_semantics=sem, vmem_limit_bytes=VMEM_LIMIT)


def _row_tile(s, want):
    t = min(s, want)
    assert s % t == 0, (s, t)
    return t


MATMUL_VMEM_BUDGET = 40 * 1024 * 1024


def _matmul_tiles(m, n, k, a_item, b_item, o_item):
    def tiles(d):
        return [t for t in range(LANES, min(d, 2048) + 1, LANES) if d % t == 0] or [d]

    best = None
    for tm in tiles(m):
        for tn in tiles(n):
            vmem = 2 * (tm * k * a_item + tn * k * b_item + tm * tn * o_item) + tm * tn * 4
            if vmem > MATMUL_VMEM_BUDGET:
                continue
            traffic = m * k * a_item + n * k * b_item * (1 if tn == n else m // tm) + m * n * o_item
            steps = (m // tm) * (n // tn)
            key = (traffic, 0, steps) if steps >= 4 else (traffic, 1, -steps)
            if best is None or key < best[0]:
                best = (key, tm, tn)
    assert best is not None, (m, n, k)
    return best[1], best[2]


def _matmul(a, b, mode, out_dtype, name, after=None):
    if mode == "nn":
        (m, k), n = a.shape, b.shape[1]
    elif mode == "nt":
        (m, k), n = a.shape, b.shape[0]
    else:
        (k, m), n = a.shape, b.shape[1]
    tm, tn = _matmul_tiles(m, n, k, a.dtype.itemsize, b.dtype.itemsize, jnp.dtype(out_dtype).itemsize)
    if mode == "nn":
        a_spec, b_spec, dims = pl.BlockSpec((tm, k), lambda i, j: (i, 0)), pl.BlockSpec((k, tn), lambda i, j: (0, j)), None
    elif mode == "nt":
        a_spec, b_spec, dims = pl.BlockSpec((tm, k), lambda i, j: (i, 0)), pl.BlockSpec((tn, k), lambda i, j: (j, 0)), _NT
    else:
        a_spec, b_spec, dims = pl.BlockSpec((k, tm), lambda i, j: (0, i)), pl.BlockSpec((k, tn), lambda i, j: (0, j)), _TN

    def body(a_ref, b_ref, *rest):
        o_ref = rest[-1]
        av, bv = a_ref[...].astype(bf16), b_ref[...].astype(bf16)
        if dims is None:
            r = jnp.dot(av, bv, preferred_element_type=f32)
        else:
            r = lax.dot_general(av, bv, dims, preferred_element_type=f32)
        o_ref[...] = r.astype(out_dtype)

    extra = [] if after is None else [after]
    return pl.pallas_call(
        body, name=name, grid=(m // tm, n // tn), in_specs=[a_spec, b_spec] + [pl.BlockSpec(memory_space=pl.ANY)] * len(extra),
        out_specs=pl.BlockSpec((tm, tn), lambda i, j: (i, j)),
        out_shape=jax.ShapeDtypeStruct((m, n), out_dtype), compiler_params=_params(2),
    )(a, b, *extra)


def _rstd(v):
    return lax.rsqrt(jnp.mean(v * v, axis=-1, keepdims=True) + RMS_EPS)


def _row_spec(tm, d):
    return pl.BlockSpec((tm, d), lambda i: (i, 0))


def _vec_spec(d, rows=1):
    return pl.BlockSpec((rows, d), lambda i: (0, 0))


def _prenorm(x, g, scale, shift, name):
    s, d = x.shape
    tm = _row_tile(s, 512)

    def body(x_ref, g_ref, sc_ref, sh_ref, h_ref):
        xv = x_ref[...]
        h = (xv * _rstd(xv) * g_ref[...]) * (1.0 + sc_ref[...]) + sh_ref[...]
        h_ref[...] = h.astype(bf16)

    return pl.pallas_call(
        body, name=name, grid=(s // tm,), in_specs=[_row_spec(tm, d)] + [_vec_spec(d)] * 3,
        out_specs=_row_spec(tm, d), out_shape=jax.ShapeDtypeStruct((s, d), bf16), compiler_params=_params(1),
    )(x, g, scale, shift)


def _postnorm_prenorm(x, y, g_post, gate, g_pre, scale, shift, name):
    s, d = x.shape
    tm = _row_tile(s, 512)

    def body(x_ref, y_ref, gp_ref, gate_ref, g_ref, sc_ref, sh_ref, x2_ref, h_ref):
        yv = y_ref[...]
        x2 = x_ref[...] + gate_ref[...] * (yv * _rstd(yv) * gp_ref[...])
        x2_ref[...] = x2
        h_ref[...] = ((x2 * _rstd(x2) * g_ref[...]) * (1.0 + sc_ref[...]) + sh_ref[...]).astype(bf16)

    return pl.pallas_call(
        body, name=name, grid=(s // tm,), in_specs=[_row_spec(tm, d)] * 2 + [_vec_spec(d)] * 5,
        out_specs=[_row_spec(tm, d)] * 2,
        out_shape=[jax.ShapeDtypeStruct((s, d), f32), jax.ShapeDtypeStruct((s, d), bf16)], compiler_params=_params(1),
    )(x, y, g_post, gate, g_pre, scale, shift)


def _rms_bwd(u, v, r):
    return r * u - v * (r * r * r) * jnp.mean(u * v, axis=-1, keepdims=True)


def _loss_tail(x, y, g, gate, target, name):
    s, d = x.shape
    tm = _row_tile(s, 512)

    def body(x_ref, y_ref, g_ref, gate_ref, t_ref, loss_ref, do_ref, dy_ref, vec_ref):
        @pl.when(pl.program_id(0) == 0)
        def _():
            loss_ref[...] = jnp.zeros_like(loss_ref)
            vec_ref[...] = jnp.zeros_like(vec_ref)
        yv = y_ref[...]
        r = _rstd(yv)
        yn = yv * r
        err = x_ref[...] + gate_ref[...] * (yn * g_ref[...]) - t_ref[...]
        loss_ref[...] += 0.5 * jnp.sum(jnp.mean(err * err, axis=-1, keepdims=True), axis=0, keepdims=True)
        dr = err / d
        do_ref[...] = dr
        dn = dr * gate_ref[...]
        vec_ref[0:1, :] += jnp.sum(dr * (yn * g_ref[...]), axis=0, keepdims=True)
        vec_ref[1:2, :] += jnp.sum(dn * yn, axis=0, keepdims=True)
        dy_ref[...] = _rms_bwd(dn * g_ref[...], yv, r).astype(bf16)

    return pl.pallas_call(
        body, name=name, grid=(s // tm,), in_specs=[_row_spec(tm, d)] * 2 + [_vec_spec(d)] * 2 + [_row_spec(tm, d)],
        out_specs=[_vec_spec(LANES), _row_spec(tm, d), _row_spec(tm, d), _vec_spec(d, 8)],
        out_shape=[jax.ShapeDtypeStruct((1, LANES), f32), jax.ShapeDtypeStruct((s, d), f32),
                   jax.ShapeDtypeStruct((s, d), bf16), jax.ShapeDtypeStruct((8, d), f32)],
        compiler_params=_params(1),
    )(x, y, g, gate, target)


def _prenorm_bwd(dh, x, g, scale, dres, name, below=None):
    s, d = x.shape
    tm = _row_tile(s, 512)

    def body(dh_ref, x_ref, g_ref, sc_ref, dr_ref, *rest):
        dx_ref, vec_ref = rest[-4:-2] if below else rest[-2:]

        @pl.when(pl.program_id(0) == 0)
        def _():
            vec_ref[...] = jnp.zeros_like(vec_ref)
            if below:
                rest[-1][...] = jnp.zeros_like(rest[-1])
        dhv, xv = dh_ref[...], x_ref[...]
        r = _rstd(xv)
        xn = xv * r
        dn = dhv * (1.0 + sc_ref[...])
        vec_ref[0:1, :] += jnp.sum(dhv, axis=0, keepdims=True)
        vec_ref[1:2, :] += jnp.sum(dhv * (xn * g_ref[...]), axis=0, keepdims=True)
        vec_ref[2:3, :] += jnp.sum(dn * xn, axis=0, keepdims=True)
        dx = dr_ref[...] + _rms_bwd(dn * g_ref[...], xv, r)
        dx_ref[...] = dx
        if below:
            y_ref, gp_ref, gate_ref, _, _, dy_ref, vec2_ref = rest
            yv = y_ref[...]
            ry = _rstd(yv)
            yn = yv * ry
            dny = dx * gate_ref[...]
            vec2_ref[0:1, :] += jnp.sum(dx * (yn * gp_ref[...]), axis=0, keepdims=True)
            vec2_ref[1:2, :] += jnp.sum(dny * yn, axis=0, keepdims=True)
            dy_ref[...] = _rms_bwd(dny * gp_ref[...], yv, ry).astype(bf16)

    in_specs = [_row_spec(tm, d)] * 2 + [_vec_spec(d)] * 2 + [_row_spec(tm, d)]
    out_specs = [_row_spec(tm, d), _vec_spec(d, 8)]
    out_shape = [jax.ShapeDtypeStruct((s, d), f32), jax.ShapeDtypeStruct((8, d), f32)]
    args = [dh, x, g, scale, dres]
    if below:
        in_specs += [_row_spec(tm, d)] + [_vec_spec(d)] * 2
        out_specs += [_row_spec(tm, d), _vec_spec(d, 8)]
        out_shape += [jax.ShapeDtypeStruct((s, d), bf16), jax.ShapeDtypeStruct((8, d), f32)]
        args += list(below)
    return pl.pallas_call(
        body, name=name, grid=(s // tm,), in_specs=in_specs, out_specs=out_specs, out_shape=out_shape,
        compiler_params=_params(1),
    )(*args)


def _lane():
    return lax.broadcasted_iota(jnp.int32, (1, LANES), 1)


def _rope_tables(pos_col, inv_freq, name):
    s = pos_col.shape[0]

    def body(p_ref, f_ref, cos_ref, sin_ref):
        ang = p_ref[...].astype(f32) * f_ref[...]
        first_half = (_lane() % HEAD_DIM) < HEAD_DIM // 2
        cos_ref[...] = jnp.cos(ang)
        sn = jnp.sin(ang)
        sin_ref[...] = jnp.where(first_half, -sn, sn)

    return pl.pallas_call(
        body, name=name, out_shape=[jax.ShapeDtypeStruct((s, LANES), f32)] * 2, compiler_params=_params(),
    )(pos_col, inv_freq)


def _swap_halves(v):
    first_half = (_lane() % HEAD_DIM) < HEAD_DIM // 2
    return jnp.where(first_half, pltpu.roll(v, LANES - HEAD_DIM // 2, axis=1), pltpu.roll(v, HEAD_DIM // 2, axis=1))


def _proj_qkv(h, w_qkv_t, cos, sin_s, name):
    s, d = h.shape
    tm = _row_tile(s, 512)
    scale = 1.0 / math.sqrt(HEAD_DIM)

    def body(h_ref, w_ref, c_ref, s_ref, qa_ref, ka_ref, va_ref, qb_ref, kb_ref, vb_ref):
        proj = lax.dot_general(h_ref[...], w_ref[...], _NT, preferred_element_type=f32)
        cs, sn = c_ref[...], s_ref[...]
        low = _lane() < HEAD_DIM

        def blk(j):
            return proj[:, j * LANES:(j + 1) * LANES]

        def rope(v):
            return v * cs + _swap_halves(v) * sn

        def expand(v):
            other = pltpu.roll(v, HEAD_DIM, axis=1)
            return jnp.where(low, v, other), jnp.where(low, other, v)

        for j in range(N_PAIRS):
            qa_ref[:, j * LANES:(j + 1) * LANES] = (rope(blk(j)) * scale).astype(bf16)
            qb_ref[:, j * LANES:(j + 1) * LANES] = (blk(6 + j) * scale).astype(bf16)
            kb_ref[:, j * LANES:(j + 1) * LANES] = blk(10 + j).astype(bf16)
            vb_ref[:, j * LANES:(j + 1) * LANES] = blk(14 + j).astype(bf16)
        k0, k1 = expand(rope(blk(4)))
        v0, v1 = expand(blk(5))
        for j in range(N_PAIRS):
            ka_ref[:, j * LANES:(j + 1) * LANES] = (k0 if j < 2 else k1).astype(bf16)
            va_ref[:, j * LANES:(j + 1) * LANES] = (v0 if j < 2 else v1).astype(bf16)

    hw = N_PAIRS * LANES
    return pl.pallas_call(
        body, name=name, grid=(s // tm,),
        in_specs=[_row_spec(tm, d), pl.BlockSpec((QKV_W, d), lambda i: (0, 0)), _row_spec(tm, LANES), _row_spec(tm, LANES)],
        out_specs=[_row_spec(tm, hw)] * 6, out_shape=[jax.ShapeDtypeStruct((s, hw), bf16)] * 6, compiler_params=_params(1),
    )(h, w_qkv_t, cos, sin_s)


def _qkv_prep_bwd(dqa_t, dka, dva, dqb_t, dkb, dvb, cos, sin_s, name):
    s = dka.shape[0]
    tm = _row_tile(s, 256)
    scale = 1.0 / math.sqrt(HEAD_DIM)
    hw = N_PAIRS * LANES
    t_spec = pl.BlockSpec((hw, tm), lambda i: (0, i))

    def body(dqa_ref, dka_ref, dva_ref, dqb_ref, dkb_ref, dvb_ref, c_ref, s_ref, o_ref):
        cs, sn = c_ref[...], s_ref[...]
        low = _lane() < HEAD_DIM

        def blk(ref, j):
            return ref[:, j * LANES:(j + 1) * LANES]

        def blk_t(ref, j):
            return ref[j * LANES:(j + 1) * LANES, :].T

        def unrope(v):
            return v * cs + _swap_halves(v * sn)

        def fold(ref):
            a, b = blk(ref, 0) + blk(ref, 1), blk(ref, 2) + blk(ref, 3)
            kv0 = a + pltpu.roll(a, HEAD_DIM, axis=1)
            kv1 = b + pltpu.roll(b, HEAD_DIM, axis=1)
            return jnp.where(low, kv0, kv1)

        for j in range(N_PAIRS):
            o_ref[:, j * LANES:(j + 1) * LANES] = (unrope(blk_t(dqa_ref, j)) * scale).astype(bf16)
            o_ref[:, (6 + j) * LANES:(7 + j) * LANES] = (blk_t(dqb_ref, j) * scale).astype(bf16)
            o_ref[:, (10 + j) * LANES:(11 + j) * LANES] = blk(dkb_ref, j).astype(bf16)
            o_ref[:, (14 + j) * LANES:(15 + j) * LANES] = blk(dvb_ref, j).astype(bf16)
        o_ref[:, 4 * LANES:5 * LANES] = unrope(fold(dka_ref)).astype(bf16)
        o_ref[:, 5 * LANES:6 * LANES] = fold(dva_ref).astype(bf16)

    return pl.pallas_call(
        body, name=name, grid=(s // tm,),
        in_specs=[t_spec, _row_spec(tm, hw), _row_spec(tm, hw), t_spec, _row_spec(tm, hw), _row_spec(tm, hw)] + [_row_spec(tm, LANES)] * 2,
        out_specs=_row_spec(tm, QKV_W), out_shape=jax.ShapeDtypeStruct((s, QKV_W), bf16), compiler_params=_params(1),
    )(dqa_t, dka, dva, dqb_t, dkb, dvb, cos, sin_s)


def _cumsum_rows(v, reverse=False):
    n = v.shape[0]
    row = lax.broadcasted_iota(jnp.int32, v.shape, 0)
    sh = 1
    while sh < n:
        if reverse:
            v = v + jnp.where(row < n - sh, pltpu.roll(v, n - sh, axis=0), 0.0)
        else:
            v = v + jnp.where(row >= sh, pltpu.roll(v, sh, axis=0), 0.0)
        sh *= 2
    return v


def _log_sigmoid(z):
    return jnp.minimum(z, 0.0) - jnp.log1p(jnp.exp(-jnp.abs(z)))


def _forget_prep(fl, bf_row, name):
    s = fl.shape[0]

    def body(f_ref, b_ref, cb_ref):
        cum = _cumsum_rows(_log_sigmoid(f_ref[...] + b_ref[...]))
        for h in range(N_HEADS):
            cb_ref[:, h * LANES:(h + 1) * LANES] = jnp.broadcast_to(cum[:, h:h + 1], (s, LANES))

    return pl.pallas_call(
        body, name=name, out_shape=jax.ShapeDtypeStruct((s, N_HEADS * LANES), f32), compiler_params=_params(),
    )(fl, bf_row)


def _forget_prep_bwd(rs, dcs, fl, bf_row, name):
    s = fl.shape[0]

    def body(r_ref, c_ref, f_ref, b_ref, df_ref, db_ref):
        eye = (lax.broadcasted_iota(jnp.int32, (N_HEADS, LANES), 0) == lax.broadcasted_iota(jnp.int32, (N_HEADS, LANES), 1)).astype(f32)
        dcum = lax.dot_general(r_ref[...], eye, _TN, precision=lax.Precision.HIGHEST, preferred_element_type=f32)
        for h in range(N_HEADS):
            dcum = dcum - jnp.where(_lane() == h, jnp.sum(c_ref[:, h * LANES:(h + 1) * LANES], axis=1, keepdims=True), 0.0)
        dlf = _cumsum_rows(dcum, reverse=True)
        z = f_ref[...] + b_ref[...]
        df = jnp.where(_lane() < N_HEADS, dlf * jax.nn.sigmoid(-z), 0.0)
        df_ref[...] = df.astype(bf16)
        db_ref[...] = jnp.zeros_like(db_ref)
        db_ref[0:1, :] = jnp.sum(df, axis=0, keepdims=True)

    return pl.pallas_call(
        body, name=name,
        out_shape=[jax.ShapeDtypeStruct((s, LANES), bf16), jax.ShapeDtypeStruct((8, LANES), f32)], compiler_params=_params(),
    )(rs, dcs, fl, bf_row)


def _tile_mask(n_keys, n_queries, off, window):
    shape = (n_keys, n_queries)
    d = lax.broadcasted_iota(jnp.int32, shape, 1) - lax.broadcasted_iota(jnp.int32, shape, 0) + off
    valid = d >= 0
    return jnp.logical_and(valid, d < window) if window else valid


def _wide(v, t):
    return jnp.concatenate([v] * (t // LANES), axis=1)


def _attn_fwd(q, k, v, name, *, cum_b=None, sink_rows=None, window=None, t=256):
    s = q.shape[0]
    t = _row_tile(s, t)
    fox, has_sink = cum_b is not None, sink_rows is not None
    assert not window or (window % LANES == 0 and LANES + window <= s)

    def body(*refs):
        q_ref, k_ref, v_ref = refs[:3]
        rest = list(refs[3:])
        cb_ref = rest.pop(0) if fox else None
        sink_ref = rest.pop(0) if has_sink else None
        o_ref, lse_ref = rest
        i = pl.program_id(1)
        low = _lane() < HEAD_DIM
        top = lax.broadcasted_iota(jnp.int32, (LANES, 1), 0) < HEAD_DIM
        q2 = q_ref[...]
        zero = jnp.zeros_like(q2)
        qms = (jnp.where(low, q2, zero), jnp.where(low, zero, q2))

        def tile(k0, n_keys, off, carry, masked, queries=slice(0, t)):
            nq = queries.stop - queries.start
            kblk, vblk = k_ref[pl.ds(k0, n_keys), :], v_ref[pl.ds(k0, n_keys), :]
            valid = _tile_mask(n_keys, nq, off, window) if masked else None
            def scores(h):
                return lax.dot_general(kblk, qms[h][queries], _NT, preferred_element_type=f32)

            def softmax(h, sc):
                m, l, _ = carry[h]
                if fox:
                    sc = sc - _wide(cb_ref[pl.ds(k0, n_keys), h * LANES:(h + 1) * LANES], nq)
                if masked:
                    sc = jnp.where(valid, sc, NEG)
                m_new = jnp.maximum(m, jnp.max(sc, axis=0, keepdims=True))
                p = jnp.exp(sc - m_new)
                alpha = jnp.exp(m - m_new)
                return m_new, alpha * l + jnp.sum(p, axis=0, keepdims=True), alpha, p.astype(bf16)

            def update(h, m_new, l, alpha, p):
                return m_new, l, alpha * carry[h][2] + lax.dot_general(vblk, p, _TN, preferred_element_type=f32)

            if window:
                return tuple(update(h, *softmax(h, scores(h))) for h in range(2))
            scs = [scores(h) for h in range(2)]
            stats = [softmax(h, scs[h]) for h in range(2)]
            return tuple(update(h, *stats[h]) for h in range(2))

        def start(nq):
            if has_sink:
                return tuple((_wide(sink_ref[h:h + 1, :], nq), jnp.ones((1, nq), f32), jnp.zeros((LANES, nq), f32))
                             for h in range(2))
            return tuple((jnp.full((1, nq), NEG, f32), jnp.zeros((1, nq), f32), jnp.zeros((LANES, nq), f32)) for h in range(2))

        def finish(carry, queries):
            (m0, l0, a0), (m1, l1, a1) = carry
            o_t = jnp.where(top, a0 * (1.0 / l0), a1 * (1.0 / l1))
            o_ref[queries, :] = o_t.T.astype(bf16)
            lse_ref[0:1, queries] = m0 + jnp.log(l0)
            lse_ref[1:2, queries] = m1 + jnp.log(l1)

        if window:
            for c in range(t // LANES):
                queries = slice(c * LANES, (c + 1) * LANES)
                q0 = i * t + c * LANES
                k0 = pl.multiple_of(jnp.maximum(q0 - window, 0), LANES)
                finish(tile(k0, LANES + window, q0 - k0, start(LANES), True, queries), queries)
        else:
            carry = lax.fori_loop(0, i, lambda kb, c: tile(pl.multiple_of(kb * t, t), t, 0, c, False), start(t))
            finish(tile(pl.multiple_of(i * t, t), t, 0, carry, True), slice(0, t))

    q_spec = pl.BlockSpec((t, LANES), lambda j, i: (i, j))
    kv_spec = pl.BlockSpec((s, LANES), lambda j, i: (0, j))
    in_specs, args = [q_spec, kv_spec, kv_spec], [q, k, v]
    if fox:
        in_specs += [pl.BlockSpec((s, 2 * LANES), lambda j, i: (0, j))]
        args += [cum_b]
    if has_sink:
        in_specs += [pl.BlockSpec((None, 2, LANES), lambda j, i: (j, 0, 0))]
        args += [sink_rows.reshape(N_PAIRS, 2, LANES)]
    return pl.pallas_call(
        body, name=name, grid=(N_PAIRS, s // t), in_specs=in_specs,
        out_specs=[q_spec, pl.BlockSpec((None, 2, t), lambda j, i: (j, 0, i))],
        out_shape=[jax.ShapeDtypeStruct((s, N_PAIRS * LANES), bf16), jax.ShapeDtypeStruct((N_PAIRS, 2, s), f32)],
        compiler_params=_params(2),
    )(*args)


def _attn_delta(do, o, name, *, lse=None, sink_rows=None):
    s, hw = do.shape
    tm = _row_tile(s, 512)
    has_sink = sink_rows is not None

    def body(*refs):
        do_ref, o_ref = refs[:2]
        if has_sink:
            lse_ref, sink_ref, dl_ref, ds_ref = refs[2:]

            @pl.when(pl.program_id(0) == 0)
            def _():
                ds_ref[...] = jnp.zeros_like(ds_ref)
        else:
            dl_ref, = refs[2:]
        for j in range(N_PAIRS):
            cols = slice(j * LANES, (j + 1) * LANES)
            prod_t = (do_ref[:, cols].astype(f32) * o_ref[:, cols].astype(f32)).T
            for h in range(2):
                dl = jnp.sum(prod_t[h * HEAD_DIM:(h + 1) * HEAD_DIM, :], axis=0, keepdims=True)
                dl_ref[j, h:h + 1, :] = dl
                if has_sink:
                    r = 2 * j + h
                    p_sink = jnp.exp(sink_ref[r:r + 1, 0:1] - lse_ref[j, h:h + 1, :])
                    ds_ref[r:r + 1, :] += -jnp.sum(p_sink * dl, axis=1, keepdims=True)

    rows_spec = pl.BlockSpec((N_PAIRS, 2, tm), lambda i: (0, 0, i))
    in_specs, args = [_row_spec(tm, hw)] * 2, [do, o]
    out_specs, out_shape = [rows_spec], [jax.ShapeDtypeStruct((N_PAIRS, 2, s), f32)]
    if has_sink:
        in_specs += [rows_spec, _vec_spec(LANES, N_HEADS)]
        args += [lse, sink_rows]
        out_specs += [_vec_spec(LANES, N_HEADS)]
        out_shape += [jax.ShapeDtypeStruct((N_HEADS, LANES), f32)]
    return pl.pallas_call(
        body, name=name, grid=(s // tm,), in_specs=in_specs, out_specs=out_specs, out_shape=out_shape,
        compiler_params=_params(1),
    )(*args)


def _attn_bwd(q, k, v, do, lse, delta, name, *, cum_b=None, window=None, t=256):
    s = q.shape[0]
    t = _row_tile(s, t)
    nblk = s // t
    fox = cum_b is not None
    assert not window or (window % LANES == 0 and LANES + window <= s)

    def body(*refs):
        k_ref, v_ref, q_ref, do_ref, lse_ref, dl_ref = refs[:6]
        rest = list(refs[6:])
        cb_ref = rest.pop(0) if fox else None
        dq_ref, dk_ref, dv_ref = rest[:3]
        dcs_ref, rs_ref = (rest[3], rest[4]) if fox else (None, None)
        b = pl.program_id(1)
        k0 = pl.multiple_of(b * t, t)

        @pl.when(b == 0)
        def _():
            dq_ref[...] = jnp.zeros_like(dq_ref)
            if fox:
                rs_ref[...] = jnp.zeros_like(rs_ref)

        dk_ref[...] = jnp.zeros_like(dk_ref)
        dv_ref[...] = jnp.zeros_like(dv_ref)
        if fox:
            dcs_ref[...] = jnp.zeros_like(dcs_ref)
        low = _lane() < HEAD_DIM
        top = lax.broadcasted_iota(jnp.int32, (LANES, 1), 0) < HEAD_DIM
        kblk, vblk = k_ref[...], v_ref[...]
        k_t = kblk.astype(f32).T.astype(bf16)
        cks = [_wide(cb_ref[pl.ds(k0, t), h * LANES:(h + 1) * LANES], t) for h in range(2)] if fox else None

        def tile(q0, n_queries, off, masked, keys=slice(0, t)):
            cols = pl.ds(q0, n_queries)
            q2, do2 = q_ref[cols, :], do_ref[cols, :]
            zero = jnp.zeros_like(q2)
            valid = _tile_mask(keys.stop - keys.start, n_queries, off, window) if masked else None
            dq_parts = []
            for h in range(2):
                qm = jnp.where(low, q2, zero) if h == 0 else jnp.where(low, zero, q2)
                dom = jnp.where(low, do2, zero) if h == 0 else jnp.where(low, zero, do2)
                sc = lax.dot_general(kblk[keys], qm, _NT, preferred_element_type=f32)
                if fox:
                    sc = sc - cks[h]
                if masked:
                    sc = jnp.where(valid, sc, NEG)
                p = jnp.exp(sc - lse_ref[h:h + 1, cols])
                dp = lax.dot_general(vblk[keys], dom, _NT, preferred_element_type=f32)
                ds = p * (dp - dl_ref[h:h + 1, cols])
                pb, dsb = p.astype(bf16), ds.astype(bf16)
                dv_ref[keys, :] += jnp.dot(pb, dom, preferred_element_type=f32)
                dk_ref[keys, :] += jnp.dot(dsb, qm, preferred_element_type=f32)
                dq_parts.append(jnp.dot(k_t[:, keys], dsb, preferred_element_type=f32))
                if fox:
                    dcs_ref[:, h * LANES:(h + 1) * LANES] += sum(ds[:, g * LANES:(g + 1) * LANES] for g in range(t // LANES))
                    rs_ref[h:h + 1, cols] += jnp.sum(ds, axis=0, keepdims=True)
            dq_ref[:, cols] += jnp.where(top, dq_parts[0], dq_parts[1])

        def later_block(qb, carry):
            tile(pl.multiple_of(qb * t, t), t, 0, False)
            return carry

        if window:
            for c in range(t // LANES):
                first = b * t + c * LANES
                q0 = pl.multiple_of(jnp.minimum(first, s - (LANES + window)), LANES)
                tile(q0, LANES + window, q0 - first, True, slice(c * LANES, (c + 1) * LANES))
        else:
            tile(k0, t, 0, True)
            lax.fori_loop(b + 1, nblk, later_block, 0)

    kv_spec = pl.BlockSpec((t, LANES), lambda j, b: (b, j))
    seq_spec = pl.BlockSpec((s, LANES), lambda j, b: (0, j))
    rows_spec = pl.BlockSpec((None, 2, s), lambda j, b: (j, 0, 0))
    hw = N_PAIRS * LANES
    in_specs, args = [kv_spec, kv_spec, seq_spec, seq_spec, rows_spec, rows_spec], [k, v, q, do, lse, delta]
    out_specs = [pl.BlockSpec((LANES, s), lambda j, b: (j, 0)), kv_spec, kv_spec]
    out_shape = [jax.ShapeDtypeStruct((hw, s), f32), jax.ShapeDtypeStruct((s, hw), f32), jax.ShapeDtypeStruct((s, hw), f32)]
    if fox:
        in_specs += [pl.BlockSpec((s, 2 * LANES), lambda j, b: (0, j))]
        args += [cum_b]
        out_specs += [pl.BlockSpec((t, 2 * LANES), lambda j, b: (b, j)), rows_spec]
        out_shape += [jax.ShapeDtypeStruct((s, N_HEADS * LANES), f32), jax.ShapeDtypeStruct((N_PAIRS, 2, s), f32)]
    return pl.pallas_call(
        body, name=name, grid=(N_PAIRS, nblk), in_specs=in_specs, out_specs=out_specs, out_shape=out_shape,
        compiler_params=_params(2),
    )(*args)


def _branch_merge(o_a, o_b, w_a, w_b, gl, name):
    s, k = o_a.shape
    d = w_a.shape[1]
    tm = _row_tile(s, 1024)

    def body(oa_ref, ob_ref, wa_ref, wb_ref, g_ref, ba_ref, bb_ref, m_ref):
        ba = jnp.dot(oa_ref[...], wa_ref[...], preferred_element_type=f32)
        bb = jnp.dot(ob_ref[...], wb_ref[...], preferred_element_type=f32)
        g0, g1 = jax.nn.sigmoid(g_ref[:, :d].astype(f32)), jax.nn.sigmoid(g_ref[:, d:].astype(f32))
        ba_ref[...] = ba.astype(bf16)
        bb_ref[...] = bb.astype(bf16)
        m_ref[...] = (g0 * ba + g1 * bb).astype(bf16)

    whole = pl.BlockSpec((k, d), lambda i: (0, 0))
    return pl.pallas_call(
        body, name=name, grid=(s // tm,),
        in_specs=[_row_spec(tm, k), _row_spec(tm, k), whole, whole, _row_spec(tm, 2 * d)],
        out_specs=[_row_spec(tm, d)] * 3, out_shape=[jax.ShapeDtypeStruct((s, d), bf16)] * 3, compiler_params=_params(1),
    )(o_a, o_b, w_a, w_b, gl)


def _merge_bwd(dm, ba, bb, gl, name):
    s, d = ba.shape
    tm = _row_tile(s, 512)

    def body(dm_ref, a_ref, b_ref, g_ref, da_ref, db_ref, dg_ref):
        dmv = dm_ref[...].astype(f32)
        g0, g1 = jax.nn.sigmoid(g_ref[:, :d].astype(f32)), jax.nn.sigmoid(g_ref[:, d:].astype(f32))
        da_ref[...] = (dmv * g0).astype(bf16)
        db_ref[...] = (dmv * g1).astype(bf16)
        dg_ref[:, :d] = (dmv * a_ref[...].astype(f32) * (g0 * (1.0 - g0))).astype(bf16)
        dg_ref[:, d:] = (dmv * b_ref[...].astype(f32) * (g1 * (1.0 - g1))).astype(bf16)

    return pl.pallas_call(
        body, name=name, grid=(s // tm,), in_specs=[_row_spec(tm, d)] * 3 + [_row_spec(tm, 2 * d)],
        out_specs=[_row_spec(tm, d)] * 2 + [_row_spec(tm, 2 * d)],
        out_shape=[jax.ShapeDtypeStruct((s, d), bf16)] * 2 + [jax.ShapeDtypeStruct((s, 2 * d), bf16)],
        compiler_params=_params(1),
    )(dm, ba, bb, gl)


GLU_TILE = 256


def _ffn_in_swiglu(h, w_t, name):
    s, d = h.shape
    f = w_t.shape[0] // 2
    tm = _row_tile(s, 2048)
    tg = GLU_TILE
    nb = f // tg

    def body(h_ref, wg_ref, wu_ref, g_ref, u_ref, act_ref):
        hv = h_ref[...]
        g = lax.dot_general(hv, wg_ref[...], _NT, preferred_element_type=f32)
        u = lax.dot_general(hv, wu_ref[...], _NT, preferred_element_type=f32)
        g_ref[...] = g.astype(bf16)
        u_ref[...] = u.astype(bf16)
        act_ref[...] = (g * jax.nn.sigmoid(g) * u).astype(bf16)

    col = pl.BlockSpec((tm, tg), lambda i, j: (i, j))
    return pl.pallas_call(
        body, name=name, grid=(s // tm, nb),
        in_specs=[pl.BlockSpec((tm, d), lambda i, j: (i, 0)), pl.BlockSpec((tg, d), lambda i, j: (j, 0)),
                  pl.BlockSpec((tg, d), lambda i, j: (j + nb, 0))],
        out_specs=[col] * 3, out_shape=[jax.ShapeDtypeStruct((s, f), bf16)] * 3, compiler_params=_params(2),
    )(h, w_t, w_t)


def _ffn_out_dgrad_swiglu(dy, w_out, g, u, name):
    s, d = dy.shape
    f = g.shape[1]
    tm = _row_tile(s, 2048)
    tg = GLU_TILE

    def body(dy_ref, w_ref, g_ref, u_ref, dg_ref, du_ref):
        dv = lax.dot_general(dy_ref[...], w_ref[...], _NT, preferred_element_type=f32)
        gv, uv = g_ref[...].astype(f32), u_ref[...].astype(f32)
        sg = jax.nn.sigmoid(gv)
        dg_ref[...] = (dv * uv * (sg * (1.0 + gv * (1.0 - sg)))).astype(bf16)
        du_ref[...] = (dv * (gv * sg)).astype(bf16)

    col = pl.BlockSpec((tm, tg), lambda i, j: (i, j))
    return pl.pallas_call(
        body, name=name, grid=(s // tm, f // tg),
        in_specs=[pl.BlockSpec((tm, d), lambda i, j: (i, 0)), pl.BlockSpec((tg, d), lambda i, j: (j, 0)), col, col],
        out_specs=[col] * 2, out_shape=[jax.ShapeDtypeStruct((s, f), bf16)] * 2, compiler_params=_params(2),
    )(dy, w_out, g, u)


def _sum_matmul(terms, name, after=None):
    s = terms[0][0].shape[0]
    d = terms[0][1].shape[1]
    k = sum(a.shape[1] for a, _, _ in terms)
    tm, tn = _matmul_tiles(s, d, k, terms[0][0].dtype.itemsize, terms[0][1].dtype.itemsize, 4)
    n = len(terms)

    def body(*refs):
        acc = jnp.dot(refs[0][...], refs[n][...], preferred_element_type=f32)
        for i in range(1, n):
            acc = acc + jnp.dot(refs[i][...], refs[n + i][...], preferred_element_type=f32)
        refs[-1][...] = acc

    extra = [] if after is None else [after]
    a_specs = [pl.BlockSpec((tm, a.shape[1]), lambda i, j: (i, 0)) for a, _, _ in terms]
    b_specs = [pl.BlockSpec((a.shape[1], tn), lambda i, j, r=r: (r, j)) for a, _, r in terms]
    return pl.pallas_call(
        body, name=name, grid=(s // tm, d // tn),
        in_specs=a_specs + b_specs + [pl.BlockSpec(memory_space=pl.ANY)] * len(extra),
        out_specs=pl.BlockSpec((tm, tn), lambda i, j: (i, j)),
        out_shape=jax.ShapeDtypeStruct((s, d), f32), compiler_params=_params(2),
    )(*[a for a, _, _ in terms], *[b for _, b, _ in terms], *extra)


def _wgrad_stack(parts, h, name):
    s, m = parts[0].shape
    d = h.shape[1]
    tm = 256
    nb = m // tm
    n = len(parts)

    def body(*refs):
        i = pl.program_id(0)
        for p in range(n):
            @pl.when(i // nb == p)
            def _(p=p):
                refs[n + 1][...] = lax.dot_general(refs[p][...], refs[n][...], _TN, preferred_element_type=f32).astype(bf16)

    a_specs = [pl.BlockSpec((s, tm), lambda i, p=p: (0, jnp.clip(i - p * nb, 0, nb - 1))) for p in range(n)]
    return pl.pallas_call(
        body, name=name, grid=(n * nb,), in_specs=a_specs + [pl.BlockSpec((s, d), lambda i: (0, 0))],
        out_specs=pl.BlockSpec((tm, d), lambda i: (i, 0)),
        out_shape=jax.ShapeDtypeStruct((n * m, d), bf16), compiler_params=_params(1),
    )(*parts, h)


def _ada_fwd(c_all, w, b, name):
    def body(c_ref, w_ref, b_ref, o_ref):
        o_ref[...] = jnp.dot(c_ref[...].astype(bf16), w_ref[...].astype(bf16), preferred_element_type=f32) + b_ref[...]

    return pl.pallas_call(
        body, name=name, out_shape=jax.ShapeDtypeStruct((c_all.shape[0], w.shape[1]), f32), compiler_params=_params(),
    )(c_all, w, b)


def _ada_wgrad(c_all, d_all, name):
    n, d = c_all.shape
    w = d_all.shape[1]

    def body(c_ref, d_ref, o_ref):
        eye = (lax.broadcasted_iota(jnp.int32, (n, n), 0) == lax.broadcasted_iota(jnp.int32, (n, n), 1)).astype(f32)
        ct = lax.dot_general(c_ref[...], eye, _TN, precision=lax.Precision.HIGHEST, preferred_element_type=f32)
        g = ct[:, 0:1] * d_ref[0:1, :]
        for bi in range(1, n):
            g = g + ct[:, bi:bi + 1] * d_ref[bi:bi + 1, :]
        o_ref[0] = g

    return pl.pallas_call(
        body, name=name, out_shape=jax.ShapeDtypeStruct((1, d, w), f32), compiler_params=_params(),
    )(c_all, d_all)


def _adamw(parts, w, m, v, name, mine=None):
    r, c = w.shape
    n_parts = parts.shape[0]
    row_tiles = [t for t in range(min(r, 256), 0, -1) if r % t == 0 and (t % 16 == 0 or t == r)]
    if row_tiles:
        tr, tc = row_tiles[0], c
    else:
        tr, tc = r, next(t for t in (256, LANES) if c % t == 0)

    def body(p_ref, *rest):
        own_ref = rest[0] if mine is not None else None
        w_ref, m_ref, v_ref, g_ref, d_ref, nm_ref, nv_ref = rest[-7:]
        if mine is not None:
            x, y, cc = _me()
            me = 4 * x + 2 * y + cc

        def part(i):
            if mine is None:
                return p_ref[i].astype(f32)
            return jnp.where(me == i, own_ref[i], p_ref[i]).astype(f32)

        g = part(0)
        for i in range(1, n_parts):
            g = g + part(i)
        mm = ADAM_B1 * m_ref[...] + (1.0 - ADAM_B1) * g
        vv = ADAM_B2 * v_ref[...] + (1.0 - ADAM_B2) * (g * g)
        m_hat = mm / (1.0 - ADAM_B1 ** ADAM_STEP)
        v_hat = vv / (1.0 - ADAM_B2 ** ADAM_STEP)
        g_ref[...] = g
        d_ref[...] = -ADAM_LR * (m_hat / (jnp.sqrt(v_hat) + ADAM_EPS) + ADAM_WD * w_ref[...])
        nm_ref[...] = mm
        nv_ref[...] = vv

    spec = pl.BlockSpec((tr, tc), lambda i, j: (i, j))
    stack = [parts] if mine is None else [parts, mine]
    return pl.pallas_call(
        body, name=name, grid=(r // tr, c // tc),
        in_specs=[pl.BlockSpec((n_parts, tr, tc), lambda i, j: (0, i, j))] * len(stack) + [spec] * 3,
        out_specs=[spec] * 4, out_shape=[jax.ShapeDtypeStruct((r, c), f32)] * 4, compiler_params=_params(2),
    )(*stack, w, m, v)


def _me():
    return lax.axis_index("x"), lax.axis_index("y"), lax.axis_index("c")


def _all_gather(arrays, name, vmem=False, after=None):
    n = len(arrays)
    space = pltpu.VMEM if vmem else pl.ANY
    extra = [] if after is None else [after]

    def body(*refs):
        ins = refs[:n]
        outs = refs[n + len(extra):2 * n + len(extra)]
        send_sems, recv_sems, local_sems = refs[2 * n + len(extra):]
        x, y, c = _me()
        me, sibling = (x, y, c), (x, y, 1 - c)
        chips = [(1 - x, y), (x, 1 - y), (1 - x, 1 - y)]

        def rows(a, dev):
            return outs[a].at[4 * dev[0] + 2 * dev[1] + dev[2]]

        def copy(a, k, block, to, src=None):
            return pltpu.make_async_remote_copy(
                src_ref=rows(a, block) if src is None else src, dst_ref=rows(a, block),
                send_sem=send_sems.at[a, k], recv_sem=recv_sems.at[a, k], device_id=to, device_id_type=MESH)

        mine = [pltpu.make_async_copy(ins[a], rows(a, me), local_sems.at[a]) for a in range(n)]
        for cp in mine:
            cp.start()
        first = []
        for a in range(n):
            first.append(copy(a, 0, me, sibling, src=ins[a]))
            first += [copy(a, 1 + j, me, (*chip, c), src=ins[a]) for j, chip in enumerate(chips)]
        for cp in first:
            cp.start()
        passed = []
        for j, chip in enumerate(chips):
            for a in range(n):
                copy(a, 1 + j, (*chip, c), me).wait_recv()
                fwd = copy(a, 4 + j, (*chip, c), sibling)
                fwd.start()
                passed.append(fwd)
        for a in range(n):
            copy(a, 0, sibling, me).wait_recv()
            for j, chip in enumerate(chips):
                copy(a, 4 + j, (*chip, 1 - c), me).wait_recv()
        for cp in first + passed:
            cp.wait_send()
        for cp in mine:
            cp.wait()

    outs = pl.pallas_call(
        body, name=name,
        in_specs=[pl.BlockSpec(memory_space=space)] * n + [pl.BlockSpec(memory_space=pl.ANY)] * len(extra),
        out_specs=[pl.BlockSpec(memory_space=space)] * n,
        out_shape=[jax.ShapeDtypeStruct((N_DEV,) + a.shape, a.dtype) for a in arrays],
        scratch_shapes=[pltpu.SemaphoreType.DMA((n, 7)), pltpu.SemaphoreType.DMA((n, 7)), pltpu.SemaphoreType.DMA((n,))],
        compiler_params=pltpu.CompilerParams(vmem_limit_bytes=VMEM_LIMIT),
    )(*arrays, *extra)
    return list(outs)


_FLIPS = ((0, 0, 1), (1, 0, 0), (0, 1, 0), (1, 1, 0), (1, 0, 1), (0, 1, 1), (1, 1, 1))
_HBM = pl.BlockSpec(memory_space=pltpu.HBM)
_SEM = pl.BlockSpec(memory_space=pltpu.SEMAPHORE)


def _exchange_copies(scatter, srcs, lands, send_sems, recv_sems):
    x, y, c = _me()
    me_row = 4 * x + 2 * y + c
    out = []
    for k, (fx, fy, fc) in enumerate(_FLIPS):
        peer = (x ^ fx, y ^ fy, c ^ fc)
        peer_row = 4 * peer[0] + 2 * peer[1] + peer[2]
        for a in range(len(srcs)):
            out.append(pltpu.make_async_remote_copy(
                src_ref=srcs[a].at[peer_row] if scatter else srcs[a], dst_ref=lands[a].at[me_row],
                send_sem=send_sems.at[7 * a + k], recv_sem=recv_sems.at[7 * a + k], device_id=peer, device_id_type=MESH))
    return out


def _exchange_start(arrays, scatter, name, after=None):
    n = len(arrays)
    lands = [lax.empty(a.shape if scatter else (N_DEV,) + a.shape, a.dtype) for a in arrays]
    extra = [] if after is None else [after]

    def body(*refs):
        srcs, zones = refs[:n], refs[n:2 * n]
        send_sems, recv_sems = refs[2 * n + len(extra)], refs[2 * n + len(extra) + 1]
        token = refs[-1]
        for cp in _exchange_copies(scatter, srcs, zones, send_sems, recv_sems):
            cp.start()
        token[...] = jnp.zeros_like(token)

    thru = [pltpu.HBM(a.shape, a.dtype) for a in list(arrays) + lands]
    outs = pl.pallas_call(
        body, name=name,
        out_shape=(pltpu.SemaphoreType.DMA((7 * n,)), pltpu.SemaphoreType.DMA((7 * n,)), *thru, jax.ShapeDtypeStruct((8, LANES), f32)),
        in_specs=[_HBM] * (2 * n) + [pl.BlockSpec(memory_space=pl.ANY)] * len(extra),
        out_specs=(_SEM, _SEM, *[_HBM] * (2 * n), pl.BlockSpec(memory_space=pltpu.VMEM)),
        input_output_aliases={i: 2 + i for i in range(2 * n)},
        compiler_params=pltpu.CompilerParams(has_side_effects=pltpu.SideEffectType.DATAFLOW_SIDE_EFFECTING),
    )(*[pltpu.with_memory_space_constraint(a, pltpu.HBM) for a in list(arrays) + lands], *extra)
    return dict(n=n, scatter=scatter, sems=outs[:2], srcs=outs[2:2 + n], lands=outs[2 + n:2 + 2 * n], token=outs[-1])


def _exchange_wait(handle, after, name):
    n, scatter = handle["n"], handle["scatter"]

    def body(*refs):
        srcs, zones = refs[:n], refs[n:2 * n]
        send_sems, recv_sems = refs[2 * n], refs[2 * n + 1]
        for cp in _exchange_copies(scatter, srcs, zones, send_sems, recv_sems):
            cp.wait_send()
            cp.wait_recv()

    thru = [pltpu.HBM(a.shape, a.dtype) for a in list(handle["srcs"]) + list(handle["lands"])]
    outs = pl.pallas_call(
        body, name=name, out_shape=tuple(thru),
        in_specs=[_HBM] * (2 * n) + [_SEM, _SEM, pl.BlockSpec(memory_space=pl.ANY)], out_specs=tuple([_HBM] * (2 * n)),
        input_output_aliases={i: i for i in range(2 * n)},
        compiler_params=pltpu.CompilerParams(has_side_effects=pltpu.SideEffectType.DATAFLOW_SIDE_EFFECTING),
    )(*handle["srcs"], *handle["lands"], *handle["sems"], after)
    return list(outs[n:])


def _cols_from_shards(g):
    return jnp.transpose(g, (1, 0, 2)).reshape(g.shape[1], -1)


def _shards_from_cols(a):
    return jnp.transpose(a.reshape(a.shape[0], N_DEV, -1), (1, 0, 2))


def _local_step(x, positions, ada, g_pre_mix, g_post_mix, b_f, sinks, g_pre_ffn, g_post_ffn, target,
                w_in_t, late_weights, on_grads):
    s, d = x.shape
    row = lambda v: v.reshape(1, -1)
    shift_m, scale_m, gate_m, shift_f, scale_f, gate_f = (ada[i:i + 1] for i in range(6))
    w_gate_t, w_qkv_t = w_in_t[F_OFF + N_HEADS:], w_in_t[:QKV_W]
    w_f_t = jnp.pad(w_in_t[F_OFF:F_OFF + N_HEADS], ((0, LANES - N_HEADS), (0, 0)))
    bf_row = jnp.pad(row(b_f), ((0, 0), (0, LANES - N_HEADS)))
    sink_rows = jnp.broadcast_to(sinks.reshape(N_HEADS, 1).astype(f32), (N_HEADS, LANES))
    inv_freq = 1.0 / (ROPE_THETA ** (jnp.arange(0, HEAD_DIM, 2, dtype=f32) / HEAD_DIM))
    cos, sin_s = _rope_tables(positions.reshape(s, 1), jnp.tile(inv_freq, 4).reshape(1, LANES), "rope_tables")

    h1 = _prenorm(x, row(g_pre_mix), scale_m, shift_m, "prenorm_mix")
    gl = _matmul(h1, w_gate_t, "nt", bf16, "proj_gate")
    qa, ka, va, qb, kb, vb = _proj_qkv(h1, w_qkv_t, cos, sin_s, "proj_qkv")
    fl = _matmul(h1, w_f_t, "nt", f32, "proj_forget")
    cum_b = _forget_prep(fl, bf_row, "forget_prep")
    o_a, lse_a = _attn_fwd(qa, ka, va, "swa_fwd", sink_rows=sink_rows, window=WINDOW, t=512)
    o_b, lse_b = _attn_fwd(qb, kb, vb, "fox_fwd", cum_b=cum_b, t=1024)
    w_branch_a, w_branch_b, w_out, w_ffn_in_t, w_ffn_out = late_weights(o_b)
    ba, bb, merged = _branch_merge(o_a, o_b, w_branch_a, w_branch_b, gl, "branch_merge")
    y1 = _matmul(merged, w_out, "nn", f32, "out_proj")

    x2, h2 = _postnorm_prenorm(x, y1, row(g_post_mix), gate_m, row(g_pre_ffn), scale_f, shift_f, "postnorm_mix_prenorm_ffn")
    g_ff, u_ff, act = _ffn_in_swiglu(h2, w_ffn_in_t, "ffn_in_swiglu")
    y2 = _matmul(act, w_ffn_out, "nn", f32, "ffn_out")
    loss_row, d_out, d_y2, vec_pf = _loss_tail(x2, y2, row(g_post_ffn), gate_f, target, "loss_tail")

    g_w_ffn_out = _matmul(act, d_y2, "tn", bf16, "ffn_out_wgrad")
    dg_ff, du_ff = _ffn_out_dgrad_swiglu(d_y2, w_ffn_out, g_ff, u_ff, "ffn_out_dgrad_swiglu")
    g_w_ffn_in_t = _wgrad_stack([dg_ff, du_ff], h2, "ffn_in_wgrad")
    sent = on_grads(dict(w_ffn_in=g_w_ffn_in_t, w_ffn_out=g_w_ffn_out))
    d_h2 = _sum_matmul([(dg_ff, w_ffn_in_t, 0), (du_ff, w_ffn_in_t, 1)], "ffn_in_dgrad", after=sent)
    d_x2, vec_nf, d_y1, vec_pm = _prenorm_bwd(d_h2, x2, row(g_pre_ffn), scale_f, d_out, "prenorm_ffn_postnorm_mix_bwd",
                                              below=(y1, row(g_post_mix), gate_m))

    g_w_out = _matmul(merged, d_y1, "tn", bf16, "out_proj_wgrad")
    d_merged = _matmul(d_y1, w_out, "nt", bf16, "out_proj_dgrad")
    d_ba, d_bb, dgl = _merge_bwd(d_merged, ba, bb, gl, "merge_bwd")
    g_w_branch_a = _matmul(o_a, d_ba, "tn", bf16, "branch_a_wgrad")
    g_w_branch_b = _matmul(o_b, d_bb, "tn", bf16, "branch_b_wgrad")
    sent = on_grads(dict(w_out=g_w_out, w_branch_a=g_w_branch_a, w_branch_b=g_w_branch_b))
    d_oa = _matmul(d_ba, w_branch_a, "nt", bf16, "branch_a_dgrad", after=sent)
    d_ob = _matmul(d_bb, w_branch_b, "nt", bf16, "branch_b_dgrad", after=sent)
    delta_a, d_sink = _attn_delta(d_oa, o_a, "swa_delta", lse=lse_a, sink_rows=sink_rows)
    delta_b, = _attn_delta(d_ob, o_b, "fox_delta")
    dqa_t, dka, dva = _attn_bwd(qa, ka, va, d_oa, lse_a, delta_a, "swa_bwd", window=WINDOW, t=512)
    dqb_t, dkb, dvb, dcs, rs = _attn_bwd(qb, kb, vb, d_ob, lse_b, delta_b, "fox_bwd", cum_b=cum_b, t=512)
    dqkv = _qkv_prep_bwd(dqa_t, dka, dva, dqb_t, dkb, dvb, cos, sin_s, "qkv_prep_bwd")
    dfl, vec_bf = _forget_prep_bwd(rs.reshape(N_HEADS, s), dcs, fl, bf_row, "forget_prep_bwd")
    g_w_in_t = jnp.concatenate([_matmul(dqkv, h1, "tn", bf16, "qkv_wgrad"), _matmul(dfl, h1, "tn", bf16, "forget_wgrad")[:N_HEADS],
                                _matmul(dgl, h1, "tn", bf16, "gate_wgrad")], axis=0)
    sent = on_grads(dict(w_in=g_w_in_t))
    d_h1 = _sum_matmul([(dgl, w_gate_t, 0), (dqkv, w_qkv_t, 0), (dfl, w_f_t, 0)], "in_proj_dgrad", after=sent)
    grad_x, vec_nm = _prenorm_bwd(d_h1, x, row(g_pre_mix), scale_m, d_x2, "prenorm_mix_bwd")

    d_ada = jnp.concatenate([vec_nm[0], vec_nm[1], vec_pm[0], vec_nf[0], vec_nf[1], vec_pf[0]])
    small = dict(b_ada=d_ada, g_pre_mix=vec_nm[2], g_post_mix=vec_pm[1], g_pre_ffn=vec_nf[2], g_post_ffn=vec_pf[1],
                 b_f=vec_bf[0, :N_HEADS], sinks=d_sink[:, 0], loss=loss_row[0, :1])
    return grad_x, small


_SMALL = (("b_ada", 6144), ("g_pre_mix", 1024), ("g_post_mix", 1024), ("g_pre_ffn", 1024), ("g_post_ffn", 1024),
          ("b_f", 128), ("sinks", 128), ("loss", 128))
_SMALL_ROWS = 88


def _pack_small(vals):
    parts = [jnp.pad(vals[k].reshape(-1).astype(f32), (0, n - vals[k].size)) for k, n in _SMALL]
    flat = jnp.concatenate(parts)
    return jnp.pad(flat, (0, _SMALL_ROWS * LANES - flat.size)).reshape(_SMALL_ROWS, LANES)


def _unpack_small(slab, shapes):
    flat, out, off = slab.reshape(-1), {}, 0
    for k, n in _SMALL:
        size = math.prod(shapes[k])
        out[k] = flat[off:off + size].reshape(shapes[k])
        off += n
    return out


def kernel(x, c, positions, w_ada, b_ada, g_pre_mix, g_post_mix, w_in, b_f, sinks, w_branch_a, w_branch_b, w_out, g_pre_ffn, g_post_ffn, w_ffn_in, w_ffn_out, loss_target, m_w_ada, m_b_ada, m_g_pre_mix, m_g_post_mix, m_w_in, m_b_f, m_sinks, m_w_branch_a, m_w_branch_b, m_w_out, m_g_pre_ffn, m_g_post_ffn, m_w_ffn_in, m_w_ffn_out, v_w_ada, v_b_ada, v_g_pre_mix, v_g_post_mix, v_w_in, v_b_f, v_sinks, v_w_branch_a, v_w_branch_b, v_w_out, v_g_pre_ffn, v_g_post_ffn, v_w_ffn_in, v_w_ffn_out):
    xi, yi, ci = _me()
    me = 4 * xi + 2 * yi + ci
    d = D_MODEL
    ada_w = w_ada.shape[2]

    c_all, = _all_gather([c], "gather_c", vmem=True)
    c_all = c_all.reshape(N_DEV, d)
    b_mine = lax.dynamic_slice(b_ada, (0, me * ada_w), (1, ada_w))
    ada_cols = _ada_fwd(c_all, w_ada[0], b_mine, "ada_fwd")

    transposed = ("w_in", "w_ffn_in")
    tr = lambda a: jnp.transpose(a[0])

    ada_all, g_in = _all_gather([ada_cols, tr(w_in).astype(bf16)], "gather_ada_w_in")
    ada = lax.dynamic_index_in_dim(ada_all, me, axis=1, keepdims=False).reshape(6, d)
    late = [w.astype(bf16) for w in (w_branch_a[0], w_branch_b[0], w_out[0], tr(w_ffn_in), w_ffn_out[0])]
    late_h = _exchange_start(late, False, "gather_late_start", after=g_in)

    def mine_into(zone, block):
        return lax.dynamic_update_index_in_dim(zone, block, me, 0)

    def rows_from_shards(g):
        return g.reshape(g.shape[0] * g.shape[1], g.shape[2])

    def late_weights(after):
        zones = _exchange_wait(late_h, after, "gather_late_wait")
        g_ba, g_bb, g_out, g_fi, g_fo = (mine_into(z, w) for z, w in zip(zones, late))
        return (_cols_from_shards(g_ba), _cols_from_shards(g_bb), rows_from_shards(g_out), rows_from_shards(g_fi),
                rows_from_shards(g_fo))

    row_sharded = ("w_out", "w_ffn_out") + transposed
    in_flight = []

    def on_grads(group):
        sends = [g.reshape(N_DEV, g.shape[0] // N_DEV, g.shape[1]) if nm in row_sharded else _shards_from_cols(g)
                 for nm, g in group.items()]
        handle = _exchange_start(sends, True, "scatter_start_%d" % len(in_flight))
        in_flight.append((list(group), sends, handle))
        return handle["token"]

    grad_x, small = _local_step(
        x[0], positions[0], ada + late_h["token"][0, 0], g_pre_mix[0], g_post_mix[0], b_f[0], sinks[0], g_pre_ffn[0],
        g_post_ffn[0], loss_target[0], rows_from_shards(g_in), late_weights, on_grads)

    ws = dict(w_in=(w_in, m_w_in, v_w_in), w_branch_a=(w_branch_a, m_w_branch_a, v_w_branch_a),
              w_branch_b=(w_branch_b, m_w_branch_b, v_w_branch_b), w_out=(w_out, m_w_out, v_w_out),
              w_ffn_in=(w_ffn_in, m_w_ffn_in, v_w_ffn_in), w_ffn_out=(w_ffn_out, m_w_ffn_out, v_w_ffn_out))
    res = {}

    def finish_group(gi, after):
        names, sends, handle = in_flight[gi]
        zones = _exchange_wait(handle, after, "scatter_wait_%d" % gi)
        for nm, zone, sent in zip(names, zones, sends):
            w, m, v = (tr(a) if nm in transposed else a[0] for a in ws[nm])
            out = _adamw(zone, w, m, v, "adamw_" + nm, mine=sent)
            after = out[0]
            res[nm] = [jnp.transpose(o) for o in out] if nm in transposed else out
        return after

    done = finish_group(1, finish_group(0, grad_x))

    slab_all, = _all_gather([_pack_small(small)], "gather_small", vmem=True, after=done)
    small_w = dict(b_ada=b_ada, g_pre_mix=g_pre_mix, g_post_mix=g_post_mix, g_pre_ffn=g_pre_ffn, g_post_ffn=g_post_ffn,
                   b_f=b_f, sinks=sinks, loss=jnp.zeros((1,), f32))
    small_m = dict(b_ada=m_b_ada, g_pre_mix=m_g_pre_mix, g_post_mix=m_g_post_mix, g_pre_ffn=m_g_pre_ffn,
                   g_post_ffn=m_g_post_ffn, b_f=m_b_f, sinks=m_sinks, loss=jnp.zeros((1,), f32))
    small_v = dict(b_ada=v_b_ada, g_pre_mix=v_g_pre_mix, g_post_mix=v_g_post_mix, g_pre_ffn=v_g_pre_ffn,
                   g_post_ffn=v_g_post_ffn, b_f=v_b_f, sinks=v_sinks, loss=jnp.ones((1,), f32))
    shapes = {k: small_w[k].shape for k, _ in _SMALL}
    s_out = _adamw(slab_all, _pack_small(small_w), _pack_small(small_m), _pack_small(small_v), "adamw_small")
    s_grad, s_delta, s_m, s_v = (_unpack_small(o, shapes) for o in s_out)

    d_ada_all = lax.dynamic_slice(slab_all[:, :6144 // LANES, :].reshape(N_DEV, 6144), (0, me * ada_w), (N_DEV, ada_w))
    ada_parts = _ada_wgrad(c_all, d_ada_all, "ada_wgrad")

    res["w_ada"] = _adamw(ada_parts, w_ada[0], m_w_ada[0], v_w_ada[0], "adamw_w_ada")
    finish_group(2, res["w_ada"][0])

    order = ["w_ada", "b_ada", "g_pre_mix", "g_post_mix", "w_in", "b_f", "sinks", "w_branch_a", "w_branch_b", "w_out",
             "g_pre_ffn", "g_post_ffn", "w_ffn_in", "w_ffn_out"]
    outs = [s_grad["loss"].reshape(()), grad_x[None]]
    for which, small_o in enumerate((s_grad, s_delta, s_m, s_v)):
        for nm in order:
            outs.append(res[nm][which][None] if nm in res else small_o[nm])
    return tuple(outs)
```

```python
import functools
import math

import jax
import jax.numpy as jnp
from jax import lax
from jax.experimental import pallas as pl
from jax.experimental.pallas import tpu as pltpu

f32 = jnp.float32
bf16 = jnp.bfloat16

D_MODEL = 1024
HEAD_DIM = 64
N_HEADS = 8
N_PAIRS = 4
QKV_W = 2304
GATE_W = 2048
F_OFF = 2304
IN_W = 4360
WINDOW = 128
ROPE_THETA = 10000.0
RMS_EPS = 1e-6
D_FF = 2816
N_DEV = 8
ADAM_LR, ADAM_B1, ADAM_B2, ADAM_EPS, ADAM_WD, ADAM_STEP = 0.001, 0.9, 0.999, 1e-08, 0.01, 10
NEG = -1e30
LANES = 128
VMEM_LIMIT = 48 * 1024 * 1024
MESH = pl.DeviceIdType.MESH

_NT = (((1,), (1,)), ((), ()))
_TN = (((0,), (0,)), ((), ()))


def _params(n_grid=0):
    sem = ("arbitrary",) * n_grid if n_grid else None
    return pltpu.CompilerParams(dimension_semantics=sem, vmem_limit_bytes=VMEM_LIMIT)


def _row_tile(s, want):
    t = min(s, want)
    assert s % t == 0, (s, t)
    return t


MATMUL_VMEM_BUDGET = 40 * 1024 * 1024


def _matmul_tiles(m, n, k, a_item, b_item, o_item):
    def tiles(d):
        return [t for t in range(LANES, min(d, 2048) + 1, LANES) if d % t == 0] or [d]

    best = None
    for tm in tiles(m):
        for tn in tiles(n):
            vmem = 2 * (tm * k * a_item + tn * k * b_item + tm * tn * o_item) + tm * tn * 4
            if vmem > MATMUL_VMEM_BUDGET:
                continue
            traffic = m * k * a_item + n * k * b_item * (1 if tn == n else m // tm) + m * n * o_item
            steps = (m // tm) * (n // tn)
            key = (traffic, 0, steps) if steps >= 4 else (traffic, 1, -steps)
            if best is None or key < best[0]:
                best = (key, tm, tn)
    assert best is not None, (m, n, k)
    return best[1], best[2]


def _matmul(a, b, mode, out_dtype, name, after=None):
    if mode == "nn":
        (m, k), n = a.shape, b.shape[1]
    elif mode == "nt":
        (m, k), n = a.shape, b.shape[0]
    else:
        (k, m), n = a.shape, b.shape[1]
    tm, tn = _matmul_tiles(m, n, k, a.dtype.itemsize, b.dtype.itemsize, jnp.dtype(out_dtype).itemsize)
    if mode == "nn":
        a_spec, b_spec, dims = pl.BlockSpec((tm, k), lambda i, j: (i, 0)), pl.BlockSpec((k, tn), lambda i, j: (0, j)), None
    elif mode == "nt":
        a_spec, b_spec, dims = pl.BlockSpec((tm, k), lambda i, j: (i, 0)), pl.BlockSpec((tn, k), lambda i, j: (j, 0)), _NT
    else:
        a_spec, b_spec, dims = pl.BlockSpec((k, tm), lambda i, j: (0, i)), pl.BlockSpec((k, tn), lambda i, j: (0, j)), _TN

    def body(a_ref, b_ref, *rest):
        o_ref = rest[-1]
        av, bv = a_ref[...].astype(bf16), b_ref[...].astype(bf16)
        if dims is None:
            r = jnp.dot(av, bv, preferred_element_type=f32)
        else:
            r = lax.dot_general(av, bv, dims, preferred_element_type=f32)
        o_ref[...] = r.astype(out_dtype)

    extra = [] if after is None else [after]
    return pl.pallas_call(
        body, name=name, grid=(m // tm, n // tn), in_specs=[a_spec, b_spec] + [pl.BlockSpec(memory_space=pl.ANY)] * len(extra),
        out_specs=pl.BlockSpec((tm, tn), lambda i, j: (i, j)),
        out_shape=jax.ShapeDtypeStruct((m, n), out_dtype), compiler_params=_params(2),
    )(a, b, *extra)


def _rstd(v):
    return lax.rsqrt(jnp.mean(v * v, axis=-1, keepdims=True) + RMS_EPS)


def _row_spec(tm, d):
    return pl.BlockSpec((tm, d), lambda i: (i, 0))


def _vec_spec(d, rows=1):
    return pl.BlockSpec((rows, d), lambda i: (0, 0))


def _prenorm(x, g, scale, shift, name):
    s, d = x.shape
    tm = _row_tile(s, 512)

    def body(x_ref, g_ref, sc_ref, sh_ref, h_ref):
        xv = x_ref[...]
        h = (xv * _rstd(xv) * g_ref[...]) * (1.0 + sc_ref[...]) + sh_ref[...]
        h_ref[...] = h.astype(bf16)

    return pl.pallas_call(
        body, name=name, grid=(s // tm,), in_specs=[_row_spec(tm, d)] + [_vec_spec(d)] * 3,
        out_specs=_row_spec(tm, d), out_shape=jax.ShapeDtypeStruct((s, d), bf16), compiler_params=_params(1),
    )(x, g, scale, shift)


def _postnorm_prenorm(x, y, g_post, gate, g_pre, scale, shift, name):
    s, d = x.shape
    tm = _row_tile(s, 512)

    def body(x_ref, y_ref, gp_ref, gate_ref, g_ref, sc_ref, sh_ref, x2_ref, h_ref):
        yv = y_ref[...]
        x2 = x_ref[...] + gate_ref[...] * (yv * _rstd(yv) * gp_ref[...])
        x2_ref[...] = x2
        h_ref[...] = ((x2 * _rstd(x2) * g_ref[...]) * (1.0 + sc_ref[...]) + sh_ref[...]).astype(bf16)

    return pl.pallas_call(
        body, name=name, grid=(s // tm,), in_specs=[_row_spec(tm, d)] * 2 + [_vec_spec(d)] * 5,
        out_specs=[_row_spec(tm, d)] * 2,
        out_shape=[jax.ShapeDtypeStruct((s, d), f32), jax.ShapeDtypeStruct((s, d), bf16)], compiler_params=_params(1),
    )(x, y, g_post, gate, g_pre, scale, shift)


def _rms_bwd(u, v, r):
    return r * u - v * (r * r * r) * jnp.mean(u * v, axis=-1, keepdims=True)


def _loss_tail(x, y, g, gate, target, name):
    s, d = x.shape
    tm = _row_tile(s, 512)

    def body(x_ref, y_ref, g_ref, gate_ref, t_ref, loss_ref, do_ref, dy_ref, vec_ref):
        @pl.when(pl.program_id(0) == 0)
        def _():
            loss_ref[...] = jnp.zeros_like(loss_ref)
            vec_ref[...] = jnp.zeros_like(vec_ref)
        yv = y_ref[...]
        r = _rstd(yv)
        yn = yv * r
        err = x_ref[...] + gate_ref[...] * (yn * g_ref[...]) - t_ref[...]
        loss_ref[...] += 0.5 * jnp.sum(jnp.mean(err * err, axis=-1, keepdims=True), axis=0, keepdims=True)
        dr = err / d
        do_ref[...] = dr
        dn = dr * gate_ref[...]
        vec_ref[0:1, :] += jnp.sum(dr * (yn * g_ref[...]), axis=0, keepdims=True)
        vec_ref[1:2, :] += jnp.sum(dn * yn, axis=0, keepdims=True)
        dy_ref[...] = _rms_bwd(dn * g_ref[...], yv, r).astype(bf16)

    return pl.pallas_call(
        body, name=name, grid=(s // tm,), in_specs=[_row_spec(tm, d)] * 2 + [_vec_spec(d)] * 2 + [_row_spec(tm, d)],
        out_specs=[_vec_spec(LANES), _row_spec(tm, d), _row_spec(tm, d), _vec_spec(d, 8)],
        out_shape=[jax.ShapeDtypeStruct((1, LANES), f32), jax.ShapeDtypeStruct((s, d), f32),
                   jax.ShapeDtypeStruct((s, d), bf16), jax.ShapeDtypeStruct((8, d), f32)],
        compiler_params=_params(1),
    )(x, y, g, gate, target)


def _dgrad_prenorm_bwd(terms, x, g, scale, dres, name, after=None, below=None):
    s, d = x.shape
    n = len(terms)
    k = sum(a.shape[1] for a, _, _ in terms)
    row_bytes = 2 * (2 * k) + d * (4 + 2 * 4 * 3 + (2 * 4 + 2 * 2 if below else 0))
    tm = next(t for t in (512, 256, 128) if s % t == 0 and 4 * k * d + t * row_bytes <= MATMUL_VMEM_BUDGET)
    extra = [] if after is None else [after]

    def body(*refs):
        a_refs, b_refs = refs[:n], refs[n:2 * n]
        x_ref, g_ref, sc_ref, dr_ref = refs[2 * n:2 * n + 4]
        n_in = 2 * n + 4 + (3 if below else 0) + len(extra)
        dx_ref, vec_ref = refs[n_in], refs[n_in + 1]
        if below:
            y_ref, gp_ref, gate_ref = refs[2 * n + 4:2 * n + 7]
            dy_ref, vec2_ref = refs[n_in + 2], refs[n_in + 3]

        @pl.when(pl.program_id(0) == 0)
        def _():
            vec_ref[...] = jnp.zeros_like(vec_ref)
            if below:
                vec2_ref[...] = jnp.zeros_like(vec2_ref)
        dhv = jnp.dot(a_refs[0][...], b_refs[0][...], preferred_element_type=f32)
        for i in range(1, n):
            dhv = dhv + jnp.dot(a_refs[i][...], b_refs[i][...], preferred_element_type=f32)
        xv = x_ref[...]
        r = _rstd(xv)
        xn = xv * r
        dn = dhv * (1.0 + sc_ref[...])
        vec_ref[0:1, :] += jnp.sum(dhv, axis=0, keepdims=True)
        vec_ref[1:2, :] += jnp.sum(dhv * (xn * g_ref[...]), axis=0, keepdims=True)
        vec_ref[2:3, :] += jnp.sum(dn * xn, axis=0, keepdims=True)
        dx = dr_ref[...] + _rms_bwd(dn * g_ref[...], xv, r)
        dx_ref[...] = dx
        if below:
            yv = y_ref[...]
            ry = _rstd(yv)
            yn = yv * ry
            dny = dx * gate_ref[...]
            vec2_ref[0:1, :] += jnp.sum(dx * (yn * gp_ref[...]), axis=0, keepdims=True)
            vec2_ref[1:2, :] += jnp.sum(dny * yn, axis=0, keepdims=True)
            dy_ref[...] = _rms_bwd(dny * gp_ref[...], yv, ry).astype(bf16)

    in_specs = ([_row_spec(tm, a.shape[1]) for a, _, _ in terms]
                + [pl.BlockSpec((a.shape[1], d), lambda i, r=r: (r, 0)) for a, _, r in terms]
                + [_row_spec(tm, d)] + [_vec_spec(d)] * 2 + [_row_spec(tm, d)])
    out_specs = [_row_spec(tm, d), _vec_spec(d, 8)]
    out_shape = [jax.ShapeDtypeStruct((s, d), f32), jax.ShapeDtypeStruct((8, d), f32)]
    args = [a for a, _, _ in terms] + [b for _, b, _ in terms] + [x, g, scale, dres]
    if below:
        in_specs += [_row_spec(tm, d)] + [_vec_spec(d)] * 2
        out_specs += [_row_spec(tm, d), _vec_spec(d, 8)]
        out_shape += [jax.ShapeDtypeStruct((s, d), bf16), jax.ShapeDtypeStruct((8, d), f32)]
        args += list(below)
    return pl.pallas_call(
        body, name=name, grid=(s // tm,), in_specs=in_specs + [pl.BlockSpec(memory_space=pl.ANY)] * len(extra),
        out_specs=out_specs, out_shape=out_shape, compiler_params=_params(1),
    )(*args, *extra)


def _lane():
    return lax.broadcasted_iota(jnp.int32, (1, LANES), 1)


def _rope_tables(pos_col, inv_freq, name):
    s = pos_col.shape[0]

    def body(p_ref, f_ref, cos_ref, sin_ref):
        ang = p_ref[...].astype(f32) * f_ref[...]
        first_half = (_lane() % HEAD_DIM) < HEAD_DIM // 2
        cos_ref[...] = jnp.cos(ang)
        sn = jnp.sin(ang)
        sin_ref[...] = jnp.where(first_half, -sn, sn)

    return pl.pallas_call(
        body, name=name, out_shape=[jax.ShapeDtypeStruct((s, LANES), f32)] * 2, compiler_params=_params(),
    )(pos_col, inv_freq)


def _swap_halves(v):
    first_half = (_lane() % HEAD_DIM) < HEAD_DIM // 2
    return jnp.where(first_half, pltpu.roll(v, LANES - HEAD_DIM // 2, axis=1), pltpu.roll(v, HEAD_DIM // 2, axis=1))


def _proj_qkv(h, w_qkv_t, cos, sin_s, name):
    s, d = h.shape
    tm = _row_tile(s, 512)
    scale = 1.0 / math.sqrt(HEAD_DIM)

    def body(h_ref, w_ref, c_ref, s_ref, qa_ref, ka_ref, va_ref, qb_ref, kb_ref, vb_ref):
        proj = lax.dot_general(h_ref[...], w_ref[...], _NT, preferred_element_type=f32)
        cs, sn = c_ref[...], s_ref[...]
        low = _lane() < HEAD_DIM

        def blk(j):
            return proj[:, j * LANES:(j + 1) * LANES]

        def rope(v):
            return v * cs + _swap_halves(v) * sn

        def expand(v):
            other = pltpu.roll(v, HEAD_DIM, axis=1)
            return jnp.where(low, v, other), jnp.where(low, other, v)

        for j in range(N_PAIRS):
            qa_ref[:, j * LANES:(j + 1) * LANES] = (rope(blk(j)) * scale).astype(bf16)
            qb_ref[:, j * LANES:(j + 1) * LANES] = (blk(6 + j) * scale).astype(bf16)
            kb_ref[:, j * LANES:(j + 1) * LANES] = blk(10 + j).astype(bf16)
            vb_ref[:, j * LANES:(j + 1) * LANES] = blk(14 + j).astype(bf16)
        k0, k1 = expand(rope(blk(4)))
        v0, v1 = expand(blk(5))
        for j in range(N_PAIRS):
            ka_ref[:, j * LANES:(j + 1) * LANES] = (k0 if j < 2 else k1).astype(bf16)
            va_ref[:, j * LANES:(j + 1) * LANES] = (v0 if j < 2 else v1).astype(bf16)

    hw = N_PAIRS * LANES
    return pl.pallas_call(
        body, name=name, grid=(s // tm,),
        in_specs=[_row_spec(tm, d), pl.BlockSpec((QKV_W, d), lambda i: (0, 0)), _row_spec(tm, LANES), _row_spec(tm, LANES)],
        out_specs=[_row_spec(tm, hw)] * 6, out_shape=[jax.ShapeDtypeStruct((s, hw), bf16)] * 6, compiler_params=_params(1),
    )(h, w_qkv_t, cos, sin_s)


def _qkv_prep_bwd(dqa_t, dka, dva, dqb_t, dkb, dvb, cos, sin_s, name):
    s = dka.shape[0]
    tm = _row_tile(s, 256)
    scale = 1.0 / math.sqrt(HEAD_DIM)
    hw = N_PAIRS * LANES
    t_spec = pl.BlockSpec((hw, tm), lambda i: (0, i))

    def body(dqa_ref, dka_ref, dva_ref, dqb_ref, dkb_ref, dvb_ref, c_ref, s_ref, o_ref):
        cs, sn = c_ref[...], s_ref[...]
        low = _lane() < HEAD_DIM

        def blk(ref, j):
            return ref[:, j * LANES:(j + 1) * LANES]

        def blk_t(ref, j):
            return ref[j * LANES:(j + 1) * LANES, :].T

        def unrope(v):
            return v * cs + _swap_halves(v * sn)

        def fold(ref):
            a, b = blk(ref, 0) + blk(ref, 1), blk(ref, 2) + blk(ref, 3)
            kv0 = a + pltpu.roll(a, HEAD_DIM, axis=1)
            kv1 = b + pltpu.roll(b, HEAD_DIM, axis=1)
            return jnp.where(low, kv0, kv1)

        for j in range(N_PAIRS):
            o_ref[:, j * LANES:(j + 1) * LANES] = (unrope(blk_t(dqa_ref, j)) * scale).astype(bf16)
            o_ref[:, (6 + j) * LANES:(7 + j) * LANES] = (blk_t(dqb_ref, j) * scale).astype(bf16)
            o_ref[:, (10 + j) * LANES:(11 + j) * LANES] = blk(dkb_ref, j).astype(bf16)
            o_ref[:, (14 + j) * LANES:(15 + j) * LANES] = blk(dvb_ref, j).astype(bf16)
        o_ref[:, 4 * LANES:5 * LANES] = unrope(fold(dka_ref)).astype(bf16)
        o_ref[:, 5 * LANES:6 * LANES] = fold(dva_ref).astype(bf16)

    return pl.pallas_call(
        body, name=name, grid=(s // tm,),
        in_specs=[t_spec, _row_spec(tm, hw), _row_spec(tm, hw), t_spec, _row_spec(tm, hw), _row_spec(tm, hw)] + [_row_spec(tm, LANES)] * 2,
        out_specs=_row_spec(tm, QKV_W), out_shape=jax.ShapeDtypeStruct((s, QKV_W), bf16), compiler_params=_params(1),
    )(dqa_t, dka, dva, dqb_t, dkb, dvb, cos, sin_s)


def _cumsum_rows(v, reverse=False):
    n = v.shape[0]
    row = lax.broadcasted_iota(jnp.int32, v.shape, 0)
    sh = 1
    while sh < n:
        if reverse:
            v = v + jnp.where(row < n - sh, pltpu.roll(v, n - sh, axis=0), 0.0)
        else:
            v = v + jnp.where(row >= sh, pltpu.roll(v, sh, axis=0), 0.0)
        sh *= 2
    return v


def _log_sigmoid(z):
    return jnp.minimum(z, 0.0) - jnp.log1p(jnp.exp(-jnp.abs(z)))


def _forget_prep(fl, bf_row, name):
    s = fl.shape[0]

    def body(f_ref, b_ref, cb_ref):
        cum = _cumsum_rows(_log_sigmoid(f_ref[...] + b_ref[...]))
        for h in range(N_HEADS):
            cb_ref[:, h * LANES:(h + 1) * LANES] = jnp.broadcast_to(cum[:, h:h + 1], (s, LANES))

    return pl.pallas_call(
        body, name=name, out_shape=jax.ShapeDtypeStruct((s, N_HEADS * LANES), f32), compiler_params=_params(),
    )(fl, bf_row)


def _forget_prep_bwd(rs, dcs, fl, bf_row, name):
    s = fl.shape[0]

    def body(r_ref, c_ref, f_ref, b_ref, df_ref, db_ref):
        eye = (lax.broadcasted_iota(jnp.int32, (N_HEADS, LANES), 0) == lax.broadcasted_iota(jnp.int32, (N_HEADS, LANES), 1)).astype(f32)
        dcum = lax.dot_general(r_ref[...], eye, _TN, precision=lax.Precision.HIGHEST, preferred_element_type=f32)
        for h in range(N_HEADS):
            dcum = dcum - jnp.where(_lane() == h, jnp.sum(c_ref[:, h * LANES:(h + 1) * LANES], axis=1, keepdims=True), 0.0)
        dlf = _cumsum_rows(dcum, reverse=True)
        z = f_ref[...] + b_ref[...]
        df = jnp.where(_lane() < N_HEADS, dlf * jax.nn.sigmoid(-z), 0.0)
        df_ref[...] = df.astype(bf16)
        db_ref[...] = jnp.zeros_like(db_ref)
        db_ref[0:1, :] = jnp.sum(df, axis=0, keepdims=True)

    return pl.pallas_call(
        body, name=name,
        out_shape=[jax.ShapeDtypeStruct((s, LANES), bf16), jax.ShapeDtypeStruct((8, LANES), f32)], compiler_params=_params(),
    )(rs, dcs, fl, bf_row)


def _tile_mask(n_keys, n_queries, off, window):
    shape = (n_keys, n_queries)
    d = lax.broadcasted_iota(jnp.int32, shape, 1) - lax.broadcasted_iota(jnp.int32, shape, 0) + off
    valid = d >= 0
    return jnp.logical_and(valid, d < window) if window else valid


def _wide(v, t):
    return jnp.concatenate([v] * (t // LANES), axis=1)


def _attn_fwd(q, k, v, name, *, cum_b=None, sink_rows=None, window=None, t=256):
    s = q.shape[0]
    t = _row_tile(s, t)
    fox, has_sink = cum_b is not None, sink_rows is not None
    assert not window or (window % LANES == 0 and LANES + window <= s)

    def body(*refs):
        q_ref, k_ref, v_ref = refs[:3]
        rest = list(refs[3:])
        cb_ref = rest.pop(0) if fox else None
        sink_ref = rest.pop(0) if has_sink else None
        o_ref, lse_ref = rest
        i = pl.program_id(1)
        low = _lane() < HEAD_DIM
        top = lax.broadcasted_iota(jnp.int32, (LANES, 1), 0) < HEAD_DIM
        q2 = q_ref[...]
        zero = jnp.zeros_like(q2)
        qms = (jnp.where(low, q2, zero), jnp.where(low, zero, q2))

        def tile(k0, n_keys, off, carry, masked, queries=slice(0, t)):
            nq = queries.stop - queries.start
            kblk, vblk = k_ref[pl.ds(k0, n_keys), :], v_ref[pl.ds(k0, n_keys), :]
            valid = _tile_mask(n_keys, nq, off, window) if masked else None
            def scores(h):
                return lax.dot_general(kblk, qms[h][queries], _NT, preferred_element_type=f32)

            def softmax(h, sc):
                m, l, _ = carry[h]
                if fox:
                    sc = sc - _wide(cb_ref[pl.ds(k0, n_keys), h * LANES:(h + 1) * LANES], nq)
                if masked:
                    sc = jnp.where(valid, sc, NEG)
                m_new = jnp.maximum(m, jnp.max(sc, axis=0, keepdims=True))
                p = jnp.exp(sc - m_new)
                alpha = jnp.exp(m - m_new)
                return m_new, alpha * l + jnp.sum(p, axis=0, keepdims=True), alpha, p.astype(bf16)

            def update(h, m_new, l, alpha, p):
                return m_new, l, alpha * carry[h][2] + lax.dot_general(vblk, p, _TN, preferred_element_type=f32)

            if window:
                return tuple(update(h, *softmax(h, scores(h))) for h in range(2))
            scs = [scores(h) for h in range(2)]
            stats = [softmax(h, scs[h]) for h in range(2)]
            return tuple(update(h, *stats[h]) for h in range(2))

        def start(nq):
            if has_sink:
                return tuple((_wide(sink_ref[h:h + 1, :], nq), jnp.ones((1, nq), f32), jnp.zeros((LANES, nq), f32))
                             for h in range(2))
            return tuple((jnp.full((1, nq), NEG, f32), jnp.zeros((1, nq), f32), jnp.zeros((LANES, nq), f32)) for h in range(2))

        def finish(carry, queries):
            (m0, l0, a0), (m1, l1, a1) = carry
            o_t = jnp.where(top, a0 * (1.0 / l0), a1 * (1.0 / l1))
            o_ref[queries, :] = o_t.T.astype(bf16)
            lse_ref[0:1, queries] = m0 + jnp.log(l0)
            lse_ref[1:2, queries] = m1 + jnp.log(l1)

        if window:
            for c in range(t // LANES):
                queries = slice(c * LANES, (c + 1) * LANES)
                q0 = i * t + c * LANES
                k0 = pl.multiple_of(jnp.maximum(q0 - window, 0), LANES)
                finish(tile(k0, LANES + window, q0 - k0, start(LANES), True, queries), queries)
        else:
            carry = lax.fori_loop(0, i, lambda kb, c: tile(pl.multiple_of(kb * t, t), t, 0, c, False), start(t))
            finish(tile(pl.multiple_of(i * t, t), t, 0, carry, True), slice(0, t))

    q_spec = pl.BlockSpec((t, LANES), lambda j, i: (i, j))
    kv_spec = pl.BlockSpec((s, LANES), lambda j, i: (0, j))
    in_specs, args = [q_spec, kv_spec, kv_spec], [q, k, v]
    if fox:
        in_specs += [pl.BlockSpec((s, 2 * LANES), lambda j, i: (0, j))]
        args += [cum_b]
    if has_sink:
        in_specs += [pl.BlockSpec((None, 2, LANES), lambda j, i: (j, 0, 0))]
        args += [sink_rows.reshape(N_PAIRS, 2, LANES)]
    return pl.pallas_call(
        body, name=name, grid=(N_PAIRS, s // t), in_specs=in_specs,
        out_specs=[q_spec, pl.BlockSpec((None, 2, t), lambda j, i: (j, 0, i))],
        out_shape=[jax.ShapeDtypeStruct((s, N_PAIRS * LANES), bf16), jax.ShapeDtypeStruct((N_PAIRS, 2, s), f32)],
        compiler_params=_params(2),
    )(*args)


def _attn_delta(do, o, name, *, lse=None, sink_rows=None):
    s, hw = do.shape
    tm = _row_tile(s, 512)
    has_sink = sink_rows is not None

    def body(*refs):
        do_ref, o_ref = refs[:2]
        if has_sink:
            lse_ref, sink_ref, dl_ref, ds_ref = refs[2:]

            @pl.when(pl.program_id(0) == 0)
            def _():
                ds_ref[...] = jnp.zeros_like(ds_ref)
        else:
            dl_ref, = refs[2:]
        for j in range(N_PAIRS):
            cols = slice(j * LANES, (j + 1) * LANES)
            prod_t = (do_ref[:, cols].astype(f32) * o_ref[:, cols].astype(f32)).T
            for h in range(2):
                dl = jnp.sum(prod_t[h * HEAD_DIM:(h + 1) * HEAD_DIM, :], axis=0, keepdims=True)
                dl_ref[j, h:h + 1, :] = dl
                if has_sink:
                    r = 2 * j + h
                    p_sink = jnp.exp(sink_ref[r:r + 1, 0:1] - lse_ref[j, h:h + 1, :])
                    ds_ref[r:r + 1, :] += -jnp.sum(p_sink * dl, axis=1, keepdims=True)

    rows_spec = pl.BlockSpec((N_PAIRS, 2, tm), lambda i: (0, 0, i))
    in_specs, args = [_row_spec(tm, hw)] * 2, [do, o]
    out_specs, out_shape = [rows_spec], [jax.ShapeDtypeStruct((N_PAIRS, 2, s), f32)]
    if has_sink:
        in_specs += [rows_spec, _vec_spec(LANES, N_HEADS)]
        args += [lse, sink_rows]
        out_specs += [_vec_spec(LANES, N_HEADS)]
        out_shape += [jax.ShapeDtypeStruct((N_HEADS, LANES), f32)]
    return pl.pallas_call(
        body, name=name, grid=(s // tm,), in_specs=in_specs, out_specs=out_specs, out_shape=out_shape,
        compiler_params=_params(1),
    )(*args)


def _attn_bwd(q, k, v, do, lse, delta, name, *, cum_b=None, window=None, t=256):
    s = q.shape[0]
    t = _row_tile(s, t)
    nblk = s // t
    fox = cum_b is not None
    assert not window or (window % LANES == 0 and LANES + window <= s)

    def body(*refs):
        k_ref, v_ref, q_ref, do_ref, lse_ref, dl_ref = refs[:6]
        rest = list(refs[6:])
        cb_ref = rest.pop(0) if fox else None
        dq_ref, dk_ref, dv_ref = rest[:3]
        dcs_ref, rs_ref = (rest[3], rest[4]) if fox else (None, None)
        b = pl.program_id(1)
        k0 = pl.multiple_of(b * t, t)

        @pl.when(b == 0)
        def _():
            dq_ref[...] = jnp.zeros_like(dq_ref)
            if fox:
                rs_ref[...] = jnp.zeros_like(rs_ref)

        dk_ref[...] = jnp.zeros_like(dk_ref)
        dv_ref[...] = jnp.zeros_like(dv_ref)
        if fox:
            dcs_ref[...] = jnp.zeros_like(dcs_ref)
        low = _lane() < HEAD_DIM
        top = lax.broadcasted_iota(jnp.int32, (LANES, 1), 0) < HEAD_DIM
        kblk, vblk = k_ref[...], v_ref[...]
        k_t = kblk.astype(f32).T.astype(bf16)
        cks = [_wide(cb_ref[pl.ds(k0, t), h * LANES:(h + 1) * LANES], t) for h in range(2)] if fox else None

        def tile(q0, n_queries, off, masked, keys=slice(0, t)):
            cols = pl.ds(q0, n_queries)
            q2, do2 = q_ref[cols, :], do_ref[cols, :]
            zero = jnp.zeros_like(q2)
            valid = _tile_mask(keys.stop - keys.start, n_queries, off, window) if masked else None
            dq_parts = []
            for h in range(2):
                qm = jnp.where(low, q2, zero) if h == 0 else jnp.where(low, zero, q2)
                dom = jnp.where(low, do2, zero) if h == 0 else jnp.where(low, zero, do2)
                sc = lax.dot_general(kblk[keys], qm, _NT, preferred_element_type=f32)
                if fox:
                    sc = sc - cks[h]
                if masked:
                    sc = jnp.where(valid, sc, NEG)
                p = jnp.exp(sc - lse_ref[h:h + 1, cols])
                dp = lax.dot_general(vblk[keys], dom, _NT, preferred_element_type=f32)
                ds = p * (dp - dl_ref[h:h + 1, cols])
                pb, dsb = p.astype(bf16), ds.astype(bf16)
                dv_ref[keys, :] += jnp.dot(pb, dom, preferred_element_type=f32)
                dk_ref[keys, :] += jnp.dot(dsb, qm, preferred_element_type=f32)
                dq_parts.append(jnp.dot(k_t[:, keys], dsb, preferred_element_type=f32))
                if fox:
                    dcs_ref[:, h * LANES:(h + 1) * LANES] += sum(ds[:, g * LANES:(g + 1) * LANES] for g in range(t // LANES))
                    rs_ref[h:h + 1, cols] += jnp.sum(ds, axis=0, keepdims=True)
            dq_ref[:, cols] += jnp.where(top, dq_parts[0], dq_parts[1])

        def later_block(qb, carry):
            tile(pl.multiple_of(qb * t, t), t, 0, False)
            return carry

        if window:
            for c in range(t // LANES):
                first = b * t + c * LANES
                q0 = pl.multiple_of(jnp.minimum(first, s - (LANES + window)), LANES)
                tile(q0, LANES + window, q0 - first, True, slice(c * LANES, (c + 1) * LANES))
        else:
            tile(k0, t, 0, True)
            lax.fori_loop(b + 1, nblk, later_block, 0)

    kv_spec = pl.BlockSpec((t, LANES), lambda j, b: (b, j))
    seq_spec = pl.BlockSpec((s, LANES), lambda j, b: (0, j))
    rows_spec = pl.BlockSpec((None, 2, s), lambda j, b: (j, 0, 0))
    hw = N_PAIRS * LANES
    in_specs, args = [kv_spec, kv_spec, seq_spec, seq_spec, rows_spec, rows_spec], [k, v, q, do, lse, delta]
    out_specs = [pl.BlockSpec((LANES, s), lambda j, b: (j, 0)), kv_spec, kv_spec]
    out_shape = [jax.ShapeDtypeStruct((hw, s), f32), jax.ShapeDtypeStruct((s, hw), f32), jax.ShapeDtypeStruct((s, hw), f32)]
    if fox:
        in_specs += [pl.BlockSpec((s, 2 * LANES), lambda j, b: (0, j))]
        args += [cum_b]
        out_specs += [pl.BlockSpec((t, 2 * LANES), lambda j, b: (b, j)), rows_spec]
        out_shape += [jax.ShapeDtypeStruct((s, N_HEADS * LANES), f32), jax.ShapeDtypeStruct((N_PAIRS, 2, s), f32)]
    return pl.pallas_call(
        body, name=name, grid=(N_PAIRS, nblk), in_specs=in_specs, out_specs=out_specs, out_shape=out_shape,
        compiler_params=_params(2),
    )(*args)


def _branch_merge(o_a, o_b, w_a, w_b, gl, name):
    s, k = o_a.shape
    d = w_a.shape[1]
    tm = _row_tile(s, 1024)

    def body(oa_ref, ob_ref, wa_ref, wb_ref, g_ref, ba_ref, bb_ref, m_ref):
        ba = jnp.dot(oa_ref[...], wa_ref[...], preferred_element_type=f32)
        bb = jnp.dot(ob_ref[...], wb_ref[...], preferred_element_type=f32)
        g0, g1 = jax.nn.sigmoid(g_ref[:, :d].astype(f32)), jax.nn.sigmoid(g_ref[:, d:].astype(f32))
        ba_ref[...] = ba.astype(bf16)
        bb_ref[...] = bb.astype(bf16)
        m_ref[...] = (g0 * ba + g1 * bb).astype(bf16)

    whole = pl.BlockSpec((k, d), lambda i: (0, 0))
    return pl.pallas_call(
        body, name=name, grid=(s // tm,),
        in_specs=[_row_spec(tm, k), _row_spec(tm, k), whole, whole, _row_spec(tm, 2 * d)],
        out_specs=[_row_spec(tm, d)] * 3, out_shape=[jax.ShapeDtypeStruct((s, d), bf16)] * 3, compiler_params=_params(1),
    )(o_a, o_b, w_a, w_b, gl)


def _out_dgrad_merge_bwd(dy, w_out, ba, bb, gl, name):
    s, d = ba.shape
    tm = _row_tile(s, 512)

    def body(dy_ref, w_ref, a_ref, b_ref, g_ref, da_ref, db_ref, dg_ref):
        dmv = lax.dot_general(dy_ref[...], w_ref[...], _NT, preferred_element_type=f32)
        g0, g1 = jax.nn.sigmoid(g_ref[:, :d].astype(f32)), jax.nn.sigmoid(g_ref[:, d:].astype(f32))
        da_ref[...] = (dmv * g0).astype(bf16)
        db_ref[...] = (dmv * g1).astype(bf16)
        dg_ref[:, :d] = (dmv * a_ref[...].astype(f32) * (g0 * (1.0 - g0))).astype(bf16)
        dg_ref[:, d:] = (dmv * b_ref[...].astype(f32) * (g1 * (1.0 - g1))).astype(bf16)

    return pl.pallas_call(
        body, name=name, grid=(s // tm,),
        in_specs=[_row_spec(tm, dy.shape[1]), pl.BlockSpec(w_out.shape, lambda i: (0, 0))] + [_row_spec(tm, d)] * 2
        + [_row_spec(tm, 2 * d)],
        out_specs=[_row_spec(tm, d)] * 2 + [_row_spec(tm, 2 * d)],
        out_shape=[jax.ShapeDtypeStruct((s, d), bf16)] * 2 + [jax.ShapeDtypeStruct((s, 2 * d), bf16)],
        compiler_params=_params(1),
    )(dy, w_out, ba, bb, gl)


GLU_TILE = 256


def _ffn_in_swiglu(h, w_t, name):
    s, d = h.shape
    f = w_t.shape[0] // 2
    tm = _row_tile(s, 2048)
    tg = GLU_TILE
    nb = f // tg

    def body(h_ref, wg_ref, wu_ref, g_ref, u_ref, act_ref):
        hv = h_ref[...]
        g = lax.dot_general(hv, wg_ref[...], _NT, preferred_element_type=f32)
        u = lax.dot_general(hv, wu_ref[...], _NT, preferred_element_type=f32)
        g_ref[...] = g.astype(bf16)
        u_ref[...] = u.astype(bf16)
        act_ref[...] = (g * jax.nn.sigmoid(g) * u).astype(bf16)

    col = pl.BlockSpec((tm, tg), lambda i, j: (i, j))
    return pl.pallas_call(
        body, name=name, grid=(s // tm, nb),
        in_specs=[pl.BlockSpec((tm, d), lambda i, j: (i, 0)), pl.BlockSpec((tg, d), lambda i, j: (j, 0)),
                  pl.BlockSpec((tg, d), lambda i, j: (j + nb, 0))],
        out_specs=[col] * 3, out_shape=[jax.ShapeDtypeStruct((s, f), bf16)] * 3, compiler_params=_params(2),
    )(h, w_t, w_t)


def _ffn_out_dgrad_swiglu(dy, w_out, g, u, name):
    s, d = dy.shape
    f = g.shape[1]
    tm = _row_tile(s, 2048)
    tg = GLU_TILE

    def body(dy_ref, w_ref, g_ref, u_ref, dg_ref, du_ref):
        dv = lax.dot_general(dy_ref[...], w_ref[...], _NT, preferred_element_type=f32)
        gv, uv = g_ref[...].astype(f32), u_ref[...].astype(f32)
        sg = jax.nn.sigmoid(gv)
        dg_ref[...] = (dv * uv * (sg * (1.0 + gv * (1.0 - sg)))).astype(bf16)
        du_ref[...] = (dv * (gv * sg)).astype(bf16)

    col = pl.BlockSpec((tm, tg), lambda i, j: (i, j))
    return pl.pallas_call(
        body, name=name, grid=(s // tm, f // tg),
        in_specs=[pl.BlockSpec((tm, d), lambda i, j: (i, 0)), pl.BlockSpec((tg, d), lambda i, j: (j, 0)), col, col],
        out_specs=[col] * 2, out_shape=[jax.ShapeDtypeStruct((s, f), bf16)] * 2, compiler_params=_params(2),
    )(dy, w_out, g, u)


def _wgrad_stack(parts, h, name):
    s, m = parts[0].shape
    d = h.shape[1]
    tm = 256
    nb = m // tm
    n = len(parts)

    def body(*refs):
        i = pl.program_id(0)
        for p in range(n):
            @pl.when(i // nb == p)
            def _(p=p):
                refs[n + 1][...] = lax.dot_general(refs[p][...], refs[n][...], _TN, preferred_element_type=f32).astype(bf16)

    a_specs = [pl.BlockSpec((s, tm), lambda i, p=p: (0, jnp.clip(i - p * nb, 0, nb - 1))) for p in range(n)]
    return pl.pallas_call(
        body, name=name, grid=(n * nb,), in_specs=a_specs + [pl.BlockSpec((s, d), lambda i: (0, 0))],
        out_specs=pl.BlockSpec((tm, d), lambda i: (i, 0)),
        out_shape=jax.ShapeDtypeStruct((n * m, d), bf16), compiler_params=_params(1),
    )(*parts, h)


def _ada_fwd(c_all, w, b, name):
    def body(c_ref, w_ref, b_ref, o_ref):
        o_ref[...] = jnp.dot(c_ref[...].astype(bf16), w_ref[...].astype(bf16), preferred_element_type=f32) + b_ref[...]

    return pl.pallas_call(
        body, name=name, out_shape=jax.ShapeDtypeStruct((c_all.shape[0], w.shape[1]), f32), compiler_params=_params(),
    )(c_all, w, b)


def _ada_wgrad(c_all, d_all, name):
    n, d = c_all.shape
    w = d_all.shape[1]

    def body(c_ref, d_ref, o_ref):
        eye = (lax.broadcasted_iota(jnp.int32, (n, n), 0) == lax.broadcasted_iota(jnp.int32, (n, n), 1)).astype(f32)
        ct = lax.dot_general(c_ref[...], eye, _TN, precision=lax.Precision.HIGHEST, preferred_element_type=f32)
        g = ct[:, 0:1] * d_ref[0:1, :]
        for bi in range(1, n):
            g = g + ct[:, bi:bi + 1] * d_ref[bi:bi + 1, :]
        o_ref[0] = g

    return pl.pallas_call(
        body, name=name, out_shape=jax.ShapeDtypeStruct((1, d, w), f32), compiler_params=_params(),
    )(c_all, d_all)


def _adamw(parts, w, m, v, name, mine=None):
    r, c = w.shape
    n_parts = parts.shape[0]
    row_tiles = [t for t in range(min(r, 256), 0, -1) if r % t == 0 and (t % 16 == 0 or t == r)]
    if row_tiles:
        tr, tc = row_tiles[0], c
    else:
        tr, tc = r, next(t for t in (256, LANES) if c % t == 0)

    def body(p_ref, *rest):
        own_ref = rest[0] if mine is not None else None
        w_ref, m_ref, v_ref, g_ref, d_ref, nm_ref, nv_ref = rest[-7:]
        if mine is not None:
            x, y, cc = _me()
            me = 4 * x + 2 * y + cc

        def part(i):
            if mine is None:
                return p_ref[i].astype(f32)
            return jnp.where(me == i, own_ref[i], p_ref[i]).astype(f32)

        g = part(0)
        for i in range(1, n_parts):
            g = g + part(i)
        mm = ADAM_B1 * m_ref[...] + (1.0 - ADAM_B1) * g
        vv = ADAM_B2 * v_ref[...] + (1.0 - ADAM_B2) * (g * g)
        m_hat = mm / (1.0 - ADAM_B1 ** ADAM_STEP)
        v_hat = vv / (1.0 - ADAM_B2 ** ADAM_STEP)
        g_ref[...] = g
        d_ref[...] = -ADAM_LR * (m_hat / (jnp.sqrt(v_hat) + ADAM_EPS) + ADAM_WD * w_ref[...])
        nm_ref[...] = mm
        nv_ref[...] = vv

    spec = pl.BlockSpec((tr, tc), lambda i, j: (i, j))
    stack = [parts] if mine is None else [parts, mine]
    return pl.pallas_call(
        body, name=name, grid=(r // tr, c // tc),
        in_specs=[pl.BlockSpec((n_parts, tr, tc), lambda i, j: (0, i, j))] * len(stack) + [spec] * 3,
        out_specs=[spec] * 4, out_shape=[jax.ShapeDtypeStruct((r, c), f32)] * 4, compiler_params=_params(2),
    )(*stack, w, m, v)


def _me():
    return lax.axis_index("x"), lax.axis_index("y"), lax.axis_index("c")


def _all_gather(arrays, name, vmem=False, after=None):
    n = len(arrays)
    space = pltpu.VMEM if vmem else pl.ANY
    extra = [] if after is None else [after]

    def body(*refs):
        ins = refs[:n]
        outs = refs[n + len(extra):2 * n + len(extra)]
        send_sems, recv_sems, local_sems = refs[2 * n + len(extra):]
        x, y, c = _me()
        me, sibling = (x, y, c), (x, y, 1 - c)
        chips = [(1 - x, y), (x, 1 - y), (1 - x, 1 - y)]

        def rows(a, dev):
            return outs[a].at[4 * dev[0] + 2 * dev[1] + dev[2]]

        def copy(a, k, block, to, src=None):
            return pltpu.make_async_remote_copy(
                src_ref=rows(a, block) if src is None else src, dst_ref=rows(a, block),
                send_sem=send_sems.at[a, k], recv_sem=recv_sems.at[a, k], device_id=to, device_id_type=MESH)

        mine = [pltpu.make_async_copy(ins[a], rows(a, me), local_sems.at[a]) for a in range(n)]
        for cp in mine:
            cp.start()
        first = []
        for a in range(n):
            first.append(copy(a, 0, me, sibling, src=ins[a]))
            first += [copy(a, 1 + j, me, (*chip, c), src=ins[a]) for j, chip in enumerate(chips)]
        for cp in first:
            cp.start()
        passed = []
        for j, chip in enumerate(chips):
            for a in range(n):
                copy(a, 1 + j, (*chip, c), me).wait_recv()
                fwd = copy(a, 4 + j, (*chip, c), sibling)
                fwd.start()
                passed.append(fwd)
        for a in range(n):
            copy(a, 0, sibling, me).wait_recv()
            for j, chip in enumerate(chips):
                copy(a, 4 + j, (*chip, 1 - c), me).wait_recv()
        for cp in first + passed:
            cp.wait_send()
        for cp in mine:
            cp.wait()

    outs = pl.pallas_call(
        body, name=name,
        in_specs=[pl.BlockSpec(memory_space=space)] * n + [pl.BlockSpec(memory_space=pl.ANY)] * len(extra),
        out_specs=[pl.BlockSpec(memory_space=space)] * n,
        out_shape=[jax.ShapeDtypeStruct((N_DEV,) + a.shape, a.dtype) for a in arrays],
        scratch_shapes=[pltpu.SemaphoreType.DMA((n, 7)), pltpu.SemaphoreType.DMA((n, 7)), pltpu.SemaphoreType.DMA((n,))],
        compiler_params=pltpu.CompilerParams(vmem_limit_bytes=VMEM_LIMIT),
    )(*arrays, *extra)
    return list(outs)


_FLIPS = ((0, 0, 1), (1, 0, 0), (0, 1, 0), (1, 1, 0), (1, 0, 1), (0, 1, 1), (1, 1, 1))
_HBM = pl.BlockSpec(memory_space=pltpu.HBM)
_SEM = pl.BlockSpec(memory_space=pltpu.SEMAPHORE)


def _exchange_copies(scatter, srcs, lands, send_sems, recv_sems):
    x, y, c = _me()
    me_row = 4 * x + 2 * y + c
    out = []
    for k, (fx, fy, fc) in enumerate(_FLIPS):
        peer = (x ^ fx, y ^ fy, c ^ fc)
        peer_row = 4 * peer[0] + 2 * peer[1] + peer[2]
        for a in range(len(srcs)):
            out.append(pltpu.make_async_remote_copy(
                src_ref=srcs[a].at[peer_row] if scatter else srcs[a], dst_ref=lands[a].at[me_row],
                send_sem=send_sems.at[7 * a + k], recv_sem=recv_sems.at[7 * a + k], device_id=peer, device_id_type=MESH))
    return out


def _exchange_start(arrays, scatter, name, after=None):
    n = len(arrays)
    lands = [lax.empty(a.shape if scatter else (N_DEV,) + a.shape, a.dtype) for a in arrays]
    extra = [] if after is None else [after]

    def body(*refs):
        srcs, zones = refs[:n], refs[n:2 * n]
        send_sems, recv_sems = refs[2 * n + len(extra)], refs[2 * n + len(extra) + 1]
        token = refs[-1]
        for cp in _exchange_copies(scatter, srcs, zones, send_sems, recv_sems):
            cp.start()
        token[...] = jnp.zeros_like(token)

    thru = [pltpu.HBM(a.shape, a.dtype) for a in list(arrays) + lands]
    outs = pl.pallas_call(
        body, name=name,
        out_shape=(pltpu.SemaphoreType.DMA((7 * n,)), pltpu.SemaphoreType.DMA((7 * n,)), *thru, jax.ShapeDtypeStruct((8, LANES), f32)),
        in_specs=[_HBM] * (2 * n) + [pl.BlockSpec(memory_space=pl.ANY)] * len(extra),
        out_specs=(_SEM, _SEM, *[_HBM] * (2 * n), pl.BlockSpec(memory_space=pltpu.VMEM)),
        input_output_aliases={i: 2 + i for i in range(2 * n)},
        compiler_params=pltpu.CompilerParams(has_side_effects=pltpu.SideEffectType.DATAFLOW_SIDE_EFFECTING),
    )(*[pltpu.with_memory_space_constraint(a, pltpu.HBM) for a in list(arrays) + lands], *extra)
    return dict(n=n, scatter=scatter, sems=outs[:2], srcs=outs[2:2 + n], lands=outs[2 + n:2 + 2 * n], token=outs[-1])


def _exchange_wait(handle, after, name):
    n, scatter = handle["n"], handle["scatter"]

    def body(*refs):
        srcs, zones = refs[:n], refs[n:2 * n]
        send_sems, recv_sems = refs[2 * n], refs[2 * n + 1]
        for cp in _exchange_copies(scatter, srcs, zones, send_sems, recv_sems):
            cp.wait_send()
            cp.wait_recv()

    thru = [pltpu.HBM(a.shape, a.dtype) for a in list(handle["srcs"]) + list(handle["lands"])]
    outs = pl.pallas_call(
        body, name=name, out_shape=tuple(thru),
        in_specs=[_HBM] * (2 * n) + [_SEM, _SEM, pl.BlockSpec(memory_space=pl.ANY)], out_specs=tuple([_HBM] * (2 * n)),
        input_output_aliases={i: i for i in range(2 * n)},
        compiler_params=pltpu.CompilerParams(has_side_effects=pltpu.SideEffectType.DATAFLOW_SIDE_EFFECTING),
    )(*handle["srcs"], *handle["lands"], *handle["sems"], after)
    return list(outs[n:])


def _cols_from_shards(g):
    return jnp.transpose(g, (1, 0, 2)).reshape(g.shape[1], -1)


def _shards_from_cols(a):
    return jnp.transpose(a.reshape(a.shape[0], N_DEV, -1), (1, 0, 2))


def _local_step(x, positions, ada, g_pre_mix, g_post_mix, b_f, sinks, g_pre_ffn, g_post_ffn, target,
                w_in_t, late_weights, on_grads):
    s, d = x.shape
    row = lambda v: v.reshape(1, -1)
    shift_m, scale_m, gate_m, shift_f, scale_f, gate_f = (ada[i:i + 1] for i in range(6))
    w_gate_t, w_qkv_t = w_in_t[F_OFF + N_HEADS:], w_in_t[:QKV_W]
    w_f_t = jnp.pad(w_in_t[F_OFF:F_OFF + N_HEADS], ((0, LANES - N_HEADS), (0, 0)))
    bf_row = jnp.pad(row(b_f), ((0, 0), (0, LANES - N_HEADS)))
    sink_rows = jnp.broadcast_to(sinks.reshape(N_HEADS, 1).astype(f32), (N_HEADS, LANES))
    inv_freq = 1.0 / (ROPE_THETA ** (jnp.arange(0, HEAD_DIM, 2, dtype=f32) / HEAD_DIM))
    cos, sin_s = _rope_tables(positions.reshape(s, 1), jnp.tile(inv_freq, 4).reshape(1, LANES), "rope_tables")

    h1 = _prenorm(x, row(g_pre_mix), scale_m, shift_m, "prenorm_mix")
    gl = _matmul(h1, w_gate_t, "nt", bf16, "proj_gate")
    qa, ka, va, qb, kb, vb = _proj_qkv(h1, w_qkv_t, cos, sin_s, "proj_qkv")
    fl = _matmul(h1, w_f_t, "nt", f32, "proj_forget")
    cum_b = _forget_prep(fl, bf_row, "forget_prep")
    o_a, lse_a = _attn_fwd(qa, ka, va, "swa_fwd", sink_rows=sink_rows, window=WINDOW, t=512)
    o_b, lse_b = _attn_fwd(qb, kb, vb, "fox_fwd", cum_b=cum_b, t=1024)
    w_branch_a, w_branch_b, w_out, w_ffn_in_t, w_ffn_out = late_weights(o_b)
    ba, bb, merged = _branch_merge(o_a, o_b, w_branch_a, w_branch_b, gl, "branch_merge")
    y1 = _matmul(merged, w_out, "nn", f32, "out_proj")

    x2, h2 = _postnorm_prenorm(x, y1, row(g_post_mix), gate_m, row(g_pre_ffn), scale_f, shift_f, "postnorm_mix_prenorm_ffn")
    g_ff, u_ff, act = _ffn_in_swiglu(h2, w_ffn_in_t, "ffn_in_swiglu")
    y2 = _matmul(act, w_ffn_out, "nn", f32, "ffn_out")
    loss_row, d_out, d_y2, vec_pf = _loss_tail(x2, y2, row(g_post_ffn), gate_f, target, "loss_tail")

    g_w_ffn_out = _matmul(act, d_y2, "tn", bf16, "ffn_out_wgrad")
    dg_ff, du_ff = _ffn_out_dgrad_swiglu(d_y2, w_ffn_out, g_ff, u_ff, "ffn_out_dgrad_swiglu")
    g_w_ffn_in_t = _wgrad_stack([dg_ff, du_ff], h2, "ffn_in_wgrad")
    sent = on_grads(dict(w_ffn_in=g_w_ffn_in_t, w_ffn_out=g_w_ffn_out))
    d_x2, vec_nf, d_y1, vec_pm = _dgrad_prenorm_bwd(
        [(dg_ff, w_ffn_in_t, 0), (du_ff, w_ffn_in_t, 1)], x2, row(g_pre_ffn), scale_f, d_out, "ffn_in_dgrad_norms_bwd",
        after=sent, below=(y1, row(g_post_mix), gate_m))

    g_w_out = _matmul(merged, d_y1, "tn", bf16, "out_proj_wgrad")
    d_ba, d_bb, dgl = _out_dgrad_merge_bwd(d_y1, w_out, ba, bb, gl, "out_proj_dgrad_merge_bwd")
    g_w_branch_a = _matmul(o_a, d_ba, "tn", bf16, "branch_a_wgrad")
    g_w_branch_b = _matmul(o_b, d_bb, "tn", bf16, "branch_b_wgrad")
    sent = on_grads(dict(w_out=g_w_out, w_branch_a=g_w_branch_a, w_branch_b=g_w_branch_b))
    d_oa = _matmul(d_ba, w_branch_a, "nt", bf16, "branch_a_dgrad", after=sent)
    d_ob = _matmul(d_bb, w_branch_b, "nt", bf16, "branch_b_dgrad", after=sent)
    delta_a, d_sink = _attn_delta(d_oa, o_a, "swa_delta", lse=lse_a, sink_rows=sink_rows)
    delta_b, = _attn_delta(d_ob, o_b, "fox_delta")
    dqa_t, dka, dva = _attn_bwd(qa, ka, va, d_oa, lse_a, delta_a, "swa_bwd", window=WINDOW, t=512)
    dqb_t, dkb, dvb, dcs, rs = _attn_bwd(qb, kb, vb, d_ob, lse_b, delta_b, "fox_bwd", cum_b=cum_b, t=512)
    dqkv = _qkv_prep_bwd(dqa_t, dka, dva, dqb_t, dkb, dvb, cos, sin_s, "qkv_prep_bwd")
    dfl, vec_bf = _forget_prep_bwd(rs.reshape(N_HEADS, s), dcs, fl, bf_row, "forget_prep_bwd")
    g_w_in_t = jnp.concatenate([_matmul(dqkv, h1, "tn", bf16, "qkv_wgrad"), _matmul(dfl, h1, "tn", bf16, "forget_wgrad")[:N_HEADS],
                                _matmul(dgl, h1, "tn", bf16, "gate_wgrad")], axis=0)
    sent = on_grads(dict(w_in=g_w_in_t))
    grad_x, vec_nm = _dgrad_prenorm_bwd([(dgl, w_gate_t, 0), (dqkv, w_qkv_t, 0), (dfl, w_f_t, 0)], x, row(g_pre_mix),
                                        scale_m, d_x2, "in_proj_dgrad_prenorm_bwd", after=sent)

    d_ada = jnp.concatenate([vec_nm[0], vec_nm[1], vec_pm[0], vec_nf[0], vec_nf[1], vec_pf[0]])
    small = dict(b_ada=d_ada, g_pre_mix=vec_nm[2], g_post_mix=vec_pm[1], g_pre_ffn=vec_nf[2], g_post_ffn=vec_pf[1],
                 b_f=vec_bf[0, :N_HEADS], sinks=d_sink[:, 0], loss=loss_row[0, :1])
    return grad_x, small


_SMALL = (("b_ada", 6144), ("g_pre_mix", 1024), ("g_post_mix", 1024), ("g_pre_ffn", 1024), ("g_post_ffn", 1024),
          ("b_f", 128), ("sinks", 128), ("loss", 128))
_SMALL_ROWS = 88


def _pack_small(vals):
    parts = [jnp.pad(vals[k].reshape(-1).astype(f32), (0, n - vals[k].size)) for k, n in _SMALL]
    flat = jnp.concatenate(parts)
    return jnp.pad(flat, (0, _SMALL_ROWS * LANES - flat.size)).reshape(_SMALL_ROWS, LANES)


def _unpack_small(slab, shapes):
    flat, out, off = slab.reshape(-1), {}, 0
    for k, n in _SMALL:
        size = math.prod(shapes[k])
        out[k] = flat[off:off + size].reshape(shapes[k])
        off += n
    return out


def kernel(x, c, positions, w_ada, b_ada, g_pre_mix, g_post_mix, w_in, b_f, sinks, w_branch_a, w_branch_b, w_out, g_pre_ffn, g_post_ffn, w_ffn_in, w_ffn_out, loss_target, m_w_ada, m_b_ada, m_g_pre_mix, m_g_post_mix, m_w_in, m_b_f, m_sinks, m_w_branch_a, m_w_branch_b, m_w_out, m_g_pre_ffn, m_g_post_ffn, m_w_ffn_in, m_w_ffn_out, v_w_ada, v_b_ada, v_g_pre_mix, v_g_post_mix, v_w_in, v_b_f, v_sinks, v_w_branch_a, v_w_branch_b, v_w_out, v_g_pre_ffn, v_g_post_ffn, v_w_ffn_in, v_w_ffn_out):
    xi, yi, ci = _me()
    me = 4 * xi + 2 * yi + ci
    d = D_MODEL
    ada_w = w_ada.shape[2]

    c_all, = _all_gather([c], "gather_c", vmem=True)
    c_all = c_all.reshape(N_DEV, d)
    b_mine = lax.dynamic_slice(b_ada, (0, me * ada_w), (1, ada_w))
    ada_cols = _ada_fwd(c_all, w_ada[0], b_mine, "ada_fwd")

    transposed = ("w_in", "w_ffn_in")
    tr = lambda a: jnp.transpose(a[0])

    ada_all, g_in = _all_gather([ada_cols, tr(w_in).astype(bf16)], "gather_ada_w_in")
    ada = lax.dynamic_index_in_dim(ada_all, me, axis=1, keepdims=False).reshape(6, d)
    late = [w.astype(bf16) for w in (w_branch_a[0], w_branch_b[0], w_out[0], tr(w_ffn_in), w_ffn_out[0])]
    late_h = _exchange_start(late, False, "gather_late_start", after=g_in)

    def mine_into(zone, block):
        return lax.dynamic_update_index_in_dim(zone, block, me, 0)

    def rows_from_shards(g):
        return g.reshape(g.shape[0] * g.shape[1], g.shape[2])

    def late_weights(after):
        zones = _exchange_wait(late_h, after, "gather_late_wait")
        g_ba, g_bb, g_out, g_fi, g_fo = (mine_into(z, w) for z, w in zip(zones, late))
        return (_cols_from_shards(g_ba), _cols_from_shards(g_bb), rows_from_shards(g_out), rows_from_shards(g_fi),
                rows_from_shards(g_fo))

    row_sharded = ("w_out", "w_ffn_out") + transposed
    in_flight = []

    def on_grads(group):
        sends = [g.reshape(N_DEV, g.shape[0] // N_DEV, g.shape[1]) if nm in row_sharded else _shards_from_cols(g)
                 for nm, g in group.items()]
        handle = _exchange_start(sends, True, "scatter_start_%d" % len(in_flight))
        in_flight.append((list(group), sends, handle))
        return handle["token"]

    grad_x, small = _local_step(
        x[0], positions[0], ada + late_h["token"][0, 0], g_pre_mix[0], g_post_mix[0], b_f[0], sinks[0], g_pre_ffn[0],
        g_post_ffn[0], loss_target[0], rows_from_shards(g_in), late_weights, on_grads)

    ws = dict(w_in=(w_in, m_w_in, v_w_in), w_branch_a=(w_branch_a, m_w_branch_a, v_w_branch_a),
              w_branch_b=(w_branch_b, m_w_branch_b, v_w_branch_b), w_out=(w_out, m_w_out, v_w_out),
              w_ffn_in=(w_ffn_in, m_w_ffn_in, v_w_ffn_in), w_ffn_out=(w_ffn_out, m_w_ffn_out, v_w_ffn_out))
    res = {}

    def finish_group(gi, after):
        names, sends, handle = in_flight[gi]
        zones = _exchange_wait(handle, after, "scatter_wait_%d" % gi)
        for nm, zone, sent in zip(names, zones, sends):
            w, m, v = (tr(a) if nm in transposed else a[0] for a in ws[nm])
            out = _adamw(zone, w, m, v, "adamw_" + nm, mine=sent)
            after = out[0]
            res[nm] = [jnp.transpose(o) for o in out] if nm in transposed else out
        return after

    done = finish_group(1, finish_group(0, grad_x))

    slab_all, = _all_gather([_pack_small(small)], "gather_small", vmem=True, after=done)
    small_w = dict(b_ada=b_ada, g_pre_mix=g_pre_mix, g_post_mix=g_post_mix, g_pre_ffn=g_pre_ffn, g_post_ffn=g_post_ffn,
                   b_f=b_f, sinks=sinks, loss=jnp.zeros((1,), f32))
    small_m = dict(b_ada=m_b_ada, g_pre_mix=m_g_pre_mix, g_post_mix=m_g_post_mix, g_pre_ffn=m_g_pre_ffn,
                   g_post_ffn=m_g_post_ffn, b_f=m_b_f, sinks=m_sinks, loss=jnp.zeros((1,), f32))
    small_v = dict(b_ada=v_b_ada, g_pre_mix=v_g_pre_mix, g_post_mix=v_g_post_mix, g_pre_ffn=v_g_pre_ffn,
                   g_post_ffn=v_g_post_ffn, b_f=v_b_f, sinks=v_sinks, loss=jnp.ones((1,), f32))
    shapes = {k: small_w[k].shape for k, _ in _SMALL}
    s_out = _adamw(slab_all, _pack_small(small_w), _pack_small(small_m), _pack_small(small_v), "adamw_small")
    s_grad, s_delta, s_m, s_v = (_unpack_small(o, shapes) for o in s_out)

    d_ada_all = lax.dynamic_slice(slab_all[:, :6144 // LANES, :].reshape(N_DEV, 6144), (0, me * ada_w), (N_DEV, ada_w))
    ada_parts = _ada_wgrad(c_all, d_ada_all, "ada_wgrad")

    res["w_ada"] = _adamw(ada_parts, w_ada[0], m_w_ada[0], v_w_ada[0], "adamw_w_ada")
    finish_group(2, res["w_ada"][0])

    order = ["w_ada", "b_ada", "g_pre_mix", "g_post_mix", "w_in", "b_f", "sinks", "w_branch_a", "w_branch_b", "w_out",
             "g_pre_ffn", "g_post_ffn", "w_ffn_in", "w_ffn_out"]
    outs = [s_grad["loss"].reshape(()), grad_x[None]]
    for which, small_o in enumerate((s_grad, s_delta, s_m, s_v)):
        for nm in order:
            outs.append(res[nm][which][None] if nm in res else small_o[nm])
    return tuple(outs)
```

```python
import functools
import math

import jax
import jax.numpy as jnp
from jax import lax
from jax.experimental import pallas as pl
from jax.experimental.pallas import tpu as pltpu

f32 = jnp.float32
bf16 = jnp.bfloat16

D_MODEL = 1024
HEAD_DIM = 64
N_HEADS = 8
N_PAIRS = 4
QKV_W = 2304
GATE_W = 2048
F_OFF = 2304
IN_W = 4360
WINDOW = 128
ROPE_THETA = 10000.0
RMS_EPS = 1e-6
D_FF = 2816
N_DEV = 8
ADAM_LR, ADAM_B1, ADAM_B2, ADAM_EPS, ADAM_WD, ADAM_STEP = 0.001, 0.9, 0.999, 1e-08, 0.01, 10
NEG = -1e30
LANES = 128
VMEM_LIMIT = 48 * 1024 * 1024
MESH = pl.DeviceIdType.MESH

_NT = (((1,), (1,)), ((), ()))
_TN = (((0,), (0,)), ((), ()))


def _params(n_grid=0):
    sem = ("arbitrary",) * n_grid if n_grid else None
    return pltpu.CompilerParams(dimension_semantics=sem, vmem_limit_bytes=VMEM_LIMIT)


def _row_tile(s, want):
    t = min(s, want)
    assert s % t == 0, (s, t)
    return t


MATMUL_VMEM_BUDGET = 40 * 1024 * 1024


def _matmul_tiles(m, n, k, a_item, b_item, o_item):
    def tiles(d):
        return [t for t in range(LANES, min(d, 2048) + 1, LANES) if d % t == 0] or [d]

    best = None
    for tm in tiles(m):
        for tn in tiles(n):
            vmem = 2 * (tm * k * a_item + tn * k * b_item + tm * tn * o_item) + tm * tn * 4
            if vmem > MATMUL_VMEM_BUDGET:
                continue
            traffic = m * k * a_item + n * k * b_item * (1 if tn == n else m // tm) + m * n * o_item
            steps = (m // tm) * (n // tn)
            key = (traffic, 0, steps) if steps >= 4 else (traffic, 1, -steps)
            if best is None or key < best[0]:
                best = (key, tm, tn)
    assert best is not None, (m, n, k)
    return best[1], best[2]


def _matmul(a, b, mode, out_dtype, name, after=None):
    if mode == "nn":
        (m, k), n = a.shape, b.shape[1]
    elif mode == "nt":
        (m, k), n = a.shape, b.shape[0]
    else:
        (k, m), n = a.shape, b.shape[1]
    tm, tn = _matmul_tiles(m, n, k, a.dtype.itemsize, b.dtype.itemsize, jnp.dtype(out_dtype).itemsize)
    if mode == "nn":
        a_spec, b_spec, dims = pl.BlockSpec((tm, k), lambda i, j: (i, 0)), pl.BlockSpec((k, tn), lambda i, j: (0, j)), None
    elif mode == "nt":
        a_spec, b_spec, dims = pl.BlockSpec((tm, k), lambda i, j: (i, 0)), pl.BlockSpec((tn, k), lambda i, j: (j, 0)), _NT
    else:
        a_spec, b_spec, dims = pl.BlockSpec((k, tm), lambda i, j: (0, i)), pl.BlockSpec((k, tn), lambda i, j: (0, j)), _TN

    def body(a_ref, b_ref, *rest):
        o_ref = rest[-1]
        av, bv = a_ref[...].astype(bf16), b_ref[...].astype(bf16)
        if dims is None:
            r = jnp.dot(av, bv, preferred_element_type=f32)
        else:
            r = lax.dot_general(av, bv, dims, preferred_element_type=f32)
        o_ref[...] = r.astype(out_dtype)

    extra = [] if after is None else [after]
    return pl.pallas_call(
        body, name=name, grid=(m // tm, n // tn), in_specs=[a_spec, b_spec] + [pl.BlockSpec(memory_space=pl.ANY)] * len(extra),
        out_specs=pl.BlockSpec((tm, tn), lambda i, j: (i, j)),
        out_shape=jax.ShapeDtypeStruct((m, n), out_dtype), compiler_params=_params(2),
    )(a, b, *extra)


def _rstd(v):
    return lax.rsqrt(jnp.mean(v * v, axis=-1, keepdims=True) + RMS_EPS)


def _row_spec(tm, d):
    return pl.BlockSpec((tm, d), lambda i: (i, 0))


def _vec_spec(d, rows=1):
    return pl.BlockSpec((rows, d), lambda i: (0, 0))


def _prenorm(x, g, scale, shift, name):
    s, d = x.shape
    tm = _row_tile(s, 512)

    def body(x_ref, g_ref, sc_ref, sh_ref, h_ref):
        xv = x_ref[...]
        h = (xv * _rstd(xv) * g_ref[...]) * (1.0 + sc_ref[...]) + sh_ref[...]
        h_ref[...] = h.astype(bf16)

    return pl.pallas_call(
        body, name=name, grid=(s // tm,), in_specs=[_row_spec(tm, d)] + [_vec_spec(d)] * 3,
        out_specs=_row_spec(tm, d), out_shape=jax.ShapeDtypeStruct((s, d), bf16), compiler_params=_params(1),
    )(x, g, scale, shift)


def _proj_spec(a, w, tm):
    return [_row_spec(tm, a.shape[1]), pl.BlockSpec(w.shape, lambda i: (0, 0))]


def _out_proj_postnorm_prenorm(a, w, x, g_post, gate, g_pre, scale, shift, name):
    s, d = x.shape
    tm = _row_tile(s, 512)

    def body(a_ref, w_ref, x_ref, gp_ref, gate_ref, g_ref, sc_ref, sh_ref, y_ref, x2_ref, h_ref):
        yv = jnp.dot(a_ref[...], w_ref[...], preferred_element_type=f32)
        y_ref[...] = yv
        x2 = x_ref[...] + gate_ref[...] * (yv * _rstd(yv) * gp_ref[...])
        x2_ref[...] = x2
        h_ref[...] = ((x2 * _rstd(x2) * g_ref[...]) * (1.0 + sc_ref[...]) + sh_ref[...]).astype(bf16)

    return pl.pallas_call(
        body, name=name, grid=(s // tm,), in_specs=_proj_spec(a, w, tm) + [_row_spec(tm, d)] + [_vec_spec(d)] * 5,
        out_specs=[_row_spec(tm, d)] * 3,
        out_shape=[jax.ShapeDtypeStruct((s, d), f32)] * 2 + [jax.ShapeDtypeStruct((s, d), bf16)], compiler_params=_params(1),
    )(a, w, x, g_post, gate, g_pre, scale, shift)


def _rms_bwd(u, v, r):
    return r * u - v * (r * r * r) * jnp.mean(u * v, axis=-1, keepdims=True)


def _out_proj_loss_tail(a, w, x, g, gate, target, name):
    s, d = x.shape
    tm = _row_tile(s, 512)

    def body(a_ref, w_ref, x_ref, g_ref, gate_ref, t_ref, loss_ref, do_ref, dy_ref, vec_ref):
        @pl.when(pl.program_id(0) == 0)
        def _():
            loss_ref[...] = jnp.zeros_like(loss_ref)
            vec_ref[...] = jnp.zeros_like(vec_ref)
        yv = jnp.dot(a_ref[...], w_ref[...], preferred_element_type=f32)
        r = _rstd(yv)
        yn = yv * r
        err = x_ref[...] + gate_ref[...] * (yn * g_ref[...]) - t_ref[...]
        loss_ref[...] += 0.5 * jnp.sum(jnp.mean(err * err, axis=-1, keepdims=True), axis=0, keepdims=True)
        dr = err / d
        do_ref[...] = dr
        dn = dr * gate_ref[...]
        vec_ref[0:1, :] += jnp.sum(dr * (yn * g_ref[...]), axis=0, keepdims=True)
        vec_ref[1:2, :] += jnp.sum(dn * yn, axis=0, keepdims=True)
        dy_ref[...] = _rms_bwd(dn * g_ref[...], yv, r).astype(bf16)

    return pl.pallas_call(
        body, name=name, grid=(s // tm,),
        in_specs=_proj_spec(a, w, tm) + [_row_spec(tm, d)] + [_vec_spec(d)] * 2 + [_row_spec(tm, d)],
        out_specs=[_vec_spec(LANES), _row_spec(tm, d), _row_spec(tm, d), _vec_spec(d, 8)],
        out_shape=[jax.ShapeDtypeStruct((1, LANES), f32), jax.ShapeDtypeStruct((s, d), f32),
                   jax.ShapeDtypeStruct((s, d), bf16), jax.ShapeDtypeStruct((8, d), f32)],
        compiler_params=_params(1),
    )(a, w, x, g, gate, target)


def _dgrad_prenorm_bwd(terms, x, g, scale, dres, name, after=None, below=None):
    s, d = x.shape
    n = len(terms)
    k = sum(a.shape[1] for a, _, _ in terms)
    row_bytes = 2 * (2 * k) + d * (4 + 2 * 4 * 3 + (2 * 4 + 2 * 2 if below else 0))
    tm = next(t for t in (512, 256, 128) if s % t == 0 and 4 * k * d + t * row_bytes <= MATMUL_VMEM_BUDGET)
    extra = [] if after is None else [after]

    def body(*refs):
        a_refs, b_refs = refs[:n], refs[n:2 * n]
        x_ref, g_ref, sc_ref, dr_ref = refs[2 * n:2 * n + 4]
        n_in = 2 * n + 4 + (3 if below else 0) + len(extra)
        dx_ref, vec_ref = refs[n_in], refs[n_in + 1]
        if below:
            y_ref, gp_ref, gate_ref = refs[2 * n + 4:2 * n + 7]
            dy_ref, vec2_ref = refs[n_in + 2], refs[n_in + 3]

        @pl.when(pl.program_id(0) == 0)
        def _():
            vec_ref[...] = jnp.zeros_like(vec_ref)
            if below:
                vec2_ref[...] = jnp.zeros_like(vec2_ref)
        dhv = jnp.dot(a_refs[0][...], b_refs[0][...], preferred_element_type=f32)
        for i in range(1, n):
            dhv = dhv + jnp.dot(a_refs[i][...], b_refs[i][...], preferred_element_type=f32)
        xv = x_ref[...]
        r = _rstd(xv)
        xn = xv * r
        dn = dhv * (1.0 + sc_ref[...])
        vec_ref[0:1, :] += jnp.sum(dhv, axis=0, keepdims=True)
        vec_ref[1:2, :] += jnp.sum(dhv * (xn * g_ref[...]), axis=0, keepdims=True)
        vec_ref[2:3, :] += jnp.sum(dn * xn, axis=0, keepdims=True)
        dx = dr_ref[...] + _rms_bwd(dn * g_ref[...], xv, r)
        dx_ref[...] = dx
        if below:
            yv = y_ref[...]
            ry = _rstd(yv)
            yn = yv * ry
            dny = dx * gate_ref[...]
            vec2_ref[0:1, :] += jnp.sum(dx * (yn * gp_ref[...]), axis=0, keepdims=True)
            vec2_ref[1:2, :] += jnp.sum(dny * yn, axis=0, keepdims=True)
            dy_ref[...] = _rms_bwd(dny * gp_ref[...], yv, ry).astype(bf16)

    in_specs = ([_row_spec(tm, a.shape[1]) for a, _, _ in terms]
                + [pl.BlockSpec((a.shape[1], d), lambda i, r=r: (r, 0)) for a, _, r in terms]
                + [_row_spec(tm, d)] + [_vec_spec(d)] * 2 + [_row_spec(tm, d)])
    out_specs = [_row_spec(tm, d), _vec_spec(d, 8)]
    out_shape = [jax.ShapeDtypeStruct((s, d), f32), jax.ShapeDtypeStruct((8, d), f32)]
    args = [a for a, _, _ in terms] + [b for _, b, _ in terms] + [x, g, scale, dres]
    if below:
        in_specs += [_row_spec(tm, d)] + [_vec_spec(d)] * 2
        out_specs += [_row_spec(tm, d), _vec_spec(d, 8)]
        out_shape += [jax.ShapeDtypeStruct((s, d), bf16), jax.ShapeDtypeStruct((8, d), f32)]
        args += list(below)
    return pl.pallas_call(
        body, name=name, grid=(s // tm,), in_specs=in_specs + [pl.BlockSpec(memory_space=pl.ANY)] * len(extra),
        out_specs=out_specs, out_shape=out_shape, compiler_params=_params(1),
    )(*args, *extra)


def _lane():
    return lax.broadcasted_iota(jnp.int32, (1, LANES), 1)


def _rope_tables(pos_col, inv_freq, name):
    s = pos_col.shape[0]

    def body(p_ref, f_ref, cos_ref, sin_ref):
        ang = p_ref[...].astype(f32) * f_ref[...]
        first_half = (_lane() % HEAD_DIM) < HEAD_DIM // 2
        cos_ref[...] = jnp.cos(ang)
        sn = jnp.sin(ang)
        sin_ref[...] = jnp.where(first_half, -sn, sn)

    return pl.pallas_call(
        body, name=name, out_shape=[jax.ShapeDtypeStruct((s, LANES), f32)] * 2, compiler_params=_params(),
    )(pos_col, inv_freq)


def _swap_halves(v):
    first_half = (_lane() % HEAD_DIM) < HEAD_DIM // 2
    return jnp.where(first_half, pltpu.roll(v, LANES - HEAD_DIM // 2, axis=1), pltpu.roll(v, HEAD_DIM // 2, axis=1))


def _proj_qkv(h, w_qkv_t, cos, sin_s, name):
    s, d = h.shape
    tm = _row_tile(s, 512)
    scale = 1.0 / math.sqrt(HEAD_DIM)

    def body(h_ref, w_ref, c_ref, s_ref, qa_ref, ka_ref, va_ref, qb_ref, kb_ref, vb_ref):
        proj = lax.dot_general(h_ref[...], w_ref[...], _NT, preferred_element_type=f32)
        cs, sn = c_ref[...], s_ref[...]
        low = _lane() < HEAD_DIM

        def blk(j):
            return proj[:, j * LANES:(j + 1) * LANES]

        def rope(v):
            return v * cs + _swap_halves(v) * sn

        def expand(v):
            other = pltpu.roll(v, HEAD_DIM, axis=1)
            return jnp.where(low, v, other), jnp.where(low, other, v)

        for j in range(N_PAIRS):
            qa_ref[:, j * LANES:(j + 1) * LANES] = (rope(blk(j)) * scale).astype(bf16)
            qb_ref[:, j * LANES:(j + 1) * LANES] = (blk(6 + j) * scale).astype(bf16)
            kb_ref[:, j * LANES:(j + 1) * LANES] = blk(10 + j).astype(bf16)
            vb_ref[:, j * LANES:(j + 1) * LANES] = blk(14 + j).astype(bf16)
        k0, k1 = expand(rope(blk(4)))
        v0, v1 = expand(blk(5))
        for j in range(N_PAIRS):
            ka_ref[:, j * LANES:(j + 1) * LANES] = (k0 if j < 2 else k1).astype(bf16)
            va_ref[:, j * LANES:(j + 1) * LANES] = (v0 if j < 2 else v1).astype(bf16)

    hw = N_PAIRS * LANES
    return pl.pallas_call(
        body, name=name, grid=(s // tm,),
        in_specs=[_row_spec(tm, d), pl.BlockSpec((QKV_W, d), lambda i: (0, 0)), _row_spec(tm, LANES), _row_spec(tm, LANES)],
        out_specs=[_row_spec(tm, hw)] * 6, out_shape=[jax.ShapeDtypeStruct((s, hw), bf16)] * 6, compiler_params=_params(1),
    )(h, w_qkv_t, cos, sin_s)


def _qkv_prep_bwd(dqa_t, dka, dva, dqb_t, dkb, dvb, cos, sin_s, name):
    s = dka.shape[0]
    tm = _row_tile(s, 256)
    scale = 1.0 / math.sqrt(HEAD_DIM)
    hw = N_PAIRS * LANES
    t_spec = pl.BlockSpec((hw, tm), lambda i: (0, i))

    def body(dqa_ref, dka_ref, dva_ref, dqb_ref, dkb_ref, dvb_ref, c_ref, s_ref, o_ref):
        cs, sn = c_ref[...], s_ref[...]
        low = _lane() < HEAD_DIM

        def blk(ref, j):
            return ref[:, j * LANES:(j + 1) * LANES]

        def blk_t(ref, j):
            return ref[j * LANES:(j + 1) * LANES, :].T

        def unrope(v):
            return v * cs + _swap_halves(v * sn)

        def fold(ref):
            a, b = blk(ref, 0) + blk(ref, 1), blk(ref, 2) + blk(ref, 3)
            kv0 = a + pltpu.roll(a, HEAD_DIM, axis=1)
            kv1 = b + pltpu.roll(b, HEAD_DIM, axis=1)
            return jnp.where(low, kv0, kv1)

        for j in range(N_PAIRS):
            o_ref[:, j * LANES:(j + 1) * LANES] = (unrope(blk_t(dqa_ref, j)) * scale).astype(bf16)
            o_ref[:, (6 + j) * LANES:(7 + j) * LANES] = (blk_t(dqb_ref, j) * scale).astype(bf16)
            o_ref[:, (10 + j) * LANES:(11 + j) * LANES] = blk(dkb_ref, j).astype(bf16)
            o_ref[:, (14 + j) * LANES:(15 + j) * LANES] = blk(dvb_ref, j).astype(bf16)
        o_ref[:, 4 * LANES:5 * LANES] = unrope(fold(dka_ref)).astype(bf16)
        o_ref[:, 5 * LANES:6 * LANES] = fold(dva_ref).astype(bf16)

    return pl.pallas_call(
        body, name=name, grid=(s // tm,),
        in_specs=[t_spec, _row_spec(tm, hw), _row_spec(tm, hw), t_spec, _row_spec(tm, hw), _row_spec(tm, hw)] + [_row_spec(tm, LANES)] * 2,
        out_specs=_row_spec(tm, QKV_W), out_shape=jax.ShapeDtypeStruct((s, QKV_W), bf16), compiler_params=_params(1),
    )(dqa_t, dka, dva, dqb_t, dkb, dvb, cos, sin_s)


def _cumsum_rows(v, reverse=False):
    n = v.shape[0]
    row = lax.broadcasted_iota(jnp.int32, v.shape, 0)
    sh = 1
    while sh < n:
        if reverse:
            v = v + jnp.where(row < n - sh, pltpu.roll(v, n - sh, axis=0), 0.0)
        else:
            v = v + jnp.where(row >= sh, pltpu.roll(v, sh, axis=0), 0.0)
        sh *= 2
    return v


def _log_sigmoid(z):
    return jnp.minimum(z, 0.0) - jnp.log1p(jnp.exp(-jnp.abs(z)))


def _forget_prep(fl, bf_row, name):
    s = fl.shape[0]

    def body(f_ref, b_ref, cb_ref):
        cum = _cumsum_rows(_log_sigmoid(f_ref[...] + b_ref[...]))
        for h in range(N_HEADS):
            cb_ref[:, h * LANES:(h + 1) * LANES] = jnp.broadcast_to(cum[:, h:h + 1], (s, LANES))

    return pl.pallas_call(
        body, name=name, out_shape=jax.ShapeDtypeStruct((s, N_HEADS * LANES), f32), compiler_params=_params(),
    )(fl, bf_row)


def _forget_prep_bwd(rs, dcs, fl, bf_row, name):
    s = fl.shape[0]

    def body(r_ref, c_ref, f_ref, b_ref, df_ref, db_ref):
        eye = (lax.broadcasted_iota(jnp.int32, (N_HEADS, LANES), 0) == lax.broadcasted_iota(jnp.int32, (N_HEADS, LANES), 1)).astype(f32)
        dcum = lax.dot_general(r_ref[...], eye, _TN, precision=lax.Precision.HIGHEST, preferred_element_type=f32)
        for h in range(N_HEADS):
            dcum = dcum - jnp.where(_lane() == h, jnp.sum(c_ref[:, h * LANES:(h + 1) * LANES], axis=1, keepdims=True), 0.0)
        dlf = _cumsum_rows(dcum, reverse=True)
        z = f_ref[...] + b_ref[...]
        df = jnp.where(_lane() < N_HEADS, dlf * jax.nn.sigmoid(-z), 0.0)
        df_ref[...] = df.astype(bf16)
        db_ref[...] = jnp.zeros_like(db_ref)
        db_ref[0:1, :] = jnp.sum(df, axis=0, keepdims=True)

    return pl.pallas_call(
        body, name=name,
        out_shape=[jax.ShapeDtypeStruct((s, LANES), bf16), jax.ShapeDtypeStruct((8, LANES), f32)], compiler_params=_params(),
    )(rs, dcs, fl, bf_row)


def _tile_mask(n_keys, n_queries, off, window):
    shape = (n_keys, n_queries)
    d = lax.broadcasted_iota(jnp.int32, shape, 1) - lax.broadcasted_iota(jnp.int32, shape, 0) + off
    valid = d >= 0
    return jnp.logical_and(valid, d < window) if window else valid


def _wide(v, t):
    return jnp.concatenate([v] * (t // LANES), axis=1)


def _attn_fwd(q, k, v, name, *, cum_b=None, sink_rows=None, window=None, t=256):
    s = q.shape[0]
    t = _row_tile(s, t)
    fox, has_sink = cum_b is not None, sink_rows is not None
    assert not window or (window % LANES == 0 and LANES + window <= s)

    def body(*refs):
        q_ref, k_ref, v_ref = refs[:3]
        rest = list(refs[3:])
        cb_ref = rest.pop(0) if fox else None
        sink_ref = rest.pop(0) if has_sink else None
        o_ref, lse_ref = rest
        i = pl.program_id(1)
        low = _lane() < HEAD_DIM
        top = lax.broadcasted_iota(jnp.int32, (LANES, 1), 0) < HEAD_DIM
        q2 = q_ref[...]
        zero = jnp.zeros_like(q2)
        qms = (jnp.where(low, q2, zero), jnp.where(low, zero, q2))

        def tile(k0, n_keys, off, carry, masked, queries=slice(0, t)):
            nq = queries.stop - queries.start
            kblk, vblk = k_ref[pl.ds(k0, n_keys), :], v_ref[pl.ds(k0, n_keys), :]
            valid = _tile_mask(n_keys, nq, off, window) if masked else None
            def scores(h):
                return lax.dot_general(kblk, qms[h][queries], _NT, preferred_element_type=f32)

            def softmax(h, sc):
                m, l, _ = carry[h]
                if fox:
                    sc = sc - _wide(cb_ref[pl.ds(k0, n_keys), h * LANES:(h + 1) * LANES], nq)
                if masked:
                    sc = jnp.where(valid, sc, NEG)
                m_new = jnp.maximum(m, jnp.max(sc, axis=0, keepdims=True))
                p = jnp.exp(sc - m_new)
                alpha = jnp.exp(m - m_new)
                return m_new, alpha * l + jnp.sum(p, axis=0, keepdims=True), alpha, p.astype(bf16)

            def update(h, m_new, l, alpha, p):
                return m_new, l, alpha * carry[h][2] + lax.dot_general(vblk, p, _TN, preferred_element_type=f32)

            if window:
                return tuple(update(h, *softmax(h, scores(h))) for h in range(2))
            scs = [scores(h) for h in range(2)]
            stats = [softmax(h, scs[h]) for h in range(2)]
            return tuple(update(h, *stats[h]) for h in range(2))

        def start(nq):
            if has_sink:
                return tuple((_wide(sink_ref[h:h + 1, :], nq), jnp.ones((1, nq), f32), jnp.zeros((LANES, nq), f32))
                             for h in range(2))
            return tuple((jnp.full((1, nq), NEG, f32), jnp.zeros((1, nq), f32), jnp.zeros((LANES, nq), f32)) for h in range(2))

        def finish(carry, queries):
            (m0, l0, a0), (m1, l1, a1) = carry
            o_t = jnp.where(top, a0 * (1.0 / l0), a1 * (1.0 / l1))
            o_ref[queries, :] = o_t.T.astype(bf16)
            lse_ref[0:1, queries] = m0 + jnp.log(l0)
            lse_ref[1:2, queries] = m1 + jnp.log(l1)

        if window:
            for c in range(t // LANES):
                queries = slice(c * LANES, (c + 1) * LANES)
                q0 = i * t + c * LANES
                k0 = pl.multiple_of(jnp.maximum(q0 - window, 0), LANES)
                finish(tile(k0, LANES + window, q0 - k0, start(LANES), True, queries), queries)
        else:
            carry = lax.fori_loop(0, i, lambda kb, c: tile(pl.multiple_of(kb * t, t), t, 0, c, False), start(t))
            finish(tile(pl.multiple_of(i * t, t), t, 0, carry, True), slice(0, t))

    q_spec = pl.BlockSpec((t, LANES), lambda j, i: (i, j))
    kv_spec = pl.BlockSpec((s, LANES), lambda j, i: (0, j))
    in_specs, args = [q_spec, kv_spec, kv_spec], [q, k, v]
    if fox:
        in_specs += [pl.BlockSpec((s, 2 * LANES), lambda j, i: (0, j))]
        args += [cum_b]
    if has_sink:
        in_specs += [pl.BlockSpec((None, 2, LANES), lambda j, i: (j, 0, 0))]
        args += [sink_rows.reshape(N_PAIRS, 2, LANES)]
    return pl.pallas_call(
        body, name=name, grid=(N_PAIRS, s // t), in_specs=in_specs,
        out_specs=[q_spec, pl.BlockSpec((None, 2, t), lambda j, i: (j, 0, i))],
        out_shape=[jax.ShapeDtypeStruct((s, N_PAIRS * LANES), bf16), jax.ShapeDtypeStruct((N_PAIRS, 2, s), f32)],
        compiler_params=_params(2),
    )(*args)


def _branch_dgrad_delta(db, w, o, name, *, lse=None, sink_rows=None, after=None):
    s, hw = o.shape
    tm = _row_tile(s, 512)
    has_sink = sink_rows is not None
    extra = [] if after is None else [after]

    def body(*refs):
        db_ref, w_ref, o_ref = refs[:3]
        outs = refs[3 + (2 if has_sink else 0) + len(extra):]
        do_ref, dl_ref = outs[:2]
        if has_sink:
            lse_ref, sink_ref = refs[3:5]
            ds_ref = outs[2]

            @pl.when(pl.program_id(0) == 0)
            def _():
                ds_ref[...] = jnp.zeros_like(ds_ref)
        do = lax.dot_general(db_ref[...], w_ref[...], _NT, preferred_element_type=f32).astype(bf16)
        do_ref[...] = do
        for j in range(N_PAIRS):
            cols = slice(j * LANES, (j + 1) * LANES)
            prod_t = (do[:, cols].astype(f32) * o_ref[:, cols].astype(f32)).T
            for h in range(2):
                dl = jnp.sum(prod_t[h * HEAD_DIM:(h + 1) * HEAD_DIM, :], axis=0, keepdims=True)
                dl_ref[j, h:h + 1, :] = dl
                if has_sink:
                    r = 2 * j + h
                    p_sink = jnp.exp(sink_ref[r:r + 1, 0:1] - lse_ref[j, h:h + 1, :])
                    ds_ref[r:r + 1, :] += -jnp.sum(p_sink * dl, axis=1, keepdims=True)

    rows_spec = pl.BlockSpec((N_PAIRS, 2, tm), lambda i: (0, 0, i))
    in_specs = [_row_spec(tm, db.shape[1]), pl.BlockSpec(w.shape, lambda i: (0, 0)), _row_spec(tm, hw)]
    args = [db, w, o]
    out_specs = [_row_spec(tm, hw), rows_spec]
    out_shape = [jax.ShapeDtypeStruct((s, hw), bf16), jax.ShapeDtypeStruct((N_PAIRS, 2, s), f32)]
    if has_sink:
        in_specs += [rows_spec, _vec_spec(LANES, N_HEADS)]
        args += [lse, sink_rows]
        out_specs += [_vec_spec(LANES, N_HEADS)]
        out_shape += [jax.ShapeDtypeStruct((N_HEADS, LANES), f32)]
    return pl.pallas_call(
        body, name=name, grid=(s // tm,), in_specs=in_specs + [pl.BlockSpec(memory_space=pl.ANY)] * len(extra),
        out_specs=out_specs, out_shape=out_shape, compiler_params=_params(1),
    )(*args, *extra)


def _attn_bwd(q, k, v, do, lse, delta, name, *, cum_b=None, window=None, t=256):
    s = q.shape[0]
    t = _row_tile(s, t)
    nblk = s // t
    fox = cum_b is not None
    assert not window or (window % LANES == 0 and LANES + window <= s)

    def body(*refs):
        k_ref, v_ref, q_ref, do_ref, lse_ref, dl_ref = refs[:6]
        rest = list(refs[6:])
        cb_ref = rest.pop(0) if fox else None
        dq_ref, dk_ref, dv_ref = rest[:3]
        dcs_ref, rs_ref = (rest[3], rest[4]) if fox else (None, None)
        b = pl.program_id(1)
        k0 = pl.multiple_of(b * t, t)

        @pl.when(b == 0)
        def _():
            dq_ref[...] = jnp.zeros_like(dq_ref)
            if fox:
                rs_ref[...] = jnp.zeros_like(rs_ref)

        dk_ref[...] = jnp.zeros_like(dk_ref)
        dv_ref[...] = jnp.zeros_like(dv_ref)
        if fox:
            dcs_ref[...] = jnp.zeros_like(dcs_ref)
        low = _lane() < HEAD_DIM
        top = lax.broadcasted_iota(jnp.int32, (LANES, 1), 0) < HEAD_DIM
        kblk, vblk = k_ref[...], v_ref[...]
        k_t = kblk.astype(f32).T.astype(bf16)
        cks = [_wide(cb_ref[pl.ds(k0, t), h * LANES:(h + 1) * LANES], t) for h in range(2)] if fox else None

        def tile(q0, n_queries, off, masked, keys=slice(0, t)):
            cols = pl.ds(q0, n_queries)
            q2, do2 = q_ref[cols, :], do_ref[cols, :]
            zero = jnp.zeros_like(q2)
            valid = _tile_mask(keys.stop - keys.start, n_queries, off, window) if masked else None
            dq_parts = []
            for h in range(2):
                qm = jnp.where(low, q2, zero) if h == 0 else jnp.where(low, zero, q2)
                dom = jnp.where(low, do2, zero) if h == 0 else jnp.where(low, zero, do2)
                sc = lax.dot_general(kblk[keys], qm, _NT, preferred_element_type=f32)
                if fox:
                    sc = sc - cks[h]
                if masked:
                    sc = jnp.where(valid, sc, NEG)
                p = jnp.exp(sc - lse_ref[h:h + 1, cols])
                dp = lax.dot_general(vblk[keys], dom, _NT, preferred_element_type=f32)
                ds = p * (dp - dl_ref[h:h + 1, cols])
                pb, dsb = p.astype(bf16), ds.astype(bf16)
                dv_ref[keys, :] += jnp.dot(pb, dom, preferred_element_type=f32)
                dk_ref[keys, :] += jnp.dot(dsb, qm, preferred_element_type=f32)
                dq_parts.append(jnp.dot(k_t[:, keys], dsb, preferred_element_type=f32))
                if fox:
                    dcs_ref[:, h * LANES:(h + 1) * LANES] += sum(ds[:, g * LANES:(g + 1) * LANES] for g in range(t // LANES))
                    rs_ref[h:h + 1, cols] += jnp.sum(ds, axis=0, keepdims=True)
            dq_ref[:, cols] += jnp.where(top, dq_parts[0], dq_parts[1])

        def later_block(qb, carry):
            tile(pl.multiple_of(qb * t, t), t, 0, False)
            return carry

        if window:
            for c in range(t // LANES):
                first = b * t + c * LANES
                q0 = pl.multiple_of(jnp.minimum(first, s - (LANES + window)), LANES)
                tile(q0, LANES + window, q0 - first, True, slice(c * LANES, (c + 1) * LANES))
        else:
            tile(k0, t, 0, True)
            lax.fori_loop(b + 1, nblk, later_block, 0)

    kv_spec = pl.BlockSpec((t, LANES), lambda j, b: (b, j))
    seq_spec = pl.BlockSpec((s, LANES), lambda j, b: (0, j))
    rows_spec = pl.BlockSpec((None, 2, s), lambda j, b: (j, 0, 0))
    hw = N_PAIRS * LANES
    in_specs, args = [kv_spec, kv_spec, seq_spec, seq_spec, rows_spec, rows_spec], [k, v, q, do, lse, delta]
    out_specs = [pl.BlockSpec((LANES, s), lambda j, b: (j, 0)), kv_spec, kv_spec]
    out_shape = [jax.ShapeDtypeStruct((hw, s), f32), jax.ShapeDtypeStruct((s, hw), f32), jax.ShapeDtypeStruct((s, hw), f32)]
    if fox:
        in_specs += [pl.BlockSpec((s, 2 * LANES), lambda j, b: (0, j))]
        args += [cum_b]
        out_specs += [pl.BlockSpec((t, 2 * LANES), lambda j, b: (b, j)), rows_spec]
        out_shape += [jax.ShapeDtypeStruct((s, N_HEADS * LANES), f32), jax.ShapeDtypeStruct((N_PAIRS, 2, s), f32)]
    return pl.pallas_call(
        body, name=name, grid=(N_PAIRS, nblk), in_specs=in_specs, out_specs=out_specs, out_shape=out_shape,
        compiler_params=_params(2),
    )(*args)


def _branch_merge(o_a, o_b, w_a, w_b, gl, name):
    s, k = o_a.shape
    d = w_a.shape[1]
    tm = _row_tile(s, 1024)

    def body(oa_ref, ob_ref, wa_ref, wb_ref, g_ref, ba_ref, bb_ref, m_ref):
        ba = jnp.dot(oa_ref[...], wa_ref[...], preferred_element_type=f32)
        bb = jnp.dot(ob_ref[...], wb_ref[...], preferred_element_type=f32)
        g0, g1 = jax.nn.sigmoid(g_ref[:, :d].astype(f32)), jax.nn.sigmoid(g_ref[:, d:].astype(f32))
        ba_ref[...] = ba.astype(bf16)
        bb_ref[...] = bb.astype(bf16)
        m_ref[...] = (g0 * ba + g1 * bb).astype(bf16)

    whole = pl.BlockSpec((k, d), lambda i: (0, 0))
    return pl.pallas_call(
        body, name=name, grid=(s // tm,),
        in_specs=[_row_spec(tm, k), _row_spec(tm, k), whole, whole, _row_spec(tm, 2 * d)],
        out_specs=[_row_spec(tm, d)] * 3, out_shape=[jax.ShapeDtypeStruct((s, d), bf16)] * 3, compiler_params=_params(1),
    )(o_a, o_b, w_a, w_b, gl)


def _out_dgrad_merge_bwd(dy, w_out, ba, bb, gl, name):
    s, d = ba.shape
    tm = _row_tile(s, 512)

    def body(dy_ref, w_ref, a_ref, b_ref, g_ref, da_ref, db_ref, dg_ref):
        dmv = lax.dot_general(dy_ref[...], w_ref[...], _NT, preferred_element_type=f32)
        g0, g1 = jax.nn.sigmoid(g_ref[:, :d].astype(f32)), jax.nn.sigmoid(g_ref[:, d:].astype(f32))
        da_ref[...] = (dmv * g0).astype(bf16)
        db_ref[...] = (dmv * g1).astype(bf16)
        dg_ref[:, :d] = (dmv * a_ref[...].astype(f32) * (g0 * (1.0 - g0))).astype(bf16)
        dg_ref[:, d:] = (dmv * b_ref[...].astype(f32) * (g1 * (1.0 - g1))).astype(bf16)

    return pl.pallas_call(
        body, name=name, grid=(s // tm,),
        in_specs=[_row_spec(tm, dy.shape[1]), pl.BlockSpec(w_out.shape, lambda i: (0, 0))] + [_row_spec(tm, d)] * 2
        + [_row_spec(tm, 2 * d)],
        out_specs=[_row_spec(tm, d)] * 2 + [_row_spec(tm, 2 * d)],
        out_shape=[jax.ShapeDtypeStruct((s, d), bf16)] * 2 + [jax.ShapeDtypeStruct((s, 2 * d), bf16)],
        compiler_params=_params(1),
    )(dy, w_out, ba, bb, gl)


GLU_TILE = 256


def _ffn_in_swiglu(h, w_t, name):
    s, d = h.shape
    f = w_t.shape[0] // 2
    tm = _row_tile(s, 2048)
    tg = GLU_TILE
    nb = f // tg

    def body(h_ref, wg_ref, wu_ref, g_ref, u_ref, act_ref):
        hv = h_ref[...]
        g = lax.dot_general(hv, wg_ref[...], _NT, preferred_element_type=f32)
        u = lax.dot_general(hv, wu_ref[...], _NT, preferred_element_type=f32)
        g_ref[...] = g.astype(bf16)
        u_ref[...] = u.astype(bf16)
        act_ref[...] = (g * jax.nn.sigmoid(g) * u).astype(bf16)

    col = pl.BlockSpec((tm, tg), lambda i, j: (i, j))
    return pl.pallas_call(
        body, name=name, grid=(s // tm, nb),
        in_specs=[pl.BlockSpec((tm, d), lambda i, j: (i, 0)), pl.BlockSpec((tg, d), lambda i, j: (j, 0)),
                  pl.BlockSpec((tg, d), lambda i, j: (j + nb, 0))],
        out_specs=[col] * 3, out_shape=[jax.ShapeDtypeStruct((s, f), bf16)] * 3, compiler_params=_params(2),
    )(h, w_t, w_t)


def _ffn_out_dgrad_swiglu(dy, w_out, g, u, name):
    s, d = dy.shape
    f = g.shape[1]
    tm = _row_tile(s, 2048)
    tg = GLU_TILE

    def body(dy_ref, w_ref, g_ref, u_ref, dg_ref, du_ref):
        dv = lax.dot_general(dy_ref[...], w_ref[...], _NT, preferred_element_type=f32)
        gv, uv = g_ref[...].astype(f32), u_ref[...].astype(f32)
        sg = jax.nn.sigmoid(gv)
        dg_ref[...] = (dv * uv * (sg * (1.0 + gv * (1.0 - sg)))).astype(bf16)
        du_ref[...] = (dv * (gv * sg)).astype(bf16)

    col = pl.BlockSpec((tm, tg), lambda i, j: (i, j))
    return pl.pallas_call(
        body, name=name, grid=(s // tm, f // tg),
        in_specs=[pl.BlockSpec((tm, d), lambda i, j: (i, 0)), pl.BlockSpec((tg, d), lambda i, j: (j, 0)), col, col],
        out_specs=[col] * 2, out_shape=[jax.ShapeDtypeStruct((s, f), bf16)] * 2, compiler_params=_params(2),
    )(dy, w_out, g, u)


def _wgrad_stack(parts, h, name):
    s, m = parts[0].shape
    d = h.shape[1]
    tm = 256
    nb = m // tm
    n = len(parts)

    def body(*refs):
        i = pl.program_id(0)
        for p in range(n):
            @pl.when(i // nb == p)
            def _(p=p):
                refs[n + 1][...] = lax.dot_general(refs[p][...], refs[n][...], _TN, preferred_element_type=f32).astype(bf16)

    a_specs = [pl.BlockSpec((s, tm), lambda i, p=p: (0, jnp.clip(i - p * nb, 0, nb - 1))) for p in range(n)]
    return pl.pallas_call(
        body, name=name, grid=(n * nb,), in_specs=a_specs + [pl.BlockSpec((s, d), lambda i: (0, 0))],
        out_specs=pl.BlockSpec((tm, d), lambda i: (i, 0)),
        out_shape=jax.ShapeDtypeStruct((n * m, d), bf16), compiler_params=_params(1),
    )(*parts, h)


def _ada_fwd(c_all, w, b, name):
    def body(c_ref, w_ref, b_ref, o_ref):
        o_ref[...] = jnp.dot(c_ref[...].astype(bf16), w_ref[...].astype(bf16), preferred_element_type=f32) + b_ref[...]

    return pl.pallas_call(
        body, name=name, out_shape=jax.ShapeDtypeStruct((c_all.shape[0], w.shape[1]), f32), compiler_params=_params(),
    )(c_all, w, b)


def _ada_wgrad(c_all, d_all, name):
    n, d = c_all.shape
    w = d_all.shape[1]

    def body(c_ref, d_ref, o_ref):
        eye = (lax.broadcasted_iota(jnp.int32, (n, n), 0) == lax.broadcasted_iota(jnp.int32, (n, n), 1)).astype(f32)
        ct = lax.dot_general(c_ref[...], eye, _TN, precision=lax.Precision.HIGHEST, preferred_element_type=f32)
        g = ct[:, 0:1] * d_ref[0:1, :]
        for bi in range(1, n):
            g = g + ct[:, bi:bi + 1] * d_ref[bi:bi + 1, :]
        o_ref[0] = g

    return pl.pallas_call(
        body, name=name, out_shape=jax.ShapeDtypeStruct((1, d, w), f32), compiler_params=_params(),
    )(c_all, d_all)


def _adamw(parts, w, m, v, name, mine=None):
    r, c = w.shape
    n_parts = parts.shape[0]
    row_tiles = [t for t in range(min(r, 256), 0, -1) if r % t == 0 and (t % 16 == 0 or t == r)]
    if row_tiles:
        tr, tc = row_tiles[0], c
    else:
        tr, tc = r, next(t for t in (256, LANES) if c % t == 0)

    def body(p_ref, *rest):
        own_ref = rest[0] if mine is not None else None
        w_ref, m_ref, v_ref, g_ref, d_ref, nm_ref, nv_ref = rest[-7:]
        if mine is not None:
            x, y, cc = _me()
            me = 4 * x + 2 * y + cc

        def part(i):
            if mine is None:
                return p_ref[i].astype(f32)
            return jnp.where(me == i, own_ref[i], p_ref[i]).astype(f32)

        g = part(0)
        for i in range(1, n_parts):
            g = g + part(i)
        mm = ADAM_B1 * m_ref[...] + (1.0 - ADAM_B1) * g
        vv = ADAM_B2 * v_ref[...] + (1.0 - ADAM_B2) * (g * g)
        m_hat = mm / (1.0 - ADAM_B1 ** ADAM_STEP)
        v_hat = vv / (1.0 - ADAM_B2 ** ADAM_STEP)
        g_ref[...] = g
        d_ref[...] = -ADAM_LR * (m_hat / (jnp.sqrt(v_hat) + ADAM_EPS) + ADAM_WD * w_ref[...])
        nm_ref[...] = mm
        nv_ref[...] = vv

    spec = pl.BlockSpec((tr, tc), lambda i, j: (i, j))
    stack = [parts] if mine is None else [parts, mine]
    return pl.pallas_call(
        body, name=name, grid=(r // tr, c // tc),
        in_specs=[pl.BlockSpec((n_parts, tr, tc), lambda i, j: (0, i, j))] * len(stack) + [spec] * 3,
        out_specs=[spec] * 4, out_shape=[jax.ShapeDtypeStruct((r, c), f32)] * 4, compiler_params=_params(2),
    )(*stack, w, m, v)


def _me():
    return lax.axis_index("x"), lax.axis_index("y"), lax.axis_index("c")


def _all_gather(arrays, name, vmem=False, after=None):
    n = len(arrays)
    space = pltpu.VMEM if vmem else pl.ANY
    extra = [] if after is None else [after]

    def body(*refs):
        ins = refs[:n]
        outs = refs[n + len(extra):2 * n + len(extra)]
        send_sems, recv_sems, local_sems = refs[2 * n + len(extra):]
        x, y, c = _me()
        me, sibling = (x, y, c), (x, y, 1 - c)
        chips = [(1 - x, y), (x, 1 - y), (1 - x, 1 - y)]

        def rows(a, dev):
            return outs[a].at[4 * dev[0] + 2 * dev[1] + dev[2]]

        def copy(a, k, block, to, src=None):
            return pltpu.make_async_remote_copy(
                src_ref=rows(a, block) if src is None else src, dst_ref=rows(a, block),
                send_sem=send_sems.at[a, k], recv_sem=recv_sems.at[a, k], device_id=to, device_id_type=MESH)

        mine = [pltpu.make_async_copy(ins[a], rows(a, me), local_sems.at[a]) for a in range(n)]
        for cp in mine:
            cp.start()
        first = []
        for a in range(n):
            first.append(copy(a, 0, me, sibling, src=ins[a]))
            first += [copy(a, 1 + j, me, (*chip, c), src=ins[a]) for j, chip in enumerate(chips)]
        for cp in first:
            cp.start()
        passed = []
        for j, chip in enumerate(chips):
            for a in range(n):
                copy(a, 1 + j, (*chip, c), me).wait_recv()
                fwd = copy(a, 4 + j, (*chip, c), sibling)
                fwd.start()
                passed.append(fwd)
        for a in range(n):
            copy(a, 0, sibling, me).wait_recv()
            for j, chip in enumerate(chips):
                copy(a, 4 + j, (*chip, 1 - c), me).wait_recv()
        for cp in first + passed:
            cp.wait_send()
        for cp in mine:
            cp.wait()

    outs = pl.pallas_call(
        body, name=name,
        in_specs=[pl.BlockSpec(memory_space=space)] * n + [pl.BlockSpec(memory_space=pl.ANY)] * len(extra),
        out_specs=[pl.BlockSpec(memory_space=space)] * n,
        out_shape=[jax.ShapeDtypeStruct((N_DEV,) + a.shape, a.dtype) for a in arrays],
        scratch_shapes=[pltpu.SemaphoreType.DMA((n, 7)), pltpu.SemaphoreType.DMA((n, 7)), pltpu.SemaphoreType.DMA((n,))],
        compiler_params=pltpu.CompilerParams(vmem_limit_bytes=VMEM_LIMIT),
    )(*arrays, *extra)
    return list(outs)


_FLIPS = ((0, 0, 1), (1, 0, 0), (0, 1, 0), (1, 1, 0), (1, 0, 1), (0, 1, 1), (1, 1, 1))
_HBM = pl.BlockSpec(memory_space=pltpu.HBM)
_SEM = pl.BlockSpec(memory_space=pltpu.SEMAPHORE)


def _exchange_copies(scatter, srcs, lands, send_sems, recv_sems):
    x, y, c = _me()
    me_row = 4 * x + 2 * y + c
    out = []
    for k, (fx, fy, fc) in enumerate(_FLIPS):
        peer = (x ^ fx, y ^ fy, c ^ fc)
        peer_row = 4 * peer[0] + 2 * peer[1] + peer[2]
        for a in range(len(srcs)):
            out.append(pltpu.make_async_remote_copy(
                src_ref=srcs[a].at[peer_row] if scatter else srcs[a], dst_ref=lands[a].at[me_row],
                send_sem=send_sems.at[7 * a + k], recv_sem=recv_sems.at[7 * a + k], device_id=peer, device_id_type=MESH))
    return out


def _exchange_start(arrays, scatter, name, after=None):
    n = len(arrays)
    lands = [lax.empty(a.shape if scatter else (N_DEV,) + a.shape, a.dtype) for a in arrays]
    extra = [] if after is None else [after]

    def body(*refs):
        srcs, zones = refs[:n], refs[n:2 * n]
        send_sems, recv_sems = refs[2 * n + len(extra)], refs[2 * n + len(extra) + 1]
        token = refs[-1]
        for cp in _exchange_copies(scatter, srcs, zones, send_sems, recv_sems):
            cp.start()
        token[...] = jnp.zeros_like(token)

    thru = [pltpu.HBM(a.shape, a.dtype) for a in list(arrays) + lands]
    outs = pl.pallas_call(
        body, name=name,
        out_shape=(pltpu.SemaphoreType.DMA((7 * n,)), pltpu.SemaphoreType.DMA((7 * n,)), *thru, jax.ShapeDtypeStruct((8, LANES), f32)),
        in_specs=[_HBM] * (2 * n) + [pl.BlockSpec(memory_space=pl.ANY)] * len(extra),
        out_specs=(_SEM, _SEM, *[_HBM] * (2 * n), pl.BlockSpec(memory_space=pltpu.VMEM)),
        input_output_aliases={i: 2 + i for i in range(2 * n)},
        compiler_params=pltpu.CompilerParams(has_side_effects=pltpu.SideEffectType.DATAFLOW_SIDE_EFFECTING),
    )(*[pltpu.with_memory_space_constraint(a, pltpu.HBM) for a in list(arrays) + lands], *extra)
    return dict(n=n, scatter=scatter, sems=outs[:2], srcs=outs[2:2 + n], lands=outs[2 + n:2 + 2 * n], token=outs[-1])


def _exchange_wait(handle, after, name):
    n, scatter = handle["n"], handle["scatter"]

    def body(*refs):
        srcs, zones = refs[:n], refs[n:2 * n]
        send_sems, recv_sems = refs[2 * n], refs[2 * n + 1]
        for cp in _exchange_copies(scatter, srcs, zones, send_sems, recv_sems):
            cp.wait_send()
            cp.wait_recv()

    thru = [pltpu.HBM(a.shape, a.dtype) for a in list(handle["srcs"]) + list(handle["lands"])]
    outs = pl.pallas_call(
        body, name=name, out_shape=tuple(thru),
        in_specs=[_HBM] * (2 * n) + [_SEM, _SEM, pl.BlockSpec(memory_space=pl.ANY)], out_specs=tuple([_HBM] * (2 * n)),
        input_output_aliases={i: i for i in range(2 * n)},
        compiler_params=pltpu.CompilerParams(has_side_effects=pltpu.SideEffectType.DATAFLOW_SIDE_EFFECTING),
    )(*handle["srcs"], *handle["lands"], *handle["sems"], after)
    return list(outs[n:])


def _cols_from_shards(g):
    return jnp.transpose(g, (1, 0, 2)).reshape(g.shape[1], -1)


def _shards_from_cols(a):
    return jnp.transpose(a.reshape(a.shape[0], N_DEV, -1), (1, 0, 2))


def _local_step(x, positions, ada, g_pre_mix, g_post_mix, b_f, sinks, g_pre_ffn, g_post_ffn, target,
                w_in_t, late_weights, on_grads):
    s, d = x.shape
    row = lambda v: v.reshape(1, -1)
    shift_m, scale_m, gate_m, shift_f, scale_f, gate_f = (ada[i:i + 1] for i in range(6))
    w_gate_t, w_qkv_t = w_in_t[F_OFF + N_HEADS:], w_in_t[:QKV_W]
    w_f_t = jnp.pad(w_in_t[F_OFF:F_OFF + N_HEADS], ((0, LANES - N_HEADS), (0, 0)))
    bf_row = jnp.pad(row(b_f), ((0, 0), (0, LANES - N_HEADS)))
    sink_rows = jnp.broadcast_to(sinks.reshape(N_HEADS, 1).astype(f32), (N_HEADS, LANES))
    inv_freq = 1.0 / (ROPE_THETA ** (jnp.arange(0, HEAD_DIM, 2, dtype=f32) / HEAD_DIM))
    cos, sin_s = _rope_tables(positions.reshape(s, 1), jnp.tile(inv_freq, 4).reshape(1, LANES), "rope_tables")

    h1 = _prenorm(x, row(g_pre_mix), scale_m, shift_m, "prenorm_mix")
    gl = _matmul(h1, w_gate_t, "nt", bf16, "proj_gate")
    qa, ka, va, qb, kb, vb = _proj_qkv(h1, w_qkv_t, cos, sin_s, "proj_qkv")
    fl = _matmul(h1, w_f_t, "nt", f32, "proj_forget")
    cum_b = _forget_prep(fl, bf_row, "forget_prep")
    o_a, lse_a = _attn_fwd(qa, ka, va, "swa_fwd", sink_rows=sink_rows, window=WINDOW, t=512)
    o_b, lse_b = _attn_fwd(qb, kb, vb, "fox_fwd", cum_b=cum_b, t=1024)
    w_branch_a, w_branch_b, w_out, w_ffn_in_t, w_ffn_out = late_weights(o_b)
    ba, bb, merged = _branch_merge(o_a, o_b, w_branch_a, w_branch_b, gl, "branch_merge")
    y1, x2, h2 = _out_proj_postnorm_prenorm(merged, w_out, x, row(g_post_mix), gate_m, row(g_pre_ffn), scale_f, shift_f,
                                            "out_proj_norms")

    g_ff, u_ff, act = _ffn_in_swiglu(h2, w_ffn_in_t, "ffn_in_swiglu")
    loss_row, d_out, d_y2, vec_pf = _out_proj_loss_tail(act, w_ffn_out, x2, row(g_post_ffn), gate_f, target, "ffn_out_loss_tail")

    g_w_ffn_out = _matmul(act, d_y2, "tn", bf16, "ffn_out_wgrad")
    dg_ff, du_ff = _ffn_out_dgrad_swiglu(d_y2, w_ffn_out, g_ff, u_ff, "ffn_out_dgrad_swiglu")
    g_w_ffn_in_t = _wgrad_stack([dg_ff, du_ff], h2, "ffn_in_wgrad")
    sent = on_grads(dict(w_ffn_in=g_w_ffn_in_t, w_ffn_out=g_w_ffn_out))
    d_x2, vec_nf, d_y1, vec_pm = _dgrad_prenorm_bwd(
        [(dg_ff, w_ffn_in_t, 0), (du_ff, w_ffn_in_t, 1)], x2, row(g_pre_ffn), scale_f, d_out, "ffn_in_dgrad_norms_bwd",
        after=sent, below=(y1, row(g_post_mix), gate_m))

    g_w_out = _matmul(merged, d_y1, "tn", bf16, "out_proj_wgrad")
    d_ba, d_bb, dgl = _out_dgrad_merge_bwd(d_y1, w_out, ba, bb, gl, "out_proj_dgrad_merge_bwd")
    g_w_branch_a = _matmul(o_a, d_ba, "tn", bf16, "branch_a_wgrad")
    g_w_branch_b = _matmul(o_b, d_bb, "tn", bf16, "branch_b_wgrad")
    sent = on_grads(dict(w_out=g_w_out, w_branch_a=g_w_branch_a, w_branch_b=g_w_branch_b))
    d_oa, delta_a, d_sink = _branch_dgrad_delta(d_ba, w_branch_a, o_a, "branch_a_dgrad_delta", lse=lse_a,
                                                sink_rows=sink_rows, after=sent)
    d_ob, delta_b = _branch_dgrad_delta(d_bb, w_branch_b, o_b, "branch_b_dgrad_delta", after=sent)
    dqa_t, dka, dva = _attn_bwd(qa, ka, va, d_oa, lse_a, delta_a, "swa_bwd", window=WINDOW, t=512)
    dqb_t, dkb, dvb, dcs, rs = _attn_bwd(qb, kb, vb, d_ob, lse_b, delta_b, "fox_bwd", cum_b=cum_b, t=512)
    dqkv = _qkv_prep_bwd(dqa_t, dka, dva, dqb_t, dkb, dvb, cos, sin_s, "qkv_prep_bwd")
    dfl, vec_bf = _forget_prep_bwd(rs.reshape(N_HEADS, s), dcs, fl, bf_row, "forget_prep_bwd")
    g_w_in_t = jnp.concatenate([_matmul(dqkv, h1, "tn", bf16, "qkv_wgrad"), _matmul(dfl, h1, "tn", bf16, "forget_wgrad")[:N_HEADS],
                                _matmul(dgl, h1, "tn", bf16, "gate_wgrad")], axis=0)
    sent = on_grads(dict(w_in=g_w_in_t))
    grad_x, vec_nm = _dgrad_prenorm_bwd([(dgl, w_gate_t, 0), (dqkv, w_qkv_t, 0), (dfl, w_f_t, 0)], x, row(g_pre_mix),
                                        scale_m, d_x2, "in_proj_dgrad_prenorm_bwd", after=sent)

    d_ada = jnp.concatenate([vec_nm[0], vec_nm[1], vec_pm[0], vec_nf[0], vec_nf[1], vec_pf[0]])
    small = dict(b_ada=d_ada, g_pre_mix=vec_nm[2], g_post_mix=vec_pm[1], g_pre_ffn=vec_nf[2], g_post_ffn=vec_pf[1],
                 b_f=vec_bf[0, :N_HEADS], sinks=d_sink[:, 0], loss=loss_row[0, :1])
    return grad_x, small


_SMALL = (("b_ada", 6144), ("g_pre_mix", 1024), ("g_post_mix", 1024), ("g_pre_ffn", 1024), ("g_post_ffn", 1024),
          ("b_f", 128), ("sinks", 128), ("loss", 128))
_SMALL_ROWS = 88


def _pack_small(vals):
    parts = [jnp.pad(vals[k].reshape(-1).astype(f32), (0, n - vals[k].size)) for k, n in _SMALL]
    flat = jnp.concatenate(parts)
    return jnp.pad(flat, (0, _SMALL_ROWS * LANES - flat.size)).reshape(_SMALL_ROWS, LANES)


def _unpack_small(slab, shapes):
    flat, out, off = slab.reshape(-1), {}, 0
    for k, n in _SMALL:
        size = math.prod(shapes[k])
        out[k] = flat[off:off + size].reshape(shapes[k])
        off += n
    return out


def kernel(x, c, positions, w_ada, b_ada, g_pre_mix, g_post_mix, w_in, b_f, sinks, w_branch_a, w_branch_b, w_out, g_pre_ffn, g_post_ffn, w_ffn_in, w_ffn_out, loss_target, m_w_ada, m_b_ada, m_g_pre_mix, m_g_post_mix, m_w_in, m_b_f, m_sinks, m_w_branch_a, m_w_branch_b, m_w_out, m_g_pre_ffn, m_g_post_ffn, m_w_ffn_in, m_w_ffn_out, v_w_ada, v_b_ada, v_g_pre_mix, v_g_post_mix, v_w_in, v_b_f, v_sinks, v_w_branch_a, v_w_branch_b, v_w_out, v_g_pre_ffn, v_g_post_ffn, v_w_ffn_in, v_w_ffn_out):
    xi, yi, ci = _me()
    me = 4 * xi + 2 * yi + ci
    d = D_MODEL
    ada_w = w_ada.shape[2]

    c_all, = _all_gather([c], "gather_c", vmem=True)
    c_all = c_all.reshape(N_DEV, d)
    b_mine = lax.dynamic_slice(b_ada, (0, me * ada_w), (1, ada_w))
    ada_cols = _ada_fwd(c_all, w_ada[0], b_mine, "ada_fwd")

    transposed = ("w_in", "w_ffn_in")
    tr = lambda a: jnp.transpose(a[0])

    ada_all, g_in = _all_gather([ada_cols, tr(w_in).astype(bf16)], "gather_ada_w_in")
    ada = lax.dynamic_index_in_dim(ada_all, me, axis=1, keepdims=False).reshape(6, d)
    late = [w.astype(bf16) for w in (w_branch_a[0], w_branch_b[0], w_out[0], tr(w_ffn_in), w_ffn_out[0])]
    late_h = _exchange_start(late, False, "gather_late_start", after=g_in)

    def mine_into(zone, block):
        return lax.dynamic_update_index_in_dim(zone, block, me, 0)

    def rows_from_shards(g):
        return g.reshape(g.shape[0] * g.shape[1], g.shape[2])

    def late_weights(after):
        zones = _exchange_wait(late_h, after, "gather_late_wait")
        g_ba, g_bb, g_out, g_fi, g_fo = (mine_into(z, w) for z, w in zip(zones, late))
        return (_cols_from_shards(g_ba), _cols_from_shards(g_bb), rows_from_shards(g_out), rows_from_shards(g_fi),
                rows_from_shards(g_fo))

    row_sharded = ("w_out", "w_ffn_out") + transposed
    in_flight = []

    def on_grads(group):
        sends = [g.reshape(N_DEV, g.shape[0] // N_DEV, g.shape[1]) if nm in row_sharded else _shards_from_cols(g)
                 for nm, g in group.items()]
        handle = _exchange_start(sends, True, "scatter_start_%d" % len(in_flight))
        in_flight.append((list(group), sends, handle))
        return handle["token"]

    grad_x, small = _local_step(
        x[0], positions[0], ada + late_h["token"][0, 0], g_pre_mix[0], g_post_mix[0], b_f[0], sinks[0], g_pre_ffn[0],
        g_post_ffn[0], loss_target[0], rows_from_shards(g_in), late_weights, on_grads)

    ws = dict(w_in=(w_in, m_w_in, v_w_in), w_branch_a=(w_branch_a, m_w_branch_a, v_w_branch_a),
              w_branch_b=(w_branch_b, m_w_branch_b, v_w_branch_b), w_out=(w_out, m_w_out, v_w_out),
              w_ffn_in=(w_ffn_in, m_w_ffn_in, v_w_ffn_in), w_ffn_out=(w_ffn_out, m_w_ffn_out, v_w_ffn_out))
    res = {}

    def finish_group(gi, after):
        names, sends, handle = in_flight[gi]
        zones = _exchange_wait(handle, after, "scatter_wait_%d" % gi)
        for nm, zone, sent in zip(names, zones, sends):
            w, m, v = (tr(a) if nm in transposed else a[0] for a in ws[nm])
            out = _adamw(zone, w, m, v, "adamw_" + nm, mine=sent)
            after = out[0]
            res[nm] = [jnp.transpose(o) for o in out] if nm in transposed else out
        return after

    done = finish_group(1, finish_group(0, grad_x))

    slab_all, = _all_gather([_pack_small(small)], "gather_small", vmem=True, after=done)
    small_w = dict(b_ada=b_ada, g_pre_mix=g_pre_mix, g_post_mix=g_post_mix, g_pre_ffn=g_pre_ffn, g_post_ffn=g_post_ffn,
                   b_f=b_f, sinks=sinks, loss=jnp.zeros((1,), f32))
    small_m = dict(b_ada=m_b_ada, g_pre_mix=m_g_pre_mix, g_post_mix=m_g_post_mix, g_pre_ffn=m_g_pre_ffn,
                   g_post_ffn=m_g_post_ffn, b_f=m_b_f, sinks=m_sinks, loss=jnp.zeros((1,), f32))
    small_v = dict(b_ada=v_b_ada, g_pre_mix=v_g_pre_mix, g_post_mix=v_g_post_mix, g_pre_ffn=v_g_pre_ffn,
                   g_post_ffn=v_g_post_ffn, b_f=v_b_f, sinks=v_sinks, loss=jnp.ones((1,), f32))
    shapes = {k: small_w[k].shape for k, _ in _SMALL}
    s_out = _adamw(slab_all, _pack_small(small_w), _pack_small(small_m), _pack_small(small_v), "adamw_small")
    s_grad, s_delta, s_m, s_v = (_unpack_small(o, shapes) for o in s_out)

    d_ada_all = lax.dynamic_slice(slab_all[:, :6144 // LANES, :].reshape(N_DEV, 6144), (0, me * ada_w), (N_DEV, ada_w))
    ada_parts = _ada_wgrad(c_all, d_ada_all, "ada_wgrad")

    res["w_ada"] = _adamw(ada_parts, w_ada[0], m_w_ada[0], v_w_ada[0], "adamw_w_ada")
    finish_group(2, res["w_ada"][0])

    order = ["w_ada", "b_ada", "g_pre_mix", "g_post_mix", "w_in", "b_f", "sinks", "w_branch_a", "w_branch_b", "w_out",
             "g_pre_ffn", "g_post_ffn", "w_ffn_in", "w_ffn_out"]
    outs = [s_grad["loss"].reshape(()), grad_x[None]]
    for which, small_o in enumerate((s_grad, s_delta, s_m, s_v)):
        for nm in order:
            outs.append(res[nm][which][None] if nm in res else small_o[nm])
    return tuple(outs)
```

```python
import functools
import math

import jax
import jax.numpy as jnp
from jax import lax
from jax.experimental import pallas as pl
from jax.experimental.pallas import tpu as pltpu

f32 = jnp.float32
bf16 = jnp.bfloat16

D_MODEL = 1024
HEAD_DIM = 64
N_HEADS = 8
N_PAIRS = 4
QKV_W = 2304
GATE_W = 2048
F_OFF = 2304
IN_W = 4360
WINDOW = 128
ROPE_THETA = 10000.0
RMS_EPS = 1e-6
D_FF = 2816
N_DEV = 8
ADAM_LR, ADAM_B1, ADAM_B2, ADAM_EPS, ADAM_WD, ADAM_STEP = 0.001, 0.9, 0.999, 1e-08, 0.01, 10
NEG = -1e30
LANES = 128
VMEM_LIMIT = 48 * 1024 * 1024
MESH = pl.DeviceIdType.MESH

_NT = (((1,), (1,)), ((), ()))
_TN = (((0,), (0,)), ((), ()))


def _params(n_grid=0):
    sem = ("arbitrary",) * n_grid if n_grid else None
    return pltpu.CompilerParams(dimension_semantics=sem, vmem_limit_bytes=VMEM_LIMIT)


def _row_tile(s, want):
    t = min(s, want)
    assert s % t == 0, (s, t)
    return t


MATMUL_VMEM_BUDGET = 40 * 1024 * 1024


def _matmul_tiles(m, n, k, a_item, b_item, o_item):
    def tiles(d):
        return [t for t in range(LANES, min(d, 2048) + 1, LANES) if d % t == 0] or [d]

    best = None
    for tm in tiles(m):
        for tn in tiles(n):
            vmem = 2 * (tm * k * a_item + tn * k * b_item + tm * tn * o_item) + tm * tn * 4
            if vmem > MATMUL_VMEM_BUDGET:
                continue
            traffic = m * k * a_item + n * k * b_item * (1 if tn == n else m // tm) + m * n * o_item
            steps = (m // tm) * (n // tn)
            key = (traffic, 0, steps) if steps >= 4 else (traffic, 1, -steps)
            if best is None or key < best[0]:
                best = (key, tm, tn)
    assert best is not None, (m, n, k)
    return best[1], best[2]


def _matmul(a, b, mode, out_dtype, name, after=None):
    if mode == "nn":
        (m, k), n = a.shape, b.shape[1]
    elif mode == "nt":
        (m, k), n = a.shape, b.shape[0]
    else:
        (k, m), n = a.shape, b.shape[1]
    tm, tn = _matmul_tiles(m, n, k, a.dtype.itemsize, b.dtype.itemsize, jnp.dtype(out_dtype).itemsize)
    if mode == "nn":
        a_spec, b_spec, dims = pl.BlockSpec((tm, k), lambda i, j: (i, 0)), pl.BlockSpec((k, tn), lambda i, j: (0, j)), None
    elif mode == "nt":
        a_spec, b_spec, dims = pl.BlockSpec((tm, k), lambda i, j: (i, 0)), pl.BlockSpec((tn, k), lambda i, j: (j, 0)), _NT
    else:
        a_spec, b_spec, dims = pl.BlockSpec((k, tm), lambda i, j: (0, i)), pl.BlockSpec((k, tn), lambda i, j: (0, j)), _TN

    def body(a_ref, b_ref, *rest):
        o_ref = rest[-1]
        av, bv = a_ref[...].astype(bf16), b_ref[...].astype(bf16)
        if dims is None:
            r = jnp.dot(av, bv, preferred_element_type=f32)
        else:
            r = lax.dot_general(av, bv, dims, preferred_element_type=f32)
        o_ref[...] = r.astype(out_dtype)

    extra = [] if after is None else [after]
    return pl.pallas_call(
        body, name=name, grid=(m // tm, n // tn), in_specs=[a_spec, b_spec] + [pl.BlockSpec(memory_space=pl.ANY)] * len(extra),
        out_specs=pl.BlockSpec((tm, tn), lambda i, j: (i, j)),
        out_shape=jax.ShapeDtypeStruct((m, n), out_dtype), compiler_params=_params(2),
    )(a, b, *extra)


def _rstd(v):
    return lax.rsqrt(jnp.mean(v * v, axis=-1, keepdims=True) + RMS_EPS)


def _row_spec(tm, d):
    return pl.BlockSpec((tm, d), lambda i: (i, 0))


def _vec_spec(d, rows=1):
    return pl.BlockSpec((rows, d), lambda i: (0, 0))


def _proj_spec(a, w, tm):
    return [_row_spec(tm, a.shape[1]), pl.BlockSpec(w.shape, lambda i: (0, 0))]


def _out_proj_postnorm_prenorm(a, w, x, g_post, gate, g_pre, scale, shift, name):
    s, d = x.shape
    tm = _row_tile(s, 512)

    def body(a_ref, w_ref, x_ref, gp_ref, gate_ref, g_ref, sc_ref, sh_ref, y_ref, x2_ref, h_ref):
        yv = jnp.dot(a_ref[...], w_ref[...], preferred_element_type=f32)
        y_ref[...] = yv
        x2 = x_ref[...] + gate_ref[...] * (yv * _rstd(yv) * gp_ref[...])
        x2_ref[...] = x2
        h_ref[...] = ((x2 * _rstd(x2) * g_ref[...]) * (1.0 + sc_ref[...]) + sh_ref[...]).astype(bf16)

    return pl.pallas_call(
        body, name=name, grid=(s // tm,), in_specs=_proj_spec(a, w, tm) + [_row_spec(tm, d)] + [_vec_spec(d)] * 5,
        out_specs=[_row_spec(tm, d)] * 3,
        out_shape=[jax.ShapeDtypeStruct((s, d), f32)] * 2 + [jax.ShapeDtypeStruct((s, d), bf16)], compiler_params=_params(1),
    )(a, w, x, g_post, gate, g_pre, scale, shift)


def _rms_bwd(u, v, r):
    return r * u - v * (r * r * r) * jnp.mean(u * v, axis=-1, keepdims=True)


def _out_proj_loss_tail(a, w, x, g, gate, target, name):
    s, d = x.shape
    tm = _row_tile(s, 512)

    def body(a_ref, w_ref, x_ref, g_ref, gate_ref, t_ref, loss_ref, do_ref, dy_ref, vec_ref):
        @pl.when(pl.program_id(0) == 0)
        def _():
            loss_ref[...] = jnp.zeros_like(loss_ref)
            vec_ref[...] = jnp.zeros_like(vec_ref)
        yv = jnp.dot(a_ref[...], w_ref[...], preferred_element_type=f32)
        r = _rstd(yv)
        yn = yv * r
        err = x_ref[...] + gate_ref[...] * (yn * g_ref[...]) - t_ref[...]
        loss_ref[...] += 0.5 * jnp.sum(jnp.mean(err * err, axis=-1, keepdims=True), axis=0, keepdims=True)
        dr = err / d
        do_ref[...] = dr
        dn = dr * gate_ref[...]
        vec_ref[0:1, :] += jnp.sum(dr * (yn * g_ref[...]), axis=0, keepdims=True)
        vec_ref[1:2, :] += jnp.sum(dn * yn, axis=0, keepdims=True)
        dy_ref[...] = _rms_bwd(dn * g_ref[...], yv, r).astype(bf16)

    return pl.pallas_call(
        body, name=name, grid=(s // tm,),
        in_specs=_proj_spec(a, w, tm) + [_row_spec(tm, d)] + [_vec_spec(d)] * 2 + [_row_spec(tm, d)],
        out_specs=[_vec_spec(LANES), _row_spec(tm, d), _row_spec(tm, d), _vec_spec(d, 8)],
        out_shape=[jax.ShapeDtypeStruct((1, LANES), f32), jax.ShapeDtypeStruct((s, d), f32),
                   jax.ShapeDtypeStruct((s, d), bf16), jax.ShapeDtypeStruct((8, d), f32)],
        compiler_params=_params(1),
    )(a, w, x, g, gate, target)


def _dgrad_prenorm_bwd(terms, x, g, scale, dres, name, after=None, below=None):
    s, d = x.shape
    n = len(terms)
    k = sum(a.shape[1] for a, _, _ in terms)
    row_bytes = 2 * (2 * k) + d * (4 + 2 * 4 * 3 + (2 * 4 + 2 * 2 if below else 0))
    tm = next(t for t in (512, 256, 128) if s % t == 0 and 4 * k * d + t * row_bytes <= MATMUL_VMEM_BUDGET)
    extra = [] if after is None else [after]

    def body(*refs):
        a_refs, b_refs = refs[:n], refs[n:2 * n]
        x_ref, g_ref, sc_ref, dr_ref = refs[2 * n:2 * n + 4]
        n_in = 2 * n + 4 + (3 if below else 0) + len(extra)
        dx_ref, vec_ref = refs[n_in], refs[n_in + 1]
        if below:
            y_ref, gp_ref, gate_ref = refs[2 * n + 4:2 * n + 7]
            dy_ref, vec2_ref = refs[n_in + 2], refs[n_in + 3]

        @pl.when(pl.program_id(0) == 0)
        def _():
            vec_ref[...] = jnp.zeros_like(vec_ref)
            if below:
                vec2_ref[...] = jnp.zeros_like(vec2_ref)
        dhv = jnp.dot(a_refs[0][...], b_refs[0][...], preferred_element_type=f32)
        for i in range(1, n):
            dhv = dhv + jnp.dot(a_refs[i][...], b_refs[i][...], preferred_element_type=f32)
        xv = x_ref[...]
        r = _rstd(xv)
        xn = xv * r
        dn = dhv * (1.0 + sc_ref[...])
        vec_ref[0:1, :] += jnp.sum(dhv, axis=0, keepdims=True)
        vec_ref[1:2, :] += jnp.sum(dhv * (xn * g_ref[...]), axis=0, keepdims=True)
        vec_ref[2:3, :] += jnp.sum(dn * xn, axis=0, keepdims=True)
        dx = dr_ref[...] + _rms_bwd(dn * g_ref[...], xv, r)
        dx_ref[...] = dx
        if below:
            yv = y_ref[...]
            ry = _rstd(yv)
            yn = yv * ry
            dny = dx * gate_ref[...]
            vec2_ref[0:1, :] += jnp.sum(dx * (yn * gp_ref[...]), axis=0, keepdims=True)
            vec2_ref[1:2, :] += jnp.sum(dny * yn, axis=0, keepdims=True)
            dy_ref[...] = _rms_bwd(dny * gp_ref[...], yv, ry).astype(bf16)

    in_specs = ([_row_spec(tm, a.shape[1]) for a, _, _ in terms]
                + [pl.BlockSpec((a.shape[1], d), lambda i, r=r: (r, 0)) for a, _, r in terms]
                + [_row_spec(tm, d)] + [_vec_spec(d)] * 2 + [_row_spec(tm, d)])
    out_specs = [_row_spec(tm, d), _vec_spec(d, 8)]
    out_shape = [jax.ShapeDtypeStruct((s, d), f32), jax.ShapeDtypeStruct((8, d), f32)]
    args = [a for a, _, _ in terms] + [b for _, b, _ in terms] + [x, g, scale, dres]
    if below:
        in_specs += [_row_spec(tm, d)] + [_vec_spec(d)] * 2
        out_specs += [_row_spec(tm, d), _vec_spec(d, 8)]
        out_shape += [jax.ShapeDtypeStruct((s, d), bf16), jax.ShapeDtypeStruct((8, d), f32)]
        args += list(below)
    return pl.pallas_call(
        body, name=name, grid=(s // tm,), in_specs=in_specs + [pl.BlockSpec(memory_space=pl.ANY)] * len(extra),
        out_specs=out_specs, out_shape=out_shape, compiler_params=_params(1),
    )(*args, *extra)


def _lane():
    return lax.broadcasted_iota(jnp.int32, (1, LANES), 1)


def _rope_tables(pos_col, inv_freq, name):
    s = pos_col.shape[0]

    def body(p_ref, f_ref, cos_ref, sin_ref):
        ang = p_ref[...].astype(f32) * f_ref[...]
        first_half = (_lane() % HEAD_DIM) < HEAD_DIM // 2
        cos_ref[...] = jnp.cos(ang)
        sn = jnp.sin(ang)
        sin_ref[...] = jnp.where(first_half, -sn, sn)

    return pl.pallas_call(
        body, name=name, out_shape=[jax.ShapeDtypeStruct((s, LANES), f32)] * 2, compiler_params=_params(),
    )(pos_col, inv_freq)


def _swap_halves(v):
    first_half = (_lane() % HEAD_DIM) < HEAD_DIM // 2
    return jnp.where(first_half, pltpu.roll(v, LANES - HEAD_DIM // 2, axis=1), pltpu.roll(v, HEAD_DIM // 2, axis=1))


def _prenorm_proj_qkv(x, g, mod_scale, mod_shift, w_qkv_t, cos, sin_s, name):
    s, d = x.shape
    tm = _row_tile(s, 512)
    scale = 1.0 / math.sqrt(HEAD_DIM)

    def body(x_ref, g_ref, msc_ref, msh_ref, w_ref, c_ref, s_ref, h_ref, qa_ref, ka_ref, va_ref, qb_ref, kb_ref, vb_ref):
        xv = x_ref[...]
        h = ((xv * _rstd(xv) * g_ref[...]) * (1.0 + msc_ref[...]) + msh_ref[...]).astype(bf16)
        h_ref[...] = h
        proj = lax.dot_general(h, w_ref[...], _NT, preferred_element_type=f32)
        cs, sn = c_ref[...], s_ref[...]
        low = _lane() < HEAD_DIM

        def blk(j):
            return proj[:, j * LANES:(j + 1) * LANES]

        def rope(v):
            return v * cs + _swap_halves(v) * sn

        def expand(v):
            other = pltpu.roll(v, HEAD_DIM, axis=1)
            return jnp.where(low, v, other), jnp.where(low, other, v)

        for j in range(N_PAIRS):
            qa_ref[:, j * LANES:(j + 1) * LANES] = (rope(blk(j)) * scale).astype(bf16)
            qb_ref[:, j * LANES:(j + 1) * LANES] = (blk(6 + j) * scale).astype(bf16)
            kb_ref[:, j * LANES:(j + 1) * LANES] = blk(10 + j).astype(bf16)
            vb_ref[:, j * LANES:(j + 1) * LANES] = blk(14 + j).astype(bf16)
        k0, k1 = expand(rope(blk(4)))
        v0, v1 = expand(blk(5))
        for j in range(N_PAIRS):
            ka_ref[:, j * LANES:(j + 1) * LANES] = (k0 if j < 2 else k1).astype(bf16)
            va_ref[:, j * LANES:(j + 1) * LANES] = (v0 if j < 2 else v1).astype(bf16)

    hw = N_PAIRS * LANES
    return pl.pallas_call(
        body, name=name, grid=(s // tm,),
        in_specs=[_row_spec(tm, d)] + [_vec_spec(d)] * 3
        + [pl.BlockSpec((QKV_W, d), lambda i: (0, 0)), _row_spec(tm, LANES), _row_spec(tm, LANES)],
        out_specs=[_row_spec(tm, d)] + [_row_spec(tm, hw)] * 6,
        out_shape=[jax.ShapeDtypeStruct((s, d), bf16)] + [jax.ShapeDtypeStruct((s, hw), bf16)] * 6, compiler_params=_params(1),
    )(x, g, mod_scale, mod_shift, w_qkv_t, cos, sin_s)


def _qkv_prep_bwd(dqa_t, dka, dva, dqb_t, dkb, dvb, cos, sin_s, name):
    s = dka.shape[0]
    tm = _row_tile(s, 256)
    scale = 1.0 / math.sqrt(HEAD_DIM)
    hw = N_PAIRS * LANES
    t_spec = pl.BlockSpec((hw, tm), lambda i: (0, i))

    def body(dqa_ref, dka_ref, dva_ref, dqb_ref, dkb_ref, dvb_ref, c_ref, s_ref, o_ref):
        cs, sn = c_ref[...], s_ref[...]
        low = _lane() < HEAD_DIM

        def blk(ref, j):
            return ref[:, j * LANES:(j + 1) * LANES]

        def blk_t(ref, j):
            return ref[j * LANES:(j + 1) * LANES, :].T

        def unrope(v):
            return v * cs + _swap_halves(v * sn)

        def fold(ref):
            a, b = blk(ref, 0) + blk(ref, 1), blk(ref, 2) + blk(ref, 3)
            kv0 = a + pltpu.roll(a, HEAD_DIM, axis=1)
            kv1 = b + pltpu.roll(b, HEAD_DIM, axis=1)
            return jnp.where(low, kv0, kv1)

        for j in range(N_PAIRS):
            o_ref[:, j * LANES:(j + 1) * LANES] = (unrope(blk_t(dqa_ref, j)) * scale).astype(bf16)
            o_ref[:, (6 + j) * LANES:(7 + j) * LANES] = (blk_t(dqb_ref, j) * scale).astype(bf16)
            o_ref[:, (10 + j) * LANES:(11 + j) * LANES] = blk(dkb_ref, j).astype(bf16)
            o_ref[:, (14 + j) * LANES:(15 + j) * LANES] = blk(dvb_ref, j).astype(bf16)
        o_ref[:, 4 * LANES:5 * LANES] = unrope(fold(dka_ref)).astype(bf16)
        o_ref[:, 5 * LANES:6 * LANES] = fold(dva_ref).astype(bf16)

    return pl.pallas_call(
        body, name=name, grid=(s // tm,),
        in_specs=[t_spec, _row_spec(tm, hw), _row_spec(tm, hw), t_spec, _row_spec(tm, hw), _row_spec(tm, hw)] + [_row_spec(tm, LANES)] * 2,
        out_specs=_row_spec(tm, QKV_W), out_shape=jax.ShapeDtypeStruct((s, QKV_W), bf16), compiler_params=_params(1),
    )(dqa_t, dka, dva, dqb_t, dkb, dvb, cos, sin_s)


def _cumsum_rows(v, reverse=False):
    n = v.shape[0]
    row = lax.broadcasted_iota(jnp.int32, v.shape, 0)
    sh = 1
    while sh < n:
        if reverse:
            v = v + jnp.where(row < n - sh, pltpu.roll(v, n - sh, axis=0), 0.0)
        else:
            v = v + jnp.where(row >= sh, pltpu.roll(v, sh, axis=0), 0.0)
        sh *= 2
    return v


def _log_sigmoid(z):
    return jnp.minimum(z, 0.0) - jnp.log1p(jnp.exp(-jnp.abs(z)))


def _forget_prep(h, w_f_t, bf_row, name):
    s = h.shape[0]

    def body(h_ref, w_ref, b_ref, f_ref, cb_ref):
        fl = lax.dot_general(h_ref[...], w_ref[...], _NT, preferred_element_type=f32)
        f_ref[...] = fl
        cum = _cumsum_rows(_log_sigmoid(fl + b_ref[...]))
        for hd in range(N_HEADS):
            cb_ref[:, hd * LANES:(hd + 1) * LANES] = jnp.broadcast_to(cum[:, hd:hd + 1], (s, LANES))

    return pl.pallas_call(
        body, name=name,
        out_shape=[jax.ShapeDtypeStruct((s, LANES), f32), jax.ShapeDtypeStruct((s, N_HEADS * LANES), f32)],
        compiler_params=_params(),
    )(h, w_f_t, bf_row)


def _forget_prep_bwd(rs, dcs, fl, bf_row, name):
    s = fl.shape[0]

    def body(r_ref, c_ref, f_ref, b_ref, df_ref, db_ref):
        eye = (lax.broadcasted_iota(jnp.int32, (N_HEADS, LANES), 0) == lax.broadcasted_iota(jnp.int32, (N_HEADS, LANES), 1)).astype(f32)
        dcum = lax.dot_general(r_ref[...], eye, _TN, precision=lax.Precision.HIGHEST, preferred_element_type=f32)
        for h in range(N_HEADS):
            dcum = dcum - jnp.where(_lane() == h, jnp.sum(c_ref[:, h * LANES:(h + 1) * LANES], axis=1, keepdims=True), 0.0)
        dlf = _cumsum_rows(dcum, reverse=True)
        z = f_ref[...] + b_ref[...]
        df = jnp.where(_lane() < N_HEADS, dlf * jax.nn.sigmoid(-z), 0.0)
        df_ref[...] = df.astype(bf16)
        db_ref[...] = jnp.zeros_like(db_ref)
        db_ref[0:1, :] = jnp.sum(df, axis=0, keepdims=True)

    return pl.pallas_call(
        body, name=name,
        out_shape=[jax.ShapeDtypeStruct((s, LANES), bf16), jax.ShapeDtypeStruct((8, LANES), f32)], compiler_params=_params(),
    )(rs, dcs, fl, bf_row)


def _tile_mask(n_keys, n_queries, off, window):
    shape = (n_keys, n_queries)
    d = lax.broadcasted_iota(jnp.int32, shape, 1) - lax.broadcasted_iota(jnp.int32, shape, 0) + off
    valid = d >= 0
    return jnp.logical_and(valid, d < window) if window else valid


def _wide(v, t):
    return jnp.concatenate([v] * (t // LANES), axis=1)


def _attn_fwd(q, k, v, name, *, cum_b=None, sink_rows=None, window=None, t=256):
    s = q.shape[0]
    t = _row_tile(s, t)
    fox, has_sink = cum_b is not None, sink_rows is not None
    assert not window or (window % LANES == 0 and LANES + window <= s)

    def body(*refs):
        q_ref, k_ref, v_ref = refs[:3]
        rest = list(refs[3:])
        cb_ref = rest.pop(0) if fox else None
        sink_ref = rest.pop(0) if has_sink else None
        o_ref, lse_ref = rest
        i = pl.program_id(1)
        low = _lane() < HEAD_DIM
        top = lax.broadcasted_iota(jnp.int32, (LANES, 1), 0) < HEAD_DIM
        q2 = q_ref[...]
        zero = jnp.zeros_like(q2)
        qms = (jnp.where(low, q2, zero), jnp.where(low, zero, q2))

        def tile(k0, n_keys, off, carry, masked, queries=slice(0, t)):
            nq = queries.stop - queries.start
            kblk, vblk = k_ref[pl.ds(k0, n_keys), :], v_ref[pl.ds(k0, n_keys), :]
            valid = _tile_mask(n_keys, nq, off, window) if masked else None
            def scores(h):
                return lax.dot_general(kblk, qms[h][queries], _NT, preferred_element_type=f32)

            def softmax(h, sc):
                m, l, _ = carry[h]
                if fox:
                    sc = sc - _wide(cb_ref[pl.ds(k0, n_keys), h * LANES:(h + 1) * LANES], nq)
                if masked:
                    sc = jnp.where(valid, sc, NEG)
                m_new = jnp.maximum(m, jnp.max(sc, axis=0, keepdims=True))
                p = jnp.exp(sc - m_new)
                alpha = jnp.exp(m - m_new)
                return m_new, alpha * l + jnp.sum(p, axis=0, keepdims=True), alpha, p.astype(bf16)

            def update(h, m_new, l, alpha, p):
                return m_new, l, alpha * carry[h][2] + lax.dot_general(vblk, p, _TN, preferred_element_type=f32)

            if window:
                return tuple(update(h, *softmax(h, scores(h))) for h in range(2))
            scs = [scores(h) for h in range(2)]
            stats = [softmax(h, scs[h]) for h in range(2)]
            return tuple(update(h, *stats[h]) for h in range(2))

        def start(nq):
            if has_sink:
                return tuple((_wide(sink_ref[h:h + 1, :], nq), jnp.ones((1, nq), f32), jnp.zeros((LANES, nq), f32))
                             for h in range(2))
            return tuple((jnp.full((1, nq), NEG, f32), jnp.zeros((1, nq), f32), jnp.zeros((LANES, nq), f32)) for h in range(2))

        def finish(carry, queries):
            (m0, l0, a0), (m1, l1, a1) = carry
            o_t = jnp.where(top, a0 * (1.0 / l0), a1 * (1.0 / l1))
            o_ref[queries, :] = o_t.T.astype(bf16)
            lse_ref[0:1, queries] = m0 + jnp.log(l0)
            lse_ref[1:2, queries] = m1 + jnp.log(l1)

        if window:
            for c in range(t // LANES):
                queries = slice(c * LANES, (c + 1) * LANES)
                q0 = i * t + c * LANES
                k0 = pl.multiple_of(jnp.maximum(q0 - window, 0), LANES)
                finish(tile(k0, LANES + window, q0 - k0, start(LANES), True, queries), queries)
        else:
            carry = lax.fori_loop(0, i, lambda kb, c: tile(pl.multiple_of(kb * t, t), t, 0, c, False), start(t))
            finish(tile(pl.multiple_of(i * t, t), t, 0, carry, True), slice(0, t))

    q_spec = pl.BlockSpec((t, LANES), lambda j, i: (i, j))
    kv_spec = pl.BlockSpec((s, LANES), lambda j, i: (0, j))
    in_specs, args = [q_spec, kv_spec, kv_spec], [q, k, v]
    if fox:
        in_specs += [pl.BlockSpec((s, 2 * LANES), lambda j, i: (0, j))]
        args += [cum_b]
    if has_sink:
        in_specs += [pl.BlockSpec((None, 2, LANES), lambda j, i: (j, 0, 0))]
        args += [sink_rows.reshape(N_PAIRS, 2, LANES)]
    return pl.pallas_call(
        body, name=name, grid=(N_PAIRS, s // t), in_specs=in_specs,
        out_specs=[q_spec, pl.BlockSpec((None, 2, t), lambda j, i: (j, 0, i))],
        out_shape=[jax.ShapeDtypeStruct((s, N_PAIRS * LANES), bf16), jax.ShapeDtypeStruct((N_PAIRS, 2, s), f32)],
        compiler_params=_params(2),
    )(*args)


def _branch_dgrad_delta(db, w, o, name, *, lse=None, sink_rows=None, after=None):
    s, hw = o.shape
    tm = _row_tile(s, 512)
    has_sink = sink_rows is not None
    extra = [] if after is None else [after]

    def body(*refs):
        db_ref, w_ref, o_ref = refs[:3]
        outs = refs[3 + (2 if has_sink else 0) + len(extra):]
        do_ref, dl_ref = outs[:2]
        if has_sink:
            lse_ref, sink_ref = refs[3:5]
            ds_ref = outs[2]

            @pl.when(pl.program_id(0) == 0)
            def _():
                ds_ref[...] = jnp.zeros_like(ds_ref)
        do = lax.dot_general(db_ref[...], w_ref[...], _NT, preferred_element_type=f32).astype(bf16)
        do_ref[...] = do
        for j in range(N_PAIRS):
            cols = slice(j * LANES, (j + 1) * LANES)
            prod_t = (do[:, cols].astype(f32) * o_ref[:, cols].astype(f32)).T
            for h in range(2):
                dl = jnp.sum(prod_t[h * HEAD_DIM:(h + 1) * HEAD_DIM, :], axis=0, keepdims=True)
                dl_ref[j, h:h + 1, :] = dl
                if has_sink:
                    r = 2 * j + h
                    p_sink = jnp.exp(sink_ref[r:r + 1, 0:1] - lse_ref[j, h:h + 1, :])
                    ds_ref[r:r + 1, :] += -jnp.sum(p_sink * dl, axis=1, keepdims=True)

    rows_spec = pl.BlockSpec((N_PAIRS, 2, tm), lambda i: (0, 0, i))
    in_specs = [_row_spec(tm, db.shape[1]), pl.BlockSpec(w.shape, lambda i: (0, 0)), _row_spec(tm, hw)]
    args = [db, w, o]
    out_specs = [_row_spec(tm, hw), rows_spec]
    out_shape = [jax.ShapeDtypeStruct((s, hw), bf16), jax.ShapeDtypeStruct((N_PAIRS, 2, s), f32)]
    if has_sink:
        in_specs += [rows_spec, _vec_spec(LANES, N_HEADS)]
        args += [lse, sink_rows]
        out_specs += [_vec_spec(LANES, N_HEADS)]
        out_shape += [jax.ShapeDtypeStruct((N_HEADS, LANES), f32)]
    return pl.pallas_call(
        body, name=name, grid=(s // tm,), in_specs=in_specs + [pl.BlockSpec(memory_space=pl.ANY)] * len(extra),
        out_specs=out_specs, out_shape=out_shape, compiler_params=_params(1),
    )(*args, *extra)


def _attn_bwd(q, k, v, do, lse, delta, name, *, cum_b=None, window=None, t=256):
    s = q.shape[0]
    t = _row_tile(s, t)
    nblk = s // t
    fox = cum_b is not None
    assert not window or (window % LANES == 0 and LANES + window <= s)

    def body(*refs):
        k_ref, v_ref, q_ref, do_ref, lse_ref, dl_ref = refs[:6]
        rest = list(refs[6:])
        cb_ref = rest.pop(0) if fox else None
        dq_ref, dk_ref, dv_ref = rest[:3]
        dcs_ref, rs_ref = (rest[3], rest[4]) if fox else (None, None)
        b = pl.program_id(1)
        k0 = pl.multiple_of(b * t, t)

        @pl.when(b == 0)
        def _():
            dq_ref[...] = jnp.zeros_like(dq_ref)
            if fox:
                rs_ref[...] = jnp.zeros_like(rs_ref)

        dk_ref[...] = jnp.zeros_like(dk_ref)
        dv_ref[...] = jnp.zeros_like(dv_ref)
        if fox:
            dcs_ref[...] = jnp.zeros_like(dcs_ref)
        low = _lane() < HEAD_DIM
        top = lax.broadcasted_iota(jnp.int32, (LANES, 1), 0) < HEAD_DIM
        kblk, vblk = k_ref[...], v_ref[...]
        k_t = kblk.astype(f32).T.astype(bf16)
        cks = [_wide(cb_ref[pl.ds(k0, t), h * LANES:(h + 1) * LANES], t) for h in range(2)] if fox else None

        def tile(q0, n_queries, off, masked, keys=slice(0, t)):
            cols = pl.ds(q0, n_queries)
            q2, do2 = q_ref[cols, :], do_ref[cols, :]
            zero = jnp.zeros_like(q2)
            valid = _tile_mask(keys.stop - keys.start, n_queries, off, window) if masked else None
            dq_parts = []
            for h in range(2):
                qm = jnp.where(low, q2, zero) if h == 0 else jnp.where(low, zero, q2)
                dom = jnp.where(low, do2, zero) if h == 0 else jnp.where(low, zero, do2)
                sc = lax.dot_general(kblk[keys], qm, _NT, preferred_element_type=f32)
                if fox:
                    sc = sc - cks[h]
                if masked:
                    sc = jnp.where(valid, sc, NEG)
                p = jnp.exp(sc - lse_ref[h:h + 1, cols])
                dp = lax.dot_general(vblk[keys], dom, _NT, preferred_element_type=f32)
                ds = p * (dp - dl_ref[h:h + 1, cols])
                pb, dsb = p.astype(bf16), ds.astype(bf16)
                dv_ref[keys, :] += jnp.dot(pb, dom, preferred_element_type=f32)
                dk_ref[keys, :] += jnp.dot(dsb, qm, preferred_element_type=f32)
                dq_parts.append(jnp.dot(k_t[:, keys], dsb, preferred_element_type=f32))
                if fox:
                    dcs_ref[:, h * LANES:(h + 1) * LANES] += sum(ds[:, g * LANES:(g + 1) * LANES] for g in range(t // LANES))
                    rs_ref[h:h + 1, cols] += jnp.sum(ds, axis=0, keepdims=True)
            dq_ref[:, cols] += jnp.where(top, dq_parts[0], dq_parts[1])

        def later_block(qb, carry):
            tile(pl.multiple_of(qb * t, t), t, 0, False)
            return carry

        if window:
            for c in range(t // LANES):
                first = b * t + c * LANES
                q0 = pl.multiple_of(jnp.minimum(first, s - (LANES + window)), LANES)
                tile(q0, LANES + window, q0 - first, True, slice(c * LANES, (c + 1) * LANES))
        else:
            tile(k0, t, 0, True)
            lax.fori_loop(b + 1, nblk, later_block, 0)

    kv_spec = pl.BlockSpec((t, LANES), lambda j, b: (b, j))
    seq_spec = pl.BlockSpec((s, LANES), lambda j, b: (0, j))
    rows_spec = pl.BlockSpec((None, 2, s), lambda j, b: (j, 0, 0))
    hw = N_PAIRS * LANES
    in_specs, args = [kv_spec, kv_spec, seq_spec, seq_spec, rows_spec, rows_spec], [k, v, q, do, lse, delta]
    out_specs = [pl.BlockSpec((LANES, s), lambda j, b: (j, 0)), kv_spec, kv_spec]
    out_shape = [jax.ShapeDtypeStruct((hw, s), f32), jax.ShapeDtypeStruct((s, hw), f32), jax.ShapeDtypeStruct((s, hw), f32)]
    if fox:
        in_specs += [pl.BlockSpec((s, 2 * LANES), lambda j, b: (0, j))]
        args += [cum_b]
        out_specs += [pl.BlockSpec((t, 2 * LANES), lambda j, b: (b, j)), rows_spec]
        out_shape += [jax.ShapeDtypeStruct((s, N_HEADS * LANES), f32), jax.ShapeDtypeStruct((N_PAIRS, 2, s), f32)]
    return pl.pallas_call(
        body, name=name, grid=(N_PAIRS, nblk), in_specs=in_specs, out_specs=out_specs, out_shape=out_shape,
        compiler_params=_params(2),
    )(*args)


def _branch_merge(o_a, o_b, w_a, w_b, gl, name):
    s, k = o_a.shape
    d = w_a.shape[1]
    tm = _row_tile(s, 1024)

    def body(oa_ref, ob_ref, wa_ref, wb_ref, g_ref, ba_ref, bb_ref, m_ref):
        ba = jnp.dot(oa_ref[...], wa_ref[...], preferred_element_type=f32)
        bb = jnp.dot(ob_ref[...], wb_ref[...], preferred_element_type=f32)
        g0, g1 = jax.nn.sigmoid(g_ref[:, :d].astype(f32)), jax.nn.sigmoid(g_ref[:, d:].astype(f32))
        ba_ref[...] = ba.astype(bf16)
        bb_ref[...] = bb.astype(bf16)
        m_ref[...] = (g0 * ba + g1 * bb).astype(bf16)

    whole = pl.BlockSpec((k, d), lambda i: (0, 0))
    return pl.pallas_call(
        body, name=name, grid=(s // tm,),
        in_specs=[_row_spec(tm, k), _row_spec(tm, k), whole, whole, _row_spec(tm, 2 * d)],
        out_specs=[_row_spec(tm, d)] * 3, out_shape=[jax.ShapeDtypeStruct((s, d), bf16)] * 3, compiler_params=_params(1),
    )(o_a, o_b, w_a, w_b, gl)


def _out_dgrad_merge_bwd(dy, w_out, ba, bb, gl, name):
    s, d = ba.shape
    tm = _row_tile(s, 512)

    def body(dy_ref, w_ref, a_ref, b_ref, g_ref, da_ref, db_ref, dg_ref):
        dmv = lax.dot_general(dy_ref[...], w_ref[...], _NT, preferred_element_type=f32)
        g0, g1 = jax.nn.sigmoid(g_ref[:, :d].astype(f32)), jax.nn.sigmoid(g_ref[:, d:].astype(f32))
        da_ref[...] = (dmv * g0).astype(bf16)
        db_ref[...] = (dmv * g1).astype(bf16)
        dg_ref[:, :d] = (dmv * a_ref[...].astype(f32) * (g0 * (1.0 - g0))).astype(bf16)
        dg_ref[:, d:] = (dmv * b_ref[...].astype(f32) * (g1 * (1.0 - g1))).astype(bf16)

    return pl.pallas_call(
        body, name=name, grid=(s // tm,),
        in_specs=[_row_spec(tm, dy.shape[1]), pl.BlockSpec(w_out.shape, lambda i: (0, 0))] + [_row_spec(tm, d)] * 2
        + [_row_spec(tm, 2 * d)],
        out_specs=[_row_spec(tm, d)] * 2 + [_row_spec(tm, 2 * d)],
        out_shape=[jax.ShapeDtypeStruct((s, d), bf16)] * 2 + [jax.ShapeDtypeStruct((s, 2 * d), bf16)],
        compiler_params=_params(1),
    )(dy, w_out, ba, bb, gl)


GLU_TILE = 256


def _ffn_in_swiglu(h, w_t, name):
    s, d = h.shape
    f = w_t.shape[0] // 2
    tm = _row_tile(s, 2048)
    tg = GLU_TILE
    nb = f // tg

    def body(h_ref, wg_ref, wu_ref, g_ref, u_ref, act_ref):
        hv = h_ref[...]
        g = lax.dot_general(hv, wg_ref[...], _NT, preferred_element_type=f32)
        u = lax.dot_general(hv, wu_ref[...], _NT, preferred_element_type=f32)
        g_ref[...] = g.astype(bf16)
        u_ref[...] = u.astype(bf16)
        act_ref[...] = (g * jax.nn.sigmoid(g) * u).astype(bf16)

    col = pl.BlockSpec((tm, tg), lambda i, j: (i, j))
    return pl.pallas_call(
        body, name=name, grid=(s // tm, nb),
        in_specs=[pl.BlockSpec((tm, d), lambda i, j: (i, 0)), pl.BlockSpec((tg, d), lambda i, j: (j, 0)),
                  pl.BlockSpec((tg, d), lambda i, j: (j + nb, 0))],
        out_specs=[col] * 3, out_shape=[jax.ShapeDtypeStruct((s, f), bf16)] * 3, compiler_params=_params(2),
    )(h, w_t, w_t)


def _ffn_out_dgrad_swiglu(dy, w_out, g, u, name):
    s, d = dy.shape
    f = g.shape[1]
    tm = _row_tile(s, 2048)
    tg = GLU_TILE

    def body(dy_ref, w_ref, g_ref, u_ref, dg_ref, du_ref):
        dv = lax.dot_general(dy_ref[...], w_ref[...], _NT, preferred_element_type=f32)
        gv, uv = g_ref[...].astype(f32), u_ref[...].astype(f32)
        sg = jax.nn.sigmoid(gv)
        dg_ref[...] = (dv * uv * (sg * (1.0 + gv * (1.0 - sg)))).astype(bf16)
        du_ref[...] = (dv * (gv * sg)).astype(bf16)

    col = pl.BlockSpec((tm, tg), lambda i, j: (i, j))
    return pl.pallas_call(
        body, name=name, grid=(s // tm, f // tg),
        in_specs=[pl.BlockSpec((tm, d), lambda i, j: (i, 0)), pl.BlockSpec((tg, d), lambda i, j: (j, 0)), col, col],
        out_specs=[col] * 2, out_shape=[jax.ShapeDtypeStruct((s, f), bf16)] * 2, compiler_params=_params(2),
    )(dy, w_out, g, u)


def _wgrad_stack(parts, h, name):
    s, m = parts[0].shape
    d = h.shape[1]
    tm = 256
    nb = m // tm
    n = len(parts)

    def body(*refs):
        i = pl.program_id(0)
        for p in range(n):
            @pl.when(i // nb == p)
            def _(p=p):
                refs[n + 1][...] = lax.dot_general(refs[p][...], refs[n][...], _TN, preferred_element_type=f32).astype(bf16)

    a_specs = [pl.BlockSpec((s, tm), lambda i, p=p: (0, jnp.clip(i - p * nb, 0, nb - 1))) for p in range(n)]
    return pl.pallas_call(
        body, name=name, grid=(n * nb,), in_specs=a_specs + [pl.BlockSpec((s, d), lambda i: (0, 0))],
        out_specs=pl.BlockSpec((tm, d), lambda i: (i, 0)),
        out_shape=jax.ShapeDtypeStruct((n * m, d), bf16), compiler_params=_params(1),
    )(*parts, h)


def _ada_fwd(c_all, w, b, name):
    def body(c_ref, w_ref, b_ref, o_ref):
        o_ref[...] = jnp.dot(c_ref[...].astype(bf16), w_ref[...].astype(bf16), preferred_element_type=f32) + b_ref[...]

    return pl.pallas_call(
        body, name=name, out_shape=jax.ShapeDtypeStruct((c_all.shape[0], w.shape[1]), f32), compiler_params=_params(),
    )(c_all, w, b)


def _ada_wgrad(c_all, d_all, name):
    n, d = c_all.shape
    w = d_all.shape[1]

    def body(c_ref, d_ref, o_ref):
        eye = (lax.broadcasted_iota(jnp.int32, (n, n), 0) == lax.broadcasted_iota(jnp.int32, (n, n), 1)).astype(f32)
        ct = lax.dot_general(c_ref[...], eye, _TN, precision=lax.Precision.HIGHEST, preferred_element_type=f32)
        g = ct[:, 0:1] * d_ref[0:1, :]
        for bi in range(1, n):
            g = g + ct[:, bi:bi + 1] * d_ref[bi:bi + 1, :]
        o_ref[0] = g

    return pl.pallas_call(
        body, name=name, out_shape=jax.ShapeDtypeStruct((1, d, w), f32), compiler_params=_params(),
    )(c_all, d_all)


def _adamw(parts, w, m, v, name, mine=None):
    r, c = w.shape
    n_parts = parts.shape[0]
    row_tiles = [t for t in range(min(r, 256), 0, -1) if r % t == 0 and (t % 16 == 0 or t == r)]
    if row_tiles:
        tr, tc = row_tiles[0], c
    else:
        tr, tc = r, next(t for t in (256, LANES) if c % t == 0)

    def body(p_ref, *rest):
        own_ref = rest[0] if mine is not None else None
        w_ref, m_ref, v_ref, g_ref, d_ref, nm_ref, nv_ref = rest[-7:]
        if mine is not None:
            x, y, cc = _me()
            me = 4 * x + 2 * y + cc

        def part(i):
            if mine is None:
                return p_ref[i].astype(f32)
            return jnp.where(me == i, own_ref[i], p_ref[i]).astype(f32)

        g = part(0)
        for i in range(1, n_parts):
            g = g + part(i)
        mm = ADAM_B1 * m_ref[...] + (1.0 - ADAM_B1) * g
        vv = ADAM_B2 * v_ref[...] + (1.0 - ADAM_B2) * (g * g)
        m_hat = mm / (1.0 - ADAM_B1 ** ADAM_STEP)
        v_hat = vv / (1.0 - ADAM_B2 ** ADAM_STEP)
        g_ref[...] = g
        d_ref[...] = -ADAM_LR * (m_hat / (jnp.sqrt(v_hat) + ADAM_EPS) + ADAM_WD * w_ref[...])
        nm_ref[...] = mm
        nv_ref[...] = vv

    spec = pl.BlockSpec((tr, tc), lambda i, j: (i, j))
    stack = [parts] if mine is None else [parts, mine]
    return pl.pallas_call(
        body, name=name, grid=(r // tr, c // tc),
        in_specs=[pl.BlockSpec((n_parts, tr, tc), lambda i, j: (0, i, j))] * len(stack) + [spec] * 3,
        out_specs=[spec] * 4, out_shape=[jax.ShapeDtypeStruct((r, c), f32)] * 4, compiler_params=_params(2),
    )(*stack, w, m, v)


def _me():
    return lax.axis_index("x"), lax.axis_index("y"), lax.axis_index("c")


def _all_gather(arrays, name, vmem=False, after=None):
    n = len(arrays)
    space = pltpu.VMEM if vmem else pl.ANY
    extra = [] if after is None else [after]

    def body(*refs):
        ins = refs[:n]
        outs = refs[n + len(extra):2 * n + len(extra)]
        send_sems, recv_sems, local_sems = refs[2 * n + len(extra):]
        x, y, c = _me()
        me, sibling = (x, y, c), (x, y, 1 - c)
        chips = [(1 - x, y), (x, 1 - y), (1 - x, 1 - y)]

        def rows(a, dev):
            return outs[a].at[4 * dev[0] + 2 * dev[1] + dev[2]]

        def copy(a, k, block, to, src=None):
            return pltpu.make_async_remote_copy(
                src_ref=rows(a, block) if src is None else src, dst_ref=rows(a, block),
                send_sem=send_sems.at[a, k], recv_sem=recv_sems.at[a, k], device_id=to, device_id_type=MESH)

        mine = [pltpu.make_async_copy(ins[a], rows(a, me), local_sems.at[a]) for a in range(n)]
        for cp in mine:
            cp.start()
        first = []
        for a in range(n):
            first.append(copy(a, 0, me, sibling, src=ins[a]))
            first += [copy(a, 1 + j, me, (*chip, c), src=ins[a]) for j, chip in enumerate(chips)]
        for cp in first:
            cp.start()
        passed = []
        for j, chip in enumerate(chips):
            for a in range(n):
                copy(a, 1 + j, (*chip, c), me).wait_recv()
                fwd = copy(a, 4 + j, (*chip, c), sibling)
                fwd.start()
                passed.append(fwd)
        for a in range(n):
            copy(a, 0, sibling, me).wait_recv()
            for j, chip in enumerate(chips):
                copy(a, 4 + j, (*chip, 1 - c), me).wait_recv()
        for cp in first + passed:
            cp.wait_send()
        for cp in mine:
            cp.wait()

    outs = pl.pallas_call(
        body, name=name,
        in_specs=[pl.BlockSpec(memory_space=space)] * n + [pl.BlockSpec(memory_space=pl.ANY)] * len(extra),
        out_specs=[pl.BlockSpec(memory_space=space)] * n,
        out_shape=[jax.ShapeDtypeStruct((N_DEV,) + a.shape, a.dtype) for a in arrays],
        scratch_shapes=[pltpu.SemaphoreType.DMA((n, 7)), pltpu.SemaphoreType.DMA((n, 7)), pltpu.SemaphoreType.DMA((n,))],
        compiler_params=pltpu.CompilerParams(vmem_limit_bytes=VMEM_LIMIT),
    )(*arrays, *extra)
    return list(outs)


_FLIPS = ((0, 0, 1), (1, 0, 0), (0, 1, 0), (1, 1, 0), (1, 0, 1), (0, 1, 1), (1, 1, 1))
_HBM = pl.BlockSpec(memory_space=pltpu.HBM)
_SEM = pl.BlockSpec(memory_space=pltpu.SEMAPHORE)


def _exchange_copies(scatter, srcs, lands, send_sems, recv_sems):
    x, y, c = _me()
    me_row = 4 * x + 2 * y + c
    out = []
    for k, (fx, fy, fc) in enumerate(_FLIPS):
        peer = (x ^ fx, y ^ fy, c ^ fc)
        peer_row = 4 * peer[0] + 2 * peer[1] + peer[2]
        for a in range(len(srcs)):
            out.append(pltpu.make_async_remote_copy(
                src_ref=srcs[a].at[peer_row] if scatter else srcs[a], dst_ref=lands[a].at[me_row],
                send_sem=send_sems.at[7 * a + k], recv_sem=recv_sems.at[7 * a + k], device_id=peer, device_id_type=MESH))
    return out


def _exchange_start(arrays, scatter, name, after=None):
    n = len(arrays)
    lands = [lax.empty(a.shape if scatter else (N_DEV,) + a.shape, a.dtype) for a in arrays]
    extra = [] if after is None else [after]

    def body(*refs):
        srcs, zones = refs[:n], refs[n:2 * n]
        send_sems, recv_sems = refs[2 * n + len(extra)], refs[2 * n + len(extra) + 1]
        token = refs[-1]
        for cp in _exchange_copies(scatter, srcs, zones, send_sems, recv_sems):
            cp.start()
        token[...] = jnp.zeros_like(token)

    thru = [pltpu.HBM(a.shape, a.dtype) for a in list(arrays) + lands]
    outs = pl.pallas_call(
        body, name=name,
        out_shape=(pltpu.SemaphoreType.DMA((7 * n,)), pltpu.SemaphoreType.DMA((7 * n,)), *thru, jax.ShapeDtypeStruct((8, LANES), f32)),
        in_specs=[_HBM] * (2 * n) + [pl.BlockSpec(memory_space=pl.ANY)] * len(extra),
        out_specs=(_SEM, _SEM, *[_HBM] * (2 * n), pl.BlockSpec(memory_space=pltpu.VMEM)),
        input_output_aliases={i: 2 + i for i in range(2 * n)},
        compiler_params=pltpu.CompilerParams(has_side_effects=pltpu.SideEffectType.DATAFLOW_SIDE_EFFECTING),
    )(*[pltpu.with_memory_space_constraint(a, pltpu.HBM) for a in list(arrays) + lands], *extra)
    return dict(n=n, scatter=scatter, sems=outs[:2], srcs=outs[2:2 + n], lands=outs[2 + n:2 + 2 * n], token=outs[-1])


def _exchange_wait(handle, after, name):
    n, scatter = handle["n"], handle["scatter"]

    def body(*refs):
        srcs, zones = refs[:n], refs[n:2 * n]
        send_sems, recv_sems = refs[2 * n], refs[2 * n + 1]
        for cp in _exchange_copies(scatter, srcs, zones, send_sems, recv_sems):
            cp.wait_send()
            cp.wait_recv()

    thru = [pltpu.HBM(a.shape, a.dtype) for a in list(handle["srcs"]) + list(handle["lands"])]
    outs = pl.pallas_call(
        body, name=name, out_shape=tuple(thru),
        in_specs=[_HBM] * (2 * n) + [_SEM, _SEM, pl.BlockSpec(memory_space=pl.ANY)], out_specs=tuple([_HBM] * (2 * n)),
        input_output_aliases={i: i for i in range(2 * n)},
        compiler_params=pltpu.CompilerParams(has_side_effects=pltpu.SideEffectType.DATAFLOW_SIDE_EFFECTING),
    )(*handle["srcs"], *handle["lands"], *handle["sems"], after)
    return list(outs[n:])


def _cols_from_shards(g):
    return jnp.transpose(g, (1, 0, 2)).reshape(g.shape[1], -1)


def _shards_from_cols(a):
    return jnp.transpose(a.reshape(a.shape[0], N_DEV, -1), (1, 0, 2))


def _local_step(x, positions, ada, g_pre_mix, g_post_mix, b_f, sinks, g_pre_ffn, g_post_ffn, target,
                w_in_t, late_weights, on_grads):
    s, d = x.shape
    row = lambda v: v.reshape(1, -1)
    shift_m, scale_m, gate_m, shift_f, scale_f, gate_f = (ada[i:i + 1] for i in range(6))
    w_gate_t, w_qkv_t = w_in_t[F_OFF + N_HEADS:], w_in_t[:QKV_W]
    w_f_t = jnp.pad(w_in_t[F_OFF:F_OFF + N_HEADS], ((0, LANES - N_HEADS), (0, 0)))
    bf_row = jnp.pad(row(b_f), ((0, 0), (0, LANES - N_HEADS)))
    sink_rows = jnp.broadcast_to(sinks.reshape(N_HEADS, 1).astype(f32), (N_HEADS, LANES))
    inv_freq = 1.0 / (ROPE_THETA ** (jnp.arange(0, HEAD_DIM, 2, dtype=f32) / HEAD_DIM))
    cos, sin_s = _rope_tables(positions.reshape(s, 1), jnp.tile(inv_freq, 4).reshape(1, LANES), "rope_tables")

    h1, qa, ka, va, qb, kb, vb = _prenorm_proj_qkv(x, row(g_pre_mix), scale_m, shift_m, w_qkv_t, cos, sin_s, "prenorm_proj_qkv")
    gl = _matmul(h1, w_gate_t, "nt", bf16, "proj_gate")
    fl, cum_b = _forget_prep(h1, w_f_t, bf_row, "proj_forget_prep")
    o_a, lse_a = _attn_fwd(qa, ka, va, "swa_fwd", sink_rows=sink_rows, window=WINDOW, t=512)
    o_b, lse_b = _attn_fwd(qb, kb, vb, "fox_fwd", cum_b=cum_b, t=1024)
    w_branch_a, w_branch_b, w_out, w_ffn_in_t, w_ffn_out = late_weights(o_b)
    ba, bb, merged = _branch_merge(o_a, o_b, w_branch_a, w_branch_b, gl, "branch_merge")
    y1, x2, h2 = _out_proj_postnorm_prenorm(merged, w_out, x, row(g_post_mix), gate_m, row(g_pre_ffn), scale_f, shift_f,
                                            "out_proj_norms")

    g_ff, u_ff, act = _ffn_in_swiglu(h2, w_ffn_in_t, "ffn_in_swiglu")
    loss_row, d_out, d_y2, vec_pf = _out_proj_loss_tail(act, w_ffn_out, x2, row(g_post_ffn), gate_f, target, "ffn_out_loss_tail")

    g_w_ffn_out = _matmul(act, d_y2, "tn", bf16, "ffn_out_wgrad")
    dg_ff, du_ff = _ffn_out_dgrad_swiglu(d_y2, w_ffn_out, g_ff, u_ff, "ffn_out_dgrad_swiglu")
    g_w_ffn_in_t = _wgrad_stack([dg_ff, du_ff], h2, "ffn_in_wgrad")
    sent = on_grads(dict(w_ffn_in=g_w_ffn_in_t, w_ffn_out=g_w_ffn_out))
    d_x2, vec_nf, d_y1, vec_pm = _dgrad_prenorm_bwd(
        [(dg_ff, w_ffn_in_t, 0), (du_ff, w_ffn_in_t, 1)], x2, row(g_pre_ffn), scale_f, d_out, "ffn_in_dgrad_norms_bwd",
        after=sent, below=(y1, row(g_post_mix), gate_m))

    g_w_out = _matmul(merged, d_y1, "tn", bf16, "out_proj_wgrad")
    d_ba, d_bb, dgl = _out_dgrad_merge_bwd(d_y1, w_out, ba, bb, gl, "out_proj_dgrad_merge_bwd")
    g_w_branch_a = _matmul(o_a, d_ba, "tn", bf16, "branch_a_wgrad")
    g_w_branch_b = _matmul(o_b, d_bb, "tn", bf16, "branch_b_wgrad")
    sent = on_grads(dict(w_out=g_w_out, w_branch_a=g_w_branch_a, w_branch_b=g_w_branch_b))
    d_oa, delta_a, d_sink = _branch_dgrad_delta(d_ba, w_branch_a, o_a, "branch_a_dgrad_delta", lse=lse_a,
                                                sink_rows=sink_rows, after=sent)
    d_ob, delta_b = _branch_dgrad_delta(d_bb, w_branch_b, o_b, "branch_b_dgrad_delta", after=sent)
    dqa_t, dka, dva = _attn_bwd(qa, ka, va, d_oa, lse_a, delta_a, "swa_bwd", window=WINDOW, t=512)
    dqb_t, dkb, dvb, dcs, rs = _attn_bwd(qb, kb, vb, d_ob, lse_b, delta_b, "fox_bwd", cum_b=cum_b, t=512)
    dqkv = _qkv_prep_bwd(dqa_t, dka, dva, dqb_t, dkb, dvb, cos, sin_s, "qkv_prep_bwd")
    dfl, vec_bf = _forget_prep_bwd(rs.reshape(N_HEADS, s), dcs, fl, bf_row, "forget_prep_bwd")
    g_w_in_t = jnp.concatenate([_matmul(dqkv, h1, "tn", bf16, "qkv_wgrad"), _matmul(dfl, h1, "tn", bf16, "forget_wgrad")[:N_HEADS],
                                _matmul(dgl, h1, "tn", bf16, "gate_wgrad")], axis=0)
    sent = on_grads(dict(w_in=g_w_in_t))
    grad_x, vec_nm = _dgrad_prenorm_bwd([(dgl, w_gate_t, 0), (dqkv, w_qkv_t, 0), (dfl, w_f_t, 0)], x, row(g_pre_mix),
                                        scale_m, d_x2, "in_proj_dgrad_prenorm_bwd", after=sent)

    d_ada = jnp.concatenate([vec_nm[0], vec_nm[1], vec_pm[0], vec_nf[0], vec_nf[1], vec_pf[0]])
    small = dict(b_ada=d_ada, g_pre_mix=vec_nm[2], g_post_mix=vec_pm[1], g_pre_ffn=vec_nf[2], g_post_ffn=vec_pf[1],
                 b_f=vec_bf[0, :N_HEADS], sinks=d_sink[:, 0], loss=loss_row[0, :1])
    return grad_x, small


_SMALL = (("b_ada", 6144), ("g_pre_mix", 1024), ("g_post_mix", 1024), ("g_pre_ffn", 1024), ("g_post_ffn", 1024),
          ("b_f", 128), ("sinks", 128), ("loss", 128))
_SMALL_ROWS = 88


def _pack_small(vals):
    parts = [jnp.pad(vals[k].reshape(-1).astype(f32), (0, n - vals[k].size)) for k, n in _SMALL]
    flat = jnp.concatenate(parts)
    return jnp.pad(flat, (0, _SMALL_ROWS * LANES - flat.size)).reshape(_SMALL_ROWS, LANES)


def _unpack_small(slab, shapes):
    flat, out, off = slab.reshape(-1), {}, 0
    for k, n in _SMALL:
        size = math.prod(shapes[k])
        out[k] = flat[off:off + size].reshape(shapes[k])
        off += n
    return out


def kernel(x, c, positions, w_ada, b_ada, g_pre_mix, g_post_mix, w_in, b_f, sinks, w_branch_a, w_branch_b, w_out, g_pre_ffn, g_post_ffn, w_ffn_in, w_ffn_out, loss_target, m_w_ada, m_b_ada, m_g_pre_mix, m_g_post_mix, m_w_in, m_b_f, m_sinks, m_w_branch_a, m_w_branch_b, m_w_out, m_g_pre_ffn, m_g_post_ffn, m_w_ffn_in, m_w_ffn_out, v_w_ada, v_b_ada, v_g_pre_mix, v_g_post_mix, v_w_in, v_b_f, v_sinks, v_w_branch_a, v_w_branch_b, v_w_out, v_g_pre_ffn, v_g_post_ffn, v_w_ffn_in, v_w_ffn_out):
    xi, yi, ci = _me()
    me = 4 * xi + 2 * yi + ci
    d = D_MODEL
    ada_w = w_ada.shape[2]

    c_all, = _all_gather([c], "gather_c", vmem=True)
    c_all = c_all.reshape(N_DEV, d)
    b_mine = lax.dynamic_slice(b_ada, (0, me * ada_w), (1, ada_w))
    ada_cols = _ada_fwd(c_all, w_ada[0], b_mine, "ada_fwd")

    transposed = ("w_in", "w_ffn_in")
    tr = lambda a: jnp.transpose(a[0])

    ada_all, g_in = _all_gather([ada_cols, tr(w_in).astype(bf16)], "gather_ada_w_in")
    ada = lax.dynamic_index_in_dim(ada_all, me, axis=1, keepdims=False).reshape(6, d)
    late = [w.astype(bf16) for w in (w_branch_a[0], w_branch_b[0], w_out[0], tr(w_ffn_in), w_ffn_out[0])]
    late_h = _exchange_start(late, False, "gather_late_start", after=g_in)

    def mine_into(zone, block):
        return lax.dynamic_update_index_in_dim(zone, block, me, 0)

    def rows_from_shards(g):
        return g.reshape(g.shape[0] * g.shape[1], g.shape[2])

    def late_weights(after):
        zones = _exchange_wait(late_h, after, "gather_late_wait")
        g_ba, g_bb, g_out, g_fi, g_fo = (mine_into(z, w) for z, w in zip(zones, late))
        return (_cols_from_shards(g_ba), _cols_from_shards(g_bb), rows_from_shards(g_out), rows_from_shards(g_fi),
                rows_from_shards(g_fo))

    row_sharded = ("w_out", "w_ffn_out") + transposed
    in_flight = []

    def on_grads(group):
        sends = [g.reshape(N_DEV, g.shape[0] // N_DEV, g.shape[1]) if nm in row_sharded else _shards_from_cols(g)
                 for nm, g in group.items()]
        handle = _exchange_start(sends, True, "scatter_start_%d" % len(in_flight))
        in_flight.append((list(group), sends, handle))
        return handle["token"]

    grad_x, small = _local_step(
        x[0], positions[0], ada + late_h["token"][0, 0], g_pre_mix[0], g_post_mix[0], b_f[0], sinks[0], g_pre_ffn[0],
        g_post_ffn[0], loss_target[0], rows_from_shards(g_in), late_weights, on_grads)

    ws = dict(w_in=(w_in, m_w_in, v_w_in), w_branch_a=(w_branch_a, m_w_branch_a, v_w_branch_a),
              w_branch_b=(w_branch_b, m_w_branch_b, v_w_branch_b), w_out=(w_out, m_w_out, v_w_out),
              w_ffn_in=(w_ffn_in, m_w_ffn_in, v_w_ffn_in), w_ffn_out=(w_ffn_out, m_w_ffn_out, v_w_ffn_out))
    res = {}

    def finish_group(gi, after):
        names, sends, handle = in_flight[gi]
        zones = _exchange_wait(handle, after, "scatter_wait_%d" % gi)
        for nm, zone, sent in zip(names, zones, sends):
            w, m, v = (tr(a) if nm in transposed else a[0] for a in ws[nm])
            out = _adamw(zone, w, m, v, "adamw_" + nm, mine=sent)
            after = out[0]
            res[nm] = [jnp.transpose(o) for o in out] if nm in transposed else out
        return after

    done = finish_group(1, finish_group(0, grad_x))

    slab_all, = _all_gather([_pack_small(small)], "gather_small", vmem=True, after=done)
    small_w = dict(b_ada=b_ada, g_pre_mix=g_pre_mix, g_post_mix=g_post_mix, g_pre_ffn=g_pre_ffn, g_post_ffn=g_post_ffn,
                   b_f=b_f, sinks=sinks, loss=jnp.zeros((1,), f32))
    small_m = dict(b_ada=m_b_ada, g_pre_mix=m_g_pre_mix, g_post_mix=m_g_post_mix, g_pre_ffn=m_g_pre_ffn,
                   g_post_ffn=m_g_post_ffn, b_f=m_b_f, sinks=m_sinks, loss=jnp.zeros((1,), f32))
    small_v = dict(b_ada=v_b_ada, g_pre_mix=v_g_pre_mix, g_post_mix=v_g_post_mix, g_pre_ffn=v_g_pre_ffn,
                   g_post_ffn=v_g_post_ffn, b_f=v_b_f, sinks=v_sinks, loss=jnp.ones((1,), f32))
    shapes = {k: small_w[k].shape for k, _ in _SMALL}
    s_out = _adamw(slab_all, _pack_small(small_w), _pack_small(small_m), _pack_small(small_v), "adamw_small")
    s_grad, s_delta, s_m, s_v = (_unpack_small(o, shapes) for o in s_out)

    d_ada_all = lax.dynamic_slice(slab_all[:, :6144 // LANES, :].reshape(N_DEV, 6144), (0, me * ada_w), (N_DEV, ada_w))
    ada_parts = _ada_wgrad(c_all, d_ada_all, "ada_wgrad")

    res["w_ada"] = _adamw(ada_parts, w_ada[0], m_w_ada[0], v_w_ada[0], "adamw_w_ada")
    finish_group(2, res["w_ada"][0])

    order = ["w_ada", "b_ada", "g_pre_mix", "g_post_mix", "w_in", "b_f", "sinks", "w_branch_a", "w_branch_b", "w_out",
             "g_pre_ffn", "g_post_ffn", "w_ffn_in", "w_ffn_out"]
    outs = [s_grad["loss"].reshape(()), grad_x[None]]
    for which, small_o in enumerate((s_grad, s_delta, s_m, s_v)):
        for nm in order:
            outs.append(res[nm][which][None] if nm in res else small_o[nm])
    return tuple(outs)
```

```python
import functools
import math

import jax
import jax.numpy as jnp
from jax import lax
from jax.experimental import pallas as pl
from jax.experimental.pallas import tpu as pltpu

f32 = jnp.float32
bf16 = jnp.bfloat16

D_MODEL = 1024
HEAD_DIM = 64
N_HEADS = 8
N_PAIRS = 4
QKV_W = 2304
GATE_W = 2048
F_OFF = 2304
IN_W = 4360
WINDOW = 128
ROPE_THETA = 10000.0
RMS_EPS = 1e-6
D_FF = 2816
N_DEV = 8
ADAM_LR, ADAM_B1, ADAM_B2, ADAM_EPS, ADAM_WD, ADAM_STEP = 0.001, 0.9, 0.999, 1e-08, 0.01, 10
NEG = -1e30
LANES = 128
VMEM_LIMIT = 48 * 1024 * 1024
MESH = pl.DeviceIdType.MESH

_NT = (((1,), (1,)), ((), ()))
_TN = (((0,), (0,)), ((), ()))


def _params(n_grid=0):
    sem = ("arbitrary",) * n_grid if n_grid else None
    return pltpu.CompilerParams(dimension_semantics=sem, vmem_limit_bytes=VMEM_LIMIT)


def _row_tile(s, want):
    t = min(s, want)
    assert s % t == 0, (s, t)
    return t


MATMUL_VMEM_BUDGET = 40 * 1024 * 1024


def _matmul_tiles(m, n, k, a_item, b_item, o_item):
    def tiles(d):
        return [t for t in range(LANES, min(d, 2048) + 1, LANES) if d % t == 0] or [d]

    best = None
    for tm in tiles(m):
        for tn in tiles(n):
            vmem = 2 * (tm * k * a_item + tn * k * b_item + tm * tn * o_item) + tm * tn * 4
            if vmem > MATMUL_VMEM_BUDGET:
                continue
            traffic = m * k * a_item + n * k * b_item * (1 if tn == n else m // tm) + m * n * o_item
            steps = (m // tm) * (n // tn)
            key = (traffic, 0, steps) if steps >= 4 else (traffic, 1, -steps)
            if best is None or key < best[0]:
                best = (key, tm, tn)
    assert best is not None, (m, n, k)
    return best[1], best[2]


def _matmul(a, b, mode, out_dtype, name, after=None):
    if mode == "nn":
        (m, k), n = a.shape, b.shape[1]
    elif mode == "nt":
        (m, k), n = a.shape, b.shape[0]
    else:
        (k, m), n = a.shape, b.shape[1]
    tm, tn = _matmul_tiles(m, n, k, a.dtype.itemsize, b.dtype.itemsize, jnp.dtype(out_dtype).itemsize)
    if mode == "nn":
        a_spec, b_spec, dims = pl.BlockSpec((tm, k), lambda i, j: (i, 0)), pl.BlockSpec((k, tn), lambda i, j: (0, j)), None
    elif mode == "nt":
        a_spec, b_spec, dims = pl.BlockSpec((tm, k), lambda i, j: (i, 0)), pl.BlockSpec((tn, k), lambda i, j: (j, 0)), _NT
    else:
        a_spec, b_spec, dims = pl.BlockSpec((k, tm), lambda i, j: (0, i)), pl.BlockSpec((k, tn), lambda i, j: (0, j)), _TN

    def body(a_ref, b_ref, *rest):
        o_ref = rest[-1]
        av, bv = a_ref[...].astype(bf16), b_ref[...].astype(bf16)
        if dims is None:
            r = jnp.dot(av, bv, preferred_element_type=f32)
        else:
            r = lax.dot_general(av, bv, dims, preferred_element_type=f32)
        o_ref[...] = r.astype(out_dtype)

    extra = [] if after is None else [after]
    return pl.pallas_call(
        body, name=name, grid=(m // tm, n // tn), in_specs=[a_spec, b_spec] + [pl.BlockSpec(memory_space=pl.ANY)] * len(extra),
        out_specs=pl.BlockSpec((tm, tn), lambda i, j: (i, j)),
        out_shape=jax.ShapeDtypeStruct((m, n), out_dtype), compiler_params=_params(2),
    )(a, b, *extra)


def _rstd(v):
    return lax.rsqrt(jnp.mean(v * v, axis=-1, keepdims=True) + RMS_EPS)


def _row_spec(tm, d):
    return pl.BlockSpec((tm, d), lambda i: (i, 0))


def _vec_spec(d, rows=1):
    return pl.BlockSpec((rows, d), lambda i: (0, 0))


def _proj_spec(a, w, tm):
    return [_row_spec(tm, a.shape[1]), pl.BlockSpec(w.shape, lambda i: (0, 0))]


def _out_proj_postnorm_prenorm(a, w, x, g_post, gate, g_pre, scale, shift, name):
    s, d = x.shape
    tm = _row_tile(s, 512)

    def body(a_ref, w_ref, x_ref, gp_ref, gate_ref, g_ref, sc_ref, sh_ref, y_ref, x2_ref, h_ref):
        yv = jnp.dot(a_ref[...], w_ref[...], preferred_element_type=f32)
        y_ref[...] = yv
        x2 = x_ref[...] + gate_ref[...] * (yv * _rstd(yv) * gp_ref[...])
        x2_ref[...] = x2
        h_ref[...] = ((x2 * _rstd(x2) * g_ref[...]) * (1.0 + sc_ref[...]) + sh_ref[...]).astype(bf16)

    return pl.pallas_call(
        body, name=name, grid=(s // tm,), in_specs=_proj_spec(a, w, tm) + [_row_spec(tm, d)] + [_vec_spec(d)] * 5,
        out_specs=[_row_spec(tm, d)] * 3,
        out_shape=[jax.ShapeDtypeStruct((s, d), f32)] * 2 + [jax.ShapeDtypeStruct((s, d), bf16)], compiler_params=_params(1),
    )(a, w, x, g_post, gate, g_pre, scale, shift)


def _rms_bwd(u, v, r):
    return r * u - v * (r * r * r) * jnp.mean(u * v, axis=-1, keepdims=True)


def _out_proj_loss_tail(a, w, x, g, gate, target, name):
    s, d = x.shape
    tm = _row_tile(s, 512)

    def body(a_ref, w_ref, x_ref, g_ref, gate_ref, t_ref, loss_ref, do_ref, dy_ref, vec_ref):
        @pl.when(pl.program_id(0) == 0)
        def _():
            loss_ref[...] = jnp.zeros_like(loss_ref)
            vec_ref[...] = jnp.zeros_like(vec_ref)
        yv = jnp.dot(a_ref[...], w_ref[...], preferred_element_type=f32)
        r = _rstd(yv)
        yn = yv * r
        err = x_ref[...] + gate_ref[...] * (yn * g_ref[...]) - t_ref[...]
        loss_ref[...] += 0.5 * jnp.sum(jnp.mean(err * err, axis=-1, keepdims=True), axis=0, keepdims=True)
        dr = err / d
        do_ref[...] = dr
        dn = dr * gate_ref[...]
        vec_ref[0:1, :] += jnp.sum(dr * (yn * g_ref[...]), axis=0, keepdims=True)
        vec_ref[1:2, :] += jnp.sum(dn * yn, axis=0, keepdims=True)
        dy_ref[...] = _rms_bwd(dn * g_ref[...], yv, r).astype(bf16)

    return pl.pallas_call(
        body, name=name, grid=(s // tm,),
        in_specs=_proj_spec(a, w, tm) + [_row_spec(tm, d)] + [_vec_spec(d)] * 2 + [_row_spec(tm, d)],
        out_specs=[_vec_spec(LANES), _row_spec(tm, d), _row_spec(tm, d), _vec_spec(d, 8)],
        out_shape=[jax.ShapeDtypeStruct((1, LANES), f32), jax.ShapeDtypeStruct((s, d), f32),
                   jax.ShapeDtypeStruct((s, d), bf16), jax.ShapeDtypeStruct((8, d), f32)],
        compiler_params=_params(1),
    )(a, w, x, g, gate, target)


def _dgrad_prenorm_bwd(terms, x, g, scale, dres, name, after=None, below=None):
    s, d = x.shape
    n = len(terms)
    k = sum(a.shape[1] for a, _, _ in terms)
    row_bytes = 2 * (2 * k) + d * (4 + 2 * 4 * 3 + (2 * 4 + 2 * 2 if below else 0))
    tm = next(t for t in (512, 256, 128) if s % t == 0 and 4 * k * d + t * row_bytes <= MATMUL_VMEM_BUDGET)
    extra = [] if after is None else [after]

    def body(*refs):
        a_refs, b_refs = refs[:n], refs[n:2 * n]
        x_ref, g_ref, sc_ref, dr_ref = refs[2 * n:2 * n + 4]
        n_in = 2 * n + 4 + (3 if below else 0) + len(extra)
        dx_ref, vec_ref = refs[n_in], refs[n_in + 1]
        if below:
            y_ref, gp_ref, gate_ref = refs[2 * n + 4:2 * n + 7]
            dy_ref, vec2_ref = refs[n_in + 2], refs[n_in + 3]

        @pl.when(pl.program_id(0) == 0)
        def _():
            vec_ref[...] = jnp.zeros_like(vec_ref)
            if below:
                vec2_ref[...] = jnp.zeros_like(vec2_ref)
        dhv = jnp.dot(a_refs[0][...], b_refs[0][...], preferred_element_type=f32)
        for i in range(1, n):
            dhv = dhv + jnp.dot(a_refs[i][...], b_refs[i][...], preferred_element_type=f32)
        xv = x_ref[...]
        r = _rstd(xv)
        xn = xv * r
        dn = dhv * (1.0 + sc_ref[...])
        vec_ref[0:1, :] += jnp.sum(dhv, axis=0, keepdims=True)
        vec_ref[1:2, :] += jnp.sum(dhv * (xn * g_ref[...]), axis=0, keepdims=True)
        vec_ref[2:3, :] += jnp.sum(dn * xn, axis=0, keepdims=True)
        dx = dr_ref[...] + _rms_bwd(dn * g_ref[...], xv, r)
        dx_ref[...] = dx
        if below:
            yv = y_ref[...]
            ry = _rstd(yv)
            yn = yv * ry
            dny = dx * gate_ref[...]
            vec2_ref[0:1, :] += jnp.sum(dx * (yn * gp_ref[...]), axis=0, keepdims=True)
            vec2_ref[1:2, :] += jnp.sum(dny * yn, axis=0, keepdims=True)
            dy_ref[...] = _rms_bwd(dny * gp_ref[...], yv, ry).astype(bf16)

    in_specs = ([_row_spec(tm, a.shape[1]) for a, _, _ in terms]
                + [pl.BlockSpec((a.shape[1], d), lambda i, r=r: (r, 0)) for a, _, r in terms]
                + [_row_spec(tm, d)] + [_vec_spec(d)] * 2 + [_row_spec(tm, d)])
    out_specs = [_row_spec(tm, d), _vec_spec(d, 8)]
    out_shape = [jax.ShapeDtypeStruct((s, d), f32), jax.ShapeDtypeStruct((8, d), f32)]
    args = [a for a, _, _ in terms] + [b for _, b, _ in terms] + [x, g, scale, dres]
    if below:
        in_specs += [_row_spec(tm, d)] + [_vec_spec(d)] * 2
        out_specs += [_row_spec(tm, d), _vec_spec(d, 8)]
        out_shape += [jax.ShapeDtypeStruct((s, d), bf16), jax.ShapeDtypeStruct((8, d), f32)]
        args += list(below)
    return pl.pallas_call(
        body, name=name, grid=(s // tm,), in_specs=in_specs + [pl.BlockSpec(memory_space=pl.ANY)] * len(extra),
        out_specs=out_specs, out_shape=out_shape, compiler_params=_params(1),
    )(*args, *extra)


def _lane():
    return lax.broadcasted_iota(jnp.int32, (1, LANES), 1)


def _rope_tables(pos_col, inv_freq, name):
    s = pos_col.shape[0]

    def body(p_ref, f_ref, cos_ref, sin_ref):
        ang = p_ref[...].astype(f32) * f_ref[...]
        first_half = (_lane() % HEAD_DIM) < HEAD_DIM // 2
        cos_ref[...] = jnp.cos(ang)
        sn = jnp.sin(ang)
        sin_ref[...] = jnp.where(first_half, -sn, sn)

    return pl.pallas_call(
        body, name=name, out_shape=[jax.ShapeDtypeStruct((s, LANES), f32)] * 2, compiler_params=_params(),
    )(pos_col, inv_freq)


def _swap_halves(v):
    first_half = (_lane() % HEAD_DIM) < HEAD_DIM // 2
    return jnp.where(first_half, pltpu.roll(v, LANES - HEAD_DIM // 2, axis=1), pltpu.roll(v, HEAD_DIM // 2, axis=1))


def _prenorm_proj_qkv(x, g, mod_scale, mod_shift, w_qkv_t, cos, sin_s, name):
    s, d = x.shape
    tm = _row_tile(s, 512)
    scale = 1.0 / math.sqrt(HEAD_DIM)

    def body(x_ref, g_ref, msc_ref, msh_ref, w_ref, c_ref, s_ref, h_ref, qa_ref, ka_ref, va_ref, qb_ref, kb_ref, vb_ref):
        xv = x_ref[...]
        h = ((xv * _rstd(xv) * g_ref[...]) * (1.0 + msc_ref[...]) + msh_ref[...]).astype(bf16)
        h_ref[...] = h
        proj = lax.dot_general(h, w_ref[...], _NT, preferred_element_type=f32)
        cs, sn = c_ref[...], s_ref[...]
        low = _lane() < HEAD_DIM

        def blk(j):
            return proj[:, j * LANES:(j + 1) * LANES]

        def rope(v):
            return v * cs + _swap_halves(v) * sn

        def expand(v):
            other = pltpu.roll(v, HEAD_DIM, axis=1)
            return jnp.where(low, v, other), jnp.where(low, other, v)

        for j in range(N_PAIRS):
            qa_ref[:, j * LANES:(j + 1) * LANES] = (rope(blk(j)) * scale).astype(bf16)
            qb_ref[:, j * LANES:(j + 1) * LANES] = (blk(6 + j) * scale).astype(bf16)
            kb_ref[:, j * LANES:(j + 1) * LANES] = blk(10 + j).astype(bf16)
            vb_ref[:, j * LANES:(j + 1) * LANES] = blk(14 + j).astype(bf16)
        k0, k1 = expand(rope(blk(4)))
        v0, v1 = expand(blk(5))
        for j in range(N_PAIRS):
            ka_ref[:, j * LANES:(j + 1) * LANES] = (k0 if j < 2 else k1).astype(bf16)
            va_ref[:, j * LANES:(j + 1) * LANES] = (v0 if j < 2 else v1).astype(bf16)

    hw = N_PAIRS * LANES
    return pl.pallas_call(
        body, name=name, grid=(s // tm,),
        in_specs=[_row_spec(tm, d)] + [_vec_spec(d)] * 3
        + [pl.BlockSpec((QKV_W, d), lambda i: (0, 0)), _row_spec(tm, LANES), _row_spec(tm, LANES)],
        out_specs=[_row_spec(tm, d)] + [_row_spec(tm, hw)] * 6,
        out_shape=[jax.ShapeDtypeStruct((s, d), bf16)] + [jax.ShapeDtypeStruct((s, hw), bf16)] * 6, compiler_params=_params(1),
    )(x, g, mod_scale, mod_shift, w_qkv_t, cos, sin_s)


def _qkv_prep_bwd(dqa_t, dka, dva, dqb_t, dkb, dvb, cos, sin_s, name):
    s = dka.shape[0]
    tm = _row_tile(s, 256)
    scale = 1.0 / math.sqrt(HEAD_DIM)
    hw = N_PAIRS * LANES
    t_spec = pl.BlockSpec((hw, tm), lambda i: (0, i))

    def body(dqa_ref, dka_ref, dva_ref, dqb_ref, dkb_ref, dvb_ref, c_ref, s_ref, o_ref):
        cs, sn = c_ref[...], s_ref[...]
        low = _lane() < HEAD_DIM

        def blk(ref, j):
            return ref[:, j * LANES:(j + 1) * LANES]

        def blk_t(ref, j):
            return ref[j * LANES:(j + 1) * LANES, :].T

        def unrope(v):
            return v * cs + _swap_halves(v * sn)

        def fold(ref):
            a, b = blk(ref, 0) + blk(ref, 1), blk(ref, 2) + blk(ref, 3)
            kv0 = a + pltpu.roll(a, HEAD_DIM, axis=1)
            kv1 = b + pltpu.roll(b, HEAD_DIM, axis=1)
            return jnp.where(low, kv0, kv1)

        for j in range(N_PAIRS):
            o_ref[:, j * LANES:(j + 1) * LANES] = (unrope(blk_t(dqa_ref, j)) * scale).astype(bf16)
            o_ref[:, (6 + j) * LANES:(7 + j) * LANES] = (blk_t(dqb_ref, j) * scale).astype(bf16)
            o_ref[:, (10 + j) * LANES:(11 + j) * LANES] = blk(dkb_ref, j).astype(bf16)
            o_ref[:, (14 + j) * LANES:(15 + j) * LANES] = blk(dvb_ref, j).astype(bf16)
        o_ref[:, 4 * LANES:5 * LANES] = unrope(fold(dka_ref)).astype(bf16)
        o_ref[:, 5 * LANES:6 * LANES] = fold(dva_ref).astype(bf16)

    return pl.pallas_call(
        body, name=name, grid=(s // tm,),
        in_specs=[t_spec, _row_spec(tm, hw), _row_spec(tm, hw), t_spec, _row_spec(tm, hw), _row_spec(tm, hw)] + [_row_spec(tm, LANES)] * 2,
        out_specs=_row_spec(tm, QKV_W), out_shape=jax.ShapeDtypeStruct((s, QKV_W), bf16), compiler_params=_params(1),
    )(dqa_t, dka, dva, dqb_t, dkb, dvb, cos, sin_s)


def _cumsum_rows(v, reverse=False):
    n = v.shape[0]
    row = lax.broadcasted_iota(jnp.int32, v.shape, 0)
    sh = 1
    while sh < n:
        if reverse:
            v = v + jnp.where(row < n - sh, pltpu.roll(v, n - sh, axis=0), 0.0)
        else:
            v = v + jnp.where(row >= sh, pltpu.roll(v, sh, axis=0), 0.0)
        sh *= 2
    return v


def _log_sigmoid(z):
    return jnp.minimum(z, 0.0) - jnp.log1p(jnp.exp(-jnp.abs(z)))


def _forget_prep(h, w_f_t, bf_row, name):
    s = h.shape[0]

    def body(h_ref, w_ref, b_ref, f_ref, cb_ref):
        fl = lax.dot_general(h_ref[...], w_ref[...], _NT, preferred_element_type=f32)
        f_ref[...] = fl
        cum = _cumsum_rows(_log_sigmoid(fl + b_ref[...]))
        for hd in range(N_HEADS):
            cb_ref[:, hd * LANES:(hd + 1) * LANES] = jnp.broadcast_to(cum[:, hd:hd + 1], (s, LANES))

    return pl.pallas_call(
        body, name=name,
        out_shape=[jax.ShapeDtypeStruct((s, LANES), f32), jax.ShapeDtypeStruct((s, N_HEADS * LANES), f32)],
        compiler_params=_params(),
    )(h, w_f_t, bf_row)


def _forget_prep_bwd(rs, dcs, fl, bf_row, name):
    s = fl.shape[0]

    def body(r_ref, c_ref, f_ref, b_ref, df_ref, db_ref):
        eye = (lax.broadcasted_iota(jnp.int32, (N_HEADS, LANES), 0) == lax.broadcasted_iota(jnp.int32, (N_HEADS, LANES), 1)).astype(f32)
        dcum = lax.dot_general(r_ref[...], eye, _TN, precision=lax.Precision.HIGHEST, preferred_element_type=f32)
        for h in range(N_HEADS):
            dcum = dcum - jnp.where(_lane() == h, jnp.sum(c_ref[:, h * LANES:(h + 1) * LANES], axis=1, keepdims=True), 0.0)
        dlf = _cumsum_rows(dcum, reverse=True)
        z = f_ref[...] + b_ref[...]
        df = jnp.where(_lane() < N_HEADS, dlf * jax.nn.sigmoid(-z), 0.0)
        df_ref[...] = df.astype(bf16)
        db_ref[...] = jnp.zeros_like(db_ref)
        db_ref[0:1, :] = jnp.sum(df, axis=0, keepdims=True)

    return pl.pallas_call(
        body, name=name,
        out_shape=[jax.ShapeDtypeStruct((s, LANES), bf16), jax.ShapeDtypeStruct((8, LANES), f32)], compiler_params=_params(),
    )(rs, dcs, fl, bf_row)


def _tile_mask(n_keys, n_queries, off, window):
    shape = (n_keys, n_queries)
    d = lax.broadcasted_iota(jnp.int32, shape, 1) - lax.broadcasted_iota(jnp.int32, shape, 0) + off
    valid = d >= 0
    return jnp.logical_and(valid, d < window) if window else valid


def _wide(v, t):
    return jnp.concatenate([v] * (t // LANES), axis=1)


def _attn_fwd(q, k, v, name, *, cum_b=None, sink_rows=None, window=None, t=256):
    s = q.shape[0]
    t = _row_tile(s, t)
    fox, has_sink = cum_b is not None, sink_rows is not None
    assert not window or (window % LANES == 0 and LANES + window <= s)

    def body(*refs):
        q_ref, k_ref, v_ref = refs[:3]
        rest = list(refs[3:])
        cb_ref = rest.pop(0) if fox else None
        sink_ref = rest.pop(0) if has_sink else None
        o_ref, lse_ref = rest
        i = pl.program_id(1)
        low = _lane() < HEAD_DIM
        top = lax.broadcasted_iota(jnp.int32, (LANES, 1), 0) < HEAD_DIM
        q2 = q_ref[...]
        zero = jnp.zeros_like(q2)
        qms = (jnp.where(low, q2, zero), jnp.where(low, zero, q2))

        def tile(k0, n_keys, off, carry, masked, queries=slice(0, t)):
            nq = queries.stop - queries.start
            kblk, vblk = k_ref[pl.ds(k0, n_keys), :], v_ref[pl.ds(k0, n_keys), :]
            valid = _tile_mask(n_keys, nq, off, window) if masked else None
            def scores(h):
                return lax.dot_general(kblk, qms[h][queries], _NT, preferred_element_type=f32)

            def softmax(h, sc):
                m, l, _ = carry[h]
                if fox:
                    sc = sc - _wide(cb_ref[pl.ds(k0, n_keys), h * LANES:(h + 1) * LANES], nq)
                if masked:
                    sc = jnp.where(valid, sc, NEG)
                m_new = jnp.maximum(m, jnp.max(sc, axis=0, keepdims=True))
                p = jnp.exp(sc - m_new)
                alpha = jnp.exp(m - m_new)
                return m_new, alpha * l + jnp.sum(p, axis=0, keepdims=True), alpha, p.astype(bf16)

            def update(h, m_new, l, alpha, p):
                return m_new, l, alpha * carry[h][2] + lax.dot_general(vblk, p, _TN, preferred_element_type=f32)

            if window:
                return tuple(update(h, *softmax(h, scores(h))) for h in range(2))
            scs = [scores(h) for h in range(2)]
            stats = [softmax(h, scs[h]) for h in range(2)]
            return tuple(update(h, *stats[h]) for h in range(2))

        def start(nq):
            if has_sink:
                return tuple((_wide(sink_ref[h:h + 1, :], nq), jnp.ones((1, nq), f32), jnp.zeros((LANES, nq), f32))
                             for h in range(2))
            return tuple((jnp.full((1, nq), NEG, f32), jnp.zeros((1, nq), f32), jnp.zeros((LANES, nq), f32)) for h in range(2))

        def finish(carry, queries):
            (m0, l0, a0), (m1, l1, a1) = carry
            o_t = jnp.where(top, a0 * (1.0 / l0), a1 * (1.0 / l1))
            o_ref[queries, :] = o_t.T.astype(bf16)
            lse_ref[0:1, queries] = m0 + jnp.log(l0)
            lse_ref[1:2, queries] = m1 + jnp.log(l1)

        if window:
            for c in range(t // LANES):
                queries = slice(c * LANES, (c + 1) * LANES)
                q0 = i * t + c * LANES
                k0 = pl.multiple_of(jnp.maximum(q0 - window, 0), LANES)
                finish(tile(k0, LANES + window, q0 - k0, start(LANES), True, queries), queries)
        else:
            carry = lax.fori_loop(0, i, lambda kb, c: tile(pl.multiple_of(kb * t, t), t, 0, c, False), start(t))
            finish(tile(pl.multiple_of(i * t, t), t, 0, carry, True), slice(0, t))

    q_spec = pl.BlockSpec((t, LANES), lambda j, i: (i, j))
    kv_spec = pl.BlockSpec((s, LANES), lambda j, i: (0, j))
    in_specs, args = [q_spec, kv_spec, kv_spec], [q, k, v]
    if fox:
        in_specs += [pl.BlockSpec((s, 2 * LANES), lambda j, i: (0, j))]
        args += [cum_b]
    if has_sink:
        in_specs += [pl.BlockSpec((None, 2, LANES), lambda j, i: (j, 0, 0))]
        args += [sink_rows.reshape(N_PAIRS, 2, LANES)]
    return pl.pallas_call(
        body, name=name, grid=(N_PAIRS, s // t), in_specs=in_specs,
        out_specs=[q_spec, pl.BlockSpec((None, 2, t), lambda j, i: (j, 0, i))],
        out_shape=[jax.ShapeDtypeStruct((s, N_PAIRS * LANES), bf16), jax.ShapeDtypeStruct((N_PAIRS, 2, s), f32)],
        compiler_params=_params(2),
    )(*args)


def _branch_dgrad_delta(db, w, o, name, *, lse=None, sink_rows=None, after=None):
    s, hw = o.shape
    tm = _row_tile(s, 512)
    has_sink = sink_rows is not None
    extra = [] if after is None else [after]

    def body(*refs):
        db_ref, w_ref, o_ref = refs[:3]
        outs = refs[3 + (2 if has_sink else 0) + len(extra):]
        do_ref, dl_ref = outs[:2]
        if has_sink:
            lse_ref, sink_ref = refs[3:5]
            ds_ref = outs[2]

            @pl.when(pl.program_id(0) == 0)
            def _():
                ds_ref[...] = jnp.zeros_like(ds_ref)
        do = lax.dot_general(db_ref[...], w_ref[...], _NT, preferred_element_type=f32).astype(bf16)
        do_ref[...] = do
        for j in range(N_PAIRS):
            cols = slice(j * LANES, (j + 1) * LANES)
            prod_t = (do[:, cols].astype(f32) * o_ref[:, cols].astype(f32)).T
            for h in range(2):
                dl = jnp.sum(prod_t[h * HEAD_DIM:(h + 1) * HEAD_DIM, :], axis=0, keepdims=True)
                dl_ref[j, h:h + 1, :] = dl
                if has_sink:
                    r = 2 * j + h
                    p_sink = jnp.exp(sink_ref[r:r + 1, 0:1] - lse_ref[j, h:h + 1, :])
                    ds_ref[r:r + 1, :] += -jnp.sum(p_sink * dl, axis=1, keepdims=True)

    rows_spec = pl.BlockSpec((N_PAIRS, 2, tm), lambda i: (0, 0, i))
    in_specs = [_row_spec(tm, db.shape[1]), pl.BlockSpec(w.shape, lambda i: (0, 0)), _row_spec(tm, hw)]
    args = [db, w, o]
    out_specs = [_row_spec(tm, hw), rows_spec]
    out_shape = [jax.ShapeDtypeStruct((s, hw), bf16), jax.ShapeDtypeStruct((N_PAIRS, 2, s), f32)]
    if has_sink:
        in_specs += [rows_spec, _vec_spec(LANES, N_HEADS)]
        args += [lse, sink_rows]
        out_specs += [_vec_spec(LANES, N_HEADS)]
        out_shape += [jax.ShapeDtypeStruct((N_HEADS, LANES), f32)]
    return pl.pallas_call(
        body, name=name, grid=(s // tm,), in_specs=in_specs + [pl.BlockSpec(memory_space=pl.ANY)] * len(extra),
        out_specs=out_specs, out_shape=out_shape, compiler_params=_params(1),
    )(*args, *extra)


def _attn_bwd(q, k, v, do, lse, delta, name, *, cum_b=None, window=None, t=256):
    s = q.shape[0]
    t = _row_tile(s, t)
    nblk = s // t
    fox = cum_b is not None
    assert not window or (window % LANES == 0 and LANES + window <= s)

    def body(*refs):
        k_ref, v_ref, q_ref, do_ref, lse_ref, dl_ref = refs[:6]
        rest = list(refs[6:])
        cb_ref = rest.pop(0) if fox else None
        dq_ref, dk_ref, dv_ref = rest[:3]
        dcs_ref, rs_ref = (rest[3], rest[4]) if fox else (None, None)
        b = pl.program_id(1)
        k0 = pl.multiple_of(b * t, t)

        @pl.when(b == 0)
        def _():
            dq_ref[...] = jnp.zeros_like(dq_ref)
            if fox:
                rs_ref[...] = jnp.zeros_like(rs_ref)

        dk_ref[...] = jnp.zeros_like(dk_ref)
        dv_ref[...] = jnp.zeros_like(dv_ref)
        if fox:
            dcs_ref[...] = jnp.zeros_like(dcs_ref)
        low = _lane() < HEAD_DIM
        top = lax.broadcasted_iota(jnp.int32, (LANES, 1), 0) < HEAD_DIM
        kblk, vblk = k_ref[...], v_ref[...]
        k_t = kblk.astype(f32).T.astype(bf16)
        cks = [_wide(cb_ref[pl.ds(k0, t), h * LANES:(h + 1) * LANES], t) for h in range(2)] if fox else None

        def tile(q0, n_queries, off, masked, keys=slice(0, t)):
            cols = pl.ds(q0, n_queries)
            q2, do2 = q_ref[cols, :], do_ref[cols, :]
            zero = jnp.zeros_like(q2)
            valid = _tile_mask(keys.stop - keys.start, n_queries, off, window) if masked else None
            dq_parts = []
            for h in range(2):
                qm = jnp.where(low, q2, zero) if h == 0 else jnp.where(low, zero, q2)
                dom = jnp.where(low, do2, zero) if h == 0 else jnp.where(low, zero, do2)
                sc = lax.dot_general(kblk[keys], qm, _NT, preferred_element_type=f32)
                if fox:
                    sc = sc - cks[h]
                if masked:
                    sc = jnp.where(valid, sc, NEG)
                p = jnp.exp(sc - lse_ref[h:h + 1, cols])
                dp = lax.dot_general(vblk[keys], dom, _NT, preferred_element_type=f32)
                ds = p * (dp - dl_ref[h:h + 1, cols])
                pb, dsb = p.astype(bf16), ds.astype(bf16)
                dv_ref[keys, :] += jnp.dot(pb, dom, preferred_element_type=f32)
                dk_ref[keys, :] += jnp.dot(dsb, qm, preferred_element_type=f32)
                dq_parts.append(jnp.dot(k_t[:, keys], dsb, preferred_element_type=f32))
                if fox:
                    dcs_ref[:, h * LANES:(h + 1) * LANES] += sum(ds[:, g * LANES:(g + 1) * LANES] for g in range(t // LANES))
                    rs_ref[h:h + 1, cols] += jnp.sum(ds, axis=0, keepdims=True)
            dq_ref[:, cols] += jnp.where(top, dq_parts[0], dq_parts[1])

        def later_block(qb, carry):
            tile(pl.multiple_of(qb * t, t), t, 0, False)
            return carry

        if window:
            for c in range(t // LANES):
                first = b * t + c * LANES
                q0 = pl.multiple_of(jnp.minimum(first, s - (LANES + window)), LANES)
                tile(q0, LANES + window, q0 - first, True, slice(c * LANES, (c + 1) * LANES))
        else:
            tile(k0, t, 0, True)
            lax.fori_loop(b + 1, nblk, later_block, 0)

    kv_spec = pl.BlockSpec((t, LANES), lambda j, b: (b, j))
    seq_spec = pl.BlockSpec((s, LANES), lambda j, b: (0, j))
    rows_spec = pl.BlockSpec((None, 2, s), lambda j, b: (j, 0, 0))
    hw = N_PAIRS * LANES
    in_specs, args = [kv_spec, kv_spec, seq_spec, seq_spec, rows_spec, rows_spec], [k, v, q, do, lse, delta]
    out_specs = [pl.BlockSpec((LANES, s), lambda j, b: (j, 0)), kv_spec, kv_spec]
    out_shape = [jax.ShapeDtypeStruct((hw, s), f32), jax.ShapeDtypeStruct((s, hw), f32), jax.ShapeDtypeStruct((s, hw), f32)]
    if fox:
        in_specs += [pl.BlockSpec((s, 2 * LANES), lambda j, b: (0, j))]
        args += [cum_b]
        out_specs += [pl.BlockSpec((t, 2 * LANES), lambda j, b: (b, j)), rows_spec]
        out_shape += [jax.ShapeDtypeStruct((s, N_HEADS * LANES), f32), jax.ShapeDtypeStruct((N_PAIRS, 2, s), f32)]
    return pl.pallas_call(
        body, name=name, grid=(N_PAIRS, nblk), in_specs=in_specs, out_specs=out_specs, out_shape=out_shape,
        compiler_params=_params(2),
    )(*args)


def _branch_merge(o_a, o_b, w_a, w_b, gl, name):
    s, k = o_a.shape
    d = w_a.shape[1]
    tm = _row_tile(s, 1024)

    def body(oa_ref, ob_ref, wa_ref, wb_ref, g_ref, ba_ref, bb_ref, m_ref):
        ba = jnp.dot(oa_ref[...], wa_ref[...], preferred_element_type=f32)
        bb = jnp.dot(ob_ref[...], wb_ref[...], preferred_element_type=f32)
        g0, g1 = jax.nn.sigmoid(g_ref[:, :d].astype(f32)), jax.nn.sigmoid(g_ref[:, d:].astype(f32))
        ba_ref[...] = ba.astype(bf16)
        bb_ref[...] = bb.astype(bf16)
        m_ref[...] = (g0 * ba + g1 * bb).astype(bf16)

    whole = pl.BlockSpec((k, d), lambda i: (0, 0))
    return pl.pallas_call(
        body, name=name, grid=(s // tm,),
        in_specs=[_row_spec(tm, k), _row_spec(tm, k), whole, whole, _row_spec(tm, 2 * d)],
        out_specs=[_row_spec(tm, d)] * 3, out_shape=[jax.ShapeDtypeStruct((s, d), bf16)] * 3, compiler_params=_params(1),
    )(o_a, o_b, w_a, w_b, gl)


def _out_dgrad_merge_bwd(dy, w_out, ba, bb, gl, name):
    s, d = ba.shape
    tm = _row_tile(s, 512)

    def body(dy_ref, w_ref, a_ref, b_ref, g_ref, da_ref, db_ref, dg_ref):
        dmv = lax.dot_general(dy_ref[...], w_ref[...], _NT, preferred_element_type=f32)
        g0, g1 = jax.nn.sigmoid(g_ref[:, :d].astype(f32)), jax.nn.sigmoid(g_ref[:, d:].astype(f32))
        da_ref[...] = (dmv * g0).astype(bf16)
        db_ref[...] = (dmv * g1).astype(bf16)
        dg_ref[:, :d] = (dmv * a_ref[...].astype(f32) * (g0 * (1.0 - g0))).astype(bf16)
        dg_ref[:, d:] = (dmv * b_ref[...].astype(f32) * (g1 * (1.0 - g1))).astype(bf16)

    return pl.pallas_call(
        body, name=name, grid=(s // tm,),
        in_specs=[_row_spec(tm, dy.shape[1]), pl.BlockSpec(w_out.shape, lambda i: (0, 0))] + [_row_spec(tm, d)] * 2
        + [_row_spec(tm, 2 * d)],
        out_specs=[_row_spec(tm, d)] * 2 + [_row_spec(tm, 2 * d)],
        out_shape=[jax.ShapeDtypeStruct((s, d), bf16)] * 2 + [jax.ShapeDtypeStruct((s, 2 * d), bf16)],
        compiler_params=_params(1),
    )(dy, w_out, ba, bb, gl)


GLU_TILE = 256


def _ffn_in_swiglu(h, w_t, name):
    s, d = h.shape
    f = w_t.shape[0] // 2
    tm = _row_tile(s, 2048)
    tg = GLU_TILE
    nb = f // tg

    def body(h_ref, wg_ref, wu_ref, g_ref, u_ref, act_ref):
        hv = h_ref[...]
        g = lax.dot_general(hv, wg_ref[...], _NT, preferred_element_type=f32)
        u = lax.dot_general(hv, wu_ref[...], _NT, preferred_element_type=f32)
        g_ref[...] = g.astype(bf16)
        u_ref[...] = u.astype(bf16)
        act_ref[...] = (g * jax.nn.sigmoid(g) * u).astype(bf16)

    col = pl.BlockSpec((tm, tg), lambda i, j: (i, j))
    return pl.pallas_call(
        body, name=name, grid=(s // tm, nb),
        in_specs=[pl.BlockSpec((tm, d), lambda i, j: (i, 0)), pl.BlockSpec((tg, d), lambda i, j: (j, 0)),
                  pl.BlockSpec((tg, d), lambda i, j: (j + nb, 0))],
        out_specs=[col] * 3, out_shape=[jax.ShapeDtypeStruct((s, f), bf16)] * 3, compiler_params=_params(2),
    )(h, w_t, w_t)


def _ffn_out_dgrad_swiglu(dy, w_out, g, u, name):
    s, d = dy.shape
    f = g.shape[1]
    tm = _row_tile(s, 2048)
    tg = GLU_TILE

    def body(dy_ref, w_ref, g_ref, u_ref, dg_ref, du_ref):
        dv = lax.dot_general(dy_ref[...], w_ref[...], _NT, preferred_element_type=f32)
        gv, uv = g_ref[...].astype(f32), u_ref[...].astype(f32)
        sg = jax.nn.sigmoid(gv)
        dg_ref[...] = (dv * uv * (sg * (1.0 + gv * (1.0 - sg)))).astype(bf16)
        du_ref[...] = (dv * (gv * sg)).astype(bf16)

    col = pl.BlockSpec((tm, tg), lambda i, j: (i, j))
    return pl.pallas_call(
        body, name=name, grid=(s // tm, f // tg),
        in_specs=[pl.BlockSpec((tm, d), lambda i, j: (i, 0)), pl.BlockSpec((tg, d), lambda i, j: (j, 0)), col, col],
        out_specs=[col] * 2, out_shape=[jax.ShapeDtypeStruct((s, f), bf16)] * 2, compiler_params=_params(2),
    )(dy, w_out, g, u)


def _wgrad_stack(parts, h, name):
    s, m = parts[0].shape
    d = h.shape[1]
    tm = 256
    nb = m // tm
    n = len(parts)

    def body(*refs):
        i = pl.program_id(0)
        for p in range(n):
            @pl.when(i // nb == p)
            def _(p=p):
                refs[n + 1][...] = lax.dot_general(refs[p][...], refs[n][...], _TN, preferred_element_type=f32).astype(bf16)

    a_specs = [pl.BlockSpec((s, tm), lambda i, p=p: (0, jnp.clip(i - p * nb, 0, nb - 1))) for p in range(n)]
    return pl.pallas_call(
        body, name=name, grid=(n * nb,), in_specs=a_specs + [pl.BlockSpec((s, d), lambda i: (0, 0))],
        out_specs=pl.BlockSpec((tm, d), lambda i: (i, 0)),
        out_shape=jax.ShapeDtypeStruct((n * m, d), bf16), compiler_params=_params(1),
    )(*parts, h)


def _ada_fwd(c_all, w, b, name):
    def body(c_ref, w_ref, b_ref, o_ref):
        o_ref[...] = jnp.dot(c_ref[...].astype(bf16), w_ref[...].astype(bf16), preferred_element_type=f32) + b_ref[...]

    return pl.pallas_call(
        body, name=name, out_shape=jax.ShapeDtypeStruct((c_all.shape[0], w.shape[1]), f32), compiler_params=_params(),
    )(c_all, w, b)


def _ada_wgrad(c_all, d_all, name):
    n, d = c_all.shape
    w = d_all.shape[1]

    def body(c_ref, d_ref, o_ref):
        eye = (lax.broadcasted_iota(jnp.int32, (n, n), 0) == lax.broadcasted_iota(jnp.int32, (n, n), 1)).astype(f32)
        ct = lax.dot_general(c_ref[...], eye, _TN, precision=lax.Precision.HIGHEST, preferred_element_type=f32)
        g = ct[:, 0:1] * d_ref[0:1, :]
        for bi in range(1, n):
            g = g + ct[:, bi:bi + 1] * d_ref[bi:bi + 1, :]
        o_ref[0] = g

    return pl.pallas_call(
        body, name=name, out_shape=jax.ShapeDtypeStruct((1, d, w), f32), compiler_params=_params(),
    )(c_all, d_all)


def _adamw(parts, w, m, v, name, mine=None):
    r, c = w.shape
    n_parts = parts.shape[0]
    row_tiles = [t for t in range(min(r, 256), 0, -1) if r % t == 0 and (t % 16 == 0 or t == r)]
    if row_tiles:
        tr, tc = row_tiles[0], c
    else:
        tr, tc = r, next(t for t in (256, LANES) if c % t == 0)

    def body(p_ref, *rest):
        own_ref = rest[0] if mine is not None else None
        w_ref, m_ref, v_ref, g_ref, d_ref, nm_ref, nv_ref = rest[-7:]
        if mine is not None:
            x, y, cc = _me()
            me = 4 * x + 2 * y + cc

        def part(i):
            if mine is None:
                return p_ref[i].astype(f32)
            return jnp.where(me == i, own_ref[i], p_ref[i]).astype(f32)

        g = part(0)
        for i in range(1, n_parts):
            g = g + part(i)
        mm = ADAM_B1 * m_ref[...] + (1.0 - ADAM_B1) * g
        vv = ADAM_B2 * v_ref[...] + (1.0 - ADAM_B2) * (g * g)
        m_hat = mm / (1.0 - ADAM_B1 ** ADAM_STEP)
        v_hat = vv / (1.0 - ADAM_B2 ** ADAM_STEP)
        g_ref[...] = g
        d_ref[...] = -ADAM_LR * (m_hat / (jnp.sqrt(v_hat) + ADAM_EPS) + ADAM_WD * w_ref[...])
        nm_ref[...] = mm
        nv_ref[...] = vv

    spec = pl.BlockSpec((tr, tc), lambda i, j: (i, j))
    stack = [parts] if mine is None else [parts, mine]
    return pl.pallas_call(
        body, name=name, grid=(r // tr, c // tc),
        in_specs=[pl.BlockSpec((n_parts, tr, tc), lambda i, j: (0, i, j))] * len(stack) + [spec] * 3,
        out_specs=[spec] * 4, out_shape=[jax.ShapeDtypeStruct((r, c), f32)] * 4, compiler_params=_params(2),
    )(*stack, w, m, v)


def _me():
    return lax.axis_index("x"), lax.axis_index("y"), lax.axis_index("c")


def _all_gather(arrays, name, vmem=False, after=None):
    n = len(arrays)
    space = pltpu.VMEM if vmem else pl.ANY
    extra = [] if after is None else [after]

    def body(*refs):
        ins = refs[:n]
        outs = refs[n + len(extra):2 * n + len(extra)]
        send_sems, recv_sems, local_sems = refs[2 * n + len(extra):]
        x, y, c = _me()
        me, sibling = (x, y, c), (x, y, 1 - c)
        chips = [(1 - x, y), (x, 1 - y), (1 - x, 1 - y)]

        def rows(a, dev):
            return outs[a].at[4 * dev[0] + 2 * dev[1] + dev[2]]

        def copy(a, k, block, to, src=None):
            return pltpu.make_async_remote_copy(
                src_ref=rows(a, block) if src is None else src, dst_ref=rows(a, block),
                send_sem=send_sems.at[a, k], recv_sem=recv_sems.at[a, k], device_id=to, device_id_type=MESH)

        mine = [pltpu.make_async_copy(ins[a], rows(a, me), local_sems.at[a]) for a in range(n)]
        for cp in mine:
            cp.start()
        first = []
        for a in range(n):
            first.append(copy(a, 0, me, sibling, src=ins[a]))
            first += [copy(a, 1 + j, me, (*chip, c), src=ins[a]) for j, chip in enumerate(chips)]
        for cp in first:
            cp.start()
        passed = []
        for j, chip in enumerate(chips):
            for a in range(n):
                copy(a, 1 + j, (*chip, c), me).wait_recv()
                fwd = copy(a, 4 + j, (*chip, c), sibling)
                fwd.start()
                passed.append(fwd)
        for a in range(n):
            copy(a, 0, sibling, me).wait_recv()
            for j, chip in enumerate(chips):
                copy(a, 4 + j, (*chip, 1 - c), me).wait_recv()
        for cp in first + passed:
            cp.wait_send()
        for cp in mine:
            cp.wait()

    outs = pl.pallas_call(
        body, name=name,
        in_specs=[pl.BlockSpec(memory_space=space)] * n + [pl.BlockSpec(memory_space=pl.ANY)] * len(extra),
        out_specs=[pl.BlockSpec(memory_space=space)] * n,
        out_shape=[jax.ShapeDtypeStruct((N_DEV,) + a.shape, a.dtype) for a in arrays],
        scratch_shapes=[pltpu.SemaphoreType.DMA((n, 7)), pltpu.SemaphoreType.DMA((n, 7)), pltpu.SemaphoreType.DMA((n,))],
        compiler_params=pltpu.CompilerParams(vmem_limit_bytes=VMEM_LIMIT),
    )(*arrays, *extra)
    return list(outs)


_FLIPS = ((0, 0, 1), (1, 0, 0), (0, 1, 0), (1, 1, 0), (1, 0, 1), (0, 1, 1), (1, 1, 1))
_HBM = pl.BlockSpec(memory_space=pltpu.HBM)
_SEM = pl.BlockSpec(memory_space=pltpu.SEMAPHORE)


def _exchange_copies(scatter, srcs, lands, send_sems, recv_sems):
    x, y, c = _me()
    me_row = 4 * x + 2 * y + c
    out = []
    for k, (fx, fy, fc) in enumerate(_FLIPS):
        peer = (x ^ fx, y ^ fy, c ^ fc)
        peer_row = 4 * peer[0] + 2 * peer[1] + peer[2]
        for a in range(len(srcs)):
            out.append(pltpu.make_async_remote_copy(
                src_ref=srcs[a].at[peer_row] if scatter else srcs[a], dst_ref=lands[a].at[me_row],
                send_sem=send_sems.at[7 * a + k], recv_sem=recv_sems.at[7 * a + k], device_id=peer, device_id_type=MESH))
    return out


def _exchange_start(arrays, scatter, name, after=None):
    n = len(arrays)
    lands = [lax.empty(a.shape if scatter else (N_DEV,) + a.shape, a.dtype) for a in arrays]
    extra = [] if after is None else [after]

    def body(*refs):
        srcs, zones = refs[:n], refs[n:2 * n]
        send_sems, recv_sems = refs[2 * n + len(extra)], refs[2 * n + len(extra) + 1]
        token = refs[-1]
        for cp in _exchange_copies(scatter, srcs, zones, send_sems, recv_sems):
            cp.start()
        token[...] = jnp.zeros_like(token)

    thru = [pltpu.HBM(a.shape, a.dtype) for a in list(arrays) + lands]
    outs = pl.pallas_call(
        body, name=name,
        out_shape=(pltpu.SemaphoreType.DMA((7 * n,)), pltpu.SemaphoreType.DMA((7 * n,)), *thru, jax.ShapeDtypeStruct((8, LANES), f32)),
        in_specs=[_HBM] * (2 * n) + [pl.BlockSpec(memory_space=pl.ANY)] * len(extra),
        out_specs=(_SEM, _SEM, *[_HBM] * (2 * n), pl.BlockSpec(memory_space=pltpu.VMEM)),
        input_output_aliases={i: 2 + i for i in range(2 * n)},
        compiler_params=pltpu.CompilerParams(has_side_effects=pltpu.SideEffectType.DATAFLOW_SIDE_EFFECTING),
    )(*[pltpu.with_memory_space_constraint(a, pltpu.HBM) for a in list(arrays) + lands], *extra)
    return dict(n=n, scatter=scatter, sems=outs[:2], srcs=outs[2:2 + n], lands=outs[2 + n:2 + 2 * n], token=outs[-1])


def _exchange_wait(handle, after, name):
    n, scatter = handle["n"], handle["scatter"]

    def body(*refs):
        srcs, zones = refs[:n], refs[n:2 * n]
        send_sems, recv_sems = refs[2 * n], refs[2 * n + 1]
        for cp in _exchange_copies(scatter, srcs, zones, send_sems, recv_sems):
            cp.wait_send()
            cp.wait_recv()

    thru = [pltpu.HBM(a.shape, a.dtype) for a in list(handle["srcs"]) + list(handle["lands"])]
    outs = pl.pallas_call(
        body, name=name, out_shape=tuple(thru),
        in_specs=[_HBM] * (2 * n) + [_SEM, _SEM, pl.BlockSpec(memory_space=pl.ANY)], out_specs=tuple([_HBM] * (2 * n)),
        input_output_aliases={i: i for i in range(2 * n)},
        compiler_params=pltpu.CompilerParams(has_side_effects=pltpu.SideEffectType.DATAFLOW_SIDE_EFFECTING),
    )(*handle["srcs"], *handle["lands"], *handle["sems"], after)
    return list(outs[n:])


def _cols_from_shards(g):
    return jnp.transpose(g, (1, 0, 2)).reshape(g.shape[1], -1)


def _shards_from_cols(a):
    return jnp.transpose(a.reshape(a.shape[0], N_DEV, -1), (1, 0, 2))


def _local_step(x, positions, ada, g_pre_mix, g_post_mix, b_f, sinks, g_pre_ffn, g_post_ffn, target,
                w_in_t, mix_weights, ffn_weights, on_grads):
    s, d = x.shape
    row = lambda v: v.reshape(1, -1)
    shift_m, scale_m, gate_m, shift_f, scale_f, gate_f = (ada[i:i + 1] for i in range(6))
    w_gate_t, w_qkv_t = w_in_t[F_OFF + N_HEADS:], w_in_t[:QKV_W]
    w_f_t = jnp.pad(w_in_t[F_OFF:F_OFF + N_HEADS], ((0, LANES - N_HEADS), (0, 0)))
    bf_row = jnp.pad(row(b_f), ((0, 0), (0, LANES - N_HEADS)))
    sink_rows = jnp.broadcast_to(sinks.reshape(N_HEADS, 1).astype(f32), (N_HEADS, LANES))
    inv_freq = 1.0 / (ROPE_THETA ** (jnp.arange(0, HEAD_DIM, 2, dtype=f32) / HEAD_DIM))
    cos, sin_s = _rope_tables(positions.reshape(s, 1), jnp.tile(inv_freq, 4).reshape(1, LANES), "rope_tables")

    h1, qa, ka, va, qb, kb, vb = _prenorm_proj_qkv(x, row(g_pre_mix), scale_m, shift_m, w_qkv_t, cos, sin_s, "prenorm_proj_qkv")
    gl = _matmul(h1, w_gate_t, "nt", bf16, "proj_gate")
    fl, cum_b = _forget_prep(h1, w_f_t, bf_row, "proj_forget_prep")
    o_a, lse_a = _attn_fwd(qa, ka, va, "swa_fwd", sink_rows=sink_rows, window=WINDOW, t=512)
    o_b, lse_b = _attn_fwd(qb, kb, vb, "fox_fwd", cum_b=cum_b, t=1024)
    everything_before = (gl[:8, :LANES] + o_a[:8, :LANES] + o_b[:8, :LANES]).astype(f32)
    w_branch_a, w_branch_b, w_out = mix_weights(everything_before)
    ba, bb, merged = _branch_merge(o_a, o_b, w_branch_a, w_branch_b, gl, "branch_merge")
    y1, x2, h2 = _out_proj_postnorm_prenorm(merged, w_out, x, row(g_post_mix), gate_m, row(g_pre_ffn), scale_f, shift_f,
                                            "out_proj_norms")

    w_ffn_in_t, w_ffn_out = ffn_weights(h2)
    g_ff, u_ff, act = _ffn_in_swiglu(h2, w_ffn_in_t, "ffn_in_swiglu")
    loss_row, d_out, d_y2, vec_pf = _out_proj_loss_tail(act, w_ffn_out, x2, row(g_post_ffn), gate_f, target, "ffn_out_loss_tail")

    g_w_ffn_out = _matmul(act, d_y2, "tn", bf16, "ffn_out_wgrad")
    dg_ff, du_ff = _ffn_out_dgrad_swiglu(d_y2, w_ffn_out, g_ff, u_ff, "ffn_out_dgrad_swiglu")
    g_w_ffn_in_t = _wgrad_stack([dg_ff, du_ff], h2, "ffn_in_wgrad")
    sent = on_grads(dict(w_ffn_in=g_w_ffn_in_t, w_ffn_out=g_w_ffn_out))
    d_x2, vec_nf, d_y1, vec_pm = _dgrad_prenorm_bwd(
        [(dg_ff, w_ffn_in_t, 0), (du_ff, w_ffn_in_t, 1)], x2, row(g_pre_ffn), scale_f, d_out, "ffn_in_dgrad_norms_bwd",
        after=sent, below=(y1, row(g_post_mix), gate_m))

    g_w_out = _matmul(merged, d_y1, "tn", bf16, "out_proj_wgrad")
    d_ba, d_bb, dgl = _out_dgrad_merge_bwd(d_y1, w_out, ba, bb, gl, "out_proj_dgrad_merge_bwd")
    g_w_branch_a = _matmul(o_a, d_ba, "tn", bf16, "branch_a_wgrad")
    g_w_branch_b = _matmul(o_b, d_bb, "tn", bf16, "branch_b_wgrad")
    sent = on_grads(dict(w_out=g_w_out, w_branch_a=g_w_branch_a, w_branch_b=g_w_branch_b))
    d_oa, delta_a, d_sink = _branch_dgrad_delta(d_ba, w_branch_a, o_a, "branch_a_dgrad_delta", lse=lse_a,
                                                sink_rows=sink_rows, after=sent)
    d_ob, delta_b = _branch_dgrad_delta(d_bb, w_branch_b, o_b, "branch_b_dgrad_delta", after=sent)
    dqa_t, dka, dva = _attn_bwd(qa, ka, va, d_oa, lse_a, delta_a, "swa_bwd", window=WINDOW, t=512)
    dqb_t, dkb, dvb, dcs, rs = _attn_bwd(qb, kb, vb, d_ob, lse_b, delta_b, "fox_bwd", cum_b=cum_b, t=512)
    dqkv = _qkv_prep_bwd(dqa_t, dka, dva, dqb_t, dkb, dvb, cos, sin_s, "qkv_prep_bwd")
    dfl, vec_bf = _forget_prep_bwd(rs.reshape(N_HEADS, s), dcs, fl, bf_row, "forget_prep_bwd")
    g_w_in_t = jnp.concatenate([_matmul(dqkv, h1, "tn", bf16, "qkv_wgrad"), _matmul(dfl, h1, "tn", bf16, "forget_wgrad")[:N_HEADS],
                                _matmul(dgl, h1, "tn", bf16, "gate_wgrad")], axis=0)
    sent = on_grads(dict(w_in=g_w_in_t))
    grad_x, vec_nm = _dgrad_prenorm_bwd([(dgl, w_gate_t, 0), (dqkv, w_qkv_t, 0), (dfl, w_f_t, 0)], x, row(g_pre_mix),
                                        scale_m, d_x2, "in_proj_dgrad_prenorm_bwd", after=sent)

    d_ada = jnp.concatenate([vec_nm[0], vec_nm[1], vec_pm[0], vec_nf[0], vec_nf[1], vec_pf[0]])
    small = dict(b_ada=d_ada, g_pre_mix=vec_nm[2], g_post_mix=vec_pm[1], g_pre_ffn=vec_nf[2], g_post_ffn=vec_pf[1],
                 b_f=vec_bf[0, :N_HEADS], sinks=d_sink[:, 0], loss=loss_row[0, :1])
    return grad_x, small


_SMALL = (("b_ada", 6144), ("g_pre_mix", 1024), ("g_post_mix", 1024), ("g_pre_ffn", 1024), ("g_post_ffn", 1024),
          ("b_f", 128), ("sinks", 128), ("loss", 128))
_SMALL_ROWS = 88


def _pack_small(vals):
    parts = [jnp.pad(vals[k].reshape(-1).astype(f32), (0, n - vals[k].size)) for k, n in _SMALL]
    flat = jnp.concatenate(parts)
    return jnp.pad(flat, (0, _SMALL_ROWS * LANES - flat.size)).reshape(_SMALL_ROWS, LANES)


def _unpack_small(slab, shapes):
    flat, out, off = slab.reshape(-1), {}, 0
    for k, n in _SMALL:
        size = math.prod(shapes[k])
        out[k] = flat[off:off + size].reshape(shapes[k])
        off += n
    return out


def kernel(x, c, positions, w_ada, b_ada, g_pre_mix, g_post_mix, w_in, b_f, sinks, w_branch_a, w_branch_b, w_out, g_pre_ffn, g_post_ffn, w_ffn_in, w_ffn_out, loss_target, m_w_ada, m_b_ada, m_g_pre_mix, m_g_post_mix, m_w_in, m_b_f, m_sinks, m_w_branch_a, m_w_branch_b, m_w_out, m_g_pre_ffn, m_g_post_ffn, m_w_ffn_in, m_w_ffn_out, v_w_ada, v_b_ada, v_g_pre_mix, v_g_post_mix, v_w_in, v_b_f, v_sinks, v_w_branch_a, v_w_branch_b, v_w_out, v_g_pre_ffn, v_g_post_ffn, v_w_ffn_in, v_w_ffn_out):
    xi, yi, ci = _me()
    me = 4 * xi + 2 * yi + ci
    d = D_MODEL
    ada_w = w_ada.shape[2]

    c_all, = _all_gather([c], "gather_c", vmem=True)
    c_all = c_all.reshape(N_DEV, d)
    b_mine = lax.dynamic_slice(b_ada, (0, me * ada_w), (1, ada_w))
    ada_cols = _ada_fwd(c_all, w_ada[0], b_mine, "ada_fwd")

    transposed = ("w_in", "w_ffn_in")
    tr = lambda a: jnp.transpose(a[0])

    ada_all, g_in = _all_gather([ada_cols, tr(w_in).astype(bf16)], "gather_ada_w_in")
    ada = lax.dynamic_index_in_dim(ada_all, me, axis=1, keepdims=False).reshape(6, d)
    late_mix = [w.astype(bf16) for w in (w_branch_a[0], w_branch_b[0], w_out[0])]
    late_ffn = [w.astype(bf16) for w in (tr(w_ffn_in), w_ffn_out[0])]
    mix_h = _exchange_start(late_mix, False, "gather_mix_start", after=g_in)
    ffn_h = _exchange_start(late_ffn, False, "gather_ffn_start", after=mix_h["token"])

    def mine_into(zone, block):
        return lax.dynamic_update_index_in_dim(zone, block, me, 0)

    def rows_from_shards(g):
        return g.reshape(g.shape[0] * g.shape[1], g.shape[2])

    def mix_weights(after):
        zones = _exchange_wait(mix_h, after, "gather_mix_wait")
        g_ba, g_bb, g_out = (mine_into(z, w) for z, w in zip(zones, late_mix))
        return _cols_from_shards(g_ba), _cols_from_shards(g_bb), rows_from_shards(g_out)

    def ffn_weights(after):
        zones = _exchange_wait(ffn_h, after, "gather_ffn_wait")
        g_fi, g_fo = (mine_into(z, w) for z, w in zip(zones, late_ffn))
        return rows_from_shards(g_fi), rows_from_shards(g_fo)

    row_sharded = ("w_out", "w_ffn_out") + transposed
    in_flight = []

    def on_grads(group):
        sends = [g.reshape(N_DEV, g.shape[0] // N_DEV, g.shape[1]) if nm in row_sharded else _shards_from_cols(g)
                 for nm, g in group.items()]
        handle = _exchange_start(sends, True, "scatter_start_%d" % len(in_flight))
        in_flight.append((list(group), sends, handle))
        return handle["token"]

    grad_x, small = _local_step(
        x[0], positions[0], ada + ffn_h["token"][0, 0], g_pre_mix[0], g_post_mix[0], b_f[0], sinks[0], g_pre_ffn[0],
        g_post_ffn[0], loss_target[0], rows_from_shards(g_in), mix_weights, ffn_weights, on_grads)

    ws = dict(w_in=(w_in, m_w_in, v_w_in), w_branch_a=(w_branch_a, m_w_branch_a, v_w_branch_a),
              w_branch_b=(w_branch_b, m_w_branch_b, v_w_branch_b), w_out=(w_out, m_w_out, v_w_out),
              w_ffn_in=(w_ffn_in, m_w_ffn_in, v_w_ffn_in), w_ffn_out=(w_ffn_out, m_w_ffn_out, v_w_ffn_out))
    res = {}

    def finish_group(gi, after):
        names, sends, handle = in_flight[gi]
        zones = _exchange_wait(handle, after, "scatter_wait_%d" % gi)
        for nm, zone, sent in zip(names, zones, sends):
            w, m, v = (tr(a) if nm in transposed else a[0] for a in ws[nm])
            out = _adamw(zone, w, m, v, "adamw_" + nm, mine=sent)
            after = out[0]
            res[nm] = [jnp.transpose(o) for o in out] if nm in transposed else out
        return after

    done = finish_group(1, finish_group(0, grad_x))

    slab_all, = _all_gather([_pack_small(small)], "gather_small", vmem=True, after=done)
    small_w = dict(b_ada=b_ada, g_pre_mix=g_pre_mix, g_post_mix=g_post_mix, g_pre_ffn=g_pre_ffn, g_post_ffn=g_post_ffn,
                   b_f=b_f, sinks=sinks, loss=jnp.zeros((1,), f32))
    small_m = dict(b_ada=m_b_ada, g_pre_mix=m_g_pre_mix, g_post_mix=m_g_post_mix, g_pre_ffn=m_g_pre_ffn,
                   g_post_ffn=m_g_post_ffn, b_f=m_b_f, sinks=m_sinks, loss=jnp.zeros((1,), f32))
    small_v = dict(b_ada=v_b_ada, g_pre_mix=v_g_pre_mix, g_post_mix=v_g_post_mix, g_pre_ffn=v_g_pre_ffn,
                   g_post_ffn=v_g_post_ffn, b_f=v_b_f, sinks=v_sinks, loss=jnp.ones((1,), f32))
    shapes = {k: small_w[k].shape for k, _ in _SMALL}
    s_out = _adamw(slab_all, _pack_small(small_w), _pack_small(small_m), _pack_small(small_v), "adamw_small")
    s_grad, s_delta, s_m, s_v = (_unpack_small(o, shapes) for o in s_out)

    d_ada_all = lax.dynamic_slice(slab_all[:, :6144 // LANES, :].reshape(N_DEV, 6144), (0, me * ada_w), (N_DEV, ada_w))
    ada_parts = _ada_wgrad(c_all, d_ada_all, "ada_wgrad")

    res["w_ada"] = _adamw(ada_parts, w_ada[0], m_w_ada[0], v_w_ada[0], "adamw_w_ada")
    finish_group(2, res["w_ada"][0])

    order = ["w_ada", "b_ada", "g_pre_mix", "g_post_mix", "w_in", "b_f", "sinks", "w_branch_a", "w_branch_b", "w_out",
             "g_pre_ffn", "g_post_ffn", "w_ffn_in", "w_ffn_out"]
    outs = [s_grad["loss"].reshape(()), grad_x[None]]
    for which, small_o in enumerate((s_grad, s_delta, s_m, s_v)):
        for nm in order:
            outs.append(res[nm][which][None] if nm in res else small_o[nm])
    return tuple(outs)
```

```python
import functools
import math

import jax
import jax.numpy as jnp
from jax import lax
from jax.experimental import pallas as pl
from jax.experimental.pallas import tpu as pltpu

f32 = jnp.float32
bf16 = jnp.bfloat16

D_MODEL = 1024
HEAD_DIM = 64
N_HEADS = 8
N_PAIRS = 4
QKV_W = 2304
GATE_W = 2048
F_OFF = 2304
IN_W = 4360
WINDOW = 128
ROPE_THETA = 10000.0
RMS_EPS = 1e-6
D_FF = 2816
N_DEV = 8
ADAM_LR, ADAM_B1, ADAM_B2, ADAM_EPS, ADAM_WD, ADAM_STEP = 0.001, 0.9, 0.999, 1e-08, 0.01, 10
NEG = -1e30
LANES = 128
VMEM_LIMIT = 48 * 1024 * 1024
MESH = pl.DeviceIdType.MESH

_NT = (((1,), (1,)), ((), ()))
_TN = (((0,), (0,)), ((), ()))


def _params(n_grid=0):
    sem = ("arbitrary",) * n_grid if n_grid else None
    return pltpu.CompilerParams(dimension_semantics=sem, vmem_limit_bytes=VMEM_LIMIT)


def _row_tile(s, want):
    t = min(s, want)
    assert s % t == 0, (s, t)
    return t


MATMUL_VMEM_BUDGET = 40 * 1024 * 1024


def _matmul_tiles(m, n, k, a_item, b_item, o_item):
    def tiles(d):
        return [t for t in range(LANES, min(d, 2048) + 1, LANES) if d % t == 0] or [d]

    best = None
    for tm in tiles(m):
        for tn in tiles(n):
            vmem = 2 * (tm * k * a_item + tn * k * b_item + tm * tn * o_item) + tm * tn * 4
            if vmem > MATMUL_VMEM_BUDGET:
                continue
            traffic = m * k * a_item + n * k * b_item * (1 if tn == n else m // tm) + m * n * o_item
            steps = (m // tm) * (n // tn)
            key = (traffic, 0, steps) if steps >= 4 else (traffic, 1, -steps)
            if best is None or key < best[0]:
                best = (key, tm, tn)
    assert best is not None, (m, n, k)
    return best[1], best[2]


def _matmul(a, b, mode, out_dtype, name, after=None):
    if mode == "nn":
        (m, k), n = a.shape, b.shape[1]
    elif mode == "nt":
        (m, k), n = a.shape, b.shape[0]
    else:
        (k, m), n = a.shape, b.shape[1]
    tm, tn = _matmul_tiles(m, n, k, a.dtype.itemsize, b.dtype.itemsize, jnp.dtype(out_dtype).itemsize)
    if mode == "nn":
        a_spec, b_spec, dims = pl.BlockSpec((tm, k), lambda i, j: (i, 0)), pl.BlockSpec((k, tn), lambda i, j: (0, j)), None
    elif mode == "nt":
        a_spec, b_spec, dims = pl.BlockSpec((tm, k), lambda i, j: (i, 0)), pl.BlockSpec((tn, k), lambda i, j: (j, 0)), _NT
    else:
        a_spec, b_spec, dims = pl.BlockSpec((k, tm), lambda i, j: (0, i)), pl.BlockSpec((k, tn), lambda i, j: (0, j)), _TN

    def body(a_ref, b_ref, *rest):
        o_ref = rest[-1]
        av, bv = a_ref[...].astype(bf16), b_ref[...].astype(bf16)
        if dims is None:
            r = jnp.dot(av, bv, preferred_element_type=f32)
        else:
            r = lax.dot_general(av, bv, dims, preferred_element_type=f32)
        o_ref[...] = r.astype(out_dtype)

    extra = [] if after is None else [after]
    return pl.pallas_call(
        body, name=name, grid=(m // tm, n // tn), in_specs=[a_spec, b_spec] + [pl.BlockSpec(memory_space=pl.ANY)] * len(extra),
        out_specs=pl.BlockSpec((tm, tn), lambda i, j: (i, j)),
        out_shape=jax.ShapeDtypeStruct((m, n), out_dtype), compiler_params=_params(2),
    )(a, b, *extra)


def _rstd(v):
    return lax.rsqrt(jnp.mean(v * v, axis=-1, keepdims=True) + RMS_EPS)


def _row_spec(tm, d):
    return pl.BlockSpec((tm, d), lambda i: (i, 0))


def _vec_spec(d, rows=1):
    return pl.BlockSpec((rows, d), lambda i: (0, 0))


def _proj_spec(a, w, tm):
    return [_row_spec(tm, a.shape[1]), pl.BlockSpec(w.shape, lambda i: (0, 0))]


def _out_proj_postnorm_prenorm(a, w, x, g_post, gate, g_pre, scale, shift, name):
    s, d = x.shape
    tm = _row_tile(s, 512)

    def body(a_ref, w_ref, x_ref, gp_ref, gate_ref, g_ref, sc_ref, sh_ref, y_ref, x2_ref, h_ref):
        yv = jnp.dot(a_ref[...], w_ref[...], preferred_element_type=f32)
        y_ref[...] = yv
        x2 = x_ref[...] + gate_ref[...] * (yv * _rstd(yv) * gp_ref[...])
        x2_ref[...] = x2
        h_ref[...] = ((x2 * _rstd(x2) * g_ref[...]) * (1.0 + sc_ref[...]) + sh_ref[...]).astype(bf16)

    return pl.pallas_call(
        body, name=name, grid=(s // tm,), in_specs=_proj_spec(a, w, tm) + [_row_spec(tm, d)] + [_vec_spec(d)] * 5,
        out_specs=[_row_spec(tm, d)] * 3,
        out_shape=[jax.ShapeDtypeStruct((s, d), f32)] * 2 + [jax.ShapeDtypeStruct((s, d), bf16)], compiler_params=_params(1),
    )(a, w, x, g_post, gate, g_pre, scale, shift)


def _rms_bwd(u, v, r):
    return r * u - v * (r * r * r) * jnp.mean(u * v, axis=-1, keepdims=True)


def _out_proj_loss_tail(a, w, x, g, gate, target, name):
    s, d = x.shape
    tm = _row_tile(s, 512)

    def body(a_ref, w_ref, x_ref, g_ref, gate_ref, t_ref, loss_ref, do_ref, dy_ref, vec_ref):
        @pl.when(pl.program_id(0) == 0)
        def _():
            loss_ref[...] = jnp.zeros_like(loss_ref)
            vec_ref[...] = jnp.zeros_like(vec_ref)
        yv = jnp.dot(a_ref[...], w_ref[...], preferred_element_type=f32)
        r = _rstd(yv)
        yn = yv * r
        err = x_ref[...] + gate_ref[...] * (yn * g_ref[...]) - t_ref[...]
        loss_ref[...] += 0.5 * jnp.sum(jnp.mean(err * err, axis=-1, keepdims=True), axis=0, keepdims=True)
        dr = err / d
        do_ref[...] = dr
        dn = dr * gate_ref[...]
        vec_ref[0:1, :] += jnp.sum(dr * (yn * g_ref[...]), axis=0, keepdims=True)
        vec_ref[1:2, :] += jnp.sum(dn * yn, axis=0, keepdims=True)
        dy_ref[...] = _rms_bwd(dn * g_ref[...], yv, r).astype(bf16)

    return pl.pallas_call(
        body, name=name, grid=(s // tm,),
        in_specs=_proj_spec(a, w, tm) + [_row_spec(tm, d)] + [_vec_spec(d)] * 2 + [_row_spec(tm, d)],
        out_specs=[_vec_spec(LANES), _row_spec(tm, d), _row_spec(tm, d), _vec_spec(d, 8)],
        out_shape=[jax.ShapeDtypeStruct((1, LANES), f32), jax.ShapeDtypeStruct((s, d), f32),
                   jax.ShapeDtypeStruct((s, d), bf16), jax.ShapeDtypeStruct((8, d), f32)],
        compiler_params=_params(1),
    )(a, w, x, g, gate, target)


def _dgrad_prenorm_bwd(terms, x, g, scale, dres, name, after=None, below=None):
    s, d = x.shape
    n = len(terms)
    k = sum(a.shape[1] for a, _, _ in terms)
    row_bytes = 2 * (2 * k) + d * (4 + 2 * 4 * 3 + (2 * 4 + 2 * 2 if below else 0))
    tm = next(t for t in (512, 256, 128) if s % t == 0 and 4 * k * d + t * row_bytes <= MATMUL_VMEM_BUDGET)
    extra = [] if after is None else [after]

    def body(*refs):
        a_refs, b_refs = refs[:n], refs[n:2 * n]
        x_ref, g_ref, sc_ref, dr_ref = refs[2 * n:2 * n + 4]
        n_in = 2 * n + 4 + (3 if below else 0) + len(extra)
        dx_ref, vec_ref = refs[n_in], refs[n_in + 1]
        if below:
            y_ref, gp_ref, gate_ref = refs[2 * n + 4:2 * n + 7]
            dy_ref, vec2_ref = refs[n_in + 2], refs[n_in + 3]

        @pl.when(pl.program_id(0) == 0)
        def _():
            vec_ref[...] = jnp.zeros_like(vec_ref)
            if below:
                vec2_ref[...] = jnp.zeros_like(vec2_ref)
        dhv = jnp.dot(a_refs[0][...], b_refs[0][...], preferred_element_type=f32)
        for i in range(1, n):
            dhv = dhv + jnp.dot(a_refs[i][...], b_refs[i][...], preferred_element_type=f32)
        xv = x_ref[...]
        r = _rstd(xv)
        xn = xv * r
        dn = dhv * (1.0 + sc_ref[...])
        vec_ref[0:1, :] += jnp.sum(dhv, axis=0, keepdims=True)
        vec_ref[1:2, :] += jnp.sum(dhv * (xn * g_ref[...]), axis=0, keepdims=True)
        vec_ref[2:3, :] += jnp.sum(dn * xn, axis=0, keepdims=True)
        dx = dr_ref[...] + _rms_bwd(dn * g_ref[...], xv, r)
        dx_ref[...] = dx
        if below:
            yv = y_ref[...]
            ry = _rstd(yv)
            yn = yv * ry
            dny = dx * gate_ref[...]
            vec2_ref[0:1, :] += jnp.sum(dx * (yn * gp_ref[...]), axis=0, keepdims=True)
            vec2_ref[1:2, :] += jnp.sum(dny * yn, axis=0, keepdims=True)
            dy_ref[...] = _rms_bwd(dny * gp_ref[...], yv, ry).astype(bf16)

    in_specs = ([_row_spec(tm, a.shape[1]) for a, _, _ in terms]
                + [pl.BlockSpec((a.shape[1], d), lambda i, r=r: (r, 0)) for a, _, r in terms]
                + [_row_spec(tm, d)] + [_vec_spec(d)] * 2 + [_row_spec(tm, d)])
    out_specs = [_row_spec(tm, d), _vec_spec(d, 8)]
    out_shape = [jax.ShapeDtypeStruct((s, d), f32), jax.ShapeDtypeStruct((8, d), f32)]
    args = [a for a, _, _ in terms] + [b for _, b, _ in terms] + [x, g, scale, dres]
    if below:
        in_specs += [_row_spec(tm, d)] + [_vec_spec(d)] * 2
        out_specs += [_row_spec(tm, d), _vec_spec(d, 8)]
        out_shape += [jax.ShapeDtypeStruct((s, d), bf16), jax.ShapeDtypeStruct((8, d), f32)]
        args += list(below)
    return pl.pallas_call(
        body, name=name, grid=(s // tm,), in_specs=in_specs + [pl.BlockSpec(memory_space=pl.ANY)] * len(extra),
        out_specs=out_specs, out_shape=out_shape, compiler_params=_params(1),
    )(*args, *extra)


def _lane():
    return lax.broadcasted_iota(jnp.int32, (1, LANES), 1)


def _rope_tables(pos_col, inv_freq, name):
    s = pos_col.shape[0]

    def body(p_ref, f_ref, cos_ref, sin_ref):
        ang = p_ref[...].astype(f32) * f_ref[...]
        first_half = (_lane() % HEAD_DIM) < HEAD_DIM // 2
        cos_ref[...] = jnp.cos(ang)
        sn = jnp.sin(ang)
        sin_ref[...] = jnp.where(first_half, -sn, sn)

    return pl.pallas_call(
        body, name=name, out_shape=[jax.ShapeDtypeStruct((s, LANES), f32)] * 2, compiler_params=_params(),
    )(pos_col, inv_freq)


def _swap_halves(v):
    first_half = (_lane() % HEAD_DIM) < HEAD_DIM // 2
    return jnp.where(first_half, pltpu.roll(v, LANES - HEAD_DIM // 2, axis=1), pltpu.roll(v, HEAD_DIM // 2, axis=1))


def _prenorm_proj_qkv(x, g, mod_scale, mod_shift, w_qkv_t, cos, sin_s, name):
    s, d = x.shape
    tm = _row_tile(s, 512)
    scale = 1.0 / math.sqrt(HEAD_DIM)

    def body(x_ref, g_ref, msc_ref, msh_ref, w_ref, c_ref, s_ref, h_ref, qa_ref, ka_ref, va_ref, qb_ref, kb_ref, vb_ref):
        xv = x_ref[...]
        h = ((xv * _rstd(xv) * g_ref[...]) * (1.0 + msc_ref[...]) + msh_ref[...]).astype(bf16)
        h_ref[...] = h
        proj = lax.dot_general(h, w_ref[...], _NT, preferred_element_type=f32)
        cs, sn = c_ref[...], s_ref[...]
        low = _lane() < HEAD_DIM

        def blk(j):
            return proj[:, j * LANES:(j + 1) * LANES]

        def rope(v):
            return v * cs + _swap_halves(v) * sn

        def expand(v):
            other = pltpu.roll(v, HEAD_DIM, axis=1)
            return jnp.where(low, v, other), jnp.where(low, other, v)

        for j in range(N_PAIRS):
            qa_ref[:, j * LANES:(j + 1) * LANES] = (rope(blk(j)) * scale).astype(bf16)
            qb_ref[:, j * LANES:(j + 1) * LANES] = (blk(6 + j) * scale).astype(bf16)
            kb_ref[:, j * LANES:(j + 1) * LANES] = blk(10 + j).astype(bf16)
            vb_ref[:, j * LANES:(j + 1) * LANES] = blk(14 + j).astype(bf16)
        k0, k1 = expand(rope(blk(4)))
        v0, v1 = expand(blk(5))
        for j in range(N_PAIRS):
            ka_ref[:, j * LANES:(j + 1) * LANES] = (k0 if j < 2 else k1).astype(bf16)
            va_ref[:, j * LANES:(j + 1) * LANES] = (v0 if j < 2 else v1).astype(bf16)

    hw = N_PAIRS * LANES
    return pl.pallas_call(
        body, name=name, grid=(s // tm,),
        in_specs=[_row_spec(tm, d)] + [_vec_spec(d)] * 3
        + [pl.BlockSpec((QKV_W, d), lambda i: (0, 0)), _row_spec(tm, LANES), _row_spec(tm, LANES)],
        out_specs=[_row_spec(tm, d)] + [_row_spec(tm, hw)] * 6,
        out_shape=[jax.ShapeDtypeStruct((s, d), bf16)] + [jax.ShapeDtypeStruct((s, hw), bf16)] * 6, compiler_params=_params(1),
    )(x, g, mod_scale, mod_shift, w_qkv_t, cos, sin_s)


def _qkv_prep_bwd(dqa_t, dka, dva, dqb_t, dkb, dvb, cos, sin_s, name):
    s = dka.shape[0]
    tm = _row_tile(s, 256)
    scale = 1.0 / math.sqrt(HEAD_DIM)
    hw = N_PAIRS * LANES
    t_spec = pl.BlockSpec((hw, tm), lambda i: (0, i))

    def body(dqa_ref, dka_ref, dva_ref, dqb_ref, dkb_ref, dvb_ref, c_ref, s_ref, o_ref):
        cs, sn = c_ref[...], s_ref[...]
        low = _lane() < HEAD_DIM

        def blk(ref, j):
            return ref[:, j * LANES:(j + 1) * LANES].astype(f32)

        def blk_t(ref, j):
            return ref[j * LANES:(j + 1) * LANES, :].T

        def unrope(v):
            return v * cs + _swap_halves(v * sn)

        def fold(ref):
            a, b = blk(ref, 0) + blk(ref, 1), blk(ref, 2) + blk(ref, 3)
            kv0 = a + pltpu.roll(a, HEAD_DIM, axis=1)
            kv1 = b + pltpu.roll(b, HEAD_DIM, axis=1)
            return jnp.where(low, kv0, kv1)

        for j in range(N_PAIRS):
            o_ref[:, j * LANES:(j + 1) * LANES] = (unrope(blk_t(dqa_ref, j)) * scale).astype(bf16)
            o_ref[:, (6 + j) * LANES:(7 + j) * LANES] = (blk_t(dqb_ref, j) * scale).astype(bf16)
            o_ref[:, (10 + j) * LANES:(11 + j) * LANES] = blk(dkb_ref, j).astype(bf16)
            o_ref[:, (14 + j) * LANES:(15 + j) * LANES] = blk(dvb_ref, j).astype(bf16)
        o_ref[:, 4 * LANES:5 * LANES] = unrope(fold(dka_ref)).astype(bf16)
        o_ref[:, 5 * LANES:6 * LANES] = fold(dva_ref).astype(bf16)

    return pl.pallas_call(
        body, name=name, grid=(s // tm,),
        in_specs=[t_spec, _row_spec(tm, hw), _row_spec(tm, hw), t_spec, _row_spec(tm, hw), _row_spec(tm, hw)] + [_row_spec(tm, LANES)] * 2,
        out_specs=_row_spec(tm, QKV_W), out_shape=jax.ShapeDtypeStruct((s, QKV_W), bf16), compiler_params=_params(1),
    )(dqa_t, dka, dva, dqb_t, dkb, dvb, cos, sin_s)


def _cumsum_rows(v, reverse=False):
    n = v.shape[0]
    row = lax.broadcasted_iota(jnp.int32, v.shape, 0)
    sh = 1
    while sh < n:
        if reverse:
            v = v + jnp.where(row < n - sh, pltpu.roll(v, n - sh, axis=0), 0.0)
        else:
            v = v + jnp.where(row >= sh, pltpu.roll(v, sh, axis=0), 0.0)
        sh *= 2
    return v


def _log_sigmoid(z):
    return jnp.minimum(z, 0.0) - jnp.log1p(jnp.exp(-jnp.abs(z)))


def _forget_prep(h, w_f_t, bf_row, name):
    s = h.shape[0]

    def body(h_ref, w_ref, b_ref, f_ref, cb_ref):
        fl = lax.dot_general(h_ref[...], w_ref[...], _NT, preferred_element_type=f32)
        f_ref[...] = fl
        cum = _cumsum_rows(_log_sigmoid(fl + b_ref[...]))
        for hd in range(N_HEADS):
            cb_ref[:, hd * LANES:(hd + 1) * LANES] = jnp.broadcast_to(cum[:, hd:hd + 1], (s, LANES))

    return pl.pallas_call(
        body, name=name,
        out_shape=[jax.ShapeDtypeStruct((s, LANES), f32), jax.ShapeDtypeStruct((s, N_HEADS * LANES), f32)],
        compiler_params=_params(),
    )(h, w_f_t, bf_row)


def _forget_prep_bwd(rs, dcs, fl, bf_row, name):
    s = fl.shape[0]

    def body(r_ref, c_ref, f_ref, b_ref, df_ref, db_ref):
        eye = (lax.broadcasted_iota(jnp.int32, (N_HEADS, LANES), 0) == lax.broadcasted_iota(jnp.int32, (N_HEADS, LANES), 1)).astype(f32)
        dcum = lax.dot_general(r_ref[...], eye, _TN, precision=lax.Precision.HIGHEST, preferred_element_type=f32)
        for h in range(N_HEADS):
            dcum = dcum - jnp.where(_lane() == h, jnp.sum(c_ref[:, h * LANES:(h + 1) * LANES], axis=1, keepdims=True), 0.0)
        dlf = _cumsum_rows(dcum, reverse=True)
        z = f_ref[...] + b_ref[...]
        df = jnp.where(_lane() < N_HEADS, dlf * jax.nn.sigmoid(-z), 0.0)
        df_ref[...] = df.astype(bf16)
        db_ref[...] = jnp.zeros_like(db_ref)
        db_ref[0:1, :] = jnp.sum(df, axis=0, keepdims=True)

    return pl.pallas_call(
        body, name=name,
        out_shape=[jax.ShapeDtypeStruct((s, LANES), bf16), jax.ShapeDtypeStruct((8, LANES), f32)], compiler_params=_params(),
    )(rs, dcs, fl, bf_row)


def _tile_mask(n_keys, n_queries, off, window):
    shape = (n_keys, n_queries)
    d = lax.broadcasted_iota(jnp.int32, shape, 1) - lax.broadcasted_iota(jnp.int32, shape, 0) + off
    valid = d >= 0
    return jnp.logical_and(valid, d < window) if window else valid


def _wide(v, t):
    return jnp.concatenate([v] * (t // LANES), axis=1)


def _attn_fwd(q, k, v, name, *, cum_b=None, sink_rows=None, window=None, t=256):
    s = q.shape[0]
    t = _row_tile(s, t)
    fox, has_sink = cum_b is not None, sink_rows is not None
    assert not window or (window % LANES == 0 and LANES + window <= s)

    def body(*refs):
        q_ref, k_ref, v_ref = refs[:3]
        rest = list(refs[3:])
        cb_ref = rest.pop(0) if fox else None
        sink_ref = rest.pop(0) if has_sink else None
        o_ref, lse_ref = rest
        i = pl.program_id(1)
        low = _lane() < HEAD_DIM
        top = lax.broadcasted_iota(jnp.int32, (LANES, 1), 0) < HEAD_DIM
        q2 = q_ref[...]
        zero = jnp.zeros_like(q2)
        qms = (jnp.where(low, q2, zero), jnp.where(low, zero, q2))

        def tile(k0, n_keys, off, carry, masked, queries=slice(0, t)):
            nq = queries.stop - queries.start
            kblk, vblk = k_ref[pl.ds(k0, n_keys), :], v_ref[pl.ds(k0, n_keys), :]
            valid = _tile_mask(n_keys, nq, off, window) if masked else None
            def scores(h):
                return lax.dot_general(kblk, qms[h][queries], _NT, preferred_element_type=f32)

            def softmax(h, sc):
                m, l, _ = carry[h]
                if fox:
                    sc = sc - _wide(cb_ref[pl.ds(k0, n_keys), h * LANES:(h + 1) * LANES], nq)
                if masked:
                    sc = jnp.where(valid, sc, NEG)
                m_new = jnp.maximum(m, jnp.max(sc, axis=0, keepdims=True))
                p = jnp.exp(sc - m_new)
                alpha = jnp.exp(m - m_new)
                return m_new, alpha * l + jnp.sum(p, axis=0, keepdims=True), alpha, p.astype(bf16)

            def update(h, m_new, l, alpha, p):
                return m_new, l, alpha * carry[h][2] + lax.dot_general(vblk, p, _TN, preferred_element_type=f32)

            if window:
                return tuple(update(h, *softmax(h, scores(h))) for h in range(2))
            scs = [scores(h) for h in range(2)]
            stats = [softmax(h, scs[h]) for h in range(2)]
            return tuple(update(h, *stats[h]) for h in range(2))

        def start(nq):
            if has_sink:
                return tuple((_wide(sink_ref[h:h + 1, :], nq), jnp.ones((1, nq), f32), jnp.zeros((LANES, nq), f32))
                             for h in range(2))
            return tuple((jnp.full((1, nq), NEG, f32), jnp.zeros((1, nq), f32), jnp.zeros((LANES, nq), f32)) for h in range(2))

        def finish(carry, queries):
            (m0, l0, a0), (m1, l1, a1) = carry
            o_t = jnp.where(top, a0 * (1.0 / l0), a1 * (1.0 / l1))
            o_ref[queries, :] = o_t.T.astype(bf16)
            lse_ref[0:1, queries] = m0 + jnp.log(l0)
            lse_ref[1:2, queries] = m1 + jnp.log(l1)

        if window:
            for c in range(t // LANES):
                queries = slice(c * LANES, (c + 1) * LANES)
                q0 = i * t + c * LANES
                k0 = pl.multiple_of(jnp.maximum(q0 - window, 0), LANES)
                finish(tile(k0, LANES + window, q0 - k0, start(LANES), True, queries), queries)
        else:
            carry = lax.fori_loop(0, i, lambda kb, c: tile(pl.multiple_of(kb * t, t), t, 0, c, False), start(t))
            finish(tile(pl.multiple_of(i * t, t), t, 0, carry, True), slice(0, t))

    q_spec = pl.BlockSpec((t, LANES), lambda j, i: (i, j))
    kv_spec = pl.BlockSpec((s, LANES), lambda j, i: (0, j))
    in_specs, args = [q_spec, kv_spec, kv_spec], [q, k, v]
    if fox:
        in_specs += [pl.BlockSpec((s, 2 * LANES), lambda j, i: (0, j))]
        args += [cum_b]
    if has_sink:
        in_specs += [pl.BlockSpec((None, 2, LANES), lambda j, i: (j, 0, 0))]
        args += [sink_rows.reshape(N_PAIRS, 2, LANES)]
    return pl.pallas_call(
        body, name=name, grid=(N_PAIRS, s // t), in_specs=in_specs,
        out_specs=[q_spec, pl.BlockSpec((None, 2, t), lambda j, i: (j, 0, i))],
        out_shape=[jax.ShapeDtypeStruct((s, N_PAIRS * LANES), bf16), jax.ShapeDtypeStruct((N_PAIRS, 2, s), f32)],
        compiler_params=_params(2),
    )(*args)


def _branch_dgrad_delta(db, w, o, name, *, lse=None, sink_rows=None, after=None):
    s, hw = o.shape
    tm = _row_tile(s, 512)
    has_sink = sink_rows is not None
    extra = [] if after is None else [after]

    def body(*refs):
        db_ref, w_ref, o_ref = refs[:3]
        outs = refs[3 + (2 if has_sink else 0) + len(extra):]
        do_ref, dl_ref = outs[:2]
        if has_sink:
            lse_ref, sink_ref = refs[3:5]
            ds_ref = outs[2]

            @pl.when(pl.program_id(0) == 0)
            def _():
                ds_ref[...] = jnp.zeros_like(ds_ref)
        do = lax.dot_general(db_ref[...], w_ref[...], _NT, preferred_element_type=f32).astype(bf16)
        do_ref[...] = do
        for j in range(N_PAIRS):
            cols = slice(j * LANES, (j + 1) * LANES)
            prod_t = (do[:, cols].astype(f32) * o_ref[:, cols].astype(f32)).T
            for h in range(2):
                dl = jnp.sum(prod_t[h * HEAD_DIM:(h + 1) * HEAD_DIM, :], axis=0, keepdims=True)
                dl_ref[j, h:h + 1, :] = dl
                if has_sink:
                    r = 2 * j + h
                    p_sink = jnp.exp(sink_ref[r:r + 1, 0:1] - lse_ref[j, h:h + 1, :])
                    ds_ref[r:r + 1, :] += -jnp.sum(p_sink * dl, axis=1, keepdims=True)

    rows_spec = pl.BlockSpec((N_PAIRS, 2, tm), lambda i: (0, 0, i))
    in_specs = [_row_spec(tm, db.shape[1]), pl.BlockSpec(w.shape, lambda i: (0, 0)), _row_spec(tm, hw)]
    args = [db, w, o]
    out_specs = [_row_spec(tm, hw), rows_spec]
    out_shape = [jax.ShapeDtypeStruct((s, hw), bf16), jax.ShapeDtypeStruct((N_PAIRS, 2, s), f32)]
    if has_sink:
        in_specs += [rows_spec, _vec_spec(LANES, N_HEADS)]
        args += [lse, sink_rows]
        out_specs += [_vec_spec(LANES, N_HEADS)]
        out_shape += [jax.ShapeDtypeStruct((N_HEADS, LANES), f32)]
    return pl.pallas_call(
        body, name=name, grid=(s // tm,), in_specs=in_specs + [pl.BlockSpec(memory_space=pl.ANY)] * len(extra),
        out_specs=out_specs, out_shape=out_shape, compiler_params=_params(1),
    )(*args, *extra)


def _attn_bwd(q, k, v, do, lse, delta, name, *, cum_b=None, window=None, t=256):
    s = q.shape[0]
    t = _row_tile(s, t)
    nblk = s // t
    fox = cum_b is not None
    assert not window or (window % LANES == 0 and LANES + window <= s)

    def body(*refs):
        k_ref, v_ref, q_ref, do_ref, lse_ref, dl_ref = refs[:6]
        rest = list(refs[6:])
        cb_ref = rest.pop(0) if fox else None
        dq_ref, dk_ref, dv_ref = rest[:3]
        dcs_ref, rs_ref = (rest[3], rest[4]) if fox else (None, None)
        dk_acc, dv_acc = rest[-2:]
        b = pl.program_id(1)
        k0 = pl.multiple_of(b * t, t)

        @pl.when(b == 0)
        def _():
            dq_ref[...] = jnp.zeros_like(dq_ref)
            if fox:
                rs_ref[...] = jnp.zeros_like(rs_ref)

        dk_acc[...] = jnp.zeros_like(dk_acc)
        dv_acc[...] = jnp.zeros_like(dv_acc)
        if fox:
            dcs_ref[...] = jnp.zeros_like(dcs_ref)
        low = _lane() < HEAD_DIM
        top = lax.broadcasted_iota(jnp.int32, (LANES, 1), 0) < HEAD_DIM
        kblk, vblk = k_ref[...], v_ref[...]
        k_t = kblk.astype(f32).T.astype(bf16)
        cks = [_wide(cb_ref[pl.ds(k0, t), h * LANES:(h + 1) * LANES], t) for h in range(2)] if fox else None

        def tile(q0, n_queries, off, masked, keys=slice(0, t)):
            cols = pl.ds(q0, n_queries)
            q2, do2 = q_ref[cols, :], do_ref[cols, :]
            zero = jnp.zeros_like(q2)
            valid = _tile_mask(keys.stop - keys.start, n_queries, off, window) if masked else None
            dq_parts = []
            for h in range(2):
                qm = jnp.where(low, q2, zero) if h == 0 else jnp.where(low, zero, q2)
                dom = jnp.where(low, do2, zero) if h == 0 else jnp.where(low, zero, do2)
                sc = lax.dot_general(kblk[keys], qm, _NT, preferred_element_type=f32)
                if fox:
                    sc = sc - cks[h]
                if masked:
                    sc = jnp.where(valid, sc, NEG)
                p = jnp.exp(sc - lse_ref[h:h + 1, cols])
                dp = lax.dot_general(vblk[keys], dom, _NT, preferred_element_type=f32)
                ds = p * (dp - dl_ref[h:h + 1, cols])
                pb, dsb = p.astype(bf16), ds.astype(bf16)
                dv_acc[keys, :] += jnp.dot(pb, dom, preferred_element_type=f32)
                dk_acc[keys, :] += jnp.dot(dsb, qm, preferred_element_type=f32)
                dq_parts.append(jnp.dot(k_t[:, keys], dsb, preferred_element_type=f32))
                if fox:
                    dcs_ref[:, h * LANES:(h + 1) * LANES] += sum(ds[:, g * LANES:(g + 1) * LANES] for g in range(t // LANES))
                    rs_ref[h:h + 1, cols] += jnp.sum(ds, axis=0, keepdims=True)
            dq_ref[:, cols] += jnp.where(top, dq_parts[0], dq_parts[1])

        def later_block(qb, carry):
            tile(pl.multiple_of(qb * t, t), t, 0, False)
            return carry

        if window:
            for c in range(t // LANES):
                first = b * t + c * LANES
                q0 = pl.multiple_of(jnp.minimum(first, s - (LANES + window)), LANES)
                tile(q0, LANES + window, q0 - first, True, slice(c * LANES, (c + 1) * LANES))
        else:
            tile(k0, t, 0, True)
            lax.fori_loop(b + 1, nblk, later_block, 0)
        dk_ref[...] = dk_acc[...].astype(bf16)
        dv_ref[...] = dv_acc[...].astype(bf16)

    kv_spec = pl.BlockSpec((t, LANES), lambda j, b: (b, j))
    seq_spec = pl.BlockSpec((s, LANES), lambda j, b: (0, j))
    rows_spec = pl.BlockSpec((None, 2, s), lambda j, b: (j, 0, 0))
    hw = N_PAIRS * LANES
    in_specs, args = [kv_spec, kv_spec, seq_spec, seq_spec, rows_spec, rows_spec], [k, v, q, do, lse, delta]
    out_specs = [pl.BlockSpec((LANES, s), lambda j, b: (j, 0)), kv_spec, kv_spec]
    out_shape = [jax.ShapeDtypeStruct((hw, s), f32), jax.ShapeDtypeStruct((s, hw), bf16), jax.ShapeDtypeStruct((s, hw), bf16)]
    if fox:
        in_specs += [pl.BlockSpec((s, 2 * LANES), lambda j, b: (0, j))]
        args += [cum_b]
        out_specs += [pl.BlockSpec((t, 2 * LANES), lambda j, b: (b, j)), rows_spec]
        out_shape += [jax.ShapeDtypeStruct((s, N_HEADS * LANES), f32), jax.ShapeDtypeStruct((N_PAIRS, 2, s), f32)]
    return pl.pallas_call(
        body, name=name, grid=(N_PAIRS, nblk), in_specs=in_specs, out_specs=out_specs, out_shape=out_shape,
        scratch_shapes=[pltpu.VMEM((t, LANES), f32)] * 2, compiler_params=_params(2),
    )(*args)


def _branch_merge(o_a, o_b, w_a, w_b, gl, name):
    s, k = o_a.shape
    d = w_a.shape[1]
    tm = _row_tile(s, 1024)

    def body(oa_ref, ob_ref, wa_ref, wb_ref, g_ref, ba_ref, bb_ref, m_ref):
        ba = jnp.dot(oa_ref[...], wa_ref[...], preferred_element_type=f32)
        bb = jnp.dot(ob_ref[...], wb_ref[...], preferred_element_type=f32)
        g0, g1 = jax.nn.sigmoid(g_ref[:, :d].astype(f32)), jax.nn.sigmoid(g_ref[:, d:].astype(f32))
        ba_ref[...] = ba.astype(bf16)
        bb_ref[...] = bb.astype(bf16)
        m_ref[...] = (g0 * ba + g1 * bb).astype(bf16)

    whole = pl.BlockSpec((k, d), lambda i: (0, 0))
    return pl.pallas_call(
        body, name=name, grid=(s // tm,),
        in_specs=[_row_spec(tm, k), _row_spec(tm, k), whole, whole, _row_spec(tm, 2 * d)],
        out_specs=[_row_spec(tm, d)] * 3, out_shape=[jax.ShapeDtypeStruct((s, d), bf16)] * 3, compiler_params=_params(1),
    )(o_a, o_b, w_a, w_b, gl)


def _out_dgrad_merge_bwd(dy, w_out, ba, bb, gl, name):
    s, d = ba.shape
    tm = _row_tile(s, 512)

    def body(dy_ref, w_ref, a_ref, b_ref, g_ref, da_ref, db_ref, dg_ref):
        dmv = lax.dot_general(dy_ref[...], w_ref[...], _NT, preferred_element_type=f32)
        g0, g1 = jax.nn.sigmoid(g_ref[:, :d].astype(f32)), jax.nn.sigmoid(g_ref[:, d:].astype(f32))
        da_ref[...] = (dmv * g0).astype(bf16)
        db_ref[...] = (dmv * g1).astype(bf16)
        dg_ref[:, :d] = (dmv * a_ref[...].astype(f32) * (g0 * (1.0 - g0))).astype(bf16)
        dg_ref[:, d:] = (dmv * b_ref[...].astype(f32) * (g1 * (1.0 - g1))).astype(bf16)

    return pl.pallas_call(
        body, name=name, grid=(s // tm,),
        in_specs=[_row_spec(tm, dy.shape[1]), pl.BlockSpec(w_out.shape, lambda i: (0, 0))] + [_row_spec(tm, d)] * 2
        + [_row_spec(tm, 2 * d)],
        out_specs=[_row_spec(tm, d)] * 2 + [_row_spec(tm, 2 * d)],
        out_shape=[jax.ShapeDtypeStruct((s, d), bf16)] * 2 + [jax.ShapeDtypeStruct((s, 2 * d), bf16)],
        compiler_params=_params(1),
    )(dy, w_out, ba, bb, gl)


GLU_TILE = 256


def _ffn_in_swiglu(h, w_t, name):
    s, d = h.shape
    f = w_t.shape[0] // 2
    tm = _row_tile(s, 2048)
    tg = GLU_TILE
    nb = f // tg

    def body(h_ref, wg_ref, wu_ref, g_ref, u_ref, act_ref):
        hv = h_ref[...]
        g = lax.dot_general(hv, wg_ref[...], _NT, preferred_element_type=f32)
        u = lax.dot_general(hv, wu_ref[...], _NT, preferred_element_type=f32)
        g_ref[...] = g.astype(bf16)
        u_ref[...] = u.astype(bf16)
        act_ref[...] = (g * jax.nn.sigmoid(g) * u).astype(bf16)

    col = pl.BlockSpec((tm, tg), lambda i, j: (i, j))
    return pl.pallas_call(
        body, name=name, grid=(s // tm, nb),
        in_specs=[pl.BlockSpec((tm, d), lambda i, j: (i, 0)), pl.BlockSpec((tg, d), lambda i, j: (j, 0)),
                  pl.BlockSpec((tg, d), lambda i, j: (j + nb, 0))],
        out_specs=[col] * 3, out_shape=[jax.ShapeDtypeStruct((s, f), bf16)] * 3, compiler_params=_params(2),
    )(h, w_t, w_t)


def _ffn_out_dgrad_swiglu(dy, w_out, g, u, name):
    s, d = dy.shape
    f = g.shape[1]
    tm = _row_tile(s, 2048)
    tg = GLU_TILE

    def body(dy_ref, w_ref, g_ref, u_ref, dg_ref, du_ref):
        dv = lax.dot_general(dy_ref[...], w_ref[...], _NT, preferred_element_type=f32)
        gv, uv = g_ref[...].astype(f32), u_ref[...].astype(f32)
        sg = jax.nn.sigmoid(gv)
        dg_ref[...] = (dv * uv * (sg * (1.0 + gv * (1.0 - sg)))).astype(bf16)
        du_ref[...] = (dv * (gv * sg)).astype(bf16)

    col = pl.BlockSpec((tm, tg), lambda i, j: (i, j))
    return pl.pallas_call(
        body, name=name, grid=(s // tm, f // tg),
        in_specs=[pl.BlockSpec((tm, d), lambda i, j: (i, 0)), pl.BlockSpec((tg, d), lambda i, j: (j, 0)), col, col],
        out_specs=[col] * 2, out_shape=[jax.ShapeDtypeStruct((s, f), bf16)] * 2, compiler_params=_params(2),
    )(dy, w_out, g, u)


def _wgrad_stack(parts, h, name):
    s, m = parts[0].shape
    d = h.shape[1]
    tm = 256
    nb = m // tm
    n = len(parts)

    def body(*refs):
        i = pl.program_id(0)
        for p in range(n):
            @pl.when(i // nb == p)
            def _(p=p):
                refs[n + 1][...] = lax.dot_general(refs[p][...], refs[n][...], _TN, preferred_element_type=f32).astype(bf16)

    a_specs = [pl.BlockSpec((s, tm), lambda i, p=p: (0, jnp.clip(i - p * nb, 0, nb - 1))) for p in range(n)]
    return pl.pallas_call(
        body, name=name, grid=(n * nb,), in_specs=a_specs + [pl.BlockSpec((s, d), lambda i: (0, 0))],
        out_specs=pl.BlockSpec((tm, d), lambda i: (i, 0)),
        out_shape=jax.ShapeDtypeStruct((n * m, d), bf16), compiler_params=_params(1),
    )(*parts, h)


def _ada_fwd(c_all, w, b, name):
    def body(c_ref, w_ref, b_ref, o_ref):
        o_ref[...] = jnp.dot(c_ref[...].astype(bf16), w_ref[...].astype(bf16), preferred_element_type=f32) + b_ref[...]

    return pl.pallas_call(
        body, name=name, out_shape=jax.ShapeDtypeStruct((c_all.shape[0], w.shape[1]), f32), compiler_params=_params(),
    )(c_all, w, b)


def _ada_wgrad(c_all, d_all, name):
    n, d = c_all.shape
    w = d_all.shape[1]

    def body(c_ref, d_ref, o_ref):
        eye = (lax.broadcasted_iota(jnp.int32, (n, n), 0) == lax.broadcasted_iota(jnp.int32, (n, n), 1)).astype(f32)
        ct = lax.dot_general(c_ref[...], eye, _TN, precision=lax.Precision.HIGHEST, preferred_element_type=f32)
        g = ct[:, 0:1] * d_ref[0:1, :]
        for bi in range(1, n):
            g = g + ct[:, bi:bi + 1] * d_ref[bi:bi + 1, :]
        o_ref[0] = g

    return pl.pallas_call(
        body, name=name, out_shape=jax.ShapeDtypeStruct((1, d, w), f32), compiler_params=_params(),
    )(c_all, d_all)


def _adamw(parts, w, m, v, name, mine=None, me=None):
    r, c = w.shape
    n_parts = parts.shape[0]
    row_tiles = [t for t in range(min(r, 256), 0, -1) if r % t == 0 and (t % 16 == 0 or t == r)]
    if row_tiles:
        tr, tc = row_tiles[0], c
    else:
        tr, tc = r, next(t for t in (256, LANES) if c % t == 0)

    def body(*refs):
        w_ref, m_ref, v_ref, g_ref, d_ref, nm_ref, nv_ref = refs[-7:]
        if mine is None:
            p_ref, = refs[:-7]
        else:
            me_ref, p_ref, own_ref = refs[:-7]

        def part(i):
            if mine is None:
                return p_ref[i].astype(f32)
            return jnp.where(me_ref[0] == i, own_ref[...], p_ref[i]).astype(f32)

        g = part(0)
        for i in range(1, n_parts):
            g = g + part(i)
        mm = ADAM_B1 * m_ref[...] + (1.0 - ADAM_B1) * g
        vv = ADAM_B2 * v_ref[...] + (1.0 - ADAM_B2) * (g * g)
        m_hat = mm / (1.0 - ADAM_B1 ** ADAM_STEP)
        v_hat = vv / (1.0 - ADAM_B2 ** ADAM_STEP)
        g_ref[...] = g
        d_ref[...] = -ADAM_LR * (m_hat / (jnp.sqrt(v_hat) + ADAM_EPS) + ADAM_WD * w_ref[...])
        nm_ref[...] = mm
        nv_ref[...] = vv

    out_shape = [jax.ShapeDtypeStruct((r, c), f32)] * 4
    if mine is None:
        spec = pl.BlockSpec((tr, tc), lambda i, j: (i, j))
        return pl.pallas_call(
            body, name=name, grid=(r // tr, c // tc),
            in_specs=[pl.BlockSpec((n_parts, tr, tc), lambda i, j: (0, i, j))] + [spec] * 3,
            out_specs=[spec] * 4, out_shape=out_shape, compiler_params=_params(2),
        )(parts, w, m, v)
    spec = pl.BlockSpec((tr, tc), lambda i, j, me_ref: (i, j))
    return pl.pallas_call(
        body, name=name, out_shape=out_shape, compiler_params=_params(2),
        grid_spec=pltpu.PrefetchScalarGridSpec(
            num_scalar_prefetch=1, grid=(r // tr, c // tc),
            in_specs=[pl.BlockSpec((n_parts, tr, tc), lambda i, j, me_ref: (0, i, j)),
                      pl.BlockSpec((None, tr, tc), lambda i, j, me_ref: (me_ref[0], i, j))] + [spec] * 3,
            out_specs=[spec] * 4),
    )(me, parts, mine, w, m, v)


def _me():
    return lax.axis_index("x"), lax.axis_index("y"), lax.axis_index("c")


def _all_gather(arrays, name, vmem=False, after=None):
    n = len(arrays)
    space = pltpu.VMEM if vmem else pl.ANY
    extra = [] if after is None else [after]

    def body(*refs):
        ins = refs[:n]
        outs = refs[n + len(extra):2 * n + len(extra)]
        send_sems, recv_sems, local_sems = refs[2 * n + len(extra):]
        x, y, c = _me()
        me, sibling = (x, y, c), (x, y, 1 - c)
        chips = [(1 - x, y), (x, 1 - y), (1 - x, 1 - y)]

        def rows(a, dev):
            return outs[a].at[4 * dev[0] + 2 * dev[1] + dev[2]]

        def copy(a, k, block, to, src=None):
            return pltpu.make_async_remote_copy(
                src_ref=rows(a, block) if src is None else src, dst_ref=rows(a, block),
                send_sem=send_sems.at[a, k], recv_sem=recv_sems.at[a, k], device_id=to, device_id_type=MESH)

        mine = [pltpu.make_async_copy(ins[a], rows(a, me), local_sems.at[a]) for a in range(n)]
        for cp in mine:
            cp.start()
        first = []
        for a in range(n):
            first.append(copy(a, 0, me, sibling, src=ins[a]))
            first += [copy(a, 1 + j, me, (*chip, c), src=ins[a]) for j, chip in enumerate(chips)]
        for cp in first:
            cp.start()
        passed = []
        for j, chip in enumerate(chips):
            for a in range(n):
                copy(a, 1 + j, (*chip, c), me).wait_recv()
                fwd = copy(a, 4 + j, (*chip, c), sibling)
                fwd.start()
                passed.append(fwd)
        for a in range(n):
            copy(a, 0, sibling, me).wait_recv()
            for j, chip in enumerate(chips):
                copy(a, 4 + j, (*chip, 1 - c), me).wait_recv()
        for cp in first + passed:
            cp.wait_send()
        for cp in mine:
            cp.wait()

    outs = pl.pallas_call(
        body, name=name,
        in_specs=[pl.BlockSpec(memory_space=space)] * n + [pl.BlockSpec(memory_space=pl.ANY)] * len(extra),
        out_specs=[pl.BlockSpec(memory_space=space)] * n,
        out_shape=[jax.ShapeDtypeStruct((N_DEV,) + a.shape, a.dtype) for a in arrays],
        scratch_shapes=[pltpu.SemaphoreType.DMA((n, 7)), pltpu.SemaphoreType.DMA((n, 7)), pltpu.SemaphoreType.DMA((n,))],
        compiler_params=pltpu.CompilerParams(vmem_limit_bytes=VMEM_LIMIT),
    )(*arrays, *extra)
    return list(outs)


_FLIPS = ((0, 0, 1), (1, 0, 0), (0, 1, 0), (1, 1, 0), (1, 0, 1), (0, 1, 1), (1, 1, 1))
_HBM = pl.BlockSpec(memory_space=pltpu.HBM)
_SEM = pl.BlockSpec(memory_space=pltpu.SEMAPHORE)


def _exchange_copies(scatter, srcs, lands, send_sems, recv_sems):
    x, y, c = _me()
    me_row = 4 * x + 2 * y + c
    out = []
    for k, (fx, fy, fc) in enumerate(_FLIPS):
        peer = (x ^ fx, y ^ fy, c ^ fc)
        peer_row = 4 * peer[0] + 2 * peer[1] + peer[2]
        for a in range(len(srcs)):
            out.append(pltpu.make_async_remote_copy(
                src_ref=srcs[a].at[peer_row] if scatter else srcs[a], dst_ref=lands[a].at[me_row],
                send_sem=send_sems.at[7 * a + k], recv_sem=recv_sems.at[7 * a + k], device_id=peer, device_id_type=MESH))
    return out


def _exchange_start(arrays, scatter, name, after=None):
    n = len(arrays)
    lands = [lax.empty(a.shape if scatter else (N_DEV,) + a.shape, a.dtype) for a in arrays]
    extra = [] if after is None else [after]

    def body(*refs):
        srcs, zones = refs[:n], refs[n:2 * n]
        send_sems, recv_sems = refs[2 * n + len(extra)], refs[2 * n + len(extra) + 1]
        token = refs[-1]
        for cp in _exchange_copies(scatter, srcs, zones, send_sems, recv_sems):
            cp.start()
        token[...] = jnp.zeros_like(token)

    thru = [pltpu.HBM(a.shape, a.dtype) for a in list(arrays) + lands]
    outs = pl.pallas_call(
        body, name=name,
        out_shape=(pltpu.SemaphoreType.DMA((7 * n,)), pltpu.SemaphoreType.DMA((7 * n,)), *thru, jax.ShapeDtypeStruct((8, LANES), f32)),
        in_specs=[_HBM] * (2 * n) + [pl.BlockSpec(memory_space=pl.ANY)] * len(extra),
        out_specs=(_SEM, _SEM, *[_HBM] * (2 * n), pl.BlockSpec(memory_space=pltpu.VMEM)),
        input_output_aliases={i: 2 + i for i in range(2 * n)},
        compiler_params=pltpu.CompilerParams(has_side_effects=pltpu.SideEffectType.DATAFLOW_SIDE_EFFECTING),
    )(*[pltpu.with_memory_space_constraint(a, pltpu.HBM) for a in list(arrays) + lands], *extra)
    return dict(n=n, scatter=scatter, sems=outs[:2], srcs=outs[2:2 + n], lands=outs[2 + n:2 + 2 * n], token=outs[-1])


def _exchange_wait(handle, after, name):
    n, scatter = handle["n"], handle["scatter"]

    def body(*refs):
        srcs, zones = refs[:n], refs[n:2 * n]
        send_sems, recv_sems = refs[2 * n], refs[2 * n + 1]
        for cp in _exchange_copies(scatter, srcs, zones, send_sems, recv_sems):
            cp.wait_send()
            cp.wait_recv()

    thru = [pltpu.HBM(a.shape, a.dtype) for a in list(handle["srcs"]) + list(handle["lands"])]
    outs = pl.pallas_call(
        body, name=name, out_shape=tuple(thru),
        in_specs=[_HBM] * (2 * n) + [_SEM, _SEM, pl.BlockSpec(memory_space=pl.ANY)], out_specs=tuple([_HBM] * (2 * n)),
        input_output_aliases={i: i for i in range(2 * n)},
        compiler_params=pltpu.CompilerParams(has_side_effects=pltpu.SideEffectType.DATAFLOW_SIDE_EFFECTING),
    )(*handle["srcs"], *handle["lands"], *handle["sems"], after)
    return list(outs[n:])


def _cols_from_shards(g):
    return jnp.transpose(g, (1, 0, 2)).reshape(g.shape[1], -1)


def _shards_from_cols(a):
    return jnp.transpose(a.reshape(a.shape[0], N_DEV, -1), (1, 0, 2))


def _local_step(x, positions, ada, g_pre_mix, g_post_mix, b_f, sinks, g_pre_ffn, g_post_ffn, target,
                w_in_t, mix_weights, ffn_weights, on_grads):
    s, d = x.shape
    row = lambda v: v.reshape(1, -1)
    shift_m, scale_m, gate_m, shift_f, scale_f, gate_f = (ada[i:i + 1] for i in range(6))
    w_gate_t, w_qkv_t = w_in_t[F_OFF + N_HEADS:], w_in_t[:QKV_W]
    w_f_t = jnp.pad(w_in_t[F_OFF:F_OFF + N_HEADS], ((0, LANES - N_HEADS), (0, 0)))
    bf_row = jnp.pad(row(b_f), ((0, 0), (0, LANES - N_HEADS)))
    sink_rows = jnp.broadcast_to(sinks.reshape(N_HEADS, 1).astype(f32), (N_HEADS, LANES))
    inv_freq = 1.0 / (ROPE_THETA ** (jnp.arange(0, HEAD_DIM, 2, dtype=f32) / HEAD_DIM))
    cos, sin_s = _rope_tables(positions.reshape(s, 1), jnp.tile(inv_freq, 4).reshape(1, LANES), "rope_tables")

    h1, qa, ka, va, qb, kb, vb = _prenorm_proj_qkv(x, row(g_pre_mix), scale_m, shift_m, w_qkv_t, cos, sin_s, "prenorm_proj_qkv")
    gl = _matmul(h1, w_gate_t, "nt", bf16, "proj_gate")
    fl, cum_b = _forget_prep(h1, w_f_t, bf_row, "proj_forget_prep")
    o_a, lse_a = _attn_fwd(qa, ka, va, "swa_fwd", sink_rows=sink_rows, window=WINDOW, t=512)
    o_b, lse_b = _attn_fwd(qb, kb, vb, "fox_fwd", cum_b=cum_b, t=1024)
    everything_before = (gl[:8, :LANES] + o_a[:8, :LANES] + o_b[:8, :LANES]).astype(f32)
    w_branch_a, w_branch_b, w_out = mix_weights(everything_before)
    ba, bb, merged = _branch_merge(o_a, o_b, w_branch_a, w_branch_b, gl, "branch_merge")
    y1, x2, h2 = _out_proj_postnorm_prenorm(merged, w_out, x, row(g_post_mix), gate_m, row(g_pre_ffn), scale_f, shift_f,
                                            "out_proj_norms")

    w_ffn_in_t, w_ffn_out = ffn_weights(h2)
    g_ff, u_ff, act = _ffn_in_swiglu(h2, w_ffn_in_t, "ffn_in_swiglu")
    loss_row, d_out, d_y2, vec_pf = _out_proj_loss_tail(act, w_ffn_out, x2, row(g_post_ffn), gate_f, target, "ffn_out_loss_tail")

    g_w_ffn_out = _matmul(act, d_y2, "tn", bf16, "ffn_out_wgrad")
    dg_ff, du_ff = _ffn_out_dgrad_swiglu(d_y2, w_ffn_out, g_ff, u_ff, "ffn_out_dgrad_swiglu")
    g_w_ffn_in_t = _wgrad_stack([dg_ff, du_ff], h2, "ffn_in_wgrad")
    sent = on_grads(dict(w_ffn_in=g_w_ffn_in_t, w_ffn_out=g_w_ffn_out))
    d_x2, vec_nf, d_y1, vec_pm = _dgrad_prenorm_bwd(
        [(dg_ff, w_ffn_in_t, 0), (du_ff, w_ffn_in_t, 1)], x2, row(g_pre_ffn), scale_f, d_out, "ffn_in_dgrad_norms_bwd",
        after=sent, below=(y1, row(g_post_mix), gate_m))

    g_w_out = _matmul(merged, d_y1, "tn", bf16, "out_proj_wgrad")
    d_ba, d_bb, dgl = _out_dgrad_merge_bwd(d_y1, w_out, ba, bb, gl, "out_proj_dgrad_merge_bwd")
    g_w_branch_a = _matmul(o_a, d_ba, "tn", bf16, "branch_a_wgrad")
    g_w_branch_b = _matmul(o_b, d_bb, "tn", bf16, "branch_b_wgrad")
    sent = on_grads(dict(w_out=g_w_out, w_branch_a=g_w_branch_a, w_branch_b=g_w_branch_b))
    d_oa, delta_a, d_sink = _branch_dgrad_delta(d_ba, w_branch_a, o_a, "branch_a_dgrad_delta", lse=lse_a,
                                                sink_rows=sink_rows, after=sent)
    d_ob, delta_b = _branch_dgrad_delta(d_bb, w_branch_b, o_b, "branch_b_dgrad_delta", after=sent)
    dqa_t, dka, dva = _attn_bwd(qa, ka, va, d_oa, lse_a, delta_a, "swa_bwd", window=WINDOW, t=512)
    dqb_t, dkb, dvb, dcs, rs = _attn_bwd(qb, kb, vb, d_ob, lse_b, delta_b, "fox_bwd", cum_b=cum_b, t=512)
    dqkv = _qkv_prep_bwd(dqa_t, dka, dva, dqb_t, dkb, dvb, cos, sin_s, "qkv_prep_bwd")
    dfl, vec_bf = _forget_prep_bwd(rs.reshape(N_HEADS, s), dcs, fl, bf_row, "forget_prep_bwd")
    g_w_in_t = jnp.concatenate([_matmul(dqkv, h1, "tn", bf16, "qkv_wgrad"), _matmul(dfl, h1, "tn", bf16, "forget_wgrad")[:N_HEADS],
                                _matmul(dgl, h1, "tn", bf16, "gate_wgrad")], axis=0)
    sent = on_grads(dict(w_in=g_w_in_t))
    grad_x, vec_nm = _dgrad_prenorm_bwd([(dgl, w_gate_t, 0), (dqkv, w_qkv_t, 0), (dfl, w_f_t, 0)], x, row(g_pre_mix),
                                        scale_m, d_x2, "in_proj_dgrad_prenorm_bwd", after=sent)

    d_ada = jnp.concatenate([vec_nm[0], vec_nm[1], vec_pm[0], vec_nf[0], vec_nf[1], vec_pf[0]])
    small = dict(b_ada=d_ada, g_pre_mix=vec_nm[2], g_post_mix=vec_pm[1], g_pre_ffn=vec_nf[2], g_post_ffn=vec_pf[1],
                 b_f=vec_bf[0, :N_HEADS], sinks=d_sink[:, 0], loss=loss_row[0, :1])
    return grad_x, small


_SMALL = (("b_ada", 6144), ("g_pre_mix", 1024), ("g_post_mix", 1024), ("g_pre_ffn", 1024), ("g_post_ffn", 1024),
          ("b_f", 128), ("sinks", 128), ("loss", 128))
_SMALL_ROWS = 88


def _pack_small(vals):
    parts = [jnp.pad(vals[k].reshape(-1).astype(f32), (0, n - vals[k].size)) for k, n in _SMALL]
    flat = jnp.concatenate(parts)
    return jnp.pad(flat, (0, _SMALL_ROWS * LANES - flat.size)).reshape(_SMALL_ROWS, LANES)


def _unpack_small(slab, shapes):
    flat, out, off = slab.reshape(-1), {}, 0
    for k, n in _SMALL:
        size = math.prod(shapes[k])
        out[k] = flat[off:off + size].reshape(shapes[k])
        off += n
    return out


def kernel(x, c, positions, w_ada, b_ada, g_pre_mix, g_post_mix, w_in, b_f, sinks, w_branch_a, w_branch_b, w_out, g_pre_ffn, g_post_ffn, w_ffn_in, w_ffn_out, loss_target, m_w_ada, m_b_ada, m_g_pre_mix, m_g_post_mix, m_w_in, m_b_f, m_sinks, m_w_branch_a, m_w_branch_b, m_w_out, m_g_pre_ffn, m_g_post_ffn, m_w_ffn_in, m_w_ffn_out, v_w_ada, v_b_ada, v_g_pre_mix, v_g_post_mix, v_w_in, v_b_f, v_sinks, v_w_branch_a, v_w_branch_b, v_w_out, v_g_pre_ffn, v_g_post_ffn, v_w_ffn_in, v_w_ffn_out):
    xi, yi, ci = _me()
    me = 4 * xi + 2 * yi + ci
    d = D_MODEL
    ada_w = w_ada.shape[2]

    c_all, = _all_gather([c], "gather_c", vmem=True)
    c_all = c_all.reshape(N_DEV, d)
    b_mine = lax.dynamic_slice(b_ada, (0, me * ada_w), (1, ada_w))
    ada_cols = _ada_fwd(c_all, w_ada[0], b_mine, "ada_fwd")

    transposed = ("w_in", "w_ffn_in")
    tr = lambda a: jnp.transpose(a[0])

    ada_all, g_in = _all_gather([ada_cols, tr(w_in).astype(bf16)], "gather_ada_w_in")
    ada = lax.dynamic_index_in_dim(ada_all, me, axis=1, keepdims=False).reshape(6, d)
    late_mix = [w.astype(bf16) for w in (w_branch_a[0], w_branch_b[0], w_out[0])]
    late_ffn = [w.astype(bf16) for w in (tr(w_ffn_in), w_ffn_out[0])]
    mix_h = _exchange_start(late_mix, False, "gather_mix_start", after=g_in)
    ffn_h = _exchange_start(late_ffn, False, "gather_ffn_start", after=mix_h["token"])

    def mine_into(zone, block):
        return lax.dynamic_update_index_in_dim(zone, block, me, 0)

    def rows_from_shards(g):
        return g.reshape(g.shape[0] * g.shape[1], g.shape[2])

    def mix_weights(after):
        zones = _exchange_wait(mix_h, after, "gather_mix_wait")
        g_ba, g_bb, g_out = (mine_into(z, w) for z, w in zip(zones, late_mix))
        return _cols_from_shards(g_ba), _cols_from_shards(g_bb), rows_from_shards(g_out)

    def ffn_weights(after):
        zones = _exchange_wait(ffn_h, after, "gather_ffn_wait")
        g_fi, g_fo = (mine_into(z, w) for z, w in zip(zones, late_ffn))
        return rows_from_shards(g_fi), rows_from_shards(g_fo)

    row_sharded = ("w_out", "w_ffn_out") + transposed
    in_flight = []

    def on_grads(group):
        sends = [g.reshape(N_DEV, g.shape[0] // N_DEV, g.shape[1]) if nm in row_sharded else _shards_from_cols(g)
                 for nm, g in group.items()]
        handle = _exchange_start(sends, True, "scatter_start_%d" % len(in_flight))
        in_flight.append((list(group), sends, handle))
        return handle["token"]

    grad_x, small = _local_step(
        x[0], positions[0], ada + ffn_h["token"][0, 0], g_pre_mix[0], g_post_mix[0], b_f[0], sinks[0], g_pre_ffn[0],
        g_post_ffn[0], loss_target[0], rows_from_shards(g_in), mix_weights, ffn_weights, on_grads)

    ws = dict(w_in=(w_in, m_w_in, v_w_in), w_branch_a=(w_branch_a, m_w_branch_a, v_w_branch_a),
              w_branch_b=(w_branch_b, m_w_branch_b, v_w_branch_b), w_out=(w_out, m_w_out, v_w_out),
              w_ffn_in=(w_ffn_in, m_w_ffn_in, v_w_ffn_in), w_ffn_out=(w_ffn_out, m_w_ffn_out, v_w_ffn_out))
    res = {}

    def finish_group(gi, after):
        names, sends, handle = in_flight[gi]
        zones = _exchange_wait(handle, after, "scatter_wait_%d" % gi)
        for nm, zone, sent in zip(names, zones, sends):
            w, m, v = (tr(a) if nm in transposed else a[0] for a in ws[nm])
            out = _adamw(zone, w, m, v, "adamw_" + nm, mine=sent, me=me.reshape(1).astype(jnp.int32))
            after = out[0]
            res[nm] = [jnp.transpose(o) for o in out] if nm in transposed else out
        return after

    done = finish_group(1, finish_group(0, grad_x))

    slab_all, = _all_gather([_pack_small(small)], "gather_small", vmem=True, after=done)
    small_w = dict(b_ada=b_ada, g_pre_mix=g_pre_mix, g_post_mix=g_post_mix, g_pre_ffn=g_pre_ffn, g_post_ffn=g_post_ffn,
                   b_f=b_f, sinks=sinks, loss=jnp.zeros((1,), f32))
    small_m = dict(b_ada=m_b_ada, g_pre_mix=m_g_pre_mix, g_post_mix=m_g_post_mix, g_pre_ffn=m_g_pre_ffn,
                   g_post_ffn=m_g_post_ffn, b_f=m_b_f, sinks=m_sinks, loss=jnp.zeros((1,), f32))
    small_v = dict(b_ada=v_b_ada, g_pre_mix=v_g_pre_mix, g_post_mix=v_g_post_mix, g_pre_ffn=v_g_pre_ffn,
                   g_post_ffn=v_g_post_ffn, b_f=v_b_f, sinks=v_sinks, loss=jnp.ones((1,), f32))
    shapes = {k: small_w[k].shape for k, _ in _SMALL}
    s_out = _adamw(slab_all, _pack_small(small_w), _pack_small(small_m), _pack_small(small_v), "adamw_small")
    s_grad, s_delta, s_m, s_v = (_unpack_small(o, shapes) for o in s_out)

    d_ada_all = lax.dynamic_slice(slab_all[:, :6144 // LANES, :].reshape(N_DEV, 6144), (0, me * ada_w), (N_DEV, ada_w))
    ada_parts = _ada_wgrad(c_all, d_ada_all, "ada_wgrad")

    res["w_ada"] = _adamw(ada_parts, w_ada[0], m_w_ada[0], v_w_ada[0], "adamw_w_ada")
    finish_group(2, res["w_ada"][0])

    order = ["w_ada", "b_ada", "g_pre_mix", "g_post_mix", "w_in", "b_f", "sinks", "w_branch_a", "w_branch_b", "w_out",
             "g_pre_ffn", "g_post_ffn", "w_ffn_in", "w_ffn_out"]
    outs = [s_grad["loss"].reshape(()), grad_x[None]]
    for which, small_o in enumerate((s_grad, s_delta, s_m, s_v)):
        for nm in order:
            outs.append(res[nm][which][None] if nm in res else small_o[nm])
    return tuple(outs)
```

```python
import math

import jax
import jax.numpy as jnp
from jax import lax
from jax.experimental import pallas as pl
from jax.experimental.pallas import tpu as pltpu

f32 = jnp.float32
bf16 = jnp.bfloat16

D_MODEL = 1024
HEAD_DIM = 64
N_HEADS = 8
N_PAIRS = 4
QKV_W = 2304
F_OFF = 2304
WINDOW = 128
ROPE_THETA = 10000.0
RMS_EPS = 1e-6
N_DEV = 8
ADAM_LR, ADAM_B1, ADAM_B2, ADAM_EPS, ADAM_WD, ADAM_STEP = 0.001, 0.9, 0.999, 1e-08, 0.01, 10
NEG = -1e30
LANES = 128
VMEM_LIMIT = 48 * 1024 * 1024
MESH = pl.DeviceIdType.MESH

_NT = (((1,), (1,)), ((), ()))
_TN = (((0,), (0,)), ((), ()))


def _params(n_grid=0):
    sem = ("arbitrary",) * n_grid if n_grid else None
    return pltpu.CompilerParams(dimension_semantics=sem, vmem_limit_bytes=VMEM_LIMIT)


def _row_tile(s, want):
    t = min(s, want)
    assert s % t == 0, (s, t)
    return t


MATMUL_VMEM_BUDGET = 40 * 1024 * 1024


def _matmul_tiles(m, n, k, a_item, b_item, o_item):
    def tiles(d):
        return [t for t in range(LANES, min(d, 2048) + 1, LANES) if d % t == 0] or [d]

    best = None
    for tm in tiles(m):
        for tn in tiles(n):
            vmem = 2 * (tm * k * a_item + tn * k * b_item + tm * tn * o_item) + tm * tn * 4
            if vmem > MATMUL_VMEM_BUDGET:
                continue
            traffic = m * k * a_item + n * k * b_item * (1 if tn == n else m // tm) + m * n * o_item
            steps = (m // tm) * (n // tn)
            key = (traffic, 0, steps) if steps >= 4 else (traffic, 1, -steps)
            if best is None or key < best[0]:
                best = (key, tm, tn)
    assert best is not None, (m, n, k)
    return best[1], best[2]


def _matmul(a, b, mode, out_dtype, name, after=None):
    if mode == "nn":
        (m, k), n = a.shape, b.shape[1]
    elif mode == "nt":
        (m, k), n = a.shape, b.shape[0]
    else:
        (k, m), n = a.shape, b.shape[1]
    tm, tn = _matmul_tiles(m, n, k, a.dtype.itemsize, b.dtype.itemsize, jnp.dtype(out_dtype).itemsize)
    if mode == "nn":
        a_spec, b_spec, dims = pl.BlockSpec((tm, k), lambda i, j: (i, 0)), pl.BlockSpec((k, tn), lambda i, j: (0, j)), None
    elif mode == "nt":
        a_spec, b_spec, dims = pl.BlockSpec((tm, k), lambda i, j: (i, 0)), pl.BlockSpec((tn, k), lambda i, j: (j, 0)), _NT
    else:
        a_spec, b_spec, dims = pl.BlockSpec((k, tm), lambda i, j: (0, i)), pl.BlockSpec((k, tn), lambda i, j: (0, j)), _TN

    def body(a_ref, b_ref, *rest):
        o_ref = rest[-1]
        av, bv = a_ref[...].astype(bf16), b_ref[...].astype(bf16)
        if dims is None:
            r = jnp.dot(av, bv, preferred_element_type=f32)
        else:
            r = lax.dot_general(av, bv, dims, preferred_element_type=f32)
        o_ref[...] = r.astype(out_dtype)

    extra = [] if after is None else [after]
    return pl.pallas_call(
        body, name=name, grid=(m // tm, n // tn), in_specs=[a_spec, b_spec] + [pl.BlockSpec(memory_space=pl.ANY)] * len(extra),
        out_specs=pl.BlockSpec((tm, tn), lambda i, j: (i, j)),
        out_shape=jax.ShapeDtypeStruct((m, n), out_dtype), compiler_params=_params(2),
    )(a, b, *extra)


def _rstd(v):
    return lax.rsqrt(jnp.mean(v * v, axis=-1, keepdims=True) + RMS_EPS)


def _row_spec(tm, d):
    return pl.BlockSpec((tm, d), lambda i: (i, 0))


def _vec_spec(d, rows=1):
    return pl.BlockSpec((rows, d), lambda i: (0, 0))


def _proj_spec(a, w, tm):
    return [_row_spec(tm, a.shape[1]), pl.BlockSpec(w.shape, lambda i: (0, 0))]


def _out_proj_postnorm_prenorm(a, w, x, g_post, gate, g_pre, scale, shift, name):
    s, d = x.shape
    tm = _row_tile(s, 512)

    def body(a_ref, w_ref, x_ref, gp_ref, gate_ref, g_ref, sc_ref, sh_ref, y_ref, x2_ref, h_ref):
        yv = jnp.dot(a_ref[...], w_ref[...], preferred_element_type=f32)
        y_ref[...] = yv
        x2 = x_ref[...] + gate_ref[...] * (yv * _rstd(yv) * gp_ref[...])
        x2_ref[...] = x2
        h_ref[...] = ((x2 * _rstd(x2) * g_ref[...]) * (1.0 + sc_ref[...]) + sh_ref[...]).astype(bf16)

    return pl.pallas_call(
        body, name=name, grid=(s // tm,), in_specs=_proj_spec(a, w, tm) + [_row_spec(tm, d)] + [_vec_spec(d)] * 5,
        out_specs=[_row_spec(tm, d)] * 3,
        out_shape=[jax.ShapeDtypeStruct((s, d), f32)] * 2 + [jax.ShapeDtypeStruct((s, d), bf16)], compiler_params=_params(1),
    )(a, w, x, g_post, gate, g_pre, scale, shift)


def _rms_bwd(u, v, r):
    return r * u - v * (r * r * r) * jnp.mean(u * v, axis=-1, keepdims=True)


def _out_proj_loss_tail(a, w, x, g, gate, target, name):
    s, d = x.shape
    tm = _row_tile(s, 512)

    def body(a_ref, w_ref, x_ref, g_ref, gate_ref, t_ref, loss_ref, do_ref, dy_ref, vec_ref):
        @pl.when(pl.program_id(0) == 0)
        def _():
            loss_ref[...] = jnp.zeros_like(loss_ref)
            vec_ref[...] = jnp.zeros_like(vec_ref)
        yv = jnp.dot(a_ref[...], w_ref[...], preferred_element_type=f32)
        r = _rstd(yv)
        yn = yv * r
        err = x_ref[...] + gate_ref[...] * (yn * g_ref[...]) - t_ref[...]
        loss_ref[...] += 0.5 * jnp.sum(jnp.mean(err * err, axis=-1, keepdims=True), axis=0, keepdims=True)
        dr = err / d
        do_ref[...] = dr
        dn = dr * gate_ref[...]
        vec_ref[0:1, :] += jnp.sum(dr * (yn * g_ref[...]), axis=0, keepdims=True)
        vec_ref[1:2, :] += jnp.sum(dn * yn, axis=0, keepdims=True)
        dy_ref[...] = _rms_bwd(dn * g_ref[...], yv, r).astype(bf16)

    return pl.pallas_call(
        body, name=name, grid=(s // tm,),
        in_specs=_proj_spec(a, w, tm) + [_row_spec(tm, d)] + [_vec_spec(d)] * 2 + [_row_spec(tm, d)],
        out_specs=[_vec_spec(LANES), _row_spec(tm, d), _row_spec(tm, d), _vec_spec(d, 8)],
        out_shape=[jax.ShapeDtypeStruct((1, LANES), f32), jax.ShapeDtypeStruct((s, d), f32),
                   jax.ShapeDtypeStruct((s, d), bf16), jax.ShapeDtypeStruct((8, d), f32)],
        compiler_params=_params(1),
    )(a, w, x, g, gate, target)


def _dgrad_prenorm_bwd(terms, x, g, scale, dres, name, after=None, below=None):
    s, d = x.shape
    n = len(terms)
    k = sum(a.shape[1] for a, _, _ in terms)
    row_bytes = 2 * (2 * k) + d * (4 + 2 * 4 * 3 + (2 * 4 + 2 * 2 if below else 0))
    tm = next(t for t in (512, 256, 128) if s % t == 0 and 4 * k * d + t * row_bytes <= MATMUL_VMEM_BUDGET)
    extra = [] if after is None else [after]

    def body(*refs):
        a_refs, b_refs = refs[:n], refs[n:2 * n]
        x_ref, g_ref, sc_ref, dr_ref = refs[2 * n:2 * n + 4]
        n_in = 2 * n + 4 + (3 if below else 0) + len(extra)
        dx_ref, vec_ref = refs[n_in], refs[n_in + 1]
        if below:
            y_ref, gp_ref, gate_ref = refs[2 * n + 4:2 * n + 7]
            dy_ref, vec2_ref = refs[n_in + 2], refs[n_in + 3]

        @pl.when(pl.program_id(0) == 0)
        def _():
            vec_ref[...] = jnp.zeros_like(vec_ref)
            if below:
                vec2_ref[...] = jnp.zeros_like(vec2_ref)
        dhv = jnp.dot(a_refs[0][...], b_refs[0][...], preferred_element_type=f32)
        for i in range(1, n):
            dhv = dhv + jnp.dot(a_refs[i][...], b_refs[i][...], preferred_element_type=f32)
        xv = x_ref[...]
        r = _rstd(xv)
        xn = xv * r
        dn = dhv * (1.0 + sc_ref[...])
        vec_ref[0:1, :] += jnp.sum(dhv, axis=0, keepdims=True)
        vec_ref[1:2, :] += jnp.sum(dhv * (xn * g_ref[...]), axis=0, keepdims=True)
        vec_ref[2:3, :] += jnp.sum(dn * xn, axis=0, keepdims=True)
        dx = dr_ref[...] + _rms_bwd(dn * g_ref[...], xv, r)
        dx_ref[...] = dx
        if below:
            yv = y_ref[...]
            ry = _rstd(yv)
            yn = yv * ry
            dny = dx * gate_ref[...]
            vec2_ref[0:1, :] += jnp.sum(dx * (yn * gp_ref[...]), axis=0, keepdims=True)
            vec2_ref[1:2, :] += jnp.sum(dny * yn, axis=0, keepdims=True)
            dy_ref[...] = _rms_bwd(dny * gp_ref[...], yv, ry).astype(bf16)

    in_specs = ([_row_spec(tm, a.shape[1]) for a, _, _ in terms]
                + [pl.BlockSpec((a.shape[1], d), lambda i, r=r: (r, 0)) for a, _, r in terms]
                + [_row_spec(tm, d)] + [_vec_spec(d)] * 2 + [_row_spec(tm, d)])
    out_specs = [_row_spec(tm, d), _vec_spec(d, 8)]
    out_shape = [jax.ShapeDtypeStruct((s, d), f32), jax.ShapeDtypeStruct((8, d), f32)]
    args = [a for a, _, _ in terms] + [b for _, b, _ in terms] + [x, g, scale, dres]
    if below:
        in_specs += [_row_spec(tm, d)] + [_vec_spec(d)] * 2
        out_specs += [_row_spec(tm, d), _vec_spec(d, 8)]
        out_shape += [jax.ShapeDtypeStruct((s, d), bf16), jax.ShapeDtypeStruct((8, d), f32)]
        args += list(below)
    return pl.pallas_call(
        body, name=name, grid=(s // tm,), in_specs=in_specs + [pl.BlockSpec(memory_space=pl.ANY)] * len(extra),
        out_specs=out_specs, out_shape=out_shape, compiler_params=_params(1),
    )(*args, *extra)


def _lane():
    return lax.broadcasted_iota(jnp.int32, (1, LANES), 1)


def _rope_tables(pos_col, inv_freq, name):
    s = pos_col.shape[0]

    def body(p_ref, f_ref, cos_ref, sin_ref):
        ang = p_ref[...].astype(f32) * f_ref[...]
        first_half = (_lane() % HEAD_DIM) < HEAD_DIM // 2
        cos_ref[...] = jnp.cos(ang)
        sn = jnp.sin(ang)
        sin_ref[...] = jnp.where(first_half, -sn, sn)

    return pl.pallas_call(
        body, name=name, out_shape=[jax.ShapeDtypeStruct((s, LANES), f32)] * 2, compiler_params=_params(),
    )(pos_col, inv_freq)


def _swap_halves(v):
    first_half = (_lane() % HEAD_DIM) < HEAD_DIM // 2
    return jnp.where(first_half, pltpu.roll(v, LANES - HEAD_DIM // 2, axis=1), pltpu.roll(v, HEAD_DIM // 2, axis=1))


def _prenorm_proj_qkv(x, g, mod_scale, mod_shift, w_qkv_t, cos, sin_s, name):
    s, d = x.shape
    tm = _row_tile(s, 512)
    scale = 1.0 / math.sqrt(HEAD_DIM)

    def body(x_ref, g_ref, msc_ref, msh_ref, w_ref, c_ref, s_ref, h_ref, qa_ref, ka_ref, va_ref, qb_ref, kb_ref, vb_ref):
        xv = x_ref[...]
        h = ((xv * _rstd(xv) * g_ref[...]) * (1.0 + msc_ref[...]) + msh_ref[...]).astype(bf16)
        h_ref[...] = h
        proj = lax.dot_general(h, w_ref[...], _NT, preferred_element_type=f32)
        cs, sn = c_ref[...], s_ref[...]
        low = _lane() < HEAD_DIM

        def blk(j):
            return proj[:, j * LANES:(j + 1) * LANES]

        def rope(v):
            return v * cs + _swap_halves(v) * sn

        def expand(v):
            other = pltpu.roll(v, HEAD_DIM, axis=1)
            return jnp.where(low, v, other), jnp.where(low, other, v)

        for j in range(N_PAIRS):
            qa_ref[:, j * LANES:(j + 1) * LANES] = (rope(blk(j)) * scale).astype(bf16)
            qb_ref[:, j * LANES:(j + 1) * LANES] = (blk(6 + j) * scale).astype(bf16)
            kb_ref[:, j * LANES:(j + 1) * LANES] = blk(10 + j).astype(bf16)
            vb_ref[:, j * LANES:(j + 1) * LANES] = blk(14 + j).astype(bf16)
        k0, k1 = expand(rope(blk(4)))
        v0, v1 = expand(blk(5))
        for j in range(N_PAIRS):
            ka_ref[:, j * LANES:(j + 1) * LANES] = (k0 if j < 2 else k1).astype(bf16)
            va_ref[:, j * LANES:(j + 1) * LANES] = (v0 if j < 2 else v1).astype(bf16)

    hw = N_PAIRS * LANES
    return pl.pallas_call(
        body, name=name, grid=(s // tm,),
        in_specs=[_row_spec(tm, d)] + [_vec_spec(d)] * 3
        + [pl.BlockSpec((QKV_W, d), lambda i: (0, 0)), _row_spec(tm, LANES), _row_spec(tm, LANES)],
        out_specs=[_row_spec(tm, d)] + [_row_spec(tm, hw)] * 6,
        out_shape=[jax.ShapeDtypeStruct((s, d), bf16)] + [jax.ShapeDtypeStruct((s, hw), bf16)] * 6, compiler_params=_params(1),
    )(x, g, mod_scale, mod_shift, w_qkv_t, cos, sin_s)


def _qkv_prep_bwd(dqa_t, dka, dva, dqb_t, dkb, dvb, cos, sin_s, name):
    s = dka.shape[0]
    tm = _row_tile(s, 256)
    scale = 1.0 / math.sqrt(HEAD_DIM)
    hw = N_PAIRS * LANES
    t_spec = pl.BlockSpec((hw, tm), lambda i: (0, i))

    def body(dqa_ref, dka_ref, dva_ref, dqb_ref, dkb_ref, dvb_ref, c_ref, s_ref, o_ref):
        cs, sn = c_ref[...], s_ref[...]
        low = _lane() < HEAD_DIM

        def blk(ref, j):
            return ref[:, j * LANES:(j + 1) * LANES].astype(f32)

        def blk_t(ref, j):
            return ref[j * LANES:(j + 1) * LANES, :].T

        def unrope(v):
            return v * cs + _swap_halves(v * sn)

        def fold(ref):
            a, b = blk(ref, 0) + blk(ref, 1), blk(ref, 2) + blk(ref, 3)
            kv0 = a + pltpu.roll(a, HEAD_DIM, axis=1)
            kv1 = b + pltpu.roll(b, HEAD_DIM, axis=1)
            return jnp.where(low, kv0, kv1)

        for j in range(N_PAIRS):
            o_ref[:, j * LANES:(j + 1) * LANES] = (unrope(blk_t(dqa_ref, j)) * scale).astype(bf16)
            o_ref[:, (6 + j) * LANES:(7 + j) * LANES] = (blk_t(dqb_ref, j) * scale).astype(bf16)
            o_ref[:, (10 + j) * LANES:(11 + j) * LANES] = blk(dkb_ref, j).astype(bf16)
            o_ref[:, (14 + j) * LANES:(15 + j) * LANES] = blk(dvb_ref, j).astype(bf16)
        o_ref[:, 4 * LANES:5 * LANES] = unrope(fold(dka_ref)).astype(bf16)
        o_ref[:, 5 * LANES:6 * LANES] = fold(dva_ref).astype(bf16)

    return pl.pallas_call(
        body, name=name, grid=(s // tm,),
        in_specs=[t_spec, _row_spec(tm, hw), _row_spec(tm, hw), t_spec, _row_spec(tm, hw), _row_spec(tm, hw)] + [_row_spec(tm, LANES)] * 2,
        out_specs=_row_spec(tm, QKV_W), out_shape=jax.ShapeDtypeStruct((s, QKV_W), bf16), compiler_params=_params(1),
    )(dqa_t, dka, dva, dqb_t, dkb, dvb, cos, sin_s)


def _cumsum_rows(v, reverse=False):
    n = v.shape[0]
    row = lax.broadcasted_iota(jnp.int32, v.shape, 0)
    sh = 1
    while sh < n:
        if reverse:
            v = v + jnp.where(row < n - sh, pltpu.roll(v, n - sh, axis=0), 0.0)
        else:
            v = v + jnp.where(row >= sh, pltpu.roll(v, sh, axis=0), 0.0)
        sh *= 2
    return v


def _log_sigmoid(z):
    return jnp.minimum(z, 0.0) - jnp.log1p(jnp.exp(-jnp.abs(z)))


def _forget_prep(h, w_f_t, bf_row, name):
    s = h.shape[0]

    def body(h_ref, w_ref, b_ref, f_ref, cb_ref):
        fl = lax.dot_general(h_ref[...], w_ref[...], _NT, preferred_element_type=f32)
        f_ref[...] = fl
        cum = _cumsum_rows(_log_sigmoid(fl + b_ref[...]))
        for hd in range(N_HEADS):
            cb_ref[:, hd * LANES:(hd + 1) * LANES] = jnp.broadcast_to(cum[:, hd:hd + 1], (s, LANES))

    return pl.pallas_call(
        body, name=name,
        out_shape=[jax.ShapeDtypeStruct((s, LANES), f32), jax.ShapeDtypeStruct((s, N_HEADS * LANES), f32)],
        compiler_params=_params(),
    )(h, w_f_t, bf_row)


def _forget_prep_bwd(rs, dcs, fl, bf_row, name):
    s = fl.shape[0]

    def body(r_ref, c_ref, f_ref, b_ref, df_ref, db_ref):
        eye = (lax.broadcasted_iota(jnp.int32, (N_HEADS, LANES), 0) == lax.broadcasted_iota(jnp.int32, (N_HEADS, LANES), 1)).astype(f32)
        dcum = lax.dot_general(r_ref[...], eye, _TN, precision=lax.Precision.HIGHEST, preferred_element_type=f32)
        for h in range(N_HEADS):
            dcum = dcum - jnp.where(_lane() == h, jnp.sum(c_ref[:, h * LANES:(h + 1) * LANES], axis=1, keepdims=True), 0.0)
        dlf = _cumsum_rows(dcum, reverse=True)
        z = f_ref[...] + b_ref[...]
        df = jnp.where(_lane() < N_HEADS, dlf * jax.nn.sigmoid(-z), 0.0)
        df_ref[...] = df.astype(bf16)
        db_ref[...] = jnp.zeros_like(db_ref)
        db_ref[0:1, :] = jnp.sum(df, axis=0, keepdims=True)

    return pl.pallas_call(
        body, name=name,
        out_shape=[jax.ShapeDtypeStruct((s, LANES), bf16), jax.ShapeDtypeStruct((8, LANES), f32)], compiler_params=_params(),
    )(rs, dcs, fl, bf_row)


def _tile_mask(n_keys, n_queries, off, window):
    shape = (n_keys, n_queries)
    d = lax.broadcasted_iota(jnp.int32, shape, 1) - lax.broadcasted_iota(jnp.int32, shape, 0) + off
    valid = d >= 0
    return jnp.logical_and(valid, d < window) if window else valid


def _wide(v, t):
    return jnp.concatenate([v] * (t // LANES), axis=1)


def _attn_fwd(q, k, v, name, *, cum_b=None, sink_rows=None, window=None, t=256):
    s = q.shape[0]
    t = _row_tile(s, t)
    fox, has_sink = cum_b is not None, sink_rows is not None
    assert not window or (window % LANES == 0 and LANES + window <= s)

    def body(*refs):
        q_ref, k_ref, v_ref = refs[:3]
        rest = list(refs[3:])
        cb_ref = rest.pop(0) if fox else None
        sink_ref = rest.pop(0) if has_sink else None
        o_ref, lse_ref = rest
        i = pl.program_id(1)
        low = _lane() < HEAD_DIM
        top = lax.broadcasted_iota(jnp.int32, (LANES, 1), 0) < HEAD_DIM
        q2 = q_ref[...]
        zero = jnp.zeros_like(q2)
        qms = (jnp.where(low, q2, zero), jnp.where(low, zero, q2))

        def tile(k0, n_keys, off, carry, masked, queries=slice(0, t)):
            nq = queries.stop - queries.start
            kblk, vblk = k_ref[pl.ds(k0, n_keys), :], v_ref[pl.ds(k0, n_keys), :]
            valid = _tile_mask(n_keys, nq, off, window) if masked else None
            def scores(h):
                return lax.dot_general(kblk, qms[h][queries], _NT, preferred_element_type=f32)

            def softmax(h, sc):
                m, l, _ = carry[h]
                if fox:
                    sc = sc - _wide(cb_ref[pl.ds(k0, n_keys), h * LANES:(h + 1) * LANES], nq)
                if masked:
                    sc = jnp.where(valid, sc, NEG)
                m_new = jnp.maximum(m, jnp.max(sc, axis=0, keepdims=True))
                p = jnp.exp(sc - m_new)
                alpha = jnp.exp(m - m_new)
                return m_new, alpha * l + jnp.sum(p, axis=0, keepdims=True), alpha, p.astype(bf16)

            def update(h, m_new, l, alpha, p):
                return m_new, l, alpha * carry[h][2] + lax.dot_general(vblk, p, _TN, preferred_element_type=f32)

            if window:
                return tuple(update(h, *softmax(h, scores(h))) for h in range(2))
            scs = [scores(h) for h in range(2)]
            stats = [softmax(h, scs[h]) for h in range(2)]
            return tuple(update(h, *stats[h]) for h in range(2))

        def start(nq):
            if has_sink:
                return tuple((_wide(sink_ref[h:h + 1, :], nq), jnp.ones((1, nq), f32), jnp.zeros((LANES, nq), f32))
                             for h in range(2))
            return tuple((jnp.full((1, nq), NEG, f32), jnp.zeros((1, nq), f32), jnp.zeros((LANES, nq), f32)) for h in range(2))

        def finish(carry, queries):
            (m0, l0, a0), (m1, l1, a1) = carry
            o_t = jnp.where(top, a0 * (1.0 / l0), a1 * (1.0 / l1))
            o_ref[queries, :] = o_t.T.astype(bf16)
            lse_ref[0:1, queries] = m0 + jnp.log(l0)
            lse_ref[1:2, queries] = m1 + jnp.log(l1)

        if window:
            for c in range(t // LANES):
                queries = slice(c * LANES, (c + 1) * LANES)
                q0 = i * t + c * LANES
                k0 = pl.multiple_of(jnp.maximum(q0 - window, 0), LANES)
                finish(tile(k0, LANES + window, q0 - k0, start(LANES), True, queries), queries)
        else:
            carry = lax.fori_loop(0, i, lambda kb, c: tile(pl.multiple_of(kb * t, t), t, 0, c, False), start(t))
            finish(tile(pl.multiple_of(i * t, t), t, 0, carry, True), slice(0, t))

    q_spec = pl.BlockSpec((t, LANES), lambda j, i: (i, j))
    kv_spec = pl.BlockSpec((s, LANES), lambda j, i: (0, j))
    in_specs, args = [q_spec, kv_spec, kv_spec], [q, k, v]
    if fox:
        in_specs += [pl.BlockSpec((s, 2 * LANES), lambda j, i: (0, j))]
        args += [cum_b]
    if has_sink:
        in_specs += [pl.BlockSpec((None, 2, LANES), lambda j, i: (j, 0, 0))]
        args += [sink_rows.reshape(N_PAIRS, 2, LANES)]
    return pl.pallas_call(
        body, name=name, grid=(N_PAIRS, s // t), in_specs=in_specs,
        out_specs=[q_spec, pl.BlockSpec((None, 2, t), lambda j, i: (j, 0, i))],
        out_shape=[jax.ShapeDtypeStruct((s, N_PAIRS * LANES), bf16), jax.ShapeDtypeStruct((N_PAIRS, 2, s), f32)],
        compiler_params=_params(2),
    )(*args)


def _branch_dgrad_delta(db, w, o, name, *, lse=None, sink_rows=None, after=None):
    s, hw = o.shape
    tm = _row_tile(s, 512)
    has_sink = sink_rows is not None
    extra = [] if after is None else [after]

    def body(*refs):
        db_ref, w_ref, o_ref = refs[:3]
        outs = refs[3 + (2 if has_sink else 0) + len(extra):]
        do_ref, dl_ref = outs[:2]
        if has_sink:
            lse_ref, sink_ref = refs[3:5]
            ds_ref = outs[2]

            @pl.when(pl.program_id(0) == 0)
            def _():
                ds_ref[...] = jnp.zeros_like(ds_ref)
        do = lax.dot_general(db_ref[...], w_ref[...], _NT, preferred_element_type=f32).astype(bf16)
        do_ref[...] = do
        for j in range(N_PAIRS):
            cols = slice(j * LANES, (j + 1) * LANES)
            prod_t = (do[:, cols].astype(f32) * o_ref[:, cols].astype(f32)).T
            for h in range(2):
                dl = jnp.sum(prod_t[h * HEAD_DIM:(h + 1) * HEAD_DIM, :], axis=0, keepdims=True)
                dl_ref[j, h:h + 1, :] = dl
                if has_sink:
                    r = 2 * j + h
                    p_sink = jnp.exp(sink_ref[r:r + 1, 0:1] - lse_ref[j, h:h + 1, :])
                    ds_ref[r:r + 1, :] += -jnp.sum(p_sink * dl, axis=1, keepdims=True)

    rows_spec = pl.BlockSpec((N_PAIRS, 2, tm), lambda i: (0, 0, i))
    in_specs = [_row_spec(tm, db.shape[1]), pl.BlockSpec(w.shape, lambda i: (0, 0)), _row_spec(tm, hw)]
    args = [db, w, o]
    out_specs = [_row_spec(tm, hw), rows_spec]
    out_shape = [jax.ShapeDtypeStruct((s, hw), bf16), jax.ShapeDtypeStruct((N_PAIRS, 2, s), f32)]
    if has_sink:
        in_specs += [rows_spec, _vec_spec(LANES, N_HEADS)]
        args += [lse, sink_rows]
        out_specs += [_vec_spec(LANES, N_HEADS)]
        out_shape += [jax.ShapeDtypeStruct((N_HEADS, LANES), f32)]
    return pl.pallas_call(
        body, name=name, grid=(s // tm,), in_specs=in_specs + [pl.BlockSpec(memory_space=pl.ANY)] * len(extra),
        out_specs=out_specs, out_shape=out_shape, compiler_params=_params(1),
    )(*args, *extra)


def _attn_bwd(q, k, v, do, lse, delta, name, *, cum_b=None, window=None, t=256):
    s = q.shape[0]
    t = _row_tile(s, t)
    nblk = s // t
    fox = cum_b is not None
    assert not window or (window % LANES == 0 and LANES + window <= s)

    def body(*refs):
        k_ref, v_ref, q_ref, do_ref, lse_ref, dl_ref = refs[:6]
        rest = list(refs[6:])
        cb_ref = rest.pop(0) if fox else None
        dq_ref, dk_ref, dv_ref = rest[:3]
        dcs_ref, rs_ref = (rest[3], rest[4]) if fox else (None, None)
        dk_acc, dv_acc = rest[-2:]
        b = pl.program_id(1)
        k0 = pl.multiple_of(b * t, t)

        @pl.when(b == 0)
        def _():
            dq_ref[...] = jnp.zeros_like(dq_ref)
            if fox:
                rs_ref[...] = jnp.zeros_like(rs_ref)

        dk_acc[...] = jnp.zeros_like(dk_acc)
        dv_acc[...] = jnp.zeros_like(dv_acc)
        if fox:
            dcs_ref[...] = jnp.zeros_like(dcs_ref)
        low = _lane() < HEAD_DIM
        top = lax.broadcasted_iota(jnp.int32, (LANES, 1), 0) < HEAD_DIM
        kblk, vblk = k_ref[...], v_ref[...]
        k_t = kblk.astype(f32).T.astype(bf16)
        cks = [_wide(cb_ref[pl.ds(k0, t), h * LANES:(h + 1) * LANES], t) for h in range(2)] if fox else None

        def tile(q0, n_queries, off, masked, keys=slice(0, t)):
            cols = pl.ds(q0, n_queries)
            q2, do2 = q_ref[cols, :], do_ref[cols, :]
            zero = jnp.zeros_like(q2)
            valid = _tile_mask(keys.stop - keys.start, n_queries, off, window) if masked else None
            dq_parts = []
            for h in range(2):
                qm = jnp.where(low, q2, zero) if h == 0 else jnp.where(low, zero, q2)
                dom = jnp.where(low, do2, zero) if h == 0 else jnp.where(low, zero, do2)
                sc = lax.dot_general(kblk[keys], qm, _NT, preferred_element_type=f32)
                if fox:
                    sc = sc - cks[h]
                if masked:
                    sc = jnp.where(valid, sc, NEG)
                p = jnp.exp(sc - lse_ref[h:h + 1, cols])
                dp = lax.dot_general(vblk[keys], dom, _NT, preferred_element_type=f32)
                ds = p * (dp - dl_ref[h:h + 1, cols])
                pb, dsb = p.astype(bf16), ds.astype(bf16)
                dv_acc[keys, :] += jnp.dot(pb, dom, preferred_element_type=f32)
                dk_acc[keys, :] += jnp.dot(dsb, qm, preferred_element_type=f32)
                dq_parts.append(jnp.dot(k_t[:, keys], dsb, preferred_element_type=f32))
                if fox:
                    dcs_ref[:, h * LANES:(h + 1) * LANES] += sum(ds[:, g * LANES:(g + 1) * LANES] for g in range(t // LANES))
                    rs_ref[h:h + 1, cols] += jnp.sum(ds, axis=0, keepdims=True)
            dq_ref[:, cols] += jnp.where(top, dq_parts[0], dq_parts[1])

        def later_block(qb, carry):
            tile(pl.multiple_of(qb * t, t), t, 0, False)
            return carry

        if window:
            for c in range(t // LANES):
                first = b * t + c * LANES
                q0 = pl.multiple_of(jnp.minimum(first, s - (LANES + window)), LANES)
                tile(q0, LANES + window, q0 - first, True, slice(c * LANES, (c + 1) * LANES))
        else:
            tile(k0, t, 0, True)
            lax.fori_loop(b + 1, nblk, later_block, 0)
        dk_ref[...] = dk_acc[...].astype(bf16)
        dv_ref[...] = dv_acc[...].astype(bf16)

    kv_spec = pl.BlockSpec((t, LANES), lambda j, b: (b, j))
    seq_spec = pl.BlockSpec((s, LANES), lambda j, b: (0, j))
    rows_spec = pl.BlockSpec((None, 2, s), lambda j, b: (j, 0, 0))
    hw = N_PAIRS * LANES
    in_specs, args = [kv_spec, kv_spec, seq_spec, seq_spec, rows_spec, rows_spec], [k, v, q, do, lse, delta]
    out_specs = [pl.BlockSpec((LANES, s), lambda j, b: (j, 0)), kv_spec, kv_spec]
    out_shape = [jax.ShapeDtypeStruct((hw, s), f32), jax.ShapeDtypeStruct((s, hw), bf16), jax.ShapeDtypeStruct((s, hw), bf16)]
    if fox:
        in_specs += [pl.BlockSpec((s, 2 * LANES), lambda j, b: (0, j))]
        args += [cum_b]
        out_specs += [pl.BlockSpec((t, 2 * LANES), lambda j, b: (b, j)), rows_spec]
        out_shape += [jax.ShapeDtypeStruct((s, N_HEADS * LANES), f32), jax.ShapeDtypeStruct((N_PAIRS, 2, s), f32)]
    return pl.pallas_call(
        body, name=name, grid=(N_PAIRS, nblk), in_specs=in_specs, out_specs=out_specs, out_shape=out_shape,
        scratch_shapes=[pltpu.VMEM((t, LANES), f32)] * 2, compiler_params=_params(2),
    )(*args)


def _branch_merge(o_a, o_b, w_a, w_b, gl, name):
    s, k = o_a.shape
    d = w_a.shape[1]
    tm = _row_tile(s, 1024)

    def body(oa_ref, ob_ref, wa_ref, wb_ref, g_ref, ba_ref, bb_ref, m_ref):
        ba = jnp.dot(oa_ref[...], wa_ref[...], preferred_element_type=f32)
        bb = jnp.dot(ob_ref[...], wb_ref[...], preferred_element_type=f32)
        g0, g1 = jax.nn.sigmoid(g_ref[:, :d].astype(f32)), jax.nn.sigmoid(g_ref[:, d:].astype(f32))
        ba_ref[...] = ba.astype(bf16)
        bb_ref[...] = bb.astype(bf16)
        m_ref[...] = (g0 * ba + g1 * bb).astype(bf16)

    whole = pl.BlockSpec((k, d), lambda i: (0, 0))
    return pl.pallas_call(
        body, name=name, grid=(s // tm,),
        in_specs=[_row_spec(tm, k), _row_spec(tm, k), whole, whole, _row_spec(tm, 2 * d)],
        out_specs=[_row_spec(tm, d)] * 3, out_shape=[jax.ShapeDtypeStruct((s, d), bf16)] * 3, compiler_params=_params(1),
    )(o_a, o_b, w_a, w_b, gl)


def _out_dgrad_merge_bwd(dy, w_out, ba, bb, gl, name):
    s, d = ba.shape
    tm = _row_tile(s, 512)

    def body(dy_ref, w_ref, a_ref, b_ref, g_ref, da_ref, db_ref, dg_ref):
        dmv = lax.dot_general(dy_ref[...], w_ref[...], _NT, preferred_element_type=f32)
        g0, g1 = jax.nn.sigmoid(g_ref[:, :d].astype(f32)), jax.nn.sigmoid(g_ref[:, d:].astype(f32))
        da_ref[...] = (dmv * g0).astype(bf16)
        db_ref[...] = (dmv * g1).astype(bf16)
        dg_ref[:, :d] = (dmv * a_ref[...].astype(f32) * (g0 * (1.0 - g0))).astype(bf16)
        dg_ref[:, d:] = (dmv * b_ref[...].astype(f32) * (g1 * (1.0 - g1))).astype(bf16)

    return pl.pallas_call(
        body, name=name, grid=(s // tm,),
        in_specs=[_row_spec(tm, dy.shape[1]), pl.BlockSpec(w_out.shape, lambda i: (0, 0))] + [_row_spec(tm, d)] * 2
        + [_row_spec(tm, 2 * d)],
        out_specs=[_row_spec(tm, d)] * 2 + [_row_spec(tm, 2 * d)],
        out_shape=[jax.ShapeDtypeStruct((s, d), bf16)] * 2 + [jax.ShapeDtypeStruct((s, 2 * d), bf16)],
        compiler_params=_params(1),
    )(dy, w_out, ba, bb, gl)


GLU_TILE = 256


def _ffn_in_swiglu(h, w_t, name):
    s, d = h.shape
    f = w_t.shape[0] // 2
    tm = _row_tile(s, 2048)
    tg = GLU_TILE
    nb = f // tg

    def body(h_ref, wg_ref, wu_ref, g_ref, u_ref, act_ref):
        hv = h_ref[...]
        g = lax.dot_general(hv, wg_ref[...], _NT, preferred_element_type=f32)
        u = lax.dot_general(hv, wu_ref[...], _NT, preferred_element_type=f32)
        g_ref[...] = g.astype(bf16)
        u_ref[...] = u.astype(bf16)
        act_ref[...] = (g * jax.nn.sigmoid(g) * u).astype(bf16)

    col = pl.BlockSpec((tm, tg), lambda i, j: (i, j))
    return pl.pallas_call(
        body, name=name, grid=(s // tm, nb),
        in_specs=[pl.BlockSpec((tm, d), lambda i, j: (i, 0)), pl.BlockSpec((tg, d), lambda i, j: (j, 0)),
                  pl.BlockSpec((tg, d), lambda i, j: (j + nb, 0))],
        out_specs=[col] * 3, out_shape=[jax.ShapeDtypeStruct((s, f), bf16)] * 3, compiler_params=_params(2),
    )(h, w_t, w_t)


def _ffn_out_dgrad_swiglu(dy, w_out, g, u, name):
    s, d = dy.shape
    f = g.shape[1]
    tm = _row_tile(s, 2048)
    tg = GLU_TILE

    def body(dy_ref, w_ref, g_ref, u_ref, dg_ref, du_ref):
        dv = lax.dot_general(dy_ref[...], w_ref[...], _NT, preferred_element_type=f32)
        gv, uv = g_ref[...].astype(f32), u_ref[...].astype(f32)
        sg = jax.nn.sigmoid(gv)
        dg_ref[...] = (dv * uv * (sg * (1.0 + gv * (1.0 - sg)))).astype(bf16)
        du_ref[...] = (dv * (gv * sg)).astype(bf16)

    col = pl.BlockSpec((tm, tg), lambda i, j: (i, j))
    return pl.pallas_call(
        body, name=name, grid=(s // tm, f // tg),
        in_specs=[pl.BlockSpec((tm, d), lambda i, j: (i, 0)), pl.BlockSpec((tg, d), lambda i, j: (j, 0)), col, col],
        out_specs=[col] * 2, out_shape=[jax.ShapeDtypeStruct((s, f), bf16)] * 2, compiler_params=_params(2),
    )(dy, w_out, g, u)


def _wgrad_stack(parts, h, name):
    s, m = parts[0].shape
    d = h.shape[1]
    tm = 256
    nb = m // tm
    n = len(parts)

    def body(*refs):
        i = pl.program_id(0)
        for p in range(n):
            @pl.when(i // nb == p)
            def _(p=p):
                refs[n + 1][...] = lax.dot_general(refs[p][...], refs[n][...], _TN, preferred_element_type=f32).astype(bf16)

    a_specs = [pl.BlockSpec((s, tm), lambda i, p=p: (0, jnp.clip(i - p * nb, 0, nb - 1))) for p in range(n)]
    return pl.pallas_call(
        body, name=name, grid=(n * nb,), in_specs=a_specs + [pl.BlockSpec((s, d), lambda i: (0, 0))],
        out_specs=pl.BlockSpec((tm, d), lambda i: (i, 0)),
        out_shape=jax.ShapeDtypeStruct((n * m, d), bf16), compiler_params=_params(1),
    )(*parts, h)


def _ada_wgrad(c_all, d_all, name):
    n, d = c_all.shape
    w = d_all.shape[1]

    def body(c_ref, d_ref, o_ref):
        eye = (lax.broadcasted_iota(jnp.int32, (n, n), 0) == lax.broadcasted_iota(jnp.int32, (n, n), 1)).astype(f32)
        ct = lax.dot_general(c_ref[...], eye, _TN, precision=lax.Precision.HIGHEST, preferred_element_type=f32)
        g = ct[:, 0:1] * d_ref[0:1, :]
        for bi in range(1, n):
            g = g + ct[:, bi:bi + 1] * d_ref[bi:bi + 1, :]
        o_ref[0] = g

    return pl.pallas_call(
        body, name=name, out_shape=jax.ShapeDtypeStruct((1, d, w), f32), compiler_params=_params(),
    )(c_all, d_all)


def _adamw(parts, w, m, v, name, mine=None, me=None):
    r, c = w.shape
    n_parts = parts.shape[0]
    row_tiles = [t for t in range(min(r, 256), 0, -1) if r % t == 0 and (t % 16 == 0 or t == r)]
    if row_tiles:
        tr, tc = row_tiles[0], c
    else:
        tr, tc = r, next(t for t in (256, LANES) if c % t == 0)

    def body(*refs):
        w_ref, m_ref, v_ref, g_ref, d_ref, nm_ref, nv_ref = refs[-7:]
        if mine is None:
            p_ref, = refs[:-7]
        else:
            me_ref, p_ref, own_ref = refs[:-7]

        def part(i):
            if mine is None:
                return p_ref[i].astype(f32)
            return jnp.where(me_ref[0] == i, own_ref[...], p_ref[i]).astype(f32)

        g = part(0)
        for i in range(1, n_parts):
            g = g + part(i)
        mm = ADAM_B1 * m_ref[...] + (1.0 - ADAM_B1) * g
        vv = ADAM_B2 * v_ref[...] + (1.0 - ADAM_B2) * (g * g)
        m_hat = mm / (1.0 - ADAM_B1 ** ADAM_STEP)
        v_hat = vv / (1.0 - ADAM_B2 ** ADAM_STEP)
        g_ref[...] = g
        d_ref[...] = -ADAM_LR * (m_hat / (jnp.sqrt(v_hat) + ADAM_EPS) + ADAM_WD * w_ref[...])
        nm_ref[...] = mm
        nv_ref[...] = vv

    out_shape = [jax.ShapeDtypeStruct((r, c), f32)] * 4
    if mine is None:
        spec = pl.BlockSpec((tr, tc), lambda i, j: (i, j))
        return pl.pallas_call(
            body, name=name, grid=(r // tr, c // tc),
            in_specs=[pl.BlockSpec((n_parts, tr, tc), lambda i, j: (0, i, j))] + [spec] * 3,
            out_specs=[spec] * 4, out_shape=out_shape, compiler_params=_params(2),
        )(parts, w, m, v)
    spec = pl.BlockSpec((tr, tc), lambda i, j, me_ref: (i, j))
    return pl.pallas_call(
        body, name=name, out_shape=out_shape, compiler_params=_params(2),
        grid_spec=pltpu.PrefetchScalarGridSpec(
            num_scalar_prefetch=1, grid=(r // tr, c // tc),
            in_specs=[pl.BlockSpec((n_parts, tr, tc), lambda i, j, me_ref: (0, i, j)),
                      pl.BlockSpec((None, tr, tc), lambda i, j, me_ref: (me_ref[0], i, j))] + [spec] * 3,
            out_specs=[spec] * 4),
    )(me, parts, mine, w, m, v)


def _me():
    return lax.axis_index("x"), lax.axis_index("y"), lax.axis_index("c")


def _all_gather(arrays, name, vmem=False, after=None):
    n = len(arrays)
    space = pltpu.VMEM if vmem else pl.ANY
    extra = [] if after is None else [after]

    def body(*refs):
        ins = refs[:n]
        outs = refs[n + len(extra):2 * n + len(extra)]
        send_sems, recv_sems, local_sems = refs[2 * n + len(extra):]
        x, y, c = _me()
        me, sibling = (x, y, c), (x, y, 1 - c)
        chips = [(1 - x, y), (x, 1 - y), (1 - x, 1 - y)]

        def rows(a, dev):
            return outs[a].at[4 * dev[0] + 2 * dev[1] + dev[2]]

        def copy(a, k, block, to, src=None):
            return pltpu.make_async_remote_copy(
                src_ref=rows(a, block) if src is None else src, dst_ref=rows(a, block),
                send_sem=send_sems.at[a, k], recv_sem=recv_sems.at[a, k], device_id=to, device_id_type=MESH)

        mine = [pltpu.make_async_copy(ins[a], rows(a, me), local_sems.at[a]) for a in range(n)]
        for cp in mine:
            cp.start()
        first = []
        for a in range(n):
            first.append(copy(a, 0, me, sibling, src=ins[a]))
            first += [copy(a, 1 + j, me, (*chip, c), src=ins[a]) for j, chip in enumerate(chips)]
        for cp in first:
            cp.start()
        passed = []
        for j, chip in enumerate(chips):
            for a in range(n):
                copy(a, 1 + j, (*chip, c), me).wait_recv()
                fwd = copy(a, 4 + j, (*chip, c), sibling)
                fwd.start()
                passed.append(fwd)
        for a in range(n):
            copy(a, 0, sibling, me).wait_recv()
            for j, chip in enumerate(chips):
                copy(a, 4 + j, (*chip, 1 - c), me).wait_recv()
        for cp in first + passed:
            cp.wait_send()
        for cp in mine:
            cp.wait()

    outs = pl.pallas_call(
        body, name=name,
        in_specs=[pl.BlockSpec(memory_space=space)] * n + [pl.BlockSpec(memory_space=pl.ANY)] * len(extra),
        out_specs=[pl.BlockSpec(memory_space=space)] * n,
        out_shape=[jax.ShapeDtypeStruct((N_DEV,) + a.shape, a.dtype) for a in arrays],
        scratch_shapes=[pltpu.SemaphoreType.DMA((n, 7)), pltpu.SemaphoreType.DMA((n, 7)), pltpu.SemaphoreType.DMA((n,))],
        compiler_params=pltpu.CompilerParams(vmem_limit_bytes=VMEM_LIMIT),
    )(*arrays, *extra)
    return list(outs)


def _gather_prologue(c, w_ada, b_mine, w_in_t, name):
    n_dev, d = N_DEV, c.shape[1]
    ada_w = w_ada.shape[1]

    def body(c_ref, w_ref, b_ref, win_ref, call_ref, ada_ref, gin_ref, cols_ref, send_sems, recv_sems, local_sems):
        x, y, cc = _me()
        me, sibling = (x, y, cc), (x, y, 1 - cc)
        chips = [(1 - x, y), (x, 1 - y), (1 - x, 1 - y)]
        outs = (call_ref, ada_ref, gin_ref)

        def rows(a, dev):
            return outs[a].at[4 * dev[0] + 2 * dev[1] + dev[2]]

        def copy(a, k, block, to, src=None):
            return pltpu.make_async_remote_copy(
                src_ref=rows(a, block) if src is None else src, dst_ref=rows(a, block),
                send_sem=send_sems.at[a, k], recv_sem=recv_sems.at[a, k], device_id=to, device_id_type=MESH)

        def begin(a, src):
            own = pltpu.make_async_copy(src, rows(a, me), local_sems.at[a])
            sends = [copy(a, 0, me, sibling, src=src)] + [copy(a, 1 + j, me, (*chip, cc), src=src) for j, chip in enumerate(chips)]
            for cp in [own] + sends:
                cp.start()
            return own, sends

        def finish(a, own, sends):
            passed = []
            for j, chip in enumerate(chips):
                copy(a, 1 + j, (*chip, cc), me).wait_recv()
                passed.append(copy(a, 4 + j, (*chip, cc), sibling))
                passed[-1].start()
            copy(a, 0, sibling, me).wait_recv()
            for j, chip in enumerate(chips):
                copy(a, 4 + j, (*chip, 1 - cc), me).wait_recv()
            for cp in sends + passed:
                cp.wait_send()
            own.wait()

        w_in_flight = begin(2, win_ref)
        finish(0, *begin(0, c_ref))
        cols_ref[...] = (jnp.dot(call_ref[:, 0, :].astype(bf16), w_ref[...].astype(bf16), preferred_element_type=f32)
                         + b_ref[...])
        finish(1, *begin(1, cols_ref))
        finish(2, *w_in_flight)

    vmem, hbm = pl.BlockSpec(memory_space=pltpu.VMEM), pl.BlockSpec(memory_space=pl.ANY)
    return pl.pallas_call(
        body, name=name, in_specs=[vmem, vmem, vmem, hbm], out_specs=[vmem, vmem, hbm],
        out_shape=[jax.ShapeDtypeStruct((n_dev, 1, d), f32), jax.ShapeDtypeStruct((n_dev, n_dev, ada_w), f32),
                   jax.ShapeDtypeStruct((n_dev,) + w_in_t.shape, w_in_t.dtype)],
        scratch_shapes=[pltpu.VMEM((n_dev, ada_w), f32), pltpu.SemaphoreType.DMA((3, 7)), pltpu.SemaphoreType.DMA((3, 7)),
                        pltpu.SemaphoreType.DMA((3,))],
        compiler_params=pltpu.CompilerParams(vmem_limit_bytes=VMEM_LIMIT),
    )(c, w_ada, b_mine, w_in_t)


_FLIPS = ((0, 0, 1), (1, 0, 0), (0, 1, 0), (1, 1, 0), (1, 0, 1), (0, 1, 1), (1, 1, 1))
_HBM = pl.BlockSpec(memory_space=pltpu.HBM)
_SEM = pl.BlockSpec(memory_space=pltpu.SEMAPHORE)


def _exchange_copies(scatter, srcs, lands, send_sems, recv_sems):
    x, y, c = _me()
    me_row = 4 * x + 2 * y + c
    out = []
    for k, (fx, fy, fc) in enumerate(_FLIPS):
        peer = (x ^ fx, y ^ fy, c ^ fc)
        peer_row = 4 * peer[0] + 2 * peer[1] + peer[2]
        for a in range(len(srcs)):
            out.append(pltpu.make_async_remote_copy(
                src_ref=srcs[a].at[peer_row] if scatter else srcs[a], dst_ref=lands[a].at[me_row],
                send_sem=send_sems.at[7 * a + k], recv_sem=recv_sems.at[7 * a + k], device_id=peer, device_id_type=MESH))
    return out


def _exchange_start(arrays, scatter, name, after=None):
    n = len(arrays)
    lands = [lax.empty(a.shape if scatter else (N_DEV,) + a.shape, a.dtype) for a in arrays]
    extra = [] if after is None else [after]

    def body(*refs):
        srcs, zones = refs[:n], refs[n:2 * n]
        send_sems, recv_sems = refs[2 * n + len(extra)], refs[2 * n + len(extra) + 1]
        token = refs[-1]
        for cp in _exchange_copies(scatter, srcs, zones, send_sems, recv_sems):
            cp.start()
        token[...] = jnp.zeros_like(token)

    thru = [pltpu.HBM(a.shape, a.dtype) for a in list(arrays) + lands]
    outs = pl.pallas_call(
        body, name=name,
        out_shape=(pltpu.SemaphoreType.DMA((7 * n,)), pltpu.SemaphoreType.DMA((7 * n,)), *thru, jax.ShapeDtypeStruct((8, LANES), f32)),
        in_specs=[_HBM] * (2 * n) + [pl.BlockSpec(memory_space=pl.ANY)] * len(extra),
        out_specs=(_SEM, _SEM, *[_HBM] * (2 * n), pl.BlockSpec(memory_space=pltpu.VMEM)),
        input_output_aliases={i: 2 + i for i in range(2 * n)},
        compiler_params=pltpu.CompilerParams(has_side_effects=pltpu.SideEffectType.DATAFLOW_SIDE_EFFECTING),
    )(*[pltpu.with_memory_space_constraint(a, pltpu.HBM) for a in list(arrays) + lands], *extra)
    return dict(n=n, scatter=scatter, sems=outs[:2], srcs=outs[2:2 + n], lands=outs[2 + n:2 + 2 * n], token=outs[-1])


def _exchange_wait(handle, after, name):
    n, scatter = handle["n"], handle["scatter"]

    def body(*refs):
        srcs, zones = refs[:n], refs[n:2 * n]
        send_sems, recv_sems = refs[2 * n], refs[2 * n + 1]
        for cp in _exchange_copies(scatter, srcs, zones, send_sems, recv_sems):
            cp.wait_send()
            cp.wait_recv()

    thru = [pltpu.HBM(a.shape, a.dtype) for a in list(handle["srcs"]) + list(handle["lands"])]
    outs = pl.pallas_call(
        body, name=name, out_shape=tuple(thru),
        in_specs=[_HBM] * (2 * n) + [_SEM, _SEM, pl.BlockSpec(memory_space=pl.ANY)], out_specs=tuple([_HBM] * (2 * n)),
        input_output_aliases={i: i for i in range(2 * n)},
        compiler_params=pltpu.CompilerParams(has_side_effects=pltpu.SideEffectType.DATAFLOW_SIDE_EFFECTING),
    )(*handle["srcs"], *handle["lands"], *handle["sems"], after)
    return list(outs[n:])


def _cols_from_shards(g):
    return jnp.transpose(g, (1, 0, 2)).reshape(g.shape[1], -1)


def _shards_from_cols(a):
    return jnp.transpose(a.reshape(a.shape[0], N_DEV, -1), (1, 0, 2))


def _local_step(x, positions, ada, g_pre_mix, g_post_mix, b_f, sinks, g_pre_ffn, g_post_ffn, target,
                w_in_t, mix_weights, ffn_weights, on_grads):
    s, d = x.shape
    row = lambda v: v.reshape(1, -1)
    shift_m, scale_m, gate_m, shift_f, scale_f, gate_f = (ada[i:i + 1] for i in range(6))
    w_gate_t, w_qkv_t = w_in_t[F_OFF + N_HEADS:], w_in_t[:QKV_W]
    w_f_t = jnp.pad(w_in_t[F_OFF:F_OFF + N_HEADS], ((0, LANES - N_HEADS), (0, 0)))
    bf_row = jnp.pad(row(b_f), ((0, 0), (0, LANES - N_HEADS)))
    sink_rows = jnp.broadcast_to(sinks.reshape(N_HEADS, 1).astype(f32), (N_HEADS, LANES))
    inv_freq = 1.0 / (ROPE_THETA ** (jnp.arange(0, HEAD_DIM, 2, dtype=f32) / HEAD_DIM))
    cos, sin_s = _rope_tables(positions.reshape(s, 1), jnp.tile(inv_freq, 4).reshape(1, LANES), "rope_tables")

    h1, qa, ka, va, qb, kb, vb = _prenorm_proj_qkv(x, row(g_pre_mix), scale_m, shift_m, w_qkv_t, cos, sin_s, "prenorm_proj_qkv")
    gl = _matmul(h1, w_gate_t, "nt", bf16, "proj_gate")
    fl, cum_b = _forget_prep(h1, w_f_t, bf_row, "proj_forget_prep")
    o_a, lse_a = _attn_fwd(qa, ka, va, "swa_fwd", sink_rows=sink_rows, window=WINDOW, t=512)
    o_b, lse_b = _attn_fwd(qb, kb, vb, "fox_fwd", cum_b=cum_b, t=1024)
    everything_before = (gl[:8, :LANES] + o_a[:8, :LANES] + o_b[:8, :LANES]).astype(f32)
    w_branch_a, w_branch_b, w_out = mix_weights(everything_before)
    ba, bb, merged = _branch_merge(o_a, o_b, w_branch_a, w_branch_b, gl, "branch_merge")
    y1, x2, h2 = _out_proj_postnorm_prenorm(merged, w_out, x, row(g_post_mix), gate_m, row(g_pre_ffn), scale_f, shift_f,
                                            "out_proj_norms")

    w_ffn_in_t, w_ffn_out = ffn_weights(h2)
    g_ff, u_ff, act = _ffn_in_swiglu(h2, w_ffn_in_t, "ffn_in_swiglu")
    loss_row, d_out, d_y2, vec_pf = _out_proj_loss_tail(act, w_ffn_out, x2, row(g_post_ffn), gate_f, target, "ffn_out_loss_tail")

    g_w_ffn_out = _matmul(act, d_y2, "tn", bf16, "ffn_out_wgrad")
    dg_ff, du_ff = _ffn_out_dgrad_swiglu(d_y2, w_ffn_out, g_ff, u_ff, "ffn_out_dgrad_swiglu")
    g_w_ffn_in_t = _wgrad_stack([dg_ff, du_ff], h2, "ffn_in_wgrad")
    sent = on_grads(dict(w_ffn_in=g_w_ffn_in_t, w_ffn_out=g_w_ffn_out))
    d_x2, vec_nf, d_y1, vec_pm = _dgrad_prenorm_bwd(
        [(dg_ff, w_ffn_in_t, 0), (du_ff, w_ffn_in_t, 1)], x2, row(g_pre_ffn), scale_f, d_out, "ffn_in_dgrad_norms_bwd",
        after=sent, below=(y1, row(g_post_mix), gate_m))

    g_w_out = _matmul(merged, d_y1, "tn", bf16, "out_proj_wgrad")
    d_ba, d_bb, dgl = _out_dgrad_merge_bwd(d_y1, w_out, ba, bb, gl, "out_proj_dgrad_merge_bwd")
    g_w_branch_a = _matmul(o_a, d_ba, "tn", bf16, "branch_a_wgrad")
    g_w_branch_b = _matmul(o_b, d_bb, "tn", bf16, "branch_b_wgrad")
    sent = on_grads(dict(w_out=g_w_out, w_branch_a=g_w_branch_a, w_branch_b=g_w_branch_b))
    d_oa, delta_a, d_sink = _branch_dgrad_delta(d_ba, w_branch_a, o_a, "branch_a_dgrad_delta", lse=lse_a,
                                                sink_rows=sink_rows, after=sent)
    d_ob, delta_b = _branch_dgrad_delta(d_bb, w_branch_b, o_b, "branch_b_dgrad_delta", after=sent)
    dqa_t, dka, dva = _attn_bwd(qa, ka, va, d_oa, lse_a, delta_a, "swa_bwd", window=WINDOW, t=512)
    dqb_t, dkb, dvb, dcs, rs = _attn_bwd(qb, kb, vb, d_ob, lse_b, delta_b, "fox_bwd", cum_b=cum_b, t=512)
    dqkv = _qkv_prep_bwd(dqa_t, dka, dva, dqb_t, dkb, dvb, cos, sin_s, "qkv_prep_bwd")
    dfl, vec_bf = _forget_prep_bwd(rs.reshape(N_HEADS, s), dcs, fl, bf_row, "forget_prep_bwd")
    g_w_in_t = jnp.concatenate([_matmul(dqkv, h1, "tn", bf16, "qkv_wgrad"), _matmul(dfl, h1, "tn", bf16, "forget_wgrad")[:N_HEADS],
                                _matmul(dgl, h1, "tn", bf16, "gate_wgrad")], axis=0)
    sent = on_grads(dict(w_in=g_w_in_t))
    grad_x, vec_nm = _dgrad_prenorm_bwd([(dgl, w_gate_t, 0), (dqkv, w_qkv_t, 0), (dfl, w_f_t, 0)], x, row(g_pre_mix),
                                        scale_m, d_x2, "in_proj_dgrad_prenorm_bwd", after=sent)

    d_ada = jnp.concatenate([vec_nm[0], vec_nm[1], vec_pm[0], vec_nf[0], vec_nf[1], vec_pf[0]])
    small = dict(b_ada=d_ada, g_pre_mix=vec_nm[2], g_post_mix=vec_pm[1], g_pre_ffn=vec_nf[2], g_post_ffn=vec_pf[1],
                 b_f=vec_bf[0, :N_HEADS], sinks=d_sink[:, 0], loss=loss_row[0, :1])
    return grad_x, small


_SMALL = (("b_ada", 6144), ("g_pre_mix", 1024), ("g_post_mix", 1024), ("g_pre_ffn", 1024), ("g_post_ffn", 1024),
          ("b_f", 128), ("sinks", 128), ("loss", 128))
_SMALL_ROWS = 88


def _pack_small(vals):
    parts = [jnp.pad(vals[k].reshape(-1).astype(f32), (0, n - vals[k].size)) for k, n in _SMALL]
    flat = jnp.concatenate(parts)
    return jnp.pad(flat, (0, _SMALL_ROWS * LANES - flat.size)).reshape(_SMALL_ROWS, LANES)


def _unpack_small(slab, shapes):
    flat, out, off = slab.reshape(-1), {}, 0
    for k, n in _SMALL:
        size = math.prod(shapes[k])
        out[k] = flat[off:off + size].reshape(shapes[k])
        off += n
    return out


def kernel(x, c, positions, w_ada, b_ada, g_pre_mix, g_post_mix, w_in, b_f, sinks, w_branch_a, w_branch_b, w_out, g_pre_ffn, g_post_ffn, w_ffn_in, w_ffn_out, loss_target, m_w_ada, m_b_ada, m_g_pre_mix, m_g_post_mix, m_w_in, m_b_f, m_sinks, m_w_branch_a, m_w_branch_b, m_w_out, m_g_pre_ffn, m_g_post_ffn, m_w_ffn_in, m_w_ffn_out, v_w_ada, v_b_ada, v_g_pre_mix, v_g_post_mix, v_w_in, v_b_f, v_sinks, v_w_branch_a, v_w_branch_b, v_w_out, v_g_pre_ffn, v_g_post_ffn, v_w_ffn_in, v_w_ffn_out):
    xi, yi, ci = _me()
    me = 4 * xi + 2 * yi + ci
    d = D_MODEL
    ada_w = w_ada.shape[2]

    transposed = ("w_in", "w_ffn_in")
    tr = lambda a: jnp.transpose(a[0])

    b_mine = lax.dynamic_slice(b_ada, (0, me * ada_w), (1, ada_w))
    c_all, ada_all, g_in = _gather_prologue(c, w_ada[0], b_mine, tr(w_in).astype(bf16), "gather_prologue")
    c_all = c_all.reshape(N_DEV, d)
    ada = lax.dynamic_index_in_dim(ada_all, me, axis=1, keepdims=False).reshape(6, d)
    late_mix = [w.astype(bf16) for w in (w_branch_a[0], w_branch_b[0], w_out[0])]
    late_ffn = [w.astype(bf16) for w in (tr(w_ffn_in), w_ffn_out[0])]
    mix_h = _exchange_start(late_mix, False, "gather_mix_start", after=g_in)
    ffn_h = _exchange_start(late_ffn, False, "gather_ffn_start", after=mix_h["token"])

    def mine_into(zone, block):
        return lax.dynamic_update_index_in_dim(zone, block, me, 0)

    def rows_from_shards(g):
        return g.reshape(g.shape[0] * g.shape[1], g.shape[2])

    def mix_weights(after):
        zones = _exchange_wait(mix_h, after, "gather_mix_wait")
        g_ba, g_bb, g_out = (mine_into(z, w) for z, w in zip(zones, late_mix))
        return _cols_from_shards(g_ba), _cols_from_shards(g_bb), rows_from_shards(g_out)

    def ffn_weights(after):
        zones = _exchange_wait(ffn_h, after, "gather_ffn_wait")
        g_fi, g_fo = (mine_into(z, w) for z, w in zip(zones, late_ffn))
        return rows_from_shards(g_fi), rows_from_shards(g_fo)

    row_sharded = ("w_out", "w_ffn_out") + transposed
    in_flight = []

    def on_grads(group):
        sends = [g.reshape(N_DEV, g.shape[0] // N_DEV, g.shape[1]) if nm in row_sharded else _shards_from_cols(g)
                 for nm, g in group.items()]
        handle = _exchange_start(sends, True, "scatter_start_%d" % len(in_flight))
        in_flight.append((list(group), sends, handle))
        return handle["token"]

    grad_x, small = _local_step(
        x[0], positions[0], ada + ffn_h["token"][0, 0], g_pre_mix[0], g_post_mix[0], b_f[0], sinks[0], g_pre_ffn[0],
        g_post_ffn[0], loss_target[0], rows_from_shards(g_in), mix_weights, ffn_weights, on_grads)

    ws = dict(w_in=(w_in, m_w_in, v_w_in), w_branch_a=(w_branch_a, m_w_branch_a, v_w_branch_a),
              w_branch_b=(w_branch_b, m_w_branch_b, v_w_branch_b), w_out=(w_out, m_w_out, v_w_out),
              w_ffn_in=(w_ffn_in, m_w_ffn_in, v_w_ffn_in), w_ffn_out=(w_ffn_out, m_w_ffn_out, v_w_ffn_out))
    res = {}

    def finish_group(gi, after):
        names, sends, handle = in_flight[gi]
        zones = _exchange_wait(handle, after, "scatter_wait_%d" % gi)
        for nm, zone, sent in zip(names, zones, sends):
            w, m, v = (tr(a) if nm in transposed else a[0] for a in ws[nm])
            out = _adamw(zone, w, m, v, "adamw_" + nm, mine=sent, me=me.reshape(1).astype(jnp.int32))
            after = out[0]
            res[nm] = [jnp.transpose(o) for o in out] if nm in transposed else out
        return after

    done = finish_group(1, finish_group(0, grad_x))

    slab_all, = _all_gather([_pack_small(small)], "gather_small", vmem=True, after=done)
    small_w = dict(b_ada=b_ada, g_pre_mix=g_pre_mix, g_post_mix=g_post_mix, g_pre_ffn=g_pre_ffn, g_post_ffn=g_post_ffn,
                   b_f=b_f, sinks=sinks, loss=jnp.zeros((1,), f32))
    small_m = dict(b_ada=m_b_ada, g_pre_mix=m_g_pre_mix, g_post_mix=m_g_post_mix, g_pre_ffn=m_g_pre_ffn,
                   g_post_ffn=m_g_post_ffn, b_f=m_b_f, sinks=m_sinks, loss=jnp.zeros((1,), f32))
    small_v = dict(b_ada=v_b_ada, g_pre_mix=v_g_pre_mix, g_post_mix=v_g_post_mix, g_pre_ffn=v_g_pre_ffn,
                   g_post_ffn=v_g_post_ffn, b_f=v_b_f, sinks=v_sinks, loss=jnp.ones((1,), f32))
    shapes = {k: small_w[k].shape for k, _ in _SMALL}
    s_out = _adamw(slab_all, _pack_small(small_w), _pack_small(small_m), _pack_small(small_v), "adamw_small")
    s_grad, s_delta, s_m, s_v = (_unpack_small(o, shapes) for o in s_out)

    d_ada_all = lax.dynamic_slice(slab_all[:, :6144 // LANES, :].reshape(N_DEV, 6144), (0, me * ada_w), (N_DEV, ada_w))
    ada_parts = _ada_wgrad(c_all, d_ada_all, "ada_wgrad")

    res["w_ada"] = _adamw(ada_parts, w_ada[0], m_w_ada[0], v_w_ada[0], "adamw_w_ada")
    finish_group(2, res["w_ada"][0])

    order = ["w_ada", "b_ada", "g_pre_mix", "g_post_mix", "w_in", "b_f", "sinks", "w_branch_a", "w_branch_b", "w_out",
             "g_pre_ffn", "g_post_ffn", "w_ffn_in", "w_ffn_out"]
    outs = [s_grad["loss"].reshape(()), grad_x[None]]
    for which, small_o in enumerate((s_grad, s_delta, s_m, s_v)):
        for nm in order:
            outs.append(res[nm][which][None] if nm in res else small_o[nm])
    return tuple(outs)
```

```python
import math

import jax
import jax.numpy as jnp
from jax import lax
from jax.experimental import pallas as pl
from jax.experimental.pallas import tpu as pltpu

f32 = jnp.float32
bf16 = jnp.bfloat16

D_MODEL = 1024
HEAD_DIM = 64
N_HEADS = 8
N_PAIRS = 4
QKV_W = 2304
F_OFF = 2304
WINDOW = 128
ROPE_THETA = 10000.0
RMS_EPS = 1e-6
N_DEV = 8
ADAM_LR, ADAM_B1, ADAM_B2, ADAM_EPS, ADAM_WD, ADAM_STEP = 0.001, 0.9, 0.999, 1e-08, 0.01, 10
NEG = -1e30
LANES = 128
VMEM_LIMIT = 48 * 1024 * 1024
MESH = pl.DeviceIdType.MESH

_NT = (((1,), (1,)), ((), ()))
_TN = (((0,), (0,)), ((), ()))


def _params(n_grid=0):
    sem = ("arbitrary",) * n_grid if n_grid else None
    return pltpu.CompilerParams(dimension_semantics=sem, vmem_limit_bytes=VMEM_LIMIT)


def _row_tile(s, want):
    t = min(s, want)
    assert s % t == 0, (s, t)
    return t


MATMUL_VMEM_BUDGET = 40 * 1024 * 1024


def _matmul_tiles(m, n, k, a_item, b_item, o_item):
    def tiles(d):
        return [t for t in range(LANES, min(d, 2048) + 1, LANES) if d % t == 0] or [d]

    best = None
    for tm in tiles(m):
        for tn in tiles(n):
            vmem = 2 * (tm * k * a_item + tn * k * b_item + tm * tn * o_item) + tm * tn * 4
            if vmem > MATMUL_VMEM_BUDGET:
                continue
            traffic = m * k * a_item + n * k * b_item * (1 if tn == n else m // tm) + m * n * o_item
            steps = (m // tm) * (n // tn)
            key = (traffic, 0, steps) if steps >= 4 else (traffic, 1, -steps)
            if best is None or key < best[0]:
                best = (key, tm, tn)
    assert best is not None, (m, n, k)
    return best[1], best[2]


def _matmul(a, b, mode, out_dtype, name, after=None):
    if mode == "nn":
        (m, k), n = a.shape, b.shape[1]
    elif mode == "nt":
        (m, k), n = a.shape, b.shape[0]
    else:
        (k, m), n = a.shape, b.shape[1]
    tm, tn = _matmul_tiles(m, n, k, a.dtype.itemsize, b.dtype.itemsize, jnp.dtype(out_dtype).itemsize)
    if mode == "nn":
        a_spec, b_spec, dims = pl.BlockSpec((tm, k), lambda i, j: (i, 0)), pl.BlockSpec((k, tn), lambda i, j: (0, j)), None
    elif mode == "nt":
        a_spec, b_spec, dims = pl.BlockSpec((tm, k), lambda i, j: (i, 0)), pl.BlockSpec((tn, k), lambda i, j: (j, 0)), _NT
    else:
        a_spec, b_spec, dims = pl.BlockSpec((k, tm), lambda i, j: (0, i)), pl.BlockSpec((k, tn), lambda i, j: (0, j)), _TN

    def body(a_ref, b_ref, *rest):
        o_ref = rest[-1]
        av, bv = a_ref[...].astype(bf16), b_ref[...].astype(bf16)
        if dims is None:
            r = jnp.dot(av, bv, preferred_element_type=f32)
        else:
            r = lax.dot_general(av, bv, dims, preferred_element_type=f32)
        o_ref[...] = r.astype(out_dtype)

    extra = [] if after is None else [after]
    return pl.pallas_call(
        body, name=name, grid=(m // tm, n // tn), in_specs=[a_spec, b_spec] + [pl.BlockSpec(memory_space=pl.ANY)] * len(extra),
        out_specs=pl.BlockSpec((tm, tn), lambda i, j: (i, j)),
        out_shape=jax.ShapeDtypeStruct((m, n), out_dtype), compiler_params=_params(2),
    )(a, b, *extra)


def _rstd(v):
    return lax.rsqrt(jnp.mean(v * v, axis=-1, keepdims=True) + RMS_EPS)


def _row_spec(tm, d):
    return pl.BlockSpec((tm, d), lambda i: (i, 0))


def _vec_spec(d, rows=1):
    return pl.BlockSpec((rows, d), lambda i: (0, 0))


def _proj_spec(a, w, tm):
    return [_row_spec(tm, a.shape[1]), pl.BlockSpec(w.shape, lambda i: (0, 0))]


def _out_proj_postnorm_prenorm(a, w, x, g_post, gate, g_pre, scale, shift, name):
    s, d = x.shape
    tm = _row_tile(s, 512)

    def body(a_ref, w_ref, x_ref, gp_ref, gate_ref, g_ref, sc_ref, sh_ref, y_ref, x2_ref, h_ref):
        yv = jnp.dot(a_ref[...], w_ref[...], preferred_element_type=f32)
        y_ref[...] = yv
        x2 = x_ref[...] + gate_ref[...] * (yv * _rstd(yv) * gp_ref[...])
        x2_ref[...] = x2
        h_ref[...] = ((x2 * _rstd(x2) * g_ref[...]) * (1.0 + sc_ref[...]) + sh_ref[...]).astype(bf16)

    return pl.pallas_call(
        body, name=name, grid=(s // tm,), in_specs=_proj_spec(a, w, tm) + [_row_spec(tm, d)] + [_vec_spec(d)] * 5,
        out_specs=[_row_spec(tm, d)] * 3,
        out_shape=[jax.ShapeDtypeStruct((s, d), f32)] * 2 + [jax.ShapeDtypeStruct((s, d), bf16)], compiler_params=_params(1),
    )(a, w, x, g_post, gate, g_pre, scale, shift)


def _rms_bwd(u, v, r):
    return r * u - v * (r * r * r) * jnp.mean(u * v, axis=-1, keepdims=True)


def _out_proj_loss_tail(a, w, x, g, gate, target, name):
    s, d = x.shape
    tm = _row_tile(s, 512)

    def body(a_ref, w_ref, x_ref, g_ref, gate_ref, t_ref, loss_ref, do_ref, dy_ref, vec_ref):
        @pl.when(pl.program_id(0) == 0)
        def _():
            loss_ref[...] = jnp.zeros_like(loss_ref)
            vec_ref[...] = jnp.zeros_like(vec_ref)
        yv = jnp.dot(a_ref[...], w_ref[...], preferred_element_type=f32)
        r = _rstd(yv)
        yn = yv * r
        err = x_ref[...] + gate_ref[...] * (yn * g_ref[...]) - t_ref[...]
        loss_ref[...] += 0.5 * jnp.sum(jnp.mean(err * err, axis=-1, keepdims=True), axis=0, keepdims=True)
        dr = err / d
        do_ref[...] = dr
        dn = dr * gate_ref[...]
        vec_ref[0:1, :] += jnp.sum(dr * (yn * g_ref[...]), axis=0, keepdims=True)
        vec_ref[1:2, :] += jnp.sum(dn * yn, axis=0, keepdims=True)
        dy_ref[...] = _rms_bwd(dn * g_ref[...], yv, r).astype(bf16)

    return pl.pallas_call(
        body, name=name, grid=(s // tm,),
        in_specs=_proj_spec(a, w, tm) + [_row_spec(tm, d)] + [_vec_spec(d)] * 2 + [_row_spec(tm, d)],
        out_specs=[_vec_spec(LANES), _row_spec(tm, d), _row_spec(tm, d), _vec_spec(d, 8)],
        out_shape=[jax.ShapeDtypeStruct((1, LANES), f32), jax.ShapeDtypeStruct((s, d), f32),
                   jax.ShapeDtypeStruct((s, d), bf16), jax.ShapeDtypeStruct((8, d), f32)],
        compiler_params=_params(1),
    )(a, w, x, g, gate, target)


def _dgrad_prenorm_bwd(terms, x, g, scale, dres, name, after=None, below=None):
    s, d = x.shape
    n = len(terms)
    k = sum(a.shape[1] for a, _, _ in terms)
    row_bytes = 2 * (2 * k) + d * (4 + 2 * 4 * 3 + (2 * 4 + 2 * 2 if below else 0))
    tm = next(t for t in (512, 256, 128) if s % t == 0 and 4 * k * d + t * row_bytes <= MATMUL_VMEM_BUDGET)
    extra = [] if after is None else [after]

    def body(*refs):
        a_refs, b_refs = refs[:n], refs[n:2 * n]
        x_ref, g_ref, sc_ref, dr_ref = refs[2 * n:2 * n + 4]
        n_in = 2 * n + 4 + (3 if below else 0) + len(extra)
        dx_ref, vec_ref = refs[n_in], refs[n_in + 1]
        if below:
            y_ref, gp_ref, gate_ref = refs[2 * n + 4:2 * n + 7]
            dy_ref, vec2_ref = refs[n_in + 2], refs[n_in + 3]

        @pl.when(pl.program_id(0) == 0)
        def _():
            vec_ref[...] = jnp.zeros_like(vec_ref)
            if below:
                vec2_ref[...] = jnp.zeros_like(vec2_ref)
        dhv = jnp.dot(a_refs[0][...], b_refs[0][...], preferred_element_type=f32)
        for i in range(1, n):
            dhv = dhv + jnp.dot(a_refs[i][...], b_refs[i][...], preferred_element_type=f32)
        xv = x_ref[...]
        r = _rstd(xv)
        xn = xv * r
        dn = dhv * (1.0 + sc_ref[...])
        vec_ref[0:1, :] += jnp.sum(dhv, axis=0, keepdims=True)
        vec_ref[1:2, :] += jnp.sum(dhv * (xn * g_ref[...]), axis=0, keepdims=True)
        vec_ref[2:3, :] += jnp.sum(dn * xn, axis=0, keepdims=True)
        dx = dr_ref[...] + _rms_bwd(dn * g_ref[...], xv, r)
        dx_ref[...] = dx
        if below:
            yv = y_ref[...]
            ry = _rstd(yv)
            yn = yv * ry
            dny = dx * gate_ref[...]
            vec2_ref[0:1, :] += jnp.sum(dx * (yn * gp_ref[...]), axis=0, keepdims=True)
            vec2_ref[1:2, :] += jnp.sum(dny * yn, axis=0, keepdims=True)
            dy_ref[...] = _rms_bwd(dny * gp_ref[...], yv, ry).astype(bf16)

    in_specs = ([_row_spec(tm, a.shape[1]) for a, _, _ in terms]
                + [pl.BlockSpec((a.shape[1], d), lambda i, r=r: (r, 0)) for a, _, r in terms]
                + [_row_spec(tm, d)] + [_vec_spec(d)] * 2 + [_row_spec(tm, d)])
    out_specs = [_row_spec(tm, d), _vec_spec(d, 8)]
    out_shape = [jax.ShapeDtypeStruct((s, d), f32), jax.ShapeDtypeStruct((8, d), f32)]
    args = [a for a, _, _ in terms] + [b for _, b, _ in terms] + [x, g, scale, dres]
    if below:
        in_specs += [_row_spec(tm, d)] + [_vec_spec(d)] * 2
        out_specs += [_row_spec(tm, d), _vec_spec(d, 8)]
        out_shape += [jax.ShapeDtypeStruct((s, d), bf16), jax.ShapeDtypeStruct((8, d), f32)]
        args += list(below)
    return pl.pallas_call(
        body, name=name, grid=(s // tm,), in_specs=in_specs + [pl.BlockSpec(memory_space=pl.ANY)] * len(extra),
        out_specs=out_specs, out_shape=out_shape, compiler_params=_params(1),
    )(*args, *extra)


def _lane():
    return lax.broadcasted_iota(jnp.int32, (1, LANES), 1)


def _rope_tables(pos_col, inv_freq, name):
    s = pos_col.shape[0]

    def body(p_ref, f_ref, cos_ref, sin_ref):
        ang = p_ref[...].astype(f32) * f_ref[...]
        first_half = (_lane() % HEAD_DIM) < HEAD_DIM // 2
        cos_ref[...] = jnp.cos(ang)
        sn = jnp.sin(ang)
        sin_ref[...] = jnp.where(first_half, -sn, sn)

    return pl.pallas_call(
        body, name=name, out_shape=[jax.ShapeDtypeStruct((s, LANES), f32)] * 2, compiler_params=_params(),
    )(pos_col, inv_freq)


def _swap_halves(v):
    first_half = (_lane() % HEAD_DIM) < HEAD_DIM // 2
    return jnp.where(first_half, pltpu.roll(v, LANES - HEAD_DIM // 2, axis=1), pltpu.roll(v, HEAD_DIM // 2, axis=1))


def _prenorm_proj_qkv(x, g, mod_scale, mod_shift, w_qkv_t, cos, sin_s, name):
    s, d = x.shape
    tm = _row_tile(s, 512)
    scale = 1.0 / math.sqrt(HEAD_DIM)

    def body(x_ref, g_ref, msc_ref, msh_ref, w_ref, c_ref, s_ref, h_ref, qa_ref, ka_ref, va_ref, qb_ref, kb_ref, vb_ref):
        xv = x_ref[...]
        h = ((xv * _rstd(xv) * g_ref[...]) * (1.0 + msc_ref[...]) + msh_ref[...]).astype(bf16)
        h_ref[...] = h
        proj = lax.dot_general(h, w_ref[...], _NT, preferred_element_type=f32)
        cs, sn = c_ref[...], s_ref[...]
        low = _lane() < HEAD_DIM

        def blk(j):
            return proj[:, j * LANES:(j + 1) * LANES]

        def rope(v):
            return v * cs + _swap_halves(v) * sn

        def expand(v):
            other = pltpu.roll(v, HEAD_DIM, axis=1)
            return jnp.where(low, v, other), jnp.where(low, other, v)

        for j in range(N_PAIRS):
            qa_ref[:, j * LANES:(j + 1) * LANES] = (rope(blk(j)) * scale).astype(bf16)
            qb_ref[:, j * LANES:(j + 1) * LANES] = (blk(6 + j) * scale).astype(bf16)
            kb_ref[:, j * LANES:(j + 1) * LANES] = blk(10 + j).astype(bf16)
            vb_ref[:, j * LANES:(j + 1) * LANES] = blk(14 + j).astype(bf16)
        k0, k1 = expand(rope(blk(4)))
        v0, v1 = expand(blk(5))
        for j in range(N_PAIRS):
            ka_ref[:, j * LANES:(j + 1) * LANES] = (k0 if j < 2 else k1).astype(bf16)
            va_ref[:, j * LANES:(j + 1) * LANES] = (v0 if j < 2 else v1).astype(bf16)

    hw = N_PAIRS * LANES
    return pl.pallas_call(
        body, name=name, grid=(s // tm,),
        in_specs=[_row_spec(tm, d)] + [_vec_spec(d)] * 3
        + [pl.BlockSpec((QKV_W, d), lambda i: (0, 0)), _row_spec(tm, LANES), _row_spec(tm, LANES)],
        out_specs=[_row_spec(tm, d)] + [_row_spec(tm, hw)] * 6,
        out_shape=[jax.ShapeDtypeStruct((s, d), bf16)] + [jax.ShapeDtypeStruct((s, hw), bf16)] * 6, compiler_params=_params(1),
    )(x, g, mod_scale, mod_shift, w_qkv_t, cos, sin_s)


def _qkv_prep_bwd(dqa_t, dka, dva, dqb_t, dkb, dvb, cos, sin_s, name):
    s = dka.shape[0]
    tm = _row_tile(s, 256)
    scale = 1.0 / math.sqrt(HEAD_DIM)
    hw = N_PAIRS * LANES
    t_spec = pl.BlockSpec((hw, tm), lambda i: (0, i))

    def body(dqa_ref, dka_ref, dva_ref, dqb_ref, dkb_ref, dvb_ref, c_ref, s_ref, o_ref):
        cs, sn = c_ref[...], s_ref[...]
        low = _lane() < HEAD_DIM

        def blk(ref, j):
            return ref[:, j * LANES:(j + 1) * LANES].astype(f32)

        def blk_t(ref, j):
            return ref[j * LANES:(j + 1) * LANES, :].T

        def unrope(v):
            return v * cs + _swap_halves(v * sn)

        def fold(ref):
            a, b = blk(ref, 0) + blk(ref, 1), blk(ref, 2) + blk(ref, 3)
            kv0 = a + pltpu.roll(a, HEAD_DIM, axis=1)
            kv1 = b + pltpu.roll(b, HEAD_DIM, axis=1)
            return jnp.where(low, kv0, kv1)

        for j in range(N_PAIRS):
            o_ref[:, j * LANES:(j + 1) * LANES] = (unrope(blk_t(dqa_ref, j)) * scale).astype(bf16)
            o_ref[:, (6 + j) * LANES:(7 + j) * LANES] = (blk_t(dqb_ref, j) * scale).astype(bf16)
            o_ref[:, (10 + j) * LANES:(11 + j) * LANES] = blk(dkb_ref, j).astype(bf16)
            o_ref[:, (14 + j) * LANES:(15 + j) * LANES] = blk(dvb_ref, j).astype(bf16)
        o_ref[:, 4 * LANES:5 * LANES] = unrope(fold(dka_ref)).astype(bf16)
        o_ref[:, 5 * LANES:6 * LANES] = fold(dva_ref).astype(bf16)

    return pl.pallas_call(
        body, name=name, grid=(s // tm,),
        in_specs=[t_spec, _row_spec(tm, hw), _row_spec(tm, hw), t_spec, _row_spec(tm, hw), _row_spec(tm, hw)] + [_row_spec(tm, LANES)] * 2,
        out_specs=_row_spec(tm, QKV_W), out_shape=jax.ShapeDtypeStruct((s, QKV_W), bf16), compiler_params=_params(1),
    )(dqa_t, dka, dva, dqb_t, dkb, dvb, cos, sin_s)


def _cumsum_rows(v, reverse=False):
    n = v.shape[0]
    row = lax.broadcasted_iota(jnp.int32, v.shape, 0)
    sh = 1
    while sh < n:
        if reverse:
            v = v + jnp.where(row < n - sh, pltpu.roll(v, n - sh, axis=0), 0.0)
        else:
            v = v + jnp.where(row >= sh, pltpu.roll(v, sh, axis=0), 0.0)
        sh *= 2
    return v


def _log_sigmoid(z):
    return jnp.minimum(z, 0.0) - jnp.log1p(jnp.exp(-jnp.abs(z)))


def _forget_prep(h, w_f_t, bf_row, name):
    s = h.shape[0]

    def body(h_ref, w_ref, b_ref, f_ref, cb_ref):
        fl = lax.dot_general(h_ref[...], w_ref[...], _NT, preferred_element_type=f32)
        f_ref[...] = fl
        cum = _cumsum_rows(_log_sigmoid(fl + b_ref[...]))
        for hd in range(N_HEADS):
            cb_ref[:, hd * LANES:(hd + 1) * LANES] = jnp.broadcast_to(cum[:, hd:hd + 1], (s, LANES))

    return pl.pallas_call(
        body, name=name,
        out_shape=[jax.ShapeDtypeStruct((s, LANES), f32), jax.ShapeDtypeStruct((s, N_HEADS * LANES), f32)],
        compiler_params=_params(),
    )(h, w_f_t, bf_row)


def _forget_prep_bwd(rs, dcs, fl, bf_row, name):
    s = fl.shape[0]

    def body(r_ref, c_ref, f_ref, b_ref, df_ref, db_ref):
        eye = (lax.broadcasted_iota(jnp.int32, (N_HEADS, LANES), 0) == lax.broadcasted_iota(jnp.int32, (N_HEADS, LANES), 1)).astype(f32)
        dcum = lax.dot_general(r_ref[...], eye, _TN, precision=lax.Precision.HIGHEST, preferred_element_type=f32)
        for h in range(N_HEADS):
            dcum = dcum - jnp.where(_lane() == h, jnp.sum(c_ref[:, h * LANES:(h + 1) * LANES], axis=1, keepdims=True), 0.0)
        dlf = _cumsum_rows(dcum, reverse=True)
        z = f_ref[...] + b_ref[...]
        df = jnp.where(_lane() < N_HEADS, dlf * jax.nn.sigmoid(-z), 0.0)
        df_ref[...] = df.astype(bf16)
        db_ref[...] = jnp.zeros_like(db_ref)
        db_ref[0:1, :] = jnp.sum(df, axis=0, keepdims=True)

    return pl.pallas_call(
        body, name=name,
        out_shape=[jax.ShapeDtypeStruct((s, LANES), bf16), jax.ShapeDtypeStruct((8, LANES), f32)], compiler_params=_params(),
    )(rs, dcs, fl, bf_row)


def _tile_mask(n_keys, n_queries, off, window):
    shape = (n_keys, n_queries)
    d = lax.broadcasted_iota(jnp.int32, shape, 1) - lax.broadcasted_iota(jnp.int32, shape, 0) + off
    valid = d >= 0
    return jnp.logical_and(valid, d < window) if window else valid


def _wide(v, t):
    return jnp.concatenate([v] * (t // LANES), axis=1)


def _attn_fwd(q, k, v, name, *, cum_b=None, sink_rows=None, window=None, t=256):
    s = q.shape[0]
    t = _row_tile(s, t)
    fox, has_sink = cum_b is not None, sink_rows is not None
    assert not window or (window % LANES == 0 and LANES + window <= s)

    def body(*refs):
        q_ref, k_ref, v_ref = refs[:3]
        rest = list(refs[3:])
        cb_ref = rest.pop(0) if fox else None
        sink_ref = rest.pop(0) if has_sink else None
        o_ref, lse_ref = rest
        i = pl.program_id(1)
        low = _lane() < HEAD_DIM
        top = lax.broadcasted_iota(jnp.int32, (LANES, 1), 0) < HEAD_DIM
        q2 = q_ref[...]
        zero = jnp.zeros_like(q2)
        qms = (jnp.where(low, q2, zero), jnp.where(low, zero, q2))

        def tile(k0, n_keys, off, carry, masked, queries=slice(0, t)):
            nq = queries.stop - queries.start
            kblk, vblk = k_ref[pl.ds(k0, n_keys), :], v_ref[pl.ds(k0, n_keys), :]
            valid = _tile_mask(n_keys, nq, off, window) if masked else None
            def scores(h):
                return lax.dot_general(kblk, qms[h][queries], _NT, preferred_element_type=f32)

            def softmax(h, sc):
                m, l, _ = carry[h]
                if fox:
                    sc = sc - _wide(cb_ref[pl.ds(k0, n_keys), h * LANES:(h + 1) * LANES], nq)
                if masked:
                    sc = jnp.where(valid, sc, NEG)
                m_new = jnp.maximum(m, jnp.max(sc, axis=0, keepdims=True))
                p = jnp.exp(sc - m_new)
                alpha = jnp.exp(m - m_new)
                return m_new, alpha * l + jnp.sum(p, axis=0, keepdims=True), alpha, p.astype(bf16)

            def update(h, m_new, l, alpha, p):
                return m_new, l, alpha * carry[h][2] + lax.dot_general(vblk, p, _TN, preferred_element_type=f32)

            if window:
                return tuple(update(h, *softmax(h, scores(h))) for h in range(2))
            scs = [scores(h) for h in range(2)]
            stats = [softmax(h, scs[h]) for h in range(2)]
            return tuple(update(h, *stats[h]) for h in range(2))

        def start(nq):
            if has_sink:
                return tuple((_wide(sink_ref[h:h + 1, :], nq), jnp.ones((1, nq), f32), jnp.zeros((LANES, nq), f32))
                             for h in range(2))
            return tuple((jnp.full((1, nq), NEG, f32), jnp.zeros((1, nq), f32), jnp.zeros((LANES, nq), f32)) for h in range(2))

        def finish(carry, queries):
            (m0, l0, a0), (m1, l1, a1) = carry
            o_t = jnp.where(top, a0 * (1.0 / l0), a1 * (1.0 / l1))
            o_ref[queries, :] = o_t.T.astype(bf16)
            lse_ref[0:1, queries] = m0 + jnp.log(l0)
            lse_ref[1:2, queries] = m1 + jnp.log(l1)

        if window:
            for c in range(t // LANES):
                queries = slice(c * LANES, (c + 1) * LANES)
                q0 = i * t + c * LANES
                k0 = pl.multiple_of(jnp.maximum(q0 - window, 0), LANES)
                finish(tile(k0, LANES + window, q0 - k0, start(LANES), True, queries), queries)
        else:
            carry = lax.fori_loop(0, i, lambda kb, c: tile(pl.multiple_of(kb * t, t), t, 0, c, False), start(t))
            finish(tile(pl.multiple_of(i * t, t), t, 0, carry, True), slice(0, t))

    q_spec = pl.BlockSpec((t, LANES), lambda j, i: (i, j))
    kv_spec = pl.BlockSpec((s, LANES), lambda j, i: (0, j))
    in_specs, args = [q_spec, kv_spec, kv_spec], [q, k, v]
    if fox:
        in_specs += [pl.BlockSpec((s, 2 * LANES), lambda j, i: (0, j))]
        args += [cum_b]
    if has_sink:
        in_specs += [pl.BlockSpec((None, 2, LANES), lambda j, i: (j, 0, 0))]
        args += [sink_rows.reshape(N_PAIRS, 2, LANES)]
    return pl.pallas_call(
        body, name=name, grid=(N_PAIRS, s // t), in_specs=in_specs,
        out_specs=[q_spec, pl.BlockSpec((None, 2, t), lambda j, i: (j, 0, i))],
        out_shape=[jax.ShapeDtypeStruct((s, N_PAIRS * LANES), bf16), jax.ShapeDtypeStruct((N_PAIRS, 2, s), f32)],
        compiler_params=_params(2),
    )(*args)


def _branch_dgrad_delta(db, w, o, name, *, lse=None, sink_rows=None, after=None):
    s, hw = o.shape
    tm = _row_tile(s, 512)
    has_sink = sink_rows is not None
    extra = [] if after is None else [after]

    def body(*refs):
        db_ref, w_ref, o_ref = refs[:3]
        outs = refs[3 + (2 if has_sink else 0) + len(extra):]
        do_ref, dl_ref = outs[:2]
        if has_sink:
            lse_ref, sink_ref = refs[3:5]
            ds_ref = outs[2]

            @pl.when(pl.program_id(0) == 0)
            def _():
                ds_ref[...] = jnp.zeros_like(ds_ref)
        do = lax.dot_general(db_ref[...], w_ref[...], _NT, preferred_element_type=f32).astype(bf16)
        do_ref[...] = do
        for j in range(N_PAIRS):
            cols = slice(j * LANES, (j + 1) * LANES)
            prod_t = (do[:, cols].astype(f32) * o_ref[:, cols].astype(f32)).T
            for h in range(2):
                dl = jnp.sum(prod_t[h * HEAD_DIM:(h + 1) * HEAD_DIM, :], axis=0, keepdims=True)
                dl_ref[j, h:h + 1, :] = dl
                if has_sink:
                    r = 2 * j + h
                    p_sink = jnp.exp(sink_ref[r:r + 1, 0:1] - lse_ref[j, h:h + 1, :])
                    ds_ref[r:r + 1, :] += -jnp.sum(p_sink * dl, axis=1, keepdims=True)

    rows_spec = pl.BlockSpec((N_PAIRS, 2, tm), lambda i: (0, 0, i))
    in_specs = [_row_spec(tm, db.shape[1]), pl.BlockSpec(w.shape, lambda i: (0, 0)), _row_spec(tm, hw)]
    args = [db, w, o]
    out_specs = [_row_spec(tm, hw), rows_spec]
    out_shape = [jax.ShapeDtypeStruct((s, hw), bf16), jax.ShapeDtypeStruct((N_PAIRS, 2, s), f32)]
    if has_sink:
        in_specs += [rows_spec, _vec_spec(LANES, N_HEADS)]
        args += [lse, sink_rows]
        out_specs += [_vec_spec(LANES, N_HEADS)]
        out_shape += [jax.ShapeDtypeStruct((N_HEADS, LANES), f32)]
    return pl.pallas_call(
        body, name=name, grid=(s // tm,), in_specs=in_specs + [pl.BlockSpec(memory_space=pl.ANY)] * len(extra),
        out_specs=out_specs, out_shape=out_shape, compiler_params=_params(1),
    )(*args, *extra)


def _attn_bwd(q, k, v, do, lse, delta, name, *, cum_b=None, window=None, t=256):
    s = q.shape[0]
    t = _row_tile(s, t)
    nblk = s // t
    fox = cum_b is not None
    assert not window or (window % LANES == 0 and LANES + window <= s)

    def body(*refs):
        k_ref, v_ref, q_ref, do_ref, lse_ref, dl_ref = refs[:6]
        rest = list(refs[6:])
        cb_ref = rest.pop(0) if fox else None
        dq_ref, dk_ref, dv_ref = rest[:3]
        dcs_ref, rs_ref = (rest[3], rest[4]) if fox else (None, None)
        dk_acc, dv_acc = rest[-2:]
        b = pl.program_id(1)
        k0 = pl.multiple_of(b * t, t)

        @pl.when(b == 0)
        def _():
            dq_ref[...] = jnp.zeros_like(dq_ref)
            if fox:
                rs_ref[...] = jnp.zeros_like(rs_ref)

        dk_acc[...] = jnp.zeros_like(dk_acc)
        dv_acc[...] = jnp.zeros_like(dv_acc)
        if fox:
            dcs_ref[...] = jnp.zeros_like(dcs_ref)
        low = _lane() < HEAD_DIM
        top = lax.broadcasted_iota(jnp.int32, (LANES, 1), 0) < HEAD_DIM
        kblk, vblk = k_ref[...], v_ref[...]
        k_t = kblk.astype(f32).T.astype(bf16)
        cks = [_wide(cb_ref[pl.ds(k0, t), h * LANES:(h + 1) * LANES], t) for h in range(2)] if fox else None

        def tile(q0, n_queries, off, masked, keys=slice(0, t)):
            cols = pl.ds(q0, n_queries)
            q2, do2 = q_ref[cols, :], do_ref[cols, :]
            zero = jnp.zeros_like(q2)
            valid = _tile_mask(keys.stop - keys.start, n_queries, off, window) if masked else None
            dq_parts = []
            for h in range(2):
                qm = jnp.where(low, q2, zero) if h == 0 else jnp.where(low, zero, q2)
                dom = jnp.where(low, do2, zero) if h == 0 else jnp.where(low, zero, do2)
                sc = lax.dot_general(kblk[keys], qm, _NT, preferred_element_type=f32)
                if fox:
                    sc = sc - cks[h]
                if masked:
                    sc = jnp.where(valid, sc, NEG)
                p = jnp.exp(sc - lse_ref[h:h + 1, cols])
                dp = lax.dot_general(vblk[keys], dom, _NT, preferred_element_type=f32)
                ds = p * (dp - dl_ref[h:h + 1, cols])
                pb, dsb = p.astype(bf16), ds.astype(bf16)
                dv_acc[keys, :] += jnp.dot(pb, dom, preferred_element_type=f32)
                dk_acc[keys, :] += jnp.dot(dsb, qm, preferred_element_type=f32)
                dq_parts.append(jnp.dot(k_t[:, keys], dsb, preferred_element_type=f32))
                if fox:
                    dcs_ref[:, h * LANES:(h + 1) * LANES] += sum(ds[:, g * LANES:(g + 1) * LANES] for g in range(t // LANES))
                    rs_ref[h:h + 1, cols] += jnp.sum(ds, axis=0, keepdims=True)
            dq_ref[:, cols] += jnp.where(top, dq_parts[0], dq_parts[1])

        def later_block(qb, carry):
            tile(pl.multiple_of(qb * t, t), t, 0, False)
            return carry

        if window:
            for c in range(t // LANES):
                first = b * t + c * LANES
                q0 = pl.multiple_of(jnp.minimum(first, s - (LANES + window)), LANES)
                tile(q0, LANES + window, q0 - first, True, slice(c * LANES, (c + 1) * LANES))
        else:
            tile(k0, t, 0, True)
            lax.fori_loop(b + 1, nblk, later_block, 0)
        dk_ref[...] = dk_acc[...].astype(bf16)
        dv_ref[...] = dv_acc[...].astype(bf16)

    kv_spec = pl.BlockSpec((t, LANES), lambda j, b: (b, j))
    seq_spec = pl.BlockSpec((s, LANES), lambda j, b: (0, j))
    rows_spec = pl.BlockSpec((None, 2, s), lambda j, b: (j, 0, 0))
    hw = N_PAIRS * LANES
    in_specs, args = [kv_spec, kv_spec, seq_spec, seq_spec, rows_spec, rows_spec], [k, v, q, do, lse, delta]
    out_specs = [pl.BlockSpec((LANES, s), lambda j, b: (j, 0)), kv_spec, kv_spec]
    out_shape = [jax.ShapeDtypeStruct((hw, s), f32), jax.ShapeDtypeStruct((s, hw), bf16), jax.ShapeDtypeStruct((s, hw), bf16)]
    if fox:
        in_specs += [pl.BlockSpec((s, 2 * LANES), lambda j, b: (0, j))]
        args += [cum_b]
        out_specs += [pl.BlockSpec((t, 2 * LANES), lambda j, b: (b, j)), rows_spec]
        out_shape += [jax.ShapeDtypeStruct((s, N_HEADS * LANES), f32), jax.ShapeDtypeStruct((N_PAIRS, 2, s), f32)]
    return pl.pallas_call(
        body, name=name, grid=(N_PAIRS, nblk), in_specs=in_specs, out_specs=out_specs, out_shape=out_shape,
        scratch_shapes=[pltpu.VMEM((t, LANES), f32)] * 2, compiler_params=_params(2),
    )(*args)


def _branch_merge(o_a, o_b, w_a, w_b, gl, name):
    s, k = o_a.shape
    d = w_a.shape[1]
    tm = _row_tile(s, 1024)

    def body(oa_ref, ob_ref, wa_ref, wb_ref, g_ref, ba_ref, bb_ref, m_ref):
        ba = jnp.dot(oa_ref[...], wa_ref[...], preferred_element_type=f32)
        bb = jnp.dot(ob_ref[...], wb_ref[...], preferred_element_type=f32)
        g0, g1 = jax.nn.sigmoid(g_ref[:, :d].astype(f32)), jax.nn.sigmoid(g_ref[:, d:].astype(f32))
        ba_ref[...] = ba.astype(bf16)
        bb_ref[...] = bb.astype(bf16)
        m_ref[...] = (g0 * ba + g1 * bb).astype(bf16)

    whole = pl.BlockSpec((k, d), lambda i: (0, 0))
    return pl.pallas_call(
        body, name=name, grid=(s // tm,),
        in_specs=[_row_spec(tm, k), _row_spec(tm, k), whole, whole, _row_spec(tm, 2 * d)],
        out_specs=[_row_spec(tm, d)] * 3, out_shape=[jax.ShapeDtypeStruct((s, d), bf16)] * 3, compiler_params=_params(1),
    )(o_a, o_b, w_a, w_b, gl)


def _out_dgrad_merge_bwd(dy, w_out, ba, bb, gl, name):
    s, d = ba.shape
    tm = _row_tile(s, 512)

    def body(dy_ref, w_ref, a_ref, b_ref, g_ref, da_ref, db_ref, dg_ref):
        dmv = lax.dot_general(dy_ref[...], w_ref[...], _NT, preferred_element_type=f32)
        g0, g1 = jax.nn.sigmoid(g_ref[:, :d].astype(f32)), jax.nn.sigmoid(g_ref[:, d:].astype(f32))
        da_ref[...] = (dmv * g0).astype(bf16)
        db_ref[...] = (dmv * g1).astype(bf16)
        dg_ref[:, :d] = (dmv * a_ref[...].astype(f32) * (g0 * (1.0 - g0))).astype(bf16)
        dg_ref[:, d:] = (dmv * b_ref[...].astype(f32) * (g1 * (1.0 - g1))).astype(bf16)

    return pl.pallas_call(
        body, name=name, grid=(s // tm,),
        in_specs=[_row_spec(tm, dy.shape[1]), pl.BlockSpec(w_out.shape, lambda i: (0, 0))] + [_row_spec(tm, d)] * 2
        + [_row_spec(tm, 2 * d)],
        out_specs=[_row_spec(tm, d)] * 2 + [_row_spec(tm, 2 * d)],
        out_shape=[jax.ShapeDtypeStruct((s, d), bf16)] * 2 + [jax.ShapeDtypeStruct((s, 2 * d), bf16)],
        compiler_params=_params(1),
    )(dy, w_out, ba, bb, gl)


GLU_TILE = 256


def _ffn_in_swiglu(h, w_t, name):
    s, d = h.shape
    f = w_t.shape[0] // 2
    tm = _row_tile(s, 2048)
    tg = GLU_TILE
    nb = f // tg

    def body(h_ref, wg_ref, wu_ref, g_ref, u_ref, act_ref):
        hv = h_ref[...]
        g = lax.dot_general(hv, wg_ref[...], _NT, preferred_element_type=f32)
        u = lax.dot_general(hv, wu_ref[...], _NT, preferred_element_type=f32)
        g_ref[...] = g.astype(bf16)
        u_ref[...] = u.astype(bf16)
        act_ref[...] = (g * jax.nn.sigmoid(g) * u).astype(bf16)

    col = pl.BlockSpec((tm, tg), lambda i, j: (i, j))
    return pl.pallas_call(
        body, name=name, grid=(s // tm, nb),
        in_specs=[pl.BlockSpec((tm, d), lambda i, j: (i, 0)), pl.BlockSpec((tg, d), lambda i, j: (j, 0)),
                  pl.BlockSpec((tg, d), lambda i, j: (j + nb, 0))],
        out_specs=[col] * 3, out_shape=[jax.ShapeDtypeStruct((s, f), bf16)] * 3, compiler_params=_params(2),
    )(h, w_t, w_t)


def _ffn_out_dgrad_swiglu(dy, w_out, g, u, name):
    s, d = dy.shape
    f = g.shape[1]
    tm = _row_tile(s, 2048)
    tg = GLU_TILE

    def body(dy_ref, w_ref, g_ref, u_ref, dg_ref, du_ref):
        dv = lax.dot_general(dy_ref[...], w_ref[...], _NT, preferred_element_type=f32)
        gv, uv = g_ref[...].astype(f32), u_ref[...].astype(f32)
        sg = jax.nn.sigmoid(gv)
        dg_ref[...] = (dv * uv * (sg * (1.0 + gv * (1.0 - sg)))).astype(bf16)
        du_ref[...] = (dv * (gv * sg)).astype(bf16)

    col = pl.BlockSpec((tm, tg), lambda i, j: (i, j))
    return pl.pallas_call(
        body, name=name, grid=(s // tm, f // tg),
        in_specs=[pl.BlockSpec((tm, d), lambda i, j: (i, 0)), pl.BlockSpec((tg, d), lambda i, j: (j, 0)), col, col],
        out_specs=[col] * 2, out_shape=[jax.ShapeDtypeStruct((s, f), bf16)] * 2, compiler_params=_params(2),
    )(dy, w_out, g, u)


def _wgrad_stack(parts, h, name):
    s, m = parts[0].shape
    d = h.shape[1]
    tm = 256
    nb = m // tm
    n = len(parts)

    def body(*refs):
        i = pl.program_id(0)
        for p in range(n):
            @pl.when(i // nb == p)
            def _(p=p):
                refs[n + 1][...] = lax.dot_general(refs[p][...], refs[n][...], _TN, preferred_element_type=f32).astype(bf16)

    a_specs = [pl.BlockSpec((s, tm), lambda i, p=p: (0, jnp.clip(i - p * nb, 0, nb - 1))) for p in range(n)]
    return pl.pallas_call(
        body, name=name, grid=(n * nb,), in_specs=a_specs + [pl.BlockSpec((s, d), lambda i: (0, 0))],
        out_specs=pl.BlockSpec((tm, d), lambda i: (i, 0)),
        out_shape=jax.ShapeDtypeStruct((n * m, d), bf16), compiler_params=_params(1),
    )(*parts, h)


def _ada_wgrad(c_all, d_all, name):
    n, d = c_all.shape
    w = d_all.shape[1]

    def body(c_ref, d_ref, o_ref):
        eye = (lax.broadcasted_iota(jnp.int32, (n, n), 0) == lax.broadcasted_iota(jnp.int32, (n, n), 1)).astype(f32)
        ct = lax.dot_general(c_ref[...], eye, _TN, precision=lax.Precision.HIGHEST, preferred_element_type=f32)
        g = ct[:, 0:1] * d_ref[0:1, :]
        for bi in range(1, n):
            g = g + ct[:, bi:bi + 1] * d_ref[bi:bi + 1, :]
        o_ref[0] = g

    return pl.pallas_call(
        body, name=name, out_shape=jax.ShapeDtypeStruct((1, d, w), f32), compiler_params=_params(),
    )(c_all, d_all)


def _adamw(parts, w, m, v, name, mine=None, me=None):
    r, c = w.shape
    n_parts = parts.shape[0]
    row_tiles = [t for t in range(min(r, 256), 0, -1) if r % t == 0 and (t % 16 == 0 or t == r)]
    if row_tiles:
        tr, tc = row_tiles[0], c
    else:
        tr, tc = r, next(t for t in (256, LANES) if c % t == 0)

    def body(*refs):
        w_ref, m_ref, v_ref, g_ref, d_ref, nm_ref, nv_ref = refs[-7:]
        if mine is None:
            p_ref, = refs[:-7]
        else:
            me_ref, p_ref, own_ref = refs[:-7]

        def part(i):
            if mine is None:
                return p_ref[i].astype(f32)
            return jnp.where(me_ref[0] == i, own_ref[...], p_ref[i]).astype(f32)

        g = part(0)
        for i in range(1, n_parts):
            g = g + part(i)
        mm = ADAM_B1 * m_ref[...] + (1.0 - ADAM_B1) * g
        vv = ADAM_B2 * v_ref[...] + (1.0 - ADAM_B2) * (g * g)
        m_hat = mm / (1.0 - ADAM_B1 ** ADAM_STEP)
        v_hat = vv / (1.0 - ADAM_B2 ** ADAM_STEP)
        g_ref[...] = g
        d_ref[...] = -ADAM_LR * (m_hat / (jnp.sqrt(v_hat) + ADAM_EPS) + ADAM_WD * w_ref[...])
        nm_ref[...] = mm
        nv_ref[...] = vv

    out_shape = [jax.ShapeDtypeStruct((r, c), f32)] * 4
    if mine is None:
        spec = pl.BlockSpec((tr, tc), lambda i, j: (i, j))
        return pl.pallas_call(
            body, name=name, grid=(r // tr, c // tc),
            in_specs=[pl.BlockSpec((n_parts, tr, tc), lambda i, j: (0, i, j))] + [spec] * 3,
            out_specs=[spec] * 4, out_shape=out_shape, compiler_params=_params(2),
        )(parts, w, m, v)
    spec = pl.BlockSpec((tr, tc), lambda i, j, me_ref: (i, j))
    return pl.pallas_call(
        body, name=name, out_shape=out_shape, compiler_params=_params(2),
        grid_spec=pltpu.PrefetchScalarGridSpec(
            num_scalar_prefetch=1, grid=(r // tr, c // tc),
            in_specs=[pl.BlockSpec((n_parts, tr, tc), lambda i, j, me_ref: (0, i, j)),
                      pl.BlockSpec((None, tr, tc), lambda i, j, me_ref: (me_ref[0], i, j))] + [spec] * 3,
            out_specs=[spec] * 4),
    )(me, parts, mine, w, m, v)


def _me():
    return lax.axis_index("x"), lax.axis_index("y"), lax.axis_index("c")


def _all_gather(arrays, name, vmem=False, after=None):
    n = len(arrays)
    space = pltpu.VMEM if vmem else pl.ANY
    extra = [] if after is None else [after]

    def body(*refs):
        ins = refs[:n]
        outs = refs[n + len(extra):2 * n + len(extra)]
        send_sems, recv_sems, local_sems = refs[2 * n + len(extra):]
        x, y, c = _me()
        me, sibling = (x, y, c), (x, y, 1 - c)
        chips = [(1 - x, y), (x, 1 - y), (1 - x, 1 - y)]

        def rows(a, dev):
            return outs[a].at[4 * dev[0] + 2 * dev[1] + dev[2]]

        def copy(a, k, block, to, src=None):
            return pltpu.make_async_remote_copy(
                src_ref=rows(a, block) if src is None else src, dst_ref=rows(a, block),
                send_sem=send_sems.at[a, k], recv_sem=recv_sems.at[a, k], device_id=to, device_id_type=MESH)

        mine = [pltpu.make_async_copy(ins[a], rows(a, me), local_sems.at[a]) for a in range(n)]
        for cp in mine:
            cp.start()
        first = []
        for a in range(n):
            first.append(copy(a, 0, me, sibling, src=ins[a]))
            first += [copy(a, 1 + j, me, (*chip, c), src=ins[a]) for j, chip in enumerate(chips)]
        for cp in first:
            cp.start()
        passed = []
        for j, chip in enumerate(chips):
            for a in range(n):
                copy(a, 1 + j, (*chip, c), me).wait_recv()
                fwd = copy(a, 4 + j, (*chip, c), sibling)
                fwd.start()
                passed.append(fwd)
        for a in range(n):
            copy(a, 0, sibling, me).wait_recv()
            for j, chip in enumerate(chips):
                copy(a, 4 + j, (*chip, 1 - c), me).wait_recv()
        for cp in first + passed:
            cp.wait_send()
        for cp in mine:
            cp.wait()

    outs = pl.pallas_call(
        body, name=name,
        in_specs=[pl.BlockSpec(memory_space=space)] * n + [pl.BlockSpec(memory_space=pl.ANY)] * len(extra),
        out_specs=[pl.BlockSpec(memory_space=space)] * n,
        out_shape=[jax.ShapeDtypeStruct((N_DEV,) + a.shape, a.dtype) for a in arrays],
        scratch_shapes=[pltpu.SemaphoreType.DMA((n, 7)), pltpu.SemaphoreType.DMA((n, 7)), pltpu.SemaphoreType.DMA((n,))],
        compiler_params=pltpu.CompilerParams(vmem_limit_bytes=VMEM_LIMIT),
    )(*arrays, *extra)
    return list(outs)


def _gather_prologue(c, w_ada, b_mine, w_in_t, name):
    n_dev, d = N_DEV, c.shape[1]
    ada_w = w_ada.shape[1]

    def body(c_ref, w_ref, b_ref, win_ref, call_ref, ada_ref, gin_ref, cols_ref, send_sems, recv_sems, local_sems):
        x, y, cc = _me()
        me, sibling = (x, y, cc), (x, y, 1 - cc)
        chips = [(1 - x, y), (x, 1 - y), (1 - x, 1 - y)]
        outs = (call_ref, ada_ref, gin_ref)

        def rows(a, dev):
            return outs[a].at[4 * dev[0] + 2 * dev[1] + dev[2]]

        def copy(a, k, block, to, src=None):
            return pltpu.make_async_remote_copy(
                src_ref=rows(a, block) if src is None else src, dst_ref=rows(a, block),
                send_sem=send_sems.at[a, k], recv_sem=recv_sems.at[a, k], device_id=to, device_id_type=MESH)

        def begin(a, src):
            own = pltpu.make_async_copy(src, rows(a, me), local_sems.at[a])
            sends = [copy(a, 0, me, sibling, src=src)] + [copy(a, 1 + j, me, (*chip, cc), src=src) for j, chip in enumerate(chips)]
            for cp in [own] + sends:
                cp.start()
            return own, sends

        def finish(a, own, sends):
            passed = []
            for j, chip in enumerate(chips):
                copy(a, 1 + j, (*chip, cc), me).wait_recv()
                passed.append(copy(a, 4 + j, (*chip, cc), sibling))
                passed[-1].start()
            copy(a, 0, sibling, me).wait_recv()
            for j, chip in enumerate(chips):
                copy(a, 4 + j, (*chip, 1 - cc), me).wait_recv()
            for cp in sends + passed:
                cp.wait_send()
            own.wait()

        finish(0, *begin(0, c_ref))
        cols_ref[...] = (jnp.dot(call_ref[:, 0, :].astype(bf16), w_ref[...].astype(bf16), preferred_element_type=f32)
                         + b_ref[...])
        finish(1, *begin(1, cols_ref))
        finish(2, *begin(2, win_ref))

    vmem, hbm = pl.BlockSpec(memory_space=pltpu.VMEM), pl.BlockSpec(memory_space=pl.ANY)
    return pl.pallas_call(
        body, name=name, in_specs=[vmem, vmem, vmem, hbm], out_specs=[vmem, vmem, hbm],
        out_shape=[jax.ShapeDtypeStruct((n_dev, 1, d), f32), jax.ShapeDtypeStruct((n_dev, n_dev, ada_w), f32),
                   jax.ShapeDtypeStruct((n_dev,) + w_in_t.shape, w_in_t.dtype)],
        scratch_shapes=[pltpu.VMEM((n_dev, ada_w), f32), pltpu.SemaphoreType.DMA((3, 7)), pltpu.SemaphoreType.DMA((3, 7)),
                        pltpu.SemaphoreType.DMA((3,))],
        compiler_params=pltpu.CompilerParams(vmem_limit_bytes=VMEM_LIMIT),
    )(c, w_ada, b_mine, w_in_t)


_FLIPS = ((0, 0, 1), (1, 0, 0), (0, 1, 0), (1, 1, 0), (1, 0, 1), (0, 1, 1), (1, 1, 1))
_HBM = pl.BlockSpec(memory_space=pltpu.HBM)
_SEM = pl.BlockSpec(memory_space=pltpu.SEMAPHORE)


def _exchange_copies(scatter, srcs, lands, send_sems, recv_sems):
    x, y, c = _me()
    me_row = 4 * x + 2 * y + c
    out = []
    for k, (fx, fy, fc) in enumerate(_FLIPS):
        peer = (x ^ fx, y ^ fy, c ^ fc)
        peer_row = 4 * peer[0] + 2 * peer[1] + peer[2]
        for a in range(len(srcs)):
            out.append(pltpu.make_async_remote_copy(
                src_ref=srcs[a].at[peer_row] if scatter else srcs[a], dst_ref=lands[a].at[me_row],
                send_sem=send_sems.at[7 * a + k], recv_sem=recv_sems.at[7 * a + k], device_id=peer, device_id_type=MESH))
    return out


def _exchange_start(arrays, scatter, name, after=None):
    n = len(arrays)
    lands = [lax.empty(a.shape if scatter else (N_DEV,) + a.shape, a.dtype) for a in arrays]
    extra = [] if after is None else [after]

    def body(*refs):
        srcs, zones = refs[:n], refs[n:2 * n]
        send_sems, recv_sems = refs[2 * n + len(extra)], refs[2 * n + len(extra) + 1]
        token = refs[-1]
        for cp in _exchange_copies(scatter, srcs, zones, send_sems, recv_sems):
            cp.start()
        token[...] = jnp.zeros_like(token)

    thru = [pltpu.HBM(a.shape, a.dtype) for a in list(arrays) + lands]
    outs = pl.pallas_call(
        body, name=name,
        out_shape=(pltpu.SemaphoreType.DMA((7 * n,)), pltpu.SemaphoreType.DMA((7 * n,)), *thru, jax.ShapeDtypeStruct((8, LANES), f32)),
        in_specs=[_HBM] * (2 * n) + [pl.BlockSpec(memory_space=pl.ANY)] * len(extra),
        out_specs=(_SEM, _SEM, *[_HBM] * (2 * n), pl.BlockSpec(memory_space=pltpu.VMEM)),
        input_output_aliases={i: 2 + i for i in range(2 * n)},
        compiler_params=pltpu.CompilerParams(has_side_effects=pltpu.SideEffectType.DATAFLOW_SIDE_EFFECTING),
    )(*[pltpu.with_memory_space_constraint(a, pltpu.HBM) for a in list(arrays) + lands], *extra)
    return dict(n=n, scatter=scatter, sems=outs[:2], srcs=outs[2:2 + n], lands=outs[2 + n:2 + 2 * n], token=outs[-1])


def _exchange_wait(handle, after, name):
    n, scatter = handle["n"], handle["scatter"]

    def body(*refs):
        srcs, zones = refs[:n], refs[n:2 * n]
        send_sems, recv_sems = refs[2 * n], refs[2 * n + 1]
        for cp in _exchange_copies(scatter, srcs, zones, send_sems, recv_sems):
            cp.wait_send()
            cp.wait_recv()

    thru = [pltpu.HBM(a.shape, a.dtype) for a in list(handle["srcs"]) + list(handle["lands"])]
    outs = pl.pallas_call(
        body, name=name, out_shape=tuple(thru),
        in_specs=[_HBM] * (2 * n) + [_SEM, _SEM, pl.BlockSpec(memory_space=pl.ANY)], out_specs=tuple([_HBM] * (2 * n)),
        input_output_aliases={i: i for i in range(2 * n)},
        compiler_params=pltpu.CompilerParams(has_side_effects=pltpu.SideEffectType.DATAFLOW_SIDE_EFFECTING),
    )(*handle["srcs"], *handle["lands"], *handle["sems"], after)
    return list(outs[n:])


def _cols_from_shards(g):
    return jnp.transpose(g, (1, 0, 2)).reshape(g.shape[1], -1)


def _shards_from_cols(a):
    return jnp.transpose(a.reshape(a.shape[0], N_DEV, -1), (1, 0, 2))


def _local_step(x, positions, ada, g_pre_mix, g_post_mix, b_f, sinks, g_pre_ffn, g_post_ffn, target,
                w_in_t, mix_weights, ffn_weights, on_grads):
    s, d = x.shape
    row = lambda v: v.reshape(1, -1)
    shift_m, scale_m, gate_m, shift_f, scale_f, gate_f = (ada[i:i + 1] for i in range(6))
    w_gate_t, w_qkv_t = w_in_t[F_OFF + N_HEADS:], w_in_t[:QKV_W]
    w_f_t = jnp.pad(w_in_t[F_OFF:F_OFF + N_HEADS], ((0, LANES - N_HEADS), (0, 0)))
    bf_row = jnp.pad(row(b_f), ((0, 0), (0, LANES - N_HEADS)))
    sink_rows = jnp.broadcast_to(sinks.reshape(N_HEADS, 1).astype(f32), (N_HEADS, LANES))
    inv_freq = 1.0 / (ROPE_THETA ** (jnp.arange(0, HEAD_DIM, 2, dtype=f32) / HEAD_DIM))
    cos, sin_s = _rope_tables(positions.reshape(s, 1), jnp.tile(inv_freq, 4).reshape(1, LANES), "rope_tables")

    h1, qa, ka, va, qb, kb, vb = _prenorm_proj_qkv(x, row(g_pre_mix), scale_m, shift_m, w_qkv_t, cos, sin_s, "prenorm_proj_qkv")
    gl = _matmul(h1, w_gate_t, "nt", bf16, "proj_gate")
    fl, cum_b = _forget_prep(h1, w_f_t, bf_row, "proj_forget_prep")
    o_a, lse_a = _attn_fwd(qa, ka, va, "swa_fwd", sink_rows=sink_rows, window=WINDOW, t=512)
    o_b, lse_b = _attn_fwd(qb, kb, vb, "fox_fwd", cum_b=cum_b, t=1024)
    everything_before = (gl[:8, :LANES] + o_a[:8, :LANES] + o_b[:8, :LANES]).astype(f32)
    w_branch_a, w_branch_b, w_out = mix_weights(everything_before)
    ba, bb, merged = _branch_merge(o_a, o_b, w_branch_a, w_branch_b, gl, "branch_merge")
    y1, x2, h2 = _out_proj_postnorm_prenorm(merged, w_out, x, row(g_post_mix), gate_m, row(g_pre_ffn), scale_f, shift_f,
                                            "out_proj_norms")

    w_ffn_in_t, w_ffn_out = ffn_weights(h2)
    g_ff, u_ff, act = _ffn_in_swiglu(h2, w_ffn_in_t, "ffn_in_swiglu")
    loss_row, d_out, d_y2, vec_pf = _out_proj_loss_tail(act, w_ffn_out, x2, row(g_post_ffn), gate_f, target, "ffn_out_loss_tail")

    g_w_ffn_out = _matmul(act, d_y2, "tn", bf16, "ffn_out_wgrad")
    dg_ff, du_ff = _ffn_out_dgrad_swiglu(d_y2, w_ffn_out, g_ff, u_ff, "ffn_out_dgrad_swiglu")
    g_w_ffn_in_t = _wgrad_stack([dg_ff, du_ff], h2, "ffn_in_wgrad")
    sent = on_grads(dict(w_ffn_in=g_w_ffn_in_t, w_ffn_out=g_w_ffn_out))
    d_x2, vec_nf, d_y1, vec_pm = _dgrad_prenorm_bwd(
        [(dg_ff, w_ffn_in_t, 0), (du_ff, w_ffn_in_t, 1)], x2, row(g_pre_ffn), scale_f, d_out, "ffn_in_dgrad_norms_bwd",
        after=sent, below=(y1, row(g_post_mix), gate_m))

    g_w_out = _matmul(merged, d_y1, "tn", bf16, "out_proj_wgrad")
    d_ba, d_bb, dgl = _out_dgrad_merge_bwd(d_y1, w_out, ba, bb, gl, "out_proj_dgrad_merge_bwd")
    g_w_branch_a = _matmul(o_a, d_ba, "tn", bf16, "branch_a_wgrad")
    g_w_branch_b = _matmul(o_b, d_bb, "tn", bf16, "branch_b_wgrad")
    sent = on_grads(dict(w_out=g_w_out, w_branch_a=g_w_branch_a, w_branch_b=g_w_branch_b))
    d_oa, delta_a, d_sink = _branch_dgrad_delta(d_ba, w_branch_a, o_a, "branch_a_dgrad_delta", lse=lse_a,
                                                sink_rows=sink_rows, after=sent)
    d_ob, delta_b = _branch_dgrad_delta(d_bb, w_branch_b, o_b, "branch_b_dgrad_delta", after=sent)
    dqa_t, dka, dva = _attn_bwd(qa, ka, va, d_oa, lse_a, delta_a, "swa_bwd", window=WINDOW, t=512)
    dqb_t, dkb, dvb, dcs, rs = _attn_bwd(qb, kb, vb, d_ob, lse_b, delta_b, "fox_bwd", cum_b=cum_b, t=512)
    dqkv = _qkv_prep_bwd(dqa_t, dka, dva, dqb_t, dkb, dvb, cos, sin_s, "qkv_prep_bwd")
    dfl, vec_bf = _forget_prep_bwd(rs.reshape(N_HEADS, s), dcs, fl, bf_row, "forget_prep_bwd")
    g_w_in_t = jnp.concatenate([_matmul(dqkv, h1, "tn", bf16, "qkv_wgrad"), _matmul(dfl, h1, "tn", bf16, "forget_wgrad")[:N_HEADS],
                                _matmul(dgl, h1, "tn", bf16, "gate_wgrad")], axis=0)
    sent = on_grads(dict(w_in=g_w_in_t))
    grad_x, vec_nm = _dgrad_prenorm_bwd([(dgl, w_gate_t, 0), (dqkv, w_qkv_t, 0), (dfl, w_f_t, 0)], x, row(g_pre_mix),
                                        scale_m, d_x2, "in_proj_dgrad_prenorm_bwd", after=sent)

    d_ada = jnp.concatenate([vec_nm[0], vec_nm[1], vec_pm[0], vec_nf[0], vec_nf[1], vec_pf[0]])
    small = dict(b_ada=d_ada, g_pre_mix=vec_nm[2], g_post_mix=vec_pm[1], g_pre_ffn=vec_nf[2], g_post_ffn=vec_pf[1],
                 b_f=vec_bf[0, :N_HEADS], sinks=d_sink[:, 0], loss=loss_row[0, :1])
    return grad_x, small


_SMALL = (("b_ada", 6144), ("g_pre_mix", 1024), ("g_post_mix", 1024), ("g_pre_ffn", 1024), ("g_post_ffn", 1024),
          ("b_f", 128), ("sinks", 128), ("loss", 128))
_SMALL_ROWS = 88


def _pack_small(vals):
    parts = [jnp.pad(vals[k].reshape(-1).astype(f32), (0, n - vals[k].size)) for k, n in _SMALL]
    flat = jnp.concatenate(parts)
    return jnp.pad(flat, (0, _SMALL_ROWS * LANES - flat.size)).reshape(_SMALL_ROWS, LANES)


def _unpack_small(slab, shapes):
    flat, out, off = slab.reshape(-1), {}, 0
    for k, n in _SMALL:
        size = math.prod(shapes[k])
        out[k] = flat[off:off + size].reshape(shapes[k])
        off += n
    return out


def kernel(x, c, positions, w_ada, b_ada, g_pre_mix, g_post_mix, w_in, b_f, sinks, w_branch_a, w_branch_b, w_out, g_pre_ffn, g_post_ffn, w_ffn_in, w_ffn_out, loss_target, m_w_ada, m_b_ada, m_g_pre_mix, m_g_post_mix, m_w_in, m_b_f, m_sinks, m_w_branch_a, m_w_branch_b, m_w_out, m_g_pre_ffn, m_g_post_ffn, m_w_ffn_in, m_w_ffn_out, v_w_ada, v_b_ada, v_g_pre_mix, v_g_post_mix, v_w_in, v_b_f, v_sinks, v_w_branch_a, v_w_branch_b, v_w_out, v_g_pre_ffn, v_g_post_ffn, v_w_ffn_in, v_w_ffn_out):
    xi, yi, ci = _me()
    me = 4 * xi + 2 * yi + ci
    d = D_MODEL
    ada_w = w_ada.shape[2]

    transposed = ("w_in", "w_ffn_in")
    tr = lambda a: jnp.transpose(a[0])

    b_mine = lax.dynamic_slice(b_ada, (0, me * ada_w), (1, ada_w))
    c_all, ada_all, g_in = _gather_prologue(c, w_ada[0], b_mine, tr(w_in).astype(bf16), "gather_prologue")
    c_all = c_all.reshape(N_DEV, d)
    ada = lax.dynamic_index_in_dim(ada_all, me, axis=1, keepdims=False).reshape(6, d)
    late_mix = [w.astype(bf16) for w in (w_branch_a[0], w_branch_b[0], w_out[0])]
    late_ffn = [w.astype(bf16) for w in (tr(w_ffn_in), w_ffn_out[0])]
    mix_h = _exchange_start(late_mix, False, "gather_mix_start", after=g_in)
    ffn_h = _exchange_start(late_ffn, False, "gather_ffn_start", after=mix_h["token"])

    def mine_into(zone, block):
        return lax.dynamic_update_index_in_dim(zone, block, me, 0)

    def rows_from_shards(g):
        return g.reshape(g.shape[0] * g.shape[1], g.shape[2])

    def mix_weights(after):
        zones = _exchange_wait(mix_h, after, "gather_mix_wait")
        g_ba, g_bb, g_out = (mine_into(z, w) for z, w in zip(zones, late_mix))
        return _cols_from_shards(g_ba), _cols_from_shards(g_bb), rows_from_shards(g_out)

    def ffn_weights(after):
        zones = _exchange_wait(ffn_h, after, "gather_ffn_wait")
        g_fi, g_fo = (mine_into(z, w) for z, w in zip(zones, late_ffn))
        return rows_from_shards(g_fi), rows_from_shards(g_fo)

    row_sharded = ("w_out", "w_ffn_out") + transposed
    in_flight = []

    def on_grads(group):
        sends = [g.reshape(N_DEV, g.shape[0] // N_DEV, g.shape[1]) if nm in row_sharded else _shards_from_cols(g)
                 for nm, g in group.items()]
        handle = _exchange_start(sends, True, "scatter_start_%d" % len(in_flight))
        in_flight.append((list(group), sends, handle))
        return handle["token"]

    grad_x, small = _local_step(
        x[0], positions[0], ada + ffn_h["token"][0, 0], g_pre_mix[0], g_post_mix[0], b_f[0], sinks[0], g_pre_ffn[0],
        g_post_ffn[0], loss_target[0], rows_from_shards(g_in), mix_weights, ffn_weights, on_grads)

    ws = dict(w_in=(w_in, m_w_in, v_w_in), w_branch_a=(w_branch_a, m_w_branch_a, v_w_branch_a),
              w_branch_b=(w_branch_b, m_w_branch_b, v_w_branch_b), w_out=(w_out, m_w_out, v_w_out),
              w_ffn_in=(w_ffn_in, m_w_ffn_in, v_w_ffn_in), w_ffn_out=(w_ffn_out, m_w_ffn_out, v_w_ffn_out))
    res = {}

    def finish_group(gi, after):
        names, sends, handle = in_flight[gi]
        zones = _exchange_wait(handle, after, "scatter_wait_%d" % gi)
        for nm, zone, sent in zip(names, zones, sends):
            w, m, v = (tr(a) if nm in transposed else a[0] for a in ws[nm])
            out = _adamw(zone, w, m, v, "adamw_" + nm, mine=sent, me=me.reshape(1).astype(jnp.int32))
            after = out[0]
            res[nm] = [jnp.transpose(o) for o in out] if nm in transposed else out
        return after

    done = finish_group(1, finish_group(0, grad_x))

    slab_all, = _all_gather([_pack_small(small)], "gather_small", vmem=True, after=done)
    small_w = dict(b_ada=b_ada, g_pre_mix=g_pre_mix, g_post_mix=g_post_mix, g_pre_ffn=g_pre_ffn, g_post_ffn=g_post_ffn,
                   b_f=b_f, sinks=sinks, loss=jnp.zeros((1,), f32))
    small_m = dict(b_ada=m_b_ada, g_pre_mix=m_g_pre_mix, g_post_mix=m_g_post_mix, g_pre_ffn=m_g_pre_ffn,
                   g_post_ffn=m_g_post_ffn, b_f=m_b_f, sinks=m_sinks, loss=jnp.zeros((1,), f32))
    small_v = dict(b_ada=v_b_ada, g_pre_mix=v_g_pre_mix, g_post_mix=v_g_post_mix, g_pre_ffn=v_g_pre_ffn,
                   g_post_ffn=v_g_post_ffn, b_f=v_b_f, sinks=v_sinks, loss=jnp.ones((1,), f32))
    shapes = {k: small_w[k].shape for k, _ in _SMALL}
    s_out = _adamw(slab_all, _pack_small(small_w), _pack_small(small_m), _pack_small(small_v), "adamw_small")
    s_grad, s_delta, s_m, s_v = (_unpack_small(o, shapes) for o in s_out)

    d_ada_all = lax.dynamic_slice(slab_all[:, :6144 // LANES, :].reshape(N_DEV, 6144), (0, me * ada_w), (N_DEV, ada_w))
    ada_parts = _ada_wgrad(c_all, d_ada_all, "ada_wgrad")

    res["w_ada"] = _adamw(ada_parts, w_ada[0], m_w_ada[0], v_w_ada[0], "adamw_w_ada")
    finish_group(2, res["w_ada"][0])

    order = ["w_ada", "b_ada", "g_pre_mix", "g_post_mix", "w_in", "b_f", "sinks", "w_branch_a", "w_branch_b", "w_out",
             "g_pre_ffn", "g_post_ffn", "w_ffn_in", "w_ffn_out"]
    outs = [s_grad["loss"].reshape(()), grad_x[None]]
    for which, small_o in enumerate((s_grad, s_delta, s_m, s_v)):
        for nm in order:
            outs.append(res[nm][which][None] if nm in res else small_o[nm])
    return tuple(outs)
```

```python
import math

import jax
import jax.numpy as jnp
from jax import lax
from jax.experimental import pallas as pl
from jax.experimental.pallas import tpu as pltpu

f32 = jnp.float32
bf16 = jnp.bfloat16

D_MODEL = 1024
HEAD_DIM = 64
N_HEADS = 8
N_PAIRS = 4
QKV_W = 2304
F_OFF = 2304
WINDOW = 128
ROPE_THETA = 10000.0
RMS_EPS = 1e-6
N_DEV = 8
ADAM_LR, ADAM_B1, ADAM_B2, ADAM_EPS, ADAM_WD, ADAM_STEP = 0.001, 0.9, 0.999, 1e-08, 0.01, 10
NEG = -1e30
L_ROW = (HEAD_DIM, 0)
LANES = 128
VMEM_LIMIT = 48 * 1024 * 1024
MESH = pl.DeviceIdType.MESH

_NT = (((1,), (1,)), ((), ()))
_TN = (((0,), (0,)), ((), ()))


def _params(n_grid=0):
    sem = ("arbitrary",) * n_grid if n_grid else None
    return pltpu.CompilerParams(dimension_semantics=sem, vmem_limit_bytes=VMEM_LIMIT)


def _row_tile(s, want):
    t = min(s, want)
    assert s % t == 0, (s, t)
    return t


MATMUL_VMEM_BUDGET = 40 * 1024 * 1024


def _matmul_tiles(m, n, k, a_item, b_item, o_item):
    def tiles(d):
        return [t for t in range(LANES, min(d, 2048) + 1, LANES) if d % t == 0] or [d]

    best = None
    for tm in tiles(m):
        for tn in tiles(n):
            vmem = 2 * (tm * k * a_item + tn * k * b_item + tm * tn * o_item) + tm * tn * 4
            if vmem > MATMUL_VMEM_BUDGET:
                continue
            traffic = m * k * a_item + n * k * b_item * (1 if tn == n else m // tm) + m * n * o_item
            steps = (m // tm) * (n // tn)
            key = (traffic, 0, steps) if steps >= 4 else (traffic, 1, -steps)
            if best is None or key < best[0]:
                best = (key, tm, tn)
    assert best is not None, (m, n, k)
    return best[1], best[2]


def _matmul(a, b, mode, out_dtype, name, after=None):
    if mode == "nn":
        (m, k), n = a.shape, b.shape[1]
    elif mode == "nt":
        (m, k), n = a.shape, b.shape[0]
    else:
        (k, m), n = a.shape, b.shape[1]
    tm, tn = _matmul_tiles(m, n, k, a.dtype.itemsize, b.dtype.itemsize, jnp.dtype(out_dtype).itemsize)
    if mode == "nn":
        a_spec, b_spec, dims = pl.BlockSpec((tm, k), lambda i, j: (i, 0)), pl.BlockSpec((k, tn), lambda i, j: (0, j)), None
    elif mode == "nt":
        a_spec, b_spec, dims = pl.BlockSpec((tm, k), lambda i, j: (i, 0)), pl.BlockSpec((tn, k), lambda i, j: (j, 0)), _NT
    else:
        a_spec, b_spec, dims = pl.BlockSpec((k, tm), lambda i, j: (0, i)), pl.BlockSpec((k, tn), lambda i, j: (0, j)), _TN

    def body(a_ref, b_ref, *rest):
        o_ref = rest[-1]
        av, bv = a_ref[...].astype(bf16), b_ref[...].astype(bf16)
        if dims is None:
            r = jnp.dot(av, bv, preferred_element_type=f32)
        else:
            r = lax.dot_general(av, bv, dims, preferred_element_type=f32)
        o_ref[...] = r.astype(out_dtype)

    extra = [] if after is None else [after]
    return pl.pallas_call(
        body, name=name, grid=(m // tm, n // tn), in_specs=[a_spec, b_spec] + [pl.BlockSpec(memory_space=pl.ANY)] * len(extra),
        out_specs=pl.BlockSpec((tm, tn), lambda i, j: (i, j)),
        out_shape=jax.ShapeDtypeStruct((m, n), out_dtype), compiler_params=_params(2),
    )(a, b, *extra)


def _rstd(v):
    return lax.rsqrt(jnp.mean(v * v, axis=-1, keepdims=True) + RMS_EPS)


def _row_spec(tm, d):
    return pl.BlockSpec((tm, d), lambda i: (i, 0))


def _vec_spec(d, rows=1):
    return pl.BlockSpec((rows, d), lambda i: (0, 0))


def _proj_spec(a, w, tm):
    return [_row_spec(tm, a.shape[1]), pl.BlockSpec(w.shape, lambda i: (0, 0))]


def _out_proj_postnorm_prenorm(a, w, x, g_post, gate, g_pre, scale, shift, name):
    s, d = x.shape
    tm = _row_tile(s, 512)

    def body(a_ref, w_ref, x_ref, gp_ref, gate_ref, g_ref, sc_ref, sh_ref, y_ref, x2_ref, h_ref):
        yv = jnp.dot(a_ref[...], w_ref[...], preferred_element_type=f32)
        y_ref[...] = yv
        x2 = x_ref[...] + gate_ref[...] * (yv * _rstd(yv) * gp_ref[...])
        x2_ref[...] = x2
        h_ref[...] = ((x2 * _rstd(x2) * g_ref[...]) * (1.0 + sc_ref[...]) + sh_ref[...]).astype(bf16)

    return pl.pallas_call(
        body, name=name, grid=(s // tm,), in_specs=_proj_spec(a, w, tm) + [_row_spec(tm, d)] + [_vec_spec(d)] * 5,
        out_specs=[_row_spec(tm, d)] * 3,
        out_shape=[jax.ShapeDtypeStruct((s, d), f32)] * 2 + [jax.ShapeDtypeStruct((s, d), bf16)], compiler_params=_params(1),
    )(a, w, x, g_post, gate, g_pre, scale, shift)


def _rms_bwd(u, v, r):
    return r * u - v * (r * r * r) * jnp.mean(u * v, axis=-1, keepdims=True)


def _out_proj_loss_tail(a, w, x, g, gate, target, name):
    s, d = x.shape
    tm = _row_tile(s, 512)

    def body(a_ref, w_ref, x_ref, g_ref, gate_ref, t_ref, loss_ref, do_ref, dy_ref, vec_ref):
        @pl.when(pl.program_id(0) == 0)
        def _():
            loss_ref[...] = jnp.zeros_like(loss_ref)
            vec_ref[...] = jnp.zeros_like(vec_ref)
        yv = jnp.dot(a_ref[...], w_ref[...], preferred_element_type=f32)
        r = _rstd(yv)
        yn = yv * r
        err = x_ref[...] + gate_ref[...] * (yn * g_ref[...]) - t_ref[...]
        loss_ref[...] += 0.5 * jnp.sum(jnp.mean(err * err, axis=-1, keepdims=True), axis=0, keepdims=True)
        dr = err / d
        do_ref[...] = dr
        dn = dr * gate_ref[...]
        vec_ref[0:1, :] += jnp.sum(dr * (yn * g_ref[...]), axis=0, keepdims=True)
        vec_ref[1:2, :] += jnp.sum(dn * yn, axis=0, keepdims=True)
        dy_ref[...] = _rms_bwd(dn * g_ref[...], yv, r).astype(bf16)

    return pl.pallas_call(
        body, name=name, grid=(s // tm,),
        in_specs=_proj_spec(a, w, tm) + [_row_spec(tm, d)] + [_vec_spec(d)] * 2 + [_row_spec(tm, d)],
        out_specs=[_vec_spec(LANES), _row_spec(tm, d), _row_spec(tm, d), _vec_spec(d, 8)],
        out_shape=[jax.ShapeDtypeStruct((1, LANES), f32), jax.ShapeDtypeStruct((s, d), f32),
                   jax.ShapeDtypeStruct((s, d), bf16), jax.ShapeDtypeStruct((8, d), f32)],
        compiler_params=_params(1),
    )(a, w, x, g, gate, target)


def _dgrad_prenorm_bwd(terms, x, g, scale, dres, name, after=None, below=None):
    s, d = x.shape
    n = len(terms)
    k = sum(a.shape[1] for a, _, _ in terms)
    row_bytes = 2 * (2 * k) + d * (4 + 2 * 4 * 3 + (2 * 4 + 2 * 2 if below else 0))
    tm = next(t for t in (512, 256, 128) if s % t == 0 and 4 * k * d + t * row_bytes <= MATMUL_VMEM_BUDGET)
    extra = [] if after is None else [after]

    def body(*refs):
        a_refs, b_refs = refs[:n], refs[n:2 * n]
        x_ref, g_ref, sc_ref, dr_ref = refs[2 * n:2 * n + 4]
        n_in = 2 * n + 4 + (3 if below else 0) + len(extra)
        dx_ref, vec_ref = refs[n_in], refs[n_in + 1]
        if below:
            y_ref, gp_ref, gate_ref = refs[2 * n + 4:2 * n + 7]
            dy_ref, vec2_ref = refs[n_in + 2], refs[n_in + 3]

        @pl.when(pl.program_id(0) == 0)
        def _():
            vec_ref[...] = jnp.zeros_like(vec_ref)
            if below:
                vec2_ref[...] = jnp.zeros_like(vec2_ref)
        dhv = jnp.dot(a_refs[0][...], b_refs[0][...], preferred_element_type=f32)
        for i in range(1, n):
            dhv = dhv + jnp.dot(a_refs[i][...], b_refs[i][...], preferred_element_type=f32)
        xv = x_ref[...]
        r = _rstd(xv)
        xn = xv * r
        dn = dhv * (1.0 + sc_ref[...])
        vec_ref[0:1, :] += jnp.sum(dhv, axis=0, keepdims=True)
        vec_ref[1:2, :] += jnp.sum(dhv * (xn * g_ref[...]), axis=0, keepdims=True)
        vec_ref[2:3, :] += jnp.sum(dn * xn, axis=0, keepdims=True)
        dx = dr_ref[...] + _rms_bwd(dn * g_ref[...], xv, r)
        dx_ref[...] = dx
        if below:
            yv = y_ref[...]
            ry = _rstd(yv)
            yn = yv * ry
            dny = dx * gate_ref[...]
            vec2_ref[0:1, :] += jnp.sum(dx * (yn * gp_ref[...]), axis=0, keepdims=True)
            vec2_ref[1:2, :] += jnp.sum(dny * yn, axis=0, keepdims=True)
            dy_ref[...] = _rms_bwd(dny * gp_ref[...], yv, ry).astype(bf16)

    in_specs = ([_row_spec(tm, a.shape[1]) for a, _, _ in terms]
                + [pl.BlockSpec((a.shape[1], d), lambda i, r=r: (r, 0)) for a, _, r in terms]
                + [_row_spec(tm, d)] + [_vec_spec(d)] * 2 + [_row_spec(tm, d)])
    out_specs = [_row_spec(tm, d), _vec_spec(d, 8)]
    out_shape = [jax.ShapeDtypeStruct((s, d), f32), jax.ShapeDtypeStruct((8, d), f32)]
    args = [a for a, _, _ in terms] + [b for _, b, _ in terms] + [x, g, scale, dres]
    if below:
        in_specs += [_row_spec(tm, d)] + [_vec_spec(d)] * 2
        out_specs += [_row_spec(tm, d), _vec_spec(d, 8)]
        out_shape += [jax.ShapeDtypeStruct((s, d), bf16), jax.ShapeDtypeStruct((8, d), f32)]
        args += list(below)
    return pl.pallas_call(
        body, name=name, grid=(s // tm,), in_specs=in_specs + [pl.BlockSpec(memory_space=pl.ANY)] * len(extra),
        out_specs=out_specs, out_shape=out_shape, compiler_params=_params(1),
    )(*args, *extra)


def _lane():
    return lax.broadcasted_iota(jnp.int32, (1, LANES), 1)


def _rope_tables(pos_col, inv_freq, name):
    s = pos_col.shape[0]

    def body(p_ref, f_ref, cos_ref, sin_ref):
        ang = p_ref[...].astype(f32) * f_ref[...]
        first_half = (_lane() % HEAD_DIM) < HEAD_DIM // 2
        cos_ref[...] = jnp.cos(ang)
        sn = jnp.sin(ang)
        sin_ref[...] = jnp.where(first_half, -sn, sn)

    return pl.pallas_call(
        body, name=name, out_shape=[jax.ShapeDtypeStruct((s, LANES), f32)] * 2, compiler_params=_params(),
    )(pos_col, inv_freq)


def _swap_halves(v):
    first_half = (_lane() % HEAD_DIM) < HEAD_DIM // 2
    return jnp.where(first_half, pltpu.roll(v, LANES - HEAD_DIM // 2, axis=1), pltpu.roll(v, HEAD_DIM // 2, axis=1))


def _prenorm_proj_qkv(x, g, mod_scale, mod_shift, w_qkv_t, cos, sin_s, name):
    s, d = x.shape
    tm = _row_tile(s, 512)
    scale = 1.0 / math.sqrt(HEAD_DIM)

    def body(x_ref, g_ref, msc_ref, msh_ref, w_ref, c_ref, s_ref, h_ref, qa_ref, ka_ref, va_ref, qb_ref, kb_ref, vb_ref):
        xv = x_ref[...]
        h = ((xv * _rstd(xv) * g_ref[...]) * (1.0 + msc_ref[...]) + msh_ref[...]).astype(bf16)
        h_ref[...] = h
        proj = lax.dot_general(h, w_ref[...], _NT, preferred_element_type=f32)
        cs, sn = c_ref[...], s_ref[...]
        low = _lane() < HEAD_DIM

        def blk(j):
            return proj[:, j * LANES:(j + 1) * LANES]

        def rope(v):
            return v * cs + _swap_halves(v) * sn

        def expand(v):
            other = pltpu.roll(v, HEAD_DIM, axis=1)
            return jnp.where(low, v, other), jnp.where(low, other, v)

        for j in range(N_PAIRS):
            qa_ref[:, j * LANES:(j + 1) * LANES] = (rope(blk(j)) * scale).astype(bf16)
            qb_ref[:, j * LANES:(j + 1) * LANES] = (blk(6 + j) * scale).astype(bf16)
            kb_ref[:, j * LANES:(j + 1) * LANES] = blk(10 + j).astype(bf16)
            vb_ref[:, j * LANES:(j + 1) * LANES] = blk(14 + j).astype(bf16)
        k0, k1 = expand(rope(blk(4)))
        v0, v1 = expand(blk(5))
        for j in range(N_PAIRS):
            ka_ref[:, j * LANES:(j + 1) * LANES] = (k0 if j < 2 else k1).astype(bf16)
            va_ref[:, j * LANES:(j + 1) * LANES] = (v0 if j < 2 else v1).astype(bf16)

    hw = N_PAIRS * LANES
    return pl.pallas_call(
        body, name=name, grid=(s // tm,),
        in_specs=[_row_spec(tm, d)] + [_vec_spec(d)] * 3
        + [pl.BlockSpec((QKV_W, d), lambda i: (0, 0)), _row_spec(tm, LANES), _row_spec(tm, LANES)],
        out_specs=[_row_spec(tm, d)] + [_row_spec(tm, hw)] * 6,
        out_shape=[jax.ShapeDtypeStruct((s, d), bf16)] + [jax.ShapeDtypeStruct((s, hw), bf16)] * 6, compiler_params=_params(1),
    )(x, g, mod_scale, mod_shift, w_qkv_t, cos, sin_s)


def _qkv_prep_bwd(dqa_t, dka, dva, dqb_t, dkb, dvb, cos, sin_s, name):
    s = dka.shape[0]
    tm = _row_tile(s, 256)
    scale = 1.0 / math.sqrt(HEAD_DIM)
    hw = N_PAIRS * LANES
    t_spec = pl.BlockSpec((hw, tm), lambda i: (0, i))

    def body(dqa_ref, dka_ref, dva_ref, dqb_ref, dkb_ref, dvb_ref, c_ref, s_ref, o_ref):
        cs, sn = c_ref[...], s_ref[...]
        low = _lane() < HEAD_DIM

        def blk(ref, j):
            return ref[:, j * LANES:(j + 1) * LANES].astype(f32)

        def blk_t(ref, j):
            return ref[j * LANES:(j + 1) * LANES, :].T

        def unrope(v):
            return v * cs + _swap_halves(v * sn)

        def fold(ref):
            a, b = blk(ref, 0) + blk(ref, 1), blk(ref, 2) + blk(ref, 3)
            kv0 = a + pltpu.roll(a, HEAD_DIM, axis=1)
            kv1 = b + pltpu.roll(b, HEAD_DIM, axis=1)
            return jnp.where(low, kv0, kv1)

        for j in range(N_PAIRS):
            o_ref[:, j * LANES:(j + 1) * LANES] = (unrope(blk_t(dqa_ref, j)) * scale).astype(bf16)
            o_ref[:, (6 + j) * LANES:(7 + j) * LANES] = (blk_t(dqb_ref, j) * scale).astype(bf16)
            o_ref[:, (10 + j) * LANES:(11 + j) * LANES] = blk(dkb_ref, j).astype(bf16)
            o_ref[:, (14 + j) * LANES:(15 + j) * LANES] = blk(dvb_ref, j).astype(bf16)
        o_ref[:, 4 * LANES:5 * LANES] = unrope(fold(dka_ref)).astype(bf16)
        o_ref[:, 5 * LANES:6 * LANES] = fold(dva_ref).astype(bf16)

    return pl.pallas_call(
        body, name=name, grid=(s // tm,),
        in_specs=[t_spec, _row_spec(tm, hw), _row_spec(tm, hw), t_spec, _row_spec(tm, hw), _row_spec(tm, hw)] + [_row_spec(tm, LANES)] * 2,
        out_specs=_row_spec(tm, QKV_W), out_shape=jax.ShapeDtypeStruct((s, QKV_W), bf16), compiler_params=_params(1),
    )(dqa_t, dka, dva, dqb_t, dkb, dvb, cos, sin_s)


def _cumsum_rows(v, reverse=False):
    n = v.shape[0]
    row = lax.broadcasted_iota(jnp.int32, v.shape, 0)
    sh = 1
    while sh < n:
        if reverse:
            v = v + jnp.where(row < n - sh, pltpu.roll(v, n - sh, axis=0), 0.0)
        else:
            v = v + jnp.where(row >= sh, pltpu.roll(v, sh, axis=0), 0.0)
        sh *= 2
    return v


def _log_sigmoid(z):
    return jnp.minimum(z, 0.0) - jnp.log1p(jnp.exp(-jnp.abs(z)))


def _forget_prep(h, w_f_t, bf_row, name):
    s = h.shape[0]

    def body(h_ref, w_ref, b_ref, f_ref, cb_ref):
        fl = lax.dot_general(h_ref[...], w_ref[...], _NT, preferred_element_type=f32)
        f_ref[...] = fl
        cum = _cumsum_rows(_log_sigmoid(fl + b_ref[...]))
        for hd in range(N_HEADS):
            cb_ref[:, hd * LANES:(hd + 1) * LANES] = jnp.broadcast_to(cum[:, hd:hd + 1], (s, LANES))

    return pl.pallas_call(
        body, name=name,
        out_shape=[jax.ShapeDtypeStruct((s, LANES), f32), jax.ShapeDtypeStruct((s, N_HEADS * LANES), f32)],
        compiler_params=_params(),
    )(h, w_f_t, bf_row)


def _forget_prep_bwd(rs, dcs, fl, bf_row, name):
    s = fl.shape[0]

    def body(r_ref, c_ref, f_ref, b_ref, df_ref, db_ref):
        eye = (lax.broadcasted_iota(jnp.int32, (N_HEADS, LANES), 0) == lax.broadcasted_iota(jnp.int32, (N_HEADS, LANES), 1)).astype(f32)
        dcum = lax.dot_general(r_ref[...], eye, _TN, precision=lax.Precision.HIGHEST, preferred_element_type=f32)
        for h in range(N_HEADS):
            dcum = dcum - jnp.where(_lane() == h, jnp.sum(c_ref[:, h * LANES:(h + 1) * LANES], axis=1, keepdims=True), 0.0)
        dlf = _cumsum_rows(dcum, reverse=True)
        z = f_ref[...] + b_ref[...]
        df = jnp.where(_lane() < N_HEADS, dlf * jax.nn.sigmoid(-z), 0.0)
        df_ref[...] = df.astype(bf16)
        db_ref[...] = jnp.zeros_like(db_ref)
        db_ref[0:1, :] = jnp.sum(df, axis=0, keepdims=True)

    return pl.pallas_call(
        body, name=name,
        out_shape=[jax.ShapeDtypeStruct((s, LANES), bf16), jax.ShapeDtypeStruct((8, LANES), f32)], compiler_params=_params(),
    )(rs, dcs, fl, bf_row)


def _tile_mask(n_keys, n_queries, off, window):
    shape = (n_keys, n_queries)
    d = lax.broadcasted_iota(jnp.int32, shape, 1) - lax.broadcasted_iota(jnp.int32, shape, 0) + off
    valid = d >= 0
    return jnp.logical_and(valid, d < window) if window else valid


def _wide(v, t):
    return jnp.concatenate([v] * (t // LANES), axis=1)


def _attn_fwd(q, k, v, name, *, cum_b=None, sink_rows=None, window=None, t=256):
    s = q.shape[0]
    t = _row_tile(s, t)
    fox, has_sink = cum_b is not None, sink_rows is not None
    assert not window or (window % LANES == 0 and LANES + window <= s)

    def body(*refs):
        q_ref, k_ref, v_ref = refs[:3]
        rest = list(refs[3:])
        cb_ref = rest.pop(0) if fox else None
        sink_ref = rest.pop(0) if has_sink else None
        o_ref, lse_ref = rest
        i = pl.program_id(1)
        low = _lane() < HEAD_DIM
        top = lax.broadcasted_iota(jnp.int32, (LANES, 1), 0) < HEAD_DIM
        q2 = q_ref[...]
        zero = jnp.zeros_like(q2)
        qms = (jnp.where(low, q2, zero), jnp.where(low, zero, q2))

        def tile(k0, n_keys, off, carry, masked, queries=slice(0, t)):
            nq = queries.stop - queries.start
            kblk, vblk = k_ref[pl.ds(k0, n_keys), :], v_ref[pl.ds(k0, n_keys), :]
            valid = _tile_mask(n_keys, nq, off, window) if masked else None
            ones = jnp.ones_like(vblk)
            vs = tuple(jnp.where(_lane() == L_ROW[h], ones, vblk) for h in range(2))

            def scores(h):
                return lax.dot_general(kblk, qms[h][queries], _NT, preferred_element_type=f32)

            def softmax(h, sc):
                m = carry[h][0]
                if fox:
                    sc = sc - _wide(cb_ref[pl.ds(k0, n_keys), h * LANES:(h + 1) * LANES], nq)
                if masked:
                    sc = jnp.where(valid, sc, NEG)
                m_new = jnp.maximum(m, jnp.max(sc, axis=0, keepdims=True))
                return m_new, jnp.exp(m - m_new), jnp.exp(sc - m_new).astype(bf16)

            def update(h, m_new, alpha, p):
                return m_new, alpha * carry[h][1] + lax.dot_general(vs[h], p, _TN, preferred_element_type=f32)

            if window:
                return tuple(update(h, *softmax(h, scores(h))) for h in range(2))
            scs = [scores(h) for h in range(2)]
            stats = [softmax(h, scs[h]) for h in range(2)]
            return tuple(update(h, *stats[h]) for h in range(2))

        def start(nq):
            if has_sink:
                row = lax.broadcasted_iota(jnp.int32, (LANES, nq), 0)
                return tuple((_wide(sink_ref[h:h + 1, :], nq), (row == L_ROW[h]).astype(f32)) for h in range(2))
            return tuple((jnp.full((1, nq), NEG, f32), jnp.zeros((LANES, nq), f32)) for h in range(2))

        def finish(carry, queries):
            (m0, a0), (m1, a1) = carry
            l0, l1 = a0[L_ROW[0]:L_ROW[0] + 1, :], a1[L_ROW[1]:L_ROW[1] + 1, :]
            o_t = jnp.where(top, a0 * (1.0 / l0), a1 * (1.0 / l1))
            o_ref[queries, :] = o_t.T.astype(bf16)
            lse_ref[0:1, queries] = m0 + jnp.log(l0)
            lse_ref[1:2, queries] = m1 + jnp.log(l1)

        if window:
            for c in range(t // LANES):
                queries = slice(c * LANES, (c + 1) * LANES)
                q0 = i * t + c * LANES
                k0 = pl.multiple_of(jnp.maximum(q0 - window, 0), LANES)
                finish(tile(k0, LANES + window, q0 - k0, start(LANES), True, queries), queries)
        else:
            carry = lax.fori_loop(0, i, lambda kb, c: tile(pl.multiple_of(kb * t, t), t, 0, c, False), start(t))
            finish(tile(pl.multiple_of(i * t, t), t, 0, carry, True), slice(0, t))

    q_spec = pl.BlockSpec((t, LANES), lambda j, i: (i, j))
    kv_spec = pl.BlockSpec((s, LANES), lambda j, i: (0, j))
    in_specs, args = [q_spec, kv_spec, kv_spec], [q, k, v]
    if fox:
        in_specs += [pl.BlockSpec((s, 2 * LANES), lambda j, i: (0, j))]
        args += [cum_b]
    if has_sink:
        in_specs += [pl.BlockSpec((None, 2, LANES), lambda j, i: (j, 0, 0))]
        args += [sink_rows.reshape(N_PAIRS, 2, LANES)]
    return pl.pallas_call(
        body, name=name, grid=(N_PAIRS, s // t), in_specs=in_specs,
        out_specs=[q_spec, pl.BlockSpec((None, 2, t), lambda j, i: (j, 0, i))],
        out_shape=[jax.ShapeDtypeStruct((s, N_PAIRS * LANES), bf16), jax.ShapeDtypeStruct((N_PAIRS, 2, s), f32)],
        compiler_params=_params(2),
    )(*args)


def _branch_dgrad_delta(db, w, o, name, *, lse=None, sink_rows=None, after=None):
    s, hw = o.shape
    tm = _row_tile(s, 512)
    has_sink = sink_rows is not None
    extra = [] if after is None else [after]

    def body(*refs):
        db_ref, w_ref, o_ref = refs[:3]
        outs = refs[3 + (2 if has_sink else 0) + len(extra):]
        do_ref, dl_ref = outs[:2]
        if has_sink:
            lse_ref, sink_ref = refs[3:5]
            ds_ref = outs[2]

            @pl.when(pl.program_id(0) == 0)
            def _():
                ds_ref[...] = jnp.zeros_like(ds_ref)
        do = lax.dot_general(db_ref[...], w_ref[...], _NT, preferred_element_type=f32).astype(bf16)
        do_ref[...] = do
        for j in range(N_PAIRS):
            cols = slice(j * LANES, (j + 1) * LANES)
            prod_t = (do[:, cols].astype(f32) * o_ref[:, cols].astype(f32)).T
            for h in range(2):
                dl = jnp.sum(prod_t[h * HEAD_DIM:(h + 1) * HEAD_DIM, :], axis=0, keepdims=True)
                dl_ref[j, h:h + 1, :] = dl
                if has_sink:
                    r = 2 * j + h
                    p_sink = jnp.exp(sink_ref[r:r + 1, 0:1] - lse_ref[j, h:h + 1, :])
                    ds_ref[r:r + 1, :] += -jnp.sum(p_sink * dl, axis=1, keepdims=True)

    rows_spec = pl.BlockSpec((N_PAIRS, 2, tm), lambda i: (0, 0, i))
    in_specs = [_row_spec(tm, db.shape[1]), pl.BlockSpec(w.shape, lambda i: (0, 0)), _row_spec(tm, hw)]
    args = [db, w, o]
    out_specs = [_row_spec(tm, hw), rows_spec]
    out_shape = [jax.ShapeDtypeStruct((s, hw), bf16), jax.ShapeDtypeStruct((N_PAIRS, 2, s), f32)]
    if has_sink:
        in_specs += [rows_spec, _vec_spec(LANES, N_HEADS)]
        args += [lse, sink_rows]
        out_specs += [_vec_spec(LANES, N_HEADS)]
        out_shape += [jax.ShapeDtypeStruct((N_HEADS, LANES), f32)]
    return pl.pallas_call(
        body, name=name, grid=(s // tm,), in_specs=in_specs + [pl.BlockSpec(memory_space=pl.ANY)] * len(extra),
        out_specs=out_specs, out_shape=out_shape, compiler_params=_params(1),
    )(*args, *extra)


def _attn_bwd(q, k, v, do, lse, delta, name, *, cum_b=None, window=None, t=256):
    s = q.shape[0]
    t = _row_tile(s, t)
    nblk = s // t
    fox = cum_b is not None
    assert not window or (window % LANES == 0 and LANES + window <= s)

    def body(*refs):
        k_ref, v_ref, q_ref, do_ref, lse_ref, dl_ref = refs[:6]
        rest = list(refs[6:])
        cb_ref = rest.pop(0) if fox else None
        dq_ref, dk_ref, dv_ref = rest[:3]
        dcs_ref, rs_ref = (rest[3], rest[4]) if fox else (None, None)
        dk_acc, dv_acc = rest[-2:]
        b = pl.program_id(1)
        k0 = pl.multiple_of(b * t, t)

        @pl.when(b == 0)
        def _():
            dq_ref[...] = jnp.zeros_like(dq_ref)
            if fox:
                rs_ref[...] = jnp.zeros_like(rs_ref)

        dk_acc[...] = jnp.zeros_like(dk_acc)
        dv_acc[...] = jnp.zeros_like(dv_acc)
        if fox:
            dcs_ref[...] = jnp.zeros_like(dcs_ref)
        low = _lane() < HEAD_DIM
        top = lax.broadcasted_iota(jnp.int32, (LANES, 1), 0) < HEAD_DIM
        kblk, vblk = k_ref[...], v_ref[...]
        k_t = kblk.astype(f32).T.astype(bf16)
        cks = [_wide(cb_ref[pl.ds(k0, t), h * LANES:(h + 1) * LANES], t) for h in range(2)] if fox else None

        def tile(q0, n_queries, off, masked, keys=slice(0, t)):
            cols = pl.ds(q0, n_queries)
            q2, do2 = q_ref[cols, :], do_ref[cols, :]
            zero = jnp.zeros_like(q2)
            valid = _tile_mask(keys.stop - keys.start, n_queries, off, window) if masked else None
            dq_parts = []
            for h in range(2):
                qm = jnp.where(low, q2, zero) if h == 0 else jnp.where(low, zero, q2)
                dom = jnp.where(low, do2, zero) if h == 0 else jnp.where(low, zero, do2)
                sc = lax.dot_general(kblk[keys], qm, _NT, preferred_element_type=f32)
                if fox:
                    sc = sc - cks[h]
                if masked:
                    sc = jnp.where(valid, sc, NEG)
                p = jnp.exp(sc - lse_ref[h:h + 1, cols])
                dp = lax.dot_general(vblk[keys], dom, _NT, preferred_element_type=f32)
                ds = p * (dp - dl_ref[h:h + 1, cols])
                pb, dsb = p.astype(bf16), ds.astype(bf16)
                dv_acc[keys, :] += jnp.dot(pb, dom, preferred_element_type=f32)
                dk_acc[keys, :] += jnp.dot(dsb, qm, preferred_element_type=f32)
                dq_parts.append(jnp.dot(k_t[:, keys], dsb, preferred_element_type=f32))
                if fox:
                    dcs_ref[:, h * LANES:(h + 1) * LANES] += sum(ds[:, g * LANES:(g + 1) * LANES] for g in range(t // LANES))
                    rs_ref[h:h + 1, cols] += jnp.sum(ds, axis=0, keepdims=True)
            dq_ref[:, cols] += jnp.where(top, dq_parts[0], dq_parts[1])

        def later_block(qb, carry):
            tile(pl.multiple_of(qb * t, t), t, 0, False)
            return carry

        if window:
            for c in range(t // LANES):
                first = b * t + c * LANES
                q0 = pl.multiple_of(jnp.minimum(first, s - (LANES + window)), LANES)
                tile(q0, LANES + window, q0 - first, True, slice(c * LANES, (c + 1) * LANES))
        else:
            tile(k0, t, 0, True)
            lax.fori_loop(b + 1, nblk, later_block, 0)
        dk_ref[...] = dk_acc[...].astype(bf16)
        dv_ref[...] = dv_acc[...].astype(bf16)

    kv_spec = pl.BlockSpec((t, LANES), lambda j, b: (b, j))
    seq_spec = pl.BlockSpec((s, LANES), lambda j, b: (0, j))
    rows_spec = pl.BlockSpec((None, 2, s), lambda j, b: (j, 0, 0))
    hw = N_PAIRS * LANES
    in_specs, args = [kv_spec, kv_spec, seq_spec, seq_spec, rows_spec, rows_spec], [k, v, q, do, lse, delta]
    out_specs = [pl.BlockSpec((LANES, s), lambda j, b: (j, 0)), kv_spec, kv_spec]
    out_shape = [jax.ShapeDtypeStruct((hw, s), f32), jax.ShapeDtypeStruct((s, hw), bf16), jax.ShapeDtypeStruct((s, hw), bf16)]
    if fox:
        in_specs += [pl.BlockSpec((s, 2 * LANES), lambda j, b: (0, j))]
        args += [cum_b]
        out_specs += [pl.BlockSpec((t, 2 * LANES), lambda j, b: (b, j)), rows_spec]
        out_shape += [jax.ShapeDtypeStruct((s, N_HEADS * LANES), f32), jax.ShapeDtypeStruct((N_PAIRS, 2, s), f32)]
    return pl.pallas_call(
        body, name=name, grid=(N_PAIRS, nblk), in_specs=in_specs, out_specs=out_specs, out_shape=out_shape,
        scratch_shapes=[pltpu.VMEM((t, LANES), f32)] * 2, compiler_params=_params(2),
    )(*args)


def _branch_merge(o_a, o_b, w_a, w_b, gl, name):
    s, k = o_a.shape
    d = w_a.shape[1]
    tm = _row_tile(s, 1024)

    def body(oa_ref, ob_ref, wa_ref, wb_ref, g_ref, ba_ref, bb_ref, m_ref):
        ba = jnp.dot(oa_ref[...], wa_ref[...], preferred_element_type=f32)
        bb = jnp.dot(ob_ref[...], wb_ref[...], preferred_element_type=f32)
        g0, g1 = jax.nn.sigmoid(g_ref[:, :d].astype(f32)), jax.nn.sigmoid(g_ref[:, d:].astype(f32))
        ba_ref[...] = ba.astype(bf16)
        bb_ref[...] = bb.astype(bf16)
        m_ref[...] = (g0 * ba + g1 * bb).astype(bf16)

    whole = pl.BlockSpec((k, d), lambda i: (0, 0))
    return pl.pallas_call(
        body, name=name, grid=(s // tm,),
        in_specs=[_row_spec(tm, k), _row_spec(tm, k), whole, whole, _row_spec(tm, 2 * d)],
        out_specs=[_row_spec(tm, d)] * 3, out_shape=[jax.ShapeDtypeStruct((s, d), bf16)] * 3, compiler_params=_params(1),
    )(o_a, o_b, w_a, w_b, gl)


def _out_dgrad_merge_bwd(dy, w_out, ba, bb, gl, name):
    s, d = ba.shape
    tm = _row_tile(s, 512)

    def body(dy_ref, w_ref, a_ref, b_ref, g_ref, da_ref, db_ref, dg_ref):
        dmv = lax.dot_general(dy_ref[...], w_ref[...], _NT, preferred_element_type=f32)
        g0, g1 = jax.nn.sigmoid(g_ref[:, :d].astype(f32)), jax.nn.sigmoid(g_ref[:, d:].astype(f32))
        da_ref[...] = (dmv * g0).astype(bf16)
        db_ref[...] = (dmv * g1).astype(bf16)
        dg_ref[:, :d] = (dmv * a_ref[...].astype(f32) * (g0 * (1.0 - g0))).astype(bf16)
        dg_ref[:, d:] = (dmv * b_ref[...].astype(f32) * (g1 * (1.0 - g1))).astype(bf16)

    return pl.pallas_call(
        body, name=name, grid=(s // tm,),
        in_specs=[_row_spec(tm, dy.shape[1]), pl.BlockSpec(w_out.shape, lambda i: (0, 0))] + [_row_spec(tm, d)] * 2
        + [_row_spec(tm, 2 * d)],
        out_specs=[_row_spec(tm, d)] * 2 + [_row_spec(tm, 2 * d)],
        out_shape=[jax.ShapeDtypeStruct((s, d), bf16)] * 2 + [jax.ShapeDtypeStruct((s, 2 * d), bf16)],
        compiler_params=_params(1),
    )(dy, w_out, ba, bb, gl)


GLU_TILE = 256


def _ffn_in_swiglu(h, w_t, name):
    s, d = h.shape
    f = w_t.shape[0] // 2
    tm = _row_tile(s, 2048)
    tg = GLU_TILE
    nb = f // tg

    def body(h_ref, wg_ref, wu_ref, g_ref, u_ref, act_ref):
        hv = h_ref[...]
        g = lax.dot_general(hv, wg_ref[...], _NT, preferred_element_type=f32)
        u = lax.dot_general(hv, wu_ref[...], _NT, preferred_element_type=f32)
        g_ref[...] = g.astype(bf16)
        u_ref[...] = u.astype(bf16)
        act_ref[...] = (g * jax.nn.sigmoid(g) * u).astype(bf16)

    col = pl.BlockSpec((tm, tg), lambda i, j: (i, j))
    return pl.pallas_call(
        body, name=name, grid=(s // tm, nb),
        in_specs=[pl.BlockSpec((tm, d), lambda i, j: (i, 0)), pl.BlockSpec((tg, d), lambda i, j: (j, 0)),
                  pl.BlockSpec((tg, d), lambda i, j: (j + nb, 0))],
        out_specs=[col] * 3, out_shape=[jax.ShapeDtypeStruct((s, f), bf16)] * 3, compiler_params=_params(2),
    )(h, w_t, w_t)


def _ffn_out_dgrad_swiglu(dy, w_out, g, u, name):
    s, d = dy.shape
    f = g.shape[1]
    tm = _row_tile(s, 2048)
    tg = GLU_TILE

    def body(dy_ref, w_ref, g_ref, u_ref, dg_ref, du_ref):
        dv = lax.dot_general(dy_ref[...], w_ref[...], _NT, preferred_element_type=f32)
        gv, uv = g_ref[...].astype(f32), u_ref[...].astype(f32)
        sg = jax.nn.sigmoid(gv)
        dg_ref[...] = (dv * uv * (sg * (1.0 + gv * (1.0 - sg)))).astype(bf16)
        du_ref[...] = (dv * (gv * sg)).astype(bf16)

    col = pl.BlockSpec((tm, tg), lambda i, j: (i, j))
    return pl.pallas_call(
        body, name=name, grid=(s // tm, f // tg),
        in_specs=[pl.BlockSpec((tm, d), lambda i, j: (i, 0)), pl.BlockSpec((tg, d), lambda i, j: (j, 0)), col, col],
        out_specs=[col] * 2, out_shape=[jax.ShapeDtypeStruct((s, f), bf16)] * 2, compiler_params=_params(2),
    )(dy, w_out, g, u)


def _wgrad_stack(parts, h, name):
    s, m = parts[0].shape
    d = h.shape[1]
    tm = 256
    nb = m // tm
    n = len(parts)

    def body(*refs):
        i = pl.program_id(0)
        for p in range(n):
            @pl.when(i // nb == p)
            def _(p=p):
                refs[n + 1][...] = lax.dot_general(refs[p][...], refs[n][...], _TN, preferred_element_type=f32).astype(bf16)

    a_specs = [pl.BlockSpec((s, tm), lambda i, p=p: (0, jnp.clip(i - p * nb, 0, nb - 1))) for p in range(n)]
    return pl.pallas_call(
        body, name=name, grid=(n * nb,), in_specs=a_specs + [pl.BlockSpec((s, d), lambda i: (0, 0))],
        out_specs=pl.BlockSpec((tm, d), lambda i: (i, 0)),
        out_shape=jax.ShapeDtypeStruct((n * m, d), bf16), compiler_params=_params(1),
    )(*parts, h)


def _ada_wgrad(c_all, d_all, name):
    n, d = c_all.shape
    w = d_all.shape[1]

    def body(c_ref, d_ref, o_ref):
        eye = (lax.broadcasted_iota(jnp.int32, (n, n), 0) == lax.broadcasted_iota(jnp.int32, (n, n), 1)).astype(f32)
        ct = lax.dot_general(c_ref[...], eye, _TN, precision=lax.Precision.HIGHEST, preferred_element_type=f32)
        g = ct[:, 0:1] * d_ref[0:1, :]
        for bi in range(1, n):
            g = g + ct[:, bi:bi + 1] * d_ref[bi:bi + 1, :]
        o_ref[0] = g

    return pl.pallas_call(
        body, name=name, out_shape=jax.ShapeDtypeStruct((1, d, w), f32), compiler_params=_params(),
    )(c_all, d_all)


def _adamw(parts, w, m, v, name, mine=None, me=None):
    r, c = w.shape
    n_parts = parts.shape[0]
    row_tiles = [t for t in range(min(r, 256), 0, -1) if r % t == 0 and (t % 16 == 0 or t == r)]
    if row_tiles:
        tr, tc = row_tiles[0], c
    else:
        tr, tc = r, next(t for t in (256, LANES) if c % t == 0)

    def body(*refs):
        w_ref, m_ref, v_ref, g_ref, d_ref, nm_ref, nv_ref = refs[-7:]
        if mine is None:
            p_ref, = refs[:-7]
        else:
            me_ref, p_ref, own_ref = refs[:-7]

        def part(i):
            if mine is None:
                return p_ref[i].astype(f32)
            return jnp.where(me_ref[0] == i, own_ref[...], p_ref[i]).astype(f32)

        g = part(0)
        for i in range(1, n_parts):
            g = g + part(i)
        mm = ADAM_B1 * m_ref[...] + (1.0 - ADAM_B1) * g
        vv = ADAM_B2 * v_ref[...] + (1.0 - ADAM_B2) * (g * g)
        m_hat = mm / (1.0 - ADAM_B1 ** ADAM_STEP)
        v_hat = vv / (1.0 - ADAM_B2 ** ADAM_STEP)
        g_ref[...] = g
        d_ref[...] = -ADAM_LR * (m_hat / (jnp.sqrt(v_hat) + ADAM_EPS) + ADAM_WD * w_ref[...])
        nm_ref[...] = mm
        nv_ref[...] = vv

    out_shape = [jax.ShapeDtypeStruct((r, c), f32)] * 4
    if mine is None:
        spec = pl.BlockSpec((tr, tc), lambda i, j: (i, j))
        return pl.pallas_call(
            body, name=name, grid=(r // tr, c // tc),
            in_specs=[pl.BlockSpec((n_parts, tr, tc), lambda i, j: (0, i, j))] + [spec] * 3,
            out_specs=[spec] * 4, out_shape=out_shape, compiler_params=_params(2),
        )(parts, w, m, v)
    spec = pl.BlockSpec((tr, tc), lambda i, j, me_ref: (i, j))
    return pl.pallas_call(
        body, name=name, out_shape=out_shape, compiler_params=_params(2),
        grid_spec=pltpu.PrefetchScalarGridSpec(
            num_scalar_prefetch=1, grid=(r // tr, c // tc),
            in_specs=[pl.BlockSpec((n_parts, tr, tc), lambda i, j, me_ref: (0, i, j)),
                      pl.BlockSpec((None, tr, tc), lambda i, j, me_ref: (me_ref[0], i, j))] + [spec] * 3,
            out_specs=[spec] * 4),
    )(me, parts, mine, w, m, v)


def _me():
    return lax.axis_index("x"), lax.axis_index("y"), lax.axis_index("c")


def _all_gather(arrays, name, vmem=False, after=None):
    n = len(arrays)
    space = pltpu.VMEM if vmem else pl.ANY
    extra = [] if after is None else [after]

    def body(*refs):
        ins = refs[:n]
        outs = refs[n + len(extra):2 * n + len(extra)]
        send_sems, recv_sems, local_sems = refs[2 * n + len(extra):]
        x, y, c = _me()
        me, sibling = (x, y, c), (x, y, 1 - c)
        chips = [(1 - x, y), (x, 1 - y), (1 - x, 1 - y)]

        def rows(a, dev):
            return outs[a].at[4 * dev[0] + 2 * dev[1] + dev[2]]

        def copy(a, k, block, to, src=None):
            return pltpu.make_async_remote_copy(
                src_ref=rows(a, block) if src is None else src, dst_ref=rows(a, block),
                send_sem=send_sems.at[a, k], recv_sem=recv_sems.at[a, k], device_id=to, device_id_type=MESH)

        mine = [pltpu.make_async_copy(ins[a], rows(a, me), local_sems.at[a]) for a in range(n)]
        for cp in mine:
            cp.start()
        first = []
        for a in range(n):
            first.append(copy(a, 0, me, sibling, src=ins[a]))
            first += [copy(a, 1 + j, me, (*chip, c), src=ins[a]) for j, chip in enumerate(chips)]
        for cp in first:
            cp.start()
        passed = []
        for j, chip in enumerate(chips):
            for a in range(n):
                copy(a, 1 + j, (*chip, c), me).wait_recv()
                fwd = copy(a, 4 + j, (*chip, c), sibling)
                fwd.start()
                passed.append(fwd)
        for a in range(n):
            copy(a, 0, sibling, me).wait_recv()
            for j, chip in enumerate(chips):
                copy(a, 4 + j, (*chip, 1 - c), me).wait_recv()
        for cp in first + passed:
            cp.wait_send()
        for cp in mine:
            cp.wait()

    outs = pl.pallas_call(
        body, name=name,
        in_specs=[pl.BlockSpec(memory_space=space)] * n + [pl.BlockSpec(memory_space=pl.ANY)] * len(extra),
        out_specs=[pl.BlockSpec(memory_space=space)] * n,
        out_shape=[jax.ShapeDtypeStruct((N_DEV,) + a.shape, a.dtype) for a in arrays],
        scratch_shapes=[pltpu.SemaphoreType.DMA((n, 7)), pltpu.SemaphoreType.DMA((n, 7)), pltpu.SemaphoreType.DMA((n,))],
        compiler_params=pltpu.CompilerParams(vmem_limit_bytes=VMEM_LIMIT),
    )(*arrays, *extra)
    return list(outs)


def _gather_prologue(c, w_ada, b_mine, w_in_t, name):
    n_dev, d = N_DEV, c.shape[1]
    ada_w = w_ada.shape[1]

    def body(c_ref, w_ref, b_ref, win_ref, call_ref, ada_ref, gin_ref, cols_ref, send_sems, recv_sems, local_sems):
        x, y, cc = _me()
        me, sibling = (x, y, cc), (x, y, 1 - cc)
        chips = [(1 - x, y), (x, 1 - y), (1 - x, 1 - y)]
        outs = (call_ref, ada_ref, gin_ref)

        def rows(a, dev):
            return outs[a].at[4 * dev[0] + 2 * dev[1] + dev[2]]

        def copy(a, k, block, to, src=None):
            return pltpu.make_async_remote_copy(
                src_ref=rows(a, block) if src is None else src, dst_ref=rows(a, block),
                send_sem=send_sems.at[a, k], recv_sem=recv_sems.at[a, k], device_id=to, device_id_type=MESH)

        def begin(a, src):
            own = pltpu.make_async_copy(src, rows(a, me), local_sems.at[a])
            sends = [copy(a, 0, me, sibling, src=src)] + [copy(a, 1 + j, me, (*chip, cc), src=src) for j, chip in enumerate(chips)]
            for cp in [own] + sends:
                cp.start()
            return own, sends

        def finish(a, own, sends):
            passed = []
            for j, chip in enumerate(chips):
                copy(a, 1 + j, (*chip, cc), me).wait_recv()
                passed.append(copy(a, 4 + j, (*chip, cc), sibling))
                passed[-1].start()
            copy(a, 0, sibling, me).wait_recv()
            for j, chip in enumerate(chips):
                copy(a, 4 + j, (*chip, 1 - cc), me).wait_recv()
            for cp in sends + passed:
                cp.wait_send()
            own.wait()

        finish(0, *begin(0, c_ref))
        cols_ref[...] = (jnp.dot(call_ref[:, 0, :].astype(bf16), w_ref[...].astype(bf16), preferred_element_type=f32)
                         + b_ref[...])
        finish(1, *begin(1, cols_ref))
        finish(2, *begin(2, win_ref))

    vmem, hbm = pl.BlockSpec(memory_space=pltpu.VMEM), pl.BlockSpec(memory_space=pl.ANY)
    return pl.pallas_call(
        body, name=name, in_specs=[vmem, vmem, vmem, hbm], out_specs=[vmem, vmem, hbm],
        out_shape=[jax.ShapeDtypeStruct((n_dev, 1, d), f32), jax.ShapeDtypeStruct((n_dev, n_dev, ada_w), f32),
                   jax.ShapeDtypeStruct((n_dev,) + w_in_t.shape, w_in_t.dtype)],
        scratch_shapes=[pltpu.VMEM((n_dev, ada_w), f32), pltpu.SemaphoreType.DMA((3, 7)), pltpu.SemaphoreType.DMA((3, 7)),
                        pltpu.SemaphoreType.DMA((3,))],
        compiler_params=pltpu.CompilerParams(vmem_limit_bytes=VMEM_LIMIT),
    )(c, w_ada, b_mine, w_in_t)


_FLIPS = ((0, 0, 1), (1, 0, 0), (0, 1, 0), (1, 1, 0), (1, 0, 1), (0, 1, 1), (1, 1, 1))
_HBM = pl.BlockSpec(memory_space=pltpu.HBM)
_SEM = pl.BlockSpec(memory_space=pltpu.SEMAPHORE)


def _exchange_copies(scatter, srcs, lands, send_sems, recv_sems):
    x, y, c = _me()
    me_row = 4 * x + 2 * y + c
    out = []
    for k, (fx, fy, fc) in enumerate(_FLIPS):
        peer = (x ^ fx, y ^ fy, c ^ fc)
        peer_row = 4 * peer[0] + 2 * peer[1] + peer[2]
        for a in range(len(srcs)):
            out.append(pltpu.make_async_remote_copy(
                src_ref=srcs[a].at[peer_row] if scatter else srcs[a], dst_ref=lands[a].at[me_row],
                send_sem=send_sems.at[7 * a + k], recv_sem=recv_sems.at[7 * a + k], device_id=peer, device_id_type=MESH))
    return out


def _exchange_start(arrays, scatter, name, after=None):
    n = len(arrays)
    lands = [lax.empty(a.shape if scatter else (N_DEV,) + a.shape, a.dtype) for a in arrays]
    extra = [] if after is None else [after]

    def body(*refs):
        srcs, zones = refs[:n], refs[n:2 * n]
        send_sems, recv_sems = refs[2 * n + len(extra)], refs[2 * n + len(extra) + 1]
        token = refs[-1]
        for cp in _exchange_copies(scatter, srcs, zones, send_sems, recv_sems):
            cp.start()
        token[...] = jnp.zeros_like(token)

    thru = [pltpu.HBM(a.shape, a.dtype) for a in list(arrays) + lands]
    outs = pl.pallas_call(
        body, name=name,
        out_shape=(pltpu.SemaphoreType.DMA((7 * n,)), pltpu.SemaphoreType.DMA((7 * n,)), *thru, jax.ShapeDtypeStruct((8, LANES), f32)),
        in_specs=[_HBM] * (2 * n) + [pl.BlockSpec(memory_space=pl.ANY)] * len(extra),
        out_specs=(_SEM, _SEM, *[_HBM] * (2 * n), pl.BlockSpec(memory_space=pltpu.VMEM)),
        input_output_aliases={i: 2 + i for i in range(2 * n)},
        compiler_params=pltpu.CompilerParams(has_side_effects=pltpu.SideEffectType.DATAFLOW_SIDE_EFFECTING),
    )(*[pltpu.with_memory_space_constraint(a, pltpu.HBM) for a in list(arrays) + lands], *extra)
    return dict(n=n, scatter=scatter, sems=outs[:2], srcs=outs[2:2 + n], lands=outs[2 + n:2 + 2 * n], token=outs[-1])


def _exchange_wait(handle, after, name):
    n, scatter = handle["n"], handle["scatter"]

    def body(*refs):
        srcs, zones = refs[:n], refs[n:2 * n]
        send_sems, recv_sems = refs[2 * n], refs[2 * n + 1]
        for cp in _exchange_copies(scatter, srcs, zones, send_sems, recv_sems):
            cp.wait_send()
            cp.wait_recv()

    thru = [pltpu.HBM(a.shape, a.dtype) for a in list(handle["srcs"]) + list(handle["lands"])]
    outs = pl.pallas_call(
        body, name=name, out_shape=tuple(thru),
        in_specs=[_HBM] * (2 * n) + [_SEM, _SEM, pl.BlockSpec(memory_space=pl.ANY)], out_specs=tuple([_HBM] * (2 * n)),
        input_output_aliases={i: i for i in range(2 * n)},
        compiler_params=pltpu.CompilerParams(has_side_effects=pltpu.SideEffectType.DATAFLOW_SIDE_EFFECTING),
    )(*handle["srcs"], *handle["lands"], *handle["sems"], after)
    return list(outs[n:])


def _cols_from_shards(g):
    return jnp.transpose(g, (1, 0, 2)).reshape(g.shape[1], -1)


def _shards_from_cols(a):
    return jnp.transpose(a.reshape(a.shape[0], N_DEV, -1), (1, 0, 2))


def _local_step(x, positions, ada, g_pre_mix, g_post_mix, b_f, sinks, g_pre_ffn, g_post_ffn, target,
                w_in_t, mix_weights, ffn_weights, on_grads):
    s, d = x.shape
    row = lambda v: v.reshape(1, -1)
    shift_m, scale_m, gate_m, shift_f, scale_f, gate_f = (ada[i:i + 1] for i in range(6))
    w_gate_t, w_qkv_t = w_in_t[F_OFF + N_HEADS:], w_in_t[:QKV_W]
    w_f_t = jnp.pad(w_in_t[F_OFF:F_OFF + N_HEADS], ((0, LANES - N_HEADS), (0, 0)))
    bf_row = jnp.pad(row(b_f), ((0, 0), (0, LANES - N_HEADS)))
    sink_rows = jnp.broadcast_to(sinks.reshape(N_HEADS, 1).astype(f32), (N_HEADS, LANES))
    inv_freq = 1.0 / (ROPE_THETA ** (jnp.arange(0, HEAD_DIM, 2, dtype=f32) / HEAD_DIM))
    cos, sin_s = _rope_tables(positions.reshape(s, 1), jnp.tile(inv_freq, 4).reshape(1, LANES), "rope_tables")

    h1, qa, ka, va, qb, kb, vb = _prenorm_proj_qkv(x, row(g_pre_mix), scale_m, shift_m, w_qkv_t, cos, sin_s, "prenorm_proj_qkv")
    gl = _matmul(h1, w_gate_t, "nt", bf16, "proj_gate")
    fl, cum_b = _forget_prep(h1, w_f_t, bf_row, "proj_forget_prep")
    o_a, lse_a = _attn_fwd(qa, ka, va, "swa_fwd", sink_rows=sink_rows, window=WINDOW, t=512)
    o_b, lse_b = _attn_fwd(qb, kb, vb, "fox_fwd", cum_b=cum_b, t=1024)
    everything_before = (gl[:8, :LANES] + o_a[:8, :LANES] + o_b[:8, :LANES]).astype(f32)
    w_branch_a, w_branch_b, w_out = mix_weights(everything_before)
    ba, bb, merged = _branch_merge(o_a, o_b, w_branch_a, w_branch_b, gl, "branch_merge")
    y1, x2, h2 = _out_proj_postnorm_prenorm(merged, w_out, x, row(g_post_mix), gate_m, row(g_pre_ffn), scale_f, shift_f,
                                            "out_proj_norms")

    w_ffn_in_t, w_ffn_out = ffn_weights(h2)
    g_ff, u_ff, act = _ffn_in_swiglu(h2, w_ffn_in_t, "ffn_in_swiglu")
    loss_row, d_out, d_y2, vec_pf = _out_proj_loss_tail(act, w_ffn_out, x2, row(g_post_ffn), gate_f, target, "ffn_out_loss_tail")

    g_w_ffn_out = _matmul(act, d_y2, "tn", bf16, "ffn_out_wgrad")
    dg_ff, du_ff = _ffn_out_dgrad_swiglu(d_y2, w_ffn_out, g_ff, u_ff, "ffn_out_dgrad_swiglu")
    g_w_ffn_in_t = _wgrad_stack([dg_ff, du_ff], h2, "ffn_in_wgrad")
    sent = on_grads(dict(w_ffn_in=g_w_ffn_in_t, w_ffn_out=g_w_ffn_out))
    d_x2, vec_nf, d_y1, vec_pm = _dgrad_prenorm_bwd(
        [(dg_ff, w_ffn_in_t, 0), (du_ff, w_ffn_in_t, 1)], x2, row(g_pre_ffn), scale_f, d_out, "ffn_in_dgrad_norms_bwd",
        after=sent, below=(y1, row(g_post_mix), gate_m))

    g_w_out = _matmul(merged, d_y1, "tn", bf16, "out_proj_wgrad")
    d_ba, d_bb, dgl = _out_dgrad_merge_bwd(d_y1, w_out, ba, bb, gl, "out_proj_dgrad_merge_bwd")
    g_w_branch_a = _matmul(o_a, d_ba, "tn", bf16, "branch_a_wgrad")
    g_w_branch_b = _matmul(o_b, d_bb, "tn", bf16, "branch_b_wgrad")
    sent = on_grads(dict(w_out=g_w_out, w_branch_a=g_w_branch_a, w_branch_b=g_w_branch_b))
    d_oa, delta_a, d_sink = _branch_dgrad_delta(d_ba, w_branch_a, o_a, "branch_a_dgrad_delta", lse=lse_a,
                                                sink_rows=sink_rows, after=sent)
    d_ob, delta_b = _branch_dgrad_delta(d_bb, w_branch_b, o_b, "branch_b_dgrad_delta", after=sent)
    dqa_t, dka, dva = _attn_bwd(qa, ka, va, d_oa, lse_a, delta_a, "swa_bwd", window=WINDOW, t=512)
    dqb_t, dkb, dvb, dcs, rs = _attn_bwd(qb, kb, vb, d_ob, lse_b, delta_b, "fox_bwd", cum_b=cum_b, t=512)
    dqkv = _qkv_prep_bwd(dqa_t, dka, dva, dqb_t, dkb, dvb, cos, sin_s, "qkv_prep_bwd")
    dfl, vec_bf = _forget_prep_bwd(rs.reshape(N_HEADS, s), dcs, fl, bf_row, "forget_prep_bwd")
    g_w_in_t = jnp.concatenate([_matmul(dqkv, h1, "tn", bf16, "qkv_wgrad"), _matmul(dfl, h1, "tn", bf16, "forget_wgrad")[:N_HEADS],
                                _matmul(dgl, h1, "tn", bf16, "gate_wgrad")], axis=0)
    sent = on_grads(dict(w_in=g_w_in_t))
    grad_x, vec_nm = _dgrad_prenorm_bwd([(dgl, w_gate_t, 0), (dqkv, w_qkv_t, 0), (dfl, w_f_t, 0)], x, row(g_pre_mix),
                                        scale_m, d_x2, "in_proj_dgrad_prenorm_bwd", after=sent)

    d_ada = jnp.concatenate([vec_nm[0], vec_nm[1], vec_pm[0], vec_nf[0], vec_nf[1], vec_pf[0]])
    small = dict(b_ada=d_ada, g_pre_mix=vec_nm[2], g_post_mix=vec_pm[1], g_pre_ffn=vec_nf[2], g_post_ffn=vec_pf[1],
                 b_f=vec_bf[0, :N_HEADS], sinks=d_sink[:, 0], loss=loss_row[0, :1])
    return grad_x, small


_SMALL = (("b_ada", 6144), ("g_pre_mix", 1024), ("g_post_mix", 1024), ("g_pre_ffn", 1024), ("g_post_ffn", 1024),
          ("b_f", 128), ("sinks", 128), ("loss", 128))
_SMALL_ROWS = 88


def _pack_small(vals):
    parts = [jnp.pad(vals[k].reshape(-1).astype(f32), (0, n - vals[k].size)) for k, n in _SMALL]
    flat = jnp.concatenate(parts)
    return jnp.pad(flat, (0, _SMALL_ROWS * LANES - flat.size)).reshape(_SMALL_ROWS, LANES)


def _unpack_small(slab, shapes):
    flat, out, off = slab.reshape(-1), {}, 0
    for k, n in _SMALL:
        size = math.prod(shapes[k])
        out[k] = flat[off:off + size].reshape(shapes[k])
        off += n
    return out


def kernel(x, c, positions, w_ada, b_ada, g_pre_mix, g_post_mix, w_in, b_f, sinks, w_branch_a, w_branch_b, w_out, g_pre_ffn, g_post_ffn, w_ffn_in, w_ffn_out, loss_target, m_w_ada, m_b_ada, m_g_pre_mix, m_g_post_mix, m_w_in, m_b_f, m_sinks, m_w_branch_a, m_w_branch_b, m_w_out, m_g_pre_ffn, m_g_post_ffn, m_w_ffn_in, m_w_ffn_out, v_w_ada, v_b_ada, v_g_pre_mix, v_g_post_mix, v_w_in, v_b_f, v_sinks, v_w_branch_a, v_w_branch_b, v_w_out, v_g_pre_ffn, v_g_post_ffn, v_w_ffn_in, v_w_ffn_out):
    xi, yi, ci = _me()
    me = 4 * xi + 2 * yi + ci
    d = D_MODEL
    ada_w = w_ada.shape[2]

    transposed = ("w_in", "w_ffn_in")
    tr = lambda a: jnp.transpose(a[0])

    b_mine = lax.dynamic_slice(b_ada, (0, me * ada_w), (1, ada_w))
    c_all, ada_all, g_in = _gather_prologue(c, w_ada[0], b_mine, tr(w_in).astype(bf16), "gather_prologue")
    c_all = c_all.reshape(N_DEV, d)
    ada = lax.dynamic_index_in_dim(ada_all, me, axis=1, keepdims=False).reshape(6, d)
    late_mix = [w.astype(bf16) for w in (w_branch_a[0], w_branch_b[0], w_out[0])]
    late_ffn = [w.astype(bf16) for w in (tr(w_ffn_in), w_ffn_out[0])]
    mix_h = _exchange_start(late_mix, False, "gather_mix_start", after=g_in)
    ffn_h = _exchange_start(late_ffn, False, "gather_ffn_start", after=mix_h["token"])

    def mine_into(zone, block):
        return lax.dynamic_update_index_in_dim(zone, block, me, 0)

    def rows_from_shards(g):
        return g.reshape(g.shape[0] * g.shape[1], g.shape[2])

    def mix_weights(after):
        zones = _exchange_wait(mix_h, after, "gather_mix_wait")
        g_ba, g_bb, g_out = (mine_into(z, w) for z, w in zip(zones, late_mix))
        return _cols_from_shards(g_ba), _cols_from_shards(g_bb), rows_from_shards(g_out)

    def ffn_weights(after):
        zones = _exchange_wait(ffn_h, after, "gather_ffn_wait")
        g_fi, g_fo = (mine_into(z, w) for z, w in zip(zones, late_ffn))
        return rows_from_shards(g_fi), rows_from_shards(g_fo)

    row_sharded = ("w_out", "w_ffn_out") + transposed
    in_flight = []

    def on_grads(group):
        sends = [g.reshape(N_DEV, g.shape[0] // N_DEV, g.shape[1]) if nm in row_sharded else _shards_from_cols(g)
                 for nm, g in group.items()]
        handle = _exchange_start(sends, True, "scatter_start_%d" % len(in_flight))
        in_flight.append((list(group), sends, handle))
        return handle["token"]

    grad_x, small = _local_step(
        x[0], positions[0], ada + ffn_h["token"][0, 0], g_pre_mix[0], g_post_mix[0], b_f[0], sinks[0], g_pre_ffn[0],
        g_post_ffn[0], loss_target[0], rows_from_shards(g_in), mix_weights, ffn_weights, on_grads)

    ws = dict(w_in=(w_in, m_w_in, v_w_in), w_branch_a=(w_branch_a, m_w_branch_a, v_w_branch_a),
              w_branch_b=(w_branch_b, m_w_branch_b, v_w_branch_b), w_out=(w_out, m_w_out, v_w_out),
              w_ffn_in=(w_ffn_in, m_w_ffn_in, v_w_ffn_in), w_ffn_out=(w_ffn_out, m_w_ffn_out, v_w_ffn_out))
    res = {}

    def finish_group(gi, after):
        names, sends, handle = in_flight[gi]
        zones = _exchange_wait(handle, after, "scatter_wait_%d" % gi)
        for nm, zone, sent in zip(names, zones, sends):
            w, m, v = (tr(a) if nm in transposed else a[0] for a in ws[nm])
            out = _adamw(zone, w, m, v, "adamw_" + nm, mine=sent, me=me.reshape(1).astype(jnp.int32))
            after = out[0]
            res[nm] = [jnp.transpose(o) for o in out] if nm in transposed else out
        return after

    done = finish_group(1, finish_group(0, grad_x))

    slab_all, = _all_gather([_pack_small(small)], "gather_small", vmem=True, after=done)
    small_w = dict(b_ada=b_ada, g_pre_mix=g_pre_mix, g_post_mix=g_post_mix, g_pre_ffn=g_pre_ffn, g_post_ffn=g_post_ffn,
                   b_f=b_f, sinks=sinks, loss=jnp.zeros((1,), f32))
    small_m = dict(b_ada=m_b_ada, g_pre_mix=m_g_pre_mix, g_post_mix=m_g_post_mix, g_pre_ffn=m_g_pre_ffn,
                   g_post_ffn=m_g_post_ffn, b_f=m_b_f, sinks=m_sinks, loss=jnp.zeros((1,), f32))
    small_v = dict(b_ada=v_b_ada, g_pre_mix=v_g_pre_mix, g_post_mix=v_g_post_mix, g_pre_ffn=v_g_pre_ffn,
                   g_post_ffn=v_g_post_ffn, b_f=v_b_f, sinks=v_sinks, loss=jnp.ones((1,), f32))
    shapes = {k: small_w[k].shape for k, _ in _SMALL}
    s_out = _adamw(slab_all, _pack_small(small_w), _pack_small(small_m), _pack_small(small_v), "adamw_small")
    s_grad, s_delta, s_m, s_v = (_unpack_small(o, shapes) for o in s_out)

    d_ada_all = lax.dynamic_slice(slab_all[:, :6144 // LANES, :].reshape(N_DEV, 6144), (0, me * ada_w), (N_DEV, ada_w))
    ada_parts = _ada_wgrad(c_all, d_ada_all, "ada_wgrad")

    res["w_ada"] = _adamw(ada_parts, w_ada[0], m_w_ada[0], v_w_ada[0], "adamw_w_ada")
    finish_group(2, res["w_ada"][0])

    order = ["w_ada", "b_ada", "g_pre_mix", "g_post_mix", "w_in", "b_f", "sinks", "w_branch_a", "w_branch_b", "w_out",
             "g_pre_ffn", "g_post_ffn", "w_ffn_in", "w_ffn_out"]
    outs = [s_grad["loss"].reshape(()), grad_x[None]]
    for which, small_o in enumerate((s_grad, s_delta, s_m, s_v)):
        for nm in order:
            outs.append(res[nm][which][None] if nm in res else small_o[nm])
    return tuple(outs)
```

```python
import math

import jax
import jax.numpy as jnp
from jax import lax
from jax.experimental import pallas as pl
from jax.experimental.pallas import tpu as pltpu

f32 = jnp.float32
bf16 = jnp.bfloat16

D_MODEL = 1024
HEAD_DIM = 64
N_HEADS = 8
N_PAIRS = 4
QKV_W = 2304
F_OFF = 2304
WINDOW = 128
ROPE_THETA = 10000.0
RMS_EPS = 1e-6
N_DEV = 8
ADAM_LR, ADAM_B1, ADAM_B2, ADAM_EPS, ADAM_WD, ADAM_STEP = 0.001, 0.9, 0.999, 1e-08, 0.01, 10
NEG = -1e30
L_ROW = (HEAD_DIM, 0)
LANES = 128
VMEM_LIMIT = 48 * 1024 * 1024
MESH = pl.DeviceIdType.MESH

_NT = (((1,), (1,)), ((), ()))
_TN = (((0,), (0,)), ((), ()))


def _params(n_grid=0):
    sem = ("arbitrary",) * n_grid if n_grid else None
    return pltpu.CompilerParams(dimension_semantics=sem, vmem_limit_bytes=VMEM_LIMIT)


def _row_tile(s, want):
    t = min(s, want)
    assert s % t == 0, (s, t)
    return t


MATMUL_VMEM_BUDGET = 40 * 1024 * 1024


def _matmul_tiles(m, n, k, a_item, b_item, o_item):
    def tiles(d):
        return [t for t in range(LANES, min(d, 2048) + 1, LANES) if d % t == 0] or [d]

    best = None
    for tm in tiles(m):
        for tn in tiles(n):
            vmem = 2 * (tm * k * a_item + tn * k * b_item + tm * tn * o_item) + tm * tn * 4
            if vmem > MATMUL_VMEM_BUDGET:
                continue
            traffic = m * k * a_item + n * k * b_item * (1 if tn == n else m // tm) + m * n * o_item
            steps = (m // tm) * (n // tn)
            key = (traffic, 0, steps) if steps >= 4 else (traffic, 1, -steps)
            if best is None or key < best[0]:
                best = (key, tm, tn)
    assert best is not None, (m, n, k)
    return best[1], best[2]


def _matmul(a, b, mode, out_dtype, name, after=None):
    if mode == "nn":
        (m, k), n = a.shape, b.shape[1]
    elif mode == "nt":
        (m, k), n = a.shape, b.shape[0]
    else:
        (k, m), n = a.shape, b.shape[1]
    tm, tn = _matmul_tiles(m, n, k, a.dtype.itemsize, b.dtype.itemsize, jnp.dtype(out_dtype).itemsize)
    if mode == "nn":
        a_spec, b_spec, dims = pl.BlockSpec((tm, k), lambda i, j: (i, 0)), pl.BlockSpec((k, tn), lambda i, j: (0, j)), None
    elif mode == "nt":
        a_spec, b_spec, dims = pl.BlockSpec((tm, k), lambda i, j: (i, 0)), pl.BlockSpec((tn, k), lambda i, j: (j, 0)), _NT
    else:
        a_spec, b_spec, dims = pl.BlockSpec((k, tm), lambda i, j: (0, i)), pl.BlockSpec((k, tn), lambda i, j: (0, j)), _TN

    def body(a_ref, b_ref, *rest):
        o_ref = rest[-1]
        av, bv = a_ref[...].astype(bf16), b_ref[...].astype(bf16)
        if dims is None:
            r = jnp.dot(av, bv, preferred_element_type=f32)
        else:
            r = lax.dot_general(av, bv, dims, preferred_element_type=f32)
        o_ref[...] = r.astype(out_dtype)

    extra = [] if after is None else [after]
    return pl.pallas_call(
        body, name=name, grid=(m // tm, n // tn), in_specs=[a_spec, b_spec] + [pl.BlockSpec(memory_space=pl.ANY)] * len(extra),
        out_specs=pl.BlockSpec((tm, tn), lambda i, j: (i, j)),
        out_shape=jax.ShapeDtypeStruct((m, n), out_dtype), compiler_params=_params(2),
    )(a, b, *extra)


def _rstd(v):
    return lax.rsqrt(jnp.mean(v * v, axis=-1, keepdims=True) + RMS_EPS)


def _row_spec(tm, d):
    return pl.BlockSpec((tm, d), lambda i: (i, 0))


def _vec_spec(d, rows=1):
    return pl.BlockSpec((rows, d), lambda i: (0, 0))


def _proj_spec(a, w, tm):
    return [_row_spec(tm, a.shape[1]), pl.BlockSpec(w.shape, lambda i: (0, 0))]


def _out_proj_postnorm_prenorm(a, w, x, g_post, gate, g_pre, scale, shift, name):
    s, d = x.shape
    tm = _row_tile(s, 512)

    def body(a_ref, w_ref, x_ref, gp_ref, gate_ref, g_ref, sc_ref, sh_ref, y_ref, x2_ref, h_ref):
        yv = jnp.dot(a_ref[...], w_ref[...], preferred_element_type=f32)
        y_ref[...] = yv
        x2 = x_ref[...] + gate_ref[...] * (yv * _rstd(yv) * gp_ref[...])
        x2_ref[...] = x2
        h_ref[...] = ((x2 * _rstd(x2) * g_ref[...]) * (1.0 + sc_ref[...]) + sh_ref[...]).astype(bf16)

    return pl.pallas_call(
        body, name=name, grid=(s // tm,), in_specs=_proj_spec(a, w, tm) + [_row_spec(tm, d)] + [_vec_spec(d)] * 5,
        out_specs=[_row_spec(tm, d)] * 3,
        out_shape=[jax.ShapeDtypeStruct((s, d), f32)] * 2 + [jax.ShapeDtypeStruct((s, d), bf16)], compiler_params=_params(1),
    )(a, w, x, g_post, gate, g_pre, scale, shift)


def _rms_bwd(u, v, r):
    return r * u - v * (r * r * r) * jnp.mean(u * v, axis=-1, keepdims=True)


def _out_proj_loss_tail(a, w, x, g, gate, target, name):
    s, d = x.shape
    tm = _row_tile(s, 512)

    def body(a_ref, w_ref, x_ref, g_ref, gate_ref, t_ref, loss_ref, do_ref, dy_ref, vec_ref):
        @pl.when(pl.program_id(0) == 0)
        def _():
            loss_ref[...] = jnp.zeros_like(loss_ref)
            vec_ref[...] = jnp.zeros_like(vec_ref)
        yv = jnp.dot(a_ref[...], w_ref[...], preferred_element_type=f32)
        r = _rstd(yv)
        yn = yv * r
        err = x_ref[...] + gate_ref[...] * (yn * g_ref[...]) - t_ref[...]
        loss_ref[...] += 0.5 * jnp.sum(jnp.mean(err * err, axis=-1, keepdims=True), axis=0, keepdims=True)
        dr = err / d
        do_ref[...] = dr
        dn = dr * gate_ref[...]
        vec_ref[0:1, :] += jnp.sum(dr * (yn * g_ref[...]), axis=0, keepdims=True)
        vec_ref[1:2, :] += jnp.sum(dn * yn, axis=0, keepdims=True)
        dy_ref[...] = _rms_bwd(dn * g_ref[...], yv, r).astype(bf16)

    return pl.pallas_call(
        body, name=name, grid=(s // tm,),
        in_specs=_proj_spec(a, w, tm) + [_row_spec(tm, d)] + [_vec_spec(d)] * 2 + [_row_spec(tm, d)],
        out_specs=[_vec_spec(LANES), _row_spec(tm, d), _row_spec(tm, d), _vec_spec(d, 8)],
        out_shape=[jax.ShapeDtypeStruct((1, LANES), f32), jax.ShapeDtypeStruct((s, d), f32),
                   jax.ShapeDtypeStruct((s, d), bf16), jax.ShapeDtypeStruct((8, d), f32)],
        compiler_params=_params(1),
    )(a, w, x, g, gate, target)


def _dgrad_prenorm_bwd(terms, x, g, scale, dres, name, after=None, below=None):
    s, d = x.shape
    n = len(terms)
    k = sum(a.shape[1] for a, _, _ in terms)
    row_bytes = 2 * (2 * k) + d * (4 + 2 * 4 * 3 + (2 * 4 + 2 * 2 if below else 0))
    tm = next(t for t in (512, 256, 128) if s % t == 0 and 4 * k * d + t * row_bytes <= MATMUL_VMEM_BUDGET)
    extra = [] if after is None else [after]

    def body(*refs):
        a_refs, b_refs = refs[:n], refs[n:2 * n]
        x_ref, g_ref, sc_ref, dr_ref = refs[2 * n:2 * n + 4]
        n_in = 2 * n + 4 + (3 if below else 0) + len(extra)
        dx_ref, vec_ref = refs[n_in], refs[n_in + 1]
        if below:
            y_ref, gp_ref, gate_ref = refs[2 * n + 4:2 * n + 7]
            dy_ref, vec2_ref = refs[n_in + 2], refs[n_in + 3]

        @pl.when(pl.program_id(0) == 0)
        def _():
            vec_ref[...] = jnp.zeros_like(vec_ref)
            if below:
                vec2_ref[...] = jnp.zeros_like(vec2_ref)
        dhv = jnp.dot(a_refs[0][...], b_refs[0][...], preferred_element_type=f32)
        for i in range(1, n):
            dhv = dhv + jnp.dot(a_refs[i][...], b_refs[i][...], preferred_element_type=f32)
        xv = x_ref[...]
        r = _rstd(xv)
        xn = xv * r
        dn = dhv * (1.0 + sc_ref[...])
        vec_ref[0:1, :] += jnp.sum(dhv, axis=0, keepdims=True)
        vec_ref[1:2, :] += jnp.sum(dhv * (xn * g_ref[...]), axis=0, keepdims=True)
        vec_ref[2:3, :] += jnp.sum(dn * xn, axis=0, keepdims=True)
        dx = dr_ref[...] + _rms_bwd(dn * g_ref[...], xv, r)
        dx_ref[...] = dx
        if below:
            yv = y_ref[...]
            ry = _rstd(yv)
            yn = yv * ry
            dny = dx * gate_ref[...]
            vec2_ref[0:1, :] += jnp.sum(dx * (yn * gp_ref[...]), axis=0, keepdims=True)
            vec2_ref[1:2, :] += jnp.sum(dny * yn, axis=0, keepdims=True)
            dy_ref[...] = _rms_bwd(dny * gp_ref[...], yv, ry).astype(bf16)

    in_specs = ([_row_spec(tm, a.shape[1]) for a, _, _ in terms]
                + [pl.BlockSpec((a.shape[1], d), lambda i, r=r: (r, 0)) for a, _, r in terms]
                + [_row_spec(tm, d)] + [_vec_spec(d)] * 2 + [_row_spec(tm, d)])
    out_specs = [_row_spec(tm, d), _vec_spec(d, 8)]
    out_shape = [jax.ShapeDtypeStruct((s, d), f32), jax.ShapeDtypeStruct((8, d), f32)]
    args = [a for a, _, _ in terms] + [b for _, b, _ in terms] + [x, g, scale, dres]
    if below:
        in_specs += [_row_spec(tm, d)] + [_vec_spec(d)] * 2
        out_specs += [_row_spec(tm, d), _vec_spec(d, 8)]
        out_shape += [jax.ShapeDtypeStruct((s, d), bf16), jax.ShapeDtypeStruct((8, d), f32)]
        args += list(below)
    return pl.pallas_call(
        body, name=name, grid=(s // tm,), in_specs=in_specs + [pl.BlockSpec(memory_space=pl.ANY)] * len(extra),
        out_specs=out_specs, out_shape=out_shape, compiler_params=_params(1),
    )(*args, *extra)


def _lane():
    return lax.broadcasted_iota(jnp.int32, (1, LANES), 1)


def _rope_tables(pos_col, inv_freq, name):
    s = pos_col.shape[0]

    def body(p_ref, f_ref, cos_ref, sin_ref):
        ang = p_ref[...].astype(f32) * f_ref[...]
        first_half = (_lane() % HEAD_DIM) < HEAD_DIM // 2
        cos_ref[...] = jnp.cos(ang)
        sn = jnp.sin(ang)
        sin_ref[...] = jnp.where(first_half, -sn, sn)

    return pl.pallas_call(
        body, name=name, out_shape=[jax.ShapeDtypeStruct((s, LANES), f32)] * 2, compiler_params=_params(),
    )(pos_col, inv_freq)


def _swap_halves(v):
    first_half = (_lane() % HEAD_DIM) < HEAD_DIM // 2
    return jnp.where(first_half, pltpu.roll(v, LANES - HEAD_DIM // 2, axis=1), pltpu.roll(v, HEAD_DIM // 2, axis=1))


def _prenorm_proj_qkv(x, g, mod_scale, mod_shift, w_qkv_t, cos, sin_s, name):
    s, d = x.shape
    tm = _row_tile(s, 512)
    scale = 1.0 / math.sqrt(HEAD_DIM)

    def body(x_ref, g_ref, msc_ref, msh_ref, w_ref, c_ref, s_ref, h_ref, qa_ref, ka_ref, va_ref, qb_ref, kb_ref, vb_ref):
        xv = x_ref[...]
        h = ((xv * _rstd(xv) * g_ref[...]) * (1.0 + msc_ref[...]) + msh_ref[...]).astype(bf16)
        h_ref[...] = h
        proj = lax.dot_general(h, w_ref[...], _NT, preferred_element_type=f32)
        cs, sn = c_ref[...], s_ref[...]
        low = _lane() < HEAD_DIM

        def blk(j):
            return proj[:, j * LANES:(j + 1) * LANES]

        def rope(v):
            return v * cs + _swap_halves(v) * sn

        def expand(v):
            other = pltpu.roll(v, HEAD_DIM, axis=1)
            return jnp.where(low, v, other), jnp.where(low, other, v)

        for j in range(N_PAIRS):
            qa_ref[:, j * LANES:(j + 1) * LANES] = (rope(blk(j)) * scale).astype(bf16)
            qb_ref[:, j * LANES:(j + 1) * LANES] = (blk(6 + j) * scale).astype(bf16)
            kb_ref[:, j * LANES:(j + 1) * LANES] = blk(10 + j).astype(bf16)
            vb_ref[:, j * LANES:(j + 1) * LANES] = blk(14 + j).astype(bf16)
        k0, k1 = expand(rope(blk(4)))
        v0, v1 = expand(blk(5))
        for j in range(N_PAIRS):
            ka_ref[:, j * LANES:(j + 1) * LANES] = (k0 if j < 2 else k1).astype(bf16)
            va_ref[:, j * LANES:(j + 1) * LANES] = (v0 if j < 2 else v1).astype(bf16)

    hw = N_PAIRS * LANES
    return pl.pallas_call(
        body, name=name, grid=(s // tm,),
        in_specs=[_row_spec(tm, d)] + [_vec_spec(d)] * 3
        + [pl.BlockSpec((QKV_W, d), lambda i: (0, 0)), _row_spec(tm, LANES), _row_spec(tm, LANES)],
        out_specs=[_row_spec(tm, d)] + [_row_spec(tm, hw)] * 6,
        out_shape=[jax.ShapeDtypeStruct((s, d), bf16)] + [jax.ShapeDtypeStruct((s, hw), bf16)] * 6, compiler_params=_params(1),
    )(x, g, mod_scale, mod_shift, w_qkv_t, cos, sin_s)


def _qkv_prep_bwd(dqa_t, dka, dva, dqb_t, dkb, dvb, cos, sin_s, name):
    s = dka.shape[0]
    tm = _row_tile(s, 256)
    scale = 1.0 / math.sqrt(HEAD_DIM)
    hw = N_PAIRS * LANES
    t_spec = pl.BlockSpec((hw, tm), lambda i: (0, i))

    def body(dqa_ref, dka_ref, dva_ref, dqb_ref, dkb_ref, dvb_ref, c_ref, s_ref, o_ref):
        cs, sn = c_ref[...], s_ref[...]
        low = _lane() < HEAD_DIM

        def blk(ref, j):
            return ref[:, j * LANES:(j + 1) * LANES].astype(f32)

        def blk_t(ref, j):
            return ref[j * LANES:(j + 1) * LANES, :].T

        def unrope(v):
            return v * cs + _swap_halves(v * sn)

        def fold(ref):
            a, b = blk(ref, 0) + blk(ref, 1), blk(ref, 2) + blk(ref, 3)
            kv0 = a + pltpu.roll(a, HEAD_DIM, axis=1)
            kv1 = b + pltpu.roll(b, HEAD_DIM, axis=1)
            return jnp.where(low, kv0, kv1)

        for j in range(N_PAIRS):
            o_ref[:, j * LANES:(j + 1) * LANES] = (unrope(blk_t(dqa_ref, j)) * scale).astype(bf16)
            o_ref[:, (6 + j) * LANES:(7 + j) * LANES] = (blk_t(dqb_ref, j) * scale).astype(bf16)
            o_ref[:, (10 + j) * LANES:(11 + j) * LANES] = blk(dkb_ref, j).astype(bf16)
            o_ref[:, (14 + j) * LANES:(15 + j) * LANES] = blk(dvb_ref, j).astype(bf16)
        o_ref[:, 4 * LANES:5 * LANES] = unrope(fold(dka_ref)).astype(bf16)
        o_ref[:, 5 * LANES:6 * LANES] = fold(dva_ref).astype(bf16)

    return pl.pallas_call(
        body, name=name, grid=(s // tm,),
        in_specs=[t_spec, _row_spec(tm, hw), _row_spec(tm, hw), t_spec, _row_spec(tm, hw), _row_spec(tm, hw)] + [_row_spec(tm, LANES)] * 2,
        out_specs=_row_spec(tm, QKV_W), out_shape=jax.ShapeDtypeStruct((s, QKV_W), bf16), compiler_params=_params(1),
    )(dqa_t, dka, dva, dqb_t, dkb, dvb, cos, sin_s)


def _cumsum_rows(v, reverse=False):
    n = v.shape[0]
    row = lax.broadcasted_iota(jnp.int32, v.shape, 0)
    sh = 1
    while sh < n:
        if reverse:
            v = v + jnp.where(row < n - sh, pltpu.roll(v, n - sh, axis=0), 0.0)
        else:
            v = v + jnp.where(row >= sh, pltpu.roll(v, sh, axis=0), 0.0)
        sh *= 2
    return v


def _log_sigmoid(z):
    return jnp.minimum(z, 0.0) - jnp.log1p(jnp.exp(-jnp.abs(z)))


def _forget_prep(h, w_f_t, bf_row, name):
    s = h.shape[0]

    def body(h_ref, w_ref, b_ref, f_ref, cb_ref):
        fl = lax.dot_general(h_ref[...], w_ref[...], _NT, preferred_element_type=f32)
        f_ref[...] = fl
        cum = _cumsum_rows(_log_sigmoid(fl + b_ref[...]))
        for hd in range(N_HEADS):
            cb_ref[:, hd * LANES:(hd + 1) * LANES] = jnp.broadcast_to(cum[:, hd:hd + 1], (s, LANES))

    return pl.pallas_call(
        body, name=name,
        out_shape=[jax.ShapeDtypeStruct((s, LANES), f32), jax.ShapeDtypeStruct((s, N_HEADS * LANES), f32)],
        compiler_params=_params(),
    )(h, w_f_t, bf_row)


def _forget_prep_bwd(rs, dcs, fl, bf_row, name):
    s = fl.shape[0]

    def body(r_ref, c_ref, f_ref, b_ref, df_ref, db_ref):
        eye = (lax.broadcasted_iota(jnp.int32, (N_HEADS, LANES), 0) == lax.broadcasted_iota(jnp.int32, (N_HEADS, LANES), 1)).astype(f32)
        dcum = lax.dot_general(r_ref[...], eye, _TN, precision=lax.Precision.HIGHEST, preferred_element_type=f32)
        for h in range(N_HEADS):
            dcum = dcum - jnp.where(_lane() == h, jnp.sum(c_ref[:, h * LANES:(h + 1) * LANES], axis=1, keepdims=True), 0.0)
        dlf = _cumsum_rows(dcum, reverse=True)
        z = f_ref[...] + b_ref[...]
        df = jnp.where(_lane() < N_HEADS, dlf * jax.nn.sigmoid(-z), 0.0)
        df_ref[...] = df.astype(bf16)
        db_ref[...] = jnp.zeros_like(db_ref)
        db_ref[0:1, :] = jnp.sum(df, axis=0, keepdims=True)

    return pl.pallas_call(
        body, name=name,
        out_shape=[jax.ShapeDtypeStruct((s, LANES), bf16), jax.ShapeDtypeStruct((8, LANES), f32)], compiler_params=_params(),
    )(rs, dcs, fl, bf_row)


def _tile_mask(n_keys, n_queries, off, window):
    shape = (n_keys, n_queries)
    d = lax.broadcasted_iota(jnp.int32, shape, 1) - lax.broadcasted_iota(jnp.int32, shape, 0) + off
    valid = d >= 0
    return jnp.logical_and(valid, d < window) if window else valid


def _wide(v, t):
    return jnp.concatenate([v] * (t // LANES), axis=1)


def _attn_fwd(q, k, v, name, *, cum_b=None, sink_rows=None, window=None, t=256):
    s = q.shape[0]
    t = _row_tile(s, t)
    fox, has_sink = cum_b is not None, sink_rows is not None
    assert not window or (window % LANES == 0 and LANES + window <= s)

    def body(*refs):
        q_ref, k_ref, v_ref = refs[:3]
        rest = list(refs[3:])
        cb_ref = rest.pop(0) if fox else None
        sink_ref = rest.pop(0) if has_sink else None
        o_ref, lse_ref = rest
        i = pl.program_id(1)
        low = _lane() < HEAD_DIM
        top = lax.broadcasted_iota(jnp.int32, (LANES, 1), 0) < HEAD_DIM
        q2 = q_ref[...]
        zero = jnp.zeros_like(q2)
        qms = (jnp.where(low, q2, zero), jnp.where(low, zero, q2))

        def tile(k0, n_keys, off, carry, masked, queries=slice(0, t)):
            nq = queries.stop - queries.start
            kblk, vblk = k_ref[pl.ds(k0, n_keys), :], v_ref[pl.ds(k0, n_keys), :]
            valid = _tile_mask(n_keys, nq, off, window) if masked else None
            ones = jnp.ones_like(vblk)
            vs = tuple(jnp.where(_lane() == L_ROW[h], ones, vblk) for h in range(2))

            def scores(h):
                return lax.dot_general(kblk, qms[h][queries], _NT, preferred_element_type=f32)

            def softmax(h, sc):
                m = carry[h][0]
                if fox:
                    sc = sc - _wide(cb_ref[pl.ds(k0, n_keys), h * LANES:(h + 1) * LANES], nq)
                if masked:
                    sc = jnp.where(valid, sc, NEG)
                m_new = jnp.maximum(m, jnp.max(sc, axis=0, keepdims=True))
                return m_new, jnp.exp(m - m_new), jnp.exp(sc - m_new).astype(bf16)

            def update(h, m_new, alpha, p):
                return m_new, alpha * carry[h][1] + lax.dot_general(vs[h], p, _TN, preferred_element_type=f32)

            if window:
                return tuple(update(h, *softmax(h, scores(h))) for h in range(2))
            scs = [scores(h) for h in range(2)]
            stats = [softmax(h, scs[h]) for h in range(2)]
            return tuple(update(h, *stats[h]) for h in range(2))

        def start(nq):
            if has_sink:
                row = lax.broadcasted_iota(jnp.int32, (LANES, nq), 0)
                return tuple((_wide(sink_ref[h:h + 1, :], nq), (row == L_ROW[h]).astype(f32)) for h in range(2))
            return tuple((jnp.full((1, nq), NEG, f32), jnp.zeros((LANES, nq), f32)) for h in range(2))

        def finish(carry, queries):
            (m0, a0), (m1, a1) = carry
            l0, l1 = a0[L_ROW[0]:L_ROW[0] + 1, :], a1[L_ROW[1]:L_ROW[1] + 1, :]
            o_t = jnp.where(top, a0 * (1.0 / l0), a1 * (1.0 / l1))
            o_ref[queries, :] = o_t.T.astype(bf16)
            lse_ref[0:1, queries] = m0 + jnp.log(l0)
            lse_ref[1:2, queries] = m1 + jnp.log(l1)

        if window:
            for c in range(t // LANES):
                queries = slice(c * LANES, (c + 1) * LANES)
                q0 = i * t + c * LANES
                k0 = pl.multiple_of(jnp.maximum(q0 - window, 0), LANES)
                finish(tile(k0, LANES + window, q0 - k0, start(LANES), True, queries), queries)
        else:
            carry = lax.fori_loop(0, i, lambda kb, c: tile(pl.multiple_of(kb * t, t), t, 0, c, False), start(t))
            finish(tile(pl.multiple_of(i * t, t), t, 0, carry, True), slice(0, t))

    q_spec = pl.BlockSpec((t, LANES), lambda j, i: (i, j))
    kv_spec = pl.BlockSpec((s, LANES), lambda j, i: (0, j))
    in_specs, args = [q_spec, kv_spec, kv_spec], [q, k, v]
    if fox:
        in_specs += [pl.BlockSpec((s, 2 * LANES), lambda j, i: (0, j))]
        args += [cum_b]
    if has_sink:
        in_specs += [pl.BlockSpec((None, 2, LANES), lambda j, i: (j, 0, 0))]
        args += [sink_rows.reshape(N_PAIRS, 2, LANES)]
    return pl.pallas_call(
        body, name=name, grid=(N_PAIRS, s // t), in_specs=in_specs,
        out_specs=[q_spec, pl.BlockSpec((None, 2, t), lambda j, i: (j, 0, i))],
        out_shape=[jax.ShapeDtypeStruct((s, N_PAIRS * LANES), bf16), jax.ShapeDtypeStruct((N_PAIRS, 2, s), f32)],
        compiler_params=_params(2),
    )(*args)


def _branch_dgrad_delta(db, w, o, name, *, lse=None, sink_rows=None, after=None):
    s, hw = o.shape
    tm = _row_tile(s, 512)
    has_sink = sink_rows is not None
    extra = [] if after is None else [after]

    def body(*refs):
        db_ref, w_ref, o_ref = refs[:3]
        outs = refs[3 + (2 if has_sink else 0) + len(extra):]
        do_ref, dl_ref = outs[:2]
        if has_sink:
            lse_ref, sink_ref = refs[3:5]
            ds_ref = outs[2]

            @pl.when(pl.program_id(0) == 0)
            def _():
                ds_ref[...] = jnp.zeros_like(ds_ref)
        do = lax.dot_general(db_ref[...], w_ref[...], _NT, preferred_element_type=f32).astype(bf16)
        do_ref[...] = do
        for j in range(N_PAIRS):
            cols = slice(j * LANES, (j + 1) * LANES)
            prod_t = (do[:, cols].astype(f32) * o_ref[:, cols].astype(f32)).T
            for h in range(2):
                dl = jnp.sum(prod_t[h * HEAD_DIM:(h + 1) * HEAD_DIM, :], axis=0, keepdims=True)
                dl_ref[j, h:h + 1, :] = dl
                if has_sink:
                    r = 2 * j + h
                    p_sink = jnp.exp(sink_ref[r:r + 1, 0:1] - lse_ref[j, h:h + 1, :])
                    ds_ref[r:r + 1, :] += -jnp.sum(p_sink * dl, axis=1, keepdims=True)

    rows_spec = pl.BlockSpec((N_PAIRS, 2, tm), lambda i: (0, 0, i))
    in_specs = [_row_spec(tm, db.shape[1]), pl.BlockSpec(w.shape, lambda i: (0, 0)), _row_spec(tm, hw)]
    args = [db, w, o]
    out_specs = [_row_spec(tm, hw), rows_spec]
    out_shape = [jax.ShapeDtypeStruct((s, hw), bf16), jax.ShapeDtypeStruct((N_PAIRS, 2, s), f32)]
    if has_sink:
        in_specs += [rows_spec, _vec_spec(LANES, N_HEADS)]
        args += [lse, sink_rows]
        out_specs += [_vec_spec(LANES, N_HEADS)]
        out_shape += [jax.ShapeDtypeStruct((N_HEADS, LANES), f32)]
    return pl.pallas_call(
        body, name=name, grid=(s // tm,), in_specs=in_specs + [pl.BlockSpec(memory_space=pl.ANY)] * len(extra),
        out_specs=out_specs, out_shape=out_shape, compiler_params=_params(1),
    )(*args, *extra)


def _attn_bwd(q, k, v, do, lse, delta, name, *, cum_b=None, window=None, t=256):
    s = q.shape[0]
    t = _row_tile(s, t)
    nblk = s // t
    fox = cum_b is not None
    assert not window or (window % LANES == 0 and LANES + window <= s)

    def body(*refs):
        k_ref, v_ref, q_ref, do_ref, lse_ref, dl_ref = refs[:6]
        rest = list(refs[6:])
        cb_ref = rest.pop(0) if fox else None
        dq_ref, dk_ref, dv_ref = rest[:3]
        dcs_ref, rs_ref = (rest[3], rest[4]) if fox else (None, None)
        dk_acc, dv_acc = rest[-2:]
        b = pl.program_id(1)
        k0 = pl.multiple_of(b * t, t)

        @pl.when(b == 0)
        def _():
            dq_ref[...] = jnp.zeros_like(dq_ref)
            if fox:
                rs_ref[...] = jnp.zeros_like(rs_ref)

        dk_acc[...] = jnp.zeros_like(dk_acc)
        dv_acc[...] = jnp.zeros_like(dv_acc)
        if fox:
            dcs_ref[...] = jnp.zeros_like(dcs_ref)
        low = _lane() < HEAD_DIM
        top = lax.broadcasted_iota(jnp.int32, (LANES, 1), 0) < HEAD_DIM
        kblk, vblk = k_ref[...], v_ref[...]
        k_t = kblk.astype(f32).T.astype(bf16)
        cks = [_wide(cb_ref[pl.ds(k0, t), h * LANES:(h + 1) * LANES], t) for h in range(2)] if fox else None

        def tile(q0, n_queries, off, masked, keys=slice(0, t)):
            cols = pl.ds(q0, n_queries)
            q2, do2 = q_ref[cols, :], do_ref[cols, :]
            zero = jnp.zeros_like(q2)
            valid = _tile_mask(keys.stop - keys.start, n_queries, off, window) if masked else None
            dq_parts = []
            for h in range(2):
                qm = jnp.where(low, q2, zero) if h == 0 else jnp.where(low, zero, q2)
                dom = jnp.where(low, do2, zero) if h == 0 else jnp.where(low, zero, do2)
                sc = lax.dot_general(kblk[keys], qm, _NT, preferred_element_type=f32)
                if fox:
                    sc = sc - cks[h]
                if masked:
                    sc = jnp.where(valid, sc, NEG)
                p = jnp.exp(sc - lse_ref[h:h + 1, cols])
                dp = lax.dot_general(vblk[keys], dom, _NT, preferred_element_type=f32)
                ds = p * (dp - dl_ref[h:h + 1, cols])
                pb, dsb = p.astype(bf16), ds.astype(bf16)
                dv_acc[keys, :] += jnp.dot(pb, dom, preferred_element_type=f32)
                dk_acc[keys, :] += jnp.dot(dsb, qm, preferred_element_type=f32)
                dq_parts.append(jnp.dot(k_t[:, keys], dsb, preferred_element_type=f32))
                if fox:
                    dcs_ref[:, h * LANES:(h + 1) * LANES] += sum(ds[:, g * LANES:(g + 1) * LANES] for g in range(t // LANES))
                    rs_ref[h:h + 1, cols] += jnp.sum(ds, axis=0, keepdims=True)
            dq_ref[:, cols] += jnp.where(top, dq_parts[0], dq_parts[1])

        def later_block(qb, carry):
            tile(pl.multiple_of(qb * t, t), t, 0, False)
            return carry

        if window:
            for c in range(t // LANES):
                first = b * t + c * LANES
                q0 = pl.multiple_of(jnp.minimum(first, s - (LANES + window)), LANES)
                tile(q0, LANES + window, q0 - first, True, slice(c * LANES, (c + 1) * LANES))
        else:
            tile(k0, t, 0, True)
            lax.fori_loop(b + 1, nblk, later_block, 0)
        dk_ref[...] = dk_acc[...].astype(bf16)
        dv_ref[...] = dv_acc[...].astype(bf16)

    kv_spec = pl.BlockSpec((t, LANES), lambda j, b: (b, j))
    seq_spec = pl.BlockSpec((s, LANES), lambda j, b: (0, j))
    rows_spec = pl.BlockSpec((None, 2, s), lambda j, b: (j, 0, 0))
    hw = N_PAIRS * LANES
    in_specs, args = [kv_spec, kv_spec, seq_spec, seq_spec, rows_spec, rows_spec], [k, v, q, do, lse, delta]
    out_specs = [pl.BlockSpec((LANES, s), lambda j, b: (j, 0)), kv_spec, kv_spec]
    out_shape = [jax.ShapeDtypeStruct((hw, s), f32), jax.ShapeDtypeStruct((s, hw), bf16), jax.ShapeDtypeStruct((s, hw), bf16)]
    if fox:
        in_specs += [pl.BlockSpec((s, 2 * LANES), lambda j, b: (0, j))]
        args += [cum_b]
        out_specs += [pl.BlockSpec((t, 2 * LANES), lambda j, b: (b, j)), rows_spec]
        out_shape += [jax.ShapeDtypeStruct((s, N_HEADS * LANES), f32), jax.ShapeDtypeStruct((N_PAIRS, 2, s), f32)]
    return pl.pallas_call(
        body, name=name, grid=(N_PAIRS, nblk), in_specs=in_specs, out_specs=out_specs, out_shape=out_shape,
        scratch_shapes=[pltpu.VMEM((t, LANES), f32)] * 2, compiler_params=_params(2),
    )(*args)


def _branch_merge(o_a, o_b, w_a, w_b, gl, name):
    s, k = o_a.shape
    d = w_a.shape[1]
    tm = _row_tile(s, 1024)

    def body(oa_ref, ob_ref, wa_ref, wb_ref, g_ref, ba_ref, bb_ref, m_ref):
        ba = jnp.dot(oa_ref[...], wa_ref[...], preferred_element_type=f32)
        bb = jnp.dot(ob_ref[...], wb_ref[...], preferred_element_type=f32)
        g0, g1 = jax.nn.sigmoid(g_ref[:, :d].astype(f32)), jax.nn.sigmoid(g_ref[:, d:].astype(f32))
        ba_ref[...] = ba.astype(bf16)
        bb_ref[...] = bb.astype(bf16)
        m_ref[...] = (g0 * ba + g1 * bb).astype(bf16)

    whole = pl.BlockSpec((k, d), lambda i: (0, 0))
    return pl.pallas_call(
        body, name=name, grid=(s // tm,),
        in_specs=[_row_spec(tm, k), _row_spec(tm, k), whole, whole, _row_spec(tm, 2 * d)],
        out_specs=[_row_spec(tm, d)] * 3, out_shape=[jax.ShapeDtypeStruct((s, d), bf16)] * 3, compiler_params=_params(1),
    )(o_a, o_b, w_a, w_b, gl)


def _out_dgrad_merge_bwd(dy, w_out, ba, bb, gl, name):
    s, d = ba.shape
    tm = _row_tile(s, 512)

    def body(dy_ref, w_ref, a_ref, b_ref, g_ref, da_ref, db_ref, dg_ref):
        dmv = lax.dot_general(dy_ref[...], w_ref[...], _NT, preferred_element_type=f32)
        g0, g1 = jax.nn.sigmoid(g_ref[:, :d].astype(f32)), jax.nn.sigmoid(g_ref[:, d:].astype(f32))
        da_ref[...] = (dmv * g0).astype(bf16)
        db_ref[...] = (dmv * g1).astype(bf16)
        dg_ref[:, :d] = (dmv * a_ref[...].astype(f32) * (g0 * (1.0 - g0))).astype(bf16)
        dg_ref[:, d:] = (dmv * b_ref[...].astype(f32) * (g1 * (1.0 - g1))).astype(bf16)

    return pl.pallas_call(
        body, name=name, grid=(s // tm,),
        in_specs=[_row_spec(tm, dy.shape[1]), pl.BlockSpec(w_out.shape, lambda i: (0, 0))] + [_row_spec(tm, d)] * 2
        + [_row_spec(tm, 2 * d)],
        out_specs=[_row_spec(tm, d)] * 2 + [_row_spec(tm, 2 * d)],
        out_shape=[jax.ShapeDtypeStruct((s, d), bf16)] * 2 + [jax.ShapeDtypeStruct((s, 2 * d), bf16)],
        compiler_params=_params(1),
    )(dy, w_out, ba, bb, gl)


GLU_TILE = 256


def _ffn_in_swiglu(h, w_t, name):
    s, d = h.shape
    f = w_t.shape[0] // 2
    tm = _row_tile(s, 2048)
    tg = GLU_TILE
    nb = f // tg

    def body(h_ref, wg_ref, wu_ref, g_ref, u_ref, act_ref):
        hv = h_ref[...]
        g = lax.dot_general(hv, wg_ref[...], _NT, preferred_element_type=f32)
        u = lax.dot_general(hv, wu_ref[...], _NT, preferred_element_type=f32)
        g_ref[...] = g.astype(bf16)
        u_ref[...] = u.astype(bf16)
        act_ref[...] = (g * jax.nn.sigmoid(g) * u).astype(bf16)

    col = pl.BlockSpec((tm, tg), lambda i, j: (i, j))
    return pl.pallas_call(
        body, name=name, grid=(s // tm, nb),
        in_specs=[pl.BlockSpec((tm, d), lambda i, j: (i, 0)), pl.BlockSpec((tg, d), lambda i, j: (j, 0)),
                  pl.BlockSpec((tg, d), lambda i, j: (j + nb, 0))],
        out_specs=[col] * 3, out_shape=[jax.ShapeDtypeStruct((s, f), bf16)] * 3, compiler_params=_params(2),
    )(h, w_t, w_t)


def _ffn_out_dgrad_swiglu(dy, w_out, g, u, name):
    s, d = dy.shape
    f = g.shape[1]
    tm = _row_tile(s, 2048)
    tg = GLU_TILE

    def body(dy_ref, w_ref, g_ref, u_ref, dg_ref, du_ref):
        dv = lax.dot_general(dy_ref[...], w_ref[...], _NT, preferred_element_type=f32)
        gv, uv = g_ref[...].astype(f32), u_ref[...].astype(f32)
        sg = jax.nn.sigmoid(gv)
        dg_ref[...] = (dv * uv * (sg * (1.0 + gv * (1.0 - sg)))).astype(bf16)
        du_ref[...] = (dv * (gv * sg)).astype(bf16)

    col = pl.BlockSpec((tm, tg), lambda i, j: (i, j))
    return pl.pallas_call(
        body, name=name, grid=(s // tm, f // tg),
        in_specs=[pl.BlockSpec((tm, d), lambda i, j: (i, 0)), pl.BlockSpec((tg, d), lambda i, j: (j, 0)), col, col],
        out_specs=[col] * 2, out_shape=[jax.ShapeDtypeStruct((s, f), bf16)] * 2, compiler_params=_params(2),
    )(dy, w_out, g, u)


def _wgrad_stack(parts, h, name):
    s, m = parts[0].shape
    d = h.shape[1]
    tm = 256
    nb = m // tm
    n = len(parts)

    def body(*refs):
        i = pl.program_id(0)
        for p in range(n):
            @pl.when(i // nb == p)
            def _(p=p):
                refs[n + 1][...] = lax.dot_general(refs[p][...], refs[n][...], _TN, preferred_element_type=f32).astype(bf16)

    a_specs = [pl.BlockSpec((s, tm), lambda i, p=p: (0, jnp.clip(i - p * nb, 0, nb - 1))) for p in range(n)]
    return pl.pallas_call(
        body, name=name, grid=(n * nb,), in_specs=a_specs + [pl.BlockSpec((s, d), lambda i: (0, 0))],
        out_specs=pl.BlockSpec((tm, d), lambda i: (i, 0)),
        out_shape=jax.ShapeDtypeStruct((n * m, d), bf16), compiler_params=_params(1),
    )(*parts, h)


def _ada_wgrad(c_all, d_all, name):
    n, d = c_all.shape
    w = d_all.shape[1]

    def body(c_ref, d_ref, o_ref):
        eye = (lax.broadcasted_iota(jnp.int32, (n, n), 0) == lax.broadcasted_iota(jnp.int32, (n, n), 1)).astype(f32)
        ct = lax.dot_general(c_ref[...], eye, _TN, precision=lax.Precision.HIGHEST, preferred_element_type=f32)
        g = ct[:, 0:1] * d_ref[0:1, :]
        for bi in range(1, n):
            g = g + ct[:, bi:bi + 1] * d_ref[bi:bi + 1, :]
        o_ref[0] = g

    return pl.pallas_call(
        body, name=name, out_shape=jax.ShapeDtypeStruct((1, d, w), f32), compiler_params=_params(),
    )(c_all, d_all)


def _adamw(parts, w, m, v, name, mine=None, me=None):
    r, c = w.shape
    n_parts = parts.shape[0]
    row_tiles = [t for t in range(min(r, 256), 0, -1) if r % t == 0 and (t % 16 == 0 or t == r)]
    if row_tiles:
        tr, tc = row_tiles[0], c
    else:
        tr, tc = r, next(t for t in (256, LANES) if c % t == 0)

    def body(*refs):
        w_ref, m_ref, v_ref, g_ref, d_ref, nm_ref, nv_ref = refs[-7:]
        if mine is None:
            p_ref, = refs[:-7]
        else:
            me_ref, p_ref, own_ref = refs[:-7]

        def part(i):
            if mine is None:
                return p_ref[i].astype(f32)
            return jnp.where(me_ref[0] == i, own_ref[...], p_ref[i]).astype(f32)

        g = part(0)
        for i in range(1, n_parts):
            g = g + part(i)
        mm = ADAM_B1 * m_ref[...] + (1.0 - ADAM_B1) * g
        vv = ADAM_B2 * v_ref[...] + (1.0 - ADAM_B2) * (g * g)
        m_hat = mm / (1.0 - ADAM_B1 ** ADAM_STEP)
        v_hat = vv / (1.0 - ADAM_B2 ** ADAM_STEP)
        g_ref[...] = g
        d_ref[...] = -ADAM_LR * (m_hat / (jnp.sqrt(v_hat) + ADAM_EPS) + ADAM_WD * w_ref[...])
        nm_ref[...] = mm
        nv_ref[...] = vv

    out_shape = [jax.ShapeDtypeStruct((r, c), f32)] * 4
    if mine is None:
        spec = pl.BlockSpec((tr, tc), lambda i, j: (i, j))
        return pl.pallas_call(
            body, name=name, grid=(r // tr, c // tc),
            in_specs=[pl.BlockSpec((n_parts, tr, tc), lambda i, j: (0, i, j))] + [spec] * 3,
            out_specs=[spec] * 4, out_shape=out_shape, compiler_params=_params(2),
        )(parts, w, m, v)
    spec = pl.BlockSpec((tr, tc), lambda i, j, me_ref: (i, j))
    return pl.pallas_call(
        body, name=name, out_shape=out_shape, compiler_params=_params(2),
        grid_spec=pltpu.PrefetchScalarGridSpec(
            num_scalar_prefetch=1, grid=(r // tr, c // tc),
            in_specs=[pl.BlockSpec((n_parts, tr, tc), lambda i, j, me_ref: (0, i, j)),
                      pl.BlockSpec((None, tr, tc), lambda i, j, me_ref: (me_ref[0], i, j))] + [spec] * 3,
            out_specs=[spec] * 4),
    )(me, parts, mine, w, m, v)


def _me():
    return lax.axis_index("x"), lax.axis_index("y"), lax.axis_index("c")


def _all_gather(arrays, name, vmem=False, after=None):
    n = len(arrays)
    space = pltpu.VMEM if vmem else pl.ANY
    extra = [] if after is None else [after]

    def body(*refs):
        ins = refs[:n]
        outs = refs[n + len(extra):2 * n + len(extra)]
        send_sems, recv_sems, local_sems = refs[2 * n + len(extra):]
        x, y, c = _me()
        me, sibling = (x, y, c), (x, y, 1 - c)
        chips = [(1 - x, y), (x, 1 - y), (1 - x, 1 - y)]

        def rows(a, dev):
            return outs[a].at[4 * dev[0] + 2 * dev[1] + dev[2]]

        def copy(a, k, block, to, src=None):
            return pltpu.make_async_remote_copy(
                src_ref=rows(a, block) if src is None else src, dst_ref=rows(a, block),
                send_sem=send_sems.at[a, k], recv_sem=recv_sems.at[a, k], device_id=to, device_id_type=MESH)

        mine = [pltpu.make_async_copy(ins[a], rows(a, me), local_sems.at[a]) for a in range(n)]
        for cp in mine:
            cp.start()
        first = []
        for a in range(n):
            first.append(copy(a, 0, me, sibling, src=ins[a]))
            first += [copy(a, 1 + j, me, (*chip, c), src=ins[a]) for j, chip in enumerate(chips)]
        for cp in first:
            cp.start()
        passed = []
        for j, chip in enumerate(chips):
            for a in range(n):
                copy(a, 1 + j, (*chip, c), me).wait_recv()
                fwd = copy(a, 4 + j, (*chip, c), sibling)
                fwd.start()
                passed.append(fwd)
        for a in range(n):
            copy(a, 0, sibling, me).wait_recv()
            for j, chip in enumerate(chips):
                copy(a, 4 + j, (*chip, 1 - c), me).wait_recv()
        for cp in first + passed:
            cp.wait_send()
        for cp in mine:
            cp.wait()

    outs = pl.pallas_call(
        body, name=name,
        in_specs=[pl.BlockSpec(memory_space=space)] * n + [pl.BlockSpec(memory_space=pl.ANY)] * len(extra),
        out_specs=[pl.BlockSpec(memory_space=space)] * n,
        out_shape=[jax.ShapeDtypeStruct((N_DEV,) + a.shape, a.dtype) for a in arrays],
        scratch_shapes=[pltpu.SemaphoreType.DMA((n, 7)), pltpu.SemaphoreType.DMA((n, 7)), pltpu.SemaphoreType.DMA((n,))],
        compiler_params=pltpu.CompilerParams(vmem_limit_bytes=VMEM_LIMIT),
    )(*arrays, *extra)
    return list(outs)


def _gather_prologue(c, w_ada, b_mine, w_in_t, name):
    n_dev, d = N_DEV, c.shape[1]
    ada_w = w_ada.shape[1]

    def body(c_ref, w_ref, b_ref, win_ref, call_ref, ada_ref, gin_ref, cols_ref, send_sems, recv_sems, local_sems):
        x, y, cc = _me()
        me, sibling = (x, y, cc), (x, y, 1 - cc)
        chips = [(1 - x, y), (x, 1 - y), (1 - x, 1 - y)]
        outs = (call_ref, ada_ref, gin_ref)

        def rows(a, dev):
            return outs[a].at[4 * dev[0] + 2 * dev[1] + dev[2]]

        def copy(a, k, block, to, src=None):
            return pltpu.make_async_remote_copy(
                src_ref=rows(a, block) if src is None else src, dst_ref=rows(a, block),
                send_sem=send_sems.at[a, k], recv_sem=recv_sems.at[a, k], device_id=to, device_id_type=MESH)

        def begin(a, src):
            own = pltpu.make_async_copy(src, rows(a, me), local_sems.at[a])
            sends = [copy(a, 0, me, sibling, src=src)] + [copy(a, 1 + j, me, (*chip, cc), src=src) for j, chip in enumerate(chips)]
            for cp in [own] + sends:
                cp.start()
            return own, sends

        def finish(a, own, sends):
            passed = []
            for j, chip in enumerate(chips):
                copy(a, 1 + j, (*chip, cc), me).wait_recv()
                passed.append(copy(a, 4 + j, (*chip, cc), sibling))
                passed[-1].start()
            copy(a, 0, sibling, me).wait_recv()
            for j, chip in enumerate(chips):
                copy(a, 4 + j, (*chip, 1 - cc), me).wait_recv()
            for cp in sends + passed:
                cp.wait_send()
            own.wait()

        finish(0, *begin(0, c_ref))
        cols_ref[...] = (jnp.dot(call_ref[:, 0, :].astype(bf16), w_ref[...].astype(bf16), preferred_element_type=f32)
                         + b_ref[...])
        finish(1, *begin(1, cols_ref))
        finish(2, *begin(2, win_ref))

    vmem, hbm = pl.BlockSpec(memory_space=pltpu.VMEM), pl.BlockSpec(memory_space=pl.ANY)
    return pl.pallas_call(
        body, name=name, in_specs=[vmem, vmem, vmem, hbm], out_specs=[vmem, vmem, hbm],
        out_shape=[jax.ShapeDtypeStruct((n_dev, 1, d), f32), jax.ShapeDtypeStruct((n_dev, n_dev, ada_w), f32),
                   jax.ShapeDtypeStruct((n_dev,) + w_in_t.shape, w_in_t.dtype)],
        scratch_shapes=[pltpu.VMEM((n_dev, ada_w), f32), pltpu.SemaphoreType.DMA((3, 7)), pltpu.SemaphoreType.DMA((3, 7)),
                        pltpu.SemaphoreType.DMA((3,))],
        compiler_params=pltpu.CompilerParams(vmem_limit_bytes=VMEM_LIMIT),
    )(c, w_ada, b_mine, w_in_t)


_FLIPS = ((0, 0, 1), (1, 0, 0), (0, 1, 0), (1, 1, 0), (1, 0, 1), (0, 1, 1), (1, 1, 1))
_HBM = pl.BlockSpec(memory_space=pltpu.HBM)
_SEM = pl.BlockSpec(memory_space=pltpu.SEMAPHORE)


def _exchange_copies(scatter, srcs, lands, send_sems, recv_sems):
    x, y, c = _me()
    me_row = 4 * x + 2 * y + c
    out = []
    for k, (fx, fy, fc) in enumerate(_FLIPS):
        peer = (x ^ fx, y ^ fy, c ^ fc)
        peer_row = 4 * peer[0] + 2 * peer[1] + peer[2]
        for a in range(len(srcs)):
            out.append(pltpu.make_async_remote_copy(
                src_ref=srcs[a].at[peer_row] if scatter else srcs[a], dst_ref=lands[a].at[me_row],
                send_sem=send_sems.at[7 * a + k], recv_sem=recv_sems.at[7 * a + k], device_id=peer, device_id_type=MESH))
    return out


def _exchange_start(arrays, scatter, name, after=None):
    n = len(arrays)
    lands = [lax.empty(a.shape if scatter else (N_DEV,) + a.shape, a.dtype) for a in arrays]
    extra = [] if after is None else [after]

    def body(*refs):
        srcs, zones = refs[:n], refs[n:2 * n]
        send_sems, recv_sems = refs[2 * n + len(extra)], refs[2 * n + len(extra) + 1]
        token = refs[-1]
        for cp in _exchange_copies(scatter, srcs, zones, send_sems, recv_sems):
            cp.start()
        token[...] = jnp.zeros_like(token)

    thru = [pltpu.HBM(a.shape, a.dtype) for a in list(arrays) + lands]
    outs = pl.pallas_call(
        body, name=name,
        out_shape=(pltpu.SemaphoreType.DMA((7 * n,)), pltpu.SemaphoreType.DMA((7 * n,)), *thru, jax.ShapeDtypeStruct((8, LANES), f32)),
        in_specs=[_HBM] * (2 * n) + [pl.BlockSpec(memory_space=pl.ANY)] * len(extra),
        out_specs=(_SEM, _SEM, *[_HBM] * (2 * n), pl.BlockSpec(memory_space=pltpu.VMEM)),
        input_output_aliases={i: 2 + i for i in range(2 * n)},
        compiler_params=pltpu.CompilerParams(has_side_effects=pltpu.SideEffectType.DATAFLOW_SIDE_EFFECTING),
    )(*[pltpu.with_memory_space_constraint(a, pltpu.HBM) for a in list(arrays) + lands], *extra)
    return dict(n=n, scatter=scatter, sems=outs[:2], srcs=outs[2:2 + n], lands=outs[2 + n:2 + 2 * n], token=outs[-1])


def _exchange_wait(handle, after, name):
    n, scatter = handle["n"], handle["scatter"]

    def body(*refs):
        srcs, zones = refs[:n], refs[n:2 * n]
        send_sems, recv_sems = refs[2 * n], refs[2 * n + 1]
        for cp in _exchange_copies(scatter, srcs, zones, send_sems, recv_sems):
            cp.wait_send()
            cp.wait_recv()

    thru = [pltpu.HBM(a.shape, a.dtype) for a in list(handle["srcs"]) + list(handle["lands"])]
    outs = pl.pallas_call(
        body, name=name, out_shape=tuple(thru),
        in_specs=[_HBM] * (2 * n) + [_SEM, _SEM, pl.BlockSpec(memory_space=pl.ANY)], out_specs=tuple([_HBM] * (2 * n)),
        input_output_aliases={i: i for i in range(2 * n)},
        compiler_params=pltpu.CompilerParams(has_side_effects=pltpu.SideEffectType.DATAFLOW_SIDE_EFFECTING),
    )(*handle["srcs"], *handle["lands"], *handle["sems"], after)
    return list(outs[n:])


def _cols_from_shards(g):
    return jnp.transpose(g, (1, 0, 2)).reshape(g.shape[1], -1)


def _shards_from_cols(a):
    return jnp.transpose(a.reshape(a.shape[0], N_DEV, -1), (1, 0, 2))


def _local_step(x, positions, ada, g_pre_mix, g_post_mix, b_f, sinks, g_pre_ffn, g_post_ffn, target,
                w_in_t, mix_weights, ffn_weights, on_grads):
    s, d = x.shape
    row = lambda v: v.reshape(1, -1)
    shift_m, scale_m, gate_m, shift_f, scale_f, gate_f = (ada[i:i + 1] for i in range(6))
    w_gate_t, w_qkv_t = w_in_t[F_OFF + N_HEADS:], w_in_t[:QKV_W]
    w_f_t = jnp.pad(w_in_t[F_OFF:F_OFF + N_HEADS], ((0, LANES - N_HEADS), (0, 0)))
    bf_row = jnp.pad(row(b_f), ((0, 0), (0, LANES - N_HEADS)))
    sink_rows = jnp.broadcast_to(sinks.reshape(N_HEADS, 1).astype(f32), (N_HEADS, LANES))
    inv_freq = 1.0 / (ROPE_THETA ** (jnp.arange(0, HEAD_DIM, 2, dtype=f32) / HEAD_DIM))
    cos, sin_s = _rope_tables(positions.reshape(s, 1), jnp.tile(inv_freq, 4).reshape(1, LANES), "rope_tables")

    h1, qa, ka, va, qb, kb, vb = _prenorm_proj_qkv(x, row(g_pre_mix), scale_m, shift_m, w_qkv_t, cos, sin_s, "prenorm_proj_qkv")
    gl = _matmul(h1, w_gate_t, "nt", bf16, "proj_gate")
    fl, cum_b = _forget_prep(h1, w_f_t, bf_row, "proj_forget_prep")
    o_a, lse_a = _attn_fwd(qa, ka, va, "swa_fwd", sink_rows=sink_rows, window=WINDOW, t=1024)
    o_b, lse_b = _attn_fwd(qb, kb, vb, "fox_fwd", cum_b=cum_b, t=1024)
    everything_before = (gl[:8, :LANES] + o_a[:8, :LANES] + o_b[:8, :LANES]).astype(f32)
    w_branch_a, w_branch_b, w_out = mix_weights(everything_before)
    ba, bb, merged = _branch_merge(o_a, o_b, w_branch_a, w_branch_b, gl, "branch_merge")
    y1, x2, h2 = _out_proj_postnorm_prenorm(merged, w_out, x, row(g_post_mix), gate_m, row(g_pre_ffn), scale_f, shift_f,
                                            "out_proj_norms")

    w_ffn_in_t, w_ffn_out = ffn_weights(h2)
    g_ff, u_ff, act = _ffn_in_swiglu(h2, w_ffn_in_t, "ffn_in_swiglu")
    loss_row, d_out, d_y2, vec_pf = _out_proj_loss_tail(act, w_ffn_out, x2, row(g_post_ffn), gate_f, target, "ffn_out_loss_tail")

    g_w_ffn_out = _matmul(act, d_y2, "tn", bf16, "ffn_out_wgrad")
    dg_ff, du_ff = _ffn_out_dgrad_swiglu(d_y2, w_ffn_out, g_ff, u_ff, "ffn_out_dgrad_swiglu")
    g_w_ffn_in_t = _wgrad_stack([dg_ff, du_ff], h2, "ffn_in_wgrad")
    sent = on_grads(dict(w_ffn_in=g_w_ffn_in_t, w_ffn_out=g_w_ffn_out))
    d_x2, vec_nf, d_y1, vec_pm = _dgrad_prenorm_bwd(
        [(dg_ff, w_ffn_in_t, 0), (du_ff, w_ffn_in_t, 1)], x2, row(g_pre_ffn), scale_f, d_out, "ffn_in_dgrad_norms_bwd",
        after=sent, below=(y1, row(g_post_mix), gate_m))

    g_w_out = _matmul(merged, d_y1, "tn", bf16, "out_proj_wgrad")
    d_ba, d_bb, dgl = _out_dgrad_merge_bwd(d_y1, w_out, ba, bb, gl, "out_proj_dgrad_merge_bwd")
    g_w_branch_a = _matmul(o_a, d_ba, "tn", bf16, "branch_a_wgrad")
    g_w_branch_b = _matmul(o_b, d_bb, "tn", bf16, "branch_b_wgrad")
    sent = on_grads(dict(w_out=g_w_out, w_branch_a=g_w_branch_a, w_branch_b=g_w_branch_b))
    d_oa, delta_a, d_sink = _branch_dgrad_delta(d_ba, w_branch_a, o_a, "branch_a_dgrad_delta", lse=lse_a,
                                                sink_rows=sink_rows, after=sent)
    d_ob, delta_b = _branch_dgrad_delta(d_bb, w_branch_b, o_b, "branch_b_dgrad_delta", after=sent)
    dqa_t, dka, dva = _attn_bwd(qa, ka, va, d_oa, lse_a, delta_a, "swa_bwd", window=WINDOW, t=1024)
    dqb_t, dkb, dvb, dcs, rs = _attn_bwd(qb, kb, vb, d_ob, lse_b, delta_b, "fox_bwd", cum_b=cum_b, t=512)
    dqkv = _qkv_prep_bwd(dqa_t, dka, dva, dqb_t, dkb, dvb, cos, sin_s, "qkv_prep_bwd")
    dfl, vec_bf = _forget_prep_bwd(rs.reshape(N_HEADS, s), dcs, fl, bf_row, "forget_prep_bwd")
    g_w_in_t = jnp.concatenate([_matmul(dqkv, h1, "tn", bf16, "qkv_wgrad"), _matmul(dfl, h1, "tn", bf16, "forget_wgrad")[:N_HEADS],
                                _matmul(dgl, h1, "tn", bf16, "gate_wgrad")], axis=0)
    sent = on_grads(dict(w_in=g_w_in_t))
    grad_x, vec_nm = _dgrad_prenorm_bwd([(dgl, w_gate_t, 0), (dqkv, w_qkv_t, 0), (dfl, w_f_t, 0)], x, row(g_pre_mix),
                                        scale_m, d_x2, "in_proj_dgrad_prenorm_bwd", after=sent)

    d_ada = jnp.concatenate([vec_nm[0], vec_nm[1], vec_pm[0], vec_nf[0], vec_nf[1], vec_pf[0]])
    small = dict(b_ada=d_ada, g_pre_mix=vec_nm[2], g_post_mix=vec_pm[1], g_pre_ffn=vec_nf[2], g_post_ffn=vec_pf[1],
                 b_f=vec_bf[0, :N_HEADS], sinks=d_sink[:, 0], loss=loss_row[0, :1])
    return grad_x, small


_SMALL = (("b_ada", 6144), ("g_pre_mix", 1024), ("g_post_mix", 1024), ("g_pre_ffn", 1024), ("g_post_ffn", 1024),
          ("b_f", 128), ("sinks", 128), ("loss", 128))
_SMALL_ROWS = 88


def _pack_small(vals):
    parts = [jnp.pad(vals[k].reshape(-1).astype(f32), (0, n - vals[k].size)) for k, n in _SMALL]
    flat = jnp.concatenate(parts)
    return jnp.pad(flat, (0, _SMALL_ROWS * LANES - flat.size)).reshape(_SMALL_ROWS, LANES)


def _unpack_small(slab, shapes):
    flat, out, off = slab.reshape(-1), {}, 0
    for k, n in _SMALL:
        size = math.prod(shapes[k])
        out[k] = flat[off:off + size].reshape(shapes[k])
        off += n
    return out


def kernel(x, c, positions, w_ada, b_ada, g_pre_mix, g_post_mix, w_in, b_f, sinks, w_branch_a, w_branch_b, w_out, g_pre_ffn, g_post_ffn, w_ffn_in, w_ffn_out, loss_target, m_w_ada, m_b_ada, m_g_pre_mix, m_g_post_mix, m_w_in, m_b_f, m_sinks, m_w_branch_a, m_w_branch_b, m_w_out, m_g_pre_ffn, m_g_post_ffn, m_w_ffn_in, m_w_ffn_out, v_w_ada, v_b_ada, v_g_pre_mix, v_g_post_mix, v_w_in, v_b_f, v_sinks, v_w_branch_a, v_w_branch_b, v_w_out, v_g_pre_ffn, v_g_post_ffn, v_w_ffn_in, v_w_ffn_out):
    xi, yi, ci = _me()
    me = 4 * xi + 2 * yi + ci
    d = D_MODEL
    ada_w = w_ada.shape[2]

    transposed = ("w_in", "w_ffn_in")
    tr = lambda a: jnp.transpose(a[0])

    b_mine = lax.dynamic_slice(b_ada, (0, me * ada_w), (1, ada_w))
    c_all, ada_all, g_in = _gather_prologue(c, w_ada[0], b_mine, tr(w_in).astype(bf16), "gather_prologue")
    c_all = c_all.reshape(N_DEV, d)
    ada = lax.dynamic_index_in_dim(ada_all, me, axis=1, keepdims=False).reshape(6, d)
    late_mix = [w.astype(bf16) for w in (w_branch_a[0], w_branch_b[0], w_out[0])]
    late_ffn = [w.astype(bf16) for w in (tr(w_ffn_in), w_ffn_out[0])]
    mix_h = _exchange_start(late_mix, False, "gather_mix_start", after=g_in)
    ffn_h = _exchange_start(late_ffn, False, "gather_ffn_start", after=mix_h["token"])

    def mine_into(zone, block):
        return lax.dynamic_update_index_in_dim(zone, block, me, 0)

    def rows_from_shards(g):
        return g.reshape(g.shape[0] * g.shape[1], g.shape[2])

    def mix_weights(after):
        zones = _exchange_wait(mix_h, after, "gather_mix_wait")
        g_ba, g_bb, g_out = (mine_into(z, w) for z, w in zip(zones, late_mix))
        return _cols_from_shards(g_ba), _cols_from_shards(g_bb), rows_from_shards(g_out)

    def ffn_weights(after):
        zones = _exchange_wait(ffn_h, after, "gather_ffn_wait")
        g_fi, g_fo = (mine_into(z, w) for z, w in zip(zones, late_ffn))
        return rows_from_shards(g_fi), rows_from_shards(g_fo)

    row_sharded = ("w_out", "w_ffn_out") + transposed
    in_flight = []

    def on_grads(group):
        sends = [g.reshape(N_DEV, g.shape[0] // N_DEV, g.shape[1]) if nm in row_sharded else _shards_from_cols(g)
                 for nm, g in group.items()]
        handle = _exchange_start(sends, True, "scatter_start_%d" % len(in_flight))
        in_flight.append((list(group), sends, handle))
        return handle["token"]

    grad_x, small = _local_step(
        x[0], positions[0], ada + ffn_h["token"][0, 0], g_pre_mix[0], g_post_mix[0], b_f[0], sinks[0], g_pre_ffn[0],
        g_post_ffn[0], loss_target[0], rows_from_shards(g_in), mix_weights, ffn_weights, on_grads)

    ws = dict(w_in=(w_in, m_w_in, v_w_in), w_branch_a=(w_branch_a, m_w_branch_a, v_w_branch_a),
              w_branch_b=(w_branch_b, m_w_branch_b, v_w_branch_b), w_out=(w_out, m_w_out, v_w_out),
              w_ffn_in=(w_ffn_in, m_w_ffn_in, v_w_ffn_in), w_ffn_out=(w_ffn_out, m_w_ffn_out, v_w_ffn_out))
    res = {}

    def finish_group(gi, after):
        names, sends, handle = in_flight[gi]
        zones = _exchange_wait(handle, after, "scatter_wait_%d" % gi)
        for nm, zone, sent in zip(names, zones, sends):
            w, m, v = (tr(a) if nm in transposed else a[0] for a in ws[nm])
            out = _adamw(zone, w, m, v, "adamw_" + nm, mine=sent, me=me.reshape(1).astype(jnp.int32))
            after = out[0]
            res[nm] = [jnp.transpose(o) for o in out] if nm in transposed else out
        return after

    done = finish_group(1, finish_group(0, grad_x))

    slab_all, = _all_gather([_pack_small(small)], "gather_small", vmem=True, after=done)
    small_w = dict(b_ada=b_ada, g_pre_mix=g_pre_mix, g_post_mix=g_post_mix, g_pre_ffn=g_pre_ffn, g_post_ffn=g_post_ffn,
                   b_f=b_f, sinks=sinks, loss=jnp.zeros((1,), f32))
    small_m = dict(b_ada=m_b_ada, g_pre_mix=m_g_pre_mix, g_post_mix=m_g_post_mix, g_pre_ffn=m_g_pre_ffn,
                   g_post_ffn=m_g_post_ffn, b_f=m_b_f, sinks=m_sinks, loss=jnp.zeros((1,), f32))
    small_v = dict(b_ada=v_b_ada, g_pre_mix=v_g_pre_mix, g_post_mix=v_g_post_mix, g_pre_ffn=v_g_pre_ffn,
                   g_post_ffn=v_g_post_ffn, b_f=v_b_f, sinks=v_sinks, loss=jnp.ones((1,), f32))
    shapes = {k: small_w[k].shape for k, _ in _SMALL}
    s_out = _adamw(slab_all, _pack_small(small_w), _pack_small(small_m), _pack_small(small_v), "adamw_small")
    s_grad, s_delta, s_m, s_v = (_unpack_small(o, shapes) for o in s_out)

    d_ada_all = lax.dynamic_slice(slab_all[:, :6144 // LANES, :].reshape(N_DEV, 6144), (0, me * ada_w), (N_DEV, ada_w))
    ada_parts = _ada_wgrad(c_all, d_ada_all, "ada_wgrad")

    res["w_ada"] = _adamw(ada_parts, w_ada[0], m_w_ada[0], v_w_ada[0], "adamw_w_ada")
    finish_group(2, res["w_ada"][0])

    order = ["w_ada", "b_ada", "g_pre_mix", "g_post_mix", "w_in", "b_f", "sinks", "w_branch_a", "w_branch_b", "w_out",
             "g_pre_ffn", "g_post_ffn", "w_ffn_in", "w_ffn_out"]
    outs = [s_grad["loss"].reshape(()), grad_x[None]]
    for which, small_o in enumerate((s_grad, s_delta, s_m, s_v)):
        for nm in order:
            outs.append(res[nm][which][None] if nm in res else small_o[nm])
    return tuple(outs)
```

```python
import math

import jax
import jax.numpy as jnp
from jax import lax
from jax.experimental import pallas as pl
from jax.experimental.pallas import tpu as pltpu

f32 = jnp.float32
bf16 = jnp.bfloat16

D_MODEL = 1024
HEAD_DIM = 64
N_HEADS = 8
N_PAIRS = 4
QKV_W = 2304
F_OFF = 2304
WINDOW = 128
ROPE_THETA = 10000.0
RMS_EPS = 1e-6
N_DEV = 8
ADAM_LR, ADAM_B1, ADAM_B2, ADAM_EPS, ADAM_WD, ADAM_STEP = 0.001, 0.9, 0.999, 1e-08, 0.01, 10
NEG = -1e30
L_ROW = (HEAD_DIM, 0)
LANES = 128
VMEM_LIMIT = 48 * 1024 * 1024
MESH = pl.DeviceIdType.MESH

_NT = (((1,), (1,)), ((), ()))
_TN = (((0,), (0,)), ((), ()))


def _params(n_grid=0):
    sem = ("arbitrary",) * n_grid if n_grid else None
    return pltpu.CompilerParams(dimension_semantics=sem, vmem_limit_bytes=VMEM_LIMIT)


def _row_tile(s, want):
    t = min(s, want)
    assert s % t == 0, (s, t)
    return t


MATMUL_VMEM_BUDGET = 40 * 1024 * 1024


def _matmul_tiles(m, n, k, a_item, b_item, o_item):
    def tiles(d):
        return [t for t in range(LANES, min(d, 2048) + 1, LANES) if d % t == 0] or [d]

    best = None
    for tm in tiles(m):
        for tn in tiles(n):
            vmem = 2 * (tm * k * a_item + tn * k * b_item + tm * tn * o_item) + tm * tn * 4
            if vmem > MATMUL_VMEM_BUDGET:
                continue
            traffic = m * k * a_item + n * k * b_item * (1 if tn == n else m // tm) + m * n * o_item
            steps = (m // tm) * (n // tn)
            key = (traffic, 0, steps) if steps >= 4 else (traffic, 1, -steps)
            if best is None or key < best[0]:
                best = (key, tm, tn)
    assert best is not None, (m, n, k)
    return best[1], best[2]


def _matmul(a, b, mode, out_dtype, name, after=None):
    if mode == "nn":
        (m, k), n = a.shape, b.shape[1]
    elif mode == "nt":
        (m, k), n = a.shape, b.shape[0]
    else:
        (k, m), n = a.shape, b.shape[1]
    tm, tn = _matmul_tiles(m, n, k, a.dtype.itemsize, b.dtype.itemsize, jnp.dtype(out_dtype).itemsize)
    if mode == "nn":
        a_spec, b_spec, dims = pl.BlockSpec((tm, k), lambda i, j: (i, 0)), pl.BlockSpec((k, tn), lambda i, j: (0, j)), None
    elif mode == "nt":
        a_spec, b_spec, dims = pl.BlockSpec((tm, k), lambda i, j: (i, 0)), pl.BlockSpec((tn, k), lambda i, j: (j, 0)), _NT
    else:
        a_spec, b_spec, dims = pl.BlockSpec((k, tm), lambda i, j: (0, i)), pl.BlockSpec((k, tn), lambda i, j: (0, j)), _TN

    def body(a_ref, b_ref, *rest):
        o_ref = rest[-1]
        av, bv = a_ref[...].astype(bf16), b_ref[...].astype(bf16)
        if dims is None:
            r = jnp.dot(av, bv, preferred_element_type=f32)
        else:
            r = lax.dot_general(av, bv, dims, preferred_element_type=f32)
        o_ref[...] = r.astype(out_dtype)

    extra = [] if after is None else [after]
    return pl.pallas_call(
        body, name=name, grid=(m // tm, n // tn), in_specs=[a_spec, b_spec] + [pl.BlockSpec(memory_space=pl.ANY)] * len(extra),
        out_specs=pl.BlockSpec((tm, tn), lambda i, j: (i, j)),
        out_shape=jax.ShapeDtypeStruct((m, n), out_dtype), compiler_params=_params(2),
    )(a, b, *extra)


def _rstd(v):
    return lax.rsqrt(jnp.mean(v * v, axis=-1, keepdims=True) + RMS_EPS)


def _row_spec(tm, d):
    return pl.BlockSpec((tm, d), lambda i: (i, 0))


def _vec_spec(d, rows=1):
    return pl.BlockSpec((rows, d), lambda i: (0, 0))


def _proj_spec(a, w, tm):
    return [_row_spec(tm, a.shape[1]), pl.BlockSpec(w.shape, lambda i: (0, 0))]


def _out_proj_postnorm_prenorm(a, w, x, g_post, gate, g_pre, scale, shift, name):
    s, d = x.shape
    tm = _row_tile(s, 512)

    def body(a_ref, w_ref, x_ref, gp_ref, gate_ref, g_ref, sc_ref, sh_ref, y_ref, x2_ref, h_ref):
        yv = jnp.dot(a_ref[...], w_ref[...], preferred_element_type=f32)
        y_ref[...] = yv
        x2 = x_ref[...] + gate_ref[...] * (yv * _rstd(yv) * gp_ref[...])
        x2_ref[...] = x2
        h_ref[...] = ((x2 * _rstd(x2) * g_ref[...]) * (1.0 + sc_ref[...]) + sh_ref[...]).astype(bf16)

    return pl.pallas_call(
        body, name=name, grid=(s // tm,), in_specs=_proj_spec(a, w, tm) + [_row_spec(tm, d)] + [_vec_spec(d)] * 5,
        out_specs=[_row_spec(tm, d)] * 3,
        out_shape=[jax.ShapeDtypeStruct((s, d), f32)] * 2 + [jax.ShapeDtypeStruct((s, d), bf16)], compiler_params=_params(1),
    )(a, w, x, g_post, gate, g_pre, scale, shift)


def _rms_bwd(u, v, r):
    return r * u - v * (r * r * r) * jnp.mean(u * v, axis=-1, keepdims=True)


def _out_proj_loss_tail(a, w, x, g, gate, target, name):
    s, d = x.shape
    tm = _row_tile(s, 512)

    def body(a_ref, w_ref, x_ref, g_ref, gate_ref, t_ref, loss_ref, do_ref, dy_ref, vec_ref):
        @pl.when(pl.program_id(0) == 0)
        def _():
            loss_ref[...] = jnp.zeros_like(loss_ref)
            vec_ref[...] = jnp.zeros_like(vec_ref)
        yv = jnp.dot(a_ref[...], w_ref[...], preferred_element_type=f32)
        r = _rstd(yv)
        yn = yv * r
        err = x_ref[...] + gate_ref[...] * (yn * g_ref[...]) - t_ref[...]
        loss_ref[...] += 0.5 * jnp.sum(jnp.mean(err * err, axis=-1, keepdims=True), axis=0, keepdims=True)
        dr = err / d
        do_ref[...] = dr
        dn = dr * gate_ref[...]
        vec_ref[0:1, :] += jnp.sum(dr * (yn * g_ref[...]), axis=0, keepdims=True)
        vec_ref[1:2, :] += jnp.sum(dn * yn, axis=0, keepdims=True)
        dy_ref[...] = _rms_bwd(dn * g_ref[...], yv, r).astype(bf16)

    return pl.pallas_call(
        body, name=name, grid=(s // tm,),
        in_specs=_proj_spec(a, w, tm) + [_row_spec(tm, d)] + [_vec_spec(d)] * 2 + [_row_spec(tm, d)],
        out_specs=[_vec_spec(LANES), _row_spec(tm, d), _row_spec(tm, d), _vec_spec(d, 8)],
        out_shape=[jax.ShapeDtypeStruct((1, LANES), f32), jax.ShapeDtypeStruct((s, d), f32),
                   jax.ShapeDtypeStruct((s, d), bf16), jax.ShapeDtypeStruct((8, d), f32)],
        compiler_params=_params(1),
    )(a, w, x, g, gate, target)


def _dgrad_prenorm_bwd(terms, x, g, scale, dres, name, after=None, below=None):
    s, d = x.shape
    n = len(terms)
    k = sum(a.shape[1] for a, _, _ in terms)
    row_bytes = 2 * (2 * k) + d * (4 + 2 * 4 * 3 + (2 * 4 + 2 * 2 if below else 0))
    tm = next(t for t in (512, 256, 128) if s % t == 0 and 4 * k * d + t * row_bytes <= MATMUL_VMEM_BUDGET)
    extra = [] if after is None else [after]

    def body(*refs):
        a_refs, b_refs = refs[:n], refs[n:2 * n]
        x_ref, g_ref, sc_ref, dr_ref = refs[2 * n:2 * n + 4]
        n_in = 2 * n + 4 + (3 if below else 0) + len(extra)
        dx_ref, vec_ref = refs[n_in], refs[n_in + 1]
        if below:
            y_ref, gp_ref, gate_ref = refs[2 * n + 4:2 * n + 7]
            dy_ref, vec2_ref = refs[n_in + 2], refs[n_in + 3]

        @pl.when(pl.program_id(0) == 0)
        def _():
            vec_ref[...] = jnp.zeros_like(vec_ref)
            if below:
                vec2_ref[...] = jnp.zeros_like(vec2_ref)
        dhv = jnp.dot(a_refs[0][...], b_refs[0][...], preferred_element_type=f32)
        for i in range(1, n):
            dhv = dhv + jnp.dot(a_refs[i][...], b_refs[i][...], preferred_element_type=f32)
        xv = x_ref[...]
        r = _rstd(xv)
        xn = xv * r
        dn = dhv * (1.0 + sc_ref[...])
        vec_ref[0:1, :] += jnp.sum(dhv, axis=0, keepdims=True)
        vec_ref[1:2, :] += jnp.sum(dhv * (xn * g_ref[...]), axis=0, keepdims=True)
        vec_ref[2:3, :] += jnp.sum(dn * xn, axis=0, keepdims=True)
        dx = dr_ref[...] + _rms_bwd(dn * g_ref[...], xv, r)
        dx_ref[...] = dx
        if below:
            yv = y_ref[...]
            ry = _rstd(yv)
            yn = yv * ry
            dny = dx * gate_ref[...]
            vec2_ref[0:1, :] += jnp.sum(dx * (yn * gp_ref[...]), axis=0, keepdims=True)
            vec2_ref[1:2, :] += jnp.sum(dny * yn, axis=0, keepdims=True)
            dy_ref[...] = _rms_bwd(dny * gp_ref[...], yv, ry).astype(bf16)

    in_specs = ([_row_spec(tm, a.shape[1]) for a, _, _ in terms]
                + [pl.BlockSpec((a.shape[1], d), lambda i, r=r: (r, 0)) for a, _, r in terms]
                + [_row_spec(tm, d)] + [_vec_spec(d)] * 2 + [_row_spec(tm, d)])
    out_specs = [_row_spec(tm, d), _vec_spec(d, 8)]
    out_shape = [jax.ShapeDtypeStruct((s, d), f32), jax.ShapeDtypeStruct((8, d), f32)]
    args = [a for a, _, _ in terms] + [b for _, b, _ in terms] + [x, g, scale, dres]
    if below:
        in_specs += [_row_spec(tm, d)] + [_vec_spec(d)] * 2
        out_specs += [_row_spec(tm, d), _vec_spec(d, 8)]
        out_shape += [jax.ShapeDtypeStruct((s, d), bf16), jax.ShapeDtypeStruct((8, d), f32)]
        args += list(below)
    return pl.pallas_call(
        body, name=name, grid=(s // tm,), in_specs=in_specs + [pl.BlockSpec(memory_space=pl.ANY)] * len(extra),
        out_specs=out_specs, out_shape=out_shape, compiler_params=_params(1),
    )(*args, *extra)


def _lane():
    return lax.broadcasted_iota(jnp.int32, (1, LANES), 1)


def _rope_tables(pos_col, inv_freq, name):
    s = pos_col.shape[0]

    def body(p_ref, f_ref, cos_ref, sin_ref):
        ang = p_ref[...].astype(f32) * f_ref[...]
        first_half = (_lane() % HEAD_DIM) < HEAD_DIM // 2
        cos_ref[...] = jnp.cos(ang)
        sn = jnp.sin(ang)
        sin_ref[...] = jnp.where(first_half, -sn, sn)

    return pl.pallas_call(
        body, name=name, out_shape=[jax.ShapeDtypeStruct((s, LANES), f32)] * 2, compiler_params=_params(),
    )(pos_col, inv_freq)


def _swap_halves(v):
    first_half = (_lane() % HEAD_DIM) < HEAD_DIM // 2
    return jnp.where(first_half, pltpu.roll(v, LANES - HEAD_DIM // 2, axis=1), pltpu.roll(v, HEAD_DIM // 2, axis=1))


def _prenorm_proj_qkv(x, g, mod_scale, mod_shift, w_qkv_t, cos, sin_s, name):
    s, d = x.shape
    tm = _row_tile(s, 512)
    scale = 1.0 / math.sqrt(HEAD_DIM)

    def body(x_ref, g_ref, msc_ref, msh_ref, w_ref, c_ref, s_ref, h_ref, qa_ref, ka_ref, va_ref, qb_ref, kb_ref, vb_ref):
        xv = x_ref[...]
        h = ((xv * _rstd(xv) * g_ref[...]) * (1.0 + msc_ref[...]) + msh_ref[...]).astype(bf16)
        h_ref[...] = h
        proj = lax.dot_general(h, w_ref[...], _NT, preferred_element_type=f32)
        cs, sn = c_ref[...], s_ref[...]
        low = _lane() < HEAD_DIM

        def blk(j):
            return proj[:, j * LANES:(j + 1) * LANES]

        def rope(v):
            return v * cs + _swap_halves(v) * sn

        def expand(v):
            other = pltpu.roll(v, HEAD_DIM, axis=1)
            return jnp.where(low, v, other), jnp.where(low, other, v)

        for j in range(N_PAIRS):
            qa_ref[:, j * LANES:(j + 1) * LANES] = (rope(blk(j)) * scale).astype(bf16)
            qb_ref[:, j * LANES:(j + 1) * LANES] = (blk(6 + j) * scale).astype(bf16)
            kb_ref[:, j * LANES:(j + 1) * LANES] = blk(10 + j).astype(bf16)
            vb_ref[:, j * LANES:(j + 1) * LANES] = blk(14 + j).astype(bf16)
        k0, k1 = expand(rope(blk(4)))
        v0, v1 = expand(blk(5))
        for j in range(N_PAIRS):
            ka_ref[:, j * LANES:(j + 1) * LANES] = (k0 if j < 2 else k1).astype(bf16)
            va_ref[:, j * LANES:(j + 1) * LANES] = (v0 if j < 2 else v1).astype(bf16)

    hw = N_PAIRS * LANES
    return pl.pallas_call(
        body, name=name, grid=(s // tm,),
        in_specs=[_row_spec(tm, d)] + [_vec_spec(d)] * 3
        + [pl.BlockSpec((QKV_W, d), lambda i: (0, 0)), _row_spec(tm, LANES), _row_spec(tm, LANES)],
        out_specs=[_row_spec(tm, d)] + [_row_spec(tm, hw)] * 6,
        out_shape=[jax.ShapeDtypeStruct((s, d), bf16)] + [jax.ShapeDtypeStruct((s, hw), bf16)] * 6, compiler_params=_params(1),
    )(x, g, mod_scale, mod_shift, w_qkv_t, cos, sin_s)


def _qkv_prep_bwd(dqa_t, dka, dva, dqb_t, dkb, dvb, cos, sin_s, name):
    s = dka.shape[0]
    tm = _row_tile(s, 256)
    scale = 1.0 / math.sqrt(HEAD_DIM)
    hw = N_PAIRS * LANES
    t_spec = pl.BlockSpec((hw, tm), lambda i: (0, i))

    def body(dqa_ref, dka_ref, dva_ref, dqb_ref, dkb_ref, dvb_ref, c_ref, s_ref, o_ref):
        cs, sn = c_ref[...], s_ref[...]
        low = _lane() < HEAD_DIM

        def blk(ref, j):
            return ref[:, j * LANES:(j + 1) * LANES].astype(f32)

        def blk_t(ref, j):
            return ref[j * LANES:(j + 1) * LANES, :].T

        def unrope(v):
            return v * cs + _swap_halves(v * sn)

        def fold(ref):
            a, b = blk(ref, 0) + blk(ref, 1), blk(ref, 2) + blk(ref, 3)
            kv0 = a + pltpu.roll(a, HEAD_DIM, axis=1)
            kv1 = b + pltpu.roll(b, HEAD_DIM, axis=1)
            return jnp.where(low, kv0, kv1)

        for j in range(N_PAIRS):
            o_ref[:, j * LANES:(j + 1) * LANES] = (unrope(blk_t(dqa_ref, j)) * scale).astype(bf16)
            o_ref[:, (6 + j) * LANES:(7 + j) * LANES] = (blk_t(dqb_ref, j) * scale).astype(bf16)
            o_ref[:, (10 + j) * LANES:(11 + j) * LANES] = blk(dkb_ref, j).astype(bf16)
            o_ref[:, (14 + j) * LANES:(15 + j) * LANES] = blk(dvb_ref, j).astype(bf16)
        o_ref[:, 4 * LANES:5 * LANES] = unrope(fold(dka_ref)).astype(bf16)
        o_ref[:, 5 * LANES:6 * LANES] = fold(dva_ref).astype(bf16)

    return pl.pallas_call(
        body, name=name, grid=(s // tm,),
        in_specs=[t_spec, _row_spec(tm, hw), _row_spec(tm, hw), t_spec, _row_spec(tm, hw), _row_spec(tm, hw)] + [_row_spec(tm, LANES)] * 2,
        out_specs=_row_spec(tm, QKV_W), out_shape=jax.ShapeDtypeStruct((s, QKV_W), bf16), compiler_params=_params(1),
    )(dqa_t, dka, dva, dqb_t, dkb, dvb, cos, sin_s)


def _cumsum_rows(v, reverse=False):
    n = v.shape[0]
    row = lax.broadcasted_iota(jnp.int32, v.shape, 0)
    sh = 1
    while sh < n:
        if reverse:
            v = v + jnp.where(row < n - sh, pltpu.roll(v, n - sh, axis=0), 0.0)
        else:
            v = v + jnp.where(row >= sh, pltpu.roll(v, sh, axis=0), 0.0)
        sh *= 2
    return v


def _log_sigmoid(z):
    return jnp.minimum(z, 0.0) - jnp.log1p(jnp.exp(-jnp.abs(z)))


def _forget_prep(h, w_f_t, bf_row, name):
    s = h.shape[0]

    def body(h_ref, w_ref, b_ref, f_ref, cb_ref):
        fl = lax.dot_general(h_ref[...], w_ref[...], _NT, preferred_element_type=f32)
        f_ref[...] = fl
        cum = _cumsum_rows(_log_sigmoid(fl + b_ref[...]))
        for hd in range(N_HEADS):
            cb_ref[:, hd * LANES:(hd + 1) * LANES] = jnp.broadcast_to(cum[:, hd:hd + 1], (s, LANES))

    return pl.pallas_call(
        body, name=name,
        out_shape=[jax.ShapeDtypeStruct((s, LANES), f32), jax.ShapeDtypeStruct((s, N_HEADS * LANES), f32)],
        compiler_params=_params(),
    )(h, w_f_t, bf_row)


def _forget_prep_bwd(rs, dcs, fl, bf_row, name):
    s = fl.shape[0]

    def body(r_ref, c_ref, f_ref, b_ref, df_ref, db_ref):
        eye = (lax.broadcasted_iota(jnp.int32, (N_HEADS, LANES), 0) == lax.broadcasted_iota(jnp.int32, (N_HEADS, LANES), 1)).astype(f32)
        dcum = lax.dot_general(r_ref[...], eye, _TN, precision=lax.Precision.HIGHEST, preferred_element_type=f32)
        for h in range(N_HEADS):
            dcum = dcum - jnp.where(_lane() == h, jnp.sum(c_ref[:, h * LANES:(h + 1) * LANES], axis=1, keepdims=True), 0.0)
        dlf = _cumsum_rows(dcum, reverse=True)
        z = f_ref[...] + b_ref[...]
        df = jnp.where(_lane() < N_HEADS, dlf * jax.nn.sigmoid(-z), 0.0)
        df_ref[...] = df.astype(bf16)
        db_ref[...] = jnp.zeros_like(db_ref)
        db_ref[0:1, :] = jnp.sum(df, axis=0, keepdims=True)

    return pl.pallas_call(
        body, name=name,
        out_shape=[jax.ShapeDtypeStruct((s, LANES), bf16), jax.ShapeDtypeStruct((8, LANES), f32)], compiler_params=_params(),
    )(rs, dcs, fl, bf_row)


def _tile_mask(n_keys, n_queries, off, window):
    shape = (n_keys, n_queries)
    d = lax.broadcasted_iota(jnp.int32, shape, 1) - lax.broadcasted_iota(jnp.int32, shape, 0) + off
    valid = d >= 0
    return jnp.logical_and(valid, d < window) if window else valid


def _wide(v, t):
    return jnp.concatenate([v] * (t // LANES), axis=1)


def _attn_fwd(q, k, v, name, *, cum_b=None, sink_rows=None, window=None, t=256):
    s = q.shape[0]
    t = _row_tile(s, t)
    fox, has_sink = cum_b is not None, sink_rows is not None
    assert not window or (window % LANES == 0 and LANES + window <= s)

    def body(*refs):
        q_ref, k_ref, v_ref = refs[:3]
        rest = list(refs[3:])
        cb_ref = rest.pop(0) if fox else None
        sink_ref = rest.pop(0) if has_sink else None
        o_ref, lse_ref = rest
        i = pl.program_id(1)
        low = _lane() < HEAD_DIM
        top = lax.broadcasted_iota(jnp.int32, (LANES, 1), 0) < HEAD_DIM
        q2 = q_ref[...]
        zero = jnp.zeros_like(q2)
        qms = (jnp.where(low, q2, zero), jnp.where(low, zero, q2))

        def tile(k0, n_keys, off, carry, masked, queries=slice(0, t)):
            nq = queries.stop - queries.start
            kblk, vblk = k_ref[pl.ds(k0, n_keys), :], v_ref[pl.ds(k0, n_keys), :]
            valid = _tile_mask(n_keys, nq, off, window) if masked else None
            ones = jnp.ones_like(vblk)
            vs = tuple(jnp.where(_lane() == L_ROW[h], ones, vblk) for h in range(2))

            def scores(h):
                return lax.dot_general(kblk, qms[h][queries], _NT, preferred_element_type=f32)

            def softmax(h, sc):
                m = carry[h][0]
                if fox:
                    sc = sc - _wide(cb_ref[pl.ds(k0, n_keys), h * LANES:(h + 1) * LANES], nq)
                if masked:
                    sc = jnp.where(valid, sc, NEG)
                m_new = jnp.maximum(m, jnp.max(sc, axis=0, keepdims=True))
                return m_new, jnp.exp(m - m_new), jnp.exp(sc - m_new).astype(bf16)

            def update(h, m_new, alpha, p):
                return m_new, alpha * carry[h][1] + lax.dot_general(vs[h], p, _TN, preferred_element_type=f32)

            if window:
                return tuple(update(h, *softmax(h, scores(h))) for h in range(2))
            scs = [scores(h) for h in range(2)]
            stats = [softmax(h, scs[h]) for h in range(2)]
            return tuple(update(h, *stats[h]) for h in range(2))

        def start(nq):
            if has_sink:
                row = lax.broadcasted_iota(jnp.int32, (LANES, nq), 0)
                return tuple((_wide(sink_ref[h:h + 1, :], nq), (row == L_ROW[h]).astype(f32)) for h in range(2))
            return tuple((jnp.full((1, nq), NEG, f32), jnp.zeros((LANES, nq), f32)) for h in range(2))

        def finish(carry, queries):
            (m0, a0), (m1, a1) = carry
            l0, l1 = a0[L_ROW[0]:L_ROW[0] + 1, :], a1[L_ROW[1]:L_ROW[1] + 1, :]
            o_t = jnp.where(top, a0 * (1.0 / l0), a1 * (1.0 / l1))
            o_ref[queries, :] = o_t.T.astype(bf16)
            lse_ref[0:1, queries] = m0 + jnp.log(l0)
            lse_ref[1:2, queries] = m1 + jnp.log(l1)

        if window:
            for c in range(t // LANES):
                queries = slice(c * LANES, (c + 1) * LANES)
                q0 = i * t + c * LANES
                k0 = pl.multiple_of(jnp.maximum(q0 - window, 0), LANES)
                finish(tile(k0, LANES + window, q0 - k0, start(LANES), True, queries), queries)
        else:
            carry = lax.fori_loop(0, i, lambda kb, c: tile(pl.multiple_of(kb * t, t), t, 0, c, False), start(t))
            finish(tile(pl.multiple_of(i * t, t), t, 0, carry, True), slice(0, t))

    q_spec = pl.BlockSpec((t, LANES), lambda j, i: (i, j))
    kv_spec = pl.BlockSpec((s, LANES), lambda j, i: (0, j))
    in_specs, args = [q_spec, kv_spec, kv_spec], [q, k, v]
    if fox:
        in_specs += [pl.BlockSpec((s, 2 * LANES), lambda j, i: (0, j))]
        args += [cum_b]
    if has_sink:
        in_specs += [pl.BlockSpec((None, 2, LANES), lambda j, i: (j, 0, 0))]
        args += [sink_rows.reshape(N_PAIRS, 2, LANES)]
    return pl.pallas_call(
        body, name=name, grid=(N_PAIRS, s // t), in_specs=in_specs,
        out_specs=[q_spec, pl.BlockSpec((None, 2, t), lambda j, i: (j, 0, i))],
        out_shape=[jax.ShapeDtypeStruct((s, N_PAIRS * LANES), bf16), jax.ShapeDtypeStruct((N_PAIRS, 2, s), f32)],
        compiler_params=_params(2),
    )(*args)


def _branch_dgrad_delta(db, w, o, name, *, lse=None, sink_rows=None, after=None):
    s, hw = o.shape
    tm = _row_tile(s, 512)
    has_sink = sink_rows is not None
    extra = [] if after is None else [after]

    def body(*refs):
        db_ref, w_ref, o_ref = refs[:3]
        outs = refs[3 + (2 if has_sink else 0) + len(extra):]
        do_ref, dl_ref = outs[:2]
        if has_sink:
            lse_ref, sink_ref = refs[3:5]
            ds_ref = outs[2]

            @pl.when(pl.program_id(0) == 0)
            def _():
                ds_ref[...] = jnp.zeros_like(ds_ref)
        do = lax.dot_general(db_ref[...], w_ref[...], _NT, preferred_element_type=f32).astype(bf16)
        do_ref[...] = do
        for j in range(N_PAIRS):
            cols = slice(j * LANES, (j + 1) * LANES)
            prod_t = (do[:, cols].astype(f32) * o_ref[:, cols].astype(f32)).T
            for h in range(2):
                dl = jnp.sum(prod_t[h * HEAD_DIM:(h + 1) * HEAD_DIM, :], axis=0, keepdims=True)
                dl_ref[j, h:h + 1, :] = dl
                if has_sink:
                    r = 2 * j + h
                    p_sink = jnp.exp(sink_ref[r:r + 1, 0:1] - lse_ref[j, h:h + 1, :])
                    ds_ref[r:r + 1, :] += -jnp.sum(p_sink * dl, axis=1, keepdims=True)

    rows_spec = pl.BlockSpec((N_PAIRS, 2, tm), lambda i: (0, 0, i))
    in_specs = [_row_spec(tm, db.shape[1]), pl.BlockSpec(w.shape, lambda i: (0, 0)), _row_spec(tm, hw)]
    args = [db, w, o]
    out_specs = [_row_spec(tm, hw), rows_spec]
    out_shape = [jax.ShapeDtypeStruct((s, hw), bf16), jax.ShapeDtypeStruct((N_PAIRS, 2, s), f32)]
    if has_sink:
        in_specs += [rows_spec, _vec_spec(LANES, N_HEADS)]
        args += [lse, sink_rows]
        out_specs += [_vec_spec(LANES, N_HEADS)]
        out_shape += [jax.ShapeDtypeStruct((N_HEADS, LANES), f32)]
    return pl.pallas_call(
        body, name=name, grid=(s // tm,), in_specs=in_specs + [pl.BlockSpec(memory_space=pl.ANY)] * len(extra),
        out_specs=out_specs, out_shape=out_shape, compiler_params=_params(1),
    )(*args, *extra)


def _attn_bwd(q, k, v, do, lse, delta, name, *, cum_b=None, window=None, t=256):
    s = q.shape[0]
    t = _row_tile(s, t)
    nblk = s // t
    fox = cum_b is not None
    assert not window or (window % LANES == 0 and LANES + window <= s)

    def body(*refs):
        k_ref, v_ref, q_ref, do_ref, lse_ref, dl_ref = refs[:6]
        rest = list(refs[6:])
        cb_ref = rest.pop(0) if fox else None
        dq_ref, dk_ref, dv_ref = rest[:3]
        dcs_ref, rs_ref = (rest[3], rest[4]) if fox else (None, None)
        dk_acc, dv_acc = rest[-2:]
        b = pl.program_id(1)
        k0 = pl.multiple_of(b * t, t)

        @pl.when(b == 0)
        def _():
            dq_ref[...] = jnp.zeros_like(dq_ref)
            if fox:
                rs_ref[...] = jnp.zeros_like(rs_ref)

        dk_acc[...] = jnp.zeros_like(dk_acc)
        dv_acc[...] = jnp.zeros_like(dv_acc)
        if fox:
            dcs_ref[...] = jnp.zeros_like(dcs_ref)
        low = _lane() < HEAD_DIM
        top = lax.broadcasted_iota(jnp.int32, (LANES, 1), 0) < HEAD_DIM
        kblk, vblk = k_ref[...], v_ref[...]
        k_t = kblk.astype(f32).T.astype(bf16)
        cks = [_wide(cb_ref[pl.ds(k0, t), h * LANES:(h + 1) * LANES], t) for h in range(2)] if fox else None

        def tile(q0, n_queries, off, masked, keys=slice(0, t)):
            cols = pl.ds(q0, n_queries)
            q2, do2 = q_ref[cols, :], do_ref[cols, :]
            zero = jnp.zeros_like(q2)
            valid = _tile_mask(keys.stop - keys.start, n_queries, off, window) if masked else None
            dq_parts = []
            for h in range(2):
                qm = jnp.where(low, q2, zero) if h == 0 else jnp.where(low, zero, q2)
                dom = jnp.where(low, do2, zero) if h == 0 else jnp.where(low, zero, do2)
                sc = lax.dot_general(kblk[keys], qm, _NT, preferred_element_type=f32)
                if fox:
                    sc = sc - cks[h]
                if masked:
                    sc = jnp.where(valid, sc, NEG)
                p = jnp.exp(sc - lse_ref[h:h + 1, cols])
                dp = lax.dot_general(vblk[keys], dom, _NT, preferred_element_type=f32)
                ds = p * (dp - dl_ref[h:h + 1, cols])
                pb, dsb = p.astype(bf16), ds.astype(bf16)
                dv_acc[keys, :] += jnp.dot(pb, dom, preferred_element_type=f32)
                dk_acc[keys, :] += jnp.dot(dsb, qm, preferred_element_type=f32)
                dq_parts.append(jnp.dot(k_t[:, keys], dsb, preferred_element_type=f32))
                if fox:
                    dcs_ref[:, h * LANES:(h + 1) * LANES] += sum(ds[:, g * LANES:(g + 1) * LANES] for g in range(t // LANES))
                    rs_ref[h:h + 1, cols] += jnp.sum(ds, axis=0, keepdims=True)
            dq_ref[:, cols] += jnp.where(top, dq_parts[0], dq_parts[1])

        def later_block(qb, carry):
            tile(pl.multiple_of(qb * t, t), t, 0, False)
            return carry

        if window:
            for c in range(t // LANES):
                first = b * t + c * LANES
                q0 = pl.multiple_of(jnp.minimum(first, s - (LANES + window)), LANES)
                tile(q0, LANES + window, q0 - first, True, slice(c * LANES, (c + 1) * LANES))
        else:
            tile(k0, t, 0, True)
            lax.fori_loop(b + 1, nblk, later_block, 0)
        dk_ref[...] = dk_acc[...].astype(bf16)
        dv_ref[...] = dv_acc[...].astype(bf16)

    kv_spec = pl.BlockSpec((t, LANES), lambda j, b: (b, j))
    seq_spec = pl.BlockSpec((s, LANES), lambda j, b: (0, j))
    rows_spec = pl.BlockSpec((None, 2, s), lambda j, b: (j, 0, 0))
    hw = N_PAIRS * LANES
    in_specs, args = [kv_spec, kv_spec, seq_spec, seq_spec, rows_spec, rows_spec], [k, v, q, do, lse, delta]
    out_specs = [pl.BlockSpec((LANES, s), lambda j, b: (j, 0)), kv_spec, kv_spec]
    out_shape = [jax.ShapeDtypeStruct((hw, s), f32), jax.ShapeDtypeStruct((s, hw), bf16), jax.ShapeDtypeStruct((s, hw), bf16)]
    if fox:
        in_specs += [pl.BlockSpec((s, 2 * LANES), lambda j, b: (0, j))]
        args += [cum_b]
        out_specs += [pl.BlockSpec((t, 2 * LANES), lambda j, b: (b, j)), rows_spec]
        out_shape += [jax.ShapeDtypeStruct((s, N_HEADS * LANES), f32), jax.ShapeDtypeStruct((N_PAIRS, 2, s), f32)]
    return pl.pallas_call(
        body, name=name, grid=(N_PAIRS, nblk), in_specs=in_specs, out_specs=out_specs, out_shape=out_shape,
        scratch_shapes=[pltpu.VMEM((t, LANES), f32)] * 2, compiler_params=_params(2),
    )(*args)


def _branch_merge(o_a, o_b, w_a, w_b, gl, name):
    s, k = o_a.shape
    d = w_a.shape[1]
    tm = _row_tile(s, 1024)

    def body(oa_ref, ob_ref, wa_ref, wb_ref, g_ref, ba_ref, bb_ref, m_ref):
        ba = jnp.dot(oa_ref[...], wa_ref[...], preferred_element_type=f32)
        bb = jnp.dot(ob_ref[...], wb_ref[...], preferred_element_type=f32)
        g0, g1 = jax.nn.sigmoid(g_ref[:, :d].astype(f32)), jax.nn.sigmoid(g_ref[:, d:].astype(f32))
        ba_ref[...] = ba.astype(bf16)
        bb_ref[...] = bb.astype(bf16)
        m_ref[...] = (g0 * ba + g1 * bb).astype(bf16)

    whole = pl.BlockSpec((k, d), lambda i: (0, 0))
    return pl.pallas_call(
        body, name=name, grid=(s // tm,),
        in_specs=[_row_spec(tm, k), _row_spec(tm, k), whole, whole, _row_spec(tm, 2 * d)],
        out_specs=[_row_spec(tm, d)] * 3, out_shape=[jax.ShapeDtypeStruct((s, d), bf16)] * 3, compiler_params=_params(1),
    )(o_a, o_b, w_a, w_b, gl)


def _out_dgrad_merge_bwd(dy, w_out, ba, bb, gl, name):
    s, d = ba.shape
    tm = _row_tile(s, 512)

    def body(dy_ref, w_ref, a_ref, b_ref, g_ref, da_ref, db_ref, dg_ref):
        dmv = lax.dot_general(dy_ref[...], w_ref[...], _NT, preferred_element_type=f32)
        g0, g1 = jax.nn.sigmoid(g_ref[:, :d].astype(f32)), jax.nn.sigmoid(g_ref[:, d:].astype(f32))
        da_ref[...] = (dmv * g0).astype(bf16)
        db_ref[...] = (dmv * g1).astype(bf16)
        dg_ref[:, :d] = (dmv * a_ref[...].astype(f32) * (g0 * (1.0 - g0))).astype(bf16)
        dg_ref[:, d:] = (dmv * b_ref[...].astype(f32) * (g1 * (1.0 - g1))).astype(bf16)

    return pl.pallas_call(
        body, name=name, grid=(s // tm,),
        in_specs=[_row_spec(tm, dy.shape[1]), pl.BlockSpec(w_out.shape, lambda i: (0, 0))] + [_row_spec(tm, d)] * 2
        + [_row_spec(tm, 2 * d)],
        out_specs=[_row_spec(tm, d)] * 2 + [_row_spec(tm, 2 * d)],
        out_shape=[jax.ShapeDtypeStruct((s, d), bf16)] * 2 + [jax.ShapeDtypeStruct((s, 2 * d), bf16)],
        compiler_params=_params(1),
    )(dy, w_out, ba, bb, gl)


GLU_TILE = 256


def _ffn_in_swiglu(h, w_t, name):
    s, d = h.shape
    f = w_t.shape[0] // 2
    tm = _row_tile(s, 2048)
    tg = GLU_TILE
    nb = f // tg

    def body(h_ref, wg_ref, wu_ref, g_ref, u_ref, act_ref):
        hv = h_ref[...]
        g = lax.dot_general(hv, wg_ref[...], _NT, preferred_element_type=f32)
        u = lax.dot_general(hv, wu_ref[...], _NT, preferred_element_type=f32)
        g_ref[...] = g.astype(bf16)
        u_ref[...] = u.astype(bf16)
        act_ref[...] = (g * jax.nn.sigmoid(g) * u).astype(bf16)

    col = pl.BlockSpec((tm, tg), lambda i, j: (i, j))
    return pl.pallas_call(
        body, name=name, grid=(s // tm, nb),
        in_specs=[pl.BlockSpec((tm, d), lambda i, j: (i, 0)), pl.BlockSpec((tg, d), lambda i, j: (j, 0)),
                  pl.BlockSpec((tg, d), lambda i, j: (j + nb, 0))],
        out_specs=[col] * 3, out_shape=[jax.ShapeDtypeStruct((s, f), bf16)] * 3, compiler_params=_params(2),
    )(h, w_t, w_t)


def _ffn_out_dgrad_swiglu(dy, w_out, g, u, name):
    s, d = dy.shape
    f = g.shape[1]
    tm = _row_tile(s, 2048)
    tg = GLU_TILE

    def body(dy_ref, w_ref, g_ref, u_ref, dg_ref, du_ref):
        dv = lax.dot_general(dy_ref[...], w_ref[...], _NT, preferred_element_type=f32)
        gv, uv = g_ref[...].astype(f32), u_ref[...].astype(f32)
        sg = jax.nn.sigmoid(gv)
        dg_ref[...] = (dv * uv * (sg * (1.0 + gv * (1.0 - sg)))).astype(bf16)
        du_ref[...] = (dv * (gv * sg)).astype(bf16)

    col = pl.BlockSpec((tm, tg), lambda i, j: (i, j))
    return pl.pallas_call(
        body, name=name, grid=(s // tm, f // tg),
        in_specs=[pl.BlockSpec((tm, d), lambda i, j: (i, 0)), pl.BlockSpec((tg, d), lambda i, j: (j, 0)), col, col],
        out_specs=[col] * 2, out_shape=[jax.ShapeDtypeStruct((s, f), bf16)] * 2, compiler_params=_params(2),
    )(dy, w_out, g, u)


def _wgrad_stack(parts, h, name):
    s, m = parts[0].shape
    d = h.shape[1]
    tm = 256
    nb = m // tm
    n = len(parts)

    def body(*refs):
        i = pl.program_id(0)
        for p in range(n):
            @pl.when(i // nb == p)
            def _(p=p):
                refs[n + 1][...] = lax.dot_general(refs[p][...], refs[n][...], _TN, preferred_element_type=f32).astype(bf16)

    a_specs = [pl.BlockSpec((s, tm), lambda i, p=p: (0, jnp.clip(i - p * nb, 0, nb - 1))) for p in range(n)]
    return pl.pallas_call(
        body, name=name, grid=(n * nb,), in_specs=a_specs + [pl.BlockSpec((s, d), lambda i: (0, 0))],
        out_specs=pl.BlockSpec((tm, d), lambda i: (i, 0)),
        out_shape=jax.ShapeDtypeStruct((n * m, d), bf16), compiler_params=_params(1),
    )(*parts, h)


def _ada_wgrad(c_all, d_all, name):
    n, d = c_all.shape
    w = d_all.shape[1]

    def body(c_ref, d_ref, o_ref):
        eye = (lax.broadcasted_iota(jnp.int32, (n, n), 0) == lax.broadcasted_iota(jnp.int32, (n, n), 1)).astype(f32)
        ct = lax.dot_general(c_ref[...], eye, _TN, precision=lax.Precision.HIGHEST, preferred_element_type=f32)
        g = ct[:, 0:1] * d_ref[0:1, :]
        for bi in range(1, n):
            g = g + ct[:, bi:bi + 1] * d_ref[bi:bi + 1, :]
        o_ref[0] = g

    return pl.pallas_call(
        body, name=name, out_shape=jax.ShapeDtypeStruct((1, d, w), f32), compiler_params=_params(),
    )(c_all, d_all)


def _adamw(parts, w, m, v, name, mine=None, me=None):
    r, c = w.shape
    n_parts = parts.shape[0]
    row_tiles = [t for t in range(min(r, 256), 0, -1) if r % t == 0 and (t % 16 == 0 or t == r)]
    if row_tiles:
        tr, tc = row_tiles[0], c
    else:
        tr, tc = r, next(t for t in (256, LANES) if c % t == 0)

    def body(*refs):
        w_ref, m_ref, v_ref, g_ref, d_ref, nm_ref, nv_ref = refs[-7:]
        if mine is None:
            p_ref, = refs[:-7]
        else:
            me_ref, p_ref, own_ref = refs[:-7]

        def part(i):
            if mine is None:
                return p_ref[i].astype(f32)
            return jnp.where(me_ref[0] == i, own_ref[...], p_ref[i]).astype(f32)

        g = part(0)
        for i in range(1, n_parts):
            g = g + part(i)
        mm = ADAM_B1 * m_ref[...] + (1.0 - ADAM_B1) * g
        vv = ADAM_B2 * v_ref[...] + (1.0 - ADAM_B2) * (g * g)
        m_hat = mm / (1.0 - ADAM_B1 ** ADAM_STEP)
        v_hat = vv / (1.0 - ADAM_B2 ** ADAM_STEP)
        g_ref[...] = g
        d_ref[...] = -ADAM_LR * (m_hat / (jnp.sqrt(v_hat) + ADAM_EPS) + ADAM_WD * w_ref[...])
        nm_ref[...] = mm
        nv_ref[...] = vv

    out_shape = [jax.ShapeDtypeStruct((r, c), f32)] * 4
    if mine is None:
        spec = pl.BlockSpec((tr, tc), lambda i, j: (i, j))
        return pl.pallas_call(
            body, name=name, grid=(r // tr, c // tc),
            in_specs=[pl.BlockSpec((n_parts, tr, tc), lambda i, j: (0, i, j))] + [spec] * 3,
            out_specs=[spec] * 4, out_shape=out_shape, compiler_params=_params(2),
        )(parts, w, m, v)
    spec = pl.BlockSpec((tr, tc), lambda i, j, me_ref: (i, j))
    return pl.pallas_call(
        body, name=name, out_shape=out_shape, compiler_params=_params(2),
        grid_spec=pltpu.PrefetchScalarGridSpec(
            num_scalar_prefetch=1, grid=(r // tr, c // tc),
            in_specs=[pl.BlockSpec((n_parts, tr, tc), lambda i, j, me_ref: (0, i, j)),
                      pl.BlockSpec((None, tr, tc), lambda i, j, me_ref: (me_ref[0], i, j))] + [spec] * 3,
            out_specs=[spec] * 4),
    )(me, parts, mine, w, m, v)


def _me():
    return lax.axis_index("x"), lax.axis_index("y"), lax.axis_index("c")


def _all_gather(arrays, name, vmem=False, after=None):
    n = len(arrays)
    space = pltpu.VMEM if vmem else pl.ANY
    extra = [] if after is None else [after]

    def body(*refs):
        ins = refs[:n]
        outs = refs[n + len(extra):2 * n + len(extra)]
        send_sems, recv_sems, local_sems = refs[2 * n + len(extra):]
        x, y, c = _me()
        me, sibling = (x, y, c), (x, y, 1 - c)
        chips = [(1 - x, y), (x, 1 - y), (1 - x, 1 - y)]

        def rows(a, dev):
            return outs[a].at[4 * dev[0] + 2 * dev[1] + dev[2]]

        def copy(a, k, block, to, src=None):
            return pltpu.make_async_remote_copy(
                src_ref=rows(a, block) if src is None else src, dst_ref=rows(a, block),
                send_sem=send_sems.at[a, k], recv_sem=recv_sems.at[a, k], device_id=to, device_id_type=MESH)

        mine = [pltpu.make_async_copy(ins[a], rows(a, me), local_sems.at[a]) for a in range(n)]
        for cp in mine:
            cp.start()
        first = []
        for a in range(n):
            first.append(copy(a, 0, me, sibling, src=ins[a]))
            first += [copy(a, 1 + j, me, (*chip, c), src=ins[a]) for j, chip in enumerate(chips)]
        for cp in first:
            cp.start()
        passed = []
        for j, chip in enumerate(chips):
            for a in range(n):
                copy(a, 1 + j, (*chip, c), me).wait_recv()
                fwd = copy(a, 4 + j, (*chip, c), sibling)
                fwd.start()
                passed.append(fwd)
        for a in range(n):
            copy(a, 0, sibling, me).wait_recv()
            for j, chip in enumerate(chips):
                copy(a, 4 + j, (*chip, 1 - c), me).wait_recv()
        for cp in first + passed:
            cp.wait_send()
        for cp in mine:
            cp.wait()

    outs = pl.pallas_call(
        body, name=name,
        in_specs=[pl.BlockSpec(memory_space=space)] * n + [pl.BlockSpec(memory_space=pl.ANY)] * len(extra),
        out_specs=[pl.BlockSpec(memory_space=space)] * n,
        out_shape=[jax.ShapeDtypeStruct((N_DEV,) + a.shape, a.dtype) for a in arrays],
        scratch_shapes=[pltpu.SemaphoreType.DMA((n, 7)), pltpu.SemaphoreType.DMA((n, 7)), pltpu.SemaphoreType.DMA((n,))],
        compiler_params=pltpu.CompilerParams(vmem_limit_bytes=VMEM_LIMIT),
    )(*arrays, *extra)
    return list(outs)


def _gather_prologue(c, w_ada, b_mine, w_in_t, name):
    n_dev, d = N_DEV, c.shape[1]
    ada_w = w_ada.shape[1]

    def body(c_ref, w_ref, b_ref, win_ref, call_ref, ada_ref, gin_ref, cols_ref, send_sems, recv_sems, local_sems):
        x, y, cc = _me()
        me, sibling = (x, y, cc), (x, y, 1 - cc)
        chips = [(1 - x, y), (x, 1 - y), (1 - x, 1 - y)]
        outs = (call_ref, ada_ref, gin_ref)

        def rows(a, dev):
            return outs[a].at[4 * dev[0] + 2 * dev[1] + dev[2]]

        def copy(a, k, block, to, src=None):
            return pltpu.make_async_remote_copy(
                src_ref=rows(a, block) if src is None else src, dst_ref=rows(a, block),
                send_sem=send_sems.at[a, k], recv_sem=recv_sems.at[a, k], device_id=to, device_id_type=MESH)

        def begin(a, src):
            own = pltpu.make_async_copy(src, rows(a, me), local_sems.at[a])
            sends = [copy(a, 0, me, sibling, src=src)] + [copy(a, 1 + j, me, (*chip, cc), src=src) for j, chip in enumerate(chips)]
            for cp in [own] + sends:
                cp.start()
            return own, sends

        def finish(a, own, sends):
            passed = []
            for j, chip in enumerate(chips):
                copy(a, 1 + j, (*chip, cc), me).wait_recv()
                passed.append(copy(a, 4 + j, (*chip, cc), sibling))
                passed[-1].start()
            copy(a, 0, sibling, me).wait_recv()
            for j, chip in enumerate(chips):
                copy(a, 4 + j, (*chip, 1 - cc), me).wait_recv()
            for cp in sends + passed:
                cp.wait_send()
            own.wait()

        finish(0, *begin(0, c_ref))
        cols_ref[...] = (jnp.dot(call_ref[:, 0, :].astype(bf16), w_ref[...].astype(bf16), preferred_element_type=f32)
                         + b_ref[...])
        finish(1, *begin(1, cols_ref))
        finish(2, *begin(2, win_ref))

    vmem, hbm = pl.BlockSpec(memory_space=pltpu.VMEM), pl.BlockSpec(memory_space=pl.ANY)
    return pl.pallas_call(
        body, name=name, in_specs=[vmem, vmem, vmem, hbm], out_specs=[vmem, vmem, hbm],
        out_shape=[jax.ShapeDtypeStruct((n_dev, 1, d), f32), jax.ShapeDtypeStruct((n_dev, n_dev, ada_w), f32),
                   jax.ShapeDtypeStruct((n_dev,) + w_in_t.shape, w_in_t.dtype)],
        scratch_shapes=[pltpu.VMEM((n_dev, ada_w), f32), pltpu.SemaphoreType.DMA((3, 7)), pltpu.SemaphoreType.DMA((3, 7)),
                        pltpu.SemaphoreType.DMA((3,))],
        compiler_params=pltpu.CompilerParams(vmem_limit_bytes=VMEM_LIMIT),
    )(c, w_ada, b_mine, w_in_t)


_FLIPS = ((0, 0, 1), (1, 0, 0), (0, 1, 0), (1, 1, 0), (1, 0, 1), (0, 1, 1), (1, 1, 1))
_HBM = pl.BlockSpec(memory_space=pltpu.HBM)
_SEM = pl.BlockSpec(memory_space=pltpu.SEMAPHORE)


def _exchange_copies(scatter, srcs, lands, send_sems, recv_sems):
    x, y, c = _me()
    me_row = 4 * x + 2 * y + c
    out = []
    for k, (fx, fy, fc) in enumerate(_FLIPS):
        peer = (x ^ fx, y ^ fy, c ^ fc)
        peer_row = 4 * peer[0] + 2 * peer[1] + peer[2]
        for a in range(len(srcs)):
            out.append(pltpu.make_async_remote_copy(
                src_ref=srcs[a].at[peer_row] if scatter else srcs[a], dst_ref=lands[a].at[me_row],
                send_sem=send_sems.at[7 * a + k], recv_sem=recv_sems.at[7 * a + k], device_id=peer, device_id_type=MESH))
    return out


def _exchange_start(arrays, scatter, name, after=None):
    n = len(arrays)
    lands = [lax.empty(a.shape if scatter else (N_DEV,) + a.shape, a.dtype) for a in arrays]
    extra = [] if after is None else [after]

    def body(*refs):
        srcs, zones = refs[:n], refs[n:2 * n]
        send_sems, recv_sems = refs[2 * n + len(extra)], refs[2 * n + len(extra) + 1]
        token = refs[-1]
        for cp in _exchange_copies(scatter, srcs, zones, send_sems, recv_sems):
            cp.start()
        token[...] = jnp.zeros_like(token)

    thru = [pltpu.HBM(a.shape, a.dtype) for a in list(arrays) + lands]
    outs = pl.pallas_call(
        body, name=name,
        out_shape=(pltpu.SemaphoreType.DMA((7 * n,)), pltpu.SemaphoreType.DMA((7 * n,)), *thru, jax.ShapeDtypeStruct((8, LANES), f32)),
        in_specs=[_HBM] * (2 * n) + [pl.BlockSpec(memory_space=pl.ANY)] * len(extra),
        out_specs=(_SEM, _SEM, *[_HBM] * (2 * n), pl.BlockSpec(memory_space=pltpu.VMEM)),
        input_output_aliases={i: 2 + i for i in range(2 * n)},
        compiler_params=pltpu.CompilerParams(has_side_effects=pltpu.SideEffectType.DATAFLOW_SIDE_EFFECTING),
    )(*[pltpu.with_memory_space_constraint(a, pltpu.HBM) for a in list(arrays) + lands], *extra)
    return dict(n=n, scatter=scatter, sems=outs[:2], srcs=outs[2:2 + n], lands=outs[2 + n:2 + 2 * n], token=outs[-1])


def _exchange_wait(handle, after, name):
    n, scatter = handle["n"], handle["scatter"]

    def body(*refs):
        srcs, zones = refs[:n], refs[n:2 * n]
        send_sems, recv_sems = refs[2 * n], refs[2 * n + 1]
        for cp in _exchange_copies(scatter, srcs, zones, send_sems, recv_sems):
            cp.wait_send()
            cp.wait_recv()

    thru = [pltpu.HBM(a.shape, a.dtype) for a in list(handle["srcs"]) + list(handle["lands"])]
    outs = pl.pallas_call(
        body, name=name, out_shape=tuple(thru),
        in_specs=[_HBM] * (2 * n) + [_SEM, _SEM, pl.BlockSpec(memory_space=pl.ANY)], out_specs=tuple([_HBM] * (2 * n)),
        input_output_aliases={i: i for i in range(2 * n)},
        compiler_params=pltpu.CompilerParams(has_side_effects=pltpu.SideEffectType.DATAFLOW_SIDE_EFFECTING),
    )(*handle["srcs"], *handle["lands"], *handle["sems"], after)
    return list(outs[n:])


def _cols_from_shards(g):
    return jnp.transpose(g, (1, 0, 2)).reshape(g.shape[1], -1)


def _shards_from_cols(a):
    return jnp.transpose(a.reshape(a.shape[0], N_DEV, -1), (1, 0, 2))


def _local_step(x, positions, ada, g_pre_mix, g_post_mix, b_f, sinks, g_pre_ffn, g_post_ffn, target,
                w_in_t, mix_weights, ffn_weights, on_grads):
    s, d = x.shape
    row = lambda v: v.reshape(1, -1)
    shift_m, scale_m, gate_m, shift_f, scale_f, gate_f = (ada[i:i + 1] for i in range(6))
    w_gate_t, w_qkv_t = w_in_t[F_OFF + N_HEADS:], w_in_t[:QKV_W]
    w_f_t = jnp.pad(w_in_t[F_OFF:F_OFF + N_HEADS], ((0, LANES - N_HEADS), (0, 0)))
    bf_row = jnp.pad(row(b_f), ((0, 0), (0, LANES - N_HEADS)))
    sink_rows = jnp.broadcast_to(sinks.reshape(N_HEADS, 1).astype(f32), (N_HEADS, LANES))
    inv_freq = 1.0 / (ROPE_THETA ** (jnp.arange(0, HEAD_DIM, 2, dtype=f32) / HEAD_DIM))
    cos, sin_s = _rope_tables(positions.reshape(s, 1), jnp.tile(inv_freq, 4).reshape(1, LANES), "rope_tables")

    h1, qa, ka, va, qb, kb, vb = _prenorm_proj_qkv(x, row(g_pre_mix), scale_m, shift_m, w_qkv_t, cos, sin_s, "prenorm_proj_qkv")
    gl = _matmul(h1, w_gate_t, "nt", bf16, "proj_gate")
    fl, cum_b = _forget_prep(h1, w_f_t, bf_row, "proj_forget_prep")
    o_a, lse_a = _attn_fwd(qa, ka, va, "swa_fwd", sink_rows=sink_rows, window=WINDOW, t=2048)
    o_b, lse_b = _attn_fwd(qb, kb, vb, "fox_fwd", cum_b=cum_b, t=1024)
    everything_before = (gl[:8, :LANES] + o_a[:8, :LANES] + o_b[:8, :LANES]).astype(f32)
    w_branch_a, w_branch_b, w_out = mix_weights(everything_before)
    ba, bb, merged = _branch_merge(o_a, o_b, w_branch_a, w_branch_b, gl, "branch_merge")
    y1, x2, h2 = _out_proj_postnorm_prenorm(merged, w_out, x, row(g_post_mix), gate_m, row(g_pre_ffn), scale_f, shift_f,
                                            "out_proj_norms")

    w_ffn_in_t, w_ffn_out = ffn_weights(h2)
    g_ff, u_ff, act = _ffn_in_swiglu(h2, w_ffn_in_t, "ffn_in_swiglu")
    loss_row, d_out, d_y2, vec_pf = _out_proj_loss_tail(act, w_ffn_out, x2, row(g_post_ffn), gate_f, target, "ffn_out_loss_tail")

    g_w_ffn_out = _matmul(act, d_y2, "tn", bf16, "ffn_out_wgrad")
    dg_ff, du_ff = _ffn_out_dgrad_swiglu(d_y2, w_ffn_out, g_ff, u_ff, "ffn_out_dgrad_swiglu")
    g_w_ffn_in_t = _wgrad_stack([dg_ff, du_ff], h2, "ffn_in_wgrad")
    sent = on_grads(dict(w_ffn_in=g_w_ffn_in_t, w_ffn_out=g_w_ffn_out))
    d_x2, vec_nf, d_y1, vec_pm = _dgrad_prenorm_bwd(
        [(dg_ff, w_ffn_in_t, 0), (du_ff, w_ffn_in_t, 1)], x2, row(g_pre_ffn), scale_f, d_out, "ffn_in_dgrad_norms_bwd",
        after=sent, below=(y1, row(g_post_mix), gate_m))

    g_w_out = _matmul(merged, d_y1, "tn", bf16, "out_proj_wgrad")
    d_ba, d_bb, dgl = _out_dgrad_merge_bwd(d_y1, w_out, ba, bb, gl, "out_proj_dgrad_merge_bwd")
    g_w_branch_a = _matmul(o_a, d_ba, "tn", bf16, "branch_a_wgrad")
    g_w_branch_b = _matmul(o_b, d_bb, "tn", bf16, "branch_b_wgrad")
    sent = on_grads(dict(w_out=g_w_out, w_branch_a=g_w_branch_a, w_branch_b=g_w_branch_b))
    d_oa, delta_a, d_sink = _branch_dgrad_delta(d_ba, w_branch_a, o_a, "branch_a_dgrad_delta", lse=lse_a,
                                                sink_rows=sink_rows, after=sent)
    d_ob, delta_b = _branch_dgrad_delta(d_bb, w_branch_b, o_b, "branch_b_dgrad_delta", after=sent)
    dqa_t, dka, dva = _attn_bwd(qa, ka, va, d_oa, lse_a, delta_a, "swa_bwd", window=WINDOW, t=2048)
    dqb_t, dkb, dvb, dcs, rs = _attn_bwd(qb, kb, vb, d_ob, lse_b, delta_b, "fox_bwd", cum_b=cum_b, t=512)
    dqkv = _qkv_prep_bwd(dqa_t, dka, dva, dqb_t, dkb, dvb, cos, sin_s, "qkv_prep_bwd")
    dfl, vec_bf = _forget_prep_bwd(rs.reshape(N_HEADS, s), dcs, fl, bf_row, "forget_prep_bwd")
    g_w_in_t = jnp.concatenate([_matmul(dqkv, h1, "tn", bf16, "qkv_wgrad"), _matmul(dfl, h1, "tn", bf16, "forget_wgrad")[:N_HEADS],
                                _matmul(dgl, h1, "tn", bf16, "gate_wgrad")], axis=0)
    sent = on_grads(dict(w_in=g_w_in_t))
    grad_x, vec_nm = _dgrad_prenorm_bwd([(dgl, w_gate_t, 0), (dqkv, w_qkv_t, 0), (dfl, w_f_t, 0)], x, row(g_pre_mix),
                                        scale_m, d_x2, "in_proj_dgrad_prenorm_bwd", after=sent)

    d_ada = jnp.concatenate([vec_nm[0], vec_nm[1], vec_pm[0], vec_nf[0], vec_nf[1], vec_pf[0]])
    small = dict(b_ada=d_ada, g_pre_mix=vec_nm[2], g_post_mix=vec_pm[1], g_pre_ffn=vec_nf[2], g_post_ffn=vec_pf[1],
                 b_f=vec_bf[0, :N_HEADS], sinks=d_sink[:, 0], loss=loss_row[0, :1])
    return grad_x, small


_SMALL = (("b_ada", 6144), ("g_pre_mix", 1024), ("g_post_mix", 1024), ("g_pre_ffn", 1024), ("g_post_ffn", 1024),
          ("b_f", 128), ("sinks", 128), ("loss", 128))
_SMALL_ROWS = 88


def _pack_small(vals):
    parts = [jnp.pad(vals[k].reshape(-1).astype(f32), (0, n - vals[k].size)) for k, n in _SMALL]
    flat = jnp.concatenate(parts)
    return jnp.pad(flat, (0, _SMALL_ROWS * LANES - flat.size)).reshape(_SMALL_ROWS, LANES)


def _unpack_small(slab, shapes):
    flat, out, off = slab.reshape(-1), {}, 0
    for k, n in _SMALL:
        size = math.prod(shapes[k])
        out[k] = flat[off:off + size].reshape(shapes[k])
        off += n
    return out


def kernel(x, c, positions, w_ada, b_ada, g_pre_mix, g_post_mix, w_in, b_f, sinks, w_branch_a, w_branch_b, w_out, g_pre_ffn, g_post_ffn, w_ffn_in, w_ffn_out, loss_target, m_w_ada, m_b_ada, m_g_pre_mix, m_g_post_mix, m_w_in, m_b_f, m_sinks, m_w_branch_a, m_w_branch_b, m_w_out, m_g_pre_ffn, m_g_post_ffn, m_w_ffn_in, m_w_ffn_out, v_w_ada, v_b_ada, v_g_pre_mix, v_g_post_mix, v_w_in, v_b_f, v_sinks, v_w_branch_a, v_w_branch_b, v_w_out, v_g_pre_ffn, v_g_post_ffn, v_w_ffn_in, v_w_ffn_out):
    xi, yi, ci = _me()
    me = 4 * xi + 2 * yi + ci
    d = D_MODEL
    ada_w = w_ada.shape[2]

    transposed = ("w_in", "w_ffn_in")
    tr = lambda a: jnp.transpose(a[0])

    b_mine = lax.dynamic_slice(b_ada, (0, me * ada_w), (1, ada_w))
    c_all, ada_all, g_in = _gather_prologue(c, w_ada[0], b_mine, tr(w_in).astype(bf16), "gather_prologue")
    c_all = c_all.reshape(N_DEV, d)
    ada = lax.dynamic_index_in_dim(ada_all, me, axis=1, keepdims=False).reshape(6, d)
    late_mix = [w.astype(bf16) for w in (w_branch_a[0], w_branch_b[0], w_out[0])]
    late_ffn = [w.astype(bf16) for w in (tr(w_ffn_in), w_ffn_out[0])]
    mix_h = _exchange_start(late_mix, False, "gather_mix_start", after=g_in)
    ffn_h = _exchange_start(late_ffn, False, "gather_ffn_start", after=mix_h["token"])

    def mine_into(zone, block):
        return lax.dynamic_update_index_in_dim(zone, block, me, 0)

    def rows_from_shards(g):
        return g.reshape(g.shape[0] * g.shape[1], g.shape[2])

    def mix_weights(after):
        zones = _exchange_wait(mix_h, after, "gather_mix_wait")
        g_ba, g_bb, g_out = (mine_into(z, w) for z, w in zip(zones, late_mix))
        return _cols_from_shards(g_ba), _cols_from_shards(g_bb), rows_from_shards(g_out)

    def ffn_weights(after):
        zones = _exchange_wait(ffn_h, after, "gather_ffn_wait")
        g_fi, g_fo = (mine_into(z, w) for z, w in zip(zones, late_ffn))
        return rows_from_shards(g_fi), rows_from_shards(g_fo)

    row_sharded = ("w_out", "w_ffn_out") + transposed
    in_flight = []

    def on_grads(group):
        sends = [g.reshape(N_DEV, g.shape[0] // N_DEV, g.shape[1]) if nm in row_sharded else _shards_from_cols(g)
                 for nm, g in group.items()]
        handle = _exchange_start(sends, True, "scatter_start_%d" % len(in_flight))
        in_flight.append((list(group), sends, handle))
        return handle["token"]

    grad_x, small = _local_step(
        x[0], positions[0], ada + ffn_h["token"][0, 0], g_pre_mix[0], g_post_mix[0], b_f[0], sinks[0], g_pre_ffn[0],
        g_post_ffn[0], loss_target[0], rows_from_shards(g_in), mix_weights, ffn_weights, on_grads)

    ws = dict(w_in=(w_in, m_w_in, v_w_in), w_branch_a=(w_branch_a, m_w_branch_a, v_w_branch_a),
              w_branch_b=(w_branch_b, m_w_branch_b, v_w_branch_b), w_out=(w_out, m_w_out, v_w_out),
              w_ffn_in=(w_ffn_in, m_w_ffn_in, v_w_ffn_in), w_ffn_out=(w_ffn_out, m_w_ffn_out, v_w_ffn_out))
    res = {}

    def finish_group(gi, after):
        names, sends, handle = in_flight[gi]
        zones = _exchange_wait(handle, after, "scatter_wait_%d" % gi)
        for nm, zone, sent in zip(names, zones, sends):
            w, m, v = (tr(a) if nm in transposed else a[0] for a in ws[nm])
            out = _adamw(zone, w, m, v, "adamw_" + nm, mine=sent, me=me.reshape(1).astype(jnp.int32))
            after = out[0]
            res[nm] = [jnp.transpose(o) for o in out] if nm in transposed else out
        return after

    done = finish_group(1, finish_group(0, grad_x))

    slab_all, = _all_gather([_pack_small(small)], "gather_small", vmem=True, after=done)
    small_w = dict(b_ada=b_ada, g_pre_mix=g_pre_mix, g_post_mix=g_post_mix, g_pre_ffn=g_pre_ffn, g_post_ffn=g_post_ffn,
                   b_f=b_f, sinks=sinks, loss=jnp.zeros((1,), f32))
    small_m = dict(b_ada=m_b_ada, g_pre_mix=m_g_pre_mix, g_post_mix=m_g_post_mix, g_pre_ffn=m_g_pre_ffn,
                   g_post_ffn=m_g_post_ffn, b_f=m_b_f, sinks=m_sinks, loss=jnp.zeros((1,), f32))
    small_v = dict(b_ada=v_b_ada, g_pre_mix=v_g_pre_mix, g_post_mix=v_g_post_mix, g_pre_ffn=v_g_pre_ffn,
                   g_post_ffn=v_g_post_ffn, b_f=v_b_f, sinks=v_sinks, loss=jnp.ones((1,), f32))
    shapes = {k: small_w[k].shape for k, _ in _SMALL}
    s_out = _adamw(slab_all, _pack_small(small_w), _pack_small(small_m), _pack_small(small_v), "adamw_small")
    s_grad, s_delta, s_m, s_v = (_unpack_small(o, shapes) for o in s_out)

    d_ada_all = lax.dynamic_slice(slab_all[:, :6144 // LANES, :].reshape(N_DEV, 6144), (0, me * ada_w), (N_DEV, ada_w))
    ada_parts = _ada_wgrad(c_all, d_ada_all, "ada_wgrad")

    res["w_ada"] = _adamw(ada_parts, w_ada[0], m_w_ada[0], v_w_ada[0], "adamw_w_ada")
    finish_group(2, res["w_ada"][0])

    order = ["w_ada", "b_ada", "g_pre_mix", "g_post_mix", "w_in", "b_f", "sinks", "w_branch_a", "w_branch_b", "w_out",
             "g_pre_ffn", "g_post_ffn", "w_ffn_in", "w_ffn_out"]
    outs = [s_grad["loss"].reshape(()), grad_x[None]]
    for which, small_o in enumerate((s_grad, s_delta, s_m, s_v)):
        for nm in order:
            outs.append(res[nm][which][None] if nm in res else small_o[nm])
    return tuple(outs)
```

```python
import math

import jax
import jax.numpy as jnp
from jax import lax
from jax.experimental import pallas as pl
from jax.experimental.pallas import tpu as pltpu

f32 = jnp.float32
bf16 = jnp.bfloat16

D_MODEL = 1024
HEAD_DIM = 64
N_HEADS = 8
N_PAIRS = 4
QKV_W = 2304
F_OFF = 2304
WINDOW = 128
ROPE_THETA = 10000.0
RMS_EPS = 1e-6
N_DEV = 8
ADAM_LR, ADAM_B1, ADAM_B2, ADAM_EPS, ADAM_WD, ADAM_STEP = 0.001, 0.9, 0.999, 1e-08, 0.01, 10
NEG = -1e30
L_ROW = (HEAD_DIM, 0)
LANES = 128
VMEM_LIMIT = 48 * 1024 * 1024
MESH = pl.DeviceIdType.MESH

_NT = (((1,), (1,)), ((), ()))
_TN = (((0,), (0,)), ((), ()))


def _params(n_grid=0):
    sem = ("arbitrary",) * n_grid if n_grid else None
    return pltpu.CompilerParams(dimension_semantics=sem, vmem_limit_bytes=VMEM_LIMIT)


def _row_tile(s, want):
    t = min(s, want)
    assert s % t == 0, (s, t)
    return t


MATMUL_VMEM_BUDGET = 40 * 1024 * 1024


def _matmul_tiles(m, n, k, a_item, b_item, o_item):
    def tiles(d):
        return [t for t in range(LANES, min(d, 2048) + 1, LANES) if d % t == 0] or [d]

    best = None
    for tm in tiles(m):
        for tn in tiles(n):
            vmem = 2 * (tm * k * a_item + tn * k * b_item + tm * tn * o_item) + tm * tn * 4
            if vmem > MATMUL_VMEM_BUDGET:
                continue
            traffic = m * k * a_item + n * k * b_item * (1 if tn == n else m // tm) + m * n * o_item
            steps = (m // tm) * (n // tn)
            key = (traffic, 0, steps) if steps >= 4 else (traffic, 1, -steps)
            if best is None or key < best[0]:
                best = (key, tm, tn)
    assert best is not None, (m, n, k)
    return best[1], best[2]


def _matmul(a, b, mode, out_dtype, name, after=None):
    if mode == "nn":
        (m, k), n = a.shape, b.shape[1]
    elif mode == "nt":
        (m, k), n = a.shape, b.shape[0]
    else:
        (k, m), n = a.shape, b.shape[1]
    tm, tn = _matmul_tiles(m, n, k, a.dtype.itemsize, b.dtype.itemsize, jnp.dtype(out_dtype).itemsize)
    if mode == "nn":
        a_spec, b_spec, dims = pl.BlockSpec((tm, k), lambda i, j: (i, 0)), pl.BlockSpec((k, tn), lambda i, j: (0, j)), None
    elif mode == "nt":
        a_spec, b_spec, dims = pl.BlockSpec((tm, k), lambda i, j: (i, 0)), pl.BlockSpec((tn, k), lambda i, j: (j, 0)), _NT
    else:
        a_spec, b_spec, dims = pl.BlockSpec((k, tm), lambda i, j: (0, i)), pl.BlockSpec((k, tn), lambda i, j: (0, j)), _TN

    def body(a_ref, b_ref, *rest):
        o_ref = rest[-1]
        av, bv = a_ref[...].astype(bf16), b_ref[...].astype(bf16)
        if dims is None:
            r = jnp.dot(av, bv, preferred_element_type=f32)
        else:
            r = lax.dot_general(av, bv, dims, preferred_element_type=f32)
        o_ref[...] = r.astype(out_dtype)

    extra = [] if after is None else [after]
    return pl.pallas_call(
        body, name=name, grid=(m // tm, n // tn), in_specs=[a_spec, b_spec] + [pl.BlockSpec(memory_space=pl.ANY)] * len(extra),
        out_specs=pl.BlockSpec((tm, tn), lambda i, j: (i, j)),
        out_shape=jax.ShapeDtypeStruct((m, n), out_dtype), compiler_params=_params(2),
    )(a, b, *extra)


def _rstd(v):
    return lax.rsqrt(jnp.mean(v * v, axis=-1, keepdims=True) + RMS_EPS)


def _row_spec(tm, d):
    return pl.BlockSpec((tm, d), lambda i: (i, 0))


def _vec_spec(d, rows=1):
    return pl.BlockSpec((rows, d), lambda i: (0, 0))


def _proj_spec(a, w, tm):
    return [_row_spec(tm, a.shape[1]), pl.BlockSpec(w.shape, lambda i: (0, 0))]


def _out_proj_postnorm_prenorm(a, w, x, g_post, gate, g_pre, scale, shift, name):
    s, d = x.shape
    tm = _row_tile(s, 512)

    def body(a_ref, w_ref, x_ref, gp_ref, gate_ref, g_ref, sc_ref, sh_ref, y_ref, x2_ref, h_ref):
        yv = jnp.dot(a_ref[...], w_ref[...], preferred_element_type=f32)
        y_ref[...] = yv
        x2 = x_ref[...] + gate_ref[...] * (yv * _rstd(yv) * gp_ref[...])
        x2_ref[...] = x2
        h_ref[...] = ((x2 * _rstd(x2) * g_ref[...]) * (1.0 + sc_ref[...]) + sh_ref[...]).astype(bf16)

    return pl.pallas_call(
        body, name=name, grid=(s // tm,), in_specs=_proj_spec(a, w, tm) + [_row_spec(tm, d)] + [_vec_spec(d)] * 5,
        out_specs=[_row_spec(tm, d)] * 3,
        out_shape=[jax.ShapeDtypeStruct((s, d), f32)] * 2 + [jax.ShapeDtypeStruct((s, d), bf16)], compiler_params=_params(1),
    )(a, w, x, g_post, gate, g_pre, scale, shift)


def _rms_bwd(u, v, r):
    return r * u - v * (r * r * r) * jnp.mean(u * v, axis=-1, keepdims=True)


def _out_proj_loss_tail(a, w, x, g, gate, target, name):
    s, d = x.shape
    tm = _row_tile(s, 512)

    def body(a_ref, w_ref, x_ref, g_ref, gate_ref, t_ref, loss_ref, do_ref, dy_ref, vec_ref):
        @pl.when(pl.program_id(0) == 0)
        def _():
            loss_ref[...] = jnp.zeros_like(loss_ref)
            vec_ref[...] = jnp.zeros_like(vec_ref)
        yv = jnp.dot(a_ref[...], w_ref[...], preferred_element_type=f32)
        r = _rstd(yv)
        yn = yv * r
        err = x_ref[...] + gate_ref[...] * (yn * g_ref[...]) - t_ref[...]
        loss_ref[...] += 0.5 * jnp.sum(jnp.mean(err * err, axis=-1, keepdims=True), axis=0, keepdims=True)
        dr = err / d
        do_ref[...] = dr
        dn = dr * gate_ref[...]
        vec_ref[0:1, :] += jnp.sum(dr * (yn * g_ref[...]), axis=0, keepdims=True)
        vec_ref[1:2, :] += jnp.sum(dn * yn, axis=0, keepdims=True)
        dy_ref[...] = _rms_bwd(dn * g_ref[...], yv, r).astype(bf16)

    return pl.pallas_call(
        body, name=name, grid=(s // tm,),
        in_specs=_proj_spec(a, w, tm) + [_row_spec(tm, d)] + [_vec_spec(d)] * 2 + [_row_spec(tm, d)],
        out_specs=[_vec_spec(LANES), _row_spec(tm, d), _row_spec(tm, d), _vec_spec(d, 8)],
        out_shape=[jax.ShapeDtypeStruct((1, LANES), f32), jax.ShapeDtypeStruct((s, d), f32),
                   jax.ShapeDtypeStruct((s, d), bf16), jax.ShapeDtypeStruct((8, d), f32)],
        compiler_params=_params(1),
    )(a, w, x, g, gate, target)


def _dgrad_prenorm_bwd(terms, x, g, scale, dres, name, after=None, below=None):
    s, d = x.shape
    n = len(terms)
    k = sum(a.shape[1] for a, _, _ in terms)
    row_bytes = 2 * (2 * k) + d * (4 + 2 * 4 * 3 + (2 * 4 + 2 * 2 if below else 0))
    tm = next(t for t in (512, 256, 128) if s % t == 0 and 4 * k * d + t * row_bytes <= MATMUL_VMEM_BUDGET)
    extra = [] if after is None else [after]

    def body(*refs):
        a_refs, b_refs = refs[:n], refs[n:2 * n]
        x_ref, g_ref, sc_ref, dr_ref = refs[2 * n:2 * n + 4]
        n_in = 2 * n + 4 + (3 if below else 0) + len(extra)
        dx_ref, vec_ref = refs[n_in], refs[n_in + 1]
        if below:
            y_ref, gp_ref, gate_ref = refs[2 * n + 4:2 * n + 7]
            dy_ref, vec2_ref = refs[n_in + 2], refs[n_in + 3]

        @pl.when(pl.program_id(0) == 0)
        def _():
            vec_ref[...] = jnp.zeros_like(vec_ref)
            if below:
                vec2_ref[...] = jnp.zeros_like(vec2_ref)
        dhv = jnp.dot(a_refs[0][...], b_refs[0][...], preferred_element_type=f32)
        for i in range(1, n):
            dhv = dhv + jnp.dot(a_refs[i][...], b_refs[i][...], preferred_element_type=f32)
        xv = x_ref[...]
        r = _rstd(xv)
        xn = xv * r
        dn = dhv * (1.0 + sc_ref[...])
        vec_ref[0:1, :] += jnp.sum(dhv, axis=0, keepdims=True)
        vec_ref[1:2, :] += jnp.sum(dhv * (xn * g_ref[...]), axis=0, keepdims=True)
        vec_ref[2:3, :] += jnp.sum(dn * xn, axis=0, keepdims=True)
        dx = dr_ref[...] + _rms_bwd(dn * g_ref[...], xv, r)
        dx_ref[...] = dx
        if below:
            yv = y_ref[...]
            ry = _rstd(yv)
            yn = yv * ry
            dny = dx * gate_ref[...]
            vec2_ref[0:1, :] += jnp.sum(dx * (yn * gp_ref[...]), axis=0, keepdims=True)
            vec2_ref[1:2, :] += jnp.sum(dny * yn, axis=0, keepdims=True)
            dy_ref[...] = _rms_bwd(dny * gp_ref[...], yv, ry).astype(bf16)

    in_specs = ([_row_spec(tm, a.shape[1]) for a, _, _ in terms]
                + [pl.BlockSpec((a.shape[1], d), lambda i, r=r: (r, 0)) for a, _, r in terms]
                + [_row_spec(tm, d)] + [_vec_spec(d)] * 2 + [_row_spec(tm, d)])
    out_specs = [_row_spec(tm, d), _vec_spec(d, 8)]
    out_shape = [jax.ShapeDtypeStruct((s, d), f32), jax.ShapeDtypeStruct((8, d), f32)]
    args = [a for a, _, _ in terms] + [b for _, b, _ in terms] + [x, g, scale, dres]
    if below:
        in_specs += [_row_spec(tm, d)] + [_vec_spec(d)] * 2
        out_specs += [_row_spec(tm, d), _vec_spec(d, 8)]
        out_shape += [jax.ShapeDtypeStruct((s, d), bf16), jax.ShapeDtypeStruct((8, d), f32)]
        args += list(below)
    return pl.pallas_call(
        body, name=name, grid=(s // tm,), in_specs=in_specs + [pl.BlockSpec(memory_space=pl.ANY)] * len(extra),
        out_specs=out_specs, out_shape=out_shape, compiler_params=_params(1),
    )(*args, *extra)


def _lane():
    return lax.broadcasted_iota(jnp.int32, (1, LANES), 1)


def _rope_tables(pos_col, inv_freq, name):
    s = pos_col.shape[0]

    def body(p_ref, f_ref, cos_ref, sin_ref):
        ang = p_ref[...].astype(f32) * f_ref[...]
        first_half = (_lane() % HEAD_DIM) < HEAD_DIM // 2
        cos_ref[...] = jnp.cos(ang)
        sn = jnp.sin(ang)
        sin_ref[...] = jnp.where(first_half, -sn, sn)

    return pl.pallas_call(
        body, name=name, out_shape=[jax.ShapeDtypeStruct((s, LANES), f32)] * 2, compiler_params=_params(),
    )(pos_col, inv_freq)


def _swap_halves(v):
    first_half = (_lane() % HEAD_DIM) < HEAD_DIM // 2
    return jnp.where(first_half, pltpu.roll(v, LANES - HEAD_DIM // 2, axis=1), pltpu.roll(v, HEAD_DIM // 2, axis=1))


def _prenorm_proj_qkv(x, g, mod_scale, mod_shift, w_qkv_t, cos, sin_s, name):
    s, d = x.shape
    tm = _row_tile(s, 512)
    scale = 1.0 / math.sqrt(HEAD_DIM)

    def body(x_ref, g_ref, msc_ref, msh_ref, w_ref, c_ref, s_ref, h_ref, qa_ref, ka_ref, va_ref, qb_ref, kb_ref, vb_ref):
        xv = x_ref[...]
        h = ((xv * _rstd(xv) * g_ref[...]) * (1.0 + msc_ref[...]) + msh_ref[...]).astype(bf16)
        h_ref[...] = h
        proj = lax.dot_general(h, w_ref[...], _NT, preferred_element_type=f32)
        cs, sn = c_ref[...], s_ref[...]
        low = _lane() < HEAD_DIM

        def blk(j):
            return proj[:, j * LANES:(j + 1) * LANES]

        def rope(v):
            return v * cs + _swap_halves(v) * sn

        def expand(v):
            other = pltpu.roll(v, HEAD_DIM, axis=1)
            return jnp.where(low, v, other), jnp.where(low, other, v)

        for j in range(N_PAIRS):
            qa_ref[:, j * LANES:(j + 1) * LANES] = (rope(blk(j)) * scale).astype(bf16)
            qb_ref[:, j * LANES:(j + 1) * LANES] = (blk(6 + j) * scale).astype(bf16)
            kb_ref[:, j * LANES:(j + 1) * LANES] = blk(10 + j).astype(bf16)
            vb_ref[:, j * LANES:(j + 1) * LANES] = blk(14 + j).astype(bf16)
        k0, k1 = expand(rope(blk(4)))
        v0, v1 = expand(blk(5))
        for j in range(N_PAIRS):
            ka_ref[:, j * LANES:(j + 1) * LANES] = (k0 if j < 2 else k1).astype(bf16)
            va_ref[:, j * LANES:(j + 1) * LANES] = (v0 if j < 2 else v1).astype(bf16)

    hw = N_PAIRS * LANES
    return pl.pallas_call(
        body, name=name, grid=(s // tm,),
        in_specs=[_row_spec(tm, d)] + [_vec_spec(d)] * 3
        + [pl.BlockSpec((QKV_W, d), lambda i: (0, 0)), _row_spec(tm, LANES), _row_spec(tm, LANES)],
        out_specs=[_row_spec(tm, d)] + [_row_spec(tm, hw)] * 6,
        out_shape=[jax.ShapeDtypeStruct((s, d), bf16)] + [jax.ShapeDtypeStruct((s, hw), bf16)] * 6, compiler_params=_params(1),
    )(x, g, mod_scale, mod_shift, w_qkv_t, cos, sin_s)


def _qkv_prep_bwd(dqa_t, dka, dva, dqb_t, dkb, dvb, cos, sin_s, name):
    s = dka.shape[0]
    tm = _row_tile(s, 256)
    scale = 1.0 / math.sqrt(HEAD_DIM)
    hw = N_PAIRS * LANES
    t_spec = pl.BlockSpec((hw, tm), lambda i: (0, i))

    def body(dqa_ref, dka_ref, dva_ref, dqb_ref, dkb_ref, dvb_ref, c_ref, s_ref, o_ref):
        cs, sn = c_ref[...], s_ref[...]
        low = _lane() < HEAD_DIM

        def blk(ref, j):
            return ref[:, j * LANES:(j + 1) * LANES].astype(f32)

        def blk_t(ref, j):
            return ref[j * LANES:(j + 1) * LANES, :].T

        def unrope(v):
            return v * cs + _swap_halves(v * sn)

        def fold(ref):
            a, b = blk(ref, 0) + blk(ref, 1), blk(ref, 2) + blk(ref, 3)
            kv0 = a + pltpu.roll(a, HEAD_DIM, axis=1)
            kv1 = b + pltpu.roll(b, HEAD_DIM, axis=1)
            return jnp.where(low, kv0, kv1)

        for j in range(N_PAIRS):
            o_ref[:, j * LANES:(j + 1) * LANES] = (unrope(blk_t(dqa_ref, j)) * scale).astype(bf16)
            o_ref[:, (6 + j) * LANES:(7 + j) * LANES] = (blk_t(dqb_ref, j) * scale).astype(bf16)
            o_ref[:, (10 + j) * LANES:(11 + j) * LANES] = blk(dkb_ref, j).astype(bf16)
            o_ref[:, (14 + j) * LANES:(15 + j) * LANES] = blk(dvb_ref, j).astype(bf16)
        o_ref[:, 4 * LANES:5 * LANES] = unrope(fold(dka_ref)).astype(bf16)
        o_ref[:, 5 * LANES:6 * LANES] = fold(dva_ref).astype(bf16)

    return pl.pallas_call(
        body, name=name, grid=(s // tm,),
        in_specs=[t_spec, _row_spec(tm, hw), _row_spec(tm, hw), t_spec, _row_spec(tm, hw), _row_spec(tm, hw)] + [_row_spec(tm, LANES)] * 2,
        out_specs=_row_spec(tm, QKV_W), out_shape=jax.ShapeDtypeStruct((s, QKV_W), bf16), compiler_params=_params(1),
    )(dqa_t, dka, dva, dqb_t, dkb, dvb, cos, sin_s)


def _cumsum_rows(v, reverse=False):
    n = v.shape[0]
    row = lax.broadcasted_iota(jnp.int32, v.shape, 0)
    sh = 1
    while sh < n:
        if reverse:
            v = v + jnp.where(row < n - sh, pltpu.roll(v, n - sh, axis=0), 0.0)
        else:
            v = v + jnp.where(row >= sh, pltpu.roll(v, sh, axis=0), 0.0)
        sh *= 2
    return v


def _log_sigmoid(z):
    return jnp.minimum(z, 0.0) - jnp.log1p(jnp.exp(-jnp.abs(z)))


def _forget_prep(h, w_f_t, bf_row, name):
    s = h.shape[0]

    def body(h_ref, w_ref, b_ref, f_ref, cb_ref):
        fl = lax.dot_general(h_ref[...], w_ref[...], _NT, preferred_element_type=f32)
        f_ref[...] = fl
        cum = _cumsum_rows(_log_sigmoid(fl + b_ref[...]))
        for hd in range(N_HEADS):
            cb_ref[:, hd * LANES:(hd + 1) * LANES] = jnp.broadcast_to(cum[:, hd:hd + 1], (s, LANES))

    return pl.pallas_call(
        body, name=name,
        out_shape=[jax.ShapeDtypeStruct((s, LANES), f32), jax.ShapeDtypeStruct((s, N_HEADS * LANES), f32)],
        compiler_params=_params(),
    )(h, w_f_t, bf_row)


def _forget_prep_bwd(rs, dcs, fl, bf_row, name):
    s = fl.shape[0]

    def body(r_ref, c_ref, f_ref, b_ref, df_ref, db_ref):
        eye = (lax.broadcasted_iota(jnp.int32, (N_HEADS, LANES), 0) == lax.broadcasted_iota(jnp.int32, (N_HEADS, LANES), 1)).astype(f32)
        dcum = lax.dot_general(r_ref[...], eye, _TN, precision=lax.Precision.HIGHEST, preferred_element_type=f32)
        for h in range(N_HEADS):
            dcum = dcum - jnp.where(_lane() == h, jnp.sum(c_ref[:, h * LANES:(h + 1) * LANES], axis=1, keepdims=True), 0.0)
        dlf = _cumsum_rows(dcum, reverse=True)
        z = f_ref[...] + b_ref[...]
        df = jnp.where(_lane() < N_HEADS, dlf * jax.nn.sigmoid(-z), 0.0)
        df_ref[...] = df.astype(bf16)
        db_ref[...] = jnp.zeros_like(db_ref)
        db_ref[0:1, :] = jnp.sum(df, axis=0, keepdims=True)

    return pl.pallas_call(
        body, name=name,
        out_shape=[jax.ShapeDtypeStruct((s, LANES), bf16), jax.ShapeDtypeStruct((8, LANES), f32)], compiler_params=_params(),
    )(rs, dcs, fl, bf_row)


def _tile_mask(n_keys, n_queries, off, window):
    shape = (n_keys, n_queries)
    d = lax.broadcasted_iota(jnp.int32, shape, 1) - lax.broadcasted_iota(jnp.int32, shape, 0) + off
    valid = d >= 0
    return jnp.logical_and(valid, d < window) if window else valid


def _wide(v, t):
    return jnp.concatenate([v] * (t // LANES), axis=1)


def _attn_fwd(q, k, v, name, *, cum_b=None, sink_rows=None, window=None, t=256):
    s = q.shape[0]
    t = _row_tile(s, t)
    fox, has_sink = cum_b is not None, sink_rows is not None
    assert not window or (window % LANES == 0 and LANES + window <= s)

    def body(*refs):
        q_ref, k_ref, v_ref = refs[:3]
        rest = list(refs[3:])
        cb_ref = rest.pop(0) if fox else None
        sink_ref = rest.pop(0) if has_sink else None
        o_ref, lse_ref = rest
        i = pl.program_id(1)
        low = _lane() < HEAD_DIM
        top = lax.broadcasted_iota(jnp.int32, (LANES, 1), 0) < HEAD_DIM
        q2 = q_ref[...]
        zero = jnp.zeros_like(q2)
        qms = (jnp.where(low, q2, zero), jnp.where(low, zero, q2))

        def tile(k0, n_keys, off, carry, masked, queries=slice(0, t)):
            nq = queries.stop - queries.start
            kblk, vblk = k_ref[pl.ds(k0, n_keys), :], v_ref[pl.ds(k0, n_keys), :]
            valid = _tile_mask(n_keys, nq, off, window) if masked else None
            ones = jnp.ones_like(vblk)
            vs = tuple(jnp.where(_lane() == L_ROW[h], ones, vblk) for h in range(2))

            def scores(h):
                return lax.dot_general(kblk, qms[h][queries], _NT, preferred_element_type=f32)

            def softmax(h, sc):
                m = carry[h][0]
                if fox:
                    sc = sc - _wide(cb_ref[pl.ds(k0, n_keys), h * LANES:(h + 1) * LANES], nq)
                if masked:
                    sc = jnp.where(valid, sc, NEG)
                m_new = jnp.maximum(m, jnp.max(sc, axis=0, keepdims=True))
                return m_new, jnp.exp(m - m_new), jnp.exp(sc - m_new).astype(bf16)

            def update(h, m_new, alpha, p):
                return m_new, alpha * carry[h][1] + lax.dot_general(vs[h], p, _TN, preferred_element_type=f32)

            if window:
                return tuple(update(h, *softmax(h, scores(h))) for h in range(2))
            scs = [scores(h) for h in range(2)]
            stats = [softmax(h, scs[h]) for h in range(2)]
            return tuple(update(h, *stats[h]) for h in range(2))

        def start(nq):
            if has_sink:
                row = lax.broadcasted_iota(jnp.int32, (LANES, nq), 0)
                return tuple((_wide(sink_ref[h:h + 1, :], nq), (row == L_ROW[h]).astype(f32)) for h in range(2))
            return tuple((jnp.full((1, nq), NEG, f32), jnp.zeros((LANES, nq), f32)) for h in range(2))

        def finish(carry, queries):
            (m0, a0), (m1, a1) = carry
            l0, l1 = a0[L_ROW[0]:L_ROW[0] + 1, :], a1[L_ROW[1]:L_ROW[1] + 1, :]
            o_t = jnp.where(top, a0 * (1.0 / l0), a1 * (1.0 / l1))
            o_ref[queries, :] = o_t.T.astype(bf16)
            lse_ref[0:1, queries] = m0 + jnp.log(l0)
            lse_ref[1:2, queries] = m1 + jnp.log(l1)

        if window:
            for c in range(t // LANES):
                queries = slice(c * LANES, (c + 1) * LANES)
                q0 = i * t + c * LANES
                k0 = pl.multiple_of(jnp.maximum(q0 - window, 0), LANES)
                finish(tile(k0, LANES + window, q0 - k0, start(LANES), True, queries), queries)
        else:
            carry = lax.fori_loop(0, i, lambda kb, c: tile(pl.multiple_of(kb * t, t), t, 0, c, False), start(t))
            finish(tile(pl.multiple_of(i * t, t), t, 0, carry, True), slice(0, t))

    q_spec = pl.BlockSpec((t, LANES), lambda j, i: (i, j))
    kv_spec = pl.BlockSpec((s, LANES), lambda j, i: (0, j))
    in_specs, args = [q_spec, kv_spec, kv_spec], [q, k, v]
    if fox:
        in_specs += [pl.BlockSpec((s, 2 * LANES), lambda j, i: (0, j))]
        args += [cum_b]
    if has_sink:
        in_specs += [pl.BlockSpec((None, 2, LANES), lambda j, i: (j, 0, 0))]
        args += [sink_rows.reshape(N_PAIRS, 2, LANES)]
    return pl.pallas_call(
        body, name=name, grid=(N_PAIRS, s // t), in_specs=in_specs,
        out_specs=[q_spec, pl.BlockSpec((None, 2, t), lambda j, i: (j, 0, i))],
        out_shape=[jax.ShapeDtypeStruct((s, N_PAIRS * LANES), bf16), jax.ShapeDtypeStruct((N_PAIRS, 2, s), f32)],
        compiler_params=_params(2),
    )(*args)


def _branch_dgrad_delta(db, w, o, name, *, lse=None, sink_rows=None, after=None):
    s, hw = o.shape
    tm = _row_tile(s, 512)
    has_sink = sink_rows is not None
    extra = [] if after is None else [after]

    def body(*refs):
        db_ref, w_ref, o_ref = refs[:3]
        outs = refs[3 + (2 if has_sink else 0) + len(extra):]
        do_ref, dl_ref = outs[:2]
        if has_sink:
            lse_ref, sink_ref = refs[3:5]
            ds_ref = outs[2]

            @pl.when(pl.program_id(0) == 0)
            def _():
                ds_ref[...] = jnp.zeros_like(ds_ref)
        do = lax.dot_general(db_ref[...], w_ref[...], _NT, preferred_element_type=f32).astype(bf16)
        do_ref[...] = do
        for j in range(N_PAIRS):
            cols = slice(j * LANES, (j + 1) * LANES)
            prod_t = (do[:, cols].astype(f32) * o_ref[:, cols].astype(f32)).T
            for h in range(2):
                dl = jnp.sum(prod_t[h * HEAD_DIM:(h + 1) * HEAD_DIM, :], axis=0, keepdims=True)
                dl_ref[j, h:h + 1, :] = dl
                if has_sink:
                    r = 2 * j + h
                    p_sink = jnp.exp(sink_ref[r:r + 1, 0:1] - lse_ref[j, h:h + 1, :])
                    ds_ref[r:r + 1, :] += -jnp.sum(p_sink * dl, axis=1, keepdims=True)

    rows_spec = pl.BlockSpec((N_PAIRS, 2, tm), lambda i: (0, 0, i))
    in_specs = [_row_spec(tm, db.shape[1]), pl.BlockSpec(w.shape, lambda i: (0, 0)), _row_spec(tm, hw)]
    args = [db, w, o]
    out_specs = [_row_spec(tm, hw), rows_spec]
    out_shape = [jax.ShapeDtypeStruct((s, hw), bf16), jax.ShapeDtypeStruct((N_PAIRS, 2, s), f32)]
    if has_sink:
        in_specs += [rows_spec, _vec_spec(LANES, N_HEADS)]
        args += [lse, sink_rows]
        out_specs += [_vec_spec(LANES, N_HEADS)]
        out_shape += [jax.ShapeDtypeStruct((N_HEADS, LANES), f32)]
    return pl.pallas_call(
        body, name=name, grid=(s // tm,), in_specs=in_specs + [pl.BlockSpec(memory_space=pl.ANY)] * len(extra),
        out_specs=out_specs, out_shape=out_shape, compiler_params=_params(1),
    )(*args, *extra)


def _attn_bwd(q, k, v, do, lse, delta, name, *, cum_b=None, window=None, t=256):
    s = q.shape[0]
    t = _row_tile(s, t)
    nblk = s // t
    fox = cum_b is not None
    assert not window or (window % LANES == 0 and LANES + window <= s)

    def body(*refs):
        k_ref, v_ref, q_ref, do_ref, lse_ref, dl_ref = refs[:6]
        rest = list(refs[6:])
        cb_ref = rest.pop(0) if fox else None
        dq_ref, dk_ref, dv_ref = rest[:3]
        dcs_ref, rs_ref = (rest[3], rest[4]) if fox else (None, None)
        dk_acc, dv_acc = rest[-2:]
        b = pl.program_id(1)
        k0 = pl.multiple_of(b * t, t)

        @pl.when(b == 0)
        def _():
            dq_ref[...] = jnp.zeros_like(dq_ref)
            if fox:
                rs_ref[...] = jnp.zeros_like(rs_ref)

        dk_acc[...] = jnp.zeros_like(dk_acc)
        dv_acc[...] = jnp.zeros_like(dv_acc)
        if fox:
            dcs_ref[...] = jnp.zeros_like(dcs_ref)
        low = _lane() < HEAD_DIM
        top = lax.broadcasted_iota(jnp.int32, (LANES, 1), 0) < HEAD_DIM
        kblk, vblk = k_ref[...], v_ref[...]
        k_t = kblk.astype(f32).T.astype(bf16)
        cks = [_wide(cb_ref[pl.ds(k0, t), h * LANES:(h + 1) * LANES], t) for h in range(2)] if fox else None

        def tile(q0, n_queries, off, masked, keys=slice(0, t)):
            cols = pl.ds(q0, n_queries)
            q2, do2 = q_ref[cols, :], do_ref[cols, :]
            zero = jnp.zeros_like(q2)
            valid = _tile_mask(keys.stop - keys.start, n_queries, off, window) if masked else None
            dq_parts = []
            for h in range(2):
                qm = jnp.where(low, q2, zero) if h == 0 else jnp.where(low, zero, q2)
                dom = jnp.where(low, do2, zero) if h == 0 else jnp.where(low, zero, do2)
                sc = lax.dot_general(kblk[keys], qm, _NT, preferred_element_type=f32)
                if fox:
                    sc = sc - cks[h]
                if masked:
                    sc = jnp.where(valid, sc, NEG)
                p = jnp.exp(sc - lse_ref[h:h + 1, cols])
                dp = lax.dot_general(vblk[keys], dom, _NT, preferred_element_type=f32)
                ds = p * (dp - dl_ref[h:h + 1, cols])
                pb, dsb = p.astype(bf16), ds.astype(bf16)
                dv_acc[keys, :] += jnp.dot(pb, dom, preferred_element_type=f32)
                dk_acc[keys, :] += jnp.dot(dsb, qm, preferred_element_type=f32)
                dq_parts.append(jnp.dot(k_t[:, keys], dsb, preferred_element_type=f32))
                if fox:
                    dcs_ref[:, h * LANES:(h + 1) * LANES] += sum(ds[:, g * LANES:(g + 1) * LANES] for g in range(t // LANES))
                    rs_ref[h:h + 1, cols] += jnp.sum(ds, axis=0, keepdims=True)
            dq_ref[:, cols] += jnp.where(top, dq_parts[0], dq_parts[1])

        def later_block(qb, carry):
            tile(pl.multiple_of(qb * t, t), t, 0, False)
            return carry

        if window:
            for c in range(t // LANES):
                first = b * t + c * LANES
                q0 = pl.multiple_of(jnp.minimum(first, s - (LANES + window)), LANES)
                tile(q0, LANES + window, q0 - first, True, slice(c * LANES, (c + 1) * LANES))
        else:
            tile(k0, t, 0, True)
            lax.fori_loop(b + 1, nblk, later_block, 0)
        dk_ref[...] = dk_acc[...].astype(bf16)
        dv_ref[...] = dv_acc[...].astype(bf16)

    kv_spec = pl.BlockSpec((t, LANES), lambda j, b: (b, j))
    seq_spec = pl.BlockSpec((s, LANES), lambda j, b: (0, j))
    rows_spec = pl.BlockSpec((None, 2, s), lambda j, b: (j, 0, 0))
    hw = N_PAIRS * LANES
    in_specs, args = [kv_spec, kv_spec, seq_spec, seq_spec, rows_spec, rows_spec], [k, v, q, do, lse, delta]
    out_specs = [pl.BlockSpec((LANES, s), lambda j, b: (j, 0)), kv_spec, kv_spec]
    out_shape = [jax.ShapeDtypeStruct((hw, s), f32), jax.ShapeDtypeStruct((s, hw), bf16), jax.ShapeDtypeStruct((s, hw), bf16)]
    if fox:
        in_specs += [pl.BlockSpec((s, 2 * LANES), lambda j, b: (0, j))]
        args += [cum_b]
        out_specs += [pl.BlockSpec((t, 2 * LANES), lambda j, b: (b, j)), rows_spec]
        out_shape += [jax.ShapeDtypeStruct((s, N_HEADS * LANES), f32), jax.ShapeDtypeStruct((N_PAIRS, 2, s), f32)]
    return pl.pallas_call(
        body, name=name, grid=(N_PAIRS, nblk), in_specs=in_specs, out_specs=out_specs, out_shape=out_shape,
        scratch_shapes=[pltpu.VMEM((t, LANES), f32)] * 2, compiler_params=_params(2),
    )(*args)


def _branch_merge(o_a, o_b, w_a, w_b, gl, name):
    s, k = o_a.shape
    d = w_a.shape[1]
    tm = _row_tile(s, 1024)

    def body(oa_ref, ob_ref, wa_ref, wb_ref, g_ref, ba_ref, bb_ref, m_ref):
        ba = jnp.dot(oa_ref[...], wa_ref[...], preferred_element_type=f32)
        bb = jnp.dot(ob_ref[...], wb_ref[...], preferred_element_type=f32)
        g0, g1 = jax.nn.sigmoid(g_ref[:, :d].astype(f32)), jax.nn.sigmoid(g_ref[:, d:].astype(f32))
        ba_ref[...] = ba.astype(bf16)
        bb_ref[...] = bb.astype(bf16)
        m_ref[...] = (g0 * ba + g1 * bb).astype(bf16)

    whole = pl.BlockSpec((k, d), lambda i: (0, 0))
    return pl.pallas_call(
        body, name=name, grid=(s // tm,),
        in_specs=[_row_spec(tm, k), _row_spec(tm, k), whole, whole, _row_spec(tm, 2 * d)],
        out_specs=[_row_spec(tm, d)] * 3, out_shape=[jax.ShapeDtypeStruct((s, d), bf16)] * 3, compiler_params=_params(1),
    )(o_a, o_b, w_a, w_b, gl)


def _out_dgrad_merge_bwd(dy, w_out, ba, bb, gl, name):
    s, d = ba.shape
    tm = _row_tile(s, 512)

    def body(dy_ref, w_ref, a_ref, b_ref, g_ref, da_ref, db_ref, dg_ref):
        dmv = lax.dot_general(dy_ref[...], w_ref[...], _NT, preferred_element_type=f32)
        g0, g1 = jax.nn.sigmoid(g_ref[:, :d].astype(f32)), jax.nn.sigmoid(g_ref[:, d:].astype(f32))
        da_ref[...] = (dmv * g0).astype(bf16)
        db_ref[...] = (dmv * g1).astype(bf16)
        dg_ref[:, :d] = (dmv * a_ref[...].astype(f32) * (g0 * (1.0 - g0))).astype(bf16)
        dg_ref[:, d:] = (dmv * b_ref[...].astype(f32) * (g1 * (1.0 - g1))).astype(bf16)

    return pl.pallas_call(
        body, name=name, grid=(s // tm,),
        in_specs=[_row_spec(tm, dy.shape[1]), pl.BlockSpec(w_out.shape, lambda i: (0, 0))] + [_row_spec(tm, d)] * 2
        + [_row_spec(tm, 2 * d)],
        out_specs=[_row_spec(tm, d)] * 2 + [_row_spec(tm, 2 * d)],
        out_shape=[jax.ShapeDtypeStruct((s, d), bf16)] * 2 + [jax.ShapeDtypeStruct((s, 2 * d), bf16)],
        compiler_params=_params(1),
    )(dy, w_out, ba, bb, gl)


GLU_TILE = 256


def _ffn_in_swiglu(h, w_t, name):
    s, d = h.shape
    f = w_t.shape[0] // 2
    tm = _row_tile(s, 2048)
    tg = GLU_TILE
    nb = f // tg

    def body(h_ref, wg_ref, wu_ref, g_ref, u_ref, act_ref):
        hv = h_ref[...]
        g = lax.dot_general(hv, wg_ref[...], _NT, preferred_element_type=f32)
        u = lax.dot_general(hv, wu_ref[...], _NT, preferred_element_type=f32)
        g_ref[...] = g.astype(bf16)
        u_ref[...] = u.astype(bf16)
        act_ref[...] = (g * jax.nn.sigmoid(g) * u).astype(bf16)

    col = pl.BlockSpec((tm, tg), lambda i, j: (i, j))
    return pl.pallas_call(
        body, name=name, grid=(s // tm, nb),
        in_specs=[pl.BlockSpec((tm, d), lambda i, j: (i, 0)), pl.BlockSpec((tg, d), lambda i, j: (j, 0)),
                  pl.BlockSpec((tg, d), lambda i, j: (j + nb, 0))],
        out_specs=[col] * 3, out_shape=[jax.ShapeDtypeStruct((s, f), bf16)] * 3, compiler_params=_params(2),
    )(h, w_t, w_t)


def _ffn_out_dgrad_swiglu(dy, w_out, g, u, name):
    s, d = dy.shape
    f = g.shape[1]
    tm = _row_tile(s, 2048)
    tg = GLU_TILE

    def body(dy_ref, w_ref, g_ref, u_ref, dg_ref, du_ref):
        dv = lax.dot_general(dy_ref[...], w_ref[...], _NT, preferred_element_type=f32)
        gv, uv = g_ref[...].astype(f32), u_ref[...].astype(f32)
        sg = jax.nn.sigmoid(gv)
        dg_ref[...] = (dv * uv * (sg * (1.0 + gv * (1.0 - sg)))).astype(bf16)
        du_ref[...] = (dv * (gv * sg)).astype(bf16)

    col = pl.BlockSpec((tm, tg), lambda i, j: (i, j))
    return pl.pallas_call(
        body, name=name, grid=(s // tm, f // tg),
        in_specs=[pl.BlockSpec((tm, d), lambda i, j: (i, 0)), pl.BlockSpec((tg, d), lambda i, j: (j, 0)), col, col],
        out_specs=[col] * 2, out_shape=[jax.ShapeDtypeStruct((s, f), bf16)] * 2, compiler_params=_params(2),
    )(dy, w_out, g, u)


def _wgrad_stack(parts, h, name):
    s, m = parts[0].shape
    d = h.shape[1]
    tm = 256
    nb = m // tm
    n = len(parts)

    def body(*refs):
        i = pl.program_id(0)
        for p in range(n):
            @pl.when(i // nb == p)
            def _(p=p):
                refs[n + 1][...] = lax.dot_general(refs[p][...], refs[n][...], _TN, preferred_element_type=f32).astype(bf16)

    a_specs = [pl.BlockSpec((s, tm), lambda i, p=p: (0, jnp.clip(i - p * nb, 0, nb - 1))) for p in range(n)]
    return pl.pallas_call(
        body, name=name, grid=(n * nb,), in_specs=a_specs + [pl.BlockSpec((s, d), lambda i: (0, 0))],
        out_specs=pl.BlockSpec((tm, d), lambda i: (i, 0)),
        out_shape=jax.ShapeDtypeStruct((n * m, d), bf16), compiler_params=_params(1),
    )(*parts, h)


def _ada_wgrad(c_all, d_all, name):
    n, d = c_all.shape
    w = d_all.shape[1]

    def body(c_ref, d_ref, o_ref):
        eye = (lax.broadcasted_iota(jnp.int32, (n, n), 0) == lax.broadcasted_iota(jnp.int32, (n, n), 1)).astype(f32)
        ct = lax.dot_general(c_ref[...], eye, _TN, precision=lax.Precision.HIGHEST, preferred_element_type=f32)
        g = ct[:, 0:1] * d_ref[0:1, :]
        for bi in range(1, n):
            g = g + ct[:, bi:bi + 1] * d_ref[bi:bi + 1, :]
        o_ref[0] = g

    return pl.pallas_call(
        body, name=name, out_shape=jax.ShapeDtypeStruct((1, d, w), f32), compiler_params=_params(),
    )(c_all, d_all)


def _adamw(parts, w, m, v, name, mine=None, me=None):
    r, c = w.shape
    n_parts = parts.shape[0]
    row_tiles = [t for t in range(min(r, 256), 0, -1) if r % t == 0 and (t % 16 == 0 or t == r)]
    if row_tiles:
        tr, tc = row_tiles[0], c
    else:
        tr, tc = r, next(t for t in (256, LANES) if c % t == 0)

    def body(*refs):
        w_ref, m_ref, v_ref, g_ref, d_ref, nm_ref, nv_ref = refs[-7:]
        if mine is None:
            p_ref, = refs[:-7]
        else:
            me_ref, p_ref, own_ref = refs[:-7]

        def part(i):
            if mine is None:
                return p_ref[i].astype(f32)
            return jnp.where(me_ref[0] == i, own_ref[...], p_ref[i]).astype(f32)

        g = part(0)
        for i in range(1, n_parts):
            g = g + part(i)
        mm = ADAM_B1 * m_ref[...] + (1.0 - ADAM_B1) * g
        vv = ADAM_B2 * v_ref[...] + (1.0 - ADAM_B2) * (g * g)
        m_hat = mm / (1.0 - ADAM_B1 ** ADAM_STEP)
        v_hat = vv / (1.0 - ADAM_B2 ** ADAM_STEP)
        g_ref[...] = g
        d_ref[...] = -ADAM_LR * (m_hat / (jnp.sqrt(v_hat) + ADAM_EPS) + ADAM_WD * w_ref[...])
        nm_ref[...] = mm
        nv_ref[...] = vv

    out_shape = [jax.ShapeDtypeStruct((r, c), f32)] * 4
    if mine is None:
        spec = pl.BlockSpec((tr, tc), lambda i, j: (i, j))
        return pl.pallas_call(
            body, name=name, grid=(r // tr, c // tc),
            in_specs=[pl.BlockSpec((n_parts, tr, tc), lambda i, j: (0, i, j))] + [spec] * 3,
            out_specs=[spec] * 4, out_shape=out_shape, compiler_params=_params(2),
        )(parts, w, m, v)
    spec = pl.BlockSpec((tr, tc), lambda i, j, me_ref: (i, j))
    return pl.pallas_call(
        body, name=name, out_shape=out_shape, compiler_params=_params(2),
        grid_spec=pltpu.PrefetchScalarGridSpec(
            num_scalar_prefetch=1, grid=(r // tr, c // tc),
            in_specs=[pl.BlockSpec((n_parts, tr, tc), lambda i, j, me_ref: (0, i, j)),
                      pl.BlockSpec((None, tr, tc), lambda i, j, me_ref: (me_ref[0], i, j))] + [spec] * 3,
            out_specs=[spec] * 4),
    )(me, parts, mine, w, m, v)


def _me():
    return lax.axis_index("x"), lax.axis_index("y"), lax.axis_index("c")


def _all_gather(arrays, name, vmem=False, after=None):
    n = len(arrays)
    space = pltpu.VMEM if vmem else pl.ANY
    extra = [] if after is None else [after]

    def body(*refs):
        ins = refs[:n]
        outs = refs[n + len(extra):2 * n + len(extra)]
        send_sems, recv_sems, local_sems = refs[2 * n + len(extra):]
        x, y, c = _me()
        me, sibling = (x, y, c), (x, y, 1 - c)
        chips = [(1 - x, y), (x, 1 - y), (1 - x, 1 - y)]

        def rows(a, dev):
            return outs[a].at[4 * dev[0] + 2 * dev[1] + dev[2]]

        def copy(a, k, block, to, src=None):
            return pltpu.make_async_remote_copy(
                src_ref=rows(a, block) if src is None else src, dst_ref=rows(a, block),
                send_sem=send_sems.at[a, k], recv_sem=recv_sems.at[a, k], device_id=to, device_id_type=MESH)

        mine = [pltpu.make_async_copy(ins[a], rows(a, me), local_sems.at[a]) for a in range(n)]
        for cp in mine:
            cp.start()
        first = []
        for a in range(n):
            first.append(copy(a, 0, me, sibling, src=ins[a]))
            first += [copy(a, 1 + j, me, (*chip, c), src=ins[a]) for j, chip in enumerate(chips)]
        for cp in first:
            cp.start()
        passed = []
        for j, chip in enumerate(chips):
            for a in range(n):
                copy(a, 1 + j, (*chip, c), me).wait_recv()
                fwd = copy(a, 4 + j, (*chip, c), sibling)
                fwd.start()
                passed.append(fwd)
        for a in range(n):
            copy(a, 0, sibling, me).wait_recv()
            for j, chip in enumerate(chips):
                copy(a, 4 + j, (*chip, 1 - c), me).wait_recv()
        for cp in first + passed:
            cp.wait_send()
        for cp in mine:
            cp.wait()

    outs = pl.pallas_call(
        body, name=name,
        in_specs=[pl.BlockSpec(memory_space=space)] * n + [pl.BlockSpec(memory_space=pl.ANY)] * len(extra),
        out_specs=[pl.BlockSpec(memory_space=space)] * n,
        out_shape=[jax.ShapeDtypeStruct((N_DEV,) + a.shape, a.dtype) for a in arrays],
        scratch_shapes=[pltpu.SemaphoreType.DMA((n, 7)), pltpu.SemaphoreType.DMA((n, 7)), pltpu.SemaphoreType.DMA((n,))],
        compiler_params=pltpu.CompilerParams(vmem_limit_bytes=VMEM_LIMIT),
    )(*arrays, *extra)
    return list(outs)


def _gather_prologue(c, w_ada, b_mine, w_in_t, name):
    n_dev, d = N_DEV, c.shape[1]
    ada_w = w_ada.shape[1]

    def body(c_ref, w_ref, b_ref, win_ref, call_ref, ada_ref, gin_ref, cols_ref, send_sems, recv_sems, local_sems):
        x, y, cc = _me()
        me, sibling = (x, y, cc), (x, y, 1 - cc)
        chips = [(1 - x, y), (x, 1 - y), (1 - x, 1 - y)]
        outs = (call_ref, ada_ref, gin_ref)

        def rows(a, dev):
            return outs[a].at[4 * dev[0] + 2 * dev[1] + dev[2]]

        def copy(a, k, block, to, src=None):
            return pltpu.make_async_remote_copy(
                src_ref=rows(a, block) if src is None else src, dst_ref=rows(a, block),
                send_sem=send_sems.at[a, k], recv_sem=recv_sems.at[a, k], device_id=to, device_id_type=MESH)

        def begin(a, src):
            own = pltpu.make_async_copy(src, rows(a, me), local_sems.at[a])
            sends = [copy(a, 0, me, sibling, src=src)] + [copy(a, 1 + j, me, (*chip, cc), src=src) for j, chip in enumerate(chips)]
            for cp in [own] + sends:
                cp.start()
            return own, sends

        def finish(a, own, sends):
            passed = []
            for j, chip in enumerate(chips):
                copy(a, 1 + j, (*chip, cc), me).wait_recv()
                passed.append(copy(a, 4 + j, (*chip, cc), sibling))
                passed[-1].start()
            copy(a, 0, sibling, me).wait_recv()
            for j, chip in enumerate(chips):
                copy(a, 4 + j, (*chip, 1 - cc), me).wait_recv()
            for cp in sends + passed:
                cp.wait_send()
            own.wait()

        finish(0, *begin(0, c_ref))
        cols_ref[...] = (jnp.dot(call_ref[:, 0, :].astype(bf16), w_ref[...].astype(bf16), preferred_element_type=f32)
                         + b_ref[...])
        finish(1, *begin(1, cols_ref))
        finish(2, *begin(2, win_ref))

    vmem, hbm = pl.BlockSpec(memory_space=pltpu.VMEM), pl.BlockSpec(memory_space=pl.ANY)
    return pl.pallas_call(
        body, name=name, in_specs=[vmem, vmem, vmem, hbm], out_specs=[vmem, vmem, hbm],
        out_shape=[jax.ShapeDtypeStruct((n_dev, 1, d), f32), jax.ShapeDtypeStruct((n_dev, n_dev, ada_w), f32),
                   jax.ShapeDtypeStruct((n_dev,) + w_in_t.shape, w_in_t.dtype)],
        scratch_shapes=[pltpu.VMEM((n_dev, ada_w), f32), pltpu.SemaphoreType.DMA((3, 7)), pltpu.SemaphoreType.DMA((3, 7)),
                        pltpu.SemaphoreType.DMA((3,))],
        compiler_params=pltpu.CompilerParams(vmem_limit_bytes=VMEM_LIMIT),
    )(c, w_ada, b_mine, w_in_t)


_FLIPS = ((0, 0, 1), (1, 0, 0), (0, 1, 0), (1, 1, 0), (1, 0, 1), (0, 1, 1), (1, 1, 1))
_HBM = pl.BlockSpec(memory_space=pltpu.HBM)
_SEM = pl.BlockSpec(memory_space=pltpu.SEMAPHORE)


def _exchange_copies(scatter, srcs, lands, send_sems, recv_sems):
    x, y, c = _me()
    me_row = 4 * x + 2 * y + c
    out = []
    for k, (fx, fy, fc) in enumerate(_FLIPS):
        peer = (x ^ fx, y ^ fy, c ^ fc)
        peer_row = 4 * peer[0] + 2 * peer[1] + peer[2]
        for a in range(len(srcs)):
            out.append(pltpu.make_async_remote_copy(
                src_ref=srcs[a].at[peer_row] if scatter else srcs[a], dst_ref=lands[a].at[me_row],
                send_sem=send_sems.at[7 * a + k], recv_sem=recv_sems.at[7 * a + k], device_id=peer, device_id_type=MESH))
    return out


def _exchange_start(arrays, scatter, name, after=None):
    n = len(arrays)
    lands = [lax.empty(a.shape if scatter else (N_DEV,) + a.shape, a.dtype) for a in arrays]
    extra = [] if after is None else [after]

    def body(*refs):
        srcs, zones = refs[:n], refs[n:2 * n]
        send_sems, recv_sems = refs[2 * n + len(extra)], refs[2 * n + len(extra) + 1]
        token = refs[-1]
        for cp in _exchange_copies(scatter, srcs, zones, send_sems, recv_sems):
            cp.start()
        token[...] = jnp.zeros_like(token)

    thru = [pltpu.HBM(a.shape, a.dtype) for a in list(arrays) + lands]
    outs = pl.pallas_call(
        body, name=name,
        out_shape=(pltpu.SemaphoreType.DMA((7 * n,)), pltpu.SemaphoreType.DMA((7 * n,)), *thru, jax.ShapeDtypeStruct((8, LANES), f32)),
        in_specs=[_HBM] * (2 * n) + [pl.BlockSpec(memory_space=pl.ANY)] * len(extra),
        out_specs=(_SEM, _SEM, *[_HBM] * (2 * n), pl.BlockSpec(memory_space=pltpu.VMEM)),
        input_output_aliases={i: 2 + i for i in range(2 * n)},
        compiler_params=pltpu.CompilerParams(has_side_effects=pltpu.SideEffectType.DATAFLOW_SIDE_EFFECTING),
    )(*[pltpu.with_memory_space_constraint(a, pltpu.HBM) for a in list(arrays) + lands], *extra)
    return dict(n=n, scatter=scatter, sems=outs[:2], srcs=outs[2:2 + n], lands=outs[2 + n:2 + 2 * n], token=outs[-1])


def _exchange_wait(handle, after, name):
    n, scatter = handle["n"], handle["scatter"]

    def body(*refs):
        srcs, zones = refs[:n], refs[n:2 * n]
        send_sems, recv_sems = refs[2 * n], refs[2 * n + 1]
        for cp in _exchange_copies(scatter, srcs, zones, send_sems, recv_sems):
            cp.wait_send()
            cp.wait_recv()

    thru = [pltpu.HBM(a.shape, a.dtype) for a in list(handle["srcs"]) + list(handle["lands"])]
    outs = pl.pallas_call(
        body, name=name, out_shape=tuple(thru),
        in_specs=[_HBM] * (2 * n) + [_SEM, _SEM, pl.BlockSpec(memory_space=pl.ANY)], out_specs=tuple([_HBM] * (2 * n)),
        input_output_aliases={i: i for i in range(2 * n)},
        compiler_params=pltpu.CompilerParams(has_side_effects=pltpu.SideEffectType.DATAFLOW_SIDE_EFFECTING),
    )(*handle["srcs"], *handle["lands"], *handle["sems"], after)
    return list(outs[n:])


def _cols_from_shards(g):
    return jnp.transpose(g, (1, 0, 2)).reshape(g.shape[1], -1)


def _shards_from_cols(a):
    return jnp.transpose(a.reshape(a.shape[0], N_DEV, -1), (1, 0, 2))


def _local_step(x, positions, ada, g_pre_mix, g_post_mix, b_f, sinks, g_pre_ffn, g_post_ffn, target,
                w_in_t, mix_weights, ffn_weights, on_grads):
    s, d = x.shape
    row = lambda v: v.reshape(1, -1)
    shift_m, scale_m, gate_m, shift_f, scale_f, gate_f = (ada[i:i + 1] for i in range(6))
    w_gate_t, w_qkv_t = w_in_t[F_OFF + N_HEADS:], w_in_t[:QKV_W]
    w_f_t = jnp.pad(w_in_t[F_OFF:F_OFF + N_HEADS], ((0, LANES - N_HEADS), (0, 0)))
    bf_row = jnp.pad(row(b_f), ((0, 0), (0, LANES - N_HEADS)))
    sink_rows = jnp.broadcast_to(sinks.reshape(N_HEADS, 1).astype(f32), (N_HEADS, LANES))
    inv_freq = 1.0 / (ROPE_THETA ** (jnp.arange(0, HEAD_DIM, 2, dtype=f32) / HEAD_DIM))
    cos, sin_s = _rope_tables(positions.reshape(s, 1), jnp.tile(inv_freq, 4).reshape(1, LANES), "rope_tables")

    h1, qa, ka, va, qb, kb, vb = _prenorm_proj_qkv(x, row(g_pre_mix), scale_m, shift_m, w_qkv_t, cos, sin_s, "prenorm_proj_qkv")
    gl = _matmul(h1, w_gate_t, "nt", bf16, "proj_gate")
    fl, cum_b = _forget_prep(h1, w_f_t, bf_row, "proj_forget_prep")
    o_a, lse_a = _attn_fwd(qa, ka, va, "swa_fwd", sink_rows=sink_rows, window=WINDOW, t=4096)
    o_b, lse_b = _attn_fwd(qb, kb, vb, "fox_fwd", cum_b=cum_b, t=1024)
    everything_before = (gl[:8, :LANES] + o_a[:8, :LANES] + o_b[:8, :LANES]).astype(f32)
    w_branch_a, w_branch_b, w_out = mix_weights(everything_before)
    ba, bb, merged = _branch_merge(o_a, o_b, w_branch_a, w_branch_b, gl, "branch_merge")
    y1, x2, h2 = _out_proj_postnorm_prenorm(merged, w_out, x, row(g_post_mix), gate_m, row(g_pre_ffn), scale_f, shift_f,
                                            "out_proj_norms")

    w_ffn_in_t, w_ffn_out = ffn_weights(h2)
    g_ff, u_ff, act = _ffn_in_swiglu(h2, w_ffn_in_t, "ffn_in_swiglu")
    loss_row, d_out, d_y2, vec_pf = _out_proj_loss_tail(act, w_ffn_out, x2, row(g_post_ffn), gate_f, target, "ffn_out_loss_tail")

    g_w_ffn_out = _matmul(act, d_y2, "tn", bf16, "ffn_out_wgrad")
    dg_ff, du_ff = _ffn_out_dgrad_swiglu(d_y2, w_ffn_out, g_ff, u_ff, "ffn_out_dgrad_swiglu")
    g_w_ffn_in_t = _wgrad_stack([dg_ff, du_ff], h2, "ffn_in_wgrad")
    sent = on_grads(dict(w_ffn_in=g_w_ffn_in_t, w_ffn_out=g_w_ffn_out))
    d_x2, vec_nf, d_y1, vec_pm = _dgrad_prenorm_bwd(
        [(dg_ff, w_ffn_in_t, 0), (du_ff, w_ffn_in_t, 1)], x2, row(g_pre_ffn), scale_f, d_out, "ffn_in_dgrad_norms_bwd",
        after=sent, below=(y1, row(g_post_mix), gate_m))

    g_w_out = _matmul(merged, d_y1, "tn", bf16, "out_proj_wgrad")
    d_ba, d_bb, dgl = _out_dgrad_merge_bwd(d_y1, w_out, ba, bb, gl, "out_proj_dgrad_merge_bwd")
    g_w_branch_a = _matmul(o_a, d_ba, "tn", bf16, "branch_a_wgrad")
    g_w_branch_b = _matmul(o_b, d_bb, "tn", bf16, "branch_b_wgrad")
    sent = on_grads(dict(w_out=g_w_out, w_branch_a=g_w_branch_a, w_branch_b=g_w_branch_b))
    d_oa, delta_a, d_sink = _branch_dgrad_delta(d_ba, w_branch_a, o_a, "branch_a_dgrad_delta", lse=lse_a,
                                                sink_rows=sink_rows, after=sent)
    d_ob, delta_b = _branch_dgrad_delta(d_bb, w_branch_b, o_b, "branch_b_dgrad_delta", after=sent)
    dqa_t, dka, dva = _attn_bwd(qa, ka, va, d_oa, lse_a, delta_a, "swa_bwd", window=WINDOW, t=2048)
    dqb_t, dkb, dvb, dcs, rs = _attn_bwd(qb, kb, vb, d_ob, lse_b, delta_b, "fox_bwd", cum_b=cum_b, t=512)
    dqkv = _qkv_prep_bwd(dqa_t, dka, dva, dqb_t, dkb, dvb, cos, sin_s, "qkv_prep_bwd")
    dfl, vec_bf = _forget_prep_bwd(rs.reshape(N_HEADS, s), dcs, fl, bf_row, "forget_prep_bwd")
    g_w_in_t = jnp.concatenate([_matmul(dqkv, h1, "tn", bf16, "qkv_wgrad"), _matmul(dfl, h1, "tn", bf16, "forget_wgrad")[:N_HEADS],
                                _matmul(dgl, h1, "tn", bf16, "gate_wgrad")], axis=0)
    sent = on_grads(dict(w_in=g_w_in_t))
    grad_x, vec_nm = _dgrad_prenorm_bwd([(dgl, w_gate_t, 0), (dqkv, w_qkv_t, 0), (dfl, w_f_t, 0)], x, row(g_pre_mix),
                                        scale_m, d_x2, "in_proj_dgrad_prenorm_bwd", after=sent)

    d_ada = jnp.concatenate([vec_nm[0], vec_nm[1], vec_pm[0], vec_nf[0], vec_nf[1], vec_pf[0]])
    small = dict(b_ada=d_ada, g_pre_mix=vec_nm[2], g_post_mix=vec_pm[1], g_pre_ffn=vec_nf[2], g_post_ffn=vec_pf[1],
                 b_f=vec_bf[0, :N_HEADS], sinks=d_sink[:, 0], loss=loss_row[0, :1])
    return grad_x, small


_SMALL = (("b_ada", 6144), ("g_pre_mix", 1024), ("g_post_mix", 1024), ("g_pre_ffn", 1024), ("g_post_ffn", 1024),
          ("b_f", 128), ("sinks", 128), ("loss", 128))
_SMALL_ROWS = 88


def _pack_small(vals):
    parts = [jnp.pad(vals[k].reshape(-1).astype(f32), (0, n - vals[k].size)) for k, n in _SMALL]
    flat = jnp.concatenate(parts)
    return jnp.pad(flat, (0, _SMALL_ROWS * LANES - flat.size)).reshape(_SMALL_ROWS, LANES)


def _unpack_small(slab, shapes):
    flat, out, off = slab.reshape(-1), {}, 0
    for k, n in _SMALL:
        size = math.prod(shapes[k])
        out[k] = flat[off:off + size].reshape(shapes[k])
        off += n
    return out


def kernel(x, c, positions, w_ada, b_ada, g_pre_mix, g_post_mix, w_in, b_f, sinks, w_branch_a, w_branch_b, w_out, g_pre_ffn, g_post_ffn, w_ffn_in, w_ffn_out, loss_target, m_w_ada, m_b_ada, m_g_pre_mix, m_g_post_mix, m_w_in, m_b_f, m_sinks, m_w_branch_a, m_w_branch_b, m_w_out, m_g_pre_ffn, m_g_post_ffn, m_w_ffn_in, m_w_ffn_out, v_w_ada, v_b_ada, v_g_pre_mix, v_g_post_mix, v_w_in, v_b_f, v_sinks, v_w_branch_a, v_w_branch_b, v_w_out, v_g_pre_ffn, v_g_post_ffn, v_w_ffn_in, v_w_ffn_out):
    xi, yi, ci = _me()
    me = 4 * xi + 2 * yi + ci
    d = D_MODEL
    ada_w = w_ada.shape[2]

    transposed = ("w_in", "w_ffn_in")
    tr = lambda a: jnp.transpose(a[0])

    b_mine = lax.dynamic_slice(b_ada, (0, me * ada_w), (1, ada_w))
    c_all, ada_all, g_in = _gather_prologue(c, w_ada[0], b_mine, tr(w_in).astype(bf16), "gather_prologue")
    c_all = c_all.reshape(N_DEV, d)
    ada = lax.dynamic_index_in_dim(ada_all, me, axis=1, keepdims=False).reshape(6, d)
    late_mix = [w.astype(bf16) for w in (w_branch_a[0], w_branch_b[0], w_out[0])]
    late_ffn = [w.astype(bf16) for w in (tr(w_ffn_in), w_ffn_out[0])]
    mix_h = _exchange_start(late_mix, False, "gather_mix_start", after=g_in)
    ffn_h = _exchange_start(late_ffn, False, "gather_ffn_start", after=mix_h["token"])

    def mine_into(zone, block):
        return lax.dynamic_update_index_in_dim(zone, block, me, 0)

    def rows_from_shards(g):
        return g.reshape(g.shape[0] * g.shape[1], g.shape[2])

    def mix_weights(after):
        zones = _exchange_wait(mix_h, after, "gather_mix_wait")
        g_ba, g_bb, g_out = (mine_into(z, w) for z, w in zip(zones, late_mix))
        return _cols_from_shards(g_ba), _cols_from_shards(g_bb), rows_from_shards(g_out)

    def ffn_weights(after):
        zones = _exchange_wait(ffn_h, after, "gather_ffn_wait")
        g_fi, g_fo = (mine_into(z, w) for z, w in zip(zones, late_ffn))
        return rows_from_shards(g_fi), rows_from_shards(g_fo)

    row_sharded = ("w_out", "w_ffn_out") + transposed
    in_flight = []

    def on_grads(group):
        sends = [g.reshape(N_DEV, g.shape[0] // N_DEV, g.shape[1]) if nm in row_sharded else _shards_from_cols(g)
                 for nm, g in group.items()]
        handle = _exchange_start(sends, True, "scatter_start_%d" % len(in_flight))
        in_flight.append((list(group), sends, handle))
        return handle["token"]

    grad_x, small = _local_step(
        x[0], positions[0], ada + ffn_h["token"][0, 0], g_pre_mix[0], g_post_mix[0], b_f[0], sinks[0], g_pre_ffn[0],
        g_post_ffn[0], loss_target[0], rows_from_shards(g_in), mix_weights, ffn_weights, on_grads)

    ws = dict(w_in=(w_in, m_w_in, v_w_in), w_branch_a=(w_branch_a, m_w_branch_a, v_w_branch_a),
              w_branch_b=(w_branch_b, m_w_branch_b, v_w_branch_b), w_out=(w_out, m_w_out, v_w_out),
              w_ffn_in=(w_ffn_in, m_w_ffn_in, v_w_ffn_in), w_ffn_out=(w_ffn_out, m_w_ffn_out, v_w_ffn_out))
    res = {}

    def finish_group(gi, after):
        names, sends, handle = in_flight[gi]
        zones = _exchange_wait(handle, after, "scatter_wait_%d" % gi)
        for nm, zone, sent in zip(names, zones, sends):
            w, m, v = (tr(a) if nm in transposed else a[0] for a in ws[nm])
            out = _adamw(zone, w, m, v, "adamw_" + nm, mine=sent, me=me.reshape(1).astype(jnp.int32))
            after = out[0]
            res[nm] = [jnp.transpose(o) for o in out] if nm in transposed else out
        return after

    done = finish_group(1, finish_group(0, grad_x))

    slab_all, = _all_gather([_pack_small(small)], "gather_small", vmem=True, after=done)
    small_w = dict(b_ada=b_ada, g_pre_mix=g_pre_mix, g_post_mix=g_post_mix, g_pre_ffn=g_pre_ffn, g_post_ffn=g_post_ffn,
                   b_f=b_f, sinks=sinks, loss=jnp.zeros((1,), f32))
    small_m = dict(b_ada=m_b_ada, g_pre_mix=m_g_pre_mix, g_post_mix=m_g_post_mix, g_pre_ffn=m_g_pre_ffn,
                   g_post_ffn=m_g_post_ffn, b_f=m_b_f, sinks=m_sinks, loss=jnp.zeros((1,), f32))
    small_v = dict(b_ada=v_b_ada, g_pre_mix=v_g_pre_mix, g_post_mix=v_g_post_mix, g_pre_ffn=v_g_pre_ffn,
                   g_post_ffn=v_g_post_ffn, b_f=v_b_f, sinks=v_sinks, loss=jnp.ones((1,), f32))
    shapes = {k: small_w[k].shape for k, _ in _SMALL}
    s_out = _adamw(slab_all, _pack_small(small_w), _pack_small(small_m), _pack_small(small_v), "adamw_small")
    s_grad, s_delta, s_m, s_v = (_unpack_small(o, shapes) for o in s_out)

    d_ada_all = lax.dynamic_slice(slab_all[:, :6144 // LANES, :].reshape(N_DEV, 6144), (0, me * ada_w), (N_DEV, ada_w))
    ada_parts = _ada_wgrad(c_all, d_ada_all, "ada_wgrad")

    res["w_ada"] = _adamw(ada_parts, w_ada[0], m_w_ada[0], v_w_ada[0], "adamw_w_ada")
    finish_group(2, res["w_ada"][0])

    order = ["w_ada", "b_ada", "g_pre_mix", "g_post_mix", "w_in", "b_f", "sinks", "w_branch_a", "w_branch_b", "w_out",
             "g_pre_ffn", "g_post_ffn", "w_ffn_in", "w_ffn_out"]
    outs = [s_grad["loss"].reshape(()), grad_x[None]]
    for which, small_o in enumerate((s_grad, s_delta, s_m, s_v)):
        for nm in order:
            outs.append(res[nm][which][None] if nm in res else small_o[nm])
    return tuple(outs)
```

```python
import math

import jax
import jax.numpy as jnp
from jax import lax
from jax.experimental import pallas as pl
from jax.experimental.pallas import tpu as pltpu

f32 = jnp.float32
bf16 = jnp.bfloat16

D_MODEL = 1024
HEAD_DIM = 64
N_HEADS = 8
N_PAIRS = 4
QKV_W = 2304
F_OFF = 2304
WINDOW = 128
ROPE_THETA = 10000.0
RMS_EPS = 1e-6
N_DEV = 8
ADAM_LR, ADAM_B1, ADAM_B2, ADAM_EPS, ADAM_WD, ADAM_STEP = 0.001, 0.9, 0.999, 1e-08, 0.01, 10
NEG = -1e30
L_ROW = (HEAD_DIM, 0)
LANES = 128
VMEM_LIMIT = 48 * 1024 * 1024
MESH = pl.DeviceIdType.MESH

_NT = (((1,), (1,)), ((), ()))
_TN = (((0,), (0,)), ((), ()))


def _params(n_grid=0):
    sem = ("arbitrary",) * n_grid if n_grid else None
    return pltpu.CompilerParams(dimension_semantics=sem, vmem_limit_bytes=VMEM_LIMIT)


def _row_tile(s, want):
    t = min(s, want)
    assert s % t == 0, (s, t)
    return t


MATMUL_VMEM_BUDGET = 40 * 1024 * 1024


def _matmul_tiles(m, n, k, a_item, b_item, o_item):
    def tiles(d):
        return [t for t in range(LANES, min(d, 2048) + 1, LANES) if d % t == 0] or [d]

    best = None
    for tm in tiles(m):
        for tn in tiles(n):
            vmem = 2 * (tm * k * a_item + tn * k * b_item + tm * tn * o_item) + tm * tn * 4
            if vmem > MATMUL_VMEM_BUDGET:
                continue
            traffic = m * k * a_item + n * k * b_item * (1 if tn == n else m // tm) + m * n * o_item
            steps = (m // tm) * (n // tn)
            key = (traffic, 0, steps) if steps >= 4 else (traffic, 1, -steps)
            if best is None or key < best[0]:
                best = (key, tm, tn)
    assert best is not None, (m, n, k)
    return best[1], best[2]


def _matmul(a, b, mode, out_dtype, name, after=None):
    if mode == "nn":
        (m, k), n = a.shape, b.shape[1]
    elif mode == "nt":
        (m, k), n = a.shape, b.shape[0]
    else:
        (k, m), n = a.shape, b.shape[1]
    tm, tn = _matmul_tiles(m, n, k, a.dtype.itemsize, b.dtype.itemsize, jnp.dtype(out_dtype).itemsize)
    if mode == "nn":
        a_spec, b_spec, dims = pl.BlockSpec((tm, k), lambda i, j: (i, 0)), pl.BlockSpec((k, tn), lambda i, j: (0, j)), None
    elif mode == "nt":
        a_spec, b_spec, dims = pl.BlockSpec((tm, k), lambda i, j: (i, 0)), pl.BlockSpec((tn, k), lambda i, j: (j, 0)), _NT
    else:
        a_spec, b_spec, dims = pl.BlockSpec((k, tm), lambda i, j: (0, i)), pl.BlockSpec((k, tn), lambda i, j: (0, j)), _TN

    def body(a_ref, b_ref, *rest):
        o_ref = rest[-1]
        av, bv = a_ref[...].astype(bf16), b_ref[...].astype(bf16)
        if dims is None:
            r = jnp.dot(av, bv, preferred_element_type=f32)
        else:
            r = lax.dot_general(av, bv, dims, preferred_element_type=f32)
        o_ref[...] = r.astype(out_dtype)

    extra = [] if after is None else [after]
    return pl.pallas_call(
        body, name=name, grid=(m // tm, n // tn), in_specs=[a_spec, b_spec] + [pl.BlockSpec(memory_space=pl.ANY)] * len(extra),
        out_specs=pl.BlockSpec((tm, tn), lambda i, j: (i, j)),
        out_shape=jax.ShapeDtypeStruct((m, n), out_dtype), compiler_params=_params(2),
    )(a, b, *extra)


def _rstd(v):
    return lax.rsqrt(jnp.mean(v * v, axis=-1, keepdims=True) + RMS_EPS)


def _row_spec(tm, d):
    return pl.BlockSpec((tm, d), lambda i: (i, 0))


def _vec_spec(d, rows=1):
    return pl.BlockSpec((rows, d), lambda i: (0, 0))


def _proj_spec(a, w, tm):
    return [_row_spec(tm, a.shape[1]), pl.BlockSpec(w.shape, lambda i: (0, 0))]


def _out_proj_postnorm_prenorm(a, w, x, g_post, gate, g_pre, scale, shift, name):
    s, d = x.shape
    tm = _row_tile(s, 512)

    def body(a_ref, w_ref, x_ref, gp_ref, gate_ref, g_ref, sc_ref, sh_ref, y_ref, x2_ref, h_ref):
        yv = jnp.dot(a_ref[...], w_ref[...], preferred_element_type=f32)
        y_ref[...] = yv
        x2 = x_ref[...] + gate_ref[...] * (yv * _rstd(yv) * gp_ref[...])
        x2_ref[...] = x2
        h_ref[...] = ((x2 * _rstd(x2) * g_ref[...]) * (1.0 + sc_ref[...]) + sh_ref[...]).astype(bf16)

    return pl.pallas_call(
        body, name=name, grid=(s // tm,), in_specs=_proj_spec(a, w, tm) + [_row_spec(tm, d)] + [_vec_spec(d)] * 5,
        out_specs=[_row_spec(tm, d)] * 3,
        out_shape=[jax.ShapeDtypeStruct((s, d), f32)] * 2 + [jax.ShapeDtypeStruct((s, d), bf16)], compiler_params=_params(1),
    )(a, w, x, g_post, gate, g_pre, scale, shift)


def _rms_bwd(u, v, r):
    return r * u - v * (r * r * r) * jnp.mean(u * v, axis=-1, keepdims=True)


def _out_proj_loss_tail(a, w, x, g, gate, target, name):
    s, d = x.shape
    tm = _row_tile(s, 512)

    def body(a_ref, w_ref, x_ref, g_ref, gate_ref, t_ref, loss_ref, do_ref, dy_ref, vec_ref):
        @pl.when(pl.program_id(0) == 0)
        def _():
            loss_ref[...] = jnp.zeros_like(loss_ref)
            vec_ref[...] = jnp.zeros_like(vec_ref)
        yv = jnp.dot(a_ref[...], w_ref[...], preferred_element_type=f32)
        r = _rstd(yv)
        yn = yv * r
        err = x_ref[...] + gate_ref[...] * (yn * g_ref[...]) - t_ref[...]
        loss_ref[...] += 0.5 * jnp.sum(jnp.mean(err * err, axis=-1, keepdims=True), axis=0, keepdims=True)
        dr = err / d
        do_ref[...] = dr
        dn = dr * gate_ref[...]
        vec_ref[0:1, :] += jnp.sum(dr * (yn * g_ref[...]), axis=0, keepdims=True)
        vec_ref[1:2, :] += jnp.sum(dn * yn, axis=0, keepdims=True)
        dy_ref[...] = _rms_bwd(dn * g_ref[...], yv, r).astype(bf16)

    return pl.pallas_call(
        body, name=name, grid=(s // tm,),
        in_specs=_proj_spec(a, w, tm) + [_row_spec(tm, d)] + [_vec_spec(d)] * 2 + [_row_spec(tm, d)],
        out_specs=[_vec_spec(LANES), _row_spec(tm, d), _row_spec(tm, d), _vec_spec(d, 8)],
        out_shape=[jax.ShapeDtypeStruct((1, LANES), f32), jax.ShapeDtypeStruct((s, d), f32),
                   jax.ShapeDtypeStruct((s, d), bf16), jax.ShapeDtypeStruct((8, d), f32)],
        compiler_params=_params(1),
    )(a, w, x, g, gate, target)


def _dgrad_prenorm_bwd(terms, x, g, scale, dres, name, after=None, below=None):
    s, d = x.shape
    n = len(terms)
    k = sum(a.shape[1] for a, _, _ in terms)
    row_bytes = 2 * (2 * k) + d * (4 + 2 * 4 * 3 + (2 * 4 + 2 * 2 if below else 0))
    tm = next(t for t in (512, 256, 128) if s % t == 0 and 4 * k * d + t * row_bytes <= MATMUL_VMEM_BUDGET)
    extra = [] if after is None else [after]

    def body(*refs):
        a_refs, b_refs = refs[:n], refs[n:2 * n]
        x_ref, g_ref, sc_ref, dr_ref = refs[2 * n:2 * n + 4]
        n_in = 2 * n + 4 + (3 if below else 0) + len(extra)
        dx_ref, vec_ref = refs[n_in], refs[n_in + 1]
        if below:
            y_ref, gp_ref, gate_ref = refs[2 * n + 4:2 * n + 7]
            dy_ref, vec2_ref = refs[n_in + 2], refs[n_in + 3]

        @pl.when(pl.program_id(0) == 0)
        def _():
            vec_ref[...] = jnp.zeros_like(vec_ref)
            if below:
                vec2_ref[...] = jnp.zeros_like(vec2_ref)
        dhv = jnp.dot(a_refs[0][...], b_refs[0][...], preferred_element_type=f32)
        for i in range(1, n):
            dhv = dhv + jnp.dot(a_refs[i][...], b_refs[i][...], preferred_element_type=f32)
        xv = x_ref[...]
        r = _rstd(xv)
        xn = xv * r
        dn = dhv * (1.0 + sc_ref[...])
        vec_ref[0:1, :] += jnp.sum(dhv, axis=0, keepdims=True)
        vec_ref[1:2, :] += jnp.sum(dhv * (xn * g_ref[...]), axis=0, keepdims=True)
        vec_ref[2:3, :] += jnp.sum(dn * xn, axis=0, keepdims=True)
        dx = dr_ref[...] + _rms_bwd(dn * g_ref[...], xv, r)
        dx_ref[...] = dx
        if below:
            yv = y_ref[...]
            ry = _rstd(yv)
            yn = yv * ry
            dny = dx * gate_ref[...]
            vec2_ref[0:1, :] += jnp.sum(dx * (yn * gp_ref[...]), axis=0, keepdims=True)
            vec2_ref[1:2, :] += jnp.sum(dny * yn, axis=0, keepdims=True)
            dy_ref[...] = _rms_bwd(dny * gp_ref[...], yv, ry).astype(bf16)

    in_specs = ([_row_spec(tm, a.shape[1]) for a, _, _ in terms]
                + [pl.BlockSpec((a.shape[1], d), lambda i, r=r: (r, 0)) for a, _, r in terms]
                + [_row_spec(tm, d)] + [_vec_spec(d)] * 2 + [_row_spec(tm, d)])
    out_specs = [_row_spec(tm, d), _vec_spec(d, 8)]
    out_shape = [jax.ShapeDtypeStruct((s, d), f32), jax.ShapeDtypeStruct((8, d), f32)]
    args = [a for a, _, _ in terms] + [b for _, b, _ in terms] + [x, g, scale, dres]
    if below:
        in_specs += [_row_spec(tm, d)] + [_vec_spec(d)] * 2
        out_specs += [_row_spec(tm, d), _vec_spec(d, 8)]
        out_shape += [jax.ShapeDtypeStruct((s, d), bf16), jax.ShapeDtypeStruct((8, d), f32)]
        args += list(below)
    return pl.pallas_call(
        body, name=name, grid=(s // tm,), in_specs=in_specs + [pl.BlockSpec(memory_space=pl.ANY)] * len(extra),
        out_specs=out_specs, out_shape=out_shape, compiler_params=_params(1),
    )(*args, *extra)


def _lane():
    return lax.broadcasted_iota(jnp.int32, (1, LANES), 1)


def _rope_tables(pos_col, inv_freq, name):
    s = pos_col.shape[0]

    def body(p_ref, f_ref, cos_ref, sin_ref):
        ang = p_ref[...].astype(f32) * f_ref[...]
        first_half = (_lane() % HEAD_DIM) < HEAD_DIM // 2
        cos_ref[...] = jnp.cos(ang)
        sn = jnp.sin(ang)
        sin_ref[...] = jnp.where(first_half, -sn, sn)

    return pl.pallas_call(
        body, name=name, out_shape=[jax.ShapeDtypeStruct((s, LANES), f32)] * 2, compiler_params=_params(),
    )(pos_col, inv_freq)


def _swap_halves(v):
    first_half = (_lane() % HEAD_DIM) < HEAD_DIM // 2
    return jnp.where(first_half, pltpu.roll(v, LANES - HEAD_DIM // 2, axis=1), pltpu.roll(v, HEAD_DIM // 2, axis=1))


def _prenorm_proj_qkv(x, g, mod_scale, mod_shift, w_qkv_t, cos, sin_s, name):
    s, d = x.shape
    tm = _row_tile(s, 512)
    scale = 1.0 / math.sqrt(HEAD_DIM)

    def body(x_ref, g_ref, msc_ref, msh_ref, w_ref, c_ref, s_ref, h_ref, qa_ref, ka_ref, va_ref, qb_ref, kb_ref, vb_ref):
        xv = x_ref[...]
        h = ((xv * _rstd(xv) * g_ref[...]) * (1.0 + msc_ref[...]) + msh_ref[...]).astype(bf16)
        h_ref[...] = h
        proj = lax.dot_general(h, w_ref[...], _NT, preferred_element_type=f32)
        cs, sn = c_ref[...], s_ref[...]
        low = _lane() < HEAD_DIM

        def blk(j):
            return proj[:, j * LANES:(j + 1) * LANES]

        def rope(v):
            return v * cs + _swap_halves(v) * sn

        def expand(v):
            other = pltpu.roll(v, HEAD_DIM, axis=1)
            return jnp.where(low, v, other), jnp.where(low, other, v)

        for j in range(N_PAIRS):
            qa_ref[:, j * LANES:(j + 1) * LANES] = (rope(blk(j)) * scale).astype(bf16)
            qb_ref[:, j * LANES:(j + 1) * LANES] = (blk(6 + j) * scale).astype(bf16)
            kb_ref[:, j * LANES:(j + 1) * LANES] = blk(10 + j).astype(bf16)
            vb_ref[:, j * LANES:(j + 1) * LANES] = blk(14 + j).astype(bf16)
        k0, k1 = expand(rope(blk(4)))
        v0, v1 = expand(blk(5))
        for j in range(N_PAIRS):
            ka_ref[:, j * LANES:(j + 1) * LANES] = (k0 if j < 2 else k1).astype(bf16)
            va_ref[:, j * LANES:(j + 1) * LANES] = (v0 if j < 2 else v1).astype(bf16)

    hw = N_PAIRS * LANES
    return pl.pallas_call(
        body, name=name, grid=(s // tm,),
        in_specs=[_row_spec(tm, d)] + [_vec_spec(d)] * 3
        + [pl.BlockSpec((QKV_W, d), lambda i: (0, 0)), _row_spec(tm, LANES), _row_spec(tm, LANES)],
        out_specs=[_row_spec(tm, d)] + [_row_spec(tm, hw)] * 6,
        out_shape=[jax.ShapeDtypeStruct((s, d), bf16)] + [jax.ShapeDtypeStruct((s, hw), bf16)] * 6, compiler_params=_params(1),
    )(x, g, mod_scale, mod_shift, w_qkv_t, cos, sin_s)


def _qkv_prep_bwd(dqa_t, dka, dva, dqb_t, dkb, dvb, cos, sin_s, name):
    s = dka.shape[0]
    tm = _row_tile(s, 256)
    scale = 1.0 / math.sqrt(HEAD_DIM)
    hw = N_PAIRS * LANES
    t_spec = pl.BlockSpec((hw, tm), lambda i: (0, i))

    def body(dqa_ref, dka_ref, dva_ref, dqb_ref, dkb_ref, dvb_ref, c_ref, s_ref, o_ref):
        cs, sn = c_ref[...], s_ref[...]
        low = _lane() < HEAD_DIM

        def blk(ref, j):
            return ref[:, j * LANES:(j + 1) * LANES].astype(f32)

        def blk_t(ref, j):
            return ref[j * LANES:(j + 1) * LANES, :].T

        def unrope(v):
            return v * cs + _swap_halves(v * sn)

        def fold(ref):
            a, b = blk(ref, 0) + blk(ref, 1), blk(ref, 2) + blk(ref, 3)
            kv0 = a + pltpu.roll(a, HEAD_DIM, axis=1)
            kv1 = b + pltpu.roll(b, HEAD_DIM, axis=1)
            return jnp.where(low, kv0, kv1)

        for j in range(N_PAIRS):
            o_ref[:, j * LANES:(j + 1) * LANES] = (unrope(blk_t(dqa_ref, j)) * scale).astype(bf16)
            o_ref[:, (6 + j) * LANES:(7 + j) * LANES] = (blk_t(dqb_ref, j) * scale).astype(bf16)
            o_ref[:, (10 + j) * LANES:(11 + j) * LANES] = blk(dkb_ref, j).astype(bf16)
            o_ref[:, (14 + j) * LANES:(15 + j) * LANES] = blk(dvb_ref, j).astype(bf16)
        o_ref[:, 4 * LANES:5 * LANES] = unrope(fold(dka_ref)).astype(bf16)
        o_ref[:, 5 * LANES:6 * LANES] = fold(dva_ref).astype(bf16)

    return pl.pallas_call(
        body, name=name, grid=(s // tm,),
        in_specs=[t_spec, _row_spec(tm, hw), _row_spec(tm, hw), t_spec, _row_spec(tm, hw), _row_spec(tm, hw)] + [_row_spec(tm, LANES)] * 2,
        out_specs=_row_spec(tm, QKV_W), out_shape=jax.ShapeDtypeStruct((s, QKV_W), bf16), compiler_params=_params(1),
    )(dqa_t, dka, dva, dqb_t, dkb, dvb, cos, sin_s)


def _cumsum_rows(v, reverse=False):
    n = v.shape[0]
    row = lax.broadcasted_iota(jnp.int32, v.shape, 0)
    sh = 1
    while sh < n:
        if reverse:
            v = v + jnp.where(row < n - sh, pltpu.roll(v, n - sh, axis=0), 0.0)
        else:
            v = v + jnp.where(row >= sh, pltpu.roll(v, sh, axis=0), 0.0)
        sh *= 2
    return v


def _log_sigmoid(z):
    return jnp.minimum(z, 0.0) - jnp.log1p(jnp.exp(-jnp.abs(z)))


def _forget_prep(h, w_f_t, bf_row, name):
    s = h.shape[0]

    def body(h_ref, w_ref, b_ref, f_ref, cb_ref):
        fl = lax.dot_general(h_ref[...], w_ref[...], _NT, preferred_element_type=f32)
        f_ref[...] = fl
        cum = _cumsum_rows(_log_sigmoid(fl + b_ref[...]))
        for hd in range(N_HEADS):
            cb_ref[:, hd * LANES:(hd + 1) * LANES] = jnp.broadcast_to(cum[:, hd:hd + 1], (s, LANES))

    return pl.pallas_call(
        body, name=name,
        out_shape=[jax.ShapeDtypeStruct((s, LANES), f32), jax.ShapeDtypeStruct((s, N_HEADS * LANES), f32)],
        compiler_params=_params(),
    )(h, w_f_t, bf_row)


def _forget_prep_bwd(rs, dcs, fl, bf_row, name):
    s = fl.shape[0]

    def body(r_ref, c_ref, f_ref, b_ref, df_ref, db_ref):
        eye = (lax.broadcasted_iota(jnp.int32, (N_HEADS, LANES), 0) == lax.broadcasted_iota(jnp.int32, (N_HEADS, LANES), 1)).astype(f32)
        dcum = lax.dot_general(r_ref[...], eye, _TN, precision=lax.Precision.HIGHEST, preferred_element_type=f32)
        for h in range(N_HEADS):
            dcum = dcum - jnp.where(_lane() == h, jnp.sum(c_ref[:, h * LANES:(h + 1) * LANES], axis=1, keepdims=True), 0.0)
        dlf = _cumsum_rows(dcum, reverse=True)
        z = f_ref[...] + b_ref[...]
        df = jnp.where(_lane() < N_HEADS, dlf * jax.nn.sigmoid(-z), 0.0)
        df_ref[...] = df.astype(bf16)
        db_ref[...] = jnp.zeros_like(db_ref)
        db_ref[0:1, :] = jnp.sum(df, axis=0, keepdims=True)

    return pl.pallas_call(
        body, name=name,
        out_shape=[jax.ShapeDtypeStruct((s, LANES), bf16), jax.ShapeDtypeStruct((8, LANES), f32)], compiler_params=_params(),
    )(rs, dcs, fl, bf_row)


def _tile_mask(n_keys, n_queries, off, window):
    shape = (n_keys, n_queries)
    d = lax.broadcasted_iota(jnp.int32, shape, 1) - lax.broadcasted_iota(jnp.int32, shape, 0) + off
    valid = d >= 0
    return jnp.logical_and(valid, d < window) if window else valid


def _wide(v, t):
    return jnp.concatenate([v] * (t // LANES), axis=1)


def _attn_fwd(q, k, v, name, *, cum_b=None, sink_rows=None, window=None, t=256):
    s = q.shape[0]
    t = _row_tile(s, t)
    fox, has_sink = cum_b is not None, sink_rows is not None
    assert not window or (window % LANES == 0 and LANES + window <= s)

    def body(*refs):
        q_ref, k_ref, v_ref = refs[:3]
        rest = list(refs[3:])
        cb_ref = rest.pop(0) if fox else None
        sink_ref = rest.pop(0) if has_sink else None
        o_ref, lse_ref = rest
        i = pl.program_id(1)
        low = _lane() < HEAD_DIM
        top = lax.broadcasted_iota(jnp.int32, (LANES, 1), 0) < HEAD_DIM
        q2 = q_ref[...]
        zero = jnp.zeros_like(q2)
        qms = (jnp.where(low, q2, zero), jnp.where(low, zero, q2))

        def tile(k0, n_keys, off, carry, masked, queries=slice(0, t)):
            nq = queries.stop - queries.start
            kblk, vblk = k_ref[pl.ds(k0, n_keys), :], v_ref[pl.ds(k0, n_keys), :]
            valid = _tile_mask(n_keys, nq, off, window) if masked else None
            ones = jnp.ones_like(vblk)
            vs = tuple(jnp.where(_lane() == L_ROW[h], ones, vblk) for h in range(2))

            def scores(h):
                return lax.dot_general(kblk, qms[h][queries], _NT, preferred_element_type=f32)

            def softmax(h, sc):
                m = carry[h][0]
                if fox:
                    sc = sc - _wide(cb_ref[pl.ds(k0, n_keys), h * LANES:(h + 1) * LANES], nq)
                if masked:
                    sc = jnp.where(valid, sc, NEG)
                m_new = jnp.maximum(m, jnp.max(sc, axis=0, keepdims=True))
                return m_new, jnp.exp(m - m_new), jnp.exp(sc - m_new).astype(bf16)

            def update(h, m_new, alpha, p):
                return m_new, alpha * carry[h][1] + lax.dot_general(vs[h], p, _TN, preferred_element_type=f32)

            if window:
                return tuple(update(h, *softmax(h, scores(h))) for h in range(2))
            scs = [scores(h) for h in range(2)]
            stats = [softmax(h, scs[h]) for h in range(2)]
            return tuple(update(h, *stats[h]) for h in range(2))

        def start(nq):
            if has_sink:
                row = lax.broadcasted_iota(jnp.int32, (LANES, nq), 0)
                return tuple((_wide(sink_ref[h:h + 1, :], nq), (row == L_ROW[h]).astype(f32)) for h in range(2))
            return tuple((jnp.full((1, nq), NEG, f32), jnp.zeros((LANES, nq), f32)) for h in range(2))

        def finish(carry, queries):
            (m0, a0), (m1, a1) = carry
            l0, l1 = a0[L_ROW[0]:L_ROW[0] + 1, :], a1[L_ROW[1]:L_ROW[1] + 1, :]
            o_t = jnp.where(top, a0 * (1.0 / l0), a1 * (1.0 / l1))
            o_ref[queries, :] = o_t.T.astype(bf16)
            lse_ref[0:1, queries] = m0 + jnp.log(l0)
            lse_ref[1:2, queries] = m1 + jnp.log(l1)

        if window:
            for c in range(t // LANES):
                queries = slice(c * LANES, (c + 1) * LANES)
                q0 = i * t + c * LANES
                k0 = pl.multiple_of(jnp.maximum(q0 - window, 0), LANES)
                finish(tile(k0, LANES + window, q0 - k0, start(LANES), True, queries), queries)
        else:
            carry = lax.fori_loop(0, i, lambda kb, c: tile(pl.multiple_of(kb * t, t), t, 0, c, False), start(t))
            finish(tile(pl.multiple_of(i * t, t), t, 0, carry, True), slice(0, t))

    q_spec = pl.BlockSpec((t, LANES), lambda j, i: (i, j))
    kv_spec = pl.BlockSpec((s, LANES), lambda j, i: (0, j))
    in_specs, args = [q_spec, kv_spec, kv_spec], [q, k, v]
    if fox:
        in_specs += [pl.BlockSpec((s, 2 * LANES), lambda j, i: (0, j))]
        args += [cum_b]
    if has_sink:
        in_specs += [pl.BlockSpec((None, 2, LANES), lambda j, i: (j, 0, 0))]
        args += [sink_rows.reshape(N_PAIRS, 2, LANES)]
    return pl.pallas_call(
        body, name=name, grid=(N_PAIRS, s // t), in_specs=in_specs,
        out_specs=[q_spec, pl.BlockSpec((None, 2, t), lambda j, i: (j, 0, i))],
        out_shape=[jax.ShapeDtypeStruct((s, N_PAIRS * LANES), bf16), jax.ShapeDtypeStruct((N_PAIRS, 2, s), f32)],
        compiler_params=_params(2),
    )(*args)


def _branch_dgrad_delta(db, w, o, name, *, lse=None, sink_rows=None, after=None):
    s, hw = o.shape
    tm = _row_tile(s, 512)
    has_sink = sink_rows is not None
    extra = [] if after is None else [after]

    def body(*refs):
        db_ref, w_ref, o_ref = refs[:3]
        outs = refs[3 + (2 if has_sink else 0) + len(extra):]
        do_ref, dl_ref = outs[:2]
        if has_sink:
            lse_ref, sink_ref = refs[3:5]
            ds_ref = outs[2]

            @pl.when(pl.program_id(0) == 0)
            def _():
                ds_ref[...] = jnp.zeros_like(ds_ref)
        do = lax.dot_general(db_ref[...], w_ref[...], _NT, preferred_element_type=f32).astype(bf16)
        do_ref[...] = do
        for j in range(N_PAIRS):
            cols = slice(j * LANES, (j + 1) * LANES)
            prod_t = (do[:, cols].astype(f32) * o_ref[:, cols].astype(f32)).T
            for h in range(2):
                dl = jnp.sum(prod_t[h * HEAD_DIM:(h + 1) * HEAD_DIM, :], axis=0, keepdims=True)
                dl_ref[j, h:h + 1, :] = dl
                if has_sink:
                    r = 2 * j + h
                    p_sink = jnp.exp(sink_ref[r:r + 1, 0:1] - lse_ref[j, h:h + 1, :])
                    ds_ref[r:r + 1, :] += -jnp.sum(p_sink * dl, axis=1, keepdims=True)

    rows_spec = pl.BlockSpec((N_PAIRS, 2, tm), lambda i: (0, 0, i))
    in_specs = [_row_spec(tm, db.shape[1]), pl.BlockSpec(w.shape, lambda i: (0, 0)), _row_spec(tm, hw)]
    args = [db, w, o]
    out_specs = [_row_spec(tm, hw), rows_spec]
    out_shape = [jax.ShapeDtypeStruct((s, hw), bf16), jax.ShapeDtypeStruct((N_PAIRS, 2, s), f32)]
    if has_sink:
        in_specs += [rows_spec, _vec_spec(LANES, N_HEADS)]
        args += [lse, sink_rows]
        out_specs += [_vec_spec(LANES, N_HEADS)]
        out_shape += [jax.ShapeDtypeStruct((N_HEADS, LANES), f32)]
    return pl.pallas_call(
        body, name=name, grid=(s // tm,), in_specs=in_specs + [pl.BlockSpec(memory_space=pl.ANY)] * len(extra),
        out_specs=out_specs, out_shape=out_shape, compiler_params=_params(1),
    )(*args, *extra)


def _attn_bwd(q, k, v, do, lse, delta, name, *, cum_b=None, window=None, t=256):
    s = q.shape[0]
    t = _row_tile(s, t)
    nblk = s // t
    fox = cum_b is not None
    assert not window or (window % LANES == 0 and LANES + window <= s)

    def body(*refs):
        k_ref, v_ref, q_ref, do_ref, lse_ref, dl_ref = refs[:6]
        rest = list(refs[6:])
        cb_ref = rest.pop(0) if fox else None
        dq_ref, dk_ref, dv_ref = rest[:3]
        dcs_ref, rs_ref = (rest[3], rest[4]) if fox else (None, None)
        dk_acc, dv_acc = rest[-2:]
        b = pl.program_id(1)
        k0 = pl.multiple_of(b * t, t)

        @pl.when(b == 0)
        def _():
            dq_ref[...] = jnp.zeros_like(dq_ref)
            if fox:
                rs_ref[...] = jnp.zeros_like(rs_ref)

        dk_acc[...] = jnp.zeros_like(dk_acc)
        dv_acc[...] = jnp.zeros_like(dv_acc)
        if fox:
            dcs_ref[...] = jnp.zeros_like(dcs_ref)
        low = _lane() < HEAD_DIM
        top = lax.broadcasted_iota(jnp.int32, (LANES, 1), 0) < HEAD_DIM
        kblk, vblk = k_ref[...], v_ref[...]
        k_t = kblk.astype(f32).T.astype(bf16)
        cks = [_wide(cb_ref[pl.ds(k0, t), h * LANES:(h + 1) * LANES], t) for h in range(2)] if fox else None

        def tile(q0, n_queries, off, masked, keys=slice(0, t)):
            cols = pl.ds(q0, n_queries)
            q2, do2 = q_ref[cols, :], do_ref[cols, :]
            zero = jnp.zeros_like(q2)
            valid = _tile_mask(keys.stop - keys.start, n_queries, off, window) if masked else None
            dq_parts = []
            for h in range(2):
                qm = jnp.where(low, q2, zero) if h == 0 else jnp.where(low, zero, q2)
                dom = jnp.where(low, do2, zero) if h == 0 else jnp.where(low, zero, do2)
                sc = lax.dot_general(kblk[keys], qm, _NT, preferred_element_type=f32)
                if fox:
                    sc = sc - cks[h]
                if masked:
                    sc = jnp.where(valid, sc, NEG)
                p = jnp.exp(sc - lse_ref[h:h + 1, cols])
                dp = lax.dot_general(vblk[keys], dom, _NT, preferred_element_type=f32)
                ds = p * (dp - dl_ref[h:h + 1, cols])
                pb, dsb = p.astype(bf16), ds.astype(bf16)
                dv_acc[keys, :] += jnp.dot(pb, dom, preferred_element_type=f32)
                dk_acc[keys, :] += jnp.dot(dsb, qm, preferred_element_type=f32)
                dq_parts.append(jnp.dot(k_t[:, keys], dsb, preferred_element_type=f32))
                if fox:
                    dcs_ref[:, h * LANES:(h + 1) * LANES] += sum(ds[:, g * LANES:(g + 1) * LANES] for g in range(t // LANES))
                    rs_ref[h:h + 1, cols] += jnp.sum(ds, axis=0, keepdims=True)
            dq_ref[:, cols] += jnp.where(top, dq_parts[0], dq_parts[1])

        def later_block(qb, carry):
            tile(pl.multiple_of(qb * t, t), t, 0, False)
            return carry

        if window:
            for c in range(t // LANES):
                first = b * t + c * LANES
                q0 = pl.multiple_of(jnp.minimum(first, s - (LANES + window)), LANES)
                tile(q0, LANES + window, q0 - first, True, slice(c * LANES, (c + 1) * LANES))
        else:
            tile(k0, t, 0, True)
            lax.fori_loop(b + 1, nblk, later_block, 0)
        dk_ref[...] = dk_acc[...].astype(bf16)
        dv_ref[...] = dv_acc[...].astype(bf16)

    kv_spec = pl.BlockSpec((t, LANES), lambda j, b: (b, j))
    seq_spec = pl.BlockSpec((s, LANES), lambda j, b: (0, j))
    rows_spec = pl.BlockSpec((None, 2, s), lambda j, b: (j, 0, 0))
    hw = N_PAIRS * LANES
    in_specs, args = [kv_spec, kv_spec, seq_spec, seq_spec, rows_spec, rows_spec], [k, v, q, do, lse, delta]
    out_specs = [pl.BlockSpec((LANES, s), lambda j, b: (j, 0)), kv_spec, kv_spec]
    out_shape = [jax.ShapeDtypeStruct((hw, s), f32), jax.ShapeDtypeStruct((s, hw), bf16), jax.ShapeDtypeStruct((s, hw), bf16)]
    if fox:
        in_specs += [pl.BlockSpec((s, 2 * LANES), lambda j, b: (0, j))]
        args += [cum_b]
        out_specs += [pl.BlockSpec((t, 2 * LANES), lambda j, b: (b, j)), rows_spec]
        out_shape += [jax.ShapeDtypeStruct((s, N_HEADS * LANES), f32), jax.ShapeDtypeStruct((N_PAIRS, 2, s), f32)]
    return pl.pallas_call(
        body, name=name, grid=(N_PAIRS, nblk), in_specs=in_specs, out_specs=out_specs, out_shape=out_shape,
        scratch_shapes=[pltpu.VMEM((t, LANES), f32)] * 2, compiler_params=_params(2),
    )(*args)


def _branch_merge(o_a, o_b, w_a, w_b, gl, name):
    s, k = o_a.shape
    d = w_a.shape[1]
    tm = _row_tile(s, 1024)

    def body(oa_ref, ob_ref, wa_ref, wb_ref, g_ref, ba_ref, bb_ref, m_ref):
        ba = jnp.dot(oa_ref[...], wa_ref[...], preferred_element_type=f32)
        bb = jnp.dot(ob_ref[...], wb_ref[...], preferred_element_type=f32)
        g0, g1 = jax.nn.sigmoid(g_ref[:, :d].astype(f32)), jax.nn.sigmoid(g_ref[:, d:].astype(f32))
        ba_ref[...] = ba.astype(bf16)
        bb_ref[...] = bb.astype(bf16)
        m_ref[...] = (g0 * ba + g1 * bb).astype(bf16)

    whole = pl.BlockSpec((k, d), lambda i: (0, 0))
    return pl.pallas_call(
        body, name=name, grid=(s // tm,),
        in_specs=[_row_spec(tm, k), _row_spec(tm, k), whole, whole, _row_spec(tm, 2 * d)],
        out_specs=[_row_spec(tm, d)] * 3, out_shape=[jax.ShapeDtypeStruct((s, d), bf16)] * 3, compiler_params=_params(1),
    )(o_a, o_b, w_a, w_b, gl)


def _out_dgrad_merge_bwd(dy, w_out, ba, bb, gl, name):
    s, d = ba.shape
    tm = _row_tile(s, 512)

    def body(dy_ref, w_ref, a_ref, b_ref, g_ref, da_ref, db_ref, dg_ref):
        dmv = lax.dot_general(dy_ref[...], w_ref[...], _NT, preferred_element_type=f32)
        g0, g1 = jax.nn.sigmoid(g_ref[:, :d].astype(f32)), jax.nn.sigmoid(g_ref[:, d:].astype(f32))
        da_ref[...] = (dmv * g0).astype(bf16)
        db_ref[...] = (dmv * g1).astype(bf16)
        dg_ref[:, :d] = (dmv * a_ref[...].astype(f32) * (g0 * (1.0 - g0))).astype(bf16)
        dg_ref[:, d:] = (dmv * b_ref[...].astype(f32) * (g1 * (1.0 - g1))).astype(bf16)

    return pl.pallas_call(
        body, name=name, grid=(s // tm,),
        in_specs=[_row_spec(tm, dy.shape[1]), pl.BlockSpec(w_out.shape, lambda i: (0, 0))] + [_row_spec(tm, d)] * 2
        + [_row_spec(tm, 2 * d)],
        out_specs=[_row_spec(tm, d)] * 2 + [_row_spec(tm, 2 * d)],
        out_shape=[jax.ShapeDtypeStruct((s, d), bf16)] * 2 + [jax.ShapeDtypeStruct((s, 2 * d), bf16)],
        compiler_params=_params(1),
    )(dy, w_out, ba, bb, gl)


GLU_TILE = 256


def _ffn_in_swiglu(h, w_t, name):
    s, d = h.shape
    f = w_t.shape[0] // 2
    tm = _row_tile(s, 2048)
    tg = GLU_TILE
    nb = f // tg

    def body(h_ref, wg_ref, wu_ref, g_ref, u_ref, act_ref):
        hv = h_ref[...]
        g = lax.dot_general(hv, wg_ref[...], _NT, preferred_element_type=f32)
        u = lax.dot_general(hv, wu_ref[...], _NT, preferred_element_type=f32)
        g_ref[...] = g.astype(bf16)
        u_ref[...] = u.astype(bf16)
        act_ref[...] = (g * jax.nn.sigmoid(g) * u).astype(bf16)

    col = pl.BlockSpec((tm, tg), lambda i, j: (i, j))
    return pl.pallas_call(
        body, name=name, grid=(s // tm, nb),
        in_specs=[pl.BlockSpec((tm, d), lambda i, j: (i, 0)), pl.BlockSpec((tg, d), lambda i, j: (j, 0)),
                  pl.BlockSpec((tg, d), lambda i, j: (j + nb, 0))],
        out_specs=[col] * 3, out_shape=[jax.ShapeDtypeStruct((s, f), bf16)] * 3, compiler_params=_params(2),
    )(h, w_t, w_t)


def _ffn_out_dgrad_swiglu(dy, w_out, g, u, name):
    s, d = dy.shape
    f = g.shape[1]
    tm = _row_tile(s, 2048)
    tg = GLU_TILE

    def body(dy_ref, w_ref, g_ref, u_ref, dg_ref, du_ref):
        dv = lax.dot_general(dy_ref[...], w_ref[...], _NT, preferred_element_type=f32)
        gv, uv = g_ref[...].astype(f32), u_ref[...].astype(f32)
        sg = jax.nn.sigmoid(gv)
        dg_ref[...] = (dv * uv * (sg * (1.0 + gv * (1.0 - sg)))).astype(bf16)
        du_ref[...] = (dv * (gv * sg)).astype(bf16)

    col = pl.BlockSpec((tm, tg), lambda i, j: (i, j))
    return pl.pallas_call(
        body, name=name, grid=(s // tm, f // tg),
        in_specs=[pl.BlockSpec((tm, d), lambda i, j: (i, 0)), pl.BlockSpec((tg, d), lambda i, j: (j, 0)), col, col],
        out_specs=[col] * 2, out_shape=[jax.ShapeDtypeStruct((s, f), bf16)] * 2, compiler_params=_params(2),
    )(dy, w_out, g, u)


def _wgrad_stack(parts, h, name):
    s, m = parts[0].shape
    d = h.shape[1]
    tm = 256
    nb = m // tm
    n = len(parts)

    def body(*refs):
        i = pl.program_id(0)
        for p in range(n):
            @pl.when(i // nb == p)
            def _(p=p):
                refs[n + 1][...] = lax.dot_general(refs[p][...], refs[n][...], _TN, preferred_element_type=f32).astype(bf16)

    a_specs = [pl.BlockSpec((s, tm), lambda i, p=p: (0, jnp.clip(i - p * nb, 0, nb - 1))) for p in range(n)]
    return pl.pallas_call(
        body, name=name, grid=(n * nb,), in_specs=a_specs + [pl.BlockSpec((s, d), lambda i: (0, 0))],
        out_specs=pl.BlockSpec((tm, d), lambda i: (i, 0)),
        out_shape=jax.ShapeDtypeStruct((n * m, d), bf16), compiler_params=_params(1),
    )(*parts, h)


def _ada_wgrad(c_all, d_all, name):
    n, d = c_all.shape
    w = d_all.shape[1]

    def body(c_ref, d_ref, o_ref):
        eye = (lax.broadcasted_iota(jnp.int32, (n, n), 0) == lax.broadcasted_iota(jnp.int32, (n, n), 1)).astype(f32)
        ct = lax.dot_general(c_ref[...], eye, _TN, precision=lax.Precision.HIGHEST, preferred_element_type=f32)
        g = ct[:, 0:1] * d_ref[0:1, :]
        for bi in range(1, n):
            g = g + ct[:, bi:bi + 1] * d_ref[bi:bi + 1, :]
        o_ref[0] = g

    return pl.pallas_call(
        body, name=name, out_shape=jax.ShapeDtypeStruct((1, d, w), f32), compiler_params=_params(),
    )(c_all, d_all)


def _adamw(parts, w, m, v, name, mine=None, me=None):
    r, c = w.shape
    n_parts = parts.shape[0]
    row_tiles = [t for t in range(min(r, 256), 0, -1) if r % t == 0 and (t % 16 == 0 or t == r)]
    if row_tiles:
        tr, tc = row_tiles[0], c
    else:
        tr, tc = r, next(t for t in (256, LANES) if c % t == 0)

    def body(*refs):
        w_ref, m_ref, v_ref, g_ref, d_ref, nm_ref, nv_ref = refs[-7:]
        if mine is None:
            p_ref, = refs[:-7]
        else:
            me_ref, p_ref, own_ref = refs[:-7]

        def part(i):
            if mine is None:
                return p_ref[i].astype(f32)
            return jnp.where(me_ref[0] == i, own_ref[...], p_ref[i]).astype(f32)

        g = part(0)
        for i in range(1, n_parts):
            g = g + part(i)
        mm = ADAM_B1 * m_ref[...] + (1.0 - ADAM_B1) * g
        vv = ADAM_B2 * v_ref[...] + (1.0 - ADAM_B2) * (g * g)
        m_hat = mm / (1.0 - ADAM_B1 ** ADAM_STEP)
        v_hat = vv / (1.0 - ADAM_B2 ** ADAM_STEP)
        g_ref[...] = g
        d_ref[...] = -ADAM_LR * (m_hat / (jnp.sqrt(v_hat) + ADAM_EPS) + ADAM_WD * w_ref[...])
        nm_ref[...] = mm
        nv_ref[...] = vv

    out_shape = [jax.ShapeDtypeStruct((r, c), f32)] * 4
    if mine is None:
        spec = pl.BlockSpec((tr, tc), lambda i, j: (i, j))
        return pl.pallas_call(
            body, name=name, grid=(r // tr, c // tc),
            in_specs=[pl.BlockSpec((n_parts, tr, tc), lambda i, j: (0, i, j))] + [spec] * 3,
            out_specs=[spec] * 4, out_shape=out_shape, compiler_params=_params(2),
        )(parts, w, m, v)
    spec = pl.BlockSpec((tr, tc), lambda i, j, me_ref: (i, j))
    return pl.pallas_call(
        body, name=name, out_shape=out_shape, compiler_params=_params(2),
        grid_spec=pltpu.PrefetchScalarGridSpec(
            num_scalar_prefetch=1, grid=(r // tr, c // tc),
            in_specs=[pl.BlockSpec((n_parts, tr, tc), lambda i, j, me_ref: (0, i, j)),
                      pl.BlockSpec((None, tr, tc), lambda i, j, me_ref: (me_ref[0], i, j))] + [spec] * 3,
            out_specs=[spec] * 4),
    )(me, parts, mine, w, m, v)


def _me():
    return lax.axis_index("x"), lax.axis_index("y"), lax.axis_index("c")


def _all_gather(arrays, name, vmem=False, after=None):
    n = len(arrays)
    space = pltpu.VMEM if vmem else pl.ANY
    extra = [] if after is None else [after]

    def body(*refs):
        ins = refs[:n]
        outs = refs[n + len(extra):2 * n + len(extra)]
        send_sems, recv_sems, local_sems = refs[2 * n + len(extra):]
        x, y, c = _me()
        me, sibling = (x, y, c), (x, y, 1 - c)
        chips = [(1 - x, y), (x, 1 - y), (1 - x, 1 - y)]

        def rows(a, dev):
            return outs[a].at[4 * dev[0] + 2 * dev[1] + dev[2]]

        def copy(a, k, block, to, src=None):
            return pltpu.make_async_remote_copy(
                src_ref=rows(a, block) if src is None else src, dst_ref=rows(a, block),
                send_sem=send_sems.at[a, k], recv_sem=recv_sems.at[a, k], device_id=to, device_id_type=MESH)

        mine = [pltpu.make_async_copy(ins[a], rows(a, me), local_sems.at[a]) for a in range(n)]
        for cp in mine:
            cp.start()
        first = []
        for a in range(n):
            first.append(copy(a, 0, me, sibling, src=ins[a]))
            first += [copy(a, 1 + j, me, (*chip, c), src=ins[a]) for j, chip in enumerate(chips)]
        for cp in first:
            cp.start()
        passed = []
        for j, chip in enumerate(chips):
            for a in range(n):
                copy(a, 1 + j, (*chip, c), me).wait_recv()
                fwd = copy(a, 4 + j, (*chip, c), sibling)
                fwd.start()
                passed.append(fwd)
        for a in range(n):
            copy(a, 0, sibling, me).wait_recv()
            for j, chip in enumerate(chips):
                copy(a, 4 + j, (*chip, 1 - c), me).wait_recv()
        for cp in first + passed:
            cp.wait_send()
        for cp in mine:
            cp.wait()

    outs = pl.pallas_call(
        body, name=name,
        in_specs=[pl.BlockSpec(memory_space=space)] * n + [pl.BlockSpec(memory_space=pl.ANY)] * len(extra),
        out_specs=[pl.BlockSpec(memory_space=space)] * n,
        out_shape=[jax.ShapeDtypeStruct((N_DEV,) + a.shape, a.dtype) for a in arrays],
        scratch_shapes=[pltpu.SemaphoreType.DMA((n, 7)), pltpu.SemaphoreType.DMA((n, 7)), pltpu.SemaphoreType.DMA((n,))],
        compiler_params=pltpu.CompilerParams(vmem_limit_bytes=VMEM_LIMIT),
    )(*arrays, *extra)
    return list(outs)


def _gather_prologue(c, w_ada, b_mine, w_in_t, name):
    n_dev, d = N_DEV, c.shape[1]
    ada_w = w_ada.shape[1]
    n_first = -(-(F_OFF + N_HEADS) // w_in_t.shape[0])

    def body(c_ref, w_ref, b_ref, win_ref, call_ref, ada_ref, gin_ref, cols_ref, send_sems, recv_sems, local_sems):
        x, y, cc = _me()
        me, sibling = (x, y, cc), (x, y, 1 - cc)
        chips = [(1 - x, y), (x, 1 - y), (1 - x, 1 - y)]
        outs = (call_ref, ada_ref, gin_ref)

        def row(dev):
            return 4 * dev[0] + 2 * dev[1] + dev[2]

        def rows(a, dev):
            return outs[a].at[row(dev)]

        def if_sent(a, dev, fn):
            if a < 2:
                fn()
            else:
                pl.when(row(dev) < n_first)(fn)

        def copy(a, k, block, to, src=None):
            return pltpu.make_async_remote_copy(
                src_ref=rows(a, block) if src is None else src, dst_ref=rows(a, block),
                send_sem=send_sems.at[a, k], recv_sem=recv_sems.at[a, k], device_id=to, device_id_type=MESH)

        def gather(a, src):
            own = pltpu.make_async_copy(src, rows(a, me), local_sems.at[a])
            sends = [copy(a, 0, me, sibling, src=src)] + [copy(a, 1 + j, me, (*chip, cc), src=src) for j, chip in enumerate(chips)]
            passed = [copy(a, 4 + j, (*chip, cc), sibling) for j, chip in enumerate(chips)]

            def begin():
                for cp in [own] + sends:
                    cp.start()
            if_sent(a, me, begin)
            for j, chip in enumerate(chips):
                def relay(j=j, chip=chip):
                    copy(a, 1 + j, (*chip, cc), me).wait_recv()
                    passed[j].start()
                if_sent(a, (*chip, cc), relay)
            if_sent(a, sibling, lambda: copy(a, 0, sibling, me).wait_recv())
            for j, chip in enumerate(chips):
                if_sent(a, (*chip, 1 - cc), lambda j=j, chip=chip: copy(a, 4 + j, (*chip, 1 - cc), me).wait_recv())
            for j, chip in enumerate(chips):
                if_sent(a, (*chip, cc), lambda j=j: passed[j].wait_send())

            def end():
                for cp in sends:
                    cp.wait_send()
                own.wait()
            if_sent(a, me, end)

        gather(0, c_ref)
        cols_ref[...] = (jnp.dot(call_ref[:, 0, :].astype(bf16), w_ref[...].astype(bf16), preferred_element_type=f32)
                         + b_ref[...])
        gather(1, cols_ref)
        gather(2, win_ref)

    vmem, hbm = pl.BlockSpec(memory_space=pltpu.VMEM), pl.BlockSpec(memory_space=pl.ANY)
    return pl.pallas_call(
        body, name=name, in_specs=[vmem, vmem, vmem, hbm], out_specs=[vmem, vmem, hbm],
        out_shape=[jax.ShapeDtypeStruct((n_dev, 1, d), f32), jax.ShapeDtypeStruct((n_dev, n_dev, ada_w), f32),
                   jax.ShapeDtypeStruct((n_dev,) + w_in_t.shape, w_in_t.dtype)],
        scratch_shapes=[pltpu.VMEM((n_dev, ada_w), f32), pltpu.SemaphoreType.DMA((3, 7)), pltpu.SemaphoreType.DMA((3, 7)),
                        pltpu.SemaphoreType.DMA((3,))],
        compiler_params=pltpu.CompilerParams(vmem_limit_bytes=VMEM_LIMIT),
    )(c, w_ada, b_mine, w_in_t)


_FLIPS = ((0, 0, 1), (1, 0, 0), (0, 1, 0), (1, 1, 0), (1, 0, 1), (0, 1, 1), (1, 1, 1))
_HBM = pl.BlockSpec(memory_space=pltpu.HBM)
_SEM = pl.BlockSpec(memory_space=pltpu.SEMAPHORE)


def _exchange_copies(scatter, srcs, lands, send_sems, recv_sems):
    x, y, c = _me()
    me_row = 4 * x + 2 * y + c
    out = []
    for k, (fx, fy, fc) in enumerate(_FLIPS):
        peer = (x ^ fx, y ^ fy, c ^ fc)
        peer_row = 4 * peer[0] + 2 * peer[1] + peer[2]
        for a in range(len(srcs)):
            out.append(pltpu.make_async_remote_copy(
                src_ref=srcs[a].at[peer_row] if scatter else srcs[a], dst_ref=lands[a].at[me_row],
                send_sem=send_sems.at[7 * a + k], recv_sem=recv_sems.at[7 * a + k], device_id=peer, device_id_type=MESH))
    return out


def _exchange_start(arrays, scatter, name, after=None):
    n = len(arrays)
    lands = [lax.empty(a.shape if scatter else (N_DEV,) + a.shape, a.dtype) for a in arrays]
    extra = [] if after is None else [after]

    def body(*refs):
        srcs, zones = refs[:n], refs[n:2 * n]
        send_sems, recv_sems = refs[2 * n + len(extra)], refs[2 * n + len(extra) + 1]
        token = refs[-1]
        for cp in _exchange_copies(scatter, srcs, zones, send_sems, recv_sems):
            cp.start()
        token[...] = jnp.zeros_like(token)

    thru = [pltpu.HBM(a.shape, a.dtype) for a in list(arrays) + lands]
    outs = pl.pallas_call(
        body, name=name,
        out_shape=(pltpu.SemaphoreType.DMA((7 * n,)), pltpu.SemaphoreType.DMA((7 * n,)), *thru, jax.ShapeDtypeStruct((8, LANES), f32)),
        in_specs=[_HBM] * (2 * n) + [pl.BlockSpec(memory_space=pl.ANY)] * len(extra),
        out_specs=(_SEM, _SEM, *[_HBM] * (2 * n), pl.BlockSpec(memory_space=pltpu.VMEM)),
        input_output_aliases={i: 2 + i for i in range(2 * n)},
        compiler_params=pltpu.CompilerParams(has_side_effects=pltpu.SideEffectType.DATAFLOW_SIDE_EFFECTING),
    )(*[pltpu.with_memory_space_constraint(a, pltpu.HBM) for a in list(arrays) + lands], *extra)
    return dict(n=n, scatter=scatter, sems=outs[:2], srcs=outs[2:2 + n], lands=outs[2 + n:2 + 2 * n], token=outs[-1])


def _exchange_wait(handle, after, name):
    n, scatter = handle["n"], handle["scatter"]

    def body(*refs):
        srcs, zones = refs[:n], refs[n:2 * n]
        send_sems, recv_sems = refs[2 * n], refs[2 * n + 1]
        for cp in _exchange_copies(scatter, srcs, zones, send_sems, recv_sems):
            cp.wait_send()
            cp.wait_recv()

    thru = [pltpu.HBM(a.shape, a.dtype) for a in list(handle["srcs"]) + list(handle["lands"])]
    outs = pl.pallas_call(
        body, name=name, out_shape=tuple(thru),
        in_specs=[_HBM] * (2 * n) + [_SEM, _SEM, pl.BlockSpec(memory_space=pl.ANY)], out_specs=tuple([_HBM] * (2 * n)),
        input_output_aliases={i: i for i in range(2 * n)},
        compiler_params=pltpu.CompilerParams(has_side_effects=pltpu.SideEffectType.DATAFLOW_SIDE_EFFECTING),
    )(*handle["srcs"], *handle["lands"], *handle["sems"], after)
    return list(outs[n:])


def _cols_from_shards(g):
    return jnp.transpose(g, (1, 0, 2)).reshape(g.shape[1], -1)


def _shards_from_cols(a):
    return jnp.transpose(a.reshape(a.shape[0], N_DEV, -1), (1, 0, 2))


def _local_step(x, positions, ada, g_pre_mix, g_post_mix, b_f, sinks, g_pre_ffn, g_post_ffn, target,
                w_in_head_t, mix_weights, ffn_weights, on_grads):
    s, d = x.shape
    row = lambda v: v.reshape(1, -1)
    shift_m, scale_m, gate_m, shift_f, scale_f, gate_f = (ada[i:i + 1] for i in range(6))
    w_qkv_t = w_in_head_t[:QKV_W]
    w_f_t = jnp.pad(w_in_head_t[F_OFF:F_OFF + N_HEADS], ((0, LANES - N_HEADS), (0, 0)))
    bf_row = jnp.pad(row(b_f), ((0, 0), (0, LANES - N_HEADS)))
    sink_rows = jnp.broadcast_to(sinks.reshape(N_HEADS, 1).astype(f32), (N_HEADS, LANES))
    inv_freq = 1.0 / (ROPE_THETA ** (jnp.arange(0, HEAD_DIM, 2, dtype=f32) / HEAD_DIM))
    cos, sin_s = _rope_tables(positions.reshape(s, 1), jnp.tile(inv_freq, 4).reshape(1, LANES), "rope_tables")

    h1, qa, ka, va, qb, kb, vb = _prenorm_proj_qkv(x, row(g_pre_mix), scale_m, shift_m, w_qkv_t, cos, sin_s, "prenorm_proj_qkv")
    fl, cum_b = _forget_prep(h1, w_f_t, bf_row, "proj_forget_prep")
    o_a, lse_a = _attn_fwd(qa, ka, va, "swa_fwd", sink_rows=sink_rows, window=WINDOW, t=2048)
    o_b, lse_b = _attn_fwd(qb, kb, vb, "fox_fwd", cum_b=cum_b, t=1024)
    everything_before = (o_a[:8, :LANES] + o_b[:8, :LANES]).astype(f32)
    w_gate_t, w_branch_a, w_branch_b, w_out = mix_weights(everything_before)
    gl = _matmul(h1, w_gate_t, "nt", bf16, "proj_gate")
    ba, bb, merged = _branch_merge(o_a, o_b, w_branch_a, w_branch_b, gl, "branch_merge")
    y1, x2, h2 = _out_proj_postnorm_prenorm(merged, w_out, x, row(g_post_mix), gate_m, row(g_pre_ffn), scale_f, shift_f,
                                            "out_proj_norms")

    w_ffn_in_t, w_ffn_out = ffn_weights(h2)
    g_ff, u_ff, act = _ffn_in_swiglu(h2, w_ffn_in_t, "ffn_in_swiglu")
    loss_row, d_out, d_y2, vec_pf = _out_proj_loss_tail(act, w_ffn_out, x2, row(g_post_ffn), gate_f, target, "ffn_out_loss_tail")

    g_w_ffn_out = _matmul(act, d_y2, "tn", bf16, "ffn_out_wgrad")
    dg_ff, du_ff = _ffn_out_dgrad_swiglu(d_y2, w_ffn_out, g_ff, u_ff, "ffn_out_dgrad_swiglu")
    g_w_ffn_in_t = _wgrad_stack([dg_ff, du_ff], h2, "ffn_in_wgrad")
    sent = on_grads(dict(w_ffn_in=g_w_ffn_in_t, w_ffn_out=g_w_ffn_out))
    d_x2, vec_nf, d_y1, vec_pm = _dgrad_prenorm_bwd(
        [(dg_ff, w_ffn_in_t, 0), (du_ff, w_ffn_in_t, 1)], x2, row(g_pre_ffn), scale_f, d_out, "ffn_in_dgrad_norms_bwd",
        after=sent, below=(y1, row(g_post_mix), gate_m))

    g_w_out = _matmul(merged, d_y1, "tn", bf16, "out_proj_wgrad")
    d_ba, d_bb, dgl = _out_dgrad_merge_bwd(d_y1, w_out, ba, bb, gl, "out_proj_dgrad_merge_bwd")
    g_w_branch_a = _matmul(o_a, d_ba, "tn", bf16, "branch_a_wgrad")
    g_w_branch_b = _matmul(o_b, d_bb, "tn", bf16, "branch_b_wgrad")
    sent = on_grads(dict(w_out=g_w_out, w_branch_a=g_w_branch_a, w_branch_b=g_w_branch_b))
    d_oa, delta_a, d_sink = _branch_dgrad_delta(d_ba, w_branch_a, o_a, "branch_a_dgrad_delta", lse=lse_a,
                                                sink_rows=sink_rows, after=sent)
    d_ob, delta_b = _branch_dgrad_delta(d_bb, w_branch_b, o_b, "branch_b_dgrad_delta", after=sent)
    dqa_t, dka, dva = _attn_bwd(qa, ka, va, d_oa, lse_a, delta_a, "swa_bwd", window=WINDOW, t=2048)
    dqb_t, dkb, dvb, dcs, rs = _attn_bwd(qb, kb, vb, d_ob, lse_b, delta_b, "fox_bwd", cum_b=cum_b, t=512)
    dqkv = _qkv_prep_bwd(dqa_t, dka, dva, dqb_t, dkb, dvb, cos, sin_s, "qkv_prep_bwd")
    dfl, vec_bf = _forget_prep_bwd(rs.reshape(N_HEADS, s), dcs, fl, bf_row, "forget_prep_bwd")
    g_w_in_t = jnp.concatenate([_matmul(dqkv, h1, "tn", bf16, "qkv_wgrad"), _matmul(dfl, h1, "tn", bf16, "forget_wgrad")[:N_HEADS],
                                _matmul(dgl, h1, "tn", bf16, "gate_wgrad")], axis=0)
    sent = on_grads(dict(w_in=g_w_in_t))
    grad_x, vec_nm = _dgrad_prenorm_bwd([(dgl, w_gate_t, 0), (dqkv, w_qkv_t, 0), (dfl, w_f_t, 0)], x, row(g_pre_mix),
                                        scale_m, d_x2, "in_proj_dgrad_prenorm_bwd", after=sent)

    d_ada = jnp.concatenate([vec_nm[0], vec_nm[1], vec_pm[0], vec_nf[0], vec_nf[1], vec_pf[0]])
    small = dict(b_ada=d_ada, g_pre_mix=vec_nm[2], g_post_mix=vec_pm[1], g_pre_ffn=vec_nf[2], g_post_ffn=vec_pf[1],
                 b_f=vec_bf[0, :N_HEADS], sinks=d_sink[:, 0], loss=loss_row[0, :1])
    return grad_x, small


_SMALL = (("b_ada", 6144), ("g_pre_mix", 1024), ("g_post_mix", 1024), ("g_pre_ffn", 1024), ("g_post_ffn", 1024),
          ("b_f", 128), ("sinks", 128), ("loss", 128))
_SMALL_ROWS = 88


def _pack_small(vals):
    parts = [jnp.pad(vals[k].reshape(-1).astype(f32), (0, n - vals[k].size)) for k, n in _SMALL]
    flat = jnp.concatenate(parts)
    return jnp.pad(flat, (0, _SMALL_ROWS * LANES - flat.size)).reshape(_SMALL_ROWS, LANES)


def _unpack_small(slab, shapes):
    flat, out, off = slab.reshape(-1), {}, 0
    for k, n in _SMALL:
        size = math.prod(shapes[k])
        out[k] = flat[off:off + size].reshape(shapes[k])
        off += n
    return out


def kernel(x, c, positions, w_ada, b_ada, g_pre_mix, g_post_mix, w_in, b_f, sinks, w_branch_a, w_branch_b, w_out, g_pre_ffn, g_post_ffn, w_ffn_in, w_ffn_out, loss_target, m_w_ada, m_b_ada, m_g_pre_mix, m_g_post_mix, m_w_in, m_b_f, m_sinks, m_w_branch_a, m_w_branch_b, m_w_out, m_g_pre_ffn, m_g_post_ffn, m_w_ffn_in, m_w_ffn_out, v_w_ada, v_b_ada, v_g_pre_mix, v_g_post_mix, v_w_in, v_b_f, v_sinks, v_w_branch_a, v_w_branch_b, v_w_out, v_g_pre_ffn, v_g_post_ffn, v_w_ffn_in, v_w_ffn_out):
    xi, yi, ci = _me()
    me = 4 * xi + 2 * yi + ci
    d = D_MODEL
    ada_w = w_ada.shape[2]

    transposed = ("w_in", "w_ffn_in")
    tr = lambda a: jnp.transpose(a[0])

    b_mine = lax.dynamic_slice(b_ada, (0, me * ada_w), (1, ada_w))
    w_in_mine = tr(w_in).astype(bf16)
    c_all, ada_all, g_in = _gather_prologue(c, w_ada[0], b_mine, w_in_mine, "gather_prologue")
    n_first = -(-(F_OFF + N_HEADS) // w_in_mine.shape[0])
    c_all = c_all.reshape(N_DEV, d)
    ada = lax.dynamic_index_in_dim(ada_all, me, axis=1, keepdims=False).reshape(6, d)
    late_mix = [w_in_mine] + [w.astype(bf16) for w in (w_branch_a[0], w_branch_b[0], w_out[0])]
    late_ffn = [w.astype(bf16) for w in (tr(w_ffn_in), w_ffn_out[0])]
    mix_h = _exchange_start(late_mix, False, "gather_mix_start", after=g_in)
    ffn_h = _exchange_start(late_ffn, False, "gather_ffn_start", after=mix_h["token"])

    def mine_into(zone, block):
        return lax.dynamic_update_index_in_dim(zone, block, me, 0)

    def rows_from_shards(g):
        return g.reshape(g.shape[0] * g.shape[1], g.shape[2])

    def mix_weights(after):
        zones = _exchange_wait(mix_h, after, "gather_mix_wait")
        g_in_all, g_ba, g_bb, g_out = (mine_into(z, w) for z, w in zip(zones, late_mix))
        return (rows_from_shards(g_in_all)[F_OFF + N_HEADS:], _cols_from_shards(g_ba), _cols_from_shards(g_bb),
                rows_from_shards(g_out))

    def ffn_weights(after):
        zones = _exchange_wait(ffn_h, after, "gather_ffn_wait")
        g_fi, g_fo = (mine_into(z, w) for z, w in zip(zones, late_ffn))
        return rows_from_shards(g_fi), rows_from_shards(g_fo)

    row_sharded = ("w_out", "w_ffn_out") + transposed
    in_flight = []

    def on_grads(group):
        sends = [g.reshape(N_DEV, g.shape[0] // N_DEV, g.shape[1]) if nm in row_sharded else _shards_from_cols(g)
                 for nm, g in group.items()]
        handle = _exchange_start(sends, True, "scatter_start_%d" % len(in_flight))
        in_flight.append((list(group), sends, handle))
        return handle["token"]

    grad_x, small = _local_step(
        x[0], positions[0], ada + ffn_h["token"][0, 0], g_pre_mix[0], g_post_mix[0], b_f[0], sinks[0], g_pre_ffn[0],
        g_post_ffn[0], loss_target[0], rows_from_shards(g_in[:n_first]), mix_weights, ffn_weights, on_grads)

    ws = dict(w_in=(w_in, m_w_in, v_w_in), w_branch_a=(w_branch_a, m_w_branch_a, v_w_branch_a),
              w_branch_b=(w_branch_b, m_w_branch_b, v_w_branch_b), w_out=(w_out, m_w_out, v_w_out),
              w_ffn_in=(w_ffn_in, m_w_ffn_in, v_w_ffn_in), w_ffn_out=(w_ffn_out, m_w_ffn_out, v_w_ffn_out))
    res = {}

    def finish_group(gi, after):
        names, sends, handle = in_flight[gi]
        zones = _exchange_wait(handle, after, "scatter_wait_%d" % gi)
        for nm, zone, sent in zip(names, zones, sends):
            w, m, v = (tr(a) if nm in transposed else a[0] for a in ws[nm])
            out = _adamw(zone, w, m, v, "adamw_" + nm, mine=sent, me=me.reshape(1).astype(jnp.int32))
            after = out[0]
            res[nm] = [jnp.transpose(o) for o in out] if nm in transposed else out
        return after

    done = finish_group(1, finish_group(0, grad_x))

    slab_all, = _all_gather([_pack_small(small)], "gather_small", vmem=True, after=done)
    small_w = dict(b_ada=b_ada, g_pre_mix=g_pre_mix, g_post_mix=g_post_mix, g_pre_ffn=g_pre_ffn, g_post_ffn=g_post_ffn,
                   b_f=b_f, sinks=sinks, loss=jnp.zeros((1,), f32))
    small_m = dict(b_ada=m_b_ada, g_pre_mix=m_g_pre_mix, g_post_mix=m_g_post_mix, g_pre_ffn=m_g_pre_ffn,
                   g_post_ffn=m_g_post_ffn, b_f=m_b_f, sinks=m_sinks, loss=jnp.zeros((1,), f32))
    small_v = dict(b_ada=v_b_ada, g_pre_mix=v_g_pre_mix, g_post_mix=v_g_post_mix, g_pre_ffn=v_g_pre_ffn,
                   g_post_ffn=v_g_post_ffn, b_f=v_b_f, sinks=v_sinks, loss=jnp.ones((1,), f32))
    shapes = {k: small_w[k].shape for k, _ in _SMALL}
    s_out = _adamw(slab_all, _pack_small(small_w), _pack_small(small_m), _pack_small(small_v), "adamw_small")
    s_grad, s_delta, s_m, s_v = (_unpack_small(o, shapes) for o in s_out)

    d_ada_all = lax.dynamic_slice(slab_all[:, :6144 // LANES, :].reshape(N_DEV, 6144), (0, me * ada_w), (N_DEV, ada_w))
    ada_parts = _ada_wgrad(c_all, d_ada_all, "ada_wgrad")

    res["w_ada"] = _adamw(ada_parts, w_ada[0], m_w_ada[0], v_w_ada[0], "adamw_w_ada")
    finish_group(2, res["w_ada"][0])

    order = ["w_ada", "b_ada", "g_pre_mix", "g_post_mix", "w_in", "b_f", "sinks", "w_branch_a", "w_branch_b", "w_out",
             "g_pre_ffn", "g_post_ffn", "w_ffn_in", "w_ffn_out"]
    outs = [s_grad["loss"].reshape(()), grad_x[None]]
    for which, small_o in enumerate((s_grad, s_delta, s_m, s_v)):
        for nm in order:
            outs.append(res[nm][which][None] if nm in res else small_o[nm])
    return tuple(outs)
```

```python
import math

import jax
import jax.numpy as jnp
from jax import lax
from jax.experimental import pallas as pl
from jax.experimental.pallas import tpu as pltpu

f32 = jnp.float32
bf16 = jnp.bfloat16

D_MODEL = 1024
HEAD_DIM = 64
N_HEADS = 8
N_PAIRS = 4
QKV_W = 2304
F_OFF = 2304
WINDOW = 128
ROPE_THETA = 10000.0
RMS_EPS = 1e-6
N_DEV = 8
ADAM_LR, ADAM_B1, ADAM_B2, ADAM_EPS, ADAM_WD, ADAM_STEP = 0.001, 0.9, 0.999, 1e-08, 0.01, 10
NEG = -1e30
L_ROW = (HEAD_DIM, 0)
LANES = 128
VMEM_LIMIT = 48 * 1024 * 1024
MESH = pl.DeviceIdType.MESH

_NT = (((1,), (1,)), ((), ()))
_TN = (((0,), (0,)), ((), ()))


def _params(n_grid=0):
    sem = ("arbitrary",) * n_grid if n_grid else None
    return pltpu.CompilerParams(dimension_semantics=sem, vmem_limit_bytes=VMEM_LIMIT)


def _row_tile(s, want):
    t = min(s, want)
    assert s % t == 0, (s, t)
    return t


MATMUL_VMEM_BUDGET = 40 * 1024 * 1024


def _matmul_tiles(m, n, k, a_item, b_item, o_item):
    def tiles(d):
        return [t for t in range(LANES, min(d, 2048) + 1, LANES) if d % t == 0] or [d]

    best = None
    for tm in tiles(m):
        for tn in tiles(n):
            vmem = 2 * (tm * k * a_item + tn * k * b_item + tm * tn * o_item) + tm * tn * 4
            if vmem > MATMUL_VMEM_BUDGET:
                continue
            traffic = m * k * a_item + n * k * b_item * (1 if tn == n else m // tm) + m * n * o_item
            steps = (m // tm) * (n // tn)
            key = (traffic, 0, steps) if steps >= 4 else (traffic, 1, -steps)
            if best is None or key < best[0]:
                best = (key, tm, tn)
    assert best is not None, (m, n, k)
    return best[1], best[2]


def _matmul(a, b, mode, out_dtype, name, after=None):
    if mode == "nn":
        (m, k), n = a.shape, b.shape[1]
    elif mode == "nt":
        (m, k), n = a.shape, b.shape[0]
    else:
        (k, m), n = a.shape, b.shape[1]
    tm, tn = _matmul_tiles(m, n, k, a.dtype.itemsize, b.dtype.itemsize, jnp.dtype(out_dtype).itemsize)
    if mode == "nn":
        a_spec, b_spec, dims = pl.BlockSpec((tm, k), lambda i, j: (i, 0)), pl.BlockSpec((k, tn), lambda i, j: (0, j)), None
    elif mode == "nt":
        a_spec, b_spec, dims = pl.BlockSpec((tm, k), lambda i, j: (i, 0)), pl.BlockSpec((tn, k), lambda i, j: (j, 0)), _NT
    else:
        a_spec, b_spec, dims = pl.BlockSpec((k, tm), lambda i, j: (0, i)), pl.BlockSpec((k, tn), lambda i, j: (0, j)), _TN

    def body(a_ref, b_ref, *rest):
        o_ref = rest[-1]
        av, bv = a_ref[...].astype(bf16), b_ref[...].astype(bf16)
        if dims is None:
            r = jnp.dot(av, bv, preferred_element_type=f32)
        else:
            r = lax.dot_general(av, bv, dims, preferred_element_type=f32)
        o_ref[...] = r.astype(out_dtype)

    extra = [] if after is None else [after]
    return pl.pallas_call(
        body, name=name, grid=(m // tm, n // tn), in_specs=[a_spec, b_spec] + [pl.BlockSpec(memory_space=pl.ANY)] * len(extra),
        out_specs=pl.BlockSpec((tm, tn), lambda i, j: (i, j)),
        out_shape=jax.ShapeDtypeStruct((m, n), out_dtype), compiler_params=_params(2),
    )(a, b, *extra)


def _rstd(v):
    return lax.rsqrt(jnp.mean(v * v, axis=-1, keepdims=True) + RMS_EPS)


def _row_spec(tm, d):
    return pl.BlockSpec((tm, d), lambda i: (i, 0))


def _vec_spec(d, rows=1):
    return pl.BlockSpec((rows, d), lambda i: (0, 0))


def _proj_spec(a, w, tm):
    return [_row_spec(tm, a.shape[1]), pl.BlockSpec(w.shape, lambda i: (0, 0))]


def _out_proj_postnorm_prenorm(a, w, x, g_post, gate, g_pre, scale, shift, name):
    s, d = x.shape
    tm = _row_tile(s, 512)

    def body(a_ref, w_ref, x_ref, gp_ref, gate_ref, g_ref, sc_ref, sh_ref, y_ref, x2_ref, h_ref):
        yv = jnp.dot(a_ref[...], w_ref[...], preferred_element_type=f32)
        y_ref[...] = yv
        x2 = x_ref[...] + gate_ref[...] * (yv * _rstd(yv) * gp_ref[...])
        x2_ref[...] = x2
        h_ref[...] = ((x2 * _rstd(x2) * g_ref[...]) * (1.0 + sc_ref[...]) + sh_ref[...]).astype(bf16)

    return pl.pallas_call(
        body, name=name, grid=(s // tm,), in_specs=_proj_spec(a, w, tm) + [_row_spec(tm, d)] + [_vec_spec(d)] * 5,
        out_specs=[_row_spec(tm, d)] * 3,
        out_shape=[jax.ShapeDtypeStruct((s, d), f32)] * 2 + [jax.ShapeDtypeStruct((s, d), bf16)], compiler_params=_params(1),
    )(a, w, x, g_post, gate, g_pre, scale, shift)


def _rms_bwd(u, v, r):
    return r * u - v * (r * r * r) * jnp.mean(u * v, axis=-1, keepdims=True)


def _out_proj_loss_tail(a, w, x, g, gate, target, name):
    s, d = x.shape
    tm = _row_tile(s, 512)

    def body(a_ref, w_ref, x_ref, g_ref, gate_ref, t_ref, loss_ref, do_ref, dy_ref, vec_ref):
        @pl.when(pl.program_id(0) == 0)
        def _():
            loss_ref[...] = jnp.zeros_like(loss_ref)
            vec_ref[...] = jnp.zeros_like(vec_ref)
        yv = jnp.dot(a_ref[...], w_ref[...], preferred_element_type=f32)
        r = _rstd(yv)
        yn = yv * r
        err = x_ref[...] + gate_ref[...] * (yn * g_ref[...]) - t_ref[...]
        loss_ref[...] += 0.5 * jnp.sum(jnp.mean(err * err, axis=-1, keepdims=True), axis=0, keepdims=True)
        dr = err / d
        do_ref[...] = dr
        dn = dr * gate_ref[...]
        vec_ref[0:1, :] += jnp.sum(dr * (yn * g_ref[...]), axis=0, keepdims=True)
        vec_ref[1:2, :] += jnp.sum(dn * yn, axis=0, keepdims=True)
        dy_ref[...] = _rms_bwd(dn * g_ref[...], yv, r).astype(bf16)

    return pl.pallas_call(
        body, name=name, grid=(s // tm,),
        in_specs=_proj_spec(a, w, tm) + [_row_spec(tm, d)] + [_vec_spec(d)] * 2 + [_row_spec(tm, d)],
        out_specs=[_vec_spec(LANES), _row_spec(tm, d), _row_spec(tm, d), _vec_spec(d, 8)],
        out_shape=[jax.ShapeDtypeStruct((1, LANES), f32), jax.ShapeDtypeStruct((s, d), f32),
                   jax.ShapeDtypeStruct((s, d), bf16), jax.ShapeDtypeStruct((8, d), f32)],
        compiler_params=_params(1),
    )(a, w, x, g, gate, target)


def _dgrad_prenorm_bwd(terms, x, g, scale, dres, name, after=None, below=None):
    s, d = x.shape
    n = len(terms)
    k = sum(a.shape[1] for a, _, _ in terms)
    row_bytes = 2 * (2 * k) + d * (4 + 2 * 4 * 3 + (2 * 4 + 2 * 2 if below else 0))
    tm = next(t for t in (512, 256, 128) if s % t == 0 and 4 * k * d + t * row_bytes <= MATMUL_VMEM_BUDGET)
    extra = [] if after is None else [after]

    def body(*refs):
        a_refs, b_refs = refs[:n], refs[n:2 * n]
        x_ref, g_ref, sc_ref, dr_ref = refs[2 * n:2 * n + 4]
        n_in = 2 * n + 4 + (3 if below else 0) + len(extra)
        dx_ref, vec_ref = refs[n_in], refs[n_in + 1]
        if below:
            y_ref, gp_ref, gate_ref = refs[2 * n + 4:2 * n + 7]
            dy_ref, vec2_ref = refs[n_in + 2], refs[n_in + 3]

        @pl.when(pl.program_id(0) == 0)
        def _():
            vec_ref[...] = jnp.zeros_like(vec_ref)
            if below:
                vec2_ref[...] = jnp.zeros_like(vec2_ref)
        dhv = jnp.dot(a_refs[0][...], b_refs[0][...], preferred_element_type=f32)
        for i in range(1, n):
            dhv = dhv + jnp.dot(a_refs[i][...], b_refs[i][...], preferred_element_type=f32)
        xv = x_ref[...]
        r = _rstd(xv)
        xn = xv * r
        dn = dhv * (1.0 + sc_ref[...])
        vec_ref[0:1, :] += jnp.sum(dhv, axis=0, keepdims=True)
        vec_ref[1:2, :] += jnp.sum(dhv * (xn * g_ref[...]), axis=0, keepdims=True)
        vec_ref[2:3, :] += jnp.sum(dn * xn, axis=0, keepdims=True)
        dx = dr_ref[...] + _rms_bwd(dn * g_ref[...], xv, r)
        dx_ref[...] = dx
        if below:
            yv = y_ref[...]
            ry = _rstd(yv)
            yn = yv * ry
            dny = dx * gate_ref[...]
            vec2_ref[0:1, :] += jnp.sum(dx * (yn * gp_ref[...]), axis=0, keepdims=True)
            vec2_ref[1:2, :] += jnp.sum(dny * yn, axis=0, keepdims=True)
            dy_ref[...] = _rms_bwd(dny * gp_ref[...], yv, ry).astype(bf16)

    in_specs = ([_row_spec(tm, a.shape[1]) for a, _, _ in terms]
                + [pl.BlockSpec((a.shape[1], d), lambda i, r=r: (r, 0)) for a, _, r in terms]
                + [_row_spec(tm, d)] + [_vec_spec(d)] * 2 + [_row_spec(tm, d)])
    out_specs = [_row_spec(tm, d), _vec_spec(d, 8)]
    out_shape = [jax.ShapeDtypeStruct((s, d), f32), jax.ShapeDtypeStruct((8, d), f32)]
    args = [a for a, _, _ in terms] + [b for _, b, _ in terms] + [x, g, scale, dres]
    if below:
        in_specs += [_row_spec(tm, d)] + [_vec_spec(d)] * 2
        out_specs += [_row_spec(tm, d), _vec_spec(d, 8)]
        out_shape += [jax.ShapeDtypeStruct((s, d), bf16), jax.ShapeDtypeStruct((8, d), f32)]
        args += list(below)
    return pl.pallas_call(
        body, name=name, grid=(s // tm,), in_specs=in_specs + [pl.BlockSpec(memory_space=pl.ANY)] * len(extra),
        out_specs=out_specs, out_shape=out_shape, compiler_params=_params(1),
    )(*args, *extra)


def _lane():
    return lax.broadcasted_iota(jnp.int32, (1, LANES), 1)


def _rope_tables(pos_col, inv_freq, name):
    s = pos_col.shape[0]

    def body(p_ref, f_ref, cos_ref, sin_ref):
        ang = p_ref[...].astype(f32) * f_ref[...]
        first_half = (_lane() % HEAD_DIM) < HEAD_DIM // 2
        cos_ref[...] = jnp.cos(ang)
        sn = jnp.sin(ang)
        sin_ref[...] = jnp.where(first_half, -sn, sn)

    return pl.pallas_call(
        body, name=name, out_shape=[jax.ShapeDtypeStruct((s, LANES), f32)] * 2, compiler_params=_params(),
    )(pos_col, inv_freq)


def _swap_halves(v):
    first_half = (_lane() % HEAD_DIM) < HEAD_DIM // 2
    return jnp.where(first_half, pltpu.roll(v, LANES - HEAD_DIM // 2, axis=1), pltpu.roll(v, HEAD_DIM // 2, axis=1))


def _prenorm_proj_qkv(x, g, mod_scale, mod_shift, w_qkv_t, cos, sin_s, name):
    s, d = x.shape
    tm = _row_tile(s, 512)
    scale = 1.0 / math.sqrt(HEAD_DIM)

    def body(x_ref, g_ref, msc_ref, msh_ref, w_ref, c_ref, s_ref, h_ref, qa_ref, ka_ref, va_ref, qb_ref, kb_ref, vb_ref):
        xv = x_ref[...]
        h = ((xv * _rstd(xv) * g_ref[...]) * (1.0 + msc_ref[...]) + msh_ref[...]).astype(bf16)
        h_ref[...] = h
        proj = lax.dot_general(h, w_ref[...], _NT, preferred_element_type=f32)
        cs, sn = c_ref[...], s_ref[...]
        low = _lane() < HEAD_DIM

        def blk(j):
            return proj[:, j * LANES:(j + 1) * LANES]

        def rope(v):
            return v * cs + _swap_halves(v) * sn

        def expand(v):
            other = pltpu.roll(v, HEAD_DIM, axis=1)
            return jnp.where(low, v, other), jnp.where(low, other, v)

        for j in range(N_PAIRS):
            qa_ref[:, j * LANES:(j + 1) * LANES] = (rope(blk(j)) * scale).astype(bf16)
            qb_ref[:, j * LANES:(j + 1) * LANES] = (blk(6 + j) * scale).astype(bf16)
            kb_ref[:, j * LANES:(j + 1) * LANES] = blk(10 + j).astype(bf16)
            vb_ref[:, j * LANES:(j + 1) * LANES] = blk(14 + j).astype(bf16)
        k0, k1 = expand(rope(blk(4)))
        v0, v1 = expand(blk(5))
        for j in range(N_PAIRS):
            ka_ref[:, j * LANES:(j + 1) * LANES] = (k0 if j < 2 else k1).astype(bf16)
            va_ref[:, j * LANES:(j + 1) * LANES] = (v0 if j < 2 else v1).astype(bf16)

    hw = N_PAIRS * LANES
    return pl.pallas_call(
        body, name=name, grid=(s // tm,),
        in_specs=[_row_spec(tm, d)] + [_vec_spec(d)] * 3
        + [pl.BlockSpec((QKV_W, d), lambda i: (0, 0)), _row_spec(tm, LANES), _row_spec(tm, LANES)],
        out_specs=[_row_spec(tm, d)] + [_row_spec(tm, hw)] * 6,
        out_shape=[jax.ShapeDtypeStruct((s, d), bf16)] + [jax.ShapeDtypeStruct((s, hw), bf16)] * 6, compiler_params=_params(1),
    )(x, g, mod_scale, mod_shift, w_qkv_t, cos, sin_s)


def _qkv_prep_bwd(dqa_t, dka, dva, dqb_t, dkb, dvb, cos, sin_s, name):
    s = dka.shape[0]
    tm = _row_tile(s, 256)
    scale = 1.0 / math.sqrt(HEAD_DIM)
    hw = N_PAIRS * LANES
    t_spec = pl.BlockSpec((hw, tm), lambda i: (0, i))

    def body(dqa_ref, dka_ref, dva_ref, dqb_ref, dkb_ref, dvb_ref, c_ref, s_ref, o_ref):
        cs, sn = c_ref[...], s_ref[...]
        low = _lane() < HEAD_DIM

        def blk(ref, j):
            return ref[:, j * LANES:(j + 1) * LANES].astype(f32)

        def blk_t(ref, j):
            return ref[j * LANES:(j + 1) * LANES, :].T

        def unrope(v):
            return v * cs + _swap_halves(v * sn)

        def fold(ref):
            a, b = blk(ref, 0) + blk(ref, 1), blk(ref, 2) + blk(ref, 3)
            kv0 = a + pltpu.roll(a, HEAD_DIM, axis=1)
            kv1 = b + pltpu.roll(b, HEAD_DIM, axis=1)
            return jnp.where(low, kv0, kv1)

        for j in range(N_PAIRS):
            o_ref[:, j * LANES:(j + 1) * LANES] = (unrope(blk_t(dqa_ref, j)) * scale).astype(bf16)
            o_ref[:, (6 + j) * LANES:(7 + j) * LANES] = (blk_t(dqb_ref, j) * scale).astype(bf16)
            o_ref[:, (10 + j) * LANES:(11 + j) * LANES] = blk(dkb_ref, j).astype(bf16)
            o_ref[:, (14 + j) * LANES:(15 + j) * LANES] = blk(dvb_ref, j).astype(bf16)
        o_ref[:, 4 * LANES:5 * LANES] = unrope(fold(dka_ref)).astype(bf16)
        o_ref[:, 5 * LANES:6 * LANES] = fold(dva_ref).astype(bf16)

    return pl.pallas_call(
        body, name=name, grid=(s // tm,),
        in_specs=[t_spec, _row_spec(tm, hw), _row_spec(tm, hw), t_spec, _row_spec(tm, hw), _row_spec(tm, hw)] + [_row_spec(tm, LANES)] * 2,
        out_specs=_row_spec(tm, QKV_W), out_shape=jax.ShapeDtypeStruct((s, QKV_W), bf16), compiler_params=_params(1),
    )(dqa_t, dka, dva, dqb_t, dkb, dvb, cos, sin_s)


def _cumsum_rows(v, reverse=False):
    n = v.shape[0]
    row = lax.broadcasted_iota(jnp.int32, v.shape, 0)
    sh = 1
    while sh < n:
        if reverse:
            v = v + jnp.where(row < n - sh, pltpu.roll(v, n - sh, axis=0), 0.0)
        else:
            v = v + jnp.where(row >= sh, pltpu.roll(v, sh, axis=0), 0.0)
        sh *= 2
    return v


def _log_sigmoid(z):
    return jnp.minimum(z, 0.0) - jnp.log1p(jnp.exp(-jnp.abs(z)))


def _forget_prep(h, w_f_t, bf_row, name):
    s = h.shape[0]

    def body(h_ref, w_ref, b_ref, f_ref, cb_ref):
        fl = lax.dot_general(h_ref[...], w_ref[...], _NT, preferred_element_type=f32)
        f_ref[...] = fl
        cum = _cumsum_rows(_log_sigmoid(fl + b_ref[...]))
        for hd in range(N_HEADS):
            cb_ref[:, hd * LANES:(hd + 1) * LANES] = jnp.broadcast_to(cum[:, hd:hd + 1], (s, LANES))

    return pl.pallas_call(
        body, name=name,
        out_shape=[jax.ShapeDtypeStruct((s, LANES), f32), jax.ShapeDtypeStruct((s, N_HEADS * LANES), f32)],
        compiler_params=_params(),
    )(h, w_f_t, bf_row)


def _forget_prep_bwd(rs, dcs, fl, bf_row, name):
    s = fl.shape[0]

    def body(r_ref, c_ref, f_ref, b_ref, df_ref, db_ref):
        eye = (lax.broadcasted_iota(jnp.int32, (N_HEADS, LANES), 0) == lax.broadcasted_iota(jnp.int32, (N_HEADS, LANES), 1)).astype(f32)
        dcum = lax.dot_general(r_ref[...], eye, _TN, precision=lax.Precision.HIGHEST, preferred_element_type=f32)
        for h in range(N_HEADS):
            dcum = dcum - jnp.where(_lane() == h, jnp.sum(c_ref[:, h * LANES:(h + 1) * LANES], axis=1, keepdims=True), 0.0)
        dlf = _cumsum_rows(dcum, reverse=True)
        z = f_ref[...] + b_ref[...]
        df = jnp.where(_lane() < N_HEADS, dlf * jax.nn.sigmoid(-z), 0.0)
        df_ref[...] = df.astype(bf16)
        db_ref[...] = jnp.zeros_like(db_ref)
        db_ref[0:1, :] = jnp.sum(df, axis=0, keepdims=True)

    return pl.pallas_call(
        body, name=name,
        out_shape=[jax.ShapeDtypeStruct((s, LANES), bf16), jax.ShapeDtypeStruct((8, LANES), f32)], compiler_params=_params(),
    )(rs, dcs, fl, bf_row)


def _tile_mask(n_keys, n_queries, off, window):
    shape = (n_keys, n_queries)
    d = lax.broadcasted_iota(jnp.int32, shape, 1) - lax.broadcasted_iota(jnp.int32, shape, 0) + off
    valid = d >= 0
    return jnp.logical_and(valid, d < window) if window else valid


def _wide(v, t):
    return jnp.concatenate([v] * (t // LANES), axis=1)


def _attn_fwd(q, k, v, name, *, cum_b=None, sink_rows=None, window=None, t=256):
    s = q.shape[0]
    t = _row_tile(s, t)
    fox, has_sink = cum_b is not None, sink_rows is not None
    assert not window or (window % LANES == 0 and LANES + window <= s)

    def body(*refs):
        q_ref, k_ref, v_ref = refs[:3]
        rest = list(refs[3:])
        cb_ref = rest.pop(0) if fox else None
        sink_ref = rest.pop(0) if has_sink else None
        o_ref, lse_ref = rest
        i = pl.program_id(1)
        low = _lane() < HEAD_DIM
        top = lax.broadcasted_iota(jnp.int32, (LANES, 1), 0) < HEAD_DIM
        q2 = q_ref[...]
        zero = jnp.zeros_like(q2)
        qms = (jnp.where(low, q2, zero), jnp.where(low, zero, q2))

        def tile(k0, n_keys, off, carry, masked, queries=slice(0, t)):
            nq = queries.stop - queries.start
            kblk, vblk = k_ref[pl.ds(k0, n_keys), :], v_ref[pl.ds(k0, n_keys), :]
            valid = _tile_mask(n_keys, nq, off, window) if masked else None
            ones = jnp.ones_like(vblk)
            vs = tuple(jnp.where(_lane() == L_ROW[h], ones, vblk) for h in range(2))

            def scores(h):
                return lax.dot_general(kblk, qms[h][queries], _NT, preferred_element_type=f32)

            def softmax(h, sc):
                m = carry[h][0]
                if fox:
                    sc = sc - _wide(cb_ref[pl.ds(k0, n_keys), h * LANES:(h + 1) * LANES], nq)
                if masked:
                    sc = jnp.where(valid, sc, NEG)
                m_new = jnp.maximum(m, jnp.max(sc, axis=0, keepdims=True))
                return m_new, jnp.exp(m - m_new), jnp.exp(sc - m_new).astype(bf16)

            def update(h, m_new, alpha, p):
                return m_new, alpha * carry[h][1] + lax.dot_general(vs[h], p, _TN, preferred_element_type=f32)

            if window:
                return tuple(update(h, *softmax(h, scores(h))) for h in range(2))
            scs = [scores(h) for h in range(2)]
            stats = [softmax(h, scs[h]) for h in range(2)]
            return tuple(update(h, *stats[h]) for h in range(2))

        def start(nq):
            if has_sink:
                row = lax.broadcasted_iota(jnp.int32, (LANES, nq), 0)
                return tuple((_wide(sink_ref[h:h + 1, :], nq), (row == L_ROW[h]).astype(f32)) for h in range(2))
            return tuple((jnp.full((1, nq), NEG, f32), jnp.zeros((LANES, nq), f32)) for h in range(2))

        def finish(carry, queries):
            (m0, a0), (m1, a1) = carry
            l0, l1 = a0[L_ROW[0]:L_ROW[0] + 1, :], a1[L_ROW[1]:L_ROW[1] + 1, :]
            o_t = jnp.where(top, a0 * (1.0 / l0), a1 * (1.0 / l1))
            o_ref[queries, :] = o_t.T.astype(bf16)
            lse_ref[0:1, queries] = m0 + jnp.log(l0)
            lse_ref[1:2, queries] = m1 + jnp.log(l1)

        if window:
            for c in range(t // LANES):
                queries = slice(c * LANES, (c + 1) * LANES)
                q0 = i * t + c * LANES
                k0 = pl.multiple_of(jnp.maximum(q0 - window, 0), LANES)
                finish(tile(k0, LANES + window, q0 - k0, start(LANES), True, queries), queries)
        else:
            carry = lax.fori_loop(0, i, lambda kb, c: tile(pl.multiple_of(kb * t, t), t, 0, c, False), start(t))
            finish(tile(pl.multiple_of(i * t, t), t, 0, carry, True), slice(0, t))

    q_spec = pl.BlockSpec((t, LANES), lambda j, i: (i, j))
    kv_spec = pl.BlockSpec((s, LANES), lambda j, i: (0, j))
    in_specs, args = [q_spec, kv_spec, kv_spec], [q, k, v]
    if fox:
        in_specs += [pl.BlockSpec((s, 2 * LANES), lambda j, i: (0, j))]
        args += [cum_b]
    if has_sink:
        in_specs += [pl.BlockSpec((None, 2, LANES), lambda j, i: (j, 0, 0))]
        args += [sink_rows.reshape(N_PAIRS, 2, LANES)]
    return pl.pallas_call(
        body, name=name, grid=(N_PAIRS, s // t), in_specs=in_specs,
        out_specs=[q_spec, pl.BlockSpec((None, 2, t), lambda j, i: (j, 0, i))],
        out_shape=[jax.ShapeDtypeStruct((s, N_PAIRS * LANES), bf16), jax.ShapeDtypeStruct((N_PAIRS, 2, s), f32)],
        compiler_params=_params(2),
    )(*args)


def _branch_dgrad_delta(db, w, o, name, *, lse=None, sink_rows=None, after=None):
    s, hw = o.shape
    tm = _row_tile(s, 512)
    has_sink = sink_rows is not None
    extra = [] if after is None else [after]

    def body(*refs):
        db_ref, w_ref, o_ref = refs[:3]
        outs = refs[3 + (2 if has_sink else 0) + len(extra):]
        do_ref, dl_ref = outs[:2]
        if has_sink:
            lse_ref, sink_ref = refs[3:5]
            ds_ref = outs[2]

            @pl.when(pl.program_id(0) == 0)
            def _():
                ds_ref[...] = jnp.zeros_like(ds_ref)
        do = lax.dot_general(db_ref[...], w_ref[...], _NT, preferred_element_type=f32).astype(bf16)
        do_ref[...] = do
        for j in range(N_PAIRS):
            cols = slice(j * LANES, (j + 1) * LANES)
            prod_t = (do[:, cols].astype(f32) * o_ref[:, cols].astype(f32)).T
            for h in range(2):
                dl = jnp.sum(prod_t[h * HEAD_DIM:(h + 1) * HEAD_DIM, :], axis=0, keepdims=True)
                dl_ref[j, h:h + 1, :] = dl
                if has_sink:
                    r = 2 * j + h
                    p_sink = jnp.exp(sink_ref[r:r + 1, 0:1] - lse_ref[j, h:h + 1, :])
                    ds_ref[r:r + 1, :] += -jnp.sum(p_sink * dl, axis=1, keepdims=True)

    rows_spec = pl.BlockSpec((N_PAIRS, 2, tm), lambda i: (0, 0, i))
    in_specs = [_row_spec(tm, db.shape[1]), pl.BlockSpec(w.shape, lambda i: (0, 0)), _row_spec(tm, hw)]
    args = [db, w, o]
    out_specs = [_row_spec(tm, hw), rows_spec]
    out_shape = [jax.ShapeDtypeStruct((s, hw), bf16), jax.ShapeDtypeStruct((N_PAIRS, 2, s), f32)]
    if has_sink:
        in_specs += [rows_spec, _vec_spec(LANES, N_HEADS)]
        args += [lse, sink_rows]
        out_specs += [_vec_spec(LANES, N_HEADS)]
        out_shape += [jax.ShapeDtypeStruct((N_HEADS, LANES), f32)]
    return pl.pallas_call(
        body, name=name, grid=(s // tm,), in_specs=in_specs + [pl.BlockSpec(memory_space=pl.ANY)] * len(extra),
        out_specs=out_specs, out_shape=out_shape, compiler_params=_params(1),
    )(*args, *extra)


def _attn_bwd(q, k, v, do, lse, delta, name, *, cum_b=None, window=None, t=256):
    s = q.shape[0]
    t = _row_tile(s, t)
    nblk = s // t
    fox = cum_b is not None
    assert not window or (window % LANES == 0 and LANES + window <= s)

    def body(*refs):
        k_ref, v_ref, q_ref, do_ref, lse_ref, dl_ref = refs[:6]
        rest = list(refs[6:])
        cb_ref = rest.pop(0) if fox else None
        dq_ref, dk_ref, dv_ref = rest[:3]
        dcs_ref, rs_ref = (rest[3], rest[4]) if fox else (None, None)
        dk_acc, dv_acc = rest[-2:]
        b = pl.program_id(1)
        k0 = pl.multiple_of(b * t, t)

        @pl.when(b == 0)
        def _():
            dq_ref[...] = jnp.zeros_like(dq_ref)
            if fox:
                rs_ref[...] = jnp.zeros_like(rs_ref)

        dk_acc[...] = jnp.zeros_like(dk_acc)
        dv_acc[...] = jnp.zeros_like(dv_acc)
        if fox:
            dcs_ref[...] = jnp.zeros_like(dcs_ref)
        low = _lane() < HEAD_DIM
        top = lax.broadcasted_iota(jnp.int32, (LANES, 1), 0) < HEAD_DIM
        kblk, vblk = k_ref[...], v_ref[...]
        k_t = kblk.astype(f32).T.astype(bf16)
        cks = [_wide(cb_ref[pl.ds(k0, t), h * LANES:(h + 1) * LANES], t) for h in range(2)] if fox else None

        def tile(q0, n_queries, off, masked, keys=slice(0, t)):
            cols = pl.ds(q0, n_queries)
            q2, do2 = q_ref[cols, :], do_ref[cols, :]
            zero = jnp.zeros_like(q2)
            valid = _tile_mask(keys.stop - keys.start, n_queries, off, window) if masked else None
            dq_parts = []
            for h in range(2):
                qm = jnp.where(low, q2, zero) if h == 0 else jnp.where(low, zero, q2)
                dom = jnp.where(low, do2, zero) if h == 0 else jnp.where(low, zero, do2)
                sc = lax.dot_general(kblk[keys], qm, _NT, preferred_element_type=f32)
                if fox:
                    sc = sc - cks[h]
                if masked:
                    sc = jnp.where(valid, sc, NEG)
                p = jnp.exp(sc - lse_ref[h:h + 1, cols])
                dp = lax.dot_general(vblk[keys], dom, _NT, preferred_element_type=f32)
                ds = p * (dp - dl_ref[h:h + 1, cols])
                pb, dsb = p.astype(bf16), ds.astype(bf16)
                dv_acc[keys, :] += jnp.dot(pb, dom, preferred_element_type=f32)
                dk_acc[keys, :] += jnp.dot(dsb, qm, preferred_element_type=f32)
                dq_parts.append(jnp.dot(k_t[:, keys], dsb, preferred_element_type=f32))
                if fox:
                    dcs_ref[:, h * LANES:(h + 1) * LANES] += sum(ds[:, g * LANES:(g + 1) * LANES] for g in range(t // LANES))
                    rs_ref[h:h + 1, cols] += jnp.sum(ds, axis=0, keepdims=True)
            dq_ref[:, cols] += jnp.where(top, dq_parts[0], dq_parts[1])

        def later_block(qb, carry):
            tile(pl.multiple_of(qb * t, t), t, 0, False)
            return carry

        if window:
            for c in range(t // LANES):
                first = b * t + c * LANES
                q0 = pl.multiple_of(jnp.minimum(first, s - (LANES + window)), LANES)
                tile(q0, LANES + window, q0 - first, True, slice(c * LANES, (c + 1) * LANES))
        else:
            tile(k0, t, 0, True)
            lax.fori_loop(b + 1, nblk, later_block, 0)
        dk_ref[...] = dk_acc[...].astype(bf16)
        dv_ref[...] = dv_acc[...].astype(bf16)

    kv_spec = pl.BlockSpec((t, LANES), lambda j, b: (b, j))
    seq_spec = pl.BlockSpec((s, LANES), lambda j, b: (0, j))
    rows_spec = pl.BlockSpec((None, 2, s), lambda j, b: (j, 0, 0))
    hw = N_PAIRS * LANES
    in_specs, args = [kv_spec, kv_spec, seq_spec, seq_spec, rows_spec, rows_spec], [k, v, q, do, lse, delta]
    out_specs = [pl.BlockSpec((LANES, s), lambda j, b: (j, 0)), kv_spec, kv_spec]
    out_shape = [jax.ShapeDtypeStruct((hw, s), f32), jax.ShapeDtypeStruct((s, hw), bf16), jax.ShapeDtypeStruct((s, hw), bf16)]
    if fox:
        in_specs += [pl.BlockSpec((s, 2 * LANES), lambda j, b: (0, j))]
        args += [cum_b]
        out_specs += [pl.BlockSpec((t, 2 * LANES), lambda j, b: (b, j)), rows_spec]
        out_shape += [jax.ShapeDtypeStruct((s, N_HEADS * LANES), f32), jax.ShapeDtypeStruct((N_PAIRS, 2, s), f32)]
    return pl.pallas_call(
        body, name=name, grid=(N_PAIRS, nblk), in_specs=in_specs, out_specs=out_specs, out_shape=out_shape,
        scratch_shapes=[pltpu.VMEM((t, LANES), f32)] * 2, compiler_params=_params(2),
    )(*args)


def _branch_merge(o_a, o_b, w_a, w_b, gl, name):
    s, k = o_a.shape
    d = w_a.shape[1]
    tm = _row_tile(s, 1024)

    def body(oa_ref, ob_ref, wa_ref, wb_ref, g_ref, ba_ref, bb_ref, m_ref):
        ba = jnp.dot(oa_ref[...], wa_ref[...], preferred_element_type=f32)
        bb = jnp.dot(ob_ref[...], wb_ref[...], preferred_element_type=f32)
        g0, g1 = jax.nn.sigmoid(g_ref[:, :d].astype(f32)), jax.nn.sigmoid(g_ref[:, d:].astype(f32))
        ba_ref[...] = ba.astype(bf16)
        bb_ref[...] = bb.astype(bf16)
        m_ref[...] = (g0 * ba + g1 * bb).astype(bf16)

    whole = pl.BlockSpec((k, d), lambda i: (0, 0))
    return pl.pallas_call(
        body, name=name, grid=(s // tm,),
        in_specs=[_row_spec(tm, k), _row_spec(tm, k), whole, whole, _row_spec(tm, 2 * d)],
        out_specs=[_row_spec(tm, d)] * 3, out_shape=[jax.ShapeDtypeStruct((s, d), bf16)] * 3, compiler_params=_params(1),
    )(o_a, o_b, w_a, w_b, gl)


def _out_dgrad_merge_bwd(dy, w_out, ba, bb, gl, name):
    s, d = ba.shape
    tm = _row_tile(s, 512)

    def body(dy_ref, w_ref, a_ref, b_ref, g_ref, da_ref, db_ref, dg_ref):
        dmv = lax.dot_general(dy_ref[...], w_ref[...], _NT, preferred_element_type=f32)
        g0, g1 = jax.nn.sigmoid(g_ref[:, :d].astype(f32)), jax.nn.sigmoid(g_ref[:, d:].astype(f32))
        da_ref[...] = (dmv * g0).astype(bf16)
        db_ref[...] = (dmv * g1).astype(bf16)
        dg_ref[:, :d] = (dmv * a_ref[...].astype(f32) * (g0 * (1.0 - g0))).astype(bf16)
        dg_ref[:, d:] = (dmv * b_ref[...].astype(f32) * (g1 * (1.0 - g1))).astype(bf16)

    return pl.pallas_call(
        body, name=name, grid=(s // tm,),
        in_specs=[_row_spec(tm, dy.shape[1]), pl.BlockSpec(w_out.shape, lambda i: (0, 0))] + [_row_spec(tm, d)] * 2
        + [_row_spec(tm, 2 * d)],
        out_specs=[_row_spec(tm, d)] * 2 + [_row_spec(tm, 2 * d)],
        out_shape=[jax.ShapeDtypeStruct((s, d), bf16)] * 2 + [jax.ShapeDtypeStruct((s, 2 * d), bf16)],
        compiler_params=_params(1),
    )(dy, w_out, ba, bb, gl)


GLU_TILE = 256


def _ffn_in_swiglu(h, w_t, name):
    s, d = h.shape
    f = w_t.shape[0] // 2
    tm = _row_tile(s, 2048)
    tg = GLU_TILE
    nb = f // tg

    def body(h_ref, wg_ref, wu_ref, g_ref, u_ref, act_ref):
        hv = h_ref[...]
        g = lax.dot_general(hv, wg_ref[...], _NT, preferred_element_type=f32)
        u = lax.dot_general(hv, wu_ref[...], _NT, preferred_element_type=f32)
        g_ref[...] = g.astype(bf16)
        u_ref[...] = u.astype(bf16)
        act_ref[...] = (g * jax.nn.sigmoid(g) * u).astype(bf16)

    col = pl.BlockSpec((tm, tg), lambda i, j: (i, j))
    return pl.pallas_call(
        body, name=name, grid=(s // tm, nb),
        in_specs=[pl.BlockSpec((tm, d), lambda i, j: (i, 0)), pl.BlockSpec((tg, d), lambda i, j: (j, 0)),
                  pl.BlockSpec((tg, d), lambda i, j: (j + nb, 0))],
        out_specs=[col] * 3, out_shape=[jax.ShapeDtypeStruct((s, f), bf16)] * 3, compiler_params=_params(2),
    )(h, w_t, w_t)


def _ffn_out_dgrad_swiglu(dy, w_out, g, u, name):
    s, d = dy.shape
    f = g.shape[1]
    tm = _row_tile(s, 2048)
    tg = GLU_TILE

    def body(dy_ref, w_ref, g_ref, u_ref, dg_ref, du_ref):
        dv = lax.dot_general(dy_ref[...], w_ref[...], _NT, preferred_element_type=f32)
        gv, uv = g_ref[...].astype(f32), u_ref[...].astype(f32)
        sg = jax.nn.sigmoid(gv)
        dg_ref[...] = (dv * uv * (sg * (1.0 + gv * (1.0 - sg)))).astype(bf16)
        du_ref[...] = (dv * (gv * sg)).astype(bf16)

    col = pl.BlockSpec((tm, tg), lambda i, j: (i, j))
    return pl.pallas_call(
        body, name=name, grid=(s // tm, f // tg),
        in_specs=[pl.BlockSpec((tm, d), lambda i, j: (i, 0)), pl.BlockSpec((tg, d), lambda i, j: (j, 0)), col, col],
        out_specs=[col] * 2, out_shape=[jax.ShapeDtypeStruct((s, f), bf16)] * 2, compiler_params=_params(2),
    )(dy, w_out, g, u)


def _wgrad_stack(parts, h, name):
    s, m = parts[0].shape
    d = h.shape[1]
    tm = 256
    nb = m // tm
    n = len(parts)

    def body(*refs):
        i = pl.program_id(0)
        for p in range(n):
            @pl.when(i // nb == p)
            def _(p=p):
                refs[n + 1][...] = lax.dot_general(refs[p][...], refs[n][...], _TN, preferred_element_type=f32).astype(bf16)

    a_specs = [pl.BlockSpec((s, tm), lambda i, p=p: (0, jnp.clip(i - p * nb, 0, nb - 1))) for p in range(n)]
    return pl.pallas_call(
        body, name=name, grid=(n * nb,), in_specs=a_specs + [pl.BlockSpec((s, d), lambda i: (0, 0))],
        out_specs=pl.BlockSpec((tm, d), lambda i: (i, 0)),
        out_shape=jax.ShapeDtypeStruct((n * m, d), bf16), compiler_params=_params(1),
    )(*parts, h)


def _ada_wgrad(c_all, d_all, name):
    n, d = c_all.shape
    w = d_all.shape[1]

    def body(c_ref, d_ref, o_ref):
        eye = (lax.broadcasted_iota(jnp.int32, (n, n), 0) == lax.broadcasted_iota(jnp.int32, (n, n), 1)).astype(f32)
        ct = lax.dot_general(c_ref[...], eye, _TN, precision=lax.Precision.HIGHEST, preferred_element_type=f32)
        g = ct[:, 0:1] * d_ref[0:1, :]
        for bi in range(1, n):
            g = g + ct[:, bi:bi + 1] * d_ref[bi:bi + 1, :]
        o_ref[0] = g

    return pl.pallas_call(
        body, name=name, out_shape=jax.ShapeDtypeStruct((1, d, w), f32), compiler_params=_params(),
    )(c_all, d_all)


def _adamw(parts, w, m, v, name, mine=None, me=None):
    r, c = w.shape
    n_parts = parts.shape[0]
    row_tiles = [t for t in range(min(r, 256), 0, -1) if r % t == 0 and (t % 16 == 0 or t == r)]
    if row_tiles:
        tr, tc = row_tiles[0], c
    else:
        tr, tc = r, next(t for t in (256, LANES) if c % t == 0)

    def body(*refs):
        w_ref, m_ref, v_ref, g_ref, d_ref, nm_ref, nv_ref = refs[-7:]
        if mine is None:
            p_ref, = refs[:-7]
        else:
            me_ref, p_ref, own_ref = refs[:-7]

        def part(i):
            if mine is None:
                return p_ref[i].astype(f32)
            return jnp.where(me_ref[0] == i, own_ref[...], p_ref[i]).astype(f32)

        g = part(0)
        for i in range(1, n_parts):
            g = g + part(i)
        mm = ADAM_B1 * m_ref[...] + (1.0 - ADAM_B1) * g
        vv = ADAM_B2 * v_ref[...] + (1.0 - ADAM_B2) * (g * g)
        m_hat = mm / (1.0 - ADAM_B1 ** ADAM_STEP)
        v_hat = vv / (1.0 - ADAM_B2 ** ADAM_STEP)
        g_ref[...] = g
        d_ref[...] = -ADAM_LR * (m_hat / (jnp.sqrt(v_hat) + ADAM_EPS) + ADAM_WD * w_ref[...])
        nm_ref[...] = mm
        nv_ref[...] = vv

    out_shape = [jax.ShapeDtypeStruct((r, c), f32)] * 4
    if mine is None:
        spec = pl.BlockSpec((tr, tc), lambda i, j: (i, j))
        return pl.pallas_call(
            body, name=name, grid=(r // tr, c // tc),
            in_specs=[pl.BlockSpec((n_parts, tr, tc), lambda i, j: (0, i, j))] + [spec] * 3,
            out_specs=[spec] * 4, out_shape=out_shape, compiler_params=_params(2),
        )(parts, w, m, v)
    spec = pl.BlockSpec((tr, tc), lambda i, j, me_ref: (i, j))
    return pl.pallas_call(
        body, name=name, out_shape=out_shape, compiler_params=_params(2),
        grid_spec=pltpu.PrefetchScalarGridSpec(
            num_scalar_prefetch=1, grid=(r // tr, c // tc),
            in_specs=[pl.BlockSpec((n_parts, tr, tc), lambda i, j, me_ref: (0, i, j)),
                      pl.BlockSpec((None, tr, tc), lambda i, j, me_ref: (me_ref[0], i, j))] + [spec] * 3,
            out_specs=[spec] * 4),
    )(me, parts, mine, w, m, v)


def _me():
    return lax.axis_index("x"), lax.axis_index("y"), lax.axis_index("c")


def _all_gather(arrays, name, vmem=False, after=None):
    n = len(arrays)
    space = pltpu.VMEM if vmem else pl.ANY
    extra = [] if after is None else [after]

    def body(*refs):
        ins = refs[:n]
        outs = refs[n + len(extra):2 * n + len(extra)]
        send_sems, recv_sems, local_sems = refs[2 * n + len(extra):]
        x, y, c = _me()
        me, sibling = (x, y, c), (x, y, 1 - c)
        chips = [(1 - x, y), (x, 1 - y), (1 - x, 1 - y)]

        def rows(a, dev):
            return outs[a].at[4 * dev[0] + 2 * dev[1] + dev[2]]

        def copy(a, k, block, to, src=None):
            return pltpu.make_async_remote_copy(
                src_ref=rows(a, block) if src is None else src, dst_ref=rows(a, block),
                send_sem=send_sems.at[a, k], recv_sem=recv_sems.at[a, k], device_id=to, device_id_type=MESH)

        mine = [pltpu.make_async_copy(ins[a], rows(a, me), local_sems.at[a]) for a in range(n)]
        for cp in mine:
            cp.start()
        first = []
        for a in range(n):
            first.append(copy(a, 0, me, sibling, src=ins[a]))
            first += [copy(a, 1 + j, me, (*chip, c), src=ins[a]) for j, chip in enumerate(chips)]
        for cp in first:
            cp.start()
        passed = []
        for j, chip in enumerate(chips):
            for a in range(n):
                copy(a, 1 + j, (*chip, c), me).wait_recv()
                fwd = copy(a, 4 + j, (*chip, c), sibling)
                fwd.start()
                passed.append(fwd)
        for a in range(n):
            copy(a, 0, sibling, me).wait_recv()
            for j, chip in enumerate(chips):
                copy(a, 4 + j, (*chip, 1 - c), me).wait_recv()
        for cp in first + passed:
            cp.wait_send()
        for cp in mine:
            cp.wait()

    outs = pl.pallas_call(
        body, name=name,
        in_specs=[pl.BlockSpec(memory_space=space)] * n + [pl.BlockSpec(memory_space=pl.ANY)] * len(extra),
        out_specs=[pl.BlockSpec(memory_space=space)] * n,
        out_shape=[jax.ShapeDtypeStruct((N_DEV,) + a.shape, a.dtype) for a in arrays],
        scratch_shapes=[pltpu.SemaphoreType.DMA((n, 7)), pltpu.SemaphoreType.DMA((n, 7)), pltpu.SemaphoreType.DMA((n,))],
        compiler_params=pltpu.CompilerParams(vmem_limit_bytes=VMEM_LIMIT),
    )(*arrays, *extra)
    return list(outs)


def _gather_prologue(c, w_ada, b_mine, w_in_t, name):
    n_dev, d = N_DEV, c.shape[1]
    ada_w = w_ada.shape[1]

    def body(c_ref, w_ref, b_ref, win_ref, call_ref, ada_ref, gin_ref, cols_ref, send_sems, recv_sems, local_sems):
        x, y, cc = _me()
        me, sibling = (x, y, cc), (x, y, 1 - cc)
        chips = [(1 - x, y), (x, 1 - y), (1 - x, 1 - y)]
        outs = (call_ref, ada_ref, gin_ref)

        def rows(a, dev):
            return outs[a].at[4 * dev[0] + 2 * dev[1] + dev[2]]

        def copy(a, k, block, to, src=None):
            return pltpu.make_async_remote_copy(
                src_ref=rows(a, block) if src is None else src, dst_ref=rows(a, block),
                send_sem=send_sems.at[a, k], recv_sem=recv_sems.at[a, k], device_id=to, device_id_type=MESH)

        def begin(a, src):
            own = pltpu.make_async_copy(src, rows(a, me), local_sems.at[a])
            sends = [copy(a, 0, me, sibling, src=src)] + [copy(a, 1 + j, me, (*chip, cc), src=src) for j, chip in enumerate(chips)]
            for cp in [own] + sends:
                cp.start()
            return own, sends

        def finish(a, own, sends):
            passed = []
            for j, chip in enumerate(chips):
                copy(a, 1 + j, (*chip, cc), me).wait_recv()
                passed.append(copy(a, 4 + j, (*chip, cc), sibling))
                passed[-1].start()
            copy(a, 0, sibling, me).wait_recv()
            for j, chip in enumerate(chips):
                copy(a, 4 + j, (*chip, 1 - cc), me).wait_recv()
            for cp in sends + passed:
                cp.wait_send()
            own.wait()

        finish(0, *begin(0, c_ref))
        cols_ref[...] = (jnp.dot(call_ref[:, 0, :].astype(bf16), w_ref[...].astype(bf16), preferred_element_type=f32)
                         + b_ref[...])
        finish(1, *begin(1, cols_ref))
        finish(2, *begin(2, win_ref))

    vmem, hbm = pl.BlockSpec(memory_space=pltpu.VMEM), pl.BlockSpec(memory_space=pl.ANY)
    return pl.pallas_call(
        body, name=name, in_specs=[vmem, vmem, vmem, hbm], out_specs=[vmem, vmem, hbm],
        out_shape=[jax.ShapeDtypeStruct((n_dev, 1, d), f32), jax.ShapeDtypeStruct((n_dev, n_dev, ada_w), f32),
                   jax.ShapeDtypeStruct((n_dev,) + w_in_t.shape, w_in_t.dtype)],
        scratch_shapes=[pltpu.VMEM((n_dev, ada_w), f32), pltpu.SemaphoreType.DMA((3, 7)), pltpu.SemaphoreType.DMA((3, 7)),
                        pltpu.SemaphoreType.DMA((3,))],
        compiler_params=pltpu.CompilerParams(vmem_limit_bytes=VMEM_LIMIT),
    )(c, w_ada, b_mine, w_in_t)


_FLIPS = ((0, 0, 1), (1, 0, 0), (0, 1, 0), (1, 1, 0), (1, 0, 1), (0, 1, 1), (1, 1, 1))
_HBM = pl.BlockSpec(memory_space=pltpu.HBM)
_SEM = pl.BlockSpec(memory_space=pltpu.SEMAPHORE)


def _exchange_copies(scatter, srcs, lands, send_sems, recv_sems):
    x, y, c = _me()
    me_row = 4 * x + 2 * y + c
    out = []
    for k, (fx, fy, fc) in enumerate(_FLIPS):
        peer = (x ^ fx, y ^ fy, c ^ fc)
        peer_row = 4 * peer[0] + 2 * peer[1] + peer[2]
        for a in range(len(srcs)):
            out.append(pltpu.make_async_remote_copy(
                src_ref=srcs[a].at[peer_row] if scatter else srcs[a], dst_ref=lands[a].at[me_row],
                send_sem=send_sems.at[7 * a + k], recv_sem=recv_sems.at[7 * a + k], device_id=peer, device_id_type=MESH))
    return out


def _exchange_start(arrays, scatter, name, after=None):
    n = len(arrays)
    lands = [lax.empty(a.shape if scatter else (N_DEV,) + a.shape, a.dtype) for a in arrays]
    extra = [] if after is None else [after]

    def body(*refs):
        srcs, zones = refs[:n], refs[n:2 * n]
        send_sems, recv_sems = refs[2 * n + len(extra)], refs[2 * n + len(extra) + 1]
        token = refs[-1]
        for cp in _exchange_copies(scatter, srcs, zones, send_sems, recv_sems):
            cp.start()
        token[...] = jnp.zeros_like(token)

    thru = [pltpu.HBM(a.shape, a.dtype) for a in list(arrays) + lands]
    outs = pl.pallas_call(
        body, name=name,
        out_shape=(pltpu.SemaphoreType.DMA((7 * n,)), pltpu.SemaphoreType.DMA((7 * n,)), *thru, jax.ShapeDtypeStruct((8, LANES), f32)),
        in_specs=[_HBM] * (2 * n) + [pl.BlockSpec(memory_space=pl.ANY)] * len(extra),
        out_specs=(_SEM, _SEM, *[_HBM] * (2 * n), pl.BlockSpec(memory_space=pltpu.VMEM)),
        input_output_aliases={i: 2 + i for i in range(2 * n)},
        compiler_params=pltpu.CompilerParams(has_side_effects=pltpu.SideEffectType.DATAFLOW_SIDE_EFFECTING),
    )(*[pltpu.with_memory_space_constraint(a, pltpu.HBM) for a in list(arrays) + lands], *extra)
    return dict(n=n, scatter=scatter, sems=outs[:2], srcs=outs[2:2 + n], lands=outs[2 + n:2 + 2 * n], token=outs[-1])


def _exchange_wait(handle, after, name):
    n, scatter = handle["n"], handle["scatter"]

    def body(*refs):
        srcs, zones = refs[:n], refs[n:2 * n]
        send_sems, recv_sems = refs[2 * n], refs[2 * n + 1]
        for cp in _exchange_copies(scatter, srcs, zones, send_sems, recv_sems):
            cp.wait_send()
            cp.wait_recv()

    thru = [pltpu.HBM(a.shape, a.dtype) for a in list(handle["srcs"]) + list(handle["lands"])]
    outs = pl.pallas_call(
        body, name=name, out_shape=tuple(thru),
        in_specs=[_HBM] * (2 * n) + [_SEM, _SEM, pl.BlockSpec(memory_space=pl.ANY)], out_specs=tuple([_HBM] * (2 * n)),
        input_output_aliases={i: i for i in range(2 * n)},
        compiler_params=pltpu.CompilerParams(has_side_effects=pltpu.SideEffectType.DATAFLOW_SIDE_EFFECTING),
    )(*handle["srcs"], *handle["lands"], *handle["sems"], after)
    return list(outs[:n]), list(outs[n:])


def _cols_from_shards(g):
    return jnp.transpose(g, (1, 0, 2)).reshape(g.shape[1], -1)


def _shards_from_cols(a):
    return jnp.transpose(a.reshape(a.shape[0], N_DEV, -1), (1, 0, 2))


def _local_step(x, positions, ada, g_pre_mix, g_post_mix, b_f, sinks, g_pre_ffn, g_post_ffn, target,
                w_in_t, mix_weights, ffn_weights, on_grads):
    s, d = x.shape
    row = lambda v: v.reshape(1, -1)
    shift_m, scale_m, gate_m, shift_f, scale_f, gate_f = (ada[i:i + 1] for i in range(6))
    w_gate_t, w_qkv_t = w_in_t[F_OFF + N_HEADS:], w_in_t[:QKV_W]
    w_f_t = jnp.pad(w_in_t[F_OFF:F_OFF + N_HEADS], ((0, LANES - N_HEADS), (0, 0)))
    bf_row = jnp.pad(row(b_f), ((0, 0), (0, LANES - N_HEADS)))
    sink_rows = jnp.broadcast_to(sinks.reshape(N_HEADS, 1).astype(f32), (N_HEADS, LANES))
    inv_freq = 1.0 / (ROPE_THETA ** (jnp.arange(0, HEAD_DIM, 2, dtype=f32) / HEAD_DIM))
    cos, sin_s = _rope_tables(positions.reshape(s, 1), jnp.tile(inv_freq, 4).reshape(1, LANES), "rope_tables")

    h1, qa, ka, va, qb, kb, vb = _prenorm_proj_qkv(x, row(g_pre_mix), scale_m, shift_m, w_qkv_t, cos, sin_s, "prenorm_proj_qkv")
    gl = _matmul(h1, w_gate_t, "nt", bf16, "proj_gate")
    fl, cum_b = _forget_prep(h1, w_f_t, bf_row, "proj_forget_prep")
    o_a, lse_a = _attn_fwd(qa, ka, va, "swa_fwd", sink_rows=sink_rows, window=WINDOW, t=2048)
    o_b, lse_b = _attn_fwd(qb, kb, vb, "fox_fwd", cum_b=cum_b, t=1024)
    everything_before = (gl[:8, :LANES] + o_a[:8, :LANES] + o_b[:8, :LANES]).astype(f32)
    w_branch_a, w_branch_b, w_out = mix_weights(everything_before)
    ba, bb, merged = _branch_merge(o_a, o_b, w_branch_a, w_branch_b, gl, "branch_merge")
    y1, x2, h2 = _out_proj_postnorm_prenorm(merged, w_out, x, row(g_post_mix), gate_m, row(g_pre_ffn), scale_f, shift_f,
                                            "out_proj_norms")

    w_ffn_in_t, w_ffn_out = ffn_weights(h2)
    g_ff, u_ff, act = _ffn_in_swiglu(h2, w_ffn_in_t, "ffn_in_swiglu")
    loss_row, d_out, d_y2, vec_pf = _out_proj_loss_tail(act, w_ffn_out, x2, row(g_post_ffn), gate_f, target, "ffn_out_loss_tail")

    g_w_ffn_out = _matmul(act, d_y2, "tn", bf16, "ffn_out_wgrad")
    dg_ff, du_ff = _ffn_out_dgrad_swiglu(d_y2, w_ffn_out, g_ff, u_ff, "ffn_out_dgrad_swiglu")
    g_w_ffn_in_t = _wgrad_stack([dg_ff, du_ff], h2, "ffn_in_wgrad")
    sent = on_grads(dict(w_ffn_in=g_w_ffn_in_t, w_ffn_out=g_w_ffn_out))
    d_x2, vec_nf, d_y1, vec_pm = _dgrad_prenorm_bwd(
        [(dg_ff, w_ffn_in_t, 0), (du_ff, w_ffn_in_t, 1)], x2, row(g_pre_ffn), scale_f, d_out, "ffn_in_dgrad_norms_bwd",
        after=sent, below=(y1, row(g_post_mix), gate_m))

    g_w_out = _matmul(merged, d_y1, "tn", bf16, "out_proj_wgrad")
    d_ba, d_bb, dgl = _out_dgrad_merge_bwd(d_y1, w_out, ba, bb, gl, "out_proj_dgrad_merge_bwd")
    g_w_branch_a = _matmul(o_a, d_ba, "tn", bf16, "branch_a_wgrad")
    g_w_branch_b = _matmul(o_b, d_bb, "tn", bf16, "branch_b_wgrad")
    sent = on_grads(dict(w_out=g_w_out, w_branch_a=g_w_branch_a, w_branch_b=g_w_branch_b))
    d_oa, delta_a, d_sink = _branch_dgrad_delta(d_ba, w_branch_a, o_a, "branch_a_dgrad_delta", lse=lse_a,
                                                sink_rows=sink_rows, after=sent)
    d_ob, delta_b = _branch_dgrad_delta(d_bb, w_branch_b, o_b, "branch_b_dgrad_delta", after=sent)
    dqa_t, dka, dva = _attn_bwd(qa, ka, va, d_oa, lse_a, delta_a, "swa_bwd", window=WINDOW, t=2048)
    dqb_t, dkb, dvb, dcs, rs = _attn_bwd(qb, kb, vb, d_ob, lse_b, delta_b, "fox_bwd", cum_b=cum_b, t=512)
    dqkv = _qkv_prep_bwd(dqa_t, dka, dva, dqb_t, dkb, dvb, cos, sin_s, "qkv_prep_bwd")
    dfl, vec_bf = _forget_prep_bwd(rs.reshape(N_HEADS, s), dcs, fl, bf_row, "forget_prep_bwd")
    g_w_in_t = jnp.concatenate([_matmul(dqkv, h1, "tn", bf16, "qkv_wgrad"), _matmul(dfl, h1, "tn", bf16, "forget_wgrad")[:N_HEADS],
                                _matmul(dgl, h1, "tn", bf16, "gate_wgrad")], axis=0)
    sent = on_grads(dict(w_in=g_w_in_t))
    grad_x, vec_nm = _dgrad_prenorm_bwd([(dgl, w_gate_t, 0), (dqkv, w_qkv_t, 0), (dfl, w_f_t, 0)], x, row(g_pre_mix),
                                        scale_m, d_x2, "in_proj_dgrad_prenorm_bwd", after=sent)

    d_ada = jnp.concatenate([vec_nm[0], vec_nm[1], vec_pm[0], vec_nf[0], vec_nf[1], vec_pf[0]])
    small = dict(b_ada=d_ada, g_pre_mix=vec_nm[2], g_post_mix=vec_pm[1], g_pre_ffn=vec_nf[2], g_post_ffn=vec_pf[1],
                 b_f=vec_bf[0, :N_HEADS], sinks=d_sink[:, 0], loss=loss_row[0, :1])
    return grad_x, small


_SMALL = (("b_ada", 6144), ("g_pre_mix", 1024), ("g_post_mix", 1024), ("g_pre_ffn", 1024), ("g_post_ffn", 1024),
          ("b_f", 128), ("sinks", 128), ("loss", 128))
_SMALL_ROWS = 88


def _pack_small(vals):
    parts = [jnp.pad(vals[k].reshape(-1).astype(f32), (0, n - vals[k].size)) for k, n in _SMALL]
    flat = jnp.concatenate(parts)
    return jnp.pad(flat, (0, _SMALL_ROWS * LANES - flat.size)).reshape(_SMALL_ROWS, LANES)


def _unpack_small(slab, shapes):
    flat, out, off = slab.reshape(-1), {}, 0
    for k, n in _SMALL:
        size = math.prod(shapes[k])
        out[k] = flat[off:off + size].reshape(shapes[k])
        off += n
    return out


def kernel(x, c, positions, w_ada, b_ada, g_pre_mix, g_post_mix, w_in, b_f, sinks, w_branch_a, w_branch_b, w_out, g_pre_ffn, g_post_ffn, w_ffn_in, w_ffn_out, loss_target, m_w_ada, m_b_ada, m_g_pre_mix, m_g_post_mix, m_w_in, m_b_f, m_sinks, m_w_branch_a, m_w_branch_b, m_w_out, m_g_pre_ffn, m_g_post_ffn, m_w_ffn_in, m_w_ffn_out, v_w_ada, v_b_ada, v_g_pre_mix, v_g_post_mix, v_w_in, v_b_f, v_sinks, v_w_branch_a, v_w_branch_b, v_w_out, v_g_pre_ffn, v_g_post_ffn, v_w_ffn_in, v_w_ffn_out):
    xi, yi, ci = _me()
    me = 4 * xi + 2 * yi + ci
    d = D_MODEL
    ada_w = w_ada.shape[2]

    transposed = ("w_in", "w_ffn_in")
    tr = lambda a: jnp.transpose(a[0])

    b_mine = lax.dynamic_slice(b_ada, (0, me * ada_w), (1, ada_w))
    c_all, ada_all, g_in = _gather_prologue(c, w_ada[0], b_mine, tr(w_in).astype(bf16), "gather_prologue")
    c_all = c_all.reshape(N_DEV, d)
    ada = lax.dynamic_index_in_dim(ada_all, me, axis=1, keepdims=False).reshape(6, d)
    late_mix = [w.astype(bf16) for w in (w_branch_a[0], w_branch_b[0], w_out[0])]
    late_ffn = [w.astype(bf16) for w in (tr(w_ffn_in), w_ffn_out[0])]
    mix_h = _exchange_start(late_mix, False, "gather_mix_start", after=g_in)
    ffn_h = _exchange_start(late_ffn, False, "gather_ffn_start", after=mix_h["token"])

    def mine_into(zone, block):
        return lax.dynamic_update_index_in_dim(zone, block, me, 0)

    def rows_from_shards(g):
        return g.reshape(g.shape[0] * g.shape[1], g.shape[2])

    def mix_weights(after):
        sent, zones = _exchange_wait(mix_h, after, "gather_mix_wait")
        g_ba, g_bb, g_out = (mine_into(z, w) for z, w in zip(zones, sent))
        return _cols_from_shards(g_ba), _cols_from_shards(g_bb), rows_from_shards(g_out)

    def ffn_weights(after):
        sent, zones = _exchange_wait(ffn_h, after, "gather_ffn_wait")
        g_fi, g_fo = (mine_into(z, w) for z, w in zip(zones, sent))
        return rows_from_shards(g_fi), rows_from_shards(g_fo)

    row_sharded = ("w_out", "w_ffn_out") + transposed
    in_flight = []

    def on_grads(group):
        sends = [g.reshape(N_DEV, g.shape[0] // N_DEV, g.shape[1]) if nm in row_sharded else _shards_from_cols(g)
                 for nm, g in group.items()]
        handle = _exchange_start(sends, True, "scatter_start_%d" % len(in_flight))
        in_flight.append((list(group), handle))
        return handle["token"]

    grad_x, small = _local_step(
        x[0], positions[0], ada + ffn_h["token"][0, 0], g_pre_mix[0], g_post_mix[0], b_f[0], sinks[0], g_pre_ffn[0],
        g_post_ffn[0], loss_target[0], rows_from_shards(g_in), mix_weights, ffn_weights, on_grads)

    ws = dict(w_in=(w_in, m_w_in, v_w_in), w_branch_a=(w_branch_a, m_w_branch_a, v_w_branch_a),
              w_branch_b=(w_branch_b, m_w_branch_b, v_w_branch_b), w_out=(w_out, m_w_out, v_w_out),
              w_ffn_in=(w_ffn_in, m_w_ffn_in, v_w_ffn_in), w_ffn_out=(w_ffn_out, m_w_ffn_out, v_w_ffn_out))
    res = {}

    def finish_group(gi, after):
        names, handle = in_flight[gi]
        sends, zones = _exchange_wait(handle, after, "scatter_wait_%d" % gi)
        for nm, zone, sent in zip(names, zones, sends):
            w, m, v = (tr(a) if nm in transposed else a[0] for a in ws[nm])
            out = _adamw(zone, w, m, v, "adamw_" + nm, mine=sent, me=me.reshape(1).astype(jnp.int32))
            after = out[0]
            res[nm] = [jnp.transpose(o) for o in out] if nm in transposed else out
        return after

    done = finish_group(1, finish_group(0, grad_x))

    slab_all, = _all_gather([_pack_small(small)], "gather_small", vmem=True, after=done)
    small_w = dict(b_ada=b_ada, g_pre_mix=g_pre_mix, g_post_mix=g_post_mix, g_pre_ffn=g_pre_ffn, g_post_ffn=g_post_ffn,
                   b_f=b_f, sinks=sinks, loss=jnp.zeros((1,), f32))
    small_m = dict(b_ada=m_b_ada, g_pre_mix=m_g_pre_mix, g_post_mix=m_g_post_mix, g_pre_ffn=m_g_pre_ffn,
                   g_post_ffn=m_g_post_ffn, b_f=m_b_f, sinks=m_sinks, loss=jnp.zeros((1,), f32))
    small_v = dict(b_ada=v_b_ada, g_pre_mix=v_g_pre_mix, g_post_mix=v_g_post_mix, g_pre_ffn=v_g_pre_ffn,
                   g_post_ffn=v_g_post_ffn, b_f=v_b_f, sinks=v_sinks, loss=jnp.ones((1,), f32))
    shapes = {k: small_w[k].shape for k, _ in _SMALL}
    s_out = _adamw(slab_all, _pack_small(small_w), _pack_small(small_m), _pack_small(small_v), "adamw_small")
    s_grad, s_delta, s_m, s_v = (_unpack_small(o, shapes) for o in s_out)

    d_ada_all = lax.dynamic_slice(slab_all[:, :6144 // LANES, :].reshape(N_DEV, 6144), (0, me * ada_w), (N_DEV, ada_w))
    ada_parts = _ada_wgrad(c_all, d_ada_all, "ada_wgrad")

    res["w_ada"] = _adamw(ada_parts, w_ada[0], m_w_ada[0], v_w_ada[0], "adamw_w_ada")
    finish_group(2, res["w_ada"][0])

    order = ["w_ada", "b_ada", "g_pre_mix", "g_post_mix", "w_in", "b_f", "sinks", "w_branch_a", "w_branch_b", "w_out",
             "g_pre_ffn", "g_post_ffn", "w_ffn_in", "w_ffn_out"]
    outs = [s_grad["loss"].reshape(()), grad_x[None]]
    for which, small_o in enumerate((s_grad, s_delta, s_m, s_v)):
        for nm in order:
            outs.append(res[nm][which][None] if nm in res else small_o[nm])
    return tuple(outs)
```

```python
import math

import jax
import jax.numpy as jnp
from jax import lax
from jax.experimental import pallas as pl
from jax.experimental.pallas import tpu as pltpu

f32 = jnp.float32
bf16 = jnp.bfloat16

D_MODEL = 1024
HEAD_DIM = 64
N_HEADS = 8
N_PAIRS = 4
QKV_W = 2304
F_OFF = 2304
WINDOW = 128
ROPE_THETA = 10000.0
RMS_EPS = 1e-6
N_DEV = 8
ADAM_LR, ADAM_B1, ADAM_B2, ADAM_EPS, ADAM_WD, ADAM_STEP = 0.001, 0.9, 0.999, 1e-08, 0.01, 10
NEG = -1e30
L_ROW = (HEAD_DIM, 0)
LANES = 128
VMEM_LIMIT = 48 * 1024 * 1024
MESH = pl.DeviceIdType.MESH

_NT = (((1,), (1,)), ((), ()))
_TN = (((0,), (0,)), ((), ()))


def _params(n_grid=0):
    sem = ("arbitrary",) * n_grid if n_grid else None
    return pltpu.CompilerParams(dimension_semantics=sem, vmem_limit_bytes=VMEM_LIMIT)


def _row_tile(s, want):
    t = min(s, want)
    assert s % t == 0, (s, t)
    return t


MATMUL_VMEM_BUDGET = 40 * 1024 * 1024


def _matmul_tiles(m, n, k, a_item, b_item, o_item):
    def tiles(d):
        return [t for t in range(LANES, min(d, 2048) + 1, LANES) if d % t == 0] or [d]

    best = None
    for tm in tiles(m):
        for tn in tiles(n):
            vmem = 2 * (tm * k * a_item + tn * k * b_item + tm * tn * o_item) + tm * tn * 4
            if vmem > MATMUL_VMEM_BUDGET:
                continue
            traffic = m * k * a_item + n * k * b_item * (1 if tn == n else m // tm) + m * n * o_item
            steps = (m // tm) * (n // tn)
            key = (traffic, 0, steps) if steps >= 4 else (traffic, 1, -steps)
            if best is None or key < best[0]:
                best = (key, tm, tn)
    assert best is not None, (m, n, k)
    return best[1], best[2]


def _matmul(a, b, mode, out_dtype, name, after=None):
    if mode == "nn":
        (m, k), n = a.shape, b.shape[1]
    elif mode == "nt":
        (m, k), n = a.shape, b.shape[0]
    else:
        (k, m), n = a.shape, b.shape[1]
    tm, tn = _matmul_tiles(m, n, k, a.dtype.itemsize, b.dtype.itemsize, jnp.dtype(out_dtype).itemsize)
    if mode == "nn":
        a_spec, b_spec, dims = pl.BlockSpec((tm, k), lambda i, j: (i, 0)), pl.BlockSpec((k, tn), lambda i, j: (0, j)), None
    elif mode == "nt":
        a_spec, b_spec, dims = pl.BlockSpec((tm, k), lambda i, j: (i, 0)), pl.BlockSpec((tn, k), lambda i, j: (j, 0)), _NT
    else:
        a_spec, b_spec, dims = pl.BlockSpec((k, tm), lambda i, j: (0, i)), pl.BlockSpec((k, tn), lambda i, j: (0, j)), _TN

    def body(a_ref, b_ref, *rest):
        o_ref = rest[-1]
        av, bv = a_ref[...].astype(bf16), b_ref[...].astype(bf16)
        if dims is None:
            r = jnp.dot(av, bv, preferred_element_type=f32)
        else:
            r = lax.dot_general(av, bv, dims, preferred_element_type=f32)
        o_ref[...] = r.astype(out_dtype)

    extra = [] if after is None else [after]
    return pl.pallas_call(
        body, name=name, grid=(m // tm, n // tn), in_specs=[a_spec, b_spec] + [pl.BlockSpec(memory_space=pl.ANY)] * len(extra),
        out_specs=pl.BlockSpec((tm, tn), lambda i, j: (i, j)),
        out_shape=jax.ShapeDtypeStruct((m, n), out_dtype), compiler_params=_params(2),
    )(a, b, *extra)


def _rstd(v):
    return lax.rsqrt(jnp.mean(v * v, axis=-1, keepdims=True) + RMS_EPS)


def _row_spec(tm, d):
    return pl.BlockSpec((tm, d), lambda i: (i, 0))


def _vec_spec(d, rows=1):
    return pl.BlockSpec((rows, d), lambda i: (0, 0))


def _proj_spec(a, w, tm):
    return [_row_spec(tm, a.shape[1]), pl.BlockSpec(w.shape, lambda i: (0, 0))]


def _out_proj_postnorm_prenorm(a, w, x, g_post, gate, g_pre, scale, shift, name):
    s, d = x.shape
    tm = _row_tile(s, 512)

    def body(a_ref, w_ref, x_ref, gp_ref, gate_ref, g_ref, sc_ref, sh_ref, y_ref, x2_ref, h_ref):
        yv = jnp.dot(a_ref[...], w_ref[...], preferred_element_type=f32)
        y_ref[...] = yv
        x2 = x_ref[...] + gate_ref[...] * (yv * _rstd(yv) * gp_ref[...])
        x2_ref[...] = x2
        h_ref[...] = ((x2 * _rstd(x2) * g_ref[...]) * (1.0 + sc_ref[...]) + sh_ref[...]).astype(bf16)

    return pl.pallas_call(
        body, name=name, grid=(s // tm,), in_specs=_proj_spec(a, w, tm) + [_row_spec(tm, d)] + [_vec_spec(d)] * 5,
        out_specs=[_row_spec(tm, d)] * 3,
        out_shape=[jax.ShapeDtypeStruct((s, d), f32)] * 2 + [jax.ShapeDtypeStruct((s, d), bf16)], compiler_params=_params(1),
    )(a, w, x, g_post, gate, g_pre, scale, shift)


def _rms_bwd(u, v, r):
    return r * u - v * (r * r * r) * jnp.mean(u * v, axis=-1, keepdims=True)


def _out_proj_loss_tail(a, w, x, g, gate, target, name):
    s, d = x.shape
    tm = _row_tile(s, 512)

    def body(a_ref, w_ref, x_ref, g_ref, gate_ref, t_ref, loss_ref, do_ref, dy_ref, vec_ref):
        @pl.when(pl.program_id(0) == 0)
        def _():
            loss_ref[...] = jnp.zeros_like(loss_ref)
            vec_ref[...] = jnp.zeros_like(vec_ref)
        yv = jnp.dot(a_ref[...], w_ref[...], preferred_element_type=f32)
        r = _rstd(yv)
        yn = yv * r
        err = x_ref[...] + gate_ref[...] * (yn * g_ref[...]) - t_ref[...]
        loss_ref[...] += 0.5 * jnp.sum(jnp.mean(err * err, axis=-1, keepdims=True), axis=0, keepdims=True)
        dr = err / d
        do_ref[...] = dr
        dn = dr * gate_ref[...]
        vec_ref[0:1, :] += jnp.sum(dr * (yn * g_ref[...]), axis=0, keepdims=True)
        vec_ref[1:2, :] += jnp.sum(dn * yn, axis=0, keepdims=True)
        dy_ref[...] = _rms_bwd(dn * g_ref[...], yv, r).astype(bf16)

    return pl.pallas_call(
        body, name=name, grid=(s // tm,),
        in_specs=_proj_spec(a, w, tm) + [_row_spec(tm, d)] + [_vec_spec(d)] * 2 + [_row_spec(tm, d)],
        out_specs=[_vec_spec(LANES), _row_spec(tm, d), _row_spec(tm, d), _vec_spec(d, 8)],
        out_shape=[jax.ShapeDtypeStruct((1, LANES), f32), jax.ShapeDtypeStruct((s, d), f32),
                   jax.ShapeDtypeStruct((s, d), bf16), jax.ShapeDtypeStruct((8, d), f32)],
        compiler_params=_params(1),
    )(a, w, x, g, gate, target)


def _dgrad_prenorm_bwd(terms, x, g, scale, dres, name, after=None, below=None):
    s, d = x.shape
    n = len(terms)
    k = sum(a.shape[1] for a, _, _ in terms)
    row_bytes = 2 * (2 * k) + d * (4 + 2 * 4 * 3 + (2 * 4 + 2 * 2 if below else 0))
    tm = next(t for t in (512, 256, 128) if s % t == 0 and 4 * k * d + t * row_bytes <= MATMUL_VMEM_BUDGET)
    extra = [] if after is None else [after]

    def body(*refs):
        a_refs, b_refs = refs[:n], refs[n:2 * n]
        x_ref, g_ref, sc_ref, dr_ref = refs[2 * n:2 * n + 4]
        n_in = 2 * n + 4 + (3 if below else 0) + len(extra)
        dx_ref, vec_ref = refs[n_in], refs[n_in + 1]
        if below:
            y_ref, gp_ref, gate_ref = refs[2 * n + 4:2 * n + 7]
            dy_ref, vec2_ref = refs[n_in + 2], refs[n_in + 3]

        @pl.when(pl.program_id(0) == 0)
        def _():
            vec_ref[...] = jnp.zeros_like(vec_ref)
            if below:
                vec2_ref[...] = jnp.zeros_like(vec2_ref)
        dhv = jnp.dot(a_refs[0][...], b_refs[0][...], preferred_element_type=f32)
        for i in range(1, n):
            dhv = dhv + jnp.dot(a_refs[i][...], b_refs[i][...], preferred_element_type=f32)
        xv = x_ref[...]
        r = _rstd(xv)
        xn = xv * r
        dn = dhv * (1.0 + sc_ref[...])
        vec_ref[0:1, :] += jnp.sum(dhv, axis=0, keepdims=True)
        vec_ref[1:2, :] += jnp.sum(dhv * (xn * g_ref[...]), axis=0, keepdims=True)
        vec_ref[2:3, :] += jnp.sum(dn * xn, axis=0, keepdims=True)
        dx = dr_ref[...] + _rms_bwd(dn * g_ref[...], xv, r)
        dx_ref[...] = dx
        if below:
            yv = y_ref[...]
            ry = _rstd(yv)
            yn = yv * ry
            dny = dx * gate_ref[...]
            vec2_ref[0:1, :] += jnp.sum(dx * (yn * gp_ref[...]), axis=0, keepdims=True)
            vec2_ref[1:2, :] += jnp.sum(dny * yn, axis=0, keepdims=True)
            dy_ref[...] = _rms_bwd(dny * gp_ref[...], yv, ry).astype(bf16)

    in_specs = ([_row_spec(tm, a.shape[1]) for a, _, _ in terms]
                + [pl.BlockSpec((a.shape[1], d), lambda i, r=r: (r, 0)) for a, _, r in terms]
                + [_row_spec(tm, d)] + [_vec_spec(d)] * 2 + [_row_spec(tm, d)])
    out_specs = [_row_spec(tm, d), _vec_spec(d, 8)]
    out_shape = [jax.ShapeDtypeStruct((s, d), f32), jax.ShapeDtypeStruct((8, d), f32)]
    args = [a for a, _, _ in terms] + [b for _, b, _ in terms] + [x, g, scale, dres]
    if below:
        in_specs += [_row_spec(tm, d)] + [_vec_spec(d)] * 2
        out_specs += [_row_spec(tm, d), _vec_spec(d, 8)]
        out_shape += [jax.ShapeDtypeStruct((s, d), bf16), jax.ShapeDtypeStruct((8, d), f32)]
        args += list(below)
    return pl.pallas_call(
        body, name=name, grid=(s // tm,), in_specs=in_specs + [pl.BlockSpec(memory_space=pl.ANY)] * len(extra),
        out_specs=out_specs, out_shape=out_shape, compiler_params=_params(1),
    )(*args, *extra)


def _lane():
    return lax.broadcasted_iota(jnp.int32, (1, LANES), 1)


def _rope_tables(pos_col, inv_freq, name):
    s = pos_col.shape[0]

    def body(p_ref, f_ref, cos_ref, sin_ref):
        ang = p_ref[...].astype(f32) * f_ref[...]
        first_half = (_lane() % HEAD_DIM) < HEAD_DIM // 2
        cos_ref[...] = jnp.cos(ang)
        sn = jnp.sin(ang)
        sin_ref[...] = jnp.where(first_half, -sn, sn)

    return pl.pallas_call(
        body, name=name, out_shape=[jax.ShapeDtypeStruct((s, LANES), f32)] * 2, compiler_params=_params(),
    )(pos_col, inv_freq)


def _swap_halves(v):
    first_half = (_lane() % HEAD_DIM) < HEAD_DIM // 2
    return jnp.where(first_half, pltpu.roll(v, LANES - HEAD_DIM // 2, axis=1), pltpu.roll(v, HEAD_DIM // 2, axis=1))


def _prenorm_proj_qkv(x, g, mod_scale, mod_shift, w_qkv_t, cos, sin_s, name):
    s, d = x.shape
    tm = _row_tile(s, 512)
    scale = 1.0 / math.sqrt(HEAD_DIM)

    def body(x_ref, g_ref, msc_ref, msh_ref, w_ref, c_ref, s_ref, h_ref, qa_ref, ka_ref, va_ref, qb_ref, kb_ref, vb_ref):
        xv = x_ref[...]
        h = ((xv * _rstd(xv) * g_ref[...]) * (1.0 + msc_ref[...]) + msh_ref[...]).astype(bf16)
        h_ref[...] = h
        proj = lax.dot_general(h, w_ref[...], _NT, preferred_element_type=f32)
        cs, sn = c_ref[...], s_ref[...]
        low = _lane() < HEAD_DIM

        def blk(j):
            return proj[:, j * LANES:(j + 1) * LANES]

        def rope(v):
            return v * cs + _swap_halves(v) * sn

        def expand(v):
            other = pltpu.roll(v, HEAD_DIM, axis=1)
            return jnp.where(low, v, other), jnp.where(low, other, v)

        for j in range(N_PAIRS):
            qa_ref[:, j * LANES:(j + 1) * LANES] = (rope(blk(j)) * scale).astype(bf16)
            qb_ref[:, j * LANES:(j + 1) * LANES] = (blk(6 + j) * scale).astype(bf16)
            kb_ref[:, j * LANES:(j + 1) * LANES] = blk(10 + j).astype(bf16)
            vb_ref[:, j * LANES:(j + 1) * LANES] = blk(14 + j).astype(bf16)
        k0, k1 = expand(rope(blk(4)))
        v0, v1 = expand(blk(5))
        for j in range(N_PAIRS):
            ka_ref[:, j * LANES:(j + 1) * LANES] = (k0 if j < 2 else k1).astype(bf16)
            va_ref[:, j * LANES:(j + 1) * LANES] = (v0 if j < 2 else v1).astype(bf16)

    hw = N_PAIRS * LANES
    return pl.pallas_call(
        body, name=name, grid=(s // tm,),
        in_specs=[_row_spec(tm, d)] + [_vec_spec(d)] * 3
        + [pl.BlockSpec((QKV_W, d), lambda i: (0, 0)), _row_spec(tm, LANES), _row_spec(tm, LANES)],
        out_specs=[_row_spec(tm, d)] + [_row_spec(tm, hw)] * 6,
        out_shape=[jax.ShapeDtypeStruct((s, d), bf16)] + [jax.ShapeDtypeStruct((s, hw), bf16)] * 6, compiler_params=_params(1),
    )(x, g, mod_scale, mod_shift, w_qkv_t, cos, sin_s)


def _qkv_prep_bwd(dqa_t, dka, dva, dqb_t, dkb, dvb, cos, sin_s, name):
    s = dka.shape[0]
    tm = _row_tile(s, 256)
    scale = 1.0 / math.sqrt(HEAD_DIM)
    hw = N_PAIRS * LANES
    t_spec = pl.BlockSpec((hw, tm), lambda i: (0, i))

    def body(dqa_ref, dka_ref, dva_ref, dqb_ref, dkb_ref, dvb_ref, c_ref, s_ref, o_ref):
        cs, sn = c_ref[...], s_ref[...]
        low = _lane() < HEAD_DIM

        def blk(ref, j):
            return ref[:, j * LANES:(j + 1) * LANES].astype(f32)

        def blk_t(ref, j):
            return ref[j * LANES:(j + 1) * LANES, :].T

        def unrope(v):
            return v * cs + _swap_halves(v * sn)

        def fold(ref):
            a, b = blk(ref, 0) + blk(ref, 1), blk(ref, 2) + blk(ref, 3)
            kv0 = a + pltpu.roll(a, HEAD_DIM, axis=1)
            kv1 = b + pltpu.roll(b, HEAD_DIM, axis=1)
            return jnp.where(low, kv0, kv1)

        for j in range(N_PAIRS):
            o_ref[:, j * LANES:(j + 1) * LANES] = (unrope(blk_t(dqa_ref, j)) * scale).astype(bf16)
            o_ref[:, (6 + j) * LANES:(7 + j) * LANES] = (blk_t(dqb_ref, j) * scale).astype(bf16)
            o_ref[:, (10 + j) * LANES:(11 + j) * LANES] = blk(dkb_ref, j).astype(bf16)
            o_ref[:, (14 + j) * LANES:(15 + j) * LANES] = blk(dvb_ref, j).astype(bf16)
        o_ref[:, 4 * LANES:5 * LANES] = unrope(fold(dka_ref)).astype(bf16)
        o_ref[:, 5 * LANES:6 * LANES] = fold(dva_ref).astype(bf16)

    return pl.pallas_call(
        body, name=name, grid=(s // tm,),
        in_specs=[t_spec, _row_spec(tm, hw), _row_spec(tm, hw), t_spec, _row_spec(tm, hw), _row_spec(tm, hw)] + [_row_spec(tm, LANES)] * 2,
        out_specs=_row_spec(tm, QKV_W), out_shape=jax.ShapeDtypeStruct((s, QKV_W), bf16), compiler_params=_params(1),
    )(dqa_t, dka, dva, dqb_t, dkb, dvb, cos, sin_s)


def _cumsum_rows(v, reverse=False):
    n = v.shape[0]
    row = lax.broadcasted_iota(jnp.int32, v.shape, 0)
    sh = 1
    while sh < n:
        if reverse:
            v = v + jnp.where(row < n - sh, pltpu.roll(v, n - sh, axis=0), 0.0)
        else:
            v = v + jnp.where(row >= sh, pltpu.roll(v, sh, axis=0), 0.0)
        sh *= 2
    return v


def _log_sigmoid(z):
    return jnp.minimum(z, 0.0) - jnp.log1p(jnp.exp(-jnp.abs(z)))


def _forget_prep(h, w_f_t, bf_row, name):
    s = h.shape[0]

    def body(h_ref, w_ref, b_ref, f_ref, cb_ref):
        fl = lax.dot_general(h_ref[...], w_ref[...], _NT, preferred_element_type=f32)
        f_ref[...] = fl
        cum = _cumsum_rows(_log_sigmoid(fl + b_ref[...]))
        for hd in range(N_HEADS):
            cb_ref[:, hd * LANES:(hd + 1) * LANES] = jnp.broadcast_to(cum[:, hd:hd + 1], (s, LANES))

    return pl.pallas_call(
        body, name=name,
        out_shape=[jax.ShapeDtypeStruct((s, LANES), f32), jax.ShapeDtypeStruct((s, N_HEADS * LANES), f32)],
        compiler_params=_params(),
    )(h, w_f_t, bf_row)


def _forget_prep_bwd(rs, dcs, fl, bf_row, name):
    s = fl.shape[0]

    def body(r_ref, c_ref, f_ref, b_ref, df_ref, db_ref):
        eye = (lax.broadcasted_iota(jnp.int32, (N_HEADS, LANES), 0) == lax.broadcasted_iota(jnp.int32, (N_HEADS, LANES), 1)).astype(f32)
        dcum = lax.dot_general(r_ref[...], eye, _TN, precision=lax.Precision.HIGHEST, preferred_element_type=f32)
        for h in range(N_HEADS):
            dcum = dcum - jnp.where(_lane() == h, jnp.sum(c_ref[:, h * LANES:(h + 1) * LANES], axis=1, keepdims=True), 0.0)
        dlf = _cumsum_rows(dcum, reverse=True)
        z = f_ref[...] + b_ref[...]
        df = jnp.where(_lane() < N_HEADS, dlf * jax.nn.sigmoid(-z), 0.0)
        df_ref[...] = df.astype(bf16)
        db_ref[...] = jnp.zeros_like(db_ref)
        db_ref[0:1, :] = jnp.sum(df, axis=0, keepdims=True)

    return pl.pallas_call(
        body, name=name,
        out_shape=[jax.ShapeDtypeStruct((s, LANES), bf16), jax.ShapeDtypeStruct((8, LANES), f32)], compiler_params=_params(),
    )(rs, dcs, fl, bf_row)


def _tile_mask(n_keys, n_queries, off, window):
    shape = (n_keys, n_queries)
    d = lax.broadcasted_iota(jnp.int32, shape, 1) - lax.broadcasted_iota(jnp.int32, shape, 0) + off
    valid = d >= 0
    return jnp.logical_and(valid, d < window) if window else valid


def _wide(v, t):
    return jnp.concatenate([v] * (t // LANES), axis=1)


def _attn_fwd(q, k, v, name, *, cum_b=None, sink_rows=None, window=None, t=256):
    s = q.shape[0]
    t = _row_tile(s, t)
    fox, has_sink = cum_b is not None, sink_rows is not None
    assert not window or (window % LANES == 0 and LANES + window <= s)

    def body(*refs):
        q_ref, k_ref, v_ref = refs[:3]
        rest = list(refs[3:])
        cb_ref = rest.pop(0) if fox else None
        sink_ref = rest.pop(0) if has_sink else None
        o_ref, lse_ref = rest
        i = pl.program_id(1)
        low = _lane() < HEAD_DIM
        top = lax.broadcasted_iota(jnp.int32, (LANES, 1), 0) < HEAD_DIM
        q2 = q_ref[...]
        zero = jnp.zeros_like(q2)
        qms = (jnp.where(low, q2, zero), jnp.where(low, zero, q2))

        def tile(k0, n_keys, off, carry, masked, queries=slice(0, t)):
            nq = queries.stop - queries.start
            kblk, vblk = k_ref[pl.ds(k0, n_keys), :], v_ref[pl.ds(k0, n_keys), :]
            valid = _tile_mask(n_keys, nq, off, window) if masked else None
            ones = jnp.ones_like(vblk)
            vs = tuple(jnp.where(_lane() == L_ROW[h], ones, vblk) for h in range(2))

            def scores(h):
                return lax.dot_general(kblk, qms[h][queries], _NT, preferred_element_type=f32)

            def softmax(h, sc):
                m = carry[h][0]
                if fox:
                    sc = sc - _wide(cb_ref[pl.ds(k0, n_keys), h * LANES:(h + 1) * LANES], nq)
                if masked:
                    sc = jnp.where(valid, sc, NEG)
                m_new = jnp.maximum(m, jnp.max(sc, axis=0, keepdims=True))
                return m_new, jnp.exp(m - m_new), jnp.exp(sc - m_new).astype(bf16)

            def update(h, m_new, alpha, p):
                return m_new, alpha * carry[h][1] + lax.dot_general(vs[h], p, _TN, preferred_element_type=f32)

            if window:
                return tuple(update(h, *softmax(h, scores(h))) for h in range(2))
            scs = [scores(h) for h in range(2)]
            stats = [softmax(h, scs[h]) for h in range(2)]
            return tuple(update(h, *stats[h]) for h in range(2))

        def start(nq):
            if has_sink:
                row = lax.broadcasted_iota(jnp.int32, (LANES, nq), 0)
                return tuple((_wide(sink_ref[h:h + 1, :], nq), (row == L_ROW[h]).astype(f32)) for h in range(2))
            return tuple((jnp.full((1, nq), NEG, f32), jnp.zeros((LANES, nq), f32)) for h in range(2))

        def finish(carry, queries):
            (m0, a0), (m1, a1) = carry
            l0, l1 = a0[L_ROW[0]:L_ROW[0] + 1, :], a1[L_ROW[1]:L_ROW[1] + 1, :]
            o_t = jnp.where(top, a0 * (1.0 / l0), a1 * (1.0 / l1))
            o_ref[queries, :] = o_t.T.astype(bf16)
            lse_ref[0:1, queries] = m0 + jnp.log(l0)
            lse_ref[1:2, queries] = m1 + jnp.log(l1)

        if window:
            for c in range(t // LANES):
                queries = slice(c * LANES, (c + 1) * LANES)
                q0 = i * t + c * LANES
                k0 = pl.multiple_of(jnp.maximum(q0 - window, 0), LANES)
                finish(tile(k0, LANES + window, q0 - k0, start(LANES), True, queries), queries)
        else:
            carry = lax.fori_loop(0, i, lambda kb, c: tile(pl.multiple_of(kb * t, t), t, 0, c, False), start(t))
            finish(tile(pl.multiple_of(i * t, t), t, 0, carry, True), slice(0, t))

    q_spec = pl.BlockSpec((t, LANES), lambda j, i: (i, j))
    kv_spec = pl.BlockSpec((s, LANES), lambda j, i: (0, j))
    in_specs, args = [q_spec, kv_spec, kv_spec], [q, k, v]
    if fox:
        in_specs += [pl.BlockSpec((s, 2 * LANES), lambda j, i: (0, j))]
        args += [cum_b]
    if has_sink:
        in_specs += [pl.BlockSpec((None, 2, LANES), lambda j, i: (j, 0, 0))]
        args += [sink_rows.reshape(N_PAIRS, 2, LANES)]
    return pl.pallas_call(
        body, name=name, grid=(N_PAIRS, s // t), in_specs=in_specs,
        out_specs=[q_spec, pl.BlockSpec((None, 2, t), lambda j, i: (j, 0, i))],
        out_shape=[jax.ShapeDtypeStruct((s, N_PAIRS * LANES), bf16), jax.ShapeDtypeStruct((N_PAIRS, 2, s), f32)],
        compiler_params=_params(2),
    )(*args)


def _branch_dgrad_delta(db, w, o, name, *, lse=None, sink_rows=None, after=None):
    s, hw = o.shape
    tm = _row_tile(s, 512)
    has_sink = sink_rows is not None
    extra = [] if after is None else [after]

    def body(*refs):
        db_ref, w_ref, o_ref = refs[:3]
        outs = refs[3 + (2 if has_sink else 0) + len(extra):]
        do_ref, dl_ref = outs[:2]
        if has_sink:
            lse_ref, sink_ref = refs[3:5]
            ds_ref = outs[2]

            @pl.when(pl.program_id(0) == 0)
            def _():
                ds_ref[...] = jnp.zeros_like(ds_ref)
        do = lax.dot_general(db_ref[...], w_ref[...], _NT, preferred_element_type=f32).astype(bf16)
        do_ref[...] = do
        for j in range(N_PAIRS):
            cols = slice(j * LANES, (j + 1) * LANES)
            prod_t = (do[:, cols].astype(f32) * o_ref[:, cols].astype(f32)).T
            for h in range(2):
                dl = jnp.sum(prod_t[h * HEAD_DIM:(h + 1) * HEAD_DIM, :], axis=0, keepdims=True)
                dl_ref[j, h:h + 1, :] = dl
                if has_sink:
                    r = 2 * j + h
                    p_sink = jnp.exp(sink_ref[r:r + 1, 0:1] - lse_ref[j, h:h + 1, :])
                    ds_ref[r:r + 1, :] += -jnp.sum(p_sink * dl, axis=1, keepdims=True)

    rows_spec = pl.BlockSpec((N_PAIRS, 2, tm), lambda i: (0, 0, i))
    in_specs = [_row_spec(tm, db.shape[1]), pl.BlockSpec(w.shape, lambda i: (0, 0)), _row_spec(tm, hw)]
    args = [db, w, o]
    out_specs = [_row_spec(tm, hw), rows_spec]
    out_shape = [jax.ShapeDtypeStruct((s, hw), bf16), jax.ShapeDtypeStruct((N_PAIRS, 2, s), f32)]
    if has_sink:
        in_specs += [rows_spec, _vec_spec(LANES, N_HEADS)]
        args += [lse, sink_rows]
        out_specs += [_vec_spec(LANES, N_HEADS)]
        out_shape += [jax.ShapeDtypeStruct((N_HEADS, LANES), f32)]
    return pl.pallas_call(
        body, name=name, grid=(s // tm,), in_specs=in_specs + [pl.BlockSpec(memory_space=pl.ANY)] * len(extra),
        out_specs=out_specs, out_shape=out_shape, compiler_params=_params(1),
    )(*args, *extra)


def _attn_bwd(q, k, v, do, lse, delta, name, *, cum_b=None, window=None, t=256):
    s = q.shape[0]
    t = _row_tile(s, t)
    nblk = s // t
    fox = cum_b is not None
    assert not window or (window % LANES == 0 and LANES + window <= s)

    def body(*refs):
        k_ref, v_ref, q_ref, do_ref, lse_ref, dl_ref = refs[:6]
        rest = list(refs[6:])
        cb_ref = rest.pop(0) if fox else None
        dq_ref, dk_ref, dv_ref = rest[:3]
        dcs_ref, rs_ref = (rest[3], rest[4]) if fox else (None, None)
        dk_acc, dv_acc = rest[-2:]
        b = pl.program_id(1)
        k0 = pl.multiple_of(b * t, t)

        @pl.when(b == 0)
        def _():
            dq_ref[...] = jnp.zeros_like(dq_ref)
            if fox:
                rs_ref[...] = jnp.zeros_like(rs_ref)

        dk_acc[...] = jnp.zeros_like(dk_acc)
        dv_acc[...] = jnp.zeros_like(dv_acc)
        if fox:
            dcs_ref[...] = jnp.zeros_like(dcs_ref)
        low = _lane() < HEAD_DIM
        top = lax.broadcasted_iota(jnp.int32, (LANES, 1), 0) < HEAD_DIM
        kblk, vblk = k_ref[...], v_ref[...]
        k_t = kblk.astype(f32).T.astype(bf16)
        cks = [_wide(cb_ref[pl.ds(k0, t), h * LANES:(h + 1) * LANES], t) for h in range(2)] if fox else None

        def tile(q0, n_queries, off, masked, keys=slice(0, t)):
            cols = pl.ds(q0, n_queries)
            q2, do2 = q_ref[cols, :], do_ref[cols, :]
            zero = jnp.zeros_like(q2)
            valid = _tile_mask(keys.stop - keys.start, n_queries, off, window) if masked else None
            dq_parts = []
            for h in range(2):
                qm = jnp.where(low, q2, zero) if h == 0 else jnp.where(low, zero, q2)
                dom = jnp.where(low, do2, zero) if h == 0 else jnp.where(low, zero, do2)
                sc = lax.dot_general(kblk[keys], qm, _NT, preferred_element_type=f32)
                if fox:
                    sc = sc - cks[h]
                if masked:
                    sc = jnp.where(valid, sc, NEG)
                p = jnp.exp(sc - lse_ref[h:h + 1, cols])
                dp = lax.dot_general(vblk[keys], dom, _NT, preferred_element_type=f32)
                ds = p * (dp - dl_ref[h:h + 1, cols])
                pb, dsb = p.astype(bf16), ds.astype(bf16)
                dv_acc[keys, :] += jnp.dot(pb, dom, preferred_element_type=f32)
                dk_acc[keys, :] += jnp.dot(dsb, qm, preferred_element_type=f32)
                dq_parts.append(jnp.dot(k_t[:, keys], dsb, preferred_element_type=f32))
                if fox:
                    dcs_ref[:, h * LANES:(h + 1) * LANES] += sum(ds[:, g * LANES:(g + 1) * LANES] for g in range(t // LANES))
                    rs_ref[h:h + 1, cols] += jnp.sum(ds, axis=0, keepdims=True)
            dq_ref[:, cols] += jnp.where(top, dq_parts[0], dq_parts[1])

        def later_block(qb, carry):
            tile(pl.multiple_of(qb * t, t), t, 0, False)
            return carry

        if window:
            for c in range(t // LANES):
                first = b * t + c * LANES
                q0 = pl.multiple_of(jnp.minimum(first, s - (LANES + window)), LANES)
                tile(q0, LANES + window, q0 - first, True, slice(c * LANES, (c + 1) * LANES))
        else:
            tile(k0, t, 0, True)
            lax.fori_loop(b + 1, nblk, later_block, 0)
        dk_ref[...] = dk_acc[...].astype(bf16)
        dv_ref[...] = dv_acc[...].astype(bf16)

    kv_spec = pl.BlockSpec((t, LANES), lambda j, b: (b, j))
    seq_spec = pl.BlockSpec((s, LANES), lambda j, b: (0, j))
    rows_spec = pl.BlockSpec((None, 2, s), lambda j, b: (j, 0, 0))
    hw = N_PAIRS * LANES
    in_specs, args = [kv_spec, kv_spec, seq_spec, seq_spec, rows_spec, rows_spec], [k, v, q, do, lse, delta]
    out_specs = [pl.BlockSpec((LANES, s), lambda j, b: (j, 0)), kv_spec, kv_spec]
    out_shape = [jax.ShapeDtypeStruct((hw, s), f32), jax.ShapeDtypeStruct((s, hw), bf16), jax.ShapeDtypeStruct((s, hw), bf16)]
    if fox:
        in_specs += [pl.BlockSpec((s, 2 * LANES), lambda j, b: (0, j))]
        args += [cum_b]
        out_specs += [pl.BlockSpec((t, 2 * LANES), lambda j, b: (b, j)), rows_spec]
        out_shape += [jax.ShapeDtypeStruct((s, N_HEADS * LANES), f32), jax.ShapeDtypeStruct((N_PAIRS, 2, s), f32)]
    return pl.pallas_call(
        body, name=name, grid=(N_PAIRS, nblk), in_specs=in_specs, out_specs=out_specs, out_shape=out_shape,
        scratch_shapes=[pltpu.VMEM((t, LANES), f32)] * 2, compiler_params=_params(2),
    )(*args)


def _branch_merge(o_a, o_b, w_a, w_b, gl, name):
    s, k = o_a.shape
    d = w_a.shape[1]
    tm = _row_tile(s, 1024)

    def body(oa_ref, ob_ref, wa_ref, wb_ref, g_ref, ba_ref, bb_ref, m_ref):
        ba = jnp.dot(oa_ref[...], wa_ref[...], preferred_element_type=f32)
        bb = jnp.dot(ob_ref[...], wb_ref[...], preferred_element_type=f32)
        g0, g1 = jax.nn.sigmoid(g_ref[:, :d].astype(f32)), jax.nn.sigmoid(g_ref[:, d:].astype(f32))
        ba_ref[...] = ba.astype(bf16)
        bb_ref[...] = bb.astype(bf16)
        m_ref[...] = (g0 * ba + g1 * bb).astype(bf16)

    whole = pl.BlockSpec((k, d), lambda i: (0, 0))
    return pl.pallas_call(
        body, name=name, grid=(s // tm,),
        in_specs=[_row_spec(tm, k), _row_spec(tm, k), whole, whole, _row_spec(tm, 2 * d)],
        out_specs=[_row_spec(tm, d)] * 3, out_shape=[jax.ShapeDtypeStruct((s, d), bf16)] * 3, compiler_params=_params(1),
    )(o_a, o_b, w_a, w_b, gl)


def _out_dgrad_merge_bwd(dy, w_out, ba, bb, gl, name):
    s, d = ba.shape
    tm = _row_tile(s, 512)

    def body(dy_ref, w_ref, a_ref, b_ref, g_ref, da_ref, db_ref, dg_ref):
        dmv = lax.dot_general(dy_ref[...], w_ref[...], _NT, preferred_element_type=f32)
        g0, g1 = jax.nn.sigmoid(g_ref[:, :d].astype(f32)), jax.nn.sigmoid(g_ref[:, d:].astype(f32))
        da_ref[...] = (dmv * g0).astype(bf16)
        db_ref[...] = (dmv * g1).astype(bf16)
        dg_ref[:, :d] = (dmv * a_ref[...].astype(f32) * (g0 * (1.0 - g0))).astype(bf16)
        dg_ref[:, d:] = (dmv * b_ref[...].astype(f32) * (g1 * (1.0 - g1))).astype(bf16)

    return pl.pallas_call(
        body, name=name, grid=(s // tm,),
        in_specs=[_row_spec(tm, dy.shape[1]), pl.BlockSpec(w_out.shape, lambda i: (0, 0))] + [_row_spec(tm, d)] * 2
        + [_row_spec(tm, 2 * d)],
        out_specs=[_row_spec(tm, d)] * 2 + [_row_spec(tm, 2 * d)],
        out_shape=[jax.ShapeDtypeStruct((s, d), bf16)] * 2 + [jax.ShapeDtypeStruct((s, 2 * d), bf16)],
        compiler_params=_params(1),
    )(dy, w_out, ba, bb, gl)


GLU_TILE = 256


def _ffn_in_swiglu(h, w_t, name):
    s, d = h.shape
    f = w_t.shape[0] // 2
    tm = _row_tile(s, 2048)
    tg = GLU_TILE
    nb = f // tg

    def body(h_ref, wg_ref, wu_ref, g_ref, u_ref, act_ref):
        hv = h_ref[...]
        g = lax.dot_general(hv, wg_ref[...], _NT, preferred_element_type=f32)
        u = lax.dot_general(hv, wu_ref[...], _NT, preferred_element_type=f32)
        g_ref[...] = g.astype(bf16)
        u_ref[...] = u.astype(bf16)
        act_ref[...] = (g * jax.nn.sigmoid(g) * u).astype(bf16)

    col = pl.BlockSpec((tm, tg), lambda i, j: (i, j))
    return pl.pallas_call(
        body, name=name, grid=(s // tm, nb),
        in_specs=[pl.BlockSpec((tm, d), lambda i, j: (i, 0)), pl.BlockSpec((tg, d), lambda i, j: (j, 0)),
                  pl.BlockSpec((tg, d), lambda i, j: (j + nb, 0))],
        out_specs=[col] * 3, out_shape=[jax.ShapeDtypeStruct((s, f), bf16)] * 3, compiler_params=_params(2),
    )(h, w_t, w_t)


def _ffn_out_dgrad_swiglu(dy, w_out, g, u, name):
    s, d = dy.shape
    f = g.shape[1]
    tm = _row_tile(s, 2048)
    tg = GLU_TILE

    def body(dy_ref, w_ref, g_ref, u_ref, dg_ref, du_ref):
        dv = lax.dot_general(dy_ref[...], w_ref[...], _NT, preferred_element_type=f32)
        gv, uv = g_ref[...].astype(f32), u_ref[...].astype(f32)
        sg = jax.nn.sigmoid(gv)
        dg_ref[...] = (dv * uv * (sg * (1.0 + gv * (1.0 - sg)))).astype(bf16)
        du_ref[...] = (dv * (gv * sg)).astype(bf16)

    col = pl.BlockSpec((tm, tg), lambda i, j: (i, j))
    return pl.pallas_call(
        body, name=name, grid=(s // tm, f // tg),
        in_specs=[pl.BlockSpec((tm, d), lambda i, j: (i, 0)), pl.BlockSpec((tg, d), lambda i, j: (j, 0)), col, col],
        out_specs=[col] * 2, out_shape=[jax.ShapeDtypeStruct((s, f), bf16)] * 2, compiler_params=_params(2),
    )(dy, w_out, g, u)


def _wgrad_stack(parts, h, name):
    s, m = parts[0].shape
    d = h.shape[1]
    tm = 256
    nb = m // tm
    n = len(parts)

    def body(*refs):
        i = pl.program_id(0)
        for p in range(n):
            @pl.when(i // nb == p)
            def _(p=p):
                refs[n + 1][...] = lax.dot_general(refs[p][...], refs[n][...], _TN, preferred_element_type=f32).astype(bf16)

    a_specs = [pl.BlockSpec((s, tm), lambda i, p=p: (0, jnp.clip(i - p * nb, 0, nb - 1))) for p in range(n)]
    return pl.pallas_call(
        body, name=name, grid=(n * nb,), in_specs=a_specs + [pl.BlockSpec((s, d), lambda i: (0, 0))],
        out_specs=pl.BlockSpec((tm, d), lambda i: (i, 0)),
        out_shape=jax.ShapeDtypeStruct((n * m, d), bf16), compiler_params=_params(1),
    )(*parts, h)


def _ada_wgrad(c_all, d_all, name):
    n, d = c_all.shape
    w = d_all.shape[1]

    def body(c_ref, d_ref, o_ref):
        eye = (lax.broadcasted_iota(jnp.int32, (n, n), 0) == lax.broadcasted_iota(jnp.int32, (n, n), 1)).astype(f32)
        ct = lax.dot_general(c_ref[...], eye, _TN, precision=lax.Precision.HIGHEST, preferred_element_type=f32)
        g = ct[:, 0:1] * d_ref[0:1, :]
        for bi in range(1, n):
            g = g + ct[:, bi:bi + 1] * d_ref[bi:bi + 1, :]
        o_ref[0] = g

    return pl.pallas_call(
        body, name=name, out_shape=jax.ShapeDtypeStruct((1, d, w), f32), compiler_params=_params(),
    )(c_all, d_all)


def _adamw(parts, w, m, v, name, mine=None, me=None):
    r, c = w.shape
    n_parts = parts.shape[0]
    row_tiles = [t for t in range(min(r, 256), 0, -1) if r % t == 0 and (t % 16 == 0 or t == r)]
    if row_tiles:
        tr, tc = row_tiles[0], c
    else:
        tr, tc = r, next(t for t in (256, LANES) if c % t == 0)

    def body(*refs):
        w_ref, m_ref, v_ref, g_ref, d_ref, nm_ref, nv_ref = refs[-7:]
        if mine is None:
            p_ref, = refs[:-7]
        else:
            me_ref, p_ref, own_ref = refs[:-7]

        def part(i):
            if mine is None:
                return p_ref[i].astype(f32)
            return jnp.where(me_ref[0] == i, own_ref[...], p_ref[i]).astype(f32)

        g = part(0)
        for i in range(1, n_parts):
            g = g + part(i)
        mm = ADAM_B1 * m_ref[...] + (1.0 - ADAM_B1) * g
        vv = ADAM_B2 * v_ref[...] + (1.0 - ADAM_B2) * (g * g)
        m_hat = mm / (1.0 - ADAM_B1 ** ADAM_STEP)
        v_hat = vv / (1.0 - ADAM_B2 ** ADAM_STEP)
        g_ref[...] = g
        d_ref[...] = -ADAM_LR * (m_hat / (jnp.sqrt(v_hat) + ADAM_EPS) + ADAM_WD * w_ref[...])
        nm_ref[...] = mm
        nv_ref[...] = vv

    out_shape = [jax.ShapeDtypeStruct((r, c), f32)] * 4
    if mine is None:
        spec = pl.BlockSpec((tr, tc), lambda i, j: (i, j))
        return pl.pallas_call(
            body, name=name, grid=(r // tr, c // tc),
            in_specs=[pl.BlockSpec((n_parts, tr, tc), lambda i, j: (0, i, j))] + [spec] * 3,
            out_specs=[spec] * 4, out_shape=out_shape, compiler_params=_params(2),
        )(parts, w, m, v)
    spec = pl.BlockSpec((tr, tc), lambda i, j, me_ref: (i, j))
    return pl.pallas_call(
        body, name=name, out_shape=out_shape, compiler_params=_params(2),
        grid_spec=pltpu.PrefetchScalarGridSpec(
            num_scalar_prefetch=1, grid=(r // tr, c // tc),
            in_specs=[pl.BlockSpec((n_parts, tr, tc), lambda i, j, me_ref: (0, i, j)),
                      pl.BlockSpec((None, tr, tc), lambda i, j, me_ref: (me_ref[0], i, j))] + [spec] * 3,
            out_specs=[spec] * 4),
    )(me, parts, mine, w, m, v)


def _me():
    return lax.axis_index("x"), lax.axis_index("y"), lax.axis_index("c")


def _gather_prologue(c, w_ada, b_mine, w_in_t, name):
    n_dev, d = N_DEV, c.shape[1]
    ada_w = w_ada.shape[1]

    def body(c_ref, w_ref, b_ref, win_ref, call_ref, ada_ref, gin_ref, cols_ref, send_sems, recv_sems, local_sems):
        x, y, cc = _me()
        me, sibling = (x, y, cc), (x, y, 1 - cc)
        chips = [(1 - x, y), (x, 1 - y), (1 - x, 1 - y)]
        outs = (call_ref, ada_ref, gin_ref)

        def rows(a, dev):
            return outs[a].at[4 * dev[0] + 2 * dev[1] + dev[2]]

        def copy(a, k, block, to, src=None):
            return pltpu.make_async_remote_copy(
                src_ref=rows(a, block) if src is None else src, dst_ref=rows(a, block),
                send_sem=send_sems.at[a, k], recv_sem=recv_sems.at[a, k], device_id=to, device_id_type=MESH)

        def begin(a, src):
            own = pltpu.make_async_copy(src, rows(a, me), local_sems.at[a])
            sends = [copy(a, 0, me, sibling, src=src)] + [copy(a, 1 + j, me, (*chip, cc), src=src) for j, chip in enumerate(chips)]
            for cp in [own] + sends:
                cp.start()
            return own, sends

        def finish(a, own, sends):
            passed = []
            for j, chip in enumerate(chips):
                copy(a, 1 + j, (*chip, cc), me).wait_recv()
                passed.append(copy(a, 4 + j, (*chip, cc), sibling))
                passed[-1].start()
            copy(a, 0, sibling, me).wait_recv()
            for j, chip in enumerate(chips):
                copy(a, 4 + j, (*chip, 1 - cc), me).wait_recv()
            for cp in sends + passed:
                cp.wait_send()
            own.wait()

        finish(0, *begin(0, c_ref))
        cols_ref[...] = (jnp.dot(call_ref[:, 0, :].astype(bf16), w_ref[...].astype(bf16), preferred_element_type=f32)
                         + b_ref[...])
        finish(1, *begin(1, cols_ref))
        finish(2, *begin(2, win_ref))

    vmem, hbm = pl.BlockSpec(memory_space=pltpu.VMEM), pl.BlockSpec(memory_space=pl.ANY)
    return pl.pallas_call(
        body, name=name, in_specs=[vmem, vmem, vmem, hbm], out_specs=[vmem, vmem, hbm],
        out_shape=[jax.ShapeDtypeStruct((n_dev, 1, d), f32), jax.ShapeDtypeStruct((n_dev, n_dev, ada_w), f32),
                   jax.ShapeDtypeStruct((n_dev,) + w_in_t.shape, w_in_t.dtype)],
        scratch_shapes=[pltpu.VMEM((n_dev, ada_w), f32), pltpu.SemaphoreType.DMA((3, 7)), pltpu.SemaphoreType.DMA((3, 7)),
                        pltpu.SemaphoreType.DMA((3,))],
        compiler_params=pltpu.CompilerParams(vmem_limit_bytes=VMEM_LIMIT),
    )(c, w_ada, b_mine, w_in_t)


_FLIPS = ((0, 0, 1), (1, 0, 0), (0, 1, 0), (1, 1, 0), (1, 0, 1), (0, 1, 1), (1, 1, 1))
_HBM = pl.BlockSpec(memory_space=pltpu.HBM)
_SEM = pl.BlockSpec(memory_space=pltpu.SEMAPHORE)


def _exchange_copies(scatter, srcs, lands, send_sems, recv_sems):
    x, y, c = _me()
    me_row = 4 * x + 2 * y + c
    out = []
    for k, (fx, fy, fc) in enumerate(_FLIPS):
        peer = (x ^ fx, y ^ fy, c ^ fc)
        peer_row = 4 * peer[0] + 2 * peer[1] + peer[2]
        for a in range(len(srcs)):
            out.append(pltpu.make_async_remote_copy(
                src_ref=srcs[a].at[peer_row] if scatter else srcs[a], dst_ref=lands[a].at[me_row],
                send_sem=send_sems.at[7 * a + k], recv_sem=recv_sems.at[7 * a + k], device_id=peer, device_id_type=MESH))
    return out


def _exchange_start(arrays, scatter, name, after=None):
    n = len(arrays)
    lands = [lax.empty(a.shape if scatter else (N_DEV,) + a.shape, a.dtype) for a in arrays]
    extra = [] if after is None else [after]

    def body(*refs):
        srcs, zones = refs[:n], refs[n:2 * n]
        send_sems, recv_sems = refs[2 * n + len(extra)], refs[2 * n + len(extra) + 1]
        token = refs[-1]
        for cp in _exchange_copies(scatter, srcs, zones, send_sems, recv_sems):
            cp.start()
        token[...] = jnp.zeros_like(token)

    thru = [pltpu.HBM(a.shape, a.dtype) for a in list(arrays) + lands]
    outs = pl.pallas_call(
        body, name=name,
        out_shape=(pltpu.SemaphoreType.DMA((7 * n,)), pltpu.SemaphoreType.DMA((7 * n,)), *thru, jax.ShapeDtypeStruct((8, LANES), f32)),
        in_specs=[_HBM] * (2 * n) + [pl.BlockSpec(memory_space=pl.ANY)] * len(extra),
        out_specs=(_SEM, _SEM, *[_HBM] * (2 * n), pl.BlockSpec(memory_space=pltpu.VMEM)),
        input_output_aliases={i: 2 + i for i in range(2 * n)},
        compiler_params=pltpu.CompilerParams(has_side_effects=pltpu.SideEffectType.DATAFLOW_SIDE_EFFECTING),
    )(*[pltpu.with_memory_space_constraint(a, pltpu.HBM) for a in list(arrays) + lands], *extra)
    return dict(n=n, scatter=scatter, sems=outs[:2], srcs=outs[2:2 + n], lands=outs[2 + n:2 + 2 * n], token=outs[-1])


def _exchange_wait(handle, after, name):
    n, scatter = handle["n"], handle["scatter"]

    def body(*refs):
        srcs, zones = refs[:n], refs[n:2 * n]
        send_sems, recv_sems = refs[2 * n], refs[2 * n + 1]
        for cp in _exchange_copies(scatter, srcs, zones, send_sems, recv_sems):
            cp.wait_send()
            cp.wait_recv()

    thru = [pltpu.HBM(a.shape, a.dtype) for a in list(handle["srcs"]) + list(handle["lands"])]
    outs = pl.pallas_call(
        body, name=name, out_shape=tuple(thru),
        in_specs=[_HBM] * (2 * n) + [_SEM, _SEM, pl.BlockSpec(memory_space=pl.ANY)], out_specs=tuple([_HBM] * (2 * n)),
        input_output_aliases={i: i for i in range(2 * n)},
        compiler_params=pltpu.CompilerParams(has_side_effects=pltpu.SideEffectType.DATAFLOW_SIDE_EFFECTING),
    )(*handle["srcs"], *handle["lands"], *handle["sems"], after)
    return list(outs[:n]), list(outs[n:])


def _cols_from_shards(g):
    return jnp.transpose(g, (1, 0, 2)).reshape(g.shape[1], -1)


def _shards_from_cols(a):
    return jnp.transpose(a.reshape(a.shape[0], N_DEV, -1), (1, 0, 2))


def _local_step(x, positions, ada, g_pre_mix, g_post_mix, b_f, sinks, g_pre_ffn, g_post_ffn, target,
                w_in_t, mix_weights, ffn_weights, on_grads):
    s, d = x.shape
    row = lambda v: v.reshape(1, -1)
    shift_m, scale_m, gate_m, shift_f, scale_f, gate_f = (ada[i:i + 1] for i in range(6))
    w_gate_t, w_qkv_t = w_in_t[F_OFF + N_HEADS:], w_in_t[:QKV_W]
    w_f_t = jnp.pad(w_in_t[F_OFF:F_OFF + N_HEADS], ((0, LANES - N_HEADS), (0, 0)))
    bf_row = jnp.pad(row(b_f), ((0, 0), (0, LANES - N_HEADS)))
    sink_rows = jnp.broadcast_to(sinks.reshape(N_HEADS, 1).astype(f32), (N_HEADS, LANES))
    inv_freq = 1.0 / (ROPE_THETA ** (jnp.arange(0, HEAD_DIM, 2, dtype=f32) / HEAD_DIM))
    cos, sin_s = _rope_tables(positions.reshape(s, 1), jnp.tile(inv_freq, 4).reshape(1, LANES), "rope_tables")

    h1, qa, ka, va, qb, kb, vb = _prenorm_proj_qkv(x, row(g_pre_mix), scale_m, shift_m, w_qkv_t, cos, sin_s, "prenorm_proj_qkv")
    gl = _matmul(h1, w_gate_t, "nt", bf16, "proj_gate")
    fl, cum_b = _forget_prep(h1, w_f_t, bf_row, "proj_forget_prep")
    o_a, lse_a = _attn_fwd(qa, ka, va, "swa_fwd", sink_rows=sink_rows, window=WINDOW, t=2048)
    o_b, lse_b = _attn_fwd(qb, kb, vb, "fox_fwd", cum_b=cum_b, t=1024)
    everything_before = (gl[:8, :LANES] + o_a[:8, :LANES] + o_b[:8, :LANES]).astype(f32)
    w_branch_a, w_branch_b, w_out = mix_weights(everything_before)
    ba, bb, merged = _branch_merge(o_a, o_b, w_branch_a, w_branch_b, gl, "branch_merge")
    y1, x2, h2 = _out_proj_postnorm_prenorm(merged, w_out, x, row(g_post_mix), gate_m, row(g_pre_ffn), scale_f, shift_f,
                                            "out_proj_norms")

    w_ffn_in_t, w_ffn_out = ffn_weights(h2)
    g_ff, u_ff, act = _ffn_in_swiglu(h2, w_ffn_in_t, "ffn_in_swiglu")
    loss_row, d_out, d_y2, vec_pf = _out_proj_loss_tail(act, w_ffn_out, x2, row(g_post_ffn), gate_f, target, "ffn_out_loss_tail")

    g_w_ffn_out = _matmul(act, d_y2, "tn", bf16, "ffn_out_wgrad")
    dg_ff, du_ff = _ffn_out_dgrad_swiglu(d_y2, w_ffn_out, g_ff, u_ff, "ffn_out_dgrad_swiglu")
    g_w_ffn_in_t = _wgrad_stack([dg_ff, du_ff], h2, "ffn_in_wgrad")
    sent = on_grads(dict(w_ffn_in=g_w_ffn_in_t, w_ffn_out=g_w_ffn_out))
    d_x2, vec_nf, d_y1, vec_pm = _dgrad_prenorm_bwd(
        [(dg_ff, w_ffn_in_t, 0), (du_ff, w_ffn_in_t, 1)], x2, row(g_pre_ffn), scale_f, d_out, "ffn_in_dgrad_norms_bwd",
        after=sent, below=(y1, row(g_post_mix), gate_m))

    g_w_out = _matmul(merged, d_y1, "tn", bf16, "out_proj_wgrad")
    d_ba, d_bb, dgl = _out_dgrad_merge_bwd(d_y1, w_out, ba, bb, gl, "out_proj_dgrad_merge_bwd")
    g_w_branch_a = _matmul(o_a, d_ba, "tn", bf16, "branch_a_wgrad")
    g_w_branch_b = _matmul(o_b, d_bb, "tn", bf16, "branch_b_wgrad")
    sent = on_grads(dict(w_out=g_w_out, w_branch_a=g_w_branch_a, w_branch_b=g_w_branch_b))
    d_oa, delta_a, d_sink = _branch_dgrad_delta(d_ba, w_branch_a, o_a, "branch_a_dgrad_delta", lse=lse_a,
                                                sink_rows=sink_rows, after=sent)
    d_ob, delta_b = _branch_dgrad_delta(d_bb, w_branch_b, o_b, "branch_b_dgrad_delta", after=sent)
    dqa_t, dka, dva = _attn_bwd(qa, ka, va, d_oa, lse_a, delta_a, "swa_bwd", window=WINDOW, t=2048)
    dqb_t, dkb, dvb, dcs, rs = _attn_bwd(qb, kb, vb, d_ob, lse_b, delta_b, "fox_bwd", cum_b=cum_b, t=512)
    dqkv = _qkv_prep_bwd(dqa_t, dka, dva, dqb_t, dkb, dvb, cos, sin_s, "qkv_prep_bwd")
    dfl, vec_bf = _forget_prep_bwd(rs.reshape(N_HEADS, s), dcs, fl, bf_row, "forget_prep_bwd")
    g_w_in_t = jnp.concatenate([_matmul(dqkv, h1, "tn", bf16, "qkv_wgrad"), _matmul(dfl, h1, "tn", bf16, "forget_wgrad")[:N_HEADS],
                                _matmul(dgl, h1, "tn", bf16, "gate_wgrad")], axis=0)
    sent = on_grads(dict(w_in=g_w_in_t))
    grad_x, vec_nm = _dgrad_prenorm_bwd([(dgl, w_gate_t, 0), (dqkv, w_qkv_t, 0), (dfl, w_f_t, 0)], x, row(g_pre_mix),
                                        scale_m, d_x2, "in_proj_dgrad_prenorm_bwd", after=sent)

    d_ada = jnp.concatenate([vec_nm[0], vec_nm[1], vec_pm[0], vec_nf[0], vec_nf[1], vec_pf[0]])
    small = dict(b_ada=d_ada, g_pre_mix=vec_nm[2], g_post_mix=vec_pm[1], g_pre_ffn=vec_nf[2], g_post_ffn=vec_pf[1],
                 b_f=vec_bf[0, :N_HEADS], sinks=d_sink[:, 0], loss=loss_row[0, :1])
    return grad_x, small


_SMALL = (("b_ada", 6144), ("g_pre_mix", 1024), ("g_post_mix", 1024), ("g_pre_ffn", 1024), ("g_post_ffn", 1024),
          ("b_f", 128), ("sinks", 128), ("loss", 128))
_SMALL_ROWS = 88


def _pack_small(vals):
    parts = [jnp.pad(vals[k].reshape(-1).astype(f32), (0, n - vals[k].size)) for k, n in _SMALL]
    flat = jnp.concatenate(parts)
    return jnp.pad(flat, (0, _SMALL_ROWS * LANES - flat.size)).reshape(_SMALL_ROWS, LANES)


def _unpack_small(slab, shapes):
    flat, out, off = slab.reshape(-1), {}, 0
    for k, n in _SMALL:
        size = math.prod(shapes[k])
        out[k] = flat[off:off + size].reshape(shapes[k])
        off += n
    return out


def kernel(x, c, positions, w_ada, b_ada, g_pre_mix, g_post_mix, w_in, b_f, sinks, w_branch_a, w_branch_b, w_out, g_pre_ffn, g_post_ffn, w_ffn_in, w_ffn_out, loss_target, m_w_ada, m_b_ada, m_g_pre_mix, m_g_post_mix, m_w_in, m_b_f, m_sinks, m_w_branch_a, m_w_branch_b, m_w_out, m_g_pre_ffn, m_g_post_ffn, m_w_ffn_in, m_w_ffn_out, v_w_ada, v_b_ada, v_g_pre_mix, v_g_post_mix, v_w_in, v_b_f, v_sinks, v_w_branch_a, v_w_branch_b, v_w_out, v_g_pre_ffn, v_g_post_ffn, v_w_ffn_in, v_w_ffn_out):
    xi, yi, ci = _me()
    me = 4 * xi + 2 * yi + ci
    d = D_MODEL
    ada_w = w_ada.shape[2]

    transposed = ("w_in", "w_ffn_in")
    tr = lambda a: jnp.transpose(a[0])

    b_mine = lax.dynamic_slice(b_ada, (0, me * ada_w), (1, ada_w))
    c_all, ada_all, g_in = _gather_prologue(c, w_ada[0], b_mine, tr(w_in).astype(bf16), "gather_prologue")
    c_all = c_all.reshape(N_DEV, d)
    ada = lax.dynamic_index_in_dim(ada_all, me, axis=1, keepdims=False).reshape(6, d)
    late_mix = [w.astype(bf16) for w in (w_branch_a[0], w_branch_b[0], w_out[0])]
    late_ffn = [w.astype(bf16) for w in (tr(w_ffn_in), w_ffn_out[0])]
    mix_h = _exchange_start(late_mix, False, "gather_mix_start", after=g_in)
    ffn_h = _exchange_start(late_ffn, False, "gather_ffn_start", after=mix_h["token"])

    def mine_into(zone, block):
        return lax.dynamic_update_index_in_dim(zone, block, me, 0)

    def rows_from_shards(g):
        return g.reshape(g.shape[0] * g.shape[1], g.shape[2])

    def mix_weights(after):
        sent, zones = _exchange_wait(mix_h, after, "gather_mix_wait")
        g_ba, g_bb, g_out = (mine_into(z, w) for z, w in zip(zones, sent))
        return _cols_from_shards(g_ba), _cols_from_shards(g_bb), rows_from_shards(g_out)

    def ffn_weights(after):
        sent, zones = _exchange_wait(ffn_h, after, "gather_ffn_wait")
        g_fi, g_fo = (mine_into(z, w) for z, w in zip(zones, sent))
        return rows_from_shards(g_fi), rows_from_shards(g_fo)

    row_sharded = ("w_out", "w_ffn_out") + transposed
    in_flight = []

    def on_grads(group):
        sends = [g.reshape(N_DEV, g.shape[0] // N_DEV, g.shape[1]) if nm in row_sharded else _shards_from_cols(g)
                 for nm, g in group.items()]
        handle = _exchange_start(sends, True, "scatter_start_%d" % len(in_flight))
        in_flight.append((list(group), handle))
        return handle["token"]

    grad_x, small = _local_step(
        x[0], positions[0], ada + ffn_h["token"][0, 0], g_pre_mix[0], g_post_mix[0], b_f[0], sinks[0], g_pre_ffn[0],
        g_post_ffn[0], loss_target[0], rows_from_shards(g_in), mix_weights, ffn_weights, on_grads)

    ws = dict(w_in=(w_in, m_w_in, v_w_in), w_branch_a=(w_branch_a, m_w_branch_a, v_w_branch_a),
              w_branch_b=(w_branch_b, m_w_branch_b, v_w_branch_b), w_out=(w_out, m_w_out, v_w_out),
              w_ffn_in=(w_ffn_in, m_w_ffn_in, v_w_ffn_in), w_ffn_out=(w_ffn_out, m_w_ffn_out, v_w_ffn_out))
    res = {}

    def finish_group(gi, after):
        names, handle = in_flight[gi]
        sends, zones = _exchange_wait(handle, after, "scatter_wait_%d" % gi)
        for nm, zone, sent in zip(names, zones, sends):
            w, m, v = (tr(a) if nm in transposed else a[0] for a in ws[nm])
            out = _adamw(zone, w, m, v, "adamw_" + nm, mine=sent, me=me.reshape(1).astype(jnp.int32))
            after = out[0]
            res[nm] = [jnp.transpose(o) for o in out] if nm in transposed else out
        return after

    small_h = _exchange_start([_pack_small(small)], False, "gather_small_start", after=grad_x)
    done = finish_group(1, finish_group(0, small_h["token"]))
    (slab_mine,), (slab_zone,) = _exchange_wait(small_h, done, "gather_small_wait")
    slab_all = mine_into(slab_zone, slab_mine)
    small_w = dict(b_ada=b_ada, g_pre_mix=g_pre_mix, g_post_mix=g_post_mix, g_pre_ffn=g_pre_ffn, g_post_ffn=g_post_ffn,
                   b_f=b_f, sinks=sinks, loss=jnp.zeros((1,), f32))
    small_m = dict(b_ada=m_b_ada, g_pre_mix=m_g_pre_mix, g_post_mix=m_g_post_mix, g_pre_ffn=m_g_pre_ffn,
                   g_post_ffn=m_g_post_ffn, b_f=m_b_f, sinks=m_sinks, loss=jnp.zeros((1,), f32))
    small_v = dict(b_ada=v_b_ada, g_pre_mix=v_g_pre_mix, g_post_mix=v_g_post_mix, g_pre_ffn=v_g_pre_ffn,
                   g_post_ffn=v_g_post_ffn, b_f=v_b_f, sinks=v_sinks, loss=jnp.ones((1,), f32))
    shapes = {k: small_w[k].shape for k, _ in _SMALL}
    s_out = _adamw(slab_all, _pack_small(small_w), _pack_small(small_m), _pack_small(small_v), "adamw_small")
    s_grad, s_delta, s_m, s_v = (_unpack_small(o, shapes) for o in s_out)

    d_ada_all = lax.dynamic_slice(slab_all[:, :6144 // LANES, :].reshape(N_DEV, 6144), (0, me * ada_w), (N_DEV, ada_w))
    ada_parts = _ada_wgrad(c_all, d_ada_all, "ada_wgrad")

    res["w_ada"] = _adamw(ada_parts, w_ada[0], m_w_ada[0], v_w_ada[0], "adamw_w_ada")
    finish_group(2, res["w_ada"][0])

    order = ["w_ada", "b_ada", "g_pre_mix", "g_post_mix", "w_in", "b_f", "sinks", "w_branch_a", "w_branch_b", "w_out",
             "g_pre_ffn", "g_post_ffn", "w_ffn_in", "w_ffn_out"]
    outs = [s_grad["loss"].reshape(()), grad_x[None]]
    for which, small_o in enumerate((s_grad, s_delta, s_m, s_v)):
        for nm in order:
            outs.append(res[nm][which][None] if nm in res else small_o[nm])
    return tuple(outs)
```

```python
import math

import jax
import jax.numpy as jnp
from jax import lax
from jax.experimental import pallas as pl
from jax.experimental.pallas import tpu as pltpu

f32 = jnp.float32
bf16 = jnp.bfloat16

D_MODEL = 1024
HEAD_DIM = 64
N_HEADS = 8
N_PAIRS = 4
QKV_W = 2304
F_OFF = 2304
WINDOW = 128
ROPE_THETA = 10000.0
RMS_EPS = 1e-6
N_DEV = 8
ADAM_LR, ADAM_B1, ADAM_B2, ADAM_EPS, ADAM_WD, ADAM_STEP = 0.001, 0.9, 0.999, 1e-08, 0.01, 10
NEG = -1e30
L_ROW = (HEAD_DIM, 0)
LANES = 128
VMEM_LIMIT = 48 * 1024 * 1024
MESH = pl.DeviceIdType.MESH

_NT = (((1,), (1,)), ((), ()))
_TN = (((0,), (0,)), ((), ()))


def _params(n_grid=0):
    sem = ("arbitrary",) * n_grid if n_grid else None
    return pltpu.CompilerParams(dimension_semantics=sem, vmem_limit_bytes=VMEM_LIMIT)


def _row_tile(s, want):
    t = min(s, want)
    assert s % t == 0, (s, t)
    return t


MATMUL_VMEM_BUDGET = 40 * 1024 * 1024


def _matmul_tiles(m, n, k, a_item, b_item, o_item):
    def tiles(d):
        return [t for t in range(LANES, min(d, 2048) + 1, LANES) if d % t == 0] or [d]

    best = None
    for tm in tiles(m):
        for tn in tiles(n):
            vmem = 2 * (tm * k * a_item + tn * k * b_item + tm * tn * o_item) + tm * tn * 4
            if vmem > MATMUL_VMEM_BUDGET:
                continue
            traffic = m * k * a_item + n * k * b_item * (1 if tn == n else m // tm) + m * n * o_item
            steps = (m // tm) * (n // tn)
            key = (traffic, 0, steps) if steps >= 4 else (traffic, 1, -steps)
            if best is None or key < best[0]:
                best = (key, tm, tn)
    assert best is not None, (m, n, k)
    return best[1], best[2]


def _matmul(a, b, mode, out_dtype, name, after=None):
    if mode == "nn":
        (m, k), n = a.shape, b.shape[1]
    elif mode == "nt":
        (m, k), n = a.shape, b.shape[0]
    else:
        (k, m), n = a.shape, b.shape[1]
    tm, tn = _matmul_tiles(m, n, k, a.dtype.itemsize, b.dtype.itemsize, jnp.dtype(out_dtype).itemsize)
    if mode == "nn":
        a_spec, b_spec, dims = pl.BlockSpec((tm, k), lambda i, j: (i, 0)), pl.BlockSpec((k, tn), lambda i, j: (0, j)), None
    elif mode == "nt":
        a_spec, b_spec, dims = pl.BlockSpec((tm, k), lambda i, j: (i, 0)), pl.BlockSpec((tn, k), lambda i, j: (j, 0)), _NT
    else:
        a_spec, b_spec, dims = pl.BlockSpec((k, tm), lambda i, j: (0, i)), pl.BlockSpec((k, tn), lambda i, j: (0, j)), _TN

    def body(a_ref, b_ref, *rest):
        o_ref = rest[-1]
        av, bv = a_ref[...].astype(bf16), b_ref[...].astype(bf16)
        if dims is None:
            r = jnp.dot(av, bv, preferred_element_type=f32)
        else:
            r = lax.dot_general(av, bv, dims, preferred_element_type=f32)
        o_ref[...] = r.astype(out_dtype)

    extra = [] if after is None else [after]
    return pl.pallas_call(
        body, name=name, grid=(m // tm, n // tn), in_specs=[a_spec, b_spec] + [pl.BlockSpec(memory_space=pl.ANY)] * len(extra),
        out_specs=pl.BlockSpec((tm, tn), lambda i, j: (i, j)),
        out_shape=jax.ShapeDtypeStruct((m, n), out_dtype), compiler_params=_params(2),
    )(a, b, *extra)


def _rstd(v):
    return lax.rsqrt(jnp.mean(v * v, axis=-1, keepdims=True) + RMS_EPS)


def _row_spec(tm, d):
    return pl.BlockSpec((tm, d), lambda i: (i, 0))


def _vec_spec(d, rows=1):
    return pl.BlockSpec((rows, d), lambda i: (0, 0))


def _proj_spec(a, w, tm):
    return [_row_spec(tm, a.shape[1]), pl.BlockSpec(w.shape, lambda i: (0, 0))]


def _out_proj_postnorm_prenorm(a, w, x, g_post, gate, g_pre, scale, shift, name):
    s, d = x.shape
    tm = _row_tile(s, 512)

    def body(a_ref, w_ref, x_ref, gp_ref, gate_ref, g_ref, sc_ref, sh_ref, y_ref, x2_ref, h_ref):
        yv = jnp.dot(a_ref[...], w_ref[...], preferred_element_type=f32)
        y_ref[...] = yv
        x2 = x_ref[...] + gate_ref[...] * (yv * _rstd(yv) * gp_ref[...])
        x2_ref[...] = x2
        h_ref[...] = ((x2 * _rstd(x2) * g_ref[...]) * (1.0 + sc_ref[...]) + sh_ref[...]).astype(bf16)

    return pl.pallas_call(
        body, name=name, grid=(s // tm,), in_specs=_proj_spec(a, w, tm) + [_row_spec(tm, d)] + [_vec_spec(d)] * 5,
        out_specs=[_row_spec(tm, d)] * 3,
        out_shape=[jax.ShapeDtypeStruct((s, d), f32)] * 2 + [jax.ShapeDtypeStruct((s, d), bf16)], compiler_params=_params(1),
    )(a, w, x, g_post, gate, g_pre, scale, shift)


def _rms_bwd(u, v, r):
    return r * u - v * (r * r * r) * jnp.mean(u * v, axis=-1, keepdims=True)


def _out_proj_loss_tail(a, w, x, g, gate, target, name):
    s, d = x.shape
    tm = _row_tile(s, 512)

    def body(a_ref, w_ref, x_ref, g_ref, gate_ref, t_ref, loss_ref, do_ref, dy_ref, vec_ref):
        @pl.when(pl.program_id(0) == 0)
        def _():
            loss_ref[...] = jnp.zeros_like(loss_ref)
            vec_ref[...] = jnp.zeros_like(vec_ref)
        yv = jnp.dot(a_ref[...], w_ref[...], preferred_element_type=f32)
        r = _rstd(yv)
        yn = yv * r
        err = x_ref[...] + gate_ref[...] * (yn * g_ref[...]) - t_ref[...]
        loss_ref[...] += 0.5 * jnp.sum(jnp.mean(err * err, axis=-1, keepdims=True), axis=0, keepdims=True)
        dr = err / d
        do_ref[...] = dr
        dn = dr * gate_ref[...]
        vec_ref[0:1, :] += jnp.sum(dr * (yn * g_ref[...]), axis=0, keepdims=True)
        vec_ref[1:2, :] += jnp.sum(dn * yn, axis=0, keepdims=True)
        dy_ref[...] = _rms_bwd(dn * g_ref[...], yv, r).astype(bf16)

    return pl.pallas_call(
        body, name=name, grid=(s // tm,),
        in_specs=_proj_spec(a, w, tm) + [_row_spec(tm, d)] + [_vec_spec(d)] * 2 + [_row_spec(tm, d)],
        out_specs=[_vec_spec(LANES), _row_spec(tm, d), _row_spec(tm, d), _vec_spec(d, 8)],
        out_shape=[jax.ShapeDtypeStruct((1, LANES), f32), jax.ShapeDtypeStruct((s, d), f32),
                   jax.ShapeDtypeStruct((s, d), bf16), jax.ShapeDtypeStruct((8, d), f32)],
        compiler_params=_params(1),
    )(a, w, x, g, gate, target)


def _dgrad_prenorm_bwd(terms, x, g, scale, dres, name, after=None, below=None):
    s, d = x.shape
    n = len(terms)
    k = sum(a.shape[1] for a, _, _ in terms)
    row_bytes = 2 * (2 * k) + d * (4 + 2 * 4 * 3 + (2 * 4 + 2 * 2 if below else 0))
    tm = next(t for t in (512, 256, 128) if s % t == 0 and 4 * k * d + t * row_bytes <= MATMUL_VMEM_BUDGET)
    extra = [] if after is None else [after]

    def body(*refs):
        a_refs, b_refs = refs[:n], refs[n:2 * n]
        x_ref, g_ref, sc_ref, dr_ref = refs[2 * n:2 * n + 4]
        n_in = 2 * n + 4 + (3 if below else 0) + len(extra)
        dx_ref, vec_ref = refs[n_in], refs[n_in + 1]
        if below:
            y_ref, gp_ref, gate_ref = refs[2 * n + 4:2 * n + 7]
            dy_ref, vec2_ref = refs[n_in + 2], refs[n_in + 3]

        @pl.when(pl.program_id(0) == 0)
        def _():
            vec_ref[...] = jnp.zeros_like(vec_ref)
            if below:
                vec2_ref[...] = jnp.zeros_like(vec2_ref)
        dhv = jnp.dot(a_refs[0][...], b_refs[0][...], preferred_element_type=f32)
        for i in range(1, n):
            dhv = dhv + jnp.dot(a_refs[i][...], b_refs[i][...], preferred_element_type=f32)
        xv = x_ref[...]
        r = _rstd(xv)
        xn = xv * r
        dn = dhv * (1.0 + sc_ref[...])
        vec_ref[0:1, :] += jnp.sum(dhv, axis=0, keepdims=True)
        vec_ref[1:2, :] += jnp.sum(dhv * (xn * g_ref[...]), axis=0, keepdims=True)
        vec_ref[2:3, :] += jnp.sum(dn * xn, axis=0, keepdims=True)
        dx = dr_ref[...] + _rms_bwd(dn * g_ref[...], xv, r)
        dx_ref[...] = dx
        if below:
            yv = y_ref[...]
            ry = _rstd(yv)
            yn = yv * ry
            dny = dx * gate_ref[...]
            vec2_ref[0:1, :] += jnp.sum(dx * (yn * gp_ref[...]), axis=0, keepdims=True)
            vec2_ref[1:2, :] += jnp.sum(dny * yn, axis=0, keepdims=True)
            dy_ref[...] = _rms_bwd(dny * gp_ref[...], yv, ry).astype(bf16)

    in_specs = ([_row_spec(tm, a.shape[1]) for a, _, _ in terms]
                + [pl.BlockSpec((a.shape[1], d), lambda i, r=r: (r, 0)) for a, _, r in terms]
                + [_row_spec(tm, d)] + [_vec_spec(d)] * 2 + [_row_spec(tm, d)])
    out_specs = [_row_spec(tm, d), _vec_spec(d, 8)]
    out_shape = [jax.ShapeDtypeStruct((s, d), f32), jax.ShapeDtypeStruct((8, d), f32)]
    args = [a for a, _, _ in terms] + [b for _, b, _ in terms] + [x, g, scale, dres]
    if below:
        in_specs += [_row_spec(tm, d)] + [_vec_spec(d)] * 2
        out_specs += [_row_spec(tm, d), _vec_spec(d, 8)]
        out_shape += [jax.ShapeDtypeStruct((s, d), bf16), jax.ShapeDtypeStruct((8, d), f32)]
        args += list(below)
    return pl.pallas_call(
        body, name=name, grid=(s // tm,), in_specs=in_specs + [pl.BlockSpec(memory_space=pl.ANY)] * len(extra),
        out_specs=out_specs, out_shape=out_shape, compiler_params=_params(1),
    )(*args, *extra)


def _lane():
    return lax.broadcasted_iota(jnp.int32, (1, LANES), 1)


def _rope_tables(pos_col, inv_freq, name):
    s = pos_col.shape[0]

    def body(p_ref, f_ref, cos_ref, sin_ref):
        ang = p_ref[...].astype(f32) * f_ref[...]
        first_half = (_lane() % HEAD_DIM) < HEAD_DIM // 2
        cos_ref[...] = jnp.cos(ang)
        sn = jnp.sin(ang)
        sin_ref[...] = jnp.where(first_half, -sn, sn)

    return pl.pallas_call(
        body, name=name, out_shape=[jax.ShapeDtypeStruct((s, LANES), f32)] * 2, compiler_params=_params(),
    )(pos_col, inv_freq)


def _swap_halves(v):
    first_half = (_lane() % HEAD_DIM) < HEAD_DIM // 2
    return jnp.where(first_half, pltpu.roll(v, LANES - HEAD_DIM // 2, axis=1), pltpu.roll(v, HEAD_DIM // 2, axis=1))


def _prenorm_proj_qkv(x, g, mod_scale, mod_shift, w_qkv_t, cos, sin_s, name):
    s, d = x.shape
    tm = _row_tile(s, 512)
    scale = 1.0 / math.sqrt(HEAD_DIM)

    def body(x_ref, g_ref, msc_ref, msh_ref, w_ref, c_ref, s_ref, h_ref, qa_ref, ka_ref, va_ref, qb_ref, kb_ref, vb_ref):
        xv = x_ref[...]
        h = ((xv * _rstd(xv) * g_ref[...]) * (1.0 + msc_ref[...]) + msh_ref[...]).astype(bf16)
        h_ref[...] = h
        proj = lax.dot_general(h, w_ref[...], _NT, preferred_element_type=f32)
        cs, sn = c_ref[...], s_ref[...]
        low = _lane() < HEAD_DIM

        def blk(j):
            return proj[:, j * LANES:(j + 1) * LANES]

        def rope(v):
            return v * cs + _swap_halves(v) * sn

        def expand(v):
            other = pltpu.roll(v, HEAD_DIM, axis=1)
            return jnp.where(low, v, other), jnp.where(low, other, v)

        for j in range(N_PAIRS):
            qa_ref[:, j * LANES:(j + 1) * LANES] = (rope(blk(j)) * scale).astype(bf16)
            qb_ref[:, j * LANES:(j + 1) * LANES] = (blk(6 + j) * scale).astype(bf16)
            kb_ref[:, j * LANES:(j + 1) * LANES] = blk(10 + j).astype(bf16)
            vb_ref[:, j * LANES:(j + 1) * LANES] = blk(14 + j).astype(bf16)
        k0, k1 = expand(rope(blk(4)))
        v0, v1 = expand(blk(5))
        for j in range(N_PAIRS):
            ka_ref[:, j * LANES:(j + 1) * LANES] = (k0 if j < 2 else k1).astype(bf16)
            va_ref[:, j * LANES:(j + 1) * LANES] = (v0 if j < 2 else v1).astype(bf16)

    hw = N_PAIRS * LANES
    return pl.pallas_call(
        body, name=name, grid=(s // tm,),
        in_specs=[_row_spec(tm, d)] + [_vec_spec(d)] * 3
        + [pl.BlockSpec((QKV_W, d), lambda i: (0, 0)), _row_spec(tm, LANES), _row_spec(tm, LANES)],
        out_specs=[_row_spec(tm, d)] + [_row_spec(tm, hw)] * 6,
        out_shape=[jax.ShapeDtypeStruct((s, d), bf16)] + [jax.ShapeDtypeStruct((s, hw), bf16)] * 6, compiler_params=_params(1),
    )(x, g, mod_scale, mod_shift, w_qkv_t, cos, sin_s)


def _qkv_prep_bwd(dqa_t, dka, dva, dqb_t, dkb, dvb, cos, sin_s, name):
    s = dka.shape[0]
    tm = _row_tile(s, 256)
    scale = 1.0 / math.sqrt(HEAD_DIM)
    hw = N_PAIRS * LANES
    t_spec = pl.BlockSpec((hw, tm), lambda i: (0, i))

    def body(dqa_ref, dka_ref, dva_ref, dqb_ref, dkb_ref, dvb_ref, c_ref, s_ref, o_ref):
        cs, sn = c_ref[...], s_ref[...]
        low = _lane() < HEAD_DIM

        def blk(ref, j):
            return ref[:, j * LANES:(j + 1) * LANES].astype(f32)

        def blk_t(ref, j):
            return ref[j * LANES:(j + 1) * LANES, :].T

        def unrope(v):
            return v * cs + _swap_halves(v * sn)

        def fold(ref):
            a, b = blk(ref, 0) + blk(ref, 1), blk(ref, 2) + blk(ref, 3)
            kv0 = a + pltpu.roll(a, HEAD_DIM, axis=1)
            kv1 = b + pltpu.roll(b, HEAD_DIM, axis=1)
            return jnp.where(low, kv0, kv1)

        for j in range(N_PAIRS):
            o_ref[:, j * LANES:(j + 1) * LANES] = (unrope(blk_t(dqa_ref, j)) * scale).astype(bf16)
            o_ref[:, (6 + j) * LANES:(7 + j) * LANES] = (blk_t(dqb_ref, j) * scale).astype(bf16)
            o_ref[:, (10 + j) * LANES:(11 + j) * LANES] = blk(dkb_ref, j).astype(bf16)
            o_ref[:, (14 + j) * LANES:(15 + j) * LANES] = blk(dvb_ref, j).astype(bf16)
        o_ref[:, 4 * LANES:5 * LANES] = unrope(fold(dka_ref)).astype(bf16)
        o_ref[:, 5 * LANES:6 * LANES] = fold(dva_ref).astype(bf16)

    return pl.pallas_call(
        body, name=name, grid=(s // tm,),
        in_specs=[t_spec, _row_spec(tm, hw), _row_spec(tm, hw), t_spec, _row_spec(tm, hw), _row_spec(tm, hw)] + [_row_spec(tm, LANES)] * 2,
        out_specs=_row_spec(tm, QKV_W), out_shape=jax.ShapeDtypeStruct((s, QKV_W), bf16), compiler_params=_params(1),
    )(dqa_t, dka, dva, dqb_t, dkb, dvb, cos, sin_s)


def _cumsum_rows(v, reverse=False):
    n = v.shape[0]
    row = lax.broadcasted_iota(jnp.int32, v.shape, 0)
    sh = 1
    while sh < n:
        if reverse:
            v = v + jnp.where(row < n - sh, pltpu.roll(v, n - sh, axis=0), 0.0)
        else:
            v = v + jnp.where(row >= sh, pltpu.roll(v, sh, axis=0), 0.0)
        sh *= 2
    return v


def _log_sigmoid(z):
    return jnp.minimum(z, 0.0) - jnp.log1p(jnp.exp(-jnp.abs(z)))


def _forget_prep(h, w_f_t, bf_row, name):
    s = h.shape[0]

    def body(h_ref, w_ref, b_ref, f_ref, cb_ref):
        fl = lax.dot_general(h_ref[...], w_ref[...], _NT, preferred_element_type=f32)
        f_ref[...] = fl
        cum = _cumsum_rows(_log_sigmoid(fl + b_ref[...]))
        for hd in range(N_HEADS):
            cb_ref[:, hd * LANES:(hd + 1) * LANES] = jnp.broadcast_to(cum[:, hd:hd + 1], (s, LANES))

    return pl.pallas_call(
        body, name=name,
        out_shape=[jax.ShapeDtypeStruct((s, LANES), f32), jax.ShapeDtypeStruct((s, N_HEADS * LANES), f32)],
        compiler_params=_params(),
    )(h, w_f_t, bf_row)


def _forget_prep_bwd(rs, dcs, fl, bf_row, name):
    s = fl.shape[0]

    def body(r_ref, c_ref, f_ref, b_ref, df_ref, db_ref):
        eye = (lax.broadcasted_iota(jnp.int32, (N_HEADS, LANES), 0) == lax.broadcasted_iota(jnp.int32, (N_HEADS, LANES), 1)).astype(f32)
        dcum = lax.dot_general(r_ref[...], eye, _TN, precision=lax.Precision.HIGHEST, preferred_element_type=f32)
        for h in range(N_HEADS):
            dcum = dcum - jnp.where(_lane() == h, jnp.sum(c_ref[:, h * LANES:(h + 1) * LANES], axis=1, keepdims=True), 0.0)
        dlf = _cumsum_rows(dcum, reverse=True)
        z = f_ref[...] + b_ref[...]
        df = jnp.where(_lane() < N_HEADS, dlf * jax.nn.sigmoid(-z), 0.0)
        df_ref[...] = df.astype(bf16)
        db_ref[...] = jnp.zeros_like(db_ref)
        db_ref[0:1, :] = jnp.sum(df, axis=0, keepdims=True)

    return pl.pallas_call(
        body, name=name,
        out_shape=[jax.ShapeDtypeStruct((s, LANES), bf16), jax.ShapeDtypeStruct((8, LANES), f32)], compiler_params=_params(),
    )(rs, dcs, fl, bf_row)


def _tile_mask(n_keys, n_queries, off, window):
    shape = (n_keys, n_queries)
    d = lax.broadcasted_iota(jnp.int32, shape, 1) - lax.broadcasted_iota(jnp.int32, shape, 0) + off
    valid = d >= 0
    return jnp.logical_and(valid, d < window) if window else valid


def _wide(v, t):
    return jnp.concatenate([v] * (t // LANES), axis=1)


def _attn_fwd(q, k, v, name, *, cum_b=None, sink_rows=None, window=None, t=256):
    s = q.shape[0]
    t = _row_tile(s, t)
    fox, has_sink = cum_b is not None, sink_rows is not None
    assert not window or (window % LANES == 0 and LANES + window <= s)

    def body(*refs):
        q_ref, k_ref, v_ref = refs[:3]
        rest = list(refs[3:])
        cb_ref = rest.pop(0) if fox else None
        sink_ref = rest.pop(0) if has_sink else None
        o_ref, lse_ref = rest
        i = pl.program_id(1)
        low = _lane() < HEAD_DIM
        top = lax.broadcasted_iota(jnp.int32, (LANES, 1), 0) < HEAD_DIM
        q2 = q_ref[...]
        zero = jnp.zeros_like(q2)
        qms = (jnp.where(low, q2, zero), jnp.where(low, zero, q2))

        def tile(k0, n_keys, off, carry, masked, queries=slice(0, t)):
            nq = queries.stop - queries.start
            kblk, vblk = k_ref[pl.ds(k0, n_keys), :], v_ref[pl.ds(k0, n_keys), :]
            valid = _tile_mask(n_keys, nq, off, window) if masked else None
            ones = jnp.ones_like(vblk)
            vs = tuple(jnp.where(_lane() == L_ROW[h], ones, vblk) for h in range(2))

            def scores(h):
                return lax.dot_general(kblk, qms[h][queries], _NT, preferred_element_type=f32)

            def softmax(h, sc):
                m = carry[h][0]
                if fox:
                    sc = sc - _wide(cb_ref[pl.ds(k0, n_keys), h * LANES:(h + 1) * LANES], nq)
                if masked:
                    sc = jnp.where(valid, sc, NEG)
                m_new = jnp.maximum(m, jnp.max(sc, axis=0, keepdims=True))
                return m_new, jnp.exp(m - m_new), jnp.exp(sc - m_new).astype(bf16)

            def update(h, m_new, alpha, p):
                return m_new, alpha * carry[h][1] + lax.dot_general(vs[h], p, _TN, preferred_element_type=f32)

            if window:
                return tuple(update(h, *softmax(h, scores(h))) for h in range(2))
            scs = [scores(h) for h in range(2)]
            stats = [softmax(h, scs[h]) for h in range(2)]
            return tuple(update(h, *stats[h]) for h in range(2))

        def start(nq):
            if has_sink:
                row = lax.broadcasted_iota(jnp.int32, (LANES, nq), 0)
                return tuple((_wide(sink_ref[h:h + 1, :], nq), (row == L_ROW[h]).astype(f32)) for h in range(2))
            return tuple((jnp.full((1, nq), NEG, f32), jnp.zeros((LANES, nq), f32)) for h in range(2))

        def finish(carry, queries):
            (m0, a0), (m1, a1) = carry
            l0, l1 = a0[L_ROW[0]:L_ROW[0] + 1, :], a1[L_ROW[1]:L_ROW[1] + 1, :]
            o_t = jnp.where(top, a0 * (1.0 / l0), a1 * (1.0 / l1))
            o_ref[queries, :] = o_t.T.astype(bf16)
            lse_ref[0:1, queries] = m0 + jnp.log(l0)
            lse_ref[1:2, queries] = m1 + jnp.log(l1)

        if window:
            for c in range(t // LANES):
                queries = slice(c * LANES, (c + 1) * LANES)
                q0 = i * t + c * LANES
                k0 = pl.multiple_of(jnp.maximum(q0 - window, 0), LANES)
                finish(tile(k0, LANES + window, q0 - k0, start(LANES), True, queries), queries)
        else:
            carry = lax.fori_loop(0, i, lambda kb, c: tile(pl.multiple_of(kb * t, t), t, 0, c, False), start(t))
            half, k_own = t // 2, pl.multiple_of(i * t, t)
            carry = tile(k_own, half, 0, carry, True)
            finish(tuple((m[:, :half], a[:, :half]) for m, a in carry), slice(0, half))
            carry = tuple((m[:, half:], a[:, half:]) for m, a in carry)
            finish(tile(pl.multiple_of(k_own + half, half), half, 0, carry, True, slice(half, t)), slice(half, t))

    q_spec = pl.BlockSpec((t, LANES), lambda j, i: (i, j))
    kv_spec = pl.BlockSpec((s, LANES), lambda j, i: (0, j))
    in_specs, args = [q_spec, kv_spec, kv_spec], [q, k, v]
    if fox:
        in_specs += [pl.BlockSpec((s, 2 * LANES), lambda j, i: (0, j))]
        args += [cum_b]
    if has_sink:
        in_specs += [pl.BlockSpec((None, 2, LANES), lambda j, i: (j, 0, 0))]
        args += [sink_rows.reshape(N_PAIRS, 2, LANES)]
    return pl.pallas_call(
        body, name=name, grid=(N_PAIRS, s // t), in_specs=in_specs,
        out_specs=[q_spec, pl.BlockSpec((None, 2, t), lambda j, i: (j, 0, i))],
        out_shape=[jax.ShapeDtypeStruct((s, N_PAIRS * LANES), bf16), jax.ShapeDtypeStruct((N_PAIRS, 2, s), f32)],
        compiler_params=_params(2),
    )(*args)


def _branch_dgrad_delta(db, w, o, name, *, lse=None, sink_rows=None, after=None):
    s, hw = o.shape
    tm = _row_tile(s, 512)
    has_sink = sink_rows is not None
    extra = [] if after is None else [after]

    def body(*refs):
        db_ref, w_ref, o_ref = refs[:3]
        outs = refs[3 + (2 if has_sink else 0) + len(extra):]
        do_ref, dl_ref = outs[:2]
        if has_sink:
            lse_ref, sink_ref = refs[3:5]
            ds_ref = outs[2]

            @pl.when(pl.program_id(0) == 0)
            def _():
                ds_ref[...] = jnp.zeros_like(ds_ref)
        do = lax.dot_general(db_ref[...], w_ref[...], _NT, preferred_element_type=f32).astype(bf16)
        do_ref[...] = do
        for j in range(N_PAIRS):
            cols = slice(j * LANES, (j + 1) * LANES)
            prod_t = (do[:, cols].astype(f32) * o_ref[:, cols].astype(f32)).T
            for h in range(2):
                dl = jnp.sum(prod_t[h * HEAD_DIM:(h + 1) * HEAD_DIM, :], axis=0, keepdims=True)
                dl_ref[j, h:h + 1, :] = dl
                if has_sink:
                    r = 2 * j + h
                    p_sink = jnp.exp(sink_ref[r:r + 1, 0:1] - lse_ref[j, h:h + 1, :])
                    ds_ref[r:r + 1, :] += -jnp.sum(p_sink * dl, axis=1, keepdims=True)

    rows_spec = pl.BlockSpec((N_PAIRS, 2, tm), lambda i: (0, 0, i))
    in_specs = [_row_spec(tm, db.shape[1]), pl.BlockSpec(w.shape, lambda i: (0, 0)), _row_spec(tm, hw)]
    args = [db, w, o]
    out_specs = [_row_spec(tm, hw), rows_spec]
    out_shape = [jax.ShapeDtypeStruct((s, hw), bf16), jax.ShapeDtypeStruct((N_PAIRS, 2, s), f32)]
    if has_sink:
        in_specs += [rows_spec, _vec_spec(LANES, N_HEADS)]
        args += [lse, sink_rows]
        out_specs += [_vec_spec(LANES, N_HEADS)]
        out_shape += [jax.ShapeDtypeStruct((N_HEADS, LANES), f32)]
    return pl.pallas_call(
        body, name=name, grid=(s // tm,), in_specs=in_specs + [pl.BlockSpec(memory_space=pl.ANY)] * len(extra),
        out_specs=out_specs, out_shape=out_shape, compiler_params=_params(1),
    )(*args, *extra)


def _attn_bwd(q, k, v, do, lse, delta, name, *, cum_b=None, window=None, t=256):
    s = q.shape[0]
    t = _row_tile(s, t)
    nblk = s // t
    fox = cum_b is not None
    assert not window or (window % LANES == 0 and LANES + window <= s)

    def body(*refs):
        k_ref, v_ref, q_ref, do_ref, lse_ref, dl_ref = refs[:6]
        rest = list(refs[6:])
        cb_ref = rest.pop(0) if fox else None
        dq_ref, dk_ref, dv_ref = rest[:3]
        dcs_ref, rs_ref = (rest[3], rest[4]) if fox else (None, None)
        dk_acc, dv_acc = rest[-2:]
        b = pl.program_id(1)
        k0 = pl.multiple_of(b * t, t)

        @pl.when(b == 0)
        def _():
            dq_ref[...] = jnp.zeros_like(dq_ref)
            if fox:
                rs_ref[...] = jnp.zeros_like(rs_ref)

        dk_acc[...] = jnp.zeros_like(dk_acc)
        dv_acc[...] = jnp.zeros_like(dv_acc)
        if fox:
            dcs_ref[...] = jnp.zeros_like(dcs_ref)
        low = _lane() < HEAD_DIM
        top = lax.broadcasted_iota(jnp.int32, (LANES, 1), 0) < HEAD_DIM
        kblk, vblk = k_ref[...], v_ref[...]
        k_t = kblk.astype(f32).T.astype(bf16)
        cks = [_wide(cb_ref[pl.ds(k0, t), h * LANES:(h + 1) * LANES], t) for h in range(2)] if fox else None

        def tile(q0, n_queries, off, masked, keys=slice(0, t)):
            cols = pl.ds(q0, n_queries)
            q2, do2 = q_ref[cols, :], do_ref[cols, :]
            zero = jnp.zeros_like(q2)
            valid = _tile_mask(keys.stop - keys.start, n_queries, off, window) if masked else None
            dq_parts = []
            for h in range(2):
                qm = jnp.where(low, q2, zero) if h == 0 else jnp.where(low, zero, q2)
                dom = jnp.where(low, do2, zero) if h == 0 else jnp.where(low, zero, do2)
                sc = lax.dot_general(kblk[keys], qm, _NT, preferred_element_type=f32)
                if fox:
                    sc = sc - cks[h][keys, :n_queries]
                if masked:
                    sc = jnp.where(valid, sc, NEG)
                p = jnp.exp(sc - lse_ref[h:h + 1, cols])
                dp = lax.dot_general(vblk[keys], dom, _NT, preferred_element_type=f32)
                ds = p * (dp - dl_ref[h:h + 1, cols])
                pb, dsb = p.astype(bf16), ds.astype(bf16)
                dv_acc[keys, :] += jnp.dot(pb, dom, preferred_element_type=f32)
                dk_acc[keys, :] += jnp.dot(dsb, qm, preferred_element_type=f32)
                dq_parts.append(jnp.dot(k_t[:, keys], dsb, preferred_element_type=f32))
                if fox:
                    dcs_ref[keys, h * LANES:(h + 1) * LANES] += sum(ds[:, g * LANES:(g + 1) * LANES]
                                                                    for g in range(n_queries // LANES))
                    rs_ref[h:h + 1, cols] += jnp.sum(ds, axis=0, keepdims=True)
            dq_ref[:, cols] += jnp.where(top, dq_parts[0], dq_parts[1])

        def later_block(qb, carry):
            tile(pl.multiple_of(qb * t, t), t, 0, False)
            return carry

        if window:
            for c in range(t // LANES):
                first = b * t + c * LANES
                q0 = pl.multiple_of(jnp.minimum(first, s - (LANES + window)), LANES)
                tile(q0, LANES + window, q0 - first, True, slice(c * LANES, (c + 1) * LANES))
        else:
            half = t // 2
            tile(k0, half, 0, True, slice(0, half))
            tile(pl.multiple_of(k0 + half, half), half, half, True)
            lax.fori_loop(b + 1, nblk, later_block, 0)
        dk_ref[...] = dk_acc[...].astype(bf16)
        dv_ref[...] = dv_acc[...].astype(bf16)

    kv_spec = pl.BlockSpec((t, LANES), lambda j, b: (b, j))
    seq_spec = pl.BlockSpec((s, LANES), lambda j, b: (0, j))
    rows_spec = pl.BlockSpec((None, 2, s), lambda j, b: (j, 0, 0))
    hw = N_PAIRS * LANES
    in_specs, args = [kv_spec, kv_spec, seq_spec, seq_spec, rows_spec, rows_spec], [k, v, q, do, lse, delta]
    out_specs = [pl.BlockSpec((LANES, s), lambda j, b: (j, 0)), kv_spec, kv_spec]
    out_shape = [jax.ShapeDtypeStruct((hw, s), f32), jax.ShapeDtypeStruct((s, hw), bf16), jax.ShapeDtypeStruct((s, hw), bf16)]
    if fox:
        in_specs += [pl.BlockSpec((s, 2 * LANES), lambda j, b: (0, j))]
        args += [cum_b]
        out_specs += [pl.BlockSpec((t, 2 * LANES), lambda j, b: (b, j)), rows_spec]
        out_shape += [jax.ShapeDtypeStruct((s, N_HEADS * LANES), f32), jax.ShapeDtypeStruct((N_PAIRS, 2, s), f32)]
    return pl.pallas_call(
        body, name=name, grid=(N_PAIRS, nblk), in_specs=in_specs, out_specs=out_specs, out_shape=out_shape,
        scratch_shapes=[pltpu.VMEM((t, LANES), f32)] * 2, compiler_params=_params(2),
    )(*args)


def _branch_merge(o_a, o_b, w_a, w_b, gl, name):
    s, k = o_a.shape
    d = w_a.shape[1]
    tm = _row_tile(s, 1024)

    def body(oa_ref, ob_ref, wa_ref, wb_ref, g_ref, ba_ref, bb_ref, m_ref):
        ba = jnp.dot(oa_ref[...], wa_ref[...], preferred_element_type=f32)
        bb = jnp.dot(ob_ref[...], wb_ref[...], preferred_element_type=f32)
        g0, g1 = jax.nn.sigmoid(g_ref[:, :d].astype(f32)), jax.nn.sigmoid(g_ref[:, d:].astype(f32))
        ba_ref[...] = ba.astype(bf16)
        bb_ref[...] = bb.astype(bf16)
        m_ref[...] = (g0 * ba + g1 * bb).astype(bf16)

    whole = pl.BlockSpec((k, d), lambda i: (0, 0))
    return pl.pallas_call(
        body, name=name, grid=(s // tm,),
        in_specs=[_row_spec(tm, k), _row_spec(tm, k), whole, whole, _row_spec(tm, 2 * d)],
        out_specs=[_row_spec(tm, d)] * 3, out_shape=[jax.ShapeDtypeStruct((s, d), bf16)] * 3, compiler_params=_params(1),
    )(o_a, o_b, w_a, w_b, gl)


def _out_dgrad_merge_bwd(dy, w_out, ba, bb, gl, name):
    s, d = ba.shape
    tm = _row_tile(s, 512)

    def body(dy_ref, w_ref, a_ref, b_ref, g_ref, da_ref, db_ref, dg_ref):
        dmv = lax.dot_general(dy_ref[...], w_ref[...], _NT, preferred_element_type=f32)
        g0, g1 = jax.nn.sigmoid(g_ref[:, :d].astype(f32)), jax.nn.sigmoid(g_ref[:, d:].astype(f32))
        da_ref[...] = (dmv * g0).astype(bf16)
        db_ref[...] = (dmv * g1).astype(bf16)
        dg_ref[:, :d] = (dmv * a_ref[...].astype(f32) * (g0 * (1.0 - g0))).astype(bf16)
        dg_ref[:, d:] = (dmv * b_ref[...].astype(f32) * (g1 * (1.0 - g1))).astype(bf16)

    return pl.pallas_call(
        body, name=name, grid=(s // tm,),
        in_specs=[_row_spec(tm, dy.shape[1]), pl.BlockSpec(w_out.shape, lambda i: (0, 0))] + [_row_spec(tm, d)] * 2
        + [_row_spec(tm, 2 * d)],
        out_specs=[_row_spec(tm, d)] * 2 + [_row_spec(tm, 2 * d)],
        out_shape=[jax.ShapeDtypeStruct((s, d), bf16)] * 2 + [jax.ShapeDtypeStruct((s, 2 * d), bf16)],
        compiler_params=_params(1),
    )(dy, w_out, ba, bb, gl)


GLU_TILE = 256


def _ffn_in_swiglu(h, w_t, name):
    s, d = h.shape
    f = w_t.shape[0] // 2
    tm = _row_tile(s, 2048)
    tg = GLU_TILE
    nb = f // tg

    def body(h_ref, wg_ref, wu_ref, g_ref, u_ref, act_ref):
        hv = h_ref[...]
        g = lax.dot_general(hv, wg_ref[...], _NT, preferred_element_type=f32)
        u = lax.dot_general(hv, wu_ref[...], _NT, preferred_element_type=f32)
        g_ref[...] = g.astype(bf16)
        u_ref[...] = u.astype(bf16)
        act_ref[...] = (g * jax.nn.sigmoid(g) * u).astype(bf16)

    col = pl.BlockSpec((tm, tg), lambda i, j: (i, j))
    return pl.pallas_call(
        body, name=name, grid=(s // tm, nb),
        in_specs=[pl.BlockSpec((tm, d), lambda i, j: (i, 0)), pl.BlockSpec((tg, d), lambda i, j: (j, 0)),
                  pl.BlockSpec((tg, d), lambda i, j: (j + nb, 0))],
        out_specs=[col] * 3, out_shape=[jax.ShapeDtypeStruct((s, f), bf16)] * 3, compiler_params=_params(2),
    )(h, w_t, w_t)


def _ffn_out_dgrad_swiglu(dy, w_out, g, u, name):
    s, d = dy.shape
    f = g.shape[1]
    tm = _row_tile(s, 2048)
    tg = GLU_TILE

    def body(dy_ref, w_ref, g_ref, u_ref, dg_ref, du_ref):
        dv = lax.dot_general(dy_ref[...], w_ref[...], _NT, preferred_element_type=f32)
        gv, uv = g_ref[...].astype(f32), u_ref[...].astype(f32)
        sg = jax.nn.sigmoid(gv)
        dg_ref[...] = (dv * uv * (sg * (1.0 + gv * (1.0 - sg)))).astype(bf16)
        du_ref[...] = (dv * (gv * sg)).astype(bf16)

    col = pl.BlockSpec((tm, tg), lambda i, j: (i, j))
    return pl.pallas_call(
        body, name=name, grid=(s // tm, f // tg),
        in_specs=[pl.BlockSpec((tm, d), lambda i, j: (i, 0)), pl.BlockSpec((tg, d), lambda i, j: (j, 0)), col, col],
        out_specs=[col] * 2, out_shape=[jax.ShapeDtypeStruct((s, f), bf16)] * 2, compiler_params=_params(2),
    )(dy, w_out, g, u)


def _wgrad_stack(parts, h, name):
    s, m = parts[0].shape
    d = h.shape[1]
    tm = 256
    nb = m // tm
    n = len(parts)

    def body(*refs):
        i = pl.program_id(0)
        for p in range(n):
            @pl.when(i // nb == p)
            def _(p=p):
                refs[n + 1][...] = lax.dot_general(refs[p][...], refs[n][...], _TN, preferred_element_type=f32).astype(bf16)

    a_specs = [pl.BlockSpec((s, tm), lambda i, p=p: (0, jnp.clip(i - p * nb, 0, nb - 1))) for p in range(n)]
    return pl.pallas_call(
        body, name=name, grid=(n * nb,), in_specs=a_specs + [pl.BlockSpec((s, d), lambda i: (0, 0))],
        out_specs=pl.BlockSpec((tm, d), lambda i: (i, 0)),
        out_shape=jax.ShapeDtypeStruct((n * m, d), bf16), compiler_params=_params(1),
    )(*parts, h)


def _ada_wgrad(c_all, d_all, name):
    n, d = c_all.shape
    w = d_all.shape[1]

    def body(c_ref, d_ref, o_ref):
        eye = (lax.broadcasted_iota(jnp.int32, (n, n), 0) == lax.broadcasted_iota(jnp.int32, (n, n), 1)).astype(f32)
        ct = lax.dot_general(c_ref[...], eye, _TN, precision=lax.Precision.HIGHEST, preferred_element_type=f32)
        g = ct[:, 0:1] * d_ref[0:1, :]
        for bi in range(1, n):
            g = g + ct[:, bi:bi + 1] * d_ref[bi:bi + 1, :]
        o_ref[0] = g

    return pl.pallas_call(
        body, name=name, out_shape=jax.ShapeDtypeStruct((1, d, w), f32), compiler_params=_params(),
    )(c_all, d_all)


def _adamw(parts, w, m, v, name, mine=None, me=None):
    r, c = w.shape
    n_parts = parts.shape[0]
    row_tiles = [t for t in range(min(r, 256), 0, -1) if r % t == 0 and (t % 16 == 0 or t == r)]
    if row_tiles:
        tr, tc = row_tiles[0], c
    else:
        tr, tc = r, next(t for t in (256, LANES) if c % t == 0)

    def body(*refs):
        w_ref, m_ref, v_ref, g_ref, d_ref, nm_ref, nv_ref = refs[-7:]
        if mine is None:
            p_ref, = refs[:-7]
        else:
            me_ref, p_ref, own_ref = refs[:-7]

        def part(i):
            if mine is None:
                return p_ref[i].astype(f32)
            return jnp.where(me_ref[0] == i, own_ref[...], p_ref[i]).astype(f32)

        g = part(0)
        for i in range(1, n_parts):
            g = g + part(i)
        mm = ADAM_B1 * m_ref[...] + (1.0 - ADAM_B1) * g
        vv = ADAM_B2 * v_ref[...] + (1.0 - ADAM_B2) * (g * g)
        m_hat = mm / (1.0 - ADAM_B1 ** ADAM_STEP)
        v_hat = vv / (1.0 - ADAM_B2 ** ADAM_STEP)
        g_ref[...] = g
        d_ref[...] = -ADAM_LR * (m_hat / (jnp.sqrt(v_hat) + ADAM_EPS) + ADAM_WD * w_ref[...])
        nm_ref[...] = mm
        nv_ref[...] = vv

    out_shape = [jax.ShapeDtypeStruct((r, c), f32)] * 4
    if mine is None:
        spec = pl.BlockSpec((tr, tc), lambda i, j: (i, j))
        return pl.pallas_call(
            body, name=name, grid=(r // tr, c // tc),
            in_specs=[pl.BlockSpec((n_parts, tr, tc), lambda i, j: (0, i, j))] + [spec] * 3,
            out_specs=[spec] * 4, out_shape=out_shape, compiler_params=_params(2),
        )(parts, w, m, v)
    spec = pl.BlockSpec((tr, tc), lambda i, j, me_ref: (i, j))
    return pl.pallas_call(
        body, name=name, out_shape=out_shape, compiler_params=_params(2),
        grid_spec=pltpu.PrefetchScalarGridSpec(
            num_scalar_prefetch=1, grid=(r // tr, c // tc),
            in_specs=[pl.BlockSpec((n_parts, tr, tc), lambda i, j, me_ref: (0, i, j)),
                      pl.BlockSpec((None, tr, tc), lambda i, j, me_ref: (me_ref[0], i, j))] + [spec] * 3,
            out_specs=[spec] * 4),
    )(me, parts, mine, w, m, v)


def _me():
    return lax.axis_index("x"), lax.axis_index("y"), lax.axis_index("c")


def _gather_prologue(c, w_ada, b_mine, w_in_t, name):
    n_dev, d = N_DEV, c.shape[1]
    ada_w = w_ada.shape[1]

    def body(c_ref, w_ref, b_ref, win_ref, call_ref, ada_ref, gin_ref, cols_ref, send_sems, recv_sems, local_sems):
        x, y, cc = _me()
        me, sibling = (x, y, cc), (x, y, 1 - cc)
        chips = [(1 - x, y), (x, 1 - y), (1 - x, 1 - y)]
        outs = (call_ref, ada_ref, gin_ref)

        def rows(a, dev):
            return outs[a].at[4 * dev[0] + 2 * dev[1] + dev[2]]

        def copy(a, k, block, to, src=None):
            return pltpu.make_async_remote_copy(
                src_ref=rows(a, block) if src is None else src, dst_ref=rows(a, block),
                send_sem=send_sems.at[a, k], recv_sem=recv_sems.at[a, k], device_id=to, device_id_type=MESH)

        def begin(a, src):
            own = pltpu.make_async_copy(src, rows(a, me), local_sems.at[a])
            sends = [copy(a, 0, me, sibling, src=src)] + [copy(a, 1 + j, me, (*chip, cc), src=src) for j, chip in enumerate(chips)]
            for cp in [own] + sends:
                cp.start()
            return own, sends

        def finish(a, own, sends):
            passed = []
            for j, chip in enumerate(chips):
                copy(a, 1 + j, (*chip, cc), me).wait_recv()
                passed.append(copy(a, 4 + j, (*chip, cc), sibling))
                passed[-1].start()
            copy(a, 0, sibling, me).wait_recv()
            for j, chip in enumerate(chips):
                copy(a, 4 + j, (*chip, 1 - cc), me).wait_recv()
            for cp in sends + passed:
                cp.wait_send()
            own.wait()

        finish(0, *begin(0, c_ref))
        cols_ref[...] = (jnp.dot(call_ref[:, 0, :].astype(bf16), w_ref[...].astype(bf16), preferred_element_type=f32)
                         + b_ref[...])
        finish(1, *begin(1, cols_ref))
        finish(2, *begin(2, win_ref))

    vmem, hbm = pl.BlockSpec(memory_space=pltpu.VMEM), pl.BlockSpec(memory_space=pl.ANY)
    return pl.pallas_call(
        body, name=name, in_specs=[vmem, vmem, vmem, hbm], out_specs=[vmem, vmem, hbm],
        out_shape=[jax.ShapeDtypeStruct((n_dev, 1, d), f32), jax.ShapeDtypeStruct((n_dev, n_dev, ada_w), f32),
                   jax.ShapeDtypeStruct((n_dev,) + w_in_t.shape, w_in_t.dtype)],
        scratch_shapes=[pltpu.VMEM((n_dev, ada_w), f32), pltpu.SemaphoreType.DMA((3, 7)), pltpu.SemaphoreType.DMA((3, 7)),
                        pltpu.SemaphoreType.DMA((3,))],
        compiler_params=pltpu.CompilerParams(vmem_limit_bytes=VMEM_LIMIT),
    )(c, w_ada, b_mine, w_in_t)


_FLIPS = ((0, 0, 1), (1, 0, 0), (0, 1, 0), (1, 1, 0), (1, 0, 1), (0, 1, 1), (1, 1, 1))
_HBM = pl.BlockSpec(memory_space=pltpu.HBM)
_SEM = pl.BlockSpec(memory_space=pltpu.SEMAPHORE)


def _exchange_copies(scatter, srcs, lands, send_sems, recv_sems):
    x, y, c = _me()
    me_row = 4 * x + 2 * y + c
    out = []
    for k, (fx, fy, fc) in enumerate(_FLIPS):
        peer = (x ^ fx, y ^ fy, c ^ fc)
        peer_row = 4 * peer[0] + 2 * peer[1] + peer[2]
        for a in range(len(srcs)):
            out.append(pltpu.make_async_remote_copy(
                src_ref=srcs[a].at[peer_row] if scatter else srcs[a], dst_ref=lands[a].at[me_row],
                send_sem=send_sems.at[7 * a + k], recv_sem=recv_sems.at[7 * a + k], device_id=peer, device_id_type=MESH))
    return out


def _exchange_start(arrays, scatter, name, after=None):
    n = len(arrays)
    lands = [lax.empty(a.shape if scatter else (N_DEV,) + a.shape, a.dtype) for a in arrays]
    extra = [] if after is None else [after]

    def body(*refs):
        srcs, zones = refs[:n], refs[n:2 * n]
        send_sems, recv_sems = refs[2 * n + len(extra)], refs[2 * n + len(extra) + 1]
        token = refs[-1]
        for cp in _exchange_copies(scatter, srcs, zones, send_sems, recv_sems):
            cp.start()
        token[...] = jnp.zeros_like(token)

    thru = [pltpu.HBM(a.shape, a.dtype) for a in list(arrays) + lands]
    outs = pl.pallas_call(
        body, name=name,
        out_shape=(pltpu.SemaphoreType.DMA((7 * n,)), pltpu.SemaphoreType.DMA((7 * n,)), *thru, jax.ShapeDtypeStruct((8, LANES), f32)),
        in_specs=[_HBM] * (2 * n) + [pl.BlockSpec(memory_space=pl.ANY)] * len(extra),
        out_specs=(_SEM, _SEM, *[_HBM] * (2 * n), pl.BlockSpec(memory_space=pltpu.VMEM)),
        input_output_aliases={i: 2 + i for i in range(2 * n)},
        compiler_params=pltpu.CompilerParams(has_side_effects=pltpu.SideEffectType.DATAFLOW_SIDE_EFFECTING),
    )(*[pltpu.with_memory_space_constraint(a, pltpu.HBM) for a in list(arrays) + lands], *extra)
    return dict(n=n, scatter=scatter, sems=outs[:2], srcs=outs[2:2 + n], lands=outs[2 + n:2 + 2 * n], token=outs[-1])


def _exchange_wait(handle, after, name):
    n, scatter = handle["n"], handle["scatter"]

    def body(*refs):
        srcs, zones = refs[:n], refs[n:2 * n]
        send_sems, recv_sems = refs[2 * n], refs[2 * n + 1]
        for cp in _exchange_copies(scatter, srcs, zones, send_sems, recv_sems):
            cp.wait_send()
            cp.wait_recv()

    thru = [pltpu.HBM(a.shape, a.dtype) for a in list(handle["srcs"]) + list(handle["lands"])]
    outs = pl.pallas_call(
        body, name=name, out_shape=tuple(thru),
        in_specs=[_HBM] * (2 * n) + [_SEM, _SEM, pl.BlockSpec(memory_space=pl.ANY)], out_specs=tuple([_HBM] * (2 * n)),
        input_output_aliases={i: i for i in range(2 * n)},
        compiler_params=pltpu.CompilerParams(has_side_effects=pltpu.SideEffectType.DATAFLOW_SIDE_EFFECTING),
    )(*handle["srcs"], *handle["lands"], *handle["sems"], after)
    return list(outs[:n]), list(outs[n:])


def _cols_from_shards(g):
    return jnp.transpose(g, (1, 0, 2)).reshape(g.shape[1], -1)


def _shards_from_cols(a):
    return jnp.transpose(a.reshape(a.shape[0], N_DEV, -1), (1, 0, 2))


def _local_step(x, positions, ada, g_pre_mix, g_post_mix, b_f, sinks, g_pre_ffn, g_post_ffn, target,
                w_in_t, mix_weights, ffn_weights, on_grads):
    s, d = x.shape
    row = lambda v: v.reshape(1, -1)
    shift_m, scale_m, gate_m, shift_f, scale_f, gate_f = (ada[i:i + 1] for i in range(6))
    w_gate_t, w_qkv_t = w_in_t[F_OFF + N_HEADS:], w_in_t[:QKV_W]
    w_f_t = jnp.pad(w_in_t[F_OFF:F_OFF + N_HEADS], ((0, LANES - N_HEADS), (0, 0)))
    bf_row = jnp.pad(row(b_f), ((0, 0), (0, LANES - N_HEADS)))
    sink_rows = jnp.broadcast_to(sinks.reshape(N_HEADS, 1).astype(f32), (N_HEADS, LANES))
    inv_freq = 1.0 / (ROPE_THETA ** (jnp.arange(0, HEAD_DIM, 2, dtype=f32) / HEAD_DIM))
    cos, sin_s = _rope_tables(positions.reshape(s, 1), jnp.tile(inv_freq, 4).reshape(1, LANES), "rope_tables")

    h1, qa, ka, va, qb, kb, vb = _prenorm_proj_qkv(x, row(g_pre_mix), scale_m, shift_m, w_qkv_t, cos, sin_s, "prenorm_proj_qkv")
    gl = _matmul(h1, w_gate_t, "nt", bf16, "proj_gate")
    fl, cum_b = _forget_prep(h1, w_f_t, bf_row, "proj_forget_prep")
    o_a, lse_a = _attn_fwd(qa, ka, va, "swa_fwd", sink_rows=sink_rows, window=WINDOW, t=2048)
    o_b, lse_b = _attn_fwd(qb, kb, vb, "fox_fwd", cum_b=cum_b, t=1024)
    everything_before = (gl[:8, :LANES] + o_a[:8, :LANES] + o_b[:8, :LANES]).astype(f32)
    w_branch_a, w_branch_b, w_out = mix_weights(everything_before)
    ba, bb, merged = _branch_merge(o_a, o_b, w_branch_a, w_branch_b, gl, "branch_merge")
    y1, x2, h2 = _out_proj_postnorm_prenorm(merged, w_out, x, row(g_post_mix), gate_m, row(g_pre_ffn), scale_f, shift_f,
                                            "out_proj_norms")

    w_ffn_in_t, w_ffn_out = ffn_weights(h2)
    g_ff, u_ff, act = _ffn_in_swiglu(h2, w_ffn_in_t, "ffn_in_swiglu")
    loss_row, d_out, d_y2, vec_pf = _out_proj_loss_tail(act, w_ffn_out, x2, row(g_post_ffn), gate_f, target, "ffn_out_loss_tail")

    g_w_ffn_out = _matmul(act, d_y2, "tn", bf16, "ffn_out_wgrad")
    dg_ff, du_ff = _ffn_out_dgrad_swiglu(d_y2, w_ffn_out, g_ff, u_ff, "ffn_out_dgrad_swiglu")
    g_w_ffn_in_t = _wgrad_stack([dg_ff, du_ff], h2, "ffn_in_wgrad")
    sent = on_grads(dict(w_ffn_in=g_w_ffn_in_t, w_ffn_out=g_w_ffn_out))
    d_x2, vec_nf, d_y1, vec_pm = _dgrad_prenorm_bwd(
        [(dg_ff, w_ffn_in_t, 0), (du_ff, w_ffn_in_t, 1)], x2, row(g_pre_ffn), scale_f, d_out, "ffn_in_dgrad_norms_bwd",
        after=sent, below=(y1, row(g_post_mix), gate_m))

    g_w_out = _matmul(merged, d_y1, "tn", bf16, "out_proj_wgrad")
    d_ba, d_bb, dgl = _out_dgrad_merge_bwd(d_y1, w_out, ba, bb, gl, "out_proj_dgrad_merge_bwd")
    g_w_branch_a = _matmul(o_a, d_ba, "tn", bf16, "branch_a_wgrad")
    g_w_branch_b = _matmul(o_b, d_bb, "tn", bf16, "branch_b_wgrad")
    sent = on_grads(dict(w_out=g_w_out, w_branch_a=g_w_branch_a, w_branch_b=g_w_branch_b))
    d_oa, delta_a, d_sink = _branch_dgrad_delta(d_ba, w_branch_a, o_a, "branch_a_dgrad_delta", lse=lse_a,
                                                sink_rows=sink_rows, after=sent)
    d_ob, delta_b = _branch_dgrad_delta(d_bb, w_branch_b, o_b, "branch_b_dgrad_delta", after=sent)
    dqa_t, dka, dva = _attn_bwd(qa, ka, va, d_oa, lse_a, delta_a, "swa_bwd", window=WINDOW, t=2048)
    dqb_t, dkb, dvb, dcs, rs = _attn_bwd(qb, kb, vb, d_ob, lse_b, delta_b, "fox_bwd", cum_b=cum_b, t=512)
    dqkv = _qkv_prep_bwd(dqa_t, dka, dva, dqb_t, dkb, dvb, cos, sin_s, "qkv_prep_bwd")
    dfl, vec_bf = _forget_prep_bwd(rs.reshape(N_HEADS, s), dcs, fl, bf_row, "forget_prep_bwd")
    g_w_in_t = jnp.concatenate([_matmul(dqkv, h1, "tn", bf16, "qkv_wgrad"), _matmul(dfl, h1, "tn", bf16, "forget_wgrad")[:N_HEADS],
                                _matmul(dgl, h1, "tn", bf16, "gate_wgrad")], axis=0)
    sent = on_grads(dict(w_in=g_w_in_t))
    grad_x, vec_nm = _dgrad_prenorm_bwd([(dgl, w_gate_t, 0), (dqkv, w_qkv_t, 0), (dfl, w_f_t, 0)], x, row(g_pre_mix),
                                        scale_m, d_x2, "in_proj_dgrad_prenorm_bwd", after=sent)

    d_ada = jnp.concatenate([vec_nm[0], vec_nm[1], vec_pm[0], vec_nf[0], vec_nf[1], vec_pf[0]])
    small = dict(b_ada=d_ada, g_pre_mix=vec_nm[2], g_post_mix=vec_pm[1], g_pre_ffn=vec_nf[2], g_post_ffn=vec_pf[1],
                 b_f=vec_bf[0, :N_HEADS], sinks=d_sink[:, 0], loss=loss_row[0, :1])
    return grad_x, small


_SMALL = (("b_ada", 6144), ("g_pre_mix", 1024), ("g_post_mix", 1024), ("g_pre_ffn", 1024), ("g_post_ffn", 1024),
          ("b_f", 128), ("sinks", 128), ("loss", 128))
_SMALL_ROWS = 88


def _pack_small(vals):
    parts = [jnp.pad(vals[k].reshape(-1).astype(f32), (0, n - vals[k].size)) for k, n in _SMALL]
    flat = jnp.concatenate(parts)
    return jnp.pad(flat, (0, _SMALL_ROWS * LANES - flat.size)).reshape(_SMALL_ROWS, LANES)


def _unpack_small(slab, shapes):
    flat, out, off = slab.reshape(-1), {}, 0
    for k, n in _SMALL:
        size = math.prod(shapes[k])
        out[k] = flat[off:off + size].reshape(shapes[k])
        off += n
    return out


def kernel(x, c, positions, w_ada, b_ada, g_pre_mix, g_post_mix, w_in, b_f, sinks, w_branch_a, w_branch_b, w_out, g_pre_ffn, g_post_ffn, w_ffn_in, w_ffn_out, loss_target, m_w_ada, m_b_ada, m_g_pre_mix, m_g_post_mix, m_w_in, m_b_f, m_sinks, m_w_branch_a, m_w_branch_b, m_w_out, m_g_pre_ffn, m_g_post_ffn, m_w_ffn_in, m_w_ffn_out, v_w_ada, v_b_ada, v_g_pre_mix, v_g_post_mix, v_w_in, v_b_f, v_sinks, v_w_branch_a, v_w_branch_b, v_w_out, v_g_pre_ffn, v_g_post_ffn, v_w_ffn_in, v_w_ffn_out):
    xi, yi, ci = _me()
    me = 4 * xi + 2 * yi + ci
    d = D_MODEL
    ada_w = w_ada.shape[2]

    transposed = ("w_in", "w_ffn_in")
    tr = lambda a: jnp.transpose(a[0])

    b_mine = lax.dynamic_slice(b_ada, (0, me * ada_w), (1, ada_w))
    c_all, ada_all, g_in = _gather_prologue(c, w_ada[0], b_mine, tr(w_in).astype(bf16), "gather_prologue")
    c_all = c_all.reshape(N_DEV, d)
    ada = lax.dynamic_index_in_dim(ada_all, me, axis=1, keepdims=False).reshape(6, d)
    late_mix = [w.astype(bf16) for w in (w_branch_a[0], w_branch_b[0], w_out[0])]
    late_ffn = [w.astype(bf16) for w in (tr(w_ffn_in), w_ffn_out[0])]
    mix_h = _exchange_start(late_mix, False, "gather_mix_start", after=g_in)
    ffn_h = _exchange_start(late_ffn, False, "gather_ffn_start", after=mix_h["token"])

    def mine_into(zone, block):
        return lax.dynamic_update_index_in_dim(zone, block, me, 0)

    def rows_from_shards(g):
        return g.reshape(g.shape[0] * g.shape[1], g.shape[2])

    def mix_weights(after):
        sent, zones = _exchange_wait(mix_h, after, "gather_mix_wait")
        g_ba, g_bb, g_out = (mine_into(z, w) for z, w in zip(zones, sent))
        return _cols_from_shards(g_ba), _cols_from_shards(g_bb), rows_from_shards(g_out)

    def ffn_weights(after):
        sent, zones = _exchange_wait(ffn_h, after, "gather_ffn_wait")
        g_fi, g_fo = (mine_into(z, w) for z, w in zip(zones, sent))
        return rows_from_shards(g_fi), rows_from_shards(g_fo)

    row_sharded = ("w_out", "w_ffn_out") + transposed
    in_flight = []

    def on_grads(group):
        sends = [g.reshape(N_DEV, g.shape[0] // N_DEV, g.shape[1]) if nm in row_sharded else _shards_from_cols(g)
                 for nm, g in group.items()]
        handle = _exchange_start(sends, True, "scatter_start_%d" % len(in_flight))
        in_flight.append((list(group), handle))
        return handle["token"]

    grad_x, small = _local_step(
        x[0], positions[0], ada + ffn_h["token"][0, 0], g_pre_mix[0], g_post_mix[0], b_f[0], sinks[0], g_pre_ffn[0],
        g_post_ffn[0], loss_target[0], rows_from_shards(g_in), mix_weights, ffn_weights, on_grads)

    ws = dict(w_in=(w_in, m_w_in, v_w_in), w_branch_a=(w_branch_a, m_w_branch_a, v_w_branch_a),
              w_branch_b=(w_branch_b, m_w_branch_b, v_w_branch_b), w_out=(w_out, m_w_out, v_w_out),
              w_ffn_in=(w_ffn_in, m_w_ffn_in, v_w_ffn_in), w_ffn_out=(w_ffn_out, m_w_ffn_out, v_w_ffn_out))
    res = {}

    def finish_group(gi, after):
        names, handle = in_flight[gi]
        sends, zones = _exchange_wait(handle, after, "scatter_wait_%d" % gi)
        for nm, zone, sent in zip(names, zones, sends):
            w, m, v = (tr(a) if nm in transposed else a[0] for a in ws[nm])
            out = _adamw(zone, w, m, v, "adamw_" + nm, mine=sent, me=me.reshape(1).astype(jnp.int32))
            after = out[0]
            res[nm] = [jnp.transpose(o) for o in out] if nm in transposed else out
        return after

    small_h = _exchange_start([_pack_small(small)], False, "gather_small_start", after=grad_x)
    done = finish_group(1, finish_group(0, small_h["token"]))
    (slab_mine,), (slab_zone,) = _exchange_wait(small_h, done, "gather_small_wait")
    slab_all = mine_into(slab_zone, slab_mine)
    small_w = dict(b_ada=b_ada, g_pre_mix=g_pre_mix, g_post_mix=g_post_mix, g_pre_ffn=g_pre_ffn, g_post_ffn=g_post_ffn,
                   b_f=b_f, sinks=sinks, loss=jnp.zeros((1,), f32))
    small_m = dict(b_ada=m_b_ada, g_pre_mix=m_g_pre_mix, g_post_mix=m_g_post_mix, g_pre_ffn=m_g_pre_ffn,
                   g_post_ffn=m_g_post_ffn, b_f=m_b_f, sinks=m_sinks, loss=jnp.zeros((1,), f32))
    small_v = dict(b_ada=v_b_ada, g_pre_mix=v_g_pre_mix, g_post_mix=v_g_post_mix, g_pre_ffn=v_g_pre_ffn,
                   g_post_ffn=v_g_post_ffn, b_f=v_b_f, sinks=v_sinks, loss=jnp.ones((1,), f32))
    shapes = {k: small_w[k].shape for k, _ in _SMALL}
    s_out = _adamw(slab_all, _pack_small(small_w), _pack_small(small_m), _pack_small(small_v), "adamw_small")
    s_grad, s_delta, s_m, s_v = (_unpack_small(o, shapes) for o in s_out)

    d_ada_all = lax.dynamic_slice(slab_all[:, :6144 // LANES, :].reshape(N_DEV, 6144), (0, me * ada_w), (N_DEV, ada_w))
    ada_parts = _ada_wgrad(c_all, d_ada_all, "ada_wgrad")

    res["w_ada"] = _adamw(ada_parts, w_ada[0], m_w_ada[0], v_w_ada[0], "adamw_w_ada")
    finish_group(2, res["w_ada"][0])

    order = ["w_ada", "b_ada", "g_pre_mix", "g_post_mix", "w_in", "b_f", "sinks", "w_branch_a", "w_branch_b", "w_out",
             "g_pre_ffn", "g_post_ffn", "w_ffn_in", "w_ffn_out"]
    outs = [s_grad["loss"].reshape(()), grad_x[None]]
    for which, small_o in enumerate((s_grad, s_delta, s_m, s_v)):
        for nm in order:
            outs.append(res[nm][which][None] if nm in res else small_o[nm])
    return tuple(outs)
```

```python
import math

import jax
import jax.numpy as jnp
from jax import lax
from jax.experimental import pallas as pl
from jax.experimental.pallas import tpu as pltpu

f32 = jnp.float32
bf16 = jnp.bfloat16

D_MODEL = 1024
HEAD_DIM = 64
N_HEADS = 8
N_PAIRS = 4
QKV_W = 2304
F_OFF = 2304
WINDOW = 128
ROPE_THETA = 10000.0
RMS_EPS = 1e-6
N_DEV = 8
ADAM_LR, ADAM_B1, ADAM_B2, ADAM_EPS, ADAM_WD, ADAM_STEP = 0.001, 0.9, 0.999, 1e-08, 0.01, 10
NEG = -1e30
L_ROW = (HEAD_DIM, 0)
LANES = 128
VMEM_LIMIT = 48 * 1024 * 1024
MESH = pl.DeviceIdType.MESH

_NT = (((1,), (1,)), ((), ()))
_TN = (((0,), (0,)), ((), ()))


def _params(n_grid=0):
    sem = ("arbitrary",) * n_grid if n_grid else None
    return pltpu.CompilerParams(dimension_semantics=sem, vmem_limit_bytes=VMEM_LIMIT)


def _in_hbm(*arrays):
    return [pltpu.with_memory_space_constraint(a, pltpu.HBM) for a in arrays]


def _row_tile(s, want):
    t = min(s, want)
    assert s % t == 0, (s, t)
    return t


MATMUL_VMEM_BUDGET = 40 * 1024 * 1024


def _matmul_tiles(m, n, k, a_item, b_item, o_item):
    def tiles(d):
        return [t for t in range(LANES, min(d, 2048) + 1, LANES) if d % t == 0] or [d]

    best = None
    for tm in tiles(m):
        for tn in tiles(n):
            vmem = 2 * (tm * k * a_item + tn * k * b_item + tm * tn * o_item) + tm * tn * 4
            if vmem > MATMUL_VMEM_BUDGET:
                continue
            traffic = m * k * a_item + n * k * b_item * (1 if tn == n else m // tm) + m * n * o_item
            steps = (m // tm) * (n // tn)
            key = (traffic, 0, steps) if steps >= 4 else (traffic, 1, -steps)
            if best is None or key < best[0]:
                best = (key, tm, tn)
    assert best is not None, (m, n, k)
    return best[1], best[2]


def _matmul(a, b, mode, out_dtype, name, after=None):
    if mode == "nn":
        (m, k), n = a.shape, b.shape[1]
    elif mode == "nt":
        (m, k), n = a.shape, b.shape[0]
    else:
        (k, m), n = a.shape, b.shape[1]
    tm, tn = _matmul_tiles(m, n, k, a.dtype.itemsize, b.dtype.itemsize, jnp.dtype(out_dtype).itemsize)
    if mode == "nn":
        a_spec, b_spec, dims = pl.BlockSpec((tm, k), lambda i, j: (i, 0)), pl.BlockSpec((k, tn), lambda i, j: (0, j)), None
    elif mode == "nt":
        a_spec, b_spec, dims = pl.BlockSpec((tm, k), lambda i, j: (i, 0)), pl.BlockSpec((tn, k), lambda i, j: (j, 0)), _NT
    else:
        a_spec, b_spec, dims = pl.BlockSpec((k, tm), lambda i, j: (0, i)), pl.BlockSpec((k, tn), lambda i, j: (0, j)), _TN

    def body(a_ref, b_ref, *rest):
        o_ref = rest[-1]
        av, bv = a_ref[...].astype(bf16), b_ref[...].astype(bf16)
        if dims is None:
            r = jnp.dot(av, bv, preferred_element_type=f32)
        else:
            r = lax.dot_general(av, bv, dims, preferred_element_type=f32)
        o_ref[...] = r.astype(out_dtype)

    extra = [] if after is None else [after]
    return pl.pallas_call(
        body, name=name, grid=(m // tm, n // tn), in_specs=[a_spec, b_spec] + [pl.BlockSpec(memory_space=pl.ANY)] * len(extra),
        out_specs=pl.BlockSpec((tm, tn), lambda i, j: (i, j)),
        out_shape=jax.ShapeDtypeStruct((m, n), out_dtype), compiler_params=_params(2),
    )(a, b, *extra)


def _rstd(v):
    return lax.rsqrt(jnp.mean(v * v, axis=-1, keepdims=True) + RMS_EPS)


def _row_spec(tm, d):
    return pl.BlockSpec((tm, d), lambda i: (i, 0))


def _vec_spec(d, rows=1):
    return pl.BlockSpec((rows, d), lambda i: (0, 0))


def _proj_spec(a, w, tm):
    return [_row_spec(tm, a.shape[1]), pl.BlockSpec(w.shape, lambda i: (0, 0))]


def _out_proj_postnorm_prenorm(a, w, x, g_post, gate, g_pre, scale, shift, name):
    s, d = x.shape
    tm = _row_tile(s, 512)

    def body(a_ref, w_ref, x_ref, gp_ref, gate_ref, g_ref, sc_ref, sh_ref, y_ref, x2_ref, h_ref):
        yv = jnp.dot(a_ref[...], w_ref[...], preferred_element_type=f32)
        y_ref[...] = yv
        x2 = x_ref[...] + gate_ref[...] * (yv * _rstd(yv) * gp_ref[...])
        x2_ref[...] = x2
        h_ref[...] = ((x2 * _rstd(x2) * g_ref[...]) * (1.0 + sc_ref[...]) + sh_ref[...]).astype(bf16)

    return pl.pallas_call(
        body, name=name, grid=(s // tm,), in_specs=_proj_spec(a, w, tm) + [_row_spec(tm, d)] + [_vec_spec(d)] * 5,
        out_specs=[_row_spec(tm, d)] * 3,
        out_shape=[jax.ShapeDtypeStruct((s, d), f32)] * 2 + [jax.ShapeDtypeStruct((s, d), bf16)], compiler_params=_params(1),
    )(a, w, x, g_post, gate, g_pre, scale, shift)


def _rms_bwd(u, v, r):
    return r * u - v * (r * r * r) * jnp.mean(u * v, axis=-1, keepdims=True)


def _out_proj_loss_tail(a, w, x, g, gate, target, name):
    s, d = x.shape
    tm = _row_tile(s, 512)

    def body(a_ref, w_ref, x_ref, g_ref, gate_ref, t_ref, loss_ref, do_ref, dy_ref, vec_ref):
        @pl.when(pl.program_id(0) == 0)
        def _():
            loss_ref[...] = jnp.zeros_like(loss_ref)
            vec_ref[...] = jnp.zeros_like(vec_ref)
        yv = jnp.dot(a_ref[...], w_ref[...], preferred_element_type=f32)
        r = _rstd(yv)
        yn = yv * r
        err = x_ref[...] + gate_ref[...] * (yn * g_ref[...]) - t_ref[...]
        loss_ref[...] += 0.5 * jnp.sum(jnp.mean(err * err, axis=-1, keepdims=True), axis=0, keepdims=True)
        dr = err / d
        do_ref[...] = dr
        dn = dr * gate_ref[...]
        vec_ref[0:1, :] += jnp.sum(dr * (yn * g_ref[...]), axis=0, keepdims=True)
        vec_ref[1:2, :] += jnp.sum(dn * yn, axis=0, keepdims=True)
        dy_ref[...] = _rms_bwd(dn * g_ref[...], yv, r).astype(bf16)

    return pl.pallas_call(
        body, name=name, grid=(s // tm,),
        in_specs=_proj_spec(a, w, tm) + [_row_spec(tm, d)] + [_vec_spec(d)] * 2 + [_row_spec(tm, d)],
        out_specs=[_vec_spec(LANES), _row_spec(tm, d), _row_spec(tm, d), _vec_spec(d, 8)],
        out_shape=[jax.ShapeDtypeStruct((1, LANES), f32), jax.ShapeDtypeStruct((s, d), f32),
                   jax.ShapeDtypeStruct((s, d), bf16), jax.ShapeDtypeStruct((8, d), f32)],
        compiler_params=_params(1),
    )(a, w, x, g, gate, target)


def _dgrad_prenorm_bwd(terms, x, g, scale, dres, name, after=None, below=None):
    s, d = x.shape
    n = len(terms)
    k = sum(a.shape[1] for a, _, _ in terms)
    row_bytes = 2 * (2 * k) + d * (4 + 2 * 4 * 3 + (2 * 4 + 2 * 2 if below else 0))
    tm = next(t for t in (512, 256, 128) if s % t == 0 and 4 * k * d + t * row_bytes <= MATMUL_VMEM_BUDGET)
    extra = [] if after is None else [after]

    def body(*refs):
        a_refs, b_refs = refs[:n], refs[n:2 * n]
        x_ref, g_ref, sc_ref, dr_ref = refs[2 * n:2 * n + 4]
        n_in = 2 * n + 4 + (3 if below else 0) + len(extra)
        dx_ref, vec_ref = refs[n_in], refs[n_in + 1]
        if below:
            y_ref, gp_ref, gate_ref = refs[2 * n + 4:2 * n + 7]
            dy_ref, vec2_ref = refs[n_in + 2], refs[n_in + 3]

        @pl.when(pl.program_id(0) == 0)
        def _():
            vec_ref[...] = jnp.zeros_like(vec_ref)
            if below:
                vec2_ref[...] = jnp.zeros_like(vec2_ref)
        dhv = jnp.dot(a_refs[0][...], b_refs[0][...], preferred_element_type=f32)
        for i in range(1, n):
            dhv = dhv + jnp.dot(a_refs[i][...], b_refs[i][...], preferred_element_type=f32)
        xv = x_ref[...]
        r = _rstd(xv)
        xn = xv * r
        dn = dhv * (1.0 + sc_ref[...])
        vec_ref[0:1, :] += jnp.sum(dhv, axis=0, keepdims=True)
        vec_ref[1:2, :] += jnp.sum(dhv * (xn * g_ref[...]), axis=0, keepdims=True)
        vec_ref[2:3, :] += jnp.sum(dn * xn, axis=0, keepdims=True)
        dx = dr_ref[...] + _rms_bwd(dn * g_ref[...], xv, r)
        dx_ref[...] = dx
        if below:
            yv = y_ref[...]
            ry = _rstd(yv)
            yn = yv * ry
            dny = dx * gate_ref[...]
            vec2_ref[0:1, :] += jnp.sum(dx * (yn * gp_ref[...]), axis=0, keepdims=True)
            vec2_ref[1:2, :] += jnp.sum(dny * yn, axis=0, keepdims=True)
            dy_ref[...] = _rms_bwd(dny * gp_ref[...], yv, ry).astype(bf16)

    in_specs = ([_row_spec(tm, a.shape[1]) for a, _, _ in terms]
                + [pl.BlockSpec((a.shape[1], d), lambda i, r=r: (r, 0)) for a, _, r in terms]
                + [_row_spec(tm, d)] + [_vec_spec(d)] * 2 + [_row_spec(tm, d)])
    out_specs = [_row_spec(tm, d), _vec_spec(d, 8)]
    out_shape = [jax.ShapeDtypeStruct((s, d), f32), jax.ShapeDtypeStruct((8, d), f32)]
    args = [a for a, _, _ in terms] + [b for _, b, _ in terms] + [x, g, scale, dres]
    if below:
        in_specs += [_row_spec(tm, d)] + [_vec_spec(d)] * 2
        out_specs += [_row_spec(tm, d), _vec_spec(d, 8)]
        out_shape += [jax.ShapeDtypeStruct((s, d), bf16), jax.ShapeDtypeStruct((8, d), f32)]
        args += list(below)
    return pl.pallas_call(
        body, name=name, grid=(s // tm,), in_specs=in_specs + [pl.BlockSpec(memory_space=pl.ANY)] * len(extra),
        out_specs=out_specs, out_shape=out_shape, compiler_params=_params(1),
    )(*args, *extra)


def _lane():
    return lax.broadcasted_iota(jnp.int32, (1, LANES), 1)


def _rope_tables(pos_col, inv_freq, name):
    s = pos_col.shape[0]

    def body(p_ref, f_ref, cos_ref, sin_ref):
        ang = p_ref[...].astype(f32) * f_ref[...]
        first_half = (_lane() % HEAD_DIM) < HEAD_DIM // 2
        cos_ref[...] = jnp.cos(ang)
        sn = jnp.sin(ang)
        sin_ref[...] = jnp.where(first_half, -sn, sn)

    return pl.pallas_call(
        body, name=name, out_shape=[jax.ShapeDtypeStruct((s, LANES), f32)] * 2, compiler_params=_params(),
    )(pos_col, inv_freq)


def _swap_halves(v):
    first_half = (_lane() % HEAD_DIM) < HEAD_DIM // 2
    return jnp.where(first_half, pltpu.roll(v, LANES - HEAD_DIM // 2, axis=1), pltpu.roll(v, HEAD_DIM // 2, axis=1))


def _prenorm_proj_qkv(x, g, mod_scale, mod_shift, w_qkv_t, cos, sin_s, name):
    s, d = x.shape
    tm = _row_tile(s, 512)
    scale = 1.0 / math.sqrt(HEAD_DIM)

    def body(x_ref, g_ref, msc_ref, msh_ref, w_ref, c_ref, s_ref, h_ref, qa_ref, ka_ref, va_ref, qb_ref, kb_ref, vb_ref):
        xv = x_ref[...]
        h = ((xv * _rstd(xv) * g_ref[...]) * (1.0 + msc_ref[...]) + msh_ref[...]).astype(bf16)
        h_ref[...] = h
        proj = lax.dot_general(h, w_ref[...], _NT, preferred_element_type=f32)
        cs, sn = c_ref[...], s_ref[...]
        low = _lane() < HEAD_DIM

        def blk(j):
            return proj[:, j * LANES:(j + 1) * LANES]

        def rope(v):
            return v * cs + _swap_halves(v) * sn

        def expand(v):
            other = pltpu.roll(v, HEAD_DIM, axis=1)
            return jnp.where(low, v, other), jnp.where(low, other, v)

        for j in range(N_PAIRS):
            qa_ref[:, j * LANES:(j + 1) * LANES] = (rope(blk(j)) * scale).astype(bf16)
            qb_ref[:, j * LANES:(j + 1) * LANES] = (blk(6 + j) * scale).astype(bf16)
            kb_ref[:, j * LANES:(j + 1) * LANES] = blk(10 + j).astype(bf16)
            vb_ref[:, j * LANES:(j + 1) * LANES] = blk(14 + j).astype(bf16)
        k0, k1 = expand(rope(blk(4)))
        v0, v1 = expand(blk(5))
        for j in range(N_PAIRS):
            ka_ref[:, j * LANES:(j + 1) * LANES] = (k0 if j < 2 else k1).astype(bf16)
            va_ref[:, j * LANES:(j + 1) * LANES] = (v0 if j < 2 else v1).astype(bf16)

    hw = N_PAIRS * LANES
    return pl.pallas_call(
        body, name=name, grid=(s // tm,),
        in_specs=[_row_spec(tm, d)] + [_vec_spec(d)] * 3
        + [pl.BlockSpec((QKV_W, d), lambda i: (0, 0)), _row_spec(tm, LANES), _row_spec(tm, LANES)],
        out_specs=[_row_spec(tm, d)] + [_row_spec(tm, hw)] * 6,
        out_shape=[jax.ShapeDtypeStruct((s, d), bf16)] + [jax.ShapeDtypeStruct((s, hw), bf16)] * 6, compiler_params=_params(1),
    )(x, g, mod_scale, mod_shift, w_qkv_t, cos, sin_s)


def _qkv_prep_bwd(dqa_t, dka, dva, dqb_t, dkb, dvb, cos, sin_s, name):
    s = dka.shape[0]
    tm = _row_tile(s, 256)
    scale = 1.0 / math.sqrt(HEAD_DIM)
    hw = N_PAIRS * LANES
    t_spec = pl.BlockSpec((hw, tm), lambda i: (0, i))

    def body(dqa_ref, dka_ref, dva_ref, dqb_ref, dkb_ref, dvb_ref, c_ref, s_ref, o_ref):
        cs, sn = c_ref[...], s_ref[...]
        low = _lane() < HEAD_DIM

        def blk(ref, j):
            return ref[:, j * LANES:(j + 1) * LANES].astype(f32)

        def blk_t(ref, j):
            return ref[j * LANES:(j + 1) * LANES, :].T

        def unrope(v):
            return v * cs + _swap_halves(v * sn)

        def fold(ref):
            a, b = blk(ref, 0) + blk(ref, 1), blk(ref, 2) + blk(ref, 3)
            kv0 = a + pltpu.roll(a, HEAD_DIM, axis=1)
            kv1 = b + pltpu.roll(b, HEAD_DIM, axis=1)
            return jnp.where(low, kv0, kv1)

        for j in range(N_PAIRS):
            o_ref[:, j * LANES:(j + 1) * LANES] = (unrope(blk_t(dqa_ref, j)) * scale).astype(bf16)
            o_ref[:, (6 + j) * LANES:(7 + j) * LANES] = (blk_t(dqb_ref, j) * scale).astype(bf16)
            o_ref[:, (10 + j) * LANES:(11 + j) * LANES] = blk(dkb_ref, j).astype(bf16)
            o_ref[:, (14 + j) * LANES:(15 + j) * LANES] = blk(dvb_ref, j).astype(bf16)
        o_ref[:, 4 * LANES:5 * LANES] = unrope(fold(dka_ref)).astype(bf16)
        o_ref[:, 5 * LANES:6 * LANES] = fold(dva_ref).astype(bf16)

    return pl.pallas_call(
        body, name=name, grid=(s // tm,),
        in_specs=[t_spec, _row_spec(tm, hw), _row_spec(tm, hw), t_spec, _row_spec(tm, hw), _row_spec(tm, hw)] + [_row_spec(tm, LANES)] * 2,
        out_specs=_row_spec(tm, QKV_W), out_shape=jax.ShapeDtypeStruct((s, QKV_W), bf16), compiler_params=_params(1),
    )(*_in_hbm(dqa_t, dka, dva, dqb_t, dkb, dvb, cos, sin_s))


def _cumsum_rows(v, reverse=False):
    n = v.shape[0]
    row = lax.broadcasted_iota(jnp.int32, v.shape, 0)
    sh = 1
    while sh < n:
        if reverse:
            v = v + jnp.where(row < n - sh, pltpu.roll(v, n - sh, axis=0), 0.0)
        else:
            v = v + jnp.where(row >= sh, pltpu.roll(v, sh, axis=0), 0.0)
        sh *= 2
    return v


def _log_sigmoid(z):
    return jnp.minimum(z, 0.0) - jnp.log1p(jnp.exp(-jnp.abs(z)))


def _forget_prep(h, w_f_t, bf_row, name):
    s = h.shape[0]

    def body(h_ref, w_ref, b_ref, f_ref, cb_ref):
        fl = lax.dot_general(h_ref[...], w_ref[...], _NT, preferred_element_type=f32)
        f_ref[...] = fl
        cum = _cumsum_rows(_log_sigmoid(fl + b_ref[...]))
        for hd in range(N_HEADS):
            cb_ref[:, hd * LANES:(hd + 1) * LANES] = jnp.broadcast_to(cum[:, hd:hd + 1], (s, LANES))

    return pl.pallas_call(
        body, name=name,
        out_shape=[jax.ShapeDtypeStruct((s, LANES), f32), jax.ShapeDtypeStruct((s, N_HEADS * LANES), f32)],
        compiler_params=_params(),
    )(h, w_f_t, bf_row)


def _forget_prep_bwd(rs, dcs, fl, bf_row, name):
    s = fl.shape[0]

    def body(r_ref, c_ref, f_ref, b_ref, df_ref, db_ref):
        eye = (lax.broadcasted_iota(jnp.int32, (N_HEADS, LANES), 0) == lax.broadcasted_iota(jnp.int32, (N_HEADS, LANES), 1)).astype(f32)
        dcum = lax.dot_general(r_ref[...], eye, _TN, precision=lax.Precision.HIGHEST, preferred_element_type=f32)
        for h in range(N_HEADS):
            dcum = dcum - jnp.where(_lane() == h, jnp.sum(c_ref[:, h * LANES:(h + 1) * LANES], axis=1, keepdims=True), 0.0)
        dlf = _cumsum_rows(dcum, reverse=True)
        z = f_ref[...] + b_ref[...]
        df = jnp.where(_lane() < N_HEADS, dlf * jax.nn.sigmoid(-z), 0.0)
        df_ref[...] = df.astype(bf16)
        db_ref[...] = jnp.zeros_like(db_ref)
        db_ref[0:1, :] = jnp.sum(df, axis=0, keepdims=True)

    return pl.pallas_call(
        body, name=name,
        out_shape=[jax.ShapeDtypeStruct((s, LANES), bf16), jax.ShapeDtypeStruct((8, LANES), f32)], compiler_params=_params(),
    )(rs, dcs, fl, bf_row)


def _tile_mask(n_keys, n_queries, off, window):
    shape = (n_keys, n_queries)
    d = lax.broadcasted_iota(jnp.int32, shape, 1) - lax.broadcasted_iota(jnp.int32, shape, 0) + off
    valid = d >= 0
    return jnp.logical_and(valid, d < window) if window else valid


def _wide(v, t):
    return jnp.concatenate([v] * (t // LANES), axis=1)


def _attn_fwd(q, k, v, name, *, cum_b=None, sink_rows=None, window=None, t=256):
    s = q.shape[0]
    t = _row_tile(s, t)
    fox, has_sink = cum_b is not None, sink_rows is not None
    assert not window or (window % LANES == 0 and LANES + window <= s)

    def body(*refs):
        q_ref, k_ref, v_ref = refs[:3]
        rest = list(refs[3:])
        cb_ref = rest.pop(0) if fox else None
        sink_ref = rest.pop(0) if has_sink else None
        o_ref, lse_ref = rest
        i = pl.program_id(1)
        low = _lane() < HEAD_DIM
        top = lax.broadcasted_iota(jnp.int32, (LANES, 1), 0) < HEAD_DIM
        q2 = q_ref[...]
        zero = jnp.zeros_like(q2)
        qms = (jnp.where(low, q2, zero), jnp.where(low, zero, q2))

        def tile(k0, n_keys, off, carry, masked, queries=slice(0, t)):
            nq = queries.stop - queries.start
            kblk, vblk = k_ref[pl.ds(k0, n_keys), :], v_ref[pl.ds(k0, n_keys), :]
            valid = _tile_mask(n_keys, nq, off, window) if masked else None
            ones = jnp.ones_like(vblk)
            vs = tuple(jnp.where(_lane() == L_ROW[h], ones, vblk) for h in range(2))

            def scores(h):
                return lax.dot_general(kblk, qms[h][queries], _NT, preferred_element_type=f32)

            def softmax(h, sc):
                m = carry[h][0]
                if fox:
                    sc = sc - _wide(cb_ref[pl.ds(k0, n_keys), h * LANES:(h + 1) * LANES], nq)
                if masked:
                    sc = jnp.where(valid, sc, NEG)
                m_new = jnp.maximum(m, jnp.max(sc, axis=0, keepdims=True))
                return m_new, jnp.exp(m - m_new), jnp.exp(sc - m_new).astype(bf16)

            def update(h, m_new, alpha, p):
                return m_new, alpha * carry[h][1] + lax.dot_general(vs[h], p, _TN, preferred_element_type=f32)

            if window:
                return tuple(update(h, *softmax(h, scores(h))) for h in range(2))
            scs = [scores(h) for h in range(2)]
            stats = [softmax(h, scs[h]) for h in range(2)]
            return tuple(update(h, *stats[h]) for h in range(2))

        def start(nq):
            if has_sink:
                row = lax.broadcasted_iota(jnp.int32, (LANES, nq), 0)
                return tuple((_wide(sink_ref[h:h + 1, :], nq), (row == L_ROW[h]).astype(f32)) for h in range(2))
            return tuple((jnp.full((1, nq), NEG, f32), jnp.zeros((LANES, nq), f32)) for h in range(2))

        def finish(carry, queries):
            (m0, a0), (m1, a1) = carry
            l0, l1 = a0[L_ROW[0]:L_ROW[0] + 1, :], a1[L_ROW[1]:L_ROW[1] + 1, :]
            o_t = jnp.where(top, a0 * (1.0 / l0), a1 * (1.0 / l1))
            o_ref[queries, :] = o_t.T.astype(bf16)
            lse_ref[0:1, queries] = m0 + jnp.log(l0)
            lse_ref[1:2, queries] = m1 + jnp.log(l1)

        if window:
            for c in range(t // LANES):
                queries = slice(c * LANES, (c + 1) * LANES)
                q0 = i * t + c * LANES
                k0 = pl.multiple_of(jnp.maximum(q0 - window, 0), LANES)
                finish(tile(k0, LANES + window, q0 - k0, start(LANES), True, queries), queries)
        else:
            carry = lax.fori_loop(0, i, lambda kb, c: tile(pl.multiple_of(kb * t, t), t, 0, c, False), start(t))
            half, k_own = t // 2, pl.multiple_of(i * t, t)
            carry = tile(k_own, half, 0, carry, True)
            finish(tuple((m[:, :half], a[:, :half]) for m, a in carry), slice(0, half))
            carry = tuple((m[:, half:], a[:, half:]) for m, a in carry)
            finish(tile(pl.multiple_of(k_own + half, half), half, 0, carry, True, slice(half, t)), slice(half, t))

    q_spec = pl.BlockSpec((t, LANES), lambda j, i: (i, j))
    kv_spec = pl.BlockSpec((s, LANES), lambda j, i: (0, j))
    in_specs, args = [q_spec, kv_spec, kv_spec], [q, k, v]
    if fox:
        in_specs += [pl.BlockSpec((s, 2 * LANES), lambda j, i: (0, j))]
        args += [cum_b]
    if has_sink:
        in_specs += [pl.BlockSpec((None, 2, LANES), lambda j, i: (j, 0, 0))]
        args += [sink_rows.reshape(N_PAIRS, 2, LANES)]
    return pl.pallas_call(
        body, name=name, grid=(N_PAIRS, s // t), in_specs=in_specs,
        out_specs=[q_spec, pl.BlockSpec((None, 2, t), lambda j, i: (j, 0, i))],
        out_shape=[jax.ShapeDtypeStruct((s, N_PAIRS * LANES), bf16), jax.ShapeDtypeStruct((N_PAIRS, 2, s), f32)],
        compiler_params=_params(2),
    )(*_in_hbm(*args))


def _branch_dgrad_delta(db, w, o, name, *, lse=None, sink_rows=None, after=None):
    s, hw = o.shape
    tm = _row_tile(s, 512)
    has_sink = sink_rows is not None
    extra = [] if after is None else [after]

    def body(*refs):
        db_ref, w_ref, o_ref = refs[:3]
        outs = refs[3 + (2 if has_sink else 0) + len(extra):]
        do_ref, dl_ref = outs[:2]
        if has_sink:
            lse_ref, sink_ref = refs[3:5]
            ds_ref = outs[2]

            @pl.when(pl.program_id(0) == 0)
            def _():
                ds_ref[...] = jnp.zeros_like(ds_ref)
        do = lax.dot_general(db_ref[...], w_ref[...], _NT, preferred_element_type=f32).astype(bf16)
        do_ref[...] = do
        for j in range(N_PAIRS):
            cols = slice(j * LANES, (j + 1) * LANES)
            prod_t = (do[:, cols].astype(f32) * o_ref[:, cols].astype(f32)).T
            for h in range(2):
                dl = jnp.sum(prod_t[h * HEAD_DIM:(h + 1) * HEAD_DIM, :], axis=0, keepdims=True)
                dl_ref[j, h:h + 1, :] = dl
                if has_sink:
                    r = 2 * j + h
                    p_sink = jnp.exp(sink_ref[r:r + 1, 0:1] - lse_ref[j, h:h + 1, :])
                    ds_ref[r:r + 1, :] += -jnp.sum(p_sink * dl, axis=1, keepdims=True)

    rows_spec = pl.BlockSpec((N_PAIRS, 2, tm), lambda i: (0, 0, i))
    in_specs = [_row_spec(tm, db.shape[1]), pl.BlockSpec(w.shape, lambda i: (0, 0)), _row_spec(tm, hw)]
    args = [db, w, o]
    out_specs = [_row_spec(tm, hw), rows_spec]
    out_shape = [jax.ShapeDtypeStruct((s, hw), bf16), jax.ShapeDtypeStruct((N_PAIRS, 2, s), f32)]
    if has_sink:
        in_specs += [rows_spec, _vec_spec(LANES, N_HEADS)]
        args += [lse, sink_rows]
        out_specs += [_vec_spec(LANES, N_HEADS)]
        out_shape += [jax.ShapeDtypeStruct((N_HEADS, LANES), f32)]
    return pl.pallas_call(
        body, name=name, grid=(s // tm,), in_specs=in_specs + [pl.BlockSpec(memory_space=pl.ANY)] * len(extra),
        out_specs=out_specs, out_shape=out_shape, compiler_params=_params(1),
    )(*args, *extra)


def _attn_bwd(q, k, v, do, lse, delta, name, *, cum_b=None, window=None, t=256):
    s = q.shape[0]
    t = _row_tile(s, t)
    nblk = s // t
    fox = cum_b is not None
    assert not window or (window % LANES == 0 and LANES + window <= s)

    def body(*refs):
        k_ref, v_ref, q_ref, do_ref, lse_ref, dl_ref = refs[:6]
        rest = list(refs[6:])
        cb_ref = rest.pop(0) if fox else None
        dq_ref, dk_ref, dv_ref = rest[:3]
        dcs_ref, rs_ref = (rest[3], rest[4]) if fox else (None, None)
        dk_acc, dv_acc = rest[-2:]
        b = pl.program_id(1)
        k0 = pl.multiple_of(b * t, t)

        @pl.when(b == 0)
        def _():
            dq_ref[...] = jnp.zeros_like(dq_ref)
            if fox:
                rs_ref[...] = jnp.zeros_like(rs_ref)

        dk_acc[...] = jnp.zeros_like(dk_acc)
        dv_acc[...] = jnp.zeros_like(dv_acc)
        if fox:
            dcs_ref[...] = jnp.zeros_like(dcs_ref)
        low = _lane() < HEAD_DIM
        top = lax.broadcasted_iota(jnp.int32, (LANES, 1), 0) < HEAD_DIM
        kblk, vblk = k_ref[...], v_ref[...]
        k_t = kblk.astype(f32).T.astype(bf16)
        cks = [_wide(cb_ref[pl.ds(k0, t), h * LANES:(h + 1) * LANES], t) for h in range(2)] if fox else None

        def tile(q0, n_queries, off, masked, keys=slice(0, t)):
            cols = pl.ds(q0, n_queries)
            q2, do2 = q_ref[cols, :], do_ref[cols, :]
            zero = jnp.zeros_like(q2)
            valid = _tile_mask(keys.stop - keys.start, n_queries, off, window) if masked else None
            dq_parts = []
            for h in range(2):
                qm = jnp.where(low, q2, zero) if h == 0 else jnp.where(low, zero, q2)
                dom = jnp.where(low, do2, zero) if h == 0 else jnp.where(low, zero, do2)
                sc = lax.dot_general(kblk[keys], qm, _NT, preferred_element_type=f32)
                if fox:
                    sc = sc - cks[h][keys, :n_queries]
                if masked:
                    sc = jnp.where(valid, sc, NEG)
                p = jnp.exp(sc - lse_ref[h:h + 1, cols])
                dp = lax.dot_general(vblk[keys], dom, _NT, preferred_element_type=f32)
                ds = p * (dp - dl_ref[h:h + 1, cols])
                pb, dsb = p.astype(bf16), ds.astype(bf16)
                dv_acc[keys, :] += jnp.dot(pb, dom, preferred_element_type=f32)
                dk_acc[keys, :] += jnp.dot(dsb, qm, preferred_element_type=f32)
                dq_parts.append(jnp.dot(k_t[:, keys], dsb, preferred_element_type=f32))
                if fox:
                    dcs_ref[keys, h * LANES:(h + 1) * LANES] += sum(ds[:, g * LANES:(g + 1) * LANES]
                                                                    for g in range(n_queries // LANES))
                    rs_ref[h:h + 1, cols] += jnp.sum(ds, axis=0, keepdims=True)
            dq_ref[:, cols] += jnp.where(top, dq_parts[0], dq_parts[1])

        def later_block(qb, carry):
            tile(pl.multiple_of(qb * t, t), t, 0, False)
            return carry

        if window:
            for c in range(t // LANES):
                first = b * t + c * LANES
                q0 = pl.multiple_of(jnp.minimum(first, s - (LANES + window)), LANES)
                tile(q0, LANES + window, q0 - first, True, slice(c * LANES, (c + 1) * LANES))
        else:
            half = t // 2
            tile(k0, half, 0, True, slice(0, half))
            tile(pl.multiple_of(k0 + half, half), half, half, True)
            lax.fori_loop(b + 1, nblk, later_block, 0)
        dk_ref[...] = dk_acc[...].astype(bf16)
        dv_ref[...] = dv_acc[...].astype(bf16)

    kv_spec = pl.BlockSpec((t, LANES), lambda j, b: (b, j))
    seq_spec = pl.BlockSpec((s, LANES), lambda j, b: (0, j))
    rows_spec = pl.BlockSpec((None, 2, s), lambda j, b: (j, 0, 0))
    hw = N_PAIRS * LANES
    in_specs, args = [kv_spec, kv_spec, seq_spec, seq_spec, rows_spec, rows_spec], [k, v, q, do, lse, delta]
    out_specs = [pl.BlockSpec((LANES, s), lambda j, b: (j, 0)), kv_spec, kv_spec]
    out_shape = [jax.ShapeDtypeStruct((hw, s), f32), jax.ShapeDtypeStruct((s, hw), bf16), jax.ShapeDtypeStruct((s, hw), bf16)]
    if fox:
        in_specs += [pl.BlockSpec((s, 2 * LANES), lambda j, b: (0, j))]
        args += [cum_b]
        out_specs += [pl.BlockSpec((t, 2 * LANES), lambda j, b: (b, j)), rows_spec]
        out_shape += [jax.ShapeDtypeStruct((s, N_HEADS * LANES), f32), jax.ShapeDtypeStruct((N_PAIRS, 2, s), f32)]
    return pl.pallas_call(
        body, name=name, grid=(N_PAIRS, nblk), in_specs=in_specs, out_specs=out_specs, out_shape=out_shape,
        scratch_shapes=[pltpu.VMEM((t, LANES), f32)] * 2, compiler_params=_params(2),
    )(*_in_hbm(*args))


def _branch_merge(o_a, o_b, w_a, w_b, gl, name):
    s, k = o_a.shape
    d = w_a.shape[1]
    tm = _row_tile(s, 1024)

    def body(oa_ref, ob_ref, wa_ref, wb_ref, g_ref, ba_ref, bb_ref, m_ref):
        ba = jnp.dot(oa_ref[...], wa_ref[...], preferred_element_type=f32)
        bb = jnp.dot(ob_ref[...], wb_ref[...], preferred_element_type=f32)
        g0, g1 = jax.nn.sigmoid(g_ref[:, :d].astype(f32)), jax.nn.sigmoid(g_ref[:, d:].astype(f32))
        ba_ref[...] = ba.astype(bf16)
        bb_ref[...] = bb.astype(bf16)
        m_ref[...] = (g0 * ba + g1 * bb).astype(bf16)

    whole = pl.BlockSpec((k, d), lambda i: (0, 0))
    return pl.pallas_call(
        body, name=name, grid=(s // tm,),
        in_specs=[_row_spec(tm, k), _row_spec(tm, k), whole, whole, _row_spec(tm, 2 * d)],
        out_specs=[_row_spec(tm, d)] * 3, out_shape=[jax.ShapeDtypeStruct((s, d), bf16)] * 3, compiler_params=_params(1),
    )(o_a, o_b, w_a, w_b, gl)


def _out_dgrad_merge_bwd(dy, w_out, ba, bb, gl, name):
    s, d = ba.shape
    tm = _row_tile(s, 512)

    def body(dy_ref, w_ref, a_ref, b_ref, g_ref, da_ref, db_ref, dg_ref):
        dmv = lax.dot_general(dy_ref[...], w_ref[...], _NT, preferred_element_type=f32)
        g0, g1 = jax.nn.sigmoid(g_ref[:, :d].astype(f32)), jax.nn.sigmoid(g_ref[:, d:].astype(f32))
        da_ref[...] = (dmv * g0).astype(bf16)
        db_ref[...] = (dmv * g1).astype(bf16)
        dg_ref[:, :d] = (dmv * a_ref[...].astype(f32) * (g0 * (1.0 - g0))).astype(bf16)
        dg_ref[:, d:] = (dmv * b_ref[...].astype(f32) * (g1 * (1.0 - g1))).astype(bf16)

    return pl.pallas_call(
        body, name=name, grid=(s // tm,),
        in_specs=[_row_spec(tm, dy.shape[1]), pl.BlockSpec(w_out.shape, lambda i: (0, 0))] + [_row_spec(tm, d)] * 2
        + [_row_spec(tm, 2 * d)],
        out_specs=[_row_spec(tm, d)] * 2 + [_row_spec(tm, 2 * d)],
        out_shape=[jax.ShapeDtypeStruct((s, d), bf16)] * 2 + [jax.ShapeDtypeStruct((s, 2 * d), bf16)],
        compiler_params=_params(1),
    )(dy, w_out, ba, bb, gl)


GLU_TILE = 256


def _ffn_in_swiglu(h, w_t, name):
    s, d = h.shape
    f = w_t.shape[0] // 2
    tm = _row_tile(s, 2048)
    tg = GLU_TILE
    nb = f // tg

    def body(h_ref, wg_ref, wu_ref, g_ref, u_ref, act_ref):
        hv = h_ref[...]
        g = lax.dot_general(hv, wg_ref[...], _NT, preferred_element_type=f32)
        u = lax.dot_general(hv, wu_ref[...], _NT, preferred_element_type=f32)
        g_ref[...] = g.astype(bf16)
        u_ref[...] = u.astype(bf16)
        act_ref[...] = (g * jax.nn.sigmoid(g) * u).astype(bf16)

    col = pl.BlockSpec((tm, tg), lambda i, j: (i, j))
    return pl.pallas_call(
        body, name=name, grid=(s // tm, nb),
        in_specs=[pl.BlockSpec((tm, d), lambda i, j: (i, 0)), pl.BlockSpec((tg, d), lambda i, j: (j, 0)),
                  pl.BlockSpec((tg, d), lambda i, j: (j + nb, 0))],
        out_specs=[col] * 3, out_shape=[jax.ShapeDtypeStruct((s, f), bf16)] * 3, compiler_params=_params(2),
    )(h, w_t, w_t)


def _ffn_out_dgrad_swiglu(dy, w_out, g, u, name):
    s, d = dy.shape
    f = g.shape[1]
    tm = _row_tile(s, 2048)
    tg = GLU_TILE

    def body(dy_ref, w_ref, g_ref, u_ref, dg_ref, du_ref):
        dv = lax.dot_general(dy_ref[...], w_ref[...], _NT, preferred_element_type=f32)
        gv, uv = g_ref[...].astype(f32), u_ref[...].astype(f32)
        sg = jax.nn.sigmoid(gv)
        dg_ref[...] = (dv * uv * (sg * (1.0 + gv * (1.0 - sg)))).astype(bf16)
        du_ref[...] = (dv * (gv * sg)).astype(bf16)

    col = pl.BlockSpec((tm, tg), lambda i, j: (i, j))
    return pl.pallas_call(
        body, name=name, grid=(s // tm, f // tg),
        in_specs=[pl.BlockSpec((tm, d), lambda i, j: (i, 0)), pl.BlockSpec((tg, d), lambda i, j: (j, 0)), col, col],
        out_specs=[col] * 2, out_shape=[jax.ShapeDtypeStruct((s, f), bf16)] * 2, compiler_params=_params(2),
    )(dy, w_out, g, u)


def _wgrad_stack(parts, h, name):
    s, m = parts[0].shape
    d = h.shape[1]
    tm = 256
    nb = m // tm
    n = len(parts)

    def body(*refs):
        i = pl.program_id(0)
        for p in range(n):
            @pl.when(i // nb == p)
            def _(p=p):
                refs[n + 1][...] = lax.dot_general(refs[p][...], refs[n][...], _TN, preferred_element_type=f32).astype(bf16)

    a_specs = [pl.BlockSpec((s, tm), lambda i, p=p: (0, jnp.clip(i - p * nb, 0, nb - 1))) for p in range(n)]
    return pl.pallas_call(
        body, name=name, grid=(n * nb,), in_specs=a_specs + [pl.BlockSpec((s, d), lambda i: (0, 0))],
        out_specs=pl.BlockSpec((tm, d), lambda i: (i, 0)),
        out_shape=jax.ShapeDtypeStruct((n * m, d), bf16), compiler_params=_params(1),
    )(*parts, h)


def _ada_wgrad(c_all, d_all, name):
    n, d = c_all.shape
    w = d_all.shape[1]

    def body(c_ref, d_ref, o_ref):
        eye = (lax.broadcasted_iota(jnp.int32, (n, n), 0) == lax.broadcasted_iota(jnp.int32, (n, n), 1)).astype(f32)
        ct = lax.dot_general(c_ref[...], eye, _TN, precision=lax.Precision.HIGHEST, preferred_element_type=f32)
        g = ct[:, 0:1] * d_ref[0:1, :]
        for bi in range(1, n):
            g = g + ct[:, bi:bi + 1] * d_ref[bi:bi + 1, :]
        o_ref[0] = g

    return pl.pallas_call(
        body, name=name, out_shape=jax.ShapeDtypeStruct((1, d, w), f32), compiler_params=_params(),
    )(c_all, d_all)


def _adamw(parts, w, m, v, name, mine=None, me=None):
    r, c = w.shape
    n_parts = parts.shape[0]
    row_tiles = [t for t in range(min(r, 256), 0, -1) if r % t == 0 and (t % 16 == 0 or t == r)]
    if row_tiles:
        tr, tc = row_tiles[0], c
    else:
        tr, tc = r, next(t for t in (256, LANES) if c % t == 0)

    def body(*refs):
        w_ref, m_ref, v_ref, g_ref, d_ref, nm_ref, nv_ref = refs[-7:]
        if mine is None:
            p_ref, = refs[:-7]
        else:
            me_ref, p_ref, own_ref = refs[:-7]

        def part(i):
            if mine is None:
                return p_ref[i].astype(f32)
            return jnp.where(me_ref[0] == i, own_ref[...], p_ref[i]).astype(f32)

        g = part(0)
        for i in range(1, n_parts):
            g = g + part(i)
        mm = ADAM_B1 * m_ref[...] + (1.0 - ADAM_B1) * g
        vv = ADAM_B2 * v_ref[...] + (1.0 - ADAM_B2) * (g * g)
        m_hat = mm / (1.0 - ADAM_B1 ** ADAM_STEP)
        v_hat = vv / (1.0 - ADAM_B2 ** ADAM_STEP)
        g_ref[...] = g
        d_ref[...] = -ADAM_LR * (m_hat / (jnp.sqrt(v_hat) + ADAM_EPS) + ADAM_WD * w_ref[...])
        nm_ref[...] = mm
        nv_ref[...] = vv

    out_shape = [jax.ShapeDtypeStruct((r, c), f32)] * 4
    if mine is None:
        spec = pl.BlockSpec((tr, tc), lambda i, j: (i, j))
        return pl.pallas_call(
            body, name=name, grid=(r // tr, c // tc),
            in_specs=[pl.BlockSpec((n_parts, tr, tc), lambda i, j: (0, i, j))] + [spec] * 3,
            out_specs=[spec] * 4, out_shape=out_shape, compiler_params=_params(2),
        )(*_in_hbm(parts, w, m, v))
    spec = pl.BlockSpec((tr, tc), lambda i, j, me_ref: (i, j))
    return pl.pallas_call(
        body, name=name, out_shape=out_shape, compiler_params=_params(2),
        grid_spec=pltpu.PrefetchScalarGridSpec(
            num_scalar_prefetch=1, grid=(r // tr, c // tc),
            in_specs=[pl.BlockSpec((n_parts, tr, tc), lambda i, j, me_ref: (0, i, j)),
                      pl.BlockSpec((None, tr, tc), lambda i, j, me_ref: (me_ref[0], i, j))] + [spec] * 3,
            out_specs=[spec] * 4),
    )(me, *_in_hbm(parts, mine, w, m, v))


def _me():
    return lax.axis_index("x"), lax.axis_index("y"), lax.axis_index("c")


def _gather_prologue(c, w_ada, b_mine, w_in_t, name):
    n_dev, d = N_DEV, c.shape[1]
    ada_w = w_ada.shape[1]

    def body(c_ref, w_ref, b_ref, win_ref, call_ref, ada_ref, gin_ref, cols_ref, send_sems, recv_sems, local_sems):
        x, y, cc = _me()
        me, sibling = (x, y, cc), (x, y, 1 - cc)
        chips = [(1 - x, y), (x, 1 - y), (1 - x, 1 - y)]
        outs = (call_ref, ada_ref, gin_ref)

        def rows(a, dev):
            return outs[a].at[4 * dev[0] + 2 * dev[1] + dev[2]]

        def copy(a, k, block, to, src=None):
            return pltpu.make_async_remote_copy(
                src_ref=rows(a, block) if src is None else src, dst_ref=rows(a, block),
                send_sem=send_sems.at[a, k], recv_sem=recv_sems.at[a, k], device_id=to, device_id_type=MESH)

        def begin(a, src):
            own = pltpu.make_async_copy(src, rows(a, me), local_sems.at[a])
            sends = [copy(a, 0, me, sibling, src=src)] + [copy(a, 1 + j, me, (*chip, cc), src=src) for j, chip in enumerate(chips)]
            for cp in [own] + sends:
                cp.start()
            return own, sends

        def finish(a, own, sends):
            passed = []
            for j, chip in enumerate(chips):
                copy(a, 1 + j, (*chip, cc), me).wait_recv()
                passed.append(copy(a, 4 + j, (*chip, cc), sibling))
                passed[-1].start()
            copy(a, 0, sibling, me).wait_recv()
            for j, chip in enumerate(chips):
                copy(a, 4 + j, (*chip, 1 - cc), me).wait_recv()
            for cp in sends + passed:
                cp.wait_send()
            own.wait()

        finish(0, *begin(0, c_ref))
        cols_ref[...] = (jnp.dot(call_ref[:, 0, :].astype(bf16), w_ref[...].astype(bf16), preferred_element_type=f32)
                         + b_ref[...])
        finish(1, *begin(1, cols_ref))
        finish(2, *begin(2, win_ref))

    vmem, hbm = pl.BlockSpec(memory_space=pltpu.VMEM), pl.BlockSpec(memory_space=pl.ANY)
    return pl.pallas_call(
        body, name=name, in_specs=[vmem, vmem, vmem, hbm], out_specs=[vmem, vmem, hbm],
        out_shape=[jax.ShapeDtypeStruct((n_dev, 1, d), f32), jax.ShapeDtypeStruct((n_dev, n_dev, ada_w), f32),
                   jax.ShapeDtypeStruct((n_dev,) + w_in_t.shape, w_in_t.dtype)],
        scratch_shapes=[pltpu.VMEM((n_dev, ada_w), f32), pltpu.SemaphoreType.DMA((3, 7)), pltpu.SemaphoreType.DMA((3, 7)),
                        pltpu.SemaphoreType.DMA((3,))],
        compiler_params=pltpu.CompilerParams(vmem_limit_bytes=VMEM_LIMIT),
    )(c, w_ada, b_mine, w_in_t)


_FLIPS = ((0, 0, 1), (1, 0, 0), (0, 1, 0), (1, 1, 0), (1, 0, 1), (0, 1, 1), (1, 1, 1))
_HBM = pl.BlockSpec(memory_space=pltpu.HBM)
_SEM = pl.BlockSpec(memory_space=pltpu.SEMAPHORE)


def _exchange_copies(scatter, srcs, lands, send_sems, recv_sems):
    x, y, c = _me()
    me_row = 4 * x + 2 * y + c
    out = []
    for k, (fx, fy, fc) in enumerate(_FLIPS):
        peer = (x ^ fx, y ^ fy, c ^ fc)
        peer_row = 4 * peer[0] + 2 * peer[1] + peer[2]
        for a in range(len(srcs)):
            out.append(pltpu.make_async_remote_copy(
                src_ref=srcs[a].at[peer_row] if scatter else srcs[a], dst_ref=lands[a].at[me_row],
                send_sem=send_sems.at[7 * a + k], recv_sem=recv_sems.at[7 * a + k], device_id=peer, device_id_type=MESH))
    return out


def _exchange_start(arrays, scatter, name, after=None):
    n = len(arrays)
    lands = [lax.empty(a.shape if scatter else (N_DEV,) + a.shape, a.dtype) for a in arrays]
    extra = [] if after is None else [after]

    def body(*refs):
        srcs, zones = refs[:n], refs[n:2 * n]
        send_sems, recv_sems = refs[2 * n + len(extra)], refs[2 * n + len(extra) + 1]
        token = refs[-1]
        for cp in _exchange_copies(scatter, srcs, zones, send_sems, recv_sems):
            cp.start()
        token[...] = jnp.zeros_like(token)

    thru = [pltpu.HBM(a.shape, a.dtype) for a in list(arrays) + lands]
    outs = pl.pallas_call(
        body, name=name,
        out_shape=(pltpu.SemaphoreType.DMA((7 * n,)), pltpu.SemaphoreType.DMA((7 * n,)), *thru, jax.ShapeDtypeStruct((8, LANES), f32)),
        in_specs=[_HBM] * (2 * n) + [pl.BlockSpec(memory_space=pl.ANY)] * len(extra),
        out_specs=(_SEM, _SEM, *[_HBM] * (2 * n), pl.BlockSpec(memory_space=pltpu.VMEM)),
        input_output_aliases={i: 2 + i for i in range(2 * n)},
        compiler_params=pltpu.CompilerParams(has_side_effects=pltpu.SideEffectType.DATAFLOW_SIDE_EFFECTING),
    )(*[pltpu.with_memory_space_constraint(a, pltpu.HBM) for a in list(arrays) + lands], *extra)
    return dict(n=n, scatter=scatter, sems=outs[:2], srcs=outs[2:2 + n], lands=outs[2 + n:2 + 2 * n], token=outs[-1])


def _exchange_wait(handle, after, name):
    n, scatter = handle["n"], handle["scatter"]

    def body(*refs):
        srcs, zones = refs[:n], refs[n:2 * n]
        send_sems, recv_sems = refs[2 * n], refs[2 * n + 1]
        for cp in _exchange_copies(scatter, srcs, zones, send_sems, recv_sems):
            cp.wait_send()
            cp.wait_recv()

    thru = [pltpu.HBM(a.shape, a.dtype) for a in list(handle["srcs"]) + list(handle["lands"])]
    outs = pl.pallas_call(
        body, name=name, out_shape=tuple(thru),
        in_specs=[_HBM] * (2 * n) + [_SEM, _SEM, pl.BlockSpec(memory_space=pl.ANY)], out_specs=tuple([_HBM] * (2 * n)),
        input_output_aliases={i: i for i in range(2 * n)},
        compiler_params=pltpu.CompilerParams(has_side_effects=pltpu.SideEffectType.DATAFLOW_SIDE_EFFECTING),
    )(*handle["srcs"], *handle["lands"], *handle["sems"], after)
    return list(outs[:n]), list(outs[n:])


def _cols_from_shards(g):
    return jnp.transpose(g, (1, 0, 2)).reshape(g.shape[1], -1)


def _shards_from_cols(a):
    return jnp.transpose(a.reshape(a.shape[0], N_DEV, -1), (1, 0, 2))


def _local_step(x, positions, ada, g_pre_mix, g_post_mix, b_f, sinks, g_pre_ffn, g_post_ffn, target,
                w_in_t, mix_weights, ffn_weights, on_grads):
    s, d = x.shape
    row = lambda v: v.reshape(1, -1)
    shift_m, scale_m, gate_m, shift_f, scale_f, gate_f = (ada[i:i + 1] for i in range(6))
    w_gate_t, w_qkv_t = w_in_t[F_OFF + N_HEADS:], w_in_t[:QKV_W]
    w_f_t = jnp.pad(w_in_t[F_OFF:F_OFF + N_HEADS], ((0, LANES - N_HEADS), (0, 0)))
    bf_row = jnp.pad(row(b_f), ((0, 0), (0, LANES - N_HEADS)))
    sink_rows = jnp.broadcast_to(sinks.reshape(N_HEADS, 1).astype(f32), (N_HEADS, LANES))
    inv_freq = 1.0 / (ROPE_THETA ** (jnp.arange(0, HEAD_DIM, 2, dtype=f32) / HEAD_DIM))
    cos, sin_s = _rope_tables(positions.reshape(s, 1), jnp.tile(inv_freq, 4).reshape(1, LANES), "rope_tables")

    h1, qa, ka, va, qb, kb, vb = _prenorm_proj_qkv(x, row(g_pre_mix), scale_m, shift_m, w_qkv_t, cos, sin_s, "prenorm_proj_qkv")
    gl = _matmul(h1, w_gate_t, "nt", bf16, "proj_gate")
    fl, cum_b = _forget_prep(h1, w_f_t, bf_row, "proj_forget_prep")
    o_a, lse_a = _attn_fwd(qa, ka, va, "swa_fwd", sink_rows=sink_rows, window=WINDOW, t=2048)
    o_b, lse_b = _attn_fwd(qb, kb, vb, "fox_fwd", cum_b=cum_b, t=1024)
    everything_before = (gl[:8, :LANES] + o_a[:8, :LANES] + o_b[:8, :LANES]).astype(f32)
    w_branch_a, w_branch_b, w_out = mix_weights(everything_before)
    ba, bb, merged = _branch_merge(o_a, o_b, w_branch_a, w_branch_b, gl, "branch_merge")
    y1, x2, h2 = _out_proj_postnorm_prenorm(merged, w_out, x, row(g_post_mix), gate_m, row(g_pre_ffn), scale_f, shift_f,
                                            "out_proj_norms")

    w_ffn_in_t, w_ffn_out = ffn_weights(h2)
    g_ff, u_ff, act = _ffn_in_swiglu(h2, w_ffn_in_t, "ffn_in_swiglu")
    loss_row, d_out, d_y2, vec_pf = _out_proj_loss_tail(act, w_ffn_out, x2, row(g_post_ffn), gate_f, target, "ffn_out_loss_tail")

    g_w_ffn_out = _matmul(act, d_y2, "tn", bf16, "ffn_out_wgrad")
    dg_ff, du_ff = _ffn_out_dgrad_swiglu(d_y2, w_ffn_out, g_ff, u_ff, "ffn_out_dgrad_swiglu")
    g_w_ffn_in_t = _wgrad_stack([dg_ff, du_ff], h2, "ffn_in_wgrad")
    sent = on_grads(dict(w_ffn_in=g_w_ffn_in_t, w_ffn_out=g_w_ffn_out))
    d_x2, vec_nf, d_y1, vec_pm = _dgrad_prenorm_bwd(
        [(dg_ff, w_ffn_in_t, 0), (du_ff, w_ffn_in_t, 1)], x2, row(g_pre_ffn), scale_f, d_out, "ffn_in_dgrad_norms_bwd",
        after=sent, below=(y1, row(g_post_mix), gate_m))

    g_w_out = _matmul(merged, d_y1, "tn", bf16, "out_proj_wgrad")
    d_ba, d_bb, dgl = _out_dgrad_merge_bwd(d_y1, w_out, ba, bb, gl, "out_proj_dgrad_merge_bwd")
    g_w_branch_a = _matmul(o_a, d_ba, "tn", bf16, "branch_a_wgrad")
    g_w_branch_b = _matmul(o_b, d_bb, "tn", bf16, "branch_b_wgrad")
    sent = on_grads(dict(w_out=g_w_out, w_branch_a=g_w_branch_a, w_branch_b=g_w_branch_b))
    d_oa, delta_a, d_sink = _branch_dgrad_delta(d_ba, w_branch_a, o_a, "branch_a_dgrad_delta", lse=lse_a,
                                                sink_rows=sink_rows, after=sent)
    d_ob, delta_b = _branch_dgrad_delta(d_bb, w_branch_b, o_b, "branch_b_dgrad_delta", after=sent)
    dqa_t, dka, dva = _attn_bwd(qa, ka, va, d_oa, lse_a, delta_a, "swa_bwd", window=WINDOW, t=2048)
    dqb_t, dkb, dvb, dcs, rs = _attn_bwd(qb, kb, vb, d_ob, lse_b, delta_b, "fox_bwd", cum_b=cum_b, t=512)
    dqkv = _qkv_prep_bwd(dqa_t, dka, dva, dqb_t, dkb, dvb, cos, sin_s, "qkv_prep_bwd")
    dfl, vec_bf = _forget_prep_bwd(rs.reshape(N_HEADS, s), dcs, fl, bf_row, "forget_prep_bwd")
    g_w_in_t = jnp.concatenate([_matmul(dqkv, h1, "tn", bf16, "qkv_wgrad"), _matmul(dfl, h1, "tn", bf16, "forget_wgrad")[:N_HEADS],
                                _matmul(dgl, h1, "tn", bf16, "gate_wgrad")], axis=0)
    sent = on_grads(dict(w_in=g_w_in_t))
    grad_x, vec_nm = _dgrad_prenorm_bwd([(dgl, w_gate_t, 0), (dqkv, w_qkv_t, 0), (dfl, w_f_t, 0)], x, row(g_pre_mix),
                                        scale_m, d_x2, "in_proj_dgrad_prenorm_bwd", after=sent)

    d_ada = jnp.concatenate([vec_nm[0], vec_nm[1], vec_pm[0], vec_nf[0], vec_nf[1], vec_pf[0]])
    small = dict(b_ada=d_ada, g_pre_mix=vec_nm[2], g_post_mix=vec_pm[1], g_pre_ffn=vec_nf[2], g_post_ffn=vec_pf[1],
                 b_f=vec_bf[0, :N_HEADS], sinks=d_sink[:, 0], loss=loss_row[0, :1])
    return grad_x, small


_SMALL = (("b_ada", 6144), ("g_pre_mix", 1024), ("g_post_mix", 1024), ("g_pre_ffn", 1024), ("g_post_ffn", 1024),
          ("b_f", 128), ("sinks", 128), ("loss", 128))
_SMALL_ROWS = 88


def _pack_small(vals):
    parts = [jnp.pad(vals[k].reshape(-1).astype(f32), (0, n - vals[k].size)) for k, n in _SMALL]
    flat = jnp.concatenate(parts)
    return jnp.pad(flat, (0, _SMALL_ROWS * LANES - flat.size)).reshape(_SMALL_ROWS, LANES)


def _unpack_small(slab, shapes):
    flat, out, off = slab.reshape(-1), {}, 0
    for k, n in _SMALL:
        size = math.prod(shapes[k])
        out[k] = flat[off:off + size].reshape(shapes[k])
        off += n
    return out


def kernel(x, c, positions, w_ada, b_ada, g_pre_mix, g_post_mix, w_in, b_f, sinks, w_branch_a, w_branch_b, w_out, g_pre_ffn, g_post_ffn, w_ffn_in, w_ffn_out, loss_target, m_w_ada, m_b_ada, m_g_pre_mix, m_g_post_mix, m_w_in, m_b_f, m_sinks, m_w_branch_a, m_w_branch_b, m_w_out, m_g_pre_ffn, m_g_post_ffn, m_w_ffn_in, m_w_ffn_out, v_w_ada, v_b_ada, v_g_pre_mix, v_g_post_mix, v_w_in, v_b_f, v_sinks, v_w_branch_a, v_w_branch_b, v_w_out, v_g_pre_ffn, v_g_post_ffn, v_w_ffn_in, v_w_ffn_out):
    xi, yi, ci = _me()
    me = 4 * xi + 2 * yi + ci
    d = D_MODEL
    ada_w = w_ada.shape[2]

    transposed = ("w_in", "w_ffn_in")
    tr = lambda a: jnp.transpose(a[0])

    b_mine = lax.dynamic_slice(b_ada, (0, me * ada_w), (1, ada_w))
    c_all, ada_all, g_in = _gather_prologue(c, w_ada[0], b_mine, tr(w_in).astype(bf16), "gather_prologue")
    c_all = c_all.reshape(N_DEV, d)
    ada = lax.dynamic_index_in_dim(ada_all, me, axis=1, keepdims=False).reshape(6, d)
    late_mix = [w.astype(bf16) for w in (w_branch_a[0], w_branch_b[0], w_out[0])]
    late_ffn = [w.astype(bf16) for w in (tr(w_ffn_in), w_ffn_out[0])]
    mix_h = _exchange_start(late_mix, False, "gather_mix_start", after=g_in)
    ffn_h = _exchange_start(late_ffn, False, "gather_ffn_start", after=mix_h["token"])

    def mine_into(zone, block):
        return lax.dynamic_update_index_in_dim(zone, block, me, 0)

    def rows_from_shards(g):
        return g.reshape(g.shape[0] * g.shape[1], g.shape[2])

    def mix_weights(after):
        sent, zones = _exchange_wait(mix_h, after, "gather_mix_wait")
        g_ba, g_bb, g_out = (mine_into(z, w) for z, w in zip(zones, sent))
        return _cols_from_shards(g_ba), _cols_from_shards(g_bb), rows_from_shards(g_out)

    def ffn_weights(after):
        sent, zones = _exchange_wait(ffn_h, after, "gather_ffn_wait")
        g_fi, g_fo = (mine_into(z, w) for z, w in zip(zones, sent))
        return rows_from_shards(g_fi), rows_from_shards(g_fo)

    row_sharded = ("w_out", "w_ffn_out") + transposed
    in_flight = []

    def on_grads(group):
        sends = [g.reshape(N_DEV, g.shape[0] // N_DEV, g.shape[1]) if nm in row_sharded else _shards_from_cols(g)
                 for nm, g in group.items()]
        handle = _exchange_start(sends, True, "scatter_start_%d" % len(in_flight))
        in_flight.append((list(group), handle))
        return handle["token"]

    grad_x, small = _local_step(
        x[0], positions[0], ada + ffn_h["token"][0, 0], g_pre_mix[0], g_post_mix[0], b_f[0], sinks[0], g_pre_ffn[0],
        g_post_ffn[0], loss_target[0], rows_from_shards(g_in), mix_weights, ffn_weights, on_grads)

    ws = dict(w_in=(w_in, m_w_in, v_w_in), w_branch_a=(w_branch_a, m_w_branch_a, v_w_branch_a),
              w_branch_b=(w_branch_b, m_w_branch_b, v_w_branch_b), w_out=(w_out, m_w_out, v_w_out),
              w_ffn_in=(w_ffn_in, m_w_ffn_in, v_w_ffn_in), w_ffn_out=(w_ffn_out, m_w_ffn_out, v_w_ffn_out))
    res = {}

    def finish_group(gi, after):
        names, handle = in_flight[gi]
        sends, zones = _exchange_wait(handle, after, "scatter_wait_%d" % gi)
        for nm, zone, sent in zip(names, zones, sends):
            w, m, v = (tr(a) if nm in transposed else a[0] for a in ws[nm])
            out = _adamw(zone, w, m, v, "adamw_" + nm, mine=sent, me=me.reshape(1).astype(jnp.int32))
            after = out[0]
            res[nm] = [jnp.transpose(o) for o in out] if nm in transposed else out
        return after

    small_h = _exchange_start([_pack_small(small)], False, "gather_small_start", after=grad_x)
    done = finish_group(1, finish_group(0, small_h["token"]))
    (slab_mine,), (slab_zone,) = _exchange_wait(small_h, done, "gather_small_wait")
    slab_all = mine_into(slab_zone, slab_mine)
    small_w = dict(b_ada=b_ada, g_pre_mix=g_pre_mix, g_post_mix=g_post_mix, g_pre_ffn=g_pre_ffn, g_post_ffn=g_post_ffn,
                   b_f=b_f, sinks=sinks, loss=jnp.zeros((1,), f32))
    small_m = dict(b_ada=m_b_ada, g_pre_mix=m_g_pre_mix, g_post_mix=m_g_post_mix, g_pre_ffn=m_g_pre_ffn,
                   g_post_ffn=m_g_post_ffn, b_f=m_b_f, sinks=m_sinks, loss=jnp.zeros((1,), f32))
    small_v = dict(b_ada=v_b_ada, g_pre_mix=v_g_pre_mix, g_post_mix=v_g_post_mix, g_pre_ffn=v_g_pre_ffn,
                   g_post_ffn=v_g_post_ffn, b_f=v_b_f, sinks=v_sinks, loss=jnp.ones((1,), f32))
    shapes = {k: small_w[k].shape for k, _ in _SMALL}
    s_out = _adamw(slab_all, _pack_small(small_w), _pack_small(small_m), _pack_small(small_v), "adamw_small")
    s_grad, s_delta, s_m, s_v = (_unpack_small(o, shapes) for o in s_out)

    d_ada_all = lax.dynamic_slice(slab_all[:, :6144 // LANES, :].reshape(N_DEV, 6144), (0, me * ada_w), (N_DEV, ada_w))
    ada_parts = _ada_wgrad(c_all, d_ada_all, "ada_wgrad")

    res["w_ada"] = _adamw(ada_parts, w_ada[0], m_w_ada[0], v_w_ada[0], "adamw_w_ada")
    finish_group(2, res["w_ada"][0])

    order = ["w_ada", "b_ada", "g_pre_mix", "g_post_mix", "w_in", "b_f", "sinks", "w_branch_a", "w_branch_b", "w_out",
             "g_pre_ffn", "g_post_ffn", "w_ffn_in", "w_ffn_out"]
    outs = [s_grad["loss"].reshape(()), grad_x[None]]
    for which, small_o in enumerate((s_grad, s_delta, s_m, s_v)):
        for nm in order:
            outs.append(res[nm][which][None] if nm in res else small_o[nm])
    return tuple(outs)
```

```python
import math

import jax
import jax.numpy as jnp
from jax import lax
from jax.experimental import pallas as pl
from jax.experimental.pallas import tpu as pltpu

f32 = jnp.float32
bf16 = jnp.bfloat16

D_MODEL = 1024
HEAD_DIM = 64
N_HEADS = 8
N_PAIRS = 4
QKV_W = 2304
F_OFF = 2304
WINDOW = 128
ROPE_THETA = 10000.0
RMS_EPS = 1e-6
N_DEV = 8
ADAM_LR, ADAM_B1, ADAM_B2, ADAM_EPS, ADAM_WD, ADAM_STEP = 0.001, 0.9, 0.999, 1e-08, 0.01, 10
NEG = -1e30
L_ROW = (HEAD_DIM, 0)
LANES = 128
VMEM_LIMIT = 48 * 1024 * 1024
MESH = pl.DeviceIdType.MESH

_NT = (((1,), (1,)), ((), ()))
_TN = (((0,), (0,)), ((), ()))


def _params(n_grid=0):
    sem = ("arbitrary",) * n_grid if n_grid else None
    return pltpu.CompilerParams(dimension_semantics=sem, vmem_limit_bytes=VMEM_LIMIT)


def _row_tile(s, want):
    t = min(s, want)
    assert s % t == 0, (s, t)
    return t


MATMUL_VMEM_BUDGET = 40 * 1024 * 1024


def _matmul_tiles(m, n, k, a_item, b_item, o_item):
    def tiles(d):
        return [t for t in range(LANES, min(d, 2048) + 1, LANES) if d % t == 0] or [d]

    best = None
    for tm in tiles(m):
        for tn in tiles(n):
            vmem = 2 * (tm * k * a_item + tn * k * b_item + tm * tn * o_item) + tm * tn * 4
            if vmem > MATMUL_VMEM_BUDGET:
                continue
            traffic = m * k * a_item + n * k * b_item * (1 if tn == n else m // tm) + m * n * o_item
            steps = (m // tm) * (n // tn)
            key = (traffic, 0, steps) if steps >= 4 else (traffic, 1, -steps)
            if best is None or key < best[0]:
                best = (key, tm, tn)
    assert best is not None, (m, n, k)
    return best[1], best[2]


def _matmul(a, b, mode, out_dtype, name, after=None):
    if mode == "nn":
        (m, k), n = a.shape, b.shape[1]
    elif mode == "nt":
        (m, k), n = a.shape, b.shape[0]
    else:
        (k, m), n = a.shape, b.shape[1]
    tm, tn = _matmul_tiles(m, n, k, a.dtype.itemsize, b.dtype.itemsize, jnp.dtype(out_dtype).itemsize)
    if mode == "nn":
        a_spec, b_spec, dims = pl.BlockSpec((tm, k), lambda i, j: (i, 0)), pl.BlockSpec((k, tn), lambda i, j: (0, j)), None
    elif mode == "nt":
        a_spec, b_spec, dims = pl.BlockSpec((tm, k), lambda i, j: (i, 0)), pl.BlockSpec((tn, k), lambda i, j: (j, 0)), _NT
    else:
        a_spec, b_spec, dims = pl.BlockSpec((k, tm), lambda i, j: (0, i)), pl.BlockSpec((k, tn), lambda i, j: (0, j)), _TN

    def body(a_ref, b_ref, *rest):
        o_ref = rest[-1]
        av, bv = a_ref[...].astype(bf16), b_ref[...].astype(bf16)
        if dims is None:
            r = jnp.dot(av, bv, preferred_element_type=f32)
        else:
            r = lax.dot_general(av, bv, dims, preferred_element_type=f32)
        o_ref[...] = r.astype(out_dtype)

    extra = [] if after is None else [after]
    return pl.pallas_call(
        body, name=name, grid=(m // tm, n // tn), in_specs=[a_spec, b_spec] + [pl.BlockSpec(memory_space=pl.ANY)] * len(extra),
        out_specs=pl.BlockSpec((tm, tn), lambda i, j: (i, j)),
        out_shape=jax.ShapeDtypeStruct((m, n), out_dtype), compiler_params=_params(2),
    )(a, b, *extra)


def _rstd(v):
    return lax.rsqrt(jnp.mean(v * v, axis=-1, keepdims=True) + RMS_EPS)


def _row_spec(tm, d):
    return pl.BlockSpec((tm, d), lambda i: (i, 0))


def _vec_spec(d, rows=1):
    return pl.BlockSpec((rows, d), lambda i: (0, 0))


def _proj_spec(a, w, tm):
    return [_row_spec(tm, a.shape[1]), pl.BlockSpec(w.shape, lambda i: (0, 0))]


def _out_proj_postnorm_prenorm(a, w, x, g_post, gate, g_pre, scale, shift, name):
    s, d = x.shape
    tm = _row_tile(s, 512)

    def body(a_ref, w_ref, x_ref, gp_ref, gate_ref, g_ref, sc_ref, sh_ref, y_ref, x2_ref, h_ref):
        yv = jnp.dot(a_ref[...], w_ref[...], preferred_element_type=f32)
        y_ref[...] = yv
        x2 = x_ref[...] + gate_ref[...] * (yv * _rstd(yv) * gp_ref[...])
        x2_ref[...] = x2
        h_ref[...] = ((x2 * _rstd(x2) * g_ref[...]) * (1.0 + sc_ref[...]) + sh_ref[...]).astype(bf16)

    return pl.pallas_call(
        body, name=name, grid=(s // tm,), in_specs=_proj_spec(a, w, tm) + [_row_spec(tm, d)] + [_vec_spec(d)] * 5,
        out_specs=[_row_spec(tm, d)] * 3,
        out_shape=[jax.ShapeDtypeStruct((s, d), f32)] * 2 + [jax.ShapeDtypeStruct((s, d), bf16)], compiler_params=_params(1),
    )(a, w, x, g_post, gate, g_pre, scale, shift)


def _rms_bwd(u, v, r):
    return r * u - v * (r * r * r) * jnp.mean(u * v, axis=-1, keepdims=True)


def _out_proj_loss_tail(a, w, x, g, gate, target, name):
    s, d = x.shape
    tm = _row_tile(s, 512)

    def body(a_ref, w_ref, x_ref, g_ref, gate_ref, t_ref, loss_ref, do_ref, dy_ref, vec_ref):
        @pl.when(pl.program_id(0) == 0)
        def _():
            loss_ref[...] = jnp.zeros_like(loss_ref)
            vec_ref[...] = jnp.zeros_like(vec_ref)
        yv = jnp.dot(a_ref[...], w_ref[...], preferred_element_type=f32)
        r = _rstd(yv)
        yn = yv * r
        err = x_ref[...] + gate_ref[...] * (yn * g_ref[...]) - t_ref[...]
        loss_ref[...] += 0.5 * jnp.sum(jnp.mean(err * err, axis=-1, keepdims=True), axis=0, keepdims=True)
        dr = err / d
        do_ref[...] = dr
        dn = dr * gate_ref[...]
        vec_ref[0:1, :] += jnp.sum(dr * (yn * g_ref[...]), axis=0, keepdims=True)
        vec_ref[1:2, :] += jnp.sum(dn * yn, axis=0, keepdims=True)
        dy_ref[...] = _rms_bwd(dn * g_ref[...], yv, r).astype(bf16)

    return pl.pallas_call(
        body, name=name, grid=(s // tm,),
        in_specs=_proj_spec(a, w, tm) + [_row_spec(tm, d)] + [_vec_spec(d)] * 2 + [_row_spec(tm, d)],
        out_specs=[_vec_spec(LANES), _row_spec(tm, d), _row_spec(tm, d), _vec_spec(d, 8)],
        out_shape=[jax.ShapeDtypeStruct((1, LANES), f32), jax.ShapeDtypeStruct((s, d), f32),
                   jax.ShapeDtypeStruct((s, d), bf16), jax.ShapeDtypeStruct((8, d), f32)],
        compiler_params=_params(1),
    )(a, w, x, g, gate, target)


def _dgrad_prenorm_bwd(terms, x, g, scale, dres, name, after=None, below=None):
    s, d = x.shape
    n = len(terms)
    k = sum(a.shape[1] for a, _, _ in terms)
    row_bytes = 2 * (2 * k) + d * (4 + 2 * 4 * 3 + (2 * 4 + 2 * 2 if below else 0))
    tm = next(t for t in (512, 256, 128) if s % t == 0 and 4 * k * d + t * row_bytes <= MATMUL_VMEM_BUDGET)
    extra = [] if after is None else [after]

    def body(*refs):
        a_refs, b_refs = refs[:n], refs[n:2 * n]
        x_ref, g_ref, sc_ref, dr_ref = refs[2 * n:2 * n + 4]
        n_in = 2 * n + 4 + (3 if below else 0) + len(extra)
        dx_ref, vec_ref = refs[n_in], refs[n_in + 1]
        if below:
            y_ref, gp_ref, gate_ref = refs[2 * n + 4:2 * n + 7]
            dy_ref, vec2_ref = refs[n_in + 2], refs[n_in + 3]

        @pl.when(pl.program_id(0) == 0)
        def _():
            vec_ref[...] = jnp.zeros_like(vec_ref)
            if below:
                vec2_ref[...] = jnp.zeros_like(vec2_ref)
        dhv = jnp.dot(a_refs[0][...], b_refs[0][...], preferred_element_type=f32)
        for i in range(1, n):
            dhv = dhv + jnp.dot(a_refs[i][...], b_refs[i][...], preferred_element_type=f32)
        xv = x_ref[...]
        r = _rstd(xv)
        xn = xv * r
        dn = dhv * (1.0 + sc_ref[...])
        vec_ref[0:1, :] += jnp.sum(dhv, axis=0, keepdims=True)
        vec_ref[1:2, :] += jnp.sum(dhv * (xn * g_ref[...]), axis=0, keepdims=True)
        vec_ref[2:3, :] += jnp.sum(dn * xn, axis=0, keepdims=True)
        dx = dr_ref[...] + _rms_bwd(dn * g_ref[...], xv, r)
        dx_ref[...] = dx
        if below:
            yv = y_ref[...]
            ry = _rstd(yv)
            yn = yv * ry
            dny = dx * gate_ref[...]
            vec2_ref[0:1, :] += jnp.sum(dx * (yn * gp_ref[...]), axis=0, keepdims=True)
            vec2_ref[1:2, :] += jnp.sum(dny * yn, axis=0, keepdims=True)
            dy_ref[...] = _rms_bwd(dny * gp_ref[...], yv, ry).astype(bf16)

    in_specs = ([_row_spec(tm, a.shape[1]) for a, _, _ in terms]
                + [pl.BlockSpec((a.shape[1], d), lambda i, r=r: (r, 0)) for a, _, r in terms]
                + [_row_spec(tm, d)] + [_vec_spec(d)] * 2 + [_row_spec(tm, d)])
    out_specs = [_row_spec(tm, d), _vec_spec(d, 8)]
    out_shape = [jax.ShapeDtypeStruct((s, d), f32), jax.ShapeDtypeStruct((8, d), f32)]
    args = [a for a, _, _ in terms] + [b for _, b, _ in terms] + [x, g, scale, dres]
    if below:
        in_specs += [_row_spec(tm, d)] + [_vec_spec(d)] * 2
        out_specs += [_row_spec(tm, d), _vec_spec(d, 8)]
        out_shape += [jax.ShapeDtypeStruct((s, d), bf16), jax.ShapeDtypeStruct((8, d), f32)]
        args += list(below)
    return pl.pallas_call(
        body, name=name, grid=(s // tm,), in_specs=in_specs + [pl.BlockSpec(memory_space=pl.ANY)] * len(extra),
        out_specs=out_specs, out_shape=out_shape, compiler_params=_params(1),
    )(*args, *extra)


def _lane():
    return lax.broadcasted_iota(jnp.int32, (1, LANES), 1)


def _rope_freq_row():
    inv_freq = 1.0 / (ROPE_THETA ** (jnp.arange(0, HEAD_DIM, 2, dtype=f32) / HEAD_DIM))
    return jnp.tile(inv_freq, 2 * LANES // HEAD_DIM).reshape(1, LANES)


def _rope_tables(pos_col, freq_row):
    ang = pos_col.astype(f32) * freq_row
    first_half = (_lane() % HEAD_DIM) < HEAD_DIM // 2
    sn = jnp.sin(ang)
    return jnp.cos(ang), jnp.where(first_half, -sn, sn)


def _swap_halves(v):
    first_half = (_lane() % HEAD_DIM) < HEAD_DIM // 2
    return jnp.where(first_half, pltpu.roll(v, LANES - HEAD_DIM // 2, axis=1), pltpu.roll(v, HEAD_DIM // 2, axis=1))


def _prenorm_proj_qkv(x, g, mod_scale, mod_shift, w_qkv_t, cos, sin_s, name):
    s, d = x.shape
    tm = _row_tile(s, 512)
    scale = 1.0 / math.sqrt(HEAD_DIM)

    def body(x_ref, g_ref, msc_ref, msh_ref, w_ref, c_ref, s_ref, h_ref, qa_ref, ka_ref, va_ref, qb_ref, kb_ref, vb_ref):
        xv = x_ref[...]
        h = ((xv * _rstd(xv) * g_ref[...]) * (1.0 + msc_ref[...]) + msh_ref[...]).astype(bf16)
        h_ref[...] = h
        proj = lax.dot_general(h, w_ref[...], _NT, preferred_element_type=f32)
        cs, sn = c_ref[...], s_ref[...]
        low = _lane() < HEAD_DIM

        def blk(j):
            return proj[:, j * LANES:(j + 1) * LANES]

        def rope(v):
            return v * cs + _swap_halves(v) * sn

        def expand(v):
            other = pltpu.roll(v, HEAD_DIM, axis=1)
            return jnp.where(low, v, other), jnp.where(low, other, v)

        for j in range(N_PAIRS):
            qa_ref[:, j * LANES:(j + 1) * LANES] = (rope(blk(j)) * scale).astype(bf16)
            qb_ref[:, j * LANES:(j + 1) * LANES] = (blk(6 + j) * scale).astype(bf16)
            kb_ref[:, j * LANES:(j + 1) * LANES] = blk(10 + j).astype(bf16)
            vb_ref[:, j * LANES:(j + 1) * LANES] = blk(14 + j).astype(bf16)
        k0, k1 = expand(rope(blk(4)))
        v0, v1 = expand(blk(5))
        for j in range(N_PAIRS):
            ka_ref[:, j * LANES:(j + 1) * LANES] = (k0 if j < 2 else k1).astype(bf16)
            va_ref[:, j * LANES:(j + 1) * LANES] = (v0 if j < 2 else v1).astype(bf16)

    hw = N_PAIRS * LANES
    return pl.pallas_call(
        body, name=name, grid=(s // tm,),
        in_specs=[_row_spec(tm, d)] + [_vec_spec(d)] * 3
        + [pl.BlockSpec((QKV_W, d), lambda i: (0, 0)), _row_spec(tm, LANES), _row_spec(tm, LANES)],
        out_specs=[_row_spec(tm, d)] + [_row_spec(tm, hw)] * 6,
        out_shape=[jax.ShapeDtypeStruct((s, d), bf16)] + [jax.ShapeDtypeStruct((s, hw), bf16)] * 6, compiler_params=_params(1),
    )(x, g, mod_scale, mod_shift, w_qkv_t, cos, sin_s)


def _qkv_prep_bwd(dqa_t, dka, dva, dqb_t, dkb, dvb, cos, sin_s, name):
    s = dka.shape[0]
    tm = _row_tile(s, 256)
    scale = 1.0 / math.sqrt(HEAD_DIM)
    hw = N_PAIRS * LANES
    t_spec = pl.BlockSpec((hw, tm), lambda i: (0, i))

    def body(dqa_ref, dka_ref, dva_ref, dqb_ref, dkb_ref, dvb_ref, c_ref, s_ref, o_ref):
        cs, sn = c_ref[...], s_ref[...]
        low = _lane() < HEAD_DIM

        def blk(ref, j):
            return ref[:, j * LANES:(j + 1) * LANES].astype(f32)

        def blk_t(ref, j):
            return ref[j * LANES:(j + 1) * LANES, :].T

        def unrope(v):
            return v * cs + _swap_halves(v * sn)

        def fold(ref):
            a, b = blk(ref, 0) + blk(ref, 1), blk(ref, 2) + blk(ref, 3)
            kv0 = a + pltpu.roll(a, HEAD_DIM, axis=1)
            kv1 = b + pltpu.roll(b, HEAD_DIM, axis=1)
            return jnp.where(low, kv0, kv1)

        for j in range(N_PAIRS):
            o_ref[:, j * LANES:(j + 1) * LANES] = (unrope(blk_t(dqa_ref, j)) * scale).astype(bf16)
            o_ref[:, (6 + j) * LANES:(7 + j) * LANES] = (blk_t(dqb_ref, j) * scale).astype(bf16)
            o_ref[:, (10 + j) * LANES:(11 + j) * LANES] = blk(dkb_ref, j).astype(bf16)
            o_ref[:, (14 + j) * LANES:(15 + j) * LANES] = blk(dvb_ref, j).astype(bf16)
        o_ref[:, 4 * LANES:5 * LANES] = unrope(fold(dka_ref)).astype(bf16)
        o_ref[:, 5 * LANES:6 * LANES] = fold(dva_ref).astype(bf16)

    return pl.pallas_call(
        body, name=name, grid=(s // tm,),
        in_specs=[t_spec, _row_spec(tm, hw), _row_spec(tm, hw), t_spec, _row_spec(tm, hw), _row_spec(tm, hw)] + [_row_spec(tm, LANES)] * 2,
        out_specs=_row_spec(tm, QKV_W), out_shape=jax.ShapeDtypeStruct((s, QKV_W), bf16), compiler_params=_params(1),
    )(dqa_t, dka, dva, dqb_t, dkb, dvb, cos, sin_s)


def _cumsum_rows(v, reverse=False):
    n = v.shape[0]
    row = lax.broadcasted_iota(jnp.int32, v.shape, 0)
    sh = 1
    while sh < n:
        if reverse:
            v = v + jnp.where(row < n - sh, pltpu.roll(v, n - sh, axis=0), 0.0)
        else:
            v = v + jnp.where(row >= sh, pltpu.roll(v, sh, axis=0), 0.0)
        sh *= 2
    return v


def _log_sigmoid(z):
    return jnp.minimum(z, 0.0) - jnp.log1p(jnp.exp(-jnp.abs(z)))


def _forget_prep(h, w_f_t, bf_row, name):
    s = h.shape[0]

    def body(h_ref, w_ref, b_ref, f_ref, cb_ref):
        fl = lax.dot_general(h_ref[...], w_ref[...], _NT, preferred_element_type=f32)
        f_ref[...] = fl
        cum = _cumsum_rows(_log_sigmoid(fl + b_ref[...]))
        for hd in range(N_HEADS):
            cb_ref[:, hd * LANES:(hd + 1) * LANES] = jnp.broadcast_to(cum[:, hd:hd + 1], (s, LANES))

    return pl.pallas_call(
        body, name=name,
        out_shape=[jax.ShapeDtypeStruct((s, LANES), f32), jax.ShapeDtypeStruct((s, N_HEADS * LANES), f32)],
        compiler_params=_params(),
    )(h, w_f_t, bf_row)


def _forget_prep_bwd(rs, dcs, fl, bf_row, name):
    s = fl.shape[0]

    def body(r_ref, c_ref, f_ref, b_ref, df_ref, db_ref):
        eye = (lax.broadcasted_iota(jnp.int32, (N_HEADS, LANES), 0) == lax.broadcasted_iota(jnp.int32, (N_HEADS, LANES), 1)).astype(f32)
        dcum = lax.dot_general(r_ref[...], eye, _TN, precision=lax.Precision.HIGHEST, preferred_element_type=f32)
        for h in range(N_HEADS):
            dcum = dcum - jnp.where(_lane() == h, jnp.sum(c_ref[:, h * LANES:(h + 1) * LANES], axis=1, keepdims=True), 0.0)
        dlf = _cumsum_rows(dcum, reverse=True)
        z = f_ref[...] + b_ref[...]
        df = jnp.where(_lane() < N_HEADS, dlf * jax.nn.sigmoid(-z), 0.0)
        df_ref[...] = df.astype(bf16)
        db_ref[...] = jnp.zeros_like(db_ref)
        db_ref[0:1, :] = jnp.sum(df, axis=0, keepdims=True)

    return pl.pallas_call(
        body, name=name,
        out_shape=[jax.ShapeDtypeStruct((s, LANES), bf16), jax.ShapeDtypeStruct((8, LANES), f32)], compiler_params=_params(),
    )(rs, dcs, fl, bf_row)


def _tile_mask(n_keys, n_queries, off, window):
    shape = (n_keys, n_queries)
    d = lax.broadcasted_iota(jnp.int32, shape, 1) - lax.broadcasted_iota(jnp.int32, shape, 0) + off
    valid = d >= 0
    return jnp.logical_and(valid, d < window) if window else valid


def _wide(v, t):
    return jnp.concatenate([v] * (t // LANES), axis=1)


def _attn_fwd(q, k, v, name, *, cum_b=None, sink_rows=None, window=None, t=256):
    s = q.shape[0]
    t = _row_tile(s, t)
    fox, has_sink = cum_b is not None, sink_rows is not None
    assert not window or (window % LANES == 0 and LANES + window <= s)

    def body(*refs):
        q_ref, k_ref, v_ref = refs[:3]
        rest = list(refs[3:])
        cb_ref = rest.pop(0) if fox else None
        sink_ref = rest.pop(0) if has_sink else None
        o_ref, lse_ref = rest
        i = pl.program_id(1)
        low = _lane() < HEAD_DIM
        top = lax.broadcasted_iota(jnp.int32, (LANES, 1), 0) < HEAD_DIM
        q2 = q_ref[...]
        zero = jnp.zeros_like(q2)
        qms = (jnp.where(low, q2, zero), jnp.where(low, zero, q2))

        def tile(k0, n_keys, off, carry, masked, queries=slice(0, t)):
            nq = queries.stop - queries.start
            kblk, vblk = k_ref[pl.ds(k0, n_keys), :], v_ref[pl.ds(k0, n_keys), :]
            valid = _tile_mask(n_keys, nq, off, window) if masked else None
            ones = jnp.ones_like(vblk)
            vs = tuple(jnp.where(_lane() == L_ROW[h], ones, vblk) for h in range(2))

            def scores(h):
                return lax.dot_general(kblk, qms[h][queries], _NT, preferred_element_type=f32)

            def softmax(h, sc):
                m = carry[h][0]
                if fox:
                    sc = sc - _wide(cb_ref[pl.ds(k0, n_keys), h * LANES:(h + 1) * LANES], nq)
                if masked:
                    sc = jnp.where(valid, sc, NEG)
                m_new = jnp.maximum(m, jnp.max(sc, axis=0, keepdims=True))
                return m_new, jnp.exp(m - m_new), jnp.exp(sc - m_new).astype(bf16)

            def update(h, m_new, alpha, p):
                return m_new, alpha * carry[h][1] + lax.dot_general(vs[h], p, _TN, preferred_element_type=f32)

            if window:
                return tuple(update(h, *softmax(h, scores(h))) for h in range(2))
            scs = [scores(h) for h in range(2)]
            stats = [softmax(h, scs[h]) for h in range(2)]
            return tuple(update(h, *stats[h]) for h in range(2))

        def start(nq):
            if has_sink:
                row = lax.broadcasted_iota(jnp.int32, (LANES, nq), 0)
                return tuple((_wide(sink_ref[h:h + 1, :], nq), (row == L_ROW[h]).astype(f32)) for h in range(2))
            return tuple((jnp.full((1, nq), NEG, f32), jnp.zeros((LANES, nq), f32)) for h in range(2))

        def finish(carry, queries):
            (m0, a0), (m1, a1) = carry
            l0, l1 = a0[L_ROW[0]:L_ROW[0] + 1, :], a1[L_ROW[1]:L_ROW[1] + 1, :]
            o_t = jnp.where(top, a0 * (1.0 / l0), a1 * (1.0 / l1))
            o_ref[queries, :] = o_t.T.astype(bf16)
            lse_ref[0:1, queries] = m0 + jnp.log(l0)
            lse_ref[1:2, queries] = m1 + jnp.log(l1)

        if window:
            for c in range(t // LANES):
                queries = slice(c * LANES, (c + 1) * LANES)
                q0 = i * t + c * LANES
                k0 = pl.multiple_of(jnp.maximum(q0 - window, 0), LANES)
                finish(tile(k0, LANES + window, q0 - k0, start(LANES), True, queries), queries)
        else:
            carry = lax.fori_loop(0, i, lambda kb, c: tile(pl.multiple_of(kb * t, t), t, 0, c, False), start(t))
            half, k_own = t // 2, pl.multiple_of(i * t, t)
            carry = tile(k_own, half, 0, carry, True)
            finish(tuple((m[:, :half], a[:, :half]) for m, a in carry), slice(0, half))
            carry = tuple((m[:, half:], a[:, half:]) for m, a in carry)
            finish(tile(pl.multiple_of(k_own + half, half), half, 0, carry, True, slice(half, t)), slice(half, t))

    q_spec = pl.BlockSpec((t, LANES), lambda j, i: (i, j))
    kv_spec = pl.BlockSpec((s, LANES), lambda j, i: (0, j))
    in_specs, args = [q_spec, kv_spec, kv_spec], [q, k, v]
    if fox:
        in_specs += [pl.BlockSpec((s, 2 * LANES), lambda j, i: (0, j))]
        args += [cum_b]
    if has_sink:
        in_specs += [pl.BlockSpec((None, 2, LANES), lambda j, i: (j, 0, 0))]
        args += [sink_rows.reshape(N_PAIRS, 2, LANES)]
    return pl.pallas_call(
        body, name=name, grid=(N_PAIRS, s // t), in_specs=in_specs,
        out_specs=[q_spec, pl.BlockSpec((None, 2, t), lambda j, i: (j, 0, i))],
        out_shape=[jax.ShapeDtypeStruct((s, N_PAIRS * LANES), bf16), jax.ShapeDtypeStruct((N_PAIRS, 2, s), f32)],
        compiler_params=_params(2),
    )(*args)


def _branch_dgrad_delta(db, w, o, name, *, lse=None, sink_rows=None, after=None):
    s, hw = o.shape
    tm = _row_tile(s, 512)
    has_sink = sink_rows is not None
    extra = [] if after is None else [after]

    def body(*refs):
        db_ref, w_ref, o_ref = refs[:3]
        outs = refs[3 + (2 if has_sink else 0) + len(extra):]
        do_ref, dl_ref = outs[:2]
        if has_sink:
            lse_ref, sink_ref = refs[3:5]
            ds_ref = outs[2]

            @pl.when(pl.program_id(0) == 0)
            def _():
                ds_ref[...] = jnp.zeros_like(ds_ref)
        do = lax.dot_general(db_ref[...], w_ref[...], _NT, preferred_element_type=f32).astype(bf16)
        do_ref[...] = do
        for j in range(N_PAIRS):
            cols = slice(j * LANES, (j + 1) * LANES)
            prod_t = (do[:, cols].astype(f32) * o_ref[:, cols].astype(f32)).T
            for h in range(2):
                dl = jnp.sum(prod_t[h * HEAD_DIM:(h + 1) * HEAD_DIM, :], axis=0, keepdims=True)
                dl_ref[j, h:h + 1, :] = dl
                if has_sink:
                    r = 2 * j + h
                    p_sink = jnp.exp(sink_ref[r:r + 1, 0:1] - lse_ref[j, h:h + 1, :])
                    ds_ref[r:r + 1, :] += -jnp.sum(p_sink * dl, axis=1, keepdims=True)

    rows_spec = pl.BlockSpec((N_PAIRS, 2, tm), lambda i: (0, 0, i))
    in_specs = [_row_spec(tm, db.shape[1]), pl.BlockSpec(w.shape, lambda i: (0, 0)), _row_spec(tm, hw)]
    args = [db, w, o]
    out_specs = [_row_spec(tm, hw), rows_spec]
    out_shape = [jax.ShapeDtypeStruct((s, hw), bf16), jax.ShapeDtypeStruct((N_PAIRS, 2, s), f32)]
    if has_sink:
        in_specs += [rows_spec, _vec_spec(LANES, N_HEADS)]
        args += [lse, sink_rows]
        out_specs += [_vec_spec(LANES, N_HEADS)]
        out_shape += [jax.ShapeDtypeStruct((N_HEADS, LANES), f32)]
    return pl.pallas_call(
        body, name=name, grid=(s // tm,), in_specs=in_specs + [pl.BlockSpec(memory_space=pl.ANY)] * len(extra),
        out_specs=out_specs, out_shape=out_shape, compiler_params=_params(1),
    )(*args, *extra)


def _attn_bwd(q, k, v, do, lse, delta, name, *, cum_b=None, window=None, t=256):
    s = q.shape[0]
    t = _row_tile(s, t)
    nblk = s // t
    fox = cum_b is not None
    assert not window or (window % LANES == 0 and LANES + window <= s)

    def body(*refs):
        k_ref, v_ref, q_ref, do_ref, lse_ref, dl_ref = refs[:6]
        rest = list(refs[6:])
        cb_ref = rest.pop(0) if fox else None
        dq_ref, dk_ref, dv_ref = rest[:3]
        dcs_ref, rs_ref = (rest[3], rest[4]) if fox else (None, None)
        dk_acc, dv_acc = rest[-2:]
        b = pl.program_id(1)
        k0 = pl.multiple_of(b * t, t)

        @pl.when(b == 0)
        def _():
            dq_ref[...] = jnp.zeros_like(dq_ref)
            if fox:
                rs_ref[...] = jnp.zeros_like(rs_ref)

        dk_acc[...] = jnp.zeros_like(dk_acc)
        dv_acc[...] = jnp.zeros_like(dv_acc)
        if fox:
            dcs_ref[...] = jnp.zeros_like(dcs_ref)
        low = _lane() < HEAD_DIM
        top = lax.broadcasted_iota(jnp.int32, (LANES, 1), 0) < HEAD_DIM
        kblk, vblk = k_ref[...], v_ref[...]
        k_t = kblk.astype(f32).T.astype(bf16)
        cks = [_wide(cb_ref[pl.ds(k0, t), h * LANES:(h + 1) * LANES], t) for h in range(2)] if fox else None

        def tile(q0, n_queries, off, masked, keys=slice(0, t)):
            cols = pl.ds(q0, n_queries)
            q2, do2 = q_ref[cols, :], do_ref[cols, :]
            zero = jnp.zeros_like(q2)
            valid = _tile_mask(keys.stop - keys.start, n_queries, off, window) if masked else None
            dq_parts = []
            for h in range(2):
                qm = jnp.where(low, q2, zero) if h == 0 else jnp.where(low, zero, q2)
                dom = jnp.where(low, do2, zero) if h == 0 else jnp.where(low, zero, do2)
                sc = lax.dot_general(kblk[keys], qm, _NT, preferred_element_type=f32)
                if fox:
                    sc = sc - cks[h][keys, :n_queries]
                if masked:
                    sc = jnp.where(valid, sc, NEG)
                p = jnp.exp(sc - lse_ref[h:h + 1, cols])
                dp = lax.dot_general(vblk[keys], dom, _NT, preferred_element_type=f32)
                ds = p * (dp - dl_ref[h:h + 1, cols])
                pb, dsb = p.astype(bf16), ds.astype(bf16)
                dv_acc[keys, :] += jnp.dot(pb, dom, preferred_element_type=f32)
                dk_acc[keys, :] += jnp.dot(dsb, qm, preferred_element_type=f32)
                dq_parts.append(jnp.dot(k_t[:, keys], dsb, preferred_element_type=f32))
                if fox:
                    dcs_ref[keys, h * LANES:(h + 1) * LANES] += sum(ds[:, g * LANES:(g + 1) * LANES]
                                                                    for g in range(n_queries // LANES))
                    rs_ref[h:h + 1, cols] += jnp.sum(ds, axis=0, keepdims=True)
            dq_ref[:, cols] += jnp.where(top, dq_parts[0], dq_parts[1])

        def later_block(qb, carry):
            tile(pl.multiple_of(qb * t, t), t, 0, False)
            return carry

        if window:
            for c in range(t // LANES):
                first = b * t + c * LANES
                q0 = pl.multiple_of(jnp.minimum(first, s - (LANES + window)), LANES)
                tile(q0, LANES + window, q0 - first, True, slice(c * LANES, (c + 1) * LANES))
        else:
            half = t // 2
            tile(k0, half, 0, True, slice(0, half))
            tile(pl.multiple_of(k0 + half, half), half, half, True)
            lax.fori_loop(b + 1, nblk, later_block, 0)
        dk_ref[...] = dk_acc[...].astype(bf16)
        dv_ref[...] = dv_acc[...].astype(bf16)

    kv_spec = pl.BlockSpec((t, LANES), lambda j, b: (b, j))
    seq_spec = pl.BlockSpec((s, LANES), lambda j, b: (0, j))
    rows_spec = pl.BlockSpec((None, 2, s), lambda j, b: (j, 0, 0))
    hw = N_PAIRS * LANES
    in_specs, args = [kv_spec, kv_spec, seq_spec, seq_spec, rows_spec, rows_spec], [k, v, q, do, lse, delta]
    out_specs = [pl.BlockSpec((LANES, s), lambda j, b: (j, 0)), kv_spec, kv_spec]
    out_shape = [jax.ShapeDtypeStruct((hw, s), f32), jax.ShapeDtypeStruct((s, hw), bf16), jax.ShapeDtypeStruct((s, hw), bf16)]
    if fox:
        in_specs += [pl.BlockSpec((s, 2 * LANES), lambda j, b: (0, j))]
        args += [cum_b]
        out_specs += [pl.BlockSpec((t, 2 * LANES), lambda j, b: (b, j)), rows_spec]
        out_shape += [jax.ShapeDtypeStruct((s, N_HEADS * LANES), f32), jax.ShapeDtypeStruct((N_PAIRS, 2, s), f32)]
    return pl.pallas_call(
        body, name=name, grid=(N_PAIRS, nblk), in_specs=in_specs, out_specs=out_specs, out_shape=out_shape,
        scratch_shapes=[pltpu.VMEM((t, LANES), f32)] * 2, compiler_params=_params(2),
    )(*args)


def _branch_merge(o_a, o_b, w_a, w_b, gl, name):
    s, k = o_a.shape
    d = w_a.shape[1]
    tm = _row_tile(s, 1024)

    def body(oa_ref, ob_ref, wa_ref, wb_ref, g_ref, ba_ref, bb_ref, m_ref):
        ba = jnp.dot(oa_ref[...], wa_ref[...], preferred_element_type=f32)
        bb = jnp.dot(ob_ref[...], wb_ref[...], preferred_element_type=f32)
        g0, g1 = jax.nn.sigmoid(g_ref[:, :d].astype(f32)), jax.nn.sigmoid(g_ref[:, d:].astype(f32))
        ba_ref[...] = ba.astype(bf16)
        bb_ref[...] = bb.astype(bf16)
        m_ref[...] = (g0 * ba + g1 * bb).astype(bf16)

    whole = pl.BlockSpec((k, d), lambda i: (0, 0))
    return pl.pallas_call(
        body, name=name, grid=(s // tm,),
        in_specs=[_row_spec(tm, k), _row_spec(tm, k), whole, whole, _row_spec(tm, 2 * d)],
        out_specs=[_row_spec(tm, d)] * 3, out_shape=[jax.ShapeDtypeStruct((s, d), bf16)] * 3, compiler_params=_params(1),
    )(o_a, o_b, w_a, w_b, gl)


def _out_dgrad_merge_bwd(dy, w_out, ba, bb, gl, name):
    s, d = ba.shape
    tm = _row_tile(s, 512)

    def body(dy_ref, w_ref, a_ref, b_ref, g_ref, da_ref, db_ref, dg_ref):
        dmv = lax.dot_general(dy_ref[...], w_ref[...], _NT, preferred_element_type=f32)
        g0, g1 = jax.nn.sigmoid(g_ref[:, :d].astype(f32)), jax.nn.sigmoid(g_ref[:, d:].astype(f32))
        da_ref[...] = (dmv * g0).astype(bf16)
        db_ref[...] = (dmv * g1).astype(bf16)
        dg_ref[:, :d] = (dmv * a_ref[...].astype(f32) * (g0 * (1.0 - g0))).astype(bf16)
        dg_ref[:, d:] = (dmv * b_ref[...].astype(f32) * (g1 * (1.0 - g1))).astype(bf16)

    return pl.pallas_call(
        body, name=name, grid=(s // tm,),
        in_specs=[_row_spec(tm, dy.shape[1]), pl.BlockSpec(w_out.shape, lambda i: (0, 0))] + [_row_spec(tm, d)] * 2
        + [_row_spec(tm, 2 * d)],
        out_specs=[_row_spec(tm, d)] * 2 + [_row_spec(tm, 2 * d)],
        out_shape=[jax.ShapeDtypeStruct((s, d), bf16)] * 2 + [jax.ShapeDtypeStruct((s, 2 * d), bf16)],
        compiler_params=_params(1),
    )(dy, w_out, ba, bb, gl)


GLU_TILE = 256


def _ffn_in_swiglu(h, w_t, name):
    s, d = h.shape
    f = w_t.shape[0] // 2
    tm = _row_tile(s, 2048)
    tg = GLU_TILE
    nb = f // tg

    def body(h_ref, wg_ref, wu_ref, g_ref, u_ref, act_ref):
        hv = h_ref[...]
        g = lax.dot_general(hv, wg_ref[...], _NT, preferred_element_type=f32)
        u = lax.dot_general(hv, wu_ref[...], _NT, preferred_element_type=f32)
        g_ref[...] = g.astype(bf16)
        u_ref[...] = u.astype(bf16)
        act_ref[...] = (g * jax.nn.sigmoid(g) * u).astype(bf16)

    col = pl.BlockSpec((tm, tg), lambda i, j: (i, j))
    return pl.pallas_call(
        body, name=name, grid=(s // tm, nb),
        in_specs=[pl.BlockSpec((tm, d), lambda i, j: (i, 0)), pl.BlockSpec((tg, d), lambda i, j: (j, 0)),
                  pl.BlockSpec((tg, d), lambda i, j: (j + nb, 0))],
        out_specs=[col] * 3, out_shape=[jax.ShapeDtypeStruct((s, f), bf16)] * 3, compiler_params=_params(2),
    )(h, w_t, w_t)


def _ffn_out_dgrad_swiglu(dy, w_out, g, u, name):
    s, d = dy.shape
    f = g.shape[1]
    tm = _row_tile(s, 2048)
    tg = GLU_TILE

    def body(dy_ref, w_ref, g_ref, u_ref, dg_ref, du_ref):
        dv = lax.dot_general(dy_ref[...], w_ref[...], _NT, preferred_element_type=f32)
        gv, uv = g_ref[...].astype(f32), u_ref[...].astype(f32)
        sg = jax.nn.sigmoid(gv)
        dg_ref[...] = (dv * uv * (sg * (1.0 + gv * (1.0 - sg)))).astype(bf16)
        du_ref[...] = (dv * (gv * sg)).astype(bf16)

    col = pl.BlockSpec((tm, tg), lambda i, j: (i, j))
    return pl.pallas_call(
        body, name=name, grid=(s // tm, f // tg),
        in_specs=[pl.BlockSpec((tm, d), lambda i, j: (i, 0)), pl.BlockSpec((tg, d), lambda i, j: (j, 0)), col, col],
        out_specs=[col] * 2, out_shape=[jax.ShapeDtypeStruct((s, f), bf16)] * 2, compiler_params=_params(2),
    )(dy, w_out, g, u)


def _wgrad_stack(parts, h, name):
    s, m = parts[0].shape
    d = h.shape[1]
    tm = 256
    nb = m // tm
    n = len(parts)

    def body(*refs):
        i = pl.program_id(0)
        for p in range(n):
            @pl.when(i // nb == p)
            def _(p=p):
                refs[n + 1][...] = lax.dot_general(refs[p][...], refs[n][...], _TN, preferred_element_type=f32).astype(bf16)

    a_specs = [pl.BlockSpec((s, tm), lambda i, p=p: (0, jnp.clip(i - p * nb, 0, nb - 1))) for p in range(n)]
    return pl.pallas_call(
        body, name=name, grid=(n * nb,), in_specs=a_specs + [pl.BlockSpec((s, d), lambda i: (0, 0))],
        out_specs=pl.BlockSpec((tm, d), lambda i: (i, 0)),
        out_shape=jax.ShapeDtypeStruct((n * m, d), bf16), compiler_params=_params(1),
    )(*parts, h)


def _ada_wgrad(c_all, d_all, name):
    n, d = c_all.shape
    w = d_all.shape[1]

    def body(c_ref, d_ref, o_ref):
        eye = (lax.broadcasted_iota(jnp.int32, (n, n), 0) == lax.broadcasted_iota(jnp.int32, (n, n), 1)).astype(f32)
        ct = lax.dot_general(c_ref[...], eye, _TN, precision=lax.Precision.HIGHEST, preferred_element_type=f32)
        g = ct[:, 0:1] * d_ref[0:1, :]
        for bi in range(1, n):
            g = g + ct[:, bi:bi + 1] * d_ref[bi:bi + 1, :]
        o_ref[0] = g

    return pl.pallas_call(
        body, name=name, out_shape=jax.ShapeDtypeStruct((1, d, w), f32), compiler_params=_params(),
    )(c_all, d_all)


def _adamw(parts, w, m, v, name, mine=None, me=None):
    r, c = w.shape
    n_parts = parts.shape[0]
    row_tiles = [t for t in range(min(r, 256), 0, -1) if r % t == 0 and (t % 16 == 0 or t == r)]
    if row_tiles:
        tr, tc = row_tiles[0], c
    else:
        tr, tc = r, next(t for t in (256, LANES) if c % t == 0)

    def body(*refs):
        w_ref, m_ref, v_ref, g_ref, d_ref, nm_ref, nv_ref = refs[-7:]
        if mine is None:
            p_ref, = refs[:-7]
        else:
            me_ref, p_ref, own_ref = refs[:-7]

        def part(i):
            if mine is None:
                return p_ref[i].astype(f32)
            return jnp.where(me_ref[0] == i, own_ref[...], p_ref[i]).astype(f32)

        g = part(0)
        for i in range(1, n_parts):
            g = g + part(i)
        mm = ADAM_B1 * m_ref[...] + (1.0 - ADAM_B1) * g
        vv = ADAM_B2 * v_ref[...] + (1.0 - ADAM_B2) * (g * g)
        m_hat = mm / (1.0 - ADAM_B1 ** ADAM_STEP)
        v_hat = vv / (1.0 - ADAM_B2 ** ADAM_STEP)
        g_ref[...] = g
        d_ref[...] = -ADAM_LR * (m_hat / (jnp.sqrt(v_hat) + ADAM_EPS) + ADAM_WD * w_ref[...])
        nm_ref[...] = mm
        nv_ref[...] = vv

    out_shape = [jax.ShapeDtypeStruct((r, c), f32)] * 4
    if mine is None:
        spec = pl.BlockSpec((tr, tc), lambda i, j: (i, j))
        return pl.pallas_call(
            body, name=name, grid=(r // tr, c // tc),
            in_specs=[pl.BlockSpec((n_parts, tr, tc), lambda i, j: (0, i, j))] + [spec] * 3,
            out_specs=[spec] * 4, out_shape=out_shape, compiler_params=_params(2),
        )(parts, w, m, v)
    spec = pl.BlockSpec((tr, tc), lambda i, j, me_ref: (i, j))
    return pl.pallas_call(
        body, name=name, out_shape=out_shape, compiler_params=_params(2),
        grid_spec=pltpu.PrefetchScalarGridSpec(
            num_scalar_prefetch=1, grid=(r // tr, c // tc),
            in_specs=[pl.BlockSpec((n_parts, tr, tc), lambda i, j, me_ref: (0, i, j)),
                      pl.BlockSpec((None, tr, tc), lambda i, j, me_ref: (me_ref[0], i, j))] + [spec] * 3,
            out_specs=[spec] * 4),
    )(me, parts, mine, w, m, v)


def _me():
    return lax.axis_index("x"), lax.axis_index("y"), lax.axis_index("c")


def _gather_prologue(c, w_ada, b_mine, w_in_t, pos_col, name):
    n_dev, d = N_DEV, c.shape[1]
    ada_w = w_ada.shape[1]
    s = pos_col.shape[0]

    def body(c_ref, w_ref, b_ref, win_ref, pos_ref, freq_ref, call_ref, ada_ref, gin_ref, cos_ref, sin_ref,
             cols_ref, send_sems, recv_sems, local_sems):
        x, y, cc = _me()
        me, sibling = (x, y, cc), (x, y, 1 - cc)
        chips = [(1 - x, y), (x, 1 - y), (1 - x, 1 - y)]
        outs = (call_ref, ada_ref, gin_ref)

        def rows(a, dev):
            return outs[a].at[4 * dev[0] + 2 * dev[1] + dev[2]]

        def copy(a, k, block, to, src=None):
            return pltpu.make_async_remote_copy(
                src_ref=rows(a, block) if src is None else src, dst_ref=rows(a, block),
                send_sem=send_sems.at[a, k], recv_sem=recv_sems.at[a, k], device_id=to, device_id_type=MESH)

        def begin(a, src):
            own = pltpu.make_async_copy(src, rows(a, me), local_sems.at[a])
            sends = [copy(a, 0, me, sibling, src=src)] + [copy(a, 1 + j, me, (*chip, cc), src=src) for j, chip in enumerate(chips)]
            for cp in [own] + sends:
                cp.start()
            return own, sends

        def finish(a, own, sends):
            passed = []
            for j, chip in enumerate(chips):
                copy(a, 1 + j, (*chip, cc), me).wait_recv()
                passed.append(copy(a, 4 + j, (*chip, cc), sibling))
                passed[-1].start()
            copy(a, 0, sibling, me).wait_recv()
            for j, chip in enumerate(chips):
                copy(a, 4 + j, (*chip, 1 - cc), me).wait_recv()
            for cp in sends + passed:
                cp.wait_send()
            own.wait()

        finish(0, *begin(0, c_ref))
        cols_ref[...] = (jnp.dot(call_ref[:, 0, :].astype(bf16), w_ref[...].astype(bf16), preferred_element_type=f32)
                         + b_ref[...])
        finish(1, *begin(1, cols_ref))
        in_flight = begin(2, win_ref)
        cos_ref[...], sin_ref[...] = _rope_tables(pos_ref[...], freq_ref[...])
        finish(2, *in_flight)

    vmem, hbm = pl.BlockSpec(memory_space=pltpu.VMEM), pl.BlockSpec(memory_space=pl.ANY)
    return pl.pallas_call(
        body, name=name, in_specs=[vmem, vmem, vmem, hbm, vmem, vmem], out_specs=[vmem, vmem, hbm, vmem, vmem],
        out_shape=[jax.ShapeDtypeStruct((n_dev, 1, d), f32), jax.ShapeDtypeStruct((n_dev, n_dev, ada_w), f32),
                   jax.ShapeDtypeStruct((n_dev,) + w_in_t.shape, w_in_t.dtype)]
        + [jax.ShapeDtypeStruct((s, LANES), f32)] * 2,
        scratch_shapes=[pltpu.VMEM((n_dev, ada_w), f32), pltpu.SemaphoreType.DMA((3, 7)), pltpu.SemaphoreType.DMA((3, 7)),
                        pltpu.SemaphoreType.DMA((3,))],
        compiler_params=pltpu.CompilerParams(vmem_limit_bytes=VMEM_LIMIT),
    )(c, w_ada, b_mine, w_in_t, pos_col, _rope_freq_row())


_FLIPS = ((0, 0, 1), (1, 0, 0), (0, 1, 0), (1, 1, 0), (1, 0, 1), (0, 1, 1), (1, 1, 1))
_HBM = pl.BlockSpec(memory_space=pltpu.HBM)
_SEM = pl.BlockSpec(memory_space=pltpu.SEMAPHORE)


def _exchange_copies(scatter, srcs, lands, send_sems, recv_sems):
    x, y, c = _me()
    me_row = 4 * x + 2 * y + c
    out = []
    for k, (fx, fy, fc) in enumerate(_FLIPS):
        peer = (x ^ fx, y ^ fy, c ^ fc)
        peer_row = 4 * peer[0] + 2 * peer[1] + peer[2]
        for a in range(len(srcs)):
            out.append(pltpu.make_async_remote_copy(
                src_ref=srcs[a].at[peer_row] if scatter else srcs[a], dst_ref=lands[a].at[me_row],
                send_sem=send_sems.at[7 * a + k], recv_sem=recv_sems.at[7 * a + k], device_id=peer, device_id_type=MESH))
    return out


def _exchange_start(arrays, scatter, name, after=None):
    n = len(arrays)
    lands = [lax.empty(a.shape if scatter else (N_DEV,) + a.shape, a.dtype) for a in arrays]
    extra = [] if after is None else [after]

    def body(*refs):
        srcs, zones = refs[:n], refs[n:2 * n]
        send_sems, recv_sems = refs[2 * n + len(extra)], refs[2 * n + len(extra) + 1]
        token = refs[-1]
        for cp in _exchange_copies(scatter, srcs, zones, send_sems, recv_sems):
            cp.start()
        token[...] = jnp.zeros_like(token)

    thru = [pltpu.HBM(a.shape, a.dtype) for a in list(arrays) + lands]
    outs = pl.pallas_call(
        body, name=name,
        out_shape=(pltpu.SemaphoreType.DMA((7 * n,)), pltpu.SemaphoreType.DMA((7 * n,)), *thru, jax.ShapeDtypeStruct((8, LANES), f32)),
        in_specs=[_HBM] * (2 * n) + [pl.BlockSpec(memory_space=pl.ANY)] * len(extra),
        out_specs=(_SEM, _SEM, *[_HBM] * (2 * n), pl.BlockSpec(memory_space=pltpu.VMEM)),
        input_output_aliases={i: 2 + i for i in range(2 * n)},
        compiler_params=pltpu.CompilerParams(has_side_effects=pltpu.SideEffectType.DATAFLOW_SIDE_EFFECTING),
    )(*[pltpu.with_memory_space_constraint(a, pltpu.HBM) for a in list(arrays) + lands], *extra)
    return dict(n=n, scatter=scatter, sems=outs[:2], srcs=outs[2:2 + n], lands=outs[2 + n:2 + 2 * n], token=outs[-1])


def _exchange_wait(handle, after, name):
    n, scatter = handle["n"], handle["scatter"]

    def body(*refs):
        srcs, zones = refs[:n], refs[n:2 * n]
        send_sems, recv_sems = refs[2 * n], refs[2 * n + 1]
        for cp in _exchange_copies(scatter, srcs, zones, send_sems, recv_sems):
            cp.wait_send()
            cp.wait_recv()

    thru = [pltpu.HBM(a.shape, a.dtype) for a in list(handle["srcs"]) + list(handle["lands"])]
    outs = pl.pallas_call(
        body, name=name, out_shape=tuple(thru),
        in_specs=[_HBM] * (2 * n) + [_SEM, _SEM, pl.BlockSpec(memory_space=pl.ANY)], out_specs=tuple([_HBM] * (2 * n)),
        input_output_aliases={i: i for i in range(2 * n)},
        compiler_params=pltpu.CompilerParams(has_side_effects=pltpu.SideEffectType.DATAFLOW_SIDE_EFFECTING),
    )(*handle["srcs"], *handle["lands"], *handle["sems"], after)
    return list(outs[:n]), list(outs[n:])


def _cols_from_shards(g):
    return jnp.transpose(g, (1, 0, 2)).reshape(g.shape[1], -1)


def _shards_from_cols(a):
    return jnp.transpose(a.reshape(a.shape[0], N_DEV, -1), (1, 0, 2))


def _local_step(x, rope, ada, g_pre_mix, g_post_mix, b_f, sinks, g_pre_ffn, g_post_ffn, target,
                w_in_t, mix_weights, ffn_weights, on_grads):
    s, d = x.shape
    row = lambda v: v.reshape(1, -1)
    shift_m, scale_m, gate_m, shift_f, scale_f, gate_f = (ada[i:i + 1] for i in range(6))
    w_gate_t, w_qkv_t = w_in_t[F_OFF + N_HEADS:], w_in_t[:QKV_W]
    w_f_t = jnp.pad(w_in_t[F_OFF:F_OFF + N_HEADS], ((0, LANES - N_HEADS), (0, 0)))
    bf_row = jnp.pad(row(b_f), ((0, 0), (0, LANES - N_HEADS)))
    sink_rows = jnp.broadcast_to(sinks.reshape(N_HEADS, 1).astype(f32), (N_HEADS, LANES))
    cos, sin_s = rope

    h1, qa, ka, va, qb, kb, vb = _prenorm_proj_qkv(x, row(g_pre_mix), scale_m, shift_m, w_qkv_t, cos, sin_s, "prenorm_proj_qkv")
    gl = _matmul(h1, w_gate_t, "nt", bf16, "proj_gate")
    fl, cum_b = _forget_prep(h1, w_f_t, bf_row, "proj_forget_prep")
    o_a, lse_a = _attn_fwd(qa, ka, va, "swa_fwd", sink_rows=sink_rows, window=WINDOW, t=2048)
    o_b, lse_b = _attn_fwd(qb, kb, vb, "fox_fwd", cum_b=cum_b, t=1024)
    everything_before = (gl[:8, :LANES] + o_a[:8, :LANES] + o_b[:8, :LANES]).astype(f32)
    w_branch_a, w_branch_b, w_out = mix_weights(everything_before)
    ba, bb, merged = _branch_merge(o_a, o_b, w_branch_a, w_branch_b, gl, "branch_merge")
    y1, x2, h2 = _out_proj_postnorm_prenorm(merged, w_out, x, row(g_post_mix), gate_m, row(g_pre_ffn), scale_f, shift_f,
                                            "out_proj_norms")

    w_ffn_in_t, w_ffn_out = ffn_weights(h2)
    g_ff, u_ff, act = _ffn_in_swiglu(h2, w_ffn_in_t, "ffn_in_swiglu")
    loss_row, d_out, d_y2, vec_pf = _out_proj_loss_tail(act, w_ffn_out, x2, row(g_post_ffn), gate_f, target, "ffn_out_loss_tail")

    g_w_ffn_out = _matmul(act, d_y2, "tn", bf16, "ffn_out_wgrad")
    dg_ff, du_ff = _ffn_out_dgrad_swiglu(d_y2, w_ffn_out, g_ff, u_ff, "ffn_out_dgrad_swiglu")
    g_w_ffn_in_t = _wgrad_stack([dg_ff, du_ff], h2, "ffn_in_wgrad")
    sent = on_grads(dict(w_ffn_in=g_w_ffn_in_t, w_ffn_out=g_w_ffn_out))
    d_x2, vec_nf, d_y1, vec_pm = _dgrad_prenorm_bwd(
        [(dg_ff, w_ffn_in_t, 0), (du_ff, w_ffn_in_t, 1)], x2, row(g_pre_ffn), scale_f, d_out, "ffn_in_dgrad_norms_bwd",
        after=sent, below=(y1, row(g_post_mix), gate_m))

    g_w_out = _matmul(merged, d_y1, "tn", bf16, "out_proj_wgrad")
    d_ba, d_bb, dgl = _out_dgrad_merge_bwd(d_y1, w_out, ba, bb, gl, "out_proj_dgrad_merge_bwd")
    g_w_branch_a = _matmul(o_a, d_ba, "tn", bf16, "branch_a_wgrad")
    g_w_branch_b = _matmul(o_b, d_bb, "tn", bf16, "branch_b_wgrad")
    sent = on_grads(dict(w_out=g_w_out, w_branch_a=g_w_branch_a, w_branch_b=g_w_branch_b))
    d_oa, delta_a, d_sink = _branch_dgrad_delta(d_ba, w_branch_a, o_a, "branch_a_dgrad_delta", lse=lse_a,
                                                sink_rows=sink_rows, after=sent)
    d_ob, delta_b = _branch_dgrad_delta(d_bb, w_branch_b, o_b, "branch_b_dgrad_delta", after=sent)
    dqa_t, dka, dva = _attn_bwd(qa, ka, va, d_oa, lse_a, delta_a, "swa_bwd", window=WINDOW, t=2048)
    dqb_t, dkb, dvb, dcs, rs = _attn_bwd(qb, kb, vb, d_ob, lse_b, delta_b, "fox_bwd", cum_b=cum_b, t=512)
    dqkv = _qkv_prep_bwd(dqa_t, dka, dva, dqb_t, dkb, dvb, cos, sin_s, "qkv_prep_bwd")
    dfl, vec_bf = _forget_prep_bwd(rs.reshape(N_HEADS, s), dcs, fl, bf_row, "forget_prep_bwd")
    g_w_in_t = jnp.concatenate([_matmul(dqkv, h1, "tn", bf16, "qkv_wgrad"), _matmul(dfl, h1, "tn", bf16, "forget_wgrad")[:N_HEADS],
                                _matmul(dgl, h1, "tn", bf16, "gate_wgrad")], axis=0)
    sent = on_grads(dict(w_in=g_w_in_t))
    grad_x, vec_nm = _dgrad_prenorm_bwd([(dgl, w_gate_t, 0), (dqkv, w_qkv_t, 0), (dfl, w_f_t, 0)], x, row(g_pre_mix),
                                        scale_m, d_x2, "in_proj_dgrad_prenorm_bwd", after=sent)

    d_ada = jnp.concatenate([vec_nm[0], vec_nm[1], vec_pm[0], vec_nf[0], vec_nf[1], vec_pf[0]])
    small = dict(b_ada=d_ada, g_pre_mix=vec_nm[2], g_post_mix=vec_pm[1], g_pre_ffn=vec_nf[2], g_post_ffn=vec_pf[1],
                 b_f=vec_bf[0, :N_HEADS], sinks=d_sink[:, 0], loss=loss_row[0, :1])
    return grad_x, small


_SMALL = (("b_ada", 6144), ("g_pre_mix", 1024), ("g_post_mix", 1024), ("g_pre_ffn", 1024), ("g_post_ffn", 1024),
          ("b_f", 128), ("sinks", 128), ("loss", 128))
_SMALL_ROWS = 88


def _pack_small(vals):
    parts = [jnp.pad(vals[k].reshape(-1).astype(f32), (0, n - vals[k].size)) for k, n in _SMALL]
    flat = jnp.concatenate(parts)
    return jnp.pad(flat, (0, _SMALL_ROWS * LANES - flat.size)).reshape(_SMALL_ROWS, LANES)


def _unpack_small(slab, shapes):
    flat, out, off = slab.reshape(-1), {}, 0
    for k, n in _SMALL:
        size = math.prod(shapes[k])
        out[k] = flat[off:off + size].reshape(shapes[k])
        off += n
    return out


def kernel(x, c, positions, w_ada, b_ada, g_pre_mix, g_post_mix, w_in, b_f, sinks, w_branch_a, w_branch_b, w_out, g_pre_ffn, g_post_ffn, w_ffn_in, w_ffn_out, loss_target, m_w_ada, m_b_ada, m_g_pre_mix, m_g_post_mix, m_w_in, m_b_f, m_sinks, m_w_branch_a, m_w_branch_b, m_w_out, m_g_pre_ffn, m_g_post_ffn, m_w_ffn_in, m_w_ffn_out, v_w_ada, v_b_ada, v_g_pre_mix, v_g_post_mix, v_w_in, v_b_f, v_sinks, v_w_branch_a, v_w_branch_b, v_w_out, v_g_pre_ffn, v_g_post_ffn, v_w_ffn_in, v_w_ffn_out):
    xi, yi, ci = _me()
    me = 4 * xi + 2 * yi + ci
    d = D_MODEL
    ada_w = w_ada.shape[2]

    transposed = ("w_in", "w_ffn_in")
    tr = lambda a: jnp.transpose(a[0])

    b_mine = lax.dynamic_slice(b_ada, (0, me * ada_w), (1, ada_w))
    c_all, ada_all, g_in, cos, sin_s = _gather_prologue(c, w_ada[0], b_mine, tr(w_in).astype(bf16),
                                                        positions[0].reshape(-1, 1), "gather_prologue")
    c_all = c_all.reshape(N_DEV, d)
    ada = lax.dynamic_index_in_dim(ada_all, me, axis=1, keepdims=False).reshape(6, d)
    late_mix = [w.astype(bf16) for w in (w_branch_a[0], w_branch_b[0], w_out[0])]
    late_ffn = [w.astype(bf16) for w in (tr(w_ffn_in), w_ffn_out[0])]
    mix_h = _exchange_start(late_mix, False, "gather_mix_start", after=g_in)
    ffn_h = _exchange_start(late_ffn, False, "gather_ffn_start", after=mix_h["token"])

    def mine_into(zone, block):
        return lax.dynamic_update_index_in_dim(zone, block, me, 0)

    def rows_from_shards(g):
        return g.reshape(g.shape[0] * g.shape[1], g.shape[2])

    def mix_weights(after):
        sent, zones = _exchange_wait(mix_h, after, "gather_mix_wait")
        g_ba, g_bb, g_out = (mine_into(z, w) for z, w in zip(zones, sent))
        return _cols_from_shards(g_ba), _cols_from_shards(g_bb), rows_from_shards(g_out)

    def ffn_weights(after):
        sent, zones = _exchange_wait(ffn_h, after, "gather_ffn_wait")
        g_fi, g_fo = (mine_into(z, w) for z, w in zip(zones, sent))
        return rows_from_shards(g_fi), rows_from_shards(g_fo)

    row_sharded = ("w_out", "w_ffn_out") + transposed
    in_flight = []

    def on_grads(group):
        sends = [g.reshape(N_DEV, g.shape[0] // N_DEV, g.shape[1]) if nm in row_sharded else _shards_from_cols(g)
                 for nm, g in group.items()]
        handle = _exchange_start(sends, True, "scatter_start_%d" % len(in_flight))
        in_flight.append((list(group), handle))
        return handle["token"]

    grad_x, small = _local_step(
        x[0], (cos, sin_s), ada + ffn_h["token"][0, 0], g_pre_mix[0], g_post_mix[0], b_f[0], sinks[0], g_pre_ffn[0],
        g_post_ffn[0], loss_target[0], rows_from_shards(g_in), mix_weights, ffn_weights, on_grads)

    ws = dict(w_in=(w_in, m_w_in, v_w_in), w_branch_a=(w_branch_a, m_w_branch_a, v_w_branch_a),
              w_branch_b=(w_branch_b, m_w_branch_b, v_w_branch_b), w_out=(w_out, m_w_out, v_w_out),
              w_ffn_in=(w_ffn_in, m_w_ffn_in, v_w_ffn_in), w_ffn_out=(w_ffn_out, m_w_ffn_out, v_w_ffn_out))
    res = {}

    def finish_group(gi, after):
        names, handle = in_flight[gi]
        sends, zones = _exchange_wait(handle, after, "scatter_wait_%d" % gi)
        for nm, zone, sent in zip(names, zones, sends):
            w, m, v = (tr(a) if nm in transposed else a[0] for a in ws[nm])
            out = _adamw(zone, w, m, v, "adamw_" + nm, mine=sent, me=me.reshape(1).astype(jnp.int32))
            after = out[0]
            res[nm] = [jnp.transpose(o) for o in out] if nm in transposed else out
        return after

    small_h = _exchange_start([_pack_small(small)], False, "gather_small_start", after=grad_x)
    done = finish_group(1, finish_group(0, small_h["token"]))
    (slab_mine,), (slab_zone,) = _exchange_wait(small_h, done, "gather_small_wait")
    slab_all = mine_into(slab_zone, slab_mine)
    small_w = dict(b_ada=b_ada, g_pre_mix=g_pre_mix, g_post_mix=g_post_mix, g_pre_ffn=g_pre_ffn, g_post_ffn=g_post_ffn,
                   b_f=b_f, sinks=sinks, loss=jnp.zeros((1,), f32))
    small_m = dict(b_ada=m_b_ada, g_pre_mix=m_g_pre_mix, g_post_mix=m_g_post_mix, g_pre_ffn=m_g_pre_ffn,
                   g_post_ffn=m_g_post_ffn, b_f=m_b_f, sinks=m_sinks, loss=jnp.zeros((1,), f32))
    small_v = dict(b_ada=v_b_ada, g_pre_mix=v_g_pre_mix, g_post_mix=v_g_post_mix, g_pre_ffn=v_g_pre_ffn,
                   g_post_ffn=v_g_post_ffn, b_f=v_b_f, sinks=v_sinks, loss=jnp.ones((1,), f32))
    shapes = {k: small_w[k].shape for k, _ in _SMALL}
    s_out = _adamw(slab_all, _pack_small(small_w), _pack_small(small_m), _pack_small(small_v), "adamw_small")
    s_grad, s_delta, s_m, s_v = (_unpack_small(o, shapes) for o in s_out)

    d_ada_all = lax.dynamic_slice(slab_all[:, :6144 // LANES, :].reshape(N_DEV, 6144), (0, me * ada_w), (N_DEV, ada_w))
    ada_parts = _ada_wgrad(c_all, d_ada_all, "ada_wgrad")

    res["w_ada"] = _adamw(ada_parts, w_ada[0], m_w_ada[0], v_w_ada[0], "adamw_w_ada")
    finish_group(2, res["w_ada"][0])

    order = ["w_ada", "b_ada", "g_pre_mix", "g_post_mix", "w_in", "b_f", "sinks", "w_branch_a", "w_branch_b", "w_out",
             "g_pre_ffn", "g_post_ffn", "w_ffn_in", "w_ffn_out"]
    outs = [s_grad["loss"].reshape(()), grad_x[None]]
    for which, small_o in enumerate((s_grad, s_delta, s_m, s_v)):
        for nm in order:
            outs.append(res[nm][which][None] if nm in res else small_o[nm])
    return tuple(outs)
```

```python
import math

import jax
import jax.numpy as jnp
from jax import lax
from jax.experimental import pallas as pl
from jax.experimental.pallas import tpu as pltpu

f32 = jnp.float32
bf16 = jnp.bfloat16

D_MODEL = 1024
HEAD_DIM = 64
N_HEADS = 8
N_PAIRS = 4
QKV_W = 2304
F_OFF = 2304
WINDOW = 128
ROPE_THETA = 10000.0
RMS_EPS = 1e-6
N_DEV = 8
ADAM_LR, ADAM_B1, ADAM_B2, ADAM_EPS, ADAM_WD, ADAM_STEP = 0.001, 0.9, 0.999, 1e-08, 0.01, 10
NEG = -1e30
L_ROW = (HEAD_DIM, 0)
LANES = 128
VMEM_LIMIT = 48 * 1024 * 1024
MESH = pl.DeviceIdType.MESH

_NT = (((1,), (1,)), ((), ()))
_TN = (((0,), (0,)), ((), ()))


def _params(n_grid=0):
    sem = ("arbitrary",) * n_grid if n_grid else None
    return pltpu.CompilerParams(dimension_semantics=sem, vmem_limit_bytes=VMEM_LIMIT)


def _row_tile(s, want):
    t = min(s, want)
    assert s % t == 0, (s, t)
    return t


MATMUL_VMEM_BUDGET = 40 * 1024 * 1024


def _matmul_tiles(m, n, k, a_item, b_item, o_item):
    def tiles(d):
        return [t for t in range(LANES, min(d, 2048) + 1, LANES) if d % t == 0] or [d]

    best = None
    for tm in tiles(m):
        for tn in tiles(n):
            vmem = 2 * (tm * k * a_item + tn * k * b_item + tm * tn * o_item) + tm * tn * 4
            if vmem > MATMUL_VMEM_BUDGET:
                continue
            traffic = m * k * a_item + n * k * b_item * (1 if tn == n else m // tm) + m * n * o_item
            steps = (m // tm) * (n // tn)
            key = (traffic, 0, steps) if steps >= 4 else (traffic, 1, -steps)
            if best is None or key < best[0]:
                best = (key, tm, tn)
    assert best is not None, (m, n, k)
    return best[1], best[2]


def _matmul(a, b, mode, out_dtype, name, after=None):
    if mode == "nn":
        (m, k), n = a.shape, b.shape[1]
    elif mode == "nt":
        (m, k), n = a.shape, b.shape[0]
    else:
        (k, m), n = a.shape, b.shape[1]
    tm, tn = _matmul_tiles(m, n, k, a.dtype.itemsize, b.dtype.itemsize, jnp.dtype(out_dtype).itemsize)
    if mode == "nn":
        a_spec, b_spec, dims = pl.BlockSpec((tm, k), lambda i, j: (i, 0)), pl.BlockSpec((k, tn), lambda i, j: (0, j)), None
    elif mode == "nt":
        a_spec, b_spec, dims = pl.BlockSpec((tm, k), lambda i, j: (i, 0)), pl.BlockSpec((tn, k), lambda i, j: (j, 0)), _NT
    else:
        a_spec, b_spec, dims = pl.BlockSpec((k, tm), lambda i, j: (0, i)), pl.BlockSpec((k, tn), lambda i, j: (0, j)), _TN

    def body(a_ref, b_ref, *rest):
        o_ref = rest[-1]
        av, bv = a_ref[...].astype(bf16), b_ref[...].astype(bf16)
        if dims is None:
            r = jnp.dot(av, bv, preferred_element_type=f32)
        else:
            r = lax.dot_general(av, bv, dims, preferred_element_type=f32)
        o_ref[...] = r.astype(out_dtype)

    extra = [] if after is None else [after]
    return pl.pallas_call(
        body, name=name, grid=(m // tm, n // tn), in_specs=[a_spec, b_spec] + [pl.BlockSpec(memory_space=pl.ANY)] * len(extra),
        out_specs=pl.BlockSpec((tm, tn), lambda i, j: (i, j)),
        out_shape=jax.ShapeDtypeStruct((m, n), out_dtype), compiler_params=_params(2),
    )(a, b, *extra)


def _rstd(v):
    return lax.rsqrt(jnp.mean(v * v, axis=-1, keepdims=True) + RMS_EPS)


def _row_spec(tm, d):
    return pl.BlockSpec((tm, d), lambda i: (i, 0))


def _vec_spec(d, rows=1):
    return pl.BlockSpec((rows, d), lambda i: (0, 0))


def _proj_spec(a, w, tm):
    return [_row_spec(tm, a.shape[1]), pl.BlockSpec(w.shape, lambda i: (0, 0))]


def _out_proj_postnorm_prenorm(a, w, x, g_post, gate, g_pre, scale, shift, name):
    s, d = x.shape
    tm = _row_tile(s, 512)

    def body(a_ref, w_ref, x_ref, gp_ref, gate_ref, g_ref, sc_ref, sh_ref, y_ref, x2_ref, h_ref):
        yv = jnp.dot(a_ref[...], w_ref[...], preferred_element_type=f32)
        y_ref[...] = yv
        x2 = x_ref[...] + gate_ref[...] * (yv * _rstd(yv) * gp_ref[...])
        x2_ref[...] = x2
        h_ref[...] = ((x2 * _rstd(x2) * g_ref[...]) * (1.0 + sc_ref[...]) + sh_ref[...]).astype(bf16)

    return pl.pallas_call(
        body, name=name, grid=(s // tm,), in_specs=_proj_spec(a, w, tm) + [_row_spec(tm, d)] + [_vec_spec(d)] * 5,
        out_specs=[_row_spec(tm, d)] * 3,
        out_shape=[jax.ShapeDtypeStruct((s, d), f32)] * 2 + [jax.ShapeDtypeStruct((s, d), bf16)], compiler_params=_params(1),
    )(a, w, x, g_post, gate, g_pre, scale, shift)


def _rms_bwd(u, v, r):
    return r * u - v * (r * r * r) * jnp.mean(u * v, axis=-1, keepdims=True)


def _out_proj_loss_tail(a, w, x, g, gate, target, name):
    s, d = x.shape
    tm = _row_tile(s, 512)

    def body(a_ref, w_ref, x_ref, g_ref, gate_ref, t_ref, loss_ref, do_ref, dy_ref, vec_ref):
        @pl.when(pl.program_id(0) == 0)
        def _():
            loss_ref[...] = jnp.zeros_like(loss_ref)
            vec_ref[...] = jnp.zeros_like(vec_ref)
        yv = jnp.dot(a_ref[...], w_ref[...], preferred_element_type=f32)
        r = _rstd(yv)
        yn = yv * r
        err = x_ref[...] + gate_ref[...] * (yn * g_ref[...]) - t_ref[...]
        loss_ref[...] += 0.5 * jnp.sum(jnp.mean(err * err, axis=-1, keepdims=True), axis=0, keepdims=True)
        dr = err / d
        do_ref[...] = dr
        dn = dr * gate_ref[...]
        vec_ref[0:1, :] += jnp.sum(dr * (yn * g_ref[...]), axis=0, keepdims=True)
        vec_ref[1:2, :] += jnp.sum(dn * yn, axis=0, keepdims=True)
        dy_ref[...] = _rms_bwd(dn * g_ref[...], yv, r).astype(bf16)

    return pl.pallas_call(
        body, name=name, grid=(s // tm,),
        in_specs=_proj_spec(a, w, tm) + [_row_spec(tm, d)] + [_vec_spec(d)] * 2 + [_row_spec(tm, d)],
        out_specs=[_vec_spec(LANES), _row_spec(tm, d), _row_spec(tm, d), _vec_spec(d, 8)],
        out_shape=[jax.ShapeDtypeStruct((1, LANES), f32), jax.ShapeDtypeStruct((s, d), f32),
                   jax.ShapeDtypeStruct((s, d), bf16), jax.ShapeDtypeStruct((8, d), f32)],
        compiler_params=_params(1),
    )(a, w, x, g, gate, target)


def _dgrad_prenorm_bwd(terms, x, g, scale, dres, name, after=None, below=None):
    s, d = x.shape
    n = len(terms)
    k = sum(a.shape[1] for a, _, _ in terms)
    row_bytes = 2 * (2 * k) + d * (4 + 2 * 4 * 3 + (2 * 4 + 2 * 2 if below else 0))
    tm = next(t for t in (512, 256, 128) if s % t == 0 and 4 * k * d + t * row_bytes <= MATMUL_VMEM_BUDGET)
    extra = [] if after is None else [after]

    def body(*refs):
        a_refs, b_refs = refs[:n], refs[n:2 * n]
        x_ref, g_ref, sc_ref, dr_ref = refs[2 * n:2 * n + 4]
        n_in = 2 * n + 4 + (3 if below else 0) + len(extra)
        dx_ref, vec_ref = refs[n_in], refs[n_in + 1]
        if below:
            y_ref, gp_ref, gate_ref = refs[2 * n + 4:2 * n + 7]
            dy_ref, vec2_ref = refs[n_in + 2], refs[n_in + 3]

        @pl.when(pl.program_id(0) == 0)
        def _():
            vec_ref[...] = jnp.zeros_like(vec_ref)
            if below:
                vec2_ref[...] = jnp.zeros_like(vec2_ref)
        dhv = jnp.dot(a_refs[0][...], b_refs[0][...], preferred_element_type=f32)
        for i in range(1, n):
            dhv = dhv + jnp.dot(a_refs[i][...], b_refs[i][...], preferred_element_type=f32)
        xv = x_ref[...]
        r = _rstd(xv)
        xn = xv * r
        dn = dhv * (1.0 + sc_ref[...])
        vec_ref[0:1, :] += jnp.sum(dhv, axis=0, keepdims=True)
        vec_ref[1:2, :] += jnp.sum(dhv * (xn * g_ref[...]), axis=0, keepdims=True)
        vec_ref[2:3, :] += jnp.sum(dn * xn, axis=0, keepdims=True)
        dx = dr_ref[...] + _rms_bwd(dn * g_ref[...], xv, r)
        dx_ref[...] = dx
        if below:
            yv = y_ref[...]
            ry = _rstd(yv)
            yn = yv * ry
            dny = dx * gate_ref[...]
            vec2_ref[0:1, :] += jnp.sum(dx * (yn * gp_ref[...]), axis=0, keepdims=True)
            vec2_ref[1:2, :] += jnp.sum(dny * yn, axis=0, keepdims=True)
            dy_ref[...] = _rms_bwd(dny * gp_ref[...], yv, ry).astype(bf16)

    in_specs = ([_row_spec(tm, a.shape[1]) for a, _, _ in terms]
                + [pl.BlockSpec((a.shape[1], d), lambda i, r=r: (r, 0)) for a, _, r in terms]
                + [_row_spec(tm, d)] + [_vec_spec(d)] * 2 + [_row_spec(tm, d)])
    out_specs = [_row_spec(tm, d), _vec_spec(d, 8)]
    out_shape = [jax.ShapeDtypeStruct((s, d), f32), jax.ShapeDtypeStruct((8, d), f32)]
    args = [a for a, _, _ in terms] + [b for _, b, _ in terms] + [x, g, scale, dres]
    if below:
        in_specs += [_row_spec(tm, d)] + [_vec_spec(d)] * 2
        out_specs += [_row_spec(tm, d), _vec_spec(d, 8)]
        out_shape += [jax.ShapeDtypeStruct((s, d), bf16), jax.ShapeDtypeStruct((8, d), f32)]
        args += list(below)
    return pl.pallas_call(
        body, name=name, grid=(s // tm,), in_specs=in_specs + [pl.BlockSpec(memory_space=pl.ANY)] * len(extra),
        out_specs=out_specs, out_shape=out_shape, compiler_params=_params(1),
    )(*args, *extra)


def _lane():
    return lax.broadcasted_iota(jnp.int32, (1, LANES), 1)


def _rope_tables(pos_col, inv_freq, name):
    s = pos_col.shape[0]

    def body(p_ref, f_ref, cos_ref, sin_ref):
        ang = p_ref[...].astype(f32) * f_ref[...]
        first_half = (_lane() % HEAD_DIM) < HEAD_DIM // 2
        cos_ref[...] = jnp.cos(ang)
        sn = jnp.sin(ang)
        sin_ref[...] = jnp.where(first_half, -sn, sn)

    return pl.pallas_call(
        body, name=name, out_shape=[jax.ShapeDtypeStruct((s, LANES), f32)] * 2, compiler_params=_params(),
    )(pos_col, inv_freq)


def _swap_halves(v):
    first_half = (_lane() % HEAD_DIM) < HEAD_DIM // 2
    return jnp.where(first_half, pltpu.roll(v, LANES - HEAD_DIM // 2, axis=1), pltpu.roll(v, HEAD_DIM // 2, axis=1))


def _prenorm_proj_qkv(x, g, mod_scale, mod_shift, w_qkv_t, cos, sin_s, name):
    s, d = x.shape
    tm = _row_tile(s, 512)
    scale = 1.0 / math.sqrt(HEAD_DIM)

    def body(x_ref, g_ref, msc_ref, msh_ref, w_ref, c_ref, s_ref, h_ref, qa_ref, ka_ref, va_ref, qb_ref, kb_ref, vb_ref):
        xv = x_ref[...]
        h = ((xv * _rstd(xv) * g_ref[...]) * (1.0 + msc_ref[...]) + msh_ref[...]).astype(bf16)
        h_ref[...] = h
        proj = lax.dot_general(h, w_ref[...], _NT, preferred_element_type=f32)
        cs, sn = c_ref[...], s_ref[...]
        low = _lane() < HEAD_DIM

        def blk(j):
            return proj[:, j * LANES:(j + 1) * LANES]

        def rope(v):
            return v * cs + _swap_halves(v) * sn

        def expand(v):
            other = pltpu.roll(v, HEAD_DIM, axis=1)
            return jnp.where(low, v, other), jnp.where(low, other, v)

        for j in range(N_PAIRS):
            qa_ref[:, j * LANES:(j + 1) * LANES] = (rope(blk(j)) * scale).astype(bf16)
            qb_ref[:, j * LANES:(j + 1) * LANES] = (blk(6 + j) * scale).astype(bf16)
            kb_ref[:, j * LANES:(j + 1) * LANES] = blk(10 + j).astype(bf16)
            vb_ref[:, j * LANES:(j + 1) * LANES] = blk(14 + j).astype(bf16)
        k0, k1 = expand(rope(blk(4)))
        v0, v1 = expand(blk(5))
        for j in range(N_PAIRS):
            ka_ref[:, j * LANES:(j + 1) * LANES] = (k0 if j < 2 else k1).astype(bf16)
            va_ref[:, j * LANES:(j + 1) * LANES] = (v0 if j < 2 else v1).astype(bf16)

    hw = N_PAIRS * LANES
    return pl.pallas_call(
        body, name=name, grid=(s // tm,),
        in_specs=[_row_spec(tm, d)] + [_vec_spec(d)] * 3
        + [pl.BlockSpec((QKV_W, d), lambda i: (0, 0)), _row_spec(tm, LANES), _row_spec(tm, LANES)],
        out_specs=[_row_spec(tm, d)] + [_row_spec(tm, hw)] * 6,
        out_shape=[jax.ShapeDtypeStruct((s, d), bf16)] + [jax.ShapeDtypeStruct((s, hw), bf16)] * 6, compiler_params=_params(1),
    )(x, g, mod_scale, mod_shift, w_qkv_t, cos, sin_s)


def _qkv_prep_bwd(dqa_t, dka, dva, dqb_t, dkb, dvb, cos, sin_s, name):
    s = dka.shape[0]
    tm = _row_tile(s, 256)
    scale = 1.0 / math.sqrt(HEAD_DIM)
    hw = N_PAIRS * LANES
    t_spec = pl.BlockSpec((hw, tm), lambda i: (0, i))

    def body(dqa_ref, dka_ref, dva_ref, dqb_ref, dkb_ref, dvb_ref, c_ref, s_ref, o_ref):
        cs, sn = c_ref[...], s_ref[...]
        low = _lane() < HEAD_DIM

        def blk(ref, j):
            return ref[:, j * LANES:(j + 1) * LANES].astype(f32)

        def blk_t(ref, j):
            return ref[j * LANES:(j + 1) * LANES, :].T

        def unrope(v):
            return v * cs + _swap_halves(v * sn)

        def fold(ref):
            a, b = blk(ref, 0) + blk(ref, 1), blk(ref, 2) + blk(ref, 3)
            kv0 = a + pltpu.roll(a, HEAD_DIM, axis=1)
            kv1 = b + pltpu.roll(b, HEAD_DIM, axis=1)
            return jnp.where(low, kv0, kv1)

        for j in range(N_PAIRS):
            o_ref[:, j * LANES:(j + 1) * LANES] = (unrope(blk_t(dqa_ref, j)) * scale).astype(bf16)
            o_ref[:, (6 + j) * LANES:(7 + j) * LANES] = (blk_t(dqb_ref, j) * scale).astype(bf16)
            o_ref[:, (10 + j) * LANES:(11 + j) * LANES] = blk(dkb_ref, j).astype(bf16)
            o_ref[:, (14 + j) * LANES:(15 + j) * LANES] = blk(dvb_ref, j).astype(bf16)
        o_ref[:, 4 * LANES:5 * LANES] = unrope(fold(dka_ref)).astype(bf16)
        o_ref[:, 5 * LANES:6 * LANES] = fold(dva_ref).astype(bf16)

    return pl.pallas_call(
        body, name=name, grid=(s // tm,),
        in_specs=[t_spec, _row_spec(tm, hw), _row_spec(tm, hw), t_spec, _row_spec(tm, hw), _row_spec(tm, hw)] + [_row_spec(tm, LANES)] * 2,
        out_specs=_row_spec(tm, QKV_W), out_shape=jax.ShapeDtypeStruct((s, QKV_W), bf16), compiler_params=_params(1),
    )(dqa_t, dka, dva, dqb_t, dkb, dvb, cos, sin_s)


def _cumsum_rows(v, reverse=False):
    n = v.shape[0]
    row = lax.broadcasted_iota(jnp.int32, v.shape, 0)
    sh = 1
    while sh < n:
        if reverse:
            v = v + jnp.where(row < n - sh, pltpu.roll(v, n - sh, axis=0), 0.0)
        else:
            v = v + jnp.where(row >= sh, pltpu.roll(v, sh, axis=0), 0.0)
        sh *= 2
    return v


def _log_sigmoid(z):
    return jnp.minimum(z, 0.0) - jnp.log1p(jnp.exp(-jnp.abs(z)))


def _forget_prep(h, w_f_t, bf_row, name):
    s, d = h.shape
    tm = _row_tile(s, 1024)

    def body(h_ref, w_ref, b_ref, f_ref, cb_ref, last_ref):
        @pl.when(pl.program_id(0) == 0)
        def _():
            last_ref[...] = jnp.zeros_like(last_ref)
        fl = lax.dot_general(h_ref[...], w_ref[...], _NT, preferred_element_type=f32)
        f_ref[...] = fl
        cum = _cumsum_rows(_log_sigmoid(fl + b_ref[...])) + last_ref[0:1, :]
        last_ref[0:1, :] = cum[tm - 1:tm, :]
        for hd in range(N_HEADS):
            cb_ref[:, hd * LANES:(hd + 1) * LANES] = jnp.broadcast_to(cum[:, hd:hd + 1], (tm, LANES))

    return pl.pallas_call(
        body, name=name, grid=(s // tm,),
        in_specs=[_row_spec(tm, d), pl.BlockSpec((LANES, d), lambda i: (0, 0)), _vec_spec(LANES)],
        out_specs=[_row_spec(tm, LANES), _row_spec(tm, N_HEADS * LANES)],
        out_shape=[jax.ShapeDtypeStruct((s, LANES), f32), jax.ShapeDtypeStruct((s, N_HEADS * LANES), f32)],
        scratch_shapes=[pltpu.VMEM((8, LANES), f32)], compiler_params=_params(1),
    )(h, w_f_t, bf_row)


def _forget_prep_bwd(rs, dcs, fl, bf_row, name):
    s = fl.shape[0]
    tm = _row_tile(s, 1024)
    n = s // tm

    def body(r_ref, c_ref, f_ref, b_ref, df_ref, db_ref, next_ref):
        @pl.when(pl.program_id(0) == 0)
        def _():
            next_ref[...] = jnp.zeros_like(next_ref)
            db_ref[...] = jnp.zeros_like(db_ref)
        eye = (lax.broadcasted_iota(jnp.int32, (N_HEADS, LANES), 0) == lax.broadcasted_iota(jnp.int32, (N_HEADS, LANES), 1)).astype(f32)
        dcum = lax.dot_general(r_ref[...], eye, _TN, precision=lax.Precision.HIGHEST, preferred_element_type=f32)
        for h in range(N_HEADS):
            dcum = dcum - jnp.where(_lane() == h, jnp.sum(c_ref[:, h * LANES:(h + 1) * LANES], axis=1, keepdims=True), 0.0)
        dlf = _cumsum_rows(dcum, reverse=True) + next_ref[0:1, :]
        next_ref[0:1, :] = dlf[0:1, :]
        z = f_ref[...] + b_ref[...]
        df = jnp.where(_lane() < N_HEADS, dlf * jax.nn.sigmoid(-z), 0.0)
        df_ref[...] = df.astype(bf16)
        db_ref[0:1, :] += jnp.sum(df, axis=0, keepdims=True)

    def rows(width):
        return pl.BlockSpec((tm, width), lambda i: (n - 1 - i, 0))

    return pl.pallas_call(
        body, name=name, grid=(n,),
        in_specs=[pl.BlockSpec((N_HEADS, tm), lambda i: (0, n - 1 - i)), rows(N_HEADS * LANES), rows(LANES), _vec_spec(LANES)],
        out_specs=[rows(LANES), _vec_spec(LANES, 8)],
        out_shape=[jax.ShapeDtypeStruct((s, LANES), bf16), jax.ShapeDtypeStruct((8, LANES), f32)],
        scratch_shapes=[pltpu.VMEM((8, LANES), f32)], compiler_params=_params(1),
    )(rs, dcs, fl, bf_row)


def _tile_mask(n_keys, n_queries, off, window):
    shape = (n_keys, n_queries)
    d = lax.broadcasted_iota(jnp.int32, shape, 1) - lax.broadcasted_iota(jnp.int32, shape, 0) + off
    valid = d >= 0
    return jnp.logical_and(valid, d < window) if window else valid


def _wide(v, t):
    return jnp.concatenate([v] * (t // LANES), axis=1)


def _attn_fwd(q, k, v, name, *, cum_b=None, sink_rows=None, window=None, t=256):
    s = q.shape[0]
    t = _row_tile(s, t)
    fox, has_sink = cum_b is not None, sink_rows is not None
    assert not window or (window % LANES == 0 and LANES + window <= s)

    def body(*refs):
        q_ref, k_ref, v_ref = refs[:3]
        rest = list(refs[3:])
        cb_ref = rest.pop(0) if fox else None
        sink_ref = rest.pop(0) if has_sink else None
        o_ref, lse_ref = rest
        i = pl.program_id(1)
        low = _lane() < HEAD_DIM
        top = lax.broadcasted_iota(jnp.int32, (LANES, 1), 0) < HEAD_DIM
        q2 = q_ref[...]
        zero = jnp.zeros_like(q2)
        qms = (jnp.where(low, q2, zero), jnp.where(low, zero, q2))

        def tile(k0, n_keys, off, carry, masked, queries=slice(0, t)):
            nq = queries.stop - queries.start
            kblk, vblk = k_ref[pl.ds(k0, n_keys), :], v_ref[pl.ds(k0, n_keys), :]
            valid = _tile_mask(n_keys, nq, off, window) if masked else None
            ones = jnp.ones_like(vblk)
            vs = tuple(jnp.where(_lane() == L_ROW[h], ones, vblk) for h in range(2))

            def scores(h):
                return lax.dot_general(kblk, qms[h][queries], _NT, preferred_element_type=f32)

            def softmax(h, sc):
                m = carry[h][0]
                if fox:
                    sc = sc - _wide(cb_ref[pl.ds(k0, n_keys), h * LANES:(h + 1) * LANES], nq)
                if masked:
                    sc = jnp.where(valid, sc, NEG)
                m_new = jnp.maximum(m, jnp.max(sc, axis=0, keepdims=True))
                return m_new, jnp.exp(m - m_new), jnp.exp(sc - m_new).astype(bf16)

            def update(h, m_new, alpha, p):
                return m_new, alpha * carry[h][1] + lax.dot_general(vs[h], p, _TN, preferred_element_type=f32)

            if window:
                return tuple(update(h, *softmax(h, scores(h))) for h in range(2))
            scs = [scores(h) for h in range(2)]
            stats = [softmax(h, scs[h]) for h in range(2)]
            return tuple(update(h, *stats[h]) for h in range(2))

        def start(nq):
            if has_sink:
                row = lax.broadcasted_iota(jnp.int32, (LANES, nq), 0)
                return tuple((_wide(sink_ref[h:h + 1, :], nq), (row == L_ROW[h]).astype(f32)) for h in range(2))
            return tuple((jnp.full((1, nq), NEG, f32), jnp.zeros((LANES, nq), f32)) for h in range(2))

        def finish(carry, queries):
            (m0, a0), (m1, a1) = carry
            l0, l1 = a0[L_ROW[0]:L_ROW[0] + 1, :], a1[L_ROW[1]:L_ROW[1] + 1, :]
            o_t = jnp.where(top, a0 * (1.0 / l0), a1 * (1.0 / l1))
            o_ref[queries, :] = o_t.T.astype(bf16)
            lse_ref[0:1, queries] = m0 + jnp.log(l0)
            lse_ref[1:2, queries] = m1 + jnp.log(l1)

        if window:
            for c in range(t // LANES):
                queries = slice(c * LANES, (c + 1) * LANES)
                q0 = i * t + c * LANES
                k0 = pl.multiple_of(jnp.maximum(q0 - window, 0), LANES)
                finish(tile(k0, LANES + window, q0 - k0, start(LANES), True, queries), queries)
        else:
            carry = lax.fori_loop(0, i, lambda kb, c: tile(pl.multiple_of(kb * t, t), t, 0, c, False), start(t))
            half, k_own = t // 2, pl.multiple_of(i * t, t)
            carry = tile(k_own, half, 0, carry, True)
            finish(tuple((m[:, :half], a[:, :half]) for m, a in carry), slice(0, half))
            carry = tuple((m[:, half:], a[:, half:]) for m, a in carry)
            finish(tile(pl.multiple_of(k_own + half, half), half, 0, carry, True, slice(half, t)), slice(half, t))

    q_spec = pl.BlockSpec((t, LANES), lambda j, i: (i, j))
    kv_spec = pl.BlockSpec((s, LANES), lambda j, i: (0, j))
    in_specs, args = [q_spec, kv_spec, kv_spec], [q, k, v]
    if fox:
        in_specs += [pl.BlockSpec((s, 2 * LANES), lambda j, i: (0, j))]
        args += [cum_b]
    if has_sink:
        in_specs += [pl.BlockSpec((None, 2, LANES), lambda j, i: (j, 0, 0))]
        args += [sink_rows.reshape(N_PAIRS, 2, LANES)]
    return pl.pallas_call(
        body, name=name, grid=(N_PAIRS, s // t), in_specs=in_specs,
        out_specs=[q_spec, pl.BlockSpec((None, 2, t), lambda j, i: (j, 0, i))],
        out_shape=[jax.ShapeDtypeStruct((s, N_PAIRS * LANES), bf16), jax.ShapeDtypeStruct((N_PAIRS, 2, s), f32)],
        compiler_params=_params(2),
    )(*args)


def _branch_dgrad_delta(db, w, o, name, *, lse=None, sink_rows=None, after=None):
    s, hw = o.shape
    tm = _row_tile(s, 512)
    has_sink = sink_rows is not None
    extra = [] if after is None else [after]

    def body(*refs):
        db_ref, w_ref, o_ref = refs[:3]
        outs = refs[3 + (2 if has_sink else 0) + len(extra):]
        do_ref, dl_ref = outs[:2]
        if has_sink:
            lse_ref, sink_ref = refs[3:5]
            ds_ref = outs[2]

            @pl.when(pl.program_id(0) == 0)
            def _():
                ds_ref[...] = jnp.zeros_like(ds_ref)
        do = lax.dot_general(db_ref[...], w_ref[...], _NT, preferred_element_type=f32).astype(bf16)
        do_ref[...] = do
        for j in range(N_PAIRS):
            cols = slice(j * LANES, (j + 1) * LANES)
            prod_t = (do[:, cols].astype(f32) * o_ref[:, cols].astype(f32)).T
            for h in range(2):
                dl = jnp.sum(prod_t[h * HEAD_DIM:(h + 1) * HEAD_DIM, :], axis=0, keepdims=True)
                dl_ref[j, h:h + 1, :] = dl
                if has_sink:
                    r = 2 * j + h
                    p_sink = jnp.exp(sink_ref[r:r + 1, 0:1] - lse_ref[j, h:h + 1, :])
                    ds_ref[r:r + 1, :] += -jnp.sum(p_sink * dl, axis=1, keepdims=True)

    rows_spec = pl.BlockSpec((N_PAIRS, 2, tm), lambda i: (0, 0, i))
    in_specs = [_row_spec(tm, db.shape[1]), pl.BlockSpec(w.shape, lambda i: (0, 0)), _row_spec(tm, hw)]
    args = [db, w, o]
    out_specs = [_row_spec(tm, hw), rows_spec]
    out_shape = [jax.ShapeDtypeStruct((s, hw), bf16), jax.ShapeDtypeStruct((N_PAIRS, 2, s), f32)]
    if has_sink:
        in_specs += [rows_spec, _vec_spec(LANES, N_HEADS)]
        args += [lse, sink_rows]
        out_specs += [_vec_spec(LANES, N_HEADS)]
        out_shape += [jax.ShapeDtypeStruct((N_HEADS, LANES), f32)]
    return pl.pallas_call(
        body, name=name, grid=(s // tm,), in_specs=in_specs + [pl.BlockSpec(memory_space=pl.ANY)] * len(extra),
        out_specs=out_specs, out_shape=out_shape, compiler_params=_params(1),
    )(*args, *extra)


def _attn_bwd(q, k, v, do, lse, delta, name, *, cum_b=None, window=None, t=256):
    s = q.shape[0]
    t = _row_tile(s, t)
    nblk = s // t
    fox = cum_b is not None
    assert not window or (window % LANES == 0 and LANES + window <= s)

    def body(*refs):
        k_ref, v_ref, q_ref, do_ref, lse_ref, dl_ref = refs[:6]
        rest = list(refs[6:])
        cb_ref = rest.pop(0) if fox else None
        dq_ref, dk_ref, dv_ref = rest[:3]
        dcs_ref, rs_ref = (rest[3], rest[4]) if fox else (None, None)
        dk_acc, dv_acc = rest[-2:]
        b = pl.program_id(1)
        k0 = pl.multiple_of(b * t, t)

        @pl.when(b == 0)
        def _():
            dq_ref[...] = jnp.zeros_like(dq_ref)
            if fox:
                rs_ref[...] = jnp.zeros_like(rs_ref)

        dk_acc[...] = jnp.zeros_like(dk_acc)
        dv_acc[...] = jnp.zeros_like(dv_acc)
        if fox:
            dcs_ref[...] = jnp.zeros_like(dcs_ref)
        low = _lane() < HEAD_DIM
        top = lax.broadcasted_iota(jnp.int32, (LANES, 1), 0) < HEAD_DIM
        kblk, vblk = k_ref[...], v_ref[...]
        k_t = kblk.astype(f32).T.astype(bf16)
        cks = [_wide(cb_ref[pl.ds(k0, t), h * LANES:(h + 1) * LANES], t) for h in range(2)] if fox else None

        def tile(q0, n_queries, off, masked, keys=slice(0, t)):
            cols = pl.ds(q0, n_queries)
            q2, do2 = q_ref[cols, :], do_ref[cols, :]
            zero = jnp.zeros_like(q2)
            valid = _tile_mask(keys.stop - keys.start, n_queries, off, window) if masked else None
            dq_parts = []
            for h in range(2):
                qm = jnp.where(low, q2, zero) if h == 0 else jnp.where(low, zero, q2)
                dom = jnp.where(low, do2, zero) if h == 0 else jnp.where(low, zero, do2)
                sc = lax.dot_general(kblk[keys], qm, _NT, preferred_element_type=f32)
                if fox:
                    sc = sc - cks[h][keys, :n_queries]
                if masked:
                    sc = jnp.where(valid, sc, NEG)
                p = jnp.exp(sc - lse_ref[h:h + 1, cols])
                dp = lax.dot_general(vblk[keys], dom, _NT, preferred_element_type=f32)
                ds = p * (dp - dl_ref[h:h + 1, cols])
                pb, dsb = p.astype(bf16), ds.astype(bf16)
                dv_acc[keys, :] += jnp.dot(pb, dom, preferred_element_type=f32)
                dk_acc[keys, :] += jnp.dot(dsb, qm, preferred_element_type=f32)
                dq_parts.append(jnp.dot(k_t[:, keys], dsb, preferred_element_type=f32))
                if fox:
                    dcs_ref[keys, h * LANES:(h + 1) * LANES] += sum(ds[:, g * LANES:(g + 1) * LANES]
                                                                    for g in range(n_queries // LANES))
                    rs_ref[h:h + 1, cols] += jnp.sum(ds, axis=0, keepdims=True)
            dq_ref[:, cols] += jnp.where(top, dq_parts[0], dq_parts[1])

        def later_block(qb, carry):
            tile(pl.multiple_of(qb * t, t), t, 0, False)
            return carry

        if window:
            for c in range(t // LANES):
                first = b * t + c * LANES
                q0 = pl.multiple_of(jnp.minimum(first, s - (LANES + window)), LANES)
                tile(q0, LANES + window, q0 - first, True, slice(c * LANES, (c + 1) * LANES))
        else:
            half = t // 2
            tile(k0, half, 0, True, slice(0, half))
            tile(pl.multiple_of(k0 + half, half), half, half, True)
            lax.fori_loop(b + 1, nblk, later_block, 0)
        dk_ref[...] = dk_acc[...].astype(bf16)
        dv_ref[...] = dv_acc[...].astype(bf16)

    kv_spec = pl.BlockSpec((t, LANES), lambda j, b: (b, j))
    seq_spec = pl.BlockSpec((s, LANES), lambda j, b: (0, j))
    rows_spec = pl.BlockSpec((None, 2, s), lambda j, b: (j, 0, 0))
    hw = N_PAIRS * LANES
    in_specs, args = [kv_spec, kv_spec, seq_spec, seq_spec, rows_spec, rows_spec], [k, v, q, do, lse, delta]
    out_specs = [pl.BlockSpec((LANES, s), lambda j, b: (j, 0)), kv_spec, kv_spec]
    out_shape = [jax.ShapeDtypeStruct((hw, s), f32), jax.ShapeDtypeStruct((s, hw), bf16), jax.ShapeDtypeStruct((s, hw), bf16)]
    if fox:
        in_specs += [pl.BlockSpec((s, 2 * LANES), lambda j, b: (0, j))]
        args += [cum_b]
        out_specs += [pl.BlockSpec((t, 2 * LANES), lambda j, b: (b, j)), rows_spec]
        out_shape += [jax.ShapeDtypeStruct((s, N_HEADS * LANES), f32), jax.ShapeDtypeStruct((N_PAIRS, 2, s), f32)]
    return pl.pallas_call(
        body, name=name, grid=(N_PAIRS, nblk), in_specs=in_specs, out_specs=out_specs, out_shape=out_shape,
        scratch_shapes=[pltpu.VMEM((t, LANES), f32)] * 2, compiler_params=_params(2),
    )(*args)


def _branch_merge(o_a, o_b, w_a, w_b, gl, name):
    s, k = o_a.shape
    d = w_a.shape[1]
    tm = _row_tile(s, 1024)

    def body(oa_ref, ob_ref, wa_ref, wb_ref, g_ref, ba_ref, bb_ref, m_ref):
        ba = jnp.dot(oa_ref[...], wa_ref[...], preferred_element_type=f32)
        bb = jnp.dot(ob_ref[...], wb_ref[...], preferred_element_type=f32)
        g0, g1 = jax.nn.sigmoid(g_ref[:, :d].astype(f32)), jax.nn.sigmoid(g_ref[:, d:].astype(f32))
        ba_ref[...] = ba.astype(bf16)
        bb_ref[...] = bb.astype(bf16)
        m_ref[...] = (g0 * ba + g1 * bb).astype(bf16)

    whole = pl.BlockSpec((k, d), lambda i: (0, 0))
    return pl.pallas_call(
        body, name=name, grid=(s // tm,),
        in_specs=[_row_spec(tm, k), _row_spec(tm, k), whole, whole, _row_spec(tm, 2 * d)],
        out_specs=[_row_spec(tm, d)] * 3, out_shape=[jax.ShapeDtypeStruct((s, d), bf16)] * 3, compiler_params=_params(1),
    )(o_a, o_b, w_a, w_b, gl)


def _out_dgrad_merge_bwd(dy, w_out, ba, bb, gl, name):
    s, d = ba.shape
    tm = _row_tile(s, 512)

    def body(dy_ref, w_ref, a_ref, b_ref, g_ref, da_ref, db_ref, dg_ref):
        dmv = lax.dot_general(dy_ref[...], w_ref[...], _NT, preferred_element_type=f32)
        g0, g1 = jax.nn.sigmoid(g_ref[:, :d].astype(f32)), jax.nn.sigmoid(g_ref[:, d:].astype(f32))
        da_ref[...] = (dmv * g0).astype(bf16)
        db_ref[...] = (dmv * g1).astype(bf16)
        dg_ref[:, :d] = (dmv * a_ref[...].astype(f32) * (g0 * (1.0 - g0))).astype(bf16)
        dg_ref[:, d:] = (dmv * b_ref[...].astype(f32) * (g1 * (1.0 - g1))).astype(bf16)

    return pl.pallas_call(
        body, name=name, grid=(s // tm,),
        in_specs=[_row_spec(tm, dy.shape[1]), pl.BlockSpec(w_out.shape, lambda i: (0, 0))] + [_row_spec(tm, d)] * 2
        + [_row_spec(tm, 2 * d)],
        out_specs=[_row_spec(tm, d)] * 2 + [_row_spec(tm, 2 * d)],
        out_shape=[jax.ShapeDtypeStruct((s, d), bf16)] * 2 + [jax.ShapeDtypeStruct((s, 2 * d), bf16)],
        compiler_params=_params(1),
    )(dy, w_out, ba, bb, gl)


GLU_TILE = 256


def _ffn_in_swiglu(h, w_t, name):
    s, d = h.shape
    f = w_t.shape[0] // 2
    tm = _row_tile(s, 2048)
    tg = GLU_TILE
    nb = f // tg

    def body(h_ref, wg_ref, wu_ref, g_ref, u_ref, act_ref):
        hv = h_ref[...]
        g = lax.dot_general(hv, wg_ref[...], _NT, preferred_element_type=f32)
        u = lax.dot_general(hv, wu_ref[...], _NT, preferred_element_type=f32)
        g_ref[...] = g.astype(bf16)
        u_ref[...] = u.astype(bf16)
        act_ref[...] = (g * jax.nn.sigmoid(g) * u).astype(bf16)

    col = pl.BlockSpec((tm, tg), lambda i, j: (i, j))
    return pl.pallas_call(
        body, name=name, grid=(s // tm, nb),
        in_specs=[pl.BlockSpec((tm, d), lambda i, j: (i, 0)), pl.BlockSpec((tg, d), lambda i, j: (j, 0)),
                  pl.BlockSpec((tg, d), lambda i, j: (j + nb, 0))],
        out_specs=[col] * 3, out_shape=[jax.ShapeDtypeStruct((s, f), bf16)] * 3, compiler_params=_params(2),
    )(h, w_t, w_t)


def _ffn_out_dgrad_swiglu(dy, w_out, g, u, name):
    s, d = dy.shape
    f = g.shape[1]
    tm = _row_tile(s, 2048)
    tg = GLU_TILE

    def body(dy_ref, w_ref, g_ref, u_ref, dg_ref, du_ref):
        dv = lax.dot_general(dy_ref[...], w_ref[...], _NT, preferred_element_type=f32)
        gv, uv = g_ref[...].astype(f32), u_ref[...].astype(f32)
        sg = jax.nn.sigmoid(gv)
        dg_ref[...] = (dv * uv * (sg * (1.0 + gv * (1.0 - sg)))).astype(bf16)
        du_ref[...] = (dv * (gv * sg)).astype(bf16)

    col = pl.BlockSpec((tm, tg), lambda i, j: (i, j))
    return pl.pallas_call(
        body, name=name, grid=(s // tm, f // tg),
        in_specs=[pl.BlockSpec((tm, d), lambda i, j: (i, 0)), pl.BlockSpec((tg, d), lambda i, j: (j, 0)), col, col],
        out_specs=[col] * 2, out_shape=[jax.ShapeDtypeStruct((s, f), bf16)] * 2, compiler_params=_params(2),
    )(dy, w_out, g, u)


def _wgrad_stack(parts, h, name):
    s, m = parts[0].shape
    d = h.shape[1]
    tm = 256
    nb = m // tm
    n = len(parts)

    def body(*refs):
        i = pl.program_id(0)
        for p in range(n):
            @pl.when(i // nb == p)
            def _(p=p):
                refs[n + 1][...] = lax.dot_general(refs[p][...], refs[n][...], _TN, preferred_element_type=f32).astype(bf16)

    a_specs = [pl.BlockSpec((s, tm), lambda i, p=p: (0, jnp.clip(i - p * nb, 0, nb - 1))) for p in range(n)]
    return pl.pallas_call(
        body, name=name, grid=(n * nb,), in_specs=a_specs + [pl.BlockSpec((s, d), lambda i: (0, 0))],
        out_specs=pl.BlockSpec((tm, d), lambda i: (i, 0)),
        out_shape=jax.ShapeDtypeStruct((n * m, d), bf16), compiler_params=_params(1),
    )(*parts, h)


def _ada_wgrad(c_all, d_all, name):
    n, d = c_all.shape
    w = d_all.shape[1]

    def body(c_ref, d_ref, o_ref):
        eye = (lax.broadcasted_iota(jnp.int32, (n, n), 0) == lax.broadcasted_iota(jnp.int32, (n, n), 1)).astype(f32)
        ct = lax.dot_general(c_ref[...], eye, _TN, precision=lax.Precision.HIGHEST, preferred_element_type=f32)
        g = ct[:, 0:1] * d_ref[0:1, :]
        for bi in range(1, n):
            g = g + ct[:, bi:bi + 1] * d_ref[bi:bi + 1, :]
        o_ref[0] = g

    return pl.pallas_call(
        body, name=name, out_shape=jax.ShapeDtypeStruct((1, d, w), f32), compiler_params=_params(),
    )(c_all, d_all)


def _adamw(parts, w, m, v, name, mine=None, me=None):
    r, c = w.shape
    n_parts = parts.shape[0]
    row_tiles = [t for t in range(min(r, 256), 0, -1) if r % t == 0 and (t % 16 == 0 or t == r)]
    if row_tiles:
        tr, tc = row_tiles[0], c
    else:
        tr, tc = r, next(t for t in (256, LANES) if c % t == 0)

    def body(*refs):
        w_ref, m_ref, v_ref, g_ref, d_ref, nm_ref, nv_ref = refs[-7:]
        if mine is None:
            p_ref, = refs[:-7]
        else:
            me_ref, p_ref, own_ref = refs[:-7]

        def part(i):
            if mine is None:
                return p_ref[i].astype(f32)
            return jnp.where(me_ref[0] == i, own_ref[...], p_ref[i]).astype(f32)

        g = part(0)
        for i in range(1, n_parts):
            g = g + part(i)
        mm = ADAM_B1 * m_ref[...] + (1.0 - ADAM_B1) * g
        vv = ADAM_B2 * v_ref[...] + (1.0 - ADAM_B2) * (g * g)
        m_hat = mm / (1.0 - ADAM_B1 ** ADAM_STEP)
        v_hat = vv / (1.0 - ADAM_B2 ** ADAM_STEP)
        g_ref[...] = g
        d_ref[...] = -ADAM_LR * (m_hat / (jnp.sqrt(v_hat) + ADAM_EPS) + ADAM_WD * w_ref[...])
        nm_ref[...] = mm
        nv_ref[...] = vv

    out_shape = [jax.ShapeDtypeStruct((r, c), f32)] * 4
    if mine is None:
        spec = pl.BlockSpec((tr, tc), lambda i, j: (i, j))
        return pl.pallas_call(
            body, name=name, grid=(r // tr, c // tc),
            in_specs=[pl.BlockSpec((n_parts, tr, tc), lambda i, j: (0, i, j))] + [spec] * 3,
            out_specs=[spec] * 4, out_shape=out_shape, compiler_params=_params(2),
        )(parts, w, m, v)
    spec = pl.BlockSpec((tr, tc), lambda i, j, me_ref: (i, j))
    return pl.pallas_call(
        body, name=name, out_shape=out_shape, compiler_params=_params(2),
        grid_spec=pltpu.PrefetchScalarGridSpec(
            num_scalar_prefetch=1, grid=(r // tr, c // tc),
            in_specs=[pl.BlockSpec((n_parts, tr, tc), lambda i, j, me_ref: (0, i, j)),
                      pl.BlockSpec((None, tr, tc), lambda i, j, me_ref: (me_ref[0], i, j))] + [spec] * 3,
            out_specs=[spec] * 4),
    )(me, parts, mine, w, m, v)


def _me():
    return lax.axis_index("x"), lax.axis_index("y"), lax.axis_index("c")


def _gather_prologue(c, w_ada, b_mine, w_in_t, name):
    n_dev, d = N_DEV, c.shape[1]
    ada_w = w_ada.shape[1]

    def body(c_ref, w_ref, b_ref, win_ref, call_ref, ada_ref, gin_ref, cols_ref, send_sems, recv_sems, local_sems):
        x, y, cc = _me()
        me, sibling = (x, y, cc), (x, y, 1 - cc)
        chips = [(1 - x, y), (x, 1 - y), (1 - x, 1 - y)]
        outs = (call_ref, ada_ref, gin_ref)

        def rows(a, dev):
            return outs[a].at[4 * dev[0] + 2 * dev[1] + dev[2]]

        def copy(a, k, block, to, src=None):
            return pltpu.make_async_remote_copy(
                src_ref=rows(a, block) if src is None else src, dst_ref=rows(a, block),
                send_sem=send_sems.at[a, k], recv_sem=recv_sems.at[a, k], device_id=to, device_id_type=MESH)

        def begin(a, src):
            own = pltpu.make_async_copy(src, rows(a, me), local_sems.at[a])
            sends = [copy(a, 0, me, sibling, src=src)] + [copy(a, 1 + j, me, (*chip, cc), src=src) for j, chip in enumerate(chips)]
            for cp in [own] + sends:
                cp.start()
            return own, sends

        def finish(a, own, sends):
            passed = []
            for j, chip in enumerate(chips):
                copy(a, 1 + j, (*chip, cc), me).wait_recv()
                passed.append(copy(a, 4 + j, (*chip, cc), sibling))
                passed[-1].start()
            copy(a, 0, sibling, me).wait_recv()
            for j, chip in enumerate(chips):
                copy(a, 4 + j, (*chip, 1 - cc), me).wait_recv()
            for cp in sends + passed:
                cp.wait_send()
            own.wait()

        finish(0, *begin(0, c_ref))
        cols_ref[...] = (jnp.dot(call_ref[:, 0, :].astype(bf16), w_ref[...].astype(bf16), preferred_element_type=f32)
                         + b_ref[...])
        finish(1, *begin(1, cols_ref))
        finish(2, *begin(2, win_ref))

    vmem, hbm = pl.BlockSpec(memory_space=pltpu.VMEM), pl.BlockSpec(memory_space=pl.ANY)
    return pl.pallas_call(
        body, name=name, in_specs=[vmem, vmem, vmem, hbm], out_specs=[vmem, vmem, hbm],
        out_shape=[jax.ShapeDtypeStruct((n_dev, 1, d), f32), jax.ShapeDtypeStruct((n_dev, n_dev, ada_w), f32),
                   jax.ShapeDtypeStruct((n_dev,) + w_in_t.shape, w_in_t.dtype)],
        scratch_shapes=[pltpu.VMEM((n_dev, ada_w), f32), pltpu.SemaphoreType.DMA((3, 7)), pltpu.SemaphoreType.DMA((3, 7)),
                        pltpu.SemaphoreType.DMA((3,))],
        compiler_params=pltpu.CompilerParams(vmem_limit_bytes=VMEM_LIMIT),
    )(c, w_ada, b_mine, w_in_t)


_FLIPS = ((0, 0, 1), (1, 0, 0), (0, 1, 0), (1, 1, 0), (1, 0, 1), (0, 1, 1), (1, 1, 1))
_HBM = pl.BlockSpec(memory_space=pltpu.HBM)
_SEM = pl.BlockSpec(memory_space=pltpu.SEMAPHORE)


def _exchange_copies(scatter, srcs, lands, send_sems, recv_sems):
    x, y, c = _me()
    me_row = 4 * x + 2 * y + c
    out = []
    for k, (fx, fy, fc) in enumerate(_FLIPS):
        peer = (x ^ fx, y ^ fy, c ^ fc)
        peer_row = 4 * peer[0] + 2 * peer[1] + peer[2]
        for a in range(len(srcs)):
            out.append(pltpu.make_async_remote_copy(
                src_ref=srcs[a].at[peer_row] if scatter else srcs[a], dst_ref=lands[a].at[me_row],
                send_sem=send_sems.at[7 * a + k], recv_sem=recv_sems.at[7 * a + k], device_id=peer, device_id_type=MESH))
    return out


def _exchange_start(arrays, scatter, name, after=None):
    n = len(arrays)
    lands = [lax.empty(a.shape if scatter else (N_DEV,) + a.shape, a.dtype) for a in arrays]
    extra = [] if after is None else [after]

    def body(*refs):
        srcs, zones = refs[:n], refs[n:2 * n]
        send_sems, recv_sems = refs[2 * n + len(extra)], refs[2 * n + len(extra) + 1]
        token = refs[-1]
        for cp in _exchange_copies(scatter, srcs, zones, send_sems, recv_sems):
            cp.start()
        token[...] = jnp.zeros_like(token)

    thru = [pltpu.HBM(a.shape, a.dtype) for a in list(arrays) + lands]
    outs = pl.pallas_call(
        body, name=name,
        out_shape=(pltpu.SemaphoreType.DMA((7 * n,)), pltpu.SemaphoreType.DMA((7 * n,)), *thru, jax.ShapeDtypeStruct((8, LANES), f32)),
        in_specs=[_HBM] * (2 * n) + [pl.BlockSpec(memory_space=pl.ANY)] * len(extra),
        out_specs=(_SEM, _SEM, *[_HBM] * (2 * n), pl.BlockSpec(memory_space=pltpu.VMEM)),
        input_output_aliases={i: 2 + i for i in range(2 * n)},
        compiler_params=pltpu.CompilerParams(has_side_effects=pltpu.SideEffectType.DATAFLOW_SIDE_EFFECTING),
    )(*[pltpu.with_memory_space_constraint(a, pltpu.HBM) for a in list(arrays) + lands], *extra)
    return dict(n=n, scatter=scatter, sems=outs[:2], srcs=outs[2:2 + n], lands=outs[2 + n:2 + 2 * n], token=outs[-1])


def _exchange_wait(handle, after, name):
    n, scatter = handle["n"], handle["scatter"]

    def body(*refs):
        srcs, zones = refs[:n], refs[n:2 * n]
        send_sems, recv_sems = refs[2 * n], refs[2 * n + 1]
        for cp in _exchange_copies(scatter, srcs, zones, send_sems, recv_sems):
            cp.wait_send()
            cp.wait_recv()

    thru = [pltpu.HBM(a.shape, a.dtype) for a in list(handle["srcs"]) + list(handle["lands"])]
    outs = pl.pallas_call(
        body, name=name, out_shape=tuple(thru),
        in_specs=[_HBM] * (2 * n) + [_SEM, _SEM, pl.BlockSpec(memory_space=pl.ANY)], out_specs=tuple([_HBM] * (2 * n)),
        input_output_aliases={i: i for i in range(2 * n)},
        compiler_params=pltpu.CompilerParams(has_side_effects=pltpu.SideEffectType.DATAFLOW_SIDE_EFFECTING),
    )(*handle["srcs"], *handle["lands"], *handle["sems"], after)
    return list(outs[:n]), list(outs[n:])


def _cols_from_shards(g):
    return jnp.transpose(g, (1, 0, 2)).reshape(g.shape[1], -1)


def _shards_from_cols(a):
    return jnp.transpose(a.reshape(a.shape[0], N_DEV, -1), (1, 0, 2))


def _local_step(x, positions, ada, g_pre_mix, g_post_mix, b_f, sinks, g_pre_ffn, g_post_ffn, target,
                w_in_t, mix_weights, ffn_weights, on_grads):
    s, d = x.shape
    row = lambda v: v.reshape(1, -1)
    shift_m, scale_m, gate_m, shift_f, scale_f, gate_f = (ada[i:i + 1] for i in range(6))
    w_gate_t, w_qkv_t = w_in_t[F_OFF + N_HEADS:], w_in_t[:QKV_W]
    w_f_t = jnp.pad(w_in_t[F_OFF:F_OFF + N_HEADS], ((0, LANES - N_HEADS), (0, 0)))
    bf_row = jnp.pad(row(b_f), ((0, 0), (0, LANES - N_HEADS)))
    sink_rows = jnp.broadcast_to(sinks.reshape(N_HEADS, 1).astype(f32), (N_HEADS, LANES))
    inv_freq = 1.0 / (ROPE_THETA ** (jnp.arange(0, HEAD_DIM, 2, dtype=f32) / HEAD_DIM))
    cos, sin_s = _rope_tables(positions.reshape(s, 1), jnp.tile(inv_freq, 4).reshape(1, LANES), "rope_tables")

    h1, qa, ka, va, qb, kb, vb = _prenorm_proj_qkv(x, row(g_pre_mix), scale_m, shift_m, w_qkv_t, cos, sin_s, "prenorm_proj_qkv")
    gl = _matmul(h1, w_gate_t, "nt", bf16, "proj_gate")
    fl, cum_b = _forget_prep(h1, w_f_t, bf_row, "proj_forget_prep")
    o_a, lse_a = _attn_fwd(qa, ka, va, "swa_fwd", sink_rows=sink_rows, window=WINDOW, t=2048)
    o_b, lse_b = _attn_fwd(qb, kb, vb, "fox_fwd", cum_b=cum_b, t=1024)
    everything_before = (gl[:8, :LANES] + o_a[:8, :LANES] + o_b[:8, :LANES]).astype(f32)
    w_branch_a, w_branch_b, w_out = mix_weights(everything_before)
    ba, bb, merged = _branch_merge(o_a, o_b, w_branch_a, w_branch_b, gl, "branch_merge")
    y1, x2, h2 = _out_proj_postnorm_prenorm(merged, w_out, x, row(g_post_mix), gate_m, row(g_pre_ffn), scale_f, shift_f,
                                            "out_proj_norms")

    w_ffn_in_t, w_ffn_out = ffn_weights(h2)
    g_ff, u_ff, act = _ffn_in_swiglu(h2, w_ffn_in_t, "ffn_in_swiglu")
    loss_row, d_out, d_y2, vec_pf = _out_proj_loss_tail(act, w_ffn_out, x2, row(g_post_ffn), gate_f, target, "ffn_out_loss_tail")

    g_w_ffn_out = _matmul(act, d_y2, "tn", bf16, "ffn_out_wgrad")
    dg_ff, du_ff = _ffn_out_dgrad_swiglu(d_y2, w_ffn_out, g_ff, u_ff, "ffn_out_dgrad_swiglu")
    g_w_ffn_in_t = _wgrad_stack([dg_ff, du_ff], h2, "ffn_in_wgrad")
    sent = on_grads(dict(w_ffn_in=g_w_ffn_in_t, w_ffn_out=g_w_ffn_out))
    d_x2, vec_nf, d_y1, vec_pm = _dgrad_prenorm_bwd(
        [(dg_ff, w_ffn_in_t, 0), (du_ff, w_ffn_in_t, 1)], x2, row(g_pre_ffn), scale_f, d_out, "ffn_in_dgrad_norms_bwd",
        after=sent, below=(y1, row(g_post_mix), gate_m))

    g_w_out = _matmul(merged, d_y1, "tn", bf16, "out_proj_wgrad")
    d_ba, d_bb, dgl = _out_dgrad_merge_bwd(d_y1, w_out, ba, bb, gl, "out_proj_dgrad_merge_bwd")
    g_w_branch_a = _matmul(o_a, d_ba, "tn", bf16, "branch_a_wgrad")
    g_w_branch_b = _matmul(o_b, d_bb, "tn", bf16, "branch_b_wgrad")
    sent = on_grads(dict(w_out=g_w_out, w_branch_a=g_w_branch_a, w_branch_b=g_w_branch_b))
    d_oa, delta_a, d_sink = _branch_dgrad_delta(d_ba, w_branch_a, o_a, "branch_a_dgrad_delta", lse=lse_a,
                                                sink_rows=sink_rows, after=sent)
    d_ob, delta_b = _branch_dgrad_delta(d_bb, w_branch_b, o_b, "branch_b_dgrad_delta", after=sent)
    dqa_t, dka, dva = _attn_bwd(qa, ka, va, d_oa, lse_a, delta_a, "swa_bwd", window=WINDOW, t=2048)
    dqb_t, dkb, dvb, dcs, rs = _attn_bwd(qb, kb, vb, d_ob, lse_b, delta_b, "fox_bwd", cum_b=cum_b, t=512)
    dqkv = _qkv_prep_bwd(dqa_t, dka, dva, dqb_t, dkb, dvb, cos, sin_s, "qkv_prep_bwd")
    dfl, vec_bf = _forget_prep_bwd(rs.reshape(N_HEADS, s), dcs, fl, bf_row, "forget_prep_bwd")
    g_w_in_t = jnp.concatenate([_matmul(dqkv, h1, "tn", bf16, "qkv_wgrad"), _matmul(dfl, h1, "tn", bf16, "forget_wgrad")[:N_HEADS],
                                _matmul(dgl, h1, "tn", bf16, "gate_wgrad")], axis=0)
    sent = on_grads(dict(w_in=g_w_in_t))
    grad_x, vec_nm = _dgrad_prenorm_bwd([(dgl, w_gate_t, 0), (dqkv, w_qkv_t, 0), (dfl, w_f_t, 0)], x, row(g_pre_mix),
                                        scale_m, d_x2, "in_proj_dgrad_prenorm_bwd", after=sent)

    d_ada = jnp.concatenate([vec_nm[0], vec_nm[1], vec_pm[0], vec_nf[0], vec_nf[1], vec_pf[0]])
    small = dict(b_ada=d_ada, g_pre_mix=vec_nm[2], g_post_mix=vec_pm[1], g_pre_ffn=vec_nf[2], g_post_ffn=vec_pf[1],
                 b_f=vec_bf[0, :N_HEADS], sinks=d_sink[:, 0], loss=loss_row[0, :1])
    return grad_x, small


_SMALL = (("b_ada", 6144), ("g_pre_mix", 1024), ("g_post_mix", 1024), ("g_pre_ffn", 1024), ("g_post_ffn", 1024),
          ("b_f", 128), ("sinks", 128), ("loss", 128))
_SMALL_ROWS = 88


def _pack_small(vals):
    parts = [jnp.pad(vals[k].reshape(-1).astype(f32), (0, n - vals[k].size)) for k, n in _SMALL]
    flat = jnp.concatenate(parts)
    return jnp.pad(flat, (0, _SMALL_ROWS * LANES - flat.size)).reshape(_SMALL_ROWS, LANES)


def _unpack_small(slab, shapes):
    flat, out, off = slab.reshape(-1), {}, 0
    for k, n in _SMALL:
        size = math.prod(shapes[k])
        out[k] = flat[off:off + size].reshape(shapes[k])
        off += n
    return out


def kernel(x, c, positions, w_ada, b_ada, g_pre_mix, g_post_mix, w_in, b_f, sinks, w_branch_a, w_branch_b, w_out, g_pre_ffn, g_post_ffn, w_ffn_in, w_ffn_out, loss_target, m_w_ada, m_b_ada, m_g_pre_mix, m_g_post_mix, m_w_in, m_b_f, m_sinks, m_w_branch_a, m_w_branch_b, m_w_out, m_g_pre_ffn, m_g_post_ffn, m_w_ffn_in, m_w_ffn_out, v_w_ada, v_b_ada, v_g_pre_mix, v_g_post_mix, v_w_in, v_b_f, v_sinks, v_w_branch_a, v_w_branch_b, v_w_out, v_g_pre_ffn, v_g_post_ffn, v_w_ffn_in, v_w_ffn_out):
    xi, yi, ci = _me()
    me = 4 * xi + 2 * yi + ci
    d = D_MODEL
    ada_w = w_ada.shape[2]

    transposed = ("w_in", "w_ffn_in")
    tr = lambda a: jnp.transpose(a[0])

    b_mine = lax.dynamic_slice(b_ada, (0, me * ada_w), (1, ada_w))
    c_all, ada_all, g_in = _gather_prologue(c, w_ada[0], b_mine, tr(w_in).astype(bf16), "gather_prologue")
    c_all = c_all.reshape(N_DEV, d)
    ada = lax.dynamic_index_in_dim(ada_all, me, axis=1, keepdims=False).reshape(6, d)
    late_mix = [w.astype(bf16) for w in (w_branch_a[0], w_branch_b[0], w_out[0])]
    late_ffn = [w.astype(bf16) for w in (tr(w_ffn_in), w_ffn_out[0])]
    mix_h = _exchange_start(late_mix, False, "gather_mix_start", after=g_in)
    ffn_h = _exchange_start(late_ffn, False, "gather_ffn_start", after=mix_h["token"])

    def mine_into(zone, block):
        return lax.dynamic_update_index_in_dim(zone, block, me, 0)

    def rows_from_shards(g):
        return g.reshape(g.shape[0] * g.shape[1], g.shape[2])

    def mix_weights(after):
        sent, zones = _exchange_wait(mix_h, after, "gather_mix_wait")
        g_ba, g_bb, g_out = (mine_into(z, w) for z, w in zip(zones, sent))
        return _cols_from_shards(g_ba), _cols_from_shards(g_bb), rows_from_shards(g_out)

    def ffn_weights(after):
        sent, zones = _exchange_wait(ffn_h, after, "gather_ffn_wait")
        g_fi, g_fo = (mine_into(z, w) for z, w in zip(zones, sent))
        return rows_from_shards(g_fi), rows_from_shards(g_fo)

    row_sharded = ("w_out", "w_ffn_out") + transposed
    in_flight = []

    def on_grads(group):
        sends = [g.reshape(N_DEV, g.shape[0] // N_DEV, g.shape[1]) if nm in row_sharded else _shards_from_cols(g)
                 for nm, g in group.items()]
        handle = _exchange_start(sends, True, "scatter_start_%d" % len(in_flight))
        in_flight.append((list(group), handle))
        return handle["token"]

    grad_x, small = _local_step(
        x[0], positions[0], ada + ffn_h["token"][0, 0], g_pre_mix[0], g_post_mix[0], b_f[0], sinks[0], g_pre_ffn[0],
        g_post_ffn[0], loss_target[0], rows_from_shards(g_in), mix_weights, ffn_weights, on_grads)

    ws = dict(w_in=(w_in, m_w_in, v_w_in), w_branch_a=(w_branch_a, m_w_branch_a, v_w_branch_a),
              w_branch_b=(w_branch_b, m_w_branch_b, v_w_branch_b), w_out=(w_out, m_w_out, v_w_out),
              w_ffn_in=(w_ffn_in, m_w_ffn_in, v_w_ffn_in), w_ffn_out=(w_ffn_out, m_w_ffn_out, v_w_ffn_out))
    res = {}

    def finish_group(gi, after):
        names, handle = in_flight[gi]
        sends, zones = _exchange_wait(handle, after, "scatter_wait_%d" % gi)
        for nm, zone, sent in zip(names, zones, sends):
            w, m, v = (tr(a) if nm in transposed else a[0] for a in ws[nm])
            out = _adamw(zone, w, m, v, "adamw_" + nm, mine=sent, me=me.reshape(1).astype(jnp.int32))
            after = out[0]
            res[nm] = [jnp.transpose(o) for o in out] if nm in transposed else out
        return after

    small_h = _exchange_start([_pack_small(small)], False, "gather_small_start", after=grad_x)
    done = finish_group(1, finish_group(0, small_h["token"]))
    (slab_mine,), (slab_zone,) = _exchange_wait(small_h, done, "gather_small_wait")
    slab_all = mine_into(slab_zone, slab_mine)
    small_w = dict(b_ada=b_ada, g_pre_mix=g_pre_mix, g_post_mix=g_post_mix, g_pre_ffn=g_pre_ffn, g_post_ffn=g_post_ffn,
                   b_f=b_f, sinks=sinks, loss=jnp.zeros((1,), f32))
    small_m = dict(b_ada=m_b_ada, g_pre_mix=m_g_pre_mix, g_post_mix=m_g_post_mix, g_pre_ffn=m_g_pre_ffn,
                   g_post_ffn=m_g_post_ffn, b_f=m_b_f, sinks=m_sinks, loss=jnp.zeros((1,), f32))
    small_v = dict(b_ada=v_b_ada, g_pre_mix=v_g_pre_mix, g_post_mix=v_g_post_mix, g_pre_ffn=v_g_pre_ffn,
                   g_post_ffn=v_g_post_ffn, b_f=v_b_f, sinks=v_sinks, loss=jnp.ones((1,), f32))
    shapes = {k: small_w[k].shape for k, _ in _SMALL}
    s_out = _adamw(slab_all, _pack_small(small_w), _pack_small(small_m), _pack_small(small_v), "adamw_small")
    s_grad, s_delta, s_m, s_v = (_unpack_small(o, shapes) for o in s_out)

    d_ada_all = lax.dynamic_slice(slab_all[:, :6144 // LANES, :].reshape(N_DEV, 6144), (0, me * ada_w), (N_DEV, ada_w))
    ada_parts = _ada_wgrad(c_all, d_ada_all, "ada_wgrad")

    res["w_ada"] = _adamw(ada_parts, w_ada[0], m_w_ada[0], v_w_ada[0], "adamw_w_ada")
    finish_group(2, res["w_ada"][0])

    order = ["w_ada", "b_ada", "g_pre_mix", "g_post_mix", "w_in", "b_f", "sinks", "w_branch_a", "w_branch_b", "w_out",
             "g_pre_ffn", "g_post_ffn", "w_ffn_in", "w_ffn_out"]
    outs = [s_grad["loss"].reshape(()), grad_x[None]]
    for which, small_o in enumerate((s_grad, s_delta, s_m, s_v)):
        for nm in order:
            outs.append(res[nm][which][None] if nm in res else small_o[nm])
    return tuple(outs)
```

```python
import math

import jax
import jax.numpy as jnp
from jax import lax
from jax.experimental import pallas as pl
from jax.experimental.pallas import tpu as pltpu

f32 = jnp.float32
bf16 = jnp.bfloat16

D_MODEL = 1024
HEAD_DIM = 64
N_HEADS = 8
N_PAIRS = 4
QKV_W = 2304
F_OFF = 2304
WINDOW = 128
ROPE_THETA = 10000.0
RMS_EPS = 1e-6
N_DEV = 8
ADAM_LR, ADAM_B1, ADAM_B2, ADAM_EPS, ADAM_WD, ADAM_STEP = 0.001, 0.9, 0.999, 1e-08, 0.01, 10
NEG = -1e30
L_ROW = (HEAD_DIM, 0)
LANES = 128
VMEM_LIMIT = 48 * 1024 * 1024
MESH = pl.DeviceIdType.MESH

_NT = (((1,), (1,)), ((), ()))
_TN = (((0,), (0,)), ((), ()))


def _params(n_grid=0):
    sem = ("arbitrary",) * n_grid if n_grid else None
    return pltpu.CompilerParams(dimension_semantics=sem, vmem_limit_bytes=VMEM_LIMIT)


def _row_tile(s, want):
    t = min(s, want)
    assert s % t == 0, (s, t)
    return t


MATMUL_VMEM_BUDGET = 40 * 1024 * 1024


def _matmul_tiles(m, n, k, a_item, b_item, o_item):
    def tiles(d):
        return [t for t in range(LANES, min(d, 2048) + 1, LANES) if d % t == 0] or [d]

    best = None
    for tm in tiles(m):
        for tn in tiles(n):
            vmem = 2 * (tm * k * a_item + tn * k * b_item + tm * tn * o_item) + tm * tn * 4
            if vmem > MATMUL_VMEM_BUDGET:
                continue
            traffic = m * k * a_item + n * k * b_item * (1 if tn == n else m // tm) + m * n * o_item
            steps = (m // tm) * (n // tn)
            key = (traffic, 0, steps) if steps >= 4 else (traffic, 1, -steps)
            if best is None or key < best[0]:
                best = (key, tm, tn)
    assert best is not None, (m, n, k)
    return best[1], best[2]


def _matmul(a, b, mode, out_dtype, name, after=None):
    if mode == "nn":
        (m, k), n = a.shape, b.shape[1]
    elif mode == "nt":
        (m, k), n = a.shape, b.shape[0]
    else:
        (k, m), n = a.shape, b.shape[1]
    tm, tn = _matmul_tiles(m, n, k, a.dtype.itemsize, b.dtype.itemsize, jnp.dtype(out_dtype).itemsize)
    if mode == "nn":
        a_spec, b_spec, dims = pl.BlockSpec((tm, k), lambda i, j: (i, 0)), pl.BlockSpec((k, tn), lambda i, j: (0, j)), None
    elif mode == "nt":
        a_spec, b_spec, dims = pl.BlockSpec((tm, k), lambda i, j: (i, 0)), pl.BlockSpec((tn, k), lambda i, j: (j, 0)), _NT
    else:
        a_spec, b_spec, dims = pl.BlockSpec((k, tm), lambda i, j: (0, i)), pl.BlockSpec((k, tn), lambda i, j: (0, j)), _TN

    def body(a_ref, b_ref, *rest):
        o_ref = rest[-1]
        av, bv = a_ref[...].astype(bf16), b_ref[...].astype(bf16)
        if dims is None:
            r = jnp.dot(av, bv, preferred_element_type=f32)
        else:
            r = lax.dot_general(av, bv, dims, preferred_element_type=f32)
        o_ref[...] = r.astype(out_dtype)

    extra = [] if after is None else [after]
    return pl.pallas_call(
        body, name=name, grid=(m // tm, n // tn), in_specs=[a_spec, b_spec] + [pl.BlockSpec(memory_space=pl.ANY)] * len(extra),
        out_specs=pl.BlockSpec((tm, tn), lambda i, j: (i, j)),
        out_shape=jax.ShapeDtypeStruct((m, n), out_dtype), compiler_params=_params(2),
    )(a, b, *extra)


def _rstd(v):
    return lax.rsqrt(jnp.mean(v * v, axis=-1, keepdims=True) + RMS_EPS)


def _row_spec(tm, d):
    return pl.BlockSpec((tm, d), lambda i: (i, 0))


def _vec_spec(d, rows=1):
    return pl.BlockSpec((rows, d), lambda i: (0, 0))


def _proj_spec(a, w, tm):
    return [_row_spec(tm, a.shape[1]), pl.BlockSpec(w.shape, lambda i: (0, 0))]


def _out_proj_postnorm_prenorm(a, w, x, g_post, gate, g_pre, scale, shift, name):
    s, d = x.shape
    tm = _row_tile(s, 512)

    def body(a_ref, w_ref, x_ref, gp_ref, gate_ref, g_ref, sc_ref, sh_ref, y_ref, x2_ref, h_ref):
        yv = jnp.dot(a_ref[...], w_ref[...], preferred_element_type=f32)
        y_ref[...] = yv
        x2 = x_ref[...] + gate_ref[...] * (yv * _rstd(yv) * gp_ref[...])
        x2_ref[...] = x2
        h_ref[...] = ((x2 * _rstd(x2) * g_ref[...]) * (1.0 + sc_ref[...]) + sh_ref[...]).astype(bf16)

    return pl.pallas_call(
        body, name=name, grid=(s // tm,), in_specs=_proj_spec(a, w, tm) + [_row_spec(tm, d)] + [_vec_spec(d)] * 5,
        out_specs=[_row_spec(tm, d)] * 3,
        out_shape=[jax.ShapeDtypeStruct((s, d), f32)] * 2 + [jax.ShapeDtypeStruct((s, d), bf16)], compiler_params=_params(1),
    )(a, w, x, g_post, gate, g_pre, scale, shift)


def _rms_bwd(u, v, r):
    return r * u - v * (r * r * r) * jnp.mean(u * v, axis=-1, keepdims=True)


def _out_proj_loss_tail(a, w, x, g, gate, target, name):
    s, d = x.shape
    tm = _row_tile(s, 512)

    def body(a_ref, w_ref, x_ref, g_ref, gate_ref, t_ref, loss_ref, do_ref, dy_ref, vec_ref):
        @pl.when(pl.program_id(0) == 0)
        def _():
            loss_ref[...] = jnp.zeros_like(loss_ref)
            vec_ref[...] = jnp.zeros_like(vec_ref)
        yv = jnp.dot(a_ref[...], w_ref[...], preferred_element_type=f32)
        r = _rstd(yv)
        yn = yv * r
        err = x_ref[...] + gate_ref[...] * (yn * g_ref[...]) - t_ref[...]
        loss_ref[...] += 0.5 * jnp.sum(jnp.mean(err * err, axis=-1, keepdims=True), axis=0, keepdims=True)
        dr = err / d
        do_ref[...] = dr
        dn = dr * gate_ref[...]
        vec_ref[0:1, :] += jnp.sum(dr * (yn * g_ref[...]), axis=0, keepdims=True)
        vec_ref[1:2, :] += jnp.sum(dn * yn, axis=0, keepdims=True)
        dy_ref[...] = _rms_bwd(dn * g_ref[...], yv, r).astype(bf16)

    return pl.pallas_call(
        body, name=name, grid=(s // tm,),
        in_specs=_proj_spec(a, w, tm) + [_row_spec(tm, d)] + [_vec_spec(d)] * 2 + [_row_spec(tm, d)],
        out_specs=[_vec_spec(LANES), _row_spec(tm, d), _row_spec(tm, d), _vec_spec(d, 8)],
        out_shape=[jax.ShapeDtypeStruct((1, LANES), f32), jax.ShapeDtypeStruct((s, d), f32),
                   jax.ShapeDtypeStruct((s, d), bf16), jax.ShapeDtypeStruct((8, d), f32)],
        compiler_params=_params(1),
    )(a, w, x, g, gate, target)


def _dgrad_prenorm_bwd(terms, x, g, scale, dres, name, after=None, below=None):
    s, d = x.shape
    n = len(terms)
    k = sum(a.shape[1] for a, _, _ in terms)
    row_bytes = 2 * (2 * k) + d * (4 + 2 * 4 * 3 + (2 * 4 + 2 * 2 if below else 0))
    tm = next(t for t in (512, 256, 128) if s % t == 0 and 4 * k * d + t * row_bytes <= MATMUL_VMEM_BUDGET)
    extra = [] if after is None else [after]

    def body(*refs):
        a_refs, b_refs = refs[:n], refs[n:2 * n]
        x_ref, g_ref, sc_ref, dr_ref = refs[2 * n:2 * n + 4]
        n_in = 2 * n + 4 + (3 if below else 0) + len(extra)
        dx_ref, vec_ref = refs[n_in], refs[n_in + 1]
        if below:
            y_ref, gp_ref, gate_ref = refs[2 * n + 4:2 * n + 7]
            dy_ref, vec2_ref = refs[n_in + 2], refs[n_in + 3]

        @pl.when(pl.program_id(0) == 0)
        def _():
            vec_ref[...] = jnp.zeros_like(vec_ref)
            if below:
                vec2_ref[...] = jnp.zeros_like(vec2_ref)
        dhv = jnp.dot(a_refs[0][...], b_refs[0][...], preferred_element_type=f32)
        for i in range(1, n):
            dhv = dhv + jnp.dot(a_refs[i][...], b_refs[i][...], preferred_element_type=f32)
        xv = x_ref[...]
        r = _rstd(xv)
        xn = xv * r
        dn = dhv * (1.0 + sc_ref[...])
        vec_ref[0:1, :] += jnp.sum(dhv, axis=0, keepdims=True)
        vec_ref[1:2, :] += jnp.sum(dhv * (xn * g_ref[...]), axis=0, keepdims=True)
        vec_ref[2:3, :] += jnp.sum(dn * xn, axis=0, keepdims=True)
        dx = dr_ref[...] + _rms_bwd(dn * g_ref[...], xv, r)
        dx_ref[...] = dx
        if below:
            yv = y_ref[...]
            ry = _rstd(yv)
            yn = yv * ry
            dny = dx * gate_ref[...]
            vec2_ref[0:1, :] += jnp.sum(dx * (yn * gp_ref[...]), axis=0, keepdims=True)
            vec2_ref[1:2, :] += jnp.sum(dny * yn, axis=0, keepdims=True)
            dy_ref[...] = _rms_bwd(dny * gp_ref[...], yv, ry).astype(bf16)

    in_specs = ([_row_spec(tm, a.shape[1]) for a, _, _ in terms]
                + [pl.BlockSpec((a.shape[1], d), lambda i, r=r: (r, 0)) for a, _, r in terms]
                + [_row_spec(tm, d)] + [_vec_spec(d)] * 2 + [_row_spec(tm, d)])
    out_specs = [_row_spec(tm, d), _vec_spec(d, 8)]
    out_shape = [jax.ShapeDtypeStruct((s, d), f32), jax.ShapeDtypeStruct((8, d), f32)]
    args = [a for a, _, _ in terms] + [b for _, b, _ in terms] + [x, g, scale, dres]
    if below:
        in_specs += [_row_spec(tm, d)] + [_vec_spec(d)] * 2
        out_specs += [_row_spec(tm, d), _vec_spec(d, 8)]
        out_shape += [jax.ShapeDtypeStruct((s, d), bf16), jax.ShapeDtypeStruct((8, d), f32)]
        args += list(below)
    return pl.pallas_call(
        body, name=name, grid=(s // tm,), in_specs=in_specs + [pl.BlockSpec(memory_space=pl.ANY)] * len(extra),
        out_specs=out_specs, out_shape=out_shape, compiler_params=_params(1),
    )(*args, *extra)


def _lane():
    return lax.broadcasted_iota(jnp.int32, (1, LANES), 1)


def _rope_tables(pos_col, inv_freq, name):
    s = pos_col.shape[0]

    def body(p_ref, f_ref, cos_ref, sin_ref):
        ang = p_ref[...].astype(f32) * f_ref[...]
        first_half = (_lane() % HEAD_DIM) < HEAD_DIM // 2
        cos_ref[...] = jnp.cos(ang)
        sn = jnp.sin(ang)
        sin_ref[...] = jnp.where(first_half, -sn, sn)

    return pl.pallas_call(
        body, name=name, out_shape=[jax.ShapeDtypeStruct((s, LANES), f32)] * 2, compiler_params=_params(),
    )(pos_col, inv_freq)


def _swap_halves(v):
    first_half = (_lane() % HEAD_DIM) < HEAD_DIM // 2
    return jnp.where(first_half, pltpu.roll(v, LANES - HEAD_DIM // 2, axis=1), pltpu.roll(v, HEAD_DIM // 2, axis=1))


def _prenorm_proj_qkv(x, g, mod_scale, mod_shift, w_qkv_t, cos, sin_s, name):
    s, d = x.shape
    tm = _row_tile(s, 512)
    scale = 1.0 / math.sqrt(HEAD_DIM)

    def body(x_ref, g_ref, msc_ref, msh_ref, w_ref, c_ref, s_ref, h_ref, qa_ref, ka_ref, va_ref, qb_ref, kb_ref, vb_ref):
        xv = x_ref[...]
        h = ((xv * _rstd(xv) * g_ref[...]) * (1.0 + msc_ref[...]) + msh_ref[...]).astype(bf16)
        h_ref[...] = h
        proj = lax.dot_general(h, w_ref[...], _NT, preferred_element_type=f32)
        cs, sn = c_ref[...], s_ref[...]
        low = _lane() < HEAD_DIM

        def blk(j):
            return proj[:, j * LANES:(j + 1) * LANES]

        def rope(v):
            return v * cs + _swap_halves(v) * sn

        def expand(v):
            other = pltpu.roll(v, HEAD_DIM, axis=1)
            return jnp.where(low, v, other), jnp.where(low, other, v)

        for j in range(N_PAIRS):
            qa_ref[:, j * LANES:(j + 1) * LANES] = (rope(blk(j)) * scale).astype(bf16)
            qb_ref[:, j * LANES:(j + 1) * LANES] = (blk(6 + j) * scale).astype(bf16)
            kb_ref[:, j * LANES:(j + 1) * LANES] = blk(10 + j).astype(bf16)
            vb_ref[:, j * LANES:(j + 1) * LANES] = blk(14 + j).astype(bf16)
        k0, k1 = expand(rope(blk(4)))
        v0, v1 = expand(blk(5))
        for j in range(N_PAIRS):
            ka_ref[:, j * LANES:(j + 1) * LANES] = (k0 if j < 2 else k1).astype(bf16)
            va_ref[:, j * LANES:(j + 1) * LANES] = (v0 if j < 2 else v1).astype(bf16)

    hw = N_PAIRS * LANES
    return pl.pallas_call(
        body, name=name, grid=(s // tm,),
        in_specs=[_row_spec(tm, d)] + [_vec_spec(d)] * 3
        + [pl.BlockSpec((QKV_W, d), lambda i: (0, 0)), _row_spec(tm, LANES), _row_spec(tm, LANES)],
        out_specs=[_row_spec(tm, d)] + [_row_spec(tm, hw)] * 6,
        out_shape=[jax.ShapeDtypeStruct((s, d), bf16)] + [jax.ShapeDtypeStruct((s, hw), bf16)] * 6, compiler_params=_params(1),
    )(x, g, mod_scale, mod_shift, w_qkv_t, cos, sin_s)


def _qkv_prep_bwd(dqa_t, dka, dva, dqb_t, dkb, dvb, cos, sin_s, name):
    s = dka.shape[0]
    tm = _row_tile(s, 256)
    scale = 1.0 / math.sqrt(HEAD_DIM)
    hw = N_PAIRS * LANES
    t_spec = pl.BlockSpec((hw, tm), lambda i: (0, i))

    def body(dqa_ref, dka_ref, dva_ref, dqb_ref, dkb_ref, dvb_ref, c_ref, s_ref, o_ref):
        cs, sn = c_ref[...], s_ref[...]
        low = _lane() < HEAD_DIM

        def blk(ref, j):
            return ref[:, j * LANES:(j + 1) * LANES].astype(f32)

        def blk_t(ref, j):
            return ref[j * LANES:(j + 1) * LANES, :].T

        def unrope(v):
            return v * cs + _swap_halves(v * sn)

        def fold(ref):
            a, b = blk(ref, 0) + blk(ref, 1), blk(ref, 2) + blk(ref, 3)
            kv0 = a + pltpu.roll(a, HEAD_DIM, axis=1)
            kv1 = b + pltpu.roll(b, HEAD_DIM, axis=1)
            return jnp.where(low, kv0, kv1)

        for j in range(N_PAIRS):
            o_ref[:, j * LANES:(j + 1) * LANES] = (unrope(blk_t(dqa_ref, j)) * scale).astype(bf16)
            o_ref[:, (6 + j) * LANES:(7 + j) * LANES] = (blk_t(dqb_ref, j) * scale).astype(bf16)
            o_ref[:, (10 + j) * LANES:(11 + j) * LANES] = blk(dkb_ref, j).astype(bf16)
            o_ref[:, (14 + j) * LANES:(15 + j) * LANES] = blk(dvb_ref, j).astype(bf16)
        o_ref[:, 4 * LANES:5 * LANES] = unrope(fold(dka_ref)).astype(bf16)
        o_ref[:, 5 * LANES:6 * LANES] = fold(dva_ref).astype(bf16)

    return pl.pallas_call(
        body, name=name, grid=(s // tm,),
        in_specs=[t_spec, _row_spec(tm, hw), _row_spec(tm, hw), t_spec, _row_spec(tm, hw), _row_spec(tm, hw)] + [_row_spec(tm, LANES)] * 2,
        out_specs=_row_spec(tm, QKV_W), out_shape=jax.ShapeDtypeStruct((s, QKV_W), bf16), compiler_params=_params(1),
    )(dqa_t, dka, dva, dqb_t, dkb, dvb, cos, sin_s)


def _cumsum_rows(v, reverse=False):
    n = v.shape[0]
    row = lax.broadcasted_iota(jnp.int32, v.shape, 0)
    sh = 1
    while sh < n:
        if reverse:
            v = v + jnp.where(row < n - sh, pltpu.roll(v, n - sh, axis=0), 0.0)
        else:
            v = v + jnp.where(row >= sh, pltpu.roll(v, sh, axis=0), 0.0)
        sh *= 2
    return v


def _log_sigmoid(z):
    return jnp.minimum(z, 0.0) - jnp.log1p(jnp.exp(-jnp.abs(z)))


def _forget_prep(h, w_f_t, bf_row, name):
    s, d = h.shape
    tm = _row_tile(s, 1024)

    def body(h_ref, w_ref, b_ref, f_ref, cb_ref, last_ref):
        @pl.when(pl.program_id(0) == 0)
        def _():
            last_ref[...] = jnp.zeros_like(last_ref)
        fl = lax.dot_general(h_ref[...], w_ref[...], _NT, preferred_element_type=f32)
        f_ref[...] = fl
        cum = _cumsum_rows(_log_sigmoid(fl + b_ref[...])) + last_ref[0:1, :]
        last_ref[0:1, :] = cum[tm - 1:tm, :]
        for hd in range(N_HEADS):
            cb_ref[:, hd * LANES:(hd + 1) * LANES] = jnp.broadcast_to(cum[:, hd:hd + 1], (tm, LANES))

    return pl.pallas_call(
        body, name=name, grid=(s // tm,),
        in_specs=[_row_spec(tm, d), pl.BlockSpec((LANES, d), lambda i: (0, 0)), _vec_spec(LANES)],
        out_specs=[_row_spec(tm, LANES), _row_spec(tm, N_HEADS * LANES)],
        out_shape=[jax.ShapeDtypeStruct((s, LANES), f32), jax.ShapeDtypeStruct((s, N_HEADS * LANES), f32)],
        scratch_shapes=[pltpu.VMEM((8, LANES), f32)], compiler_params=_params(1),
    )(h, w_f_t, bf_row)


def _forget_prep_bwd(rs, dcs, fl, bf_row, name):
    s = fl.shape[0]
    tm = _row_tile(s, 1024)
    n = s // tm

    def body(r_ref, c_ref, f_ref, b_ref, df_ref, db_ref, next_ref):
        @pl.when(pl.program_id(0) == 0)
        def _():
            next_ref[...] = jnp.zeros_like(next_ref)
            db_ref[...] = jnp.zeros_like(db_ref)
        eye = (lax.broadcasted_iota(jnp.int32, (N_HEADS, LANES), 0) == lax.broadcasted_iota(jnp.int32, (N_HEADS, LANES), 1)).astype(f32)
        dcum = lax.dot_general(r_ref[...], eye, _TN, precision=lax.Precision.HIGHEST, preferred_element_type=f32)
        for h in range(N_HEADS):
            dcum = dcum - jnp.where(_lane() == h, jnp.sum(c_ref[:, h * LANES:(h + 1) * LANES], axis=1, keepdims=True), 0.0)
        dlf = _cumsum_rows(dcum, reverse=True) + next_ref[0:1, :]
        next_ref[0:1, :] = dlf[0:1, :]
        z = f_ref[...] + b_ref[...]
        df = jnp.where(_lane() < N_HEADS, dlf * jax.nn.sigmoid(-z), 0.0)
        df_ref[...] = df.astype(bf16)
        db_ref[0:1, :] += jnp.sum(df, axis=0, keepdims=True)

    def rows(width):
        return pl.BlockSpec((tm, width), lambda i: (n - 1 - i, 0))

    return pl.pallas_call(
        body, name=name, grid=(n,),
        in_specs=[pl.BlockSpec((N_HEADS, tm), lambda i: (0, n - 1 - i)), rows(N_HEADS * LANES), rows(LANES), _vec_spec(LANES)],
        out_specs=[rows(LANES), _vec_spec(LANES, 8)],
        out_shape=[jax.ShapeDtypeStruct((s, LANES), bf16), jax.ShapeDtypeStruct((8, LANES), f32)],
        scratch_shapes=[pltpu.VMEM((8, LANES), f32)], compiler_params=_params(1),
    )(rs, dcs, fl, bf_row)


def _tile_mask(n_keys, n_queries, off, window):
    shape = (n_keys, n_queries)
    d = lax.broadcasted_iota(jnp.int32, shape, 1) - lax.broadcasted_iota(jnp.int32, shape, 0) + off
    valid = d >= 0
    return jnp.logical_and(valid, d < window) if window else valid


def _wide(v, t):
    return jnp.concatenate([v] * (t // LANES), axis=1)


def _attn_fwd(q, k, v, name, *, cum_b=None, sink_rows=None, window=None, t=256):
    s = q.shape[0]
    t = _row_tile(s, t)
    fox, has_sink = cum_b is not None, sink_rows is not None
    assert not window or (window % LANES == 0 and LANES + window <= s)

    def body(*refs):
        q_ref, k_ref, v_ref = refs[:3]
        rest = list(refs[3:])
        cb_ref = rest.pop(0) if fox else None
        sink_ref = rest.pop(0) if has_sink else None
        o_ref, lse_ref = rest
        i = pl.program_id(1)
        low = _lane() < HEAD_DIM
        top = lax.broadcasted_iota(jnp.int32, (LANES, 1), 0) < HEAD_DIM
        q2 = q_ref[...]
        zero = jnp.zeros_like(q2)
        qms = (jnp.where(low, q2, zero), jnp.where(low, zero, q2))

        def tile(k0, n_keys, off, carry, masked, queries=slice(0, t)):
            nq = queries.stop - queries.start
            kblk, vblk = k_ref[pl.ds(k0, n_keys), :], v_ref[pl.ds(k0, n_keys), :]
            valid = _tile_mask(n_keys, nq, off, window) if masked else None
            ones = jnp.ones_like(vblk)
            vs = tuple(jnp.where(_lane() == L_ROW[h], ones, vblk) for h in range(2))

            def scores(h):
                return lax.dot_general(kblk, qms[h][queries], _NT, preferred_element_type=f32)

            def softmax(h, sc):
                m = carry[h][0]
                if fox:
                    sc = sc - _wide(cb_ref[pl.ds(k0, n_keys), h * LANES:(h + 1) * LANES], nq)
                if masked:
                    sc = jnp.where(valid, sc, NEG)
                m_new = jnp.maximum(m, jnp.max(sc, axis=0, keepdims=True))
                return m_new, jnp.exp(m - m_new), jnp.exp(sc - m_new).astype(bf16)

            def update(h, m_new, alpha, p):
                return m_new, alpha * carry[h][1] + lax.dot_general(vs[h], p, _TN, preferred_element_type=f32)

            if window:
                return tuple(update(h, *softmax(h, scores(h))) for h in range(2))
            scs = [scores(h) for h in range(2)]
            stats = [softmax(h, scs[h]) for h in range(2)]
            return tuple(update(h, *stats[h]) for h in range(2))

        def start(nq):
            if has_sink:
                row = lax.broadcasted_iota(jnp.int32, (LANES, nq), 0)
                return tuple((_wide(sink_ref[h:h + 1, :], nq), (row == L_ROW[h]).astype(f32)) for h in range(2))
            return tuple((jnp.full((1, nq), NEG, f32), jnp.zeros((LANES, nq), f32)) for h in range(2))

        def finish(carry, queries):
            (m0, a0), (m1, a1) = carry
            l0, l1 = a0[L_ROW[0]:L_ROW[0] + 1, :], a1[L_ROW[1]:L_ROW[1] + 1, :]
            o_t = jnp.where(top, a0 * (1.0 / l0), a1 * (1.0 / l1))
            o_ref[queries, :] = o_t.T.astype(bf16)
            lse_ref[0:1, queries] = m0 + jnp.log(l0)
            lse_ref[1:2, queries] = m1 + jnp.log(l1)

        if window:
            for c in range(t // LANES):
                queries = slice(c * LANES, (c + 1) * LANES)
                q0 = i * t + c * LANES
                k0 = pl.multiple_of(jnp.maximum(q0 - window, 0), LANES)
                finish(tile(k0, LANES + window, q0 - k0, start(LANES), True, queries), queries)
        else:
            carry = lax.fori_loop(0, i, lambda kb, c: tile(pl.multiple_of(kb * t, t), t, 0, c, False), start(t))
            half, k_own = t // 2, pl.multiple_of(i * t, t)
            carry = tile(k_own, half, 0, carry, True)
            finish(tuple((m[:, :half], a[:, :half]) for m, a in carry), slice(0, half))
            carry = tuple((m[:, half:], a[:, half:]) for m, a in carry)
            finish(tile(pl.multiple_of(k_own + half, half), half, 0, carry, True, slice(half, t)), slice(half, t))

    q_spec = pl.BlockSpec((t, LANES), lambda j, i: (i, j))
    kv_spec = pl.BlockSpec((s, LANES), lambda j, i: (0, j))
    in_specs, args = [q_spec, kv_spec, kv_spec], [q, k, v]
    if fox:
        in_specs += [pl.BlockSpec((s, 2 * LANES), lambda j, i: (0, j))]
        args += [cum_b]
    if has_sink:
        in_specs += [pl.BlockSpec((None, 2, LANES), lambda j, i: (j, 0, 0))]
        args += [sink_rows.reshape(N_PAIRS, 2, LANES)]
    return pl.pallas_call(
        body, name=name, grid=(N_PAIRS, s // t), in_specs=in_specs,
        out_specs=[q_spec, pl.BlockSpec((None, 2, t), lambda j, i: (j, 0, i))],
        out_shape=[jax.ShapeDtypeStruct((s, N_PAIRS * LANES), bf16), jax.ShapeDtypeStruct((N_PAIRS, 2, s), f32)],
        compiler_params=_params(2),
    )(*args)


def _branch_dgrad_delta(db, w, o, name, *, lse=None, sink_rows=None, after=None):
    s, hw = o.shape
    tm = _row_tile(s, 512)
    has_sink = sink_rows is not None
    extra = [] if after is None else [after]

    def body(*refs):
        db_ref, w_ref, o_ref = refs[:3]
        outs = refs[3 + (2 if has_sink else 0) + len(extra):]
        do_ref, dl_ref = outs[:2]
        if has_sink:
            lse_ref, sink_ref = refs[3:5]
            ds_ref = outs[2]

            @pl.when(pl.program_id(0) == 0)
            def _():
                ds_ref[...] = jnp.zeros_like(ds_ref)
        do = lax.dot_general(db_ref[...], w_ref[...], _NT, preferred_element_type=f32).astype(bf16)
        do_ref[...] = do
        for j in range(N_PAIRS):
            cols = slice(j * LANES, (j + 1) * LANES)
            prod_t = (do[:, cols].astype(f32) * o_ref[:, cols].astype(f32)).T
            for h in range(2):
                dl = jnp.sum(prod_t[h * HEAD_DIM:(h + 1) * HEAD_DIM, :], axis=0, keepdims=True)
                dl_ref[j, h:h + 1, :] = dl
                if has_sink:
                    r = 2 * j + h
                    p_sink = jnp.exp(sink_ref[r:r + 1, 0:1] - lse_ref[j, h:h + 1, :])
                    ds_ref[r:r + 1, :] += -jnp.sum(p_sink * dl, axis=1, keepdims=True)

    rows_spec = pl.BlockSpec((N_PAIRS, 2, tm), lambda i: (0, 0, i))
    in_specs = [_row_spec(tm, db.shape[1]), pl.BlockSpec(w.shape, lambda i: (0, 0)), _row_spec(tm, hw)]
    args = [db, w, o]
    out_specs = [_row_spec(tm, hw), rows_spec]
    out_shape = [jax.ShapeDtypeStruct((s, hw), bf16), jax.ShapeDtypeStruct((N_PAIRS, 2, s), f32)]
    if has_sink:
        in_specs += [rows_spec, _vec_spec(LANES, N_HEADS)]
        args += [lse, sink_rows]
        out_specs += [_vec_spec(LANES, N_HEADS)]
        out_shape += [jax.ShapeDtypeStruct((N_HEADS, LANES), f32)]
    return pl.pallas_call(
        body, name=name, grid=(s // tm,), in_specs=in_specs + [pl.BlockSpec(memory_space=pl.ANY)] * len(extra),
        out_specs=out_specs, out_shape=out_shape, compiler_params=_params(1),
    )(*args, *extra)


def _attn_bwd(q, k, v, do, lse, delta, name, *, cum_b=None, window=None, t=256):
    s = q.shape[0]
    t = _row_tile(s, t)
    nblk = s // t
    fox = cum_b is not None
    assert not window or (window % LANES == 0 and LANES + window <= s)

    def body(*refs):
        k_ref, v_ref, q_ref, do_ref, lse_ref, dl_ref = refs[:6]
        rest = list(refs[6:])
        cb_ref = rest.pop(0) if fox else None
        dq_ref, dk_ref, dv_ref = rest[:3]
        dcs_ref, rs_ref = (rest[3], rest[4]) if fox else (None, None)
        dk_acc, dv_acc = rest[-2:]
        b = pl.program_id(1)
        k0 = pl.multiple_of(b * t, t)

        @pl.when(b == 0)
        def _():
            dq_ref[...] = jnp.zeros_like(dq_ref)
            if fox:
                rs_ref[...] = jnp.zeros_like(rs_ref)

        dk_acc[...] = jnp.zeros_like(dk_acc)
        dv_acc[...] = jnp.zeros_like(dv_acc)
        if fox:
            dcs_ref[...] = jnp.zeros_like(dcs_ref)
        low = _lane() < HEAD_DIM
        top = lax.broadcasted_iota(jnp.int32, (LANES, 1), 0) < HEAD_DIM
        kblk, vblk = k_ref[...], v_ref[...]
        k_t = kblk.astype(f32).T.astype(bf16)
        cks = [_wide(cb_ref[pl.ds(k0, t), h * LANES:(h + 1) * LANES], t) for h in range(2)] if fox else None

        def tile(q0, n_queries, off, masked, keys=slice(0, t)):
            cols = pl.ds(q0, n_queries)
            q2, do2 = q_ref[cols, :], do_ref[cols, :]
            zero = jnp.zeros_like(q2)
            valid = _tile_mask(keys.stop - keys.start, n_queries, off, window) if masked else None
            dq_parts = []
            for h in range(2):
                qm = jnp.where(low, q2, zero) if h == 0 else jnp.where(low, zero, q2)
                dom = jnp.where(low, do2, zero) if h == 0 else jnp.where(low, zero, do2)
                sc = lax.dot_general(kblk[keys], qm, _NT, preferred_element_type=f32)
                if fox:
                    sc = sc - cks[h][keys, :n_queries]
                if masked:
                    sc = jnp.where(valid, sc, NEG)
                p = jnp.exp(sc - lse_ref[h:h + 1, cols])
                dp = lax.dot_general(vblk[keys], dom, _NT, preferred_element_type=f32)
                ds = p * (dp - dl_ref[h:h + 1, cols])
                pb, dsb = p.astype(bf16), ds.astype(bf16)
                dv_acc[keys, :] += jnp.dot(pb, dom, preferred_element_type=f32)
                dk_acc[keys, :] += jnp.dot(dsb, qm, preferred_element_type=f32)
                dq_parts.append(jnp.dot(k_t[:, keys], dsb, preferred_element_type=f32))
                if fox:
                    dcs_ref[keys, h * LANES:(h + 1) * LANES] += sum(ds[:, g * LANES:(g + 1) * LANES]
                                                                    for g in range(n_queries // LANES))
                    rs_ref[h:h + 1, cols] += jnp.sum(ds, axis=0, keepdims=True)
            dq_ref[:, cols] += jnp.where(top, dq_parts[0], dq_parts[1])

        def later_block(qb, carry):
            tile(pl.multiple_of(qb * t, t), t, 0, False)
            return carry

        if window:
            for c in range(t // LANES):
                first = b * t + c * LANES
                q0 = pl.multiple_of(jnp.minimum(first, s - (LANES + window)), LANES)
                tile(q0, LANES + window, q0 - first, True, slice(c * LANES, (c + 1) * LANES))
        else:
            half = t // 2
            tile(k0, half, 0, True, slice(0, half))
            tile(pl.multiple_of(k0 + half, half), half, half, True)
            lax.fori_loop(b + 1, nblk, later_block, 0)
        dk_ref[...] = dk_acc[...].astype(bf16)
        dv_ref[...] = dv_acc[...].astype(bf16)

    kv_spec = pl.BlockSpec((t, LANES), lambda j, b: (b, j))
    seq_spec = pl.BlockSpec((s, LANES), lambda j, b: (0, j))
    rows_spec = pl.BlockSpec((None, 2, s), lambda j, b: (j, 0, 0))
    hw = N_PAIRS * LANES
    in_specs, args = [kv_spec, kv_spec, seq_spec, seq_spec, rows_spec, rows_spec], [k, v, q, do, lse, delta]
    out_specs = [pl.BlockSpec((LANES, s), lambda j, b: (j, 0)), kv_spec, kv_spec]
    out_shape = [jax.ShapeDtypeStruct((hw, s), f32), jax.ShapeDtypeStruct((s, hw), bf16), jax.ShapeDtypeStruct((s, hw), bf16)]
    if fox:
        in_specs += [pl.BlockSpec((s, 2 * LANES), lambda j, b: (0, j))]
        args += [cum_b]
        out_specs += [pl.BlockSpec((t, 2 * LANES), lambda j, b: (b, j)), rows_spec]
        out_shape += [jax.ShapeDtypeStruct((s, N_HEADS * LANES), f32), jax.ShapeDtypeStruct((N_PAIRS, 2, s), f32)]
    return pl.pallas_call(
        body, name=name, grid=(N_PAIRS, nblk), in_specs=in_specs, out_specs=out_specs, out_shape=out_shape,
        scratch_shapes=[pltpu.VMEM((t, LANES), f32)] * 2, compiler_params=_params(2),
    )(*args)


def _branch_merge(o_a, o_b, w_a, w_b, gl, name):
    s, k = o_a.shape
    d = w_a.shape[1]
    tm = _row_tile(s, 1024)

    def body(oa_ref, ob_ref, wa_ref, wb_ref, g_ref, ba_ref, bb_ref, m_ref):
        ba = jnp.dot(oa_ref[...], wa_ref[...], preferred_element_type=f32)
        bb = jnp.dot(ob_ref[...], wb_ref[...], preferred_element_type=f32)
        g0, g1 = jax.nn.sigmoid(g_ref[:, :d].astype(f32)), jax.nn.sigmoid(g_ref[:, d:].astype(f32))
        ba_ref[...] = ba.astype(bf16)
        bb_ref[...] = bb.astype(bf16)
        m_ref[...] = (g0 * ba + g1 * bb).astype(bf16)

    whole = pl.BlockSpec((k, d), lambda i: (0, 0))
    return pl.pallas_call(
        body, name=name, grid=(s // tm,),
        in_specs=[_row_spec(tm, k), _row_spec(tm, k), whole, whole, _row_spec(tm, 2 * d)],
        out_specs=[_row_spec(tm, d)] * 3, out_shape=[jax.ShapeDtypeStruct((s, d), bf16)] * 3, compiler_params=_params(1),
    )(o_a, o_b, w_a, w_b, gl)


def _out_dgrad_merge_bwd(dy, w_out, ba, bb, gl, name):
    s, d = ba.shape
    tm = _row_tile(s, 512)

    def body(dy_ref, w_ref, a_ref, b_ref, g_ref, da_ref, db_ref, dg_ref):
        dmv = lax.dot_general(dy_ref[...], w_ref[...], _NT, preferred_element_type=f32)
        g0, g1 = jax.nn.sigmoid(g_ref[:, :d].astype(f32)), jax.nn.sigmoid(g_ref[:, d:].astype(f32))
        da_ref[...] = (dmv * g0).astype(bf16)
        db_ref[...] = (dmv * g1).astype(bf16)
        dg_ref[:, :d] = (dmv * a_ref[...].astype(f32) * (g0 * (1.0 - g0))).astype(bf16)
        dg_ref[:, d:] = (dmv * b_ref[...].astype(f32) * (g1 * (1.0 - g1))).astype(bf16)

    return pl.pallas_call(
        body, name=name, grid=(s // tm,),
        in_specs=[_row_spec(tm, dy.shape[1]), pl.BlockSpec(w_out.shape, lambda i: (0, 0))] + [_row_spec(tm, d)] * 2
        + [_row_spec(tm, 2 * d)],
        out_specs=[_row_spec(tm, d)] * 2 + [_row_spec(tm, 2 * d)],
        out_shape=[jax.ShapeDtypeStruct((s, d), bf16)] * 2 + [jax.ShapeDtypeStruct((s, 2 * d), bf16)],
        compiler_params=_params(1),
    )(dy, w_out, ba, bb, gl)


GLU_TILE = 256


def _ffn_in_swiglu(h, w_t, name):
    s, d = h.shape
    f = w_t.shape[0] // 2
    tm = _row_tile(s, 2048)
    tg = GLU_TILE
    nb = f // tg

    def body(h_ref, wg_ref, wu_ref, g_ref, u_ref, act_ref):
        hv = h_ref[...]
        g = lax.dot_general(hv, wg_ref[...], _NT, preferred_element_type=f32)
        u = lax.dot_general(hv, wu_ref[...], _NT, preferred_element_type=f32)
        g_ref[...] = g.astype(bf16)
        u_ref[...] = u.astype(bf16)
        act_ref[...] = (g * jax.nn.sigmoid(g) * u).astype(bf16)

    col = pl.BlockSpec((tm, tg), lambda i, j: (i, j))
    return pl.pallas_call(
        body, name=name, grid=(s // tm, nb),
        in_specs=[pl.BlockSpec((tm, d), lambda i, j: (i, 0)), pl.BlockSpec((tg, d), lambda i, j: (j, 0)),
                  pl.BlockSpec((tg, d), lambda i, j: (j + nb, 0))],
        out_specs=[col] * 3, out_shape=[jax.ShapeDtypeStruct((s, f), bf16)] * 3, compiler_params=_params(2),
    )(h, w_t, w_t)


def _ffn_out_dgrad_swiglu(dy, w_out, g, u, name):
    s, d = dy.shape
    f = g.shape[1]
    tm = _row_tile(s, 2048)
    tg = GLU_TILE

    def body(dy_ref, w_ref, g_ref, u_ref, dg_ref, du_ref):
        dv = lax.dot_general(dy_ref[...], w_ref[...], _NT, preferred_element_type=f32)
        gv, uv = g_ref[...].astype(f32), u_ref[...].astype(f32)
        sg = jax.nn.sigmoid(gv)
        dg_ref[...] = (dv * uv * (sg * (1.0 + gv * (1.0 - sg)))).astype(bf16)
        du_ref[...] = (dv * (gv * sg)).astype(bf16)

    col = pl.BlockSpec((tm, tg), lambda i, j: (i, j))
    return pl.pallas_call(
        body, name=name, grid=(s // tm, f // tg),
        in_specs=[pl.BlockSpec((tm, d), lambda i, j: (i, 0)), pl.BlockSpec((tg, d), lambda i, j: (j, 0)), col, col],
        out_specs=[col] * 2, out_shape=[jax.ShapeDtypeStruct((s, f), bf16)] * 2, compiler_params=_params(2),
    )(dy, w_out, g, u)


def _wgrad_stack(parts, h, name):
    s, m = parts[0].shape
    d = h.shape[1]
    tm = 256
    nb = m // tm
    n = len(parts)

    def body(*refs):
        i = pl.program_id(0)
        for p in range(n):
            @pl.when(i // nb == p)
            def _(p=p):
                refs[n + 1][...] = lax.dot_general(refs[p][...], refs[n][...], _TN, preferred_element_type=f32).astype(bf16)

    a_specs = [pl.BlockSpec((s, tm), lambda i, p=p: (0, jnp.clip(i - p * nb, 0, nb - 1))) for p in range(n)]
    return pl.pallas_call(
        body, name=name, grid=(n * nb,), in_specs=a_specs + [pl.BlockSpec((s, d), lambda i: (0, 0))],
        out_specs=pl.BlockSpec((tm, d), lambda i: (i, 0)),
        out_shape=jax.ShapeDtypeStruct((n * m, d), bf16), compiler_params=_params(1),
    )(*parts, h)


def _ada_wgrad(c_all, d_all, name):
    n, d = c_all.shape
    w = d_all.shape[1]

    def body(c_ref, d_ref, o_ref):
        eye = (lax.broadcasted_iota(jnp.int32, (n, n), 0) == lax.broadcasted_iota(jnp.int32, (n, n), 1)).astype(f32)
        ct = lax.dot_general(c_ref[...], eye, _TN, precision=lax.Precision.HIGHEST, preferred_element_type=f32)
        g = ct[:, 0:1] * d_ref[0:1, :]
        for bi in range(1, n):
            g = g + ct[:, bi:bi + 1] * d_ref[bi:bi + 1, :]
        o_ref[0] = g

    return pl.pallas_call(
        body, name=name, out_shape=jax.ShapeDtypeStruct((1, d, w), f32), compiler_params=_params(),
    )(c_all, d_all)


def _adamw(parts, w, m, v, name, mine=None, me=None):
    r, c = w.shape
    n_parts = parts.shape[0]
    row_tiles = [t for t in range(min(r, 256), 0, -1) if r % t == 0 and (t % 16 == 0 or t == r)]
    if row_tiles:
        tr, tc = row_tiles[0], c
    else:
        tr, tc = r, next(t for t in (256, LANES) if c % t == 0)

    def body(*refs):
        w_ref, m_ref, v_ref, g_ref, d_ref, nm_ref, nv_ref = refs[-7:]
        if mine is None:
            p_ref, = refs[:-7]
        else:
            me_ref, p_ref, own_ref = refs[:-7]

        def part(i):
            if mine is None:
                return p_ref[i].astype(f32)
            return jnp.where(me_ref[0] == i, own_ref[...], p_ref[i]).astype(f32)

        g = part(0)
        for i in range(1, n_parts):
            g = g + part(i)
        mm = ADAM_B1 * m_ref[...] + (1.0 - ADAM_B1) * g
        vv = ADAM_B2 * v_ref[...] + (1.0 - ADAM_B2) * (g * g)
        m_hat = mm / (1.0 - ADAM_B1 ** ADAM_STEP)
        v_hat = vv / (1.0 - ADAM_B2 ** ADAM_STEP)
        g_ref[...] = g
        d_ref[...] = -ADAM_LR * (m_hat / (jnp.sqrt(v_hat) + ADAM_EPS) + ADAM_WD * w_ref[...])
        nm_ref[...] = mm
        nv_ref[...] = vv

    out_shape = [jax.ShapeDtypeStruct((r, c), f32)] * 4
    if mine is None:
        spec = pl.BlockSpec((tr, tc), lambda i, j: (i, j))
        return pl.pallas_call(
            body, name=name, grid=(r // tr, c // tc),
            in_specs=[pl.BlockSpec((n_parts, tr, tc), lambda i, j: (0, i, j))] + [spec] * 3,
            out_specs=[spec] * 4, out_shape=out_shape, compiler_params=_params(2),
        )(parts, w, m, v)
    spec = pl.BlockSpec((tr, tc), lambda i, j, me_ref: (i, j))
    return pl.pallas_call(
        body, name=name, out_shape=out_shape, compiler_params=_params(2),
        grid_spec=pltpu.PrefetchScalarGridSpec(
            num_scalar_prefetch=1, grid=(r // tr, c // tc),
            in_specs=[pl.BlockSpec((n_parts, tr, tc), lambda i, j, me_ref: (0, i, j)),
                      pl.BlockSpec((None, tr, tc), lambda i, j, me_ref: (me_ref[0], i, j))] + [spec] * 3,
            out_specs=[spec] * 4),
    )(me, parts, mine, w, m, v)


def _me():
    return lax.axis_index("x"), lax.axis_index("y"), lax.axis_index("c")


def _gather_prologue(c, w_ada, b_mine, w_in_t, name):
    n_dev, d = N_DEV, c.shape[1]
    ada_w = w_ada.shape[1]

    def body(c_ref, w_ref, b_ref, win_ref, call_ref, ada_ref, gin_ref, cols_ref, send_sems, recv_sems, local_sems):
        x, y, cc = _me()
        me, sibling = (x, y, cc), (x, y, 1 - cc)
        chips = [(1 - x, y), (x, 1 - y), (1 - x, 1 - y)]
        outs = (call_ref, ada_ref, gin_ref)

        def rows(a, dev):
            return outs[a].at[4 * dev[0] + 2 * dev[1] + dev[2]]

        def copy(a, k, block, to, src=None):
            return pltpu.make_async_remote_copy(
                src_ref=rows(a, block) if src is None else src, dst_ref=rows(a, block),
                send_sem=send_sems.at[a, k], recv_sem=recv_sems.at[a, k], device_id=to, device_id_type=MESH)

        def begin(a, src):
            own = pltpu.make_async_copy(src, rows(a, me), local_sems.at[a])
            sends = [copy(a, 0, me, sibling, src=src)] + [copy(a, 1 + j, me, (*chip, cc), src=src) for j, chip in enumerate(chips)]
            for cp in [own] + sends:
                cp.start()
            return own, sends

        def finish(a, own, sends):
            passed = []
            for j, chip in enumerate(chips):
                copy(a, 1 + j, (*chip, cc), me).wait_recv()
                passed.append(copy(a, 4 + j, (*chip, cc), sibling))
                passed[-1].start()
            copy(a, 0, sibling, me).wait_recv()
            for j, chip in enumerate(chips):
                copy(a, 4 + j, (*chip, 1 - cc), me).wait_recv()
            for cp in sends + passed:
                cp.wait_send()
            own.wait()

        finish(0, *begin(0, c_ref))
        cols_ref[...] = (jnp.dot(call_ref[:, 0, :].astype(bf16), w_ref[...].astype(bf16), preferred_element_type=f32)
                         + b_ref[...])
        finish(1, *begin(1, cols_ref))
        finish(2, *begin(2, win_ref))

    vmem, hbm = pl.BlockSpec(memory_space=pltpu.VMEM), pl.BlockSpec(memory_space=pl.ANY)
    return pl.pallas_call(
        body, name=name, in_specs=[vmem, vmem, vmem, hbm], out_specs=[vmem, vmem, hbm],
        out_shape=[jax.ShapeDtypeStruct((n_dev, 1, d), f32), jax.ShapeDtypeStruct((n_dev, n_dev, ada_w), f32),
                   jax.ShapeDtypeStruct((n_dev,) + w_in_t.shape, w_in_t.dtype)],
        scratch_shapes=[pltpu.VMEM((n_dev, ada_w), f32), pltpu.SemaphoreType.DMA((3, 7)), pltpu.SemaphoreType.DMA((3, 7)),
                        pltpu.SemaphoreType.DMA((3,))],
        compiler_params=pltpu.CompilerParams(vmem_limit_bytes=VMEM_LIMIT),
    )(c, w_ada, b_mine, w_in_t)


_FLIPS = ((0, 0, 1), (1, 0, 0), (0, 1, 0), (1, 1, 0), (1, 0, 1), (0, 1, 1), (1, 1, 1))
_HBM = pl.BlockSpec(memory_space=pltpu.HBM)
_SEM = pl.BlockSpec(memory_space=pltpu.SEMAPHORE)


def _exchange_copies(scatter, srcs, lands, send_sems, recv_sems):
    x, y, c = _me()
    me_row = 4 * x + 2 * y + c
    out = []
    for k, (fx, fy, fc) in enumerate(_FLIPS):
        peer = (x ^ fx, y ^ fy, c ^ fc)
        peer_row = 4 * peer[0] + 2 * peer[1] + peer[2]
        for a in range(len(srcs)):
            out.append(pltpu.make_async_remote_copy(
                src_ref=srcs[a].at[peer_row] if scatter else srcs[a], dst_ref=lands[a].at[me_row],
                send_sem=send_sems.at[7 * a + k], recv_sem=recv_sems.at[7 * a + k], device_id=peer, device_id_type=MESH))
    return out


def _exchange_start(arrays, scatter, name, after=None):
    n = len(arrays)
    lands = [lax.empty(a.shape if scatter else (N_DEV,) + a.shape, a.dtype) for a in arrays]
    extra = [] if after is None else [after]

    def body(*refs):
        srcs, zones = refs[:n], refs[n:2 * n]
        send_sems, recv_sems = refs[2 * n + len(extra)], refs[2 * n + len(extra) + 1]
        token = refs[-1]
        for cp in _exchange_copies(scatter, srcs, zones, send_sems, recv_sems):
            cp.start()
        token[...] = jnp.zeros_like(token)

    thru = [pltpu.HBM(a.shape, a.dtype) for a in list(arrays) + lands]
    outs = pl.pallas_call(
        body, name=name,
        out_shape=(pltpu.SemaphoreType.DMA((7 * n,)), pltpu.SemaphoreType.DMA((7 * n,)), *thru, jax.ShapeDtypeStruct((8, LANES), f32)),
        in_specs=[_HBM] * (2 * n) + [pl.BlockSpec(memory_space=pl.ANY)] * len(extra),
        out_specs=(_SEM, _SEM, *[_HBM] * (2 * n), pl.BlockSpec(memory_space=pltpu.VMEM)),
        input_output_aliases={i: 2 + i for i in range(2 * n)},
        compiler_params=pltpu.CompilerParams(has_side_effects=pltpu.SideEffectType.DATAFLOW_SIDE_EFFECTING),
    )(*[pltpu.with_memory_space_constraint(a, pltpu.HBM) for a in list(arrays) + lands], *extra)
    return dict(n=n, scatter=scatter, sems=outs[:2], srcs=outs[2:2 + n], lands=outs[2 + n:2 + 2 * n], token=outs[-1])


def _exchange_wait(handle, after, name):
    n, scatter = handle["n"], handle["scatter"]

    def body(*refs):
        srcs, zones = refs[:n], refs[n:2 * n]
        send_sems, recv_sems = refs[2 * n], refs[2 * n + 1]
        for cp in _exchange_copies(scatter, srcs, zones, send_sems, recv_sems):
            cp.wait_send()
            cp.wait_recv()

    thru = [pltpu.HBM(a.shape, a.dtype) for a in list(handle["srcs"]) + list(handle["lands"])]
    outs = pl.pallas_call(
        body, name=name, out_shape=tuple(thru),
        in_specs=[_HBM] * (2 * n) + [_SEM, _SEM, pl.BlockSpec(memory_space=pl.ANY)], out_specs=tuple([_HBM] * (2 * n)),
        input_output_aliases={i: i for i in range(2 * n)},
        compiler_params=pltpu.CompilerParams(has_side_effects=pltpu.SideEffectType.DATAFLOW_SIDE_EFFECTING),
    )(*handle["srcs"], *handle["lands"], *handle["sems"], after)
    return list(outs[:n]), list(outs[n:])


def _cols_from_shards(g):
    return jnp.transpose(g, (1, 0, 2)).reshape(g.shape[1], -1)


def _shards_from_cols(a):
    return jnp.transpose(a.reshape(a.shape[0], N_DEV, -1), (1, 0, 2))


def _local_step(x, positions, ada, g_pre_mix, g_post_mix, b_f, sinks, g_pre_ffn, g_post_ffn, target,
                w_in_t, mix_weights, ffn_weights, on_grads):
    s, d = x.shape
    row = lambda v: v.reshape(1, -1)
    shift_m, scale_m, gate_m, shift_f, scale_f, gate_f = (ada[i:i + 1] for i in range(6))
    w_gate_t, w_qkv_t = w_in_t[F_OFF + N_HEADS:], w_in_t
    w_f_t = jnp.pad(w_in_t[F_OFF:F_OFF + N_HEADS], ((0, LANES - N_HEADS), (0, 0)))
    bf_row = jnp.pad(row(b_f), ((0, 0), (0, LANES - N_HEADS)))
    sink_rows = jnp.broadcast_to(sinks.reshape(N_HEADS, 1).astype(f32), (N_HEADS, LANES))
    inv_freq = 1.0 / (ROPE_THETA ** (jnp.arange(0, HEAD_DIM, 2, dtype=f32) / HEAD_DIM))
    cos, sin_s = _rope_tables(positions.reshape(s, 1), jnp.tile(inv_freq, 4).reshape(1, LANES), "rope_tables")

    h1, qa, ka, va, qb, kb, vb = _prenorm_proj_qkv(x, row(g_pre_mix), scale_m, shift_m, w_qkv_t, cos, sin_s, "prenorm_proj_qkv")
    gl = _matmul(h1, w_gate_t, "nt", bf16, "proj_gate")
    fl, cum_b = _forget_prep(h1, w_f_t, bf_row, "proj_forget_prep")
    o_a, lse_a = _attn_fwd(qa, ka, va, "swa_fwd", sink_rows=sink_rows, window=WINDOW, t=2048)
    o_b, lse_b = _attn_fwd(qb, kb, vb, "fox_fwd", cum_b=cum_b, t=1024)
    everything_before = (gl[:8, :LANES] + o_a[:8, :LANES] + o_b[:8, :LANES]).astype(f32)
    w_branch_a, w_branch_b, w_out = mix_weights(everything_before)
    ba, bb, merged = _branch_merge(o_a, o_b, w_branch_a, w_branch_b, gl, "branch_merge")
    y1, x2, h2 = _out_proj_postnorm_prenorm(merged, w_out, x, row(g_post_mix), gate_m, row(g_pre_ffn), scale_f, shift_f,
                                            "out_proj_norms")

    w_ffn_in_t, w_ffn_out = ffn_weights(h2)
    g_ff, u_ff, act = _ffn_in_swiglu(h2, w_ffn_in_t, "ffn_in_swiglu")
    loss_row, d_out, d_y2, vec_pf = _out_proj_loss_tail(act, w_ffn_out, x2, row(g_post_ffn), gate_f, target, "ffn_out_loss_tail")

    g_w_ffn_out = _matmul(act, d_y2, "tn", bf16, "ffn_out_wgrad")
    dg_ff, du_ff = _ffn_out_dgrad_swiglu(d_y2, w_ffn_out, g_ff, u_ff, "ffn_out_dgrad_swiglu")
    g_w_ffn_in_t = _wgrad_stack([dg_ff, du_ff], h2, "ffn_in_wgrad")
    sent = on_grads(dict(w_ffn_in=g_w_ffn_in_t, w_ffn_out=g_w_ffn_out))
    d_x2, vec_nf, d_y1, vec_pm = _dgrad_prenorm_bwd(
        [(dg_ff, w_ffn_in_t, 0), (du_ff, w_ffn_in_t, 1)], x2, row(g_pre_ffn), scale_f, d_out, "ffn_in_dgrad_norms_bwd",
        after=sent, below=(y1, row(g_post_mix), gate_m))

    g_w_out = _matmul(merged, d_y1, "tn", bf16, "out_proj_wgrad")
    d_ba, d_bb, dgl = _out_dgrad_merge_bwd(d_y1, w_out, ba, bb, gl, "out_proj_dgrad_merge_bwd")
    g_w_branch_a = _matmul(o_a, d_ba, "tn", bf16, "branch_a_wgrad")
    g_w_branch_b = _matmul(o_b, d_bb, "tn", bf16, "branch_b_wgrad")
    sent = on_grads(dict(w_out=g_w_out, w_branch_a=g_w_branch_a, w_branch_b=g_w_branch_b))
    d_oa, delta_a, d_sink = _branch_dgrad_delta(d_ba, w_branch_a, o_a, "branch_a_dgrad_delta", lse=lse_a,
                                                sink_rows=sink_rows, after=sent)
    d_ob, delta_b = _branch_dgrad_delta(d_bb, w_branch_b, o_b, "branch_b_dgrad_delta", after=sent)
    dqa_t, dka, dva = _attn_bwd(qa, ka, va, d_oa, lse_a, delta_a, "swa_bwd", window=WINDOW, t=2048)
    dqb_t, dkb, dvb, dcs, rs = _attn_bwd(qb, kb, vb, d_ob, lse_b, delta_b, "fox_bwd", cum_b=cum_b, t=512)
    dqkv = _qkv_prep_bwd(dqa_t, dka, dva, dqb_t, dkb, dvb, cos, sin_s, "qkv_prep_bwd")
    dfl, vec_bf = _forget_prep_bwd(rs.reshape(N_HEADS, s), dcs, fl, bf_row, "forget_prep_bwd")
    g_w_in_t = jnp.concatenate([_matmul(dqkv, h1, "tn", bf16, "qkv_wgrad"), _matmul(dfl, h1, "tn", bf16, "forget_wgrad")[:N_HEADS],
                                _matmul(dgl, h1, "tn", bf16, "gate_wgrad")], axis=0)
    sent = on_grads(dict(w_in=g_w_in_t))
    grad_x, vec_nm = _dgrad_prenorm_bwd([(dgl, w_gate_t, 0), (dqkv, w_qkv_t, 0), (dfl, w_f_t, 0)], x, row(g_pre_mix),
                                        scale_m, d_x2, "in_proj_dgrad_prenorm_bwd", after=sent)

    d_ada = jnp.concatenate([vec_nm[0], vec_nm[1], vec_pm[0], vec_nf[0], vec_nf[1], vec_pf[0]])
    small = dict(b_ada=d_ada, g_pre_mix=vec_nm[2], g_post_mix=vec_pm[1], g_pre_ffn=vec_nf[2], g_post_ffn=vec_pf[1],
                 b_f=vec_bf[0, :N_HEADS], sinks=d_sink[:, 0], loss=loss_row[0, :1])
    return grad_x, small


_SMALL = (("b_ada", 6144), ("g_pre_mix", 1024), ("g_post_mix", 1024), ("g_pre_ffn", 1024), ("g_post_ffn", 1024),
          ("b_f", 128), ("sinks", 128), ("loss", 128))
_SMALL_ROWS = 88


def _pack_small(vals):
    parts = [jnp.pad(vals[k].reshape(-1).astype(f32), (0, n - vals[k].size)) for k, n in _SMALL]
    flat = jnp.concatenate(parts)
    return jnp.pad(flat, (0, _SMALL_ROWS * LANES - flat.size)).reshape(_SMALL_ROWS, LANES)


def _unpack_small(slab, shapes):
    flat, out, off = slab.reshape(-1), {}, 0
    for k, n in _SMALL:
        size = math.prod(shapes[k])
        out[k] = flat[off:off + size].reshape(shapes[k])
        off += n
    return out


def kernel(x, c, positions, w_ada, b_ada, g_pre_mix, g_post_mix, w_in, b_f, sinks, w_branch_a, w_branch_b, w_out, g_pre_ffn, g_post_ffn, w_ffn_in, w_ffn_out, loss_target, m_w_ada, m_b_ada, m_g_pre_mix, m_g_post_mix, m_w_in, m_b_f, m_sinks, m_w_branch_a, m_w_branch_b, m_w_out, m_g_pre_ffn, m_g_post_ffn, m_w_ffn_in, m_w_ffn_out, v_w_ada, v_b_ada, v_g_pre_mix, v_g_post_mix, v_w_in, v_b_f, v_sinks, v_w_branch_a, v_w_branch_b, v_w_out, v_g_pre_ffn, v_g_post_ffn, v_w_ffn_in, v_w_ffn_out):
    xi, yi, ci = _me()
    me = 4 * xi + 2 * yi + ci
    d = D_MODEL
    ada_w = w_ada.shape[2]

    transposed = ("w_in", "w_ffn_in")
    tr = lambda a: jnp.transpose(a[0])

    b_mine = lax.dynamic_slice(b_ada, (0, me * ada_w), (1, ada_w))
    c_all, ada_all, g_in = _gather_prologue(c, w_ada[0], b_mine, tr(w_in).astype(bf16), "gather_prologue")
    c_all = c_all.reshape(N_DEV, d)
    ada = lax.dynamic_index_in_dim(ada_all, me, axis=1, keepdims=False).reshape(6, d)
    late_mix = [w.astype(bf16) for w in (w_branch_a[0], w_branch_b[0], w_out[0])]
    late_ffn = [w.astype(bf16) for w in (tr(w_ffn_in), w_ffn_out[0])]
    mix_h = _exchange_start(late_mix, False, "gather_mix_start", after=g_in)
    ffn_h = _exchange_start(late_ffn, False, "gather_ffn_start", after=mix_h["token"])

    def mine_into(zone, block):
        return lax.dynamic_update_index_in_dim(zone, block, me, 0)

    def rows_from_shards(g):
        return g.reshape(g.shape[0] * g.shape[1], g.shape[2])

    def mix_weights(after):
        sent, zones = _exchange_wait(mix_h, after, "gather_mix_wait")
        g_ba, g_bb, g_out = (mine_into(z, w) for z, w in zip(zones, sent))
        return _cols_from_shards(g_ba), _cols_from_shards(g_bb), rows_from_shards(g_out)

    def ffn_weights(after):
        sent, zones = _exchange_wait(ffn_h, after, "gather_ffn_wait")
        g_fi, g_fo = (mine_into(z, w) for z, w in zip(zones, sent))
        return rows_from_shards(g_fi), rows_from_shards(g_fo)

    row_sharded = ("w_out", "w_ffn_out") + transposed
    in_flight = []

    def on_grads(group):
        sends = [g.reshape(N_DEV, g.shape[0] // N_DEV, g.shape[1]) if nm in row_sharded else _shards_from_cols(g)
                 for nm, g in group.items()]
        handle = _exchange_start(sends, True, "scatter_start_%d" % len(in_flight))
        in_flight.append((list(group), handle))
        return handle["token"]

    grad_x, small = _local_step(
        x[0], positions[0], ada + ffn_h["token"][0, 0], g_pre_mix[0], g_post_mix[0], b_f[0], sinks[0], g_pre_ffn[0],
        g_post_ffn[0], loss_target[0], rows_from_shards(g_in), mix_weights, ffn_weights, on_grads)

    ws = dict(w_in=(w_in, m_w_in, v_w_in), w_branch_a=(w_branch_a, m_w_branch_a, v_w_branch_a),
              w_branch_b=(w_branch_b, m_w_branch_b, v_w_branch_b), w_out=(w_out, m_w_out, v_w_out),
              w_ffn_in=(w_ffn_in, m_w_ffn_in, v_w_ffn_in), w_ffn_out=(w_ffn_out, m_w_ffn_out, v_w_ffn_out))
    res = {}

    def finish_group(gi, after):
        names, handle = in_flight[gi]
        sends, zones = _exchange_wait(handle, after, "scatter_wait_%d" % gi)
        for nm, zone, sent in zip(names, zones, sends):
            w, m, v = (tr(a) if nm in transposed else a[0] for a in ws[nm])
            out = _adamw(zone, w, m, v, "adamw_" + nm, mine=sent, me=me.reshape(1).astype(jnp.int32))
            after = out[0]
            res[nm] = [jnp.transpose(o) for o in out] if nm in transposed else out
        return after

    small_h = _exchange_start([_pack_small(small)], False, "gather_small_start", after=grad_x)
    done = finish_group(1, finish_group(0, small_h["token"]))
    (slab_mine,), (slab_zone,) = _exchange_wait(small_h, done, "gather_small_wait")
    slab_all = mine_into(slab_zone, slab_mine)
    small_w = dict(b_ada=b_ada, g_pre_mix=g_pre_mix, g_post_mix=g_post_mix, g_pre_ffn=g_pre_ffn, g_post_ffn=g_post_ffn,
                   b_f=b_f, sinks=sinks, loss=jnp.zeros((1,), f32))
    small_m = dict(b_ada=m_b_ada, g_pre_mix=m_g_pre_mix, g_post_mix=m_g_post_mix, g_pre_ffn=m_g_pre_ffn,
                   g_post_ffn=m_g_post_ffn, b_f=m_b_f, sinks=m_sinks, loss=jnp.zeros((1,), f32))
    small_v = dict(b_ada=v_b_ada, g_pre_mix=v_g_pre_mix, g_post_mix=v_g_post_mix, g_pre_ffn=v_g_pre_ffn,
                   g_post_ffn=v_g_post_ffn, b_f=v_b_f, sinks=v_sinks, loss=jnp.ones((1,), f32))
    shapes = {k: small_w[k].shape for k, _ in _SMALL}
    s_out = _adamw(slab_all, _pack_small(small_w), _pack_small(small_m), _pack_small(small_v), "adamw_small")
    s_grad, s_delta, s_m, s_v = (_unpack_small(o, shapes) for o in s_out)

    d_ada_all = lax.dynamic_slice(slab_all[:, :6144 // LANES, :].reshape(N_DEV, 6144), (0, me * ada_w), (N_DEV, ada_w))
    ada_parts = _ada_wgrad(c_all, d_ada_all, "ada_wgrad")

    res["w_ada"] = _adamw(ada_parts, w_ada[0], m_w_ada[0], v_w_ada[0], "adamw_w_ada")
    finish_group(2, res["w_ada"][0])

    order = ["w_ada", "b_ada", "g_pre_mix", "g_post_mix", "w_in", "b_f", "sinks", "w_branch_a", "w_branch_b", "w_out",
             "g_pre_ffn", "g_post_ffn", "w_ffn_in", "w_ffn_out"]
    outs = [s_grad["loss"].reshape(()), grad_x[None]]
    for which, small_o in enumerate((s_grad, s_delta, s_m, s_v)):
        for nm in order:
            outs.append(res[nm][which][None] if nm in res else small_o[nm])
    return tuple(outs)
```

```python
import math

import jax
import jax.numpy as jnp
from jax import lax
from jax.experimental import pallas as pl
from jax.experimental.pallas import tpu as pltpu

f32 = jnp.float32
bf16 = jnp.bfloat16

D_MODEL = 1024
HEAD_DIM = 64
N_HEADS = 8
N_PAIRS = 4
QKV_W = 2304
F_OFF = 2304
WINDOW = 128
ROPE_THETA = 10000.0
RMS_EPS = 1e-6
N_DEV = 8
ADAM_LR, ADAM_B1, ADAM_B2, ADAM_EPS, ADAM_WD, ADAM_STEP = 0.001, 0.9, 0.999, 1e-08, 0.01, 10
NEG = -1e30
L_ROW = (HEAD_DIM, 0)
LANES = 128
VMEM_LIMIT = 48 * 1024 * 1024
MESH = pl.DeviceIdType.MESH

_NT = (((1,), (1,)), ((), ()))
_TN = (((0,), (0,)), ((), ()))


def _params(n_grid=0):
    sem = ("arbitrary",) * n_grid if n_grid else None
    return pltpu.CompilerParams(dimension_semantics=sem, vmem_limit_bytes=VMEM_LIMIT)


def _row_tile(s, want):
    t = min(s, want)
    assert s % t == 0, (s, t)
    return t


MATMUL_VMEM_BUDGET = 40 * 1024 * 1024


def _matmul_tiles(m, n, k, a_item, b_item, o_item):
    def tiles(d):
        return [t for t in range(LANES, min(d, 2048) + 1, LANES) if d % t == 0] or [d]

    best = None
    for tm in tiles(m):
        for tn in tiles(n):
            vmem = 2 * (tm * k * a_item + tn * k * b_item + tm * tn * o_item) + tm * tn * 4
            if vmem > MATMUL_VMEM_BUDGET:
                continue
            traffic = m * k * a_item + n * k * b_item * (1 if tn == n else m // tm) + m * n * o_item
            steps = (m // tm) * (n // tn)
            key = (traffic, 0, steps) if steps >= 4 else (traffic, 1, -steps)
            if best is None or key < best[0]:
                best = (key, tm, tn)
    assert best is not None, (m, n, k)
    return best[1], best[2]


def _matmul(a, b, mode, out_dtype, name, after=None):
    if mode == "nn":
        (m, k), n = a.shape, b.shape[1]
    elif mode == "nt":
        (m, k), n = a.shape, b.shape[0]
    else:
        (k, m), n = a.shape, b.shape[1]
    tm, tn = _matmul_tiles(m, n, k, a.dtype.itemsize, b.dtype.itemsize, jnp.dtype(out_dtype).itemsize)
    if mode == "nn":
        a_spec, b_spec, dims = pl.BlockSpec((tm, k), lambda i, j: (i, 0)), pl.BlockSpec((k, tn), lambda i, j: (0, j)), None
    elif mode == "nt":
        a_spec, b_spec, dims = pl.BlockSpec((tm, k), lambda i, j: (i, 0)), pl.BlockSpec((tn, k), lambda i, j: (j, 0)), _NT
    else:
        a_spec, b_spec, dims = pl.BlockSpec((k, tm), lambda i, j: (0, i)), pl.BlockSpec((k, tn), lambda i, j: (0, j)), _TN

    def body(a_ref, b_ref, *rest):
        o_ref = rest[-1]
        av, bv = a_ref[...].astype(bf16), b_ref[...].astype(bf16)
        if dims is None:
            r = jnp.dot(av, bv, preferred_element_type=f32)
        else:
            r = lax.dot_general(av, bv, dims, preferred_element_type=f32)
        o_ref[...] = r.astype(out_dtype)

    extra = [] if after is None else [after]
    return pl.pallas_call(
        body, name=name, grid=(m // tm, n // tn), in_specs=[a_spec, b_spec] + [pl.BlockSpec(memory_space=pl.ANY)] * len(extra),
        out_specs=pl.BlockSpec((tm, tn), lambda i, j: (i, j)),
        out_shape=jax.ShapeDtypeStruct((m, n), out_dtype), compiler_params=_params(2),
    )(a, b, *extra)


def _rstd(v):
    return lax.rsqrt(jnp.mean(v * v, axis=-1, keepdims=True) + RMS_EPS)


def _row_spec(tm, d):
    return pl.BlockSpec((tm, d), lambda i: (i, 0))


def _vec_spec(d, rows=1):
    return pl.BlockSpec((rows, d), lambda i: (0, 0))


def _proj_spec(a, w, tm):
    return [_row_spec(tm, a.shape[1]), pl.BlockSpec(w.shape, lambda i: (0, 0))]


def _out_proj_postnorm_prenorm(a, w, x, g_post, gate, g_pre, scale, shift, name):
    s, d = x.shape
    tm = _row_tile(s, 512)

    def body(a_ref, w_ref, x_ref, gp_ref, gate_ref, g_ref, sc_ref, sh_ref, y_ref, x2_ref, h_ref):
        yv = jnp.dot(a_ref[...], w_ref[...], preferred_element_type=f32)
        y_ref[...] = yv
        x2 = x_ref[...] + gate_ref[...] * (yv * _rstd(yv) * gp_ref[...])
        x2_ref[...] = x2
        h_ref[...] = ((x2 * _rstd(x2) * g_ref[...]) * (1.0 + sc_ref[...]) + sh_ref[...]).astype(bf16)

    return pl.pallas_call(
        body, name=name, grid=(s // tm,), in_specs=_proj_spec(a, w, tm) + [_row_spec(tm, d)] + [_vec_spec(d)] * 5,
        out_specs=[_row_spec(tm, d)] * 3,
        out_shape=[jax.ShapeDtypeStruct((s, d), f32)] * 2 + [jax.ShapeDtypeStruct((s, d), bf16)], compiler_params=_params(1),
    )(a, w, x, g_post, gate, g_pre, scale, shift)


def _rms_bwd(u, v, r):
    return r * u - v * (r * r * r) * jnp.mean(u * v, axis=-1, keepdims=True)


def _out_proj_loss_tail(a, w, x, g, gate, target, name):
    s, d = x.shape
    tm = _row_tile(s, 512)

    def body(a_ref, w_ref, x_ref, g_ref, gate_ref, t_ref, loss_ref, do_ref, dy_ref, vec_ref):
        @pl.when(pl.program_id(0) == 0)
        def _():
            loss_ref[...] = jnp.zeros_like(loss_ref)
            vec_ref[...] = jnp.zeros_like(vec_ref)
        yv = jnp.dot(a_ref[...], w_ref[...], preferred_element_type=f32)
        r = _rstd(yv)
        yn = yv * r
        err = x_ref[...] + gate_ref[...] * (yn * g_ref[...]) - t_ref[...]
        loss_ref[...] += 0.5 * jnp.sum(jnp.mean(err * err, axis=-1, keepdims=True), axis=0, keepdims=True)
        dr = err / d
        do_ref[...] = dr
        dn = dr * gate_ref[...]
        vec_ref[0:1, :] += jnp.sum(dr * (yn * g_ref[...]), axis=0, keepdims=True)
        vec_ref[1:2, :] += jnp.sum(dn * yn, axis=0, keepdims=True)
        dy_ref[...] = _rms_bwd(dn * g_ref[...], yv, r).astype(bf16)

    return pl.pallas_call(
        body, name=name, grid=(s // tm,),
        in_specs=_proj_spec(a, w, tm) + [_row_spec(tm, d)] + [_vec_spec(d)] * 2 + [_row_spec(tm, d)],
        out_specs=[_vec_spec(LANES), _row_spec(tm, d), _row_spec(tm, d), _vec_spec(d, 8)],
        out_shape=[jax.ShapeDtypeStruct((1, LANES), f32), jax.ShapeDtypeStruct((s, d), f32),
                   jax.ShapeDtypeStruct((s, d), bf16), jax.ShapeDtypeStruct((8, d), f32)],
        compiler_params=_params(1),
    )(a, w, x, g, gate, target)


def _dgrad_prenorm_bwd(terms, x, g, scale, dres, name, after=None, below=None):
    s, d = x.shape
    n = len(terms)
    k = sum(a.shape[1] for a, _, _ in terms)
    row_bytes = 2 * (2 * k) + d * (4 + 2 * 4 * 3 + (2 * 4 + 2 * 2 if below else 0))
    tm = next(t for t in (512, 256, 128) if s % t == 0 and 4 * k * d + t * row_bytes <= MATMUL_VMEM_BUDGET)
    extra = [] if after is None else [after]

    def body(*refs):
        a_refs, b_refs = refs[:n], refs[n:2 * n]
        x_ref, g_ref, sc_ref, dr_ref = refs[2 * n:2 * n + 4]
        n_in = 2 * n + 4 + (3 if below else 0) + len(extra)
        dx_ref, vec_ref = refs[n_in], refs[n_in + 1]
        if below:
            y_ref, gp_ref, gate_ref = refs[2 * n + 4:2 * n + 7]
            dy_ref, vec2_ref = refs[n_in + 2], refs[n_in + 3]

        @pl.when(pl.program_id(0) == 0)
        def _():
            vec_ref[...] = jnp.zeros_like(vec_ref)
            if below:
                vec2_ref[...] = jnp.zeros_like(vec2_ref)
        dhv = jnp.dot(a_refs[0][...], b_refs[0][...], preferred_element_type=f32)
        for i in range(1, n):
            dhv = dhv + jnp.dot(a_refs[i][...], b_refs[i][...], preferred_element_type=f32)
        xv = x_ref[...]
        r = _rstd(xv)
        xn = xv * r
        dn = dhv * (1.0 + sc_ref[...])
        vec_ref[0:1, :] += jnp.sum(dhv, axis=0, keepdims=True)
        vec_ref[1:2, :] += jnp.sum(dhv * (xn * g_ref[...]), axis=0, keepdims=True)
        vec_ref[2:3, :] += jnp.sum(dn * xn, axis=0, keepdims=True)
        dx = dr_ref[...] + _rms_bwd(dn * g_ref[...], xv, r)
        dx_ref[...] = dx
        if below:
            yv = y_ref[...]
            ry = _rstd(yv)
            yn = yv * ry
            dny = dx * gate_ref[...]
            vec2_ref[0:1, :] += jnp.sum(dx * (yn * gp_ref[...]), axis=0, keepdims=True)
            vec2_ref[1:2, :] += jnp.sum(dny * yn, axis=0, keepdims=True)
            dy_ref[...] = _rms_bwd(dny * gp_ref[...], yv, ry).astype(bf16)

    in_specs = ([_row_spec(tm, a.shape[1]) for a, _, _ in terms]
                + [pl.BlockSpec((a.shape[1], d), lambda i, r=r: (r, 0)) for a, _, r in terms]
                + [_row_spec(tm, d)] + [_vec_spec(d)] * 2 + [_row_spec(tm, d)])
    out_specs = [_row_spec(tm, d), _vec_spec(d, 8)]
    out_shape = [jax.ShapeDtypeStruct((s, d), f32), jax.ShapeDtypeStruct((8, d), f32)]
    args = [a for a, _, _ in terms] + [b for _, b, _ in terms] + [x, g, scale, dres]
    if below:
        in_specs += [_row_spec(tm, d)] + [_vec_spec(d)] * 2
        out_specs += [_row_spec(tm, d), _vec_spec(d, 8)]
        out_shape += [jax.ShapeDtypeStruct((s, d), bf16), jax.ShapeDtypeStruct((8, d), f32)]
        args += list(below)
    return pl.pallas_call(
        body, name=name, grid=(s // tm,), in_specs=in_specs + [pl.BlockSpec(memory_space=pl.ANY)] * len(extra),
        out_specs=out_specs, out_shape=out_shape, compiler_params=_params(1),
    )(*args, *extra)


def _lane():
    return lax.broadcasted_iota(jnp.int32, (1, LANES), 1)


def _rope_tables(pos_col, inv_freq, name):
    s = pos_col.shape[0]

    def body(p_ref, f_ref, cos_ref, sin_ref):
        ang = p_ref[...].astype(f32) * f_ref[...]
        first_half = (_lane() % HEAD_DIM) < HEAD_DIM // 2
        cos_ref[...] = jnp.cos(ang)
        sn = jnp.sin(ang)
        sin_ref[...] = jnp.where(first_half, -sn, sn)

    return pl.pallas_call(
        body, name=name, out_shape=[jax.ShapeDtypeStruct((s, LANES), f32)] * 2, compiler_params=_params(),
    )(pos_col, inv_freq)


def _swap_halves(v):
    first_half = (_lane() % HEAD_DIM) < HEAD_DIM // 2
    return jnp.where(first_half, pltpu.roll(v, LANES - HEAD_DIM // 2, axis=1), pltpu.roll(v, HEAD_DIM // 2, axis=1))


def _prenorm_proj_qkv(x, g, mod_scale, mod_shift, w_qkv_t, cos, sin_s, name):
    s, d = x.shape
    tm = _row_tile(s, 512)
    scale = 1.0 / math.sqrt(HEAD_DIM)

    def body(x_ref, g_ref, msc_ref, msh_ref, w_ref, c_ref, s_ref, h_ref, qa_ref, ka_ref, va_ref, qb_ref, kb_ref, vb_ref):
        xv = x_ref[...]
        h = ((xv * _rstd(xv) * g_ref[...]) * (1.0 + msc_ref[...]) + msh_ref[...]).astype(bf16)
        h_ref[...] = h
        proj = lax.dot_general(h, w_ref[...], _NT, preferred_element_type=f32)
        cs, sn = c_ref[...], s_ref[...]
        low = _lane() < HEAD_DIM

        def blk(j):
            return proj[:, j * LANES:(j + 1) * LANES]

        def rope(v):
            return v * cs + _swap_halves(v) * sn

        def expand(v):
            other = pltpu.roll(v, HEAD_DIM, axis=1)
            return jnp.where(low, v, other), jnp.where(low, other, v)

        for j in range(N_PAIRS):
            qa_ref[:, j * LANES:(j + 1) * LANES] = (rope(blk(j)) * scale).astype(bf16)
            qb_ref[:, j * LANES:(j + 1) * LANES] = (blk(6 + j) * scale).astype(bf16)
            kb_ref[:, j * LANES:(j + 1) * LANES] = blk(10 + j).astype(bf16)
            vb_ref[:, j * LANES:(j + 1) * LANES] = blk(14 + j).astype(bf16)
        k0, k1 = expand(rope(blk(4)))
        v0, v1 = expand(blk(5))
        for j in range(N_PAIRS):
            ka_ref[:, j * LANES:(j + 1) * LANES] = (k0 if j < 2 else k1).astype(bf16)
            va_ref[:, j * LANES:(j + 1) * LANES] = (v0 if j < 2 else v1).astype(bf16)

    hw = N_PAIRS * LANES
    return pl.pallas_call(
        body, name=name, grid=(s // tm,),
        in_specs=[_row_spec(tm, d)] + [_vec_spec(d)] * 3
        + [pl.BlockSpec((QKV_W, d), lambda i: (0, 0)), _row_spec(tm, LANES), _row_spec(tm, LANES)],
        out_specs=[_row_spec(tm, d)] + [_row_spec(tm, hw)] * 6,
        out_shape=[jax.ShapeDtypeStruct((s, d), bf16)] + [jax.ShapeDtypeStruct((s, hw), bf16)] * 6, compiler_params=_params(1),
    )(x, g, mod_scale, mod_shift, w_qkv_t, cos, sin_s)


def _qkv_prep_bwd(dqa_t, dka, dva, dqb_t, dkb, dvb, cos, sin_s, name):
    s = dka.shape[0]
    tm = _row_tile(s, 256)
    scale = 1.0 / math.sqrt(HEAD_DIM)
    hw = N_PAIRS * LANES
    t_spec = pl.BlockSpec((hw, tm), lambda i: (0, i))

    def body(dqa_ref, dka_ref, dva_ref, dqb_ref, dkb_ref, dvb_ref, c_ref, s_ref, o_ref):
        cs, sn = c_ref[...], s_ref[...]
        low = _lane() < HEAD_DIM

        def blk(ref, j):
            return ref[:, j * LANES:(j + 1) * LANES].astype(f32)

        def blk_t(ref, j):
            return ref[j * LANES:(j + 1) * LANES, :].T

        def unrope(v):
            return v * cs + _swap_halves(v * sn)

        def fold(ref):
            a, b = blk(ref, 0) + blk(ref, 1), blk(ref, 2) + blk(ref, 3)
            kv0 = a + pltpu.roll(a, HEAD_DIM, axis=1)
            kv1 = b + pltpu.roll(b, HEAD_DIM, axis=1)
            return jnp.where(low, kv0, kv1)

        for j in range(N_PAIRS):
            o_ref[:, j * LANES:(j + 1) * LANES] = (unrope(blk_t(dqa_ref, j)) * scale).astype(bf16)
            o_ref[:, (6 + j) * LANES:(7 + j) * LANES] = (blk_t(dqb_ref, j) * scale).astype(bf16)
            o_ref[:, (10 + j) * LANES:(11 + j) * LANES] = blk(dkb_ref, j).astype(bf16)
            o_ref[:, (14 + j) * LANES:(15 + j) * LANES] = blk(dvb_ref, j).astype(bf16)
        o_ref[:, 4 * LANES:5 * LANES] = unrope(fold(dka_ref)).astype(bf16)
        o_ref[:, 5 * LANES:6 * LANES] = fold(dva_ref).astype(bf16)

    return pl.pallas_call(
        body, name=name, grid=(s // tm,),
        in_specs=[t_spec, _row_spec(tm, hw), _row_spec(tm, hw), t_spec, _row_spec(tm, hw), _row_spec(tm, hw)] + [_row_spec(tm, LANES)] * 2,
        out_specs=_row_spec(tm, QKV_W), out_shape=jax.ShapeDtypeStruct((s, QKV_W), bf16), compiler_params=_params(1),
    )(dqa_t, dka, dva, dqb_t, dkb, dvb, cos, sin_s)


def _cumsum_rows(v, reverse=False):
    n = v.shape[0]
    row = lax.broadcasted_iota(jnp.int32, v.shape, 0)
    sh = 1
    while sh < n:
        if reverse:
            v = v + jnp.where(row < n - sh, pltpu.roll(v, n - sh, axis=0), 0.0)
        else:
            v = v + jnp.where(row >= sh, pltpu.roll(v, sh, axis=0), 0.0)
        sh *= 2
    return v


def _log_sigmoid(z):
    return jnp.minimum(z, 0.0) - jnp.log1p(jnp.exp(-jnp.abs(z)))


def _forget_prep(h, w_f_t, bf_row, name):
    s, d = h.shape
    tm = _row_tile(s, 1024)

    def body(h_ref, w_ref, b_ref, f_ref, cb_ref, last_ref):
        @pl.when(pl.program_id(0) == 0)
        def _():
            last_ref[...] = jnp.zeros_like(last_ref)
        fl = lax.dot_general(h_ref[...], w_ref[...], _NT, preferred_element_type=f32)
        f_ref[...] = fl
        cum = _cumsum_rows(_log_sigmoid(fl + b_ref[...])) + last_ref[0:1, :]
        last_ref[0:1, :] = cum[tm - 1:tm, :]
        for hd in range(N_HEADS):
            cb_ref[:, hd * LANES:(hd + 1) * LANES] = jnp.broadcast_to(cum[:, hd:hd + 1], (tm, LANES))

    return pl.pallas_call(
        body, name=name, grid=(s // tm,),
        in_specs=[_row_spec(tm, d), pl.BlockSpec((LANES, d), lambda i: (0, 0)), _vec_spec(LANES)],
        out_specs=[_row_spec(tm, LANES), _row_spec(tm, N_HEADS * LANES)],
        out_shape=[jax.ShapeDtypeStruct((s, LANES), f32), jax.ShapeDtypeStruct((s, N_HEADS * LANES), f32)],
        scratch_shapes=[pltpu.VMEM((8, LANES), f32)], compiler_params=_params(1),
    )(h, w_f_t, bf_row)


def _forget_prep_bwd(rs, dcs, fl, bf_row, name):
    s = fl.shape[0]
    tm = _row_tile(s, 1024)
    n = s // tm

    def body(r_ref, c_ref, f_ref, b_ref, df_ref, db_ref, next_ref):
        @pl.when(pl.program_id(0) == 0)
        def _():
            next_ref[...] = jnp.zeros_like(next_ref)
            db_ref[...] = jnp.zeros_like(db_ref)
        eye = (lax.broadcasted_iota(jnp.int32, (N_HEADS, LANES), 0) == lax.broadcasted_iota(jnp.int32, (N_HEADS, LANES), 1)).astype(f32)
        dcum = lax.dot_general(r_ref[...], eye, _TN, precision=lax.Precision.HIGHEST, preferred_element_type=f32)
        for h in range(N_HEADS):
            dcum = dcum - jnp.where(_lane() == h, jnp.sum(c_ref[:, h * LANES:(h + 1) * LANES], axis=1, keepdims=True), 0.0)
        dlf = _cumsum_rows(dcum, reverse=True) + next_ref[0:1, :]
        next_ref[0:1, :] = dlf[0:1, :]
        z = f_ref[...] + b_ref[...]
        df = jnp.where(_lane() < N_HEADS, dlf * jax.nn.sigmoid(-z), 0.0)
        df_ref[...] = df.astype(bf16)
        db_ref[0:1, :] += jnp.sum(df, axis=0, keepdims=True)

    def rows(width):
        return pl.BlockSpec((tm, width), lambda i: (n - 1 - i, 0))

    return pl.pallas_call(
        body, name=name, grid=(n,),
        in_specs=[pl.BlockSpec((N_HEADS, tm), lambda i: (0, n - 1 - i)), rows(N_HEADS * LANES), rows(LANES), _vec_spec(LANES)],
        out_specs=[rows(LANES), _vec_spec(LANES, 8)],
        out_shape=[jax.ShapeDtypeStruct((s, LANES), bf16), jax.ShapeDtypeStruct((8, LANES), f32)],
        scratch_shapes=[pltpu.VMEM((8, LANES), f32)], compiler_params=_params(1),
    )(rs, dcs, fl, bf_row)


def _tile_mask(n_keys, n_queries, off, window):
    shape = (n_keys, n_queries)
    d = lax.broadcasted_iota(jnp.int32, shape, 1) - lax.broadcasted_iota(jnp.int32, shape, 0) + off
    valid = d >= 0
    return jnp.logical_and(valid, d < window) if window else valid


def _wide(v, t):
    return jnp.concatenate([v] * (t // LANES), axis=1)


def _attn_fwd(q, k, v, name, *, cum_b=None, sink_rows=None, window=None, t=256):
    s = q.shape[0]
    t = _row_tile(s, t)
    fox, has_sink = cum_b is not None, sink_rows is not None
    assert not window or (window % LANES == 0 and LANES + window <= s)

    def body(*refs):
        q_ref, k_ref, v_ref = refs[:3]
        rest = list(refs[3:])
        cb_ref = rest.pop(0) if fox else None
        sink_ref = rest.pop(0) if has_sink else None
        o_ref, lse_ref = rest
        i = pl.program_id(1)
        low = _lane() < HEAD_DIM
        top = lax.broadcasted_iota(jnp.int32, (LANES, 1), 0) < HEAD_DIM
        q2 = q_ref[...]
        zero = jnp.zeros_like(q2)
        qms = (jnp.where(low, q2, zero), jnp.where(low, zero, q2))

        def tile(k0, n_keys, off, carry, masked, queries=slice(0, t)):
            nq = queries.stop - queries.start
            kblk, vblk = k_ref[pl.ds(k0, n_keys), :], v_ref[pl.ds(k0, n_keys), :]
            valid = _tile_mask(n_keys, nq, off, window) if masked else None
            ones = jnp.ones_like(vblk)
            vs = tuple(jnp.where(_lane() == L_ROW[h], ones, vblk) for h in range(2))

            def scores(h):
                return lax.dot_general(kblk, qms[h][queries], _NT, preferred_element_type=f32)

            def softmax(h, sc):
                m = carry[h][0]
                if fox:
                    sc = sc - _wide(cb_ref[pl.ds(k0, n_keys), h * LANES:(h + 1) * LANES], nq)
                if masked:
                    sc = jnp.where(valid, sc, NEG)
                m_new = jnp.maximum(m, jnp.max(sc, axis=0, keepdims=True))
                return m_new, jnp.exp(m - m_new), jnp.exp(sc - m_new).astype(bf16)

            def update(h, m_new, alpha, p):
                return m_new, alpha * carry[h][1] + lax.dot_general(vs[h], p, _TN, preferred_element_type=f32)

            if window:
                return tuple(update(h, *softmax(h, scores(h))) for h in range(2))
            scs = [scores(h) for h in range(2)]
            stats = [softmax(h, scs[h]) for h in range(2)]
            return tuple(update(h, *stats[h]) for h in range(2))

        def start(nq):
            if has_sink:
                row = lax.broadcasted_iota(jnp.int32, (LANES, nq), 0)
                return tuple((_wide(sink_ref[h:h + 1, :], nq), (row == L_ROW[h]).astype(f32)) for h in range(2))
            return tuple((jnp.full((1, nq), NEG, f32), jnp.zeros((LANES, nq), f32)) for h in range(2))

        def finish(carry, queries):
            (m0, a0), (m1, a1) = carry
            l0, l1 = a0[L_ROW[0]:L_ROW[0] + 1, :], a1[L_ROW[1]:L_ROW[1] + 1, :]
            o_t = jnp.where(top, a0 * (1.0 / l0), a1 * (1.0 / l1))
            o_ref[queries, :] = o_t.T.astype(bf16)
            lse_ref[0:1, queries] = m0 + jnp.log(l0)
            lse_ref[1:2, queries] = m1 + jnp.log(l1)

        if window:
            for c in range(t // LANES):
                queries = slice(c * LANES, (c + 1) * LANES)
                q0 = i * t + c * LANES
                k0 = pl.multiple_of(jnp.maximum(q0 - window, 0), LANES)
                finish(tile(k0, LANES + window, q0 - k0, start(LANES), True, queries), queries)
        else:
            carry = lax.fori_loop(0, i, lambda kb, c: tile(pl.multiple_of(kb * t, t), t, 0, c, False), start(t))
            half, k_own = t // 2, pl.multiple_of(i * t, t)
            carry = tile(k_own, half, 0, carry, True)
            finish(tuple((m[:, :half], a[:, :half]) for m, a in carry), slice(0, half))
            carry = tuple((m[:, half:], a[:, half:]) for m, a in carry)
            finish(tile(pl.multiple_of(k_own + half, half), half, 0, carry, True, slice(half, t)), slice(half, t))

    q_spec = pl.BlockSpec((t, LANES), lambda j, i: (i, j))
    kv_spec = pl.BlockSpec((s, LANES), lambda j, i: (0, j))
    in_specs, args = [q_spec, kv_spec, kv_spec], [q, k, v]
    if fox:
        in_specs += [pl.BlockSpec((s, 2 * LANES), lambda j, i: (0, j))]
        args += [cum_b]
    if has_sink:
        in_specs += [pl.BlockSpec((None, 2, LANES), lambda j, i: (j, 0, 0))]
        args += [sink_rows.reshape(N_PAIRS, 2, LANES)]
    return pl.pallas_call(
        body, name=name, grid=(N_PAIRS, s // t), in_specs=in_specs,
        out_specs=[q_spec, pl.BlockSpec((None, 2, t), lambda j, i: (j, 0, i))],
        out_shape=[jax.ShapeDtypeStruct((s, N_PAIRS * LANES), bf16), jax.ShapeDtypeStruct((N_PAIRS, 2, s), f32)],
        compiler_params=_params(2),
    )(*args)


def _branch_dgrad_delta(db, w, o, name, *, lse=None, sink_rows=None, after=None):
    s, hw = o.shape
    tm = _row_tile(s, 512)
    has_sink = sink_rows is not None
    extra = [] if after is None else [after]

    def body(*refs):
        db_ref, w_ref, o_ref = refs[:3]
        outs = refs[3 + (2 if has_sink else 0) + len(extra):]
        do_ref, dl_ref = outs[:2]
        if has_sink:
            lse_ref, sink_ref = refs[3:5]
            ds_ref = outs[2]

            @pl.when(pl.program_id(0) == 0)
            def _():
                ds_ref[...] = jnp.zeros_like(ds_ref)
        do = lax.dot_general(db_ref[...], w_ref[...], _NT, preferred_element_type=f32).astype(bf16)
        do_ref[...] = do
        for j in range(N_PAIRS):
            cols = slice(j * LANES, (j + 1) * LANES)
            prod_t = (do[:, cols].astype(f32) * o_ref[:, cols].astype(f32)).T
            for h in range(2):
                dl = jnp.sum(prod_t[h * HEAD_DIM:(h + 1) * HEAD_DIM, :], axis=0, keepdims=True)
                dl_ref[j, h:h + 1, :] = dl
                if has_sink:
                    r = 2 * j + h
                    p_sink = jnp.exp(sink_ref[r:r + 1, 0:1] - lse_ref[j, h:h + 1, :])
                    ds_ref[r:r + 1, :] += -jnp.sum(p_sink * dl, axis=1, keepdims=True)

    rows_spec = pl.BlockSpec((N_PAIRS, 2, tm), lambda i: (0, 0, i))
    in_specs = [_row_spec(tm, db.shape[1]), pl.BlockSpec(w.shape, lambda i: (0, 0)), _row_spec(tm, hw)]
    args = [db, w, o]
    out_specs = [_row_spec(tm, hw), rows_spec]
    out_shape = [jax.ShapeDtypeStruct((s, hw), bf16), jax.ShapeDtypeStruct((N_PAIRS, 2, s), f32)]
    if has_sink:
        in_specs += [rows_spec, _vec_spec(LANES, N_HEADS)]
        args += [lse, sink_rows]
        out_specs += [_vec_spec(LANES, N_HEADS)]
        out_shape += [jax.ShapeDtypeStruct((N_HEADS, LANES), f32)]
    return pl.pallas_call(
        body, name=name, grid=(s // tm,), in_specs=in_specs + [pl.BlockSpec(memory_space=pl.ANY)] * len(extra),
        out_specs=out_specs, out_shape=out_shape, compiler_params=_params(1),
    )(*args, *extra)


def _attn_bwd(q, k, v, do, lse, delta, name, *, cum_b=None, window=None, t=256):
    s = q.shape[0]
    t = _row_tile(s, t)
    nblk = s // t
    fox = cum_b is not None
    assert not window or (window % LANES == 0 and LANES + window <= s)

    def body(*refs):
        k_ref, v_ref, q_ref, do_ref, lse_ref, dl_ref = refs[:6]
        rest = list(refs[6:])
        cb_ref = rest.pop(0) if fox else None
        dq_ref, dk_ref, dv_ref = rest[:3]
        dcs_ref, rs_ref = (rest[3], rest[4]) if fox else (None, None)
        dk_acc, dv_acc = rest[-2:]
        b = pl.program_id(1)
        k0 = pl.multiple_of(b * t, t)

        @pl.when(b == 0)
        def _():
            dq_ref[...] = jnp.zeros_like(dq_ref)
            if fox:
                rs_ref[...] = jnp.zeros_like(rs_ref)

        dk_acc[...] = jnp.zeros_like(dk_acc)
        dv_acc[...] = jnp.zeros_like(dv_acc)
        if fox:
            dcs_ref[...] = jnp.zeros_like(dcs_ref)
        low = _lane() < HEAD_DIM
        top = lax.broadcasted_iota(jnp.int32, (LANES, 1), 0) < HEAD_DIM
        kblk, vblk = k_ref[...], v_ref[...]
        k_t = kblk.astype(f32).T.astype(bf16)
        cks = [_wide(cb_ref[pl.ds(k0, t), h * LANES:(h + 1) * LANES], t) for h in range(2)] if fox else None

        def tile(q0, n_queries, off, masked, keys=slice(0, t)):
            cols = pl.ds(q0, n_queries)
            q2, do2 = q_ref[cols, :], do_ref[cols, :]
            zero = jnp.zeros_like(q2)
            valid = _tile_mask(keys.stop - keys.start, n_queries, off, window) if masked else None
            dq_parts = []
            for h in range(2):
                qm = jnp.where(low, q2, zero) if h == 0 else jnp.where(low, zero, q2)
                dom = jnp.where(low, do2, zero) if h == 0 else jnp.where(low, zero, do2)
                sc = lax.dot_general(kblk[keys], qm, _NT, preferred_element_type=f32)
                if fox:
                    sc = sc - cks[h][keys, :n_queries]
                if masked:
                    sc = jnp.where(valid, sc, NEG)
                p = jnp.exp(sc - lse_ref[h:h + 1, cols])
                dp = lax.dot_general(vblk[keys], dom, _NT, preferred_element_type=f32)
                ds = p * (dp - dl_ref[h:h + 1, cols])
                pb, dsb = p.astype(bf16), ds.astype(bf16)
                dv_acc[keys, :] += jnp.dot(pb, dom, preferred_element_type=f32)
                dk_acc[keys, :] += jnp.dot(dsb, qm, preferred_element_type=f32)
                dq_parts.append(jnp.dot(k_t[:, keys], dsb, preferred_element_type=f32))
                if fox:
                    dcs_ref[keys, h * LANES:(h + 1) * LANES] += sum(ds[:, g * LANES:(g + 1) * LANES]
                                                                    for g in range(n_queries // LANES))
                    rs_ref[h:h + 1, cols] += jnp.sum(ds, axis=0, keepdims=True)
            dq_ref[:, cols] += jnp.where(top, dq_parts[0], dq_parts[1])

        def later_block(qb, carry):
            tile(pl.multiple_of(qb * t, t), t, 0, False)
            return carry

        if window:
            for c in range(t // LANES):
                first = b * t + c * LANES
                q0 = pl.multiple_of(jnp.minimum(first, s - (LANES + window)), LANES)
                tile(q0, LANES + window, q0 - first, True, slice(c * LANES, (c + 1) * LANES))
        else:
            half = t // 2
            tile(k0, half, 0, True, slice(0, half))
            tile(pl.multiple_of(k0 + half, half), half, half, True)
            lax.fori_loop(b + 1, nblk, later_block, 0)
        dk_ref[...] = dk_acc[...].astype(bf16)
        dv_ref[...] = dv_acc[...].astype(bf16)

    kv_spec = pl.BlockSpec((t, LANES), lambda j, b: (b, j))
    seq_spec = pl.BlockSpec((s, LANES), lambda j, b: (0, j))
    rows_spec = pl.BlockSpec((None, 2, s), lambda j, b: (j, 0, 0))
    hw = N_PAIRS * LANES
    in_specs, args = [kv_spec, kv_spec, seq_spec, seq_spec, rows_spec, rows_spec], [k, v, q, do, lse, delta]
    out_specs = [pl.BlockSpec((LANES, s), lambda j, b: (j, 0)), kv_spec, kv_spec]
    out_shape = [jax.ShapeDtypeStruct((hw, s), f32), jax.ShapeDtypeStruct((s, hw), bf16), jax.ShapeDtypeStruct((s, hw), bf16)]
    if fox:
        in_specs += [pl.BlockSpec((s, 2 * LANES), lambda j, b: (0, j))]
        args += [cum_b]
        out_specs += [pl.BlockSpec((t, 2 * LANES), lambda j, b: (b, j)), rows_spec]
        out_shape += [jax.ShapeDtypeStruct((s, N_HEADS * LANES), f32), jax.ShapeDtypeStruct((N_PAIRS, 2, s), f32)]
    return pl.pallas_call(
        body, name=name, grid=(N_PAIRS, nblk), in_specs=in_specs, out_specs=out_specs, out_shape=out_shape,
        scratch_shapes=[pltpu.VMEM((t, LANES), f32)] * 2, compiler_params=_params(2),
    )(*args)


def _branch_merge(o_a, o_b, w_a, w_b, gl, name):
    s, k = o_a.shape
    d = w_a.shape[1]
    tm = _row_tile(s, 1024)

    def body(oa_ref, ob_ref, wa_ref, wb_ref, g_ref, ba_ref, bb_ref, m_ref):
        ba = jnp.dot(oa_ref[...], wa_ref[...], preferred_element_type=f32)
        bb = jnp.dot(ob_ref[...], wb_ref[...], preferred_element_type=f32)
        g0, g1 = jax.nn.sigmoid(g_ref[:, :d].astype(f32)), jax.nn.sigmoid(g_ref[:, d:].astype(f32))
        ba_ref[...] = ba.astype(bf16)
        bb_ref[...] = bb.astype(bf16)
        m_ref[...] = (g0 * ba + g1 * bb).astype(bf16)

    whole = pl.BlockSpec((k, d), lambda i: (0, 0))
    return pl.pallas_call(
        body, name=name, grid=(s // tm,),
        in_specs=[_row_spec(tm, k), _row_spec(tm, k), whole, whole, _row_spec(tm, 2 * d)],
        out_specs=[_row_spec(tm, d)] * 3, out_shape=[jax.ShapeDtypeStruct((s, d), bf16)] * 3, compiler_params=_params(1),
    )(o_a, o_b, w_a, w_b, gl)


def _out_dgrad_merge_bwd(dy, w_out, ba, bb, gl, name):
    s, d = ba.shape
    tm = _row_tile(s, 512)

    def body(dy_ref, w_ref, a_ref, b_ref, g_ref, da_ref, db_ref, dg_ref):
        dmv = lax.dot_general(dy_ref[...], w_ref[...], _NT, preferred_element_type=f32)
        g0, g1 = jax.nn.sigmoid(g_ref[:, :d].astype(f32)), jax.nn.sigmoid(g_ref[:, d:].astype(f32))
        da_ref[...] = (dmv * g0).astype(bf16)
        db_ref[...] = (dmv * g1).astype(bf16)
        dg_ref[:, :d] = (dmv * a_ref[...].astype(f32) * (g0 * (1.0 - g0))).astype(bf16)
        dg_ref[:, d:] = (dmv * b_ref[...].astype(f32) * (g1 * (1.0 - g1))).astype(bf16)

    return pl.pallas_call(
        body, name=name, grid=(s // tm,),
        in_specs=[_row_spec(tm, dy.shape[1]), pl.BlockSpec(w_out.shape, lambda i: (0, 0))] + [_row_spec(tm, d)] * 2
        + [_row_spec(tm, 2 * d)],
        out_specs=[_row_spec(tm, d)] * 2 + [_row_spec(tm, 2 * d)],
        out_shape=[jax.ShapeDtypeStruct((s, d), bf16)] * 2 + [jax.ShapeDtypeStruct((s, 2 * d), bf16)],
        compiler_params=_params(1),
    )(dy, w_out, ba, bb, gl)


GLU_TILE = 256


def _ffn_in_swiglu(h, w_t, name):
    s, d = h.shape
    f = w_t.shape[0] // 2
    tm = _row_tile(s, 2048)
    tg = GLU_TILE
    nb = f // tg

    def body(h_ref, wg_ref, wu_ref, g_ref, u_ref, act_ref):
        hv = h_ref[...]
        g = lax.dot_general(hv, wg_ref[...], _NT, preferred_element_type=f32)
        u = lax.dot_general(hv, wu_ref[...], _NT, preferred_element_type=f32)
        g_ref[...] = g.astype(bf16)
        u_ref[...] = u.astype(bf16)
        act_ref[...] = (g * jax.nn.sigmoid(g) * u).astype(bf16)

    col = pl.BlockSpec((tm, tg), lambda i, j: (i, j))
    return pl.pallas_call(
        body, name=name, grid=(s // tm, nb),
        in_specs=[pl.BlockSpec((tm, d), lambda i, j: (i, 0)), pl.BlockSpec((tg, d), lambda i, j: (j, 0)),
                  pl.BlockSpec((tg, d), lambda i, j: (j + nb, 0))],
        out_specs=[col] * 3, out_shape=[jax.ShapeDtypeStruct((s, f), bf16)] * 3, compiler_params=_params(2),
    )(h, w_t, w_t)


def _ffn_out_dgrad_swiglu(dy, w_out, g, u, name):
    s, d = dy.shape
    f = g.shape[1]
    tm = _row_tile(s, 2048)
    tg = GLU_TILE

    def body(dy_ref, w_ref, g_ref, u_ref, dg_ref, du_ref):
        dv = lax.dot_general(dy_ref[...], w_ref[...], _NT, preferred_element_type=f32)
        gv, uv = g_ref[...].astype(f32), u_ref[...].astype(f32)
        sg = jax.nn.sigmoid(gv)
        dg_ref[...] = (dv * uv * (sg * (1.0 + gv * (1.0 - sg)))).astype(bf16)
        du_ref[...] = (dv * (gv * sg)).astype(bf16)

    col = pl.BlockSpec((tm, tg), lambda i, j: (i, j))
    return pl.pallas_call(
        body, name=name, grid=(s // tm, f // tg),
        in_specs=[pl.BlockSpec((tm, d), lambda i, j: (i, 0)), pl.BlockSpec((tg, d), lambda i, j: (j, 0)), col, col],
        out_specs=[col] * 2, out_shape=[jax.ShapeDtypeStruct((s, f), bf16)] * 2, compiler_params=_params(2),
    )(dy, w_out, g, u)


def _wgrad_stack(parts, h, name):
    s, m = parts[0].shape
    d = h.shape[1]
    tm = 256
    nb = m // tm
    n = len(parts)

    def body(*refs):
        i = pl.program_id(0)
        for p in range(n):
            @pl.when(i // nb == p)
            def _(p=p):
                refs[n + 1][...] = lax.dot_general(refs[p][...], refs[n][...], _TN, preferred_element_type=f32).astype(bf16)

    a_specs = [pl.BlockSpec((s, tm), lambda i, p=p: (0, jnp.clip(i - p * nb, 0, nb - 1))) for p in range(n)]
    return pl.pallas_call(
        body, name=name, grid=(n * nb,), in_specs=a_specs + [pl.BlockSpec((s, d), lambda i: (0, 0))],
        out_specs=pl.BlockSpec((tm, d), lambda i: (i, 0)),
        out_shape=jax.ShapeDtypeStruct((n * m, d), bf16), compiler_params=_params(1),
    )(*parts, h)


def _ada_wgrad(c_all, d_all, name):
    n, d = c_all.shape
    w = d_all.shape[1]

    def body(c_ref, d_ref, o_ref):
        eye = (lax.broadcasted_iota(jnp.int32, (n, n), 0) == lax.broadcasted_iota(jnp.int32, (n, n), 1)).astype(f32)
        ct = lax.dot_general(c_ref[...], eye, _TN, precision=lax.Precision.HIGHEST, preferred_element_type=f32)
        g = ct[:, 0:1] * d_ref[0:1, :]
        for bi in range(1, n):
            g = g + ct[:, bi:bi + 1] * d_ref[bi:bi + 1, :]
        o_ref[0] = g

    return pl.pallas_call(
        body, name=name, out_shape=jax.ShapeDtypeStruct((1, d, w), f32), compiler_params=_params(),
    )(c_all, d_all)


def _adamw(parts, w, m, v, name, mine=None, me=None):
    r, c = w.shape
    n_parts = parts.shape[0]
    row_tiles = [t for t in range(min(r, 256), 0, -1) if r % t == 0 and (t % 16 == 0 or t == r)]
    if row_tiles:
        tr, tc = row_tiles[0], c
    else:
        tr, tc = r, next(t for t in (256, LANES) if c % t == 0)

    def body(*refs):
        w_ref, m_ref, v_ref, g_ref, d_ref, nm_ref, nv_ref = refs[-7:]
        if mine is None:
            p_ref, = refs[:-7]
        else:
            me_ref, p_ref, own_ref = refs[:-7]

        def part(i):
            if mine is None:
                return p_ref[i].astype(f32)
            return jnp.where(me_ref[0] == i, own_ref[...], p_ref[i]).astype(f32)

        g = part(0)
        for i in range(1, n_parts):
            g = g + part(i)
        mm = ADAM_B1 * m_ref[...] + (1.0 - ADAM_B1) * g
        vv = ADAM_B2 * v_ref[...] + (1.0 - ADAM_B2) * (g * g)
        m_hat = mm / (1.0 - ADAM_B1 ** ADAM_STEP)
        v_hat = vv / (1.0 - ADAM_B2 ** ADAM_STEP)
        g_ref[...] = g
        d_ref[...] = -ADAM_LR * (m_hat / (jnp.sqrt(v_hat) + ADAM_EPS) + ADAM_WD * w_ref[...])
        nm_ref[...] = mm
        nv_ref[...] = vv

    out_shape = [jax.ShapeDtypeStruct((r, c), f32)] * 4
    if mine is None:
        spec = pl.BlockSpec((tr, tc), lambda i, j: (i, j))
        return pl.pallas_call(
            body, name=name, grid=(r // tr, c // tc),
            in_specs=[pl.BlockSpec((n_parts, tr, tc), lambda i, j: (0, i, j))] + [spec] * 3,
            out_specs=[spec] * 4, out_shape=out_shape, compiler_params=_params(2),
        )(parts, w, m, v)
    spec = pl.BlockSpec((tr, tc), lambda i, j, me_ref: (i, j))
    return pl.pallas_call(
        body, name=name, out_shape=out_shape, compiler_params=_params(2),
        grid_spec=pltpu.PrefetchScalarGridSpec(
            num_scalar_prefetch=1, grid=(r // tr, c // tc),
            in_specs=[pl.BlockSpec((n_parts, tr, tc), lambda i, j, me_ref: (0, i, j)),
                      pl.BlockSpec((None, tr, tc), lambda i, j, me_ref: (me_ref[0], i, j))] + [spec] * 3,
            out_specs=[spec] * 4),
    )(me, parts, mine, w, m, v)


def _me():
    return lax.axis_index("x"), lax.axis_index("y"), lax.axis_index("c")


def _gather_prologue(c, w_ada, b_mine, w_in_t, name):
    n_dev, d = N_DEV, c.shape[1]
    ada_w = w_ada.shape[1]

    def body(c_ref, w_ref, b_ref, win_ref, call_ref, ada_ref, gin_ref, cols_ref, send_sems, recv_sems, local_sems):
        x, y, cc = _me()
        me, sibling = (x, y, cc), (x, y, 1 - cc)
        chips = [(1 - x, y), (x, 1 - y), (1 - x, 1 - y)]
        outs = (call_ref, ada_ref, gin_ref)

        def rows(a, dev):
            return outs[a].at[4 * dev[0] + 2 * dev[1] + dev[2]]

        def copy(a, k, block, to, src=None):
            return pltpu.make_async_remote_copy(
                src_ref=rows(a, block) if src is None else src, dst_ref=rows(a, block),
                send_sem=send_sems.at[a, k], recv_sem=recv_sems.at[a, k], device_id=to, device_id_type=MESH)

        def begin(a, src):
            own = pltpu.make_async_copy(src, rows(a, me), local_sems.at[a])
            sends = [copy(a, 0, me, sibling, src=src)] + [copy(a, 1 + j, me, (*chip, cc), src=src) for j, chip in enumerate(chips)]
            for cp in [own] + sends:
                cp.start()
            return own, sends

        def finish(a, own, sends):
            passed = []
            for j, chip in enumerate(chips):
                copy(a, 1 + j, (*chip, cc), me).wait_recv()
                passed.append(copy(a, 4 + j, (*chip, cc), sibling))
                passed[-1].start()
            copy(a, 0, sibling, me).wait_recv()
            for j, chip in enumerate(chips):
                copy(a, 4 + j, (*chip, 1 - cc), me).wait_recv()
            for cp in sends + passed:
                cp.wait_send()
            own.wait()

        finish(0, *begin(0, c_ref))
        cols_ref[...] = (jnp.dot(call_ref[:, 0, :].astype(bf16), w_ref[...].astype(bf16), preferred_element_type=f32)
                         + b_ref[...])
        finish(1, *begin(1, cols_ref))
        finish(2, *begin(2, win_ref))

    vmem, hbm = pl.BlockSpec(memory_space=pltpu.VMEM), pl.BlockSpec(memory_space=pl.ANY)
    return pl.pallas_call(
        body, name=name, in_specs=[vmem, vmem, vmem, hbm], out_specs=[vmem, vmem, hbm],
        out_shape=[jax.ShapeDtypeStruct((n_dev, 1, d), f32), jax.ShapeDtypeStruct((n_dev, n_dev, ada_w), f32),
                   jax.ShapeDtypeStruct((n_dev,) + w_in_t.shape, w_in_t.dtype)],
        scratch_shapes=[pltpu.VMEM((n_dev, ada_w), f32), pltpu.SemaphoreType.DMA((3, 7)), pltpu.SemaphoreType.DMA((3, 7)),
                        pltpu.SemaphoreType.DMA((3,))],
        compiler_params=pltpu.CompilerParams(vmem_limit_bytes=VMEM_LIMIT),
    )(c, w_ada, b_mine, w_in_t)


_FLIPS = ((0, 0, 1), (1, 0, 0), (0, 1, 0), (1, 1, 0), (1, 0, 1), (0, 1, 1), (1, 1, 1))
_HBM = pl.BlockSpec(memory_space=pltpu.HBM)
_SEM = pl.BlockSpec(memory_space=pltpu.SEMAPHORE)


def _exchange_copies(scatter, srcs, lands, send_sems, recv_sems):
    x, y, c = _me()
    me_row = 4 * x + 2 * y + c
    out = []
    for k, (fx, fy, fc) in enumerate(_FLIPS):
        peer = (x ^ fx, y ^ fy, c ^ fc)
        peer_row = 4 * peer[0] + 2 * peer[1] + peer[2]
        for a in range(len(srcs)):
            out.append(pltpu.make_async_remote_copy(
                src_ref=srcs[a].at[peer_row] if scatter else srcs[a], dst_ref=lands[a].at[me_row],
                send_sem=send_sems.at[7 * a + k], recv_sem=recv_sems.at[7 * a + k], device_id=peer, device_id_type=MESH))
    return out


def _own_copies(srcs, lands, own_sems):
    x, y, c = _me()
    return [pltpu.make_async_copy(srcs[a], lands[a].at[4 * x + 2 * y + c], own_sems.at[a]) for a in range(len(srcs))]


def _exchange_start(arrays, scatter, name, after=None):
    n = len(arrays)
    lands = [lax.empty(a.shape if scatter else (N_DEV,) + a.shape, a.dtype) for a in arrays]
    extra = [] if after is None else [after]

    def body(*refs):
        srcs, zones = refs[:n], refs[n:2 * n]
        send_sems, recv_sems, own_sems = refs[2 * n + len(extra):2 * n + len(extra) + 3]
        token = refs[-1]
        for cp in _exchange_copies(scatter, srcs, zones, send_sems, recv_sems):
            cp.start()
        for cp in [] if scatter else _own_copies(srcs, zones, own_sems):
            cp.start()
        token[...] = jnp.zeros_like(token)

    thru = [pltpu.HBM(a.shape, a.dtype) for a in list(arrays) + lands]
    outs = pl.pallas_call(
        body, name=name,
        out_shape=(pltpu.SemaphoreType.DMA((7 * n,)), pltpu.SemaphoreType.DMA((7 * n,)), pltpu.SemaphoreType.DMA((n,)), *thru,
                   jax.ShapeDtypeStruct((8, LANES), f32)),
        in_specs=[_HBM] * (2 * n) + [pl.BlockSpec(memory_space=pl.ANY)] * len(extra),
        out_specs=(_SEM, _SEM, _SEM, *[_HBM] * (2 * n), pl.BlockSpec(memory_space=pltpu.VMEM)),
        input_output_aliases={i: 3 + i for i in range(2 * n)},
        compiler_params=pltpu.CompilerParams(has_side_effects=pltpu.SideEffectType.DATAFLOW_SIDE_EFFECTING),
    )(*[pltpu.with_memory_space_constraint(a, pltpu.HBM) for a in list(arrays) + lands], *extra)
    return dict(n=n, scatter=scatter, sems=outs[:3], srcs=outs[3:3 + n], lands=outs[3 + n:3 + 2 * n], token=outs[-1])


def _exchange_wait(handle, after, name):
    n, scatter = handle["n"], handle["scatter"]

    def body(*refs):
        srcs, zones = refs[:n], refs[n:2 * n]
        send_sems, recv_sems, own_sems = refs[2 * n:2 * n + 3]
        for cp in _exchange_copies(scatter, srcs, zones, send_sems, recv_sems):
            cp.wait_send()
            cp.wait_recv()
        for cp in [] if scatter else _own_copies(srcs, zones, own_sems):
            cp.wait()

    thru = [pltpu.HBM(a.shape, a.dtype) for a in list(handle["srcs"]) + list(handle["lands"])]
    outs = pl.pallas_call(
        body, name=name, out_shape=tuple(thru),
        in_specs=[_HBM] * (2 * n) + [_SEM, _SEM, _SEM, pl.BlockSpec(memory_space=pl.ANY)], out_specs=tuple([_HBM] * (2 * n)),
        input_output_aliases={i: i for i in range(2 * n)},
        compiler_params=pltpu.CompilerParams(has_side_effects=pltpu.SideEffectType.DATAFLOW_SIDE_EFFECTING),
    )(*handle["srcs"], *handle["lands"], *handle["sems"], after)
    return list(outs[:n]), list(outs[n:])


def _cols_from_shards(g):
    return jnp.transpose(g, (1, 0, 2)).reshape(g.shape[1], -1)


def _shards_from_cols(a):
    return jnp.transpose(a.reshape(a.shape[0], N_DEV, -1), (1, 0, 2))


def _local_step(x, positions, ada, g_pre_mix, g_post_mix, b_f, sinks, g_pre_ffn, g_post_ffn, target,
                w_in_t, mix_weights, ffn_weights, on_grads):
    s, d = x.shape
    row = lambda v: v.reshape(1, -1)
    shift_m, scale_m, gate_m, shift_f, scale_f, gate_f = (ada[i:i + 1] for i in range(6))
    w_gate_t, w_qkv_t = w_in_t[F_OFF + N_HEADS:], w_in_t
    w_f_t = jnp.pad(w_in_t[F_OFF:F_OFF + N_HEADS], ((0, LANES - N_HEADS), (0, 0)))
    bf_row = jnp.pad(row(b_f), ((0, 0), (0, LANES - N_HEADS)))
    sink_rows = jnp.broadcast_to(sinks.reshape(N_HEADS, 1).astype(f32), (N_HEADS, LANES))
    inv_freq = 1.0 / (ROPE_THETA ** (jnp.arange(0, HEAD_DIM, 2, dtype=f32) / HEAD_DIM))
    cos, sin_s = _rope_tables(positions.reshape(s, 1), jnp.tile(inv_freq, 4).reshape(1, LANES), "rope_tables")

    h1, qa, ka, va, qb, kb, vb = _prenorm_proj_qkv(x, row(g_pre_mix), scale_m, shift_m, w_qkv_t, cos, sin_s, "prenorm_proj_qkv")
    gl = _matmul(h1, w_gate_t, "nt", bf16, "proj_gate")
    fl, cum_b = _forget_prep(h1, w_f_t, bf_row, "proj_forget_prep")
    o_a, lse_a = _attn_fwd(qa, ka, va, "swa_fwd", sink_rows=sink_rows, window=WINDOW, t=2048)
    o_b, lse_b = _attn_fwd(qb, kb, vb, "fox_fwd", cum_b=cum_b, t=1024)
    everything_before = (gl[:8, :LANES] + o_a[:8, :LANES] + o_b[:8, :LANES]).astype(f32)
    w_branch_a, w_branch_b, w_out = mix_weights(everything_before)
    ba, bb, merged = _branch_merge(o_a, o_b, w_branch_a, w_branch_b, gl, "branch_merge")
    y1, x2, h2 = _out_proj_postnorm_prenorm(merged, w_out, x, row(g_post_mix), gate_m, row(g_pre_ffn), scale_f, shift_f,
                                            "out_proj_norms")

    w_ffn_in_t, w_ffn_out = ffn_weights(h2)
    g_ff, u_ff, act = _ffn_in_swiglu(h2, w_ffn_in_t, "ffn_in_swiglu")
    loss_row, d_out, d_y2, vec_pf = _out_proj_loss_tail(act, w_ffn_out, x2, row(g_post_ffn), gate_f, target, "ffn_out_loss_tail")

    g_w_ffn_out = _matmul(act, d_y2, "tn", bf16, "ffn_out_wgrad")
    dg_ff, du_ff = _ffn_out_dgrad_swiglu(d_y2, w_ffn_out, g_ff, u_ff, "ffn_out_dgrad_swiglu")
    g_w_ffn_in_t = _wgrad_stack([dg_ff, du_ff], h2, "ffn_in_wgrad")
    sent = on_grads(dict(w_ffn_in=g_w_ffn_in_t, w_ffn_out=g_w_ffn_out))
    d_x2, vec_nf, d_y1, vec_pm = _dgrad_prenorm_bwd(
        [(dg_ff, w_ffn_in_t, 0), (du_ff, w_ffn_in_t, 1)], x2, row(g_pre_ffn), scale_f, d_out, "ffn_in_dgrad_norms_bwd",
        after=sent, below=(y1, row(g_post_mix), gate_m))

    g_w_out = _matmul(merged, d_y1, "tn", bf16, "out_proj_wgrad")
    d_ba, d_bb, dgl = _out_dgrad_merge_bwd(d_y1, w_out, ba, bb, gl, "out_proj_dgrad_merge_bwd")
    g_w_branch_a = _matmul(o_a, d_ba, "tn", bf16, "branch_a_wgrad")
    g_w_branch_b = _matmul(o_b, d_bb, "tn", bf16, "branch_b_wgrad")
    sent = on_grads(dict(w_out=g_w_out, w_branch_a=g_w_branch_a, w_branch_b=g_w_branch_b))
    d_oa, delta_a, d_sink = _branch_dgrad_delta(d_ba, w_branch_a, o_a, "branch_a_dgrad_delta", lse=lse_a,
                                                sink_rows=sink_rows, after=sent)
    d_ob, delta_b = _branch_dgrad_delta(d_bb, w_branch_b, o_b, "branch_b_dgrad_delta", after=sent)
    dqa_t, dka, dva = _attn_bwd(qa, ka, va, d_oa, lse_a, delta_a, "swa_bwd", window=WINDOW, t=2048)
    dqb_t, dkb, dvb, dcs, rs = _attn_bwd(qb, kb, vb, d_ob, lse_b, delta_b, "fox_bwd", cum_b=cum_b, t=512)
    dqkv = _qkv_prep_bwd(dqa_t, dka, dva, dqb_t, dkb, dvb, cos, sin_s, "qkv_prep_bwd")
    dfl, vec_bf = _forget_prep_bwd(rs.reshape(N_HEADS, s), dcs, fl, bf_row, "forget_prep_bwd")
    g_w_in_t = jnp.concatenate([_matmul(dqkv, h1, "tn", bf16, "qkv_wgrad"), _matmul(dfl, h1, "tn", bf16, "forget_wgrad")[:N_HEADS],
                                _matmul(dgl, h1, "tn", bf16, "gate_wgrad")], axis=0)
    sent = on_grads(dict(w_in=g_w_in_t))
    grad_x, vec_nm = _dgrad_prenorm_bwd([(dgl, w_gate_t, 0), (dqkv, w_qkv_t, 0), (dfl, w_f_t, 0)], x, row(g_pre_mix),
                                        scale_m, d_x2, "in_proj_dgrad_prenorm_bwd", after=sent)

    d_ada = jnp.concatenate([vec_nm[0], vec_nm[1], vec_pm[0], vec_nf[0], vec_nf[1], vec_pf[0]])
    small = dict(b_ada=d_ada, g_pre_mix=vec_nm[2], g_post_mix=vec_pm[1], g_pre_ffn=vec_nf[2], g_post_ffn=vec_pf[1],
                 b_f=vec_bf[0, :N_HEADS], sinks=d_sink[:, 0], loss=loss_row[0, :1])
    return grad_x, small


_SMALL = (("b_ada", 6144), ("g_pre_mix", 1024), ("g_post_mix", 1024), ("g_pre_ffn", 1024), ("g_post_ffn", 1024),
          ("b_f", 128), ("sinks", 128), ("loss", 128))
_SMALL_ROWS = 88


def _pack_small(vals):
    parts = [jnp.pad(vals[k].reshape(-1).astype(f32), (0, n - vals[k].size)) for k, n in _SMALL]
    flat = jnp.concatenate(parts)
    return jnp.pad(flat, (0, _SMALL_ROWS * LANES - flat.size)).reshape(_SMALL_ROWS, LANES)


def _unpack_small(slab, shapes):
    flat, out, off = slab.reshape(-1), {}, 0
    for k, n in _SMALL:
        size = math.prod(shapes[k])
        out[k] = flat[off:off + size].reshape(shapes[k])
        off += n
    return out


def kernel(x, c, positions, w_ada, b_ada, g_pre_mix, g_post_mix, w_in, b_f, sinks, w_branch_a, w_branch_b, w_out, g_pre_ffn, g_post_ffn, w_ffn_in, w_ffn_out, loss_target, m_w_ada, m_b_ada, m_g_pre_mix, m_g_post_mix, m_w_in, m_b_f, m_sinks, m_w_branch_a, m_w_branch_b, m_w_out, m_g_pre_ffn, m_g_post_ffn, m_w_ffn_in, m_w_ffn_out, v_w_ada, v_b_ada, v_g_pre_mix, v_g_post_mix, v_w_in, v_b_f, v_sinks, v_w_branch_a, v_w_branch_b, v_w_out, v_g_pre_ffn, v_g_post_ffn, v_w_ffn_in, v_w_ffn_out):
    xi, yi, ci = _me()
    me = 4 * xi + 2 * yi + ci
    d = D_MODEL
    ada_w = w_ada.shape[2]

    transposed = ("w_in", "w_ffn_in")
    tr = lambda a: jnp.transpose(a[0])

    b_mine = lax.dynamic_slice(b_ada, (0, me * ada_w), (1, ada_w))
    c_all, ada_all, g_in = _gather_prologue(c, w_ada[0], b_mine, tr(w_in).astype(bf16), "gather_prologue")
    c_all = c_all.reshape(N_DEV, d)
    ada = lax.dynamic_index_in_dim(ada_all, me, axis=1, keepdims=False).reshape(6, d)
    late_mix = [w.astype(bf16) for w in (w_branch_a[0], w_branch_b[0], w_out[0])]
    late_ffn = [w.astype(bf16) for w in (tr(w_ffn_in), w_ffn_out[0])]
    mix_h = _exchange_start(late_mix, False, "gather_mix_start", after=g_in)
    ffn_h = _exchange_start(late_ffn, False, "gather_ffn_start", after=mix_h["token"])

    def rows_from_shards(g):
        return g.reshape(g.shape[0] * g.shape[1], g.shape[2])

    def mix_weights(after):
        _, (g_ba, g_bb, g_out) = _exchange_wait(mix_h, after, "gather_mix_wait")
        return _cols_from_shards(g_ba), _cols_from_shards(g_bb), rows_from_shards(g_out)

    def ffn_weights(after):
        _, (g_fi, g_fo) = _exchange_wait(ffn_h, after, "gather_ffn_wait")
        return rows_from_shards(g_fi), rows_from_shards(g_fo)

    row_sharded = ("w_out", "w_ffn_out") + transposed
    in_flight = []

    def on_grads(group):
        sends = [g.reshape(N_DEV, g.shape[0] // N_DEV, g.shape[1]) if nm in row_sharded else _shards_from_cols(g)
                 for nm, g in group.items()]
        handle = _exchange_start(sends, True, "scatter_start_%d" % len(in_flight))
        in_flight.append((list(group), handle))
        return handle["token"]

    grad_x, small = _local_step(
        x[0], positions[0], ada + ffn_h["token"][0, 0], g_pre_mix[0], g_post_mix[0], b_f[0], sinks[0], g_pre_ffn[0],
        g_post_ffn[0], loss_target[0], rows_from_shards(g_in), mix_weights, ffn_weights, on_grads)

    ws = dict(w_in=(w_in, m_w_in, v_w_in), w_branch_a=(w_branch_a, m_w_branch_a, v_w_branch_a),
              w_branch_b=(w_branch_b, m_w_branch_b, v_w_branch_b), w_out=(w_out, m_w_out, v_w_out),
              w_ffn_in=(w_ffn_in, m_w_ffn_in, v_w_ffn_in), w_ffn_out=(w_ffn_out, m_w_ffn_out, v_w_ffn_out))
    res = {}

    def finish_group(gi, after):
        names, handle = in_flight[gi]
        sends, zones = _exchange_wait(handle, after, "scatter_wait_%d" % gi)
        for nm, zone, sent in zip(names, zones, sends):
            w, m, v = (tr(a) if nm in transposed else a[0] for a in ws[nm])
            out = _adamw(zone, w, m, v, "adamw_" + nm, mine=sent, me=me.reshape(1).astype(jnp.int32))
            after = out[0]
            res[nm] = [jnp.transpose(o) for o in out] if nm in transposed else out
        return after

    small_h = _exchange_start([_pack_small(small)], False, "gather_small_start", after=grad_x)
    done = finish_group(1, finish_group(0, small_h["token"]))
    _, (slab_all,) = _exchange_wait(small_h, done, "gather_small_wait")
    small_w = dict(b_ada=b_ada, g_pre_mix=g_pre_mix, g_post_mix=g_post_mix, g_pre_ffn=g_pre_ffn, g_post_ffn=g_post_ffn,
                   b_f=b_f, sinks=sinks, loss=jnp.zeros((1,), f32))
    small_m = dict(b_ada=m_b_ada, g_pre_mix=m_g_pre_mix, g_post_mix=m_g_post_mix, g_pre_ffn=m_g_pre_ffn,
                   g_post_ffn=m_g_post_ffn, b_f=m_b_f, sinks=m_sinks, loss=jnp.zeros((1,), f32))
    small_v = dict(b_ada=v_b_ada, g_pre_mix=v_g_pre_mix, g_post_mix=v_g_post_mix, g_pre_ffn=v_g_pre_ffn,
                   g_post_ffn=v_g_post_ffn, b_f=v_b_f, sinks=v_sinks, loss=jnp.ones((1,), f32))
    shapes = {k: small_w[k].shape for k, _ in _SMALL}
    s_out = _adamw(slab_all, _pack_small(small_w), _pack_small(small_m), _pack_small(small_v), "adamw_small")
    s_grad, s_delta, s_m, s_v = (_unpack_small(o, shapes) for o in s_out)

    d_ada_all = lax.dynamic_slice(slab_all[:, :6144 // LANES, :].reshape(N_DEV, 6144), (0, me * ada_w), (N_DEV, ada_w))
    ada_parts = _ada_wgrad(c_all, d_ada_all, "ada_wgrad")

    res["w_ada"] = _adamw(ada_parts, w_ada[0], m_w_ada[0], v_w_ada[0], "adamw_w_ada")
    finish_group(2, res["w_ada"][0])

    order = ["w_ada", "b_ada", "g_pre_mix", "g_post_mix", "w_in", "b_f", "sinks", "w_branch_a", "w_branch_b", "w_out",
             "g_pre_ffn", "g_post_ffn", "w_ffn_in", "w_ffn_out"]
    outs = [s_grad["loss"].reshape(()), grad_x[None]]
    for which, small_o in enumerate((s_grad, s_delta, s_m, s_v)):
        for nm in order:
            outs.append(res[nm][which][None] if nm in res else small_o[nm])
    return tuple(outs)
```

```python
import math

import jax
import jax.numpy as jnp
from jax import lax
from jax.experimental import pallas as pl
from jax.experimental.pallas import tpu as pltpu

f32 = jnp.float32
bf16 = jnp.bfloat16

D_MODEL = 1024
HEAD_DIM = 64
N_HEADS = 8
N_PAIRS = 4
QKV_W = 2304
F_OFF = 2304
WINDOW = 128
ROPE_THETA = 10000.0
RMS_EPS = 1e-6
N_DEV = 8
ADAM_LR, ADAM_B1, ADAM_B2, ADAM_EPS, ADAM_WD, ADAM_STEP = 0.001, 0.9, 0.999, 1e-08, 0.01, 10
NEG = -1e30
L_ROW = (HEAD_DIM, 0)
LANES = 128
VMEM_LIMIT = 48 * 1024 * 1024
MESH = pl.DeviceIdType.MESH

_NT = (((1,), (1,)), ((), ()))
_TN = (((0,), (0,)), ((), ()))


def _params(n_grid=0):
    sem = ("arbitrary",) * n_grid if n_grid else None
    return pltpu.CompilerParams(dimension_semantics=sem, vmem_limit_bytes=VMEM_LIMIT)


def _row_tile(s, want):
    t = min(s, want)
    assert s % t == 0, (s, t)
    return t


MATMUL_VMEM_BUDGET = 40 * 1024 * 1024


def _matmul_tiles(m, n, k, a_item, b_item, o_item):
    def tiles(d):
        return [t for t in range(LANES, min(d, 2048) + 1, LANES) if d % t == 0] or [d]

    best = None
    for tm in tiles(m):
        for tn in tiles(n):
            vmem = 2 * (tm * k * a_item + tn * k * b_item + tm * tn * o_item) + tm * tn * 4
            if vmem > MATMUL_VMEM_BUDGET:
                continue
            traffic = m * k * a_item + n * k * b_item * (1 if tn == n else m // tm) + m * n * o_item
            steps = (m // tm) * (n // tn)
            key = (traffic, 0, steps) if steps >= 4 else (traffic, 1, -steps)
            if best is None or key < best[0]:
                best = (key, tm, tn)
    assert best is not None, (m, n, k)
    return best[1], best[2]


def _matmul(a, b, mode, out_dtype, name, after=None):
    if mode == "nn":
        (m, k), n = a.shape, b.shape[1]
    elif mode == "nt":
        (m, k), n = a.shape, b.shape[0]
    else:
        (k, m), n = a.shape, b.shape[1]
    tm, tn = _matmul_tiles(m, n, k, a.dtype.itemsize, b.dtype.itemsize, jnp.dtype(out_dtype).itemsize)
    if mode == "nn":
        a_spec, b_spec, dims = pl.BlockSpec((tm, k), lambda i, j: (i, 0)), pl.BlockSpec((k, tn), lambda i, j: (0, j)), None
    elif mode == "nt":
        a_spec, b_spec, dims = pl.BlockSpec((tm, k), lambda i, j: (i, 0)), pl.BlockSpec((tn, k), lambda i, j: (j, 0)), _NT
    else:
        a_spec, b_spec, dims = pl.BlockSpec((k, tm), lambda i, j: (0, i)), pl.BlockSpec((k, tn), lambda i, j: (0, j)), _TN

    def body(a_ref, b_ref, *rest):
        o_ref = rest[-1]
        av, bv = a_ref[...].astype(bf16), b_ref[...].astype(bf16)
        if dims is None:
            r = jnp.dot(av, bv, preferred_element_type=f32)
        else:
            r = lax.dot_general(av, bv, dims, preferred_element_type=f32)
        o_ref[...] = r.astype(out_dtype)

    extra = [] if after is None else [after]
    return pl.pallas_call(
        body, name=name, grid=(m // tm, n // tn), in_specs=[a_spec, b_spec] + [pl.BlockSpec(memory_space=pl.ANY)] * len(extra),
        out_specs=pl.BlockSpec((tm, tn), lambda i, j: (i, j)),
        out_shape=jax.ShapeDtypeStruct((m, n), out_dtype), compiler_params=_params(2),
    )(a, b, *extra)


def _rstd(v):
    return lax.rsqrt(jnp.mean(v * v, axis=-1, keepdims=True) + RMS_EPS)


def _row_spec(tm, d):
    return pl.BlockSpec((tm, d), lambda i: (i, 0))


def _vec_spec(d, rows=1):
    return pl.BlockSpec((rows, d), lambda i: (0, 0))


def _proj_spec(a, w, tm):
    return [_row_spec(tm, a.shape[1]), pl.BlockSpec(w.shape, lambda i: (0, 0))]


def _out_proj_postnorm_prenorm(a, w, x, g_post, gate, g_pre, scale, shift, name):
    s, d = x.shape
    tm = _row_tile(s, 512)

    def body(a_ref, w_ref, x_ref, gp_ref, gate_ref, g_ref, sc_ref, sh_ref, y_ref, x2_ref, h_ref):
        yv = jnp.dot(a_ref[...], w_ref[...], preferred_element_type=f32)
        y_ref[...] = yv
        x2 = x_ref[...] + gate_ref[...] * (yv * _rstd(yv) * gp_ref[...])
        x2_ref[...] = x2
        h_ref[...] = ((x2 * _rstd(x2) * g_ref[...]) * (1.0 + sc_ref[...]) + sh_ref[...]).astype(bf16)

    return pl.pallas_call(
        body, name=name, grid=(s // tm,), in_specs=_proj_spec(a, w, tm) + [_row_spec(tm, d)] + [_vec_spec(d)] * 5,
        out_specs=[_row_spec(tm, d)] * 3,
        out_shape=[jax.ShapeDtypeStruct((s, d), f32)] * 2 + [jax.ShapeDtypeStruct((s, d), bf16)], compiler_params=_params(1),
    )(a, w, x, g_post, gate, g_pre, scale, shift)


def _rms_bwd(u, v, r):
    return r * u - v * (r * r * r) * jnp.mean(u * v, axis=-1, keepdims=True)


def _out_proj_loss_tail(a, w, x, g, gate, target, name):
    s, d = x.shape
    tm = _row_tile(s, 512)

    def body(a_ref, w_ref, x_ref, g_ref, gate_ref, t_ref, loss_ref, do_ref, dy_ref, vec_ref):
        @pl.when(pl.program_id(0) == 0)
        def _():
            loss_ref[...] = jnp.zeros_like(loss_ref)
            vec_ref[...] = jnp.zeros_like(vec_ref)
        yv = jnp.dot(a_ref[...], w_ref[...], preferred_element_type=f32)
        r = _rstd(yv)
        yn = yv * r
        err = x_ref[...] + gate_ref[...] * (yn * g_ref[...]) - t_ref[...]
        loss_ref[...] += 0.5 * jnp.sum(jnp.mean(err * err, axis=-1, keepdims=True), axis=0, keepdims=True)
        dr = err / d
        do_ref[...] = dr
        dn = dr * gate_ref[...]
        vec_ref[0:1, :] += jnp.sum(dr * (yn * g_ref[...]), axis=0, keepdims=True)
        vec_ref[1:2, :] += jnp.sum(dn * yn, axis=0, keepdims=True)
        dy_ref[...] = _rms_bwd(dn * g_ref[...], yv, r).astype(bf16)

    return pl.pallas_call(
        body, name=name, grid=(s // tm,),
        in_specs=_proj_spec(a, w, tm) + [_row_spec(tm, d)] + [_vec_spec(d)] * 2 + [_row_spec(tm, d)],
        out_specs=[_vec_spec(LANES), _row_spec(tm, d), _row_spec(tm, d), _vec_spec(d, 8)],
        out_shape=[jax.ShapeDtypeStruct((1, LANES), f32), jax.ShapeDtypeStruct((s, d), f32),
                   jax.ShapeDtypeStruct((s, d), bf16), jax.ShapeDtypeStruct((8, d), f32)],
        compiler_params=_params(1),
    )(a, w, x, g, gate, target)


def _dgrad_prenorm_bwd(terms, x, g, scale, dres, name, after=None, below=None):
    s, d = x.shape
    n = len(terms)
    k = sum(a.shape[1] for a, _, _ in terms)
    row_bytes = 2 * (2 * k) + d * (4 + 2 * 4 * 3 + (2 * 4 + 2 * 2 if below else 0))
    tm = next(t for t in (512, 256, 128) if s % t == 0 and 4 * k * d + t * row_bytes <= MATMUL_VMEM_BUDGET)
    extra = [] if after is None else [after]

    def body(*refs):
        a_refs, b_refs = refs[:n], refs[n:2 * n]
        x_ref, g_ref, sc_ref, dr_ref = refs[2 * n:2 * n + 4]
        n_in = 2 * n + 4 + (3 if below else 0) + len(extra)
        dx_ref, vec_ref = refs[n_in], refs[n_in + 1]
        if below:
            y_ref, gp_ref, gate_ref = refs[2 * n + 4:2 * n + 7]
            dy_ref, vec2_ref = refs[n_in + 2], refs[n_in + 3]

        @pl.when(pl.program_id(0) == 0)
        def _():
            vec_ref[...] = jnp.zeros_like(vec_ref)
            if below:
                vec2_ref[...] = jnp.zeros_like(vec2_ref)
        dhv = jnp.dot(a_refs[0][...], b_refs[0][...], preferred_element_type=f32)
        for i in range(1, n):
            dhv = dhv + jnp.dot(a_refs[i][...], b_refs[i][...], preferred_element_type=f32)
        xv = x_ref[...]
        r = _rstd(xv)
        xn = xv * r
        dn = dhv * (1.0 + sc_ref[...])
        vec_ref[0:1, :] += jnp.sum(dhv, axis=0, keepdims=True)
        vec_ref[1:2, :] += jnp.sum(dhv * (xn * g_ref[...]), axis=0, keepdims=True)
        vec_ref[2:3, :] += jnp.sum(dn * xn, axis=0, keepdims=True)
        dx = dr_ref[...] + _rms_bwd(dn * g_ref[...], xv, r)
        dx_ref[...] = dx
        if below:
            yv = y_ref[...]
            ry = _rstd(yv)
            yn = yv * ry
            dny = dx * gate_ref[...]
            vec2_ref[0:1, :] += jnp.sum(dx * (yn * gp_ref[...]), axis=0, keepdims=True)
            vec2_ref[1:2, :] += jnp.sum(dny * yn, axis=0, keepdims=True)
            dy_ref[...] = _rms_bwd(dny * gp_ref[...], yv, ry).astype(bf16)

    in_specs = ([_row_spec(tm, a.shape[1]) for a, _, _ in terms]
                + [pl.BlockSpec((a.shape[1], d), lambda i, r=r: (r, 0)) for a, _, r in terms]
                + [_row_spec(tm, d)] + [_vec_spec(d)] * 2 + [_row_spec(tm, d)])
    out_specs = [_row_spec(tm, d), _vec_spec(d, 8)]
    out_shape = [jax.ShapeDtypeStruct((s, d), f32), jax.ShapeDtypeStruct((8, d), f32)]
    args = [a for a, _, _ in terms] + [b for _, b, _ in terms] + [x, g, scale, dres]
    if below:
        in_specs += [_row_spec(tm, d)] + [_vec_spec(d)] * 2
        out_specs += [_row_spec(tm, d), _vec_spec(d, 8)]
        out_shape += [jax.ShapeDtypeStruct((s, d), bf16), jax.ShapeDtypeStruct((8, d), f32)]
        args += list(below)
    return pl.pallas_call(
        body, name=name, grid=(s // tm,), in_specs=in_specs + [pl.BlockSpec(memory_space=pl.ANY)] * len(extra),
        out_specs=out_specs, out_shape=out_shape, compiler_params=_params(1),
    )(*args, *extra)


def _lane():
    return lax.broadcasted_iota(jnp.int32, (1, LANES), 1)


def _rope_tables(pos_col, inv_freq, name):
    s = pos_col.shape[0]

    def body(p_ref, f_ref, cos_ref, sin_ref):
        ang = p_ref[...].astype(f32) * f_ref[...]
        first_half = (_lane() % HEAD_DIM) < HEAD_DIM // 2
        cos_ref[...] = jnp.cos(ang)
        sn = jnp.sin(ang)
        sin_ref[...] = jnp.where(first_half, -sn, sn)

    return pl.pallas_call(
        body, name=name, out_shape=[jax.ShapeDtypeStruct((s, LANES), f32)] * 2, compiler_params=_params(),
    )(pos_col, inv_freq)


def _swap_halves(v):
    first_half = (_lane() % HEAD_DIM) < HEAD_DIM // 2
    return jnp.where(first_half, pltpu.roll(v, LANES - HEAD_DIM // 2, axis=1), pltpu.roll(v, HEAD_DIM // 2, axis=1))


def _prenorm_proj_qkv(x, g, mod_scale, mod_shift, w_qkv_t, cos, sin_s, name):
    s, d = x.shape
    tm = _row_tile(s, 512)
    scale = 1.0 / math.sqrt(HEAD_DIM)

    def body(x_ref, g_ref, msc_ref, msh_ref, w_ref, c_ref, s_ref, h_ref, qa_ref, ka_ref, va_ref, qb_ref, kb_ref, vb_ref):
        xv = x_ref[...]
        h = ((xv * _rstd(xv) * g_ref[...]) * (1.0 + msc_ref[...]) + msh_ref[...]).astype(bf16)
        h_ref[...] = h
        proj = lax.dot_general(h, w_ref[...], _NT, preferred_element_type=f32)
        cs, sn = c_ref[...], s_ref[...]
        low = _lane() < HEAD_DIM

        def blk(j):
            return proj[:, j * LANES:(j + 1) * LANES]

        def rope(v):
            return v * cs + _swap_halves(v) * sn

        def expand(v):
            other = pltpu.roll(v, HEAD_DIM, axis=1)
            return jnp.where(low, v, other), jnp.where(low, other, v)

        for j in range(N_PAIRS):
            qa_ref[:, j * LANES:(j + 1) * LANES] = (rope(blk(j)) * scale).astype(bf16)
            qb_ref[:, j * LANES:(j + 1) * LANES] = (blk(6 + j) * scale).astype(bf16)
            kb_ref[:, j * LANES:(j + 1) * LANES] = blk(10 + j).astype(bf16)
            vb_ref[:, j * LANES:(j + 1) * LANES] = blk(14 + j).astype(bf16)
        k0, k1 = expand(rope(blk(4)))
        v0, v1 = expand(blk(5))
        for j in range(N_PAIRS):
            ka_ref[:, j * LANES:(j + 1) * LANES] = (k0 if j < 2 else k1).astype(bf16)
            va_ref[:, j * LANES:(j + 1) * LANES] = (v0 if j < 2 else v1).astype(bf16)

    hw = N_PAIRS * LANES
    return pl.pallas_call(
        body, name=name, grid=(s // tm,),
        in_specs=[_row_spec(tm, d)] + [_vec_spec(d)] * 3
        + [pl.BlockSpec((QKV_W, d), lambda i: (0, 0)), _row_spec(tm, LANES), _row_spec(tm, LANES)],
        out_specs=[_row_spec(tm, d)] + [_row_spec(tm, hw)] * 6,
        out_shape=[jax.ShapeDtypeStruct((s, d), bf16)] + [jax.ShapeDtypeStruct((s, hw), bf16)] * 6, compiler_params=_params(1),
    )(x, g, mod_scale, mod_shift, w_qkv_t, cos, sin_s)


def _qkv_prep_bwd(dqa_t, dka, dva, dqb_t, dkb, dvb, cos, sin_s, name):
    s = dka.shape[0]
    tm = _row_tile(s, 256)
    scale = 1.0 / math.sqrt(HEAD_DIM)
    hw = N_PAIRS * LANES
    t_spec = pl.BlockSpec((hw, tm), lambda i: (0, i))

    def body(dqa_ref, dka_ref, dva_ref, dqb_ref, dkb_ref, dvb_ref, c_ref, s_ref, o_ref):
        cs, sn = c_ref[...], s_ref[...]
        low = _lane() < HEAD_DIM

        def blk(ref, j):
            return ref[:, j * LANES:(j + 1) * LANES].astype(f32)

        def blk_t(ref, j):
            return ref[j * LANES:(j + 1) * LANES, :].T

        def unrope(v):
            return v * cs + _swap_halves(v * sn)

        def fold(ref):
            a, b = blk(ref, 0) + blk(ref, 1), blk(ref, 2) + blk(ref, 3)
            kv0 = a + pltpu.roll(a, HEAD_DIM, axis=1)
            kv1 = b + pltpu.roll(b, HEAD_DIM, axis=1)
            return jnp.where(low, kv0, kv1)

        for j in range(N_PAIRS):
            o_ref[:, j * LANES:(j + 1) * LANES] = (unrope(blk_t(dqa_ref, j)) * scale).astype(bf16)
            o_ref[:, (6 + j) * LANES:(7 + j) * LANES] = (blk_t(dqb_ref, j) * scale).astype(bf16)
            o_ref[:, (10 + j) * LANES:(11 + j) * LANES] = blk(dkb_ref, j).astype(bf16)
            o_ref[:, (14 + j) * LANES:(15 + j) * LANES] = blk(dvb_ref, j).astype(bf16)
        o_ref[:, 4 * LANES:5 * LANES] = unrope(fold(dka_ref)).astype(bf16)
        o_ref[:, 5 * LANES:6 * LANES] = fold(dva_ref).astype(bf16)

    return pl.pallas_call(
        body, name=name, grid=(s // tm,),
        in_specs=[t_spec, _row_spec(tm, hw), _row_spec(tm, hw), t_spec, _row_spec(tm, hw), _row_spec(tm, hw)] + [_row_spec(tm, LANES)] * 2,
        out_specs=_row_spec(tm, QKV_W), out_shape=jax.ShapeDtypeStruct((s, QKV_W), bf16), compiler_params=_params(1),
    )(dqa_t, dka, dva, dqb_t, dkb, dvb, cos, sin_s)


def _cumsum_rows(v, reverse=False):
    n = v.shape[0]
    row = lax.broadcasted_iota(jnp.int32, v.shape, 0)
    sh = 1
    while sh < n:
        if reverse:
            v = v + jnp.where(row < n - sh, pltpu.roll(v, n - sh, axis=0), 0.0)
        else:
            v = v + jnp.where(row >= sh, pltpu.roll(v, sh, axis=0), 0.0)
        sh *= 2
    return v


def _sigmoid(z):
    return pl.reciprocal(1.0 + jnp.exp(-z), approx=True)


def _log_sigmoid(z):
    return jnp.minimum(z, 0.0) - jnp.log1p(jnp.exp(-jnp.abs(z)))


def _forget_prep(h, w_f_t, bf_row, name):
    s, d = h.shape
    tm = _row_tile(s, 1024)

    def body(h_ref, w_ref, b_ref, f_ref, cb_ref, last_ref):
        @pl.when(pl.program_id(0) == 0)
        def _():
            last_ref[...] = jnp.zeros_like(last_ref)
        fl = lax.dot_general(h_ref[...], w_ref[...], _NT, preferred_element_type=f32)
        f_ref[...] = fl
        cum = _cumsum_rows(_log_sigmoid(fl + b_ref[...])) + last_ref[0:1, :]
        last_ref[0:1, :] = cum[tm - 1:tm, :]
        for hd in range(N_HEADS):
            cb_ref[:, hd * LANES:(hd + 1) * LANES] = jnp.broadcast_to(cum[:, hd:hd + 1], (tm, LANES))

    return pl.pallas_call(
        body, name=name, grid=(s // tm,),
        in_specs=[_row_spec(tm, d), pl.BlockSpec((LANES, d), lambda i: (0, 0)), _vec_spec(LANES)],
        out_specs=[_row_spec(tm, LANES), _row_spec(tm, N_HEADS * LANES)],
        out_shape=[jax.ShapeDtypeStruct((s, LANES), f32), jax.ShapeDtypeStruct((s, N_HEADS * LANES), f32)],
        scratch_shapes=[pltpu.VMEM((8, LANES), f32)], compiler_params=_params(1),
    )(h, w_f_t, bf_row)


def _forget_prep_bwd(rs, dcs, fl, bf_row, name):
    s = fl.shape[0]
    tm = _row_tile(s, 1024)
    n = s // tm

    def body(r_ref, c_ref, f_ref, b_ref, df_ref, db_ref, next_ref):
        @pl.when(pl.program_id(0) == 0)
        def _():
            next_ref[...] = jnp.zeros_like(next_ref)
            db_ref[...] = jnp.zeros_like(db_ref)
        eye = (lax.broadcasted_iota(jnp.int32, (N_HEADS, LANES), 0) == lax.broadcasted_iota(jnp.int32, (N_HEADS, LANES), 1)).astype(f32)
        dcum = lax.dot_general(r_ref[...], eye, _TN, precision=lax.Precision.HIGHEST, preferred_element_type=f32)
        for h in range(N_HEADS):
            dcum = dcum - jnp.where(_lane() == h, jnp.sum(c_ref[:, h * LANES:(h + 1) * LANES], axis=1, keepdims=True), 0.0)
        dlf = _cumsum_rows(dcum, reverse=True) + next_ref[0:1, :]
        next_ref[0:1, :] = dlf[0:1, :]
        z = f_ref[...] + b_ref[...]
        df = jnp.where(_lane() < N_HEADS, dlf * jax.nn.sigmoid(-z), 0.0)
        df_ref[...] = df.astype(bf16)
        db_ref[0:1, :] += jnp.sum(df, axis=0, keepdims=True)

    def rows(width):
        return pl.BlockSpec((tm, width), lambda i: (n - 1 - i, 0))

    return pl.pallas_call(
        body, name=name, grid=(n,),
        in_specs=[pl.BlockSpec((N_HEADS, tm), lambda i: (0, n - 1 - i)), rows(N_HEADS * LANES), rows(LANES), _vec_spec(LANES)],
        out_specs=[rows(LANES), _vec_spec(LANES, 8)],
        out_shape=[jax.ShapeDtypeStruct((s, LANES), bf16), jax.ShapeDtypeStruct((8, LANES), f32)],
        scratch_shapes=[pltpu.VMEM((8, LANES), f32)], compiler_params=_params(1),
    )(rs, dcs, fl, bf_row)


def _tile_mask(n_keys, n_queries, off, window):
    shape = (n_keys, n_queries)
    d = lax.broadcasted_iota(jnp.int32, shape, 1) - lax.broadcasted_iota(jnp.int32, shape, 0) + off
    valid = d >= 0
    return jnp.logical_and(valid, d < window) if window else valid


def _wide(v, t):
    return jnp.concatenate([v] * (t // LANES), axis=1)


def _attn_fwd(q, k, v, name, *, cum_b=None, sink_rows=None, window=None, t=256):
    s = q.shape[0]
    t = _row_tile(s, t)
    fox, has_sink = cum_b is not None, sink_rows is not None
    assert not window or (window % LANES == 0 and LANES + window <= s)

    def body(*refs):
        q_ref, k_ref, v_ref = refs[:3]
        rest = list(refs[3:])
        cb_ref = rest.pop(0) if fox else None
        sink_ref = rest.pop(0) if has_sink else None
        o_ref, lse_ref = rest
        i = pl.program_id(1)
        low = _lane() < HEAD_DIM
        top = lax.broadcasted_iota(jnp.int32, (LANES, 1), 0) < HEAD_DIM
        q2 = q_ref[...]
        zero = jnp.zeros_like(q2)
        qms = (jnp.where(low, q2, zero), jnp.where(low, zero, q2))

        def tile(k0, n_keys, off, carry, masked, queries=slice(0, t)):
            nq = queries.stop - queries.start
            kblk, vblk = k_ref[pl.ds(k0, n_keys), :], v_ref[pl.ds(k0, n_keys), :]
            valid = _tile_mask(n_keys, nq, off, window) if masked else None
            ones = jnp.ones_like(vblk)
            vs = tuple(jnp.where(_lane() == L_ROW[h], ones, vblk) for h in range(2))

            def scores(h):
                return lax.dot_general(kblk, qms[h][queries], _NT, preferred_element_type=f32)

            def softmax(h, sc):
                m = carry[h][0]
                if fox:
                    sc = sc - _wide(cb_ref[pl.ds(k0, n_keys), h * LANES:(h + 1) * LANES], nq)
                if masked:
                    sc = jnp.where(valid, sc, NEG)
                m_new = jnp.maximum(m, jnp.max(sc, axis=0, keepdims=True))
                return m_new, jnp.exp(m - m_new), jnp.exp(sc - m_new).astype(bf16)

            def update(h, m_new, alpha, p):
                return m_new, alpha * carry[h][1] + lax.dot_general(vs[h], p, _TN, preferred_element_type=f32)

            if window:
                return tuple(update(h, *softmax(h, scores(h))) for h in range(2))
            scs = [scores(h) for h in range(2)]
            stats = [softmax(h, scs[h]) for h in range(2)]
            return tuple(update(h, *stats[h]) for h in range(2))

        def start(nq):
            if has_sink:
                row = lax.broadcasted_iota(jnp.int32, (LANES, nq), 0)
                return tuple((_wide(sink_ref[h:h + 1, :], nq), (row == L_ROW[h]).astype(f32)) for h in range(2))
            return tuple((jnp.full((1, nq), NEG, f32), jnp.zeros((LANES, nq), f32)) for h in range(2))

        def finish(carry, queries):
            (m0, a0), (m1, a1) = carry
            l0, l1 = a0[L_ROW[0]:L_ROW[0] + 1, :], a1[L_ROW[1]:L_ROW[1] + 1, :]
            o_t = jnp.where(top, a0 * (1.0 / l0), a1 * (1.0 / l1))
            o_ref[queries, :] = o_t.T.astype(bf16)
            lse_ref[0:1, queries] = m0 + jnp.log(l0)
            lse_ref[1:2, queries] = m1 + jnp.log(l1)

        if window:
            for c in range(t // LANES):
                queries = slice(c * LANES, (c + 1) * LANES)
                q0 = i * t + c * LANES
                k0 = pl.multiple_of(jnp.maximum(q0 - window, 0), LANES)
                finish(tile(k0, LANES + window, q0 - k0, start(LANES), True, queries), queries)
        else:
            carry = lax.fori_loop(0, i, lambda kb, c: tile(pl.multiple_of(kb * t, t), t, 0, c, False), start(t))
            half, k_own = t // 2, pl.multiple_of(i * t, t)
            carry = tile(k_own, half, 0, carry, True)
            finish(tuple((m[:, :half], a[:, :half]) for m, a in carry), slice(0, half))
            carry = tuple((m[:, half:], a[:, half:]) for m, a in carry)
            finish(tile(pl.multiple_of(k_own + half, half), half, 0, carry, True, slice(half, t)), slice(half, t))

    q_spec = pl.BlockSpec((t, LANES), lambda j, i: (i, j))
    kv_spec = pl.BlockSpec((s, LANES), lambda j, i: (0, j))
    in_specs, args = [q_spec, kv_spec, kv_spec], [q, k, v]
    if fox:
        in_specs += [pl.BlockSpec((s, 2 * LANES), lambda j, i: (0, j))]
        args += [cum_b]
    if has_sink:
        in_specs += [pl.BlockSpec((None, 2, LANES), lambda j, i: (j, 0, 0))]
        args += [sink_rows.reshape(N_PAIRS, 2, LANES)]
    return pl.pallas_call(
        body, name=name, grid=(N_PAIRS, s // t), in_specs=in_specs,
        out_specs=[q_spec, pl.BlockSpec((None, 2, t), lambda j, i: (j, 0, i))],
        out_shape=[jax.ShapeDtypeStruct((s, N_PAIRS * LANES), bf16), jax.ShapeDtypeStruct((N_PAIRS, 2, s), f32)],
        compiler_params=_params(2),
    )(*args)


def _branch_dgrad_delta(db, w, o, name, *, lse=None, sink_rows=None, after=None):
    s, hw = o.shape
    tm = _row_tile(s, 512)
    has_sink = sink_rows is not None
    extra = [] if after is None else [after]

    def body(*refs):
        db_ref, w_ref, o_ref = refs[:3]
        outs = refs[3 + (2 if has_sink else 0) + len(extra):]
        do_ref, dl_ref = outs[:2]
        if has_sink:
            lse_ref, sink_ref = refs[3:5]
            ds_ref = outs[2]

            @pl.when(pl.program_id(0) == 0)
            def _():
                ds_ref[...] = jnp.zeros_like(ds_ref)
        do = lax.dot_general(db_ref[...], w_ref[...], _NT, preferred_element_type=f32).astype(bf16)
        do_ref[...] = do
        for j in range(N_PAIRS):
            cols = slice(j * LANES, (j + 1) * LANES)
            prod_t = (do[:, cols].astype(f32) * o_ref[:, cols].astype(f32)).T
            for h in range(2):
                dl = jnp.sum(prod_t[h * HEAD_DIM:(h + 1) * HEAD_DIM, :], axis=0, keepdims=True)
                dl_ref[j, h:h + 1, :] = dl
                if has_sink:
                    r = 2 * j + h
                    p_sink = jnp.exp(sink_ref[r:r + 1, 0:1] - lse_ref[j, h:h + 1, :])
                    ds_ref[r:r + 1, :] += -jnp.sum(p_sink * dl, axis=1, keepdims=True)

    rows_spec = pl.BlockSpec((N_PAIRS, 2, tm), lambda i: (0, 0, i))
    in_specs = [_row_spec(tm, db.shape[1]), pl.BlockSpec(w.shape, lambda i: (0, 0)), _row_spec(tm, hw)]
    args = [db, w, o]
    out_specs = [_row_spec(tm, hw), rows_spec]
    out_shape = [jax.ShapeDtypeStruct((s, hw), bf16), jax.ShapeDtypeStruct((N_PAIRS, 2, s), f32)]
    if has_sink:
        in_specs += [rows_spec, _vec_spec(LANES, N_HEADS)]
        args += [lse, sink_rows]
        out_specs += [_vec_spec(LANES, N_HEADS)]
        out_shape += [jax.ShapeDtypeStruct((N_HEADS, LANES), f32)]
    return pl.pallas_call(
        body, name=name, grid=(s // tm,), in_specs=in_specs + [pl.BlockSpec(memory_space=pl.ANY)] * len(extra),
        out_specs=out_specs, out_shape=out_shape, compiler_params=_params(1),
    )(*args, *extra)


def _attn_bwd(q, k, v, do, lse, delta, name, *, cum_b=None, window=None, t=256):
    s = q.shape[0]
    t = _row_tile(s, t)
    nblk = s // t
    fox = cum_b is not None
    assert not window or (window % LANES == 0 and LANES + window <= s)

    def body(*refs):
        k_ref, v_ref, q_ref, do_ref, lse_ref, dl_ref = refs[:6]
        rest = list(refs[6:])
        cb_ref = rest.pop(0) if fox else None
        dq_ref, dk_ref, dv_ref = rest[:3]
        dcs_ref, rs_ref = (rest[3], rest[4]) if fox else (None, None)
        dk_acc, dv_acc = rest[-2:]
        b = pl.program_id(1)
        k0 = pl.multiple_of(b * t, t)

        @pl.when(b == 0)
        def _():
            dq_ref[...] = jnp.zeros_like(dq_ref)
            if fox:
                rs_ref[...] = jnp.zeros_like(rs_ref)

        dk_acc[...] = jnp.zeros_like(dk_acc)
        dv_acc[...] = jnp.zeros_like(dv_acc)
        if fox:
            dcs_ref[...] = jnp.zeros_like(dcs_ref)
        low = _lane() < HEAD_DIM
        top = lax.broadcasted_iota(jnp.int32, (LANES, 1), 0) < HEAD_DIM
        kblk, vblk = k_ref[...], v_ref[...]
        k_t = kblk.astype(f32).T.astype(bf16)
        cks = [_wide(cb_ref[pl.ds(k0, t), h * LANES:(h + 1) * LANES], t) for h in range(2)] if fox else None

        def tile(q0, n_queries, off, masked, keys=slice(0, t)):
            cols = pl.ds(q0, n_queries)
            q2, do2 = q_ref[cols, :], do_ref[cols, :]
            zero = jnp.zeros_like(q2)
            valid = _tile_mask(keys.stop - keys.start, n_queries, off, window) if masked else None
            dq_parts = []
            for h in range(2):
                qm = jnp.where(low, q2, zero) if h == 0 else jnp.where(low, zero, q2)
                dom = jnp.where(low, do2, zero) if h == 0 else jnp.where(low, zero, do2)
                sc = lax.dot_general(kblk[keys], qm, _NT, preferred_element_type=f32)
                if fox:
                    sc = sc - cks[h][keys, :n_queries]
                if masked:
                    sc = jnp.where(valid, sc, NEG)
                p = jnp.exp(sc - lse_ref[h:h + 1, cols])
                dp = lax.dot_general(vblk[keys], dom, _NT, preferred_element_type=f32)
                ds = p * (dp - dl_ref[h:h + 1, cols])
                pb, dsb = p.astype(bf16), ds.astype(bf16)
                dv_acc[keys, :] += jnp.dot(pb, dom, preferred_element_type=f32)
                dk_acc[keys, :] += jnp.dot(dsb, qm, preferred_element_type=f32)
                dq_parts.append(jnp.dot(k_t[:, keys], dsb, preferred_element_type=f32))
                if fox:
                    dcs_ref[keys, h * LANES:(h + 1) * LANES] += sum(ds[:, g * LANES:(g + 1) * LANES]
                                                                    for g in range(n_queries // LANES))
                    rs_ref[h:h + 1, cols] += jnp.sum(ds, axis=0, keepdims=True)
            dq_ref[:, cols] += jnp.where(top, dq_parts[0], dq_parts[1])

        def later_block(qb, carry):
            tile(pl.multiple_of(qb * t, t), t, 0, False)
            return carry

        if window:
            for c in range(t // LANES):
                first = b * t + c * LANES
                q0 = pl.multiple_of(jnp.minimum(first, s - (LANES + window)), LANES)
                tile(q0, LANES + window, q0 - first, True, slice(c * LANES, (c + 1) * LANES))
        else:
            half = t // 2
            tile(k0, half, 0, True, slice(0, half))
            tile(pl.multiple_of(k0 + half, half), half, half, True)
            lax.fori_loop(b + 1, nblk, later_block, 0)
        dk_ref[...] = dk_acc[...].astype(bf16)
        dv_ref[...] = dv_acc[...].astype(bf16)

    kv_spec = pl.BlockSpec((t, LANES), lambda j, b: (b, j))
    seq_spec = pl.BlockSpec((s, LANES), lambda j, b: (0, j))
    rows_spec = pl.BlockSpec((None, 2, s), lambda j, b: (j, 0, 0))
    hw = N_PAIRS * LANES
    in_specs, args = [kv_spec, kv_spec, seq_spec, seq_spec, rows_spec, rows_spec], [k, v, q, do, lse, delta]
    out_specs = [pl.BlockSpec((LANES, s), lambda j, b: (j, 0)), kv_spec, kv_spec]
    out_shape = [jax.ShapeDtypeStruct((hw, s), f32), jax.ShapeDtypeStruct((s, hw), bf16), jax.ShapeDtypeStruct((s, hw), bf16)]
    if fox:
        in_specs += [pl.BlockSpec((s, 2 * LANES), lambda j, b: (0, j))]
        args += [cum_b]
        out_specs += [pl.BlockSpec((t, 2 * LANES), lambda j, b: (b, j)), rows_spec]
        out_shape += [jax.ShapeDtypeStruct((s, N_HEADS * LANES), f32), jax.ShapeDtypeStruct((N_PAIRS, 2, s), f32)]
    return pl.pallas_call(
        body, name=name, grid=(N_PAIRS, nblk), in_specs=in_specs, out_specs=out_specs, out_shape=out_shape,
        scratch_shapes=[pltpu.VMEM((t, LANES), f32)] * 2, compiler_params=_params(2),
    )(*args)


def _branch_merge(o_a, o_b, w_a, w_b, gl, name):
    s, k = o_a.shape
    d = w_a.shape[1]
    tm = _row_tile(s, 1024)

    def body(oa_ref, ob_ref, wa_ref, wb_ref, g_ref, ba_ref, bb_ref, m_ref):
        ba = jnp.dot(oa_ref[...], wa_ref[...], preferred_element_type=f32)
        bb = jnp.dot(ob_ref[...], wb_ref[...], preferred_element_type=f32)
        g0, g1 = _sigmoid(g_ref[:, :d].astype(f32)), _sigmoid(g_ref[:, d:].astype(f32))
        ba_ref[...] = ba.astype(bf16)
        bb_ref[...] = bb.astype(bf16)
        m_ref[...] = (g0 * ba + g1 * bb).astype(bf16)

    whole = pl.BlockSpec((k, d), lambda i: (0, 0))
    return pl.pallas_call(
        body, name=name, grid=(s // tm,),
        in_specs=[_row_spec(tm, k), _row_spec(tm, k), whole, whole, _row_spec(tm, 2 * d)],
        out_specs=[_row_spec(tm, d)] * 3, out_shape=[jax.ShapeDtypeStruct((s, d), bf16)] * 3, compiler_params=_params(1),
    )(o_a, o_b, w_a, w_b, gl)


def _out_dgrad_merge_bwd(dy, w_out, ba, bb, gl, name):
    s, d = ba.shape
    tm = _row_tile(s, 512)

    def body(dy_ref, w_ref, a_ref, b_ref, g_ref, da_ref, db_ref, dg_ref):
        dmv = lax.dot_general(dy_ref[...], w_ref[...], _NT, preferred_element_type=f32)
        g0, g1 = _sigmoid(g_ref[:, :d].astype(f32)), _sigmoid(g_ref[:, d:].astype(f32))
        da_ref[...] = (dmv * g0).astype(bf16)
        db_ref[...] = (dmv * g1).astype(bf16)
        dg_ref[:, :d] = (dmv * a_ref[...].astype(f32) * (g0 * (1.0 - g0))).astype(bf16)
        dg_ref[:, d:] = (dmv * b_ref[...].astype(f32) * (g1 * (1.0 - g1))).astype(bf16)

    return pl.pallas_call(
        body, name=name, grid=(s // tm,),
        in_specs=[_row_spec(tm, dy.shape[1]), pl.BlockSpec(w_out.shape, lambda i: (0, 0))] + [_row_spec(tm, d)] * 2
        + [_row_spec(tm, 2 * d)],
        out_specs=[_row_spec(tm, d)] * 2 + [_row_spec(tm, 2 * d)],
        out_shape=[jax.ShapeDtypeStruct((s, d), bf16)] * 2 + [jax.ShapeDtypeStruct((s, 2 * d), bf16)],
        compiler_params=_params(1),
    )(dy, w_out, ba, bb, gl)


GLU_TILE = 256


def _ffn_in_swiglu(h, w_t, name):
    s, d = h.shape
    f = w_t.shape[0] // 2
    tm = _row_tile(s, 2048)
    tg = GLU_TILE
    nb = f // tg

    def body(h_ref, wg_ref, wu_ref, g_ref, u_ref, act_ref):
        hv = h_ref[...]
        g = lax.dot_general(hv, wg_ref[...], _NT, preferred_element_type=f32)
        u = lax.dot_general(hv, wu_ref[...], _NT, preferred_element_type=f32)
        g_ref[...] = g.astype(bf16)
        u_ref[...] = u.astype(bf16)
        act_ref[...] = (g * _sigmoid(g) * u).astype(bf16)

    col = pl.BlockSpec((tm, tg), lambda i, j: (i, j))
    return pl.pallas_call(
        body, name=name, grid=(s // tm, nb),
        in_specs=[pl.BlockSpec((tm, d), lambda i, j: (i, 0)), pl.BlockSpec((tg, d), lambda i, j: (j, 0)),
                  pl.BlockSpec((tg, d), lambda i, j: (j + nb, 0))],
        out_specs=[col] * 3, out_shape=[jax.ShapeDtypeStruct((s, f), bf16)] * 3, compiler_params=_params(2),
    )(h, w_t, w_t)


def _ffn_out_dgrad_swiglu(dy, w_out, g, u, name):
    s, d = dy.shape
    f = g.shape[1]
    tm = _row_tile(s, 2048)
    tg = GLU_TILE

    def body(dy_ref, w_ref, g_ref, u_ref, dg_ref, du_ref):
        dv = lax.dot_general(dy_ref[...], w_ref[...], _NT, preferred_element_type=f32)
        gv, uv = g_ref[...].astype(f32), u_ref[...].astype(f32)
        sg = _sigmoid(gv)
        dg_ref[...] = (dv * uv * (sg * (1.0 + gv * (1.0 - sg)))).astype(bf16)
        du_ref[...] = (dv * (gv * sg)).astype(bf16)

    col = pl.BlockSpec((tm, tg), lambda i, j: (i, j))
    return pl.pallas_call(
        body, name=name, grid=(s // tm, f // tg),
        in_specs=[pl.BlockSpec((tm, d), lambda i, j: (i, 0)), pl.BlockSpec((tg, d), lambda i, j: (j, 0)), col, col],
        out_specs=[col] * 2, out_shape=[jax.ShapeDtypeStruct((s, f), bf16)] * 2, compiler_params=_params(2),
    )(dy, w_out, g, u)


def _wgrad_stack(parts, h, name):
    s, m = parts[0].shape
    d = h.shape[1]
    tm = 256
    nb = m // tm
    n = len(parts)

    def body(*refs):
        i = pl.program_id(0)
        for p in range(n):
            @pl.when(i // nb == p)
            def _(p=p):
                refs[n + 1][...] = lax.dot_general(refs[p][...], refs[n][...], _TN, preferred_element_type=f32).astype(bf16)

    a_specs = [pl.BlockSpec((s, tm), lambda i, p=p: (0, jnp.clip(i - p * nb, 0, nb - 1))) for p in range(n)]
    return pl.pallas_call(
        body, name=name, grid=(n * nb,), in_specs=a_specs + [pl.BlockSpec((s, d), lambda i: (0, 0))],
        out_specs=pl.BlockSpec((tm, d), lambda i: (i, 0)),
        out_shape=jax.ShapeDtypeStruct((n * m, d), bf16), compiler_params=_params(1),
    )(*parts, h)


def _ada_wgrad(c_all, d_all, name):
    n, d = c_all.shape
    w = d_all.shape[1]

    def body(c_ref, d_ref, o_ref):
        eye = (lax.broadcasted_iota(jnp.int32, (n, n), 0) == lax.broadcasted_iota(jnp.int32, (n, n), 1)).astype(f32)
        ct = lax.dot_general(c_ref[...], eye, _TN, precision=lax.Precision.HIGHEST, preferred_element_type=f32)
        g = ct[:, 0:1] * d_ref[0:1, :]
        for bi in range(1, n):
            g = g + ct[:, bi:bi + 1] * d_ref[bi:bi + 1, :]
        o_ref[0] = g

    return pl.pallas_call(
        body, name=name, out_shape=jax.ShapeDtypeStruct((1, d, w), f32), compiler_params=_params(),
    )(c_all, d_all)


def _adamw(parts, w, m, v, name, mine=None, me=None):
    r, c = w.shape
    n_parts = parts.shape[0]
    row_tiles = [t for t in range(min(r, 256), 0, -1) if r % t == 0 and (t % 16 == 0 or t == r)]
    if row_tiles:
        tr, tc = row_tiles[0], c
    else:
        tr, tc = r, next(t for t in (256, LANES) if c % t == 0)

    def body(*refs):
        w_ref, m_ref, v_ref, g_ref, d_ref, nm_ref, nv_ref = refs[-7:]
        if mine is None:
            p_ref, = refs[:-7]
        else:
            me_ref, p_ref, own_ref = refs[:-7]

        def part(i):
            if mine is None:
                return p_ref[i].astype(f32)
            return jnp.where(me_ref[0] == i, own_ref[...], p_ref[i]).astype(f32)

        g = part(0)
        for i in range(1, n_parts):
            g = g + part(i)
        mm = ADAM_B1 * m_ref[...] + (1.0 - ADAM_B1) * g
        vv = ADAM_B2 * v_ref[...] + (1.0 - ADAM_B2) * (g * g)
        m_hat = mm / (1.0 - ADAM_B1 ** ADAM_STEP)
        v_hat = vv / (1.0 - ADAM_B2 ** ADAM_STEP)
        g_ref[...] = g
        d_ref[...] = -ADAM_LR * (m_hat / (jnp.sqrt(v_hat) + ADAM_EPS) + ADAM_WD * w_ref[...])
        nm_ref[...] = mm
        nv_ref[...] = vv

    out_shape = [jax.ShapeDtypeStruct((r, c), f32)] * 4
    if mine is None:
        spec = pl.BlockSpec((tr, tc), lambda i, j: (i, j))
        return pl.pallas_call(
            body, name=name, grid=(r // tr, c // tc),
            in_specs=[pl.BlockSpec((n_parts, tr, tc), lambda i, j: (0, i, j))] + [spec] * 3,
            out_specs=[spec] * 4, out_shape=out_shape, compiler_params=_params(2),
        )(parts, w, m, v)
    spec = pl.BlockSpec((tr, tc), lambda i, j, me_ref: (i, j))
    return pl.pallas_call(
        body, name=name, out_shape=out_shape, compiler_params=_params(2),
        grid_spec=pltpu.PrefetchScalarGridSpec(
            num_scalar_prefetch=1, grid=(r // tr, c // tc),
            in_specs=[pl.BlockSpec((n_parts, tr, tc), lambda i, j, me_ref: (0, i, j)),
                      pl.BlockSpec((None, tr, tc), lambda i, j, me_ref: (me_ref[0], i, j))] + [spec] * 3,
            out_specs=[spec] * 4),
    )(me, parts, mine, w, m, v)


def _me():
    return lax.axis_index("x"), lax.axis_index("y"), lax.axis_index("c")


def _gather_prologue(c, w_ada, b_mine, w_in_t, name):
    n_dev, d = N_DEV, c.shape[1]
    ada_w = w_ada.shape[1]

    def body(c_ref, w_ref, b_ref, win_ref, call_ref, ada_ref, gin_ref, cols_ref, send_sems, recv_sems, local_sems):
        x, y, cc = _me()
        me, sibling = (x, y, cc), (x, y, 1 - cc)
        chips = [(1 - x, y), (x, 1 - y), (1 - x, 1 - y)]
        outs = (call_ref, ada_ref, gin_ref)

        def rows(a, dev):
            return outs[a].at[4 * dev[0] + 2 * dev[1] + dev[2]]

        def copy(a, k, block, to, src=None):
            return pltpu.make_async_remote_copy(
                src_ref=rows(a, block) if src is None else src, dst_ref=rows(a, block),
                send_sem=send_sems.at[a, k], recv_sem=recv_sems.at[a, k], device_id=to, device_id_type=MESH)

        def begin(a, src):
            own = pltpu.make_async_copy(src, rows(a, me), local_sems.at[a])
            sends = [copy(a, 0, me, sibling, src=src)] + [copy(a, 1 + j, me, (*chip, cc), src=src) for j, chip in enumerate(chips)]
            for cp in [own] + sends:
                cp.start()
            return own, sends

        def finish(a, own, sends):
            passed = []
            for j, chip in enumerate(chips):
                copy(a, 1 + j, (*chip, cc), me).wait_recv()
                passed.append(copy(a, 4 + j, (*chip, cc), sibling))
                passed[-1].start()
            copy(a, 0, sibling, me).wait_recv()
            for j, chip in enumerate(chips):
                copy(a, 4 + j, (*chip, 1 - cc), me).wait_recv()
            for cp in sends + passed:
                cp.wait_send()
            own.wait()

        finish(0, *begin(0, c_ref))
        cols_ref[...] = (jnp.dot(call_ref[:, 0, :].astype(bf16), w_ref[...].astype(bf16), preferred_element_type=f32)
                         + b_ref[...])
        finish(1, *begin(1, cols_ref))
        finish(2, *begin(2, win_ref))

    vmem, hbm = pl.BlockSpec(memory_space=pltpu.VMEM), pl.BlockSpec(memory_space=pl.ANY)
    return pl.pallas_call(
        body, name=name, in_specs=[vmem, vmem, vmem, hbm], out_specs=[vmem, vmem, hbm],
        out_shape=[jax.ShapeDtypeStruct((n_dev, 1, d), f32), jax.ShapeDtypeStruct((n_dev, n_dev, ada_w), f32),
                   jax.ShapeDtypeStruct((n_dev,) + w_in_t.shape, w_in_t.dtype)],
        scratch_shapes=[pltpu.VMEM((n_dev, ada_w), f32), pltpu.SemaphoreType.DMA((3, 7)), pltpu.SemaphoreType.DMA((3, 7)),
                        pltpu.SemaphoreType.DMA((3,))],
        compiler_params=pltpu.CompilerParams(vmem_limit_bytes=VMEM_LIMIT),
    )(c, w_ada, b_mine, w_in_t)


_FLIPS = ((0, 0, 1), (1, 0, 0), (0, 1, 0), (1, 1, 0), (1, 0, 1), (0, 1, 1), (1, 1, 1))
_HBM = pl.BlockSpec(memory_space=pltpu.HBM)
_SEM = pl.BlockSpec(memory_space=pltpu.SEMAPHORE)


def _exchange_copies(scatter, srcs, lands, send_sems, recv_sems):
    x, y, c = _me()
    me_row = 4 * x + 2 * y + c
    out = []
    for k, (fx, fy, fc) in enumerate(_FLIPS):
        peer = (x ^ fx, y ^ fy, c ^ fc)
        peer_row = 4 * peer[0] + 2 * peer[1] + peer[2]
        for a in range(len(srcs)):
            out.append(pltpu.make_async_remote_copy(
                src_ref=srcs[a].at[peer_row] if scatter else srcs[a], dst_ref=lands[a].at[me_row],
                send_sem=send_sems.at[7 * a + k], recv_sem=recv_sems.at[7 * a + k], device_id=peer, device_id_type=MESH))
    return out


def _own_copies(srcs, lands, own_sems):
    x, y, c = _me()
    return [pltpu.make_async_copy(srcs[a], lands[a].at[4 * x + 2 * y + c], own_sems.at[a]) for a in range(len(srcs))]


def _exchange_start(arrays, scatter, name, after=None):
    n = len(arrays)
    lands = [lax.empty(a.shape if scatter else (N_DEV,) + a.shape, a.dtype) for a in arrays]
    extra = [] if after is None else [after]

    def body(*refs):
        srcs, zones = refs[:n], refs[n:2 * n]
        send_sems, recv_sems, own_sems = refs[2 * n + len(extra):2 * n + len(extra) + 3]
        token = refs[-1]
        for cp in _exchange_copies(scatter, srcs, zones, send_sems, recv_sems):
            cp.start()
        for cp in [] if scatter else _own_copies(srcs, zones, own_sems):
            cp.start()
        token[...] = jnp.zeros_like(token)

    thru = [pltpu.HBM(a.shape, a.dtype) for a in list(arrays) + lands]
    outs = pl.pallas_call(
        body, name=name,
        out_shape=(pltpu.SemaphoreType.DMA((7 * n,)), pltpu.SemaphoreType.DMA((7 * n,)), pltpu.SemaphoreType.DMA((n,)), *thru,
                   jax.ShapeDtypeStruct((8, LANES), f32)),
        in_specs=[_HBM] * (2 * n) + [pl.BlockSpec(memory_space=pl.ANY)] * len(extra),
        out_specs=(_SEM, _SEM, _SEM, *[_HBM] * (2 * n), pl.BlockSpec(memory_space=pltpu.VMEM)),
        input_output_aliases={i: 3 + i for i in range(2 * n)},
        compiler_params=pltpu.CompilerParams(has_side_effects=pltpu.SideEffectType.DATAFLOW_SIDE_EFFECTING),
    )(*[pltpu.with_memory_space_constraint(a, pltpu.HBM) for a in list(arrays) + lands], *extra)
    return dict(n=n, scatter=scatter, sems=outs[:3], srcs=outs[3:3 + n], lands=outs[3 + n:3 + 2 * n], token=outs[-1])


def _exchange_wait(handle, after, name):
    n, scatter = handle["n"], handle["scatter"]

    def body(*refs):
        srcs, zones = refs[:n], refs[n:2 * n]
        send_sems, recv_sems, own_sems = refs[2 * n:2 * n + 3]
        for cp in _exchange_copies(scatter, srcs, zones, send_sems, recv_sems):
            cp.wait_send()
            cp.wait_recv()
        for cp in [] if scatter else _own_copies(srcs, zones, own_sems):
            cp.wait()

    thru = [pltpu.HBM(a.shape, a.dtype) for a in list(handle["srcs"]) + list(handle["lands"])]
    outs = pl.pallas_call(
        body, name=name, out_shape=tuple(thru),
        in_specs=[_HBM] * (2 * n) + [_SEM, _SEM, _SEM, pl.BlockSpec(memory_space=pl.ANY)], out_specs=tuple([_HBM] * (2 * n)),
        input_output_aliases={i: i for i in range(2 * n)},
        compiler_params=pltpu.CompilerParams(has_side_effects=pltpu.SideEffectType.DATAFLOW_SIDE_EFFECTING),
    )(*handle["srcs"], *handle["lands"], *handle["sems"], after)
    return list(outs[:n]), list(outs[n:])


def _cols_from_shards(g):
    return jnp.transpose(g, (1, 0, 2)).reshape(g.shape[1], -1)


def _shards_from_cols(a):
    return jnp.transpose(a.reshape(a.shape[0], N_DEV, -1), (1, 0, 2))


def _local_step(x, positions, ada, g_pre_mix, g_post_mix, b_f, sinks, g_pre_ffn, g_post_ffn, target,
                w_in_t, mix_weights, ffn_weights, on_grads):
    s, d = x.shape
    row = lambda v: v.reshape(1, -1)
    shift_m, scale_m, gate_m, shift_f, scale_f, gate_f = (ada[i:i + 1] for i in range(6))
    w_gate_t, w_qkv_t = w_in_t[F_OFF + N_HEADS:], w_in_t
    w_f_t = jnp.pad(w_in_t[F_OFF:F_OFF + N_HEADS], ((0, LANES - N_HEADS), (0, 0)))
    bf_row = jnp.pad(row(b_f), ((0, 0), (0, LANES - N_HEADS)))
    sink_rows = jnp.broadcast_to(sinks.reshape(N_HEADS, 1).astype(f32), (N_HEADS, LANES))
    inv_freq = 1.0 / (ROPE_THETA ** (jnp.arange(0, HEAD_DIM, 2, dtype=f32) / HEAD_DIM))
    cos, sin_s = _rope_tables(positions.reshape(s, 1), jnp.tile(inv_freq, 4).reshape(1, LANES), "rope_tables")

    h1, qa, ka, va, qb, kb, vb = _prenorm_proj_qkv(x, row(g_pre_mix), scale_m, shift_m, w_qkv_t, cos, sin_s, "prenorm_proj_qkv")
    gl = _matmul(h1, w_gate_t, "nt", bf16, "proj_gate")
    fl, cum_b = _forget_prep(h1, w_f_t, bf_row, "proj_forget_prep")
    o_a, lse_a = _attn_fwd(qa, ka, va, "swa_fwd", sink_rows=sink_rows, window=WINDOW, t=2048)
    o_b, lse_b = _attn_fwd(qb, kb, vb, "fox_fwd", cum_b=cum_b, t=1024)
    everything_before = (gl[:8, :LANES] + o_a[:8, :LANES] + o_b[:8, :LANES]).astype(f32)
    w_branch_a, w_branch_b, w_out = mix_weights(everything_before)
    ba, bb, merged = _branch_merge(o_a, o_b, w_branch_a, w_branch_b, gl, "branch_merge")
    y1, x2, h2 = _out_proj_postnorm_prenorm(merged, w_out, x, row(g_post_mix), gate_m, row(g_pre_ffn), scale_f, shift_f,
                                            "out_proj_norms")

    w_ffn_in_t, w_ffn_out = ffn_weights(h2)
    g_ff, u_ff, act = _ffn_in_swiglu(h2, w_ffn_in_t, "ffn_in_swiglu")
    loss_row, d_out, d_y2, vec_pf = _out_proj_loss_tail(act, w_ffn_out, x2, row(g_post_ffn), gate_f, target, "ffn_out_loss_tail")

    g_w_ffn_out = _matmul(act, d_y2, "tn", bf16, "ffn_out_wgrad")
    dg_ff, du_ff = _ffn_out_dgrad_swiglu(d_y2, w_ffn_out, g_ff, u_ff, "ffn_out_dgrad_swiglu")
    g_w_ffn_in_t = _wgrad_stack([dg_ff, du_ff], h2, "ffn_in_wgrad")
    sent = on_grads(dict(w_ffn_in=g_w_ffn_in_t, w_ffn_out=g_w_ffn_out))
    d_x2, vec_nf, d_y1, vec_pm = _dgrad_prenorm_bwd(
        [(dg_ff, w_ffn_in_t, 0), (du_ff, w_ffn_in_t, 1)], x2, row(g_pre_ffn), scale_f, d_out, "ffn_in_dgrad_norms_bwd",
        after=sent, below=(y1, row(g_post_mix), gate_m))

    g_w_out = _matmul(merged, d_y1, "tn", bf16, "out_proj_wgrad")
    d_ba, d_bb, dgl = _out_dgrad_merge_bwd(d_y1, w_out, ba, bb, gl, "out_proj_dgrad_merge_bwd")
    g_w_branch_a = _matmul(o_a, d_ba, "tn", bf16, "branch_a_wgrad")
    g_w_branch_b = _matmul(o_b, d_bb, "tn", bf16, "branch_b_wgrad")
    sent = on_grads(dict(w_out=g_w_out, w_branch_a=g_w_branch_a, w_branch_b=g_w_branch_b))
    d_oa, delta_a, d_sink = _branch_dgrad_delta(d_ba, w_branch_a, o_a, "branch_a_dgrad_delta", lse=lse_a,
                                                sink_rows=sink_rows, after=sent)
    d_ob, delta_b = _branch_dgrad_delta(d_bb, w_branch_b, o_b, "branch_b_dgrad_delta", after=sent)
    dqa_t, dka, dva = _attn_bwd(qa, ka, va, d_oa, lse_a, delta_a, "swa_bwd", window=WINDOW, t=2048)
    dqb_t, dkb, dvb, dcs, rs = _attn_bwd(qb, kb, vb, d_ob, lse_b, delta_b, "fox_bwd", cum_b=cum_b, t=512)
    dqkv = _qkv_prep_bwd(dqa_t, dka, dva, dqb_t, dkb, dvb, cos, sin_s, "qkv_prep_bwd")
    dfl, vec_bf = _forget_prep_bwd(rs.reshape(N_HEADS, s), dcs, fl, bf_row, "forget_prep_bwd")
    g_w_in_t = jnp.concatenate([_matmul(dqkv, h1, "tn", bf16, "qkv_wgrad"), _matmul(dfl, h1, "tn", bf16, "forget_wgrad")[:N_HEADS],
                                _matmul(dgl, h1, "tn", bf16, "gate_wgrad")], axis=0)
    sent = on_grads(dict(w_in=g_w_in_t))
    grad_x, vec_nm = _dgrad_prenorm_bwd([(dgl, w_gate_t, 0), (dqkv, w_qkv_t, 0), (dfl, w_f_t, 0)], x, row(g_pre_mix),
                                        scale_m, d_x2, "in_proj_dgrad_prenorm_bwd", after=sent)

    d_ada = jnp.concatenate([vec_nm[0], vec_nm[1], vec_pm[0], vec_nf[0], vec_nf[1], vec_pf[0]])
    small = dict(b_ada=d_ada, g_pre_mix=vec_nm[2], g_post_mix=vec_pm[1], g_pre_ffn=vec_nf[2], g_post_ffn=vec_pf[1],
                 b_f=vec_bf[0, :N_HEADS], sinks=d_sink[:, 0], loss=loss_row[0, :1])
    return grad_x, small


_SMALL = (("b_ada", 6144), ("g_pre_mix", 1024), ("g_post_mix", 1024), ("g_pre_ffn", 1024), ("g_post_ffn", 1024),
          ("b_f", 128), ("sinks", 128), ("loss", 128))
_SMALL_ROWS = 88


def _pack_small(vals):
    parts = [jnp.pad(vals[k].reshape(-1).astype(f32), (0, n - vals[k].size)) for k, n in _SMALL]
    flat = jnp.concatenate(parts)
    return jnp.pad(flat, (0, _SMALL_ROWS * LANES - flat.size)).reshape(_SMALL_ROWS, LANES)


def _unpack_small(slab, shapes):
    flat, out, off = slab.reshape(-1), {}, 0
    for k, n in _SMALL:
        size = math.prod(shapes[k])
        out[k] = flat[off:off + size].reshape(shapes[k])
        off += n
    return out


def kernel(x, c, positions, w_ada, b_ada, g_pre_mix, g_post_mix, w_in, b_f, sinks, w_branch_a, w_branch_b, w_out, g_pre_ffn, g_post_ffn, w_ffn_in, w_ffn_out, loss_target, m_w_ada, m_b_ada, m_g_pre_mix, m_g_post_mix, m_w_in, m_b_f, m_sinks, m_w_branch_a, m_w_branch_b, m_w_out, m_g_pre_ffn, m_g_post_ffn, m_w_ffn_in, m_w_ffn_out, v_w_ada, v_b_ada, v_g_pre_mix, v_g_post_mix, v_w_in, v_b_f, v_sinks, v_w_branch_a, v_w_branch_b, v_w_out, v_g_pre_ffn, v_g_post_ffn, v_w_ffn_in, v_w_ffn_out):
    xi, yi, ci = _me()
    me = 4 * xi + 2 * yi + ci
    d = D_MODEL
    ada_w = w_ada.shape[2]

    transposed = ("w_in", "w_ffn_in")
    tr = lambda a: jnp.transpose(a[0])

    b_mine = lax.dynamic_slice(b_ada, (0, me * ada_w), (1, ada_w))
    c_all, ada_all, g_in = _gather_prologue(c, w_ada[0], b_mine, tr(w_in).astype(bf16), "gather_prologue")
    c_all = c_all.reshape(N_DEV, d)
    ada = lax.dynamic_index_in_dim(ada_all, me, axis=1, keepdims=False).reshape(6, d)
    late_mix = [w.astype(bf16) for w in (w_branch_a[0], w_branch_b[0], w_out[0])]
    late_ffn = [w.astype(bf16) for w in (tr(w_ffn_in), w_ffn_out[0])]
    mix_h = _exchange_start(late_mix, False, "gather_mix_start", after=g_in)
    ffn_h = _exchange_start(late_ffn, False, "gather_ffn_start", after=mix_h["token"])

    def rows_from_shards(g):
        return g.reshape(g.shape[0] * g.shape[1], g.shape[2])

    def mix_weights(after):
        _, (g_ba, g_bb, g_out) = _exchange_wait(mix_h, after, "gather_mix_wait")
        return _cols_from_shards(g_ba), _cols_from_shards(g_bb), rows_from_shards(g_out)

    def ffn_weights(after):
        _, (g_fi, g_fo) = _exchange_wait(ffn_h, after, "gather_ffn_wait")
        return rows_from_shards(g_fi), rows_from_shards(g_fo)

    row_sharded = ("w_out", "w_ffn_out") + transposed
    in_flight = []

    def on_grads(group):
        sends = [g.reshape(N_DEV, g.shape[0] // N_DEV, g.shape[1]) if nm in row_sharded else _shards_from_cols(g)
                 for nm, g in group.items()]
        handle = _exchange_start(sends, True, "scatter_start_%d" % len(in_flight))
        in_flight.append((list(group), handle))
        return handle["token"]

    grad_x, small = _local_step(
        x[0], positions[0], ada + ffn_h["token"][0, 0], g_pre_mix[0], g_post_mix[0], b_f[0], sinks[0], g_pre_ffn[0],
        g_post_ffn[0], loss_target[0], rows_from_shards(g_in), mix_weights, ffn_weights, on_grads)

    ws = dict(w_in=(w_in, m_w_in, v_w_in), w_branch_a=(w_branch_a, m_w_branch_a, v_w_branch_a),
              w_branch_b=(w_branch_b, m_w_branch_b, v_w_branch_b), w_out=(w_out, m_w_out, v_w_out),
              w_ffn_in=(w_ffn_in, m_w_ffn_in, v_w_ffn_in), w_ffn_out=(w_ffn_out, m_w_ffn_out, v_w_ffn_out))
    res = {}

    def finish_group(gi, after):
        names, handle = in_flight[gi]
        sends, zones = _exchange_wait(handle, after, "scatter_wait_%d" % gi)
        for nm, zone, sent in zip(names, zones, sends):
            w, m, v = (tr(a) if nm in transposed else a[0] for a in ws[nm])
            out = _adamw(zone, w, m, v, "adamw_" + nm, mine=sent, me=me.reshape(1).astype(jnp.int32))
            after = out[0]
            res[nm] = [jnp.transpose(o) for o in out] if nm in transposed else out
        return after

    small_h = _exchange_start([_pack_small(small)], False, "gather_small_start", after=grad_x)
    done = finish_group(1, finish_group(0, small_h["token"]))
    _, (slab_all,) = _exchange_wait(small_h, done, "gather_small_wait")
    small_w = dict(b_ada=b_ada, g_pre_mix=g_pre_mix, g_post_mix=g_post_mix, g_pre_ffn=g_pre_ffn, g_post_ffn=g_post_ffn,
                   b_f=b_f, sinks=sinks, loss=jnp.zeros((1,), f32))
    small_m = dict(b_ada=m_b_ada, g_pre_mix=m_g_pre_mix, g_post_mix=m_g_post_mix, g_pre_ffn=m_g_pre_ffn,
                   g_post_ffn=m_g_post_ffn, b_f=m_b_f, sinks=m_sinks, loss=jnp.zeros((1,), f32))
    small_v = dict(b_ada=v_b_ada, g_pre_mix=v_g_pre_mix, g_post_mix=v_g_post_mix, g_pre_ffn=v_g_pre_ffn,
                   g_post_ffn=v_g_post_ffn, b_f=v_b_f, sinks=v_sinks, loss=jnp.ones((1,), f32))
    shapes = {k: small_w[k].shape for k, _ in _SMALL}
    s_out = _adamw(slab_all, _pack_small(small_w), _pack_small(small_m), _pack_small(small_v), "adamw_small")
    s_grad, s_delta, s_m, s_v = (_unpack_small(o, shapes) for o in s_out)

    d_ada_all = lax.dynamic_slice(slab_all[:, :6144 // LANES, :].reshape(N_DEV, 6144), (0, me * ada_w), (N_DEV, ada_w))
    ada_parts = _ada_wgrad(c_all, d_ada_all, "ada_wgrad")

    res["w_ada"] = _adamw(ada_parts, w_ada[0], m_w_ada[0], v_w_ada[0], "adamw_w_ada")
    finish_group(2, res["w_ada"][0])

    order = ["w_ada", "b_ada", "g_pre_mix", "g_post_mix", "w_in", "b_f", "sinks", "w_branch_a", "w_branch_b", "w_out",
             "g_pre_ffn", "g_post_ffn", "w_ffn_in", "w_ffn_out"]
    outs = [s_grad["loss"].reshape(()), grad_x[None]]
    for which, small_o in enumerate((s_grad, s_delta, s_m, s_v)):
        for nm in order:
            outs.append(res[nm][which][None] if nm in res else small_o[nm])
    return tuple(outs)
```

```python
import math

import jax
import jax.numpy as jnp
from jax import lax
from jax.experimental import pallas as pl
from jax.experimental.pallas import tpu as pltpu

f32 = jnp.float32
bf16 = jnp.bfloat16

D_MODEL = 1024
HEAD_DIM = 64
N_HEADS = 8
N_PAIRS = 4
QKV_W = 2304
F_OFF = 2304
WINDOW = 128
ROPE_THETA = 10000.0
RMS_EPS = 1e-6
N_DEV = 8
ADAM_LR, ADAM_B1, ADAM_B2, ADAM_EPS, ADAM_WD, ADAM_STEP = 0.001, 0.9, 0.999, 1e-08, 0.01, 10
NEG = -1e30
L_ROW = (HEAD_DIM, 0)
LANES = 128
VMEM_LIMIT = 48 * 1024 * 1024
MESH = pl.DeviceIdType.MESH

_NT = (((1,), (1,)), ((), ()))
_TN = (((0,), (0,)), ((), ()))


def _params(n_grid=0):
    sem = ("arbitrary",) * n_grid if n_grid else None
    return pltpu.CompilerParams(dimension_semantics=sem, vmem_limit_bytes=VMEM_LIMIT)


def _row_tile(s, want):
    t = min(s, want)
    assert s % t == 0, (s, t)
    return t


MATMUL_VMEM_BUDGET = 40 * 1024 * 1024


def _matmul_tiles(m, n, k, a_item, b_item, o_item):
    def tiles(d):
        return [t for t in range(LANES, min(d, 2048) + 1, LANES) if d % t == 0] or [d]

    best = None
    for tm in tiles(m):
        for tn in tiles(n):
            vmem = 2 * (tm * k * a_item + tn * k * b_item + tm * tn * o_item) + tm * tn * 4
            if vmem > MATMUL_VMEM_BUDGET:
                continue
            traffic = m * k * a_item + n * k * b_item * (1 if tn == n else m // tm) + m * n * o_item
            steps = (m // tm) * (n // tn)
            key = (traffic, 0, steps) if steps >= 4 else (traffic, 1, -steps)
            if best is None or key < best[0]:
                best = (key, tm, tn)
    assert best is not None, (m, n, k)
    return best[1], best[2]


def _matmul(a, b, mode, out_dtype, name, after=None):
    if mode == "nn":
        (m, k), n = a.shape, b.shape[1]
    elif mode == "nt":
        (m, k), n = a.shape, b.shape[0]
    else:
        (k, m), n = a.shape, b.shape[1]
    tm, tn = _matmul_tiles(m, n, k, a.dtype.itemsize, b.dtype.itemsize, jnp.dtype(out_dtype).itemsize)
    if mode == "nn":
        a_spec, b_spec, dims = pl.BlockSpec((tm, k), lambda i, j: (i, 0)), pl.BlockSpec((k, tn), lambda i, j: (0, j)), None
    elif mode == "nt":
        a_spec, b_spec, dims = pl.BlockSpec((tm, k), lambda i, j: (i, 0)), pl.BlockSpec((tn, k), lambda i, j: (j, 0)), _NT
    else:
        a_spec, b_spec, dims = pl.BlockSpec((k, tm), lambda i, j: (0, i)), pl.BlockSpec((k, tn), lambda i, j: (0, j)), _TN

    def body(a_ref, b_ref, *rest):
        o_ref = rest[-1]
        av, bv = a_ref[...].astype(bf16), b_ref[...].astype(bf16)
        if dims is None:
            r = jnp.dot(av, bv, preferred_element_type=f32)
        else:
            r = lax.dot_general(av, bv, dims, preferred_element_type=f32)
        o_ref[...] = r.astype(out_dtype)

    extra = [] if after is None else [after]
    return pl.pallas_call(
        body, name=name, grid=(m // tm, n // tn), in_specs=[a_spec, b_spec] + [pl.BlockSpec(memory_space=pl.ANY)] * len(extra),
        out_specs=pl.BlockSpec((tm, tn), lambda i, j: (i, j)),
        out_shape=jax.ShapeDtypeStruct((m, n), out_dtype), compiler_params=_params(2),
    )(a, b, *extra)


def _rstd(v):
    return lax.rsqrt(jnp.mean(v * v, axis=-1, keepdims=True) + RMS_EPS)


def _row_spec(tm, d):
    return pl.BlockSpec((tm, d), lambda i: (i, 0))


def _vec_spec(d, rows=1):
    return pl.BlockSpec((rows, d), lambda i: (0, 0))


def _proj_spec(a, w, tm):
    return [_row_spec(tm, a.shape[1]), pl.BlockSpec(w.shape, lambda i: (0, 0))]


def _out_proj_postnorm_prenorm(a, w, x, g_post, gate, g_pre, scale, shift, name):
    s, d = x.shape
    tm = _row_tile(s, 512)

    def body(a_ref, w_ref, x_ref, gp_ref, gate_ref, g_ref, sc_ref, sh_ref, y_ref, x2_ref, h_ref):
        yv = jnp.dot(a_ref[...], w_ref[...], preferred_element_type=f32)
        y_ref[...] = yv
        x2 = x_ref[...] + gate_ref[...] * (yv * _rstd(yv) * gp_ref[...])
        x2_ref[...] = x2
        h_ref[...] = ((x2 * _rstd(x2) * g_ref[...]) * (1.0 + sc_ref[...]) + sh_ref[...]).astype(bf16)

    return pl.pallas_call(
        body, name=name, grid=(s // tm,), in_specs=_proj_spec(a, w, tm) + [_row_spec(tm, d)] + [_vec_spec(d)] * 5,
        out_specs=[_row_spec(tm, d)] * 3,
        out_shape=[jax.ShapeDtypeStruct((s, d), f32)] * 2 + [jax.ShapeDtypeStruct((s, d), bf16)], compiler_params=_params(1),
    )(a, w, x, g_post, gate, g_pre, scale, shift)


def _rms_bwd(u, v, r):
    return r * u - v * (r * r * r) * jnp.mean(u * v, axis=-1, keepdims=True)


def _out_proj_loss_tail(a, w, x, g, gate, target, name):
    s, d = x.shape
    tm = _row_tile(s, 512)

    def body(a_ref, w_ref, x_ref, g_ref, gate_ref, t_ref, loss_ref, do_ref, dy_ref, vec_ref):
        @pl.when(pl.program_id(0) == 0)
        def _():
            loss_ref[...] = jnp.zeros_like(loss_ref)
            vec_ref[...] = jnp.zeros_like(vec_ref)
        yv = jnp.dot(a_ref[...], w_ref[...], preferred_element_type=f32)
        r = _rstd(yv)
        yn = yv * r
        err = x_ref[...] + gate_ref[...] * (yn * g_ref[...]) - t_ref[...]
        loss_ref[...] += 0.5 * jnp.sum(jnp.mean(err * err, axis=-1, keepdims=True), axis=0, keepdims=True)
        dr = err / d
        do_ref[...] = dr
        dn = dr * gate_ref[...]
        vec_ref[0:1, :] += jnp.sum(dr * (yn * g_ref[...]), axis=0, keepdims=True)
        vec_ref[1:2, :] += jnp.sum(dn * yn, axis=0, keepdims=True)
        dy_ref[...] = _rms_bwd(dn * g_ref[...], yv, r).astype(bf16)

    return pl.pallas_call(
        body, name=name, grid=(s // tm,),
        in_specs=_proj_spec(a, w, tm) + [_row_spec(tm, d)] + [_vec_spec(d)] * 2 + [_row_spec(tm, d)],
        out_specs=[_vec_spec(LANES), _row_spec(tm, d), _row_spec(tm, d), _vec_spec(d, 8)],
        out_shape=[jax.ShapeDtypeStruct((1, LANES), f32), jax.ShapeDtypeStruct((s, d), f32),
                   jax.ShapeDtypeStruct((s, d), bf16), jax.ShapeDtypeStruct((8, d), f32)],
        compiler_params=_params(1),
    )(a, w, x, g, gate, target)


def _dgrad_prenorm_bwd(terms, x, g, scale, dres, name, after=None, below=None):
    s, d = x.shape
    n = len(terms)
    k = sum(a.shape[1] for a, _, _ in terms)
    row_bytes = 2 * (2 * k) + d * (4 + 2 * 4 * 3 + (2 * 4 + 2 * 2 if below else 0))
    tm = next(t for t in (512, 256, 128) if s % t == 0 and 4 * k * d + t * row_bytes <= MATMUL_VMEM_BUDGET)
    extra = [] if after is None else [after]

    def body(*refs):
        a_refs, b_refs = refs[:n], refs[n:2 * n]
        x_ref, g_ref, sc_ref, dr_ref = refs[2 * n:2 * n + 4]
        n_in = 2 * n + 4 + (3 if below else 0) + len(extra)
        dx_ref, vec_ref = refs[n_in], refs[n_in + 1]
        if below:
            y_ref, gp_ref, gate_ref = refs[2 * n + 4:2 * n + 7]
            dy_ref, vec2_ref = refs[n_in + 2], refs[n_in + 3]

        @pl.when(pl.program_id(0) == 0)
        def _():
            vec_ref[...] = jnp.zeros_like(vec_ref)
            if below:
                vec2_ref[...] = jnp.zeros_like(vec2_ref)
        dhv = jnp.dot(a_refs[0][...], b_refs[0][...], preferred_element_type=f32)
        for i in range(1, n):
            dhv = dhv + jnp.dot(a_refs[i][...], b_refs[i][...], preferred_element_type=f32)
        xv = x_ref[...]
        r = _rstd(xv)
        xn = xv * r
        dn = dhv * (1.0 + sc_ref[...])
        vec_ref[0:1, :] += jnp.sum(dhv, axis=0, keepdims=True)
        vec_ref[1:2, :] += jnp.sum(dhv * (xn * g_ref[...]), axis=0, keepdims=True)
        vec_ref[2:3, :] += jnp.sum(dn * xn, axis=0, keepdims=True)
        dx = dr_ref[...] + _rms_bwd(dn * g_ref[...], xv, r)
        dx_ref[...] = dx
        if below:
            yv = y_ref[...]
            ry = _rstd(yv)
            yn = yv * ry
            dny = dx * gate_ref[...]
            vec2_ref[0:1, :] += jnp.sum(dx * (yn * gp_ref[...]), axis=0, keepdims=True)
            vec2_ref[1:2, :] += jnp.sum(dny * yn, axis=0, keepdims=True)
            dy_ref[...] = _rms_bwd(dny * gp_ref[...], yv, ry).astype(bf16)

    in_specs = ([_row_spec(tm, a.shape[1]) for a, _, _ in terms]
                + [pl.BlockSpec((a.shape[1], d), lambda i, r=r: (r, 0)) for a, _, r in terms]
                + [_row_spec(tm, d)] + [_vec_spec(d)] * 2 + [_row_spec(tm, d)])
    out_specs = [_row_spec(tm, d), _vec_spec(d, 8)]
    out_shape = [jax.ShapeDtypeStruct((s, d), f32), jax.ShapeDtypeStruct((8, d), f32)]
    args = [a for a, _, _ in terms] + [b for _, b, _ in terms] + [x, g, scale, dres]
    if below:
        in_specs += [_row_spec(tm, d)] + [_vec_spec(d)] * 2
        out_specs += [_row_spec(tm, d), _vec_spec(d, 8)]
        out_shape += [jax.ShapeDtypeStruct((s, d), bf16), jax.ShapeDtypeStruct((8, d), f32)]
        args += list(below)
    return pl.pallas_call(
        body, name=name, grid=(s // tm,), in_specs=in_specs + [pl.BlockSpec(memory_space=pl.ANY)] * len(extra),
        out_specs=out_specs, out_shape=out_shape, compiler_params=_params(1),
    )(*args, *extra)


def _lane():
    return lax.broadcasted_iota(jnp.int32, (1, LANES), 1)


def _rope_tables(pos_col, inv_freq, name):
    s = pos_col.shape[0]

    def body(p_ref, f_ref, cos_ref, sin_ref):
        ang = p_ref[...].astype(f32) * f_ref[...]
        first_half = (_lane() % HEAD_DIM) < HEAD_DIM // 2
        cos_ref[...] = jnp.cos(ang)
        sn = jnp.sin(ang)
        sin_ref[...] = jnp.where(first_half, -sn, sn)

    return pl.pallas_call(
        body, name=name, out_shape=[jax.ShapeDtypeStruct((s, LANES), f32)] * 2, compiler_params=_params(),
    )(pos_col, inv_freq)


def _swap_halves(v):
    first_half = (_lane() % HEAD_DIM) < HEAD_DIM // 2
    return jnp.where(first_half, pltpu.roll(v, LANES - HEAD_DIM // 2, axis=1), pltpu.roll(v, HEAD_DIM // 2, axis=1))


def _prenorm_proj_qkv(x, g, mod_scale, mod_shift, w_qkv_t, cos, sin_s, name):
    s, d = x.shape
    tm = _row_tile(s, 512)
    scale = 1.0 / math.sqrt(HEAD_DIM)

    def body(x_ref, g_ref, msc_ref, msh_ref, w_ref, c_ref, s_ref, h_ref, qa_ref, ka_ref, va_ref, qb_ref, kb_ref, vb_ref):
        xv = x_ref[...]
        h = ((xv * _rstd(xv) * g_ref[...]) * (1.0 + msc_ref[...]) + msh_ref[...]).astype(bf16)
        h_ref[...] = h
        proj = lax.dot_general(h, w_ref[...], _NT, preferred_element_type=f32)
        cs, sn = c_ref[...], s_ref[...]
        low = _lane() < HEAD_DIM

        def blk(j):
            return proj[:, j * LANES:(j + 1) * LANES]

        def rope(v):
            return v * cs + _swap_halves(v) * sn

        def expand(v):
            other = pltpu.roll(v, HEAD_DIM, axis=1)
            return jnp.where(low, v, other), jnp.where(low, other, v)

        for j in range(N_PAIRS):
            qa_ref[:, j * LANES:(j + 1) * LANES] = (rope(blk(j)) * scale).astype(bf16)
            qb_ref[:, j * LANES:(j + 1) * LANES] = (blk(6 + j) * scale).astype(bf16)
            kb_ref[:, j * LANES:(j + 1) * LANES] = blk(10 + j).astype(bf16)
            vb_ref[:, j * LANES:(j + 1) * LANES] = blk(14 + j).astype(bf16)
        k0, k1 = expand(rope(blk(4)))
        v0, v1 = expand(blk(5))
        for j in range(N_PAIRS):
            ka_ref[:, j * LANES:(j + 1) * LANES] = (k0 if j < 2 else k1).astype(bf16)
            va_ref[:, j * LANES:(j + 1) * LANES] = (v0 if j < 2 else v1).astype(bf16)

    hw = N_PAIRS * LANES
    return pl.pallas_call(
        body, name=name, grid=(s // tm,),
        in_specs=[_row_spec(tm, d)] + [_vec_spec(d)] * 3
        + [pl.BlockSpec((QKV_W, d), lambda i: (0, 0)), _row_spec(tm, LANES), _row_spec(tm, LANES)],
        out_specs=[_row_spec(tm, d)] + [_row_spec(tm, hw)] * 6,
        out_shape=[jax.ShapeDtypeStruct((s, d), bf16)] + [jax.ShapeDtypeStruct((s, hw), bf16)] * 6, compiler_params=_params(1),
    )(x, g, mod_scale, mod_shift, w_qkv_t, cos, sin_s)


def _qkv_prep_bwd(dqa_t, dka, dva, dqb_t, dkb, dvb, cos, sin_s, name):
    s = dka.shape[0]
    tm = _row_tile(s, 256)
    scale = 1.0 / math.sqrt(HEAD_DIM)
    hw = N_PAIRS * LANES
    t_spec = pl.BlockSpec((hw, tm), lambda i: (0, i))

    def body(dqa_ref, dka_ref, dva_ref, dqb_ref, dkb_ref, dvb_ref, c_ref, s_ref, o_ref):
        cs, sn = c_ref[...], s_ref[...]
        low = _lane() < HEAD_DIM

        def blk(ref, j):
            return ref[:, j * LANES:(j + 1) * LANES].astype(f32)

        def blk_t(ref, j):
            return ref[j * LANES:(j + 1) * LANES, :].T

        def unrope(v):
            return v * cs + _swap_halves(v * sn)

        def fold(ref):
            a, b = blk(ref, 0) + blk(ref, 1), blk(ref, 2) + blk(ref, 3)
            kv0 = a + pltpu.roll(a, HEAD_DIM, axis=1)
            kv1 = b + pltpu.roll(b, HEAD_DIM, axis=1)
            return jnp.where(low, kv0, kv1)

        for j in range(N_PAIRS):
            o_ref[:, j * LANES:(j + 1) * LANES] = (unrope(blk_t(dqa_ref, j)) * scale).astype(bf16)
            o_ref[:, (6 + j) * LANES:(7 + j) * LANES] = (blk_t(dqb_ref, j) * scale).astype(bf16)
            o_ref[:, (10 + j) * LANES:(11 + j) * LANES] = blk(dkb_ref, j).astype(bf16)
            o_ref[:, (14 + j) * LANES:(15 + j) * LANES] = blk(dvb_ref, j).astype(bf16)
        o_ref[:, 4 * LANES:5 * LANES] = unrope(fold(dka_ref)).astype(bf16)
        o_ref[:, 5 * LANES:6 * LANES] = fold(dva_ref).astype(bf16)

    return pl.pallas_call(
        body, name=name, grid=(s // tm,),
        in_specs=[t_spec, _row_spec(tm, hw), _row_spec(tm, hw), t_spec, _row_spec(tm, hw), _row_spec(tm, hw)] + [_row_spec(tm, LANES)] * 2,
        out_specs=_row_spec(tm, QKV_W), out_shape=jax.ShapeDtypeStruct((s, QKV_W), bf16), compiler_params=_params(1),
    )(dqa_t, dka, dva, dqb_t, dkb, dvb, cos, sin_s)


def _cumsum_rows(v, reverse=False):
    n = v.shape[0]
    row = lax.broadcasted_iota(jnp.int32, v.shape, 0)
    sh = 1
    while sh < n:
        if reverse:
            v = v + jnp.where(row < n - sh, pltpu.roll(v, n - sh, axis=0), 0.0)
        else:
            v = v + jnp.where(row >= sh, pltpu.roll(v, sh, axis=0), 0.0)
        sh *= 2
    return v


def _log_sigmoid(z):
    return jnp.minimum(z, 0.0) - jnp.log1p(jnp.exp(-jnp.abs(z)))


def _forget_prep(h, w_f_t, bf_row, name):
    s, d = h.shape
    tm = _row_tile(s, 1024)

    def body(h_ref, w_ref, b_ref, f_ref, cb_ref, last_ref):
        @pl.when(pl.program_id(0) == 0)
        def _():
            last_ref[...] = jnp.zeros_like(last_ref)
        fl = lax.dot_general(h_ref[...], w_ref[...], _NT, preferred_element_type=f32)
        f_ref[...] = fl
        cum = _cumsum_rows(_log_sigmoid(fl + b_ref[...])) + last_ref[0:1, :]
        last_ref[0:1, :] = cum[tm - 1:tm, :]
        for hd in range(N_HEADS):
            cb_ref[:, hd * LANES:(hd + 1) * LANES] = jnp.broadcast_to(cum[:, hd:hd + 1], (tm, LANES))

    return pl.pallas_call(
        body, name=name, grid=(s // tm,),
        in_specs=[_row_spec(tm, d), pl.BlockSpec((LANES, d), lambda i: (0, 0)), _vec_spec(LANES)],
        out_specs=[_row_spec(tm, LANES), _row_spec(tm, N_HEADS * LANES)],
        out_shape=[jax.ShapeDtypeStruct((s, LANES), f32), jax.ShapeDtypeStruct((s, N_HEADS * LANES), f32)],
        scratch_shapes=[pltpu.VMEM((8, LANES), f32)], compiler_params=_params(1),
    )(h, w_f_t, bf_row)


def _forget_prep_bwd(rs, dcs, fl, bf_row, name):
    s = fl.shape[0]
    tm = _row_tile(s, 1024)
    n = s // tm

    def body(r_ref, c_ref, f_ref, b_ref, df_ref, db_ref, next_ref):
        @pl.when(pl.program_id(0) == 0)
        def _():
            next_ref[...] = jnp.zeros_like(next_ref)
            db_ref[...] = jnp.zeros_like(db_ref)
        eye = (lax.broadcasted_iota(jnp.int32, (N_HEADS, LANES), 0) == lax.broadcasted_iota(jnp.int32, (N_HEADS, LANES), 1)).astype(f32)
        dcum = lax.dot_general(r_ref[...], eye, _TN, precision=lax.Precision.HIGHEST, preferred_element_type=f32)
        for h in range(N_HEADS):
            dcum = dcum - jnp.where(_lane() == h, jnp.sum(c_ref[:, h * LANES:(h + 1) * LANES], axis=1, keepdims=True), 0.0)
        dlf = _cumsum_rows(dcum, reverse=True) + next_ref[0:1, :]
        next_ref[0:1, :] = dlf[0:1, :]
        z = f_ref[...] + b_ref[...]
        df = jnp.where(_lane() < N_HEADS, dlf * jax.nn.sigmoid(-z), 0.0)
        df_ref[...] = df.astype(bf16)
        db_ref[0:1, :] += jnp.sum(df, axis=0, keepdims=True)

    def rows(width):
        return pl.BlockSpec((tm, width), lambda i: (n - 1 - i, 0))

    return pl.pallas_call(
        body, name=name, grid=(n,),
        in_specs=[pl.BlockSpec((N_HEADS, tm), lambda i: (0, n - 1 - i)), rows(N_HEADS * LANES), rows(LANES), _vec_spec(LANES)],
        out_specs=[rows(LANES), _vec_spec(LANES, 8)],
        out_shape=[jax.ShapeDtypeStruct((s, LANES), bf16), jax.ShapeDtypeStruct((8, LANES), f32)],
        scratch_shapes=[pltpu.VMEM((8, LANES), f32)], compiler_params=_params(1),
    )(rs, dcs, fl, bf_row)


def _tile_mask(n_keys, n_queries, off, window):
    shape = (n_keys, n_queries)
    d = lax.broadcasted_iota(jnp.int32, shape, 1) - lax.broadcasted_iota(jnp.int32, shape, 0) + off
    valid = d >= 0
    return jnp.logical_and(valid, d < window) if window else valid


def _wide(v, t):
    return jnp.concatenate([v] * (t // LANES), axis=1)


def _attn_fwd(q, k, v, name, *, cum_b=None, sink_rows=None, window=None, t=256):
    s = q.shape[0]
    t = _row_tile(s, t)
    fox, has_sink = cum_b is not None, sink_rows is not None
    assert not window or (window % LANES == 0 and LANES + window <= s)

    def body(*refs):
        q_ref, k_ref, v_ref = refs[:3]
        rest = list(refs[3:])
        cb_ref = rest.pop(0) if fox else None
        sink_ref = rest.pop(0) if has_sink else None
        o_ref, lse_ref = rest
        i = pl.program_id(1)
        low = _lane() < HEAD_DIM
        top = lax.broadcasted_iota(jnp.int32, (LANES, 1), 0) < HEAD_DIM
        q2 = q_ref[...]
        zero = jnp.zeros_like(q2)
        qms = (jnp.where(low, q2, zero), jnp.where(low, zero, q2))

        def tile(k0, n_keys, off, carry, masked, queries=slice(0, t)):
            nq = queries.stop - queries.start
            kblk, vblk = k_ref[pl.ds(k0, n_keys), :], v_ref[pl.ds(k0, n_keys), :]
            valid = _tile_mask(n_keys, nq, off, window) if masked else None
            ones = jnp.ones_like(vblk)
            vs = tuple(jnp.where(_lane() == L_ROW[h], ones, vblk) for h in range(2))

            def scores(h):
                return lax.dot_general(kblk, qms[h][queries], _NT, preferred_element_type=f32)

            def softmax(h, sc):
                m = carry[h][0]
                if fox:
                    sc = sc - _wide(cb_ref[pl.ds(k0, n_keys), h * LANES:(h + 1) * LANES], nq)
                if masked:
                    sc = jnp.where(valid, sc, NEG)
                m_new = jnp.maximum(m, jnp.max(sc, axis=0, keepdims=True))
                return m_new, jnp.exp(m - m_new), jnp.exp(sc - m_new).astype(bf16)

            def update(h, m_new, alpha, p):
                return m_new, alpha * carry[h][1] + lax.dot_general(vs[h], p, _TN, preferred_element_type=f32)

            if window:
                return tuple(update(h, *softmax(h, scores(h))) for h in range(2))
            scs = [scores(h) for h in range(2)]
            stats = [softmax(h, scs[h]) for h in range(2)]
            return tuple(update(h, *stats[h]) for h in range(2))

        def start(nq):
            if has_sink:
                row = lax.broadcasted_iota(jnp.int32, (LANES, nq), 0)
                return tuple((_wide(sink_ref[h:h + 1, :], nq), (row == L_ROW[h]).astype(f32)) for h in range(2))
            return tuple((jnp.full((1, nq), NEG, f32), jnp.zeros((LANES, nq), f32)) for h in range(2))

        def finish(carry, queries):
            (m0, a0), (m1, a1) = carry
            l0, l1 = a0[L_ROW[0]:L_ROW[0] + 1, :], a1[L_ROW[1]:L_ROW[1] + 1, :]
            o_t = jnp.where(top, a0 * (1.0 / l0), a1 * (1.0 / l1))
            o_ref[queries, :] = o_t.T.astype(bf16)
            lse_ref[0:1, queries] = m0 + jnp.log(l0)
            lse_ref[1:2, queries] = m1 + jnp.log(l1)

        if window:
            for c in range(t // LANES):
                queries = slice(c * LANES, (c + 1) * LANES)
                q0 = i * t + c * LANES
                k0 = pl.multiple_of(jnp.maximum(q0 - window, 0), LANES)
                finish(tile(k0, LANES + window, q0 - k0, start(LANES), True, queries), queries)
        else:
            carry = lax.fori_loop(0, i, lambda kb, c: tile(pl.multiple_of(kb * t, t), t, 0, c, False), start(t))
            half, k_own = t // 2, pl.multiple_of(i * t, t)
            carry = tile(k_own, half, 0, carry, True)
            finish(tuple((m[:, :half], a[:, :half]) for m, a in carry), slice(0, half))
            carry = tuple((m[:, half:], a[:, half:]) for m, a in carry)
            finish(tile(pl.multiple_of(k_own + half, half), half, 0, carry, True, slice(half, t)), slice(half, t))

    q_spec = pl.BlockSpec((t, LANES), lambda j, i: (i, j))
    kv_spec = pl.BlockSpec((s, LANES), lambda j, i: (0, j))
    in_specs, args = [q_spec, kv_spec, kv_spec], [q, k, v]
    if fox:
        in_specs += [pl.BlockSpec((s, 2 * LANES), lambda j, i: (0, j))]
        args += [cum_b]
    if has_sink:
        in_specs += [pl.BlockSpec((None, 2, LANES), lambda j, i: (j, 0, 0))]
        args += [sink_rows.reshape(N_PAIRS, 2, LANES)]
    return pl.pallas_call(
        body, name=name, grid=(N_PAIRS, s // t), in_specs=in_specs,
        out_specs=[q_spec, pl.BlockSpec((None, 2, t), lambda j, i: (j, 0, i))],
        out_shape=[jax.ShapeDtypeStruct((s, N_PAIRS * LANES), bf16), jax.ShapeDtypeStruct((N_PAIRS, 2, s), f32)],
        compiler_params=_params(2),
    )(*args)


def _branch_dgrad_delta(db, w, o, name, *, lse=None, sink_rows=None, after=None):
    s, hw = o.shape
    tm = _row_tile(s, 512)
    has_sink = sink_rows is not None
    extra = [] if after is None else [after]

    def body(*refs):
        db_ref, w_ref, o_ref = refs[:3]
        outs = refs[3 + (2 if has_sink else 0) + len(extra):]
        do_ref, dl_ref = outs[:2]
        if has_sink:
            lse_ref, sink_ref = refs[3:5]
            ds_ref = outs[2]

            @pl.when(pl.program_id(0) == 0)
            def _():
                ds_ref[...] = jnp.zeros_like(ds_ref)
        do = lax.dot_general(db_ref[...], w_ref[...], _NT, preferred_element_type=f32).astype(bf16)
        do_ref[...] = do
        for j in range(N_PAIRS):
            cols = slice(j * LANES, (j + 1) * LANES)
            prod_t = (do[:, cols].astype(f32) * o_ref[:, cols].astype(f32)).T
            for h in range(2):
                dl = jnp.sum(prod_t[h * HEAD_DIM:(h + 1) * HEAD_DIM, :], axis=0, keepdims=True)
                dl_ref[j, h:h + 1, :] = dl
                if has_sink:
                    r = 2 * j + h
                    p_sink = jnp.exp(sink_ref[r:r + 1, 0:1] - lse_ref[j, h:h + 1, :])
                    ds_ref[r:r + 1, :] += -jnp.sum(p_sink * dl, axis=1, keepdims=True)

    rows_spec = pl.BlockSpec((N_PAIRS, 2, tm), lambda i: (0, 0, i))
    in_specs = [_row_spec(tm, db.shape[1]), pl.BlockSpec(w.shape, lambda i: (0, 0)), _row_spec(tm, hw)]
    args = [db, w, o]
    out_specs = [_row_spec(tm, hw), rows_spec]
    out_shape = [jax.ShapeDtypeStruct((s, hw), bf16), jax.ShapeDtypeStruct((N_PAIRS, 2, s), f32)]
    if has_sink:
        in_specs += [rows_spec, _vec_spec(LANES, N_HEADS)]
        args += [lse, sink_rows]
        out_specs += [_vec_spec(LANES, N_HEADS)]
        out_shape += [jax.ShapeDtypeStruct((N_HEADS, LANES), f32)]
    return pl.pallas_call(
        body, name=name, grid=(s // tm,), in_specs=in_specs + [pl.BlockSpec(memory_space=pl.ANY)] * len(extra),
        out_specs=out_specs, out_shape=out_shape, compiler_params=_params(1),
    )(*args, *extra)


def _attn_bwd(q, k, v, do, lse, delta, name, *, cum_b=None, window=None, t=256):
    s = q.shape[0]
    t = _row_tile(s, t)
    nblk = s // t
    fox = cum_b is not None
    assert not window or (window % LANES == 0 and LANES + window <= s)

    def body(*refs):
        k_ref, v_ref, q_ref, do_ref, lse_ref, dl_ref = refs[:6]
        rest = list(refs[6:])
        cb_ref = rest.pop(0) if fox else None
        dq_ref, dk_ref, dv_ref = rest[:3]
        dcs_ref, rs_ref = (rest[3], rest[4]) if fox else (None, None)
        dk_acc, dv_acc = rest[-2:]
        b = pl.program_id(1)
        k0 = pl.multiple_of(b * t, t)

        @pl.when(b == 0)
        def _():
            dq_ref[...] = jnp.zeros_like(dq_ref)
            if fox:
                rs_ref[...] = jnp.zeros_like(rs_ref)

        dk_acc[...] = jnp.zeros_like(dk_acc)
        dv_acc[...] = jnp.zeros_like(dv_acc)
        if fox:
            dcs_ref[...] = jnp.zeros_like(dcs_ref)
        low = _lane() < HEAD_DIM
        top = lax.broadcasted_iota(jnp.int32, (LANES, 1), 0) < HEAD_DIM
        kblk, vblk = k_ref[...], v_ref[...]
        k_t = kblk.astype(f32).T.astype(bf16)
        cks = [_wide(cb_ref[pl.ds(k0, t), h * LANES:(h + 1) * LANES], t) for h in range(2)] if fox else None

        def tile(q0, n_queries, off, masked, keys=slice(0, t)):
            cols = pl.ds(q0, n_queries)
            q2, do2 = q_ref[cols, :], do_ref[cols, :]
            zero = jnp.zeros_like(q2)
            valid = _tile_mask(keys.stop - keys.start, n_queries, off, window) if masked else None
            dq_parts = []
            for h in range(2):
                qm = jnp.where(low, q2, zero) if h == 0 else jnp.where(low, zero, q2)
                dom = jnp.where(low, do2, zero) if h == 0 else jnp.where(low, zero, do2)
                sc = lax.dot_general(kblk[keys], qm, _NT, preferred_element_type=f32)
                if fox:
                    sc = sc - cks[h][keys, :n_queries]
                if masked:
                    sc = jnp.where(valid, sc, NEG)
                p = jnp.exp(sc - lse_ref[h:h + 1, cols])
                dp = lax.dot_general(vblk[keys], dom, _NT, preferred_element_type=f32)
                ds = p * (dp - dl_ref[h:h + 1, cols])
                pb, dsb = p.astype(bf16), ds.astype(bf16)
                dv_acc[keys, :] += jnp.dot(pb, dom, preferred_element_type=f32)
                dk_acc[keys, :] += jnp.dot(dsb, qm, preferred_element_type=f32)
                dq_parts.append(jnp.dot(k_t[:, keys], dsb, preferred_element_type=f32))
                if fox:
                    dcs_ref[keys, h * LANES:(h + 1) * LANES] += sum(ds[:, g * LANES:(g + 1) * LANES]
                                                                    for g in range(n_queries // LANES))
                    rs_ref[h:h + 1, cols] += jnp.sum(ds, axis=0, keepdims=True)
            dq_ref[:, cols] += jnp.where(top, dq_parts[0], dq_parts[1])

        def later_block(qb, carry):
            tile(pl.multiple_of(qb * t, t), t, 0, False)
            return carry

        if window:
            for c in range(t // LANES):
                first = b * t + c * LANES
                q0 = pl.multiple_of(jnp.minimum(first, s - (LANES + window)), LANES)
                tile(q0, LANES + window, q0 - first, True, slice(c * LANES, (c + 1) * LANES))
        else:
            half = t // 2
            tile(k0, half, 0, True, slice(0, half))
            tile(pl.multiple_of(k0 + half, half), half, half, True)
            lax.fori_loop(b + 1, nblk, later_block, 0)
        dk_ref[...] = dk_acc[...].astype(bf16)
        dv_ref[...] = dv_acc[...].astype(bf16)

    kv_spec = pl.BlockSpec((t, LANES), lambda j, b: (b, j))
    seq_spec = pl.BlockSpec((s, LANES), lambda j, b: (0, j))
    rows_spec = pl.BlockSpec((None, 2, s), lambda j, b: (j, 0, 0))
    hw = N_PAIRS * LANES
    in_specs, args = [kv_spec, kv_spec, seq_spec, seq_spec, rows_spec, rows_spec], [k, v, q, do, lse, delta]
    out_specs = [pl.BlockSpec((LANES, s), lambda j, b: (j, 0)), kv_spec, kv_spec]
    out_shape = [jax.ShapeDtypeStruct((hw, s), f32), jax.ShapeDtypeStruct((s, hw), bf16), jax.ShapeDtypeStruct((s, hw), bf16)]
    if fox:
        in_specs += [pl.BlockSpec((s, 2 * LANES), lambda j, b: (0, j))]
        args += [cum_b]
        out_specs += [pl.BlockSpec((t, 2 * LANES), lambda j, b: (b, j)), rows_spec]
        out_shape += [jax.ShapeDtypeStruct((s, N_HEADS * LANES), f32), jax.ShapeDtypeStruct((N_PAIRS, 2, s), f32)]
    return pl.pallas_call(
        body, name=name, grid=(N_PAIRS, nblk), in_specs=in_specs, out_specs=out_specs, out_shape=out_shape,
        scratch_shapes=[pltpu.VMEM((t, LANES), f32)] * 2, compiler_params=_params(2),
    )(*args)


def _branch_merge(o_a, o_b, w_a, w_b, gl, name):
    s, k = o_a.shape
    d = w_a.shape[1]
    tm = _row_tile(s, 1024)

    def body(oa_ref, ob_ref, wa_ref, wb_ref, g_ref, ba_ref, bb_ref, m_ref):
        ba = jnp.dot(oa_ref[...], wa_ref[...], preferred_element_type=f32)
        bb = jnp.dot(ob_ref[...], wb_ref[...], preferred_element_type=f32)
        g0, g1 = jax.nn.sigmoid(g_ref[:, :d].astype(f32)), jax.nn.sigmoid(g_ref[:, d:].astype(f32))
        ba_ref[...] = ba.astype(bf16)
        bb_ref[...] = bb.astype(bf16)
        m_ref[...] = (g0 * ba + g1 * bb).astype(bf16)

    whole = pl.BlockSpec((k, d), lambda i: (0, 0))
    return pl.pallas_call(
        body, name=name, grid=(s // tm,),
        in_specs=[_row_spec(tm, k), _row_spec(tm, k), whole, whole, _row_spec(tm, 2 * d)],
        out_specs=[_row_spec(tm, d)] * 3, out_shape=[jax.ShapeDtypeStruct((s, d), bf16)] * 3, compiler_params=_params(1),
    )(o_a, o_b, w_a, w_b, gl)


def _out_dgrad_merge_bwd(dy, w_out, ba, bb, gl, name):
    s, d = ba.shape
    tm = _row_tile(s, 512)

    def body(dy_ref, w_ref, a_ref, b_ref, g_ref, da_ref, db_ref, dg_ref):
        dmv = lax.dot_general(dy_ref[...], w_ref[...], _NT, preferred_element_type=f32)
        g0, g1 = jax.nn.sigmoid(g_ref[:, :d].astype(f32)), jax.nn.sigmoid(g_ref[:, d:].astype(f32))
        da_ref[...] = (dmv * g0).astype(bf16)
        db_ref[...] = (dmv * g1).astype(bf16)
        dg_ref[:, :d] = (dmv * a_ref[...].astype(f32) * (g0 * (1.0 - g0))).astype(bf16)
        dg_ref[:, d:] = (dmv * b_ref[...].astype(f32) * (g1 * (1.0 - g1))).astype(bf16)

    return pl.pallas_call(
        body, name=name, grid=(s // tm,),
        in_specs=[_row_spec(tm, dy.shape[1]), pl.BlockSpec(w_out.shape, lambda i: (0, 0))] + [_row_spec(tm, d)] * 2
        + [_row_spec(tm, 2 * d)],
        out_specs=[_row_spec(tm, d)] * 2 + [_row_spec(tm, 2 * d)],
        out_shape=[jax.ShapeDtypeStruct((s, d), bf16)] * 2 + [jax.ShapeDtypeStruct((s, 2 * d), bf16)],
        compiler_params=_params(1),
    )(dy, w_out, ba, bb, gl)


GLU_TILE = 256


def _ffn_in_swiglu(h, w_t, name):
    s, d = h.shape
    f = w_t.shape[0] // 2
    tm = _row_tile(s, 4096)
    tg = GLU_TILE
    nb = f // tg

    def body(h_ref, wg_ref, wu_ref, g_ref, u_ref, act_ref):
        hv = h_ref[...]
        g = lax.dot_general(hv, wg_ref[...], _NT, preferred_element_type=f32)
        u = lax.dot_general(hv, wu_ref[...], _NT, preferred_element_type=f32)
        g_ref[...] = g.astype(bf16)
        u_ref[...] = u.astype(bf16)
        act_ref[...] = (g * jax.nn.sigmoid(g) * u).astype(bf16)

    col = pl.BlockSpec((tm, tg), lambda i, j: (i, j))
    return pl.pallas_call(
        body, name=name, grid=(s // tm, nb),
        in_specs=[pl.BlockSpec((tm, d), lambda i, j: (i, 0)), pl.BlockSpec((tg, d), lambda i, j: (j, 0)),
                  pl.BlockSpec((tg, d), lambda i, j: (j + nb, 0))],
        out_specs=[col] * 3, out_shape=[jax.ShapeDtypeStruct((s, f), bf16)] * 3, compiler_params=_params(2),
    )(h, w_t, w_t)


def _ffn_out_dgrad_swiglu(dy, w_out, g, u, name):
    s, d = dy.shape
    f = g.shape[1]
    tm = _row_tile(s, 4096)
    tg = GLU_TILE

    def body(dy_ref, w_ref, g_ref, u_ref, dg_ref, du_ref):
        dv = lax.dot_general(dy_ref[...], w_ref[...], _NT, preferred_element_type=f32)
        gv, uv = g_ref[...].astype(f32), u_ref[...].astype(f32)
        sg = jax.nn.sigmoid(gv)
        dg_ref[...] = (dv * uv * (sg * (1.0 + gv * (1.0 - sg)))).astype(bf16)
        du_ref[...] = (dv * (gv * sg)).astype(bf16)

    col = pl.BlockSpec((tm, tg), lambda i, j: (i, j))
    return pl.pallas_call(
        body, name=name, grid=(s // tm, f // tg),
        in_specs=[pl.BlockSpec((tm, d), lambda i, j: (i, 0)), pl.BlockSpec((tg, d), lambda i, j: (j, 0)), col, col],
        out_specs=[col] * 2, out_shape=[jax.ShapeDtypeStruct((s, f), bf16)] * 2, compiler_params=_params(2),
    )(dy, w_out, g, u)


def _wgrad_stack(parts, h, name):
    s, m = parts[0].shape
    d = h.shape[1]
    tm = 256
    nb = m // tm
    n = len(parts)

    def body(*refs):
        i = pl.program_id(0)
        for p in range(n):
            @pl.when(i // nb == p)
            def _(p=p):
                refs[n + 1][...] = lax.dot_general(refs[p][...], refs[n][...], _TN, preferred_element_type=f32).astype(bf16)

    a_specs = [pl.BlockSpec((s, tm), lambda i, p=p: (0, jnp.clip(i - p * nb, 0, nb - 1))) for p in range(n)]
    return pl.pallas_call(
        body, name=name, grid=(n * nb,), in_specs=a_specs + [pl.BlockSpec((s, d), lambda i: (0, 0))],
        out_specs=pl.BlockSpec((tm, d), lambda i: (i, 0)),
        out_shape=jax.ShapeDtypeStruct((n * m, d), bf16), compiler_params=_params(1),
    )(*parts, h)


def _ada_wgrad(c_all, d_all, name):
    n, d = c_all.shape
    w = d_all.shape[1]

    def body(c_ref, d_ref, o_ref):
        eye = (lax.broadcasted_iota(jnp.int32, (n, n), 0) == lax.broadcasted_iota(jnp.int32, (n, n), 1)).astype(f32)
        ct = lax.dot_general(c_ref[...], eye, _TN, precision=lax.Precision.HIGHEST, preferred_element_type=f32)
        g = ct[:, 0:1] * d_ref[0:1, :]
        for bi in range(1, n):
            g = g + ct[:, bi:bi + 1] * d_ref[bi:bi + 1, :]
        o_ref[0] = g

    return pl.pallas_call(
        body, name=name, out_shape=jax.ShapeDtypeStruct((1, d, w), f32), compiler_params=_params(),
    )(c_all, d_all)


def _adamw(parts, w, m, v, name, mine=None, me=None):
    r, c = w.shape
    n_parts = parts.shape[0]
    row_tiles = [t for t in range(min(r, 256), 0, -1) if r % t == 0 and (t % 16 == 0 or t == r)]
    if row_tiles:
        tr, tc = row_tiles[0], c
    else:
        tr, tc = r, next(t for t in (256, LANES) if c % t == 0)

    def body(*refs):
        w_ref, m_ref, v_ref, g_ref, d_ref, nm_ref, nv_ref = refs[-7:]
        if mine is None:
            p_ref, = refs[:-7]
        else:
            me_ref, p_ref, own_ref = refs[:-7]

        def part(i):
            if mine is None:
                return p_ref[i].astype(f32)
            return jnp.where(me_ref[0] == i, own_ref[...], p_ref[i]).astype(f32)

        g = part(0)
        for i in range(1, n_parts):
            g = g + part(i)
        mm = ADAM_B1 * m_ref[...] + (1.0 - ADAM_B1) * g
        vv = ADAM_B2 * v_ref[...] + (1.0 - ADAM_B2) * (g * g)
        m_hat = mm / (1.0 - ADAM_B1 ** ADAM_STEP)
        v_hat = vv / (1.0 - ADAM_B2 ** ADAM_STEP)
        g_ref[...] = g
        d_ref[...] = -ADAM_LR * (m_hat / (jnp.sqrt(v_hat) + ADAM_EPS) + ADAM_WD * w_ref[...])
        nm_ref[...] = mm
        nv_ref[...] = vv

    out_shape = [jax.ShapeDtypeStruct((r, c), f32)] * 4
    if mine is None:
        spec = pl.BlockSpec((tr, tc), lambda i, j: (i, j))
        return pl.pallas_call(
            body, name=name, grid=(r // tr, c // tc),
            in_specs=[pl.BlockSpec((n_parts, tr, tc), lambda i, j: (0, i, j))] + [spec] * 3,
            out_specs=[spec] * 4, out_shape=out_shape, compiler_params=_params(2),
        )(parts, w, m, v)
    spec = pl.BlockSpec((tr, tc), lambda i, j, me_ref: (i, j))
    return pl.pallas_call(
        body, name=name, out_shape=out_shape, compiler_params=_params(2),
        grid_spec=pltpu.PrefetchScalarGridSpec(
            num_scalar_prefetch=1, grid=(r // tr, c // tc),
            in_specs=[pl.BlockSpec((n_parts, tr, tc), lambda i, j, me_ref: (0, i, j)),
                      pl.BlockSpec((None, tr, tc), lambda i, j, me_ref: (me_ref[0], i, j))] + [spec] * 3,
            out_specs=[spec] * 4),
    )(me, parts, mine, w, m, v)


def _me():
    return lax.axis_index("x"), lax.axis_index("y"), lax.axis_index("c")


def _gather_prologue(c, w_ada, b_mine, w_in_t, name):
    n_dev, d = N_DEV, c.shape[1]
    ada_w = w_ada.shape[1]

    def body(c_ref, w_ref, b_ref, win_ref, call_ref, ada_ref, gin_ref, cols_ref, send_sems, recv_sems, local_sems):
        x, y, cc = _me()
        me, sibling = (x, y, cc), (x, y, 1 - cc)
        chips = [(1 - x, y), (x, 1 - y), (1 - x, 1 - y)]
        outs = (call_ref, ada_ref, gin_ref)

        def rows(a, dev):
            return outs[a].at[4 * dev[0] + 2 * dev[1] + dev[2]]

        def copy(a, k, block, to, src=None):
            return pltpu.make_async_remote_copy(
                src_ref=rows(a, block) if src is None else src, dst_ref=rows(a, block),
                send_sem=send_sems.at[a, k], recv_sem=recv_sems.at[a, k], device_id=to, device_id_type=MESH)

        def begin(a, src):
            own = pltpu.make_async_copy(src, rows(a, me), local_sems.at[a])
            sends = [copy(a, 0, me, sibling, src=src)] + [copy(a, 1 + j, me, (*chip, cc), src=src) for j, chip in enumerate(chips)]
            for cp in [own] + sends:
                cp.start()
            return own, sends

        def finish(a, own, sends):
            passed = []
            for j, chip in enumerate(chips):
                copy(a, 1 + j, (*chip, cc), me).wait_recv()
                passed.append(copy(a, 4 + j, (*chip, cc), sibling))
                passed[-1].start()
            copy(a, 0, sibling, me).wait_recv()
            for j, chip in enumerate(chips):
                copy(a, 4 + j, (*chip, 1 - cc), me).wait_recv()
            for cp in sends + passed:
                cp.wait_send()
            own.wait()

        finish(0, *begin(0, c_ref))
        cols_ref[...] = (jnp.dot(call_ref[:, 0, :].astype(bf16), w_ref[...].astype(bf16), preferred_element_type=f32)
                         + b_ref[...])
        finish(1, *begin(1, cols_ref))
        finish(2, *begin(2, win_ref))

    vmem, hbm = pl.BlockSpec(memory_space=pltpu.VMEM), pl.BlockSpec(memory_space=pl.ANY)
    return pl.pallas_call(
        body, name=name, in_specs=[vmem, vmem, vmem, hbm], out_specs=[vmem, vmem, hbm],
        out_shape=[jax.ShapeDtypeStruct((n_dev, 1, d), f32), jax.ShapeDtypeStruct((n_dev, n_dev, ada_w), f32),
                   jax.ShapeDtypeStruct((n_dev,) + w_in_t.shape, w_in_t.dtype)],
        scratch_shapes=[pltpu.VMEM((n_dev, ada_w), f32), pltpu.SemaphoreType.DMA((3, 7)), pltpu.SemaphoreType.DMA((3, 7)),
                        pltpu.SemaphoreType.DMA((3,))],
        compiler_params=pltpu.CompilerParams(vmem_limit_bytes=VMEM_LIMIT),
    )(c, w_ada, b_mine, w_in_t)


_FLIPS = ((0, 0, 1), (1, 0, 0), (0, 1, 0), (1, 1, 0), (1, 0, 1), (0, 1, 1), (1, 1, 1))
_HBM = pl.BlockSpec(memory_space=pltpu.HBM)
_SEM = pl.BlockSpec(memory_space=pltpu.SEMAPHORE)


def _exchange_copies(scatter, srcs, lands, send_sems, recv_sems):
    x, y, c = _me()
    me_row = 4 * x + 2 * y + c
    out = []
    for k, (fx, fy, fc) in enumerate(_FLIPS):
        peer = (x ^ fx, y ^ fy, c ^ fc)
        peer_row = 4 * peer[0] + 2 * peer[1] + peer[2]
        for a in range(len(srcs)):
            out.append(pltpu.make_async_remote_copy(
                src_ref=srcs[a].at[peer_row] if scatter else srcs[a], dst_ref=lands[a].at[me_row],
                send_sem=send_sems.at[7 * a + k], recv_sem=recv_sems.at[7 * a + k], device_id=peer, device_id_type=MESH))
    return out


def _own_copies(srcs, lands, own_sems):
    x, y, c = _me()
    return [pltpu.make_async_copy(srcs[a], lands[a].at[4 * x + 2 * y + c], own_sems.at[a]) for a in range(len(srcs))]


def _exchange_start(arrays, scatter, name, after=None):
    n = len(arrays)
    lands = [lax.empty(a.shape if scatter else (N_DEV,) + a.shape, a.dtype) for a in arrays]
    extra = [] if after is None else [after]

    def body(*refs):
        srcs, zones = refs[:n], refs[n:2 * n]
        send_sems, recv_sems, own_sems = refs[2 * n + len(extra):2 * n + len(extra) + 3]
        token = refs[-1]
        for cp in _exchange_copies(scatter, srcs, zones, send_sems, recv_sems):
            cp.start()
        for cp in [] if scatter else _own_copies(srcs, zones, own_sems):
            cp.start()
        token[...] = jnp.zeros_like(token)

    thru = [pltpu.HBM(a.shape, a.dtype) for a in list(arrays) + lands]
    outs = pl.pallas_call(
        body, name=name,
        out_shape=(pltpu.SemaphoreType.DMA((7 * n,)), pltpu.SemaphoreType.DMA((7 * n,)), pltpu.SemaphoreType.DMA((n,)), *thru,
                   jax.ShapeDtypeStruct((8, LANES), f32)),
        in_specs=[_HBM] * (2 * n) + [pl.BlockSpec(memory_space=pl.ANY)] * len(extra),
        out_specs=(_SEM, _SEM, _SEM, *[_HBM] * (2 * n), pl.BlockSpec(memory_space=pltpu.VMEM)),
        input_output_aliases={i: 3 + i for i in range(2 * n)},
        compiler_params=pltpu.CompilerParams(has_side_effects=pltpu.SideEffectType.DATAFLOW_SIDE_EFFECTING),
    )(*[pltpu.with_memory_space_constraint(a, pltpu.HBM) for a in list(arrays) + lands], *extra)
    return dict(n=n, scatter=scatter, sems=outs[:3], srcs=outs[3:3 + n], lands=outs[3 + n:3 + 2 * n], token=outs[-1])


def _exchange_wait(handle, after, name):
    n, scatter = handle["n"], handle["scatter"]

    def body(*refs):
        srcs, zones = refs[:n], refs[n:2 * n]
        send_sems, recv_sems, own_sems = refs[2 * n:2 * n + 3]
        for cp in _exchange_copies(scatter, srcs, zones, send_sems, recv_sems):
            cp.wait_send()
            cp.wait_recv()
        for cp in [] if scatter else _own_copies(srcs, zones, own_sems):
            cp.wait()

    thru = [pltpu.HBM(a.shape, a.dtype) for a in list(handle["srcs"]) + list(handle["lands"])]
    outs = pl.pallas_call(
        body, name=name, out_shape=tuple(thru),
        in_specs=[_HBM] * (2 * n) + [_SEM, _SEM, _SEM, pl.BlockSpec(memory_space=pl.ANY)], out_specs=tuple([_HBM] * (2 * n)),
        input_output_aliases={i: i for i in range(2 * n)},
        compiler_params=pltpu.CompilerParams(has_side_effects=pltpu.SideEffectType.DATAFLOW_SIDE_EFFECTING),
    )(*handle["srcs"], *handle["lands"], *handle["sems"], after)
    return list(outs[:n]), list(outs[n:])


def _cols_from_shards(g):
    return jnp.transpose(g, (1, 0, 2)).reshape(g.shape[1], -1)


def _shards_from_cols(a):
    return jnp.transpose(a.reshape(a.shape[0], N_DEV, -1), (1, 0, 2))


def _local_step(x, positions, ada, g_pre_mix, g_post_mix, b_f, sinks, g_pre_ffn, g_post_ffn, target,
                w_in_t, mix_weights, ffn_weights, on_grads):
    s, d = x.shape
    row = lambda v: v.reshape(1, -1)
    shift_m, scale_m, gate_m, shift_f, scale_f, gate_f = (ada[i:i + 1] for i in range(6))
    w_gate_t, w_qkv_t = w_in_t[F_OFF + N_HEADS:], w_in_t
    w_f_t = jnp.pad(w_in_t[F_OFF:F_OFF + N_HEADS], ((0, LANES - N_HEADS), (0, 0)))
    bf_row = jnp.pad(row(b_f), ((0, 0), (0, LANES - N_HEADS)))
    sink_rows = jnp.broadcast_to(sinks.reshape(N_HEADS, 1).astype(f32), (N_HEADS, LANES))
    inv_freq = 1.0 / (ROPE_THETA ** (jnp.arange(0, HEAD_DIM, 2, dtype=f32) / HEAD_DIM))
    cos, sin_s = _rope_tables(positions.reshape(s, 1), jnp.tile(inv_freq, 4).reshape(1, LANES), "rope_tables")

    h1, qa, ka, va, qb, kb, vb = _prenorm_proj_qkv(x, row(g_pre_mix), scale_m, shift_m, w_qkv_t, cos, sin_s, "prenorm_proj_qkv")
    gl = _matmul(h1, w_gate_t, "nt", bf16, "proj_gate")
    fl, cum_b = _forget_prep(h1, w_f_t, bf_row, "proj_forget_prep")
    o_a, lse_a = _attn_fwd(qa, ka, va, "swa_fwd", sink_rows=sink_rows, window=WINDOW, t=2048)
    o_b, lse_b = _attn_fwd(qb, kb, vb, "fox_fwd", cum_b=cum_b, t=1024)
    everything_before = (gl[:8, :LANES] + o_a[:8, :LANES] + o_b[:8, :LANES]).astype(f32)
    w_branch_a, w_branch_b, w_out = mix_weights(everything_before)
    ba, bb, merged = _branch_merge(o_a, o_b, w_branch_a, w_branch_b, gl, "branch_merge")
    y1, x2, h2 = _out_proj_postnorm_prenorm(merged, w_out, x, row(g_post_mix), gate_m, row(g_pre_ffn), scale_f, shift_f,
                                            "out_proj_norms")

    w_ffn_in_t, w_ffn_out = ffn_weights(h2)
    g_ff, u_ff, act = _ffn_in_swiglu(h2, w_ffn_in_t, "ffn_in_swiglu")
    loss_row, d_out, d_y2, vec_pf = _out_proj_loss_tail(act, w_ffn_out, x2, row(g_post_ffn), gate_f, target, "ffn_out_loss_tail")

    g_w_ffn_out = _matmul(act, d_y2, "tn", bf16, "ffn_out_wgrad")
    dg_ff, du_ff = _ffn_out_dgrad_swiglu(d_y2, w_ffn_out, g_ff, u_ff, "ffn_out_dgrad_swiglu")
    g_w_ffn_in_t = _wgrad_stack([dg_ff, du_ff], h2, "ffn_in_wgrad")
    sent = on_grads(dict(w_ffn_in=g_w_ffn_in_t, w_ffn_out=g_w_ffn_out))
    d_x2, vec_nf, d_y1, vec_pm = _dgrad_prenorm_bwd(
        [(dg_ff, w_ffn_in_t, 0), (du_ff, w_ffn_in_t, 1)], x2, row(g_pre_ffn), scale_f, d_out, "ffn_in_dgrad_norms_bwd",
        after=sent, below=(y1, row(g_post_mix), gate_m))

    g_w_out = _matmul(merged, d_y1, "tn", bf16, "out_proj_wgrad")
    d_ba, d_bb, dgl = _out_dgrad_merge_bwd(d_y1, w_out, ba, bb, gl, "out_proj_dgrad_merge_bwd")
    g_w_branch_a = _matmul(o_a, d_ba, "tn", bf16, "branch_a_wgrad")
    g_w_branch_b = _matmul(o_b, d_bb, "tn", bf16, "branch_b_wgrad")
    sent = on_grads(dict(w_out=g_w_out, w_branch_a=g_w_branch_a, w_branch_b=g_w_branch_b))
    d_oa, delta_a, d_sink = _branch_dgrad_delta(d_ba, w_branch_a, o_a, "branch_a_dgrad_delta", lse=lse_a,
                                                sink_rows=sink_rows, after=sent)
    d_ob, delta_b = _branch_dgrad_delta(d_bb, w_branch_b, o_b, "branch_b_dgrad_delta", after=sent)
    dqa_t, dka, dva = _attn_bwd(qa, ka, va, d_oa, lse_a, delta_a, "swa_bwd", window=WINDOW, t=2048)
    dqb_t, dkb, dvb, dcs, rs = _attn_bwd(qb, kb, vb, d_ob, lse_b, delta_b, "fox_bwd", cum_b=cum_b, t=512)
    dqkv = _qkv_prep_bwd(dqa_t, dka, dva, dqb_t, dkb, dvb, cos, sin_s, "qkv_prep_bwd")
    dfl, vec_bf = _forget_prep_bwd(rs.reshape(N_HEADS, s), dcs, fl, bf_row, "forget_prep_bwd")
    g_w_in_t = jnp.concatenate([_matmul(dqkv, h1, "tn", bf16, "qkv_wgrad"), _matmul(dfl, h1, "tn", bf16, "forget_wgrad")[:N_HEADS],
                                _matmul(dgl, h1, "tn", bf16, "gate_wgrad")], axis=0)
    sent = on_grads(dict(w_in=g_w_in_t))
    grad_x, vec_nm = _dgrad_prenorm_bwd([(dgl, w_gate_t, 0), (dqkv, w_qkv_t, 0), (dfl, w_f_t, 0)], x, row(g_pre_mix),
                                        scale_m, d_x2, "in_proj_dgrad_prenorm_bwd", after=sent)

    d_ada = jnp.concatenate([vec_nm[0], vec_nm[1], vec_pm[0], vec_nf[0], vec_nf[1], vec_pf[0]])
    small = dict(b_ada=d_ada, g_pre_mix=vec_nm[2], g_post_mix=vec_pm[1], g_pre_ffn=vec_nf[2], g_post_ffn=vec_pf[1],
                 b_f=vec_bf[0, :N_HEADS], sinks=d_sink[:, 0], loss=loss_row[0, :1])
    return grad_x, small


_SMALL = (("b_ada", 6144), ("g_pre_mix", 1024), ("g_post_mix", 1024), ("g_pre_ffn", 1024), ("g_post_ffn", 1024),
          ("b_f", 128), ("sinks", 128), ("loss", 128))
_SMALL_ROWS = 88


def _pack_small(vals):
    parts = [jnp.pad(vals[k].reshape(-1).astype(f32), (0, n - vals[k].size)) for k, n in _SMALL]
    flat = jnp.concatenate(parts)
    return jnp.pad(flat, (0, _SMALL_ROWS * LANES - flat.size)).reshape(_SMALL_ROWS, LANES)


def _unpack_small(slab, shapes):
    flat, out, off = slab.reshape(-1), {}, 0
    for k, n in _SMALL:
        size = math.prod(shapes[k])
        out[k] = flat[off:off + size].reshape(shapes[k])
        off += n
    return out


def kernel(x, c, positions, w_ada, b_ada, g_pre_mix, g_post_mix, w_in, b_f, sinks, w_branch_a, w_branch_b, w_out, g_pre_ffn, g_post_ffn, w_ffn_in, w_ffn_out, loss_target, m_w_ada, m_b_ada, m_g_pre_mix, m_g_post_mix, m_w_in, m_b_f, m_sinks, m_w_branch_a, m_w_branch_b, m_w_out, m_g_pre_ffn, m_g_post_ffn, m_w_ffn_in, m_w_ffn_out, v_w_ada, v_b_ada, v_g_pre_mix, v_g_post_mix, v_w_in, v_b_f, v_sinks, v_w_branch_a, v_w_branch_b, v_w_out, v_g_pre_ffn, v_g_post_ffn, v_w_ffn_in, v_w_ffn_out):
    xi, yi, ci = _me()
    me = 4 * xi + 2 * yi + ci
    d = D_MODEL
    ada_w = w_ada.shape[2]

    transposed = ("w_in", "w_ffn_in")
    tr = lambda a: jnp.transpose(a[0])

    b_mine = lax.dynamic_slice(b_ada, (0, me * ada_w), (1, ada_w))
    c_all, ada_all, g_in = _gather_prologue(c, w_ada[0], b_mine, tr(w_in).astype(bf16), "gather_prologue")
    c_all = c_all.reshape(N_DEV, d)
    ada = lax.dynamic_index_in_dim(ada_all, me, axis=1, keepdims=False).reshape(6, d)
    late_mix = [w.astype(bf16) for w in (w_branch_a[0], w_branch_b[0], w_out[0])]
    late_ffn = [w.astype(bf16) for w in (tr(w_ffn_in), w_ffn_out[0])]
    mix_h = _exchange_start(late_mix, False, "gather_mix_start", after=g_in)
    ffn_h = _exchange_start(late_ffn, False, "gather_ffn_start", after=mix_h["token"])

    def rows_from_shards(g):
        return g.reshape(g.shape[0] * g.shape[1], g.shape[2])

    def mix_weights(after):
        _, (g_ba, g_bb, g_out) = _exchange_wait(mix_h, after, "gather_mix_wait")
        return _cols_from_shards(g_ba), _cols_from_shards(g_bb), rows_from_shards(g_out)

    def ffn_weights(after):
        _, (g_fi, g_fo) = _exchange_wait(ffn_h, after, "gather_ffn_wait")
        return rows_from_shards(g_fi), rows_from_shards(g_fo)

    row_sharded = ("w_out", "w_ffn_out") + transposed
    in_flight = []

    def on_grads(group):
        sends = [g.reshape(N_DEV, g.shape[0] // N_DEV, g.shape[1]) if nm in row_sharded else _shards_from_cols(g)
                 for nm, g in group.items()]
        handle = _exchange_start(sends, True, "scatter_start_%d" % len(in_flight))
        in_flight.append((list(group), handle))
        return handle["token"]

    grad_x, small = _local_step(
        x[0], positions[0], ada + ffn_h["token"][0, 0], g_pre_mix[0], g_post_mix[0], b_f[0], sinks[0], g_pre_ffn[0],
        g_post_ffn[0], loss_target[0], rows_from_shards(g_in), mix_weights, ffn_weights, on_grads)

    ws = dict(w_in=(w_in, m_w_in, v_w_in), w_branch_a=(w_branch_a, m_w_branch_a, v_w_branch_a),
              w_branch_b=(w_branch_b, m_w_branch_b, v_w_branch_b), w_out=(w_out, m_w_out, v_w_out),
              w_ffn_in=(w_ffn_in, m_w_ffn_in, v_w_ffn_in), w_ffn_out=(w_ffn_out, m_w_ffn_out, v_w_ffn_out))
    res = {}

    def finish_group(gi, after):
        names, handle = in_flight[gi]
        sends, zones = _exchange_wait(handle, after, "scatter_wait_%d" % gi)
        for nm, zone, sent in zip(names, zones, sends):
            w, m, v = (tr(a) if nm in transposed else a[0] for a in ws[nm])
            out = _adamw(zone, w, m, v, "adamw_" + nm, mine=sent, me=me.reshape(1).astype(jnp.int32))
            after = out[0]
            res[nm] = [jnp.transpose(o) for o in out] if nm in transposed else out
        return after

    small_h = _exchange_start([_pack_small(small)], False, "gather_small_start", after=grad_x)
    done = finish_group(1, finish_group(0, small_h["token"]))
    _, (slab_all,) = _exchange_wait(small_h, done, "gather_small_wait")
    small_w = dict(b_ada=b_ada, g_pre_mix=g_pre_mix, g_post_mix=g_post_mix, g_pre_ffn=g_pre_ffn, g_post_ffn=g_post_ffn,
                   b_f=b_f, sinks=sinks, loss=jnp.zeros((1,), f32))
    small_m = dict(b_ada=m_b_ada, g_pre_mix=m_g_pre_mix, g_post_mix=m_g_post_mix, g_pre_ffn=m_g_pre_ffn,
                   g_post_ffn=m_g_post_ffn, b_f=m_b_f, sinks=m_sinks, loss=jnp.zeros((1,), f32))
    small_v = dict(b_ada=v_b_ada, g_pre_mix=v_g_pre_mix, g_post_mix=v_g_post_mix, g_pre_ffn=v_g_pre_ffn,
                   g_post_ffn=v_g_post_ffn, b_f=v_b_f, sinks=v_sinks, loss=jnp.ones((1,), f32))
    shapes = {k: small_w[k].shape for k, _ in _SMALL}
    s_out = _adamw(slab_all, _pack_small(small_w), _pack_small(small_m), _pack_small(small_v), "adamw_small")
    s_grad, s_delta, s_m, s_v = (_unpack_small(o, shapes) for o in s_out)

    d_ada_all = lax.dynamic_slice(slab_all[:, :6144 // LANES, :].reshape(N_DEV, 6144), (0, me * ada_w), (N_DEV, ada_w))
    ada_parts = _ada_wgrad(c_all, d_ada_all, "ada_wgrad")

    res["w_ada"] = _adamw(ada_parts, w_ada[0], m_w_ada[0], v_w_ada[0], "adamw_w_ada")
    finish_group(2, res["w_ada"][0])

    order = ["w_ada", "b_ada", "g_pre_mix", "g_post_mix", "w_in", "b_f", "sinks", "w_branch_a", "w_branch_b", "w_out",
             "g_pre_ffn", "g_post_ffn", "w_ffn_in", "w_ffn_out"]
    outs = [s_grad["loss"].reshape(()), grad_x[None]]
    for which, small_o in enumerate((s_grad, s_delta, s_m, s_v)):
        for nm in order:
            outs.append(res[nm][which][None] if nm in res else small_o[nm])
    return tuple(outs)
```

```python
import math

import jax
import jax.numpy as jnp
from jax import lax
from jax.experimental import pallas as pl
from jax.experimental.pallas import tpu as pltpu

f32 = jnp.float32
bf16 = jnp.bfloat16

D_MODEL = 1024
HEAD_DIM = 64
N_HEADS = 8
N_PAIRS = 4
QKV_W = 2304
F_OFF = 2304
WINDOW = 128
ROPE_THETA = 10000.0
RMS_EPS = 1e-6
N_DEV = 8
ADAM_LR, ADAM_B1, ADAM_B2, ADAM_EPS, ADAM_WD, ADAM_STEP = 0.001, 0.9, 0.999, 1e-08, 0.01, 10
NEG = -1e30
L_ROW = (HEAD_DIM, 0)
LANES = 128
VMEM_LIMIT = 48 * 1024 * 1024
MESH = pl.DeviceIdType.MESH

_NT = (((1,), (1,)), ((), ()))
_TN = (((0,), (0,)), ((), ()))


def _params(n_grid=0):
    sem = ("arbitrary",) * n_grid if n_grid else None
    return pltpu.CompilerParams(dimension_semantics=sem, vmem_limit_bytes=VMEM_LIMIT)


def _row_tile(s, want):
    t = min(s, want)
    assert s % t == 0, (s, t)
    return t


MATMUL_VMEM_BUDGET = 40 * 1024 * 1024


def _matmul_tiles(m, n, k, a_item, b_item, o_item):
    def tiles(d):
        return [t for t in range(LANES, min(d, 2048) + 1, LANES) if d % t == 0] or [d]

    best = None
    for tm in tiles(m):
        for tn in tiles(n):
            vmem = 2 * (tm * k * a_item + tn * k * b_item + tm * tn * o_item) + tm * tn * 4
            if vmem > MATMUL_VMEM_BUDGET:
                continue
            traffic = m * k * a_item + n * k * b_item * (1 if tn == n else m // tm) + m * n * o_item
            steps = (m // tm) * (n // tn)
            key = (traffic, 0, steps) if steps >= 4 else (traffic, 1, -steps)
            if best is None or key < best[0]:
                best = (key, tm, tn)
    assert best is not None, (m, n, k)
    return best[1], best[2]


def _matmul(a, b, mode, out_dtype, name, after=None):
    if mode == "nn":
        (m, k), n = a.shape, b.shape[1]
    elif mode == "nt":
        (m, k), n = a.shape, b.shape[0]
    else:
        (k, m), n = a.shape, b.shape[1]
    tm, tn = _matmul_tiles(m, n, k, a.dtype.itemsize, b.dtype.itemsize, jnp.dtype(out_dtype).itemsize)
    if mode == "nn":
        a_spec, b_spec, dims = pl.BlockSpec((tm, k), lambda i, j: (i, 0)), pl.BlockSpec((k, tn), lambda i, j: (0, j)), None
    elif mode == "nt":
        a_spec, b_spec, dims = pl.BlockSpec((tm, k), lambda i, j: (i, 0)), pl.BlockSpec((tn, k), lambda i, j: (j, 0)), _NT
    else:
        a_spec, b_spec, dims = pl.BlockSpec((k, tm), lambda i, j: (0, i)), pl.BlockSpec((k, tn), lambda i, j: (0, j)), _TN

    def body(a_ref, b_ref, *rest):
        o_ref = rest[-1]
        av, bv = a_ref[...].astype(bf16), b_ref[...].astype(bf16)
        if dims is None:
            r = jnp.dot(av, bv, preferred_element_type=f32)
        else:
            r = lax.dot_general(av, bv, dims, preferred_element_type=f32)
        o_ref[...] = r.astype(out_dtype)

    extra = [] if after is None else [after]
    return pl.pallas_call(
        body, name=name, grid=(m // tm, n // tn), in_specs=[a_spec, b_spec] + [pl.BlockSpec(memory_space=pl.ANY)] * len(extra),
        out_specs=pl.BlockSpec((tm, tn), lambda i, j: (i, j)),
        out_shape=jax.ShapeDtypeStruct((m, n), out_dtype), compiler_params=_params(2),
    )(a, b, *extra)


def _rstd(v):
    return lax.rsqrt(jnp.mean(v * v, axis=-1, keepdims=True) + RMS_EPS)


def _row_spec(tm, d):
    return pl.BlockSpec((tm, d), lambda i: (i, 0))


def _vec_spec(d, rows=1):
    return pl.BlockSpec((rows, d), lambda i: (0, 0))


def _proj_spec(a, w, tm):
    return [_row_spec(tm, a.shape[1]), pl.BlockSpec(w.shape, lambda i: (0, 0))]


def _out_proj_postnorm_prenorm(a, w, x, g_post, gate, g_pre, scale, shift, name):
    s, d = x.shape
    tm = _row_tile(s, 512)

    def body(a_ref, w_ref, x_ref, gp_ref, gate_ref, g_ref, sc_ref, sh_ref, y_ref, x2_ref, h_ref):
        yv = jnp.dot(a_ref[...], w_ref[...], preferred_element_type=f32)
        y_ref[...] = yv
        x2 = x_ref[...] + gate_ref[...] * (yv * _rstd(yv) * gp_ref[...])
        x2_ref[...] = x2
        h_ref[...] = ((x2 * _rstd(x2) * g_ref[...]) * (1.0 + sc_ref[...]) + sh_ref[...]).astype(bf16)

    return pl.pallas_call(
        body, name=name, grid=(s // tm,), in_specs=_proj_spec(a, w, tm) + [_row_spec(tm, d)] + [_vec_spec(d)] * 5,
        out_specs=[_row_spec(tm, d)] * 3,
        out_shape=[jax.ShapeDtypeStruct((s, d), f32)] * 2 + [jax.ShapeDtypeStruct((s, d), bf16)], compiler_params=_params(1),
    )(a, w, x, g_post, gate, g_pre, scale, shift)


def _rms_bwd(u, v, r):
    return r * u - v * (r * r * r) * jnp.mean(u * v, axis=-1, keepdims=True)


def _out_proj_loss_tail(a, w, x, g, gate, target, name):
    s, d = x.shape
    tm = _row_tile(s, 512)

    def body(a_ref, w_ref, x_ref, g_ref, gate_ref, t_ref, loss_ref, do_ref, dy_ref, vec_ref):
        @pl.when(pl.program_id(0) == 0)
        def _():
            loss_ref[...] = jnp.zeros_like(loss_ref)
            vec_ref[...] = jnp.zeros_like(vec_ref)
        yv = jnp.dot(a_ref[...], w_ref[...], preferred_element_type=f32)
        r = _rstd(yv)
        yn = yv * r
        err = x_ref[...] + gate_ref[...] * (yn * g_ref[...]) - t_ref[...]
        loss_ref[...] += 0.5 * jnp.sum(jnp.mean(err * err, axis=-1, keepdims=True), axis=0, keepdims=True)
        dr = err / d
        do_ref[...] = dr
        dn = dr * gate_ref[...]
        vec_ref[0:1, :] += jnp.sum(dr * (yn * g_ref[...]), axis=0, keepdims=True)
        vec_ref[1:2, :] += jnp.sum(dn * yn, axis=0, keepdims=True)
        dy_ref[...] = _rms_bwd(dn * g_ref[...], yv, r).astype(bf16)

    return pl.pallas_call(
        body, name=name, grid=(s // tm,),
        in_specs=_proj_spec(a, w, tm) + [_row_spec(tm, d)] + [_vec_spec(d)] * 2 + [_row_spec(tm, d)],
        out_specs=[_vec_spec(LANES), _row_spec(tm, d), _row_spec(tm, d), _vec_spec(d, 8)],
        out_shape=[jax.ShapeDtypeStruct((1, LANES), f32), jax.ShapeDtypeStruct((s, d), f32),
                   jax.ShapeDtypeStruct((s, d), bf16), jax.ShapeDtypeStruct((8, d), f32)],
        compiler_params=_params(1),
    )(a, w, x, g, gate, target)


def _dgrad_prenorm_bwd(terms, x, g, scale, dres, name, after=None, below=None):
    s, d = x.shape
    n = len(terms)
    k = sum(a.shape[1] for a, _, _ in terms)
    row_bytes = 2 * (2 * k) + d * (4 + 2 * 4 * 3 + (2 * 4 + 2 * 2 if below else 0))
    tm = next(t for t in (512, 256, 128) if s % t == 0 and 4 * k * d + t * row_bytes <= MATMUL_VMEM_BUDGET)
    extra = [] if after is None else [after]

    def body(*refs):
        a_refs, b_refs = refs[:n], refs[n:2 * n]
        x_ref, g_ref, sc_ref, dr_ref = refs[2 * n:2 * n + 4]
        n_in = 2 * n + 4 + (3 if below else 0) + len(extra)
        dx_ref, vec_ref = refs[n_in], refs[n_in + 1]
        if below:
            y_ref, gp_ref, gate_ref = refs[2 * n + 4:2 * n + 7]
            dy_ref, vec2_ref = refs[n_in + 2], refs[n_in + 3]

        @pl.when(pl.program_id(0) == 0)
        def _():
            vec_ref[...] = jnp.zeros_like(vec_ref)
            if below:
                vec2_ref[...] = jnp.zeros_like(vec2_ref)
        dhv = jnp.dot(a_refs[0][...], b_refs[0][...], preferred_element_type=f32)
        for i in range(1, n):
            dhv = dhv + jnp.dot(a_refs[i][...], b_refs[i][...], preferred_element_type=f32)
        xv = x_ref[...]
        r = _rstd(xv)
        xn = xv * r
        dn = dhv * (1.0 + sc_ref[...])
        vec_ref[0:1, :] += jnp.sum(dhv, axis=0, keepdims=True)
        vec_ref[1:2, :] += jnp.sum(dhv * (xn * g_ref[...]), axis=0, keepdims=True)
        vec_ref[2:3, :] += jnp.sum(dn * xn, axis=0, keepdims=True)
        dx = dr_ref[...] + _rms_bwd(dn * g_ref[...], xv, r)
        dx_ref[...] = dx
        if below:
            yv = y_ref[...]
            ry = _rstd(yv)
            yn = yv * ry
            dny = dx * gate_ref[...]
            vec2_ref[0:1, :] += jnp.sum(dx * (yn * gp_ref[...]), axis=0, keepdims=True)
            vec2_ref[1:2, :] += jnp.sum(dny * yn, axis=0, keepdims=True)
            dy_ref[...] = _rms_bwd(dny * gp_ref[...], yv, ry).astype(bf16)

    in_specs = ([_row_spec(tm, a.shape[1]) for a, _, _ in terms]
                + [pl.BlockSpec((a.shape[1], d), lambda i, r=r: (r, 0)) for a, _, r in terms]
                + [_row_spec(tm, d)] + [_vec_spec(d)] * 2 + [_row_spec(tm, d)])
    out_specs = [_row_spec(tm, d), _vec_spec(d, 8)]
    out_shape = [jax.ShapeDtypeStruct((s, d), f32), jax.ShapeDtypeStruct((8, d), f32)]
    args = [a for a, _, _ in terms] + [b for _, b, _ in terms] + [x, g, scale, dres]
    if below:
        in_specs += [_row_spec(tm, d)] + [_vec_spec(d)] * 2
        out_specs += [_row_spec(tm, d), _vec_spec(d, 8)]
        out_shape += [jax.ShapeDtypeStruct((s, d), bf16), jax.ShapeDtypeStruct((8, d), f32)]
        args += list(below)
    return pl.pallas_call(
        body, name=name, grid=(s // tm,), in_specs=in_specs + [pl.BlockSpec(memory_space=pl.ANY)] * len(extra),
        out_specs=out_specs, out_shape=out_shape, compiler_params=_params(1),
    )(*args, *extra)


def _lane():
    return lax.broadcasted_iota(jnp.int32, (1, LANES), 1)


def _rope_tables(pos_col, inv_freq, name):
    s = pos_col.shape[0]

    def body(p_ref, f_ref, cos_ref, sin_ref):
        ang = p_ref[...].astype(f32) * f_ref[...]
        first_half = (_lane() % HEAD_DIM) < HEAD_DIM // 2
        cos_ref[...] = jnp.cos(ang)
        sn = jnp.sin(ang)
        sin_ref[...] = jnp.where(first_half, -sn, sn)

    return pl.pallas_call(
        body, name=name, out_shape=[jax.ShapeDtypeStruct((s, LANES), f32)] * 2, compiler_params=_params(),
    )(pos_col, inv_freq)


def _swap_halves(v):
    first_half = (_lane() % HEAD_DIM) < HEAD_DIM // 2
    return jnp.where(first_half, pltpu.roll(v, LANES - HEAD_DIM // 2, axis=1), pltpu.roll(v, HEAD_DIM // 2, axis=1))


def _prenorm_proj_qkv(x, g, mod_scale, mod_shift, w_qkv_t, cos, sin_s, name):
    s, d = x.shape
    tm = _row_tile(s, 512)
    scale = 1.0 / math.sqrt(HEAD_DIM)

    def body(x_ref, g_ref, msc_ref, msh_ref, w_ref, c_ref, s_ref, h_ref, qa_ref, ka_ref, va_ref, qb_ref, kb_ref, vb_ref):
        xv = x_ref[...]
        h = ((xv * _rstd(xv) * g_ref[...]) * (1.0 + msc_ref[...]) + msh_ref[...]).astype(bf16)
        h_ref[...] = h
        proj = lax.dot_general(h, w_ref[...], _NT, preferred_element_type=f32)
        cs, sn = c_ref[...], s_ref[...]
        low = _lane() < HEAD_DIM

        def blk(j):
            return proj[:, j * LANES:(j + 1) * LANES]

        def rope(v):
            return v * cs + _swap_halves(v) * sn

        def expand(v):
            other = pltpu.roll(v, HEAD_DIM, axis=1)
            return jnp.where(low, v, other), jnp.where(low, other, v)

        for j in range(N_PAIRS):
            qa_ref[:, j * LANES:(j + 1) * LANES] = (rope(blk(j)) * scale).astype(bf16)
            qb_ref[:, j * LANES:(j + 1) * LANES] = (blk(6 + j) * scale).astype(bf16)
            kb_ref[:, j * LANES:(j + 1) * LANES] = blk(10 + j).astype(bf16)
            vb_ref[:, j * LANES:(j + 1) * LANES] = blk(14 + j).astype(bf16)
        k0, k1 = expand(rope(blk(4)))
        v0, v1 = expand(blk(5))
        for j in range(N_PAIRS):
            ka_ref[:, j * LANES:(j + 1) * LANES] = (k0 if j < 2 else k1).astype(bf16)
            va_ref[:, j * LANES:(j + 1) * LANES] = (v0 if j < 2 else v1).astype(bf16)

    hw = N_PAIRS * LANES
    return pl.pallas_call(
        body, name=name, grid=(s // tm,),
        in_specs=[_row_spec(tm, d)] + [_vec_spec(d)] * 3
        + [pl.BlockSpec((QKV_W, d), lambda i: (0, 0)), _row_spec(tm, LANES), _row_spec(tm, LANES)],
        out_specs=[_row_spec(tm, d)] + [_row_spec(tm, hw)] * 6,
        out_shape=[jax.ShapeDtypeStruct((s, d), bf16)] + [jax.ShapeDtypeStruct((s, hw), bf16)] * 6, compiler_params=_params(1),
    )(x, g, mod_scale, mod_shift, w_qkv_t, cos, sin_s)


def _qkv_prep_bwd(dqa_t, dka, dva, dqb_t, dkb, dvb, cos, sin_s, name):
    s = dka.shape[0]
    tm = _row_tile(s, 512)
    scale = 1.0 / math.sqrt(HEAD_DIM)
    hw = N_PAIRS * LANES
    t_spec = pl.BlockSpec((hw, tm), lambda i: (0, i))

    def body(dqa_ref, dka_ref, dva_ref, dqb_ref, dkb_ref, dvb_ref, c_ref, s_ref, o_ref):
        cs, sn = c_ref[...], s_ref[...]
        low = _lane() < HEAD_DIM

        def blk(ref, j):
            return ref[:, j * LANES:(j + 1) * LANES].astype(f32)

        def blk_t(ref, j):
            return ref[j * LANES:(j + 1) * LANES, :].T

        def unrope(v):
            return v * cs + _swap_halves(v * sn)

        def fold(ref):
            a, b = blk(ref, 0) + blk(ref, 1), blk(ref, 2) + blk(ref, 3)
            kv0 = a + pltpu.roll(a, HEAD_DIM, axis=1)
            kv1 = b + pltpu.roll(b, HEAD_DIM, axis=1)
            return jnp.where(low, kv0, kv1)

        for j in range(N_PAIRS):
            o_ref[:, j * LANES:(j + 1) * LANES] = (unrope(blk_t(dqa_ref, j)) * scale).astype(bf16)
            o_ref[:, (6 + j) * LANES:(7 + j) * LANES] = (blk_t(dqb_ref, j) * scale).astype(bf16)
            o_ref[:, (10 + j) * LANES:(11 + j) * LANES] = blk(dkb_ref, j).astype(bf16)
            o_ref[:, (14 + j) * LANES:(15 + j) * LANES] = blk(dvb_ref, j).astype(bf16)
        o_ref[:, 4 * LANES:5 * LANES] = unrope(fold(dka_ref)).astype(bf16)
        o_ref[:, 5 * LANES:6 * LANES] = fold(dva_ref).astype(bf16)

    return pl.pallas_call(
        body, name=name, grid=(s // tm,),
        in_specs=[t_spec, _row_spec(tm, hw), _row_spec(tm, hw), t_spec, _row_spec(tm, hw), _row_spec(tm, hw)] + [_row_spec(tm, LANES)] * 2,
        out_specs=_row_spec(tm, QKV_W), out_shape=jax.ShapeDtypeStruct((s, QKV_W), bf16), compiler_params=_params(1),
    )(dqa_t, dka, dva, dqb_t, dkb, dvb, cos, sin_s)


def _cumsum_rows(v, reverse=False):
    n = v.shape[0]
    row = lax.broadcasted_iota(jnp.int32, v.shape, 0)
    sh = 1
    while sh < n:
        if reverse:
            v = v + jnp.where(row < n - sh, pltpu.roll(v, n - sh, axis=0), 0.0)
        else:
            v = v + jnp.where(row >= sh, pltpu.roll(v, sh, axis=0), 0.0)
        sh *= 2
    return v


def _log_sigmoid(z):
    return jnp.minimum(z, 0.0) - jnp.log1p(jnp.exp(-jnp.abs(z)))


def _forget_prep(h, w_f_t, bf_row, name):
    s, d = h.shape
    tm = _row_tile(s, 1024)

    def body(h_ref, w_ref, b_ref, f_ref, cb_ref, last_ref):
        @pl.when(pl.program_id(0) == 0)
        def _():
            last_ref[...] = jnp.zeros_like(last_ref)
        fl = lax.dot_general(h_ref[...], w_ref[...], _NT, preferred_element_type=f32)
        f_ref[...] = fl
        cum = _cumsum_rows(_log_sigmoid(fl + b_ref[...])) + last_ref[0:1, :]
        last_ref[0:1, :] = cum[tm - 1:tm, :]
        for hd in range(N_HEADS):
            cb_ref[:, hd * LANES:(hd + 1) * LANES] = jnp.broadcast_to(cum[:, hd:hd + 1], (tm, LANES))

    return pl.pallas_call(
        body, name=name, grid=(s // tm,),
        in_specs=[_row_spec(tm, d), pl.BlockSpec((LANES, d), lambda i: (0, 0)), _vec_spec(LANES)],
        out_specs=[_row_spec(tm, LANES), _row_spec(tm, N_HEADS * LANES)],
        out_shape=[jax.ShapeDtypeStruct((s, LANES), f32), jax.ShapeDtypeStruct((s, N_HEADS * LANES), f32)],
        scratch_shapes=[pltpu.VMEM((8, LANES), f32)], compiler_params=_params(1),
    )(h, w_f_t, bf_row)


def _forget_prep_bwd(rs, dcs, fl, bf_row, name):
    s = fl.shape[0]
    tm = _row_tile(s, 1024)
    n = s // tm

    def body(r_ref, c_ref, f_ref, b_ref, df_ref, db_ref, next_ref):
        @pl.when(pl.program_id(0) == 0)
        def _():
            next_ref[...] = jnp.zeros_like(next_ref)
            db_ref[...] = jnp.zeros_like(db_ref)
        eye = (lax.broadcasted_iota(jnp.int32, (N_HEADS, LANES), 0) == lax.broadcasted_iota(jnp.int32, (N_HEADS, LANES), 1)).astype(f32)
        dcum = lax.dot_general(r_ref[...], eye, _TN, precision=lax.Precision.HIGHEST, preferred_element_type=f32)
        for h in range(N_HEADS):
            dcum = dcum - jnp.where(_lane() == h, jnp.sum(c_ref[:, h * LANES:(h + 1) * LANES], axis=1, keepdims=True), 0.0)
        dlf = _cumsum_rows(dcum, reverse=True) + next_ref[0:1, :]
        next_ref[0:1, :] = dlf[0:1, :]
        z = f_ref[...] + b_ref[...]
        df = jnp.where(_lane() < N_HEADS, dlf * jax.nn.sigmoid(-z), 0.0)
        df_ref[...] = df.astype(bf16)
        db_ref[0:1, :] += jnp.sum(df, axis=0, keepdims=True)

    def rows(width):
        return pl.BlockSpec((tm, width), lambda i: (n - 1 - i, 0))

    return pl.pallas_call(
        body, name=name, grid=(n,),
        in_specs=[pl.BlockSpec((N_HEADS, tm), lambda i: (0, n - 1 - i)), rows(N_HEADS * LANES), rows(LANES), _vec_spec(LANES)],
        out_specs=[rows(LANES), _vec_spec(LANES, 8)],
        out_shape=[jax.ShapeDtypeStruct((s, LANES), bf16), jax.ShapeDtypeStruct((8, LANES), f32)],
        scratch_shapes=[pltpu.VMEM((8, LANES), f32)], compiler_params=_params(1),
    )(rs, dcs, fl, bf_row)


def _tile_mask(n_keys, n_queries, off, window):
    shape = (n_keys, n_queries)
    d = lax.broadcasted_iota(jnp.int32, shape, 1) - lax.broadcasted_iota(jnp.int32, shape, 0) + off
    valid = d >= 0
    return jnp.logical_and(valid, d < window) if window else valid


def _wide(v, t):
    return jnp.concatenate([v] * (t // LANES), axis=1)


def _attn_fwd(q, k, v, name, *, cum_b=None, sink_rows=None, window=None, t=256):
    s = q.shape[0]
    t = _row_tile(s, t)
    fox, has_sink = cum_b is not None, sink_rows is not None
    assert not window or (window % LANES == 0 and LANES + window <= s)

    def body(*refs):
        q_ref, k_ref, v_ref = refs[:3]
        rest = list(refs[3:])
        cb_ref = rest.pop(0) if fox else None
        sink_ref = rest.pop(0) if has_sink else None
        o_ref, lse_ref = rest
        i = pl.program_id(1)
        low = _lane() < HEAD_DIM
        top = lax.broadcasted_iota(jnp.int32, (LANES, 1), 0) < HEAD_DIM
        q2 = q_ref[...]
        zero = jnp.zeros_like(q2)
        qms = (jnp.where(low, q2, zero), jnp.where(low, zero, q2))

        def tile(k0, n_keys, off, carry, masked, queries=slice(0, t)):
            nq = queries.stop - queries.start
            kblk, vblk = k_ref[pl.ds(k0, n_keys), :], v_ref[pl.ds(k0, n_keys), :]
            valid = _tile_mask(n_keys, nq, off, window) if masked else None
            ones = jnp.ones_like(vblk)
            vs = tuple(jnp.where(_lane() == L_ROW[h], ones, vblk) for h in range(2))

            def scores(h):
                return lax.dot_general(kblk, qms[h][queries], _NT, preferred_element_type=f32)

            def softmax(h, sc):
                m = carry[h][0]
                if fox:
                    sc = sc - _wide(cb_ref[pl.ds(k0, n_keys), h * LANES:(h + 1) * LANES], nq)
                if masked:
                    sc = jnp.where(valid, sc, NEG)
                m_new = jnp.maximum(m, jnp.max(sc, axis=0, keepdims=True))
                return m_new, jnp.exp(m - m_new), jnp.exp(sc - m_new).astype(bf16)

            def update(h, m_new, alpha, p):
                return m_new, alpha * carry[h][1] + lax.dot_general(vs[h], p, _TN, preferred_element_type=f32)

            if window:
                return tuple(update(h, *softmax(h, scores(h))) for h in range(2))
            scs = [scores(h) for h in range(2)]
            stats = [softmax(h, scs[h]) for h in range(2)]
            return tuple(update(h, *stats[h]) for h in range(2))

        def start(nq):
            if has_sink:
                row = lax.broadcasted_iota(jnp.int32, (LANES, nq), 0)
                return tuple((_wide(sink_ref[h:h + 1, :], nq), (row == L_ROW[h]).astype(f32)) for h in range(2))
            return tuple((jnp.full((1, nq), NEG, f32), jnp.zeros((LANES, nq), f32)) for h in range(2))

        def finish(carry, queries):
            (m0, a0), (m1, a1) = carry
            l0, l1 = a0[L_ROW[0]:L_ROW[0] + 1, :], a1[L_ROW[1]:L_ROW[1] + 1, :]
            o_t = jnp.where(top, a0 * (1.0 / l0), a1 * (1.0 / l1))
            o_ref[queries, :] = o_t.T.astype(bf16)
            lse_ref[0:1, queries] = m0 + jnp.log(l0)
            lse_ref[1:2, queries] = m1 + jnp.log(l1)

        if window:
            for c in range(t // LANES):
                queries = slice(c * LANES, (c + 1) * LANES)
                q0 = i * t + c * LANES
                k0 = pl.multiple_of(jnp.maximum(q0 - window, 0), LANES)
                finish(tile(k0, LANES + window, q0 - k0, start(LANES), True, queries), queries)
        else:
            carry = lax.fori_loop(0, i, lambda kb, c: tile(pl.multiple_of(kb * t, t), t, 0, c, False), start(t))
            half, k_own = t // 2, pl.multiple_of(i * t, t)
            carry = tile(k_own, half, 0, carry, True)
            finish(tuple((m[:, :half], a[:, :half]) for m, a in carry), slice(0, half))
            carry = tuple((m[:, half:], a[:, half:]) for m, a in carry)
            finish(tile(pl.multiple_of(k_own + half, half), half, 0, carry, True, slice(half, t)), slice(half, t))

    q_spec = pl.BlockSpec((t, LANES), lambda j, i: (i, j))
    kv_spec = pl.BlockSpec((s, LANES), lambda j, i: (0, j))
    in_specs, args = [q_spec, kv_spec, kv_spec], [q, k, v]
    if fox:
        in_specs += [pl.BlockSpec((s, 2 * LANES), lambda j, i: (0, j))]
        args += [cum_b]
    if has_sink:
        in_specs += [pl.BlockSpec((None, 2, LANES), lambda j, i: (j, 0, 0))]
        args += [sink_rows.reshape(N_PAIRS, 2, LANES)]
    return pl.pallas_call(
        body, name=name, grid=(N_PAIRS, s // t), in_specs=in_specs,
        out_specs=[q_spec, pl.BlockSpec((None, 2, t), lambda j, i: (j, 0, i))],
        out_shape=[jax.ShapeDtypeStruct((s, N_PAIRS * LANES), bf16), jax.ShapeDtypeStruct((N_PAIRS, 2, s), f32)],
        compiler_params=_params(2),
    )(*args)


def _branch_dgrad_delta(db, w, o, name, *, lse=None, sink_rows=None, after=None):
    s, hw = o.shape
    tm = _row_tile(s, 1024)
    has_sink = sink_rows is not None
    extra = [] if after is None else [after]

    def body(*refs):
        db_ref, w_ref, o_ref = refs[:3]
        outs = refs[3 + (2 if has_sink else 0) + len(extra):]
        do_ref, dl_ref = outs[:2]
        if has_sink:
            lse_ref, sink_ref = refs[3:5]
            ds_ref = outs[2]

            @pl.when(pl.program_id(0) == 0)
            def _():
                ds_ref[...] = jnp.zeros_like(ds_ref)
        do = lax.dot_general(db_ref[...], w_ref[...], _NT, preferred_element_type=f32).astype(bf16)
        do_ref[...] = do
        for j in range(N_PAIRS):
            cols = slice(j * LANES, (j + 1) * LANES)
            prod_t = (do[:, cols].astype(f32) * o_ref[:, cols].astype(f32)).T
            for h in range(2):
                dl = jnp.sum(prod_t[h * HEAD_DIM:(h + 1) * HEAD_DIM, :], axis=0, keepdims=True)
                dl_ref[j, h:h + 1, :] = dl
                if has_sink:
                    r = 2 * j + h
                    p_sink = jnp.exp(sink_ref[r:r + 1, 0:1] - lse_ref[j, h:h + 1, :])
                    ds_ref[r:r + 1, :] += -jnp.sum(p_sink * dl, axis=1, keepdims=True)

    rows_spec = pl.BlockSpec((N_PAIRS, 2, tm), lambda i: (0, 0, i))
    in_specs = [_row_spec(tm, db.shape[1]), pl.BlockSpec(w.shape, lambda i: (0, 0)), _row_spec(tm, hw)]
    args = [db, w, o]
    out_specs = [_row_spec(tm, hw), rows_spec]
    out_shape = [jax.ShapeDtypeStruct((s, hw), bf16), jax.ShapeDtypeStruct((N_PAIRS, 2, s), f32)]
    if has_sink:
        in_specs += [rows_spec, _vec_spec(LANES, N_HEADS)]
        args += [lse, sink_rows]
        out_specs += [_vec_spec(LANES, N_HEADS)]
        out_shape += [jax.ShapeDtypeStruct((N_HEADS, LANES), f32)]
    return pl.pallas_call(
        body, name=name, grid=(s // tm,), in_specs=in_specs + [pl.BlockSpec(memory_space=pl.ANY)] * len(extra),
        out_specs=out_specs, out_shape=out_shape, compiler_params=_params(1),
    )(*args, *extra)


def _attn_bwd(q, k, v, do, lse, delta, name, *, cum_b=None, window=None, t=256):
    s = q.shape[0]
    t = _row_tile(s, t)
    nblk = s // t
    fox = cum_b is not None
    assert not window or (window % LANES == 0 and LANES + window <= s)

    def body(*refs):
        k_ref, v_ref, q_ref, do_ref, lse_ref, dl_ref = refs[:6]
        rest = list(refs[6:])
        cb_ref = rest.pop(0) if fox else None
        dq_ref, dk_ref, dv_ref = rest[:3]
        dcs_ref, rs_ref = (rest[3], rest[4]) if fox else (None, None)
        dk_acc, dv_acc = rest[-2:]
        b = pl.program_id(1)
        k0 = pl.multiple_of(b * t, t)

        @pl.when(b == 0)
        def _():
            dq_ref[...] = jnp.zeros_like(dq_ref)
            if fox:
                rs_ref[...] = jnp.zeros_like(rs_ref)

        dk_acc[...] = jnp.zeros_like(dk_acc)
        dv_acc[...] = jnp.zeros_like(dv_acc)
        if fox:
            dcs_ref[...] = jnp.zeros_like(dcs_ref)
        low = _lane() < HEAD_DIM
        top = lax.broadcasted_iota(jnp.int32, (LANES, 1), 0) < HEAD_DIM
        kblk, vblk = k_ref[...], v_ref[...]
        k_t = kblk.astype(f32).T.astype(bf16)
        cks = [_wide(cb_ref[pl.ds(k0, t), h * LANES:(h + 1) * LANES], t) for h in range(2)] if fox else None

        def tile(q0, n_queries, off, masked, keys=slice(0, t)):
            cols = pl.ds(q0, n_queries)
            q2, do2 = q_ref[cols, :], do_ref[cols, :]
            zero = jnp.zeros_like(q2)
            valid = _tile_mask(keys.stop - keys.start, n_queries, off, window) if masked else None
            dq_parts = []
            for h in range(2):
                qm = jnp.where(low, q2, zero) if h == 0 else jnp.where(low, zero, q2)
                dom = jnp.where(low, do2, zero) if h == 0 else jnp.where(low, zero, do2)
                sc = lax.dot_general(kblk[keys], qm, _NT, preferred_element_type=f32)
                if fox:
                    sc = sc - cks[h][keys, :n_queries]
                if masked:
                    sc = jnp.where(valid, sc, NEG)
                p = jnp.exp(sc - lse_ref[h:h + 1, cols])
                dp = lax.dot_general(vblk[keys], dom, _NT, preferred_element_type=f32)
                ds = p * (dp - dl_ref[h:h + 1, cols])
                pb, dsb = p.astype(bf16), ds.astype(bf16)
                dv_acc[keys, :] += jnp.dot(pb, dom, preferred_element_type=f32)
                dk_acc[keys, :] += jnp.dot(dsb, qm, preferred_element_type=f32)
                dq_parts.append(jnp.dot(k_t[:, keys], dsb, preferred_element_type=f32))
                if fox:
                    dcs_ref[keys, h * LANES:(h + 1) * LANES] += sum(ds[:, g * LANES:(g + 1) * LANES]
                                                                    for g in range(n_queries // LANES))
                    rs_ref[h:h + 1, cols] += jnp.sum(ds, axis=0, keepdims=True)
            dq_ref[:, cols] += jnp.where(top, dq_parts[0], dq_parts[1])

        def later_block(qb, carry):
            tile(pl.multiple_of(qb * t, t), t, 0, False)
            return carry

        if window:
            for c in range(t // LANES):
                first = b * t + c * LANES
                q0 = pl.multiple_of(jnp.minimum(first, s - (LANES + window)), LANES)
                tile(q0, LANES + window, q0 - first, True, slice(c * LANES, (c + 1) * LANES))
        else:
            half = t // 2
            tile(k0, half, 0, True, slice(0, half))
            tile(pl.multiple_of(k0 + half, half), half, half, True)
            lax.fori_loop(b + 1, nblk, later_block, 0)
        dk_ref[...] = dk_acc[...].astype(bf16)
        dv_ref[...] = dv_acc[...].astype(bf16)

    kv_spec = pl.BlockSpec((t, LANES), lambda j, b: (b, j))
    seq_spec = pl.BlockSpec((s, LANES), lambda j, b: (0, j))
    rows_spec = pl.BlockSpec((None, 2, s), lambda j, b: (j, 0, 0))
    hw = N_PAIRS * LANES
    in_specs, args = [kv_spec, kv_spec, seq_spec, seq_spec, rows_spec, rows_spec], [k, v, q, do, lse, delta]
    out_specs = [pl.BlockSpec((LANES, s), lambda j, b: (j, 0)), kv_spec, kv_spec]
    out_shape = [jax.ShapeDtypeStruct((hw, s), f32), jax.ShapeDtypeStruct((s, hw), bf16), jax.ShapeDtypeStruct((s, hw), bf16)]
    if fox:
        in_specs += [pl.BlockSpec((s, 2 * LANES), lambda j, b: (0, j))]
        args += [cum_b]
        out_specs += [pl.BlockSpec((t, 2 * LANES), lambda j, b: (b, j)), rows_spec]
        out_shape += [jax.ShapeDtypeStruct((s, N_HEADS * LANES), f32), jax.ShapeDtypeStruct((N_PAIRS, 2, s), f32)]
    return pl.pallas_call(
        body, name=name, grid=(N_PAIRS, nblk), in_specs=in_specs, out_specs=out_specs, out_shape=out_shape,
        scratch_shapes=[pltpu.VMEM((t, LANES), f32)] * 2, compiler_params=_params(2),
    )(*args)


def _branch_merge(o_a, o_b, w_a, w_b, gl, name):
    s, k = o_a.shape
    d = w_a.shape[1]
    tm = _row_tile(s, 1024)

    def body(oa_ref, ob_ref, wa_ref, wb_ref, g_ref, ba_ref, bb_ref, m_ref):
        ba = jnp.dot(oa_ref[...], wa_ref[...], preferred_element_type=f32)
        bb = jnp.dot(ob_ref[...], wb_ref[...], preferred_element_type=f32)
        g0, g1 = jax.nn.sigmoid(g_ref[:, :d].astype(f32)), jax.nn.sigmoid(g_ref[:, d:].astype(f32))
        ba_ref[...] = ba.astype(bf16)
        bb_ref[...] = bb.astype(bf16)
        m_ref[...] = (g0 * ba + g1 * bb).astype(bf16)

    whole = pl.BlockSpec((k, d), lambda i: (0, 0))
    return pl.pallas_call(
        body, name=name, grid=(s // tm,),
        in_specs=[_row_spec(tm, k), _row_spec(tm, k), whole, whole, _row_spec(tm, 2 * d)],
        out_specs=[_row_spec(tm, d)] * 3, out_shape=[jax.ShapeDtypeStruct((s, d), bf16)] * 3, compiler_params=_params(1),
    )(o_a, o_b, w_a, w_b, gl)


def _out_dgrad_merge_bwd(dy, w_out, ba, bb, gl, name):
    s, d = ba.shape
    tm = _row_tile(s, 512)

    def body(dy_ref, w_ref, a_ref, b_ref, g_ref, da_ref, db_ref, dg_ref):
        dmv = lax.dot_general(dy_ref[...], w_ref[...], _NT, preferred_element_type=f32)
        g0, g1 = jax.nn.sigmoid(g_ref[:, :d].astype(f32)), jax.nn.sigmoid(g_ref[:, d:].astype(f32))
        da_ref[...] = (dmv * g0).astype(bf16)
        db_ref[...] = (dmv * g1).astype(bf16)
        dg_ref[:, :d] = (dmv * a_ref[...].astype(f32) * (g0 * (1.0 - g0))).astype(bf16)
        dg_ref[:, d:] = (dmv * b_ref[...].astype(f32) * (g1 * (1.0 - g1))).astype(bf16)

    return pl.pallas_call(
        body, name=name, grid=(s // tm,),
        in_specs=[_row_spec(tm, dy.shape[1]), pl.BlockSpec(w_out.shape, lambda i: (0, 0))] + [_row_spec(tm, d)] * 2
        + [_row_spec(tm, 2 * d)],
        out_specs=[_row_spec(tm, d)] * 2 + [_row_spec(tm, 2 * d)],
        out_shape=[jax.ShapeDtypeStruct((s, d), bf16)] * 2 + [jax.ShapeDtypeStruct((s, 2 * d), bf16)],
        compiler_params=_params(1),
    )(dy, w_out, ba, bb, gl)


GLU_TILE = 256


def _ffn_in_swiglu(h, w_t, name):
    s, d = h.shape
    f = w_t.shape[0] // 2
    tm = _row_tile(s, 4096)
    tg = GLU_TILE
    nb = f // tg

    def body(h_ref, wg_ref, wu_ref, g_ref, u_ref, act_ref):
        hv = h_ref[...]
        g = lax.dot_general(hv, wg_ref[...], _NT, preferred_element_type=f32)
        u = lax.dot_general(hv, wu_ref[...], _NT, preferred_element_type=f32)
        g_ref[...] = g.astype(bf16)
        u_ref[...] = u.astype(bf16)
        act_ref[...] = (g * jax.nn.sigmoid(g) * u).astype(bf16)

    col = pl.BlockSpec((tm, tg), lambda i, j: (i, j))
    return pl.pallas_call(
        body, name=name, grid=(s // tm, nb),
        in_specs=[pl.BlockSpec((tm, d), lambda i, j: (i, 0)), pl.BlockSpec((tg, d), lambda i, j: (j, 0)),
                  pl.BlockSpec((tg, d), lambda i, j: (j + nb, 0))],
        out_specs=[col] * 3, out_shape=[jax.ShapeDtypeStruct((s, f), bf16)] * 3, compiler_params=_params(2),
    )(h, w_t, w_t)


def _ffn_out_dgrad_swiglu(dy, w_out, g, u, name):
    s, d = dy.shape
    f = g.shape[1]
    tm = _row_tile(s, 4096)
    tg = GLU_TILE

    def body(dy_ref, w_ref, g_ref, u_ref, dg_ref, du_ref):
        dv = lax.dot_general(dy_ref[...], w_ref[...], _NT, preferred_element_type=f32)
        gv, uv = g_ref[...].astype(f32), u_ref[...].astype(f32)
        sg = jax.nn.sigmoid(gv)
        dg_ref[...] = (dv * uv * (sg * (1.0 + gv * (1.0 - sg)))).astype(bf16)
        du_ref[...] = (dv * (gv * sg)).astype(bf16)

    col = pl.BlockSpec((tm, tg), lambda i, j: (i, j))
    return pl.pallas_call(
        body, name=name, grid=(s // tm, f // tg),
        in_specs=[pl.BlockSpec((tm, d), lambda i, j: (i, 0)), pl.BlockSpec((tg, d), lambda i, j: (j, 0)), col, col],
        out_specs=[col] * 2, out_shape=[jax.ShapeDtypeStruct((s, f), bf16)] * 2, compiler_params=_params(2),
    )(dy, w_out, g, u)


def _wgrad_stack(parts, h, name):
    s, m = parts[0].shape
    d = h.shape[1]
    tm = 256
    nb = m // tm
    n = len(parts)

    def body(*refs):
        i = pl.program_id(0)
        for p in range(n):
            @pl.when(i // nb == p)
            def _(p=p):
                refs[n + 1][...] = lax.dot_general(refs[p][...], refs[n][...], _TN, preferred_element_type=f32).astype(bf16)

    a_specs = [pl.BlockSpec((s, tm), lambda i, p=p: (0, jnp.clip(i - p * nb, 0, nb - 1))) for p in range(n)]
    return pl.pallas_call(
        body, name=name, grid=(n * nb,), in_specs=a_specs + [pl.BlockSpec((s, d), lambda i: (0, 0))],
        out_specs=pl.BlockSpec((tm, d), lambda i: (i, 0)),
        out_shape=jax.ShapeDtypeStruct((n * m, d), bf16), compiler_params=_params(1),
    )(*parts, h)


def _ada_wgrad(c_all, d_all, name):
    n, d = c_all.shape
    w = d_all.shape[1]

    def body(c_ref, d_ref, o_ref):
        eye = (lax.broadcasted_iota(jnp.int32, (n, n), 0) == lax.broadcasted_iota(jnp.int32, (n, n), 1)).astype(f32)
        ct = lax.dot_general(c_ref[...], eye, _TN, precision=lax.Precision.HIGHEST, preferred_element_type=f32)
        g = ct[:, 0:1] * d_ref[0:1, :]
        for bi in range(1, n):
            g = g + ct[:, bi:bi + 1] * d_ref[bi:bi + 1, :]
        o_ref[0] = g

    return pl.pallas_call(
        body, name=name, out_shape=jax.ShapeDtypeStruct((1, d, w), f32), compiler_params=_params(),
    )(c_all, d_all)


def _adamw(parts, w, m, v, name, mine=None, me=None):
    r, c = w.shape
    n_parts = parts.shape[0]
    row_tiles = [t for t in range(min(r, 256), 0, -1) if r % t == 0 and (t % 16 == 0 or t == r)]
    if row_tiles:
        tr, tc = row_tiles[0], c
    else:
        tr, tc = r, next(t for t in (256, LANES) if c % t == 0)

    def body(*refs):
        w_ref, m_ref, v_ref, g_ref, d_ref, nm_ref, nv_ref = refs[-7:]
        if mine is None:
            p_ref, = refs[:-7]
        else:
            me_ref, p_ref, own_ref = refs[:-7]

        def part(i):
            if mine is None:
                return p_ref[i].astype(f32)
            return jnp.where(me_ref[0] == i, own_ref[...], p_ref[i]).astype(f32)

        g = part(0)
        for i in range(1, n_parts):
            g = g + part(i)
        mm = ADAM_B1 * m_ref[...] + (1.0 - ADAM_B1) * g
        vv = ADAM_B2 * v_ref[...] + (1.0 - ADAM_B2) * (g * g)
        m_hat = mm / (1.0 - ADAM_B1 ** ADAM_STEP)
        v_hat = vv / (1.0 - ADAM_B2 ** ADAM_STEP)
        g_ref[...] = g
        d_ref[...] = -ADAM_LR * (m_hat / (jnp.sqrt(v_hat) + ADAM_EPS) + ADAM_WD * w_ref[...])
        nm_ref[...] = mm
        nv_ref[...] = vv

    out_shape = [jax.ShapeDtypeStruct((r, c), f32)] * 4
    if mine is None:
        spec = pl.BlockSpec((tr, tc), lambda i, j: (i, j))
        return pl.pallas_call(
            body, name=name, grid=(r // tr, c // tc),
            in_specs=[pl.BlockSpec((n_parts, tr, tc), lambda i, j: (0, i, j))] + [spec] * 3,
            out_specs=[spec] * 4, out_shape=out_shape, compiler_params=_params(2),
        )(parts, w, m, v)
    spec = pl.BlockSpec((tr, tc), lambda i, j, me_ref: (i, j))
    return pl.pallas_call(
        body, name=name, out_shape=out_shape, compiler_params=_params(2),
        grid_spec=pltpu.PrefetchScalarGridSpec(
            num_scalar_prefetch=1, grid=(r // tr, c // tc),
            in_specs=[pl.BlockSpec((n_parts, tr, tc), lambda i, j, me_ref: (0, i, j)),
                      pl.BlockSpec((None, tr, tc), lambda i, j, me_ref: (me_ref[0], i, j))] + [spec] * 3,
            out_specs=[spec] * 4),
    )(me, parts, mine, w, m, v)


def _me():
    return lax.axis_index("x"), lax.axis_index("y"), lax.axis_index("c")


def _gather_prologue(c, w_ada, b_mine, w_in_t, name):
    n_dev, d = N_DEV, c.shape[1]
    ada_w = w_ada.shape[1]

    def body(c_ref, w_ref, b_ref, win_ref, call_ref, ada_ref, gin_ref, cols_ref, send_sems, recv_sems, local_sems):
        x, y, cc = _me()
        me, sibling = (x, y, cc), (x, y, 1 - cc)
        chips = [(1 - x, y), (x, 1 - y), (1 - x, 1 - y)]
        outs = (call_ref, ada_ref, gin_ref)

        def rows(a, dev):
            return outs[a].at[4 * dev[0] + 2 * dev[1] + dev[2]]

        def copy(a, k, block, to, src=None):
            return pltpu.make_async_remote_copy(
                src_ref=rows(a, block) if src is None else src, dst_ref=rows(a, block),
                send_sem=send_sems.at[a, k], recv_sem=recv_sems.at[a, k], device_id=to, device_id_type=MESH)

        def begin(a, src):
            own = pltpu.make_async_copy(src, rows(a, me), local_sems.at[a])
            sends = [copy(a, 0, me, sibling, src=src)] + [copy(a, 1 + j, me, (*chip, cc), src=src) for j, chip in enumerate(chips)]
            for cp in [own] + sends:
                cp.start()
            return own, sends

        def finish(a, own, sends):
            passed = []
            for j, chip in enumerate(chips):
                copy(a, 1 + j, (*chip, cc), me).wait_recv()
                passed.append(copy(a, 4 + j, (*chip, cc), sibling))
                passed[-1].start()
            copy(a, 0, sibling, me).wait_recv()
            for j, chip in enumerate(chips):
                copy(a, 4 + j, (*chip, 1 - cc), me).wait_recv()
            for cp in sends + passed:
                cp.wait_send()
            own.wait()

        finish(0, *begin(0, c_ref))
        cols_ref[...] = (jnp.dot(call_ref[:, 0, :].astype(bf16), w_ref[...].astype(bf16), preferred_element_type=f32)
                         + b_ref[...])
        finish(1, *begin(1, cols_ref))
        finish(2, *begin(2, win_ref))

    vmem, hbm = pl.BlockSpec(memory_space=pltpu.VMEM), pl.BlockSpec(memory_space=pl.ANY)
    return pl.pallas_call(
        body, name=name, in_specs=[vmem, vmem, vmem, hbm], out_specs=[vmem, vmem, hbm],
        out_shape=[jax.ShapeDtypeStruct((n_dev, 1, d), f32), jax.ShapeDtypeStruct((n_dev, n_dev, ada_w), f32),
                   jax.ShapeDtypeStruct((n_dev,) + w_in_t.shape, w_in_t.dtype)],
        scratch_shapes=[pltpu.VMEM((n_dev, ada_w), f32), pltpu.SemaphoreType.DMA((3, 7)), pltpu.SemaphoreType.DMA((3, 7)),
                        pltpu.SemaphoreType.DMA((3,))],
        compiler_params=pltpu.CompilerParams(vmem_limit_bytes=VMEM_LIMIT),
    )(c, w_ada, b_mine, w_in_t)


_FLIPS = ((0, 0, 1), (1, 0, 0), (0, 1, 0), (1, 1, 0), (1, 0, 1), (0, 1, 1), (1, 1, 1))
_HBM = pl.BlockSpec(memory_space=pltpu.HBM)
_SEM = pl.BlockSpec(memory_space=pltpu.SEMAPHORE)


def _exchange_copies(scatter, srcs, lands, send_sems, recv_sems):
    x, y, c = _me()
    me_row = 4 * x + 2 * y + c
    out = []
    for k, (fx, fy, fc) in enumerate(_FLIPS):
        peer = (x ^ fx, y ^ fy, c ^ fc)
        peer_row = 4 * peer[0] + 2 * peer[1] + peer[2]
        for a in range(len(srcs)):
            out.append(pltpu.make_async_remote_copy(
                src_ref=srcs[a].at[peer_row] if scatter else srcs[a], dst_ref=lands[a].at[me_row],
                send_sem=send_sems.at[7 * a + k], recv_sem=recv_sems.at[7 * a + k], device_id=peer, device_id_type=MESH))
    return out


def _own_copies(srcs, lands, own_sems):
    x, y, c = _me()
    return [pltpu.make_async_copy(srcs[a], lands[a].at[4 * x + 2 * y + c], own_sems.at[a]) for a in range(len(srcs))]


def _exchange_start(arrays, scatter, name, after=None):
    n = len(arrays)
    lands = [lax.empty(a.shape if scatter else (N_DEV,) + a.shape, a.dtype) for a in arrays]
    extra = [] if after is None else [after]

    def body(*refs):
        srcs, zones = refs[:n], refs[n:2 * n]
        send_sems, recv_sems, own_sems = refs[2 * n + len(extra):2 * n + len(extra) + 3]
        token = refs[-1]
        for cp in _exchange_copies(scatter, srcs, zones, send_sems, recv_sems):
            cp.start()
        for cp in [] if scatter else _own_copies(srcs, zones, own_sems):
            cp.start()
        token[...] = jnp.zeros_like(token)

    thru = [pltpu.HBM(a.shape, a.dtype) for a in list(arrays) + lands]
    outs = pl.pallas_call(
        body, name=name,
        out_shape=(pltpu.SemaphoreType.DMA((7 * n,)), pltpu.SemaphoreType.DMA((7 * n,)), pltpu.SemaphoreType.DMA((n,)), *thru,
                   jax.ShapeDtypeStruct((8, LANES), f32)),
        in_specs=[_HBM] * (2 * n) + [pl.BlockSpec(memory_space=pl.ANY)] * len(extra),
        out_specs=(_SEM, _SEM, _SEM, *[_HBM] * (2 * n), pl.BlockSpec(memory_space=pltpu.VMEM)),
        input_output_aliases={i: 3 + i for i in range(2 * n)},
        compiler_params=pltpu.CompilerParams(has_side_effects=pltpu.SideEffectType.DATAFLOW_SIDE_EFFECTING),
    )(*[pltpu.with_memory_space_constraint(a, pltpu.HBM) for a in list(arrays) + lands], *extra)
    return dict(n=n, scatter=scatter, sems=outs[:3], srcs=outs[3:3 + n], lands=outs[3 + n:3 + 2 * n], token=outs[-1])


def _exchange_wait(handle, after, name):
    n, scatter = handle["n"], handle["scatter"]

    def body(*refs):
        srcs, zones = refs[:n], refs[n:2 * n]
        send_sems, recv_sems, own_sems = refs[2 * n:2 * n + 3]
        for cp in _exchange_copies(scatter, srcs, zones, send_sems, recv_sems):
            cp.wait_send()
            cp.wait_recv()
        for cp in [] if scatter else _own_copies(srcs, zones, own_sems):
            cp.wait()

    thru = [pltpu.HBM(a.shape, a.dtype) for a in list(handle["srcs"]) + list(handle["lands"])]
    outs = pl.pallas_call(
        body, name=name, out_shape=tuple(thru),
        in_specs=[_HBM] * (2 * n) + [_SEM, _SEM, _SEM, pl.BlockSpec(memory_space=pl.ANY)], out_specs=tuple([_HBM] * (2 * n)),
        input_output_aliases={i: i for i in range(2 * n)},
        compiler_params=pltpu.CompilerParams(has_side_effects=pltpu.SideEffectType.DATAFLOW_SIDE_EFFECTING),
    )(*handle["srcs"], *handle["lands"], *handle["sems"], after)
    return list(outs[:n]), list(outs[n:])


def _cols_from_shards(g):
    return jnp.transpose(g, (1, 0, 2)).reshape(g.shape[1], -1)


def _shards_from_cols(a):
    return jnp.transpose(a.reshape(a.shape[0], N_DEV, -1), (1, 0, 2))


def _local_step(x, positions, ada, g_pre_mix, g_post_mix, b_f, sinks, g_pre_ffn, g_post_ffn, target,
                w_in_t, mix_weights, ffn_weights, on_grads):
    s, d = x.shape
    row = lambda v: v.reshape(1, -1)
    shift_m, scale_m, gate_m, shift_f, scale_f, gate_f = (ada[i:i + 1] for i in range(6))
    w_gate_t, w_qkv_t = w_in_t[F_OFF + N_HEADS:], w_in_t
    w_f_t = jnp.pad(w_in_t[F_OFF:F_OFF + N_HEADS], ((0, LANES - N_HEADS), (0, 0)))
    bf_row = jnp.pad(row(b_f), ((0, 0), (0, LANES - N_HEADS)))
    sink_rows = jnp.broadcast_to(sinks.reshape(N_HEADS, 1).astype(f32), (N_HEADS, LANES))
    inv_freq = 1.0 / (ROPE_THETA ** (jnp.arange(0, HEAD_DIM, 2, dtype=f32) / HEAD_DIM))
    cos, sin_s = _rope_tables(positions.reshape(s, 1), jnp.tile(inv_freq, 4).reshape(1, LANES), "rope_tables")

    h1, qa, ka, va, qb, kb, vb = _prenorm_proj_qkv(x, row(g_pre_mix), scale_m, shift_m, w_qkv_t, cos, sin_s, "prenorm_proj_qkv")
    gl = _matmul(h1, w_gate_t, "nt", bf16, "proj_gate")
    fl, cum_b = _forget_prep(h1, w_f_t, bf_row, "proj_forget_prep")
    o_a, lse_a = _attn_fwd(qa, ka, va, "swa_fwd", sink_rows=sink_rows, window=WINDOW, t=2048)
    o_b, lse_b = _attn_fwd(qb, kb, vb, "fox_fwd", cum_b=cum_b, t=1024)
    everything_before = (gl[:8, :LANES] + o_a[:8, :LANES] + o_b[:8, :LANES]).astype(f32)
    w_branch_a, w_branch_b, w_out = mix_weights(everything_before)
    ba, bb, merged = _branch_merge(o_a, o_b, w_branch_a, w_branch_b, gl, "branch_merge")
    y1, x2, h2 = _out_proj_postnorm_prenorm(merged, w_out, x, row(g_post_mix), gate_m, row(g_pre_ffn), scale_f, shift_f,
                                            "out_proj_norms")

    w_ffn_in_t, w_ffn_out = ffn_weights(h2)
    g_ff, u_ff, act = _ffn_in_swiglu(h2, w_ffn_in_t, "ffn_in_swiglu")
    loss_row, d_out, d_y2, vec_pf = _out_proj_loss_tail(act, w_ffn_out, x2, row(g_post_ffn), gate_f, target, "ffn_out_loss_tail")

    g_w_ffn_out = _matmul(act, d_y2, "tn", bf16, "ffn_out_wgrad")
    dg_ff, du_ff = _ffn_out_dgrad_swiglu(d_y2, w_ffn_out, g_ff, u_ff, "ffn_out_dgrad_swiglu")
    g_w_ffn_in_t = _wgrad_stack([dg_ff, du_ff], h2, "ffn_in_wgrad")
    sent = on_grads(dict(w_ffn_in=g_w_ffn_in_t, w_ffn_out=g_w_ffn_out))
    d_x2, vec_nf, d_y1, vec_pm = _dgrad_prenorm_bwd(
        [(dg_ff, w_ffn_in_t, 0), (du_ff, w_ffn_in_t, 1)], x2, row(g_pre_ffn), scale_f, d_out, "ffn_in_dgrad_norms_bwd",
        after=sent, below=(y1, row(g_post_mix), gate_m))

    g_w_out = _matmul(merged, d_y1, "tn", bf16, "out_proj_wgrad")
    d_ba, d_bb, dgl = _out_dgrad_merge_bwd(d_y1, w_out, ba, bb, gl, "out_proj_dgrad_merge_bwd")
    g_w_branch_a = _matmul(o_a, d_ba, "tn", bf16, "branch_a_wgrad")
    g_w_branch_b = _matmul(o_b, d_bb, "tn", bf16, "branch_b_wgrad")
    sent = on_grads(dict(w_out=g_w_out, w_branch_a=g_w_branch_a, w_branch_b=g_w_branch_b))
    d_oa, delta_a, d_sink = _branch_dgrad_delta(d_ba, w_branch_a, o_a, "branch_a_dgrad_delta", lse=lse_a,
                                                sink_rows=sink_rows, after=sent)
    d_ob, delta_b = _branch_dgrad_delta(d_bb, w_branch_b, o_b, "branch_b_dgrad_delta", after=sent)
    dqa_t, dka, dva = _attn_bwd(qa, ka, va, d_oa, lse_a, delta_a, "swa_bwd", window=WINDOW, t=2048)
    dqb_t, dkb, dvb, dcs, rs = _attn_bwd(qb, kb, vb, d_ob, lse_b, delta_b, "fox_bwd", cum_b=cum_b, t=512)
    dqkv = _qkv_prep_bwd(dqa_t, dka, dva, dqb_t, dkb, dvb, cos, sin_s, "qkv_prep_bwd")
    dfl, vec_bf = _forget_prep_bwd(rs.reshape(N_HEADS, s), dcs, fl, bf_row, "forget_prep_bwd")
    g_w_in_t = jnp.concatenate([_matmul(dqkv, h1, "tn", bf16, "qkv_wgrad"), _matmul(dfl, h1, "tn", bf16, "forget_wgrad")[:N_HEADS],
                                _matmul(dgl, h1, "tn", bf16, "gate_wgrad")], axis=0)
    sent = on_grads(dict(w_in=g_w_in_t))
    grad_x, vec_nm = _dgrad_prenorm_bwd([(dgl, w_gate_t, 0), (dqkv, w_qkv_t, 0), (dfl, w_f_t, 0)], x, row(g_pre_mix),
                                        scale_m, d_x2, "in_proj_dgrad_prenorm_bwd", after=sent)

    d_ada = jnp.concatenate([vec_nm[0], vec_nm[1], vec_pm[0], vec_nf[0], vec_nf[1], vec_pf[0]])
    small = dict(b_ada=d_ada, g_pre_mix=vec_nm[2], g_post_mix=vec_pm[1], g_pre_ffn=vec_nf[2], g_post_ffn=vec_pf[1],
                 b_f=vec_bf[0, :N_HEADS], sinks=d_sink[:, 0], loss=loss_row[0, :1])
    return grad_x, small


_SMALL = (("b_ada", 6144), ("g_pre_mix", 1024), ("g_post_mix", 1024), ("g_pre_ffn", 1024), ("g_post_ffn", 1024),
          ("b_f", 128), ("sinks", 128), ("loss", 128))
_SMALL_ROWS = 88


def _pack_small(vals):
    parts = [jnp.pad(vals[k].reshape(-1).astype(f32), (0, n - vals[k].size)) for k, n in _SMALL]
    flat = jnp.concatenate(parts)
    return jnp.pad(flat, (0, _SMALL_ROWS * LANES - flat.size)).reshape(_SMALL_ROWS, LANES)


def _unpack_small(slab, shapes):
    flat, out, off = slab.reshape(-1), {}, 0
    for k, n in _SMALL:
        size = math.prod(shapes[k])
        out[k] = flat[off:off + size].reshape(shapes[k])
        off += n
    return out


def kernel(x, c, positions, w_ada, b_ada, g_pre_mix, g_post_mix, w_in, b_f, sinks, w_branch_a, w_branch_b, w_out, g_pre_ffn, g_post_ffn, w_ffn_in, w_ffn_out, loss_target, m_w_ada, m_b_ada, m_g_pre_mix, m_g_post_mix, m_w_in, m_b_f, m_sinks, m_w_branch_a, m_w_branch_b, m_w_out, m_g_pre_ffn, m_g_post_ffn, m_w_ffn_in, m_w_ffn_out, v_w_ada, v_b_ada, v_g_pre_mix, v_g_post_mix, v_w_in, v_b_f, v_sinks, v_w_branch_a, v_w_branch_b, v_w_out, v_g_pre_ffn, v_g_post_ffn, v_w_ffn_in, v_w_ffn_out):
    xi, yi, ci = _me()
    me = 4 * xi + 2 * yi + ci
    d = D_MODEL
    ada_w = w_ada.shape[2]

    transposed = ("w_in", "w_ffn_in")
    tr = lambda a: jnp.transpose(a[0])

    b_mine = lax.dynamic_slice(b_ada, (0, me * ada_w), (1, ada_w))
    c_all, ada_all, g_in = _gather_prologue(c, w_ada[0], b_mine, tr(w_in).astype(bf16), "gather_prologue")
    c_all = c_all.reshape(N_DEV, d)
    ada = lax.dynamic_index_in_dim(ada_all, me, axis=1, keepdims=False).reshape(6, d)
    late_mix = [w.astype(bf16) for w in (w_branch_a[0], w_branch_b[0], w_out[0])]
    late_ffn = [w.astype(bf16) for w in (tr(w_ffn_in), w_ffn_out[0])]
    mix_h = _exchange_start(late_mix, False, "gather_mix_start", after=g_in)
    ffn_h = _exchange_start(late_ffn, False, "gather_ffn_start", after=mix_h["token"])

    def rows_from_shards(g):
        return g.reshape(g.shape[0] * g.shape[1], g.shape[2])

    def mix_weights(after):
        _, (g_ba, g_bb, g_out) = _exchange_wait(mix_h, after, "gather_mix_wait")
        return _cols_from_shards(g_ba), _cols_from_shards(g_bb), rows_from_shards(g_out)

    def ffn_weights(after):
        _, (g_fi, g_fo) = _exchange_wait(ffn_h, after, "gather_ffn_wait")
        return rows_from_shards(g_fi), rows_from_shards(g_fo)

    row_sharded = ("w_out", "w_ffn_out") + transposed
    in_flight = []

    def on_grads(group):
        sends = [g.reshape(N_DEV, g.shape[0] // N_DEV, g.shape[1]) if nm in row_sharded else _shards_from_cols(g)
                 for nm, g in group.items()]
        handle = _exchange_start(sends, True, "scatter_start_%d" % len(in_flight))
        in_flight.append((list(group), handle))
        return handle["token"]

    grad_x, small = _local_step(
        x[0], positions[0], ada + ffn_h["token"][0, 0], g_pre_mix[0], g_post_mix[0], b_f[0], sinks[0], g_pre_ffn[0],
        g_post_ffn[0], loss_target[0], rows_from_shards(g_in), mix_weights, ffn_weights, on_grads)

    ws = dict(w_in=(w_in, m_w_in, v_w_in), w_branch_a=(w_branch_a, m_w_branch_a, v_w_branch_a),
              w_branch_b=(w_branch_b, m_w_branch_b, v_w_branch_b), w_out=(w_out, m_w_out, v_w_out),
              w_ffn_in=(w_ffn_in, m_w_ffn_in, v_w_ffn_in), w_ffn_out=(w_ffn_out, m_w_ffn_out, v_w_ffn_out))
    res = {}

    def finish_group(gi, after):
        names, handle = in_flight[gi]
        sends, zones = _exchange_wait(handle, after, "scatter_wait_%d" % gi)
        for nm, zone, sent in zip(names, zones, sends):
            w, m, v = (tr(a) if nm in transposed else a[0] for a in ws[nm])
            out = _adamw(zone, w, m, v, "adamw_" + nm, mine=sent, me=me.reshape(1).astype(jnp.int32))
            after = out[0]
            res[nm] = [jnp.transpose(o) for o in out] if nm in transposed else out
        return after

    small_h = _exchange_start([_pack_small(small)], False, "gather_small_start", after=grad_x)
    done = finish_group(1, finish_group(0, small_h["token"]))
    _, (slab_all,) = _exchange_wait(small_h, done, "gather_small_wait")
    small_w = dict(b_ada=b_ada, g_pre_mix=g_pre_mix, g_post_mix=g_post_mix, g_pre_ffn=g_pre_ffn, g_post_ffn=g_post_ffn,
                   b_f=b_f, sinks=sinks, loss=jnp.zeros((1,), f32))
    small_m = dict(b_ada=m_b_ada, g_pre_mix=m_g_pre_mix, g_post_mix=m_g_post_mix, g_pre_ffn=m_g_pre_ffn,
                   g_post_ffn=m_g_post_ffn, b_f=m_b_f, sinks=m_sinks, loss=jnp.zeros((1,), f32))
    small_v = dict(b_ada=v_b_ada, g_pre_mix=v_g_pre_mix, g_post_mix=v_g_post_mix, g_pre_ffn=v_g_pre_ffn,
                   g_post_ffn=v_g_post_ffn, b_f=v_b_f, sinks=v_sinks, loss=jnp.ones((1,), f32))
    shapes = {k: small_w[k].shape for k, _ in _SMALL}
    s_out = _adamw(slab_all, _pack_small(small_w), _pack_small(small_m), _pack_small(small_v), "adamw_small")
    s_grad, s_delta, s_m, s_v = (_unpack_small(o, shapes) for o in s_out)

    d_ada_all = lax.dynamic_slice(slab_all[:, :6144 // LANES, :].reshape(N_DEV, 6144), (0, me * ada_w), (N_DEV, ada_w))
    ada_parts = _ada_wgrad(c_all, d_ada_all, "ada_wgrad")

    res["w_ada"] = _adamw(ada_parts, w_ada[0], m_w_ada[0], v_w_ada[0], "adamw_w_ada")
    finish_group(2, res["w_ada"][0])

    order = ["w_ada", "b_ada", "g_pre_mix", "g_post_mix", "w_in", "b_f", "sinks", "w_branch_a", "w_branch_b", "w_out",
             "g_pre_ffn", "g_post_ffn", "w_ffn_in", "w_ffn_out"]
    outs = [s_grad["loss"].reshape(()), grad_x[None]]
    for which, small_o in enumerate((s_grad, s_delta, s_m, s_v)):
        for nm in order:
            outs.append(res[nm][which][None] if nm in res else small_o[nm])
    return tuple(outs)
```

```python
import math

import jax
import jax.numpy as jnp
from jax import lax
from jax.experimental import pallas as pl
from jax.experimental.pallas import tpu as pltpu

f32 = jnp.float32
bf16 = jnp.bfloat16

D_MODEL = 1024
HEAD_DIM = 64
N_HEADS = 8
N_PAIRS = 4
QKV_W = 2304
F_OFF = 2304
WINDOW = 128
ROPE_THETA = 10000.0
RMS_EPS = 1e-6
N_DEV = 8
ADAM_LR, ADAM_B1, ADAM_B2, ADAM_EPS, ADAM_WD, ADAM_STEP = 0.001, 0.9, 0.999, 1e-08, 0.01, 10
NEG = -1e30
L_ROW = (HEAD_DIM, 0)
LANES = 128
VMEM_LIMIT = 48 * 1024 * 1024
MESH = pl.DeviceIdType.MESH

_NT = (((1,), (1,)), ((), ()))
_TN = (((0,), (0,)), ((), ()))


def _params(n_grid=0):
    sem = ("arbitrary",) * n_grid if n_grid else None
    return pltpu.CompilerParams(dimension_semantics=sem, vmem_limit_bytes=VMEM_LIMIT)


def _row_tile(s, want):
    t = min(s, want)
    assert s % t == 0, (s, t)
    return t


MATMUL_VMEM_BUDGET = 40 * 1024 * 1024


def _matmul_tiles(m, n, k, a_item, b_item, o_item):
    def tiles(d):
        return [t for t in range(LANES, min(d, 2048) + 1, LANES) if d % t == 0] or [d]

    best = None
    for tm in tiles(m):
        for tn in tiles(n):
            vmem = 2 * (tm * k * a_item + tn * k * b_item + tm * tn * o_item) + tm * tn * 4
            if vmem > MATMUL_VMEM_BUDGET:
                continue
            traffic = m * k * a_item + n * k * b_item * (1 if tn == n else m // tm) + m * n * o_item
            steps = (m // tm) * (n // tn)
            key = (traffic, 0, steps) if steps >= 4 else (traffic, 1, -steps)
            if best is None or key < best[0]:
                best = (key, tm, tn)
    assert best is not None, (m, n, k)
    return best[1], best[2]


def _matmul(a, b, mode, out_dtype, name, after=None):
    if mode == "nn":
        (m, k), n = a.shape, b.shape[1]
    elif mode == "nt":
        (m, k), n = a.shape, b.shape[0]
    else:
        (k, m), n = a.shape, b.shape[1]
    tm, tn = _matmul_tiles(m, n, k, a.dtype.itemsize, b.dtype.itemsize, jnp.dtype(out_dtype).itemsize)
    if mode == "nn":
        a_spec, b_spec, dims = pl.BlockSpec((tm, k), lambda i, j: (i, 0)), pl.BlockSpec((k, tn), lambda i, j: (0, j)), None
    elif mode == "nt":
        a_spec, b_spec, dims = pl.BlockSpec((tm, k), lambda i, j: (i, 0)), pl.BlockSpec((tn, k), lambda i, j: (j, 0)), _NT
    else:
        a_spec, b_spec, dims = pl.BlockSpec((k, tm), lambda i, j: (0, i)), pl.BlockSpec((k, tn), lambda i, j: (0, j)), _TN

    def body(a_ref, b_ref, *rest):
        o_ref = rest[-1]
        av, bv = a_ref[...].astype(bf16), b_ref[...].astype(bf16)
        if dims is None:
            r = jnp.dot(av, bv, preferred_element_type=f32)
        else:
            r = lax.dot_general(av, bv, dims, preferred_element_type=f32)
        o_ref[...] = r.astype(out_dtype)

    extra = [] if after is None else [after]
    return pl.pallas_call(
        body, name=name, grid=(m // tm, n // tn), in_specs=[a_spec, b_spec] + [pl.BlockSpec(memory_space=pl.ANY)] * len(extra),
        out_specs=pl.BlockSpec((tm, tn), lambda i, j: (i, j)),
        out_shape=jax.ShapeDtypeStruct((m, n), out_dtype), compiler_params=_params(2),
    )(a, b, *extra)


def _rstd(v):
    return lax.rsqrt(jnp.mean(v * v, axis=-1, keepdims=True) + RMS_EPS)


def _row_spec(tm, d):
    return pl.BlockSpec((tm, d), lambda i: (i, 0))


def _vec_spec(d, rows=1):
    return pl.BlockSpec((rows, d), lambda i: (0, 0))


def _proj_spec(a, w, tm):
    return [_row_spec(tm, a.shape[1]), pl.BlockSpec(w.shape, lambda i: (0, 0))]


def _out_proj_postnorm_prenorm(a, w, x, g_post, gate, g_pre, scale, shift, name):
    s, d = x.shape
    tm = _row_tile(s, 512)

    def body(a_ref, w_ref, x_ref, gp_ref, gate_ref, g_ref, sc_ref, sh_ref, y_ref, x2_ref, h_ref):
        yv = jnp.dot(a_ref[...], w_ref[...], preferred_element_type=f32)
        y_ref[...] = yv
        x2 = x_ref[...] + gate_ref[...] * (yv * _rstd(yv) * gp_ref[...])
        x2_ref[...] = x2
        h_ref[...] = ((x2 * _rstd(x2) * g_ref[...]) * (1.0 + sc_ref[...]) + sh_ref[...]).astype(bf16)

    return pl.pallas_call(
        body, name=name, grid=(s // tm,), in_specs=_proj_spec(a, w, tm) + [_row_spec(tm, d)] + [_vec_spec(d)] * 5,
        out_specs=[_row_spec(tm, d)] * 3,
        out_shape=[jax.ShapeDtypeStruct((s, d), f32)] * 2 + [jax.ShapeDtypeStruct((s, d), bf16)], compiler_params=_params(1),
    )(a, w, x, g_post, gate, g_pre, scale, shift)


def _rms_bwd(u, v, r):
    return r * u - v * (r * r * r) * jnp.mean(u * v, axis=-1, keepdims=True)


def _out_proj_loss_tail(a, w, x, g, gate, target, name):
    s, d = x.shape
    tm = _row_tile(s, 512)

    def body(a_ref, w_ref, x_ref, g_ref, gate_ref, t_ref, loss_ref, do_ref, dy_ref, vec_ref):
        @pl.when(pl.program_id(0) == 0)
        def _():
            loss_ref[...] = jnp.zeros_like(loss_ref)
            vec_ref[...] = jnp.zeros_like(vec_ref)
        yv = jnp.dot(a_ref[...], w_ref[...], preferred_element_type=f32)
        r = _rstd(yv)
        yn = yv * r
        err = x_ref[...] + gate_ref[...] * (yn * g_ref[...]) - t_ref[...]
        loss_ref[...] += 0.5 * jnp.sum(jnp.mean(err * err, axis=-1, keepdims=True), axis=0, keepdims=True)
        dr = err / d
        do_ref[...] = dr
        dn = dr * gate_ref[...]
        vec_ref[0:1, :] += jnp.sum(dr * (yn * g_ref[...]), axis=0, keepdims=True)
        vec_ref[1:2, :] += jnp.sum(dn * yn, axis=0, keepdims=True)
        dy_ref[...] = _rms_bwd(dn * g_ref[...], yv, r).astype(bf16)

    return pl.pallas_call(
        body, name=name, grid=(s // tm,),
        in_specs=_proj_spec(a, w, tm) + [_row_spec(tm, d)] + [_vec_spec(d)] * 2 + [_row_spec(tm, d)],
        out_specs=[_vec_spec(LANES), _row_spec(tm, d), _row_spec(tm, d), _vec_spec(d, 8)],
        out_shape=[jax.ShapeDtypeStruct((1, LANES), f32), jax.ShapeDtypeStruct((s, d), f32),
                   jax.ShapeDtypeStruct((s, d), bf16), jax.ShapeDtypeStruct((8, d), f32)],
        compiler_params=_params(1),
    )(a, w, x, g, gate, target)


def _dgrad_prenorm_bwd(terms, x, g, scale, dres, name, after=None, below=None):
    s, d = x.shape
    n = len(terms)
    k = sum(a.shape[1] for a, _, _ in terms)
    row_bytes = 2 * (2 * k) + d * (4 + 2 * 4 * 3 + (2 * 4 + 2 * 2 if below else 0))
    tm = next(t for t in (512, 256, 128) if s % t == 0 and 4 * k * d + t * row_bytes <= MATMUL_VMEM_BUDGET)
    extra = [] if after is None else [after]

    def body(*refs):
        a_refs, b_refs = refs[:n], refs[n:2 * n]
        x_ref, g_ref, sc_ref, dr_ref = refs[2 * n:2 * n + 4]
        n_in = 2 * n + 4 + (3 if below else 0) + len(extra)
        dx_ref, vec_ref = refs[n_in], refs[n_in + 1]
        if below:
            y_ref, gp_ref, gate_ref = refs[2 * n + 4:2 * n + 7]
            dy_ref, vec2_ref = refs[n_in + 2], refs[n_in + 3]

        @pl.when(pl.program_id(0) == 0)
        def _():
            vec_ref[...] = jnp.zeros_like(vec_ref)
            if below:
                vec2_ref[...] = jnp.zeros_like(vec2_ref)
        dhv = jnp.dot(a_refs[0][...], b_refs[0][...], preferred_element_type=f32)
        for i in range(1, n):
            dhv = dhv + jnp.dot(a_refs[i][...], b_refs[i][...], preferred_element_type=f32)
        xv = x_ref[...]
        r = _rstd(xv)
        xn = xv * r
        dn = dhv * (1.0 + sc_ref[...])
        vec_ref[0:1, :] += jnp.sum(dhv, axis=0, keepdims=True)
        vec_ref[1:2, :] += jnp.sum(dhv * (xn * g_ref[...]), axis=0, keepdims=True)
        vec_ref[2:3, :] += jnp.sum(dn * xn, axis=0, keepdims=True)
        dx = dr_ref[...] + _rms_bwd(dn * g_ref[...], xv, r)
        dx_ref[...] = dx
        if below:
            yv = y_ref[...]
            ry = _rstd(yv)
            yn = yv * ry
            dny = dx * gate_ref[...]
            vec2_ref[0:1, :] += jnp.sum(dx * (yn * gp_ref[...]), axis=0, keepdims=True)
            vec2_ref[1:2, :] += jnp.sum(dny * yn, axis=0, keepdims=True)
            dy_ref[...] = _rms_bwd(dny * gp_ref[...], yv, ry).astype(bf16)

    in_specs = ([_row_spec(tm, a.shape[1]) for a, _, _ in terms]
                + [pl.BlockSpec((a.shape[1], d), lambda i, r=r: (r, 0)) for a, _, r in terms]
                + [_row_spec(tm, d)] + [_vec_spec(d)] * 2 + [_row_spec(tm, d)])
    out_specs = [_row_spec(tm, d), _vec_spec(d, 8)]
    out_shape = [jax.ShapeDtypeStruct((s, d), f32), jax.ShapeDtypeStruct((8, d), f32)]
    args = [a for a, _, _ in terms] + [b for _, b, _ in terms] + [x, g, scale, dres]
    if below:
        in_specs += [_row_spec(tm, d)] + [_vec_spec(d)] * 2
        out_specs += [_row_spec(tm, d), _vec_spec(d, 8)]
        out_shape += [jax.ShapeDtypeStruct((s, d), bf16), jax.ShapeDtypeStruct((8, d), f32)]
        args += list(below)
    return pl.pallas_call(
        body, name=name, grid=(s // tm,), in_specs=in_specs + [pl.BlockSpec(memory_space=pl.ANY)] * len(extra),
        out_specs=out_specs, out_shape=out_shape, compiler_params=_params(1),
    )(*args, *extra)


def _lane():
    return lax.broadcasted_iota(jnp.int32, (1, LANES), 1)


def _rope_tables(pos_col, inv_freq, name):
    s = pos_col.shape[0]

    def body(p_ref, f_ref, cos_ref, sin_ref):
        ang = p_ref[...].astype(f32) * f_ref[...]
        first_half = (_lane() % HEAD_DIM) < HEAD_DIM // 2
        cos_ref[...] = jnp.cos(ang)
        sn = jnp.sin(ang)
        sin_ref[...] = jnp.where(first_half, -sn, sn)

    return pl.pallas_call(
        body, name=name, out_shape=[jax.ShapeDtypeStruct((s, LANES), f32)] * 2, compiler_params=_params(),
    )(pos_col, inv_freq)


def _swap_halves(v):
    first_half = (_lane() % HEAD_DIM) < HEAD_DIM // 2
    return jnp.where(first_half, pltpu.roll(v, LANES - HEAD_DIM // 2, axis=1), pltpu.roll(v, HEAD_DIM // 2, axis=1))


def _prenorm_proj_qkv(x, g, mod_scale, mod_shift, w_qkv_t, cos, sin_s, name):
    s, d = x.shape
    tm = _row_tile(s, 512)
    scale = 1.0 / math.sqrt(HEAD_DIM)

    def body(x_ref, g_ref, msc_ref, msh_ref, w_ref, c_ref, s_ref, h_ref, qa_ref, ka_ref, va_ref, qb_ref, kb_ref, vb_ref):
        xv = x_ref[...]
        h = ((xv * _rstd(xv) * g_ref[...]) * (1.0 + msc_ref[...]) + msh_ref[...]).astype(bf16)
        h_ref[...] = h
        proj = lax.dot_general(h, w_ref[...], _NT, preferred_element_type=f32)
        cs, sn = c_ref[...], s_ref[...]
        low = _lane() < HEAD_DIM

        def blk(j):
            return proj[:, j * LANES:(j + 1) * LANES]

        def rope(v):
            return v * cs + _swap_halves(v) * sn

        def expand(v):
            other = pltpu.roll(v, HEAD_DIM, axis=1)
            return jnp.where(low, v, other), jnp.where(low, other, v)

        for j in range(N_PAIRS):
            qa_ref[:, j * LANES:(j + 1) * LANES] = (rope(blk(j)) * scale).astype(bf16)
            qb_ref[:, j * LANES:(j + 1) * LANES] = (blk(6 + j) * scale).astype(bf16)
            kb_ref[:, j * LANES:(j + 1) * LANES] = blk(10 + j).astype(bf16)
            vb_ref[:, j * LANES:(j + 1) * LANES] = blk(14 + j).astype(bf16)
        k0, k1 = expand(rope(blk(4)))
        v0, v1 = expand(blk(5))
        for j in range(N_PAIRS):
            ka_ref[:, j * LANES:(j + 1) * LANES] = (k0 if j < 2 else k1).astype(bf16)
            va_ref[:, j * LANES:(j + 1) * LANES] = (v0 if j < 2 else v1).astype(bf16)

    hw = N_PAIRS * LANES
    return pl.pallas_call(
        body, name=name, grid=(s // tm,),
        in_specs=[_row_spec(tm, d)] + [_vec_spec(d)] * 3
        + [pl.BlockSpec((QKV_W, d), lambda i: (0, 0)), _row_spec(tm, LANES), _row_spec(tm, LANES)],
        out_specs=[_row_spec(tm, d)] + [_row_spec(tm, hw)] * 6,
        out_shape=[jax.ShapeDtypeStruct((s, d), bf16)] + [jax.ShapeDtypeStruct((s, hw), bf16)] * 6, compiler_params=_params(1),
    )(x, g, mod_scale, mod_shift, w_qkv_t, cos, sin_s)


def _qkv_prep_bwd(dqa_t, dka, dva, dqb_t, dkb, dvb, cos, sin_s, name):
    s = dka.shape[0]
    tm = _row_tile(s, 1024)
    scale = 1.0 / math.sqrt(HEAD_DIM)
    hw = N_PAIRS * LANES
    t_spec = pl.BlockSpec((hw, tm), lambda i: (0, i))

    def body(dqa_ref, dka_ref, dva_ref, dqb_ref, dkb_ref, dvb_ref, c_ref, s_ref, o_ref):
        cs, sn = c_ref[...], s_ref[...]
        low = _lane() < HEAD_DIM

        def blk(ref, j):
            return ref[:, j * LANES:(j + 1) * LANES].astype(f32)

        def blk_t(ref, j):
            return ref[j * LANES:(j + 1) * LANES, :].T

        def unrope(v):
            return v * cs + _swap_halves(v * sn)

        def fold(ref):
            a, b = blk(ref, 0) + blk(ref, 1), blk(ref, 2) + blk(ref, 3)
            kv0 = a + pltpu.roll(a, HEAD_DIM, axis=1)
            kv1 = b + pltpu.roll(b, HEAD_DIM, axis=1)
            return jnp.where(low, kv0, kv1)

        for j in range(N_PAIRS):
            o_ref[:, j * LANES:(j + 1) * LANES] = (unrope(blk_t(dqa_ref, j)) * scale).astype(bf16)
            o_ref[:, (6 + j) * LANES:(7 + j) * LANES] = (blk_t(dqb_ref, j) * scale).astype(bf16)
            o_ref[:, (10 + j) * LANES:(11 + j) * LANES] = blk(dkb_ref, j).astype(bf16)
            o_ref[:, (14 + j) * LANES:(15 + j) * LANES] = blk(dvb_ref, j).astype(bf16)
        o_ref[:, 4 * LANES:5 * LANES] = unrope(fold(dka_ref)).astype(bf16)
        o_ref[:, 5 * LANES:6 * LANES] = fold(dva_ref).astype(bf16)

    return pl.pallas_call(
        body, name=name, grid=(s // tm,),
        in_specs=[t_spec, _row_spec(tm, hw), _row_spec(tm, hw), t_spec, _row_spec(tm, hw), _row_spec(tm, hw)] + [_row_spec(tm, LANES)] * 2,
        out_specs=_row_spec(tm, QKV_W), out_shape=jax.ShapeDtypeStruct((s, QKV_W), bf16), compiler_params=_params(1),
    )(dqa_t, dka, dva, dqb_t, dkb, dvb, cos, sin_s)


def _cumsum_rows(v, reverse=False):
    n = v.shape[0]
    row = lax.broadcasted_iota(jnp.int32, v.shape, 0)
    sh = 1
    while sh < n:
        if reverse:
            v = v + jnp.where(row < n - sh, pltpu.roll(v, n - sh, axis=0), 0.0)
        else:
            v = v + jnp.where(row >= sh, pltpu.roll(v, sh, axis=0), 0.0)
        sh *= 2
    return v


def _log_sigmoid(z):
    return jnp.minimum(z, 0.0) - jnp.log1p(jnp.exp(-jnp.abs(z)))


def _forget_prep(h, w_f_t, bf_row, name):
    s, d = h.shape
    tm = _row_tile(s, 1024)

    def body(h_ref, w_ref, b_ref, f_ref, cb_ref, last_ref):
        @pl.when(pl.program_id(0) == 0)
        def _():
            last_ref[...] = jnp.zeros_like(last_ref)
        fl = lax.dot_general(h_ref[...], w_ref[...], _NT, preferred_element_type=f32)
        f_ref[...] = fl
        cum = _cumsum_rows(_log_sigmoid(fl + b_ref[...])) + last_ref[0:1, :]
        last_ref[0:1, :] = cum[tm - 1:tm, :]
        for hd in range(N_HEADS):
            cb_ref[:, hd * LANES:(hd + 1) * LANES] = jnp.broadcast_to(cum[:, hd:hd + 1], (tm, LANES))

    return pl.pallas_call(
        body, name=name, grid=(s // tm,),
        in_specs=[_row_spec(tm, d), pl.BlockSpec((LANES, d), lambda i: (0, 0)), _vec_spec(LANES)],
        out_specs=[_row_spec(tm, LANES), _row_spec(tm, N_HEADS * LANES)],
        out_shape=[jax.ShapeDtypeStruct((s, LANES), f32), jax.ShapeDtypeStruct((s, N_HEADS * LANES), f32)],
        scratch_shapes=[pltpu.VMEM((8, LANES), f32)], compiler_params=_params(1),
    )(h, w_f_t, bf_row)


def _forget_prep_bwd(rs, dcs, fl, bf_row, name):
    s = fl.shape[0]
    tm = _row_tile(s, 1024)
    n = s // tm

    def body(r_ref, c_ref, f_ref, b_ref, df_ref, db_ref, next_ref):
        @pl.when(pl.program_id(0) == 0)
        def _():
            next_ref[...] = jnp.zeros_like(next_ref)
            db_ref[...] = jnp.zeros_like(db_ref)
        eye = (lax.broadcasted_iota(jnp.int32, (N_HEADS, LANES), 0) == lax.broadcasted_iota(jnp.int32, (N_HEADS, LANES), 1)).astype(f32)
        dcum = lax.dot_general(r_ref[...], eye, _TN, precision=lax.Precision.HIGHEST, preferred_element_type=f32)
        for h in range(N_HEADS):
            dcum = dcum - jnp.where(_lane() == h, jnp.sum(c_ref[:, h * LANES:(h + 1) * LANES], axis=1, keepdims=True), 0.0)
        dlf = _cumsum_rows(dcum, reverse=True) + next_ref[0:1, :]
        next_ref[0:1, :] = dlf[0:1, :]
        z = f_ref[...] + b_ref[...]
        df = jnp.where(_lane() < N_HEADS, dlf * jax.nn.sigmoid(-z), 0.0)
        df_ref[...] = df.astype(bf16)
        db_ref[0:1, :] += jnp.sum(df, axis=0, keepdims=True)

    def rows(width):
        return pl.BlockSpec((tm, width), lambda i: (n - 1 - i, 0))

    return pl.pallas_call(
        body, name=name, grid=(n,),
        in_specs=[pl.BlockSpec((N_HEADS, tm), lambda i: (0, n - 1 - i)), rows(N_HEADS * LANES), rows(LANES), _vec_spec(LANES)],
        out_specs=[rows(LANES), _vec_spec(LANES, 8)],
        out_shape=[jax.ShapeDtypeStruct((s, LANES), bf16), jax.ShapeDtypeStruct((8, LANES), f32)],
        scratch_shapes=[pltpu.VMEM((8, LANES), f32)], compiler_params=_params(1),
    )(rs, dcs, fl, bf_row)


def _tile_mask(n_keys, n_queries, off, window):
    shape = (n_keys, n_queries)
    d = lax.broadcasted_iota(jnp.int32, shape, 1) - lax.broadcasted_iota(jnp.int32, shape, 0) + off
    valid = d >= 0
    return jnp.logical_and(valid, d < window) if window else valid


def _wide(v, t):
    return jnp.concatenate([v] * (t // LANES), axis=1)


def _attn_fwd(q, k, v, name, *, cum_b=None, sink_rows=None, window=None, t=256):
    s = q.shape[0]
    t = _row_tile(s, t)
    fox, has_sink = cum_b is not None, sink_rows is not None
    assert not window or (window % LANES == 0 and LANES + window <= s)

    def body(*refs):
        q_ref, k_ref, v_ref = refs[:3]
        rest = list(refs[3:])
        cb_ref = rest.pop(0) if fox else None
        sink_ref = rest.pop(0) if has_sink else None
        o_ref, lse_ref = rest
        i = pl.program_id(1)
        low = _lane() < HEAD_DIM
        top = lax.broadcasted_iota(jnp.int32, (LANES, 1), 0) < HEAD_DIM
        q2 = q_ref[...]
        zero = jnp.zeros_like(q2)
        qms = (jnp.where(low, q2, zero), jnp.where(low, zero, q2))

        def tile(k0, n_keys, off, carry, masked, queries=slice(0, t)):
            nq = queries.stop - queries.start
            kblk, vblk = k_ref[pl.ds(k0, n_keys), :], v_ref[pl.ds(k0, n_keys), :]
            valid = _tile_mask(n_keys, nq, off, window) if masked else None
            ones = jnp.ones_like(vblk)
            vs = tuple(jnp.where(_lane() == L_ROW[h], ones, vblk) for h in range(2))

            def scores(h):
                return lax.dot_general(kblk, qms[h][queries], _NT, preferred_element_type=f32)

            def softmax(h, sc):
                m = carry[h][0]
                if fox:
                    sc = sc - _wide(cb_ref[pl.ds(k0, n_keys), h * LANES:(h + 1) * LANES], nq)
                if masked:
                    sc = jnp.where(valid, sc, NEG)
                m_new = jnp.maximum(m, jnp.max(sc, axis=0, keepdims=True))
                return m_new, jnp.exp(m - m_new), jnp.exp(sc - m_new).astype(bf16)

            def update(h, m_new, alpha, p):
                return m_new, alpha * carry[h][1] + lax.dot_general(vs[h], p, _TN, preferred_element_type=f32)

            if window:
                return tuple(update(h, *softmax(h, scores(h))) for h in range(2))
            scs = [scores(h) for h in range(2)]
            stats = [softmax(h, scs[h]) for h in range(2)]
            return tuple(update(h, *stats[h]) for h in range(2))

        def start(nq):
            if has_sink:
                row = lax.broadcasted_iota(jnp.int32, (LANES, nq), 0)
                return tuple((_wide(sink_ref[h:h + 1, :], nq), (row == L_ROW[h]).astype(f32)) for h in range(2))
            return tuple((jnp.full((1, nq), NEG, f32), jnp.zeros((LANES, nq), f32)) for h in range(2))

        def finish(carry, queries):
            (m0, a0), (m1, a1) = carry
            l0, l1 = a0[L_ROW[0]:L_ROW[0] + 1, :], a1[L_ROW[1]:L_ROW[1] + 1, :]
            o_t = jnp.where(top, a0 * (1.0 / l0), a1 * (1.0 / l1))
            o_ref[queries, :] = o_t.T.astype(bf16)
            lse_ref[0:1, queries] = m0 + jnp.log(l0)
            lse_ref[1:2, queries] = m1 + jnp.log(l1)

        if window:
            for c in range(t // LANES):
                queries = slice(c * LANES, (c + 1) * LANES)
                q0 = i * t + c * LANES
                k0 = pl.multiple_of(jnp.maximum(q0 - window, 0), LANES)
                finish(tile(k0, LANES + window, q0 - k0, start(LANES), True, queries), queries)
        else:
            carry = lax.fori_loop(0, i, lambda kb, c: tile(pl.multiple_of(kb * t, t), t, 0, c, False), start(t))
            half, k_own = t // 2, pl.multiple_of(i * t, t)
            carry = tile(k_own, half, 0, carry, True)
            finish(tuple((m[:, :half], a[:, :half]) for m, a in carry), slice(0, half))
            carry = tuple((m[:, half:], a[:, half:]) for m, a in carry)
            finish(tile(pl.multiple_of(k_own + half, half), half, 0, carry, True, slice(half, t)), slice(half, t))

    q_spec = pl.BlockSpec((t, LANES), lambda j, i: (i, j))
    kv_spec = pl.BlockSpec((s, LANES), lambda j, i: (0, j))
    in_specs, args = [q_spec, kv_spec, kv_spec], [q, k, v]
    if fox:
        in_specs += [pl.BlockSpec((s, 2 * LANES), lambda j, i: (0, j))]
        args += [cum_b]
    if has_sink:
        in_specs += [pl.BlockSpec((None, 2, LANES), lambda j, i: (j, 0, 0))]
        args += [sink_rows.reshape(N_PAIRS, 2, LANES)]
    return pl.pallas_call(
        body, name=name, grid=(N_PAIRS, s // t), in_specs=in_specs,
        out_specs=[q_spec, pl.BlockSpec((None, 2, t), lambda j, i: (j, 0, i))],
        out_shape=[jax.ShapeDtypeStruct((s, N_PAIRS * LANES), bf16), jax.ShapeDtypeStruct((N_PAIRS, 2, s), f32)],
        compiler_params=_params(2),
    )(*args)


def _branch_dgrad_delta(db, w, o, name, *, lse=None, sink_rows=None, after=None):
    s, hw = o.shape
    tm = _row_tile(s, 2048)
    has_sink = sink_rows is not None
    extra = [] if after is None else [after]

    def body(*refs):
        db_ref, w_ref, o_ref = refs[:3]
        outs = refs[3 + (2 if has_sink else 0) + len(extra):]
        do_ref, dl_ref = outs[:2]
        if has_sink:
            lse_ref, sink_ref = refs[3:5]
            ds_ref = outs[2]

            @pl.when(pl.program_id(0) == 0)
            def _():
                ds_ref[...] = jnp.zeros_like(ds_ref)
        do = lax.dot_general(db_ref[...], w_ref[...], _NT, preferred_element_type=f32).astype(bf16)
        do_ref[...] = do
        for j in range(N_PAIRS):
            cols = slice(j * LANES, (j + 1) * LANES)
            prod_t = (do[:, cols].astype(f32) * o_ref[:, cols].astype(f32)).T
            for h in range(2):
                dl = jnp.sum(prod_t[h * HEAD_DIM:(h + 1) * HEAD_DIM, :], axis=0, keepdims=True)
                dl_ref[j, h:h + 1, :] = dl
                if has_sink:
                    r = 2 * j + h
                    p_sink = jnp.exp(sink_ref[r:r + 1, 0:1] - lse_ref[j, h:h + 1, :])
                    ds_ref[r:r + 1, :] += -jnp.sum(p_sink * dl, axis=1, keepdims=True)

    rows_spec = pl.BlockSpec((N_PAIRS, 2, tm), lambda i: (0, 0, i))
    in_specs = [_row_spec(tm, db.shape[1]), pl.BlockSpec(w.shape, lambda i: (0, 0)), _row_spec(tm, hw)]
    args = [db, w, o]
    out_specs = [_row_spec(tm, hw), rows_spec]
    out_shape = [jax.ShapeDtypeStruct((s, hw), bf16), jax.ShapeDtypeStruct((N_PAIRS, 2, s), f32)]
    if has_sink:
        in_specs += [rows_spec, _vec_spec(LANES, N_HEADS)]
        args += [lse, sink_rows]
        out_specs += [_vec_spec(LANES, N_HEADS)]
        out_shape += [jax.ShapeDtypeStruct((N_HEADS, LANES), f32)]
    return pl.pallas_call(
        body, name=name, grid=(s // tm,), in_specs=in_specs + [pl.BlockSpec(memory_space=pl.ANY)] * len(extra),
        out_specs=out_specs, out_shape=out_shape, compiler_params=_params(1),
    )(*args, *extra)


def _attn_bwd(q, k, v, do, lse, delta, name, *, cum_b=None, window=None, t=256):
    s = q.shape[0]
    t = _row_tile(s, t)
    nblk = s // t
    fox = cum_b is not None
    assert not window or (window % LANES == 0 and LANES + window <= s)

    def body(*refs):
        k_ref, v_ref, q_ref, do_ref, lse_ref, dl_ref = refs[:6]
        rest = list(refs[6:])
        cb_ref = rest.pop(0) if fox else None
        dq_ref, dk_ref, dv_ref = rest[:3]
        dcs_ref, rs_ref = (rest[3], rest[4]) if fox else (None, None)
        dk_acc, dv_acc = rest[-2:]
        b = pl.program_id(1)
        k0 = pl.multiple_of(b * t, t)

        @pl.when(b == 0)
        def _():
            dq_ref[...] = jnp.zeros_like(dq_ref)
            if fox:
                rs_ref[...] = jnp.zeros_like(rs_ref)

        dk_acc[...] = jnp.zeros_like(dk_acc)
        dv_acc[...] = jnp.zeros_like(dv_acc)
        if fox:
            dcs_ref[...] = jnp.zeros_like(dcs_ref)
        low = _lane() < HEAD_DIM
        top = lax.broadcasted_iota(jnp.int32, (LANES, 1), 0) < HEAD_DIM
        kblk, vblk = k_ref[...], v_ref[...]
        k_t = kblk.astype(f32).T.astype(bf16)
        cks = [_wide(cb_ref[pl.ds(k0, t), h * LANES:(h + 1) * LANES], t) for h in range(2)] if fox else None

        def tile(q0, n_queries, off, masked, keys=slice(0, t)):
            cols = pl.ds(q0, n_queries)
            q2, do2 = q_ref[cols, :], do_ref[cols, :]
            zero = jnp.zeros_like(q2)
            valid = _tile_mask(keys.stop - keys.start, n_queries, off, window) if masked else None
            dq_parts = []
            for h in range(2):
                qm = jnp.where(low, q2, zero) if h == 0 else jnp.where(low, zero, q2)
                dom = jnp.where(low, do2, zero) if h == 0 else jnp.where(low, zero, do2)
                sc = lax.dot_general(kblk[keys], qm, _NT, preferred_element_type=f32)
                if fox:
                    sc = sc - cks[h][keys, :n_queries]
                if masked:
                    sc = jnp.where(valid, sc, NEG)
                p = jnp.exp(sc - lse_ref[h:h + 1, cols])
                dp = lax.dot_general(vblk[keys], dom, _NT, preferred_element_type=f32)
                ds = p * (dp - dl_ref[h:h + 1, cols])
                pb, dsb = p.astype(bf16), ds.astype(bf16)
                dv_acc[keys, :] += jnp.dot(pb, dom, preferred_element_type=f32)
                dk_acc[keys, :] += jnp.dot(dsb, qm, preferred_element_type=f32)
                dq_parts.append(jnp.dot(k_t[:, keys], dsb, preferred_element_type=f32))
                if fox:
                    dcs_ref[keys, h * LANES:(h + 1) * LANES] += sum(ds[:, g * LANES:(g + 1) * LANES]
                                                                    for g in range(n_queries // LANES))
                    rs_ref[h:h + 1, cols] += jnp.sum(ds, axis=0, keepdims=True)
            dq_ref[:, cols] += jnp.where(top, dq_parts[0], dq_parts[1])

        def later_block(qb, carry):
            tile(pl.multiple_of(qb * t, t), t, 0, False)
            return carry

        if window:
            for c in range(t // LANES):
                first = b * t + c * LANES
                q0 = pl.multiple_of(jnp.minimum(first, s - (LANES + window)), LANES)
                tile(q0, LANES + window, q0 - first, True, slice(c * LANES, (c + 1) * LANES))
        else:
            half = t // 2
            tile(k0, half, 0, True, slice(0, half))
            tile(pl.multiple_of(k0 + half, half), half, half, True)
            lax.fori_loop(b + 1, nblk, later_block, 0)
        dk_ref[...] = dk_acc[...].astype(bf16)
        dv_ref[...] = dv_acc[...].astype(bf16)

    kv_spec = pl.BlockSpec((t, LANES), lambda j, b: (b, j))
    seq_spec = pl.BlockSpec((s, LANES), lambda j, b: (0, j))
    rows_spec = pl.BlockSpec((None, 2, s), lambda j, b: (j, 0, 0))
    hw = N_PAIRS * LANES
    in_specs, args = [kv_spec, kv_spec, seq_spec, seq_spec, rows_spec, rows_spec], [k, v, q, do, lse, delta]
    out_specs = [pl.BlockSpec((LANES, s), lambda j, b: (j, 0)), kv_spec, kv_spec]
    out_shape = [jax.ShapeDtypeStruct((hw, s), f32), jax.ShapeDtypeStruct((s, hw), bf16), jax.ShapeDtypeStruct((s, hw), bf16)]
    if fox:
        in_specs += [pl.BlockSpec((s, 2 * LANES), lambda j, b: (0, j))]
        args += [cum_b]
        out_specs += [pl.BlockSpec((t, 2 * LANES), lambda j, b: (b, j)), rows_spec]
        out_shape += [jax.ShapeDtypeStruct((s, N_HEADS * LANES), f32), jax.ShapeDtypeStruct((N_PAIRS, 2, s), f32)]
    return pl.pallas_call(
        body, name=name, grid=(N_PAIRS, nblk), in_specs=in_specs, out_specs=out_specs, out_shape=out_shape,
        scratch_shapes=[pltpu.VMEM((t, LANES), f32)] * 2, compiler_params=_params(2),
    )(*args)


def _branch_merge(o_a, o_b, w_a, w_b, gl, name):
    s, k = o_a.shape
    d = w_a.shape[1]
    tm = _row_tile(s, 1024)

    def body(oa_ref, ob_ref, wa_ref, wb_ref, g_ref, ba_ref, bb_ref, m_ref):
        ba = jnp.dot(oa_ref[...], wa_ref[...], preferred_element_type=f32)
        bb = jnp.dot(ob_ref[...], wb_ref[...], preferred_element_type=f32)
        g0, g1 = jax.nn.sigmoid(g_ref[:, :d].astype(f32)), jax.nn.sigmoid(g_ref[:, d:].astype(f32))
        ba_ref[...] = ba.astype(bf16)
        bb_ref[...] = bb.astype(bf16)
        m_ref[...] = (g0 * ba + g1 * bb).astype(bf16)

    whole = pl.BlockSpec((k, d), lambda i: (0, 0))
    return pl.pallas_call(
        body, name=name, grid=(s // tm,),
        in_specs=[_row_spec(tm, k), _row_spec(tm, k), whole, whole, _row_spec(tm, 2 * d)],
        out_specs=[_row_spec(tm, d)] * 3, out_shape=[jax.ShapeDtypeStruct((s, d), bf16)] * 3, compiler_params=_params(1),
    )(o_a, o_b, w_a, w_b, gl)


def _out_dgrad_merge_bwd(dy, w_out, ba, bb, gl, name):
    s, d = ba.shape
    tm = _row_tile(s, 1024)

    def body(dy_ref, w_ref, a_ref, b_ref, g_ref, da_ref, db_ref, dg_ref):
        dmv = lax.dot_general(dy_ref[...], w_ref[...], _NT, preferred_element_type=f32)
        g0, g1 = jax.nn.sigmoid(g_ref[:, :d].astype(f32)), jax.nn.sigmoid(g_ref[:, d:].astype(f32))
        da_ref[...] = (dmv * g0).astype(bf16)
        db_ref[...] = (dmv * g1).astype(bf16)
        dg_ref[:, :d] = (dmv * a_ref[...].astype(f32) * (g0 * (1.0 - g0))).astype(bf16)
        dg_ref[:, d:] = (dmv * b_ref[...].astype(f32) * (g1 * (1.0 - g1))).astype(bf16)

    return pl.pallas_call(
        body, name=name, grid=(s // tm,),
        in_specs=[_row_spec(tm, dy.shape[1]), pl.BlockSpec(w_out.shape, lambda i: (0, 0))] + [_row_spec(tm, d)] * 2
        + [_row_spec(tm, 2 * d)],
        out_specs=[_row_spec(tm, d)] * 2 + [_row_spec(tm, 2 * d)],
        out_shape=[jax.ShapeDtypeStruct((s, d), bf16)] * 2 + [jax.ShapeDtypeStruct((s, 2 * d), bf16)],
        compiler_params=_params(1),
    )(dy, w_out, ba, bb, gl)


GLU_TILE = 256


def _ffn_in_swiglu(h, w_t, name):
    s, d = h.shape
    f = w_t.shape[0] // 2
    tm = _row_tile(s, 4096)
    tg = GLU_TILE
    nb = f // tg

    def body(h_ref, wg_ref, wu_ref, g_ref, u_ref, act_ref):
        hv = h_ref[...]
        g = lax.dot_general(hv, wg_ref[...], _NT, preferred_element_type=f32)
        u = lax.dot_general(hv, wu_ref[...], _NT, preferred_element_type=f32)
        g_ref[...] = g.astype(bf16)
        u_ref[...] = u.astype(bf16)
        act_ref[...] = (g * jax.nn.sigmoid(g) * u).astype(bf16)

    col = pl.BlockSpec((tm, tg), lambda i, j: (i, j))
    return pl.pallas_call(
        body, name=name, grid=(s // tm, nb),
        in_specs=[pl.BlockSpec((tm, d), lambda i, j: (i, 0)), pl.BlockSpec((tg, d), lambda i, j: (j, 0)),
                  pl.BlockSpec((tg, d), lambda i, j: (j + nb, 0))],
        out_specs=[col] * 3, out_shape=[jax.ShapeDtypeStruct((s, f), bf16)] * 3, compiler_params=_params(2),
    )(h, w_t, w_t)


def _ffn_out_dgrad_swiglu(dy, w_out, g, u, name):
    s, d = dy.shape
    f = g.shape[1]
    tm = _row_tile(s, 4096)
    tg = GLU_TILE

    def body(dy_ref, w_ref, g_ref, u_ref, dg_ref, du_ref):
        dv = lax.dot_general(dy_ref[...], w_ref[...], _NT, preferred_element_type=f32)
        gv, uv = g_ref[...].astype(f32), u_ref[...].astype(f32)
        sg = jax.nn.sigmoid(gv)
        dg_ref[...] = (dv * uv * (sg * (1.0 + gv * (1.0 - sg)))).astype(bf16)
        du_ref[...] = (dv * (gv * sg)).astype(bf16)

    col = pl.BlockSpec((tm, tg), lambda i, j: (i, j))
    return pl.pallas_call(
        body, name=name, grid=(s // tm, f // tg),
        in_specs=[pl.BlockSpec((tm, d), lambda i, j: (i, 0)), pl.BlockSpec((tg, d), lambda i, j: (j, 0)), col, col],
        out_specs=[col] * 2, out_shape=[jax.ShapeDtypeStruct((s, f), bf16)] * 2, compiler_params=_params(2),
    )(dy, w_out, g, u)


def _wgrad_stack(parts, h, name):
    s, m = parts[0].shape
    d = h.shape[1]
    tm = 256
    nb = m // tm
    n = len(parts)

    def body(*refs):
        i = pl.program_id(0)
        for p in range(n):
            @pl.when(i // nb == p)
            def _(p=p):
                refs[n + 1][...] = lax.dot_general(refs[p][...], refs[n][...], _TN, preferred_element_type=f32).astype(bf16)

    a_specs = [pl.BlockSpec((s, tm), lambda i, p=p: (0, jnp.clip(i - p * nb, 0, nb - 1))) for p in range(n)]
    return pl.pallas_call(
        body, name=name, grid=(n * nb,), in_specs=a_specs + [pl.BlockSpec((s, d), lambda i: (0, 0))],
        out_specs=pl.BlockSpec((tm, d), lambda i: (i, 0)),
        out_shape=jax.ShapeDtypeStruct((n * m, d), bf16), compiler_params=_params(1),
    )(*parts, h)


def _ada_wgrad(c_all, d_all, name):
    n, d = c_all.shape
    w = d_all.shape[1]

    def body(c_ref, d_ref, o_ref):
        eye = (lax.broadcasted_iota(jnp.int32, (n, n), 0) == lax.broadcasted_iota(jnp.int32, (n, n), 1)).astype(f32)
        ct = lax.dot_general(c_ref[...], eye, _TN, precision=lax.Precision.HIGHEST, preferred_element_type=f32)
        g = ct[:, 0:1] * d_ref[0:1, :]
        for bi in range(1, n):
            g = g + ct[:, bi:bi + 1] * d_ref[bi:bi + 1, :]
        o_ref[0] = g

    return pl.pallas_call(
        body, name=name, out_shape=jax.ShapeDtypeStruct((1, d, w), f32), compiler_params=_params(),
    )(c_all, d_all)


def _adamw(parts, w, m, v, name, mine=None, me=None):
    r, c = w.shape
    n_parts = parts.shape[0]
    row_tiles = [t for t in range(min(r, 256), 0, -1) if r % t == 0 and (t % 16 == 0 or t == r)]
    if row_tiles:
        tr, tc = row_tiles[0], c
    else:
        tr, tc = r, next(t for t in (256, LANES) if c % t == 0)

    def body(*refs):
        w_ref, m_ref, v_ref, g_ref, d_ref, nm_ref, nv_ref = refs[-7:]
        if mine is None:
            p_ref, = refs[:-7]
        else:
            me_ref, p_ref, own_ref = refs[:-7]

        def part(i):
            if mine is None:
                return p_ref[i].astype(f32)
            return jnp.where(me_ref[0] == i, own_ref[...], p_ref[i]).astype(f32)

        g = part(0)
        for i in range(1, n_parts):
            g = g + part(i)
        mm = ADAM_B1 * m_ref[...] + (1.0 - ADAM_B1) * g
        vv = ADAM_B2 * v_ref[...] + (1.0 - ADAM_B2) * (g * g)
        m_hat = mm / (1.0 - ADAM_B1 ** ADAM_STEP)
        v_hat = vv / (1.0 - ADAM_B2 ** ADAM_STEP)
        g_ref[...] = g
        d_ref[...] = -ADAM_LR * (m_hat / (jnp.sqrt(v_hat) + ADAM_EPS) + ADAM_WD * w_ref[...])
        nm_ref[...] = mm
        nv_ref[...] = vv

    out_shape = [jax.ShapeDtypeStruct((r, c), f32)] * 4
    if mine is None:
        spec = pl.BlockSpec((tr, tc), lambda i, j: (i, j))
        return pl.pallas_call(
            body, name=name, grid=(r // tr, c // tc),
            in_specs=[pl.BlockSpec((n_parts, tr, tc), lambda i, j: (0, i, j))] + [spec] * 3,
            out_specs=[spec] * 4, out_shape=out_shape, compiler_params=_params(2),
        )(parts, w, m, v)
    spec = pl.BlockSpec((tr, tc), lambda i, j, me_ref: (i, j))
    return pl.pallas_call(
        body, name=name, out_shape=out_shape, compiler_params=_params(2),
        grid_spec=pltpu.PrefetchScalarGridSpec(
            num_scalar_prefetch=1, grid=(r // tr, c // tc),
            in_specs=[pl.BlockSpec((n_parts, tr, tc), lambda i, j, me_ref: (0, i, j)),
                      pl.BlockSpec((None, tr, tc), lambda i, j, me_ref: (me_ref[0], i, j))] + [spec] * 3,
            out_specs=[spec] * 4),
    )(me, parts, mine, w, m, v)


def _me():
    return lax.axis_index("x"), lax.axis_index("y"), lax.axis_index("c")


def _gather_prologue(c, w_ada, b_mine, w_in_t, name):
    n_dev, d = N_DEV, c.shape[1]
    ada_w = w_ada.shape[1]

    def body(c_ref, w_ref, b_ref, win_ref, call_ref, ada_ref, gin_ref, cols_ref, send_sems, recv_sems, local_sems):
        x, y, cc = _me()
        me, sibling = (x, y, cc), (x, y, 1 - cc)
        chips = [(1 - x, y), (x, 1 - y), (1 - x, 1 - y)]
        outs = (call_ref, ada_ref, gin_ref)

        def rows(a, dev):
            return outs[a].at[4 * dev[0] + 2 * dev[1] + dev[2]]

        def copy(a, k, block, to, src=None):
            return pltpu.make_async_remote_copy(
                src_ref=rows(a, block) if src is None else src, dst_ref=rows(a, block),
                send_sem=send_sems.at[a, k], recv_sem=recv_sems.at[a, k], device_id=to, device_id_type=MESH)

        def begin(a, src):
            own = pltpu.make_async_copy(src, rows(a, me), local_sems.at[a])
            sends = [copy(a, 0, me, sibling, src=src)] + [copy(a, 1 + j, me, (*chip, cc), src=src) for j, chip in enumerate(chips)]
            for cp in [own] + sends:
                cp.start()
            return own, sends

        def finish(a, own, sends):
            passed = []
            for j, chip in enumerate(chips):
                copy(a, 1 + j, (*chip, cc), me).wait_recv()
                passed.append(copy(a, 4 + j, (*chip, cc), sibling))
                passed[-1].start()
            copy(a, 0, sibling, me).wait_recv()
            for j, chip in enumerate(chips):
                copy(a, 4 + j, (*chip, 1 - cc), me).wait_recv()
            for cp in sends + passed:
                cp.wait_send()
            own.wait()

        finish(0, *begin(0, c_ref))
        cols_ref[...] = (jnp.dot(call_ref[:, 0, :].astype(bf16), w_ref[...].astype(bf16), preferred_element_type=f32)
                         + b_ref[...])
        finish(1, *begin(1, cols_ref))
        finish(2, *begin(2, win_ref))

    vmem, hbm = pl.BlockSpec(memory_space=pltpu.VMEM), pl.BlockSpec(memory_space=pl.ANY)
    return pl.pallas_call(
        body, name=name, in_specs=[vmem, vmem, vmem, hbm], out_specs=[vmem, vmem, hbm],
        out_shape=[jax.ShapeDtypeStruct((n_dev, 1, d), f32), jax.ShapeDtypeStruct((n_dev, n_dev, ada_w), f32),
                   jax.ShapeDtypeStruct((n_dev,) + w_in_t.shape, w_in_t.dtype)],
        scratch_shapes=[pltpu.VMEM((n_dev, ada_w), f32), pltpu.SemaphoreType.DMA((3, 7)), pltpu.SemaphoreType.DMA((3, 7)),
                        pltpu.SemaphoreType.DMA((3,))],
        compiler_params=pltpu.CompilerParams(vmem_limit_bytes=VMEM_LIMIT),
    )(c, w_ada, b_mine, w_in_t)


_FLIPS = ((0, 0, 1), (1, 0, 0), (0, 1, 0), (1, 1, 0), (1, 0, 1), (0, 1, 1), (1, 1, 1))
_HBM = pl.BlockSpec(memory_space=pltpu.HBM)
_SEM = pl.BlockSpec(memory_space=pltpu.SEMAPHORE)


def _exchange_copies(scatter, srcs, lands, send_sems, recv_sems):
    x, y, c = _me()
    me_row = 4 * x + 2 * y + c
    out = []
    for k, (fx, fy, fc) in enumerate(_FLIPS):
        peer = (x ^ fx, y ^ fy, c ^ fc)
        peer_row = 4 * peer[0] + 2 * peer[1] + peer[2]
        for a in range(len(srcs)):
            out.append(pltpu.make_async_remote_copy(
                src_ref=srcs[a].at[peer_row] if scatter else srcs[a], dst_ref=lands[a].at[me_row],
                send_sem=send_sems.at[7 * a + k], recv_sem=recv_sems.at[7 * a + k], device_id=peer, device_id_type=MESH))
    return out


def _own_copies(srcs, lands, own_sems):
    x, y, c = _me()
    return [pltpu.make_async_copy(srcs[a], lands[a].at[4 * x + 2 * y + c], own_sems.at[a]) for a in range(len(srcs))]


def _exchange_start(arrays, scatter, name, after=None):
    n = len(arrays)
    lands = [lax.empty(a.shape if scatter else (N_DEV,) + a.shape, a.dtype) for a in arrays]
    extra = [] if after is None else [after]

    def body(*refs):
        srcs, zones = refs[:n], refs[n:2 * n]
        send_sems, recv_sems, own_sems = refs[2 * n + len(extra):2 * n + len(extra) + 3]
        token = refs[-1]
        for cp in _exchange_copies(scatter, srcs, zones, send_sems, recv_sems):
            cp.start()
        for cp in [] if scatter else _own_copies(srcs, zones, own_sems):
            cp.start()
        token[...] = jnp.zeros_like(token)

    thru = [pltpu.HBM(a.shape, a.dtype) for a in list(arrays) + lands]
    outs = pl.pallas_call(
        body, name=name,
        out_shape=(pltpu.SemaphoreType.DMA((7 * n,)), pltpu.SemaphoreType.DMA((7 * n,)), pltpu.SemaphoreType.DMA((n,)), *thru,
                   jax.ShapeDtypeStruct((8, LANES), f32)),
        in_specs=[_HBM] * (2 * n) + [pl.BlockSpec(memory_space=pl.ANY)] * len(extra),
        out_specs=(_SEM, _SEM, _SEM, *[_HBM] * (2 * n), pl.BlockSpec(memory_space=pltpu.VMEM)),
        input_output_aliases={i: 3 + i for i in range(2 * n)},
        compiler_params=pltpu.CompilerParams(has_side_effects=pltpu.SideEffectType.DATAFLOW_SIDE_EFFECTING),
    )(*[pltpu.with_memory_space_constraint(a, pltpu.HBM) for a in list(arrays) + lands], *extra)
    return dict(n=n, scatter=scatter, sems=outs[:3], srcs=outs[3:3 + n], lands=outs[3 + n:3 + 2 * n], token=outs[-1])


def _exchange_wait(handle, after, name):
    n, scatter = handle["n"], handle["scatter"]

    def body(*refs):
        srcs, zones = refs[:n], refs[n:2 * n]
        send_sems, recv_sems, own_sems = refs[2 * n:2 * n + 3]
        for cp in _exchange_copies(scatter, srcs, zones, send_sems, recv_sems):
            cp.wait_send()
            cp.wait_recv()
        for cp in [] if scatter else _own_copies(srcs, zones, own_sems):
            cp.wait()

    thru = [pltpu.HBM(a.shape, a.dtype) for a in list(handle["srcs"]) + list(handle["lands"])]
    outs = pl.pallas_call(
        body, name=name, out_shape=tuple(thru),
        in_specs=[_HBM] * (2 * n) + [_SEM, _SEM, _SEM, pl.BlockSpec(memory_space=pl.ANY)], out_specs=tuple([_HBM] * (2 * n)),
        input_output_aliases={i: i for i in range(2 * n)},
        compiler_params=pltpu.CompilerParams(has_side_effects=pltpu.SideEffectType.DATAFLOW_SIDE_EFFECTING),
    )(*handle["srcs"], *handle["lands"], *handle["sems"], after)
    return list(outs[:n]), list(outs[n:])


def _cols_from_shards(g):
    return jnp.transpose(g, (1, 0, 2)).reshape(g.shape[1], -1)


def _shards_from_cols(a):
    return jnp.transpose(a.reshape(a.shape[0], N_DEV, -1), (1, 0, 2))


def _local_step(x, positions, ada, g_pre_mix, g_post_mix, b_f, sinks, g_pre_ffn, g_post_ffn, target,
                w_in_t, mix_weights, ffn_weights, on_grads):
    s, d = x.shape
    row = lambda v: v.reshape(1, -1)
    shift_m, scale_m, gate_m, shift_f, scale_f, gate_f = (ada[i:i + 1] for i in range(6))
    w_gate_t, w_qkv_t = w_in_t[F_OFF + N_HEADS:], w_in_t
    w_f_t = jnp.pad(w_in_t[F_OFF:F_OFF + N_HEADS], ((0, LANES - N_HEADS), (0, 0)))
    bf_row = jnp.pad(row(b_f), ((0, 0), (0, LANES - N_HEADS)))
    sink_rows = jnp.broadcast_to(sinks.reshape(N_HEADS, 1).astype(f32), (N_HEADS, LANES))
    inv_freq = 1.0 / (ROPE_THETA ** (jnp.arange(0, HEAD_DIM, 2, dtype=f32) / HEAD_DIM))
    cos, sin_s = _rope_tables(positions.reshape(s, 1), jnp.tile(inv_freq, 4).reshape(1, LANES), "rope_tables")

    h1, qa, ka, va, qb, kb, vb = _prenorm_proj_qkv(x, row(g_pre_mix), scale_m, shift_m, w_qkv_t, cos, sin_s, "prenorm_proj_qkv")
    gl = _matmul(h1, w_gate_t, "nt", bf16, "proj_gate")
    fl, cum_b = _forget_prep(h1, w_f_t, bf_row, "proj_forget_prep")
    o_a, lse_a = _attn_fwd(qa, ka, va, "swa_fwd", sink_rows=sink_rows, window=WINDOW, t=2048)
    o_b, lse_b = _attn_fwd(qb, kb, vb, "fox_fwd", cum_b=cum_b, t=1024)
    everything_before = (gl[:8, :LANES] + o_a[:8, :LANES] + o_b[:8, :LANES]).astype(f32)
    w_branch_a, w_branch_b, w_out = mix_weights(everything_before)
    ba, bb, merged = _branch_merge(o_a, o_b, w_branch_a, w_branch_b, gl, "branch_merge")
    y1, x2, h2 = _out_proj_postnorm_prenorm(merged, w_out, x, row(g_post_mix), gate_m, row(g_pre_ffn), scale_f, shift_f,
                                            "out_proj_norms")

    w_ffn_in_t, w_ffn_out = ffn_weights(h2)
    g_ff, u_ff, act = _ffn_in_swiglu(h2, w_ffn_in_t, "ffn_in_swiglu")
    loss_row, d_out, d_y2, vec_pf = _out_proj_loss_tail(act, w_ffn_out, x2, row(g_post_ffn), gate_f, target, "ffn_out_loss_tail")

    g_w_ffn_out = _matmul(act, d_y2, "tn", bf16, "ffn_out_wgrad")
    dg_ff, du_ff = _ffn_out_dgrad_swiglu(d_y2, w_ffn_out, g_ff, u_ff, "ffn_out_dgrad_swiglu")
    g_w_ffn_in_t = _wgrad_stack([dg_ff, du_ff], h2, "ffn_in_wgrad")
    sent = on_grads(dict(w_ffn_in=g_w_ffn_in_t, w_ffn_out=g_w_ffn_out))
    d_x2, vec_nf, d_y1, vec_pm = _dgrad_prenorm_bwd(
        [(dg_ff, w_ffn_in_t, 0), (du_ff, w_ffn_in_t, 1)], x2, row(g_pre_ffn), scale_f, d_out, "ffn_in_dgrad_norms_bwd",
        after=sent, below=(y1, row(g_post_mix), gate_m))

    g_w_out = _matmul(merged, d_y1, "tn", bf16, "out_proj_wgrad")
    d_ba, d_bb, dgl = _out_dgrad_merge_bwd(d_y1, w_out, ba, bb, gl, "out_proj_dgrad_merge_bwd")
    g_w_branch_a = _matmul(o_a, d_ba, "tn", bf16, "branch_a_wgrad")
    g_w_branch_b = _matmul(o_b, d_bb, "tn", bf16, "branch_b_wgrad")
    sent = on_grads(dict(w_out=g_w_out, w_branch_a=g_w_branch_a, w_branch_b=g_w_branch_b))
    d_oa, delta_a, d_sink = _branch_dgrad_delta(d_ba, w_branch_a, o_a, "branch_a_dgrad_delta", lse=lse_a,
                                                sink_rows=sink_rows, after=sent)
    d_ob, delta_b = _branch_dgrad_delta(d_bb, w_branch_b, o_b, "branch_b_dgrad_delta", after=sent)
    dqa_t, dka, dva = _attn_bwd(qa, ka, va, d_oa, lse_a, delta_a, "swa_bwd", window=WINDOW, t=2048)
    dqb_t, dkb, dvb, dcs, rs = _attn_bwd(qb, kb, vb, d_ob, lse_b, delta_b, "fox_bwd", cum_b=cum_b, t=512)
    dqkv = _qkv_prep_bwd(dqa_t, dka, dva, dqb_t, dkb, dvb, cos, sin_s, "qkv_prep_bwd")
    dfl, vec_bf = _forget_prep_bwd(rs.reshape(N_HEADS, s), dcs, fl, bf_row, "forget_prep_bwd")
    g_w_in_t = jnp.concatenate([_matmul(dqkv, h1, "tn", bf16, "qkv_wgrad"), _matmul(dfl, h1, "tn", bf16, "forget_wgrad")[:N_HEADS],
                                _matmul(dgl, h1, "tn", bf16, "gate_wgrad")], axis=0)
    sent = on_grads(dict(w_in=g_w_in_t))
    grad_x, vec_nm = _dgrad_prenorm_bwd([(dgl, w_gate_t, 0), (dqkv, w_qkv_t, 0), (dfl, w_f_t, 0)], x, row(g_pre_mix),
                                        scale_m, d_x2, "in_proj_dgrad_prenorm_bwd", after=sent)

    d_ada = jnp.concatenate([vec_nm[0], vec_nm[1], vec_pm[0], vec_nf[0], vec_nf[1], vec_pf[0]])
    small = dict(b_ada=d_ada, g_pre_mix=vec_nm[2], g_post_mix=vec_pm[1], g_pre_ffn=vec_nf[2], g_post_ffn=vec_pf[1],
                 b_f=vec_bf[0, :N_HEADS], sinks=d_sink[:, 0], loss=loss_row[0, :1])
    return grad_x, small


_SMALL = (("b_ada", 6144), ("g_pre_mix", 1024), ("g_post_mix", 1024), ("g_pre_ffn", 1024), ("g_post_ffn", 1024),
          ("b_f", 128), ("sinks", 128), ("loss", 128))
_SMALL_ROWS = 88


def _pack_small(vals):
    parts = [jnp.pad(vals[k].reshape(-1).astype(f32), (0, n - vals[k].size)) for k, n in _SMALL]
    flat = jnp.concatenate(parts)
    return jnp.pad(flat, (0, _SMALL_ROWS * LANES - flat.size)).reshape(_SMALL_ROWS, LANES)


def _unpack_small(slab, shapes):
    flat, out, off = slab.reshape(-1), {}, 0
    for k, n in _SMALL:
        size = math.prod(shapes[k])
        out[k] = flat[off:off + size].reshape(shapes[k])
        off += n
    return out


def kernel(x, c, positions, w_ada, b_ada, g_pre_mix, g_post_mix, w_in, b_f, sinks, w_branch_a, w_branch_b, w_out, g_pre_ffn, g_post_ffn, w_ffn_in, w_ffn_out, loss_target, m_w_ada, m_b_ada, m_g_pre_mix, m_g_post_mix, m_w_in, m_b_f, m_sinks, m_w_branch_a, m_w_branch_b, m_w_out, m_g_pre_ffn, m_g_post_ffn, m_w_ffn_in, m_w_ffn_out, v_w_ada, v_b_ada, v_g_pre_mix, v_g_post_mix, v_w_in, v_b_f, v_sinks, v_w_branch_a, v_w_branch_b, v_w_out, v_g_pre_ffn, v_g_post_ffn, v_w_ffn_in, v_w_ffn_out):
    xi, yi, ci = _me()
    me = 4 * xi + 2 * yi + ci
    d = D_MODEL
    ada_w = w_ada.shape[2]

    transposed = ("w_in", "w_ffn_in")
    tr = lambda a: jnp.transpose(a[0])

    b_mine = lax.dynamic_slice(b_ada, (0, me * ada_w), (1, ada_w))
    c_all, ada_all, g_in = _gather_prologue(c, w_ada[0], b_mine, tr(w_in).astype(bf16), "gather_prologue")
    c_all = c_all.reshape(N_DEV, d)
    ada = lax.dynamic_index_in_dim(ada_all, me, axis=1, keepdims=False).reshape(6, d)
    late_mix = [w.astype(bf16) for w in (w_branch_a[0], w_branch_b[0], w_out[0])]
    late_ffn = [w.astype(bf16) for w in (tr(w_ffn_in), w_ffn_out[0])]
    mix_h = _exchange_start(late_mix, False, "gather_mix_start", after=g_in)
    ffn_h = _exchange_start(late_ffn, False, "gather_ffn_start", after=mix_h["token"])

    def rows_from_shards(g):
        return g.reshape(g.shape[0] * g.shape[1], g.shape[2])

    def mix_weights(after):
        _, (g_ba, g_bb, g_out) = _exchange_wait(mix_h, after, "gather_mix_wait")
        return _cols_from_shards(g_ba), _cols_from_shards(g_bb), rows_from_shards(g_out)

    def ffn_weights(after):
        _, (g_fi, g_fo) = _exchange_wait(ffn_h, after, "gather_ffn_wait")
        return rows_from_shards(g_fi), rows_from_shards(g_fo)

    row_sharded = ("w_out", "w_ffn_out") + transposed
    in_flight = []

    def on_grads(group):
        sends = [g.reshape(N_DEV, g.shape[0] // N_DEV, g.shape[1]) if nm in row_sharded else _shards_from_cols(g)
                 for nm, g in group.items()]
        handle = _exchange_start(sends, True, "scatter_start_%d" % len(in_flight))
        in_flight.append((list(group), handle))
        return handle["token"]

    grad_x, small = _local_step(
        x[0], positions[0], ada + ffn_h["token"][0, 0], g_pre_mix[0], g_post_mix[0], b_f[0], sinks[0], g_pre_ffn[0],
        g_post_ffn[0], loss_target[0], rows_from_shards(g_in), mix_weights, ffn_weights, on_grads)

    ws = dict(w_in=(w_in, m_w_in, v_w_in), w_branch_a=(w_branch_a, m_w_branch_a, v_w_branch_a),
              w_branch_b=(w_branch_b, m_w_branch_b, v_w_branch_b), w_out=(w_out, m_w_out, v_w_out),
              w_ffn_in=(w_ffn_in, m_w_ffn_in, v_w_ffn_in), w_ffn_out=(w_ffn_out, m_w_ffn_out, v_w_ffn_out))
    res = {}

    def finish_group(gi, after):
        names, handle = in_flight[gi]
        sends, zones = _exchange_wait(handle, after, "scatter_wait_%d" % gi)
        for nm, zone, sent in zip(names, zones, sends):
            w, m, v = (tr(a) if nm in transposed else a[0] for a in ws[nm])
            out = _adamw(zone, w, m, v, "adamw_" + nm, mine=sent, me=me.reshape(1).astype(jnp.int32))
            after = out[0]
            res[nm] = [jnp.transpose(o) for o in out] if nm in transposed else out
        return after

    small_h = _exchange_start([_pack_small(small)], False, "gather_small_start", after=grad_x)
    done = finish_group(1, finish_group(0, small_h["token"]))
    _, (slab_all,) = _exchange_wait(small_h, done, "gather_small_wait")
    small_w = dict(b_ada=b_ada, g_pre_mix=g_pre_mix, g_post_mix=g_post_mix, g_pre_ffn=g_pre_ffn, g_post_ffn=g_post_ffn,
                   b_f=b_f, sinks=sinks, loss=jnp.zeros((1,), f32))
    small_m = dict(b_ada=m_b_ada, g_pre_mix=m_g_pre_mix, g_post_mix=m_g_post_mix, g_pre_ffn=m_g_pre_ffn,
                   g_post_ffn=m_g_post_ffn, b_f=m_b_f, sinks=m_sinks, loss=jnp.zeros((1,), f32))
    small_v = dict(b_ada=v_b_ada, g_pre_mix=v_g_pre_mix, g_post_mix=v_g_post_mix, g_pre_ffn=v_g_pre_ffn,
                   g_post_ffn=v_g_post_ffn, b_f=v_b_f, sinks=v_sinks, loss=jnp.ones((1,), f32))
    shapes = {k: small_w[k].shape for k, _ in _SMALL}
    s_out = _adamw(slab_all, _pack_small(small_w), _pack_small(small_m), _pack_small(small_v), "adamw_small")
    s_grad, s_delta, s_m, s_v = (_unpack_small(o, shapes) for o in s_out)

    d_ada_all = lax.dynamic_slice(slab_all[:, :6144 // LANES, :].reshape(N_DEV, 6144), (0, me * ada_w), (N_DEV, ada_w))
    ada_parts = _ada_wgrad(c_all, d_ada_all, "ada_wgrad")

    res["w_ada"] = _adamw(ada_parts, w_ada[0], m_w_ada[0], v_w_ada[0], "adamw_w_ada")
    finish_group(2, res["w_ada"][0])

    order = ["w_ada", "b_ada", "g_pre_mix", "g_post_mix", "w_in", "b_f", "sinks", "w_branch_a", "w_branch_b", "w_out",
             "g_pre_ffn", "g_post_ffn", "w_ffn_in", "w_ffn_out"]
    outs = [s_grad["loss"].reshape(()), grad_x[None]]
    for which, small_o in enumerate((s_grad, s_delta, s_m, s_v)):
        for nm in order:
            outs.append(res[nm][which][None] if nm in res else small_o[nm])
    return tuple(outs)
```

```python
import math

import jax
import jax.numpy as jnp
from jax import lax
from jax.experimental import pallas as pl
from jax.experimental.pallas import tpu as pltpu

f32 = jnp.float32
bf16 = jnp.bfloat16

D_MODEL = 1024
HEAD_DIM = 64
N_HEADS = 8
N_PAIRS = 4
QKV_W = 2304
F_OFF = 2304
WINDOW = 128
ROPE_THETA = 10000.0
RMS_EPS = 1e-6
N_DEV = 8
ADAM_LR, ADAM_B1, ADAM_B2, ADAM_EPS, ADAM_WD, ADAM_STEP = 0.001, 0.9, 0.999, 1e-08, 0.01, 10
NEG = -1e30
L_ROW = (HEAD_DIM, 0)
LANES = 128
VMEM_LIMIT = 48 * 1024 * 1024
MESH = pl.DeviceIdType.MESH

_NT = (((1,), (1,)), ((), ()))
_TN = (((0,), (0,)), ((), ()))


def _params(n_grid=0):
    sem = ("arbitrary",) * n_grid if n_grid else None
    return pltpu.CompilerParams(dimension_semantics=sem, vmem_limit_bytes=VMEM_LIMIT)


def _row_tile(s, want):
    t = min(s, want)
    assert s % t == 0, (s, t)
    return t


MATMUL_VMEM_BUDGET = 40 * 1024 * 1024


def _matmul_tiles(m, n, k, a_item, b_item, o_item):
    def tiles(d):
        return [t for t in range(LANES, min(d, 2048) + 1, LANES) if d % t == 0] or [d]

    best = None
    for tm in tiles(m):
        for tn in tiles(n):
            vmem = 2 * (tm * k * a_item + tn * k * b_item + tm * tn * o_item) + tm * tn * 4
            if vmem > MATMUL_VMEM_BUDGET:
                continue
            traffic = m * k * a_item + n * k * b_item * (1 if tn == n else m // tm) + m * n * o_item
            steps = (m // tm) * (n // tn)
            key = (traffic, 0, steps) if steps >= 4 else (traffic, 1, -steps)
            if best is None or key < best[0]:
                best = (key, tm, tn)
    assert best is not None, (m, n, k)
    return best[1], best[2]


def _matmul(a, b, mode, out_dtype, name, after=None):
    if mode == "nn":
        (m, k), n = a.shape, b.shape[1]
    elif mode == "nt":
        (m, k), n = a.shape, b.shape[0]
    else:
        (k, m), n = a.shape, b.shape[1]
    tm, tn = _matmul_tiles(m, n, k, a.dtype.itemsize, b.dtype.itemsize, jnp.dtype(out_dtype).itemsize)
    if mode == "nn":
        a_spec, b_spec, dims = pl.BlockSpec((tm, k), lambda i, j: (i, 0)), pl.BlockSpec((k, tn), lambda i, j: (0, j)), None
    elif mode == "nt":
        a_spec, b_spec, dims = pl.BlockSpec((tm, k), lambda i, j: (i, 0)), pl.BlockSpec((tn, k), lambda i, j: (j, 0)), _NT
    else:
        a_spec, b_spec, dims = pl.BlockSpec((k, tm), lambda i, j: (0, i)), pl.BlockSpec((k, tn), lambda i, j: (0, j)), _TN

    def body(a_ref, b_ref, *rest):
        o_ref = rest[-1]
        av, bv = a_ref[...].astype(bf16), b_ref[...].astype(bf16)
        if dims is None:
            r = jnp.dot(av, bv, preferred_element_type=f32)
        else:
            r = lax.dot_general(av, bv, dims, preferred_element_type=f32)
        o_ref[...] = r.astype(out_dtype)

    extra = [] if after is None else [after]
    return pl.pallas_call(
        body, name=name, grid=(m // tm, n // tn), in_specs=[a_spec, b_spec] + [pl.BlockSpec(memory_space=pl.ANY)] * len(extra),
        out_specs=pl.BlockSpec((tm, tn), lambda i, j: (i, j)),
        out_shape=jax.ShapeDtypeStruct((m, n), out_dtype), compiler_params=_params(2),
    )(a, b, *extra)


def _rstd(v):
    return lax.rsqrt(jnp.mean(v * v, axis=-1, keepdims=True) + RMS_EPS)


def _row_spec(tm, d):
    return pl.BlockSpec((tm, d), lambda i: (i, 0))


def _vec_spec(d, rows=1):
    return pl.BlockSpec((rows, d), lambda i: (0, 0))


def _proj_spec(a, w, tm):
    return [_row_spec(tm, a.shape[1]), pl.BlockSpec(w.shape, lambda i: (0, 0))]


def _out_proj_postnorm_prenorm(a, w, x, g_post, gate, g_pre, scale, shift, name):
    s, d = x.shape
    tm = _row_tile(s, 512)

    def body(a_ref, w_ref, x_ref, gp_ref, gate_ref, g_ref, sc_ref, sh_ref, y_ref, x2_ref, h_ref):
        yv = jnp.dot(a_ref[...], w_ref[...], preferred_element_type=f32)
        y_ref[...] = yv
        x2 = x_ref[...] + gate_ref[...] * (yv * _rstd(yv) * gp_ref[...])
        x2_ref[...] = x2
        h_ref[...] = ((x2 * _rstd(x2) * g_ref[...]) * (1.0 + sc_ref[...]) + sh_ref[...]).astype(bf16)

    return pl.pallas_call(
        body, name=name, grid=(s // tm,), in_specs=_proj_spec(a, w, tm) + [_row_spec(tm, d)] + [_vec_spec(d)] * 5,
        out_specs=[_row_spec(tm, d)] * 3,
        out_shape=[jax.ShapeDtypeStruct((s, d), f32)] * 2 + [jax.ShapeDtypeStruct((s, d), bf16)], compiler_params=_params(1),
    )(a, w, x, g_post, gate, g_pre, scale, shift)


def _rms_bwd(u, v, r):
    return r * u - v * (r * r * r) * jnp.mean(u * v, axis=-1, keepdims=True)


def _out_proj_loss_tail(a, w, x, g, gate, target, name):
    s, d = x.shape
    tm = _row_tile(s, 512)

    def body(a_ref, w_ref, x_ref, g_ref, gate_ref, t_ref, loss_ref, do_ref, dy_ref, vec_ref):
        @pl.when(pl.program_id(0) == 0)
        def _():
            loss_ref[...] = jnp.zeros_like(loss_ref)
            vec_ref[...] = jnp.zeros_like(vec_ref)
        yv = jnp.dot(a_ref[...], w_ref[...], preferred_element_type=f32)
        r = _rstd(yv)
        yn = yv * r
        err = x_ref[...] + gate_ref[...] * (yn * g_ref[...]) - t_ref[...]
        loss_ref[...] += 0.5 * jnp.sum(jnp.mean(err * err, axis=-1, keepdims=True), axis=0, keepdims=True)
        dr = err / d
        do_ref[...] = dr
        dn = dr * gate_ref[...]
        vec_ref[0:1, :] += jnp.sum(dr * (yn * g_ref[...]), axis=0, keepdims=True)
        vec_ref[1:2, :] += jnp.sum(dn * yn, axis=0, keepdims=True)
        dy_ref[...] = _rms_bwd(dn * g_ref[...], yv, r).astype(bf16)

    return pl.pallas_call(
        body, name=name, grid=(s // tm,),
        in_specs=_proj_spec(a, w, tm) + [_row_spec(tm, d)] + [_vec_spec(d)] * 2 + [_row_spec(tm, d)],
        out_specs=[_vec_spec(LANES), _row_spec(tm, d), _row_spec(tm, d), _vec_spec(d, 8)],
        out_shape=[jax.ShapeDtypeStruct((1, LANES), f32), jax.ShapeDtypeStruct((s, d), f32),
                   jax.ShapeDtypeStruct((s, d), bf16), jax.ShapeDtypeStruct((8, d), f32)],
        compiler_params=_params(1),
    )(a, w, x, g, gate, target)


def _dgrad_prenorm_bwd(terms, x, g, scale, dres, name, after=None, below=None):
    s, d = x.shape
    n = len(terms)
    k = sum(a.shape[1] for a, _, _ in terms)
    row_bytes = 2 * (2 * k) + d * (4 + 2 * 4 * 3 + (2 * 4 + 2 * 2 if below else 0))
    tm = next(t for t in (512, 256, 128) if s % t == 0 and 2 * k * d + t * row_bytes <= MATMUL_VMEM_BUDGET)
    extra = [] if after is None else [after]

    def body(*refs):
        a_refs, b_refs = refs[:n], refs[n:2 * n]
        x_ref, g_ref, sc_ref, dr_ref = refs[2 * n:2 * n + 4]
        n_in = 2 * n + 4 + (3 if below else 0) + len(extra)
        dx_ref, vec_ref = refs[n_in], refs[n_in + 1]
        if below:
            y_ref, gp_ref, gate_ref = refs[2 * n + 4:2 * n + 7]
            dy_ref, vec2_ref = refs[n_in + 2], refs[n_in + 3]

        @pl.when(pl.program_id(0) == 0)
        def _():
            vec_ref[...] = jnp.zeros_like(vec_ref)
            if below:
                vec2_ref[...] = jnp.zeros_like(vec2_ref)
        dhv = jnp.dot(a_refs[0][...], b_refs[0][...], preferred_element_type=f32)
        for i in range(1, n):
            dhv = dhv + jnp.dot(a_refs[i][...], b_refs[i][...], preferred_element_type=f32)
        xv = x_ref[...]
        r = _rstd(xv)
        xn = xv * r
        dn = dhv * (1.0 + sc_ref[...])
        vec_ref[0:1, :] += jnp.sum(dhv, axis=0, keepdims=True)
        vec_ref[1:2, :] += jnp.sum(dhv * (xn * g_ref[...]), axis=0, keepdims=True)
        vec_ref[2:3, :] += jnp.sum(dn * xn, axis=0, keepdims=True)
        dx = dr_ref[...] + _rms_bwd(dn * g_ref[...], xv, r)
        dx_ref[...] = dx
        if below:
            yv = y_ref[...]
            ry = _rstd(yv)
            yn = yv * ry
            dny = dx * gate_ref[...]
            vec2_ref[0:1, :] += jnp.sum(dx * (yn * gp_ref[...]), axis=0, keepdims=True)
            vec2_ref[1:2, :] += jnp.sum(dny * yn, axis=0, keepdims=True)
            dy_ref[...] = _rms_bwd(dny * gp_ref[...], yv, ry).astype(bf16)

    in_specs = ([_row_spec(tm, a.shape[1]) for a, _, _ in terms]
                + [pl.BlockSpec((a.shape[1], d), lambda i, r=r: (r, 0), pipeline_mode=pl.Buffered(1))
                   for a, _, r in terms]
                + [_row_spec(tm, d)] + [_vec_spec(d)] * 2 + [_row_spec(tm, d)])
    out_specs = [_row_spec(tm, d), _vec_spec(d, 8)]
    out_shape = [jax.ShapeDtypeStruct((s, d), f32), jax.ShapeDtypeStruct((8, d), f32)]
    args = [a for a, _, _ in terms] + [b for _, b, _ in terms] + [x, g, scale, dres]
    if below:
        in_specs += [_row_spec(tm, d)] + [_vec_spec(d)] * 2
        out_specs += [_row_spec(tm, d), _vec_spec(d, 8)]
        out_shape += [jax.ShapeDtypeStruct((s, d), bf16), jax.ShapeDtypeStruct((8, d), f32)]
        args += list(below)
    return pl.pallas_call(
        body, name=name, grid=(s // tm,), in_specs=in_specs + [pl.BlockSpec(memory_space=pl.ANY)] * len(extra),
        out_specs=out_specs, out_shape=out_shape, compiler_params=_params(1),
    )(*args, *extra)


def _lane():
    return lax.broadcasted_iota(jnp.int32, (1, LANES), 1)


def _rope_tables(pos_col, inv_freq, name):
    s = pos_col.shape[0]

    def body(p_ref, f_ref, cos_ref, sin_ref):
        ang = p_ref[...].astype(f32) * f_ref[...]
        first_half = (_lane() % HEAD_DIM) < HEAD_DIM // 2
        cos_ref[...] = jnp.cos(ang)
        sn = jnp.sin(ang)
        sin_ref[...] = jnp.where(first_half, -sn, sn)

    return pl.pallas_call(
        body, name=name, out_shape=[jax.ShapeDtypeStruct((s, LANES), f32)] * 2, compiler_params=_params(),
    )(pos_col, inv_freq)


def _swap_halves(v):
    first_half = (_lane() % HEAD_DIM) < HEAD_DIM // 2
    return jnp.where(first_half, pltpu.roll(v, LANES - HEAD_DIM // 2, axis=1), pltpu.roll(v, HEAD_DIM // 2, axis=1))


def _prenorm_proj_qkv(x, g, mod_scale, mod_shift, w_qkv_t, cos, sin_s, name):
    s, d = x.shape
    tm = _row_tile(s, 512)
    scale = 1.0 / math.sqrt(HEAD_DIM)

    def body(x_ref, g_ref, msc_ref, msh_ref, w_ref, c_ref, s_ref, h_ref, qa_ref, ka_ref, va_ref, qb_ref, kb_ref, vb_ref):
        xv = x_ref[...]
        h = ((xv * _rstd(xv) * g_ref[...]) * (1.0 + msc_ref[...]) + msh_ref[...]).astype(bf16)
        h_ref[...] = h
        proj = lax.dot_general(h, w_ref[...], _NT, preferred_element_type=f32)
        cs, sn = c_ref[...], s_ref[...]
        low = _lane() < HEAD_DIM

        def blk(j):
            return proj[:, j * LANES:(j + 1) * LANES]

        def rope(v):
            return v * cs + _swap_halves(v) * sn

        def expand(v):
            other = pltpu.roll(v, HEAD_DIM, axis=1)
            return jnp.where(low, v, other), jnp.where(low, other, v)

        for j in range(N_PAIRS):
            qa_ref[:, j * LANES:(j + 1) * LANES] = (rope(blk(j)) * scale).astype(bf16)
            qb_ref[:, j * LANES:(j + 1) * LANES] = (blk(6 + j) * scale).astype(bf16)
            kb_ref[:, j * LANES:(j + 1) * LANES] = blk(10 + j).astype(bf16)
            vb_ref[:, j * LANES:(j + 1) * LANES] = blk(14 + j).astype(bf16)
        k0, k1 = expand(rope(blk(4)))
        v0, v1 = expand(blk(5))
        for j in range(N_PAIRS):
            ka_ref[:, j * LANES:(j + 1) * LANES] = (k0 if j < 2 else k1).astype(bf16)
            va_ref[:, j * LANES:(j + 1) * LANES] = (v0 if j < 2 else v1).astype(bf16)

    hw = N_PAIRS * LANES
    return pl.pallas_call(
        body, name=name, grid=(s // tm,),
        in_specs=[_row_spec(tm, d)] + [_vec_spec(d)] * 3
        + [pl.BlockSpec((QKV_W, d), lambda i: (0, 0)), _row_spec(tm, LANES), _row_spec(tm, LANES)],
        out_specs=[_row_spec(tm, d)] + [_row_spec(tm, hw)] * 6,
        out_shape=[jax.ShapeDtypeStruct((s, d), bf16)] + [jax.ShapeDtypeStruct((s, hw), bf16)] * 6, compiler_params=_params(1),
    )(x, g, mod_scale, mod_shift, w_qkv_t, cos, sin_s)


def _qkv_prep_bwd(dqa_t, dka, dva, dqb_t, dkb, dvb, cos, sin_s, name):
    s = dka.shape[0]
    tm = _row_tile(s, 512)
    scale = 1.0 / math.sqrt(HEAD_DIM)
    hw = N_PAIRS * LANES
    t_spec = pl.BlockSpec((hw, tm), lambda i: (0, i))

    def body(dqa_ref, dka_ref, dva_ref, dqb_ref, dkb_ref, dvb_ref, c_ref, s_ref, o_ref):
        cs, sn = c_ref[...], s_ref[...]
        low = _lane() < HEAD_DIM

        def blk(ref, j):
            return ref[:, j * LANES:(j + 1) * LANES].astype(f32)

        def blk_t(ref, j):
            return ref[j * LANES:(j + 1) * LANES, :].T

        def unrope(v):
            return v * cs + _swap_halves(v * sn)

        def fold(ref):
            a, b = blk(ref, 0) + blk(ref, 1), blk(ref, 2) + blk(ref, 3)
            kv0 = a + pltpu.roll(a, HEAD_DIM, axis=1)
            kv1 = b + pltpu.roll(b, HEAD_DIM, axis=1)
            return jnp.where(low, kv0, kv1)

        for j in range(N_PAIRS):
            o_ref[:, j * LANES:(j + 1) * LANES] = (unrope(blk_t(dqa_ref, j)) * scale).astype(bf16)
            o_ref[:, (6 + j) * LANES:(7 + j) * LANES] = (blk_t(dqb_ref, j) * scale).astype(bf16)
            o_ref[:, (10 + j) * LANES:(11 + j) * LANES] = blk(dkb_ref, j).astype(bf16)
            o_ref[:, (14 + j) * LANES:(15 + j) * LANES] = blk(dvb_ref, j).astype(bf16)
        o_ref[:, 4 * LANES:5 * LANES] = unrope(fold(dka_ref)).astype(bf16)
        o_ref[:, 5 * LANES:6 * LANES] = fold(dva_ref).astype(bf16)

    return pl.pallas_call(
        body, name=name, grid=(s // tm,),
        in_specs=[t_spec, _row_spec(tm, hw), _row_spec(tm, hw), t_spec, _row_spec(tm, hw), _row_spec(tm, hw)] + [_row_spec(tm, LANES)] * 2,
        out_specs=_row_spec(tm, QKV_W), out_shape=jax.ShapeDtypeStruct((s, QKV_W), bf16), compiler_params=_params(1),
    )(dqa_t, dka, dva, dqb_t, dkb, dvb, cos, sin_s)


def _cumsum_rows(v, reverse=False):
    n = v.shape[0]
    row = lax.broadcasted_iota(jnp.int32, v.shape, 0)
    sh = 1
    while sh < n:
        if reverse:
            v = v + jnp.where(row < n - sh, pltpu.roll(v, n - sh, axis=0), 0.0)
        else:
            v = v + jnp.where(row >= sh, pltpu.roll(v, sh, axis=0), 0.0)
        sh *= 2
    return v


def _log_sigmoid(z):
    return jnp.minimum(z, 0.0) - jnp.log1p(jnp.exp(-jnp.abs(z)))


def _forget_prep(h, w_f_t, bf_row, name):
    s, d = h.shape
    tm = _row_tile(s, 1024)

    def body(h_ref, w_ref, b_ref, f_ref, cb_ref, last_ref):
        @pl.when(pl.program_id(0) == 0)
        def _():
            last_ref[...] = jnp.zeros_like(last_ref)
        fl = lax.dot_general(h_ref[...], w_ref[...], _NT, preferred_element_type=f32)
        f_ref[...] = fl
        cum = _cumsum_rows(_log_sigmoid(fl + b_ref[...])) + last_ref[0:1, :]
        last_ref[0:1, :] = cum[tm - 1:tm, :]
        for hd in range(N_HEADS):
            cb_ref[:, hd * LANES:(hd + 1) * LANES] = jnp.broadcast_to(cum[:, hd:hd + 1], (tm, LANES))

    return pl.pallas_call(
        body, name=name, grid=(s // tm,),
        in_specs=[_row_spec(tm, d), pl.BlockSpec((LANES, d), lambda i: (0, 0)), _vec_spec(LANES)],
        out_specs=[_row_spec(tm, LANES), _row_spec(tm, N_HEADS * LANES)],
        out_shape=[jax.ShapeDtypeStruct((s, LANES), f32), jax.ShapeDtypeStruct((s, N_HEADS * LANES), f32)],
        scratch_shapes=[pltpu.VMEM((8, LANES), f32)], compiler_params=_params(1),
    )(h, w_f_t, bf_row)


def _forget_prep_bwd(rs, dcs, fl, bf_row, name):
    s = fl.shape[0]
    tm = _row_tile(s, 1024)
    n = s // tm

    def body(r_ref, c_ref, f_ref, b_ref, df_ref, db_ref, next_ref):
        @pl.when(pl.program_id(0) == 0)
        def _():
            next_ref[...] = jnp.zeros_like(next_ref)
            db_ref[...] = jnp.zeros_like(db_ref)
        eye = (lax.broadcasted_iota(jnp.int32, (N_HEADS, LANES), 0) == lax.broadcasted_iota(jnp.int32, (N_HEADS, LANES), 1)).astype(f32)
        dcum = lax.dot_general(r_ref[...], eye, _TN, precision=lax.Precision.HIGHEST, preferred_element_type=f32)
        for h in range(N_HEADS):
            dcum = dcum - jnp.where(_lane() == h, jnp.sum(c_ref[:, h * LANES:(h + 1) * LANES], axis=1, keepdims=True), 0.0)
        dlf = _cumsum_rows(dcum, reverse=True) + next_ref[0:1, :]
        next_ref[0:1, :] = dlf[0:1, :]
        z = f_ref[...] + b_ref[...]
        df = jnp.where(_lane() < N_HEADS, dlf * jax.nn.sigmoid(-z), 0.0)
        df_ref[...] = df.astype(bf16)
        db_ref[0:1, :] += jnp.sum(df, axis=0, keepdims=True)

    def rows(width):
        return pl.BlockSpec((tm, width), lambda i: (n - 1 - i, 0))

    return pl.pallas_call(
        body, name=name, grid=(n,),
        in_specs=[pl.BlockSpec((N_HEADS, tm), lambda i: (0, n - 1 - i)), rows(N_HEADS * LANES), rows(LANES), _vec_spec(LANES)],
        out_specs=[rows(LANES), _vec_spec(LANES, 8)],
        out_shape=[jax.ShapeDtypeStruct((s, LANES), bf16), jax.ShapeDtypeStruct((8, LANES), f32)],
        scratch_shapes=[pltpu.VMEM((8, LANES), f32)], compiler_params=_params(1),
    )(rs, dcs, fl, bf_row)


def _tile_mask(n_keys, n_queries, off, window):
    shape = (n_keys, n_queries)
    d = lax.broadcasted_iota(jnp.int32, shape, 1) - lax.broadcasted_iota(jnp.int32, shape, 0) + off
    valid = d >= 0
    return jnp.logical_and(valid, d < window) if window else valid


def _wide(v, t):
    return jnp.concatenate([v] * (t // LANES), axis=1)


def _attn_fwd(q, k, v, name, *, cum_b=None, sink_rows=None, window=None, t=256):
    s = q.shape[0]
    t = _row_tile(s, t)
    fox, has_sink = cum_b is not None, sink_rows is not None
    assert not window or (window % LANES == 0 and LANES + window <= s)

    def body(*refs):
        q_ref, k_ref, v_ref = refs[:3]
        rest = list(refs[3:])
        cb_ref = rest.pop(0) if fox else None
        sink_ref = rest.pop(0) if has_sink else None
        o_ref, lse_ref = rest
        i = pl.program_id(1)
        low = _lane() < HEAD_DIM
        top = lax.broadcasted_iota(jnp.int32, (LANES, 1), 0) < HEAD_DIM
        q2 = q_ref[...]
        zero = jnp.zeros_like(q2)
        qms = (jnp.where(low, q2, zero), jnp.where(low, zero, q2))

        def tile(k0, n_keys, off, carry, masked, queries=slice(0, t)):
            nq = queries.stop - queries.start
            kblk, vblk = k_ref[pl.ds(k0, n_keys), :], v_ref[pl.ds(k0, n_keys), :]
            valid = _tile_mask(n_keys, nq, off, window) if masked else None
            ones = jnp.ones_like(vblk)
            vs = tuple(jnp.where(_lane() == L_ROW[h], ones, vblk) for h in range(2))

            def scores(h):
                return lax.dot_general(kblk, qms[h][queries], _NT, preferred_element_type=f32)

            def softmax(h, sc):
                m = carry[h][0]
                if fox:
                    sc = sc - _wide(cb_ref[pl.ds(k0, n_keys), h * LANES:(h + 1) * LANES], nq)
                if masked:
                    sc = jnp.where(valid, sc, NEG)
                m_new = jnp.maximum(m, jnp.max(sc, axis=0, keepdims=True))
                return m_new, jnp.exp(m - m_new), jnp.exp(sc - m_new).astype(bf16)

            def update(h, m_new, alpha, p):
                return m_new, alpha * carry[h][1] + lax.dot_general(vs[h], p, _TN, preferred_element_type=f32)

            if window:
                return tuple(update(h, *softmax(h, scores(h))) for h in range(2))
            scs = [scores(h) for h in range(2)]
            stats = [softmax(h, scs[h]) for h in range(2)]
            return tuple(update(h, *stats[h]) for h in range(2))

        def start(nq):
            if has_sink:
                row = lax.broadcasted_iota(jnp.int32, (LANES, nq), 0)
                return tuple((_wide(sink_ref[h:h + 1, :], nq), (row == L_ROW[h]).astype(f32)) for h in range(2))
            return tuple((jnp.full((1, nq), NEG, f32), jnp.zeros((LANES, nq), f32)) for h in range(2))

        def finish(carry, queries):
            (m0, a0), (m1, a1) = carry
            l0, l1 = a0[L_ROW[0]:L_ROW[0] + 1, :], a1[L_ROW[1]:L_ROW[1] + 1, :]
            o_t = jnp.where(top, a0 * (1.0 / l0), a1 * (1.0 / l1))
            o_ref[queries, :] = o_t.T.astype(bf16)
            lse_ref[0:1, queries] = m0 + jnp.log(l0)
            lse_ref[1:2, queries] = m1 + jnp.log(l1)

        if window:
            for c in range(t // LANES):
                queries = slice(c * LANES, (c + 1) * LANES)
                q0 = i * t + c * LANES
                k0 = pl.multiple_of(jnp.maximum(q0 - window, 0), LANES)
                finish(tile(k0, LANES + window, q0 - k0, start(LANES), True, queries), queries)
        else:
            carry = lax.fori_loop(0, i, lambda kb, c: tile(pl.multiple_of(kb * t, t), t, 0, c, False), start(t))
            half, k_own = t // 2, pl.multiple_of(i * t, t)
            carry = tile(k_own, half, 0, carry, True)
            finish(tuple((m[:, :half], a[:, :half]) for m, a in carry), slice(0, half))
            carry = tuple((m[:, half:], a[:, half:]) for m, a in carry)
            finish(tile(pl.multiple_of(k_own + half, half), half, 0, carry, True, slice(half, t)), slice(half, t))

    q_spec = pl.BlockSpec((t, LANES), lambda j, i: (i, j))
    kv_spec = pl.BlockSpec((s, LANES), lambda j, i: (0, j))
    in_specs, args = [q_spec, kv_spec, kv_spec], [q, k, v]
    if fox:
        in_specs += [pl.BlockSpec((s, 2 * LANES), lambda j, i: (0, j))]
        args += [cum_b]
    if has_sink:
        in_specs += [pl.BlockSpec((None, 2, LANES), lambda j, i: (j, 0, 0))]
        args += [sink_rows.reshape(N_PAIRS, 2, LANES)]
    return pl.pallas_call(
        body, name=name, grid=(N_PAIRS, s // t), in_specs=in_specs,
        out_specs=[q_spec, pl.BlockSpec((None, 2, t), lambda j, i: (j, 0, i))],
        out_shape=[jax.ShapeDtypeStruct((s, N_PAIRS * LANES), bf16), jax.ShapeDtypeStruct((N_PAIRS, 2, s), f32)],
        compiler_params=_params(2),
    )(*args)


def _branch_dgrad_delta(db, w, o, name, *, lse=None, sink_rows=None, after=None):
    s, hw = o.shape
    tm = _row_tile(s, 1024)
    has_sink = sink_rows is not None
    extra = [] if after is None else [after]

    def body(*refs):
        db_ref, w_ref, o_ref = refs[:3]
        outs = refs[3 + (2 if has_sink else 0) + len(extra):]
        do_ref, dl_ref = outs[:2]
        if has_sink:
            lse_ref, sink_ref = refs[3:5]
            ds_ref = outs[2]

            @pl.when(pl.program_id(0) == 0)
            def _():
                ds_ref[...] = jnp.zeros_like(ds_ref)
        do = lax.dot_general(db_ref[...], w_ref[...], _NT, preferred_element_type=f32).astype(bf16)
        do_ref[...] = do
        for j in range(N_PAIRS):
            cols = slice(j * LANES, (j + 1) * LANES)
            prod_t = (do[:, cols].astype(f32) * o_ref[:, cols].astype(f32)).T
            for h in range(2):
                dl = jnp.sum(prod_t[h * HEAD_DIM:(h + 1) * HEAD_DIM, :], axis=0, keepdims=True)
                dl_ref[j, h:h + 1, :] = dl
                if has_sink:
                    r = 2 * j + h
                    p_sink = jnp.exp(sink_ref[r:r + 1, 0:1] - lse_ref[j, h:h + 1, :])
                    ds_ref[r:r + 1, :] += -jnp.sum(p_sink * dl, axis=1, keepdims=True)

    rows_spec = pl.BlockSpec((N_PAIRS, 2, tm), lambda i: (0, 0, i))
    in_specs = [_row_spec(tm, db.shape[1]), pl.BlockSpec(w.shape, lambda i: (0, 0)), _row_spec(tm, hw)]
    args = [db, w, o]
    out_specs = [_row_spec(tm, hw), rows_spec]
    out_shape = [jax.ShapeDtypeStruct((s, hw), bf16), jax.ShapeDtypeStruct((N_PAIRS, 2, s), f32)]
    if has_sink:
        in_specs += [rows_spec, _vec_spec(LANES, N_HEADS)]
        args += [lse, sink_rows]
        out_specs += [_vec_spec(LANES, N_HEADS)]
        out_shape += [jax.ShapeDtypeStruct((N_HEADS, LANES), f32)]
    return pl.pallas_call(
        body, name=name, grid=(s // tm,), in_specs=in_specs + [pl.BlockSpec(memory_space=pl.ANY)] * len(extra),
        out_specs=out_specs, out_shape=out_shape, compiler_params=_params(1),
    )(*args, *extra)


def _attn_bwd(q, k, v, do, lse, delta, name, *, cum_b=None, window=None, t=256):
    s = q.shape[0]
    t = _row_tile(s, t)
    nblk = s // t
    fox = cum_b is not None
    assert not window or (window % LANES == 0 and LANES + window <= s)

    def body(*refs):
        k_ref, v_ref, q_ref, do_ref, lse_ref, dl_ref = refs[:6]
        rest = list(refs[6:])
        cb_ref = rest.pop(0) if fox else None
        dq_ref, dk_ref, dv_ref = rest[:3]
        dcs_ref, rs_ref = (rest[3], rest[4]) if fox else (None, None)
        dk_acc, dv_acc = rest[-2:]
        b = pl.program_id(1)
        k0 = pl.multiple_of(b * t, t)

        @pl.when(b == 0)
        def _():
            dq_ref[...] = jnp.zeros_like(dq_ref)
            if fox:
                rs_ref[...] = jnp.zeros_like(rs_ref)

        dk_acc[...] = jnp.zeros_like(dk_acc)
        dv_acc[...] = jnp.zeros_like(dv_acc)
        if fox:
            dcs_ref[...] = jnp.zeros_like(dcs_ref)
        low = _lane() < HEAD_DIM
        top = lax.broadcasted_iota(jnp.int32, (LANES, 1), 0) < HEAD_DIM
        kblk, vblk = k_ref[...], v_ref[...]
        k_t = kblk.astype(f32).T.astype(bf16)
        cks = [_wide(cb_ref[pl.ds(k0, t), h * LANES:(h + 1) * LANES], t) for h in range(2)] if fox else None

        def tile(q0, n_queries, off, masked, keys=slice(0, t)):
            cols = pl.ds(q0, n_queries)
            q2, do2 = q_ref[cols, :], do_ref[cols, :]
            zero = jnp.zeros_like(q2)
            valid = _tile_mask(keys.stop - keys.start, n_queries, off, window) if masked else None
            dq_parts = []
            for h in range(2):
                qm = jnp.where(low, q2, zero) if h == 0 else jnp.where(low, zero, q2)
                dom = jnp.where(low, do2, zero) if h == 0 else jnp.where(low, zero, do2)
                sc = lax.dot_general(kblk[keys], qm, _NT, preferred_element_type=f32)
                if fox:
                    sc = sc - cks[h][keys, :n_queries]
                if masked:
                    sc = jnp.where(valid, sc, NEG)
                p = jnp.exp(sc - lse_ref[h:h + 1, cols])
                dp = lax.dot_general(vblk[keys], dom, _NT, preferred_element_type=f32)
                ds = p * (dp - dl_ref[h:h + 1, cols])
                pb, dsb = p.astype(bf16), ds.astype(bf16)
                dv_acc[keys, :] += jnp.dot(pb, dom, preferred_element_type=f32)
                dk_acc[keys, :] += jnp.dot(dsb, qm, preferred_element_type=f32)
                dq_parts.append(jnp.dot(k_t[:, keys], dsb, preferred_element_type=f32))
                if fox:
                    dcs_ref[keys, h * LANES:(h + 1) * LANES] += sum(ds[:, g * LANES:(g + 1) * LANES]
                                                                    for g in range(n_queries // LANES))
                    rs_ref[h:h + 1, cols] += jnp.sum(ds, axis=0, keepdims=True)
            dq_ref[:, cols] += jnp.where(top, dq_parts[0], dq_parts[1])

        def later_block(qb, carry):
            tile(pl.multiple_of(qb * t, t), t, 0, False)
            return carry

        if window:
            for c in range(t // LANES):
                first = b * t + c * LANES
                q0 = pl.multiple_of(jnp.minimum(first, s - (LANES + window)), LANES)
                tile(q0, LANES + window, q0 - first, True, slice(c * LANES, (c + 1) * LANES))
        else:
            half = t // 2
            tile(k0, half, 0, True, slice(0, half))
            tile(pl.multiple_of(k0 + half, half), half, half, True)
            lax.fori_loop(b + 1, nblk, later_block, 0)
        dk_ref[...] = dk_acc[...].astype(bf16)
        dv_ref[...] = dv_acc[...].astype(bf16)

    kv_spec = pl.BlockSpec((t, LANES), lambda j, b: (b, j))
    seq_spec = pl.BlockSpec((s, LANES), lambda j, b: (0, j))
    rows_spec = pl.BlockSpec((None, 2, s), lambda j, b: (j, 0, 0))
    hw = N_PAIRS * LANES
    in_specs, args = [kv_spec, kv_spec, seq_spec, seq_spec, rows_spec, rows_spec], [k, v, q, do, lse, delta]
    out_specs = [pl.BlockSpec((LANES, s), lambda j, b: (j, 0)), kv_spec, kv_spec]
    out_shape = [jax.ShapeDtypeStruct((hw, s), f32), jax.ShapeDtypeStruct((s, hw), bf16), jax.ShapeDtypeStruct((s, hw), bf16)]
    if fox:
        in_specs += [pl.BlockSpec((s, 2 * LANES), lambda j, b: (0, j))]
        args += [cum_b]
        out_specs += [pl.BlockSpec((t, 2 * LANES), lambda j, b: (b, j)), rows_spec]
        out_shape += [jax.ShapeDtypeStruct((s, N_HEADS * LANES), f32), jax.ShapeDtypeStruct((N_PAIRS, 2, s), f32)]
    return pl.pallas_call(
        body, name=name, grid=(N_PAIRS, nblk), in_specs=in_specs, out_specs=out_specs, out_shape=out_shape,
        scratch_shapes=[pltpu.VMEM((t, LANES), f32)] * 2, compiler_params=_params(2),
    )(*args)


def _branch_merge(o_a, o_b, w_a, w_b, gl, name):
    s, k = o_a.shape
    d = w_a.shape[1]
    tm = _row_tile(s, 1024)

    def body(oa_ref, ob_ref, wa_ref, wb_ref, g_ref, ba_ref, bb_ref, m_ref):
        ba = jnp.dot(oa_ref[...], wa_ref[...], preferred_element_type=f32)
        bb = jnp.dot(ob_ref[...], wb_ref[...], preferred_element_type=f32)
        g0, g1 = jax.nn.sigmoid(g_ref[:, :d].astype(f32)), jax.nn.sigmoid(g_ref[:, d:].astype(f32))
        ba_ref[...] = ba.astype(bf16)
        bb_ref[...] = bb.astype(bf16)
        m_ref[...] = (g0 * ba + g1 * bb).astype(bf16)

    whole = pl.BlockSpec((k, d), lambda i: (0, 0))
    return pl.pallas_call(
        body, name=name, grid=(s // tm,),
        in_specs=[_row_spec(tm, k), _row_spec(tm, k), whole, whole, _row_spec(tm, 2 * d)],
        out_specs=[_row_spec(tm, d)] * 3, out_shape=[jax.ShapeDtypeStruct((s, d), bf16)] * 3, compiler_params=_params(1),
    )(o_a, o_b, w_a, w_b, gl)


def _out_dgrad_merge_bwd(dy, w_out, ba, bb, gl, name):
    s, d = ba.shape
    tm = _row_tile(s, 512)

    def body(dy_ref, w_ref, a_ref, b_ref, g_ref, da_ref, db_ref, dg_ref):
        dmv = lax.dot_general(dy_ref[...], w_ref[...], _NT, preferred_element_type=f32)
        g0, g1 = jax.nn.sigmoid(g_ref[:, :d].astype(f32)), jax.nn.sigmoid(g_ref[:, d:].astype(f32))
        da_ref[...] = (dmv * g0).astype(bf16)
        db_ref[...] = (dmv * g1).astype(bf16)
        dg_ref[:, :d] = (dmv * a_ref[...].astype(f32) * (g0 * (1.0 - g0))).astype(bf16)
        dg_ref[:, d:] = (dmv * b_ref[...].astype(f32) * (g1 * (1.0 - g1))).astype(bf16)

    return pl.pallas_call(
        body, name=name, grid=(s // tm,),
        in_specs=[_row_spec(tm, dy.shape[1]), pl.BlockSpec(w_out.shape, lambda i: (0, 0))] + [_row_spec(tm, d)] * 2
        + [_row_spec(tm, 2 * d)],
        out_specs=[_row_spec(tm, d)] * 2 + [_row_spec(tm, 2 * d)],
        out_shape=[jax.ShapeDtypeStruct((s, d), bf16)] * 2 + [jax.ShapeDtypeStruct((s, 2 * d), bf16)],
        compiler_params=_params(1),
    )(dy, w_out, ba, bb, gl)


GLU_TILE = 256


def _ffn_in_swiglu(h, w_t, name):
    s, d = h.shape
    f = w_t.shape[0] // 2
    tm = _row_tile(s, 4096)
    tg = GLU_TILE
    nb = f // tg

    def body(h_ref, wg_ref, wu_ref, g_ref, u_ref, act_ref):
        hv = h_ref[...]
        g = lax.dot_general(hv, wg_ref[...], _NT, preferred_element_type=f32)
        u = lax.dot_general(hv, wu_ref[...], _NT, preferred_element_type=f32)
        g_ref[...] = g.astype(bf16)
        u_ref[...] = u.astype(bf16)
        act_ref[...] = (g * jax.nn.sigmoid(g) * u).astype(bf16)

    col = pl.BlockSpec((tm, tg), lambda i, j: (i, j))
    return pl.pallas_call(
        body, name=name, grid=(s // tm, nb),
        in_specs=[pl.BlockSpec((tm, d), lambda i, j: (i, 0)), pl.BlockSpec((tg, d), lambda i, j: (j, 0)),
                  pl.BlockSpec((tg, d), lambda i, j: (j + nb, 0))],
        out_specs=[col] * 3, out_shape=[jax.ShapeDtypeStruct((s, f), bf16)] * 3, compiler_params=_params(2),
    )(h, w_t, w_t)


def _ffn_out_dgrad_swiglu(dy, w_out, g, u, name):
    s, d = dy.shape
    f = g.shape[1]
    tm = _row_tile(s, 4096)
    tg = GLU_TILE

    def body(dy_ref, w_ref, g_ref, u_ref, dg_ref, du_ref):
        dv = lax.dot_general(dy_ref[...], w_ref[...], _NT, preferred_element_type=f32)
        gv, uv = g_ref[...].astype(f32), u_ref[...].astype(f32)
        sg = jax.nn.sigmoid(gv)
        dg_ref[...] = (dv * uv * (sg * (1.0 + gv * (1.0 - sg)))).astype(bf16)
        du_ref[...] = (dv * (gv * sg)).astype(bf16)

    col = pl.BlockSpec((tm, tg), lambda i, j: (i, j))
    return pl.pallas_call(
        body, name=name, grid=(s // tm, f // tg),
        in_specs=[pl.BlockSpec((tm, d), lambda i, j: (i, 0)), pl.BlockSpec((tg, d), lambda i, j: (j, 0)), col, col],
        out_specs=[col] * 2, out_shape=[jax.ShapeDtypeStruct((s, f), bf16)] * 2, compiler_params=_params(2),
    )(dy, w_out, g, u)


def _wgrad_stack(parts, h, name):
    s, m = parts[0].shape
    d = h.shape[1]
    tm = 256
    nb = m // tm
    n = len(parts)

    def body(*refs):
        i = pl.program_id(0)
        for p in range(n):
            @pl.when(i // nb == p)
            def _(p=p):
                refs[n + 1][...] = lax.dot_general(refs[p][...], refs[n][...], _TN, preferred_element_type=f32).astype(bf16)

    a_specs = [pl.BlockSpec((s, tm), lambda i, p=p: (0, jnp.clip(i - p * nb, 0, nb - 1))) for p in range(n)]
    return pl.pallas_call(
        body, name=name, grid=(n * nb,), in_specs=a_specs + [pl.BlockSpec((s, d), lambda i: (0, 0))],
        out_specs=pl.BlockSpec((tm, d), lambda i: (i, 0)),
        out_shape=jax.ShapeDtypeStruct((n * m, d), bf16), compiler_params=_params(1),
    )(*parts, h)


def _ada_wgrad(c_all, d_all, name):
    n, d = c_all.shape
    w = d_all.shape[1]

    def body(c_ref, d_ref, o_ref):
        eye = (lax.broadcasted_iota(jnp.int32, (n, n), 0) == lax.broadcasted_iota(jnp.int32, (n, n), 1)).astype(f32)
        ct = lax.dot_general(c_ref[...], eye, _TN, precision=lax.Precision.HIGHEST, preferred_element_type=f32)
        g = ct[:, 0:1] * d_ref[0:1, :]
        for bi in range(1, n):
            g = g + ct[:, bi:bi + 1] * d_ref[bi:bi + 1, :]
        o_ref[0] = g

    return pl.pallas_call(
        body, name=name, out_shape=jax.ShapeDtypeStruct((1, d, w), f32), compiler_params=_params(),
    )(c_all, d_all)


def _adamw(parts, w, m, v, name, mine=None, me=None):
    r, c = w.shape
    n_parts = parts.shape[0]
    row_tiles = [t for t in range(min(r, 256), 0, -1) if r % t == 0 and (t % 16 == 0 or t == r)]
    if row_tiles:
        tr, tc = row_tiles[0], c
    else:
        tr, tc = r, next(t for t in (256, LANES) if c % t == 0)

    def body(*refs):
        w_ref, m_ref, v_ref, g_ref, d_ref, nm_ref, nv_ref = refs[-7:]
        if mine is None:
            p_ref, = refs[:-7]
        else:
            me_ref, p_ref, own_ref = refs[:-7]

        def part(i):
            if mine is None:
                return p_ref[i].astype(f32)
            return jnp.where(me_ref[0] == i, own_ref[...], p_ref[i]).astype(f32)

        g = part(0)
        for i in range(1, n_parts):
            g = g + part(i)
        mm = ADAM_B1 * m_ref[...] + (1.0 - ADAM_B1) * g
        vv = ADAM_B2 * v_ref[...] + (1.0 - ADAM_B2) * (g * g)
        m_hat = mm / (1.0 - ADAM_B1 ** ADAM_STEP)
        v_hat = vv / (1.0 - ADAM_B2 ** ADAM_STEP)
        g_ref[...] = g
        d_ref[...] = -ADAM_LR * (m_hat / (jnp.sqrt(v_hat) + ADAM_EPS) + ADAM_WD * w_ref[...])
        nm_ref[...] = mm
        nv_ref[...] = vv

    out_shape = [jax.ShapeDtypeStruct((r, c), f32)] * 4
    if mine is None:
        spec = pl.BlockSpec((tr, tc), lambda i, j: (i, j))
        return pl.pallas_call(
            body, name=name, grid=(r // tr, c // tc),
            in_specs=[pl.BlockSpec((n_parts, tr, tc), lambda i, j: (0, i, j))] + [spec] * 3,
            out_specs=[spec] * 4, out_shape=out_shape, compiler_params=_params(2),
        )(parts, w, m, v)
    spec = pl.BlockSpec((tr, tc), lambda i, j, me_ref: (i, j))
    return pl.pallas_call(
        body, name=name, out_shape=out_shape, compiler_params=_params(2),
        grid_spec=pltpu.PrefetchScalarGridSpec(
            num_scalar_prefetch=1, grid=(r // tr, c // tc),
            in_specs=[pl.BlockSpec((n_parts, tr, tc), lambda i, j, me_ref: (0, i, j)),
                      pl.BlockSpec((None, tr, tc), lambda i, j, me_ref: (me_ref[0], i, j))] + [spec] * 3,
            out_specs=[spec] * 4),
    )(me, parts, mine, w, m, v)


def _me():
    return lax.axis_index("x"), lax.axis_index("y"), lax.axis_index("c")


def _gather_prologue(c, w_ada, b_mine, w_in_t, name):
    n_dev, d = N_DEV, c.shape[1]
    ada_w = w_ada.shape[1]

    def body(c_ref, w_ref, b_ref, win_ref, call_ref, ada_ref, gin_ref, cols_ref, send_sems, recv_sems, local_sems):
        x, y, cc = _me()
        me, sibling = (x, y, cc), (x, y, 1 - cc)
        chips = [(1 - x, y), (x, 1 - y), (1 - x, 1 - y)]
        outs = (call_ref, ada_ref, gin_ref)

        def rows(a, dev):
            return outs[a].at[4 * dev[0] + 2 * dev[1] + dev[2]]

        def copy(a, k, block, to, src=None):
            return pltpu.make_async_remote_copy(
                src_ref=rows(a, block) if src is None else src, dst_ref=rows(a, block),
                send_sem=send_sems.at[a, k], recv_sem=recv_sems.at[a, k], device_id=to, device_id_type=MESH)

        def begin(a, src):
            own = pltpu.make_async_copy(src, rows(a, me), local_sems.at[a])
            sends = [copy(a, 0, me, sibling, src=src)] + [copy(a, 1 + j, me, (*chip, cc), src=src) for j, chip in enumerate(chips)]
            for cp in [own] + sends:
                cp.start()
            return own, sends

        def finish(a, own, sends):
            passed = []
            for j, chip in enumerate(chips):
                copy(a, 1 + j, (*chip, cc), me).wait_recv()
                passed.append(copy(a, 4 + j, (*chip, cc), sibling))
                passed[-1].start()
            copy(a, 0, sibling, me).wait_recv()
            for j, chip in enumerate(chips):
                copy(a, 4 + j, (*chip, 1 - cc), me).wait_recv()
            for cp in sends + passed:
                cp.wait_send()
            own.wait()

        finish(0, *begin(0, c_ref))
        cols_ref[...] = (jnp.dot(call_ref[:, 0, :].astype(bf16), w_ref[...].astype(bf16), preferred_element_type=f32)
                         + b_ref[...])
        finish(1, *begin(1, cols_ref))
        finish(2, *begin(2, win_ref))

    vmem, hbm = pl.BlockSpec(memory_space=pltpu.VMEM), pl.BlockSpec(memory_space=pl.ANY)
    return pl.pallas_call(
        body, name=name, in_specs=[vmem, vmem, vmem, hbm], out_specs=[vmem, vmem, hbm],
        out_shape=[jax.ShapeDtypeStruct((n_dev, 1, d), f32), jax.ShapeDtypeStruct((n_dev, n_dev, ada_w), f32),
                   jax.ShapeDtypeStruct((n_dev,) + w_in_t.shape, w_in_t.dtype)],
        scratch_shapes=[pltpu.VMEM((n_dev, ada_w), f32), pltpu.SemaphoreType.DMA((3, 7)), pltpu.SemaphoreType.DMA((3, 7)),
                        pltpu.SemaphoreType.DMA((3,))],
        compiler_params=pltpu.CompilerParams(vmem_limit_bytes=VMEM_LIMIT),
    )(c, w_ada, b_mine, w_in_t)


_FLIPS = ((0, 0, 1), (1, 0, 0), (0, 1, 0), (1, 1, 0), (1, 0, 1), (0, 1, 1), (1, 1, 1))
_HBM = pl.BlockSpec(memory_space=pltpu.HBM)
_SEM = pl.BlockSpec(memory_space=pltpu.SEMAPHORE)


def _exchange_copies(scatter, srcs, lands, send_sems, recv_sems):
    x, y, c = _me()
    me_row = 4 * x + 2 * y + c
    out = []
    for k, (fx, fy, fc) in enumerate(_FLIPS):
        peer = (x ^ fx, y ^ fy, c ^ fc)
        peer_row = 4 * peer[0] + 2 * peer[1] + peer[2]
        for a in range(len(srcs)):
            out.append(pltpu.make_async_remote_copy(
                src_ref=srcs[a].at[peer_row] if scatter else srcs[a], dst_ref=lands[a].at[me_row],
                send_sem=send_sems.at[7 * a + k], recv_sem=recv_sems.at[7 * a + k], device_id=peer, device_id_type=MESH))
    return out


def _own_copies(srcs, lands, own_sems):
    x, y, c = _me()
    return [pltpu.make_async_copy(srcs[a], lands[a].at[4 * x + 2 * y + c], own_sems.at[a]) for a in range(len(srcs))]


def _exchange_start(arrays, scatter, name, after=None):
    n = len(arrays)
    lands = [lax.empty(a.shape if scatter else (N_DEV,) + a.shape, a.dtype) for a in arrays]
    extra = [] if after is None else [after]

    def body(*refs):
        srcs, zones = refs[:n], refs[n:2 * n]
        send_sems, recv_sems, own_sems = refs[2 * n + len(extra):2 * n + len(extra) + 3]
        token = refs[-1]
        for cp in _exchange_copies(scatter, srcs, zones, send_sems, recv_sems):
            cp.start()
        for cp in [] if scatter else _own_copies(srcs, zones, own_sems):
            cp.start()
        token[...] = jnp.zeros_like(token)

    thru = [pltpu.HBM(a.shape, a.dtype) for a in list(arrays) + lands]
    outs = pl.pallas_call(
        body, name=name,
        out_shape=(pltpu.SemaphoreType.DMA((7 * n,)), pltpu.SemaphoreType.DMA((7 * n,)), pltpu.SemaphoreType.DMA((n,)), *thru,
                   jax.ShapeDtypeStruct((8, LANES), f32)),
        in_specs=[_HBM] * (2 * n) + [pl.BlockSpec(memory_space=pl.ANY)] * len(extra),
        out_specs=(_SEM, _SEM, _SEM, *[_HBM] * (2 * n), pl.BlockSpec(memory_space=pltpu.VMEM)),
        input_output_aliases={i: 3 + i for i in range(2 * n)},
        compiler_params=pltpu.CompilerParams(has_side_effects=pltpu.SideEffectType.DATAFLOW_SIDE_EFFECTING),
    )(*[pltpu.with_memory_space_constraint(a, pltpu.HBM) for a in list(arrays) + lands], *extra)
    return dict(n=n, scatter=scatter, sems=outs[:3], srcs=outs[3:3 + n], lands=outs[3 + n:3 + 2 * n], token=outs[-1])


def _exchange_wait(handle, after, name):
    n, scatter = handle["n"], handle["scatter"]

    def body(*refs):
        srcs, zones = refs[:n], refs[n:2 * n]
        send_sems, recv_sems, own_sems = refs[2 * n:2 * n + 3]
        for cp in _exchange_copies(scatter, srcs, zones, send_sems, recv_sems):
            cp.wait_send()
            cp.wait_recv()
        for cp in [] if scatter else _own_copies(srcs, zones, own_sems):
            cp.wait()

    thru = [pltpu.HBM(a.shape, a.dtype) for a in list(handle["srcs"]) + list(handle["lands"])]
    outs = pl.pallas_call(
        body, name=name, out_shape=tuple(thru),
        in_specs=[_HBM] * (2 * n) + [_SEM, _SEM, _SEM, pl.BlockSpec(memory_space=pl.ANY)], out_specs=tuple([_HBM] * (2 * n)),
        input_output_aliases={i: i for i in range(2 * n)},
        compiler_params=pltpu.CompilerParams(has_side_effects=pltpu.SideEffectType.DATAFLOW_SIDE_EFFECTING),
    )(*handle["srcs"], *handle["lands"], *handle["sems"], after)
    return list(outs[:n]), list(outs[n:])


def _cols_from_shards(g):
    return jnp.transpose(g, (1, 0, 2)).reshape(g.shape[1], -1)


def _shards_from_cols(a):
    return jnp.transpose(a.reshape(a.shape[0], N_DEV, -1), (1, 0, 2))


def _local_step(x, positions, ada, g_pre_mix, g_post_mix, b_f, sinks, g_pre_ffn, g_post_ffn, target,
                w_in_t, mix_weights, ffn_weights, on_grads):
    s, d = x.shape
    row = lambda v: v.reshape(1, -1)
    shift_m, scale_m, gate_m, shift_f, scale_f, gate_f = (ada[i:i + 1] for i in range(6))
    w_gate_t, w_qkv_t = w_in_t[F_OFF + N_HEADS:], w_in_t
    w_f_t = jnp.pad(w_in_t[F_OFF:F_OFF + N_HEADS], ((0, LANES - N_HEADS), (0, 0)))
    bf_row = jnp.pad(row(b_f), ((0, 0), (0, LANES - N_HEADS)))
    sink_rows = jnp.broadcast_to(sinks.reshape(N_HEADS, 1).astype(f32), (N_HEADS, LANES))
    inv_freq = 1.0 / (ROPE_THETA ** (jnp.arange(0, HEAD_DIM, 2, dtype=f32) / HEAD_DIM))
    cos, sin_s = _rope_tables(positions.reshape(s, 1), jnp.tile(inv_freq, 4).reshape(1, LANES), "rope_tables")

    h1, qa, ka, va, qb, kb, vb = _prenorm_proj_qkv(x, row(g_pre_mix), scale_m, shift_m, w_qkv_t, cos, sin_s, "prenorm_proj_qkv")
    gl = _matmul(h1, w_gate_t, "nt", bf16, "proj_gate")
    fl, cum_b = _forget_prep(h1, w_f_t, bf_row, "proj_forget_prep")
    o_a, lse_a = _attn_fwd(qa, ka, va, "swa_fwd", sink_rows=sink_rows, window=WINDOW, t=2048)
    o_b, lse_b = _attn_fwd(qb, kb, vb, "fox_fwd", cum_b=cum_b, t=1024)
    everything_before = (gl[:8, :LANES] + o_a[:8, :LANES] + o_b[:8, :LANES]).astype(f32)
    w_branch_a, w_branch_b, w_out = mix_weights(everything_before)
    ba, bb, merged = _branch_merge(o_a, o_b, w_branch_a, w_branch_b, gl, "branch_merge")
    y1, x2, h2 = _out_proj_postnorm_prenorm(merged, w_out, x, row(g_post_mix), gate_m, row(g_pre_ffn), scale_f, shift_f,
                                            "out_proj_norms")

    w_ffn_in_t, w_ffn_out = ffn_weights(h2)
    g_ff, u_ff, act = _ffn_in_swiglu(h2, w_ffn_in_t, "ffn_in_swiglu")
    loss_row, d_out, d_y2, vec_pf = _out_proj_loss_tail(act, w_ffn_out, x2, row(g_post_ffn), gate_f, target, "ffn_out_loss_tail")

    g_w_ffn_out = _matmul(act, d_y2, "tn", bf16, "ffn_out_wgrad")
    dg_ff, du_ff = _ffn_out_dgrad_swiglu(d_y2, w_ffn_out, g_ff, u_ff, "ffn_out_dgrad_swiglu")
    g_w_ffn_in_t = _wgrad_stack([dg_ff, du_ff], h2, "ffn_in_wgrad")
    sent = on_grads(dict(w_ffn_in=g_w_ffn_in_t, w_ffn_out=g_w_ffn_out))
    d_x2, vec_nf, d_y1, vec_pm = _dgrad_prenorm_bwd(
        [(dg_ff, w_ffn_in_t, 0), (du_ff, w_ffn_in_t, 1)], x2, row(g_pre_ffn), scale_f, d_out, "ffn_in_dgrad_norms_bwd",
        after=sent, below=(y1, row(g_post_mix), gate_m))

    g_w_out = _matmul(merged, d_y1, "tn", bf16, "out_proj_wgrad")
    d_ba, d_bb, dgl = _out_dgrad_merge_bwd(d_y1, w_out, ba, bb, gl, "out_proj_dgrad_merge_bwd")
    g_w_branch_a = _matmul(o_a, d_ba, "tn", bf16, "branch_a_wgrad")
    g_w_branch_b = _matmul(o_b, d_bb, "tn", bf16, "branch_b_wgrad")
    sent = on_grads(dict(w_out=g_w_out, w_branch_a=g_w_branch_a, w_branch_b=g_w_branch_b))
    d_oa, delta_a, d_sink = _branch_dgrad_delta(d_ba, w_branch_a, o_a, "branch_a_dgrad_delta", lse=lse_a,
                                                sink_rows=sink_rows, after=sent)
    d_ob, delta_b = _branch_dgrad_delta(d_bb, w_branch_b, o_b, "branch_b_dgrad_delta", after=sent)
    dqa_t, dka, dva = _attn_bwd(qa, ka, va, d_oa, lse_a, delta_a, "swa_bwd", window=WINDOW, t=2048)
    dqb_t, dkb, dvb, dcs, rs = _attn_bwd(qb, kb, vb, d_ob, lse_b, delta_b, "fox_bwd", cum_b=cum_b, t=512)
    dqkv = _qkv_prep_bwd(dqa_t, dka, dva, dqb_t, dkb, dvb, cos, sin_s, "qkv_prep_bwd")
    dfl, vec_bf = _forget_prep_bwd(rs.reshape(N_HEADS, s), dcs, fl, bf_row, "forget_prep_bwd")
    g_w_in_t = jnp.concatenate([_matmul(dqkv, h1, "tn", bf16, "qkv_wgrad"), _matmul(dfl, h1, "tn", bf16, "forget_wgrad")[:N_HEADS],
                                _matmul(dgl, h1, "tn", bf16, "gate_wgrad")], axis=0)
    sent = on_grads(dict(w_in=g_w_in_t))
    grad_x, vec_nm = _dgrad_prenorm_bwd([(dgl, w_gate_t, 0), (dqkv, w_qkv_t, 0), (dfl, w_f_t, 0)], x, row(g_pre_mix),
                                        scale_m, d_x2, "in_proj_dgrad_prenorm_bwd", after=sent)

    d_ada = jnp.concatenate([vec_nm[0], vec_nm[1], vec_pm[0], vec_nf[0], vec_nf[1], vec_pf[0]])
    small = dict(b_ada=d_ada, g_pre_mix=vec_nm[2], g_post_mix=vec_pm[1], g_pre_ffn=vec_nf[2], g_post_ffn=vec_pf[1],
                 b_f=vec_bf[0, :N_HEADS], sinks=d_sink[:, 0], loss=loss_row[0, :1])
    return grad_x, small


_SMALL = (("b_ada", 6144), ("g_pre_mix", 1024), ("g_post_mix", 1024), ("g_pre_ffn", 1024), ("g_post_ffn", 1024),
          ("b_f", 128), ("sinks", 128), ("loss", 128))
_SMALL_ROWS = 88


def _pack_small(vals):
    parts = [jnp.pad(vals[k].reshape(-1).astype(f32), (0, n - vals[k].size)) for k, n in _SMALL]
    flat = jnp.concatenate(parts)
    return jnp.pad(flat, (0, _SMALL_ROWS * LANES - flat.size)).reshape(_SMALL_ROWS, LANES)


def _unpack_small(slab, shapes):
    flat, out, off = slab.reshape(-1), {}, 0
    for k, n in _SMALL:
        size = math.prod(shapes[k])
        out[k] = flat[off:off + size].reshape(shapes[k])
        off += n
    return out


def kernel(x, c, positions, w_ada, b_ada, g_pre_mix, g_post_mix, w_in, b_f, sinks, w_branch_a, w_branch_b, w_out, g_pre_ffn, g_post_ffn, w_ffn_in, w_ffn_out, loss_target, m_w_ada, m_b_ada, m_g_pre_mix, m_g_post_mix, m_w_in, m_b_f, m_sinks, m_w_branch_a, m_w_branch_b, m_w_out, m_g_pre_ffn, m_g_post_ffn, m_w_ffn_in, m_w_ffn_out, v_w_ada, v_b_ada, v_g_pre_mix, v_g_post_mix, v_w_in, v_b_f, v_sinks, v_w_branch_a, v_w_branch_b, v_w_out, v_g_pre_ffn, v_g_post_ffn, v_w_ffn_in, v_w_ffn_out):
    xi, yi, ci = _me()
    me = 4 * xi + 2 * yi + ci
    d = D_MODEL
    ada_w = w_ada.shape[2]

    transposed = ("w_in", "w_ffn_in")
    tr = lambda a: jnp.transpose(a[0])

    b_mine = lax.dynamic_slice(b_ada, (0, me * ada_w), (1, ada_w))
    c_all, ada_all, g_in = _gather_prologue(c, w_ada[0], b_mine, tr(w_in).astype(bf16), "gather_prologue")
    c_all = c_all.reshape(N_DEV, d)
    ada = lax.dynamic_index_in_dim(ada_all, me, axis=1, keepdims=False).reshape(6, d)
    late_mix = [w.astype(bf16) for w in (w_branch_a[0], w_branch_b[0], w_out[0])]
    late_ffn = [w.astype(bf16) for w in (tr(w_ffn_in), w_ffn_out[0])]
    mix_h = _exchange_start(late_mix, False, "gather_mix_start", after=g_in)
    ffn_h = _exchange_start(late_ffn, False, "gather_ffn_start", after=mix_h["token"])

    def rows_from_shards(g):
        return g.reshape(g.shape[0] * g.shape[1], g.shape[2])

    def mix_weights(after):
        _, (g_ba, g_bb, g_out) = _exchange_wait(mix_h, after, "gather_mix_wait")
        return _cols_from_shards(g_ba), _cols_from_shards(g_bb), rows_from_shards(g_out)

    def ffn_weights(after):
        _, (g_fi, g_fo) = _exchange_wait(ffn_h, after, "gather_ffn_wait")
        return rows_from_shards(g_fi), rows_from_shards(g_fo)

    row_sharded = ("w_out", "w_ffn_out") + transposed
    in_flight = []

    def on_grads(group):
        sends = [g.reshape(N_DEV, g.shape[0] // N_DEV, g.shape[1]) if nm in row_sharded else _shards_from_cols(g)
                 for nm, g in group.items()]
        handle = _exchange_start(sends, True, "scatter_start_%d" % len(in_flight))
        in_flight.append((list(group), handle))
        return handle["token"]

    grad_x, small = _local_step(
        x[0], positions[0], ada + ffn_h["token"][0, 0], g_pre_mix[0], g_post_mix[0], b_f[0], sinks[0], g_pre_ffn[0],
        g_post_ffn[0], loss_target[0], rows_from_shards(g_in), mix_weights, ffn_weights, on_grads)

    ws = dict(w_in=(w_in, m_w_in, v_w_in), w_branch_a=(w_branch_a, m_w_branch_a, v_w_branch_a),
              w_branch_b=(w_branch_b, m_w_branch_b, v_w_branch_b), w_out=(w_out, m_w_out, v_w_out),
              w_ffn_in=(w_ffn_in, m_w_ffn_in, v_w_ffn_in), w_ffn_out=(w_ffn_out, m_w_ffn_out, v_w_ffn_out))
    res = {}

    def finish_group(gi, after):
        names, handle = in_flight[gi]
        sends, zones = _exchange_wait(handle, after, "scatter_wait_%d" % gi)
        for nm, zone, sent in zip(names, zones, sends):
            w, m, v = (tr(a) if nm in transposed else a[0] for a in ws[nm])
            out = _adamw(zone, w, m, v, "adamw_" + nm, mine=sent, me=me.reshape(1).astype(jnp.int32))
            after = out[0]
            res[nm] = [jnp.transpose(o) for o in out] if nm in transposed else out
        return after

    small_h = _exchange_start([_pack_small(small)], False, "gather_small_start", after=grad_x)
    done = finish_group(1, finish_group(0, small_h["token"]))
    _, (slab_all,) = _exchange_wait(small_h, done, "gather_small_wait")
    small_w = dict(b_ada=b_ada, g_pre_mix=g_pre_mix, g_post_mix=g_post_mix, g_pre_ffn=g_pre_ffn, g_post_ffn=g_post_ffn,
                   b_f=b_f, sinks=sinks, loss=jnp.zeros((1,), f32))
    small_m = dict(b_ada=m_b_ada, g_pre_mix=m_g_pre_mix, g_post_mix=m_g_post_mix, g_pre_ffn=m_g_pre_ffn,
                   g_post_ffn=m_g_post_ffn, b_f=m_b_f, sinks=m_sinks, loss=jnp.zeros((1,), f32))
    small_v = dict(b_ada=v_b_ada, g_pre_mix=v_g_pre_mix, g_post_mix=v_g_post_mix, g_pre_ffn=v_g_pre_ffn,
                   g_post_ffn=v_g_post_ffn, b_f=v_b_f, sinks=v_sinks, loss=jnp.ones((1,), f32))
    shapes = {k: small_w[k].shape for k, _ in _SMALL}
    s_out = _adamw(slab_all, _pack_small(small_w), _pack_small(small_m), _pack_small(small_v), "adamw_small")
    s_grad, s_delta, s_m, s_v = (_unpack_small(o, shapes) for o in s_out)

    d_ada_all = lax.dynamic_slice(slab_all[:, :6144 // LANES, :].reshape(N_DEV, 6144), (0, me * ada_w), (N_DEV, ada_w))
    ada_parts = _ada_wgrad(c_all, d_ada_all, "ada_wgrad")

    res["w_ada"] = _adamw(ada_parts, w_ada[0], m_w_ada[0], v_w_ada[0], "adamw_w_ada")
    finish_group(2, res["w_ada"][0])

    order = ["w_ada", "b_ada", "g_pre_mix", "g_post_mix", "w_in", "b_f", "sinks", "w_branch_a", "w_branch_b", "w_out",
             "g_pre_ffn", "g_post_ffn", "w_ffn_in", "w_ffn_out"]
    outs = [s_grad["loss"].reshape(()), grad_x[None]]
    for which, small_o in enumerate((s_grad, s_delta, s_m, s_v)):
        for nm in order:
            outs.append(res[nm][which][None] if nm in res else small_o[nm])
    return tuple(outs)
```

```python
import math

import jax
import jax.numpy as jnp
from jax import lax
from jax.experimental import pallas as pl
from jax.experimental.pallas import tpu as pltpu

f32 = jnp.float32
bf16 = jnp.bfloat16

D_MODEL = 1024
HEAD_DIM = 64
N_HEADS = 8
N_PAIRS = 4
QKV_W = 2304
F_OFF = 2304
WINDOW = 128
ROPE_THETA = 10000.0
RMS_EPS = 1e-6
N_DEV = 8
ADAM_LR, ADAM_B1, ADAM_B2, ADAM_EPS, ADAM_WD, ADAM_STEP = 0.001, 0.9, 0.999, 1e-08, 0.01, 10
NEG = -1e30
L_ROW = (HEAD_DIM, 0)
LANES = 128
VMEM_LIMIT = 48 * 1024 * 1024
MESH = pl.DeviceIdType.MESH

_NT = (((1,), (1,)), ((), ()))
_TN = (((0,), (0,)), ((), ()))


def _params(n_grid=0):
    sem = ("arbitrary",) * n_grid if n_grid else None
    return pltpu.CompilerParams(dimension_semantics=sem, vmem_limit_bytes=VMEM_LIMIT)


def _row_tile(s, want):
    t = min(s, want)
    assert s % t == 0, (s, t)
    return t


MATMUL_VMEM_BUDGET = 40 * 1024 * 1024


def _matmul_tiles(m, n, k, a_item, b_item, o_item):
    def tiles(d):
        return [t for t in range(LANES, min(d, 2048) + 1, LANES) if d % t == 0] or [d]

    best = None
    for tm in tiles(m):
        for tn in tiles(n):
            vmem = 2 * (tm * k * a_item + tn * k * b_item + tm * tn * o_item) + tm * tn * 4
            if vmem > MATMUL_VMEM_BUDGET:
                continue
            traffic = m * k * a_item + n * k * b_item * (1 if tn == n else m // tm) + m * n * o_item
            steps = (m // tm) * (n // tn)
            key = (traffic, 0, steps) if steps >= 4 else (traffic, 1, -steps)
            if best is None or key < best[0]:
                best = (key, tm, tn)
    assert best is not None, (m, n, k)
    return best[1], best[2]


def _matmul(a, b, mode, out_dtype, name, after=None):
    if mode == "nn":
        (m, k), n = a.shape, b.shape[1]
    elif mode == "nt":
        (m, k), n = a.shape, b.shape[0]
    else:
        (k, m), n = a.shape, b.shape[1]
    tm, tn = _matmul_tiles(m, n, k, a.dtype.itemsize, b.dtype.itemsize, jnp.dtype(out_dtype).itemsize)
    if mode == "nn":
        a_spec, b_spec, dims = pl.BlockSpec((tm, k), lambda i, j: (i, 0)), pl.BlockSpec((k, tn), lambda i, j: (0, j)), None
    elif mode == "nt":
        a_spec, b_spec, dims = pl.BlockSpec((tm, k), lambda i, j: (i, 0)), pl.BlockSpec((tn, k), lambda i, j: (j, 0)), _NT
    else:
        a_spec, b_spec, dims = pl.BlockSpec((k, tm), lambda i, j: (0, i)), pl.BlockSpec((k, tn), lambda i, j: (0, j)), _TN

    def body(a_ref, b_ref, *rest):
        o_ref = rest[-1]
        av, bv = a_ref[...].astype(bf16), b_ref[...].astype(bf16)
        if dims is None:
            r = jnp.dot(av, bv, preferred_element_type=f32)
        else:
            r = lax.dot_general(av, bv, dims, preferred_element_type=f32)
        o_ref[...] = r.astype(out_dtype)

    extra = [] if after is None else [after]
    return pl.pallas_call(
        body, name=name, grid=(m // tm, n // tn), in_specs=[a_spec, b_spec] + [pl.BlockSpec(memory_space=pl.ANY)] * len(extra),
        out_specs=pl.BlockSpec((tm, tn), lambda i, j: (i, j)),
        out_shape=jax.ShapeDtypeStruct((m, n), out_dtype), compiler_params=_params(2),
    )(a, b, *extra)


def _rstd(v):
    return lax.rsqrt(jnp.mean(v * v, axis=-1, keepdims=True) + RMS_EPS)


def _row_spec(tm, d):
    return pl.BlockSpec((tm, d), lambda i: (i, 0))


def _vec_spec(d, rows=1):
    return pl.BlockSpec((rows, d), lambda i: (0, 0))


def _proj_spec(a, w, tm):
    return [_row_spec(tm, a.shape[1]), pl.BlockSpec(w.shape, lambda i: (0, 0))]


def _out_proj_postnorm_prenorm(a, w, x, g_post, gate, g_pre, scale, shift, name):
    s, d = x.shape
    tm = _row_tile(s, 512)

    def body(a_ref, w_ref, x_ref, gp_ref, gate_ref, g_ref, sc_ref, sh_ref, y_ref, x2_ref, h_ref):
        yv = jnp.dot(a_ref[...], w_ref[...], preferred_element_type=f32)
        y_ref[...] = yv
        x2 = x_ref[...] + gate_ref[...] * (yv * _rstd(yv) * gp_ref[...])
        x2_ref[...] = x2
        h_ref[...] = ((x2 * _rstd(x2) * g_ref[...]) * (1.0 + sc_ref[...]) + sh_ref[...]).astype(bf16)

    return pl.pallas_call(
        body, name=name, grid=(s // tm,), in_specs=_proj_spec(a, w, tm) + [_row_spec(tm, d)] + [_vec_spec(d)] * 5,
        out_specs=[_row_spec(tm, d)] * 3,
        out_shape=[jax.ShapeDtypeStruct((s, d), f32)] * 2 + [jax.ShapeDtypeStruct((s, d), bf16)], compiler_params=_params(1),
    )(a, w, x, g_post, gate, g_pre, scale, shift)


def _rms_bwd(u, v, r):
    return r * u - v * (r * r * r) * jnp.mean(u * v, axis=-1, keepdims=True)


def _out_proj_loss_tail(a, w, x, g, gate, target, name):
    s, d = x.shape
    tm = _row_tile(s, 512)

    def body(a_ref, w_ref, x_ref, g_ref, gate_ref, t_ref, loss_ref, do_ref, dy_ref, vec_ref):
        @pl.when(pl.program_id(0) == 0)
        def _():
            loss_ref[...] = jnp.zeros_like(loss_ref)
            vec_ref[...] = jnp.zeros_like(vec_ref)
        yv = jnp.dot(a_ref[...], w_ref[...], preferred_element_type=f32)
        r = _rstd(yv)
        yn = yv * r
        err = x_ref[...] + gate_ref[...] * (yn * g_ref[...]) - t_ref[...]
        loss_ref[...] += 0.5 * jnp.sum(jnp.mean(err * err, axis=-1, keepdims=True), axis=0, keepdims=True)
        dr = err / d
        do_ref[...] = dr
        dn = dr * gate_ref[...]
        vec_ref[0:1, :] += jnp.sum(dr * (yn * g_ref[...]), axis=0, keepdims=True)
        vec_ref[1:2, :] += jnp.sum(dn * yn, axis=0, keepdims=True)
        dy_ref[...] = _rms_bwd(dn * g_ref[...], yv, r).astype(bf16)

    return pl.pallas_call(
        body, name=name, grid=(s // tm,),
        in_specs=_proj_spec(a, w, tm) + [_row_spec(tm, d)] + [_vec_spec(d)] * 2 + [_row_spec(tm, d)],
        out_specs=[_vec_spec(LANES), _row_spec(tm, d), _row_spec(tm, d), _vec_spec(d, 8)],
        out_shape=[jax.ShapeDtypeStruct((1, LANES), f32), jax.ShapeDtypeStruct((s, d), f32),
                   jax.ShapeDtypeStruct((s, d), bf16), jax.ShapeDtypeStruct((8, d), f32)],
        compiler_params=_params(1),
    )(a, w, x, g, gate, target)


def _dgrad_prenorm_bwd(terms, x, g, scale, dres, name, after=None, below=None):
    s, d = x.shape
    n = len(terms)
    k = sum(a.shape[1] for a, _, _ in terms)
    row_bytes = 2 * (2 * k) + d * (4 + 2 * 4 * 3 + (2 * 4 + 2 * 2 if below else 0))
    tm = next(t for t in (512, 256, 128) if s % t == 0 and 4 * k * d + t * row_bytes <= MATMUL_VMEM_BUDGET)
    extra = [] if after is None else [after]

    def body(*refs):
        a_refs, b_refs = refs[:n], refs[n:2 * n]
        x_ref, g_ref, sc_ref, dr_ref = refs[2 * n:2 * n + 4]
        n_in = 2 * n + 4 + (3 if below else 0) + len(extra)
        dx_ref, vec_ref = refs[n_in], refs[n_in + 1]
        if below:
            y_ref, gp_ref, gate_ref = refs[2 * n + 4:2 * n + 7]
            dy_ref, vec2_ref = refs[n_in + 2], refs[n_in + 3]

        @pl.when(pl.program_id(0) == 0)
        def _():
            vec_ref[...] = jnp.zeros_like(vec_ref)
            if below:
                vec2_ref[...] = jnp.zeros_like(vec2_ref)
        dhv = jnp.dot(a_refs[0][...], b_refs[0][...], preferred_element_type=f32)
        for i in range(1, n):
            dhv = dhv + jnp.dot(a_refs[i][...], b_refs[i][...], preferred_element_type=f32)
        xv = x_ref[...]
        r = _rstd(xv)
        xn = xv * r
        dn = dhv * (1.0 + sc_ref[...])
        vec_ref[0:1, :] += jnp.sum(dhv, axis=0, keepdims=True)
        vec_ref[1:2, :] += jnp.sum(dhv * (xn * g_ref[...]), axis=0, keepdims=True)
        vec_ref[2:3, :] += jnp.sum(dn * xn, axis=0, keepdims=True)
        dx = dr_ref[...] + _rms_bwd(dn * g_ref[...], xv, r)
        dx_ref[...] = dx
        if below:
            yv = y_ref[...]
            ry = _rstd(yv)
            yn = yv * ry
            dny = dx * gate_ref[...]
            vec2_ref[0:1, :] += jnp.sum(dx * (yn * gp_ref[...]), axis=0, keepdims=True)
            vec2_ref[1:2, :] += jnp.sum(dny * yn, axis=0, keepdims=True)
            dy_ref[...] = _rms_bwd(dny * gp_ref[...], yv, ry).astype(bf16)

    in_specs = ([_row_spec(tm, a.shape[1]) for a, _, _ in terms]
                + [pl.BlockSpec((a.shape[1], d), lambda i, r=r: (r, 0)) for a, _, r in terms]
                + [_row_spec(tm, d)] + [_vec_spec(d)] * 2 + [_row_spec(tm, d)])
    out_specs = [_row_spec(tm, d), _vec_spec(d, 8)]
    out_shape = [jax.ShapeDtypeStruct((s, d), f32), jax.ShapeDtypeStruct((8, d), f32)]
    args = [a for a, _, _ in terms] + [b for _, b, _ in terms] + [x, g, scale, dres]
    if below:
        in_specs += [_row_spec(tm, d)] + [_vec_spec(d)] * 2
        out_specs += [_row_spec(tm, d), _vec_spec(d, 8)]
        out_shape += [jax.ShapeDtypeStruct((s, d), bf16), jax.ShapeDtypeStruct((8, d), f32)]
        args += list(below)
    return pl.pallas_call(
        body, name=name, grid=(s // tm,), in_specs=in_specs + [pl.BlockSpec(memory_space=pl.ANY)] * len(extra),
        out_specs=out_specs, out_shape=out_shape, compiler_params=_params(1),
    )(*args, *extra)


def _lane():
    return lax.broadcasted_iota(jnp.int32, (1, LANES), 1)


def _rope_tables(pos_col, inv_freq, name):
    s = pos_col.shape[0]

    def body(p_ref, f_ref, cos_ref, sin_ref):
        ang = p_ref[...].astype(f32) * f_ref[...]
        first_half = (_lane() % HEAD_DIM) < HEAD_DIM // 2
        cos_ref[...] = jnp.cos(ang)
        sn = jnp.sin(ang)
        sin_ref[...] = jnp.where(first_half, -sn, sn)

    return pl.pallas_call(
        body, name=name, out_shape=[jax.ShapeDtypeStruct((s, LANES), f32)] * 2, compiler_params=_params(),
    )(pos_col, inv_freq)


def _swap_halves(v):
    first_half = (_lane() % HEAD_DIM) < HEAD_DIM // 2
    return jnp.where(first_half, pltpu.roll(v, LANES - HEAD_DIM // 2, axis=1), pltpu.roll(v, HEAD_DIM // 2, axis=1))


def _prenorm_proj_qkv(x, g, mod_scale, mod_shift, w_qkv_t, cos, sin_s, name):
    s, d = x.shape
    tm = _row_tile(s, 512)
    scale = 1.0 / math.sqrt(HEAD_DIM)

    def body(x_ref, g_ref, msc_ref, msh_ref, w_ref, c_ref, s_ref, h_ref, qa_ref, ka_ref, va_ref, qb_ref, kb_ref, vb_ref):
        xv = x_ref[...]
        h = ((xv * _rstd(xv) * g_ref[...]) * (1.0 + msc_ref[...]) + msh_ref[...]).astype(bf16)
        h_ref[...] = h
        proj = lax.dot_general(h, w_ref[...], _NT, preferred_element_type=f32)
        cs, sn = c_ref[...], s_ref[...]
        low = _lane() < HEAD_DIM

        def blk(j):
            return proj[:, j * LANES:(j + 1) * LANES]

        def rope(v):
            return v * cs + _swap_halves(v) * sn

        def expand(v):
            other = pltpu.roll(v, HEAD_DIM, axis=1)
            return jnp.where(low, v, other), jnp.where(low, other, v)

        for j in range(N_PAIRS):
            qa_ref[:, j * LANES:(j + 1) * LANES] = (rope(blk(j)) * scale).astype(bf16)
            qb_ref[:, j * LANES:(j + 1) * LANES] = (blk(6 + j) * scale).astype(bf16)
            kb_ref[:, j * LANES:(j + 1) * LANES] = blk(10 + j).astype(bf16)
            vb_ref[:, j * LANES:(j + 1) * LANES] = blk(14 + j).astype(bf16)
        k0, k1 = expand(rope(blk(4)))
        v0, v1 = expand(blk(5))
        for j in range(N_PAIRS):
            ka_ref[:, j * LANES:(j + 1) * LANES] = (k0 if j < 2 else k1).astype(bf16)
            va_ref[:, j * LANES:(j + 1) * LANES] = (v0 if j < 2 else v1).astype(bf16)

    hw = N_PAIRS * LANES
    return pl.pallas_call(
        body, name=name, grid=(s // tm,),
        in_specs=[_row_spec(tm, d)] + [_vec_spec(d)] * 3
        + [pl.BlockSpec((QKV_W, d), lambda i: (0, 0)), _row_spec(tm, LANES), _row_spec(tm, LANES)],
        out_specs=[_row_spec(tm, d)] + [_row_spec(tm, hw)] * 6,
        out_shape=[jax.ShapeDtypeStruct((s, d), bf16)] + [jax.ShapeDtypeStruct((s, hw), bf16)] * 6, compiler_params=_params(1),
    )(x, g, mod_scale, mod_shift, w_qkv_t, cos, sin_s)


def _qkv_prep_bwd(dqa_t, dka, dva, dqb_t, dkb, dvb, cos, sin_s, name):
    s = dka.shape[0]
    tm = _row_tile(s, 512)
    scale = 1.0 / math.sqrt(HEAD_DIM)
    hw = N_PAIRS * LANES
    t_spec = pl.BlockSpec((hw, tm), lambda i: (0, i))

    def body(dqa_ref, dka_ref, dva_ref, dqb_ref, dkb_ref, dvb_ref, c_ref, s_ref, o_ref):
        cs, sn = c_ref[...], s_ref[...]
        low = _lane() < HEAD_DIM

        def blk(ref, j):
            return ref[:, j * LANES:(j + 1) * LANES].astype(f32)

        def blk_t(ref, j):
            return ref[j * LANES:(j + 1) * LANES, :].T

        def unrope(v):
            return v * cs + _swap_halves(v * sn)

        def fold(ref):
            a, b = blk(ref, 0) + blk(ref, 1), blk(ref, 2) + blk(ref, 3)
            kv0 = a + pltpu.roll(a, HEAD_DIM, axis=1)
            kv1 = b + pltpu.roll(b, HEAD_DIM, axis=1)
            return jnp.where(low, kv0, kv1)

        for j in range(N_PAIRS):
            o_ref[:, j * LANES:(j + 1) * LANES] = (unrope(blk_t(dqa_ref, j)) * scale).astype(bf16)
            o_ref[:, (6 + j) * LANES:(7 + j) * LANES] = (blk_t(dqb_ref, j) * scale).astype(bf16)
            o_ref[:, (10 + j) * LANES:(11 + j) * LANES] = blk(dkb_ref, j).astype(bf16)
            o_ref[:, (14 + j) * LANES:(15 + j) * LANES] = blk(dvb_ref, j).astype(bf16)
        o_ref[:, 4 * LANES:5 * LANES] = unrope(fold(dka_ref)).astype(bf16)
        o_ref[:, 5 * LANES:6 * LANES] = fold(dva_ref).astype(bf16)

    return pl.pallas_call(
        body, name=name, grid=(s // tm,),
        in_specs=[t_spec, _row_spec(tm, hw), _row_spec(tm, hw), t_spec, _row_spec(tm, hw), _row_spec(tm, hw)] + [_row_spec(tm, LANES)] * 2,
        out_specs=_row_spec(tm, QKV_W), out_shape=jax.ShapeDtypeStruct((s, QKV_W), bf16), compiler_params=_params(1),
    )(dqa_t, dka, dva, dqb_t, dkb, dvb, cos, sin_s)


def _cumsum_rows(v, reverse=False):
    n = v.shape[0]
    row = lax.broadcasted_iota(jnp.int32, v.shape, 0)
    sh = 1
    while sh < n:
        if reverse:
            v = v + jnp.where(row < n - sh, pltpu.roll(v, n - sh, axis=0), 0.0)
        else:
            v = v + jnp.where(row >= sh, pltpu.roll(v, sh, axis=0), 0.0)
        sh *= 2
    return v


def _log_sigmoid(z):
    return jnp.minimum(z, 0.0) - jnp.log1p(jnp.exp(-jnp.abs(z)))


def _forget_prep(h, w_f_t, bf_row, name):
    s, d = h.shape
    tm = _row_tile(s, 1024)

    def body(h_ref, w_ref, b_ref, f_ref, cb_ref, last_ref):
        @pl.when(pl.program_id(0) == 0)
        def _():
            last_ref[...] = jnp.zeros_like(last_ref)
        fl = lax.dot_general(h_ref[...], w_ref[...], _NT, preferred_element_type=f32)
        f_ref[...] = fl
        cum = _cumsum_rows(_log_sigmoid(fl + b_ref[...])) + last_ref[0:1, :]
        last_ref[0:1, :] = cum[tm - 1:tm, :]
        for hd in range(N_HEADS):
            cb_ref[:, hd * LANES:(hd + 1) * LANES] = jnp.broadcast_to(cum[:, hd:hd + 1], (tm, LANES))

    return pl.pallas_call(
        body, name=name, grid=(s // tm,),
        in_specs=[_row_spec(tm, d), pl.BlockSpec((LANES, d), lambda i: (0, 0)), _vec_spec(LANES)],
        out_specs=[_row_spec(tm, LANES), _row_spec(tm, N_HEADS * LANES)],
        out_shape=[jax.ShapeDtypeStruct((s, LANES), f32), jax.ShapeDtypeStruct((s, N_HEADS * LANES), f32)],
        scratch_shapes=[pltpu.VMEM((8, LANES), f32)], compiler_params=_params(1),
    )(h, w_f_t, bf_row)


def _forget_prep_bwd(rs, dcs, fl, bf_row, name):
    s = fl.shape[0]
    tm = _row_tile(s, 1024)
    n = s // tm

    def body(r_ref, c_ref, f_ref, b_ref, df_ref, db_ref, next_ref):
        @pl.when(pl.program_id(0) == 0)
        def _():
            next_ref[...] = jnp.zeros_like(next_ref)
            db_ref[...] = jnp.zeros_like(db_ref)
        eye = (lax.broadcasted_iota(jnp.int32, (N_HEADS, LANES), 0) == lax.broadcasted_iota(jnp.int32, (N_HEADS, LANES), 1)).astype(f32)
        dcum = lax.dot_general(r_ref[...], eye, _TN, precision=lax.Precision.HIGHEST, preferred_element_type=f32)
        for h in range(N_HEADS):
            dcum = dcum - jnp.where(_lane() == h, jnp.sum(c_ref[:, h * LANES:(h + 1) * LANES], axis=1, keepdims=True), 0.0)
        dlf = _cumsum_rows(dcum, reverse=True) + next_ref[0:1, :]
        next_ref[0:1, :] = dlf[0:1, :]
        z = f_ref[...] + b_ref[...]
        df = jnp.where(_lane() < N_HEADS, dlf * jax.nn.sigmoid(-z), 0.0)
        df_ref[...] = df.astype(bf16)
        db_ref[0:1, :] += jnp.sum(df, axis=0, keepdims=True)

    def rows(width):
        return pl.BlockSpec((tm, width), lambda i: (n - 1 - i, 0))

    return pl.pallas_call(
        body, name=name, grid=(n,),
        in_specs=[pl.BlockSpec((N_HEADS, tm), lambda i: (0, n - 1 - i)), rows(N_HEADS * LANES), rows(LANES), _vec_spec(LANES)],
        out_specs=[rows(LANES), _vec_spec(LANES, 8)],
        out_shape=[jax.ShapeDtypeStruct((s, LANES), bf16), jax.ShapeDtypeStruct((8, LANES), f32)],
        scratch_shapes=[pltpu.VMEM((8, LANES), f32)], compiler_params=_params(1),
    )(rs, dcs, fl, bf_row)


def _tile_mask(n_keys, n_queries, off, window):
    shape = (n_keys, n_queries)
    d = lax.broadcasted_iota(jnp.int32, shape, 1) - lax.broadcasted_iota(jnp.int32, shape, 0) + off
    valid = d >= 0
    return jnp.logical_and(valid, d < window) if window else valid


def _wide(v, t):
    return jnp.concatenate([v] * (t // LANES), axis=1)


def _attn_fwd(q, k, v, name, *, cum_b=None, sink_rows=None, window=None, t=256):
    s = q.shape[0]
    t = _row_tile(s, t)
    fox, has_sink = cum_b is not None, sink_rows is not None
    assert not window or (window % LANES == 0 and LANES + window <= s)

    def body(*refs):
        q_ref, k_ref, v_ref = refs[:3]
        rest = list(refs[3:])
        cb_ref = rest.pop(0) if fox else None
        sink_ref = rest.pop(0) if has_sink else None
        o_ref, lse_ref = rest
        i = pl.program_id(1)
        low = _lane() < HEAD_DIM
        top = lax.broadcasted_iota(jnp.int32, (LANES, 1), 0) < HEAD_DIM
        q2 = q_ref[...]
        zero = jnp.zeros_like(q2)
        qms = (jnp.where(low, q2, zero), jnp.where(low, zero, q2))

        def tile(k0, n_keys, off, carry, masked, queries=slice(0, t)):
            nq = queries.stop - queries.start
            kblk, vblk = k_ref[pl.ds(k0, n_keys), :], v_ref[pl.ds(k0, n_keys), :]
            valid = _tile_mask(n_keys, nq, off, window) if masked else None
            ones = jnp.ones_like(vblk)
            vs = tuple(jnp.where(_lane() == L_ROW[h], ones, vblk) for h in range(2))

            def scores(h):
                return lax.dot_general(kblk, qms[h][queries], _NT, preferred_element_type=f32)

            def softmax(h, sc):
                m = carry[h][0]
                if fox:
                    sc = sc - _wide(cb_ref[pl.ds(k0, n_keys), h * LANES:(h + 1) * LANES], nq)
                if masked:
                    sc = jnp.where(valid, sc, NEG)
                m_new = jnp.maximum(m, jnp.max(sc, axis=0, keepdims=True))
                return m_new, jnp.exp(m - m_new), jnp.exp(sc - m_new).astype(bf16)

            def update(h, m_new, alpha, p):
                return m_new, alpha * carry[h][1] + lax.dot_general(vs[h], p, _TN, preferred_element_type=f32)

            if window:
                return tuple(update(h, *softmax(h, scores(h))) for h in range(2))
            scs = [scores(h) for h in range(2)]
            stats = [softmax(h, scs[h]) for h in range(2)]
            return tuple(update(h, *stats[h]) for h in range(2))

        def start(nq):
            if has_sink:
                row = lax.broadcasted_iota(jnp.int32, (LANES, nq), 0)
                return tuple((_wide(sink_ref[h:h + 1, :], nq), (row == L_ROW[h]).astype(f32)) for h in range(2))
            return tuple((jnp.full((1, nq), NEG, f32), jnp.zeros((LANES, nq), f32)) for h in range(2))

        def finish(carry, queries):
            (m0, a0), (m1, a1) = carry
            l0, l1 = a0[L_ROW[0]:L_ROW[0] + 1, :], a1[L_ROW[1]:L_ROW[1] + 1, :]
            o_t = jnp.where(top, a0 * (1.0 / l0), a1 * (1.0 / l1))
            o_ref[queries, :] = o_t.T.astype(bf16)
            lse_ref[0:1, queries] = m0 + jnp.log(l0)
            lse_ref[1:2, queries] = m1 + jnp.log(l1)

        if window:
            for c in range(t // LANES):
                queries = slice(c * LANES, (c + 1) * LANES)
                q0 = i * t + c * LANES
                k0 = pl.multiple_of(jnp.maximum(q0 - window, 0), LANES)
                finish(tile(k0, LANES + window, q0 - k0, start(LANES), True, queries), queries)
        else:
            half, k_own = t // 2, pl.multiple_of(i * t, t)
            carry = lax.fori_loop(0, 2 * i, lambda kb, c: tile(pl.multiple_of(kb * half, half), half, 0, c, False), start(t))
            carry = tile(k_own, half, 0, carry, True)
            finish(tuple((m[:, :half], a[:, :half]) for m, a in carry), slice(0, half))
            carry = tuple((m[:, half:], a[:, half:]) for m, a in carry)
            finish(tile(pl.multiple_of(k_own + half, half), half, 0, carry, True, slice(half, t)), slice(half, t))

    q_spec = pl.BlockSpec((t, LANES), lambda j, i: (i, j))
    kv_spec = pl.BlockSpec((s, LANES), lambda j, i: (0, j))
    in_specs, args = [q_spec, kv_spec, kv_spec], [q, k, v]
    if fox:
        in_specs += [pl.BlockSpec((s, 2 * LANES), lambda j, i: (0, j))]
        args += [cum_b]
    if has_sink:
        in_specs += [pl.BlockSpec((None, 2, LANES), lambda j, i: (j, 0, 0))]
        args += [sink_rows.reshape(N_PAIRS, 2, LANES)]
    return pl.pallas_call(
        body, name=name, grid=(N_PAIRS, s // t), in_specs=in_specs,
        out_specs=[q_spec, pl.BlockSpec((None, 2, t), lambda j, i: (j, 0, i))],
        out_shape=[jax.ShapeDtypeStruct((s, N_PAIRS * LANES), bf16), jax.ShapeDtypeStruct((N_PAIRS, 2, s), f32)],
        compiler_params=_params(2),
    )(*args)


def _branch_dgrad_delta(db, w, o, name, *, lse=None, sink_rows=None, after=None):
    s, hw = o.shape
    tm = _row_tile(s, 1024)
    has_sink = sink_rows is not None
    extra = [] if after is None else [after]

    def body(*refs):
        db_ref, w_ref, o_ref = refs[:3]
        outs = refs[3 + (2 if has_sink else 0) + len(extra):]
        do_ref, dl_ref = outs[:2]
        if has_sink:
            lse_ref, sink_ref = refs[3:5]
            ds_ref = outs[2]

            @pl.when(pl.program_id(0) == 0)
            def _():
                ds_ref[...] = jnp.zeros_like(ds_ref)
        do = lax.dot_general(db_ref[...], w_ref[...], _NT, preferred_element_type=f32).astype(bf16)
        do_ref[...] = do
        for j in range(N_PAIRS):
            cols = slice(j * LANES, (j + 1) * LANES)
            prod_t = (do[:, cols].astype(f32) * o_ref[:, cols].astype(f32)).T
            for h in range(2):
                dl = jnp.sum(prod_t[h * HEAD_DIM:(h + 1) * HEAD_DIM, :], axis=0, keepdims=True)
                dl_ref[j, h:h + 1, :] = dl
                if has_sink:
                    r = 2 * j + h
                    p_sink = jnp.exp(sink_ref[r:r + 1, 0:1] - lse_ref[j, h:h + 1, :])
                    ds_ref[r:r + 1, :] += -jnp.sum(p_sink * dl, axis=1, keepdims=True)

    rows_spec = pl.BlockSpec((N_PAIRS, 2, tm), lambda i: (0, 0, i))
    in_specs = [_row_spec(tm, db.shape[1]), pl.BlockSpec(w.shape, lambda i: (0, 0)), _row_spec(tm, hw)]
    args = [db, w, o]
    out_specs = [_row_spec(tm, hw), rows_spec]
    out_shape = [jax.ShapeDtypeStruct((s, hw), bf16), jax.ShapeDtypeStruct((N_PAIRS, 2, s), f32)]
    if has_sink:
        in_specs += [rows_spec, _vec_spec(LANES, N_HEADS)]
        args += [lse, sink_rows]
        out_specs += [_vec_spec(LANES, N_HEADS)]
        out_shape += [jax.ShapeDtypeStruct((N_HEADS, LANES), f32)]
    return pl.pallas_call(
        body, name=name, grid=(s // tm,), in_specs=in_specs + [pl.BlockSpec(memory_space=pl.ANY)] * len(extra),
        out_specs=out_specs, out_shape=out_shape, compiler_params=_params(1),
    )(*args, *extra)


def _attn_bwd(q, k, v, do, lse, delta, name, *, cum_b=None, window=None, t=256):
    s = q.shape[0]
    t = _row_tile(s, t)
    nblk = s // t
    fox = cum_b is not None
    assert not window or (window % LANES == 0 and LANES + window <= s)

    def body(*refs):
        k_ref, v_ref, q_ref, do_ref, lse_ref, dl_ref = refs[:6]
        rest = list(refs[6:])
        cb_ref = rest.pop(0) if fox else None
        dq_ref, dk_ref, dv_ref = rest[:3]
        dcs_ref, rs_ref = (rest[3], rest[4]) if fox else (None, None)
        dk_acc, dv_acc = rest[-2:]
        b = pl.program_id(1)
        k0 = pl.multiple_of(b * t, t)

        @pl.when(b == 0)
        def _():
            dq_ref[...] = jnp.zeros_like(dq_ref)
            if fox:
                rs_ref[...] = jnp.zeros_like(rs_ref)

        dk_acc[...] = jnp.zeros_like(dk_acc)
        dv_acc[...] = jnp.zeros_like(dv_acc)
        if fox:
            dcs_ref[...] = jnp.zeros_like(dcs_ref)
        low = _lane() < HEAD_DIM
        top = lax.broadcasted_iota(jnp.int32, (LANES, 1), 0) < HEAD_DIM
        kblk, vblk = k_ref[...], v_ref[...]
        k_t = kblk.astype(f32).T.astype(bf16)
        cks = [_wide(cb_ref[pl.ds(k0, t), h * LANES:(h + 1) * LANES], t) for h in range(2)] if fox else None

        def tile(q0, n_queries, off, masked, keys=slice(0, t)):
            cols = pl.ds(q0, n_queries)
            q2, do2 = q_ref[cols, :], do_ref[cols, :]
            zero = jnp.zeros_like(q2)
            valid = _tile_mask(keys.stop - keys.start, n_queries, off, window) if masked else None
            dq_parts = []
            for h in range(2):
                qm = jnp.where(low, q2, zero) if h == 0 else jnp.where(low, zero, q2)
                dom = jnp.where(low, do2, zero) if h == 0 else jnp.where(low, zero, do2)
                sc = lax.dot_general(kblk[keys], qm, _NT, preferred_element_type=f32)
                if fox:
                    sc = sc - cks[h][keys, :n_queries]
                if masked:
                    sc = jnp.where(valid, sc, NEG)
                p = jnp.exp(sc - lse_ref[h:h + 1, cols])
                dp = lax.dot_general(vblk[keys], dom, _NT, preferred_element_type=f32)
                ds = p * (dp - dl_ref[h:h + 1, cols])
                pb, dsb = p.astype(bf16), ds.astype(bf16)
                dv_acc[keys, :] += jnp.dot(pb, dom, preferred_element_type=f32)
                dk_acc[keys, :] += jnp.dot(dsb, qm, preferred_element_type=f32)
                dq_parts.append(jnp.dot(k_t[:, keys], dsb, preferred_element_type=f32))
                if fox:
                    dcs_ref[keys, h * LANES:(h + 1) * LANES] += sum(ds[:, g * LANES:(g + 1) * LANES]
                                                                    for g in range(n_queries // LANES))
                    rs_ref[h:h + 1, cols] += jnp.sum(ds, axis=0, keepdims=True)
            dq_ref[:, cols] += jnp.where(top, dq_parts[0], dq_parts[1])

        def later_block(qb, carry):
            tile(pl.multiple_of(qb * t, t), t, 0, False)
            return carry

        if window:
            for c in range(t // LANES):
                first = b * t + c * LANES
                q0 = pl.multiple_of(jnp.minimum(first, s - (LANES + window)), LANES)
                tile(q0, LANES + window, q0 - first, True, slice(c * LANES, (c + 1) * LANES))
        else:
            half = t // 2
            tile(k0, half, 0, True, slice(0, half))
            tile(pl.multiple_of(k0 + half, half), half, half, True)
            lax.fori_loop(b + 1, nblk, later_block, 0)
        dk_ref[...] = dk_acc[...].astype(bf16)
        dv_ref[...] = dv_acc[...].astype(bf16)

    kv_spec = pl.BlockSpec((t, LANES), lambda j, b: (b, j))
    seq_spec = pl.BlockSpec((s, LANES), lambda j, b: (0, j))
    rows_spec = pl.BlockSpec((None, 2, s), lambda j, b: (j, 0, 0))
    hw = N_PAIRS * LANES
    in_specs, args = [kv_spec, kv_spec, seq_spec, seq_spec, rows_spec, rows_spec], [k, v, q, do, lse, delta]
    out_specs = [pl.BlockSpec((LANES, s), lambda j, b: (j, 0)), kv_spec, kv_spec]
    out_shape = [jax.ShapeDtypeStruct((hw, s), f32), jax.ShapeDtypeStruct((s, hw), bf16), jax.ShapeDtypeStruct((s, hw), bf16)]
    if fox:
        in_specs += [pl.BlockSpec((s, 2 * LANES), lambda j, b: (0, j))]
        args += [cum_b]
        out_specs += [pl.BlockSpec((t, 2 * LANES), lambda j, b: (b, j)), rows_spec]
        out_shape += [jax.ShapeDtypeStruct((s, N_HEADS * LANES), f32), jax.ShapeDtypeStruct((N_PAIRS, 2, s), f32)]
    return pl.pallas_call(
        body, name=name, grid=(N_PAIRS, nblk), in_specs=in_specs, out_specs=out_specs, out_shape=out_shape,
        scratch_shapes=[pltpu.VMEM((t, LANES), f32)] * 2, compiler_params=_params(2),
    )(*args)


def _branch_merge(o_a, o_b, w_a, w_b, gl, name):
    s, k = o_a.shape
    d = w_a.shape[1]
    tm = _row_tile(s, 1024)

    def body(oa_ref, ob_ref, wa_ref, wb_ref, g_ref, ba_ref, bb_ref, m_ref):
        ba = jnp.dot(oa_ref[...], wa_ref[...], preferred_element_type=f32)
        bb = jnp.dot(ob_ref[...], wb_ref[...], preferred_element_type=f32)
        g0, g1 = jax.nn.sigmoid(g_ref[:, :d].astype(f32)), jax.nn.sigmoid(g_ref[:, d:].astype(f32))
        ba_ref[...] = ba.astype(bf16)
        bb_ref[...] = bb.astype(bf16)
        m_ref[...] = (g0 * ba + g1 * bb).astype(bf16)

    whole = pl.BlockSpec((k, d), lambda i: (0, 0))
    return pl.pallas_call(
        body, name=name, grid=(s // tm,),
        in_specs=[_row_spec(tm, k), _row_spec(tm, k), whole, whole, _row_spec(tm, 2 * d)],
        out_specs=[_row_spec(tm, d)] * 3, out_shape=[jax.ShapeDtypeStruct((s, d), bf16)] * 3, compiler_params=_params(1),
    )(o_a, o_b, w_a, w_b, gl)


def _out_dgrad_merge_bwd(dy, w_out, ba, bb, gl, name):
    s, d = ba.shape
    tm = _row_tile(s, 512)

    def body(dy_ref, w_ref, a_ref, b_ref, g_ref, da_ref, db_ref, dg_ref):
        dmv = lax.dot_general(dy_ref[...], w_ref[...], _NT, preferred_element_type=f32)
        g0, g1 = jax.nn.sigmoid(g_ref[:, :d].astype(f32)), jax.nn.sigmoid(g_ref[:, d:].astype(f32))
        da_ref[...] = (dmv * g0).astype(bf16)
        db_ref[...] = (dmv * g1).astype(bf16)
        dg_ref[:, :d] = (dmv * a_ref[...].astype(f32) * (g0 * (1.0 - g0))).astype(bf16)
        dg_ref[:, d:] = (dmv * b_ref[...].astype(f32) * (g1 * (1.0 - g1))).astype(bf16)

    return pl.pallas_call(
        body, name=name, grid=(s // tm,),
        in_specs=[_row_spec(tm, dy.shape[1]), pl.BlockSpec(w_out.shape, lambda i: (0, 0))] + [_row_spec(tm, d)] * 2
        + [_row_spec(tm, 2 * d)],
        out_specs=[_row_spec(tm, d)] * 2 + [_row_spec(tm, 2 * d)],
        out_shape=[jax.ShapeDtypeStruct((s, d), bf16)] * 2 + [jax.ShapeDtypeStruct((s, 2 * d), bf16)],
        compiler_params=_params(1),
    )(dy, w_out, ba, bb, gl)


GLU_TILE = 256


def _ffn_in_swiglu(h, w_t, name):
    s, d = h.shape
    f = w_t.shape[0] // 2
    tm = _row_tile(s, 4096)
    tg = GLU_TILE
    nb = f // tg

    def body(h_ref, wg_ref, wu_ref, g_ref, u_ref, act_ref):
        hv = h_ref[...]
        g = lax.dot_general(hv, wg_ref[...], _NT, preferred_element_type=f32)
        u = lax.dot_general(hv, wu_ref[...], _NT, preferred_element_type=f32)
        g_ref[...] = g.astype(bf16)
        u_ref[...] = u.astype(bf16)
        act_ref[...] = (g * jax.nn.sigmoid(g) * u).astype(bf16)

    col = pl.BlockSpec((tm, tg), lambda i, j: (i, j))
    return pl.pallas_call(
        body, name=name, grid=(s // tm, nb),
        in_specs=[pl.BlockSpec((tm, d), lambda i, j: (i, 0)), pl.BlockSpec((tg, d), lambda i, j: (j, 0)),
                  pl.BlockSpec((tg, d), lambda i, j: (j + nb, 0))],
        out_specs=[col] * 3, out_shape=[jax.ShapeDtypeStruct((s, f), bf16)] * 3, compiler_params=_params(2),
    )(h, w_t, w_t)


def _ffn_out_dgrad_swiglu(dy, w_out, g, u, name):
    s, d = dy.shape
    f = g.shape[1]
    tm = _row_tile(s, 4096)
    tg = GLU_TILE

    def body(dy_ref, w_ref, g_ref, u_ref, dg_ref, du_ref):
        dv = lax.dot_general(dy_ref[...], w_ref[...], _NT, preferred_element_type=f32)
        gv, uv = g_ref[...].astype(f32), u_ref[...].astype(f32)
        sg = jax.nn.sigmoid(gv)
        dg_ref[...] = (dv * uv * (sg * (1.0 + gv * (1.0 - sg)))).astype(bf16)
        du_ref[...] = (dv * (gv * sg)).astype(bf16)

    col = pl.BlockSpec((tm, tg), lambda i, j: (i, j))
    return pl.pallas_call(
        body, name=name, grid=(s // tm, f // tg),
        in_specs=[pl.BlockSpec((tm, d), lambda i, j: (i, 0)), pl.BlockSpec((tg, d), lambda i, j: (j, 0)), col, col],
        out_specs=[col] * 2, out_shape=[jax.ShapeDtypeStruct((s, f), bf16)] * 2, compiler_params=_params(2),
    )(dy, w_out, g, u)


def _wgrad_stack(parts, h, name):
    s, m = parts[0].shape
    d = h.shape[1]
    tm = 256
    nb = m // tm
    n = len(parts)

    def body(*refs):
        i = pl.program_id(0)
        for p in range(n):
            @pl.when(i // nb == p)
            def _(p=p):
                refs[n + 1][...] = lax.dot_general(refs[p][...], refs[n][...], _TN, preferred_element_type=f32).astype(bf16)

    a_specs = [pl.BlockSpec((s, tm), lambda i, p=p: (0, jnp.clip(i - p * nb, 0, nb - 1))) for p in range(n)]
    return pl.pallas_call(
        body, name=name, grid=(n * nb,), in_specs=a_specs + [pl.BlockSpec((s, d), lambda i: (0, 0))],
        out_specs=pl.BlockSpec((tm, d), lambda i: (i, 0)),
        out_shape=jax.ShapeDtypeStruct((n * m, d), bf16), compiler_params=_params(1),
    )(*parts, h)


def _ada_wgrad(c_all, d_all, name):
    n, d = c_all.shape
    w = d_all.shape[1]

    def body(c_ref, d_ref, o_ref):
        eye = (lax.broadcasted_iota(jnp.int32, (n, n), 0) == lax.broadcasted_iota(jnp.int32, (n, n), 1)).astype(f32)
        ct = lax.dot_general(c_ref[...], eye, _TN, precision=lax.Precision.HIGHEST, preferred_element_type=f32)
        g = ct[:, 0:1] * d_ref[0:1, :]
        for bi in range(1, n):
            g = g + ct[:, bi:bi + 1] * d_ref[bi:bi + 1, :]
        o_ref[0] = g

    return pl.pallas_call(
        body, name=name, out_shape=jax.ShapeDtypeStruct((1, d, w), f32), compiler_params=_params(),
    )(c_all, d_all)


def _adamw(parts, w, m, v, name, mine=None, me=None):
    r, c = w.shape
    n_parts = parts.shape[0]
    row_tiles = [t for t in range(min(r, 256), 0, -1) if r % t == 0 and (t % 16 == 0 or t == r)]
    if row_tiles:
        tr, tc = row_tiles[0], c
    else:
        tr, tc = r, next(t for t in (256, LANES) if c % t == 0)

    def body(*refs):
        w_ref, m_ref, v_ref, g_ref, d_ref, nm_ref, nv_ref = refs[-7:]
        if mine is None:
            p_ref, = refs[:-7]
        else:
            me_ref, p_ref, own_ref = refs[:-7]

        def part(i):
            if mine is None:
                return p_ref[i].astype(f32)
            return jnp.where(me_ref[0] == i, own_ref[...], p_ref[i]).astype(f32)

        g = part(0)
        for i in range(1, n_parts):
            g = g + part(i)
        mm = ADAM_B1 * m_ref[...] + (1.0 - ADAM_B1) * g
        vv = ADAM_B2 * v_ref[...] + (1.0 - ADAM_B2) * (g * g)
        m_hat = mm / (1.0 - ADAM_B1 ** ADAM_STEP)
        v_hat = vv / (1.0 - ADAM_B2 ** ADAM_STEP)
        g_ref[...] = g
        d_ref[...] = -ADAM_LR * (m_hat / (jnp.sqrt(v_hat) + ADAM_EPS) + ADAM_WD * w_ref[...])
        nm_ref[...] = mm
        nv_ref[...] = vv

    out_shape = [jax.ShapeDtypeStruct((r, c), f32)] * 4
    if mine is None:
        spec = pl.BlockSpec((tr, tc), lambda i, j: (i, j))
        return pl.pallas_call(
            body, name=name, grid=(r // tr, c // tc),
            in_specs=[pl.BlockSpec((n_parts, tr, tc), lambda i, j: (0, i, j))] + [spec] * 3,
            out_specs=[spec] * 4, out_shape=out_shape, compiler_params=_params(2),
        )(parts, w, m, v)
    spec = pl.BlockSpec((tr, tc), lambda i, j, me_ref: (i, j))
    return pl.pallas_call(
        body, name=name, out_shape=out_shape, compiler_params=_params(2),
        grid_spec=pltpu.PrefetchScalarGridSpec(
            num_scalar_prefetch=1, grid=(r // tr, c // tc),
            in_specs=[pl.BlockSpec((n_parts, tr, tc), lambda i, j, me_ref: (0, i, j)),
                      pl.BlockSpec((None, tr, tc), lambda i, j, me_ref: (me_ref[0], i, j))] + [spec] * 3,
            out_specs=[spec] * 4),
    )(me, parts, mine, w, m, v)


def _me():
    return lax.axis_index("x"), lax.axis_index("y"), lax.axis_index("c")


def _gather_prologue(c, w_ada, b_mine, w_in_t, name):
    n_dev, d = N_DEV, c.shape[1]
    ada_w = w_ada.shape[1]

    def body(c_ref, w_ref, b_ref, win_ref, call_ref, ada_ref, gin_ref, cols_ref, send_sems, recv_sems, local_sems):
        x, y, cc = _me()
        me, sibling = (x, y, cc), (x, y, 1 - cc)
        chips = [(1 - x, y), (x, 1 - y), (1 - x, 1 - y)]
        outs = (call_ref, ada_ref, gin_ref)

        def rows(a, dev):
            return outs[a].at[4 * dev[0] + 2 * dev[1] + dev[2]]

        def copy(a, k, block, to, src=None):
            return pltpu.make_async_remote_copy(
                src_ref=rows(a, block) if src is None else src, dst_ref=rows(a, block),
                send_sem=send_sems.at[a, k], recv_sem=recv_sems.at[a, k], device_id=to, device_id_type=MESH)

        def begin(a, src):
            own = pltpu.make_async_copy(src, rows(a, me), local_sems.at[a])
            sends = [copy(a, 0, me, sibling, src=src)] + [copy(a, 1 + j, me, (*chip, cc), src=src) for j, chip in enumerate(chips)]
            for cp in [own] + sends:
                cp.start()
            return own, sends

        def finish(a, own, sends):
            passed = []
            for j, chip in enumerate(chips):
                copy(a, 1 + j, (*chip, cc), me).wait_recv()
                passed.append(copy(a, 4 + j, (*chip, cc), sibling))
                passed[-1].start()
            copy(a, 0, sibling, me).wait_recv()
            for j, chip in enumerate(chips):
                copy(a, 4 + j, (*chip, 1 - cc), me).wait_recv()
            for cp in sends + passed:
                cp.wait_send()
            own.wait()

        finish(0, *begin(0, c_ref))
        cols_ref[...] = (jnp.dot(call_ref[:, 0, :].astype(bf16), w_ref[...].astype(bf16), preferred_element_type=f32)
                         + b_ref[...])
        finish(1, *begin(1, cols_ref))
        finish(2, *begin(2, win_ref))

    vmem, hbm = pl.BlockSpec(memory_space=pltpu.VMEM), pl.BlockSpec(memory_space=pl.ANY)
    return pl.pallas_call(
        body, name=name, in_specs=[vmem, vmem, vmem, hbm], out_specs=[vmem, vmem, hbm],
        out_shape=[jax.ShapeDtypeStruct((n_dev, 1, d), f32), jax.ShapeDtypeStruct((n_dev, n_dev, ada_w), f32),
                   jax.ShapeDtypeStruct((n_dev,) + w_in_t.shape, w_in_t.dtype)],
        scratch_shapes=[pltpu.VMEM((n_dev, ada_w), f32), pltpu.SemaphoreType.DMA((3, 7)), pltpu.SemaphoreType.DMA((3, 7)),
                        pltpu.SemaphoreType.DMA((3,))],
        compiler_params=pltpu.CompilerParams(vmem_limit_bytes=VMEM_LIMIT),
    )(c, w_ada, b_mine, w_in_t)


_FLIPS = ((0, 0, 1), (1, 0, 0), (0, 1, 0), (1, 1, 0), (1, 0, 1), (0, 1, 1), (1, 1, 1))
_HBM = pl.BlockSpec(memory_space=pltpu.HBM)
_SEM = pl.BlockSpec(memory_space=pltpu.SEMAPHORE)


def _exchange_copies(scatter, srcs, lands, send_sems, recv_sems):
    x, y, c = _me()
    me_row = 4 * x + 2 * y + c
    out = []
    for k, (fx, fy, fc) in enumerate(_FLIPS):
        peer = (x ^ fx, y ^ fy, c ^ fc)
        peer_row = 4 * peer[0] + 2 * peer[1] + peer[2]
        for a in range(len(srcs)):
            out.append(pltpu.make_async_remote_copy(
                src_ref=srcs[a].at[peer_row] if scatter else srcs[a], dst_ref=lands[a].at[me_row],
                send_sem=send_sems.at[7 * a + k], recv_sem=recv_sems.at[7 * a + k], device_id=peer, device_id_type=MESH))
    return out


def _own_copies(srcs, lands, own_sems):
    x, y, c = _me()
    return [pltpu.make_async_copy(srcs[a], lands[a].at[4 * x + 2 * y + c], own_sems.at[a]) for a in range(len(srcs))]


def _exchange_start(arrays, scatter, name, after=None):
    n = len(arrays)
    lands = [lax.empty(a.shape if scatter else (N_DEV,) + a.shape, a.dtype) for a in arrays]
    extra = [] if after is None else [after]

    def body(*refs):
        srcs, zones = refs[:n], refs[n:2 * n]
        send_sems, recv_sems, own_sems = refs[2 * n + len(extra):2 * n + len(extra) + 3]
        token = refs[-1]
        for cp in _exchange_copies(scatter, srcs, zones, send_sems, recv_sems):
            cp.start()
        for cp in [] if scatter else _own_copies(srcs, zones, own_sems):
            cp.start()
        token[...] = jnp.zeros_like(token)

    thru = [pltpu.HBM(a.shape, a.dtype) for a in list(arrays) + lands]
    outs = pl.pallas_call(
        body, name=name,
        out_shape=(pltpu.SemaphoreType.DMA((7 * n,)), pltpu.SemaphoreType.DMA((7 * n,)), pltpu.SemaphoreType.DMA((n,)), *thru,
                   jax.ShapeDtypeStruct((8, LANES), f32)),
        in_specs=[_HBM] * (2 * n) + [pl.BlockSpec(memory_space=pl.ANY)] * len(extra),
        out_specs=(_SEM, _SEM, _SEM, *[_HBM] * (2 * n), pl.BlockSpec(memory_space=pltpu.VMEM)),
        input_output_aliases={i: 3 + i for i in range(2 * n)},
        compiler_params=pltpu.CompilerParams(has_side_effects=pltpu.SideEffectType.DATAFLOW_SIDE_EFFECTING),
    )(*[pltpu.with_memory_space_constraint(a, pltpu.HBM) for a in list(arrays) + lands], *extra)
    return dict(n=n, scatter=scatter, sems=outs[:3], srcs=outs[3:3 + n], lands=outs[3 + n:3 + 2 * n], token=outs[-1])


def _exchange_wait(handle, after, name):
    n, scatter = handle["n"], handle["scatter"]

    def body(*refs):
        srcs, zones = refs[:n], refs[n:2 * n]
        send_sems, recv_sems, own_sems = refs[2 * n:2 * n + 3]
        for cp in _exchange_copies(scatter, srcs, zones, send_sems, recv_sems):
            cp.wait_send()
            cp.wait_recv()
        for cp in [] if scatter else _own_copies(srcs, zones, own_sems):
            cp.wait()

    thru = [pltpu.HBM(a.shape, a.dtype) for a in list(handle["srcs"]) + list(handle["lands"])]
    outs = pl.pallas_call(
        body, name=name, out_shape=tuple(thru),
        in_specs=[_HBM] * (2 * n) + [_SEM, _SEM, _SEM, pl.BlockSpec(memory_space=pl.ANY)], out_specs=tuple([_HBM] * (2 * n)),
        input_output_aliases={i: i for i in range(2 * n)},
        compiler_params=pltpu.CompilerParams(has_side_effects=pltpu.SideEffectType.DATAFLOW_SIDE_EFFECTING),
    )(*handle["srcs"], *handle["lands"], *handle["sems"], after)
    return list(outs[:n]), list(outs[n:])


def _cols_from_shards(g):
    return jnp.transpose(g, (1, 0, 2)).reshape(g.shape[1], -1)


def _shards_from_cols(a):
    return jnp.transpose(a.reshape(a.shape[0], N_DEV, -1), (1, 0, 2))


def _local_step(x, positions, ada, g_pre_mix, g_post_mix, b_f, sinks, g_pre_ffn, g_post_ffn, target,
                w_in_t, mix_weights, ffn_weights, on_grads):
    s, d = x.shape
    row = lambda v: v.reshape(1, -1)
    shift_m, scale_m, gate_m, shift_f, scale_f, gate_f = (ada[i:i + 1] for i in range(6))
    w_gate_t, w_qkv_t = w_in_t[F_OFF + N_HEADS:], w_in_t
    w_f_t = jnp.pad(w_in_t[F_OFF:F_OFF + N_HEADS], ((0, LANES - N_HEADS), (0, 0)))
    bf_row = jnp.pad(row(b_f), ((0, 0), (0, LANES - N_HEADS)))
    sink_rows = jnp.broadcast_to(sinks.reshape(N_HEADS, 1).astype(f32), (N_HEADS, LANES))
    inv_freq = 1.0 / (ROPE_THETA ** (jnp.arange(0, HEAD_DIM, 2, dtype=f32) / HEAD_DIM))
    cos, sin_s = _rope_tables(positions.reshape(s, 1), jnp.tile(inv_freq, 4).reshape(1, LANES), "rope_tables")

    h1, qa, ka, va, qb, kb, vb = _prenorm_proj_qkv(x, row(g_pre_mix), scale_m, shift_m, w_qkv_t, cos, sin_s, "prenorm_proj_qkv")
    gl = _matmul(h1, w_gate_t, "nt", bf16, "proj_gate")
    fl, cum_b = _forget_prep(h1, w_f_t, bf_row, "proj_forget_prep")
    o_a, lse_a = _attn_fwd(qa, ka, va, "swa_fwd", sink_rows=sink_rows, window=WINDOW, t=2048)
    o_b, lse_b = _attn_fwd(qb, kb, vb, "fox_fwd", cum_b=cum_b, t=1024)
    everything_before = (gl[:8, :LANES] + o_a[:8, :LANES] + o_b[:8, :LANES]).astype(f32)
    w_branch_a, w_branch_b, w_out = mix_weights(everything_before)
    ba, bb, merged = _branch_merge(o_a, o_b, w_branch_a, w_branch_b, gl, "branch_merge")
    y1, x2, h2 = _out_proj_postnorm_prenorm(merged, w_out, x, row(g_post_mix), gate_m, row(g_pre_ffn), scale_f, shift_f,
                                            "out_proj_norms")

    w_ffn_in_t, w_ffn_out = ffn_weights(h2)
    g_ff, u_ff, act = _ffn_in_swiglu(h2, w_ffn_in_t, "ffn_in_swiglu")
    loss_row, d_out, d_y2, vec_pf = _out_proj_loss_tail(act, w_ffn_out, x2, row(g_post_ffn), gate_f, target, "ffn_out_loss_tail")

    g_w_ffn_out = _matmul(act, d_y2, "tn", bf16, "ffn_out_wgrad")
    dg_ff, du_ff = _ffn_out_dgrad_swiglu(d_y2, w_ffn_out, g_ff, u_ff, "ffn_out_dgrad_swiglu")
    g_w_ffn_in_t = _wgrad_stack([dg_ff, du_ff], h2, "ffn_in_wgrad")
    sent = on_grads(dict(w_ffn_in=g_w_ffn_in_t, w_ffn_out=g_w_ffn_out))
    d_x2, vec_nf, d_y1, vec_pm = _dgrad_prenorm_bwd(
        [(dg_ff, w_ffn_in_t, 0), (du_ff, w_ffn_in_t, 1)], x2, row(g_pre_ffn), scale_f, d_out, "ffn_in_dgrad_norms_bwd",
        after=sent, below=(y1, row(g_post_mix), gate_m))

    g_w_out = _matmul(merged, d_y1, "tn", bf16, "out_proj_wgrad")
    d_ba, d_bb, dgl = _out_dgrad_merge_bwd(d_y1, w_out, ba, bb, gl, "out_proj_dgrad_merge_bwd")
    g_w_branch_a = _matmul(o_a, d_ba, "tn", bf16, "branch_a_wgrad")
    g_w_branch_b = _matmul(o_b, d_bb, "tn", bf16, "branch_b_wgrad")
    sent = on_grads(dict(w_out=g_w_out, w_branch_a=g_w_branch_a, w_branch_b=g_w_branch_b))
    d_oa, delta_a, d_sink = _branch_dgrad_delta(d_ba, w_branch_a, o_a, "branch_a_dgrad_delta", lse=lse_a,
                                                sink_rows=sink_rows, after=sent)
    d_ob, delta_b = _branch_dgrad_delta(d_bb, w_branch_b, o_b, "branch_b_dgrad_delta", after=sent)
    dqa_t, dka, dva = _attn_bwd(qa, ka, va, d_oa, lse_a, delta_a, "swa_bwd", window=WINDOW, t=2048)
    dqb_t, dkb, dvb, dcs, rs = _attn_bwd(qb, kb, vb, d_ob, lse_b, delta_b, "fox_bwd", cum_b=cum_b, t=512)
    dqkv = _qkv_prep_bwd(dqa_t, dka, dva, dqb_t, dkb, dvb, cos, sin_s, "qkv_prep_bwd")
    dfl, vec_bf = _forget_prep_bwd(rs.reshape(N_HEADS, s), dcs, fl, bf_row, "forget_prep_bwd")
    g_w_in_t = jnp.concatenate([_matmul(dqkv, h1, "tn", bf16, "qkv_wgrad"), _matmul(dfl, h1, "tn", bf16, "forget_wgrad")[:N_HEADS],
                                _matmul(dgl, h1, "tn", bf16, "gate_wgrad")], axis=0)
    sent = on_grads(dict(w_in=g_w_in_t))
    grad_x, vec_nm = _dgrad_prenorm_bwd([(dgl, w_gate_t, 0), (dqkv, w_qkv_t, 0), (dfl, w_f_t, 0)], x, row(g_pre_mix),
                                        scale_m, d_x2, "in_proj_dgrad_prenorm_bwd", after=sent)

    d_ada = jnp.concatenate([vec_nm[0], vec_nm[1], vec_pm[0], vec_nf[0], vec_nf[1], vec_pf[0]])
    small = dict(b_ada=d_ada, g_pre_mix=vec_nm[2], g_post_mix=vec_pm[1], g_pre_ffn=vec_nf[2], g_post_ffn=vec_pf[1],
                 b_f=vec_bf[0, :N_HEADS], sinks=d_sink[:, 0], loss=loss_row[0, :1])
    return grad_x, small


_SMALL = (("b_ada", 6144), ("g_pre_mix", 1024), ("g_post_mix", 1024), ("g_pre_ffn", 1024), ("g_post_ffn", 1024),
          ("b_f", 128), ("sinks", 128), ("loss", 128))
_SMALL_ROWS = 88


def _pack_small(vals):
    parts = [jnp.pad(vals[k].reshape(-1).astype(f32), (0, n - vals[k].size)) for k, n in _SMALL]
    flat = jnp.concatenate(parts)
    return jnp.pad(flat, (0, _SMALL_ROWS * LANES - flat.size)).reshape(_SMALL_ROWS, LANES)


def _unpack_small(slab, shapes):
    flat, out, off = slab.reshape(-1), {}, 0
    for k, n in _SMALL:
        size = math.prod(shapes[k])
        out[k] = flat[off:off + size].reshape(shapes[k])
        off += n
    return out


def kernel(x, c, positions, w_ada, b_ada, g_pre_mix, g_post_mix, w_in, b_f, sinks, w_branch_a, w_branch_b, w_out, g_pre_ffn, g_post_ffn, w_ffn_in, w_ffn_out, loss_target, m_w_ada, m_b_ada, m_g_pre_mix, m_g_post_mix, m_w_in, m_b_f, m_sinks, m_w_branch_a, m_w_branch_b, m_w_out, m_g_pre_ffn, m_g_post_ffn, m_w_ffn_in, m_w_ffn_out, v_w_ada, v_b_ada, v_g_pre_mix, v_g_post_mix, v_w_in, v_b_f, v_sinks, v_w_branch_a, v_w_branch_b, v_w_out, v_g_pre_ffn, v_g_post_ffn, v_w_ffn_in, v_w_ffn_out):
    xi, yi, ci = _me()
    me = 4 * xi + 2 * yi + ci
    d = D_MODEL
    ada_w = w_ada.shape[2]

    transposed = ("w_in", "w_ffn_in")
    tr = lambda a: jnp.transpose(a[0])

    b_mine = lax.dynamic_slice(b_ada, (0, me * ada_w), (1, ada_w))
    c_all, ada_all, g_in = _gather_prologue(c, w_ada[0], b_mine, tr(w_in).astype(bf16), "gather_prologue")
    c_all = c_all.reshape(N_DEV, d)
    ada = lax.dynamic_index_in_dim(ada_all, me, axis=1, keepdims=False).reshape(6, d)
    late_mix = [w.astype(bf16) for w in (w_branch_a[0], w_branch_b[0], w_out[0])]
    late_ffn = [w.astype(bf16) for w in (tr(w_ffn_in), w_ffn_out[0])]
    mix_h = _exchange_start(late_mix, False, "gather_mix_start", after=g_in)
    ffn_h = _exchange_start(late_ffn, False, "gather_ffn_start", after=mix_h["token"])

    def rows_from_shards(g):
        return g.reshape(g.shape[0] * g.shape[1], g.shape[2])

    def mix_weights(after):
        _, (g_ba, g_bb, g_out) = _exchange_wait(mix_h, after, "gather_mix_wait")
        return _cols_from_shards(g_ba), _cols_from_shards(g_bb), rows_from_shards(g_out)

    def ffn_weights(after):
        _, (g_fi, g_fo) = _exchange_wait(ffn_h, after, "gather_ffn_wait")
        return rows_from_shards(g_fi), rows_from_shards(g_fo)

    row_sharded = ("w_out", "w_ffn_out") + transposed
    in_flight = []

    def on_grads(group):
        sends = [g.reshape(N_DEV, g.shape[0] // N_DEV, g.shape[1]) if nm in row_sharded else _shards_from_cols(g)
                 for nm, g in group.items()]
        handle = _exchange_start(sends, True, "scatter_start_%d" % len(in_flight))
        in_flight.append((list(group), handle))
        return handle["token"]

    grad_x, small = _local_step(
        x[0], positions[0], ada + ffn_h["token"][0, 0], g_pre_mix[0], g_post_mix[0], b_f[0], sinks[0], g_pre_ffn[0],
        g_post_ffn[0], loss_target[0], rows_from_shards(g_in), mix_weights, ffn_weights, on_grads)

    ws = dict(w_in=(w_in, m_w_in, v_w_in), w_branch_a=(w_branch_a, m_w_branch_a, v_w_branch_a),
              w_branch_b=(w_branch_b, m_w_branch_b, v_w_branch_b), w_out=(w_out, m_w_out, v_w_out),
              w_ffn_in=(w_ffn_in, m_w_ffn_in, v_w_ffn_in), w_ffn_out=(w_ffn_out, m_w_ffn_out, v_w_ffn_out))
    res = {}

    def finish_group(gi, after):
        names, handle = in_flight[gi]
        sends, zones = _exchange_wait(handle, after, "scatter_wait_%d" % gi)
        for nm, zone, sent in zip(names, zones, sends):
            w, m, v = (tr(a) if nm in transposed else a[0] for a in ws[nm])
            out = _adamw(zone, w, m, v, "adamw_" + nm, mine=sent, me=me.reshape(1).astype(jnp.int32))
            after = out[0]
            res[nm] = [jnp.transpose(o) for o in out] if nm in transposed else out
        return after

    small_h = _exchange_start([_pack_small(small)], False, "gather_small_start", after=grad_x)
    done = finish_group(1, finish_group(0, small_h["token"]))
    _, (slab_all,) = _exchange_wait(small_h, done, "gather_small_wait")
    small_w = dict(b_ada=b_ada, g_pre_mix=g_pre_mix, g_post_mix=g_post_mix, g_pre_ffn=g_pre_ffn, g_post_ffn=g_post_ffn,
                   b_f=b_f, sinks=sinks, loss=jnp.zeros((1,), f32))
    small_m = dict(b_ada=m_b_ada, g_pre_mix=m_g_pre_mix, g_post_mix=m_g_post_mix, g_pre_ffn=m_g_pre_ffn,
                   g_post_ffn=m_g_post_ffn, b_f=m_b_f, sinks=m_sinks, loss=jnp.zeros((1,), f32))
    small_v = dict(b_ada=v_b_ada, g_pre_mix=v_g_pre_mix, g_post_mix=v_g_post_mix, g_pre_ffn=v_g_pre_ffn,
                   g_post_ffn=v_g_post_ffn, b_f=v_b_f, sinks=v_sinks, loss=jnp.ones((1,), f32))
    shapes = {k: small_w[k].shape for k, _ in _SMALL}
    s_out = _adamw(slab_all, _pack_small(small_w), _pack_small(small_m), _pack_small(small_v), "adamw_small")
    s_grad, s_delta, s_m, s_v = (_unpack_small(o, shapes) for o in s_out)

    d_ada_all = lax.dynamic_slice(slab_all[:, :6144 // LANES, :].reshape(N_DEV, 6144), (0, me * ada_w), (N_DEV, ada_w))
    ada_parts = _ada_wgrad(c_all, d_ada_all, "ada_wgrad")

    res["w_ada"] = _adamw(ada_parts, w_ada[0], m_w_ada[0], v_w_ada[0], "adamw_w_ada")
    finish_group(2, res["w_ada"][0])

    order = ["w_ada", "b_ada", "g_pre_mix", "g_post_mix", "w_in", "b_f", "sinks", "w_branch_a", "w_branch_b", "w_out",
             "g_pre_ffn", "g_post_ffn", "w_ffn_in", "w_ffn_out"]
    outs = [s_grad["loss"].reshape(()), grad_x[None]]
    for which, small_o in enumerate((s_grad, s_delta, s_m, s_v)):
        for nm in order:
            outs.append(res[nm][which][None] if nm in res else small_o[nm])
    return tuple(outs)
```

```python
import math

import jax
import jax.numpy as jnp
from jax import lax
from jax.experimental import pallas as pl
from jax.experimental.pallas import tpu as pltpu

f32 = jnp.float32
bf16 = jnp.bfloat16

D_MODEL = 1024
HEAD_DIM = 64
N_HEADS = 8
N_PAIRS = 4
QKV_W = 2304
F_OFF = 2304
WINDOW = 128
ROPE_THETA = 10000.0
RMS_EPS = 1e-6
N_DEV = 8
ADAM_LR, ADAM_B1, ADAM_B2, ADAM_EPS, ADAM_WD, ADAM_STEP = 0.001, 0.9, 0.999, 1e-08, 0.01, 10
NEG = -1e30
L_ROW = (HEAD_DIM, 0)
LANES = 128
VMEM_LIMIT = 48 * 1024 * 1024
MESH = pl.DeviceIdType.MESH

_NT = (((1,), (1,)), ((), ()))
_TN = (((0,), (0,)), ((), ()))


def _params(n_grid=0):
    sem = ("arbitrary",) * n_grid if n_grid else None
    return pltpu.CompilerParams(dimension_semantics=sem, vmem_limit_bytes=VMEM_LIMIT)


def _row_tile(s, want):
    t = min(s, want)
    assert s % t == 0, (s, t)
    return t


MATMUL_VMEM_BUDGET = 40 * 1024 * 1024


def _matmul_tiles(m, n, k, a_item, b_item, o_item):
    def tiles(d):
        return [t for t in range(LANES, min(d, 2048) + 1, LANES) if d % t == 0] or [d]

    best = None
    for tm in tiles(m):
        for tn in tiles(n):
            vmem = 2 * (tm * k * a_item + tn * k * b_item + tm * tn * o_item) + tm * tn * 4
            if vmem > MATMUL_VMEM_BUDGET:
                continue
            traffic = m * k * a_item + n * k * b_item * (1 if tn == n else m // tm) + m * n * o_item
            steps = (m // tm) * (n // tn)
            key = (traffic, 0, steps) if steps >= 4 else (traffic, 1, -steps)
            if best is None or key < best[0]:
                best = (key, tm, tn)
    assert best is not None, (m, n, k)
    return best[1], best[2]


def _matmul(a, b, mode, out_dtype, name, after=None):
    if mode == "nn":
        (m, k), n = a.shape, b.shape[1]
    elif mode == "nt":
        (m, k), n = a.shape, b.shape[0]
    else:
        (k, m), n = a.shape, b.shape[1]
    tm, tn = _matmul_tiles(m, n, k, a.dtype.itemsize, b.dtype.itemsize, jnp.dtype(out_dtype).itemsize)
    if mode == "nn":
        a_spec, b_spec, dims = pl.BlockSpec((tm, k), lambda i, j: (i, 0)), pl.BlockSpec((k, tn), lambda i, j: (0, j)), None
    elif mode == "nt":
        a_spec, b_spec, dims = pl.BlockSpec((tm, k), lambda i, j: (i, 0)), pl.BlockSpec((tn, k), lambda i, j: (j, 0)), _NT
    else:
        a_spec, b_spec, dims = pl.BlockSpec((k, tm), lambda i, j: (0, i)), pl.BlockSpec((k, tn), lambda i, j: (0, j)), _TN

    def body(a_ref, b_ref, *rest):
        o_ref = rest[-1]
        av, bv = a_ref[...].astype(bf16), b_ref[...].astype(bf16)
        if dims is None:
            r = jnp.dot(av, bv, preferred_element_type=f32)
        else:
            r = lax.dot_general(av, bv, dims, preferred_element_type=f32)
        o_ref[...] = r.astype(out_dtype)

    extra = [] if after is None else [after]
    return pl.pallas_call(
        body, name=name, grid=(m // tm, n // tn), in_specs=[a_spec, b_spec] + [pl.BlockSpec(memory_space=pl.ANY)] * len(extra),
        out_specs=pl.BlockSpec((tm, tn), lambda i, j: (i, j)),
        out_shape=jax.ShapeDtypeStruct((m, n), out_dtype), compiler_params=_params(2),
    )(a, b, *extra)


def _rstd(v):
    return lax.rsqrt(jnp.mean(v * v, axis=-1, keepdims=True) + RMS_EPS)


def _row_spec(tm, d):
    return pl.BlockSpec((tm, d), lambda i: (i, 0))


def _vec_spec(d, rows=1):
    return pl.BlockSpec((rows, d), lambda i: (0, 0))


def _proj_spec(a, w, tm):
    return [_row_spec(tm, a.shape[1]), pl.BlockSpec(w.shape, lambda i: (0, 0))]


def _out_proj_postnorm_prenorm(a, w, x, g_post, gate, g_pre, scale, shift, name):
    s, d = x.shape
    tm = _row_tile(s, 512)

    def body(a_ref, w_ref, x_ref, gp_ref, gate_ref, g_ref, sc_ref, sh_ref, y_ref, x2_ref, h_ref):
        yv = jnp.dot(a_ref[...], w_ref[...], preferred_element_type=f32)
        y_ref[...] = yv
        x2 = x_ref[...] + gate_ref[...] * (yv * _rstd(yv) * gp_ref[...])
        x2_ref[...] = x2
        h_ref[...] = ((x2 * _rstd(x2) * g_ref[...]) * (1.0 + sc_ref[...]) + sh_ref[...]).astype(bf16)

    return pl.pallas_call(
        body, name=name, grid=(s // tm,), in_specs=_proj_spec(a, w, tm) + [_row_spec(tm, d)] + [_vec_spec(d)] * 5,
        out_specs=[_row_spec(tm, d)] * 3,
        out_shape=[jax.ShapeDtypeStruct((s, d), f32)] * 2 + [jax.ShapeDtypeStruct((s, d), bf16)], compiler_params=_params(1),
    )(a, w, x, g_post, gate, g_pre, scale, shift)


def _rms_bwd(u, v, r):
    return r * u - v * (r * r * r) * jnp.mean(u * v, axis=-1, keepdims=True)


def _out_proj_loss_tail(a, w, x, g, gate, target, name):
    s, d = x.shape
    tm = _row_tile(s, 512)

    def body(a_ref, w_ref, x_ref, g_ref, gate_ref, t_ref, loss_ref, do_ref, dy_ref, vec_ref):
        @pl.when(pl.program_id(0) == 0)
        def _():
            loss_ref[...] = jnp.zeros_like(loss_ref)
            vec_ref[...] = jnp.zeros_like(vec_ref)
        yv = jnp.dot(a_ref[...], w_ref[...], preferred_element_type=f32)
        r = _rstd(yv)
        yn = yv * r
        err = x_ref[...] + gate_ref[...] * (yn * g_ref[...]) - t_ref[...]
        loss_ref[...] += 0.5 * jnp.sum(jnp.mean(err * err, axis=-1, keepdims=True), axis=0, keepdims=True)
        dr = err / d
        do_ref[...] = dr
        dn = dr * gate_ref[...]
        vec_ref[0:1, :] += jnp.sum(dr * (yn * g_ref[...]), axis=0, keepdims=True)
        vec_ref[1:2, :] += jnp.sum(dn * yn, axis=0, keepdims=True)
        dy_ref[...] = _rms_bwd(dn * g_ref[...], yv, r).astype(bf16)

    return pl.pallas_call(
        body, name=name, grid=(s // tm,),
        in_specs=_proj_spec(a, w, tm) + [_row_spec(tm, d)] + [_vec_spec(d)] * 2 + [_row_spec(tm, d)],
        out_specs=[_vec_spec(LANES), _row_spec(tm, d), _row_spec(tm, d), _vec_spec(d, 8)],
        out_shape=[jax.ShapeDtypeStruct((1, LANES), f32), jax.ShapeDtypeStruct((s, d), f32),
                   jax.ShapeDtypeStruct((s, d), bf16), jax.ShapeDtypeStruct((8, d), f32)],
        compiler_params=_params(1),
    )(a, w, x, g, gate, target)


def _dgrad_prenorm_bwd(terms, x, g, scale, dres, name, after=None, below=None):
    s, d = x.shape
    n = len(terms)
    k = sum(a.shape[1] for a, _, _ in terms)
    row_bytes = 2 * (2 * k) + d * (4 + 2 * 4 * 3 + (2 * 4 + 2 * 2 if below else 0))
    tm = next(t for t in (512, 256, 128) if s % t == 0 and 4 * k * d + t * row_bytes <= MATMUL_VMEM_BUDGET)
    extra = [] if after is None else [after]

    def body(*refs):
        a_refs, b_refs = refs[:n], refs[n:2 * n]
        x_ref, g_ref, sc_ref, dr_ref = refs[2 * n:2 * n + 4]
        n_in = 2 * n + 4 + (3 if below else 0) + len(extra)
        dx_ref, vec_ref = refs[n_in], refs[n_in + 1]
        if below:
            y_ref, gp_ref, gate_ref = refs[2 * n + 4:2 * n + 7]
            dy_ref, vec2_ref = refs[n_in + 2], refs[n_in + 3]

        @pl.when(pl.program_id(0) == 0)
        def _():
            vec_ref[...] = jnp.zeros_like(vec_ref)
            if below:
                vec2_ref[...] = jnp.zeros_like(vec2_ref)
        dhv = jnp.dot(a_refs[0][...], b_refs[0][...], preferred_element_type=f32)
        for i in range(1, n):
            dhv = dhv + jnp.dot(a_refs[i][...], b_refs[i][...], preferred_element_type=f32)
        xv = x_ref[...]
        r = _rstd(xv)
        xn = xv * r
        dn = dhv * (1.0 + sc_ref[...])
        vec_ref[0:1, :] += jnp.sum(dhv, axis=0, keepdims=True)
        vec_ref[1:2, :] += jnp.sum(dhv * (xn * g_ref[...]), axis=0, keepdims=True)
        vec_ref[2:3, :] += jnp.sum(dn * xn, axis=0, keepdims=True)
        dx = dr_ref[...] + _rms_bwd(dn * g_ref[...], xv, r)
        dx_ref[...] = dx
        if below:
            yv = y_ref[...]
            ry = _rstd(yv)
            yn = yv * ry
            dny = dx * gate_ref[...]
            vec2_ref[0:1, :] += jnp.sum(dx * (yn * gp_ref[...]), axis=0, keepdims=True)
            vec2_ref[1:2, :] += jnp.sum(dny * yn, axis=0, keepdims=True)
            dy_ref[...] = _rms_bwd(dny * gp_ref[...], yv, ry).astype(bf16)

    in_specs = ([_row_spec(tm, a.shape[1]) for a, _, _ in terms]
                + [pl.BlockSpec((a.shape[1], d), lambda i, r=r: (r, 0)) for a, _, r in terms]
                + [_row_spec(tm, d)] + [_vec_spec(d)] * 2 + [_row_spec(tm, d)])
    out_specs = [_row_spec(tm, d), _vec_spec(d, 8)]
    out_shape = [jax.ShapeDtypeStruct((s, d), f32), jax.ShapeDtypeStruct((8, d), f32)]
    args = [a for a, _, _ in terms] + [b for _, b, _ in terms] + [x, g, scale, dres]
    if below:
        in_specs += [_row_spec(tm, d)] + [_vec_spec(d)] * 2
        out_specs += [_row_spec(tm, d), _vec_spec(d, 8)]
        out_shape += [jax.ShapeDtypeStruct((s, d), bf16), jax.ShapeDtypeStruct((8, d), f32)]
        args += list(below)
    return pl.pallas_call(
        body, name=name, grid=(s // tm,), in_specs=in_specs + [pl.BlockSpec(memory_space=pl.ANY)] * len(extra),
        out_specs=out_specs, out_shape=out_shape, compiler_params=_params(1),
    )(*args, *extra)


def _lane():
    return lax.broadcasted_iota(jnp.int32, (1, LANES), 1)


def _rope_tables(pos_col, inv_freq, name):
    s = pos_col.shape[0]

    def body(p_ref, f_ref, cos_ref, sin_ref):
        ang = p_ref[...].astype(f32) * f_ref[...]
        first_half = (_lane() % HEAD_DIM) < HEAD_DIM // 2
        cos_ref[...] = jnp.cos(ang)
        sn = jnp.sin(ang)
        sin_ref[...] = jnp.where(first_half, -sn, sn)

    return pl.pallas_call(
        body, name=name, out_shape=[jax.ShapeDtypeStruct((s, LANES), f32)] * 2, compiler_params=_params(),
    )(pos_col, inv_freq)


def _swap_halves(v):
    first_half = (_lane() % HEAD_DIM) < HEAD_DIM // 2
    return jnp.where(first_half, pltpu.roll(v, LANES - HEAD_DIM // 2, axis=1), pltpu.roll(v, HEAD_DIM // 2, axis=1))


def _prenorm_proj_qkv(x, g, mod_scale, mod_shift, w_qkv_t, cos, sin_s, name):
    s, d = x.shape
    tm = _row_tile(s, 512)
    scale = 1.0 / math.sqrt(HEAD_DIM)

    def body(x_ref, g_ref, msc_ref, msh_ref, w_ref, c_ref, s_ref, h_ref, qa_ref, ka_ref, va_ref, qb_ref, kb_ref, vb_ref):
        xv = x_ref[...]
        h = ((xv * _rstd(xv) * g_ref[...]) * (1.0 + msc_ref[...]) + msh_ref[...]).astype(bf16)
        h_ref[...] = h
        proj = lax.dot_general(h, w_ref[...], _NT, preferred_element_type=f32)
        cs, sn = c_ref[...], s_ref[...]
        low = _lane() < HEAD_DIM

        def blk(j):
            return proj[:, j * LANES:(j + 1) * LANES]

        def rope(v):
            return v * cs + _swap_halves(v) * sn

        def expand(v):
            other = pltpu.roll(v, HEAD_DIM, axis=1)
            return jnp.where(low, v, other), jnp.where(low, other, v)

        for j in range(N_PAIRS):
            qa_ref[:, j * LANES:(j + 1) * LANES] = (rope(blk(j)) * scale).astype(bf16)
            qb_ref[:, j * LANES:(j + 1) * LANES] = (blk(6 + j) * scale).astype(bf16)
            kb_ref[:, j * LANES:(j + 1) * LANES] = blk(10 + j).astype(bf16)
            vb_ref[:, j * LANES:(j + 1) * LANES] = blk(14 + j).astype(bf16)
        k0, k1 = expand(rope(blk(4)))
        v0, v1 = expand(blk(5))
        for j in range(N_PAIRS):
            ka_ref[:, j * LANES:(j + 1) * LANES] = (k0 if j < 2 else k1).astype(bf16)
            va_ref[:, j * LANES:(j + 1) * LANES] = (v0 if j < 2 else v1).astype(bf16)

    hw = N_PAIRS * LANES
    return pl.pallas_call(
        body, name=name, grid=(s // tm,),
        in_specs=[_row_spec(tm, d)] + [_vec_spec(d)] * 3
        + [pl.BlockSpec((QKV_W, d), lambda i: (0, 0)), _row_spec(tm, LANES), _row_spec(tm, LANES)],
        out_specs=[_row_spec(tm, d)] + [_row_spec(tm, hw)] * 6,
        out_shape=[jax.ShapeDtypeStruct((s, d), bf16)] + [jax.ShapeDtypeStruct((s, hw), bf16)] * 6, compiler_params=_params(1),
    )(x, g, mod_scale, mod_shift, w_qkv_t, cos, sin_s)


def _qkv_prep_bwd(dqa_t, dka, dva, dqb_t, dkb, dvb, cos, sin_s, name):
    s = dka.shape[0]
    tm = _row_tile(s, 512)
    scale = 1.0 / math.sqrt(HEAD_DIM)
    hw = N_PAIRS * LANES
    t_spec = pl.BlockSpec((hw, tm), lambda i: (0, i))

    def body(dqa_ref, dka_ref, dva_ref, dqb_ref, dkb_ref, dvb_ref, c_ref, s_ref, o_ref):
        cs, sn = c_ref[...], s_ref[...]
        low = _lane() < HEAD_DIM

        def blk(ref, j):
            return ref[:, j * LANES:(j + 1) * LANES].astype(f32)

        def blk_t(ref, j):
            return ref[j * LANES:(j + 1) * LANES, :].T

        def unrope(v):
            return v * cs + _swap_halves(v * sn)

        def fold(ref):
            a, b = blk(ref, 0) + blk(ref, 1), blk(ref, 2) + blk(ref, 3)
            kv0 = a + pltpu.roll(a, HEAD_DIM, axis=1)
            kv1 = b + pltpu.roll(b, HEAD_DIM, axis=1)
            return jnp.where(low, kv0, kv1)

        for j in range(N_PAIRS):
            o_ref[:, j * LANES:(j + 1) * LANES] = (unrope(blk_t(dqa_ref, j)) * scale).astype(bf16)
            o_ref[:, (6 + j) * LANES:(7 + j) * LANES] = (blk_t(dqb_ref, j) * scale).astype(bf16)
            o_ref[:, (10 + j) * LANES:(11 + j) * LANES] = blk(dkb_ref, j).astype(bf16)
            o_ref[:, (14 + j) * LANES:(15 + j) * LANES] = blk(dvb_ref, j).astype(bf16)
        o_ref[:, 4 * LANES:5 * LANES] = unrope(fold(dka_ref)).astype(bf16)
        o_ref[:, 5 * LANES:6 * LANES] = fold(dva_ref).astype(bf16)

    return pl.pallas_call(
        body, name=name, grid=(s // tm,),
        in_specs=[t_spec, _row_spec(tm, hw), _row_spec(tm, hw), t_spec, _row_spec(tm, hw), _row_spec(tm, hw)] + [_row_spec(tm, LANES)] * 2,
        out_specs=_row_spec(tm, QKV_W), out_shape=jax.ShapeDtypeStruct((s, QKV_W), bf16), compiler_params=_params(1),
    )(dqa_t, dka, dva, dqb_t, dkb, dvb, cos, sin_s)


def _cumsum_rows(v, reverse=False):
    n = v.shape[0]
    row = lax.broadcasted_iota(jnp.int32, v.shape, 0)
    sh = 1
    while sh < n:
        if reverse:
            v = v + jnp.where(row < n - sh, pltpu.roll(v, n - sh, axis=0), 0.0)
        else:
            v = v + jnp.where(row >= sh, pltpu.roll(v, sh, axis=0), 0.0)
        sh *= 2
    return v


def _log_sigmoid(z):
    return jnp.minimum(z, 0.0) - jnp.log1p(jnp.exp(-jnp.abs(z)))


def _forget_prep(h, w_f_t, bf_row, name):
    s, d = h.shape
    tm = _row_tile(s, 1024)

    def body(h_ref, w_ref, b_ref, f_ref, cb_ref, last_ref):
        @pl.when(pl.program_id(0) == 0)
        def _():
            last_ref[...] = jnp.zeros_like(last_ref)
        fl = lax.dot_general(h_ref[...], w_ref[...], _NT, preferred_element_type=f32)
        f_ref[...] = fl
        cum = _cumsum_rows(_log_sigmoid(fl + b_ref[...])) + last_ref[0:1, :]
        last_ref[0:1, :] = cum[tm - 1:tm, :]
        for hd in range(N_HEADS):
            cb_ref[:, hd * LANES:(hd + 1) * LANES] = jnp.broadcast_to(cum[:, hd:hd + 1], (tm, LANES))

    return pl.pallas_call(
        body, name=name, grid=(s // tm,),
        in_specs=[_row_spec(tm, d), pl.BlockSpec((LANES, d), lambda i: (0, 0)), _vec_spec(LANES)],
        out_specs=[_row_spec(tm, LANES), _row_spec(tm, N_HEADS * LANES)],
        out_shape=[jax.ShapeDtypeStruct((s, LANES), f32), jax.ShapeDtypeStruct((s, N_HEADS * LANES), f32)],
        scratch_shapes=[pltpu.VMEM((8, LANES), f32)], compiler_params=_params(1),
    )(h, w_f_t, bf_row)


def _forget_prep_bwd(rs, dcs, fl, bf_row, name):
    s = fl.shape[0]
    tm = _row_tile(s, 1024)
    n = s // tm

    def body(r_ref, c_ref, f_ref, b_ref, df_ref, db_ref, next_ref):
        @pl.when(pl.program_id(0) == 0)
        def _():
            next_ref[...] = jnp.zeros_like(next_ref)
            db_ref[...] = jnp.zeros_like(db_ref)
        eye = (lax.broadcasted_iota(jnp.int32, (N_HEADS, LANES), 0) == lax.broadcasted_iota(jnp.int32, (N_HEADS, LANES), 1)).astype(f32)
        dcum = lax.dot_general(r_ref[...], eye, _TN, precision=lax.Precision.HIGHEST, preferred_element_type=f32)
        for h in range(N_HEADS):
            dcum = dcum - jnp.where(_lane() == h, jnp.sum(c_ref[:, h * LANES:(h + 1) * LANES], axis=1, keepdims=True), 0.0)
        dlf = _cumsum_rows(dcum, reverse=True) + next_ref[0:1, :]
        next_ref[0:1, :] = dlf[0:1, :]
        z = f_ref[...] + b_ref[...]
        df = jnp.where(_lane() < N_HEADS, dlf * jax.nn.sigmoid(-z), 0.0)
        df_ref[...] = df.astype(bf16)
        db_ref[0:1, :] += jnp.sum(df, axis=0, keepdims=True)

    def rows(width):
        return pl.BlockSpec((tm, width), lambda i: (n - 1 - i, 0))

    return pl.pallas_call(
        body, name=name, grid=(n,),
        in_specs=[pl.BlockSpec((N_HEADS, tm), lambda i: (0, n - 1 - i)), rows(N_HEADS * LANES), rows(LANES), _vec_spec(LANES)],
        out_specs=[rows(LANES), _vec_spec(LANES, 8)],
        out_shape=[jax.ShapeDtypeStruct((s, LANES), bf16), jax.ShapeDtypeStruct((8, LANES), f32)],
        scratch_shapes=[pltpu.VMEM((8, LANES), f32)], compiler_params=_params(1),
    )(rs, dcs, fl, bf_row)


def _tile_mask(n_keys, n_queries, off, window):
    shape = (n_keys, n_queries)
    d = lax.broadcasted_iota(jnp.int32, shape, 1) - lax.broadcasted_iota(jnp.int32, shape, 0) + off
    valid = d >= 0
    return jnp.logical_and(valid, d < window) if window else valid


def _wide(v, t):
    return jnp.concatenate([v] * (t // LANES), axis=1)


def _attn_fwd(q, k, v, name, *, cum_b=None, sink_rows=None, window=None, t=256):
    s = q.shape[0]
    t = _row_tile(s, t)
    fox, has_sink = cum_b is not None, sink_rows is not None
    assert not window or (window % LANES == 0 and LANES + window <= s)

    def body(*refs):
        q_ref, k_ref, v_ref = refs[:3]
        rest = list(refs[3:])
        cb_ref = rest.pop(0) if fox else None
        sink_ref = rest.pop(0) if has_sink else None
        o_ref, lse_ref = rest
        i = pl.program_id(1)
        low = _lane() < HEAD_DIM
        top = lax.broadcasted_iota(jnp.int32, (LANES, 1), 0) < HEAD_DIM
        q2 = q_ref[...]
        zero = jnp.zeros_like(q2)
        qms = (jnp.where(low, q2, zero), jnp.where(low, zero, q2))

        def tile(k0, n_keys, off, carry, masked, queries=slice(0, t)):
            nq = queries.stop - queries.start
            kblk, vblk = k_ref[pl.ds(k0, n_keys), :], v_ref[pl.ds(k0, n_keys), :]
            valid = _tile_mask(n_keys, nq, off, window) if masked else None
            ones = jnp.ones_like(vblk)
            vs = tuple(jnp.where(_lane() == L_ROW[h], ones, vblk) for h in range(2))

            def scores(h):
                return lax.dot_general(kblk, qms[h][queries], _NT, preferred_element_type=f32)

            def softmax(h, sc):
                m = carry[h][0]
                if fox:
                    sc = sc - _wide(cb_ref[pl.ds(k0, n_keys), h * LANES:(h + 1) * LANES], nq)
                if masked:
                    sc = jnp.where(valid, sc, NEG)
                m_new = jnp.maximum(m, jnp.max(sc, axis=0, keepdims=True))
                return m_new, jnp.exp(m - m_new), jnp.exp(sc - m_new).astype(bf16)

            def update(h, m_new, alpha, p):
                return m_new, alpha * carry[h][1] + lax.dot_general(vs[h], p, _TN, preferred_element_type=f32)

            if window:
                return tuple(update(h, *softmax(h, scores(h))) for h in range(2))
            scs = [scores(h) for h in range(2)]
            stats = [softmax(h, scs[h]) for h in range(2)]
            return tuple(update(h, *stats[h]) for h in range(2))

        def start(nq):
            if has_sink:
                row = lax.broadcasted_iota(jnp.int32, (LANES, nq), 0)
                return tuple((_wide(sink_ref[h:h + 1, :], nq), (row == L_ROW[h]).astype(f32)) for h in range(2))
            return tuple((jnp.full((1, nq), NEG, f32), jnp.zeros((LANES, nq), f32)) for h in range(2))

        def finish(carry, queries):
            (m0, a0), (m1, a1) = carry
            l0, l1 = a0[L_ROW[0]:L_ROW[0] + 1, :], a1[L_ROW[1]:L_ROW[1] + 1, :]
            o_t = jnp.where(top, a0 * (1.0 / l0), a1 * (1.0 / l1))
            o_ref[queries, :] = o_t.T.astype(bf16)
            lse_ref[0:1, queries] = m0 + jnp.log(l0)
            lse_ref[1:2, queries] = m1 + jnp.log(l1)

        if window:
            for c in range(t // LANES):
                queries = slice(c * LANES, (c + 1) * LANES)
                q0 = i * t + c * LANES
                k0 = pl.multiple_of(jnp.maximum(q0 - window, 0), LANES)
                finish(tile(k0, LANES + window, q0 - k0, start(LANES), True, queries), queries)
        else:
            carry = lax.fori_loop(0, i, lambda kb, c: tile(pl.multiple_of(kb * t, t), t, 0, c, False), start(t))
            half, k_own = t // 2, pl.multiple_of(i * t, t)
            carry = tile(k_own, half, 0, carry, True)
            finish(tuple((m[:, :half], a[:, :half]) for m, a in carry), slice(0, half))
            carry = tuple((m[:, half:], a[:, half:]) for m, a in carry)
            finish(tile(pl.multiple_of(k_own + half, half), half, 0, carry, True, slice(half, t)), slice(half, t))

    q_spec = pl.BlockSpec((t, LANES), lambda j, i: (i, j))
    kv_spec = pl.BlockSpec((s, LANES), lambda j, i: (0, j))
    in_specs, args = [q_spec, kv_spec, kv_spec], [q, k, v]
    if fox:
        in_specs += [pl.BlockSpec((s, 2 * LANES), lambda j, i: (0, j))]
        args += [cum_b]
    if has_sink:
        in_specs += [pl.BlockSpec((None, 2, LANES), lambda j, i: (j, 0, 0))]
        args += [sink_rows.reshape(N_PAIRS, 2, LANES)]
    return pl.pallas_call(
        body, name=name, grid=(N_PAIRS, s // t), in_specs=in_specs,
        out_specs=[q_spec, pl.BlockSpec((None, 2, t), lambda j, i: (j, 0, i))],
        out_shape=[jax.ShapeDtypeStruct((s, N_PAIRS * LANES), bf16), jax.ShapeDtypeStruct((N_PAIRS, 2, s), f32)],
        compiler_params=_params(2),
    )(*args)


def _branch_dgrad_delta(db, w, o, name, *, lse=None, sink_rows=None, after=None):
    s, hw = o.shape
    tm = _row_tile(s, 1024)
    has_sink = sink_rows is not None
    extra = [] if after is None else [after]

    def body(*refs):
        db_ref, w_ref, o_ref = refs[:3]
        outs = refs[3 + (2 if has_sink else 0) + len(extra):]
        do_ref, dl_ref = outs[:2]
        if has_sink:
            lse_ref, sink_ref = refs[3:5]
            ds_ref = outs[2]

            @pl.when(pl.program_id(0) == 0)
            def _():
                ds_ref[...] = jnp.zeros_like(ds_ref)
        do = lax.dot_general(db_ref[...], w_ref[...], _NT, preferred_element_type=f32).astype(bf16)
        do_ref[...] = do
        for j in range(N_PAIRS):
            cols = slice(j * LANES, (j + 1) * LANES)
            prod_t = (do[:, cols].astype(f32) * o_ref[:, cols].astype(f32)).T
            for h in range(2):
                dl = jnp.sum(prod_t[h * HEAD_DIM:(h + 1) * HEAD_DIM, :], axis=0, keepdims=True)
                dl_ref[j, h:h + 1, :] = dl
                if has_sink:
                    r = 2 * j + h
                    p_sink = jnp.exp(sink_ref[r:r + 1, 0:1] - lse_ref[j, h:h + 1, :])
                    ds_ref[r:r + 1, :] += -jnp.sum(p_sink * dl, axis=1, keepdims=True)

    rows_spec = pl.BlockSpec((N_PAIRS, 2, tm), lambda i: (0, 0, i))
    in_specs = [_row_spec(tm, db.shape[1]), pl.BlockSpec(w.shape, lambda i: (0, 0)), _row_spec(tm, hw)]
    args = [db, w, o]
    out_specs = [_row_spec(tm, hw), rows_spec]
    out_shape = [jax.ShapeDtypeStruct((s, hw), bf16), jax.ShapeDtypeStruct((N_PAIRS, 2, s), f32)]
    if has_sink:
        in_specs += [rows_spec, _vec_spec(LANES, N_HEADS)]
        args += [lse, sink_rows]
        out_specs += [_vec_spec(LANES, N_HEADS)]
        out_shape += [jax.ShapeDtypeStruct((N_HEADS, LANES), f32)]
    return pl.pallas_call(
        body, name=name, grid=(s // tm,), in_specs=in_specs + [pl.BlockSpec(memory_space=pl.ANY)] * len(extra),
        out_specs=out_specs, out_shape=out_shape, compiler_params=_params(1),
    )(*args, *extra)


def _attn_bwd(q, k, v, do, lse, delta, name, *, cum_b=None, window=None, t=256):
    s = q.shape[0]
    t = _row_tile(s, t)
    nblk = s // t
    fox = cum_b is not None
    assert not window or (window % LANES == 0 and LANES + window <= s)

    def body(*refs):
        k_ref, v_ref, q_ref, do_ref, lse_ref, dl_ref = refs[:6]
        rest = list(refs[6:])
        cb_ref = rest.pop(0) if fox else None
        dq_ref, dk_ref, dv_ref = rest[:3]
        dcs_ref, rs_ref = (rest[3], rest[4]) if fox else (None, None)
        dk_acc, dv_acc = rest[-2:]
        b = pl.program_id(1)
        k0 = pl.multiple_of(b * t, t)

        @pl.when(b == 0)
        def _():
            dq_ref[...] = jnp.zeros_like(dq_ref)
            if fox:
                rs_ref[...] = jnp.zeros_like(rs_ref)

        dk_acc[...] = jnp.zeros_like(dk_acc)
        dv_acc[...] = jnp.zeros_like(dv_acc)
        if fox:
            dcs_ref[...] = jnp.zeros_like(dcs_ref)
        low = _lane() < HEAD_DIM
        top = lax.broadcasted_iota(jnp.int32, (LANES, 1), 0) < HEAD_DIM
        kblk, vblk = k_ref[...], v_ref[...]
        k_t = kblk.astype(f32).T.astype(bf16)
        cks = [_wide(cb_ref[pl.ds(k0, t), h * LANES:(h + 1) * LANES], 2 * t) for h in range(2)] if fox else None

        def tile(q0, n_queries, off, masked, keys=slice(0, t)):
            cols = pl.ds(q0, n_queries)
            q2, do2 = q_ref[cols, :], do_ref[cols, :]
            zero = jnp.zeros_like(q2)
            valid = _tile_mask(keys.stop - keys.start, n_queries, off, window) if masked else None
            dq_parts = []
            for h in range(2):
                qm = jnp.where(low, q2, zero) if h == 0 else jnp.where(low, zero, q2)
                dom = jnp.where(low, do2, zero) if h == 0 else jnp.where(low, zero, do2)
                sc = lax.dot_general(kblk[keys], qm, _NT, preferred_element_type=f32)
                if fox:
                    sc = sc - cks[h][keys, :n_queries]
                if masked:
                    sc = jnp.where(valid, sc, NEG)
                p = jnp.exp(sc - lse_ref[h:h + 1, cols])
                dp = lax.dot_general(vblk[keys], dom, _NT, preferred_element_type=f32)
                ds = p * (dp - dl_ref[h:h + 1, cols])
                pb, dsb = p.astype(bf16), ds.astype(bf16)
                dv_acc[keys, :] += jnp.dot(pb, dom, preferred_element_type=f32)
                dk_acc[keys, :] += jnp.dot(dsb, qm, preferred_element_type=f32)
                dq_parts.append(jnp.dot(k_t[:, keys], dsb, preferred_element_type=f32))
                if fox:
                    dcs_ref[keys, h * LANES:(h + 1) * LANES] += sum(ds[:, g * LANES:(g + 1) * LANES]
                                                                    for g in range(n_queries // LANES))
                    rs_ref[h:h + 1, cols] += jnp.sum(ds, axis=0, keepdims=True)
            dq_ref[:, cols] += jnp.where(top, dq_parts[0], dq_parts[1])

        later = nblk - 1 - b

        def later_pair(p, carry):
            tile(pl.multiple_of((b + 1 + later % 2 + 2 * p) * t, t), 2 * t, 0, False)
            return carry

        if window:
            for c in range(t // LANES):
                first = b * t + c * LANES
                q0 = pl.multiple_of(jnp.minimum(first, s - (LANES + window)), LANES)
                tile(q0, LANES + window, q0 - first, True, slice(c * LANES, (c + 1) * LANES))
        else:
            half = t // 2
            tile(k0, half, 0, True, slice(0, half))
            tile(pl.multiple_of(k0 + half, half), half, half, True)
            @pl.when(later % 2 == 1)
            def _():
                tile(pl.multiple_of(k0 + t, t), t, 0, False)
            lax.fori_loop(0, later // 2, later_pair, 0)
        dk_ref[...] = dk_acc[...].astype(bf16)
        dv_ref[...] = dv_acc[...].astype(bf16)

    kv_spec = pl.BlockSpec((t, LANES), lambda j, b: (b, j))
    seq_spec = pl.BlockSpec((s, LANES), lambda j, b: (0, j))
    rows_spec = pl.BlockSpec((None, 2, s), lambda j, b: (j, 0, 0))
    hw = N_PAIRS * LANES
    in_specs, args = [kv_spec, kv_spec, seq_spec, seq_spec, rows_spec, rows_spec], [k, v, q, do, lse, delta]
    out_specs = [pl.BlockSpec((LANES, s), lambda j, b: (j, 0)), kv_spec, kv_spec]
    out_shape = [jax.ShapeDtypeStruct((hw, s), f32), jax.ShapeDtypeStruct((s, hw), bf16), jax.ShapeDtypeStruct((s, hw), bf16)]
    if fox:
        in_specs += [pl.BlockSpec((s, 2 * LANES), lambda j, b: (0, j))]
        args += [cum_b]
        out_specs += [pl.BlockSpec((t, 2 * LANES), lambda j, b: (b, j)), rows_spec]
        out_shape += [jax.ShapeDtypeStruct((s, N_HEADS * LANES), f32), jax.ShapeDtypeStruct((N_PAIRS, 2, s), f32)]
    return pl.pallas_call(
        body, name=name, grid=(N_PAIRS, nblk), in_specs=in_specs, out_specs=out_specs, out_shape=out_shape,
        scratch_shapes=[pltpu.VMEM((t, LANES), f32)] * 2, compiler_params=_params(2),
    )(*args)


def _branch_merge(o_a, o_b, w_a, w_b, gl, name):
    s, k = o_a.shape
    d = w_a.shape[1]
    tm = _row_tile(s, 1024)

    def body(oa_ref, ob_ref, wa_ref, wb_ref, g_ref, ba_ref, bb_ref, m_ref):
        ba = jnp.dot(oa_ref[...], wa_ref[...], preferred_element_type=f32)
        bb = jnp.dot(ob_ref[...], wb_ref[...], preferred_element_type=f32)
        g0, g1 = jax.nn.sigmoid(g_ref[:, :d].astype(f32)), jax.nn.sigmoid(g_ref[:, d:].astype(f32))
        ba_ref[...] = ba.astype(bf16)
        bb_ref[...] = bb.astype(bf16)
        m_ref[...] = (g0 * ba + g1 * bb).astype(bf16)

    whole = pl.BlockSpec((k, d), lambda i: (0, 0))
    return pl.pallas_call(
        body, name=name, grid=(s // tm,),
        in_specs=[_row_spec(tm, k), _row_spec(tm, k), whole, whole, _row_spec(tm, 2 * d)],
        out_specs=[_row_spec(tm, d)] * 3, out_shape=[jax.ShapeDtypeStruct((s, d), bf16)] * 3, compiler_params=_params(1),
    )(o_a, o_b, w_a, w_b, gl)


def _out_dgrad_merge_bwd(dy, w_out, ba, bb, gl, name):
    s, d = ba.shape
    tm = _row_tile(s, 512)

    def body(dy_ref, w_ref, a_ref, b_ref, g_ref, da_ref, db_ref, dg_ref):
        dmv = lax.dot_general(dy_ref[...], w_ref[...], _NT, preferred_element_type=f32)
        g0, g1 = jax.nn.sigmoid(g_ref[:, :d].astype(f32)), jax.nn.sigmoid(g_ref[:, d:].astype(f32))
        da_ref[...] = (dmv * g0).astype(bf16)
        db_ref[...] = (dmv * g1).astype(bf16)
        dg_ref[:, :d] = (dmv * a_ref[...].astype(f32) * (g0 * (1.0 - g0))).astype(bf16)
        dg_ref[:, d:] = (dmv * b_ref[...].astype(f32) * (g1 * (1.0 - g1))).astype(bf16)

    return pl.pallas_call(
        body, name=name, grid=(s // tm,),
        in_specs=[_row_spec(tm, dy.shape[1]), pl.BlockSpec(w_out.shape, lambda i: (0, 0))] + [_row_spec(tm, d)] * 2
        + [_row_spec(tm, 2 * d)],
        out_specs=[_row_spec(tm, d)] * 2 + [_row_spec(tm, 2 * d)],
        out_shape=[jax.ShapeDtypeStruct((s, d), bf16)] * 2 + [jax.ShapeDtypeStruct((s, 2 * d), bf16)],
        compiler_params=_params(1),
    )(dy, w_out, ba, bb, gl)


GLU_TILE = 256


def _ffn_in_swiglu(h, w_t, name):
    s, d = h.shape
    f = w_t.shape[0] // 2
    tm = _row_tile(s, 4096)
    tg = GLU_TILE
    nb = f // tg

    def body(h_ref, wg_ref, wu_ref, g_ref, u_ref, act_ref):
        hv = h_ref[...]
        g = lax.dot_general(hv, wg_ref[...], _NT, preferred_element_type=f32)
        u = lax.dot_general(hv, wu_ref[...], _NT, preferred_element_type=f32)
        g_ref[...] = g.astype(bf16)
        u_ref[...] = u.astype(bf16)
        act_ref[...] = (g * jax.nn.sigmoid(g) * u).astype(bf16)

    col = pl.BlockSpec((tm, tg), lambda i, j: (i, j))
    return pl.pallas_call(
        body, name=name, grid=(s // tm, nb),
        in_specs=[pl.BlockSpec((tm, d), lambda i, j: (i, 0)), pl.BlockSpec((tg, d), lambda i, j: (j, 0)),
                  pl.BlockSpec((tg, d), lambda i, j: (j + nb, 0))],
        out_specs=[col] * 3, out_shape=[jax.ShapeDtypeStruct((s, f), bf16)] * 3, compiler_params=_params(2),
    )(h, w_t, w_t)


def _ffn_out_dgrad_swiglu(dy, w_out, g, u, name):
    s, d = dy.shape
    f = g.shape[1]
    tm = _row_tile(s, 4096)
    tg = GLU_TILE

    def body(dy_ref, w_ref, g_ref, u_ref, dg_ref, du_ref):
        dv = lax.dot_general(dy_ref[...], w_ref[...], _NT, preferred_element_type=f32)
        gv, uv = g_ref[...].astype(f32), u_ref[...].astype(f32)
        sg = jax.nn.sigmoid(gv)
        dg_ref[...] = (dv * uv * (sg * (1.0 + gv * (1.0 - sg)))).astype(bf16)
        du_ref[...] = (dv * (gv * sg)).astype(bf16)

    col = pl.BlockSpec((tm, tg), lambda i, j: (i, j))
    return pl.pallas_call(
        body, name=name, grid=(s // tm, f // tg),
        in_specs=[pl.BlockSpec((tm, d), lambda i, j: (i, 0)), pl.BlockSpec((tg, d), lambda i, j: (j, 0)), col, col],
        out_specs=[col] * 2, out_shape=[jax.ShapeDtypeStruct((s, f), bf16)] * 2, compiler_params=_params(2),
    )(dy, w_out, g, u)


def _wgrad_stack(parts, h, name):
    s, m = parts[0].shape
    d = h.shape[1]
    tm = 256
    nb = m // tm
    n = len(parts)

    def body(*refs):
        i = pl.program_id(0)
        for p in range(n):
            @pl.when(i // nb == p)
            def _(p=p):
                refs[n + 1][...] = lax.dot_general(refs[p][...], refs[n][...], _TN, preferred_element_type=f32).astype(bf16)

    a_specs = [pl.BlockSpec((s, tm), lambda i, p=p: (0, jnp.clip(i - p * nb, 0, nb - 1))) for p in range(n)]
    return pl.pallas_call(
        body, name=name, grid=(n * nb,), in_specs=a_specs + [pl.BlockSpec((s, d), lambda i: (0, 0))],
        out_specs=pl.BlockSpec((tm, d), lambda i: (i, 0)),
        out_shape=jax.ShapeDtypeStruct((n * m, d), bf16), compiler_params=_params(1),
    )(*parts, h)


def _ada_wgrad(c_all, d_all, name):
    n, d = c_all.shape
    w = d_all.shape[1]

    def body(c_ref, d_ref, o_ref):
        eye = (lax.broadcasted_iota(jnp.int32, (n, n), 0) == lax.broadcasted_iota(jnp.int32, (n, n), 1)).astype(f32)
        ct = lax.dot_general(c_ref[...], eye, _TN, precision=lax.Precision.HIGHEST, preferred_element_type=f32)
        g = ct[:, 0:1] * d_ref[0:1, :]
        for bi in range(1, n):
            g = g + ct[:, bi:bi + 1] * d_ref[bi:bi + 1, :]
        o_ref[0] = g

    return pl.pallas_call(
        body, name=name, out_shape=jax.ShapeDtypeStruct((1, d, w), f32), compiler_params=_params(),
    )(c_all, d_all)


def _adamw(parts, w, m, v, name, mine=None, me=None):
    r, c = w.shape
    n_parts = parts.shape[0]
    row_tiles = [t for t in range(min(r, 256), 0, -1) if r % t == 0 and (t % 16 == 0 or t == r)]
    if row_tiles:
        tr, tc = row_tiles[0], c
    else:
        tr, tc = r, next(t for t in (256, LANES) if c % t == 0)

    def body(*refs):
        w_ref, m_ref, v_ref, g_ref, d_ref, nm_ref, nv_ref = refs[-7:]
        if mine is None:
            p_ref, = refs[:-7]
        else:
            me_ref, p_ref, own_ref = refs[:-7]

        def part(i):
            if mine is None:
                return p_ref[i].astype(f32)
            return jnp.where(me_ref[0] == i, own_ref[...], p_ref[i]).astype(f32)

        g = part(0)
        for i in range(1, n_parts):
            g = g + part(i)
        mm = ADAM_B1 * m_ref[...] + (1.0 - ADAM_B1) * g
        vv = ADAM_B2 * v_ref[...] + (1.0 - ADAM_B2) * (g * g)
        m_hat = mm / (1.0 - ADAM_B1 ** ADAM_STEP)
        v_hat = vv / (1.0 - ADAM_B2 ** ADAM_STEP)
        g_ref[...] = g
        d_ref[...] = -ADAM_LR * (m_hat / (jnp.sqrt(v_hat) + ADAM_EPS) + ADAM_WD * w_ref[...])
        nm_ref[...] = mm
        nv_ref[...] = vv

    out_shape = [jax.ShapeDtypeStruct((r, c), f32)] * 4
    if mine is None:
        spec = pl.BlockSpec((tr, tc), lambda i, j: (i, j))
        return pl.pallas_call(
            body, name=name, grid=(r // tr, c // tc),
            in_specs=[pl.BlockSpec((n_parts, tr, tc), lambda i, j: (0, i, j))] + [spec] * 3,
            out_specs=[spec] * 4, out_shape=out_shape, compiler_params=_params(2),
        )(parts, w, m, v)
    spec = pl.BlockSpec((tr, tc), lambda i, j, me_ref: (i, j))
    return pl.pallas_call(
        body, name=name, out_shape=out_shape, compiler_params=_params(2),
        grid_spec=pltpu.PrefetchScalarGridSpec(
            num_scalar_prefetch=1, grid=(r // tr, c // tc),
            in_specs=[pl.BlockSpec((n_parts, tr, tc), lambda i, j, me_ref: (0, i, j)),
                      pl.BlockSpec((None, tr, tc), lambda i, j, me_ref: (me_ref[0], i, j))] + [spec] * 3,
            out_specs=[spec] * 4),
    )(me, parts, mine, w, m, v)


def _me():
    return lax.axis_index("x"), lax.axis_index("y"), lax.axis_index("c")


def _gather_prologue(c, w_ada, b_mine, w_in_t, name):
    n_dev, d = N_DEV, c.shape[1]
    ada_w = w_ada.shape[1]

    def body(c_ref, w_ref, b_ref, win_ref, call_ref, ada_ref, gin_ref, cols_ref, send_sems, recv_sems, local_sems):
        x, y, cc = _me()
        me, sibling = (x, y, cc), (x, y, 1 - cc)
        chips = [(1 - x, y), (x, 1 - y), (1 - x, 1 - y)]
        outs = (call_ref, ada_ref, gin_ref)

        def rows(a, dev):
            return outs[a].at[4 * dev[0] + 2 * dev[1] + dev[2]]

        def copy(a, k, block, to, src=None):
            return pltpu.make_async_remote_copy(
                src_ref=rows(a, block) if src is None else src, dst_ref=rows(a, block),
                send_sem=send_sems.at[a, k], recv_sem=recv_sems.at[a, k], device_id=to, device_id_type=MESH)

        def begin(a, src):
            own = pltpu.make_async_copy(src, rows(a, me), local_sems.at[a])
            sends = [copy(a, 0, me, sibling, src=src)] + [copy(a, 1 + j, me, (*chip, cc), src=src) for j, chip in enumerate(chips)]
            for cp in [own] + sends:
                cp.start()
            return own, sends

        def finish(a, own, sends):
            passed = []
            for j, chip in enumerate(chips):
                copy(a, 1 + j, (*chip, cc), me).wait_recv()
                passed.append(copy(a, 4 + j, (*chip, cc), sibling))
                passed[-1].start()
            copy(a, 0, sibling, me).wait_recv()
            for j, chip in enumerate(chips):
                copy(a, 4 + j, (*chip, 1 - cc), me).wait_recv()
            for cp in sends + passed:
                cp.wait_send()
            own.wait()

        finish(0, *begin(0, c_ref))
        cols_ref[...] = (jnp.dot(call_ref[:, 0, :].astype(bf16), w_ref[...].astype(bf16), preferred_element_type=f32)
                         + b_ref[...])
        finish(1, *begin(1, cols_ref))
        finish(2, *begin(2, win_ref))

    vmem, hbm = pl.BlockSpec(memory_space=pltpu.VMEM), pl.BlockSpec(memory_space=pl.ANY)
    return pl.pallas_call(
        body, name=name, in_specs=[vmem, vmem, vmem, hbm], out_specs=[vmem, vmem, hbm],
        out_shape=[jax.ShapeDtypeStruct((n_dev, 1, d), f32), jax.ShapeDtypeStruct((n_dev, n_dev, ada_w), f32),
                   jax.ShapeDtypeStruct((n_dev,) + w_in_t.shape, w_in_t.dtype)],
        scratch_shapes=[pltpu.VMEM((n_dev, ada_w), f32), pltpu.SemaphoreType.DMA((3, 7)), pltpu.SemaphoreType.DMA((3, 7)),
                        pltpu.SemaphoreType.DMA((3,))],
        compiler_params=pltpu.CompilerParams(vmem_limit_bytes=VMEM_LIMIT),
    )(c, w_ada, b_mine, w_in_t)


_FLIPS = ((0, 0, 1), (1, 0, 0), (0, 1, 0), (1, 1, 0), (1, 0, 1), (0, 1, 1), (1, 1, 1))
_HBM = pl.BlockSpec(memory_space=pltpu.HBM)
_SEM = pl.BlockSpec(memory_space=pltpu.SEMAPHORE)


def _exchange_copies(scatter, srcs, lands, send_sems, recv_sems):
    x, y, c = _me()
    me_row = 4 * x + 2 * y + c
    out = []
    for k, (fx, fy, fc) in enumerate(_FLIPS):
        peer = (x ^ fx, y ^ fy, c ^ fc)
        peer_row = 4 * peer[0] + 2 * peer[1] + peer[2]
        for a in range(len(srcs)):
            out.append(pltpu.make_async_remote_copy(
                src_ref=srcs[a].at[peer_row] if scatter else srcs[a], dst_ref=lands[a].at[me_row],
                send_sem=send_sems.at[7 * a + k], recv_sem=recv_sems.at[7 * a + k], device_id=peer, device_id_type=MESH))
    return out


def _own_copies(srcs, lands, own_sems):
    x, y, c = _me()
    return [pltpu.make_async_copy(srcs[a], lands[a].at[4 * x + 2 * y + c], own_sems.at[a]) for a in range(len(srcs))]


def _exchange_start(arrays, scatter, name, after=None):
    n = len(arrays)
    lands = [lax.empty(a.shape if scatter else (N_DEV,) + a.shape, a.dtype) for a in arrays]
    extra = [] if after is None else [after]

    def body(*refs):
        srcs, zones = refs[:n], refs[n:2 * n]
        send_sems, recv_sems, own_sems = refs[2 * n + len(extra):2 * n + len(extra) + 3]
        token = refs[-1]
        for cp in _exchange_copies(scatter, srcs, zones, send_sems, recv_sems):
            cp.start()
        for cp in [] if scatter else _own_copies(srcs, zones, own_sems):
            cp.start()
        token[...] = jnp.zeros_like(token)

    thru = [pltpu.HBM(a.shape, a.dtype) for a in list(arrays) + lands]
    outs = pl.pallas_call(
        body, name=name,
        out_shape=(pltpu.SemaphoreType.DMA((7 * n,)), pltpu.SemaphoreType.DMA((7 * n,)), pltpu.SemaphoreType.DMA((n,)), *thru,
                   jax.ShapeDtypeStruct((8, LANES), f32)),
        in_specs=[_HBM] * (2 * n) + [pl.BlockSpec(memory_space=pl.ANY)] * len(extra),
        out_specs=(_SEM, _SEM, _SEM, *[_HBM] * (2 * n), pl.BlockSpec(memory_space=pltpu.VMEM)),
        input_output_aliases={i: 3 + i for i in range(2 * n)},
        compiler_params=pltpu.CompilerParams(has_side_effects=pltpu.SideEffectType.DATAFLOW_SIDE_EFFECTING),
    )(*[pltpu.with_memory_space_constraint(a, pltpu.HBM) for a in list(arrays) + lands], *extra)
    return dict(n=n, scatter=scatter, sems=outs[:3], srcs=outs[3:3 + n], lands=outs[3 + n:3 + 2 * n], token=outs[-1])


def _exchange_wait(handle, after, name):
    n, scatter = handle["n"], handle["scatter"]

    def body(*refs):
        srcs, zones = refs[:n], refs[n:2 * n]
        send_sems, recv_sems, own_sems = refs[2 * n:2 * n + 3]
        for cp in _exchange_copies(scatter, srcs, zones, send_sems, recv_sems):
            cp.wait_send()
            cp.wait_recv()
        for cp in [] if scatter else _own_copies(srcs, zones, own_sems):
            cp.wait()

    thru = [pltpu.HBM(a.shape, a.dtype) for a in list(handle["srcs"]) + list(handle["lands"])]
    outs = pl.pallas_call(
        body, name=name, out_shape=tuple(thru),
        in_specs=[_HBM] * (2 * n) + [_SEM, _SEM, _SEM, pl.BlockSpec(memory_space=pl.ANY)], out_specs=tuple([_HBM] * (2 * n)),
        input_output_aliases={i: i for i in range(2 * n)},
        compiler_params=pltpu.CompilerParams(has_side_effects=pltpu.SideEffectType.DATAFLOW_SIDE_EFFECTING),
    )(*handle["srcs"], *handle["lands"], *handle["sems"], after)
    return list(outs[:n]), list(outs[n:])


def _cols_from_shards(g):
    return jnp.transpose(g, (1, 0, 2)).reshape(g.shape[1], -1)


def _shards_from_cols(a):
    return jnp.transpose(a.reshape(a.shape[0], N_DEV, -1), (1, 0, 2))


def _local_step(x, positions, ada, g_pre_mix, g_post_mix, b_f, sinks, g_pre_ffn, g_post_ffn, target,
                w_in_t, mix_weights, ffn_weights, on_grads):
    s, d = x.shape
    row = lambda v: v.reshape(1, -1)
    shift_m, scale_m, gate_m, shift_f, scale_f, gate_f = (ada[i:i + 1] for i in range(6))
    w_gate_t, w_qkv_t = w_in_t[F_OFF + N_HEADS:], w_in_t
    w_f_t = jnp.pad(w_in_t[F_OFF:F_OFF + N_HEADS], ((0, LANES - N_HEADS), (0, 0)))
    bf_row = jnp.pad(row(b_f), ((0, 0), (0, LANES - N_HEADS)))
    sink_rows = jnp.broadcast_to(sinks.reshape(N_HEADS, 1).astype(f32), (N_HEADS, LANES))
    inv_freq = 1.0 / (ROPE_THETA ** (jnp.arange(0, HEAD_DIM, 2, dtype=f32) / HEAD_DIM))
    cos, sin_s = _rope_tables(positions.reshape(s, 1), jnp.tile(inv_freq, 4).reshape(1, LANES), "rope_tables")

    h1, qa, ka, va, qb, kb, vb = _prenorm_proj_qkv(x, row(g_pre_mix), scale_m, shift_m, w_qkv_t, cos, sin_s, "prenorm_proj_qkv")
    gl = _matmul(h1, w_gate_t, "nt", bf16, "proj_gate")
    fl, cum_b = _forget_prep(h1, w_f_t, bf_row, "proj_forget_prep")
    o_a, lse_a = _attn_fwd(qa, ka, va, "swa_fwd", sink_rows=sink_rows, window=WINDOW, t=2048)
    o_b, lse_b = _attn_fwd(qb, kb, vb, "fox_fwd", cum_b=cum_b, t=1024)
    everything_before = (gl[:8, :LANES] + o_a[:8, :LANES] + o_b[:8, :LANES]).astype(f32)
    w_branch_a, w_branch_b, w_out = mix_weights(everything_before)
    ba, bb, merged = _branch_merge(o_a, o_b, w_branch_a, w_branch_b, gl, "branch_merge")
    y1, x2, h2 = _out_proj_postnorm_prenorm(merged, w_out, x, row(g_post_mix), gate_m, row(g_pre_ffn), scale_f, shift_f,
                                            "out_proj_norms")

    w_ffn_in_t, w_ffn_out = ffn_weights(h2)
    g_ff, u_ff, act = _ffn_in_swiglu(h2, w_ffn_in_t, "ffn_in_swiglu")
    loss_row, d_out, d_y2, vec_pf = _out_proj_loss_tail(act, w_ffn_out, x2, row(g_post_ffn), gate_f, target, "ffn_out_loss_tail")

    g_w_ffn_out = _matmul(act, d_y2, "tn", bf16, "ffn_out_wgrad")
    dg_ff, du_ff = _ffn_out_dgrad_swiglu(d_y2, w_ffn_out, g_ff, u_ff, "ffn_out_dgrad_swiglu")
    g_w_ffn_in_t = _wgrad_stack([dg_ff, du_ff], h2, "ffn_in_wgrad")
    sent = on_grads(dict(w_ffn_in=g_w_ffn_in_t, w_ffn_out=g_w_ffn_out))
    d_x2, vec_nf, d_y1, vec_pm = _dgrad_prenorm_bwd(
        [(dg_ff, w_ffn_in_t, 0), (du_ff, w_ffn_in_t, 1)], x2, row(g_pre_ffn), scale_f, d_out, "ffn_in_dgrad_norms_bwd",
        after=sent, below=(y1, row(g_post_mix), gate_m))

    g_w_out = _matmul(merged, d_y1, "tn", bf16, "out_proj_wgrad")
    d_ba, d_bb, dgl = _out_dgrad_merge_bwd(d_y1, w_out, ba, bb, gl, "out_proj_dgrad_merge_bwd")
    g_w_branch_a = _matmul(o_a, d_ba, "tn", bf16, "branch_a_wgrad")
    g_w_branch_b = _matmul(o_b, d_bb, "tn", bf16, "branch_b_wgrad")
    sent = on_grads(dict(w_out=g_w_out, w_branch_a=g_w_branch_a, w_branch_b=g_w_branch_b))
    d_oa, delta_a, d_sink = _branch_dgrad_delta(d_ba, w_branch_a, o_a, "branch_a_dgrad_delta", lse=lse_a,
                                                sink_rows=sink_rows, after=sent)
    d_ob, delta_b = _branch_dgrad_delta(d_bb, w_branch_b, o_b, "branch_b_dgrad_delta", after=sent)
    dqa_t, dka, dva = _attn_bwd(qa, ka, va, d_oa, lse_a, delta_a, "swa_bwd", window=WINDOW, t=2048)
    dqb_t, dkb, dvb, dcs, rs = _attn_bwd(qb, kb, vb, d_ob, lse_b, delta_b, "fox_bwd", cum_b=cum_b, t=512)
    dqkv = _qkv_prep_bwd(dqa_t, dka, dva, dqb_t, dkb, dvb, cos, sin_s, "qkv_prep_bwd")
    dfl, vec_bf = _forget_prep_bwd(rs.reshape(N_HEADS, s), dcs, fl, bf_row, "forget_prep_bwd")
    g_w_in_t = jnp.concatenate([_matmul(dqkv, h1, "tn", bf16, "qkv_wgrad"), _matmul(dfl, h1, "tn", bf16, "forget_wgrad")[:N_HEADS],
                                _matmul(dgl, h1, "tn", bf16, "gate_wgrad")], axis=0)
    sent = on_grads(dict(w_in=g_w_in_t))
    grad_x, vec_nm = _dgrad_prenorm_bwd([(dgl, w_gate_t, 0), (dqkv, w_qkv_t, 0), (dfl, w_f_t, 0)], x, row(g_pre_mix),
                                        scale_m, d_x2, "in_proj_dgrad_prenorm_bwd", after=sent)

    d_ada = jnp.concatenate([vec_nm[0], vec_nm[1], vec_pm[0], vec_nf[0], vec_nf[1], vec_pf[0]])
    small = dict(b_ada=d_ada, g_pre_mix=vec_nm[2], g_post_mix=vec_pm[1], g_pre_ffn=vec_nf[2], g_post_ffn=vec_pf[1],
                 b_f=vec_bf[0, :N_HEADS], sinks=d_sink[:, 0], loss=loss_row[0, :1])
    return grad_x, small


_SMALL = (("b_ada", 6144), ("g_pre_mix", 1024), ("g_post_mix", 1024), ("g_pre_ffn", 1024), ("g_post_ffn", 1024),
          ("b_f", 128), ("sinks", 128), ("loss", 128))
_SMALL_ROWS = 88


def _pack_small(vals):
    parts = [jnp.pad(vals[k].reshape(-1).astype(f32), (0, n - vals[k].size)) for k, n in _SMALL]
    flat = jnp.concatenate(parts)
    return jnp.pad(flat, (0, _SMALL_ROWS * LANES - flat.size)).reshape(_SMALL_ROWS, LANES)


def _unpack_small(slab, shapes):
    flat, out, off = slab.reshape(-1), {}, 0
    for k, n in _SMALL:
        size = math.prod(shapes[k])
        out[k] = flat[off:off + size].reshape(shapes[k])
        off += n
    return out


def kernel(x, c, positions, w_ada, b_ada, g_pre_mix, g_post_mix, w_in, b_f, sinks, w_branch_a, w_branch_b, w_out, g_pre_ffn, g_post_ffn, w_ffn_in, w_ffn_out, loss_target, m_w_ada, m_b_ada, m_g_pre_mix, m_g_post_mix, m_w_in, m_b_f, m_sinks, m_w_branch_a, m_w_branch_b, m_w_out, m_g_pre_ffn, m_g_post_ffn, m_w_ffn_in, m_w_ffn_out, v_w_ada, v_b_ada, v_g_pre_mix, v_g_post_mix, v_w_in, v_b_f, v_sinks, v_w_branch_a, v_w_branch_b, v_w_out, v_g_pre_ffn, v_g_post_ffn, v_w_ffn_in, v_w_ffn_out):
    xi, yi, ci = _me()
    me = 4 * xi + 2 * yi + ci
    d = D_MODEL
    ada_w = w_ada.shape[2]

    transposed = ("w_in", "w_ffn_in")
    tr = lambda a: jnp.transpose(a[0])

    b_mine = lax.dynamic_slice(b_ada, (0, me * ada_w), (1, ada_w))
    c_all, ada_all, g_in = _gather_prologue(c, w_ada[0], b_mine, tr(w_in).astype(bf16), "gather_prologue")
    c_all = c_all.reshape(N_DEV, d)
    ada = lax.dynamic_index_in_dim(ada_all, me, axis=1, keepdims=False).reshape(6, d)
    late_mix = [w.astype(bf16) for w in (w_branch_a[0], w_branch_b[0], w_out[0])]
    late_ffn = [w.astype(bf16) for w in (tr(w_ffn_in), w_ffn_out[0])]
    mix_h = _exchange_start(late_mix, False, "gather_mix_start", after=g_in)
    ffn_h = _exchange_start(late_ffn, False, "gather_ffn_start", after=mix_h["token"])

    def rows_from_shards(g):
        return g.reshape(g.shape[0] * g.shape[1], g.shape[2])

    def mix_weights(after):
        _, (g_ba, g_bb, g_out) = _exchange_wait(mix_h, after, "gather_mix_wait")
        return _cols_from_shards(g_ba), _cols_from_shards(g_bb), rows_from_shards(g_out)

    def ffn_weights(after):
        _, (g_fi, g_fo) = _exchange_wait(ffn_h, after, "gather_ffn_wait")
        return rows_from_shards(g_fi), rows_from_shards(g_fo)

    row_sharded = ("w_out", "w_ffn_out") + transposed
    in_flight = []

    def on_grads(group):
        sends = [g.reshape(N_DEV, g.shape[0] // N_DEV, g.shape[1]) if nm in row_sharded else _shards_from_cols(g)
                 for nm, g in group.items()]
        handle = _exchange_start(sends, True, "scatter_start_%d" % len(in_flight))
        in_flight.append((list(group), handle))
        return handle["token"]

    grad_x, small = _local_step(
        x[0], positions[0], ada + ffn_h["token"][0, 0], g_pre_mix[0], g_post_mix[0], b_f[0], sinks[0], g_pre_ffn[0],
        g_post_ffn[0], loss_target[0], rows_from_shards(g_in), mix_weights, ffn_weights, on_grads)

    ws = dict(w_in=(w_in, m_w_in, v_w_in), w_branch_a=(w_branch_a, m_w_branch_a, v_w_branch_a),
              w_branch_b=(w_branch_b, m_w_branch_b, v_w_branch_b), w_out=(w_out, m_w_out, v_w_out),
              w_ffn_in=(w_ffn_in, m_w_ffn_in, v_w_ffn_in), w_ffn_out=(w_ffn_out, m_w_ffn_out, v_w_ffn_out))
    res = {}

    def finish_group(gi, after):
        names, handle = in_flight[gi]
        sends, zones = _exchange_wait(handle, after, "scatter_wait_%d" % gi)
        for nm, zone, sent in zip(names, zones, sends):
            w, m, v = (tr(a) if nm in transposed else a[0] for a in ws[nm])
            out = _adamw(zone, w, m, v, "adamw_" + nm, mine=sent, me=me.reshape(1).astype(jnp.int32))
            after = out[0]
            res[nm] = [jnp.transpose(o) for o in out] if nm in transposed else out
        return after

    small_h = _exchange_start([_pack_small(small)], False, "gather_small_start", after=grad_x)
    done = finish_group(1, finish_group(0, small_h["token"]))
    _, (slab_all,) = _exchange_wait(small_h, done, "gather_small_wait")
    small_w = dict(b_ada=b_ada, g_pre_mix=g_pre_mix, g_post_mix=g_post_mix, g_pre_ffn=g_pre_ffn, g_post_ffn=g_post_ffn,
                   b_f=b_f, sinks=sinks, loss=jnp.zeros((1,), f32))
    small_m = dict(b_ada=m_b_ada, g_pre_mix=m_g_pre_mix, g_post_mix=m_g_post_mix, g_pre_ffn=m_g_pre_ffn,
                   g_post_ffn=m_g_post_ffn, b_f=m_b_f, sinks=m_sinks, loss=jnp.zeros((1,), f32))
    small_v = dict(b_ada=v_b_ada, g_pre_mix=v_g_pre_mix, g_post_mix=v_g_post_mix, g_pre_ffn=v_g_pre_ffn,
                   g_post_ffn=v_g_post_ffn, b_f=v_b_f, sinks=v_sinks, loss=jnp.ones((1,), f32))
    shapes = {k: small_w[k].shape for k, _ in _SMALL}
    s_out = _adamw(slab_all, _pack_small(small_w), _pack_small(small_m), _pack_small(small_v), "adamw_small")
    s_grad, s_delta, s_m, s_v = (_unpack_small(o, shapes) for o in s_out)

    d_ada_all = lax.dynamic_slice(slab_all[:, :6144 // LANES, :].reshape(N_DEV, 6144), (0, me * ada_w), (N_DEV, ada_w))
    ada_parts = _ada_wgrad(c_all, d_ada_all, "ada_wgrad")

    res["w_ada"] = _adamw(ada_parts, w_ada[0], m_w_ada[0], v_w_ada[0], "adamw_w_ada")
    finish_group(2, res["w_ada"][0])

    order = ["w_ada", "b_ada", "g_pre_mix", "g_post_mix", "w_in", "b_f", "sinks", "w_branch_a", "w_branch_b", "w_out",
             "g_pre_ffn", "g_post_ffn", "w_ffn_in", "w_ffn_out"]
    outs = [s_grad["loss"].reshape(()), grad_x[None]]
    for which, small_o in enumerate((s_grad, s_delta, s_m, s_v)):
        for nm in order:
            outs.append(res[nm][which][None] if nm in res else small_o[nm])
    return tuple(outs)
```

```python
import math

import jax
import jax.numpy as jnp
from jax import lax
from jax.experimental import pallas as pl
from jax.experimental.pallas import tpu as pltpu

f32 = jnp.float32
bf16 = jnp.bfloat16

D_MODEL = 1024
HEAD_DIM = 64
N_HEADS = 8
N_PAIRS = 4
QKV_W = 2304
F_OFF = 2304
WINDOW = 128
ROPE_THETA = 10000.0
RMS_EPS = 1e-6
N_DEV = 8
ADAM_LR, ADAM_B1, ADAM_B2, ADAM_EPS, ADAM_WD, ADAM_STEP = 0.001, 0.9, 0.999, 1e-08, 0.01, 10
NEG = -1e30
L_ROW = (HEAD_DIM, 0)
LANES = 128
VMEM_LIMIT = 48 * 1024 * 1024
MESH = pl.DeviceIdType.MESH

_NT = (((1,), (1,)), ((), ()))
_TN = (((0,), (0,)), ((), ()))


def _params(n_grid=0):
    sem = ("arbitrary",) * n_grid if n_grid else None
    return pltpu.CompilerParams(dimension_semantics=sem, vmem_limit_bytes=VMEM_LIMIT)


def _row_tile(s, want):
    t = min(s, want)
    assert s % t == 0, (s, t)
    return t


MATMUL_VMEM_BUDGET = 40 * 1024 * 1024


def _matmul_tiles(m, n, k, a_item, b_item, o_item):
    def tiles(d):
        return [t for t in range(LANES, min(d, 2048) + 1, LANES) if d % t == 0] or [d]

    best = None
    for tm in tiles(m):
        for tn in tiles(n):
            vmem = 2 * (tm * k * a_item + tn * k * b_item + tm * tn * o_item) + tm * tn * 4
            if vmem > MATMUL_VMEM_BUDGET:
                continue
            traffic = m * k * a_item + n * k * b_item * (1 if tn == n else m // tm) + m * n * o_item
            steps = (m // tm) * (n // tn)
            key = (traffic, 0, steps) if steps >= 4 else (traffic, 1, -steps)
            if best is None or key < best[0]:
                best = (key, tm, tn)
    assert best is not None, (m, n, k)
    return best[1], best[2]


def _matmul(a, b, mode, out_dtype, name, after=None):
    if mode == "nn":
        (m, k), n = a.shape, b.shape[1]
    elif mode == "nt":
        (m, k), n = a.shape, b.shape[0]
    else:
        (k, m), n = a.shape, b.shape[1]
    tm, tn = _matmul_tiles(m, n, k, a.dtype.itemsize, b.dtype.itemsize, jnp.dtype(out_dtype).itemsize)
    if mode == "nn":
        a_spec, b_spec, dims = pl.BlockSpec((tm, k), lambda i, j: (i, 0)), pl.BlockSpec((k, tn), lambda i, j: (0, j)), None
    elif mode == "nt":
        a_spec, b_spec, dims = pl.BlockSpec((tm, k), lambda i, j: (i, 0)), pl.BlockSpec((tn, k), lambda i, j: (j, 0)), _NT
    else:
        a_spec, b_spec, dims = pl.BlockSpec((k, tm), lambda i, j: (0, i)), pl.BlockSpec((k, tn), lambda i, j: (0, j)), _TN

    def body(a_ref, b_ref, *rest):
        o_ref = rest[-1]
        av, bv = a_ref[...].astype(bf16), b_ref[...].astype(bf16)
        if dims is None:
            r = jnp.dot(av, bv, preferred_element_type=f32)
        else:
            r = lax.dot_general(av, bv, dims, preferred_element_type=f32)
        o_ref[...] = r.astype(out_dtype)

    extra = [] if after is None else [after]
    return pl.pallas_call(
        body, name=name, grid=(m // tm, n // tn), in_specs=[a_spec, b_spec] + [pl.BlockSpec(memory_space=pl.ANY)] * len(extra),
        out_specs=pl.BlockSpec((tm, tn), lambda i, j: (i, j)),
        out_shape=jax.ShapeDtypeStruct((m, n), out_dtype), compiler_params=_params(2),
    )(a, b, *extra)


def _rstd(v):
    return lax.rsqrt(jnp.mean(v * v, axis=-1, keepdims=True) + RMS_EPS)


def _row_spec(tm, d):
    return pl.BlockSpec((tm, d), lambda i: (i, 0))


def _vec_spec(d, rows=1):
    return pl.BlockSpec((rows, d), lambda i: (0, 0))


def _proj_spec(a, w, tm):
    return [_row_spec(tm, a.shape[1]), pl.BlockSpec(w.shape, lambda i: (0, 0))]


def _out_proj_postnorm_prenorm(a, w, x, g_post, gate, g_pre, scale, shift, name):
    s, d = x.shape
    tm = _row_tile(s, 512)

    def body(a_ref, w_ref, x_ref, gp_ref, gate_ref, g_ref, sc_ref, sh_ref, y_ref, x2_ref, h_ref):
        yv = jnp.dot(a_ref[...], w_ref[...], preferred_element_type=f32)
        y_ref[...] = yv
        x2 = x_ref[...] + gate_ref[...] * (yv * _rstd(yv) * gp_ref[...])
        x2_ref[...] = x2
        h_ref[...] = ((x2 * _rstd(x2) * g_ref[...]) * (1.0 + sc_ref[...]) + sh_ref[...]).astype(bf16)

    return pl.pallas_call(
        body, name=name, grid=(s // tm,), in_specs=_proj_spec(a, w, tm) + [_row_spec(tm, d)] + [_vec_spec(d)] * 5,
        out_specs=[_row_spec(tm, d)] * 3,
        out_shape=[jax.ShapeDtypeStruct((s, d), f32)] * 2 + [jax.ShapeDtypeStruct((s, d), bf16)], compiler_params=_params(1),
    )(a, w, x, g_post, gate, g_pre, scale, shift)


def _rms_bwd(u, v, r):
    return r * u - v * (r * r * r) * jnp.mean(u * v, axis=-1, keepdims=True)


def _out_proj_loss_tail(a, w, x, g, gate, target, name):
    s, d = x.shape
    tm = _row_tile(s, 512)

    def body(a_ref, w_ref, x_ref, g_ref, gate_ref, t_ref, loss_ref, do_ref, dy_ref, vec_ref):
        @pl.when(pl.program_id(0) == 0)
        def _():
            loss_ref[...] = jnp.zeros_like(loss_ref)
            vec_ref[...] = jnp.zeros_like(vec_ref)
        yv = jnp.dot(a_ref[...], w_ref[...], preferred_element_type=f32)
        r = _rstd(yv)
        yn = yv * r
        err = x_ref[...] + gate_ref[...] * (yn * g_ref[...]) - t_ref[...]
        loss_ref[...] += 0.5 * jnp.sum(jnp.mean(err * err, axis=-1, keepdims=True), axis=0, keepdims=True)
        dr = err / d
        do_ref[...] = dr
        dn = dr * gate_ref[...]
        vec_ref[0:1, :] += jnp.sum(dr * (yn * g_ref[...]), axis=0, keepdims=True)
        vec_ref[1:2, :] += jnp.sum(dn * yn, axis=0, keepdims=True)
        dy_ref[...] = _rms_bwd(dn * g_ref[...], yv, r).astype(bf16)

    return pl.pallas_call(
        body, name=name, grid=(s // tm,),
        in_specs=_proj_spec(a, w, tm) + [_row_spec(tm, d)] + [_vec_spec(d)] * 2 + [_row_spec(tm, d)],
        out_specs=[_vec_spec(LANES), _row_spec(tm, d), _row_spec(tm, d), _vec_spec(d, 8)],
        out_shape=[jax.ShapeDtypeStruct((1, LANES), f32), jax.ShapeDtypeStruct((s, d), f32),
                   jax.ShapeDtypeStruct((s, d), bf16), jax.ShapeDtypeStruct((8, d), f32)],
        compiler_params=_params(1),
    )(a, w, x, g, gate, target)


def _dgrad_prenorm_bwd(terms, x, g, scale, dres, name, after=None, below=None):
    s, d = x.shape
    n = len(terms)
    k = sum(a.shape[1] for a, _, _ in terms)
    row_bytes = 2 * (2 * k) + d * (4 + 2 * 4 * 3 + (2 * 4 + 2 * 2 if below else 0))
    tm = next(t for t in (512, 256, 128) if s % t == 0 and 4 * k * d + t * row_bytes <= MATMUL_VMEM_BUDGET)
    extra = [] if after is None else [after]

    def body(*refs):
        a_refs, b_refs = refs[:n], refs[n:2 * n]
        x_ref, g_ref, sc_ref, dr_ref = refs[2 * n:2 * n + 4]
        n_in = 2 * n + 4 + (3 if below else 0) + len(extra)
        dx_ref, vec_ref = refs[n_in], refs[n_in + 1]
        if below:
            y_ref, gp_ref, gate_ref = refs[2 * n + 4:2 * n + 7]
            dy_ref, vec2_ref = refs[n_in + 2], refs[n_in + 3]

        @pl.when(pl.program_id(0) == 0)
        def _():
            vec_ref[...] = jnp.zeros_like(vec_ref)
            if below:
                vec2_ref[...] = jnp.zeros_like(vec2_ref)
        dhv = jnp.dot(a_refs[0][...], b_refs[0][...], preferred_element_type=f32)
        for i in range(1, n):
            dhv = dhv + jnp.dot(a_refs[i][...], b_refs[i][...], preferred_element_type=f32)
        xv = x_ref[...]
        r = _rstd(xv)
        xn = xv * r
        dn = dhv * (1.0 + sc_ref[...])
        vec_ref[0:1, :] += jnp.sum(dhv, axis=0, keepdims=True)
        vec_ref[1:2, :] += jnp.sum(dhv * (xn * g_ref[...]), axis=0, keepdims=True)
        vec_ref[2:3, :] += jnp.sum(dn * xn, axis=0, keepdims=True)
        dx = dr_ref[...] + _rms_bwd(dn * g_ref[...], xv, r)
        dx_ref[...] = dx
        if below:
            yv = y_ref[...]
            ry = _rstd(yv)
            yn = yv * ry
            dny = dx * gate_ref[...]
            vec2_ref[0:1, :] += jnp.sum(dx * (yn * gp_ref[...]), axis=0, keepdims=True)
            vec2_ref[1:2, :] += jnp.sum(dny * yn, axis=0, keepdims=True)
            dy_ref[...] = _rms_bwd(dny * gp_ref[...], yv, ry).astype(bf16)

    in_specs = ([_row_spec(tm, a.shape[1]) for a, _, _ in terms]
                + [pl.BlockSpec((a.shape[1], d), lambda i, r=r: (r, 0)) for a, _, r in terms]
                + [_row_spec(tm, d)] + [_vec_spec(d)] * 2 + [_row_spec(tm, d)])
    out_specs = [_row_spec(tm, d), _vec_spec(d, 8)]
    out_shape = [jax.ShapeDtypeStruct((s, d), f32), jax.ShapeDtypeStruct((8, d), f32)]
    args = [a for a, _, _ in terms] + [b for _, b, _ in terms] + [x, g, scale, dres]
    if below:
        in_specs += [_row_spec(tm, d)] + [_vec_spec(d)] * 2
        out_specs += [_row_spec(tm, d), _vec_spec(d, 8)]
        out_shape += [jax.ShapeDtypeStruct((s, d), bf16), jax.ShapeDtypeStruct((8, d), f32)]
        args += list(below)
    return pl.pallas_call(
        body, name=name, grid=(s // tm,), in_specs=in_specs + [pl.BlockSpec(memory_space=pl.ANY)] * len(extra),
        out_specs=out_specs, out_shape=out_shape, compiler_params=_params(1),
    )(*args, *extra)


def _lane():
    return lax.broadcasted_iota(jnp.int32, (1, LANES), 1)


def _rope_tables(pos_col, inv_freq, name):
    s = pos_col.shape[0]

    def body(p_ref, f_ref, cos_ref, sin_ref):
        ang = p_ref[...].astype(f32) * f_ref[...]
        first_half = (_lane() % HEAD_DIM) < HEAD_DIM // 2
        cos_ref[...] = jnp.cos(ang)
        sn = jnp.sin(ang)
        sin_ref[...] = jnp.where(first_half, -sn, sn)

    return pl.pallas_call(
        body, name=name, out_shape=[jax.ShapeDtypeStruct((s, LANES), f32)] * 2, compiler_params=_params(),
    )(pos_col, inv_freq)


def _swap_halves(v):
    first_half = (_lane() % HEAD_DIM) < HEAD_DIM // 2
    return jnp.where(first_half, pltpu.roll(v, LANES - HEAD_DIM // 2, axis=1), pltpu.roll(v, HEAD_DIM // 2, axis=1))


def _prenorm_proj_qkv(x, g, mod_scale, mod_shift, w_qkv_t, cos, sin_s, name):
    s, d = x.shape
    tm = _row_tile(s, 512)
    scale = 1.0 / math.sqrt(HEAD_DIM)

    def body(x_ref, g_ref, msc_ref, msh_ref, w_ref, c_ref, s_ref, h_ref, qa_ref, ka_ref, va_ref, qb_ref, kb_ref, vb_ref):
        xv = x_ref[...]
        h = ((xv * _rstd(xv) * g_ref[...]) * (1.0 + msc_ref[...]) + msh_ref[...]).astype(bf16)
        h_ref[...] = h
        proj = lax.dot_general(h, w_ref[...], _NT, preferred_element_type=f32)
        cs, sn = c_ref[...], s_ref[...]
        low = _lane() < HEAD_DIM

        def blk(j):
            return proj[:, j * LANES:(j + 1) * LANES]

        def rope(v):
            return v * cs + _swap_halves(v) * sn

        def expand(v):
            other = pltpu.roll(v, HEAD_DIM, axis=1)
            return jnp.where(low, v, other), jnp.where(low, other, v)

        for j in range(N_PAIRS):
            qa_ref[:, j * LANES:(j + 1) * LANES] = (rope(blk(j)) * scale).astype(bf16)
            qb_ref[:, j * LANES:(j + 1) * LANES] = (blk(6 + j) * scale).astype(bf16)
            kb_ref[:, j * LANES:(j + 1) * LANES] = blk(10 + j).astype(bf16)
            vb_ref[:, j * LANES:(j + 1) * LANES] = blk(14 + j).astype(bf16)
        k0, k1 = expand(rope(blk(4)))
        v0, v1 = expand(blk(5))
        for j in range(N_PAIRS):
            ka_ref[:, j * LANES:(j + 1) * LANES] = (k0 if j < 2 else k1).astype(bf16)
            va_ref[:, j * LANES:(j + 1) * LANES] = (v0 if j < 2 else v1).astype(bf16)

    hw = N_PAIRS * LANES
    return pl.pallas_call(
        body, name=name, grid=(s // tm,),
        in_specs=[_row_spec(tm, d)] + [_vec_spec(d)] * 3
        + [pl.BlockSpec((QKV_W, d), lambda i: (0, 0)), _row_spec(tm, LANES), _row_spec(tm, LANES)],
        out_specs=[_row_spec(tm, d)] + [_row_spec(tm, hw)] * 6,
        out_shape=[jax.ShapeDtypeStruct((s, d), bf16)] + [jax.ShapeDtypeStruct((s, hw), bf16)] * 6, compiler_params=_params(1),
    )(x, g, mod_scale, mod_shift, w_qkv_t, cos, sin_s)


def _qkv_prep_bwd(dqa_t, dka, dva, dqb_t, dkb, dvb, cos, sin_s, name):
    s = dka.shape[0]
    tm = _row_tile(s, 512)
    scale = 1.0 / math.sqrt(HEAD_DIM)
    hw = N_PAIRS * LANES
    t_spec = pl.BlockSpec((hw, tm), lambda i: (0, i))

    def body(dqa_ref, dka_ref, dva_ref, dqb_ref, dkb_ref, dvb_ref, c_ref, s_ref, o_ref):
        cs, sn = c_ref[...], s_ref[...]
        low = _lane() < HEAD_DIM

        def blk(ref, j):
            return ref[:, j * LANES:(j + 1) * LANES].astype(f32)

        def blk_t(ref, j):
            return ref[j * LANES:(j + 1) * LANES, :].T

        def unrope(v):
            return v * cs + _swap_halves(v * sn)

        def fold(ref):
            a, b = blk(ref, 0) + blk(ref, 1), blk(ref, 2) + blk(ref, 3)
            kv0 = a + pltpu.roll(a, HEAD_DIM, axis=1)
            kv1 = b + pltpu.roll(b, HEAD_DIM, axis=1)
            return jnp.where(low, kv0, kv1)

        for j in range(N_PAIRS):
            o_ref[:, j * LANES:(j + 1) * LANES] = (unrope(blk_t(dqa_ref, j)) * scale).astype(bf16)
            o_ref[:, (6 + j) * LANES:(7 + j) * LANES] = (blk_t(dqb_ref, j) * scale).astype(bf16)
            o_ref[:, (10 + j) * LANES:(11 + j) * LANES] = blk(dkb_ref, j).astype(bf16)
            o_ref[:, (14 + j) * LANES:(15 + j) * LANES] = blk(dvb_ref, j).astype(bf16)
        o_ref[:, 4 * LANES:5 * LANES] = unrope(fold(dka_ref)).astype(bf16)
        o_ref[:, 5 * LANES:6 * LANES] = fold(dva_ref).astype(bf16)

    return pl.pallas_call(
        body, name=name, grid=(s // tm,),
        in_specs=[t_spec, _row_spec(tm, hw), _row_spec(tm, hw), t_spec, _row_spec(tm, hw), _row_spec(tm, hw)] + [_row_spec(tm, LANES)] * 2,
        out_specs=_row_spec(tm, QKV_W), out_shape=jax.ShapeDtypeStruct((s, QKV_W), bf16), compiler_params=_params(1),
    )(dqa_t, dka, dva, dqb_t, dkb, dvb, cos, sin_s)


def _cumsum_rows(v, reverse=False):
    n = v.shape[0]
    row = lax.broadcasted_iota(jnp.int32, v.shape, 0)
    sh = 1
    while sh < n:
        if reverse:
            v = v + jnp.where(row < n - sh, pltpu.roll(v, n - sh, axis=0), 0.0)
        else:
            v = v + jnp.where(row >= sh, pltpu.roll(v, sh, axis=0), 0.0)
        sh *= 2
    return v


def _log_sigmoid(z):
    return jnp.minimum(z, 0.0) - jnp.log1p(jnp.exp(-jnp.abs(z)))


def _forget_prep(h, w_f_t, bf_row, name):
    s, d = h.shape
    tm = _row_tile(s, 1024)

    def body(h_ref, w_ref, b_ref, f_ref, cb_ref, last_ref):
        @pl.when(pl.program_id(0) == 0)
        def _():
            last_ref[...] = jnp.zeros_like(last_ref)
        fl = lax.dot_general(h_ref[...], w_ref[...], _NT, preferred_element_type=f32)
        f_ref[...] = fl
        cum = _cumsum_rows(_log_sigmoid(fl + b_ref[...])) + last_ref[0:1, :]
        last_ref[0:1, :] = cum[tm - 1:tm, :]
        for hd in range(N_HEADS):
            cb_ref[:, hd * LANES:(hd + 1) * LANES] = jnp.broadcast_to(cum[:, hd:hd + 1], (tm, LANES))

    return pl.pallas_call(
        body, name=name, grid=(s // tm,),
        in_specs=[_row_spec(tm, d), pl.BlockSpec((LANES, d), lambda i: (0, 0)), _vec_spec(LANES)],
        out_specs=[_row_spec(tm, LANES), _row_spec(tm, N_HEADS * LANES)],
        out_shape=[jax.ShapeDtypeStruct((s, LANES), f32), jax.ShapeDtypeStruct((s, N_HEADS * LANES), f32)],
        scratch_shapes=[pltpu.VMEM((8, LANES), f32)], compiler_params=_params(1),
    )(h, w_f_t, bf_row)


def _forget_prep_bwd(rs, dcs, fl, bf_row, name):
    s = fl.shape[0]
    tm = _row_tile(s, 1024)
    n = s // tm

    def body(r_ref, c_ref, f_ref, b_ref, df_ref, db_ref, next_ref):
        @pl.when(pl.program_id(0) == 0)
        def _():
            next_ref[...] = jnp.zeros_like(next_ref)
            db_ref[...] = jnp.zeros_like(db_ref)
        eye = (lax.broadcasted_iota(jnp.int32, (N_HEADS, LANES), 0) == lax.broadcasted_iota(jnp.int32, (N_HEADS, LANES), 1)).astype(f32)
        dcum = lax.dot_general(r_ref[...], eye, _TN, precision=lax.Precision.HIGHEST, preferred_element_type=f32)
        for h in range(N_HEADS):
            dcum = dcum - jnp.where(_lane() == h, jnp.sum(c_ref[:, h * LANES:(h + 1) * LANES], axis=1, keepdims=True), 0.0)
        dlf = _cumsum_rows(dcum, reverse=True) + next_ref[0:1, :]
        next_ref[0:1, :] = dlf[0:1, :]
        z = f_ref[...] + b_ref[...]
        df = jnp.where(_lane() < N_HEADS, dlf * jax.nn.sigmoid(-z), 0.0)
        df_ref[...] = df.astype(bf16)
        db_ref[0:1, :] += jnp.sum(df, axis=0, keepdims=True)

    def rows(width):
        return pl.BlockSpec((tm, width), lambda i: (n - 1 - i, 0))

    return pl.pallas_call(
        body, name=name, grid=(n,),
        in_specs=[pl.BlockSpec((N_HEADS, tm), lambda i: (0, n - 1 - i)), rows(N_HEADS * LANES), rows(LANES), _vec_spec(LANES)],
        out_specs=[rows(LANES), _vec_spec(LANES, 8)],
        out_shape=[jax.ShapeDtypeStruct((s, LANES), bf16), jax.ShapeDtypeStruct((8, LANES), f32)],
        scratch_shapes=[pltpu.VMEM((8, LANES), f32)], compiler_params=_params(1),
    )(rs, dcs, fl, bf_row)


def _tile_mask(n_keys, n_queries, off, window):
    shape = (n_keys, n_queries)
    d = lax.broadcasted_iota(jnp.int32, shape, 1) - lax.broadcasted_iota(jnp.int32, shape, 0) + off
    valid = d >= 0
    return jnp.logical_and(valid, d < window) if window else valid


def _wide(v, t):
    return jnp.concatenate([v] * (t // LANES), axis=1)


def _attn_fwd(q, k, v, name, *, cum_b=None, sink_rows=None, window=None, t=256):
    s = q.shape[0]
    t = _row_tile(s, t)
    fox, has_sink = cum_b is not None, sink_rows is not None
    assert not window or (window % LANES == 0 and LANES + window <= s)

    def body(*refs):
        q_ref, k_ref, v_ref = refs[:3]
        rest = list(refs[3:])
        cb_ref = rest.pop(0) if fox else None
        sink_ref = rest.pop(0) if has_sink else None
        o_ref, lse_ref = rest
        i = pl.program_id(1)
        low = _lane() < HEAD_DIM
        top = lax.broadcasted_iota(jnp.int32, (LANES, 1), 0) < HEAD_DIM
        q2 = q_ref[...]
        zero = jnp.zeros_like(q2)
        qms = (jnp.where(low, q2, zero), jnp.where(low, zero, q2))

        def tile(k0, n_keys, off, carry, masked, queries=slice(0, t)):
            nq = queries.stop - queries.start
            kblk, vblk = k_ref[pl.ds(k0, n_keys), :], v_ref[pl.ds(k0, n_keys), :]
            valid = _tile_mask(n_keys, nq, off, window) if masked else None
            ones = jnp.ones_like(vblk)
            vs = tuple(jnp.where(_lane() == L_ROW[h], ones, vblk) for h in range(2))

            def scores(h):
                return lax.dot_general(kblk, qms[h][queries], _NT, preferred_element_type=f32)

            def softmax(h, sc):
                m = carry[h][0]
                if fox:
                    sc = sc - _wide(cb_ref[pl.ds(k0, n_keys), h * LANES:(h + 1) * LANES], nq)
                if masked:
                    sc = jnp.where(valid, sc, NEG)
                m_new = jnp.maximum(m, jnp.max(sc, axis=0, keepdims=True))
                return m_new, jnp.exp(m - m_new), jnp.exp(sc - m_new).astype(bf16)

            def update(h, m_new, alpha, p):
                return m_new, alpha * carry[h][1] + lax.dot_general(vs[h], p, _TN, preferred_element_type=f32)

            if window:
                return tuple(update(h, *softmax(h, scores(h))) for h in range(2))
            scs = [scores(h) for h in range(2)]
            stats = [softmax(h, scs[h]) for h in range(2)]
            return tuple(update(h, *stats[h]) for h in range(2))

        def start(nq):
            if has_sink:
                row = lax.broadcasted_iota(jnp.int32, (LANES, nq), 0)
                return tuple((_wide(sink_ref[h:h + 1, :], nq), (row == L_ROW[h]).astype(f32)) for h in range(2))
            return tuple((jnp.full((1, nq), NEG, f32), jnp.zeros((LANES, nq), f32)) for h in range(2))

        def finish(carry, queries):
            (m0, a0), (m1, a1) = carry
            l0, l1 = a0[L_ROW[0]:L_ROW[0] + 1, :], a1[L_ROW[1]:L_ROW[1] + 1, :]
            o_t = jnp.where(top, a0 * (1.0 / l0), a1 * (1.0 / l1))
            o_ref[queries, :] = o_t.T.astype(bf16)
            lse_ref[0:1, queries] = m0 + jnp.log(l0)
            lse_ref[1:2, queries] = m1 + jnp.log(l1)

        if window:
            for c in range(t // LANES):
                queries = slice(c * LANES, (c + 1) * LANES)
                q0 = i * t + c * LANES
                k0 = pl.multiple_of(jnp.maximum(q0 - window, 0), LANES)
                finish(tile(k0, LANES + window, q0 - k0, start(LANES), True, queries), queries)
        else:
            carry = lax.fori_loop(0, i, lambda kb, c: tile(pl.multiple_of(kb * t, t), t, 0, c, False), start(t))
            half, k_own = t // 2, pl.multiple_of(i * t, t)
            carry = tile(k_own, half, 0, carry, True)
            finish(tuple((m[:, :half], a[:, :half]) for m, a in carry), slice(0, half))
            carry = tuple((m[:, half:], a[:, half:]) for m, a in carry)
            finish(tile(pl.multiple_of(k_own + half, half), half, 0, carry, True, slice(half, t)), slice(half, t))

    q_spec = pl.BlockSpec((t, LANES), lambda j, i: (i, j))
    kv_spec = pl.BlockSpec((s, LANES), lambda j, i: (0, j))
    in_specs, args = [q_spec, kv_spec, kv_spec], [q, k, v]
    if fox:
        in_specs += [pl.BlockSpec((s, 2 * LANES), lambda j, i: (0, j))]
        args += [cum_b]
    if has_sink:
        in_specs += [pl.BlockSpec((None, 2, LANES), lambda j, i: (j, 0, 0))]
        args += [sink_rows.reshape(N_PAIRS, 2, LANES)]
    return pl.pallas_call(
        body, name=name, grid=(N_PAIRS, s // t), in_specs=in_specs,
        out_specs=[q_spec, pl.BlockSpec((None, 2, t), lambda j, i: (j, 0, i))],
        out_shape=[jax.ShapeDtypeStruct((s, N_PAIRS * LANES), bf16), jax.ShapeDtypeStruct((N_PAIRS, 2, s), f32)],
        compiler_params=_params(2),
    )(*args)


def _branch_dgrad_delta(db, w, o, name, *, lse=None, sink_rows=None, after=None):
    s, hw = o.shape
    tm = _row_tile(s, 1024)
    has_sink = sink_rows is not None
    extra = [] if after is None else [after]

    def body(*refs):
        db_ref, w_ref, o_ref = refs[:3]
        outs = refs[3 + (2 if has_sink else 0) + len(extra):]
        do_ref, dl_ref = outs[:2]
        if has_sink:
            lse_ref, sink_ref = refs[3:5]
            ds_ref = outs[2]

            @pl.when(pl.program_id(0) == 0)
            def _():
                ds_ref[...] = jnp.zeros_like(ds_ref)
        do = lax.dot_general(db_ref[...], w_ref[...], _NT, preferred_element_type=f32).astype(bf16)
        do_ref[...] = do
        for j in range(N_PAIRS):
            cols = slice(j * LANES, (j + 1) * LANES)
            prod_t = (do[:, cols].astype(f32) * o_ref[:, cols].astype(f32)).T
            for h in range(2):
                dl = jnp.sum(prod_t[h * HEAD_DIM:(h + 1) * HEAD_DIM, :], axis=0, keepdims=True)
                dl_ref[j, h:h + 1, :] = dl
                if has_sink:
                    r = 2 * j + h
                    p_sink = jnp.exp(sink_ref[r:r + 1, 0:1] - lse_ref[j, h:h + 1, :])
                    ds_ref[r:r + 1, :] += -jnp.sum(p_sink * dl, axis=1, keepdims=True)

    rows_spec = pl.BlockSpec((N_PAIRS, 2, tm), lambda i: (0, 0, i))
    in_specs = [_row_spec(tm, db.shape[1]), pl.BlockSpec(w.shape, lambda i: (0, 0)), _row_spec(tm, hw)]
    args = [db, w, o]
    out_specs = [_row_spec(tm, hw), rows_spec]
    out_shape = [jax.ShapeDtypeStruct((s, hw), bf16), jax.ShapeDtypeStruct((N_PAIRS, 2, s), f32)]
    if has_sink:
        in_specs += [rows_spec, _vec_spec(LANES, N_HEADS)]
        args += [lse, sink_rows]
        out_specs += [_vec_spec(LANES, N_HEADS)]
        out_shape += [jax.ShapeDtypeStruct((N_HEADS, LANES), f32)]
    return pl.pallas_call(
        body, name=name, grid=(s // tm,), in_specs=in_specs + [pl.BlockSpec(memory_space=pl.ANY)] * len(extra),
        out_specs=out_specs, out_shape=out_shape, compiler_params=_params(1),
    )(*args, *extra)


def _attn_bwd(q, k, v, do, lse, delta, name, *, cum_b=None, window=None, t=256):
    s = q.shape[0]
    t = _row_tile(s, t)
    nblk = s // t
    fox = cum_b is not None
    assert not window or (window % LANES == 0 and LANES + window <= s)

    def body(*refs):
        k_ref, v_ref, q_ref, do_ref, lse_ref, dl_ref = refs[:6]
        rest = list(refs[6:])
        cb_ref = rest.pop(0) if fox else None
        dq_ref, dk_ref, dv_ref = rest[:3]
        dcs_ref, rs_ref = (rest[3], rest[4]) if fox else (None, None)
        dk_acc, dv_acc = rest[-2:]
        b = pl.program_id(1)
        k0 = pl.multiple_of(b * t, t)

        @pl.when(b == 0)
        def _():
            dq_ref[...] = jnp.zeros_like(dq_ref)
            if fox:
                rs_ref[...] = jnp.zeros_like(rs_ref)

        dk_acc[...] = jnp.zeros_like(dk_acc)
        dv_acc[...] = jnp.zeros_like(dv_acc)
        if fox:
            dcs_ref[...] = jnp.zeros_like(dcs_ref)
        low = _lane() < HEAD_DIM
        top = lax.broadcasted_iota(jnp.int32, (LANES, 1), 0) < HEAD_DIM
        kblk, vblk = k_ref[...], v_ref[...]
        k_t = kblk.astype(f32).T.astype(bf16)
        cks = [_wide(cb_ref[pl.ds(k0, t), h * LANES:(h + 1) * LANES], 2 * t) for h in range(2)] if fox else None

        def tile(q0, n_queries, off, masked, keys=slice(0, t)):
            cols = pl.ds(q0, n_queries)
            q2, do2 = q_ref[cols, :], do_ref[cols, :]
            zero = jnp.zeros_like(q2)
            valid = _tile_mask(keys.stop - keys.start, n_queries, off, window) if masked else None
            dq_parts = []
            for h in range(2):
                qm = jnp.where(low, q2, zero) if h == 0 else jnp.where(low, zero, q2)
                dom = jnp.where(low, do2, zero) if h == 0 else jnp.where(low, zero, do2)
                sc = lax.dot_general(kblk[keys], qm, _NT, preferred_element_type=f32)
                if fox:
                    sc = sc - cks[h][keys, :n_queries]
                if masked:
                    sc = jnp.where(valid, sc, NEG)
                p = jnp.exp(sc - lse_ref[h:h + 1, cols])
                dp = lax.dot_general(vblk[keys], dom, _NT, preferred_element_type=f32)
                ds = p * (dp - dl_ref[h:h + 1, cols])
                pb, dsb = p.astype(bf16), ds.astype(bf16)
                dv_acc[keys, :] += jnp.dot(pb, dom, preferred_element_type=f32)
                dk_acc[keys, :] += jnp.dot(dsb, qm, preferred_element_type=f32)
                dq_parts.append(jnp.dot(k_t[:, keys], dsb, preferred_element_type=f32))
                if fox:
                    dcs_ref[keys, h * LANES:(h + 1) * LANES] += sum(ds[:, g * LANES:(g + 1) * LANES]
                                                                    for g in range(n_queries // LANES))
                    rs_ref[h:h + 1, cols] += jnp.sum(ds, axis=0, keepdims=True)
            dq_ref[:, cols] += jnp.where(top, dq_parts[0], dq_parts[1])

        later = nblk - 1 - b

        def later_pair(p, carry):
            tile(pl.multiple_of((b + 1 + later % 2 + 2 * p) * t, t), 2 * t, 0, False)
            return carry

        if window:
            for c in range(t // LANES):
                first = b * t + c * LANES
                q0 = pl.multiple_of(jnp.minimum(first, s - (LANES + window)), LANES)
                tile(q0, LANES + window, q0 - first, True, slice(c * LANES, (c + 1) * LANES))
        else:
            half = t // 2
            tile(k0, half, 0, True, slice(0, half))
            tile(pl.multiple_of(k0 + half, half), half, half, True)
            @pl.when(later % 2 == 1)
            def _():
                tile(pl.multiple_of(k0 + t, t), t, 0, False)
            lax.fori_loop(0, later // 2, later_pair, 0)
        dk_ref[...] = dk_acc[...].astype(bf16)
        dv_ref[...] = dv_acc[...].astype(bf16)

    kv_spec = pl.BlockSpec((t, LANES), lambda j, b: (b, j))
    seq_spec = pl.BlockSpec((s, LANES), lambda j, b: (0, j))
    rows_spec = pl.BlockSpec((None, 2, s), lambda j, b: (j, 0, 0))
    hw = N_PAIRS * LANES
    in_specs, args = [kv_spec, kv_spec, seq_spec, seq_spec, rows_spec, rows_spec], [k, v, q, do, lse, delta]
    out_specs = [pl.BlockSpec((LANES, s), lambda j, b: (j, 0)), kv_spec, kv_spec]
    out_shape = [jax.ShapeDtypeStruct((hw, s), f32), jax.ShapeDtypeStruct((s, hw), bf16), jax.ShapeDtypeStruct((s, hw), bf16)]
    if fox:
        in_specs += [pl.BlockSpec((s, 2 * LANES), lambda j, b: (0, j))]
        args += [cum_b]
        out_specs += [pl.BlockSpec((t, 2 * LANES), lambda j, b: (b, j)), rows_spec]
        out_shape += [jax.ShapeDtypeStruct((s, N_HEADS * LANES), f32), jax.ShapeDtypeStruct((N_PAIRS, 2, s), f32)]
    return pl.pallas_call(
        body, name=name, grid=(N_PAIRS, nblk), in_specs=in_specs, out_specs=out_specs, out_shape=out_shape,
        scratch_shapes=[pltpu.VMEM((t, LANES), f32)] * 2, compiler_params=_params(2),
    )(*args)


def _branch_merge(o_a, o_b, w_a, w_b, gl, name):
    s, k = o_a.shape
    d = w_a.shape[1]
    tm = _row_tile(s, 1024)

    def body(oa_ref, ob_ref, wa_ref, wb_ref, g_ref, ba_ref, bb_ref, m_ref):
        ba = jnp.dot(oa_ref[...], wa_ref[...], preferred_element_type=f32)
        bb = jnp.dot(ob_ref[...], wb_ref[...], preferred_element_type=f32)
        g0, g1 = jax.nn.sigmoid(g_ref[:, :d].astype(f32)), jax.nn.sigmoid(g_ref[:, d:].astype(f32))
        ba_ref[...] = ba.astype(bf16)
        bb_ref[...] = bb.astype(bf16)
        m_ref[...] = (g0 * ba + g1 * bb).astype(bf16)

    whole = pl.BlockSpec((k, d), lambda i: (0, 0))
    return pl.pallas_call(
        body, name=name, grid=(s // tm,),
        in_specs=[_row_spec(tm, k), _row_spec(tm, k), whole, whole, _row_spec(tm, 2 * d)],
        out_specs=[_row_spec(tm, d)] * 3, out_shape=[jax.ShapeDtypeStruct((s, d), bf16)] * 3, compiler_params=_params(1),
    )(o_a, o_b, w_a, w_b, gl)


def _out_dgrad_merge_bwd(dy, w_out, ba, bb, gl, name):
    s, d = ba.shape
    tm = _row_tile(s, 512)

    def body(dy_ref, w_ref, a_ref, b_ref, g_ref, da_ref, db_ref, dg_ref):
        dmv = lax.dot_general(dy_ref[...], w_ref[...], _NT, preferred_element_type=f32)
        g0, g1 = jax.nn.sigmoid(g_ref[:, :d].astype(f32)), jax.nn.sigmoid(g_ref[:, d:].astype(f32))
        da_ref[...] = (dmv * g0).astype(bf16)
        db_ref[...] = (dmv * g1).astype(bf16)
        dg_ref[:, :d] = (dmv * a_ref[...].astype(f32) * (g0 * (1.0 - g0))).astype(bf16)
        dg_ref[:, d:] = (dmv * b_ref[...].astype(f32) * (g1 * (1.0 - g1))).astype(bf16)

    return pl.pallas_call(
        body, name=name, grid=(s // tm,),
        in_specs=[_row_spec(tm, dy.shape[1]), pl.BlockSpec(w_out.shape, lambda i: (0, 0))] + [_row_spec(tm, d)] * 2
        + [_row_spec(tm, 2 * d)],
        out_specs=[_row_spec(tm, d)] * 2 + [_row_spec(tm, 2 * d)],
        out_shape=[jax.ShapeDtypeStruct((s, d), bf16)] * 2 + [jax.ShapeDtypeStruct((s, 2 * d), bf16)],
        compiler_params=_params(1),
    )(dy, w_out, ba, bb, gl)


GLU_TILE = 256


def _ffn_in_swiglu(h, w_t, name):
    s, d = h.shape
    f = w_t.shape[0] // 2
    tm = _row_tile(s, 4096)
    tg = GLU_TILE
    nb = f // tg

    def body(h_ref, wg_ref, wu_ref, g_ref, u_ref, act_ref):
        hv = h_ref[...]
        g = lax.dot_general(hv, wg_ref[...], _NT, preferred_element_type=f32)
        u = lax.dot_general(hv, wu_ref[...], _NT, preferred_element_type=f32)
        g_ref[...] = g.astype(bf16)
        u_ref[...] = u.astype(bf16)
        act_ref[...] = (g * jax.nn.sigmoid(g) * u).astype(bf16)

    col = pl.BlockSpec((tm, tg), lambda i, j: (i, j))
    return pl.pallas_call(
        body, name=name, grid=(s // tm, nb),
        in_specs=[pl.BlockSpec((tm, d), lambda i, j: (i, 0)), pl.BlockSpec((tg, d), lambda i, j: (j, 0)),
                  pl.BlockSpec((tg, d), lambda i, j: (j + nb, 0))],
        out_specs=[col] * 3, out_shape=[jax.ShapeDtypeStruct((s, f), bf16)] * 3, compiler_params=_params(2),
    )(h, w_t, w_t)


def _ffn_out_dgrad_swiglu(dy, w_out, g, u, name):
    s, d = dy.shape
    f = g.shape[1]
    tm = _row_tile(s, 4096)
    tg = GLU_TILE

    def body(dy_ref, w_ref, g_ref, u_ref, dg_ref, du_ref):
        dv = lax.dot_general(dy_ref[...], w_ref[...], _NT, preferred_element_type=f32)
        gv, uv = g_ref[...].astype(f32), u_ref[...].astype(f32)
        sg = jax.nn.sigmoid(gv)
        dg_ref[...] = (dv * uv * (sg * (1.0 + gv * (1.0 - sg)))).astype(bf16)
        du_ref[...] = (dv * (gv * sg)).astype(bf16)

    col = pl.BlockSpec((tm, tg), lambda i, j: (i, j))
    return pl.pallas_call(
        body, name=name, grid=(s // tm, f // tg),
        in_specs=[pl.BlockSpec((tm, d), lambda i, j: (i, 0)), pl.BlockSpec((tg, d), lambda i, j: (j, 0)), col, col],
        out_specs=[col] * 2, out_shape=[jax.ShapeDtypeStruct((s, f), bf16)] * 2, compiler_params=_params(2),
    )(dy, w_out, g, u)


def _wgrad_stack(parts, h, name):
    s, m = parts[0].shape
    d = h.shape[1]
    tm = 256
    nb = m // tm
    n = len(parts)

    def body(*refs):
        i = pl.program_id(0)
        for p in range(n):
            @pl.when(i // nb == p)
            def _(p=p):
                refs[n + 1][...] = lax.dot_general(refs[p][...], refs[n][...], _TN, preferred_element_type=f32).astype(bf16)

    a_specs = [pl.BlockSpec((s, tm), lambda i, p=p: (0, jnp.clip(i - p * nb, 0, nb - 1))) for p in range(n)]
    return pl.pallas_call(
        body, name=name, grid=(n * nb,), in_specs=a_specs + [pl.BlockSpec((s, d), lambda i: (0, 0))],
        out_specs=pl.BlockSpec((tm, d), lambda i: (i, 0)),
        out_shape=jax.ShapeDtypeStruct((n * m, d), bf16), compiler_params=_params(1),
    )(*parts, h)


def _ada_wgrad(c_all, d_all, name):
    n, d = c_all.shape
    w = d_all.shape[1]

    def body(c_ref, d_ref, o_ref):
        eye = (lax.broadcasted_iota(jnp.int32, (n, n), 0) == lax.broadcasted_iota(jnp.int32, (n, n), 1)).astype(f32)
        ct = lax.dot_general(c_ref[...], eye, _TN, precision=lax.Precision.HIGHEST, preferred_element_type=f32)
        g = ct[:, 0:1] * d_ref[0:1, :]
        for bi in range(1, n):
            g = g + ct[:, bi:bi + 1] * d_ref[bi:bi + 1, :]
        o_ref[0] = g

    return pl.pallas_call(
        body, name=name, out_shape=jax.ShapeDtypeStruct((1, d, w), f32), compiler_params=_params(),
    )(c_all, d_all)


def _adamw(parts, w, m, v, name, mine=None, me=None):
    r, c = w.shape
    n_parts = parts.shape[0]
    row_tiles = [t for t in range(min(r, 256), 0, -1) if r % t == 0 and (t % 16 == 0 or t == r)]
    if row_tiles:
        tr, tc = row_tiles[0], c
    else:
        tr, tc = r, next(t for t in (256, LANES) if c % t == 0)

    def body(*refs):
        w_ref, m_ref, v_ref, g_ref, d_ref, nm_ref, nv_ref = refs[-7:]
        if mine is None:
            p_ref, = refs[:-7]
        else:
            me_ref, p_ref, own_ref = refs[:-7]

        def part(i):
            if mine is None:
                return p_ref[i].astype(f32)
            return jnp.where(me_ref[0] == i, own_ref[...], p_ref[i]).astype(f32)

        g = part(0)
        for i in range(1, n_parts):
            g = g + part(i)
        mm = ADAM_B1 * m_ref[...] + (1.0 - ADAM_B1) * g
        vv = ADAM_B2 * v_ref[...] + (1.0 - ADAM_B2) * (g * g)
        m_hat = mm / (1.0 - ADAM_B1 ** ADAM_STEP)
        v_hat = vv / (1.0 - ADAM_B2 ** ADAM_STEP)
        g_ref[...] = g
        d_ref[...] = -ADAM_LR * (m_hat / (jnp.sqrt(v_hat) + ADAM_EPS) + ADAM_WD * w_ref[...])
        nm_ref[...] = mm
        nv_ref[...] = vv

    out_shape = [jax.ShapeDtypeStruct((r, c), f32)] * 4
    if mine is None:
        spec = pl.BlockSpec((tr, tc), lambda i, j: (i, j))
        return pl.pallas_call(
            body, name=name, grid=(r // tr, c // tc),
            in_specs=[pl.BlockSpec((n_parts, tr, tc), lambda i, j: (0, i, j))] + [spec] * 3,
            out_specs=[spec] * 4, out_shape=out_shape, compiler_params=_params(2),
        )(parts, w, m, v)
    spec = pl.BlockSpec((tr, tc), lambda i, j, me_ref: (i, j))
    return pl.pallas_call(
        body, name=name, out_shape=out_shape, compiler_params=_params(2),
        grid_spec=pltpu.PrefetchScalarGridSpec(
            num_scalar_prefetch=1, grid=(r // tr, c // tc),
            in_specs=[pl.BlockSpec((n_parts, tr, tc), lambda i, j, me_ref: (0, i, j)),
                      pl.BlockSpec((None, tr, tc), lambda i, j, me_ref: (me_ref[0], i, j))] + [spec] * 3,
            out_specs=[spec] * 4),
    )(me, parts, mine, w, m, v)


def _me():
    return lax.axis_index("x"), lax.axis_index("y"), lax.axis_index("c")


def _gather_prologue(c, w_ada, b_mine, w_in_t, name):
    n_dev, d = N_DEV, c.shape[1]
    ada_w = w_ada.shape[1]

    def body(c_ref, w_ref, b_ref, win_ref, call_ref, ada_ref, gin_ref, cols_ref, send_sems, recv_sems, local_sems):
        x, y, cc = _me()
        me, sibling = (x, y, cc), (x, y, 1 - cc)
        chips = [(1 - x, y), (x, 1 - y), (1 - x, 1 - y)]
        outs = (call_ref, ada_ref, gin_ref)

        def rows(a, dev):
            return outs[a].at[4 * dev[0] + 2 * dev[1] + dev[2]]

        def copy(a, k, block, to, src=None):
            return pltpu.make_async_remote_copy(
                src_ref=rows(a, block) if src is None else src, dst_ref=rows(a, block),
                send_sem=send_sems.at[a, k], recv_sem=recv_sems.at[a, k], device_id=to, device_id_type=MESH)

        def begin(a, src):
            own = pltpu.make_async_copy(src, rows(a, me), local_sems.at[a])
            sends = [copy(a, 0, me, sibling, src=src)] + [copy(a, 1 + j, me, (*chip, cc), src=src) for j, chip in enumerate(chips)]
            for cp in [own] + sends:
                cp.start()
            return own, sends

        def finish(a, own, sends):
            passed = []
            for j, chip in enumerate(chips):
                copy(a, 1 + j, (*chip, cc), me).wait_recv()
                passed.append(copy(a, 4 + j, (*chip, cc), sibling))
                passed[-1].start()
            copy(a, 0, sibling, me).wait_recv()
            for j, chip in enumerate(chips):
                copy(a, 4 + j, (*chip, 1 - cc), me).wait_recv()
            for cp in sends + passed:
                cp.wait_send()
            own.wait()

        finish(0, *begin(0, c_ref))
        cols_ref[...] = (jnp.dot(call_ref[:, 0, :].astype(bf16), w_ref[...].astype(bf16), preferred_element_type=f32)
                         + b_ref[...])
        finish(1, *begin(1, cols_ref))
        finish(2, *begin(2, win_ref))

    vmem, hbm = pl.BlockSpec(memory_space=pltpu.VMEM), pl.BlockSpec(memory_space=pl.ANY)
    return pl.pallas_call(
        body, name=name, in_specs=[vmem, vmem, vmem, hbm], out_specs=[vmem, vmem, hbm],
        out_shape=[jax.ShapeDtypeStruct((n_dev, 1, d), f32), jax.ShapeDtypeStruct((n_dev, n_dev, ada_w), f32),
                   jax.ShapeDtypeStruct((n_dev,) + w_in_t.shape, w_in_t.dtype)],
        scratch_shapes=[pltpu.VMEM((n_dev, ada_w), f32), pltpu.SemaphoreType.DMA((3, 7)), pltpu.SemaphoreType.DMA((3, 7)),
                        pltpu.SemaphoreType.DMA((3,))],
        compiler_params=pltpu.CompilerParams(vmem_limit_bytes=VMEM_LIMIT),
    )(c, w_ada, b_mine, w_in_t)


_FLIPS = ((0, 0, 1), (1, 0, 0), (0, 1, 0), (1, 1, 0), (1, 0, 1), (0, 1, 1), (1, 1, 1))
_HBM = pl.BlockSpec(memory_space=pltpu.HBM)
_SEM = pl.BlockSpec(memory_space=pltpu.SEMAPHORE)


def _exchange_copies(scatter, srcs, lands, send_sems, recv_sems):
    x, y, c = _me()
    me_row = 4 * x + 2 * y + c
    out = []
    for k, (fx, fy, fc) in enumerate(_FLIPS):
        peer = (x ^ fx, y ^ fy, c ^ fc)
        peer_row = 4 * peer[0] + 2 * peer[1] + peer[2]
        for a in range(len(srcs)):
            out.append(pltpu.make_async_remote_copy(
                src_ref=srcs[a].at[peer_row] if scatter else srcs[a], dst_ref=lands[a].at[me_row],
                send_sem=send_sems.at[7 * a + k], recv_sem=recv_sems.at[7 * a + k], device_id=peer, device_id_type=MESH))
    return out


def _own_copies(srcs, lands, own_sems):
    x, y, c = _me()
    return [pltpu.make_async_copy(srcs[a], lands[a].at[4 * x + 2 * y + c], own_sems.at[a]) for a in range(len(srcs))]


def _exchange_start(arrays, scatter, name, after=None):
    n = len(arrays)
    lands = [lax.empty(a.shape if scatter else (N_DEV,) + a.shape, a.dtype) for a in arrays]
    extra = [] if after is None else [after]

    def body(*refs):
        srcs, zones = refs[:n], refs[n:2 * n]
        send_sems, recv_sems, own_sems = refs[2 * n + len(extra):2 * n + len(extra) + 3]
        token = refs[-1]
        for cp in _exchange_copies(scatter, srcs, zones, send_sems, recv_sems):
            cp.start()
        for cp in [] if scatter else _own_copies(srcs, zones, own_sems):
            cp.start()
        token[...] = jnp.zeros_like(token)

    thru = [pltpu.HBM(a.shape, a.dtype) for a in list(arrays) + lands]
    outs = pl.pallas_call(
        body, name=name,
        out_shape=(pltpu.SemaphoreType.DMA((7 * n,)), pltpu.SemaphoreType.DMA((7 * n,)), pltpu.SemaphoreType.DMA((n,)), *thru,
                   jax.ShapeDtypeStruct((8, LANES), f32)),
        in_specs=[_HBM] * (2 * n) + [pl.BlockSpec(memory_space=pl.ANY)] * len(extra),
        out_specs=(_SEM, _SEM, _SEM, *[_HBM] * (2 * n), pl.BlockSpec(memory_space=pltpu.VMEM)),
        input_output_aliases={i: 3 + i for i in range(2 * n)},
        compiler_params=pltpu.CompilerParams(has_side_effects=pltpu.SideEffectType.DATAFLOW_SIDE_EFFECTING),
    )(*[pltpu.with_memory_space_constraint(a, pltpu.HBM) for a in list(arrays) + lands], *extra)
    return dict(n=n, scatter=scatter, sems=outs[:3], srcs=outs[3:3 + n], lands=outs[3 + n:3 + 2 * n], token=outs[-1])


def _exchange_wait(handle, after, name):
    n, scatter = handle["n"], handle["scatter"]

    def body(*refs):
        srcs, zones = refs[:n], refs[n:2 * n]
        send_sems, recv_sems, own_sems = refs[2 * n:2 * n + 3]
        for cp in _exchange_copies(scatter, srcs, zones, send_sems, recv_sems):
            cp.wait_send()
            cp.wait_recv()
        for cp in [] if scatter else _own_copies(srcs, zones, own_sems):
            cp.wait()

    thru = [pltpu.HBM(a.shape, a.dtype) for a in list(handle["srcs"]) + list(handle["lands"])]
    outs = pl.pallas_call(
        body, name=name, out_shape=tuple(thru),
        in_specs=[_HBM] * (2 * n) + [_SEM, _SEM, _SEM, pl.BlockSpec(memory_space=pl.ANY)], out_specs=tuple([_HBM] * (2 * n)),
        input_output_aliases={i: i for i in range(2 * n)},
        compiler_params=pltpu.CompilerParams(has_side_effects=pltpu.SideEffectType.DATAFLOW_SIDE_EFFECTING),
    )(*handle["srcs"], *handle["lands"], *handle["sems"], after)
    return list(outs[:n]), list(outs[n:])


def _cols_from_shards(g):
    return jnp.transpose(g, (1, 0, 2)).reshape(g.shape[1], -1)


def _shards_from_cols(a):
    return jnp.transpose(a.reshape(a.shape[0], N_DEV, -1), (1, 0, 2))


def _local_step(x, positions, ada, g_pre_mix, g_post_mix, b_f, sinks, g_pre_ffn, g_post_ffn, target,
                w_in_t, mix_weights, ffn_weights, on_grads):
    s, d = x.shape
    row = lambda v: v.reshape(1, -1)
    shift_m, scale_m, gate_m, shift_f, scale_f, gate_f = (ada[i:i + 1] for i in range(6))
    w_gate_t, w_qkv_t = w_in_t[F_OFF + N_HEADS:], w_in_t
    w_f_t = jnp.pad(w_in_t[F_OFF:F_OFF + N_HEADS], ((0, LANES - N_HEADS), (0, 0)))
    bf_row = jnp.pad(row(b_f), ((0, 0), (0, LANES - N_HEADS)))
    sink_rows = jnp.broadcast_to(sinks.reshape(N_HEADS, 1).astype(f32), (N_HEADS, LANES))
    inv_freq = 1.0 / (ROPE_THETA ** (jnp.arange(0, HEAD_DIM, 2, dtype=f32) / HEAD_DIM))
    pos_rows = jnp.broadcast_to(positions.reshape(s, 1), (s, LANES))
    cos, sin_s = _rope_tables(pos_rows, jnp.tile(inv_freq, 4).reshape(1, LANES), "rope_tables")

    h1, qa, ka, va, qb, kb, vb = _prenorm_proj_qkv(x, row(g_pre_mix), scale_m, shift_m, w_qkv_t, cos, sin_s, "prenorm_proj_qkv")
    gl = _matmul(h1, w_gate_t, "nt", bf16, "proj_gate")
    fl, cum_b = _forget_prep(h1, w_f_t, bf_row, "proj_forget_prep")
    o_a, lse_a = _attn_fwd(qa, ka, va, "swa_fwd", sink_rows=sink_rows, window=WINDOW, t=2048)
    o_b, lse_b = _attn_fwd(qb, kb, vb, "fox_fwd", cum_b=cum_b, t=1024)
    everything_before = (gl[:8, :LANES] + o_a[:8, :LANES] + o_b[:8, :LANES]).astype(f32)
    w_branch_a, w_branch_b, w_out = mix_weights(everything_before)
    ba, bb, merged = _branch_merge(o_a, o_b, w_branch_a, w_branch_b, gl, "branch_merge")
    y1, x2, h2 = _out_proj_postnorm_prenorm(merged, w_out, x, row(g_post_mix), gate_m, row(g_pre_ffn), scale_f, shift_f,
                                            "out_proj_norms")

    w_ffn_in_t, w_ffn_out = ffn_weights(h2)
    g_ff, u_ff, act = _ffn_in_swiglu(h2, w_ffn_in_t, "ffn_in_swiglu")
    loss_row, d_out, d_y2, vec_pf = _out_proj_loss_tail(act, w_ffn_out, x2, row(g_post_ffn), gate_f, target, "ffn_out_loss_tail")

    g_w_ffn_out = _matmul(act, d_y2, "tn", bf16, "ffn_out_wgrad")
    dg_ff, du_ff = _ffn_out_dgrad_swiglu(d_y2, w_ffn_out, g_ff, u_ff, "ffn_out_dgrad_swiglu")
    g_w_ffn_in_t = _wgrad_stack([dg_ff, du_ff], h2, "ffn_in_wgrad")
    sent = on_grads(dict(w_ffn_in=g_w_ffn_in_t, w_ffn_out=g_w_ffn_out))
    d_x2, vec_nf, d_y1, vec_pm = _dgrad_prenorm_bwd(
        [(dg_ff, w_ffn_in_t, 0), (du_ff, w_ffn_in_t, 1)], x2, row(g_pre_ffn), scale_f, d_out, "ffn_in_dgrad_norms_bwd",
        after=sent, below=(y1, row(g_post_mix), gate_m))

    g_w_out = _matmul(merged, d_y1, "tn", bf16, "out_proj_wgrad")
    d_ba, d_bb, dgl = _out_dgrad_merge_bwd(d_y1, w_out, ba, bb, gl, "out_proj_dgrad_merge_bwd")
    g_w_branch_a = _matmul(o_a, d_ba, "tn", bf16, "branch_a_wgrad")
    g_w_branch_b = _matmul(o_b, d_bb, "tn", bf16, "branch_b_wgrad")
    sent = on_grads(dict(w_out=g_w_out, w_branch_a=g_w_branch_a, w_branch_b=g_w_branch_b))
    d_oa, delta_a, d_sink = _branch_dgrad_delta(d_ba, w_branch_a, o_a, "branch_a_dgrad_delta", lse=lse_a,
                                                sink_rows=sink_rows, after=sent)
    d_ob, delta_b = _branch_dgrad_delta(d_bb, w_branch_b, o_b, "branch_b_dgrad_delta", after=sent)
    dqa_t, dka, dva = _attn_bwd(qa, ka, va, d_oa, lse_a, delta_a, "swa_bwd", window=WINDOW, t=2048)
    dqb_t, dkb, dvb, dcs, rs = _attn_bwd(qb, kb, vb, d_ob, lse_b, delta_b, "fox_bwd", cum_b=cum_b, t=512)
    dqkv = _qkv_prep_bwd(dqa_t, dka, dva, dqb_t, dkb, dvb, cos, sin_s, "qkv_prep_bwd")
    dfl, vec_bf = _forget_prep_bwd(rs.reshape(N_HEADS, s), dcs, fl, bf_row, "forget_prep_bwd")
    g_w_in_t = jnp.concatenate([_matmul(dqkv, h1, "tn", bf16, "qkv_wgrad"), _matmul(dfl, h1, "tn", bf16, "forget_wgrad")[:N_HEADS],
                                _matmul(dgl, h1, "tn", bf16, "gate_wgrad")], axis=0)
    sent = on_grads(dict(w_in=g_w_in_t))
    grad_x, vec_nm = _dgrad_prenorm_bwd([(dgl, w_gate_t, 0), (dqkv, w_qkv_t, 0), (dfl, w_f_t, 0)], x, row(g_pre_mix),
                                        scale_m, d_x2, "in_proj_dgrad_prenorm_bwd", after=sent)

    d_ada = jnp.concatenate([vec_nm[0], vec_nm[1], vec_pm[0], vec_nf[0], vec_nf[1], vec_pf[0]])
    small = dict(b_ada=d_ada, g_pre_mix=vec_nm[2], g_post_mix=vec_pm[1], g_pre_ffn=vec_nf[2], g_post_ffn=vec_pf[1],
                 b_f=vec_bf[0, :N_HEADS], sinks=d_sink[:, 0], loss=loss_row[0, :1])
    return grad_x, small


_SMALL = (("b_ada", 6144), ("g_pre_mix", 1024), ("g_post_mix", 1024), ("g_pre_ffn", 1024), ("g_post_ffn", 1024),
          ("b_f", 128), ("sinks", 128), ("loss", 128))
_SMALL_ROWS = 88


def _pack_small(vals):
    parts = [jnp.pad(vals[k].reshape(-1).astype(f32), (0, n - vals[k].size)) for k, n in _SMALL]
    flat = jnp.concatenate(parts)
    return jnp.pad(flat, (0, _SMALL_ROWS * LANES - flat.size)).reshape(_SMALL_ROWS, LANES)


def _unpack_small(slab, shapes):
    flat, out, off = slab.reshape(-1), {}, 0
    for k, n in _SMALL:
        size = math.prod(shapes[k])
        out[k] = flat[off:off + size].reshape(shapes[k])
        off += n
    return out


def kernel(x, c, positions, w_ada, b_ada, g_pre_mix, g_post_mix, w_in, b_f, sinks, w_branch_a, w_branch_b, w_out, g_pre_ffn, g_post_ffn, w_ffn_in, w_ffn_out, loss_target, m_w_ada, m_b_ada, m_g_pre_mix, m_g_post_mix, m_w_in, m_b_f, m_sinks, m_w_branch_a, m_w_branch_b, m_w_out, m_g_pre_ffn, m_g_post_ffn, m_w_ffn_in, m_w_ffn_out, v_w_ada, v_b_ada, v_g_pre_mix, v_g_post_mix, v_w_in, v_b_f, v_sinks, v_w_branch_a, v_w_branch_b, v_w_out, v_g_pre_ffn, v_g_post_ffn, v_w_ffn_in, v_w_ffn_out):
    xi, yi, ci = _me()
    me = 4 * xi + 2 * yi + ci
    d = D_MODEL
    ada_w = w_ada.shape[2]

    transposed = ("w_in", "w_ffn_in")
    tr = lambda a: jnp.transpose(a[0])

    b_mine = lax.dynamic_slice(b_ada, (0, me * ada_w), (1, ada_w))
    c_all, ada_all, g_in = _gather_prologue(c, w_ada[0], b_mine, tr(w_in).astype(bf16), "gather_prologue")
    c_all = c_all.reshape(N_DEV, d)
    ada = lax.dynamic_index_in_dim(ada_all, me, axis=1, keepdims=False).reshape(6, d)
    late_mix = [w.astype(bf16) for w in (w_branch_a[0], w_branch_b[0], w_out[0])]
    late_ffn = [w.astype(bf16) for w in (tr(w_ffn_in), w_ffn_out[0])]
    mix_h = _exchange_start(late_mix, False, "gather_mix_start", after=g_in)
    ffn_h = _exchange_start(late_ffn, False, "gather_ffn_start", after=mix_h["token"])

    def rows_from_shards(g):
        return g.reshape(g.shape[0] * g.shape[1], g.shape[2])

    def mix_weights(after):
        _, (g_ba, g_bb, g_out) = _exchange_wait(mix_h, after, "gather_mix_wait")
        return _cols_from_shards(g_ba), _cols_from_shards(g_bb), rows_from_shards(g_out)

    def ffn_weights(after):
        _, (g_fi, g_fo) = _exchange_wait(ffn_h, after, "gather_ffn_wait")
        return rows_from_shards(g_fi), rows_from_shards(g_fo)

    row_sharded = ("w_out", "w_ffn_out") + transposed
    in_flight = []

    def on_grads(group):
        sends = [g.reshape(N_DEV, g.shape[0] // N_DEV, g.shape[1]) if nm in row_sharded else _shards_from_cols(g)
                 for nm, g in group.items()]
        handle = _exchange_start(sends, True, "scatter_start_%d" % len(in_flight))
        in_flight.append((list(group), handle))
        return handle["token"]

    grad_x, small = _local_step(
        x[0], positions[0], ada + ffn_h["token"][0, 0], g_pre_mix[0], g_post_mix[0], b_f[0], sinks[0], g_pre_ffn[0],
        g_post_ffn[0], loss_target[0], rows_from_shards(g_in), mix_weights, ffn_weights, on_grads)

    ws = dict(w_in=(w_in, m_w_in, v_w_in), w_branch_a=(w_branch_a, m_w_branch_a, v_w_branch_a),
              w_branch_b=(w_branch_b, m_w_branch_b, v_w_branch_b), w_out=(w_out, m_w_out, v_w_out),
              w_ffn_in=(w_ffn_in, m_w_ffn_in, v_w_ffn_in), w_ffn_out=(w_ffn_out, m_w_ffn_out, v_w_ffn_out))
    res = {}

    def finish_group(gi, after):
        names, handle = in_flight[gi]
        sends, zones = _exchange_wait(handle, after, "scatter_wait_%d" % gi)
        for nm, zone, sent in zip(names, zones, sends):
            w, m, v = (tr(a) if nm in transposed else a[0] for a in ws[nm])
            out = _adamw(zone, w, m, v, "adamw_" + nm, mine=sent, me=me.reshape(1).astype(jnp.int32))
            after = out[0]
            res[nm] = [jnp.transpose(o) for o in out] if nm in transposed else out
        return after

    small_h = _exchange_start([_pack_small(small)], False, "gather_small_start", after=grad_x)
    done = finish_group(1, finish_group(0, small_h["token"]))
    _, (slab_all,) = _exchange_wait(small_h, done, "gather_small_wait")
    small_w = dict(b_ada=b_ada, g_pre_mix=g_pre_mix, g_post_mix=g_post_mix, g_pre_ffn=g_pre_ffn, g_post_ffn=g_post_ffn,
                   b_f=b_f, sinks=sinks, loss=jnp.zeros((1,), f32))
    small_m = dict(b_ada=m_b_ada, g_pre_mix=m_g_pre_mix, g_post_mix=m_g_post_mix, g_pre_ffn=m_g_pre_ffn,
                   g_post_ffn=m_g_post_ffn, b_f=m_b_f, sinks=m_sinks, loss=jnp.zeros((1,), f32))
    small_v = dict(b_ada=v_b_ada, g_pre_mix=v_g_pre_mix, g_post_mix=v_g_post_mix, g_pre_ffn=v_g_pre_ffn,
                   g_post_ffn=v_g_post_ffn, b_f=v_b_f, sinks=v_sinks, loss=jnp.ones((1,), f32))
    shapes = {k: small_w[k].shape for k, _ in _SMALL}
    s_out = _adamw(slab_all, _pack_small(small_w), _pack_small(small_m), _pack_small(small_v), "adamw_small")
    s_grad, s_delta, s_m, s_v = (_unpack_small(o, shapes) for o in s_out)

    d_ada_all = lax.dynamic_slice(slab_all[:, :6144 // LANES, :].reshape(N_DEV, 6144), (0, me * ada_w), (N_DEV, ada_w))
    ada_parts = _ada_wgrad(c_all, d_ada_all, "ada_wgrad")

    res["w_ada"] = _adamw(ada_parts, w_ada[0], m_w_ada[0], v_w_ada[0], "adamw_w_ada")
    finish_group(2, res["w_ada"][0])

    order = ["w_ada", "b_ada", "g_pre_mix", "g_post_mix", "w_in", "b_f", "sinks", "w_branch_a", "w_branch_b", "w_out",
             "g_pre_ffn", "g_post_ffn", "w_ffn_in", "w_ffn_out"]
    outs = [s_grad["loss"].reshape(()), grad_x[None]]
    for which, small_o in enumerate((s_grad, s_delta, s_m, s_v)):
        for nm in order:
            outs.append(res[nm][which][None] if nm in res else small_o[nm])
    return tuple(outs)
```
